```python
import math
import jax, jax.numpy as jnp
from jax import lax
import numpy as np

D_MODEL = 1024
BATCH = 8
SEQ = 4096
DEPTH = 2

PLE_DIM = 256
D_FF = 2816
HEAD_DIM = 64
SB_HEADS = 8
SWA_HEADS = 8
SWA_KV_HEADS = 2
WINDOW = 128
Q_BLOCK = 128
GDN_K_HEADS = 8
GDN_V_HEADS = 16
GDN_HEAD_DIM = 128
GDN_CONV = 4
GDN_CHUNK = 64
EPS = 1e-6
N_EVEN = (DEPTH + 1) // 2
N_ODD = DEPTH // 2

SB_W = SB_HEADS * HEAD_DIM
SWA_QW = SWA_HEADS * HEAD_DIM
SWA_KVW = SWA_KV_HEADS * HEAD_DIM
ATT_IN = 3 * SB_W + SWA_QW + 2 * SWA_KVW
ATT_OUT = SB_W + SWA_QW
GDN_KW = GDN_K_HEADS * GDN_HEAD_DIM
GDN_VW = GDN_V_HEADS * GDN_HEAD_DIM
GDN_CONV_W = 2 * GDN_KW + GDN_VW
GDN_IN = GDN_CONV_W + GDN_VW + 2 * GDN_V_HEADS

kernel_name = 'hybrid_stickbreak_swa_gdn_macaron'


def rmsnorm(x, g):
    xf = x.astype(jnp.float32)
    y = xf * lax.rsqrt(jnp.mean(xf * xf, axis=-1, keepdims=True) + EPS) * g.astype(jnp.float32)
    return y.astype(x.dtype)


def l2norm(x):
    xf = x.astype(jnp.float32)
    return xf * lax.rsqrt(jnp.sum(xf * xf, axis=-1, keepdims=True) + EPS)


def swiglu(h, w_gate, w_up, w_down):
    return (jax.nn.silu(h @ w_gate) * (h @ w_up)) @ w_down


def alibi_slopes(n):
    return jnp.asarray(2.0 ** (-8.0 * (np.arange(n) + 1) / n), dtype=jnp.float32)


def stick_breaking_attention(q, k, v):
    b, h, t, d = q.shape
    nblk = t // Q_BLOCK
    qb = q.reshape(b, h, nblk, Q_BLOCK, d).transpose(2, 0, 1, 3, 4)
    key_pos = jnp.arange(t)
    scale = d ** -0.5

    def block(args):
        qi, i = args
        z = jnp.einsum('bhqd,bhkd->bhqk', qi, k).astype(jnp.float32) * scale
        q_pos = i * Q_BLOCK + jnp.arange(Q_BLOCK)
        causal = key_pos[None, :] < q_pos[:, None]
        log_keep = jnp.where(causal, jax.nn.log_sigmoid(-z), 0.0)
        log_between = lax.cumsum(log_keep, axis=3, reverse=True) - log_keep
        w = jnp.where(causal, jnp.exp(jax.nn.log_sigmoid(z) + log_between), 0.0)
        return jnp.einsum('bhqk,bhkd->bhqd', w.astype(v.dtype), v)

    out = lax.map(block, (qb, jnp.arange(nblk)))
    return out.transpose(1, 2, 0, 3, 4).reshape(b, h, t, d)


def swa_sink_attention(q, k, v, q_gain, k_gain, sinks, slopes):
    q = rmsnorm(q, q_gain)
    k = rmsnorm(k, k_gain)
    b, t, hq, d = q.shape
    hkv = k.shape[2]
    g = hq // hkv
    nblk = t // WINDOW
    qb = q.reshape(b, nblk, WINDOW, hkv, g, d)
    kpad = jnp.pad(k, ((0, 0), (WINDOW, 0), (0, 0), (0, 0)))
    vpad = jnp.pad(v, ((0, 0), (WINDOW, 0), (0, 0), (0, 0)))
    kb = jnp.concatenate([kpad[:, :t].reshape(b, nblk, WINDOW, hkv, d),
                          k.reshape(b, nblk, WINDOW, hkv, d)], axis=2)
    vb = jnp.concatenate([vpad[:, :t].reshape(b, nblk, WINDOW, hkv, d),
                          v.reshape(b, nblk, WINDOW, hkv, d)], axis=2)
    s = jnp.einsum('bnqhgd,bnkhd->bnhgqk', qb, kb).astype(jnp.float32) * (d ** -0.5)
    qi = jnp.arange(WINDOW)[:, None]
    kj = jnp.arange(2 * WINDOW)[None, :]
    dist = (qi + WINDOW - kj)
    band = (dist >= 0) & (dist < WINDOW)
    valid = band[None] & ((jnp.arange(nblk)[:, None, None] > 0) | (kj >= WINDOW)[None])
    bias = -slopes.reshape(hkv, g)[:, :, None, None] * dist.astype(jnp.float32)
    s = jnp.where(valid[None, :, None, None], s + bias[None, None], -jnp.inf)
    sink = jnp.broadcast_to(sinks.astype(jnp.float32).reshape(hkv, g)[None, None, :, :, None, None],
                            s.shape[:-1] + (1,))
    probs = jax.nn.softmax(jnp.concatenate([s, sink], axis=-1), axis=-1)[..., :-1]
    o = jnp.einsum('bnhgqk,bnkhd->bnqhgd', probs.astype(v.dtype), vb)
    return o.reshape(b, t, hq, d)


def attention_mixer(h, w_in, q_gain, k_gain, sinks, w_out):
    b, t, _ = h.shape
    proj = h @ w_in
    cuts = [SB_W, 2 * SB_W, 3 * SB_W, 3 * SB_W + SWA_QW, 3 * SB_W + SWA_QW + SWA_KVW]
    sq, sk, sv, bq, bk, bv = jnp.split(proj, cuts, axis=-1)
    heads = lambda z, n: z.reshape(b, t, n, HEAD_DIM)
    a_out = stick_breaking_attention(heads(sq, SB_HEADS).transpose(0, 2, 1, 3),
                                     heads(sk, SB_HEADS).transpose(0, 2, 1, 3),
                                     heads(sv, SB_HEADS).transpose(0, 2, 1, 3))
    b_out = swa_sink_attention(heads(bq, SWA_HEADS), heads(bk, SWA_KV_HEADS), heads(bv, SWA_KV_HEADS),
                               q_gain, k_gain, sinks, alibi_slopes(SWA_HEADS))
    o = jnp.concatenate([a_out.transpose(0, 2, 1, 3).reshape(b, t, SB_W),
                         b_out.reshape(b, t, SWA_QW)], axis=-1)
    return o @ w_out


def causal_depthwise_conv(x, w):
    kk, c = w.shape
    return lax.conv_general_dilated(x, w[:, None, :].astype(x.dtype), window_strides=(1,),
                                    padding=[(kk - 1, 0)], dimension_numbers=('NWC', 'WIO', 'NWC'),
                                    feature_group_count=c)


def chunk_gated_delta_rule(q, k, v, g, beta):
    b, t, h, dk = q.shape
    dv = v.shape[-1]
    c = GDN_CHUNK
    n = t // c
    chunks = lambda z: z.astype(jnp.float32).reshape(b, n, c, h, -1).transpose(0, 3, 1, 2, 4)
    q, k, v = chunks(q), chunks(k), chunks(v)
    g = g.astype(jnp.float32).reshape(b, n, c, h).transpose(0, 3, 1, 2)
    beta = beta.astype(jnp.float32).reshape(b, n, c, h).transpose(0, 3, 1, 2)
    gc = jnp.cumsum(g, axis=-1)
    idx = jnp.arange(c)
    lower_incl = idx[:, None] >= idx[None, :]
    strict = idx[:, None] > idx[None, :]
    decay = jnp.exp(jnp.where(lower_incl, gc[..., :, None] - gc[..., None, :], -jnp.inf))
    kbeta = k * beta[..., None]
    lmat = jnp.where(strict, jnp.einsum('bhncd,bhnsd->bhncs', kbeta, k) * decay, 0.0)
    tmat = lmat + jnp.eye(c, dtype=jnp.float32)
    rhs = jnp.concatenate([v * beta[..., None], kbeta * jnp.exp(gc)[..., None]], axis=-1)
    sol = lax.linalg.triangular_solve(tmat, rhs, left_side=True, lower=True, unit_diagonal=True)
    u, w = sol[..., :dv], sol[..., dv:]
    attn = jnp.where(lower_incl, jnp.einsum('bhncd,bhnsd->bhncs', q, k) * decay, 0.0)
    q_dec = q * jnp.exp(gc)[..., None]
    k_tail = k * jnp.exp(gc[..., -1:] - gc)[..., None]
    chunk_dec = jnp.exp(gc[..., -1])

    def step(state, inp):
        u_c, w_c, qd_c, a_c, kt_c, dec_c = inp
        v_new = u_c - jnp.einsum('bhcd,bhdv->bhcv', w_c, state)
        o_c = jnp.einsum('bhcd,bhdv->bhcv', qd_c, state) + jnp.einsum('bhcs,bhsv->bhcv', a_c, v_new)
        state = state * dec_c[..., None, None] + jnp.einsum('bhcd,bhcv->bhdv', kt_c, v_new)
        return state, o_c

    xs = tuple(jnp.moveaxis(z, 2, 0) for z in (u, w, q_dec, attn, k_tail, chunk_dec))
    s0 = jnp.zeros((b, h, dk, dv), jnp.float32)
    _, o = lax.scan(step, s0, xs)
    return o.transpose(1, 0, 3, 2, 4).reshape(b, t, h, dv)


def gdn_mixer(h, w_in, conv_w, a_log, dt_bias, out_gain, w_out):
    b, t, _ = h.shape
    proj = h @ w_in
    qkv, z, beta_logit, a = jnp.split(
        proj, [GDN_CONV_W, GDN_CONV_W + GDN_VW, GDN_CONV_W + GDN_VW + GDN_V_HEADS], axis=-1)
    qkv = jax.nn.silu(causal_depthwise_conv(qkv, conv_w))
    q, k, v = jnp.split(qkv, [GDN_KW, 2 * GDN_KW], axis=-1)
    q = l2norm(q.reshape(b, t, GDN_K_HEADS, GDN_HEAD_DIM)) * (GDN_HEAD_DIM ** -0.5)
    k = l2norm(k.reshape(b, t, GDN_K_HEADS, GDN_HEAD_DIM))
    rep = GDN_V_HEADS // GDN_K_HEADS
    q = jnp.repeat(q, rep, axis=2)
    k = jnp.repeat(k, rep, axis=2)
    v = v.reshape(b, t, GDN_V_HEADS, GDN_HEAD_DIM)
    beta = jax.nn.sigmoid(beta_logit.astype(jnp.float32))
    g = -jnp.exp(a_log.astype(jnp.float32)) * jax.nn.softplus(a.astype(jnp.float32) + dt_bias.astype(jnp.float32))
    o = chunk_gated_delta_rule(q, k, v, g, beta).astype(h.dtype)
    o = rmsnorm(o, out_gain) * jax.nn.silu(z.reshape(b, t, GDN_V_HEADS, GDN_HEAD_DIM))
    return o.reshape(b, t, GDN_VW) @ w_out


def _fwd_setup_inputs(seed: int = 0) -> dict:
    key = jax.random.key(seed)
    ks = jax.random.split(key, 24)
    f32 = jnp.float32
    nrm = lambda kk, shape, fan_in: jax.random.normal(kk, shape, f32) * (fan_in ** -0.5)
    gain = lambda kk, shape: 1.0 + 0.02 * jax.random.normal(kk, shape, f32)
    dt = jnp.exp(jax.random.uniform(ks[15], (N_ODD, GDN_V_HEADS), f32, math.log(1e-3), math.log(0.1)))
    return {
        'x': jax.random.normal(ks[0], (BATCH, SEQ, D_MODEL), f32),
        'p': jax.random.normal(ks[1], (DEPTH, BATCH, SEQ, PLE_DIM), f32),
        'ffn_norm': gain(ks[2], (DEPTH, 2, D_MODEL)),
        'ffn_w_gate': nrm(ks[3], (DEPTH, 2, D_MODEL, D_FF), D_MODEL),
        'ffn_w_up': nrm(ks[4], (DEPTH, 2, D_MODEL, D_FF), D_MODEL),
        'ffn_w_down': nrm(ks[5], (DEPTH, 2, D_FF, D_MODEL), D_FF),
        'mix_norm': gain(ks[6], (DEPTH, D_MODEL)),
        'att_w_in': nrm(ks[7], (N_EVEN, D_MODEL, ATT_IN), D_MODEL),
        'att_q_norm': gain(ks[8], (N_EVEN, HEAD_DIM)),
        'att_k_norm': gain(ks[9], (N_EVEN, HEAD_DIM)),
        'att_sinks': 0.5 * jax.random.normal(ks[10], (N_EVEN, SWA_HEADS), f32),
        'att_w_out': nrm(ks[11], (N_EVEN, ATT_OUT, D_MODEL), ATT_OUT),
        'gdn_w_in': nrm(ks[12], (N_ODD, D_MODEL, GDN_IN), D_MODEL),
        'gdn_conv_w': nrm(ks[13], (N_ODD, GDN_CONV, GDN_CONV_W), GDN_CONV),
        'gdn_a_log': jnp.log(jax.random.uniform(ks[14], (N_ODD, GDN_V_HEADS), f32, 1.0, 16.0)),
        'gdn_dt_bias': dt + jnp.log(-jnp.expm1(-dt)),
        'gdn_out_norm': gain(ks[16], (N_ODD, GDN_HEAD_DIM)),
        'gdn_w_out': nrm(ks[17], (N_ODD, GDN_VW, D_MODEL), GDN_VW),
        'ple_norm': gain(ks[18], (DEPTH, D_MODEL)),
        'ple_w_gate': nrm(ks[19], (DEPTH, D_MODEL, D_MODEL), D_MODEL),
        'ple_w_proj': nrm(ks[20], (DEPTH, PLE_DIM, D_MODEL), PLE_DIM),
    }


def _fwd_reference(x, p, ffn_norm, ffn_w_gate, ffn_w_up, ffn_w_down, mix_norm,
              att_w_in, att_q_norm, att_k_norm, att_sinks, att_w_out,
              gdn_w_in, gdn_conv_w, gdn_a_log, gdn_dt_bias, gdn_out_norm, gdn_w_out,
              ple_norm, ple_w_gate, ple_w_proj):
    h = x
    for i in range(DEPTH):
        h = h + 0.5 * swiglu(rmsnorm(h, ffn_norm[i, 0]), ffn_w_gate[i, 0], ffn_w_up[i, 0], ffn_w_down[i, 0])
        hn = rmsnorm(h, mix_norm[i])
        j = i // 2
        if i % 2 == 0:
            h = h + attention_mixer(hn, att_w_in[j], att_q_norm[j], att_k_norm[j], att_sinks[j], att_w_out[j])
        else:
            h = h + gdn_mixer(hn, gdn_w_in[j], gdn_conv_w[j], gdn_a_log[j], gdn_dt_bias[j],
                              gdn_out_norm[j], gdn_w_out[j])
        h = h + 0.5 * swiglu(rmsnorm(h, ffn_norm[i, 1]), ffn_w_gate[i, 1], ffn_w_up[i, 1], ffn_w_down[i, 1])
        gate = jax.nn.sigmoid(rmsnorm(h, ple_norm[i]) @ ple_w_gate[i])
        h = h + gate * (p[i] @ ple_w_proj[i])
    return h


import jax as _jax
import jax.numpy as _jnp

TWIN_FORMAT = 'train_step'
FWD_PARAMS = ['x', 'p', 'ffn_norm', 'ffn_w_gate', 'ffn_w_up', 'ffn_w_down', 'mix_norm', 'att_w_in', 'att_q_norm', 'att_k_norm', 'att_sinks', 'att_w_out', 'gdn_w_in', 'gdn_conv_w', 'gdn_a_log', 'gdn_dt_bias', 'gdn_out_norm', 'gdn_w_out', 'ple_norm', 'ple_w_gate', 'ple_w_proj']
TWIN_WEIGHTS = ['ffn_norm', 'ffn_w_gate', 'ffn_w_up', 'ffn_w_down', 'mix_norm', 'att_w_in', 'att_q_norm', 'att_k_norm', 'att_sinks', 'att_w_out', 'gdn_w_in', 'gdn_conv_w', 'gdn_a_log', 'gdn_dt_bias', 'gdn_out_norm', 'gdn_w_out', 'ple_norm', 'ple_w_gate', 'ple_w_proj']
TWIN_DIFF_INPUT = 'x'
TWIN_INPUTS = ['x', 'p', 'ffn_norm', 'ffn_w_gate', 'ffn_w_up', 'ffn_w_down', 'mix_norm', 'att_w_in', 'att_q_norm', 'att_k_norm', 'att_sinks', 'att_w_out', 'gdn_w_in', 'gdn_conv_w', 'gdn_a_log', 'gdn_dt_bias', 'gdn_out_norm', 'gdn_w_out', 'ple_norm', 'ple_w_gate', 'ple_w_proj', 'loss_target', 'm_ffn_norm', 'm_ffn_w_gate', 'm_ffn_w_up', 'm_ffn_w_down', 'm_mix_norm', 'm_att_w_in', 'm_att_q_norm', 'm_att_k_norm', 'm_att_sinks', 'm_att_w_out', 'm_gdn_w_in', 'm_gdn_conv_w', 'm_gdn_a_log', 'm_gdn_dt_bias', 'm_gdn_out_norm', 'm_gdn_w_out', 'm_ple_norm', 'm_ple_w_gate', 'm_ple_w_proj', 'v_ffn_norm', 'v_ffn_w_gate', 'v_ffn_w_up', 'v_ffn_w_down', 'v_mix_norm', 'v_att_w_in', 'v_att_q_norm', 'v_att_k_norm', 'v_att_sinks', 'v_att_w_out', 'v_gdn_w_in', 'v_gdn_conv_w', 'v_gdn_a_log', 'v_gdn_dt_bias', 'v_gdn_out_norm', 'v_gdn_w_out', 'v_ple_norm', 'v_ple_w_gate', 'v_ple_w_proj']
TWIN_OUTPUTS = ['loss', 'grad_x', 'grad_ffn_norm', 'grad_ffn_w_gate', 'grad_ffn_w_up', 'grad_ffn_w_down', 'grad_mix_norm', 'grad_att_w_in', 'grad_att_q_norm', 'grad_att_k_norm', 'grad_att_sinks', 'grad_att_w_out', 'grad_gdn_w_in', 'grad_gdn_conv_w', 'grad_gdn_a_log', 'grad_gdn_dt_bias', 'grad_gdn_out_norm', 'grad_gdn_w_out', 'grad_ple_norm', 'grad_ple_w_gate', 'grad_ple_w_proj', 'delta_ffn_norm', 'delta_ffn_w_gate', 'delta_ffn_w_up', 'delta_ffn_w_down', 'delta_mix_norm', 'delta_att_w_in', 'delta_att_q_norm', 'delta_att_k_norm', 'delta_att_sinks', 'delta_att_w_out', 'delta_gdn_w_in', 'delta_gdn_conv_w', 'delta_gdn_a_log', 'delta_gdn_dt_bias', 'delta_gdn_out_norm', 'delta_gdn_w_out', 'delta_ple_norm', 'delta_ple_w_gate', 'delta_ple_w_proj', 'new_m_ffn_norm', 'new_m_ffn_w_gate', 'new_m_ffn_w_up', 'new_m_ffn_w_down', 'new_m_mix_norm', 'new_m_att_w_in', 'new_m_att_q_norm', 'new_m_att_k_norm', 'new_m_att_sinks', 'new_m_att_w_out', 'new_m_gdn_w_in', 'new_m_gdn_conv_w', 'new_m_gdn_a_log', 'new_m_gdn_dt_bias', 'new_m_gdn_out_norm', 'new_m_gdn_w_out', 'new_m_ple_norm', 'new_m_ple_w_gate', 'new_m_ple_w_proj', 'new_v_ffn_norm', 'new_v_ffn_w_gate', 'new_v_ffn_w_up', 'new_v_ffn_w_down', 'new_v_mix_norm', 'new_v_att_w_in', 'new_v_att_q_norm', 'new_v_att_k_norm', 'new_v_att_sinks', 'new_v_att_w_out', 'new_v_gdn_w_in', 'new_v_gdn_conv_w', 'new_v_gdn_a_log', 'new_v_gdn_dt_bias', 'new_v_gdn_out_norm', 'new_v_gdn_w_out', 'new_v_ple_norm', 'new_v_ple_w_gate', 'new_v_ple_w_proj']
TWIN_LEAF_KINDS = {'loss': 'loss', 'grad_x': 'grad_x', 'grad_ffn_norm': 'grad_w', 'grad_ffn_w_gate': 'grad_w', 'grad_ffn_w_up': 'grad_w', 'grad_ffn_w_down': 'grad_w', 'grad_mix_norm': 'grad_w', 'grad_att_w_in': 'grad_w', 'grad_att_q_norm': 'grad_w', 'grad_att_k_norm': 'grad_w', 'grad_att_sinks': 'grad_w', 'grad_att_w_out': 'grad_w', 'grad_gdn_w_in': 'grad_w', 'grad_gdn_conv_w': 'grad_w', 'grad_gdn_a_log': 'grad_w', 'grad_gdn_dt_bias': 'grad_w', 'grad_gdn_out_norm': 'grad_w', 'grad_gdn_w_out': 'grad_w', 'grad_ple_norm': 'grad_w', 'grad_ple_w_gate': 'grad_w', 'grad_ple_w_proj': 'grad_w', 'delta_ffn_norm': 'delta_w', 'delta_ffn_w_gate': 'delta_w', 'delta_ffn_w_up': 'delta_w', 'delta_ffn_w_down': 'delta_w', 'delta_mix_norm': 'delta_w', 'delta_att_w_in': 'delta_w', 'delta_att_q_norm': 'delta_w', 'delta_att_k_norm': 'delta_w', 'delta_att_sinks': 'delta_w', 'delta_att_w_out': 'delta_w', 'delta_gdn_w_in': 'delta_w', 'delta_gdn_conv_w': 'delta_w', 'delta_gdn_a_log': 'delta_w', 'delta_gdn_dt_bias': 'delta_w', 'delta_gdn_out_norm': 'delta_w', 'delta_gdn_w_out': 'delta_w', 'delta_ple_norm': 'delta_w', 'delta_ple_w_gate': 'delta_w', 'delta_ple_w_proj': 'delta_w', 'new_m_ffn_norm': 'new_m', 'new_m_ffn_w_gate': 'new_m', 'new_m_ffn_w_up': 'new_m', 'new_m_ffn_w_down': 'new_m', 'new_m_mix_norm': 'new_m', 'new_m_att_w_in': 'new_m', 'new_m_att_q_norm': 'new_m', 'new_m_att_k_norm': 'new_m', 'new_m_att_sinks': 'new_m', 'new_m_att_w_out': 'new_m', 'new_m_gdn_w_in': 'new_m', 'new_m_gdn_conv_w': 'new_m', 'new_m_gdn_a_log': 'new_m', 'new_m_gdn_dt_bias': 'new_m', 'new_m_gdn_out_norm': 'new_m', 'new_m_gdn_w_out': 'new_m', 'new_m_ple_norm': 'new_m', 'new_m_ple_w_gate': 'new_m', 'new_m_ple_w_proj': 'new_m', 'new_v_ffn_norm': 'new_v', 'new_v_ffn_w_gate': 'new_v', 'new_v_ffn_w_up': 'new_v', 'new_v_ffn_w_down': 'new_v', 'new_v_mix_norm': 'new_v', 'new_v_att_w_in': 'new_v', 'new_v_att_q_norm': 'new_v', 'new_v_att_k_norm': 'new_v', 'new_v_att_sinks': 'new_v', 'new_v_att_w_out': 'new_v', 'new_v_gdn_w_in': 'new_v', 'new_v_gdn_conv_w': 'new_v', 'new_v_gdn_a_log': 'new_v', 'new_v_gdn_dt_bias': 'new_v', 'new_v_gdn_out_norm': 'new_v', 'new_v_gdn_w_out': 'new_v', 'new_v_ple_norm': 'new_v', 'new_v_ple_w_gate': 'new_v', 'new_v_ple_w_proj': 'new_v'}


def _forward(args):
    return _fwd_reference(*[args[k] for k in FWD_PARAMS])


def _output_shape():
    def fwd():
        inp = _fwd_setup_inputs(0)
        return _fwd_reference(*[inp[k] for k in FWD_PARAMS])
    out = _jax.eval_shape(fwd)
    return out.shape, out.dtype

N_MICROBATCH = 1
ADAM_LR = 0.001
ADAM_B1 = 0.9
ADAM_B2 = 0.999
ADAM_EPS = 1e-08
ADAM_WD = 0.01
ADAM_STEP = 10
PER_EXAMPLE_BATCH_AXIS = {'x': 0, 'p': 1, 'loss_target': 0}
SHARED_INPUTS = []
_WEIGHT_DTYPES = {'ffn_norm': _jnp.float32, 'ffn_w_gate': _jnp.float32, 'ffn_w_up': _jnp.float32, 'ffn_w_down': _jnp.float32, 'mix_norm': _jnp.float32, 'att_w_in': _jnp.float32, 'att_q_norm': _jnp.float32, 'att_k_norm': _jnp.float32, 'att_sinks': _jnp.float32, 'att_w_out': _jnp.float32, 'gdn_w_in': _jnp.float32, 'gdn_conv_w': _jnp.float32, 'gdn_a_log': _jnp.float32, 'gdn_dt_bias': _jnp.float32, 'gdn_out_norm': _jnp.float32, 'gdn_w_out': _jnp.float32, 'ple_norm': _jnp.float32, 'ple_w_gate': _jnp.float32, 'ple_w_proj': _jnp.float32}
MOMENT_SCALE = {'ffn_norm': 6.175561e+00, 'ffn_w_gate': 1.094266e-01, 'ffn_w_up': 1.205506e-01, 'ffn_w_down': 1.980989e-01, 'mix_norm': 1.176580e+01, 'att_w_in': 3.245316e-01, 'att_q_norm': 9.639875e+00, 'att_k_norm': 9.619077e+00, 'att_sinks': 2.986353e+01, 'att_w_out': 4.113304e-01, 'gdn_w_in': 2.648752e-01, 'gdn_conv_w': 3.379729e-01, 'gdn_a_log': 1.474570e+01, 'gdn_dt_bias': 1.416570e+01, 'gdn_out_norm': 8.971456e+01, 'gdn_w_out': 7.495206e-01, 'ple_norm': 9.925686e-01, 'ple_w_gate': 1.800180e-01, 'ple_w_proj': 5.283885e-01}


def _to_microbatches(a, axis):
    t = _jnp.moveaxis(a, axis, 0)
    t = t.reshape((N_MICROBATCH, t.shape[0] // N_MICROBATCH) + t.shape[1:])
    return _jnp.moveaxis(t, 1, axis + 1)


def setup_inputs(seed: int = 0) -> dict:
    inp = _fwd_setup_inputs(seed)
    key = _jax.random.fold_in(_jax.random.key(seed), 7919)
    shape, _ = _output_shape()
    out = dict(inp)
    out["loss_target"] = _jax.random.normal(_jax.random.fold_in(key, 0), shape, _jnp.float32)
    for i, name in enumerate(TWIN_WEIGHTS):
        w = inp[name].astype(_jnp.float32)
        if MOMENT_SCALE is None:
            s = _jnp.sqrt(_jnp.mean(_jnp.square(w)) + 1e-30)
        else:
            s = MOMENT_SCALE[name]
        km, kv = _jax.random.split(_jax.random.fold_in(key, i + 1))
        out[name] = w
        out["m_" + name] = s * _jax.random.normal(km, w.shape, _jnp.float32)
        out["v_" + name] = (s * s) * _jax.random.uniform(kv, w.shape, _jnp.float32, 0.5, 1.5)
    if N_MICROBATCH > 1:
        for name, axis in PER_EXAMPLE_BATCH_AXIS.items():
            out[name] = _to_microbatches(out[name], axis)
    return {'x': out['x'], 'p': out['p'], 'ffn_norm': out['ffn_norm'], 'ffn_w_gate': out['ffn_w_gate'], 'ffn_w_up': out['ffn_w_up'], 'ffn_w_down': out['ffn_w_down'], 'mix_norm': out['mix_norm'], 'att_w_in': out['att_w_in'], 'att_q_norm': out['att_q_norm'], 'att_k_norm': out['att_k_norm'], 'att_sinks': out['att_sinks'], 'att_w_out': out['att_w_out'], 'gdn_w_in': out['gdn_w_in'], 'gdn_conv_w': out['gdn_conv_w'], 'gdn_a_log': out['gdn_a_log'], 'gdn_dt_bias': out['gdn_dt_bias'], 'gdn_out_norm': out['gdn_out_norm'], 'gdn_w_out': out['gdn_w_out'], 'ple_norm': out['ple_norm'], 'ple_w_gate': out['ple_w_gate'], 'ple_w_proj': out['ple_w_proj'], 'loss_target': out['loss_target'], 'm_ffn_norm': out['m_ffn_norm'], 'm_ffn_w_gate': out['m_ffn_w_gate'], 'm_ffn_w_up': out['m_ffn_w_up'], 'm_ffn_w_down': out['m_ffn_w_down'], 'm_mix_norm': out['m_mix_norm'], 'm_att_w_in': out['m_att_w_in'], 'm_att_q_norm': out['m_att_q_norm'], 'm_att_k_norm': out['m_att_k_norm'], 'm_att_sinks': out['m_att_sinks'], 'm_att_w_out': out['m_att_w_out'], 'm_gdn_w_in': out['m_gdn_w_in'], 'm_gdn_conv_w': out['m_gdn_conv_w'], 'm_gdn_a_log': out['m_gdn_a_log'], 'm_gdn_dt_bias': out['m_gdn_dt_bias'], 'm_gdn_out_norm': out['m_gdn_out_norm'], 'm_gdn_w_out': out['m_gdn_w_out'], 'm_ple_norm': out['m_ple_norm'], 'm_ple_w_gate': out['m_ple_w_gate'], 'm_ple_w_proj': out['m_ple_w_proj'], 'v_ffn_norm': out['v_ffn_norm'], 'v_ffn_w_gate': out['v_ffn_w_gate'], 'v_ffn_w_up': out['v_ffn_w_up'], 'v_ffn_w_down': out['v_ffn_w_down'], 'v_mix_norm': out['v_mix_norm'], 'v_att_w_in': out['v_att_w_in'], 'v_att_q_norm': out['v_att_q_norm'], 'v_att_k_norm': out['v_att_k_norm'], 'v_att_sinks': out['v_att_sinks'], 'v_att_w_out': out['v_att_w_out'], 'v_gdn_w_in': out['v_gdn_w_in'], 'v_gdn_conv_w': out['v_gdn_conv_w'], 'v_gdn_a_log': out['v_gdn_a_log'], 'v_gdn_dt_bias': out['v_gdn_dt_bias'], 'v_gdn_out_norm': out['v_gdn_out_norm'], 'v_gdn_w_out': out['v_gdn_w_out'], 'v_ple_norm': out['v_ple_norm'], 'v_ple_w_gate': out['v_ple_w_gate'], 'v_ple_w_proj': out['v_ple_w_proj']}


def _loss(weights, diff, rest, loss_target):
    with _jax.named_scope("forward"):
        args = {**rest, TWIN_DIFF_INPUT: diff, **{k: w.astype(_WEIGHT_DTYPES[k]) for k, w in weights.items()}}
        y = _forward(args)
    with _jax.named_scope("loss_head"):
        err = _jnp.square(y.astype(_jnp.float32) - loss_target)
        return 0.5 * _jnp.sum(_jnp.mean(err, axis=-1)) if err.ndim else 0.5 * err


def _adamw(w, g, m, v):
    m = ADAM_B1 * m + (1.0 - ADAM_B1) * g
    v = ADAM_B2 * v + (1.0 - ADAM_B2) * _jnp.square(g)
    m_hat = m / (1.0 - ADAM_B1 ** ADAM_STEP)
    v_hat = v / (1.0 - ADAM_B2 ** ADAM_STEP)
    delta = -ADAM_LR * (m_hat / (_jnp.sqrt(v_hat) + ADAM_EPS) + ADAM_WD * w)
    return delta, m, v


def reference(x, p, ffn_norm, ffn_w_gate, ffn_w_up, ffn_w_down, mix_norm, att_w_in, att_q_norm, att_k_norm, att_sinks, att_w_out, gdn_w_in, gdn_conv_w, gdn_a_log, gdn_dt_bias, gdn_out_norm, gdn_w_out, ple_norm, ple_w_gate, ple_w_proj, loss_target, m_ffn_norm, m_ffn_w_gate, m_ffn_w_up, m_ffn_w_down, m_mix_norm, m_att_w_in, m_att_q_norm, m_att_k_norm, m_att_sinks, m_att_w_out, m_gdn_w_in, m_gdn_conv_w, m_gdn_a_log, m_gdn_dt_bias, m_gdn_out_norm, m_gdn_w_out, m_ple_norm, m_ple_w_gate, m_ple_w_proj, v_ffn_norm, v_ffn_w_gate, v_ffn_w_up, v_ffn_w_down, v_mix_norm, v_att_w_in, v_att_q_norm, v_att_k_norm, v_att_sinks, v_att_w_out, v_gdn_w_in, v_gdn_conv_w, v_gdn_a_log, v_gdn_dt_bias, v_gdn_out_norm, v_gdn_w_out, v_ple_norm, v_ple_w_gate, v_ple_w_proj):
    given = dict(x=x, p=p, ffn_norm=ffn_norm, ffn_w_gate=ffn_w_gate, ffn_w_up=ffn_w_up, ffn_w_down=ffn_w_down, mix_norm=mix_norm, att_w_in=att_w_in, att_q_norm=att_q_norm, att_k_norm=att_k_norm, att_sinks=att_sinks, att_w_out=att_w_out, gdn_w_in=gdn_w_in, gdn_conv_w=gdn_conv_w, gdn_a_log=gdn_a_log, gdn_dt_bias=gdn_dt_bias, gdn_out_norm=gdn_out_norm, gdn_w_out=gdn_w_out, ple_norm=ple_norm, ple_w_gate=ple_w_gate, ple_w_proj=ple_w_proj, loss_target=loss_target, m_ffn_norm=m_ffn_norm, m_ffn_w_gate=m_ffn_w_gate, m_ffn_w_up=m_ffn_w_up, m_ffn_w_down=m_ffn_w_down, m_mix_norm=m_mix_norm, m_att_w_in=m_att_w_in, m_att_q_norm=m_att_q_norm, m_att_k_norm=m_att_k_norm, m_att_sinks=m_att_sinks, m_att_w_out=m_att_w_out, m_gdn_w_in=m_gdn_w_in, m_gdn_conv_w=m_gdn_conv_w, m_gdn_a_log=m_gdn_a_log, m_gdn_dt_bias=m_gdn_dt_bias, m_gdn_out_norm=m_gdn_out_norm, m_gdn_w_out=m_gdn_w_out, m_ple_norm=m_ple_norm, m_ple_w_gate=m_ple_w_gate, m_ple_w_proj=m_ple_w_proj, v_ffn_norm=v_ffn_norm, v_ffn_w_gate=v_ffn_w_gate, v_ffn_w_up=v_ffn_w_up, v_ffn_w_down=v_ffn_w_down, v_mix_norm=v_mix_norm, v_att_w_in=v_att_w_in, v_att_q_norm=v_att_q_norm, v_att_k_norm=v_att_k_norm, v_att_sinks=v_att_sinks, v_att_w_out=v_att_w_out, v_gdn_w_in=v_gdn_w_in, v_gdn_conv_w=v_gdn_conv_w, v_gdn_a_log=v_gdn_a_log, v_gdn_dt_bias=v_gdn_dt_bias, v_gdn_out_norm=v_gdn_out_norm, v_gdn_w_out=v_gdn_w_out, v_ple_norm=v_ple_norm, v_ple_w_gate=v_ple_w_gate, v_ple_w_proj=v_ple_w_proj)
    weights = {n: given[n] for n in TWIN_WEIGHTS}
    shared = {n: given[n] for n in SHARED_INPUTS}
    per_example = {n: given[n] for n in ['x', 'p']}
    grad_fn = _jax.value_and_grad(_loss, argnums=(0, 1))

    def one_microbatch(ex, loss_target):
        ex = dict(ex)
        diff = ex.pop(TWIN_DIFF_INPUT)
        return grad_fn(weights, diff, {**shared, **ex}, loss_target)

    if N_MICROBATCH == 1:
        loss, (grad_w, grad_x) = one_microbatch(per_example, given["loss_target"])
    else:
        def body(carry, xs):
            loss_sum, grad_sum = carry
            l_k, (gw_k, gx_k) = one_microbatch(xs[0], xs[1])
            with _jax.named_scope("update"):
                return (loss_sum + l_k, _jax.tree.map(_jnp.add, grad_sum, gw_k)), gx_k

        init = (_jnp.zeros((), _jnp.float32), _jax.tree.map(_jnp.zeros_like, weights))
        (loss, grad_w), grad_x = _jax.lax.scan(body, init, (per_example, given["loss_target"]))
    with _jax.named_scope("update"):
        delta_w, new_m, new_v = {}, {}, {}
        for n in TWIN_WEIGHTS:
            delta_w[n], new_m[n], new_v[n] = _adamw(weights[n], grad_w[n], given["m_" + n], given["v_" + n])
    return (loss, grad_x, *[grad_w[n] for n in TWIN_WEIGHTS], *[delta_w[n] for n in TWIN_WEIGHTS],
            *[new_m[n] for n in TWIN_WEIGHTS], *[new_v[n] for n in TWIN_WEIGHTS])
```

```python
import math

import jax
import jax.numpy as jnp
from jax import lax
from jax.experimental import pallas as pl
from jax.experimental.pallas import tpu as pltpu

F32 = jnp.float32
BF16 = jnp.bfloat16

N_DEV = 8
D_MODEL = 1024
D_FF = 2816
PLE_DIM = 256
HEAD_DIM = 64
SB_HEADS = 8
SWA_HEADS = 8
SWA_KV_HEADS = 2
SWA_GROUP = SWA_HEADS // SWA_KV_HEADS
WINDOW = 128
Q_BLOCK = 128
GDN_K_HEADS = 8
GDN_V_HEADS = 16
GDN_HEAD_DIM = 128
GDN_CONV = 4
GDN_CHUNK = 64
EPS = 1e-6
SB_W = SB_HEADS * HEAD_DIM
SWA_QW = SWA_HEADS * HEAD_DIM
SWA_KVW = SWA_KV_HEADS * HEAD_DIM
ATT_IN = 3 * SB_W + SWA_QW + 2 * SWA_KVW
GDN_KW = GDN_K_HEADS * GDN_HEAD_DIM
GDN_VW = GDN_V_HEADS * GDN_HEAD_DIM
GDN_CONV_W = 2 * GDN_KW + GDN_VW
GDN_IN = GDN_CONV_W + GDN_VW + 2 * GDN_V_HEADS
GDN_IN_PAD = GDN_CONV_W + GDN_VW + 2 * 128

ADAM_LR = 0.001
ADAM_B1 = 0.9
ADAM_B2 = 0.999
ADAM_EPS = 1e-08
ADAM_WD = 0.01
ADAM_STEP = 10

LANE = 128
VMEM_LIMIT = 56 * 1024 * 1024
PACK_W = 1024
PACK_ROW_ALIGN = 1024

NN = ((1,), (0,))
NT = ((1,), (1,))
TN = ((0,), (0,))

SHARDED = (
    ("ffn_norm", 2), ("gdn_conv_w", 2),
    ("ffn_w_gate", 3), ("ffn_w_up", 3), ("ffn_w_down", 2), ("att_w_in", 2), ("att_w_out", 1),
    ("gdn_w_in", 2), ("gdn_w_out", 1), ("ple_w_gate", 1), ("ple_w_proj", 2),
)
SMALL_SHARDED = 2
REPLICATED = ("mix_norm", "att_q_norm", "att_k_norm", "att_sinks", "gdn_a_log", "gdn_dt_bias",
              "gdn_out_norm", "ple_norm")
WEIGHTS = ("ffn_norm", "ffn_w_gate", "ffn_w_up", "ffn_w_down", "mix_norm", "att_w_in", "att_q_norm",
           "att_k_norm", "att_sinks", "att_w_out", "gdn_w_in", "gdn_conv_w", "gdn_a_log", "gdn_dt_bias",
           "gdn_out_norm", "gdn_w_out", "ple_norm", "ple_w_gate", "ple_w_proj")


def _pcall(body, **kw):
    return pl.pallas_call(body, **kw)


def _params(sem=None):
    if sem is None:
        return pltpu.CompilerParams(vmem_limit_bytes=VMEM_LIMIT)
    return pltpu.CompilerParams(dimension_semantics=sem, vmem_limit_bytes=VMEM_LIMIT)


def _dot(a, b, dims=NN):
    return lax.dot_general(a, b, (dims, ((), ())), preferred_element_type=F32)


def _bdot(a, b, dims=NN):
    return _dot(a.astype(BF16), b.astype(BF16), dims)


def _split(a):
    hi = a.astype(BF16)
    lo = (a - hi.astype(F32)).astype(BF16)
    return hi, lo


def _dot3(a, b, dims=NN):
    ah, al = _split(a)
    bh, bl = _split(b)
    return _dot(ah, bh, dims) + (_dot(ah, bl, dims) + _dot(al, bh, dims))


def _dot2m(a, m, dims=NN):
    ah, al = _split(a)
    return _dot(ah, m, dims) + _dot(al, m, dims)


def _mdot2(m, a, dims=NN):
    ah, al = _split(a)
    return _dot(m, ah, dims) + _dot(m, al, dims)


def _sigmoid(x):
    return 1.0 / (1.0 + jnp.exp(-x))


def _softplus(x):
    return jnp.maximum(x, 0.0) + jnp.log(1.0 + jnp.exp(-jnp.abs(x)))


def _pick(n, cap):
    if n <= cap:
        return n
    for t in range(cap - cap % LANE, 0, -LANE):
        if n % t == 0:
            return t
    raise ValueError(f"no tile for {n} under {cap}")


def _iota2(shape, axis):
    return lax.broadcasted_iota(jnp.int32, shape, axis)


def _mm(a, b, mode, out_dtype=F32, res=None, alpha=1.0, a2=None, b2=None, name="mm"):
    if mode == "nn":
        (M, K), N = a.shape, b.shape[1]
    elif mode == "nt":
        (M, K), N = a.shape, b.shape[0]
    else:
        (K, M), N = a.shape, b.shape[1]
    tm, tn, tk = _pick(M, 512), _pick(N, 1408), _pick(K, 1408)
    nk = K // tk
    dims = {"nn": NN, "nt": NT, "tn": TN}[mode]
    a_spec = pl.BlockSpec((tk, tm), lambda i, j, k: (k, i)) if mode == "tn" else pl.BlockSpec((tm, tk), lambda i, j, k: (i, k))
    b_spec = pl.BlockSpec((tn, tk), lambda i, j, k: (j, k)) if mode == "nt" else pl.BlockSpec((tk, tn), lambda i, j, k: (k, j))
    o_spec = pl.BlockSpec((tm, tn), lambda i, j, k: (i, j))
    two = a2 is not None
    has_res = res is not None

    def body(*refs):
        refs = list(refs)
        a_ref, b_ref = refs[0], refs[1]
        pos = 2
        if two:
            a2_ref, b2_ref = refs[2], refs[3]
            pos = 4
        if has_res:
            res_ref = refs[pos]
            pos += 1
        o_ref, acc_ref = refs[pos], refs[pos + 1]
        k = pl.program_id(2)
        part = _bdot(a_ref[...], b_ref[...], dims)
        if two:
            part = part + _bdot(a2_ref[...], b2_ref[...], dims)

        def finish(acc):
            out = acc * alpha if alpha != 1.0 else acc
            if has_res:
                out = res_ref[...] + out
            o_ref[...] = out.astype(out_dtype)

        if nk == 1:
            finish(part)
        else:
            @pl.when(k == 0)
            def _():
                acc_ref[...] = part

            @pl.when(k > 0)
            def _():
                acc_ref[...] += part

            @pl.when(k == nk - 1)
            def _():
                finish(acc_ref[...])

    ins = [a, b]
    specs = [a_spec, b_spec]
    if two:
        ins += [a2, b2]
        specs += [a_spec, b_spec]
    if has_res:
        ins.append(res)
        specs.append(o_spec)
    return _pcall(
        body, name=name, grid=(M // tm, N // tn, nk), in_specs=specs, out_specs=o_spec,
        out_shape=jax.ShapeDtypeStruct((M, N), out_dtype),
        scratch_shapes=[pltpu.VMEM((tm, tn) if nk > 1 else (8, LANE), F32)],
        compiler_params=_params(("parallel", "parallel", "arbitrary")),
    )(*ins)


ROW_TILE = 256


def _rms_fwd(h, g, name):
    T, D = h.shape
    tr = _pick(T, ROW_TILE)

    def body(h_ref, g_ref, n_ref):
        x = h_ref[...]
        r = lax.rsqrt(jnp.mean(x * x, axis=-1, keepdims=True) + EPS)
        n_ref[...] = (x * r * g_ref[...]).astype(BF16)

    return _pcall(
        body, name=name, grid=(T // tr,),
        in_specs=[pl.BlockSpec((tr, D), lambda i: (i, 0)), pl.BlockSpec((1, D), lambda i: (0, 0))],
        out_specs=pl.BlockSpec((tr, D), lambda i: (i, 0)),
        out_shape=jax.ShapeDtypeStruct((T, D), BF16), compiler_params=_params(("parallel",)),
    )(h, g.reshape(1, D))


def _rms_bwd(dn, h, g, dres, name):
    T, D = h.shape
    tr = _pick(T, ROW_TILE)

    def body(dn_ref, h_ref, g_ref, dres_ref, dh_ref, dg_ref):
        x = h_ref[...]
        r = lax.rsqrt(jnp.mean(x * x, axis=-1, keepdims=True) + EPS)
        xh = x * r
        d = dn_ref[...].astype(F32)
        dxh = d * g_ref[...]
        dh_ref[...] = dres_ref[...] + r * (dxh - xh * jnp.mean(dxh * xh, axis=-1, keepdims=True))
        part = jnp.sum(d * xh, axis=0, keepdims=True)

        @pl.when(pl.program_id(0) == 0)
        def _():
            dg_ref[...] = part

        @pl.when(pl.program_id(0) > 0)
        def _():
            dg_ref[...] += part

    row = pl.BlockSpec((tr, D), lambda i: (i, 0))
    vec = pl.BlockSpec((1, D), lambda i: (0, 0))
    dh, dg = _pcall(
        body, name=name, grid=(T // tr,), in_specs=[row, row, vec, row], out_specs=[row, vec],
        out_shape=[jax.ShapeDtypeStruct((T, D), F32), jax.ShapeDtypeStruct((1, D), F32)],
        compiler_params=_params(("arbitrary",)),
    )(dn, h, g.reshape(1, D), dres)
    return dh, dg.reshape(D)


def _gateup(n, wg, wu, name):
    T, D = n.shape
    F = wg.shape[1]
    tm, tn = _pick(T, 512), _pick(F, 1408)

    def body(n_ref, wg_ref, wu_ref, a_ref, b_ref, hid_ref):
        x = n_ref[...]
        a = _dot(x, wg_ref[...])
        b = _dot(x, wu_ref[...])
        a_ref[...] = a.astype(BF16)
        b_ref[...] = b.astype(BF16)
        hid_ref[...] = (a * _sigmoid(a) * b).astype(BF16)

    o_spec = pl.BlockSpec((tm, tn), lambda i, j: (i, j))
    w_spec = pl.BlockSpec((D, tn), lambda i, j: (0, j))
    sh = jax.ShapeDtypeStruct((T, F), BF16)
    return _pcall(
        body, name=name, grid=(T // tm, F // tn),
        in_specs=[pl.BlockSpec((tm, D), lambda i, j: (i, 0)), w_spec, w_spec],
        out_specs=[o_spec, o_spec, o_spec], out_shape=[sh, sh, sh],
        compiler_params=_params(("parallel", "parallel")),
    )(n, wg, wu)


def _ffn_dhid(dy, wd, a, b, name):
    T, D = dy.shape
    F = wd.shape[0]
    tm, tn = _pick(T, 512), _pick(F, 1408)

    def body(dy_ref, wd_ref, a_ref, b_ref, da_ref, db_ref):
        dhid = 0.5 * _bdot(dy_ref[...], wd_ref[...], NT)
        av = a_ref[...].astype(F32)
        bv = b_ref[...].astype(F32)
        s = _sigmoid(av)
        da_ref[...] = (dhid * bv * s * (1.0 + av * (1.0 - s))).astype(BF16)
        db_ref[...] = (dhid * av * s).astype(BF16)

    o_spec = pl.BlockSpec((tm, tn), lambda i, j: (i, j))
    sh = jax.ShapeDtypeStruct((T, F), BF16)
    return _pcall(
        body, name=name, grid=(T // tm, F // tn),
        in_specs=[pl.BlockSpec((tm, D), lambda i, j: (i, 0)), pl.BlockSpec((tn, D), lambda i, j: (j, 0)), o_spec, o_spec],
        out_specs=[o_spec, o_spec], out_shape=[sh, sh],
        compiler_params=_params(("parallel", "parallel")),
    )(dy, wd, a, b)


def _ffn_fwd(h, g, wg, wu, wd, tag):
    n = _rms_fwd(h, g, f"{tag}_norm")
    a, b, hid = _gateup(n, wg, wu, f"{tag}_gateup")
    h2 = _mm(hid, wd, "nn", res=h, alpha=0.5, name=f"{tag}_down")
    return h2, (h, n, a, b, hid)


def _ffn_bwd(dh2, saved, g, wg, wu, wd, tag):
    h, n, a, b, hid = saved
    da, db = _ffn_dhid(dh2, wd, a, b, f"{tag}_dhid")
    dwd = _mm(hid, dh2, "tn", alpha=0.5, name=f"{tag}_dwd")
    dwg = _mm(n, da, "tn", name=f"{tag}_dwg")
    dwu = _mm(n, db, "tn", name=f"{tag}_dwu")
    dn = _mm(da, wg, "nt", a2=db, b2=wu, name=f"{tag}_dn")
    dh, dg = _rms_bwd(dn, h, g, dh2, f"{tag}_dnorm")
    return dh, dg, dwg, dwu, dwd


def _ple_fwd(h, p, g, w_gate, w_proj, tag):
    T, D = h.shape
    pn = _rms_fwd(h, g, f"{tag}_norm")
    tm, tn = _pick(T, 512), _pick(D, 1024)
    P = p.shape[1]

    def body(pn_ref, p_ref, wg_ref, wp_ref, h_ref, o_ref, gl_ref, pp_ref):
        gl = _dot(pn_ref[...], wg_ref[...])
        pp = _bdot(p_ref[...], wp_ref[...])
        gl_ref[...] = gl
        pp_ref[...] = pp
        o_ref[...] = h_ref[...] + _sigmoid(gl) * pp

    o_spec = pl.BlockSpec((tm, tn), lambda i, j: (i, j))
    sh = jax.ShapeDtypeStruct((T, D), F32)
    h2, gl, pp = _pcall(
        body, name=f"{tag}_fwd", grid=(T // tm, D // tn),
        in_specs=[pl.BlockSpec((tm, D), lambda i, j: (i, 0)), pl.BlockSpec((tm, P), lambda i, j: (i, 0)),
                  pl.BlockSpec((D, tn), lambda i, j: (0, j)), pl.BlockSpec((P, tn), lambda i, j: (0, j)), o_spec],
        out_specs=[o_spec, o_spec, o_spec], out_shape=[sh, sh, sh],
        compiler_params=_params(("parallel", "parallel")),
    )(pn, p, w_gate, w_proj, h)
    return h2, (h, pn, gl, pp)


def _ple_bwd(dh2, saved, p, g, w_gate, tag):
    h, pn, gl, pp = saved
    T, D = h.shape
    tr = _pick(T, ROW_TILE)

    def body(d_ref, gl_ref, pp_ref, dgl_ref, dpp_ref):
        d = d_ref[...]
        s = _sigmoid(gl_ref[...])
        dpp_ref[...] = (d * s).astype(BF16)
        dgl_ref[...] = (d * pp_ref[...] * s * (1.0 - s)).astype(BF16)

    row = pl.BlockSpec((tr, D), lambda i: (i, 0))
    sh = jax.ShapeDtypeStruct((T, D), BF16)
    dgl, dpp = _pcall(body, name=f"{tag}_dgate", grid=(T // tr,), in_specs=[row, row, row], out_specs=[row, row],
                      out_shape=[sh, sh], compiler_params=_params(("parallel",)))(dh2, gl, pp)
    dw_proj = _mm(p, dpp, "tn", name=f"{tag}_dwproj")
    dw_gate = _mm(pn, dgl, "tn", name=f"{tag}_dwgate")
    dpn = _mm(dgl, w_gate, "nt", name=f"{tag}_dpn")
    dh, dg = _rms_bwd(dpn, h, g, dh2, f"{tag}_dnorm")
    return dh, dg, dw_gate, dw_proj


def _loss_head(y, target):
    T, D = y.shape
    tr = _pick(T, ROW_TILE)

    def body(y_ref, t_ref, dy_ref, l_ref):
        e = y_ref[...] - t_ref[...]
        dy_ref[...] = e * (1.0 / D)
        part = jnp.sum(e * e, axis=0, keepdims=True)

        @pl.when(pl.program_id(0) == 0)
        def _():
            l_ref[...] = part

        @pl.when(pl.program_id(0) > 0)
        def _():
            l_ref[...] += part

    row = pl.BlockSpec((tr, D), lambda i: (i, 0))
    vec = pl.BlockSpec((1, D), lambda i: (0, 0))
    dy, l = _pcall(body, name="loss_head", grid=(T // tr,), in_specs=[row, row], out_specs=[row, vec],
                   out_shape=[jax.ShapeDtypeStruct((T, D), F32), jax.ShapeDtypeStruct((1, D), F32)],
                   compiler_params=_params(("arbitrary",)))(y, target)
    return (0.5 / D) * jnp.sum(l), dy


def _sb_masks(i, j):
    row = _iota2((Q_BLOCK, Q_BLOCK), 0)
    col = _iota2((Q_BLOCK, Q_BLOCK), 1)
    valid = (col + j * Q_BLOCK) < (row + i * Q_BLOCK)
    after = (row > col).astype(BF16)
    return valid, after, col


def _sb_fwd(q, k, v):
    H, T, d = q.shape
    nblk = T // Q_BLOCK
    scale = d ** -0.5

    def body(q_ref, k_ref, v_ref, o_ref, c_ref, run_ref):
        i = pl.program_id(1)
        qv = q_ref[...]
        o_ref[...] = jnp.zeros_like(o_ref)
        c_ref[...] = jnp.zeros_like(c_ref)
        run_ref[...] = jnp.zeros_like(run_ref)

        @pl.loop(0, i + 1)
        def _(jj):
            j = i - jj
            off = pl.multiple_of(j * Q_BLOCK, Q_BLOCK)
            kj = k_ref[pl.ds(off, Q_BLOCK), :]
            vj = v_ref[pl.ds(off, Q_BLOCK), :]
            valid, after, col = _sb_masks(i, j)
            c = run_ref[...]
            z = _dot(qv, kj, NT) * scale
            sp = _softplus(z)
            lk = jnp.where(valid, -sp, 0.0)
            between = _dot2m(lk, after)
            w = jnp.where(valid, jnp.exp((z - sp) + between + c), 0.0)
            o_ref[...] += _bdot(w, vj)
            c_ref[...] = jnp.where(col == j, c, c_ref[...])
            run_ref[...] = c + jnp.sum(lk, axis=1, keepdims=True)

    blk = pl.BlockSpec((None, Q_BLOCK, d), lambda h, i: (h, i, 0))
    full = pl.BlockSpec((None, T, d), lambda h, i: (h, 0, 0))
    return _pcall(
        body, name="sb_fwd", grid=(H, nblk), in_specs=[blk, full, full],
        out_specs=[blk, pl.BlockSpec((None, Q_BLOCK, LANE), lambda h, i: (h, i, 0))],
        out_shape=[jax.ShapeDtypeStruct((H, T, d), F32), jax.ShapeDtypeStruct((H, T, LANE), F32)],
        scratch_shapes=[pltpu.VMEM((Q_BLOCK, 1), F32)],
        compiler_params=_params(("parallel", "parallel")),
    )(q, k, v)


def _sb_bwd(q, k, v, carry, do):
    H, T, d = q.shape
    nblk = T // Q_BLOCK
    scale = d ** -0.5

    def body(q_ref, k_ref, v_ref, c_ref, do_ref, dq_ref, dk_ref, dv_ref, run_ref):
        i = pl.program_id(1)

        @pl.when(i == 0)
        def _():
            dk_ref[...] = jnp.zeros_like(dk_ref)
            dv_ref[...] = jnp.zeros_like(dv_ref)

        qv = q_ref[...]
        dov = do_ref[...].astype(BF16)
        cm = c_ref[...]
        dq_ref[...] = jnp.zeros_like(dq_ref)
        run_ref[...] = jnp.zeros_like(run_ref)

        @pl.loop(0, i + 1)
        def _(j):
            gsum = run_ref[...]
            off = pl.multiple_of(j * Q_BLOCK, Q_BLOCK)
            kj = k_ref[pl.ds(off, Q_BLOCK), :]
            vj = v_ref[pl.ds(off, Q_BLOCK), :]
            valid, after, col = _sb_masks(i, j)
            before = (_iota2((Q_BLOCK, Q_BLOCK), 0) < col).astype(BF16)
            z = _dot(qv, kj, NT) * scale
            sp = _softplus(z)
            lk = jnp.where(valid, -sp, 0.0)
            between = _dot2m(lk, after)
            c = jnp.sum(jnp.where(col == j, cm, 0.0), axis=1, keepdims=True)
            ls = z - sp
            w = jnp.where(valid, jnp.exp(ls + between + c), 0.0)
            dw = _dot(dov, vj, NT)
            g = dw * w
            gpre = gsum + _dot2m(g, before)
            sig = jnp.exp(ls)
            dz = jnp.where(valid, g * (1.0 - sig) - sig * gpre, 0.0) * scale
            dzb = dz.astype(BF16)
            dq_ref[...] += _dot(dzb, kj)
            dk_ref[pl.ds(off, Q_BLOCK), :] += _dot(dzb, qv, TN)
            dv_ref[pl.ds(off, Q_BLOCK), :] += _dot(w.astype(BF16), dov, TN)
            run_ref[...] = gsum + jnp.sum(g, axis=1, keepdims=True)

    blk = pl.BlockSpec((None, Q_BLOCK, d), lambda h, i: (h, i, 0))
    full = pl.BlockSpec((None, T, d), lambda h, i: (h, 0, 0))
    sh = jax.ShapeDtypeStruct((H, T, d), F32)
    return _pcall(
        body, name="sb_bwd", grid=(H, nblk),
        in_specs=[blk, full, full, pl.BlockSpec((None, Q_BLOCK, LANE), lambda h, i: (h, i, 0)), blk],
        out_specs=[blk, full, full], out_shape=[sh, sh, sh],
        scratch_shapes=[pltpu.VMEM((Q_BLOCK, 1), F32)],
        compiler_params=_params(("parallel", "arbitrary")),
    )(q, k, v, carry, do)


def _swa_fwd(q, k, v, qg, kg, sinks, slopes):
    Hq, T, d = q.shape
    Hkv = k.shape[0]
    G = Hq // Hkv
    W = WINDOW
    nblk = T // W
    scale = d ** -0.5

    def body(q_ref, kp_ref, kc_ref, vp_ref, vc_ref, qg_ref, kg_ref, sk_ref, sl_ref, o_ref):
        hk = pl.program_id(0)
        n = pl.program_id(1)
        kcat = jnp.concatenate([kp_ref[...], kc_ref[...]], axis=0)
        vcat = jnp.concatenate([vp_ref[...], vc_ref[...]], axis=0).astype(BF16)
        rk = lax.rsqrt(jnp.mean(kcat * kcat, axis=-1, keepdims=True) + EPS)
        kn = (kcat * rk * kg_ref[...]).astype(BF16)
        row = _iota2((W, 2 * W), 0)
        col = _iota2((W, 2 * W), 1)
        dist = row + W - col
        valid = (dist >= 0) & (dist < W) & ((n > 0) | (col >= W))
        distf = dist.astype(F32)
        for g in range(G):
            qh = q_ref[g]
            rq = lax.rsqrt(jnp.mean(qh * qh, axis=-1, keepdims=True) + EPS)
            qn = (qh * rq * qg_ref[...]).astype(BF16)
            sink = sk_ref[pl.ds(hk * G + g, 1), :][:, :1]
            slope = sl_ref[pl.ds(hk * G + g, 1), :][:, :1]
            s = _dot(qn, kn, NT) * scale - slope * distf
            s = jnp.where(valid, s, -1e30)
            m = jnp.maximum(jnp.max(s, axis=1, keepdims=True), sink)
            p = jnp.where(valid, jnp.exp(s - m), 0.0)
            den = jnp.sum(p, axis=1, keepdims=True) + jnp.exp(sink - m)
            o_ref[g] = _bdot(p / den, vcat)

    qblk = pl.BlockSpec((G, W, d), lambda h, n: (h, n, 0))
    prev = pl.BlockSpec((None, W, d), lambda h, n: (h, jnp.maximum(n - 1, 0), 0))
    cur = pl.BlockSpec((None, W, d), lambda h, n: (h, n, 0))
    gain = pl.BlockSpec((1, d), lambda h, n: (0, 0))
    perhead = pl.BlockSpec((Hq, LANE), lambda h, n: (0, 0))
    return _pcall(
        body, name="swa_fwd", grid=(Hkv, nblk),
        in_specs=[qblk, prev, cur, prev, cur, gain, gain, perhead, perhead], out_specs=qblk,
        out_shape=jax.ShapeDtypeStruct((Hq, T, d), F32), compiler_params=_params(("parallel", "parallel")),
    )(q, k, k, v, v, qg, kg, sinks, slopes)


def _swa_bwd(q, k, v, qg, kg, sinks, slopes, do):
    Hq, T, d = q.shape
    Hkv = k.shape[0]
    G = Hq // Hkv
    W = WINDOW
    nblk = T // W
    scale = d ** -0.5

    def body(q_ref, kp_ref, kc_ref, vp_ref, vc_ref, qg_ref, kg_ref, sk_ref, sl_ref, do_ref,
             dq_ref, dk_ref, dv_ref, dqg_ref, dkg_ref, dsk_ref):
        hk = pl.program_id(0)
        n = pl.program_id(1)

        @pl.when((hk == 0) & (n == 0))
        def _():
            dqg_ref[...] = jnp.zeros_like(dqg_ref)
            dkg_ref[...] = jnp.zeros_like(dkg_ref)
            dsk_ref[...] = jnp.zeros_like(dsk_ref)

        @pl.when(n == 0)
        def _():
            dk_ref[...] = jnp.zeros_like(dk_ref)
            dv_ref[...] = jnp.zeros_like(dv_ref)

        kcat = jnp.concatenate([kp_ref[...], kc_ref[...]], axis=0)
        vcat = jnp.concatenate([vp_ref[...], vc_ref[...]], axis=0).astype(BF16)
        rk = lax.rsqrt(jnp.mean(kcat * kcat, axis=-1, keepdims=True) + EPS)
        kh = kcat * rk
        kn = (kh * kg_ref[...]).astype(BF16)
        row = _iota2((W, 2 * W), 0)
        col = _iota2((W, 2 * W), 1)
        dist = row + W - col
        valid = (dist >= 0) & (dist < W) & ((n > 0) | (col >= W))
        distf = dist.astype(F32)
        rowh = _iota2((Hq, LANE), 0)
        dkn = jnp.zeros((2 * W, d), F32)
        dvc = jnp.zeros((2 * W, d), F32)
        dqg = jnp.zeros((1, d), F32)
        dsk = jnp.zeros((Hq, LANE), F32)
        for g in range(G):
            qh = q_ref[g]
            rq = lax.rsqrt(jnp.mean(qh * qh, axis=-1, keepdims=True) + EPS)
            qhh = qh * rq
            qn = (qhh * qg_ref[...]).astype(BF16)
            sink = sk_ref[pl.ds(hk * G + g, 1), :][:, :1]
            slope = sl_ref[pl.ds(hk * G + g, 1), :][:, :1]
            s = _dot(qn, kn, NT) * scale - slope * distf
            s = jnp.where(valid, s, -1e30)
            m = jnp.maximum(jnp.max(s, axis=1, keepdims=True), sink)
            p = jnp.where(valid, jnp.exp(s - m), 0.0)
            esink = jnp.exp(sink - m)
            den = jnp.sum(p, axis=1, keepdims=True) + esink
            prob = p / den
            dov = do_ref[g].astype(BF16)
            dp = _dot(dov, vcat, NT)
            dd = jnp.sum(prob * dp, axis=1, keepdims=True)
            ds = prob * (dp - dd)
            dsink = -jnp.sum((esink / den) * dd, axis=0, keepdims=True)
            dsk = dsk + jnp.where(rowh == hk * G + g, dsink, 0.0)
            dsb = (ds * scale).astype(BF16)
            dqn = _dot(dsb, kn)
            dkn = dkn + _dot(dsb, qn, TN)
            dvc = dvc + _dot(prob.astype(BF16), dov, TN)
            dqh = dqn * qg_ref[...]
            dq_ref[g] = rq * (dqh - qhh * jnp.mean(dqh * qhh, axis=-1, keepdims=True))
            dqg = dqg + jnp.sum(dqn * qhh, axis=0, keepdims=True)
        dkh = dkn * kg_ref[...]
        dkraw = rk * (dkh - kh * jnp.mean(dkh * kh, axis=-1, keepdims=True))
        dqg_ref[...] += dqg
        dkg_ref[...] += jnp.sum(dkn * kh, axis=0, keepdims=True)
        dsk_ref[...] += dsk
        offp = pl.multiple_of(jnp.maximum(n - 1, 0) * W, W)
        offc = pl.multiple_of(n * W, W)
        dk_ref[pl.ds(offp, W), :] += dkraw[:W]
        dv_ref[pl.ds(offp, W), :] += dvc[:W]
        dk_ref[pl.ds(offc, W), :] += dkraw[W:]
        dv_ref[pl.ds(offc, W), :] += dvc[W:]

    qblk = pl.BlockSpec((G, W, d), lambda h, n: (h, n, 0))
    prev = pl.BlockSpec((None, W, d), lambda h, n: (h, jnp.maximum(n - 1, 0), 0))
    cur = pl.BlockSpec((None, W, d), lambda h, n: (h, n, 0))
    gain = pl.BlockSpec((1, d), lambda h, n: (0, 0))
    perhead = pl.BlockSpec((Hq, LANE), lambda h, n: (0, 0))
    full = pl.BlockSpec((None, T, d), lambda h, n: (h, 0, 0))
    kv = jax.ShapeDtypeStruct((Hkv, T, d), F32)
    gs = jax.ShapeDtypeStruct((1, d), F32)
    return _pcall(
        body, name="swa_bwd", grid=(Hkv, nblk),
        in_specs=[qblk, prev, cur, prev, cur, gain, gain, perhead, perhead, qblk],
        out_specs=[qblk, full, full, gain, gain, perhead],
        out_shape=[jax.ShapeDtypeStruct((Hq, T, d), F32), kv, kv, gs, gs, jax.ShapeDtypeStruct((Hq, LANE), F32)],
        compiler_params=_params(("arbitrary", "arbitrary")),
    )(q, k, k, v, v, qg, kg, sinks, slopes, do)


def _heads(z, n):
    T = z.shape[0]
    return z.reshape(T, n, HEAD_DIM).transpose(1, 0, 2)


def _unheads(z):
    n, T, d = z.shape
    return z.transpose(1, 0, 2).reshape(T, n * d)


def _alibi():
    s = [2.0 ** (-8.0 * (i + 1) / SWA_HEADS) for i in range(SWA_HEADS)]
    return jnp.broadcast_to(jnp.asarray(s, F32)[:, None], (SWA_HEADS, LANE))


def _att_fwd(h, g, w_in, w_out, q_gain, k_gain, sinks):
    hn = _rms_fwd(h, g, "att_norm")
    proj = _mm(hn, w_in, "nn", name="att_in")
    c = [0, SB_W, 2 * SB_W, 3 * SB_W, 3 * SB_W + SWA_QW, 3 * SB_W + SWA_QW + SWA_KVW, ATT_IN]
    sq, sk, sv = (_heads(proj[:, c[i]:c[i + 1]], SB_HEADS).astype(BF16) for i in range(3))
    bq = _heads(proj[:, c[3]:c[4]], SWA_HEADS)
    bk = _heads(proj[:, c[4]:c[5]], SWA_KV_HEADS)
    bv = _heads(proj[:, c[5]:c[6]], SWA_KV_HEADS)
    a_out, carry = _sb_fwd(sq, sk, sv)
    sk128 = jnp.broadcast_to(sinks.reshape(SWA_HEADS, 1), (SWA_HEADS, LANE))
    qg, kg = q_gain.reshape(1, HEAD_DIM), k_gain.reshape(1, HEAD_DIM)
    b_out = _swa_fwd(bq, bk, bv, qg, kg, sk128, _alibi())
    o = jnp.concatenate([_unheads(a_out), _unheads(b_out)], axis=-1).astype(BF16)
    h2 = _mm(o, w_out, "nn", res=h, name="att_out")
    return h2, (h, hn, sq, sk, sv, bq, bk, bv, carry, o, sk128, qg, kg)


def _att_bwd(dh2, saved, g, w_in, w_out):
    h, hn, sq, sk, sv, bq, bk, bv, carry, o, sk128, qg, kg = saved
    do = _mm(dh2, w_out, "nt", name="att_do")
    dw_out = _mm(o, dh2, "tn", name="att_dwout")
    da = _heads(do[:, :SB_W], SB_HEADS)
    db = _heads(do[:, SB_W:], SWA_HEADS)
    dsq, dsk, dsv = _sb_bwd(sq, sk, sv, carry, da)
    dbq, dbk, dbv, dqg, dkg, dsink = _swa_bwd(bq, bk, bv, qg, kg, sk128, _alibi(), db)
    dproj = jnp.concatenate([_unheads(z) for z in (dsq, dsk, dsv, dbq, dbk, dbv)], axis=-1).astype(BF16)
    dw_in = _mm(hn, dproj, "tn", name="att_dwin")
    dhn = _mm(dproj, w_in, "nt", name="att_dhn")
    dh, dg = _rms_bwd(dhn, h, g, dh2, "att_dnorm")
    return dh, dg, dw_in, dw_out, dqg.reshape(HEAD_DIM), dkg.reshape(HEAD_DIM), dsink[:, 0]


CONV_ROWS = 512
CONV_COLS = 512
HALO = 8


def _shifted(xcat, s, tm):
    if s == 0:
        return xcat[HALO:HALO + tm]
    return pltpu.roll(xcat, s, 0)[HALO:HALO + tm]


def _conv_pre(x_ref, halo_ref, w_ref, i, tm):
    xc = x_ref[...]
    halo = jnp.where(i > 0, halo_ref[...], 0.0)
    xcat = jnp.concatenate([halo, xc], axis=0)
    w = w_ref[...]
    y = w[GDN_CONV - 1:GDN_CONV] * xc
    for kk in range(GDN_CONV - 1):
        y = y + w[kk:kk + 1] * _shifted(xcat, GDN_CONV - 1 - kk, tm)
    return xcat, y


def _l2_heads(s, qscale_of):
    outs, rs = [], []
    for hh in range(s.shape[1] // GDN_HEAD_DIM):
        sh = s[:, hh * GDN_HEAD_DIM:(hh + 1) * GDN_HEAD_DIM]
        r = lax.rsqrt(jnp.sum(sh * sh, axis=-1, keepdims=True) + EPS)
        outs.append(sh * r)
        rs.append(r)
    return outs, rs


def _conv_specs(T, col0, tm, tc):
    cur = pl.BlockSpec((tm, tc), lambda j, i: (i, j + col0 // tc))
    halo = pl.BlockSpec((HALO, tc), lambda j, i: (jnp.maximum(i * (tm // HALO) - 1, 0), j + col0 // tc))
    wsp = pl.BlockSpec((GDN_CONV, tc), lambda j, i: (0, j + col0 // tc))
    out = pl.BlockSpec((tm, tc), lambda j, i: (i, j))
    return cur, halo, wsp, out


def _conv_fwd(proj, conv_w, col0, width, norm, name):
    T = proj.shape[0]
    tm, tc = _pick(T, CONV_ROWS), CONV_COLS
    cur, halo, wsp, out = _conv_specs(T, col0, tm, tc)
    n_q_tiles = (width // 2) // tc

    def body(x_ref, halo_ref, w_ref, o_ref):
        j, i = pl.program_id(0), pl.program_id(1)
        _, y = _conv_pre(x_ref, halo_ref, w_ref, i, tm)
        s = y * _sigmoid(y)
        if norm:
            outs, _ = _l2_heads(s, None)
            qs = jnp.where(j < n_q_tiles, GDN_HEAD_DIM ** -0.5, 1.0)
            o_ref[...] = jnp.concatenate(outs, axis=1) * qs
        else:
            o_ref[...] = s

    return _pcall(body, name=name, grid=(width // tc, T // tm), in_specs=[cur, halo, wsp], out_specs=out,
                  out_shape=jax.ShapeDtypeStruct((T, width), F32),
                  compiler_params=_params(("parallel", "parallel")))(proj, proj, conv_w)


def _conv_bwd_pre(proj, conv_w, dout, col0, width, norm, name):
    T = proj.shape[0]
    tm, tc = _pick(T, CONV_ROWS), CONV_COLS
    cur, halo, wsp, out = _conv_specs(T, col0, tm, tc)
    n_q_tiles = (width // 2) // tc

    def body(x_ref, halo_ref, w_ref, d_ref, dy_ref, dw_ref):
        j, i = pl.program_id(0), pl.program_id(1)
        xcat, y = _conv_pre(x_ref, halo_ref, w_ref, i, tm)
        sg = _sigmoid(y)
        s = y * sg
        d = d_ref[...]
        if norm:
            qs = jnp.where(j < n_q_tiles, GDN_HEAD_DIM ** -0.5, 1.0)
            d = d * qs
            outs, rs = _l2_heads(s, None)
            parts = []
            for hh, (nh, r) in enumerate(zip(outs, rs)):
                dh = d[:, hh * GDN_HEAD_DIM:(hh + 1) * GDN_HEAD_DIM]
                parts.append(r * (dh - nh * jnp.sum(dh * nh, axis=-1, keepdims=True)))
            ds = jnp.concatenate(parts, axis=1)
        else:
            ds = d
        dy = ds * sg * (1.0 + y * (1.0 - sg))
        dy_ref[...] = dy
        rows = [jnp.sum(dy * _shifted(xcat, GDN_CONV - 1 - kk, tm), axis=0, keepdims=True) for kk in range(GDN_CONV)]
        part = jnp.concatenate(rows, axis=0)

        @pl.when(i == 0)
        def _():
            dw_ref[...] = part

        @pl.when(i > 0)
        def _():
            dw_ref[...] += part

    wout = pl.BlockSpec((GDN_CONV, tc), lambda j, i: (0, j))
    return _pcall(body, name=name, grid=(width // tc, T // tm), in_specs=[cur, halo, wsp, out], out_specs=[out, wout],
                  out_shape=[jax.ShapeDtypeStruct((T, width), F32), jax.ShapeDtypeStruct((GDN_CONV, width), F32)],
                  compiler_params=_params(("parallel", "arbitrary")))(proj, proj, conv_w, dout)


def _conv_bwd_in(dy, conv_w, name):
    T, C = dy.shape
    tm, tc = _pick(T, CONV_ROWS), CONV_COLS
    nrow = T // tm

    def body(d_ref, nxt_ref, w_ref, dx_ref):
        i = pl.program_id(0)
        dc = d_ref[...]
        nxt = jnp.where(i < nrow - 1, nxt_ref[...], 0.0)
        dcat = jnp.concatenate([dc, nxt], axis=0)
        w = w_ref[...]
        dx = w[GDN_CONV - 1:GDN_CONV] * dc
        for kk in range(GDN_CONV - 1):
            s = GDN_CONV - 1 - kk
            dx = dx + w[kk:kk + 1] * pltpu.roll(dcat, tm + HALO - s, 0)[:tm]
        dx_ref[...] = dx.astype(BF16)

    cur = pl.BlockSpec((tm, tc), lambda i, j: (i, j))
    nxt = pl.BlockSpec((HALO, tc), lambda i, j: (jnp.minimum((i + 1) * (tm // HALO), T // HALO - 1), j))
    wsp = pl.BlockSpec((GDN_CONV, tc), lambda i, j: (0, j))
    return _pcall(body, name=name, grid=(nrow, C // tc), in_specs=[cur, nxt, wsp], out_specs=cur,
                  out_shape=jax.ShapeDtypeStruct((T, C), BF16),
                  compiler_params=_params(("parallel", "parallel")))(dy, dy, conv_w)


GATE_ROWS = 512


def _chunk_mask(n, lower):
    row = _iota2((n, n), 0)
    col = _iota2((n, n), 1)
    same = (row // GDN_CHUNK) == (col // GDN_CHUNK)
    tri = (row >= col) if lower else (row <= col)
    return (same & tri).astype(BF16)


def _gates_fwd(proj, a_log, dt_bias):
    T = proj.shape[0]
    tm = _pick(T, GATE_ROWS)
    c0 = (GDN_CONV_W + GDN_VW) // LANE

    def body(bl_ref, a_ref, alog_ref, dt_ref, beta_ref, g_ref, gc_ref):
        beta_ref[...] = _sigmoid(bl_ref[...])
        g = -jnp.exp(alog_ref[...]) * _softplus(a_ref[...] + dt_ref[...])
        g_ref[...] = g
        gc_ref[...] = _mdot2(_chunk_mask(tm, True), g)

    blk = lambda c: pl.BlockSpec((tm, LANE), lambda i: (i, c))
    vec = pl.BlockSpec((1, LANE), lambda i: (0, 0))
    sh = jax.ShapeDtypeStruct((T, LANE), F32)
    return _pcall(body, name="gdn_gates", grid=(T // tm,), in_specs=[blk(c0), blk(c0 + 1), vec, vec],
                  out_specs=[blk(0), blk(0), blk(0)], out_shape=[sh, sh, sh],
                  compiler_params=_params(("parallel",)))(proj, proj, a_log, dt_bias)


def _gates_bwd(proj, a_log, dt_bias, beta, g, dbeta, dgc):
    T = proj.shape[0]
    tm = _pick(T, GATE_ROWS)
    c0 = (GDN_CONV_W + GDN_VW) // LANE

    def body(a_ref, alog_ref, dt_ref, beta_ref, g_ref, dbeta_ref, dgc_ref, dbl_ref, da_ref, dalog_ref, ddt_ref):
        dg = _mdot2(_chunk_mask(tm, False), dgc_ref[...])
        b = beta_ref[...]
        dbl_ref[...] = (dbeta_ref[...] * b * (1.0 - b)).astype(BF16)
        da = dg * (-jnp.exp(alog_ref[...])) * _sigmoid(a_ref[...] + dt_ref[...])
        da_ref[...] = da.astype(BF16)
        p1 = jnp.sum(dg * g_ref[...], axis=0, keepdims=True)
        p2 = jnp.sum(da, axis=0, keepdims=True)

        @pl.when(pl.program_id(0) == 0)
        def _():
            dalog_ref[...] = p1
            ddt_ref[...] = p2

        @pl.when(pl.program_id(0) > 0)
        def _():
            dalog_ref[...] += p1
            ddt_ref[...] += p2

    blk = lambda c: pl.BlockSpec((tm, LANE), lambda i: (i, c))
    vec = pl.BlockSpec((1, LANE), lambda i: (0, 0))
    shb = jax.ShapeDtypeStruct((T, LANE), BF16)
    shv = jax.ShapeDtypeStruct((1, LANE), F32)
    return _pcall(body, name="gdn_dgates", grid=(T // tm,),
                  in_specs=[blk(c0 + 1), vec, vec, blk(0), blk(0), blk(0), blk(0)],
                  out_specs=[blk(0), blk(0), vec, vec], out_shape=[shb, shb, shv, shv],
                  compiler_params=_params(("arbitrary",)))(proj, a_log, dt_bias, beta, g, dbeta, dgc)


def _inv_unit_lower(L):
    C = L.shape[0]
    row = _iota2((C, C), 0)
    col = _iota2((C, C), 1)
    blk16 = (row // 16) == (col // 16)
    blk32 = (row // 32) == (col // 32)
    eye = (row == col).astype(F32)
    x = -jnp.where(blk16, L, 0.0)
    a = eye + x
    for _ in range(3):
        x = _dot3(x, x)
        a = a + _dot3(a, x)
    lo = jnp.where(blk32 & ~blk16, L, 0.0)
    a = a - _dot3(_dot3(a, lo), a)
    lo = jnp.where(~blk32, L, 0.0)
    a = a - _dot3(_dot3(a, lo), a)
    return a


def _gdn_chunk_common(q, k, beta, gcol, grow):
    C = GDN_CHUNK
    row = _iota2((C, C), 0)
    col = _iota2((C, C), 1)
    incl = row >= col
    strict = row > col
    dm = jnp.where(incl, jnp.exp(jnp.where(incl, gcol - grow, 0.0)), 0.0)
    kb = k * beta
    mm = _bdot(kb, k, NT)
    lmat = jnp.where(strict, mm * dm, 0.0)
    nmat = _bdot(q, k, NT)
    pmat = nmat * dm
    gam = jnp.exp(gcol)
    glast = grow[:, C - 1:C]
    tail = jnp.exp(glast - gcol)
    return dm, kb, mm, lmat, nmat, pmat, gam, jnp.exp(glast), tail, strict, incl


def _gdn_specs(T):
    C, D = GDN_CHUNK, GDN_HEAD_DIM
    n = T // C
    qk = pl.BlockSpec((C, D), lambda h, i: (i, h))
    v2 = pl.BlockSpec((C, 2 * D), lambda h, i: (i, h))
    colv = pl.BlockSpec((2, C, 1), lambda h, i: (h, i, 0))
    rowv = pl.BlockSpec((2, None, 1, C), lambda h, i: (h, i, 0, 0))
    st = pl.BlockSpec((2, None, D, D), lambda h, i: (h, i, 0, 0))
    am = pl.BlockSpec((2, None, C, C), lambda h, i: (h, i, 0, 0))
    return n, qk, v2, colv, rowv, st, am


def _gdn_fwd(q, k, v, beta, gcol, grow):
    T = q.shape[0]
    C, D = GDN_CHUNK, GDN_HEAD_DIM
    n, qk, v2, colv, rowv, st, am = _gdn_specs(T)

    def body(q_ref, k_ref, v_ref, b_ref, gc_ref, gr_ref, o_ref, s_ref, a_ref, vn_ref, state):
        @pl.when(pl.program_id(1) == 0)
        def _():
            state[...] = jnp.zeros_like(state)

        qv, kv = q_ref[...], k_ref[...]
        for e in range(2):
            beta, gcol, grow = b_ref[e], gc_ref[e], gr_ref[e]
            dm, kb, mm, lmat, nmat, pmat, gam, glast, tail, strict, incl = _gdn_chunk_common(qv, kv, beta, gcol, grow)
            a = _inv_unit_lower(lmat)
            s = state[e]
            s_ref[e] = s
            a_ref[e] = a
            vv = v_ref[:, e * D:(e + 1) * D]
            r = beta * (vv - _bdot(kv * gam, s))
            vn = _dot3(a, r)
            vn_ref[:, e * D:(e + 1) * D] = vn
            o_ref[:, e * D:(e + 1) * D] = _bdot(qv * gam, s) + _bdot(pmat, vn)
            state[e] = glast * s + _bdot(kv * tail, vn, TN)

    shv = jax.ShapeDtypeStruct((T, 2 * GDN_K_HEADS * D), F32)
    return _pcall(
        body, name="gdn_fwd", grid=(GDN_K_HEADS, n), in_specs=[qk, qk, v2, colv, colv, rowv],
        out_specs=[v2, st, am, v2],
        out_shape=[shv, jax.ShapeDtypeStruct((GDN_V_HEADS, n, D, D), F32),
                   jax.ShapeDtypeStruct((GDN_V_HEADS, n, C, C), F32), shv],
        scratch_shapes=[pltpu.VMEM((2, D, D), F32)],
        compiler_params=_params(("parallel", "arbitrary")),
    )(q, k, v, beta, gcol, grow)


def _gdn_bwd(q, k, v, beta, gcol, grow, states, amat, vnew, do):
    T = q.shape[0]
    C, D = GDN_CHUNK, GDN_HEAD_DIM
    n, qk, v2, colv, rowv, st, am = _gdn_specs(T)
    rev = lambda spec: pl.BlockSpec(spec.block_shape, (lambda f: (lambda h, i: f(h, n - 1 - i)))(spec.index_map))
    qk, v2, colv, rowv, st, am = (rev(s) for s in (qk, v2, colv, rowv, st, am))

    def body(q_ref, k_ref, v_ref, b_ref, gc_ref, gr_ref, s_ref, a_ref, vn_ref, do_ref,
             dq_ref, dk_ref, dv_ref, db_ref, dgc_ref, dstate):
        @pl.when(pl.program_id(1) == 0)
        def _():
            dstate[...] = jnp.zeros_like(dstate)

        qv, kv = q_ref[...], k_ref[...]
        ones = jnp.ones((C, LANE), BF16)
        lastrow = _iota2((C, 1), 0) == C - 1
        dq_tot = jnp.zeros((C, D), F32)
        dk_tot = jnp.zeros((C, D), F32)
        for e in range(2):
            beta, gcol, grow = b_ref[e], gc_ref[e], gr_ref[e]
            dm, kb, mm, lmat, nmat, pmat, gam, glast, tail, strict, incl = _gdn_chunk_common(qv, kv, beta, gcol, grow)
            pmat = jnp.where(incl, pmat, 0.0)
            s, a = s_ref[e], a_ref[e]
            vn = vn_ref[:, e * D:(e + 1) * D]
            dov = do_ref[:, e * D:(e + 1) * D]
            dsn = dstate[e]
            kd, qd, kt = kv * gam, qv * gam, kv * tail
            xres = v_ref[:, e * D:(e + 1) * D] - _bdot(kd, s)
            dvn = _bdot(pmat, dov, TN) + _bdot(kt, dsn)
            dr = _dot3(a, dvn, TN)
            drb = beta * dr
            dv_ref[:, e * D:(e + 1) * D] = drb
            dbeta = jnp.sum(dr * xres, axis=1, keepdims=True)
            dkd = -_bdot(drb, s, NT)
            dstate[e] = _bdot(qd, dov, TN) + glast * dsn - _bdot(kd, drb, TN)
            dqd = _bdot(dov, s, NT)
            dp = jnp.where(incl, _bdot(dov, vn, NT), 0.0)
            dl = -jnp.where(strict, _bdot(dr, vn, NT), 0.0)
            dmm = dl * dm
            dnn = dp * dm
            emat = dl * lmat + dp * pmat
            dkb = _bdot(dmm, kv)
            dkt = _bdot(vn, dsn, NT)
            dk_tot = dk_tot + (beta * dkb + _bdot(dmm, kb, TN) + _bdot(dnn, qv, TN) + gam * dkd + tail * dkt)
            dq_tot = dq_tot + (_bdot(dnn, kv) + gam * dqd)
            db_ref[e] = dbeta + jnp.sum(dkb * kv, axis=1, keepdims=True)
            colsum = _dot2m(emat, ones, TN)[:, :1]
            tails = jnp.sum(dkt * kt, axis=1, keepdims=True)
            dgc = (jnp.sum(emat, axis=1, keepdims=True) - colsum + jnp.sum(dkd * kd, axis=1, keepdims=True)
                   + jnp.sum(dqd * qd, axis=1, keepdims=True) - tails)
            dlast = jnp.sum(tails, axis=0, keepdims=True) + glast * jnp.sum(jnp.sum(s * dsn, axis=1, keepdims=True), axis=0, keepdims=True)
            dgc_ref[e] = dgc + jnp.where(lastrow, dlast, 0.0)
        dq_ref[...] = dq_tot
        dk_ref[...] = dk_tot

    shq = jax.ShapeDtypeStruct((T, GDN_K_HEADS * D), F32)
    shv = jax.ShapeDtypeStruct((T, GDN_V_HEADS * D), F32)
    shc = jax.ShapeDtypeStruct((GDN_V_HEADS, T, 1), F32)
    return _pcall(
        body, name="gdn_bwd", grid=(GDN_K_HEADS, n),
        in_specs=[qk, qk, v2, colv, colv, rowv, st, am, v2, v2],
        out_specs=[qk, qk, v2, colv, colv], out_shape=[shq, shq, shv, shc, shc],
        scratch_shapes=[pltpu.VMEM((2, D, D), F32)],
        compiler_params=_params(("parallel", "arbitrary")),
    )(q, k, v, beta, gcol, grow, states, amat, vnew, do)


def _outgate_fwd(o, proj, gain):
    T = o.shape[0]
    tm, tc = _pick(T, CONV_ROWS), CONV_COLS
    z0 = GDN_CONV_W // tc

    def body(o_ref, z_ref, g_ref, y_ref):
        z = z_ref[...]
        sz = z * _sigmoid(z)
        parts = []
        for hh in range(tc // GDN_HEAD_DIM):
            oh = o_ref[:, hh * GDN_HEAD_DIM:(hh + 1) * GDN_HEAD_DIM]
            r = lax.rsqrt(jnp.mean(oh * oh, axis=-1, keepdims=True) + EPS)
            parts.append(oh * r * g_ref[...])
        y_ref[...] = (jnp.concatenate(parts, axis=1) * sz).astype(BF16)

    blk = pl.BlockSpec((tm, tc), lambda i, j: (i, j))
    return _pcall(body, name="gdn_outgate", grid=(T // tm, GDN_VW // tc),
                  in_specs=[blk, pl.BlockSpec((tm, tc), lambda i, j: (i, j + z0)), pl.BlockSpec((1, GDN_HEAD_DIM), lambda i, j: (0, 0))],
                  out_specs=blk, out_shape=jax.ShapeDtypeStruct((T, GDN_VW), BF16),
                  compiler_params=_params(("parallel", "parallel")))(o, proj, gain)


def _outgate_bwd(dy, o, proj, gain):
    T = o.shape[0]
    tm, tc = _pick(T, CONV_ROWS), CONV_COLS
    z0 = GDN_CONV_W // tc
    nh = tc // GDN_HEAD_DIM

    def body(dy_ref, o_ref, z_ref, g_ref, do_ref, dz_ref, dg_ref):
        z = z_ref[...]
        sg = _sigmoid(z)
        sz = z * sg
        dy = dy_ref[...]
        dgain = jnp.zeros((1, GDN_HEAD_DIM), F32)
        dos, ys = [], []
        for hh in range(nh):
            sl = slice(hh * GDN_HEAD_DIM, (hh + 1) * GDN_HEAD_DIM)
            oh = o_ref[:, sl]
            r = lax.rsqrt(jnp.mean(oh * oh, axis=-1, keepdims=True) + EPS)
            xh = oh * r
            dn = dy[:, sl] * sz[:, sl]
            dgain = dgain + jnp.sum(dn * xh, axis=0, keepdims=True)
            dxh = dn * g_ref[...]
            dos.append(r * (dxh - xh * jnp.mean(dxh * xh, axis=-1, keepdims=True)))
            ys.append(xh * g_ref[...])
        do_ref[...] = jnp.concatenate(dos, axis=1)
        dz_ref[...] = (dy * jnp.concatenate(ys, axis=1) * sg * (1.0 + z * (1.0 - sg))).astype(BF16)
        first = (pl.program_id(0) == 0) & (pl.program_id(1) == 0)

        @pl.when(first)
        def _():
            dg_ref[...] = dgain

        @pl.when(jnp.logical_not(first))
        def _():
            dg_ref[...] += dgain

    blk = pl.BlockSpec((tm, tc), lambda i, j: (i, j))
    vec = pl.BlockSpec((1, GDN_HEAD_DIM), lambda i, j: (0, 0))
    return _pcall(body, name="gdn_doutgate", grid=(T // tm, GDN_VW // tc),
                  in_specs=[blk, blk, pl.BlockSpec((tm, tc), lambda i, j: (i, j + z0)), vec],
                  out_specs=[blk, blk, vec],
                  out_shape=[jax.ShapeDtypeStruct((T, GDN_VW), F32), jax.ShapeDtypeStruct((T, GDN_VW), BF16),
                             jax.ShapeDtypeStruct((1, GDN_HEAD_DIM), F32)],
                  compiler_params=_params(("arbitrary", "arbitrary")))(dy, o, proj, gain)


def _pad_lanes(vec):
    return jnp.pad(vec.reshape(1, -1), ((0, 0), (0, LANE - vec.shape[-1])))


def _head_cols(a):
    return a[:, :GDN_V_HEADS].T[:, :, None]


def _gdn_pad_in(w_in):
    c = GDN_CONV_W + GDN_VW
    z = jnp.zeros(w_in.shape[:-1] + (LANE - GDN_V_HEADS,), w_in.dtype)
    return jnp.concatenate([w_in[..., :c + GDN_V_HEADS], z, w_in[..., c + GDN_V_HEADS:], z], axis=-1)


def _gdn_unpad_in(dw):
    c = GDN_CONV_W + GDN_VW
    return jnp.concatenate([dw[..., :c + GDN_V_HEADS], dw[..., c + LANE:c + LANE + GDN_V_HEADS]], axis=-1)


def _gdn_mixer_fwd(h, g, w_in_pad, conv_w, a_log, dt_bias, out_gain, w_out):
    T = h.shape[0]
    hn = _rms_fwd(h, g, "gdn_norm")
    proj = _mm(hn, w_in_pad, "nn", name="gdn_in")
    qk = _conv_fwd(proj, conv_w, 0, 2 * GDN_KW, True, "gdn_conv_qk")
    vv = _conv_fwd(proj, conv_w, 2 * GDN_KW, GDN_VW, False, "gdn_conv_v")
    alog, dtb = _pad_lanes(a_log), _pad_lanes(dt_bias)
    beta, gl, gc = _gates_fwd(proj, alog, dtb)
    bcol, gcol = _head_cols(beta), _head_cols(gc)
    grow = gcol.reshape(GDN_V_HEADS, T // GDN_CHUNK, 1, GDN_CHUNK)
    qn, kn = qk[:, :GDN_KW], qk[:, GDN_KW:]
    o, states, amat, vnew = _gdn_fwd(qn, kn, vv, bcol, gcol, grow)
    gain = out_gain.reshape(1, GDN_HEAD_DIM)
    y = _outgate_fwd(o, proj, gain)
    h2 = _mm(y, w_out, "nn", res=h, name="gdn_out")
    return h2, (h, hn, proj, qn, kn, vv, beta, gl, bcol, gcol, grow, o, states, amat, vnew, y, alog, dtb, gain)


def _gdn_mixer_bwd(dh2, saved, g, w_in_pad, conv_w, w_out):
    h, hn, proj, qn, kn, vv, beta, gl, bcol, gcol, grow, o, states, amat, vnew, y, alog, dtb, gain = saved
    T = h.shape[0]
    dy = _mm(dh2, w_out, "nt", name="gdn_dy")
    dw_out = _mm(y, dh2, "tn", name="gdn_dwout")
    do, dz, dgain = _outgate_bwd(dy, o, proj, gain)
    dq, dk, dv, dbcol, dgccol = _gdn_bwd(qn, kn, vv, bcol, gcol, grow, states, amat, vnew, do)
    dqk = jnp.concatenate([dq, dk], axis=1)
    dy_qk, dcw_qk = _conv_bwd_pre(proj, conv_w, dqk, 0, 2 * GDN_KW, True, "gdn_dconv_qk")
    dy_v, dcw_v = _conv_bwd_pre(proj, conv_w, dv, 2 * GDN_KW, GDN_VW, False, "gdn_dconv_v")
    dx_qk = _conv_bwd_in(dy_qk, conv_w[:, :2 * GDN_KW], "gdn_dconvin_qk")
    dx_v = _conv_bwd_in(dy_v, conv_w[:, 2 * GDN_KW:], "gdn_dconvin_v")
    lanes = lambda c: jnp.pad(c[:, :, 0].T, ((0, 0), (0, LANE - GDN_V_HEADS)))
    dbl, da, dalog, ddt = _gates_bwd(proj, alog, dtb, beta, gl, lanes(dbcol), lanes(dgccol))
    dproj = jnp.concatenate([dx_qk, dx_v, dz, dbl, da], axis=1)
    dw_in_pad = _mm(hn, dproj, "tn", name="gdn_dwin")
    dhn = _mm(dproj, w_in_pad, "nt", name="gdn_dhn")
    dh, dg = _rms_bwd(dhn, h, g, dh2, "gdn_dnorm")
    dconv = jnp.concatenate([dcw_qk, dcw_v], axis=1)
    return (dh, dg, _gdn_unpad_in(dw_in_pad), dconv, dalog[0, :GDN_V_HEADS], ddt[0, :GDN_V_HEADS],
            dgain.reshape(GDN_HEAD_DIM), dw_out)


def _local_step(x, p, target, w):
    h = x
    tape = []
    gdn_in_pad = _gdn_pad_in(w["gdn_w_in"][0])
    for i in range(2):
        h, s1 = _ffn_fwd(h, w["ffn_norm"][i, 0], w["ffn_w_gate"][i, 0], w["ffn_w_up"][i, 0], w["ffn_w_down"][i, 0], f"ffn{i}a")
        if i == 0:
            h, s2 = _att_fwd(h, w["mix_norm"][0], w["att_w_in"][0], w["att_w_out"][0], w["att_q_norm"][0],
                             w["att_k_norm"][0], w["att_sinks"][0])
        else:
            h, s2 = _gdn_mixer_fwd(h, w["mix_norm"][1], gdn_in_pad, w["gdn_conv_w"][0], w["gdn_a_log"][0],
                                   w["gdn_dt_bias"][0], w["gdn_out_norm"][0], w["gdn_w_out"][0])
        h, s3 = _ffn_fwd(h, w["ffn_norm"][i, 1], w["ffn_w_gate"][i, 1], w["ffn_w_up"][i, 1], w["ffn_w_down"][i, 1], f"ffn{i}b")
        h, s4 = _ple_fwd(h, p[i], w["ple_norm"][i], w["ple_w_gate"][i], w["ple_w_proj"][i], f"ple{i}")
        tape.append((s1, s2, s3, s4))

    loss, dh = _loss_head(h, target)

    g = {}
    ffn_norm = [[None, None], [None, None]]
    ffn_g = [[None, None], [None, None]]
    ffn_u = [[None, None], [None, None]]
    ffn_d = [[None, None], [None, None]]
    mix_norm, ple_norm, ple_g, ple_p = [None, None], [None, None], [None, None], [None, None]
    for i in (1, 0):
        s1, s2, s3, s4 = tape[i]
        dh, ple_norm[i], ple_g[i], ple_p[i] = _ple_bwd(dh, s4, p[i], w["ple_norm"][i], w["ple_w_gate"][i], f"ple{i}")
        dh, ffn_norm[i][1], ffn_g[i][1], ffn_u[i][1], ffn_d[i][1] = _ffn_bwd(
            dh, s3, w["ffn_norm"][i, 1], w["ffn_w_gate"][i, 1], w["ffn_w_up"][i, 1], w["ffn_w_down"][i, 1], f"ffn{i}b")
        if i == 0:
            dh, mix_norm[0], dwin, dwout, dqg, dkg, dsink = _att_bwd(dh, s2, w["mix_norm"][0], w["att_w_in"][0], w["att_w_out"][0])
            g["att_w_in"], g["att_w_out"] = dwin[None], dwout[None]
            g["att_q_norm"], g["att_k_norm"], g["att_sinks"] = dqg[None], dkg[None], dsink[None]
        else:
            dh, mix_norm[1], dwin, dconv, dalog, ddt, dgain, dwout = _gdn_mixer_bwd(
                dh, s2, w["mix_norm"][1], gdn_in_pad, w["gdn_conv_w"][0], w["gdn_w_out"][0])
            g["gdn_w_in"], g["gdn_conv_w"], g["gdn_w_out"] = dwin[None], dconv[None], dwout[None]
            g["gdn_a_log"], g["gdn_dt_bias"], g["gdn_out_norm"] = dalog[None], ddt[None], dgain[None]
        dh, ffn_norm[i][0], ffn_g[i][0], ffn_u[i][0], ffn_d[i][0] = _ffn_bwd(
            dh, s1, w["ffn_norm"][i, 0], w["ffn_w_gate"][i, 0], w["ffn_w_up"][i, 0], w["ffn_w_down"][i, 0], f"ffn{i}a")
    st2 = lambda rows: jnp.stack([jnp.stack(r) for r in rows])
    g["ffn_norm"], g["ffn_w_gate"], g["ffn_w_up"], g["ffn_w_down"] = st2(ffn_norm), st2(ffn_g), st2(ffn_u), st2(ffn_d)
    g["mix_norm"], g["ple_norm"] = jnp.stack(mix_norm), jnp.stack(ple_norm)
    g["ple_w_gate"], g["ple_w_proj"] = jnp.stack(ple_g), jnp.stack(ple_p)
    return loss, dh, g


MESH = pl.DeviceIdType.MESH


def _me_and_peers():
    x, y, c = lax.axis_index("x"), lax.axis_index("y"), lax.axis_index("c")
    flip = lambda v, bit: 1 - v if bit else v
    peers = []
    for k in range(1, N_DEV):
        px, py, pc = flip(x, k & 4), flip(y, k & 2), flip(c, k & 1)
        peers.append(((px, py, pc), 4 * px + 2 * py + pc))
    return 4 * x + 2 * y + c, peers


def _exchange(arrays, scatter, name):
    n = len(arrays)

    def body(*refs):
        ins, outs = refs[:n], refs[n:2 * n]
        send_sems, recv_sems, local_sems = refs[2 * n:]
        me, peers = _me_and_peers()
        local = []
        for a in range(n):
            src = ins[a].at[me] if scatter else ins[a]
            cp = pltpu.make_async_copy(src, outs[a].at[me], local_sems.at[a])
            cp.start()
            local.append(cp)
        remote = []
        for a in range(n):
            for k, (peer, peer_idx) in enumerate(peers):
                src = ins[a].at[peer_idx] if scatter else ins[a]
                cp = pltpu.make_async_remote_copy(
                    src_ref=src, dst_ref=outs[a].at[me], send_sem=send_sems.at[a, k], recv_sem=recv_sems.at[a, k],
                    device_id=peer, device_id_type=MESH)
                cp.start()
                remote.append(cp)
        for cp in remote:
            cp.wait()
        for cp in local:
            cp.wait()

    hbm = pl.BlockSpec(memory_space=pl.ANY)
    out_shape = [jax.ShapeDtypeStruct(a.shape if scatter else (N_DEV,) + a.shape, a.dtype) for a in arrays]
    return _pcall(
        body, name=name, in_specs=[hbm] * n, out_specs=[hbm] * n, out_shape=out_shape,
        scratch_shapes=[pltpu.SemaphoreType.DMA((n, N_DEV - 1)), pltpu.SemaphoreType.DMA((n, N_DEV - 1)),
                        pltpu.SemaphoreType.DMA((n,))],
        compiler_params=pltpu.CompilerParams(has_side_effects=True),
    )(*arrays)


def _adamw(parts, w, m, v, name):
    R, Wd = w.shape
    tr = _pick(R, 256) if R % 256 == 0 else R
    c1 = 1.0 / (1.0 - ADAM_B1 ** ADAM_STEP)
    c2 = 1.0 / (1.0 - ADAM_B2 ** ADAM_STEP)

    def body(p_ref, w_ref, m_ref, v_ref, g_ref, d_ref, nm_ref, nv_ref):
        g = p_ref[0]
        for dev in range(1, N_DEV):
            g = g + p_ref[dev]
        mn = ADAM_B1 * m_ref[...] + (1.0 - ADAM_B1) * g
        vn = ADAM_B2 * v_ref[...] + (1.0 - ADAM_B2) * (g * g)
        g_ref[...] = g
        nm_ref[...] = mn
        nv_ref[...] = vn
        d_ref[...] = -ADAM_LR * ((mn * c1) / (jnp.sqrt(vn * c2) + ADAM_EPS) + ADAM_WD * w_ref[...])

    row = pl.BlockSpec((tr, Wd), lambda i: (i, 0))
    sh = jax.ShapeDtypeStruct((R, Wd), F32)
    return _pcall(body, name=name, grid=(R // tr,),
                  in_specs=[pl.BlockSpec((N_DEV, tr, Wd), lambda i: (0, i, 0)), row, row, row],
                  out_specs=[row, row, row, row], out_shape=[sh, sh, sh, sh],
                  compiler_params=_params(("parallel",)))(parts, w, m, v)


def _pack(pieces, row_align):
    rows, offs, r = [], [], 0
    for a in pieces:
        flat = a.reshape(-1)
        nr = -(-flat.shape[0] // PACK_W)
        flat = jnp.pad(flat, (0, nr * PACK_W - flat.shape[0]))
        rows.append(flat.reshape(nr, PACK_W))
        offs.append(r)
        r += nr
    pad = (-r) % row_align
    if pad:
        rows.append(jnp.zeros((pad, PACK_W), pieces[0].dtype))
    return jnp.concatenate(rows, axis=0), offs


def _unpack(flat, offs, shapes):
    out = []
    for off, shp in zip(offs, shapes):
        size = math.prod(shp)
        nr = -(-size // PACK_W)
        out.append(flat[..., off:off + nr, :].reshape(flat.shape[:-2] + (nr * PACK_W,))[..., :size].reshape(flat.shape[:-2] + tuple(shp)))
    return out


def _to_full(gathered, axis):
    z = jnp.moveaxis(gathered, 0, axis)
    shp = list(z.shape)
    return z.reshape(shp[:axis] + [shp[axis] * shp[axis + 1]] + shp[axis + 2:])


def _to_shards(full, axis):
    shp = list(full.shape)
    z = full.reshape(shp[:axis] + [N_DEV, shp[axis] // N_DEV] + shp[axis + 1:])
    return jnp.moveaxis(z, axis, 0)


def kernel(x, p, ffn_norm, ffn_w_gate, ffn_w_up, ffn_w_down, mix_norm, att_w_in, att_q_norm, att_k_norm, att_sinks, att_w_out, gdn_w_in, gdn_conv_w, gdn_a_log, gdn_dt_bias, gdn_out_norm, gdn_w_out, ple_norm, ple_w_gate, ple_w_proj, loss_target, m_ffn_norm, m_ffn_w_gate, m_ffn_w_up, m_ffn_w_down, m_mix_norm, m_att_w_in, m_att_q_norm, m_att_k_norm, m_att_sinks, m_att_w_out, m_gdn_w_in, m_gdn_conv_w, m_gdn_a_log, m_gdn_dt_bias, m_gdn_out_norm, m_gdn_w_out, m_ple_norm, m_ple_w_gate, m_ple_w_proj, v_ffn_norm, v_ffn_w_gate, v_ffn_w_up, v_ffn_w_down, v_mix_norm, v_att_w_in, v_att_q_norm, v_att_k_norm, v_att_sinks, v_att_w_out, v_gdn_w_in, v_gdn_conv_w, v_gdn_a_log, v_gdn_dt_bias, v_gdn_out_norm, v_gdn_w_out, v_ple_norm, v_ple_w_gate, v_ple_w_proj):
    args = dict(locals())
    wts = {n: args[n] for n in WEIGHTS}
    mom = {n: args["m_" + n] for n in WEIGHTS}
    var = {n: args["v_" + n] for n in WEIGHTS}
    sh_names = [n for n, _ in SHARDED]
    sh_axis = dict(SHARDED)
    sh_shapes = [wts[n].shape for n in sh_names]
    rep_shapes = [wts[n].shape for n in REPLICATED]

    w_flat, offs = _pack([wts[n] for n in sh_names], PACK_ROW_ALIGN)
    m_flat, _ = _pack([mom[n] for n in sh_names], PACK_ROW_ALIGN)
    v_flat, _ = _pack([var[n] for n in sh_names], PACK_ROW_ALIGN)
    wr_flat, roffs = _pack([wts[n] for n in REPLICATED], 8)
    mr_flat, _ = _pack([mom[n] for n in REPLICATED], 8)
    vr_flat, _ = _pack([var[n] for n in REPLICATED], 8)
    n_small = offs[SMALL_SHARDED]
    small_rows = -(-n_small // 8) * 8

    w_bf = w_flat.astype(BF16)
    g_bf, g_small = _exchange([w_bf, w_flat[:small_rows]], False, "gather_weights")
    full = {}
    for n, piece in zip(sh_names[:SMALL_SHARDED], _unpack(g_small, offs[:SMALL_SHARDED], sh_shapes[:SMALL_SHARDED])):
        full[n] = _to_full(piece, sh_axis[n])
    for n, piece in zip(sh_names[SMALL_SHARDED:], _unpack(g_bf, offs[SMALL_SHARDED:], sh_shapes[SMALL_SHARDED:])):
        full[n] = _to_full(piece, sh_axis[n])
    for n in REPLICATED:
        full[n] = wts[n]

    loss, grad_x, grads = _local_step(x[0], p[:, 0], loss_target[0], full)
    loss = lax.psum(loss, ("x", "y", "c"))

    pieces = [_to_shards(grads[n], sh_axis[n]) for n in sh_names]
    send = jnp.stack([_pack([pc[d] for pc in pieces], PACK_ROW_ALIGN)[0] for d in range(N_DEV)])
    rep_part, _ = _pack([grads[n] for n in REPLICATED], 8)
    rep_send = jnp.broadcast_to(rep_part[None], (N_DEV,) + rep_part.shape)
    recv, rep_recv = _exchange([send, rep_send], True, "exchange_grads")

    g_flat, d_flat, nm_flat, nv_flat = _adamw(recv, w_flat, m_flat, v_flat, "adamw_sharded")
    gr_flat, dr_flat, nmr_flat, nvr_flat = _adamw(rep_recv, wr_flat, mr_flat, vr_flat, "adamw_replicated")

    outs = {}
    for tag, fs, fr in (("grad", g_flat, gr_flat), ("delta", d_flat, dr_flat), ("new_m", nm_flat, nmr_flat), ("new_v", nv_flat, nvr_flat)):
        for n, piece in zip(sh_names, _unpack(fs, offs, sh_shapes)):
            outs[(tag, n)] = piece
        for n, piece in zip(REPLICATED, _unpack(fr, roffs, rep_shapes)):
            outs[(tag, n)] = piece
    result = [loss, grad_x[None]]
    for tag in ("grad", "delta", "new_m", "new_v"):
        result += [outs[(tag, n)] for n in WEIGHTS]
    return tuple(result)
```

```python
import math

import jax
import jax.numpy as jnp
from jax import lax
from jax.experimental import pallas as pl
from jax.experimental.pallas import tpu as pltpu

F32 = jnp.float32
BF16 = jnp.bfloat16

N_DEV = 8
D_MODEL = 1024
D_FF = 2816
PLE_DIM = 256
HEAD_DIM = 64
SB_HEADS = 8
SWA_HEADS = 8
SWA_KV_HEADS = 2
SWA_GROUP = SWA_HEADS // SWA_KV_HEADS
WINDOW = 128
Q_BLOCK = 128
GDN_K_HEADS = 8
GDN_V_HEADS = 16
GDN_HEAD_DIM = 128
GDN_CONV = 4
GDN_CHUNK = 64
EPS = 1e-6
SB_W = SB_HEADS * HEAD_DIM
SWA_QW = SWA_HEADS * HEAD_DIM
SWA_KVW = SWA_KV_HEADS * HEAD_DIM
ATT_IN = 3 * SB_W + SWA_QW + 2 * SWA_KVW
GDN_KW = GDN_K_HEADS * GDN_HEAD_DIM
GDN_VW = GDN_V_HEADS * GDN_HEAD_DIM
GDN_CONV_W = 2 * GDN_KW + GDN_VW
GDN_IN = GDN_CONV_W + GDN_VW + 2 * GDN_V_HEADS
GDN_IN_PAD = GDN_CONV_W + GDN_VW + 2 * 128

ADAM_LR = 0.001
ADAM_B1 = 0.9
ADAM_B2 = 0.999
ADAM_EPS = 1e-08
ADAM_WD = 0.01
ADAM_STEP = 10

LANE = 128
VMEM_LIMIT = 56 * 1024 * 1024
PACK_W = 1024
PACK_ROW_ALIGN = 1024

NN = ((1,), (0,))
NT = ((1,), (1,))
TN = ((0,), (0,))

SHARDED = (
    ("ffn_norm", 2), ("gdn_conv_w", 2),
    ("ffn_w_gate", 3), ("ffn_w_up", 3), ("ffn_w_down", 2), ("att_w_in", 2), ("att_w_out", 1),
    ("gdn_w_in", 2), ("gdn_w_out", 1), ("ple_w_gate", 1), ("ple_w_proj", 2),
)
SMALL_SHARDED = 2
REPLICATED = ("mix_norm", "att_q_norm", "att_k_norm", "att_sinks", "gdn_a_log", "gdn_dt_bias",
              "gdn_out_norm", "ple_norm")
WEIGHTS = ("ffn_norm", "ffn_w_gate", "ffn_w_up", "ffn_w_down", "mix_norm", "att_w_in", "att_q_norm",
           "att_k_norm", "att_sinks", "att_w_out", "gdn_w_in", "gdn_conv_w", "gdn_a_log", "gdn_dt_bias",
           "gdn_out_norm", "gdn_w_out", "ple_norm", "ple_w_gate", "ple_w_proj")


def _pcall(body, **kw):
    return pl.pallas_call(body, **kw)


def _params(sem=None):
    if sem is None:
        return pltpu.CompilerParams(vmem_limit_bytes=VMEM_LIMIT)
    return pltpu.CompilerParams(dimension_semantics=sem, vmem_limit_bytes=VMEM_LIMIT)


def _dot(a, b, dims=NN):
    return lax.dot_general(a, b, (dims, ((), ())), preferred_element_type=F32)


def _bdot(a, b, dims=NN):
    return _dot(a.astype(BF16), b.astype(BF16), dims)


def _split(a):
    hi = a.astype(BF16)
    lo = (a - hi.astype(F32)).astype(BF16)
    return hi, lo


def _dot3(a, b, dims=NN):
    ah, al = _split(a)
    bh, bl = _split(b)
    return _dot(ah, bh, dims) + (_dot(ah, bl, dims) + _dot(al, bh, dims))


def _dot2m(a, m, dims=NN):
    ah, al = _split(a)
    return _dot(ah, m, dims) + _dot(al, m, dims)


def _mdot2(m, a, dims=NN):
    ah, al = _split(a)
    return _dot(m, ah, dims) + _dot(m, al, dims)


def _sigmoid(x):
    return 1.0 / (1.0 + jnp.exp(-x))


def _softplus(x):
    return jnp.maximum(x, 0.0) + jnp.log(1.0 + jnp.exp(-jnp.abs(x)))


def _pick(n, cap):
    if n <= cap:
        return n
    for t in range(cap - cap % LANE, 0, -LANE):
        if n % t == 0:
            return t
    raise ValueError(f"no tile for {n} under {cap}")


def _iota2(shape, axis):
    return lax.broadcasted_iota(jnp.int32, shape, axis)


def _mm(a, b, mode, out_dtype=F32, res=None, alpha=1.0, a2=None, b2=None, name="mm"):
    if mode == "nn":
        (M, K), N = a.shape, b.shape[1]
    elif mode == "nt":
        (M, K), N = a.shape, b.shape[0]
    else:
        (K, M), N = a.shape, b.shape[1]
    tm, tn, tk = _pick(M, 512), _pick(N, 1408), _pick(K, 1408)
    nk = K // tk
    dims = {"nn": NN, "nt": NT, "tn": TN}[mode]
    a_spec = pl.BlockSpec((tk, tm), lambda i, j, k: (k, i)) if mode == "tn" else pl.BlockSpec((tm, tk), lambda i, j, k: (i, k))
    b_spec = pl.BlockSpec((tn, tk), lambda i, j, k: (j, k)) if mode == "nt" else pl.BlockSpec((tk, tn), lambda i, j, k: (k, j))
    o_spec = pl.BlockSpec((tm, tn), lambda i, j, k: (i, j))
    two = a2 is not None
    has_res = res is not None

    def body(*refs):
        refs = list(refs)
        a_ref, b_ref = refs[0], refs[1]
        pos = 2
        if two:
            a2_ref, b2_ref = refs[2], refs[3]
            pos = 4
        if has_res:
            res_ref = refs[pos]
            pos += 1
        o_ref, acc_ref = refs[pos], refs[pos + 1]
        k = pl.program_id(2)
        part = _bdot(a_ref[...], b_ref[...], dims)
        if two:
            part = part + _bdot(a2_ref[...], b2_ref[...], dims)

        def finish(acc):
            out = acc * alpha if alpha != 1.0 else acc
            if has_res:
                out = res_ref[...] + out
            o_ref[...] = out.astype(out_dtype)

        if nk == 1:
            finish(part)
        else:
            @pl.when(k == 0)
            def _():
                acc_ref[...] = part

            @pl.when(k > 0)
            def _():
                acc_ref[...] += part

            @pl.when(k == nk - 1)
            def _():
                finish(acc_ref[...])

    ins = [a, b]
    specs = [a_spec, b_spec]
    if two:
        ins += [a2, b2]
        specs += [a_spec, b_spec]
    if has_res:
        ins.append(res)
        specs.append(o_spec)
    return _pcall(
        body, name=name, grid=(M // tm, N // tn, nk), in_specs=specs, out_specs=o_spec,
        out_shape=jax.ShapeDtypeStruct((M, N), out_dtype),
        scratch_shapes=[pltpu.VMEM((tm, tn) if nk > 1 else (8, LANE), F32)],
        compiler_params=_params(("parallel", "parallel", "arbitrary")),
    )(*ins)


ROW_TILE = 256


def _rms_fwd(h, g, name):
    T, D = h.shape
    tr = _pick(T, ROW_TILE)

    def body(h_ref, g_ref, n_ref):
        x = h_ref[...]
        r = lax.rsqrt(jnp.mean(x * x, axis=-1, keepdims=True) + EPS)
        n_ref[...] = (x * r * g_ref[...]).astype(BF16)

    return _pcall(
        body, name=name, grid=(T // tr,),
        in_specs=[pl.BlockSpec((tr, D), lambda i: (i, 0)), pl.BlockSpec((1, D), lambda i: (0, 0))],
        out_specs=pl.BlockSpec((tr, D), lambda i: (i, 0)),
        out_shape=jax.ShapeDtypeStruct((T, D), BF16), compiler_params=_params(("parallel",)),
    )(h, g.reshape(1, D))


def _rms_bwd(dn, h, g, dres, name):
    T, D = h.shape
    tr = _pick(T, ROW_TILE)

    def body(dn_ref, h_ref, g_ref, dres_ref, dh_ref, dg_ref):
        x = h_ref[...]
        r = lax.rsqrt(jnp.mean(x * x, axis=-1, keepdims=True) + EPS)
        xh = x * r
        d = dn_ref[...].astype(F32)
        dxh = d * g_ref[...]
        dh_ref[...] = dres_ref[...] + r * (dxh - xh * jnp.mean(dxh * xh, axis=-1, keepdims=True))
        part = jnp.sum(d * xh, axis=0, keepdims=True)

        @pl.when(pl.program_id(0) == 0)
        def _():
            dg_ref[...] = part

        @pl.when(pl.program_id(0) > 0)
        def _():
            dg_ref[...] += part

    row = pl.BlockSpec((tr, D), lambda i: (i, 0))
    vec = pl.BlockSpec((1, D), lambda i: (0, 0))
    dh, dg = _pcall(
        body, name=name, grid=(T // tr,), in_specs=[row, row, vec, row], out_specs=[row, vec],
        out_shape=[jax.ShapeDtypeStruct((T, D), F32), jax.ShapeDtypeStruct((1, D), F32)],
        compiler_params=_params(("arbitrary",)),
    )(dn, h, g.reshape(1, D), dres)
    return dh, dg.reshape(D)


def _gateup(n, wg, wu, name):
    T, D = n.shape
    F = wg.shape[1]
    tm, tn = _pick(T, 512), _pick(F, 1408)

    def body(n_ref, wg_ref, wu_ref, a_ref, b_ref, hid_ref):
        x = n_ref[...]
        a = _dot(x, wg_ref[...])
        b = _dot(x, wu_ref[...])
        a_ref[...] = a.astype(BF16)
        b_ref[...] = b.astype(BF16)
        hid_ref[...] = (a * _sigmoid(a) * b).astype(BF16)

    o_spec = pl.BlockSpec((tm, tn), lambda i, j: (i, j))
    w_spec = pl.BlockSpec((D, tn), lambda i, j: (0, j))
    sh = jax.ShapeDtypeStruct((T, F), BF16)
    return _pcall(
        body, name=name, grid=(T // tm, F // tn),
        in_specs=[pl.BlockSpec((tm, D), lambda i, j: (i, 0)), w_spec, w_spec],
        out_specs=[o_spec, o_spec, o_spec], out_shape=[sh, sh, sh],
        compiler_params=_params(("parallel", "parallel")),
    )(n, wg, wu)


def _ffn_dhid(dy, wd, a, b, name):
    T, D = dy.shape
    F = wd.shape[0]
    tm, tn = _pick(T, 512), _pick(F, 1408)

    def body(dy_ref, wd_ref, a_ref, b_ref, da_ref, db_ref):
        dhid = 0.5 * _bdot(dy_ref[...], wd_ref[...], NT)
        av = a_ref[...].astype(F32)
        bv = b_ref[...].astype(F32)
        s = _sigmoid(av)
        da_ref[...] = (dhid * bv * s * (1.0 + av * (1.0 - s))).astype(BF16)
        db_ref[...] = (dhid * av * s).astype(BF16)

    o_spec = pl.BlockSpec((tm, tn), lambda i, j: (i, j))
    sh = jax.ShapeDtypeStruct((T, F), BF16)
    return _pcall(
        body, name=name, grid=(T // tm, F // tn),
        in_specs=[pl.BlockSpec((tm, D), lambda i, j: (i, 0)), pl.BlockSpec((tn, D), lambda i, j: (j, 0)), o_spec, o_spec],
        out_specs=[o_spec, o_spec], out_shape=[sh, sh],
        compiler_params=_params(("parallel", "parallel")),
    )(dy, wd, a, b)


def _ffn_fwd(h, g, wg, wu, wd, tag):
    n = _rms_fwd(h, g, f"{tag}_norm")
    a, b, hid = _gateup(n, wg, wu, f"{tag}_gateup")
    h2 = _mm(hid, wd, "nn", res=h, alpha=0.5, name=f"{tag}_down")
    return h2, (h, n, a, b, hid)


def _ffn_bwd(dh2, saved, g, wg, wu, wd, tag):
    h, n, a, b, hid = saved
    da, db = _ffn_dhid(dh2, wd, a, b, f"{tag}_dhid")
    dwd = _mm(hid, dh2, "tn", alpha=0.5, name=f"{tag}_dwd")
    dwg = _mm(n, da, "tn", name=f"{tag}_dwg")
    dwu = _mm(n, db, "tn", name=f"{tag}_dwu")
    dn = _mm(da, wg, "nt", a2=db, b2=wu, name=f"{tag}_dn")
    dh, dg = _rms_bwd(dn, h, g, dh2, f"{tag}_dnorm")
    return dh, dg, dwg, dwu, dwd


def _ple_fwd(h, p, g, w_gate, w_proj, tag):
    T, D = h.shape
    pn = _rms_fwd(h, g, f"{tag}_norm")
    tm, tn = _pick(T, 512), _pick(D, 1024)
    P = p.shape[1]

    def body(pn_ref, p_ref, wg_ref, wp_ref, h_ref, o_ref, gl_ref, pp_ref):
        gl = _dot(pn_ref[...], wg_ref[...])
        pp = _bdot(p_ref[...], wp_ref[...])
        gl_ref[...] = gl
        pp_ref[...] = pp
        o_ref[...] = h_ref[...] + _sigmoid(gl) * pp

    o_spec = pl.BlockSpec((tm, tn), lambda i, j: (i, j))
    sh = jax.ShapeDtypeStruct((T, D), F32)
    h2, gl, pp = _pcall(
        body, name=f"{tag}_fwd", grid=(T // tm, D // tn),
        in_specs=[pl.BlockSpec((tm, D), lambda i, j: (i, 0)), pl.BlockSpec((tm, P), lambda i, j: (i, 0)),
                  pl.BlockSpec((D, tn), lambda i, j: (0, j)), pl.BlockSpec((P, tn), lambda i, j: (0, j)), o_spec],
        out_specs=[o_spec, o_spec, o_spec], out_shape=[sh, sh, sh],
        compiler_params=_params(("parallel", "parallel")),
    )(pn, p, w_gate, w_proj, h)
    return h2, (h, pn, gl, pp)


def _ple_bwd(dh2, saved, p, g, w_gate, tag):
    h, pn, gl, pp = saved
    T, D = h.shape
    tr = _pick(T, ROW_TILE)

    def body(d_ref, gl_ref, pp_ref, dgl_ref, dpp_ref):
        d = d_ref[...]
        s = _sigmoid(gl_ref[...])
        dpp_ref[...] = (d * s).astype(BF16)
        dgl_ref[...] = (d * pp_ref[...] * s * (1.0 - s)).astype(BF16)

    row = pl.BlockSpec((tr, D), lambda i: (i, 0))
    sh = jax.ShapeDtypeStruct((T, D), BF16)
    dgl, dpp = _pcall(body, name=f"{tag}_dgate", grid=(T // tr,), in_specs=[row, row, row], out_specs=[row, row],
                      out_shape=[sh, sh], compiler_params=_params(("parallel",)))(dh2, gl, pp)
    dw_proj = _mm(p, dpp, "tn", name=f"{tag}_dwproj")
    dw_gate = _mm(pn, dgl, "tn", name=f"{tag}_dwgate")
    dpn = _mm(dgl, w_gate, "nt", name=f"{tag}_dpn")
    dh, dg = _rms_bwd(dpn, h, g, dh2, f"{tag}_dnorm")
    return dh, dg, dw_gate, dw_proj


def _loss_head(y, target):
    T, D = y.shape
    tr = _pick(T, ROW_TILE)

    def body(y_ref, t_ref, dy_ref, l_ref):
        e = y_ref[...] - t_ref[...]
        dy_ref[...] = e * (1.0 / D)
        part = jnp.sum(e * e, axis=0, keepdims=True)

        @pl.when(pl.program_id(0) == 0)
        def _():
            l_ref[...] = part

        @pl.when(pl.program_id(0) > 0)
        def _():
            l_ref[...] += part

    row = pl.BlockSpec((tr, D), lambda i: (i, 0))
    vec = pl.BlockSpec((1, D), lambda i: (0, 0))
    dy, l = _pcall(body, name="loss_head", grid=(T // tr,), in_specs=[row, row], out_specs=[row, vec],
                   out_shape=[jax.ShapeDtypeStruct((T, D), F32), jax.ShapeDtypeStruct((1, D), F32)],
                   compiler_params=_params(("arbitrary",)))(y, target)
    return (0.5 / D) * jnp.sum(l), dy


SB_GROUP_FWD = 8
SB_GROUP_BWD = 4


def _sb_consts():
    row = _iota2((Q_BLOCK, Q_BLOCK), 0)
    col = _iota2((Q_BLOCK, Q_BLOCK), 1)
    after = (row > col).astype(BF16)
    before = (row < col).astype(BF16)
    return col < row, after, before, col


def _sb_fwd(q, k, v):
    H, T, d = q.shape
    nblk = T // Q_BLOCK
    scale = d ** -0.5
    G = SB_GROUP_FWD

    def body(q_ref, k_ref, v_ref, o_ref, c_ref, run_ref):
        i = pl.program_id(1)
        causal, after, _, col = _sb_consts()
        qs = [q_ref[g] * scale for g in range(G)]
        o_ref[...] = jnp.zeros_like(o_ref)
        c_ref[...] = jnp.zeros_like(c_ref)
        run_ref[...] = jnp.zeros_like(run_ref)

        def pair(j, diag):
            off = pl.multiple_of(j * Q_BLOCK, Q_BLOCK)
            R = range(G)
            kj = [k_ref[g, pl.ds(off, Q_BLOCK), :] for g in R]
            vj = [v_ref[g, pl.ds(off, Q_BLOCK), :] for g in R]
            c = [run_ref[g] for g in R]
            acc = [o_ref[g] for g in R]
            cm = None if diag else [c_ref[g] for g in R]
            z = [_dot(qs[g], kj[g], NT) for g in R]
            sp = [_softplus(z[g]) for g in R]
            lk = [jnp.where(causal, -sp[g], 0.0) if diag else -sp[g] for g in R]
            btw = [_dot2m(lk[g], after) for g in R]
            e = [jnp.exp((z[g] - sp[g]) + btw[g] + c[g]) for g in R]
            w = [jnp.where(causal, e[g], 0.0) if diag else e[g] for g in R]
            pv = [_bdot(w[g], vj[g]) for g in R]
            rs = [jnp.sum(lk[g], axis=1, keepdims=True) for g in R]
            for g in R:
                o_ref[g] = acc[g] + pv[g]
                if not diag:
                    c_ref[g] = jnp.where(col == j, c[g], cm[g])
                run_ref[g] = c[g] + rs[g]

        pair(i, True)

        @pl.loop(0, i)
        def _(jj):
            pair(i - 1 - jj, False)

    blk = pl.BlockSpec((G, Q_BLOCK, d), lambda h, i: (h, i, 0))
    full = pl.BlockSpec((G, T, d), lambda h, i: (h, 0, 0))
    return _pcall(
        body, name="sb_fwd", grid=(H // G, nblk), in_specs=[blk, full, full],
        out_specs=[blk, pl.BlockSpec((G, Q_BLOCK, LANE), lambda h, i: (h, i, 0))],
        out_shape=[jax.ShapeDtypeStruct((H, T, d), F32), jax.ShapeDtypeStruct((H, T, LANE), F32)],
        scratch_shapes=[pltpu.VMEM((G, Q_BLOCK, 1), F32)],
        compiler_params=_params(("parallel", "parallel")),
    )(q, k, v)


def _sb_bwd(q, k, v, carry, do):
    H, T, d = q.shape
    nblk = T // Q_BLOCK
    scale = d ** -0.5
    G = SB_GROUP_BWD

    def body(q_ref, k_ref, v_ref, c_ref, do_ref, dq_ref, dk_ref, dv_ref, run_ref):
        i = pl.program_id(1)

        @pl.when(i == 0)
        def _():
            dk_ref[...] = jnp.zeros_like(dk_ref)
            dv_ref[...] = jnp.zeros_like(dv_ref)

        causal, after, before, col = _sb_consts()
        qs = [q_ref[g] * scale for g in range(G)]
        dov = [do_ref[g].astype(BF16) for g in range(G)]
        dq_ref[...] = jnp.zeros_like(dq_ref)
        run_ref[...] = jnp.zeros_like(run_ref)

        def pair(j, diag):
            off = pl.multiple_of(j * Q_BLOCK, Q_BLOCK)
            R = range(G)
            rows = pl.ds(off, Q_BLOCK)
            kj = [k_ref[g, rows, :] for g in R]
            vj = [v_ref[g, rows, :] for g in R]
            gsum = [run_ref[g] for g in R]
            dq0 = [dq_ref[g] for g in R]
            dk0 = [dk_ref[g, rows, :] for g in R]
            dv0 = [dv_ref[g, rows, :] for g in R]
            cm = None if diag else [c_ref[g] for g in R]
            z = [_dot(qs[g], kj[g], NT) for g in R]
            sp = [_softplus(z[g]) for g in R]
            lk = [jnp.where(causal, -sp[g], 0.0) if diag else -sp[g] for g in R]
            ls = [z[g] - sp[g] for g in R]
            logw = [ls[g] + _dot2m(lk[g], after) for g in R]
            if not diag:
                logw = [logw[g] + jnp.sum(jnp.where(col == j, cm[g], 0.0), axis=1, keepdims=True) for g in R]
            e = [jnp.exp(logw[g]) for g in R]
            w = [jnp.where(causal, e[g], 0.0) if diag else e[g] for g in R]
            gw = [_dot(dov[g], vj[g], NT) * w[g] for g in R]
            gpre = [gsum[g] + _dot2m(gw[g], before) for g in R]
            sig = [jnp.exp(ls[g]) for g in R]
            dz = [gw[g] * (1.0 - sig[g]) - sig[g] * gpre[g] for g in R]
            if diag:
                dz = [jnp.where(causal, dz[g], 0.0) for g in R]
            dzb = [dz[g].astype(BF16) for g in R]
            dq1 = [_dot(dzb[g], kj[g]) for g in R]
            dk1 = [_dot(dzb[g], qs[g], TN) for g in R]
            dv1 = [_dot(w[g].astype(BF16), dov[g], TN) for g in R]
            gs1 = [jnp.sum(gw[g], axis=1, keepdims=True) for g in R]
            for g in R:
                dq_ref[g] = dq0[g] + dq1[g]
                dk_ref[g, rows, :] = dk0[g] + dk1[g]
                dv_ref[g, rows, :] = dv0[g] + dv1[g]
                run_ref[g] = gsum[g] + gs1[g]

        @pl.loop(0, i)
        def _(j):
            pair(j, False)

        pair(i, True)
        dq_ref[...] = dq_ref[...] * scale

    blk = pl.BlockSpec((G, Q_BLOCK, d), lambda h, i: (h, i, 0))
    full = pl.BlockSpec((G, T, d), lambda h, i: (h, 0, 0), pipeline_mode=pl.Buffered(1))
    sh = jax.ShapeDtypeStruct((H, T, d), F32)
    return _pcall(
        body, name="sb_bwd", grid=(H // G, nblk),
        in_specs=[blk, full, full, pl.BlockSpec((G, Q_BLOCK, LANE), lambda h, i: (h, i, 0)), blk],
        out_specs=[blk, full, full], out_shape=[sh, sh, sh],
        scratch_shapes=[pltpu.VMEM((G, Q_BLOCK, 1), F32)],
        compiler_params=_params(("parallel", "arbitrary")),
    )(q, k, v, carry, do)


def _swa_fwd(q, k, v, qg, kg, sinks, slopes):
    Hq, T, d = q.shape
    Hkv = k.shape[0]
    G = Hq // Hkv
    W = WINDOW
    nblk = T // W
    scale = d ** -0.5

    def body(q_ref, kp_ref, kc_ref, vp_ref, vc_ref, qg_ref, kg_ref, sk_ref, sl_ref, o_ref):
        hk = pl.program_id(0)
        n = pl.program_id(1)
        kcat = jnp.concatenate([kp_ref[...], kc_ref[...]], axis=0)
        vcat = jnp.concatenate([vp_ref[...], vc_ref[...]], axis=0).astype(BF16)
        rk = lax.rsqrt(jnp.mean(kcat * kcat, axis=-1, keepdims=True) + EPS)
        kn = (kcat * rk * kg_ref[...]).astype(BF16)
        row = _iota2((W, 2 * W), 0)
        col = _iota2((W, 2 * W), 1)
        dist = row + W - col
        valid = (dist >= 0) & (dist < W) & ((n > 0) | (col >= W))
        distf = dist.astype(F32)
        for g in range(G):
            qh = q_ref[g]
            rq = lax.rsqrt(jnp.mean(qh * qh, axis=-1, keepdims=True) + EPS)
            qn = (qh * rq * qg_ref[...]).astype(BF16)
            sink = sk_ref[pl.ds(hk * G + g, 1), :][:, :1]
            slope = sl_ref[pl.ds(hk * G + g, 1), :][:, :1]
            s = _dot(qn, kn, NT) * scale - slope * distf
            s = jnp.where(valid, s, -1e30)
            m = jnp.maximum(jnp.max(s, axis=1, keepdims=True), sink)
            p = jnp.where(valid, jnp.exp(s - m), 0.0)
            den = jnp.sum(p, axis=1, keepdims=True) + jnp.exp(sink - m)
            o_ref[g] = _bdot(p / den, vcat)

    qblk = pl.BlockSpec((G, W, d), lambda h, n: (h, n, 0))
    prev = pl.BlockSpec((None, W, d), lambda h, n: (h, jnp.maximum(n - 1, 0), 0))
    cur = pl.BlockSpec((None, W, d), lambda h, n: (h, n, 0))
    gain = pl.BlockSpec((1, d), lambda h, n: (0, 0))
    perhead = pl.BlockSpec((Hq, LANE), lambda h, n: (0, 0))
    return _pcall(
        body, name="swa_fwd", grid=(Hkv, nblk),
        in_specs=[qblk, prev, cur, prev, cur, gain, gain, perhead, perhead], out_specs=qblk,
        out_shape=jax.ShapeDtypeStruct((Hq, T, d), F32), compiler_params=_params(("parallel", "parallel")),
    )(q, k, k, v, v, qg, kg, sinks, slopes)


def _swa_bwd(q, k, v, qg, kg, sinks, slopes, do):
    Hq, T, d = q.shape
    Hkv = k.shape[0]
    G = Hq // Hkv
    W = WINDOW
    nblk = T // W
    scale = d ** -0.5

    def body(q_ref, kp_ref, kc_ref, vp_ref, vc_ref, qg_ref, kg_ref, sk_ref, sl_ref, do_ref,
             dq_ref, dk_ref, dv_ref, dqg_ref, dkg_ref, dsk_ref):
        hk = pl.program_id(0)
        n = pl.program_id(1)

        @pl.when((hk == 0) & (n == 0))
        def _():
            dqg_ref[...] = jnp.zeros_like(dqg_ref)
            dkg_ref[...] = jnp.zeros_like(dkg_ref)
            dsk_ref[...] = jnp.zeros_like(dsk_ref)

        @pl.when(n == 0)
        def _():
            dk_ref[...] = jnp.zeros_like(dk_ref)
            dv_ref[...] = jnp.zeros_like(dv_ref)

        kcat = jnp.concatenate([kp_ref[...], kc_ref[...]], axis=0)
        vcat = jnp.concatenate([vp_ref[...], vc_ref[...]], axis=0).astype(BF16)
        rk = lax.rsqrt(jnp.mean(kcat * kcat, axis=-1, keepdims=True) + EPS)
        kh = kcat * rk
        kn = (kh * kg_ref[...]).astype(BF16)
        row = _iota2((W, 2 * W), 0)
        col = _iota2((W, 2 * W), 1)
        dist = row + W - col
        valid = (dist >= 0) & (dist < W) & ((n > 0) | (col >= W))
        distf = dist.astype(F32)
        rowh = _iota2((Hq, LANE), 0)
        dkn = jnp.zeros((2 * W, d), F32)
        dvc = jnp.zeros((2 * W, d), F32)
        dqg = jnp.zeros((1, d), F32)
        dsk = jnp.zeros((Hq, LANE), F32)
        for g in range(G):
            qh = q_ref[g]
            rq = lax.rsqrt(jnp.mean(qh * qh, axis=-1, keepdims=True) + EPS)
            qhh = qh * rq
            qn = (qhh * qg_ref[...]).astype(BF16)
            sink = sk_ref[pl.ds(hk * G + g, 1), :][:, :1]
            slope = sl_ref[pl.ds(hk * G + g, 1), :][:, :1]
            s = _dot(qn, kn, NT) * scale - slope * distf
            s = jnp.where(valid, s, -1e30)
            m = jnp.maximum(jnp.max(s, axis=1, keepdims=True), sink)
            p = jnp.where(valid, jnp.exp(s - m), 0.0)
            esink = jnp.exp(sink - m)
            den = jnp.sum(p, axis=1, keepdims=True) + esink
            prob = p / den
            dov = do_ref[g].astype(BF16)
            dp = _dot(dov, vcat, NT)
            dd = jnp.sum(prob * dp, axis=1, keepdims=True)
            ds = prob * (dp - dd)
            dsink = -jnp.sum((esink / den) * dd, axis=0, keepdims=True)
            dsk = dsk + jnp.where(rowh == hk * G + g, dsink, 0.0)
            dsb = (ds * scale).astype(BF16)
            dqn = _dot(dsb, kn)
            dkn = dkn + _dot(dsb, qn, TN)
            dvc = dvc + _dot(prob.astype(BF16), dov, TN)
            dqh = dqn * qg_ref[...]
            dq_ref[g] = rq * (dqh - qhh * jnp.mean(dqh * qhh, axis=-1, keepdims=True))
            dqg = dqg + jnp.sum(dqn * qhh, axis=0, keepdims=True)
        dkh = dkn * kg_ref[...]
        dkraw = rk * (dkh - kh * jnp.mean(dkh * kh, axis=-1, keepdims=True))
        dqg_ref[...] += dqg
        dkg_ref[...] += jnp.sum(dkn * kh, axis=0, keepdims=True)
        dsk_ref[...] += dsk
        offp = pl.multiple_of(jnp.maximum(n - 1, 0) * W, W)
        offc = pl.multiple_of(n * W, W)
        dk_ref[pl.ds(offp, W), :] += dkraw[:W]
        dv_ref[pl.ds(offp, W), :] += dvc[:W]
        dk_ref[pl.ds(offc, W), :] += dkraw[W:]
        dv_ref[pl.ds(offc, W), :] += dvc[W:]

    qblk = pl.BlockSpec((G, W, d), lambda h, n: (h, n, 0))
    prev = pl.BlockSpec((None, W, d), lambda h, n: (h, jnp.maximum(n - 1, 0), 0))
    cur = pl.BlockSpec((None, W, d), lambda h, n: (h, n, 0))
    gain = pl.BlockSpec((1, d), lambda h, n: (0, 0))
    perhead = pl.BlockSpec((Hq, LANE), lambda h, n: (0, 0))
    full = pl.BlockSpec((None, T, d), lambda h, n: (h, 0, 0))
    kv = jax.ShapeDtypeStruct((Hkv, T, d), F32)
    gs = jax.ShapeDtypeStruct((1, d), F32)
    return _pcall(
        body, name="swa_bwd", grid=(Hkv, nblk),
        in_specs=[qblk, prev, cur, prev, cur, gain, gain, perhead, perhead, qblk],
        out_specs=[qblk, full, full, gain, gain, perhead],
        out_shape=[jax.ShapeDtypeStruct((Hq, T, d), F32), kv, kv, gs, gs, jax.ShapeDtypeStruct((Hq, LANE), F32)],
        compiler_params=_params(("arbitrary", "arbitrary")),
    )(q, k, k, v, v, qg, kg, sinks, slopes, do)


def _heads(z, n):
    T = z.shape[0]
    return z.reshape(T, n, HEAD_DIM).transpose(1, 0, 2)


def _unheads(z):
    n, T, d = z.shape
    return z.transpose(1, 0, 2).reshape(T, n * d)


def _alibi():
    s = [2.0 ** (-8.0 * (i + 1) / SWA_HEADS) for i in range(SWA_HEADS)]
    return jnp.broadcast_to(jnp.asarray(s, F32)[:, None], (SWA_HEADS, LANE))


def _att_fwd(h, g, w_in, w_out, q_gain, k_gain, sinks):
    hn = _rms_fwd(h, g, "att_norm")
    proj = _mm(hn, w_in, "nn", name="att_in")
    c = [0, SB_W, 2 * SB_W, 3 * SB_W, 3 * SB_W + SWA_QW, 3 * SB_W + SWA_QW + SWA_KVW, ATT_IN]
    sq, sk, sv = (_heads(proj[:, c[i]:c[i + 1]], SB_HEADS).astype(BF16) for i in range(3))
    bq = _heads(proj[:, c[3]:c[4]], SWA_HEADS)
    bk = _heads(proj[:, c[4]:c[5]], SWA_KV_HEADS)
    bv = _heads(proj[:, c[5]:c[6]], SWA_KV_HEADS)
    a_out, carry = _sb_fwd(sq, sk, sv)
    sk128 = jnp.broadcast_to(sinks.reshape(SWA_HEADS, 1), (SWA_HEADS, LANE))
    qg, kg = q_gain.reshape(1, HEAD_DIM), k_gain.reshape(1, HEAD_DIM)
    b_out = _swa_fwd(bq, bk, bv, qg, kg, sk128, _alibi())
    o = jnp.concatenate([_unheads(a_out), _unheads(b_out)], axis=-1).astype(BF16)
    h2 = _mm(o, w_out, "nn", res=h, name="att_out")
    return h2, (h, hn, sq, sk, sv, bq, bk, bv, carry, o, sk128, qg, kg)


def _att_bwd(dh2, saved, g, w_in, w_out):
    h, hn, sq, sk, sv, bq, bk, bv, carry, o, sk128, qg, kg = saved
    do = _mm(dh2, w_out, "nt", name="att_do")
    dw_out = _mm(o, dh2, "tn", name="att_dwout")
    da = _heads(do[:, :SB_W], SB_HEADS)
    db = _heads(do[:, SB_W:], SWA_HEADS)
    dsq, dsk, dsv = _sb_bwd(sq, sk, sv, carry, da)
    dbq, dbk, dbv, dqg, dkg, dsink = _swa_bwd(bq, bk, bv, qg, kg, sk128, _alibi(), db)
    dproj = jnp.concatenate([_unheads(z) for z in (dsq, dsk, dsv, dbq, dbk, dbv)], axis=-1).astype(BF16)
    dw_in = _mm(hn, dproj, "tn", name="att_dwin")
    dhn = _mm(dproj, w_in, "nt", name="att_dhn")
    dh, dg = _rms_bwd(dhn, h, g, dh2, "att_dnorm")
    return dh, dg, dw_in, dw_out, dqg.reshape(HEAD_DIM), dkg.reshape(HEAD_DIM), dsink[:, 0]


CONV_ROWS = 512
CONV_COLS = 512
HALO = 8


def _shifted(xcat, s, tm):
    if s == 0:
        return xcat[HALO:HALO + tm]
    return pltpu.roll(xcat, s, 0)[HALO:HALO + tm]


def _conv_pre(x_ref, halo_ref, w_ref, i, tm):
    xc = x_ref[...]
    halo = jnp.where(i > 0, halo_ref[...], 0.0)
    xcat = jnp.concatenate([halo, xc], axis=0)
    w = w_ref[...]
    y = w[GDN_CONV - 1:GDN_CONV] * xc
    for kk in range(GDN_CONV - 1):
        y = y + w[kk:kk + 1] * _shifted(xcat, GDN_CONV - 1 - kk, tm)
    return xcat, y


def _l2_heads(s, qscale_of):
    outs, rs = [], []
    for hh in range(s.shape[1] // GDN_HEAD_DIM):
        sh = s[:, hh * GDN_HEAD_DIM:(hh + 1) * GDN_HEAD_DIM]
        r = lax.rsqrt(jnp.sum(sh * sh, axis=-1, keepdims=True) + EPS)
        outs.append(sh * r)
        rs.append(r)
    return outs, rs


def _conv_specs(T, col0, tm, tc):
    cur = pl.BlockSpec((tm, tc), lambda j, i: (i, j + col0 // tc))
    halo = pl.BlockSpec((HALO, tc), lambda j, i: (jnp.maximum(i * (tm // HALO) - 1, 0), j + col0 // tc))
    wsp = pl.BlockSpec((GDN_CONV, tc), lambda j, i: (0, j + col0 // tc))
    out = pl.BlockSpec((tm, tc), lambda j, i: (i, j))
    return cur, halo, wsp, out


def _conv_fwd(proj, conv_w, col0, width, norm, name):
    T = proj.shape[0]
    tm, tc = _pick(T, CONV_ROWS), CONV_COLS
    cur, halo, wsp, out = _conv_specs(T, col0, tm, tc)
    n_q_tiles = (width // 2) // tc

    def body(x_ref, halo_ref, w_ref, o_ref):
        j, i = pl.program_id(0), pl.program_id(1)
        _, y = _conv_pre(x_ref, halo_ref, w_ref, i, tm)
        s = y * _sigmoid(y)
        if norm:
            outs, _ = _l2_heads(s, None)
            qs = jnp.where(j < n_q_tiles, GDN_HEAD_DIM ** -0.5, 1.0)
            o_ref[...] = jnp.concatenate(outs, axis=1) * qs
        else:
            o_ref[...] = s

    return _pcall(body, name=name, grid=(width // tc, T // tm), in_specs=[cur, halo, wsp], out_specs=out,
                  out_shape=jax.ShapeDtypeStruct((T, width), F32),
                  compiler_params=_params(("parallel", "parallel")))(proj, proj, conv_w)


def _conv_bwd_pre(proj, conv_w, dout, col0, width, norm, name):
    T = proj.shape[0]
    tm, tc = _pick(T, CONV_ROWS), CONV_COLS
    cur, halo, wsp, out = _conv_specs(T, col0, tm, tc)
    n_q_tiles = (width // 2) // tc

    def body(x_ref, halo_ref, w_ref, d_ref, dy_ref, dw_ref):
        j, i = pl.program_id(0), pl.program_id(1)
        xcat, y = _conv_pre(x_ref, halo_ref, w_ref, i, tm)
        sg = _sigmoid(y)
        s = y * sg
        d = d_ref[...]
        if norm:
            qs = jnp.where(j < n_q_tiles, GDN_HEAD_DIM ** -0.5, 1.0)
            d = d * qs
            outs, rs = _l2_heads(s, None)
            parts = []
            for hh, (nh, r) in enumerate(zip(outs, rs)):
                dh = d[:, hh * GDN_HEAD_DIM:(hh + 1) * GDN_HEAD_DIM]
                parts.append(r * (dh - nh * jnp.sum(dh * nh, axis=-1, keepdims=True)))
            ds = jnp.concatenate(parts, axis=1)
        else:
            ds = d
        dy = ds * sg * (1.0 + y * (1.0 - sg))
        dy_ref[...] = dy
        rows = [jnp.sum(dy * _shifted(xcat, GDN_CONV - 1 - kk, tm), axis=0, keepdims=True) for kk in range(GDN_CONV)]
        part = jnp.concatenate(rows, axis=0)

        @pl.when(i == 0)
        def _():
            dw_ref[...] = part

        @pl.when(i > 0)
        def _():
            dw_ref[...] += part

    wout = pl.BlockSpec((GDN_CONV, tc), lambda j, i: (0, j))
    return _pcall(body, name=name, grid=(width // tc, T // tm), in_specs=[cur, halo, wsp, out], out_specs=[out, wout],
                  out_shape=[jax.ShapeDtypeStruct((T, width), F32), jax.ShapeDtypeStruct((GDN_CONV, width), F32)],
                  compiler_params=_params(("parallel", "arbitrary")))(proj, proj, conv_w, dout)


def _conv_bwd_in(dy, conv_w, name):
    T, C = dy.shape
    tm, tc = _pick(T, CONV_ROWS), CONV_COLS
    nrow = T // tm

    def body(d_ref, nxt_ref, w_ref, dx_ref):
        i = pl.program_id(0)
        dc = d_ref[...]
        nxt = jnp.where(i < nrow - 1, nxt_ref[...], 0.0)
        dcat = jnp.concatenate([dc, nxt], axis=0)
        w = w_ref[...]
        dx = w[GDN_CONV - 1:GDN_CONV] * dc
        for kk in range(GDN_CONV - 1):
            s = GDN_CONV - 1 - kk
            dx = dx + w[kk:kk + 1] * pltpu.roll(dcat, tm + HALO - s, 0)[:tm]
        dx_ref[...] = dx.astype(BF16)

    cur = pl.BlockSpec((tm, tc), lambda i, j: (i, j))
    nxt = pl.BlockSpec((HALO, tc), lambda i, j: (jnp.minimum((i + 1) * (tm // HALO), T // HALO - 1), j))
    wsp = pl.BlockSpec((GDN_CONV, tc), lambda i, j: (0, j))
    return _pcall(body, name=name, grid=(nrow, C // tc), in_specs=[cur, nxt, wsp], out_specs=cur,
                  out_shape=jax.ShapeDtypeStruct((T, C), BF16),
                  compiler_params=_params(("parallel", "parallel")))(dy, dy, conv_w)


GATE_ROWS = 512


def _chunk_mask(n, lower):
    row = _iota2((n, n), 0)
    col = _iota2((n, n), 1)
    same = (row // GDN_CHUNK) == (col // GDN_CHUNK)
    tri = (row >= col) if lower else (row <= col)
    return (same & tri).astype(BF16)


def _gates_fwd(proj, a_log, dt_bias):
    T = proj.shape[0]
    tm = _pick(T, GATE_ROWS)
    c0 = (GDN_CONV_W + GDN_VW) // LANE

    def body(bl_ref, a_ref, alog_ref, dt_ref, beta_ref, g_ref, gc_ref):
        beta_ref[...] = _sigmoid(bl_ref[...])
        g = -jnp.exp(alog_ref[...]) * _softplus(a_ref[...] + dt_ref[...])
        g_ref[...] = g
        gc_ref[...] = _mdot2(_chunk_mask(tm, True), g)

    blk = lambda c: pl.BlockSpec((tm, LANE), lambda i: (i, c))
    vec = pl.BlockSpec((1, LANE), lambda i: (0, 0))
    sh = jax.ShapeDtypeStruct((T, LANE), F32)
    return _pcall(body, name="gdn_gates", grid=(T // tm,), in_specs=[blk(c0), blk(c0 + 1), vec, vec],
                  out_specs=[blk(0), blk(0), blk(0)], out_shape=[sh, sh, sh],
                  compiler_params=_params(("parallel",)))(proj, proj, a_log, dt_bias)


def _gates_bwd(proj, a_log, dt_bias, beta, g, dbeta, dgc):
    T = proj.shape[0]
    tm = _pick(T, GATE_ROWS)
    c0 = (GDN_CONV_W + GDN_VW) // LANE

    def body(a_ref, alog_ref, dt_ref, beta_ref, g_ref, dbeta_ref, dgc_ref, dbl_ref, da_ref, dalog_ref, ddt_ref):
        dg = _mdot2(_chunk_mask(tm, False), dgc_ref[...])
        b = beta_ref[...]
        dbl_ref[...] = (dbeta_ref[...] * b * (1.0 - b)).astype(BF16)
        da = dg * (-jnp.exp(alog_ref[...])) * _sigmoid(a_ref[...] + dt_ref[...])
        da_ref[...] = da.astype(BF16)
        p1 = jnp.sum(dg * g_ref[...], axis=0, keepdims=True)
        p2 = jnp.sum(da, axis=0, keepdims=True)

        @pl.when(pl.program_id(0) == 0)
        def _():
            dalog_ref[...] = p1
            ddt_ref[...] = p2

        @pl.when(pl.program_id(0) > 0)
        def _():
            dalog_ref[...] += p1
            ddt_ref[...] += p2

    blk = lambda c: pl.BlockSpec((tm, LANE), lambda i: (i, c))
    vec = pl.BlockSpec((1, LANE), lambda i: (0, 0))
    shb = jax.ShapeDtypeStruct((T, LANE), BF16)
    shv = jax.ShapeDtypeStruct((1, LANE), F32)
    return _pcall(body, name="gdn_dgates", grid=(T // tm,),
                  in_specs=[blk(c0 + 1), vec, vec, blk(0), blk(0), blk(0), blk(0)],
                  out_specs=[blk(0), blk(0), vec, vec], out_shape=[shb, shb, shv, shv],
                  compiler_params=_params(("arbitrary",)))(proj, a_log, dt_bias, beta, g, dbeta, dgc)


def _inv_unit_lower(Ls):
    C = Ls[0].shape[0]
    row = _iota2((C, C), 0)
    col = _iota2((C, C), 1)
    blk16 = (row // 16) == (col // 16)
    blk32 = (row // 32) == (col // 32)
    eye = (row == col).astype(F32)
    xs = [-jnp.where(blk16, L, 0.0) for L in Ls]
    inv = [eye + x for x in xs]
    for _ in range(3):
        xs = [_dot3(x, x) for x in xs]
        inv = [a + _dot3(a, x) for a, x in zip(inv, xs)]
    for mask in (blk32 & ~blk16, ~blk32):
        t = [_dot3(a, jnp.where(mask, L, 0.0)) for a, L in zip(inv, Ls)]
        inv = [a - _dot3(ti, a) for a, ti in zip(inv, t)]
    return inv


GDN_GROUP = 4
GDN_PREP_CHUNKS = 4


def _gdn_specs(T):
    C, D, E = GDN_CHUNK, GDN_HEAD_DIM, GDN_GROUP
    n = T // C
    qk = pl.BlockSpec((C, (E // 2) * D), lambda h, i: (i, h))
    vE = pl.BlockSpec((C, E * D), lambda h, i: (i, h))
    colv = pl.BlockSpec((E, C, 1), lambda h, i: (h, i, 0))
    rowv = pl.BlockSpec((E, None, 1, C), lambda h, i: (h, i, 0, 0))
    st = pl.BlockSpec((E, None, D, D), lambda h, i: (h, i, 0, 0))
    am = pl.BlockSpec((E, None, C, C), lambda h, i: (h, i, 0, 0))
    return n, qk, vE, colv, rowv, st, am


def _gdn_decay(gcol, grow):
    C = GDN_CHUNK
    row = _iota2((C, C), 0)
    col = _iota2((C, C), 1)
    incl = row >= col
    dm = jnp.where(incl, jnp.exp(jnp.where(incl, gcol - grow, 0.0)), 0.0)
    glast = grow[:, C - 1:C]
    return dm, jnp.exp(gcol), jnp.exp(glast), jnp.exp(glast - gcol), row > col, incl


def _gdn_prep(k, beta, gcol, grow):
    T = k.shape[0]
    C, D, B = GDN_CHUNK, GDN_HEAD_DIM, GDN_PREP_CHUNKS
    n = T // C

    def body(k_ref, b_ref, gc_ref, gr_ref, a_ref):
        idx = [(e, cb) for e in range(2) for cb in range(B)]
        kc = {cb: k_ref[cb * C:(cb + 1) * C, :] for cb in range(B)}
        lm = []
        for e, cb in idx:
            beta = b_ref[e, cb * C:(cb + 1) * C, :]
            dm, _, _, _, strict, _ = _gdn_decay(gc_ref[e, cb * C:(cb + 1) * C, :], gr_ref[e, cb])
            lm.append(jnp.where(strict, _bdot(kc[cb] * beta, kc[cb], NT) * dm, 0.0))
        inv = _inv_unit_lower(lm)
        for (e, cb), a in zip(idx, inv):
            a_ref[e, cb] = a

    return _pcall(
        body, name="gdn_prep", grid=(GDN_K_HEADS, n // B),
        in_specs=[pl.BlockSpec((B * C, D), lambda h, i: (i, h)), pl.BlockSpec((2, B * C, 1), lambda h, i: (h, i, 0)),
                  pl.BlockSpec((2, B * C, 1), lambda h, i: (h, i, 0)), pl.BlockSpec((2, B, 1, C), lambda h, i: (h, i, 0, 0))],
        out_specs=pl.BlockSpec((2, B, C, C), lambda h, i: (h, i, 0, 0)),
        out_shape=jax.ShapeDtypeStruct((GDN_V_HEADS, n, C, C), F32),
        compiler_params=_params(("parallel", "parallel")),
    )(k, beta, gcol, grow)


def _gdn_fwd(q, k, v, beta, gcol, grow, amat):
    T = q.shape[0]
    C, D, E = GDN_CHUNK, GDN_HEAD_DIM, GDN_GROUP
    n, qk, vE, colv, rowv, st, am = _gdn_specs(T)
    R = range(E)

    def body(q_ref, k_ref, v_ref, b_ref, gc_ref, gr_ref, a_ref, o_ref, s_ref, vn_ref, state):
        @pl.when(pl.program_id(1) == 0)
        def _():
            state[...] = jnp.zeros_like(state)

        qv = [q_ref[:, (e // 2) * D:(e // 2 + 1) * D] for e in R]
        kv = [k_ref[:, (e // 2) * D:(e // 2 + 1) * D] for e in R]
        vv = [v_ref[:, e * D:(e + 1) * D] for e in R]
        beta = [b_ref[e] for e in R]
        a = [a_ref[e] for e in R]
        s = [state[e] for e in R]
        dec = [_gdn_decay(gc_ref[e], gr_ref[e]) for e in R]
        pm = [_bdot(qv[e], kv[e], NT) * dec[e][0] for e in R]
        r = [beta[e] * (vv[e] - _bdot(kv[e] * dec[e][1], s[e])) for e in R]
        vn = [_dot3(a[e], r[e]) for e in R]
        o = [_bdot(qv[e] * dec[e][1], s[e]) + _bdot(pm[e], vn[e]) for e in R]
        s2 = [dec[e][2] * s[e] + _bdot(kv[e] * dec[e][3], vn[e], TN) for e in R]
        for e in R:
            s_ref[e] = s[e]
            vn_ref[:, e * D:(e + 1) * D] = vn[e]
            o_ref[:, e * D:(e + 1) * D] = o[e]
            state[e] = s2[e]

    shv = jax.ShapeDtypeStruct((T, GDN_V_HEADS * D), F32)
    return _pcall(
        body, name="gdn_fwd", grid=(GDN_V_HEADS // E, n), in_specs=[qk, qk, vE, colv, colv, rowv, am],
        out_specs=[vE, st, vE],
        out_shape=[shv, jax.ShapeDtypeStruct((GDN_V_HEADS, n, D, D), F32), shv],
        scratch_shapes=[pltpu.VMEM((E, D, D), F32)],
        compiler_params=_params(("parallel", "arbitrary")),
    )(q, k, v, beta, gcol, grow, amat)


def _gdn_bwd(q, k, v, beta, gcol, grow, states, amat, vnew, do):
    T = q.shape[0]
    C, D, E = GDN_CHUNK, GDN_HEAD_DIM, GDN_GROUP
    n, qk, vE, colv, rowv, st, am = _gdn_specs(T)
    rev = lambda spec: pl.BlockSpec(spec.block_shape, (lambda f: (lambda h, i: f(h, n - 1 - i)))(spec.index_map))
    qk, vE, colv, rowv, st, am = (rev(s) for s in (qk, vE, colv, rowv, st, am))
    R = range(E)

    def body(q_ref, k_ref, v_ref, b_ref, gc_ref, gr_ref, s_ref, a_ref, vn_ref, do_ref,
             dq_ref, dk_ref, dv_ref, db_ref, dgc_ref, dstate):
        @pl.when(pl.program_id(1) == 0)
        def _():
            dstate[...] = jnp.zeros_like(dstate)

        M = lambda f: [f(e) for e in R]
        rsum = lambda x: jnp.sum(x, axis=1, keepdims=True)
        qv = M(lambda e: q_ref[:, (e // 2) * D:(e // 2 + 1) * D])
        kv = M(lambda e: k_ref[:, (e // 2) * D:(e // 2 + 1) * D])
        vv = M(lambda e: v_ref[:, e * D:(e + 1) * D])
        vn = M(lambda e: vn_ref[:, e * D:(e + 1) * D])
        dov = M(lambda e: do_ref[:, e * D:(e + 1) * D])
        beta = M(lambda e: b_ref[e])
        s = M(lambda e: s_ref[e])
        a = M(lambda e: a_ref[e])
        dsn = M(lambda e: dstate[e])
        dec = M(lambda e: _gdn_decay(gc_ref[e], gr_ref[e]))
        dm, gam, glast, tail = (M(lambda e: dec[e][i]) for i in range(4))
        strict, incl = dec[0][4], dec[0][5]
        kb = M(lambda e: kv[e] * beta[e])
        kd = M(lambda e: kv[e] * gam[e])
        qd = M(lambda e: qv[e] * gam[e])
        kt = M(lambda e: kv[e] * tail[e])
        lmat = M(lambda e: jnp.where(strict, _bdot(kb[e], kv[e], NT) * dm[e], 0.0))
        pmat = M(lambda e: _bdot(qv[e], kv[e], NT) * dm[e])
        xres = M(lambda e: vv[e] - _bdot(kd[e], s[e]))
        dvn = M(lambda e: _bdot(pmat[e], dov[e], TN) + _bdot(kt[e], dsn[e]))
        dqd = M(lambda e: _bdot(dov[e], s[e], NT))
        dp = M(lambda e: jnp.where(incl, _bdot(dov[e], vn[e], NT), 0.0))
        dkt = M(lambda e: _bdot(vn[e], dsn[e], NT))
        dr = M(lambda e: _dot3(a[e], dvn[e], TN))
        drb = M(lambda e: beta[e] * dr[e])
        dkd = M(lambda e: -_bdot(drb[e], s[e], NT))
        ds2 = M(lambda e: _bdot(qd[e], dov[e], TN) + glast[e] * dsn[e] - _bdot(kd[e], drb[e], TN))
        dl = M(lambda e: -jnp.where(strict, _bdot(dr[e], vn[e], NT), 0.0))
        dmm = M(lambda e: dl[e] * dm[e])
        dnn = M(lambda e: dp[e] * dm[e])
        emat = M(lambda e: dl[e] * lmat[e] + dp[e] * pmat[e])
        dkb = M(lambda e: _bdot(dmm[e], kv[e]))
        dk = M(lambda e: beta[e] * dkb[e] + _bdot(dmm[e], kb[e], TN) + _bdot(dnn[e], qv[e], TN)
               + gam[e] * dkd[e] + tail[e] * dkt[e])
        dq = M(lambda e: _bdot(dnn[e], kv[e]) + gam[e] * dqd[e])
        dbeta = M(lambda e: rsum(dr[e] * xres[e]) + rsum(dkb[e] * kv[e]))
        ones = jnp.ones((C, LANE), BF16)
        colsum = M(lambda e: _dot2m(emat[e], ones, TN)[:, :1])
        tails = M(lambda e: rsum(dkt[e] * kt[e]))
        lastrow = _iota2((C, 1), 0) == C - 1
        dlast = M(lambda e: jnp.sum(tails[e], axis=0, keepdims=True)
                  + glast[e] * jnp.sum(rsum(s[e] * dsn[e]), axis=0, keepdims=True))
        dgc = M(lambda e: rsum(emat[e]) - colsum[e] + rsum(dkd[e] * kd[e]) + rsum(dqd[e] * qd[e]) - tails[e]
                + jnp.where(lastrow, dlast[e], 0.0))
        for e in R:
            dv_ref[:, e * D:(e + 1) * D] = drb[e]
            db_ref[e] = dbeta[e]
            dgc_ref[e] = dgc[e]
            dstate[e] = ds2[e]
        for kh in range(E // 2):
            dq_ref[:, kh * D:(kh + 1) * D] = dq[2 * kh] + dq[2 * kh + 1]
            dk_ref[:, kh * D:(kh + 1) * D] = dk[2 * kh] + dk[2 * kh + 1]

    shq = jax.ShapeDtypeStruct((T, GDN_K_HEADS * D), F32)
    shv = jax.ShapeDtypeStruct((T, GDN_V_HEADS * D), F32)
    shc = jax.ShapeDtypeStruct((GDN_V_HEADS, T, 1), F32)
    return _pcall(
        body, name="gdn_bwd", grid=(GDN_V_HEADS // E, n),
        in_specs=[qk, qk, vE, colv, colv, rowv, st, am, vE, vE],
        out_specs=[qk, qk, vE, colv, colv], out_shape=[shq, shq, shv, shc, shc],
        scratch_shapes=[pltpu.VMEM((E, D, D), F32)],
        compiler_params=_params(("parallel", "arbitrary")),
    )(q, k, v, beta, gcol, grow, states, amat, vnew, do)


def _outgate_fwd(o, proj, gain):
    T = o.shape[0]
    tm, tc = _pick(T, CONV_ROWS), CONV_COLS
    z0 = GDN_CONV_W // tc

    def body(o_ref, z_ref, g_ref, y_ref):
        z = z_ref[...]
        sz = z * _sigmoid(z)
        parts = []
        for hh in range(tc // GDN_HEAD_DIM):
            oh = o_ref[:, hh * GDN_HEAD_DIM:(hh + 1) * GDN_HEAD_DIM]
            r = lax.rsqrt(jnp.mean(oh * oh, axis=-1, keepdims=True) + EPS)
            parts.append(oh * r * g_ref[...])
        y_ref[...] = (jnp.concatenate(parts, axis=1) * sz).astype(BF16)

    blk = pl.BlockSpec((tm, tc), lambda i, j: (i, j))
    return _pcall(body, name="gdn_outgate", grid=(T // tm, GDN_VW // tc),
                  in_specs=[blk, pl.BlockSpec((tm, tc), lambda i, j: (i, j + z0)), pl.BlockSpec((1, GDN_HEAD_DIM), lambda i, j: (0, 0))],
                  out_specs=blk, out_shape=jax.ShapeDtypeStruct((T, GDN_VW), BF16),
                  compiler_params=_params(("parallel", "parallel")))(o, proj, gain)


def _outgate_bwd(dy, o, proj, gain):
    T = o.shape[0]
    tm, tc = _pick(T, CONV_ROWS), CONV_COLS
    z0 = GDN_CONV_W // tc
    nh = tc // GDN_HEAD_DIM

    def body(dy_ref, o_ref, z_ref, g_ref, do_ref, dz_ref, dg_ref):
        z = z_ref[...]
        sg = _sigmoid(z)
        sz = z * sg
        dy = dy_ref[...]
        dgain = jnp.zeros((1, GDN_HEAD_DIM), F32)
        dos, ys = [], []
        for hh in range(nh):
            sl = slice(hh * GDN_HEAD_DIM, (hh + 1) * GDN_HEAD_DIM)
            oh = o_ref[:, sl]
            r = lax.rsqrt(jnp.mean(oh * oh, axis=-1, keepdims=True) + EPS)
            xh = oh * r
            dn = dy[:, sl] * sz[:, sl]
            dgain = dgain + jnp.sum(dn * xh, axis=0, keepdims=True)
            dxh = dn * g_ref[...]
            dos.append(r * (dxh - xh * jnp.mean(dxh * xh, axis=-1, keepdims=True)))
            ys.append(xh * g_ref[...])
        do_ref[...] = jnp.concatenate(dos, axis=1)
        dz_ref[...] = (dy * jnp.concatenate(ys, axis=1) * sg * (1.0 + z * (1.0 - sg))).astype(BF16)
        first = (pl.program_id(0) == 0) & (pl.program_id(1) == 0)

        @pl.when(first)
        def _():
            dg_ref[...] = dgain

        @pl.when(jnp.logical_not(first))
        def _():
            dg_ref[...] += dgain

    blk = pl.BlockSpec((tm, tc), lambda i, j: (i, j))
    vec = pl.BlockSpec((1, GDN_HEAD_DIM), lambda i, j: (0, 0))
    return _pcall(body, name="gdn_doutgate", grid=(T // tm, GDN_VW // tc),
                  in_specs=[blk, blk, pl.BlockSpec((tm, tc), lambda i, j: (i, j + z0)), vec],
                  out_specs=[blk, blk, vec],
                  out_shape=[jax.ShapeDtypeStruct((T, GDN_VW), F32), jax.ShapeDtypeStruct((T, GDN_VW), BF16),
                             jax.ShapeDtypeStruct((1, GDN_HEAD_DIM), F32)],
                  compiler_params=_params(("arbitrary", "arbitrary")))(dy, o, proj, gain)


def _pad_lanes(vec):
    return jnp.pad(vec.reshape(1, -1), ((0, 0), (0, LANE - vec.shape[-1])))


def _head_cols(a):
    return a[:, :GDN_V_HEADS].T[:, :, None]


def _gdn_pad_in(w_in):
    c = GDN_CONV_W + GDN_VW
    z = jnp.zeros(w_in.shape[:-1] + (LANE - GDN_V_HEADS,), w_in.dtype)
    return jnp.concatenate([w_in[..., :c + GDN_V_HEADS], z, w_in[..., c + GDN_V_HEADS:], z], axis=-1)


def _gdn_unpad_in(dw):
    c = GDN_CONV_W + GDN_VW
    return jnp.concatenate([dw[..., :c + GDN_V_HEADS], dw[..., c + LANE:c + LANE + GDN_V_HEADS]], axis=-1)


def _gdn_mixer_fwd(h, g, w_in_pad, conv_w, a_log, dt_bias, out_gain, w_out):
    T = h.shape[0]
    hn = _rms_fwd(h, g, "gdn_norm")
    proj = _mm(hn, w_in_pad, "nn", name="gdn_in")
    qk = _conv_fwd(proj, conv_w, 0, 2 * GDN_KW, True, "gdn_conv_qk")
    vv = _conv_fwd(proj, conv_w, 2 * GDN_KW, GDN_VW, False, "gdn_conv_v")
    alog, dtb = _pad_lanes(a_log), _pad_lanes(dt_bias)
    beta, gl, gc = _gates_fwd(proj, alog, dtb)
    bcol, gcol = _head_cols(beta), _head_cols(gc)
    grow = gcol.reshape(GDN_V_HEADS, T // GDN_CHUNK, 1, GDN_CHUNK)
    qn, kn = qk[:, :GDN_KW], qk[:, GDN_KW:]
    amat = _gdn_prep(kn, bcol, gcol, grow)
    o, states, vnew = _gdn_fwd(qn, kn, vv, bcol, gcol, grow, amat)
    gain = out_gain.reshape(1, GDN_HEAD_DIM)
    y = _outgate_fwd(o, proj, gain)
    h2 = _mm(y, w_out, "nn", res=h, name="gdn_out")
    return h2, (h, hn, proj, qn, kn, vv, beta, gl, bcol, gcol, grow, o, states, amat, vnew, y, alog, dtb, gain)


def _gdn_mixer_bwd(dh2, saved, g, w_in_pad, conv_w, w_out):
    h, hn, proj, qn, kn, vv, beta, gl, bcol, gcol, grow, o, states, amat, vnew, y, alog, dtb, gain = saved
    T = h.shape[0]
    dy = _mm(dh2, w_out, "nt", name="gdn_dy")
    dw_out = _mm(y, dh2, "tn", name="gdn_dwout")
    do, dz, dgain = _outgate_bwd(dy, o, proj, gain)
    dq, dk, dv, dbcol, dgccol = _gdn_bwd(qn, kn, vv, bcol, gcol, grow, states, amat, vnew, do)
    dqk = jnp.concatenate([dq, dk], axis=1)
    dy_qk, dcw_qk = _conv_bwd_pre(proj, conv_w, dqk, 0, 2 * GDN_KW, True, "gdn_dconv_qk")
    dy_v, dcw_v = _conv_bwd_pre(proj, conv_w, dv, 2 * GDN_KW, GDN_VW, False, "gdn_dconv_v")
    dx_qk = _conv_bwd_in(dy_qk, conv_w[:, :2 * GDN_KW], "gdn_dconvin_qk")
    dx_v = _conv_bwd_in(dy_v, conv_w[:, 2 * GDN_KW:], "gdn_dconvin_v")
    lanes = lambda c: jnp.pad(c[:, :, 0].T, ((0, 0), (0, LANE - GDN_V_HEADS)))
    dbl, da, dalog, ddt = _gates_bwd(proj, alog, dtb, beta, gl, lanes(dbcol), lanes(dgccol))
    dproj = jnp.concatenate([dx_qk, dx_v, dz, dbl, da], axis=1)
    dw_in_pad = _mm(hn, dproj, "tn", name="gdn_dwin")
    dhn = _mm(dproj, w_in_pad, "nt", name="gdn_dhn")
    dh, dg = _rms_bwd(dhn, h, g, dh2, "gdn_dnorm")
    dconv = jnp.concatenate([dcw_qk, dcw_v], axis=1)
    return (dh, dg, _gdn_unpad_in(dw_in_pad), dconv, dalog[0, :GDN_V_HEADS], ddt[0, :GDN_V_HEADS],
            dgain.reshape(GDN_HEAD_DIM), dw_out)


def _local_step(x, p, target, w):
    h = x
    tape = []
    gdn_in_pad = _gdn_pad_in(w["gdn_w_in"][0])
    for i in range(2):
        h, s1 = _ffn_fwd(h, w["ffn_norm"][i, 0], w["ffn_w_gate"][i, 0], w["ffn_w_up"][i, 0], w["ffn_w_down"][i, 0], f"ffn{i}a")
        if i == 0:
            h, s2 = _att_fwd(h, w["mix_norm"][0], w["att_w_in"][0], w["att_w_out"][0], w["att_q_norm"][0],
                             w["att_k_norm"][0], w["att_sinks"][0])
        else:
            h, s2 = _gdn_mixer_fwd(h, w["mix_norm"][1], gdn_in_pad, w["gdn_conv_w"][0], w["gdn_a_log"][0],
                                   w["gdn_dt_bias"][0], w["gdn_out_norm"][0], w["gdn_w_out"][0])
        h, s3 = _ffn_fwd(h, w["ffn_norm"][i, 1], w["ffn_w_gate"][i, 1], w["ffn_w_up"][i, 1], w["ffn_w_down"][i, 1], f"ffn{i}b")
        h, s4 = _ple_fwd(h, p[i], w["ple_norm"][i], w["ple_w_gate"][i], w["ple_w_proj"][i], f"ple{i}")
        tape.append((s1, s2, s3, s4))

    loss, dh = _loss_head(h, target)

    g = {}
    ffn_norm = [[None, None], [None, None]]
    ffn_g = [[None, None], [None, None]]
    ffn_u = [[None, None], [None, None]]
    ffn_d = [[None, None], [None, None]]
    mix_norm, ple_norm, ple_g, ple_p = [None, None], [None, None], [None, None], [None, None]
    for i in (1, 0):
        s1, s2, s3, s4 = tape[i]
        dh, ple_norm[i], ple_g[i], ple_p[i] = _ple_bwd(dh, s4, p[i], w["ple_norm"][i], w["ple_w_gate"][i], f"ple{i}")
        dh, ffn_norm[i][1], ffn_g[i][1], ffn_u[i][1], ffn_d[i][1] = _ffn_bwd(
            dh, s3, w["ffn_norm"][i, 1], w["ffn_w_gate"][i, 1], w["ffn_w_up"][i, 1], w["ffn_w_down"][i, 1], f"ffn{i}b")
        if i == 0:
            dh, mix_norm[0], dwin, dwout, dqg, dkg, dsink = _att_bwd(dh, s2, w["mix_norm"][0], w["att_w_in"][0], w["att_w_out"][0])
            g["att_w_in"], g["att_w_out"] = dwin[None], dwout[None]
            g["att_q_norm"], g["att_k_norm"], g["att_sinks"] = dqg[None], dkg[None], dsink[None]
        else:
            dh, mix_norm[1], dwin, dconv, dalog, ddt, dgain, dwout = _gdn_mixer_bwd(
                dh, s2, w["mix_norm"][1], gdn_in_pad, w["gdn_conv_w"][0], w["gdn_w_out"][0])
            g["gdn_w_in"], g["gdn_conv_w"], g["gdn_w_out"] = dwin[None], dconv[None], dwout[None]
            g["gdn_a_log"], g["gdn_dt_bias"], g["gdn_out_norm"] = dalog[None], ddt[None], dgain[None]
        dh, ffn_norm[i][0], ffn_g[i][0], ffn_u[i][0], ffn_d[i][0] = _ffn_bwd(
            dh, s1, w["ffn_norm"][i, 0], w["ffn_w_gate"][i, 0], w["ffn_w_up"][i, 0], w["ffn_w_down"][i, 0], f"ffn{i}a")
    st2 = lambda rows: jnp.stack([jnp.stack(r) for r in rows])
    g["ffn_norm"], g["ffn_w_gate"], g["ffn_w_up"], g["ffn_w_down"] = st2(ffn_norm), st2(ffn_g), st2(ffn_u), st2(ffn_d)
    g["mix_norm"], g["ple_norm"] = jnp.stack(mix_norm), jnp.stack(ple_norm)
    g["ple_w_gate"], g["ple_w_proj"] = jnp.stack(ple_g), jnp.stack(ple_p)
    return loss, dh, g


MESH = pl.DeviceIdType.MESH


def _me_and_peers():
    x, y, c = lax.axis_index("x"), lax.axis_index("y"), lax.axis_index("c")
    flip = lambda v, bit: 1 - v if bit else v
    peers = []
    for k in range(1, N_DEV):
        px, py, pc = flip(x, k & 4), flip(y, k & 2), flip(c, k & 1)
        peers.append(((px, py, pc), 4 * px + 2 * py + pc))
    return 4 * x + 2 * y + c, peers


def _exchange(arrays, scatter, name):
    n = len(arrays)

    def body(*refs):
        ins, outs = refs[:n], refs[n:2 * n]
        send_sems, recv_sems, local_sems = refs[2 * n:]
        me, peers = _me_and_peers()
        local = []
        for a in range(n):
            src = ins[a].at[me] if scatter else ins[a]
            cp = pltpu.make_async_copy(src, outs[a].at[me], local_sems.at[a])
            cp.start()
            local.append(cp)
        remote = []
        for a in range(n):
            for k, (peer, peer_idx) in enumerate(peers):
                src = ins[a].at[peer_idx] if scatter else ins[a]
                cp = pltpu.make_async_remote_copy(
                    src_ref=src, dst_ref=outs[a].at[me], send_sem=send_sems.at[a, k], recv_sem=recv_sems.at[a, k],
                    device_id=peer, device_id_type=MESH)
                cp.start()
                remote.append(cp)
        for cp in remote:
            cp.wait()
        for cp in local:
            cp.wait()

    hbm = pl.BlockSpec(memory_space=pl.ANY)
    out_shape = [jax.ShapeDtypeStruct(a.shape if scatter else (N_DEV,) + a.shape, a.dtype) for a in arrays]
    return _pcall(
        body, name=name, in_specs=[hbm] * n, out_specs=[hbm] * n, out_shape=out_shape,
        scratch_shapes=[pltpu.SemaphoreType.DMA((n, N_DEV - 1)), pltpu.SemaphoreType.DMA((n, N_DEV - 1)),
                        pltpu.SemaphoreType.DMA((n,))],
        compiler_params=pltpu.CompilerParams(has_side_effects=True),
    )(*arrays)


def _adamw(parts, w, m, v, name):
    R, Wd = w.shape
    tr = _pick(R, 256) if R % 256 == 0 else R
    c1 = 1.0 / (1.0 - ADAM_B1 ** ADAM_STEP)
    c2 = 1.0 / (1.0 - ADAM_B2 ** ADAM_STEP)

    def body(p_ref, w_ref, m_ref, v_ref, g_ref, d_ref, nm_ref, nv_ref):
        g = p_ref[0]
        for dev in range(1, N_DEV):
            g = g + p_ref[dev]
        mn = ADAM_B1 * m_ref[...] + (1.0 - ADAM_B1) * g
        vn = ADAM_B2 * v_ref[...] + (1.0 - ADAM_B2) * (g * g)
        g_ref[...] = g
        nm_ref[...] = mn
        nv_ref[...] = vn
        d_ref[...] = -ADAM_LR * ((mn * c1) / (jnp.sqrt(vn * c2) + ADAM_EPS) + ADAM_WD * w_ref[...])

    row = pl.BlockSpec((tr, Wd), lambda i: (i, 0))
    sh = jax.ShapeDtypeStruct((R, Wd), F32)
    return _pcall(body, name=name, grid=(R // tr,),
                  in_specs=[pl.BlockSpec((N_DEV, tr, Wd), lambda i: (0, i, 0)), row, row, row],
                  out_specs=[row, row, row, row], out_shape=[sh, sh, sh, sh],
                  compiler_params=_params(("parallel",)))(parts, w, m, v)


def _pack(pieces, row_align):
    rows, offs, r = [], [], 0
    for a in pieces:
        flat = a.reshape(-1)
        nr = -(-flat.shape[0] // PACK_W)
        flat = jnp.pad(flat, (0, nr * PACK_W - flat.shape[0]))
        rows.append(flat.reshape(nr, PACK_W))
        offs.append(r)
        r += nr
    pad = (-r) % row_align
    if pad:
        rows.append(jnp.zeros((pad, PACK_W), pieces[0].dtype))
    return jnp.concatenate(rows, axis=0), offs


def _unpack(flat, offs, shapes):
    out = []
    for off, shp in zip(offs, shapes):
        size = math.prod(shp)
        nr = -(-size // PACK_W)
        out.append(flat[..., off:off + nr, :].reshape(flat.shape[:-2] + (nr * PACK_W,))[..., :size].reshape(flat.shape[:-2] + tuple(shp)))
    return out


def _to_full(gathered, axis):
    z = jnp.moveaxis(gathered, 0, axis)
    shp = list(z.shape)
    return z.reshape(shp[:axis] + [shp[axis] * shp[axis + 1]] + shp[axis + 2:])


def _to_shards(full, axis):
    shp = list(full.shape)
    z = full.reshape(shp[:axis] + [N_DEV, shp[axis] // N_DEV] + shp[axis + 1:])
    return jnp.moveaxis(z, axis, 0)


def kernel(x, p, ffn_norm, ffn_w_gate, ffn_w_up, ffn_w_down, mix_norm, att_w_in, att_q_norm, att_k_norm, att_sinks, att_w_out, gdn_w_in, gdn_conv_w, gdn_a_log, gdn_dt_bias, gdn_out_norm, gdn_w_out, ple_norm, ple_w_gate, ple_w_proj, loss_target, m_ffn_norm, m_ffn_w_gate, m_ffn_w_up, m_ffn_w_down, m_mix_norm, m_att_w_in, m_att_q_norm, m_att_k_norm, m_att_sinks, m_att_w_out, m_gdn_w_in, m_gdn_conv_w, m_gdn_a_log, m_gdn_dt_bias, m_gdn_out_norm, m_gdn_w_out, m_ple_norm, m_ple_w_gate, m_ple_w_proj, v_ffn_norm, v_ffn_w_gate, v_ffn_w_up, v_ffn_w_down, v_mix_norm, v_att_w_in, v_att_q_norm, v_att_k_norm, v_att_sinks, v_att_w_out, v_gdn_w_in, v_gdn_conv_w, v_gdn_a_log, v_gdn_dt_bias, v_gdn_out_norm, v_gdn_w_out, v_ple_norm, v_ple_w_gate, v_ple_w_proj):
    args = dict(locals())
    wts = {n: args[n] for n in WEIGHTS}
    mom = {n: args["m_" + n] for n in WEIGHTS}
    var = {n: args["v_" + n] for n in WEIGHTS}
    sh_names = [n for n, _ in SHARDED]
    sh_axis = dict(SHARDED)
    sh_shapes = [wts[n].shape for n in sh_names]
    rep_shapes = [wts[n].shape for n in REPLICATED]

    w_flat, offs = _pack([wts[n] for n in sh_names], PACK_ROW_ALIGN)
    m_flat, _ = _pack([mom[n] for n in sh_names], PACK_ROW_ALIGN)
    v_flat, _ = _pack([var[n] for n in sh_names], PACK_ROW_ALIGN)
    wr_flat, roffs = _pack([wts[n] for n in REPLICATED], 8)
    mr_flat, _ = _pack([mom[n] for n in REPLICATED], 8)
    vr_flat, _ = _pack([var[n] for n in REPLICATED], 8)
    n_small = offs[SMALL_SHARDED]
    small_rows = -(-n_small // 8) * 8

    w_bf = w_flat.astype(BF16)
    g_bf, g_small = _exchange([w_bf, w_flat[:small_rows]], False, "gather_weights")
    full = {}
    for n, piece in zip(sh_names[:SMALL_SHARDED], _unpack(g_small, offs[:SMALL_SHARDED], sh_shapes[:SMALL_SHARDED])):
        full[n] = _to_full(piece, sh_axis[n])
    for n, piece in zip(sh_names[SMALL_SHARDED:], _unpack(g_bf, offs[SMALL_SHARDED:], sh_shapes[SMALL_SHARDED:])):
        full[n] = _to_full(piece, sh_axis[n])
    for n in REPLICATED:
        full[n] = wts[n]

    loss, grad_x, grads = _local_step(x[0], p[:, 0], loss_target[0], full)
    loss = lax.psum(loss, ("x", "y", "c"))

    pieces = [_to_shards(grads[n], sh_axis[n]) for n in sh_names]
    send = jnp.stack([_pack([pc[d] for pc in pieces], PACK_ROW_ALIGN)[0] for d in range(N_DEV)])
    rep_part, _ = _pack([grads[n] for n in REPLICATED], 8)
    rep_send = jnp.broadcast_to(rep_part[None], (N_DEV,) + rep_part.shape)
    recv, rep_recv = _exchange([send, rep_send], True, "exchange_grads")

    g_flat, d_flat, nm_flat, nv_flat = _adamw(recv, w_flat, m_flat, v_flat, "adamw_sharded")
    gr_flat, dr_flat, nmr_flat, nvr_flat = _adamw(rep_recv, wr_flat, mr_flat, vr_flat, "adamw_replicated")

    outs = {}
    for tag, fs, fr in (("grad", g_flat, gr_flat), ("delta", d_flat, dr_flat), ("new_m", nm_flat, nmr_flat), ("new_v", nv_flat, nvr_flat)):
        for n, piece in zip(sh_names, _unpack(fs, offs, sh_shapes)):
            outs[(tag, n)] = piece
        for n, piece in zip(REPLICATED, _unpack(fr, roffs, rep_shapes)):
            outs[(tag, n)] = piece
    result = [loss, grad_x[None]]
    for tag in ("grad", "delta", "new_m", "new_v"):
        result += [outs[(tag, n)] for n in WEIGHTS]
    return tuple(result)
```

```python
import math

import jax
import jax.numpy as jnp
from jax import lax
from jax.experimental import pallas as pl
from jax.experimental.pallas import tpu as pltpu

F32 = jnp.float32
BF16 = jnp.bfloat16

N_DEV = 8
D_MODEL = 1024
D_FF = 2816
PLE_DIM = 256
HEAD_DIM = 64
SB_HEADS = 8
SWA_HEADS = 8
SWA_KV_HEADS = 2
SWA_GROUP = SWA_HEADS // SWA_KV_HEADS
WINDOW = 128
Q_BLOCK = 128
GDN_K_HEADS = 8
GDN_V_HEADS = 16
GDN_HEAD_DIM = 128
GDN_CONV = 4
GDN_CHUNK = 64
EPS = 1e-6
SB_W = SB_HEADS * HEAD_DIM
SWA_QW = SWA_HEADS * HEAD_DIM
SWA_KVW = SWA_KV_HEADS * HEAD_DIM
ATT_IN = 3 * SB_W + SWA_QW + 2 * SWA_KVW
GDN_KW = GDN_K_HEADS * GDN_HEAD_DIM
GDN_VW = GDN_V_HEADS * GDN_HEAD_DIM
GDN_CONV_W = 2 * GDN_KW + GDN_VW
GDN_IN = GDN_CONV_W + GDN_VW + 2 * GDN_V_HEADS
GDN_IN_PAD = GDN_CONV_W + GDN_VW + 2 * 128

ADAM_LR = 0.001
ADAM_B1 = 0.9
ADAM_B2 = 0.999
ADAM_EPS = 1e-08
ADAM_WD = 0.01
ADAM_STEP = 10

LANE = 128
VMEM_LIMIT = 56 * 1024 * 1024
PACK_W = 1024

NN = ((1,), (0,))
NT = ((1,), (1,))
TN = ((0,), (0,))

SHARDED = (
    ("ffn_norm", 2), ("gdn_conv_w", 2),
    ("ffn_w_gate", 3), ("ffn_w_up", 3), ("ffn_w_down", 2), ("att_w_in", 2), ("att_w_out", 1),
    ("gdn_w_in", 2), ("gdn_w_out", 1), ("ple_w_gate", 1), ("ple_w_proj", 2),
)
SMALL_SHARDED = 2
REPLICATED = ("mix_norm", "att_q_norm", "att_k_norm", "att_sinks", "gdn_a_log", "gdn_dt_bias",
              "gdn_out_norm", "ple_norm")
WEIGHTS = ("ffn_norm", "ffn_w_gate", "ffn_w_up", "ffn_w_down", "mix_norm", "att_w_in", "att_q_norm",
           "att_k_norm", "att_sinks", "att_w_out", "gdn_w_in", "gdn_conv_w", "gdn_a_log", "gdn_dt_bias",
           "gdn_out_norm", "gdn_w_out", "ple_norm", "ple_w_gate", "ple_w_proj")


def _pcall(body, **kw):
    return pl.pallas_call(body, **kw)


def _params(sem=None):
    if sem is None:
        return pltpu.CompilerParams(vmem_limit_bytes=VMEM_LIMIT)
    return pltpu.CompilerParams(dimension_semantics=sem, vmem_limit_bytes=VMEM_LIMIT)


def _dot(a, b, dims=NN):
    return lax.dot_general(a, b, (dims, ((), ())), preferred_element_type=F32)


def _bdot(a, b, dims=NN):
    return _dot(a.astype(BF16), b.astype(BF16), dims)


def _split(a):
    hi = a.astype(BF16)
    lo = (a - hi.astype(F32)).astype(BF16)
    return hi, lo


def _dot3(a, b, dims=NN):
    ah, al = _split(a)
    bh, bl = _split(b)
    return _dot(ah, bh, dims) + (_dot(ah, bl, dims) + _dot(al, bh, dims))


def _dot2m(a, m, dims=NN):
    ah, al = _split(a)
    return _dot(ah, m, dims) + _dot(al, m, dims)


def _mdot2(m, a, dims=NN):
    ah, al = _split(a)
    return _dot(m, ah, dims) + _dot(m, al, dims)


def _sigmoid(x):
    return 1.0 / (1.0 + jnp.exp(-x))


def _softplus(x):
    return jnp.maximum(x, 0.0) + jnp.log(1.0 + jnp.exp(-jnp.abs(x)))


def _pick(n, cap):
    if n <= cap:
        return n
    for t in range(cap - cap % LANE, 0, -LANE):
        if n % t == 0:
            return t
    raise ValueError(f"no tile for {n} under {cap}")


def _iota2(shape, axis):
    return lax.broadcasted_iota(jnp.int32, shape, axis)


def _mm(a, b, mode, out_dtype=F32, res=None, alpha=1.0, a2=None, b2=None, name="mm"):
    if mode == "nn":
        (M, K), N = a.shape, b.shape[1]
    elif mode == "nt":
        (M, K), N = a.shape, b.shape[0]
    else:
        (K, M), N = a.shape, b.shape[1]
    tm, tn, tk = _pick(M, 1408 if mode == "tn" else 512), _pick(N, 1408), _pick(K, 1024 if mode == "tn" else 1408)
    nk = K // tk
    dims = {"nn": NN, "nt": NT, "tn": TN}[mode]
    a_spec = pl.BlockSpec((tk, tm), lambda i, j, k: (k, i)) if mode == "tn" else pl.BlockSpec((tm, tk), lambda i, j, k: (i, k))
    b_spec = pl.BlockSpec((tn, tk), lambda i, j, k: (j, k)) if mode == "nt" else pl.BlockSpec((tk, tn), lambda i, j, k: (k, j))
    o_spec = pl.BlockSpec((tm, tn), lambda i, j, k: (i, j))
    two = a2 is not None
    has_res = res is not None

    def body(*refs):
        refs = list(refs)
        a_ref, b_ref = refs[0], refs[1]
        pos = 2
        if two:
            a2_ref, b2_ref = refs[2], refs[3]
            pos = 4
        if has_res:
            res_ref = refs[pos]
            pos += 1
        o_ref, acc_ref = refs[pos], refs[pos + 1]
        k = pl.program_id(2)
        part = _bdot(a_ref[...], b_ref[...], dims)
        if two:
            part = part + _bdot(a2_ref[...], b2_ref[...], dims)

        def finish(acc):
            out = acc * alpha if alpha != 1.0 else acc
            if has_res:
                out = res_ref[...] + out
            o_ref[...] = out.astype(out_dtype)

        if nk == 1:
            finish(part)
        else:
            @pl.when(k == 0)
            def _():
                acc_ref[...] = part

            @pl.when(k > 0)
            def _():
                acc_ref[...] += part

            @pl.when(k == nk - 1)
            def _():
                finish(acc_ref[...])

    ins = [a, b]
    specs = [a_spec, b_spec]
    if two:
        ins += [a2, b2]
        specs += [a_spec, b_spec]
    if has_res:
        ins.append(res)
        specs.append(o_spec)
    return _pcall(
        body, name=name, grid=(M // tm, N // tn, nk), in_specs=specs, out_specs=o_spec,
        out_shape=jax.ShapeDtypeStruct((M, N), out_dtype),
        scratch_shapes=[pltpu.VMEM((tm, tn) if nk > 1 else (8, LANE), F32)],
        compiler_params=_params(("parallel", "parallel", "arbitrary")),
    )(*ins)


ROW_TILE = 256


def _rms_fwd(h, g, name):
    T, D = h.shape
    tr = _pick(T, ROW_TILE)

    def body(h_ref, g_ref, n_ref):
        x = h_ref[...]
        r = lax.rsqrt(jnp.mean(x * x, axis=-1, keepdims=True) + EPS)
        n_ref[...] = (x * r * g_ref[...]).astype(BF16)

    return _pcall(
        body, name=name, grid=(T // tr,),
        in_specs=[pl.BlockSpec((tr, D), lambda i: (i, 0)), pl.BlockSpec((1, D), lambda i: (0, 0))],
        out_specs=pl.BlockSpec((tr, D), lambda i: (i, 0)),
        out_shape=jax.ShapeDtypeStruct((T, D), BF16), compiler_params=_params(("parallel",)),
    )(h, g.reshape(1, D))


def _rms_bwd(dn, h, g, dres, name):
    T, D = h.shape
    tr = _pick(T, ROW_TILE)

    def body(dn_ref, h_ref, g_ref, dres_ref, dh_ref, dg_ref):
        x = h_ref[...]
        r = lax.rsqrt(jnp.mean(x * x, axis=-1, keepdims=True) + EPS)
        xh = x * r
        d = dn_ref[...].astype(F32)
        dxh = d * g_ref[...]
        dh_ref[...] = dres_ref[...] + r * (dxh - xh * jnp.mean(dxh * xh, axis=-1, keepdims=True))
        part = jnp.sum(d * xh, axis=0, keepdims=True)

        @pl.when(pl.program_id(0) == 0)
        def _():
            dg_ref[...] = part

        @pl.when(pl.program_id(0) > 0)
        def _():
            dg_ref[...] += part

    row = pl.BlockSpec((tr, D), lambda i: (i, 0))
    vec = pl.BlockSpec((1, D), lambda i: (0, 0))
    dh, dg = _pcall(
        body, name=name, grid=(T // tr,), in_specs=[row, row, vec, row], out_specs=[row, vec],
        out_shape=[jax.ShapeDtypeStruct((T, D), F32), jax.ShapeDtypeStruct((1, D), F32)],
        compiler_params=_params(("arbitrary",)),
    )(dn, h, g.reshape(1, D), dres)
    return dh, dg.reshape(D)


def _gateup(n, wg, wu, name):
    T, D = n.shape
    F = wg.shape[1]
    tm, tn = _pick(T, 512), _pick(F, 1408)

    def body(n_ref, wg_ref, wu_ref, a_ref, b_ref, hid_ref):
        x = n_ref[...]
        a = _dot(x, wg_ref[...])
        b = _dot(x, wu_ref[...])
        a_ref[...] = a.astype(BF16)
        b_ref[...] = b.astype(BF16)
        hid_ref[...] = (a * _sigmoid(a) * b).astype(BF16)

    o_spec = pl.BlockSpec((tm, tn), lambda i, j: (i, j))
    w_spec = pl.BlockSpec((D, tn), lambda i, j: (0, j))
    sh = jax.ShapeDtypeStruct((T, F), BF16)
    return _pcall(
        body, name=name, grid=(T // tm, F // tn),
        in_specs=[pl.BlockSpec((tm, D), lambda i, j: (i, 0)), w_spec, w_spec],
        out_specs=[o_spec, o_spec, o_spec], out_shape=[sh, sh, sh],
        compiler_params=_params(("parallel", "parallel")),
    )(n, wg, wu)


def _ffn_dhid(dy, wd, a, b, name):
    T, D = dy.shape
    F = wd.shape[0]
    tm, tn = _pick(T, 512), _pick(F, 1408)

    def body(dy_ref, wd_ref, a_ref, b_ref, da_ref, db_ref):
        dhid = 0.5 * _bdot(dy_ref[...], wd_ref[...], NT)
        av = a_ref[...].astype(F32)
        bv = b_ref[...].astype(F32)
        s = _sigmoid(av)
        da_ref[...] = (dhid * bv * s * (1.0 + av * (1.0 - s))).astype(BF16)
        db_ref[...] = (dhid * av * s).astype(BF16)

    o_spec = pl.BlockSpec((tm, tn), lambda i, j: (i, j))
    sh = jax.ShapeDtypeStruct((T, F), BF16)
    return _pcall(
        body, name=name, grid=(T // tm, F // tn),
        in_specs=[pl.BlockSpec((tm, D), lambda i, j: (i, 0)), pl.BlockSpec((tn, D), lambda i, j: (j, 0)), o_spec, o_spec],
        out_specs=[o_spec, o_spec], out_shape=[sh, sh],
        compiler_params=_params(("parallel", "parallel")),
    )(dy, wd, a, b)


def _ffn_fwd(h, g, wg, wu, wd, tag):
    n = _rms_fwd(h, g, f"{tag}_norm")
    a, b, hid = _gateup(n, wg, wu, f"{tag}_gateup")
    h2 = _mm(hid, wd, "nn", res=h, alpha=0.5, name=f"{tag}_down")
    return h2, (h, n, a, b, hid)


def _ffn_bwd(dh2, saved, g, wg, wu, wd, tag):
    h, n, a, b, hid = saved
    da, db = _ffn_dhid(dh2, wd, a, b, f"{tag}_dhid")
    dwd = _mm(hid, dh2, "tn", alpha=0.5, out_dtype=BF16, name=f"{tag}_dwd")
    dwg = _mm(n, da, "tn", out_dtype=BF16, name=f"{tag}_dwg")
    dwu = _mm(n, db, "tn", out_dtype=BF16, name=f"{tag}_dwu")
    dn = _mm(da, wg, "nt", a2=db, b2=wu, name=f"{tag}_dn")
    dh, dg = _rms_bwd(dn, h, g, dh2, f"{tag}_dnorm")
    return dh, dg, dwg, dwu, dwd


def _ple_fwd(h, p, g, w_gate, w_proj, tag):
    T, D = h.shape
    pn = _rms_fwd(h, g, f"{tag}_norm")
    tm, tn = _pick(T, 512), _pick(D, 1024)
    P = p.shape[1]

    def body(pn_ref, p_ref, wg_ref, wp_ref, h_ref, o_ref, gl_ref, pp_ref):
        gl = _dot(pn_ref[...], wg_ref[...])
        pp = _bdot(p_ref[...], wp_ref[...])
        gl_ref[...] = gl
        pp_ref[...] = pp
        o_ref[...] = h_ref[...] + _sigmoid(gl) * pp

    o_spec = pl.BlockSpec((tm, tn), lambda i, j: (i, j))
    sh = jax.ShapeDtypeStruct((T, D), F32)
    h2, gl, pp = _pcall(
        body, name=f"{tag}_fwd", grid=(T // tm, D // tn),
        in_specs=[pl.BlockSpec((tm, D), lambda i, j: (i, 0)), pl.BlockSpec((tm, P), lambda i, j: (i, 0)),
                  pl.BlockSpec((D, tn), lambda i, j: (0, j)), pl.BlockSpec((P, tn), lambda i, j: (0, j)), o_spec],
        out_specs=[o_spec, o_spec, o_spec], out_shape=[sh, sh, sh],
        compiler_params=_params(("parallel", "parallel")),
    )(pn, p, w_gate, w_proj, h)
    return h2, (h, pn, gl, pp)


def _ple_bwd(dh2, saved, p, g, w_gate, tag):
    h, pn, gl, pp = saved
    T, D = h.shape
    tr = _pick(T, ROW_TILE)

    def body(d_ref, gl_ref, pp_ref, dgl_ref, dpp_ref):
        d = d_ref[...]
        s = _sigmoid(gl_ref[...])
        dpp_ref[...] = (d * s).astype(BF16)
        dgl_ref[...] = (d * pp_ref[...] * s * (1.0 - s)).astype(BF16)

    row = pl.BlockSpec((tr, D), lambda i: (i, 0))
    sh = jax.ShapeDtypeStruct((T, D), BF16)
    dgl, dpp = _pcall(body, name=f"{tag}_dgate", grid=(T // tr,), in_specs=[row, row, row], out_specs=[row, row],
                      out_shape=[sh, sh], compiler_params=_params(("parallel",)))(dh2, gl, pp)
    dw_proj = _mm(p, dpp, "tn", out_dtype=BF16, name=f"{tag}_dwproj")
    dw_gate = _mm(pn, dgl, "tn", out_dtype=BF16, name=f"{tag}_dwgate")
    dpn = _mm(dgl, w_gate, "nt", name=f"{tag}_dpn")
    dh, dg = _rms_bwd(dpn, h, g, dh2, f"{tag}_dnorm")
    return dh, dg, dw_gate, dw_proj


def _loss_head(y, target):
    T, D = y.shape
    tr = _pick(T, ROW_TILE)

    def body(y_ref, t_ref, dy_ref, l_ref):
        e = y_ref[...] - t_ref[...]
        dy_ref[...] = e * (1.0 / D)
        part = jnp.sum(e * e, axis=0, keepdims=True)

        @pl.when(pl.program_id(0) == 0)
        def _():
            l_ref[...] = part

        @pl.when(pl.program_id(0) > 0)
        def _():
            l_ref[...] += part

    row = pl.BlockSpec((tr, D), lambda i: (i, 0))
    vec = pl.BlockSpec((1, D), lambda i: (0, 0))
    dy, l = _pcall(body, name="loss_head", grid=(T // tr,), in_specs=[row, row], out_specs=[row, vec],
                   out_shape=[jax.ShapeDtypeStruct((T, D), F32), jax.ShapeDtypeStruct((1, D), F32)],
                   compiler_params=_params(("arbitrary",)))(y, target)
    return (0.5 / D) * jnp.sum(l), dy


SB_GROUP_FWD = 8
SB_GROUP_BWD = 4


def _sb_consts():
    row = _iota2((Q_BLOCK, Q_BLOCK), 0)
    col = _iota2((Q_BLOCK, Q_BLOCK), 1)
    after = (row > col).astype(BF16)
    before = (row < col).astype(BF16)
    return col < row, after, before, col


def _sb_fwd(q, k, v):
    H, T, d = q.shape
    nblk = T // Q_BLOCK
    scale = d ** -0.5
    G = SB_GROUP_FWD

    def body(q_ref, k_ref, v_ref, o_ref, c_ref, run_ref):
        i = pl.program_id(1)
        causal, after, _, col = _sb_consts()
        qs = [q_ref[g] * scale for g in range(G)]
        o_ref[...] = jnp.zeros_like(o_ref)
        c_ref[...] = jnp.zeros_like(c_ref)
        run_ref[...] = jnp.zeros_like(run_ref)

        def pair(j, diag):
            off = pl.multiple_of(j * Q_BLOCK, Q_BLOCK)
            R = range(G)
            kj = [k_ref[g, pl.ds(off, Q_BLOCK), :] for g in R]
            vj = [v_ref[g, pl.ds(off, Q_BLOCK), :] for g in R]
            c = [run_ref[g] for g in R]
            acc = [o_ref[g] for g in R]
            cm = None if diag else [c_ref[g] for g in R]
            z = [_dot(qs[g], kj[g], NT) for g in R]
            sp = [_softplus(z[g]) for g in R]
            lk = [jnp.where(causal, -sp[g], 0.0) if diag else -sp[g] for g in R]
            btw = [_dot2m(lk[g], after) for g in R]
            e = [jnp.exp((z[g] - sp[g]) + btw[g] + c[g]) for g in R]
            w = [jnp.where(causal, e[g], 0.0) if diag else e[g] for g in R]
            pv = [_bdot(w[g], vj[g]) for g in R]
            rs = [jnp.sum(lk[g], axis=1, keepdims=True) for g in R]
            for g in R:
                o_ref[g] = acc[g] + pv[g]
                if not diag:
                    c_ref[g] = jnp.where(col == j, c[g], cm[g])
                run_ref[g] = c[g] + rs[g]

        pair(i, True)

        @pl.loop(0, i)
        def _(jj):
            pair(i - 1 - jj, False)

    blk = pl.BlockSpec((G, Q_BLOCK, d), lambda h, i: (h, i, 0))
    full = pl.BlockSpec((G, T, d), lambda h, i: (h, 0, 0))
    return _pcall(
        body, name="sb_fwd", grid=(H // G, nblk), in_specs=[blk, full, full],
        out_specs=[blk, pl.BlockSpec((G, Q_BLOCK, LANE), lambda h, i: (h, i, 0))],
        out_shape=[jax.ShapeDtypeStruct((H, T, d), F32), jax.ShapeDtypeStruct((H, T, LANE), F32)],
        scratch_shapes=[pltpu.VMEM((G, Q_BLOCK, 1), F32)],
        compiler_params=_params(("parallel", "parallel")),
    )(q, k, v)


def _sb_bwd(q, k, v, carry, do):
    H, T, d = q.shape
    nblk = T // Q_BLOCK
    scale = d ** -0.5
    G = SB_GROUP_BWD

    def body(q_ref, k_ref, v_ref, c_ref, do_ref, dq_ref, dk_ref, dv_ref, run_ref):
        i = pl.program_id(1)

        @pl.when(i == 0)
        def _():
            dk_ref[...] = jnp.zeros_like(dk_ref)
            dv_ref[...] = jnp.zeros_like(dv_ref)

        causal, after, before, col = _sb_consts()
        qs = [q_ref[g] * scale for g in range(G)]
        dov = [do_ref[g].astype(BF16) for g in range(G)]
        dq_ref[...] = jnp.zeros_like(dq_ref)
        run_ref[...] = jnp.zeros_like(run_ref)

        def pair(j, diag):
            off = pl.multiple_of(j * Q_BLOCK, Q_BLOCK)
            R = range(G)
            rows = pl.ds(off, Q_BLOCK)
            kj = [k_ref[g, rows, :] for g in R]
            vj = [v_ref[g, rows, :] for g in R]
            gsum = [run_ref[g] for g in R]
            dq0 = [dq_ref[g] for g in R]
            dk0 = [dk_ref[g, rows, :] for g in R]
            dv0 = [dv_ref[g, rows, :] for g in R]
            cm = None if diag else [c_ref[g] for g in R]
            z = [_dot(qs[g], kj[g], NT) for g in R]
            sp = [_softplus(z[g]) for g in R]
            lk = [jnp.where(causal, -sp[g], 0.0) if diag else -sp[g] for g in R]
            ls = [z[g] - sp[g] for g in R]
            logw = [ls[g] + _dot2m(lk[g], after) for g in R]
            if not diag:
                logw = [logw[g] + jnp.sum(jnp.where(col == j, cm[g], 0.0), axis=1, keepdims=True) for g in R]
            e = [jnp.exp(logw[g]) for g in R]
            w = [jnp.where(causal, e[g], 0.0) if diag else e[g] for g in R]
            gw = [_dot(dov[g], vj[g], NT) * w[g] for g in R]
            gpre = [gsum[g] + _dot2m(gw[g], before) for g in R]
            sig = [jnp.exp(ls[g]) for g in R]
            dz = [gw[g] * (1.0 - sig[g]) - sig[g] * gpre[g] for g in R]
            if diag:
                dz = [jnp.where(causal, dz[g], 0.0) for g in R]
            dzb = [dz[g].astype(BF16) for g in R]
            dq1 = [_dot(dzb[g], kj[g]) for g in R]
            dk1 = [_dot(dzb[g], qs[g], TN) for g in R]
            dv1 = [_dot(w[g].astype(BF16), dov[g], TN) for g in R]
            gs1 = [jnp.sum(gw[g], axis=1, keepdims=True) for g in R]
            for g in R:
                dq_ref[g] = dq0[g] + dq1[g]
                dk_ref[g, rows, :] = dk0[g] + dk1[g]
                dv_ref[g, rows, :] = dv0[g] + dv1[g]
                run_ref[g] = gsum[g] + gs1[g]

        @pl.loop(0, i)
        def _(j):
            pair(j, False)

        pair(i, True)
        dq_ref[...] = dq_ref[...] * scale

    blk = pl.BlockSpec((G, Q_BLOCK, d), lambda h, i: (h, i, 0))
    full = pl.BlockSpec((G, T, d), lambda h, i: (h, 0, 0), pipeline_mode=pl.Buffered(1))
    sh = jax.ShapeDtypeStruct((H, T, d), F32)
    return _pcall(
        body, name="sb_bwd", grid=(H // G, nblk),
        in_specs=[blk, full, full, pl.BlockSpec((G, Q_BLOCK, LANE), lambda h, i: (h, i, 0)), blk],
        out_specs=[blk, full, full], out_shape=[sh, sh, sh],
        scratch_shapes=[pltpu.VMEM((G, Q_BLOCK, 1), F32)],
        compiler_params=_params(("parallel", "arbitrary")),
    )(q, k, v, carry, do)


def _swa_fwd(q, k, v, qg, kg, sinks, slopes):
    Hq, T, d = q.shape
    Hkv = k.shape[0]
    G = Hq // Hkv
    W = WINDOW
    nblk = T // W
    scale = d ** -0.5

    def body(q_ref, kp_ref, kc_ref, vp_ref, vc_ref, qg_ref, kg_ref, sk_ref, sl_ref, o_ref):
        hk = pl.program_id(0)
        n = pl.program_id(1)
        kcat = jnp.concatenate([kp_ref[...], kc_ref[...]], axis=0)
        vcat = jnp.concatenate([vp_ref[...], vc_ref[...]], axis=0).astype(BF16)
        rk = lax.rsqrt(jnp.mean(kcat * kcat, axis=-1, keepdims=True) + EPS)
        kn = (kcat * rk * kg_ref[...]).astype(BF16)
        row = _iota2((W, 2 * W), 0)
        col = _iota2((W, 2 * W), 1)
        dist = row + W - col
        valid = (dist >= 0) & (dist < W) & ((n > 0) | (col >= W))
        distf = dist.astype(F32)
        for g in range(G):
            qh = q_ref[g]
            rq = lax.rsqrt(jnp.mean(qh * qh, axis=-1, keepdims=True) + EPS)
            qn = (qh * rq * qg_ref[...]).astype(BF16)
            sink = sk_ref[pl.ds(hk * G + g, 1), :][:, :1]
            slope = sl_ref[pl.ds(hk * G + g, 1), :][:, :1]
            s = _dot(qn, kn, NT) * scale - slope * distf
            s = jnp.where(valid, s, -1e30)
            m = jnp.maximum(jnp.max(s, axis=1, keepdims=True), sink)
            p = jnp.where(valid, jnp.exp(s - m), 0.0)
            den = jnp.sum(p, axis=1, keepdims=True) + jnp.exp(sink - m)
            o_ref[g] = _bdot(p / den, vcat)

    qblk = pl.BlockSpec((G, W, d), lambda h, n: (h, n, 0))
    prev = pl.BlockSpec((None, W, d), lambda h, n: (h, jnp.maximum(n - 1, 0), 0))
    cur = pl.BlockSpec((None, W, d), lambda h, n: (h, n, 0))
    gain = pl.BlockSpec((1, d), lambda h, n: (0, 0))
    perhead = pl.BlockSpec((Hq, LANE), lambda h, n: (0, 0))
    return _pcall(
        body, name="swa_fwd", grid=(Hkv, nblk),
        in_specs=[qblk, prev, cur, prev, cur, gain, gain, perhead, perhead], out_specs=qblk,
        out_shape=jax.ShapeDtypeStruct((Hq, T, d), F32), compiler_params=_params(("parallel", "parallel")),
    )(q, k, k, v, v, qg, kg, sinks, slopes)


def _swa_bwd(q, k, v, qg, kg, sinks, slopes, do):
    Hq, T, d = q.shape
    Hkv = k.shape[0]
    G = Hq // Hkv
    W = WINDOW
    nblk = T // W
    scale = d ** -0.5

    def body(q_ref, kp_ref, kc_ref, vp_ref, vc_ref, qg_ref, kg_ref, sk_ref, sl_ref, do_ref,
             dq_ref, dk_ref, dv_ref, dqg_ref, dkg_ref, dsk_ref):
        hk = pl.program_id(0)
        n = pl.program_id(1)

        @pl.when((hk == 0) & (n == 0))
        def _():
            dqg_ref[...] = jnp.zeros_like(dqg_ref)
            dkg_ref[...] = jnp.zeros_like(dkg_ref)
            dsk_ref[...] = jnp.zeros_like(dsk_ref)

        @pl.when(n == 0)
        def _():
            dk_ref[...] = jnp.zeros_like(dk_ref)
            dv_ref[...] = jnp.zeros_like(dv_ref)

        kcat = jnp.concatenate([kp_ref[...], kc_ref[...]], axis=0)
        vcat = jnp.concatenate([vp_ref[...], vc_ref[...]], axis=0).astype(BF16)
        rk = lax.rsqrt(jnp.mean(kcat * kcat, axis=-1, keepdims=True) + EPS)
        kh = kcat * rk
        kn = (kh * kg_ref[...]).astype(BF16)
        row = _iota2((W, 2 * W), 0)
        col = _iota2((W, 2 * W), 1)
        dist = row + W - col
        valid = (dist >= 0) & (dist < W) & ((n > 0) | (col >= W))
        distf = dist.astype(F32)
        rowh = _iota2((Hq, LANE), 0)
        dkn = jnp.zeros((2 * W, d), F32)
        dvc = jnp.zeros((2 * W, d), F32)
        dqg = jnp.zeros((1, d), F32)
        dsk = jnp.zeros((Hq, LANE), F32)
        for g in range(G):
            qh = q_ref[g]
            rq = lax.rsqrt(jnp.mean(qh * qh, axis=-1, keepdims=True) + EPS)
            qhh = qh * rq
            qn = (qhh * qg_ref[...]).astype(BF16)
            sink = sk_ref[pl.ds(hk * G + g, 1), :][:, :1]
            slope = sl_ref[pl.ds(hk * G + g, 1), :][:, :1]
            s = _dot(qn, kn, NT) * scale - slope * distf
            s = jnp.where(valid, s, -1e30)
            m = jnp.maximum(jnp.max(s, axis=1, keepdims=True), sink)
            p = jnp.where(valid, jnp.exp(s - m), 0.0)
            esink = jnp.exp(sink - m)
            den = jnp.sum(p, axis=1, keepdims=True) + esink
            prob = p / den
            dov = do_ref[g].astype(BF16)
            dp = _dot(dov, vcat, NT)
            dd = jnp.sum(prob * dp, axis=1, keepdims=True)
            ds = prob * (dp - dd)
            dsink = -jnp.sum((esink / den) * dd, axis=0, keepdims=True)
            dsk = dsk + jnp.where(rowh == hk * G + g, dsink, 0.0)
            dsb = (ds * scale).astype(BF16)
            dqn = _dot(dsb, kn)
            dkn = dkn + _dot(dsb, qn, TN)
            dvc = dvc + _dot(prob.astype(BF16), dov, TN)
            dqh = dqn * qg_ref[...]
            dq_ref[g] = rq * (dqh - qhh * jnp.mean(dqh * qhh, axis=-1, keepdims=True))
            dqg = dqg + jnp.sum(dqn * qhh, axis=0, keepdims=True)
        dkh = dkn * kg_ref[...]
        dkraw = rk * (dkh - kh * jnp.mean(dkh * kh, axis=-1, keepdims=True))
        dqg_ref[...] += dqg
        dkg_ref[...] += jnp.sum(dkn * kh, axis=0, keepdims=True)
        dsk_ref[...] += dsk
        offp = pl.multiple_of(jnp.maximum(n - 1, 0) * W, W)
        offc = pl.multiple_of(n * W, W)
        dk_ref[pl.ds(offp, W), :] += dkraw[:W]
        dv_ref[pl.ds(offp, W), :] += dvc[:W]
        dk_ref[pl.ds(offc, W), :] += dkraw[W:]
        dv_ref[pl.ds(offc, W), :] += dvc[W:]

    qblk = pl.BlockSpec((G, W, d), lambda h, n: (h, n, 0))
    prev = pl.BlockSpec((None, W, d), lambda h, n: (h, jnp.maximum(n - 1, 0), 0))
    cur = pl.BlockSpec((None, W, d), lambda h, n: (h, n, 0))
    gain = pl.BlockSpec((1, d), lambda h, n: (0, 0))
    perhead = pl.BlockSpec((Hq, LANE), lambda h, n: (0, 0))
    full = pl.BlockSpec((None, T, d), lambda h, n: (h, 0, 0))
    kv = jax.ShapeDtypeStruct((Hkv, T, d), F32)
    gs = jax.ShapeDtypeStruct((1, d), F32)
    return _pcall(
        body, name="swa_bwd", grid=(Hkv, nblk),
        in_specs=[qblk, prev, cur, prev, cur, gain, gain, perhead, perhead, qblk],
        out_specs=[qblk, full, full, gain, gain, perhead],
        out_shape=[jax.ShapeDtypeStruct((Hq, T, d), F32), kv, kv, gs, gs, jax.ShapeDtypeStruct((Hq, LANE), F32)],
        compiler_params=_params(("arbitrary", "arbitrary")),
    )(q, k, k, v, v, qg, kg, sinks, slopes, do)


def _heads(z, n):
    T = z.shape[0]
    return z.reshape(T, n, HEAD_DIM).transpose(1, 0, 2)


def _unheads(z):
    n, T, d = z.shape
    return z.transpose(1, 0, 2).reshape(T, n * d)


def _alibi():
    s = [2.0 ** (-8.0 * (i + 1) / SWA_HEADS) for i in range(SWA_HEADS)]
    return jnp.broadcast_to(jnp.asarray(s, F32)[:, None], (SWA_HEADS, LANE))


def _att_fwd(h, g, w_in, w_out, q_gain, k_gain, sinks):
    hn = _rms_fwd(h, g, "att_norm")
    proj = _mm(hn, w_in, "nn", name="att_in")
    c = [0, SB_W, 2 * SB_W, 3 * SB_W, 3 * SB_W + SWA_QW, 3 * SB_W + SWA_QW + SWA_KVW, ATT_IN]
    sq, sk, sv = (_heads(proj[:, c[i]:c[i + 1]], SB_HEADS).astype(BF16) for i in range(3))
    bq = _heads(proj[:, c[3]:c[4]], SWA_HEADS)
    bk = _heads(proj[:, c[4]:c[5]], SWA_KV_HEADS)
    bv = _heads(proj[:, c[5]:c[6]], SWA_KV_HEADS)
    a_out, carry = _sb_fwd(sq, sk, sv)
    sk128 = jnp.broadcast_to(sinks.reshape(SWA_HEADS, 1), (SWA_HEADS, LANE))
    qg, kg = q_gain.reshape(1, HEAD_DIM), k_gain.reshape(1, HEAD_DIM)
    b_out = _swa_fwd(bq, bk, bv, qg, kg, sk128, _alibi())
    o = jnp.concatenate([_unheads(a_out), _unheads(b_out)], axis=-1).astype(BF16)
    h2 = _mm(o, w_out, "nn", res=h, name="att_out")
    return h2, (h, hn, sq, sk, sv, bq, bk, bv, carry, o, sk128, qg, kg)


def _att_bwd(dh2, saved, g, w_in, w_out):
    h, hn, sq, sk, sv, bq, bk, bv, carry, o, sk128, qg, kg = saved
    do = _mm(dh2, w_out, "nt", name="att_do")
    dw_out = _mm(o, dh2, "tn", out_dtype=BF16, name="att_dwout")
    da = _heads(do[:, :SB_W], SB_HEADS)
    db = _heads(do[:, SB_W:], SWA_HEADS)
    dsq, dsk, dsv = _sb_bwd(sq, sk, sv, carry, da)
    dbq, dbk, dbv, dqg, dkg, dsink = _swa_bwd(bq, bk, bv, qg, kg, sk128, _alibi(), db)
    dproj = jnp.concatenate([_unheads(z) for z in (dsq, dsk, dsv, dbq, dbk, dbv)], axis=-1).astype(BF16)
    dw_in = _mm(hn, dproj, "tn", out_dtype=BF16, name="att_dwin")
    dhn = _mm(dproj, w_in, "nt", name="att_dhn")
    dh, dg = _rms_bwd(dhn, h, g, dh2, "att_dnorm")
    return dh, dg, dw_in, dw_out, dqg.reshape(HEAD_DIM), dkg.reshape(HEAD_DIM), dsink[:, 0]


CONV_ROWS = 512
CONV_COLS = 512
HALO = 8


def _shifted(xcat, s, tm):
    if s == 0:
        return xcat[HALO:HALO + tm]
    return pltpu.roll(xcat, s, 0)[HALO:HALO + tm]


def _conv_pre(x_ref, halo_ref, w_ref, i, tm):
    xc = x_ref[...]
    halo = jnp.where(i > 0, halo_ref[...], 0.0)
    xcat = jnp.concatenate([halo, xc], axis=0)
    w = w_ref[...]
    y = w[GDN_CONV - 1:GDN_CONV] * xc
    for kk in range(GDN_CONV - 1):
        y = y + w[kk:kk + 1] * _shifted(xcat, GDN_CONV - 1 - kk, tm)
    return xcat, y


def _l2_heads(s, qscale_of):
    outs, rs = [], []
    for hh in range(s.shape[1] // GDN_HEAD_DIM):
        sh = s[:, hh * GDN_HEAD_DIM:(hh + 1) * GDN_HEAD_DIM]
        r = lax.rsqrt(jnp.sum(sh * sh, axis=-1, keepdims=True) + EPS)
        outs.append(sh * r)
        rs.append(r)
    return outs, rs


def _conv_specs(T, col0, tm, tc):
    cur = pl.BlockSpec((tm, tc), lambda j, i: (i, j + col0 // tc))
    halo = pl.BlockSpec((HALO, tc), lambda j, i: (jnp.maximum(i * (tm // HALO) - 1, 0), j + col0 // tc))
    wsp = pl.BlockSpec((GDN_CONV, tc), lambda j, i: (0, j + col0 // tc))
    out = pl.BlockSpec((tm, tc), lambda j, i: (i, j))
    return cur, halo, wsp, out


def _conv_fwd(proj, conv_w, col0, width, norm, name):
    T = proj.shape[0]
    tm, tc = _pick(T, CONV_ROWS), CONV_COLS
    cur, halo, wsp, out = _conv_specs(T, col0, tm, tc)
    n_q_tiles = (width // 2) // tc

    def body(x_ref, halo_ref, w_ref, o_ref):
        j, i = pl.program_id(0), pl.program_id(1)
        _, y = _conv_pre(x_ref, halo_ref, w_ref, i, tm)
        s = y * _sigmoid(y)
        if norm:
            outs, _ = _l2_heads(s, None)
            qs = jnp.where(j < n_q_tiles, GDN_HEAD_DIM ** -0.5, 1.0)
            o_ref[...] = jnp.concatenate(outs, axis=1) * qs
        else:
            o_ref[...] = s

    return _pcall(body, name=name, grid=(width // tc, T // tm), in_specs=[cur, halo, wsp], out_specs=out,
                  out_shape=jax.ShapeDtypeStruct((T, width), F32),
                  compiler_params=_params(("parallel", "parallel")))(proj, proj, conv_w)


def _conv_bwd_pre(proj, conv_w, dout, col0, width, norm, name):
    T = proj.shape[0]
    tm, tc = _pick(T, CONV_ROWS), CONV_COLS
    cur, halo, wsp, out = _conv_specs(T, col0, tm, tc)
    n_q_tiles = (width // 2) // tc

    def body(x_ref, halo_ref, w_ref, d_ref, dy_ref, dw_ref):
        j, i = pl.program_id(0), pl.program_id(1)
        xcat, y = _conv_pre(x_ref, halo_ref, w_ref, i, tm)
        sg = _sigmoid(y)
        s = y * sg
        d = d_ref[...]
        if norm:
            qs = jnp.where(j < n_q_tiles, GDN_HEAD_DIM ** -0.5, 1.0)
            d = d * qs
            outs, rs = _l2_heads(s, None)
            parts = []
            for hh, (nh, r) in enumerate(zip(outs, rs)):
                dh = d[:, hh * GDN_HEAD_DIM:(hh + 1) * GDN_HEAD_DIM]
                parts.append(r * (dh - nh * jnp.sum(dh * nh, axis=-1, keepdims=True)))
            ds = jnp.concatenate(parts, axis=1)
        else:
            ds = d
        dy = ds * sg * (1.0 + y * (1.0 - sg))
        dy_ref[...] = dy
        rows = [jnp.sum(dy * _shifted(xcat, GDN_CONV - 1 - kk, tm), axis=0, keepdims=True) for kk in range(GDN_CONV)]
        part = jnp.concatenate(rows, axis=0)

        @pl.when(i == 0)
        def _():
            dw_ref[...] = part

        @pl.when(i > 0)
        def _():
            dw_ref[...] += part

    wout = pl.BlockSpec((GDN_CONV, tc), lambda j, i: (0, j))
    return _pcall(body, name=name, grid=(width // tc, T // tm), in_specs=[cur, halo, wsp, out], out_specs=[out, wout],
                  out_shape=[jax.ShapeDtypeStruct((T, width), F32), jax.ShapeDtypeStruct((GDN_CONV, width), F32)],
                  compiler_params=_params(("parallel", "arbitrary")))(proj, proj, conv_w, dout)


def _conv_bwd_in(dy, conv_w, name):
    T, C = dy.shape
    tm, tc = _pick(T, CONV_ROWS), CONV_COLS
    nrow = T // tm

    def body(d_ref, nxt_ref, w_ref, dx_ref):
        i = pl.program_id(0)
        dc = d_ref[...]
        nxt = jnp.where(i < nrow - 1, nxt_ref[...], 0.0)
        dcat = jnp.concatenate([dc, nxt], axis=0)
        w = w_ref[...]
        dx = w[GDN_CONV - 1:GDN_CONV] * dc
        for kk in range(GDN_CONV - 1):
            s = GDN_CONV - 1 - kk
            dx = dx + w[kk:kk + 1] * pltpu.roll(dcat, tm + HALO - s, 0)[:tm]
        dx_ref[...] = dx.astype(BF16)

    cur = pl.BlockSpec((tm, tc), lambda i, j: (i, j))
    nxt = pl.BlockSpec((HALO, tc), lambda i, j: (jnp.minimum((i + 1) * (tm // HALO), T // HALO - 1), j))
    wsp = pl.BlockSpec((GDN_CONV, tc), lambda i, j: (0, j))
    return _pcall(body, name=name, grid=(nrow, C // tc), in_specs=[cur, nxt, wsp], out_specs=cur,
                  out_shape=jax.ShapeDtypeStruct((T, C), BF16),
                  compiler_params=_params(("parallel", "parallel")))(dy, dy, conv_w)


GATE_ROWS = 512


def _chunk_mask(n, lower):
    row = _iota2((n, n), 0)
    col = _iota2((n, n), 1)
    same = (row // GDN_CHUNK) == (col // GDN_CHUNK)
    tri = (row >= col) if lower else (row <= col)
    return (same & tri).astype(BF16)


def _gates_fwd(proj, a_log, dt_bias):
    T = proj.shape[0]
    tm = _pick(T, GATE_ROWS)
    c0 = (GDN_CONV_W + GDN_VW) // LANE

    def body(bl_ref, a_ref, alog_ref, dt_ref, beta_ref, g_ref, gc_ref):
        beta_ref[...] = _sigmoid(bl_ref[...])
        g = -jnp.exp(alog_ref[...]) * _softplus(a_ref[...] + dt_ref[...])
        g_ref[...] = g
        gc_ref[...] = _mdot2(_chunk_mask(tm, True), g)

    blk = lambda c: pl.BlockSpec((tm, LANE), lambda i: (i, c))
    vec = pl.BlockSpec((1, LANE), lambda i: (0, 0))
    sh = jax.ShapeDtypeStruct((T, LANE), F32)
    return _pcall(body, name="gdn_gates", grid=(T // tm,), in_specs=[blk(c0), blk(c0 + 1), vec, vec],
                  out_specs=[blk(0), blk(0), blk(0)], out_shape=[sh, sh, sh],
                  compiler_params=_params(("parallel",)))(proj, proj, a_log, dt_bias)


def _gates_bwd(proj, a_log, dt_bias, beta, g, dbeta, dgc):
    T = proj.shape[0]
    tm = _pick(T, GATE_ROWS)
    c0 = (GDN_CONV_W + GDN_VW) // LANE

    def body(a_ref, alog_ref, dt_ref, beta_ref, g_ref, dbeta_ref, dgc_ref, dbl_ref, da_ref, dalog_ref, ddt_ref):
        dg = _mdot2(_chunk_mask(tm, False), dgc_ref[...])
        b = beta_ref[...]
        dbl_ref[...] = (dbeta_ref[...] * b * (1.0 - b)).astype(BF16)
        da = dg * (-jnp.exp(alog_ref[...])) * _sigmoid(a_ref[...] + dt_ref[...])
        da_ref[...] = da.astype(BF16)
        p1 = jnp.sum(dg * g_ref[...], axis=0, keepdims=True)
        p2 = jnp.sum(da, axis=0, keepdims=True)

        @pl.when(pl.program_id(0) == 0)
        def _():
            dalog_ref[...] = p1
            ddt_ref[...] = p2

        @pl.when(pl.program_id(0) > 0)
        def _():
            dalog_ref[...] += p1
            ddt_ref[...] += p2

    blk = lambda c: pl.BlockSpec((tm, LANE), lambda i: (i, c))
    vec = pl.BlockSpec((1, LANE), lambda i: (0, 0))
    shb = jax.ShapeDtypeStruct((T, LANE), BF16)
    shv = jax.ShapeDtypeStruct((1, LANE), F32)
    return _pcall(body, name="gdn_dgates", grid=(T // tm,),
                  in_specs=[blk(c0 + 1), vec, vec, blk(0), blk(0), blk(0), blk(0)],
                  out_specs=[blk(0), blk(0), vec, vec], out_shape=[shb, shb, shv, shv],
                  compiler_params=_params(("arbitrary",)))(proj, a_log, dt_bias, beta, g, dbeta, dgc)


def _inv_unit_lower(Ls):
    C = Ls[0].shape[0]
    row = _iota2((C, C), 0)
    col = _iota2((C, C), 1)
    blk16 = (row // 16) == (col // 16)
    blk32 = (row // 32) == (col // 32)
    eye = (row == col).astype(F32)
    xs = [-jnp.where(blk16, L, 0.0) for L in Ls]
    inv = [eye + x for x in xs]
    for _ in range(3):
        xs = [_dot3(x, x) for x in xs]
        inv = [a + _dot3(a, x) for a, x in zip(inv, xs)]
    for mask in (blk32 & ~blk16, ~blk32):
        t = [_dot3(a, jnp.where(mask, L, 0.0)) for a, L in zip(inv, Ls)]
        inv = [a - _dot3(ti, a) for a, ti in zip(inv, t)]
    return inv


GDN_GROUP = 4
GDN_PREP_CHUNKS = 4


def _gdn_specs(T):
    C, D, E = GDN_CHUNK, GDN_HEAD_DIM, GDN_GROUP
    n = T // C
    qk = pl.BlockSpec((C, (E // 2) * D), lambda h, i: (i, h))
    vE = pl.BlockSpec((C, E * D), lambda h, i: (i, h))
    colv = pl.BlockSpec((E, C, 1), lambda h, i: (h, i, 0))
    rowv = pl.BlockSpec((E, None, 1, C), lambda h, i: (h, i, 0, 0))
    st = pl.BlockSpec((E, None, D, D), lambda h, i: (h, i, 0, 0))
    am = pl.BlockSpec((E, None, C, C), lambda h, i: (h, i, 0, 0))
    return n, qk, vE, colv, rowv, st, am


def _gdn_decay(gcol, grow):
    C = GDN_CHUNK
    row = _iota2((C, C), 0)
    col = _iota2((C, C), 1)
    incl = row >= col
    dm = jnp.where(incl, jnp.exp(jnp.where(incl, gcol - grow, 0.0)), 0.0)
    glast = grow[:, C - 1:C]
    return dm, jnp.exp(gcol), jnp.exp(glast), jnp.exp(glast - gcol), row > col, incl


def _gdn_prep(k, beta, gcol, grow):
    T = k.shape[0]
    C, D, B = GDN_CHUNK, GDN_HEAD_DIM, GDN_PREP_CHUNKS
    n = T // C

    def body(k_ref, b_ref, gc_ref, gr_ref, a_ref):
        idx = [(e, cb) for e in range(2) for cb in range(B)]
        kc = {cb: k_ref[cb * C:(cb + 1) * C, :] for cb in range(B)}
        lm = []
        for e, cb in idx:
            beta = b_ref[e, cb * C:(cb + 1) * C, :]
            dm, _, _, _, strict, _ = _gdn_decay(gc_ref[e, cb * C:(cb + 1) * C, :], gr_ref[e, cb])
            lm.append(jnp.where(strict, _bdot(kc[cb] * beta, kc[cb], NT) * dm, 0.0))
        inv = _inv_unit_lower(lm)
        for (e, cb), a in zip(idx, inv):
            a_ref[e, cb] = a

    return _pcall(
        body, name="gdn_prep", grid=(GDN_K_HEADS, n // B),
        in_specs=[pl.BlockSpec((B * C, D), lambda h, i: (i, h)), pl.BlockSpec((2, B * C, 1), lambda h, i: (h, i, 0)),
                  pl.BlockSpec((2, B * C, 1), lambda h, i: (h, i, 0)), pl.BlockSpec((2, B, 1, C), lambda h, i: (h, i, 0, 0))],
        out_specs=pl.BlockSpec((2, B, C, C), lambda h, i: (h, i, 0, 0)),
        out_shape=jax.ShapeDtypeStruct((GDN_V_HEADS, n, C, C), F32),
        compiler_params=_params(("parallel", "parallel")),
    )(k, beta, gcol, grow)


def _gdn_fwd(q, k, v, beta, gcol, grow, amat):
    T = q.shape[0]
    C, D, E = GDN_CHUNK, GDN_HEAD_DIM, GDN_GROUP
    n, qk, vE, colv, rowv, st, am = _gdn_specs(T)
    R = range(E)

    def body(q_ref, k_ref, v_ref, b_ref, gc_ref, gr_ref, a_ref, o_ref, s_ref, vn_ref, state):
        @pl.when(pl.program_id(1) == 0)
        def _():
            state[...] = jnp.zeros_like(state)

        qv = [q_ref[:, (e // 2) * D:(e // 2 + 1) * D] for e in R]
        kv = [k_ref[:, (e // 2) * D:(e // 2 + 1) * D] for e in R]
        vv = [v_ref[:, e * D:(e + 1) * D] for e in R]
        beta = [b_ref[e] for e in R]
        a = [a_ref[e] for e in R]
        s = [state[e] for e in R]
        dec = [_gdn_decay(gc_ref[e], gr_ref[e]) for e in R]
        pm = [_bdot(qv[e], kv[e], NT) * dec[e][0] for e in R]
        r = [beta[e] * (vv[e] - _bdot(kv[e] * dec[e][1], s[e])) for e in R]
        vn = [_dot3(a[e], r[e]) for e in R]
        o = [_bdot(qv[e] * dec[e][1], s[e]) + _bdot(pm[e], vn[e]) for e in R]
        s2 = [dec[e][2] * s[e] + _bdot(kv[e] * dec[e][3], vn[e], TN) for e in R]
        for e in R:
            s_ref[e] = s[e]
            vn_ref[:, e * D:(e + 1) * D] = vn[e]
            o_ref[:, e * D:(e + 1) * D] = o[e]
            state[e] = s2[e]

    shv = jax.ShapeDtypeStruct((T, GDN_V_HEADS * D), F32)
    return _pcall(
        body, name="gdn_fwd", grid=(GDN_V_HEADS // E, n), in_specs=[qk, qk, vE, colv, colv, rowv, am],
        out_specs=[vE, st, vE],
        out_shape=[shv, jax.ShapeDtypeStruct((GDN_V_HEADS, n, D, D), F32), shv],
        scratch_shapes=[pltpu.VMEM((E, D, D), F32)],
        compiler_params=_params(("parallel", "arbitrary")),
    )(q, k, v, beta, gcol, grow, amat)


def _gdn_bwd(q, k, v, beta, gcol, grow, states, amat, vnew, do):
    T = q.shape[0]
    C, D, E = GDN_CHUNK, GDN_HEAD_DIM, GDN_GROUP
    n, qk, vE, colv, rowv, st, am = _gdn_specs(T)
    rev = lambda spec: pl.BlockSpec(spec.block_shape, (lambda f: (lambda h, i: f(h, n - 1 - i)))(spec.index_map))
    qk, vE, colv, rowv, st, am = (rev(s) for s in (qk, vE, colv, rowv, st, am))
    R = range(E)

    def body(q_ref, k_ref, v_ref, b_ref, gc_ref, gr_ref, s_ref, a_ref, vn_ref, do_ref,
             dq_ref, dk_ref, dv_ref, db_ref, dgc_ref, dstate):
        @pl.when(pl.program_id(1) == 0)
        def _():
            dstate[...] = jnp.zeros_like(dstate)

        M = lambda f: [f(e) for e in R]
        rsum = lambda x: jnp.sum(x, axis=1, keepdims=True)
        qv = M(lambda e: q_ref[:, (e // 2) * D:(e // 2 + 1) * D])
        kv = M(lambda e: k_ref[:, (e // 2) * D:(e // 2 + 1) * D])
        vv = M(lambda e: v_ref[:, e * D:(e + 1) * D])
        vn = M(lambda e: vn_ref[:, e * D:(e + 1) * D])
        dov = M(lambda e: do_ref[:, e * D:(e + 1) * D])
        beta = M(lambda e: b_ref[e])
        s = M(lambda e: s_ref[e])
        a = M(lambda e: a_ref[e])
        dsn = M(lambda e: dstate[e])
        dec = M(lambda e: _gdn_decay(gc_ref[e], gr_ref[e]))
        dm, gam, glast, tail = (M(lambda e: dec[e][i]) for i in range(4))
        strict, incl = dec[0][4], dec[0][5]
        kb = M(lambda e: kv[e] * beta[e])
        kd = M(lambda e: kv[e] * gam[e])
        qd = M(lambda e: qv[e] * gam[e])
        kt = M(lambda e: kv[e] * tail[e])
        lmat = M(lambda e: jnp.where(strict, _bdot(kb[e], kv[e], NT) * dm[e], 0.0))
        pmat = M(lambda e: _bdot(qv[e], kv[e], NT) * dm[e])
        xres = M(lambda e: vv[e] - _bdot(kd[e], s[e]))
        dvn = M(lambda e: _bdot(pmat[e], dov[e], TN) + _bdot(kt[e], dsn[e]))
        dqd = M(lambda e: _bdot(dov[e], s[e], NT))
        dp = M(lambda e: jnp.where(incl, _bdot(dov[e], vn[e], NT), 0.0))
        dkt = M(lambda e: _bdot(vn[e], dsn[e], NT))
        dr = M(lambda e: _dot3(a[e], dvn[e], TN))
        drb = M(lambda e: beta[e] * dr[e])
        dkd = M(lambda e: -_bdot(drb[e], s[e], NT))
        ds2 = M(lambda e: _bdot(qd[e], dov[e], TN) + glast[e] * dsn[e] - _bdot(kd[e], drb[e], TN))
        dl = M(lambda e: -jnp.where(strict, _bdot(dr[e], vn[e], NT), 0.0))
        dmm = M(lambda e: dl[e] * dm[e])
        dnn = M(lambda e: dp[e] * dm[e])
        emat = M(lambda e: dl[e] * lmat[e] + dp[e] * pmat[e])
        dkb = M(lambda e: _bdot(dmm[e], kv[e]))
        dk = M(lambda e: beta[e] * dkb[e] + _bdot(dmm[e], kb[e], TN) + _bdot(dnn[e], qv[e], TN)
               + gam[e] * dkd[e] + tail[e] * dkt[e])
        dq = M(lambda e: _bdot(dnn[e], kv[e]) + gam[e] * dqd[e])
        dbeta = M(lambda e: rsum(dr[e] * xres[e]) + rsum(dkb[e] * kv[e]))
        ones = jnp.ones((C, LANE), BF16)
        colsum = M(lambda e: _dot2m(emat[e], ones, TN)[:, :1])
        tails = M(lambda e: rsum(dkt[e] * kt[e]))
        lastrow = _iota2((C, 1), 0) == C - 1
        dlast = M(lambda e: jnp.sum(tails[e], axis=0, keepdims=True)
                  + glast[e] * jnp.sum(rsum(s[e] * dsn[e]), axis=0, keepdims=True))
        dgc = M(lambda e: rsum(emat[e]) - colsum[e] + rsum(dkd[e] * kd[e]) + rsum(dqd[e] * qd[e]) - tails[e]
                + jnp.where(lastrow, dlast[e], 0.0))
        for e in R:
            dv_ref[:, e * D:(e + 1) * D] = drb[e]
            db_ref[e] = dbeta[e]
            dgc_ref[e] = dgc[e]
            dstate[e] = ds2[e]
        for kh in range(E // 2):
            dq_ref[:, kh * D:(kh + 1) * D] = dq[2 * kh] + dq[2 * kh + 1]
            dk_ref[:, kh * D:(kh + 1) * D] = dk[2 * kh] + dk[2 * kh + 1]

    shq = jax.ShapeDtypeStruct((T, GDN_K_HEADS * D), F32)
    shv = jax.ShapeDtypeStruct((T, GDN_V_HEADS * D), F32)
    shc = jax.ShapeDtypeStruct((GDN_V_HEADS, T, 1), F32)
    return _pcall(
        body, name="gdn_bwd", grid=(GDN_V_HEADS // E, n),
        in_specs=[qk, qk, vE, colv, colv, rowv, st, am, vE, vE],
        out_specs=[qk, qk, vE, colv, colv], out_shape=[shq, shq, shv, shc, shc],
        scratch_shapes=[pltpu.VMEM((E, D, D), F32)],
        compiler_params=_params(("parallel", "arbitrary")),
    )(q, k, v, beta, gcol, grow, states, amat, vnew, do)


def _outgate_fwd(o, proj, gain):
    T = o.shape[0]
    tm, tc = _pick(T, CONV_ROWS), CONV_COLS
    z0 = GDN_CONV_W // tc

    def body(o_ref, z_ref, g_ref, y_ref):
        z = z_ref[...]
        sz = z * _sigmoid(z)
        parts = []
        for hh in range(tc // GDN_HEAD_DIM):
            oh = o_ref[:, hh * GDN_HEAD_DIM:(hh + 1) * GDN_HEAD_DIM]
            r = lax.rsqrt(jnp.mean(oh * oh, axis=-1, keepdims=True) + EPS)
            parts.append(oh * r * g_ref[...])
        y_ref[...] = (jnp.concatenate(parts, axis=1) * sz).astype(BF16)

    blk = pl.BlockSpec((tm, tc), lambda i, j: (i, j))
    return _pcall(body, name="gdn_outgate", grid=(T // tm, GDN_VW // tc),
                  in_specs=[blk, pl.BlockSpec((tm, tc), lambda i, j: (i, j + z0)), pl.BlockSpec((1, GDN_HEAD_DIM), lambda i, j: (0, 0))],
                  out_specs=blk, out_shape=jax.ShapeDtypeStruct((T, GDN_VW), BF16),
                  compiler_params=_params(("parallel", "parallel")))(o, proj, gain)


def _outgate_bwd(dy, o, proj, gain):
    T = o.shape[0]
    tm, tc = _pick(T, CONV_ROWS), CONV_COLS
    z0 = GDN_CONV_W // tc
    nh = tc // GDN_HEAD_DIM

    def body(dy_ref, o_ref, z_ref, g_ref, do_ref, dz_ref, dg_ref):
        z = z_ref[...]
        sg = _sigmoid(z)
        sz = z * sg
        dy = dy_ref[...]
        dgain = jnp.zeros((1, GDN_HEAD_DIM), F32)
        dos, ys = [], []
        for hh in range(nh):
            sl = slice(hh * GDN_HEAD_DIM, (hh + 1) * GDN_HEAD_DIM)
            oh = o_ref[:, sl]
            r = lax.rsqrt(jnp.mean(oh * oh, axis=-1, keepdims=True) + EPS)
            xh = oh * r
            dn = dy[:, sl] * sz[:, sl]
            dgain = dgain + jnp.sum(dn * xh, axis=0, keepdims=True)
            dxh = dn * g_ref[...]
            dos.append(r * (dxh - xh * jnp.mean(dxh * xh, axis=-1, keepdims=True)))
            ys.append(xh * g_ref[...])
        do_ref[...] = jnp.concatenate(dos, axis=1)
        dz_ref[...] = (dy * jnp.concatenate(ys, axis=1) * sg * (1.0 + z * (1.0 - sg))).astype(BF16)
        first = (pl.program_id(0) == 0) & (pl.program_id(1) == 0)

        @pl.when(first)
        def _():
            dg_ref[...] = dgain

        @pl.when(jnp.logical_not(first))
        def _():
            dg_ref[...] += dgain

    blk = pl.BlockSpec((tm, tc), lambda i, j: (i, j))
    vec = pl.BlockSpec((1, GDN_HEAD_DIM), lambda i, j: (0, 0))
    return _pcall(body, name="gdn_doutgate", grid=(T // tm, GDN_VW // tc),
                  in_specs=[blk, blk, pl.BlockSpec((tm, tc), lambda i, j: (i, j + z0)), vec],
                  out_specs=[blk, blk, vec],
                  out_shape=[jax.ShapeDtypeStruct((T, GDN_VW), F32), jax.ShapeDtypeStruct((T, GDN_VW), BF16),
                             jax.ShapeDtypeStruct((1, GDN_HEAD_DIM), F32)],
                  compiler_params=_params(("arbitrary", "arbitrary")))(dy, o, proj, gain)


def _pad_lanes(vec):
    return jnp.pad(vec.reshape(1, -1), ((0, 0), (0, LANE - vec.shape[-1])))


def _head_cols(a):
    return a[:, :GDN_V_HEADS].T[:, :, None]


def _gdn_pad_in(w_in):
    c = GDN_CONV_W + GDN_VW
    z = jnp.zeros(w_in.shape[:-1] + (LANE - GDN_V_HEADS,), w_in.dtype)
    return jnp.concatenate([w_in[..., :c + GDN_V_HEADS], z, w_in[..., c + GDN_V_HEADS:], z], axis=-1)


def _gdn_unpad_in(dw):
    c = GDN_CONV_W + GDN_VW
    return jnp.concatenate([dw[..., :c + GDN_V_HEADS], dw[..., c + LANE:c + LANE + GDN_V_HEADS]], axis=-1)


def _gdn_mixer_fwd(h, g, w_in_pad, conv_w, a_log, dt_bias, out_gain, w_out):
    T = h.shape[0]
    hn = _rms_fwd(h, g, "gdn_norm")
    proj = _mm(hn, w_in_pad, "nn", name="gdn_in")
    qk = _conv_fwd(proj, conv_w, 0, 2 * GDN_KW, True, "gdn_conv_qk")
    vv = _conv_fwd(proj, conv_w, 2 * GDN_KW, GDN_VW, False, "gdn_conv_v")
    alog, dtb = _pad_lanes(a_log), _pad_lanes(dt_bias)
    beta, gl, gc = _gates_fwd(proj, alog, dtb)
    bcol, gcol = _head_cols(beta), _head_cols(gc)
    grow = gcol.reshape(GDN_V_HEADS, T // GDN_CHUNK, 1, GDN_CHUNK)
    qn, kn = qk[:, :GDN_KW], qk[:, GDN_KW:]
    amat = _gdn_prep(kn, bcol, gcol, grow)
    o, states, vnew = _gdn_fwd(qn, kn, vv, bcol, gcol, grow, amat)
    gain = out_gain.reshape(1, GDN_HEAD_DIM)
    y = _outgate_fwd(o, proj, gain)
    h2 = _mm(y, w_out, "nn", res=h, name="gdn_out")
    return h2, (h, hn, proj, qn, kn, vv, beta, gl, bcol, gcol, grow, o, states, amat, vnew, y, alog, dtb, gain)


def _gdn_mixer_bwd(dh2, saved, g, w_in_pad, conv_w, w_out):
    h, hn, proj, qn, kn, vv, beta, gl, bcol, gcol, grow, o, states, amat, vnew, y, alog, dtb, gain = saved
    T = h.shape[0]
    dy = _mm(dh2, w_out, "nt", name="gdn_dy")
    dw_out = _mm(y, dh2, "tn", out_dtype=BF16, name="gdn_dwout")
    do, dz, dgain = _outgate_bwd(dy, o, proj, gain)
    dq, dk, dv, dbcol, dgccol = _gdn_bwd(qn, kn, vv, bcol, gcol, grow, states, amat, vnew, do)
    dqk = jnp.concatenate([dq, dk], axis=1)
    dy_qk, dcw_qk = _conv_bwd_pre(proj, conv_w, dqk, 0, 2 * GDN_KW, True, "gdn_dconv_qk")
    dy_v, dcw_v = _conv_bwd_pre(proj, conv_w, dv, 2 * GDN_KW, GDN_VW, False, "gdn_dconv_v")
    dx_qk = _conv_bwd_in(dy_qk, conv_w[:, :2 * GDN_KW], "gdn_dconvin_qk")
    dx_v = _conv_bwd_in(dy_v, conv_w[:, 2 * GDN_KW:], "gdn_dconvin_v")
    lanes = lambda c: jnp.pad(c[:, :, 0].T, ((0, 0), (0, LANE - GDN_V_HEADS)))
    dbl, da, dalog, ddt = _gates_bwd(proj, alog, dtb, beta, gl, lanes(dbcol), lanes(dgccol))
    dproj = jnp.concatenate([dx_qk, dx_v, dz, dbl, da], axis=1)
    dw_in_pad = _mm(hn, dproj, "tn", out_dtype=BF16, name="gdn_dwin")
    dhn = _mm(dproj, w_in_pad, "nt", name="gdn_dhn")
    dh, dg = _rms_bwd(dhn, h, g, dh2, "gdn_dnorm")
    dconv = jnp.concatenate([dcw_qk, dcw_v], axis=1)
    return (dh, dg, _gdn_unpad_in(dw_in_pad), dconv, dalog[0, :GDN_V_HEADS], ddt[0, :GDN_V_HEADS],
            dgain.reshape(GDN_HEAD_DIM), dw_out)


def _local_step(x, p, target, w):
    h = x
    tape = []
    gdn_in_pad = _gdn_pad_in(w["gdn_w_in"][0])
    for i in range(2):
        h, s1 = _ffn_fwd(h, w["ffn_norm"][i, 0], w["ffn_w_gate"][i, 0], w["ffn_w_up"][i, 0], w["ffn_w_down"][i, 0], f"ffn{i}a")
        if i == 0:
            h, s2 = _att_fwd(h, w["mix_norm"][0], w["att_w_in"][0], w["att_w_out"][0], w["att_q_norm"][0],
                             w["att_k_norm"][0], w["att_sinks"][0])
        else:
            h, s2 = _gdn_mixer_fwd(h, w["mix_norm"][1], gdn_in_pad, w["gdn_conv_w"][0], w["gdn_a_log"][0],
                                   w["gdn_dt_bias"][0], w["gdn_out_norm"][0], w["gdn_w_out"][0])
        h, s3 = _ffn_fwd(h, w["ffn_norm"][i, 1], w["ffn_w_gate"][i, 1], w["ffn_w_up"][i, 1], w["ffn_w_down"][i, 1], f"ffn{i}b")
        h, s4 = _ple_fwd(h, p[i], w["ple_norm"][i], w["ple_w_gate"][i], w["ple_w_proj"][i], f"ple{i}")
        tape.append((s1, s2, s3, s4))

    loss, dh = _loss_head(h, target)

    g = {}
    ffn_norm = [[None, None], [None, None]]
    ffn_g = [[None, None], [None, None]]
    ffn_u = [[None, None], [None, None]]
    ffn_d = [[None, None], [None, None]]
    mix_norm, ple_norm, ple_g, ple_p = [None, None], [None, None], [None, None], [None, None]
    for i in (1, 0):
        s1, s2, s3, s4 = tape[i]
        dh, ple_norm[i], ple_g[i], ple_p[i] = _ple_bwd(dh, s4, p[i], w["ple_norm"][i], w["ple_w_gate"][i], f"ple{i}")
        dh, ffn_norm[i][1], ffn_g[i][1], ffn_u[i][1], ffn_d[i][1] = _ffn_bwd(
            dh, s3, w["ffn_norm"][i, 1], w["ffn_w_gate"][i, 1], w["ffn_w_up"][i, 1], w["ffn_w_down"][i, 1], f"ffn{i}b")
        if i == 0:
            dh, mix_norm[0], dwin, dwout, dqg, dkg, dsink = _att_bwd(dh, s2, w["mix_norm"][0], w["att_w_in"][0], w["att_w_out"][0])
            g["att_w_in"], g["att_w_out"] = dwin[None], dwout[None]
            g["att_q_norm"], g["att_k_norm"], g["att_sinks"] = dqg[None], dkg[None], dsink[None]
        else:
            dh, mix_norm[1], dwin, dconv, dalog, ddt, dgain, dwout = _gdn_mixer_bwd(
                dh, s2, w["mix_norm"][1], gdn_in_pad, w["gdn_conv_w"][0], w["gdn_w_out"][0])
            g["gdn_w_in"], g["gdn_conv_w"], g["gdn_w_out"] = dwin[None], dconv[None], dwout[None]
            g["gdn_a_log"], g["gdn_dt_bias"], g["gdn_out_norm"] = dalog[None], ddt[None], dgain[None]
        dh, ffn_norm[i][0], ffn_g[i][0], ffn_u[i][0], ffn_d[i][0] = _ffn_bwd(
            dh, s1, w["ffn_norm"][i, 0], w["ffn_w_gate"][i, 0], w["ffn_w_up"][i, 0], w["ffn_w_down"][i, 0], f"ffn{i}a")
    st2 = lambda rows: jnp.stack([jnp.stack(r) for r in rows])
    g["ffn_norm"], g["ffn_w_gate"], g["ffn_w_up"], g["ffn_w_down"] = st2(ffn_norm), st2(ffn_g), st2(ffn_u), st2(ffn_d)
    g["mix_norm"], g["ple_norm"] = jnp.stack(mix_norm), jnp.stack(ple_norm)
    g["ple_w_gate"], g["ple_w_proj"] = jnp.stack(ple_g), jnp.stack(ple_p)
    return loss, dh, g


MESH = pl.DeviceIdType.MESH
N_CHIP = 4


def _place():
    x, y, c = lax.axis_index("x"), lax.axis_index("y"), lax.axis_index("c")
    others = [((1 - x, y), 2 * (1 - x) + y), ((x, 1 - y), 2 * x + (1 - y)), ((1 - x, 1 - y), 2 * (1 - x) + (1 - y))]
    return x, y, c, 4 * x + 2 * y + c, 2 * x + y, (x, y, 1 - c), others


def _comm_call(body, arrays, out_shape, n_sems, name):
    hbm = pl.BlockSpec(memory_space=pl.ANY)
    n = len(arrays)
    return _pcall(
        body, name=name, in_specs=[hbm] * n, out_specs=[hbm] * len(out_shape), out_shape=out_shape,
        scratch_shapes=[pltpu.SemaphoreType.DMA((n, n_sems)), pltpu.SemaphoreType.DMA((n, n_sems)),
                        pltpu.SemaphoreType.DMA((n, N_CHIP))],
        compiler_params=pltpu.CompilerParams(has_side_effects=True),
    )(*arrays)


def _all_gather(arrays):
    n = len(arrays)

    def body(*refs):
        ins, outs = refs[:n], refs[n:2 * n]
        send_sems, recv_sems, local_sems = refs[2 * n:]
        x, y, c, me, my_chip, sibling, others = _place()

        def copy(a, k, block, to, src=None):
            dst = outs[a].at[block]
            return pltpu.make_async_remote_copy(
                src_ref=dst if src is None else src, dst_ref=dst, send_sem=send_sems.at[a, k],
                recv_sem=recv_sems.at[a, k], device_id=to, device_id_type=MESH)

        local = [pltpu.make_async_copy(ins[a], outs[a].at[me], local_sems.at[a, 0]) for a in range(n)]
        for cp in local:
            cp.start()
        first = []
        for a in range(n):
            first.append(copy(a, 0, me, sibling, src=ins[a]))
            first += [copy(a, 1 + j, me, (*chip, c), src=ins[a]) for j, (chip, _) in enumerate(others)]
        for cp in first:
            cp.start()
        passed = []
        for a in range(n):
            for j, (chip, chip_idx) in enumerate(others):
                blk = 2 * chip_idx + c
                copy(a, 1 + j, blk, (x, y, c)).wait_recv()
                fwd = copy(a, 4 + j, blk, sibling)
                fwd.start()
                passed.append(fwd)
        for a in range(n):
            copy(a, 0, 2 * my_chip + (1 - c), (x, y, c)).wait_recv()
            for j, (chip, chip_idx) in enumerate(others):
                copy(a, 4 + j, 2 * chip_idx + (1 - c), (x, y, c)).wait_recv()
        for cp in first + passed:
            cp.wait_send()
        for cp in local:
            cp.wait()

    out_shape = [jax.ShapeDtypeStruct((N_DEV,) + a.shape, a.dtype) for a in arrays]
    return _comm_call(body, arrays, out_shape, N_DEV - 1, "gather_weights")


def _exchange_sibling(arrays):
    n = len(arrays)

    def body(*refs):
        ins, own, got = refs[:n], refs[n:2 * n], refs[2 * n:3 * n]
        send_sems, recv_sems, local_sems = refs[3 * n:]
        x, y, c, me, my_chip, sibling, others = _place()
        local, remote = [], []
        for a in range(n):
            for chip in range(N_CHIP):
                cp = pltpu.make_async_copy(ins[a].at[2 * chip + c], own[a].at[chip], local_sems.at[a, chip])
                cp.start()
                local.append(cp)
                rc = pltpu.make_async_remote_copy(
                    src_ref=ins[a].at[2 * chip + (1 - c)], dst_ref=got[a].at[chip], send_sem=send_sems.at[a, chip],
                    recv_sem=recv_sems.at[a, chip], device_id=sibling, device_id_type=MESH)
                rc.start()
                remote.append(rc)
        for rc in remote:
            rc.wait()
        for cp in local:
            cp.wait()

    half = [jax.ShapeDtypeStruct((N_CHIP,) + a.shape[1:], a.dtype) for a in arrays]
    res = _comm_call(body, arrays, half + half, N_CHIP, "exchange_sibling")
    return res[:n], res[n:]


def _exchange_chips(arrays):
    n = len(arrays)

    def body(*refs):
        ins, outs = refs[:n], refs[n:2 * n]
        send_sems, recv_sems, local_sems = refs[2 * n:]
        x, y, c, me, my_chip, sibling, others = _place()
        local, remote = [], []
        for a in range(n):
            cp = pltpu.make_async_copy(ins[a].at[my_chip], outs[a].at[my_chip], local_sems.at[a, 0])
            cp.start()
            local.append(cp)
            for j, (chip, chip_idx) in enumerate(others):
                rc = pltpu.make_async_remote_copy(
                    src_ref=ins[a].at[chip_idx], dst_ref=outs[a].at[my_chip], send_sem=send_sems.at[a, j],
                    recv_sem=recv_sems.at[a, j], device_id=(*chip, c), device_id_type=MESH)
                rc.start()
                remote.append(rc)
        for rc in remote:
            rc.wait()
        for cp in local:
            cp.wait()

    out_shape = [jax.ShapeDtypeStruct(a.shape, a.dtype) for a in arrays]
    return _comm_call(body, arrays, out_shape, N_CHIP - 1, "exchange_chips")


def _as_rows(a, lead):
    shp = a.shape
    return a.reshape(shp[:lead] + (math.prod(shp[lead:-1]), shp[-1]))


def _row_tile(rows, cap=512):
    if rows <= cap:
        return rows
    for t in range(cap - cap % 8, 0, -8):
        if rows % t == 0:
            return t
    return rows


def _pair_sum(own, got, name):
    a2, b2 = _as_rows(own, 0), _as_rows(got, 0)
    rows, last = a2.shape
    tr = _row_tile(rows)

    def body(a_ref, b_ref, o_ref):
        o_ref[...] = (a_ref[...].astype(F32) + b_ref[...].astype(F32)).astype(o_ref.dtype)

    blk = pl.BlockSpec((tr, last), lambda i: (i, 0))
    out = _pcall(body, name=name, grid=(rows // tr,), in_specs=[blk, blk], out_specs=blk,
                 out_shape=jax.ShapeDtypeStruct(a2.shape, own.dtype), compiler_params=_params(("parallel",)))(a2, b2)
    return out.reshape(own.shape)


def _adamw(parts, w, m, v, name):
    p3 = _as_rows(parts, 1)
    w2, m2, v2 = (_as_rows(z, 0) for z in (w, m, v))
    rows, last = w2.shape
    tr = _row_tile(rows)
    c1 = 1.0 / (1.0 - ADAM_B1 ** ADAM_STEP)
    c2 = 1.0 / (1.0 - ADAM_B2 ** ADAM_STEP)

    def body(p_ref, w_ref, m_ref, v_ref, g_ref, d_ref, nm_ref, nv_ref):
        g = p_ref[0].astype(F32)
        for chip in range(1, N_CHIP):
            g = g + p_ref[chip].astype(F32)
        mn = ADAM_B1 * m_ref[...] + (1.0 - ADAM_B1) * g
        vn = ADAM_B2 * v_ref[...] + (1.0 - ADAM_B2) * (g * g)
        g_ref[...] = g
        nm_ref[...] = mn
        nv_ref[...] = vn
        d_ref[...] = -ADAM_LR * ((mn * c1) / (jnp.sqrt(vn * c2) + ADAM_EPS) + ADAM_WD * w_ref[...])

    row = pl.BlockSpec((tr, last), lambda i: (i, 0))
    sh = jax.ShapeDtypeStruct((rows, last), F32)
    outs = _pcall(body, name=name, grid=(rows // tr,),
                  in_specs=[pl.BlockSpec((N_CHIP, tr, last), lambda i: (0, i, 0)), row, row, row],
                  out_specs=[row, row, row, row], out_shape=[sh, sh, sh, sh],
                  compiler_params=_params(("parallel",)))(p3, w2, m2, v2)
    return [o.reshape(w.shape) for o in outs]


def _pack(pieces, row_align):
    rows, offs, r = [], [], 0
    for a in pieces:
        flat = a.reshape(-1)
        nr = -(-flat.shape[0] // PACK_W)
        flat = jnp.pad(flat, (0, nr * PACK_W - flat.shape[0]))
        rows.append(flat.reshape(nr, PACK_W))
        offs.append(r)
        r += nr
    pad = (-r) % row_align
    if pad:
        rows.append(jnp.zeros((pad, PACK_W), pieces[0].dtype))
    return jnp.concatenate(rows, axis=0), offs


def _unpack(flat, offs, shapes):
    out = []
    for off, shp in zip(offs, shapes):
        size = math.prod(shp)
        nr = -(-size // PACK_W)
        out.append(flat[..., off:off + nr, :].reshape(flat.shape[:-2] + (nr * PACK_W,))[..., :size].reshape(flat.shape[:-2] + tuple(shp)))
    return out


def _to_full(gathered, axis):
    z = jnp.moveaxis(gathered, 0, axis)
    shp = list(z.shape)
    return z.reshape(shp[:axis] + [shp[axis] * shp[axis + 1]] + shp[axis + 2:])


def _to_shards(full, axis):
    shp = list(full.shape)
    z = full.reshape(shp[:axis] + [N_DEV, shp[axis] // N_DEV] + shp[axis + 1:])
    return jnp.moveaxis(z, axis, 0)


def kernel(x, p, ffn_norm, ffn_w_gate, ffn_w_up, ffn_w_down, mix_norm, att_w_in, att_q_norm, att_k_norm, att_sinks, att_w_out, gdn_w_in, gdn_conv_w, gdn_a_log, gdn_dt_bias, gdn_out_norm, gdn_w_out, ple_norm, ple_w_gate, ple_w_proj, loss_target, m_ffn_norm, m_ffn_w_gate, m_ffn_w_up, m_ffn_w_down, m_mix_norm, m_att_w_in, m_att_q_norm, m_att_k_norm, m_att_sinks, m_att_w_out, m_gdn_w_in, m_gdn_conv_w, m_gdn_a_log, m_gdn_dt_bias, m_gdn_out_norm, m_gdn_w_out, m_ple_norm, m_ple_w_gate, m_ple_w_proj, v_ffn_norm, v_ffn_w_gate, v_ffn_w_up, v_ffn_w_down, v_mix_norm, v_att_w_in, v_att_q_norm, v_att_k_norm, v_att_sinks, v_att_w_out, v_gdn_w_in, v_gdn_conv_w, v_gdn_a_log, v_gdn_dt_bias, v_gdn_out_norm, v_gdn_w_out, v_ple_norm, v_ple_w_gate, v_ple_w_proj):
    args = dict(locals())
    wts = {n: args[n] for n in WEIGHTS}
    mom = {n: args["m_" + n] for n in WEIGHTS}
    var = {n: args["v_" + n] for n in WEIGHTS}
    axis = dict(SHARDED)
    vecs = [n for n, _ in SHARDED[:SMALL_SHARDED]]
    mats = [n for n, _ in SHARDED[SMALL_SHARDED:]]
    small = vecs + list(REPLICATED)
    small_shapes = [wts[n].shape for n in small]

    vec_pack, voffs = _pack([wts[n] for n in vecs], 8)
    gathered = _all_gather([wts[n].astype(BF16) for n in mats] + [vec_pack])
    full = {n: _to_full(g, axis[n]) for n, g in zip(mats, gathered)}
    for n, piece in zip(vecs, _unpack(gathered[-1], voffs, [wts[n].shape for n in vecs])):
        full[n] = _to_full(piece, axis[n])
    for n in REPLICATED:
        full[n] = wts[n]

    loss, grad_x, grads = _local_step(x[0], p[:, 0], loss_target[0], full)
    loss = lax.psum(loss, ("x", "y", "c"))

    vec_shards = [_to_shards(grads[n], axis[n]) for n in vecs]
    small_send = jnp.stack([_pack([s[d] for s in vec_shards] + [grads[n] for n in REPLICATED], 8)[0] for d in range(N_DEV)])
    send = [_to_shards(grads[n], axis[n]) for n in mats] + [small_send]
    own, got = _exchange_sibling(send)
    chip_sums = [_pair_sum(o, g, f"pair_sum_{i}") for i, (o, g) in enumerate(zip(own, got))]
    parts = _exchange_chips(chip_sums)

    outs = {}
    for n, part in zip(mats, parts):
        outs[n] = _adamw(part, wts[n], mom[n], var[n], f"adamw_{n}")
    small_w, soffs = _pack([wts[n] for n in small], 8)
    small_m, _ = _pack([mom[n] for n in small], 8)
    small_v, _ = _pack([var[n] for n in small], 8)
    small_out = [_unpack(z, soffs, small_shapes) for z in _adamw(parts[-1], small_w, small_m, small_v, "adamw_small")]
    for i, n in enumerate(small):
        outs[n] = [small_out[k][i] for k in range(4)]
    result = [loss, grad_x[None]]
    for k in range(4):
        result += [outs[n][k] for n in WEIGHTS]
    return tuple(result)
```

```python
import math

import jax
import jax.numpy as jnp
from jax import lax
from jax.experimental import pallas as pl
from jax.experimental.pallas import tpu as pltpu

F32 = jnp.float32
BF16 = jnp.bfloat16

N_DEV = 8
D_MODEL = 1024
D_FF = 2816
PLE_DIM = 256
HEAD_DIM = 64
SB_HEADS = 8
SWA_HEADS = 8
SWA_KV_HEADS = 2
SWA_GROUP = SWA_HEADS // SWA_KV_HEADS
WINDOW = 128
Q_BLOCK = 128
GDN_K_HEADS = 8
GDN_V_HEADS = 16
GDN_HEAD_DIM = 128
GDN_CONV = 4
GDN_CHUNK = 64
EPS = 1e-6
SB_W = SB_HEADS * HEAD_DIM
SWA_QW = SWA_HEADS * HEAD_DIM
SWA_KVW = SWA_KV_HEADS * HEAD_DIM
ATT_IN = 3 * SB_W + SWA_QW + 2 * SWA_KVW
GDN_KW = GDN_K_HEADS * GDN_HEAD_DIM
GDN_VW = GDN_V_HEADS * GDN_HEAD_DIM
GDN_CONV_W = 2 * GDN_KW + GDN_VW
GDN_IN = GDN_CONV_W + GDN_VW + 2 * GDN_V_HEADS
GDN_IN_PAD = GDN_CONV_W + GDN_VW + 2 * 128

ADAM_LR = 0.001
ADAM_B1 = 0.9
ADAM_B2 = 0.999
ADAM_EPS = 1e-08
ADAM_WD = 0.01
ADAM_STEP = 10

LANE = 128
VMEM_LIMIT = 56 * 1024 * 1024
PACK_W = 1024

NN = ((1,), (0,))
NT = ((1,), (1,))
TN = ((0,), (0,))

SHARDED = (
    ("ffn_norm", 2), ("gdn_conv_w", 2),
    ("ffn_w_gate", 3), ("ffn_w_up", 3), ("ffn_w_down", 2), ("att_w_in", 2), ("att_w_out", 1),
    ("gdn_w_in", 2), ("gdn_w_out", 1), ("ple_w_gate", 1), ("ple_w_proj", 2),
)
SMALL_SHARDED = 2
REPLICATED = ("mix_norm", "att_q_norm", "att_k_norm", "att_sinks", "gdn_a_log", "gdn_dt_bias",
              "gdn_out_norm", "ple_norm")
WEIGHTS = ("ffn_norm", "ffn_w_gate", "ffn_w_up", "ffn_w_down", "mix_norm", "att_w_in", "att_q_norm",
           "att_k_norm", "att_sinks", "att_w_out", "gdn_w_in", "gdn_conv_w", "gdn_a_log", "gdn_dt_bias",
           "gdn_out_norm", "gdn_w_out", "ple_norm", "ple_w_gate", "ple_w_proj")


def _pcall(body, **kw):
    return pl.pallas_call(body, **kw)


def _params(sem=None):
    if sem is None:
        return pltpu.CompilerParams(vmem_limit_bytes=VMEM_LIMIT)
    return pltpu.CompilerParams(dimension_semantics=sem, vmem_limit_bytes=VMEM_LIMIT)


def _dot(a, b, dims=NN):
    return lax.dot_general(a, b, (dims, ((), ())), preferred_element_type=F32)


def _bdot(a, b, dims=NN):
    return _dot(a.astype(BF16), b.astype(BF16), dims)


def _split(a):
    hi = a.astype(BF16)
    lo = (a - hi.astype(F32)).astype(BF16)
    return hi, lo


def _dot3(a, b, dims=NN):
    ah, al = _split(a)
    bh, bl = _split(b)
    return _dot(ah, bh, dims) + (_dot(ah, bl, dims) + _dot(al, bh, dims))


def _dot2m(a, m, dims=NN):
    ah, al = _split(a)
    return _dot(ah, m, dims) + _dot(al, m, dims)


def _mdot2(m, a, dims=NN):
    ah, al = _split(a)
    return _dot(m, ah, dims) + _dot(m, al, dims)


def _sigmoid(x):
    return 1.0 / (1.0 + jnp.exp(-x))


def _softplus(x):
    return jnp.maximum(x, 0.0) + jnp.log(1.0 + jnp.exp(-jnp.abs(x)))


def _pick(n, cap):
    if n <= cap:
        return n
    for t in range(cap - cap % LANE, 0, -LANE):
        if n % t == 0:
            return t
    raise ValueError(f"no tile for {n} under {cap}")


def _iota2(shape, axis):
    return lax.broadcasted_iota(jnp.int32, shape, axis)


def _mm(a, b, mode, out_dtype=F32, res=None, alpha=1.0, a2=None, b2=None, name="mm"):
    if mode == "nn":
        (M, K), N = a.shape, b.shape[1]
    elif mode == "nt":
        (M, K), N = a.shape, b.shape[0]
    else:
        (K, M), N = a.shape, b.shape[1]
    tm, tn, tk = _pick(M, 1408 if mode == "tn" else 512), _pick(N, 1408), _pick(K, 1024 if mode == "tn" else 1408)
    nk = K // tk
    dims = {"nn": NN, "nt": NT, "tn": TN}[mode]
    a_spec = pl.BlockSpec((tk, tm), lambda i, j, k: (k, i)) if mode == "tn" else pl.BlockSpec((tm, tk), lambda i, j, k: (i, k))
    b_spec = pl.BlockSpec((tn, tk), lambda i, j, k: (j, k)) if mode == "nt" else pl.BlockSpec((tk, tn), lambda i, j, k: (k, j))
    o_spec = pl.BlockSpec((tm, tn), lambda i, j, k: (i, j))
    two = a2 is not None
    has_res = res is not None

    def body(*refs):
        refs = list(refs)
        a_ref, b_ref = refs[0], refs[1]
        pos = 2
        if two:
            a2_ref, b2_ref = refs[2], refs[3]
            pos = 4
        if has_res:
            res_ref = refs[pos]
            pos += 1
        o_ref, acc_ref = refs[pos], refs[pos + 1]
        k = pl.program_id(2)
        part = _bdot(a_ref[...], b_ref[...], dims)
        if two:
            part = part + _bdot(a2_ref[...], b2_ref[...], dims)

        def finish(acc):
            out = acc * alpha if alpha != 1.0 else acc
            if has_res:
                out = res_ref[...] + out
            o_ref[...] = out.astype(out_dtype)

        if nk == 1:
            finish(part)
        else:
            @pl.when(k == 0)
            def _():
                acc_ref[...] = part

            @pl.when(k > 0)
            def _():
                acc_ref[...] += part

            @pl.when(k == nk - 1)
            def _():
                finish(acc_ref[...])

    ins = [a, b]
    specs = [a_spec, b_spec]
    if two:
        ins += [a2, b2]
        specs += [a_spec, b_spec]
    if has_res:
        ins.append(res)
        specs.append(o_spec)
    return _pcall(
        body, name=name, grid=(M // tm, N // tn, nk), in_specs=specs, out_specs=o_spec,
        out_shape=jax.ShapeDtypeStruct((M, N), out_dtype),
        scratch_shapes=[pltpu.VMEM((tm, tn) if nk > 1 else (8, LANE), F32)],
        compiler_params=_params(("parallel", "parallel", "arbitrary")),
    )(*ins)


ROW_TILE = 256


def _rms_fwd(h, g, name):
    T, D = h.shape
    tr = _pick(T, ROW_TILE)

    def body(h_ref, g_ref, n_ref):
        x = h_ref[...]
        r = lax.rsqrt(jnp.mean(x * x, axis=-1, keepdims=True) + EPS)
        n_ref[...] = (x * r * g_ref[...]).astype(BF16)

    return _pcall(
        body, name=name, grid=(T // tr,),
        in_specs=[pl.BlockSpec((tr, D), lambda i: (i, 0)), pl.BlockSpec((1, D), lambda i: (0, 0))],
        out_specs=pl.BlockSpec((tr, D), lambda i: (i, 0)),
        out_shape=jax.ShapeDtypeStruct((T, D), BF16), compiler_params=_params(("parallel",)),
    )(h, g.reshape(1, D))


def _rms_bwd(dn, h, g, dres, name):
    T, D = h.shape
    tr = _pick(T, ROW_TILE)

    def body(dn_ref, h_ref, g_ref, dres_ref, dh_ref, dg_ref):
        x = h_ref[...]
        r = lax.rsqrt(jnp.mean(x * x, axis=-1, keepdims=True) + EPS)
        xh = x * r
        d = dn_ref[...].astype(F32)
        dxh = d * g_ref[...]
        dh_ref[...] = dres_ref[...] + r * (dxh - xh * jnp.mean(dxh * xh, axis=-1, keepdims=True))
        part = jnp.sum(d * xh, axis=0, keepdims=True)

        @pl.when(pl.program_id(0) == 0)
        def _():
            dg_ref[...] = part

        @pl.when(pl.program_id(0) > 0)
        def _():
            dg_ref[...] += part

    row = pl.BlockSpec((tr, D), lambda i: (i, 0))
    vec = pl.BlockSpec((1, D), lambda i: (0, 0))
    dh, dg = _pcall(
        body, name=name, grid=(T // tr,), in_specs=[row, row, vec, row], out_specs=[row, vec],
        out_shape=[jax.ShapeDtypeStruct((T, D), F32), jax.ShapeDtypeStruct((1, D), F32)],
        compiler_params=_params(("arbitrary",)),
    )(dn, h, g.reshape(1, D), dres)
    return dh, dg.reshape(D)


def _gateup(n, wg, wu, name):
    T, D = n.shape
    F = wg.shape[1]
    tm, tn = _pick(T, 512), _pick(F, 1408)

    def body(n_ref, wg_ref, wu_ref, a_ref, b_ref, hid_ref):
        x = n_ref[...]
        a = _dot(x, wg_ref[...])
        b = _dot(x, wu_ref[...])
        a_ref[...] = a.astype(BF16)
        b_ref[...] = b.astype(BF16)
        hid_ref[...] = (a * _sigmoid(a) * b).astype(BF16)

    o_spec = pl.BlockSpec((tm, tn), lambda i, j: (i, j))
    w_spec = pl.BlockSpec((D, tn), lambda i, j: (0, j))
    sh = jax.ShapeDtypeStruct((T, F), BF16)
    return _pcall(
        body, name=name, grid=(T // tm, F // tn),
        in_specs=[pl.BlockSpec((tm, D), lambda i, j: (i, 0)), w_spec, w_spec],
        out_specs=[o_spec, o_spec, o_spec], out_shape=[sh, sh, sh],
        compiler_params=_params(("parallel", "parallel")),
    )(n, wg, wu)


def _ffn_dhid(dy, wd, a, b, name):
    T, D = dy.shape
    F = wd.shape[0]
    tm, tn = _pick(T, 512), _pick(F, 1408)

    def body(dy_ref, wd_ref, a_ref, b_ref, da_ref, db_ref):
        dhid = 0.5 * _bdot(dy_ref[...], wd_ref[...], NT)
        av = a_ref[...].astype(F32)
        bv = b_ref[...].astype(F32)
        s = _sigmoid(av)
        da_ref[...] = (dhid * bv * s * (1.0 + av * (1.0 - s))).astype(BF16)
        db_ref[...] = (dhid * av * s).astype(BF16)

    o_spec = pl.BlockSpec((tm, tn), lambda i, j: (i, j))
    sh = jax.ShapeDtypeStruct((T, F), BF16)
    return _pcall(
        body, name=name, grid=(T // tm, F // tn),
        in_specs=[pl.BlockSpec((tm, D), lambda i, j: (i, 0)), pl.BlockSpec((tn, D), lambda i, j: (j, 0)), o_spec, o_spec],
        out_specs=[o_spec, o_spec], out_shape=[sh, sh],
        compiler_params=_params(("parallel", "parallel")),
    )(dy, wd, a, b)


def _ffn_fwd(h, g, wg, wu, wd, tag):
    n = _rms_fwd(h, g, f"{tag}_norm")
    a, b, hid = _gateup(n, wg, wu, f"{tag}_gateup")
    h2 = _mm(hid, wd, "nn", res=h, alpha=0.5, name=f"{tag}_down")
    return h2, (h, n, a, b, hid)


def _ffn_bwd(dh2, saved, g, wg, wu, wd, tag):
    h, n, a, b, hid = saved
    da, db = _ffn_dhid(dh2, wd, a, b, f"{tag}_dhid")
    dwd = _mm(hid, dh2, "tn", alpha=0.5, out_dtype=BF16, name=f"{tag}_dwd")
    dwg = _mm(n, da, "tn", out_dtype=BF16, name=f"{tag}_dwg")
    dwu = _mm(n, db, "tn", out_dtype=BF16, name=f"{tag}_dwu")
    dn = _mm(da, wg, "nt", a2=db, b2=wu, name=f"{tag}_dn")
    dh, dg = _rms_bwd(dn, h, g, dh2, f"{tag}_dnorm")
    return dh, dg, dwg, dwu, dwd


def _ple_fwd(h, p, g, w_gate, w_proj, tag):
    T, D = h.shape
    pn = _rms_fwd(h, g, f"{tag}_norm")
    tm, tn = _pick(T, 512), _pick(D, 1024)
    P = p.shape[1]

    def body(pn_ref, p_ref, wg_ref, wp_ref, h_ref, o_ref, gl_ref, pp_ref):
        gl = _dot(pn_ref[...], wg_ref[...])
        pp = _bdot(p_ref[...], wp_ref[...])
        gl_ref[...] = gl
        pp_ref[...] = pp
        o_ref[...] = h_ref[...] + _sigmoid(gl) * pp

    o_spec = pl.BlockSpec((tm, tn), lambda i, j: (i, j))
    sh = jax.ShapeDtypeStruct((T, D), F32)
    h2, gl, pp = _pcall(
        body, name=f"{tag}_fwd", grid=(T // tm, D // tn),
        in_specs=[pl.BlockSpec((tm, D), lambda i, j: (i, 0)), pl.BlockSpec((tm, P), lambda i, j: (i, 0)),
                  pl.BlockSpec((D, tn), lambda i, j: (0, j)), pl.BlockSpec((P, tn), lambda i, j: (0, j)), o_spec],
        out_specs=[o_spec, o_spec, o_spec], out_shape=[sh, sh, sh],
        compiler_params=_params(("parallel", "parallel")),
    )(pn, p, w_gate, w_proj, h)
    return h2, (h, pn, gl, pp)


def _ple_bwd(dh2, saved, p, g, w_gate, tag):
    h, pn, gl, pp = saved
    T, D = h.shape
    tr = _pick(T, ROW_TILE)

    def body(d_ref, gl_ref, pp_ref, dgl_ref, dpp_ref):
        d = d_ref[...]
        s = _sigmoid(gl_ref[...])
        dpp_ref[...] = (d * s).astype(BF16)
        dgl_ref[...] = (d * pp_ref[...] * s * (1.0 - s)).astype(BF16)

    row = pl.BlockSpec((tr, D), lambda i: (i, 0))
    sh = jax.ShapeDtypeStruct((T, D), BF16)
    dgl, dpp = _pcall(body, name=f"{tag}_dgate", grid=(T // tr,), in_specs=[row, row, row], out_specs=[row, row],
                      out_shape=[sh, sh], compiler_params=_params(("parallel",)))(dh2, gl, pp)
    dw_proj = _mm(p, dpp, "tn", out_dtype=BF16, name=f"{tag}_dwproj")
    dw_gate = _mm(pn, dgl, "tn", out_dtype=BF16, name=f"{tag}_dwgate")
    dpn = _mm(dgl, w_gate, "nt", name=f"{tag}_dpn")
    dh, dg = _rms_bwd(dpn, h, g, dh2, f"{tag}_dnorm")
    return dh, dg, dw_gate, dw_proj


def _loss_head(y, target):
    T, D = y.shape
    tr = _pick(T, ROW_TILE)

    def body(y_ref, t_ref, dy_ref, l_ref):
        e = y_ref[...] - t_ref[...]
        dy_ref[...] = e * (1.0 / D)
        part = jnp.sum(e * e, axis=0, keepdims=True)

        @pl.when(pl.program_id(0) == 0)
        def _():
            l_ref[...] = part

        @pl.when(pl.program_id(0) > 0)
        def _():
            l_ref[...] += part

    row = pl.BlockSpec((tr, D), lambda i: (i, 0))
    vec = pl.BlockSpec((1, D), lambda i: (0, 0))
    dy, l = _pcall(body, name="loss_head", grid=(T // tr,), in_specs=[row, row], out_specs=[row, vec],
                   out_shape=[jax.ShapeDtypeStruct((T, D), F32), jax.ShapeDtypeStruct((1, D), F32)],
                   compiler_params=_params(("arbitrary",)))(y, target)
    return (0.5 / D) * jnp.sum(l), dy


SB_GROUP_FWD = 8
SB_GROUP_BWD = 4


def _sb_consts():
    row = _iota2((Q_BLOCK, Q_BLOCK), 0)
    col = _iota2((Q_BLOCK, Q_BLOCK), 1)
    after = (row > col).astype(BF16)
    before = (row < col).astype(BF16)
    return col < row, after, before, col


def _sb_fwd(q, k, v):
    H, T, d = q.shape
    nblk = T // Q_BLOCK
    scale = d ** -0.5
    G = SB_GROUP_FWD

    def body(q_ref, k_ref, v_ref, o_ref, c_ref, run_ref):
        i = pl.program_id(1)
        causal, after, _, col = _sb_consts()
        qs = [q_ref[g] * scale for g in range(G)]
        o_ref[...] = jnp.zeros_like(o_ref)
        c_ref[...] = jnp.zeros_like(c_ref)
        run_ref[...] = jnp.zeros_like(run_ref)

        def pair(j, diag):
            off = pl.multiple_of(j * Q_BLOCK, Q_BLOCK)
            R = range(G)
            kj = [k_ref[g, pl.ds(off, Q_BLOCK), :] for g in R]
            vj = [v_ref[g, pl.ds(off, Q_BLOCK), :] for g in R]
            c = [run_ref[g] for g in R]
            acc = [o_ref[g] for g in R]
            cm = None if diag else [c_ref[g] for g in R]
            z = [_dot(qs[g], kj[g], NT) for g in R]
            sp = [_softplus(z[g]) for g in R]
            lk = [jnp.where(causal, -sp[g], 0.0) if diag else -sp[g] for g in R]
            btw = [_dot2m(lk[g], after) for g in R]
            e = [jnp.exp((z[g] - sp[g]) + btw[g] + c[g]) for g in R]
            w = [jnp.where(causal, e[g], 0.0) if diag else e[g] for g in R]
            pv = [_bdot(w[g], vj[g]) for g in R]
            rs = [jnp.sum(lk[g], axis=1, keepdims=True) for g in R]
            for g in R:
                o_ref[g] = acc[g] + pv[g]
                if not diag:
                    c_ref[g] = jnp.where(col == j, c[g], cm[g])
                run_ref[g] = c[g] + rs[g]

        pair(i, True)

        @pl.loop(0, i)
        def _(jj):
            pair(i - 1 - jj, False)

    blk = pl.BlockSpec((G, Q_BLOCK, d), lambda h, i: (h, i, 0))
    full = pl.BlockSpec((G, T, d), lambda h, i: (h, 0, 0))
    return _pcall(
        body, name="sb_fwd", grid=(H // G, nblk), in_specs=[blk, full, full],
        out_specs=[blk, pl.BlockSpec((G, Q_BLOCK, LANE), lambda h, i: (h, i, 0))],
        out_shape=[jax.ShapeDtypeStruct((H, T, d), F32), jax.ShapeDtypeStruct((H, T, LANE), F32)],
        scratch_shapes=[pltpu.VMEM((G, Q_BLOCK, 1), F32)],
        compiler_params=_params(("parallel", "parallel")),
    )(q, k, v)


def _sb_bwd(q, k, v, carry, do):
    H, T, d = q.shape
    nblk = T // Q_BLOCK
    scale = d ** -0.5
    G = SB_GROUP_BWD

    def body(q_ref, k_ref, v_ref, c_ref, do_ref, dq_ref, dk_ref, dv_ref, run_ref):
        i = pl.program_id(1)

        @pl.when(i == 0)
        def _():
            dk_ref[...] = jnp.zeros_like(dk_ref)
            dv_ref[...] = jnp.zeros_like(dv_ref)

        causal, after, before, col = _sb_consts()
        qs = [q_ref[g] * scale for g in range(G)]
        dov = [do_ref[g].astype(BF16) for g in range(G)]
        dq_ref[...] = jnp.zeros_like(dq_ref)
        run_ref[...] = jnp.zeros_like(run_ref)

        def pair(j, diag):
            off = pl.multiple_of(j * Q_BLOCK, Q_BLOCK)
            R = range(G)
            rows = pl.ds(off, Q_BLOCK)
            kj = [k_ref[g, rows, :] for g in R]
            vj = [v_ref[g, rows, :] for g in R]
            gsum = [run_ref[g] for g in R]
            dq0 = [dq_ref[g] for g in R]
            dk0 = [dk_ref[g, rows, :] for g in R]
            dv0 = [dv_ref[g, rows, :] for g in R]
            cm = None if diag else [c_ref[g] for g in R]
            z = [_dot(qs[g], kj[g], NT) for g in R]
            sp = [_softplus(z[g]) for g in R]
            lk = [jnp.where(causal, -sp[g], 0.0) if diag else -sp[g] for g in R]
            ls = [z[g] - sp[g] for g in R]
            logw = [ls[g] + _dot2m(lk[g], after) for g in R]
            if not diag:
                logw = [logw[g] + jnp.sum(jnp.where(col == j, cm[g], 0.0), axis=1, keepdims=True) for g in R]
            e = [jnp.exp(logw[g]) for g in R]
            w = [jnp.where(causal, e[g], 0.0) if diag else e[g] for g in R]
            gw = [_dot(dov[g], vj[g], NT) * w[g] for g in R]
            gpre = [gsum[g] + _dot2m(gw[g], before) for g in R]
            sig = [jnp.exp(ls[g]) for g in R]
            dz = [gw[g] * (1.0 - sig[g]) - sig[g] * gpre[g] for g in R]
            if diag:
                dz = [jnp.where(causal, dz[g], 0.0) for g in R]
            dzb = [dz[g].astype(BF16) for g in R]
            dq1 = [_dot(dzb[g], kj[g]) for g in R]
            dk1 = [_dot(dzb[g], qs[g], TN) for g in R]
            dv1 = [_dot(w[g].astype(BF16), dov[g], TN) for g in R]
            gs1 = [jnp.sum(gw[g], axis=1, keepdims=True) for g in R]
            for g in R:
                dq_ref[g] = dq0[g] + dq1[g]
                dk_ref[g, rows, :] = dk0[g] + dk1[g]
                dv_ref[g, rows, :] = dv0[g] + dv1[g]
                run_ref[g] = gsum[g] + gs1[g]

        @pl.loop(0, i)
        def _(j):
            pair(j, False)

        pair(i, True)
        dq_ref[...] = dq_ref[...] * scale

    blk = pl.BlockSpec((G, Q_BLOCK, d), lambda h, i: (h, i, 0))
    full = pl.BlockSpec((G, T, d), lambda h, i: (h, 0, 0), pipeline_mode=pl.Buffered(1))
    sh = jax.ShapeDtypeStruct((H, T, d), F32)
    return _pcall(
        body, name="sb_bwd", grid=(H // G, nblk),
        in_specs=[blk, full, full, pl.BlockSpec((G, Q_BLOCK, LANE), lambda h, i: (h, i, 0)), blk],
        out_specs=[blk, full, full], out_shape=[sh, sh, sh],
        scratch_shapes=[pltpu.VMEM((G, Q_BLOCK, 1), F32)],
        compiler_params=_params(("parallel", "arbitrary")),
    )(q, k, v, carry, do)


def _swa_fwd(q, k, v, qg, kg, sinks, slopes):
    Hq, T, d = q.shape
    Hkv = k.shape[0]
    G = Hq // Hkv
    W = WINDOW
    nblk = T // W
    scale = d ** -0.5

    def body(q_ref, kp_ref, kc_ref, vp_ref, vc_ref, qg_ref, kg_ref, sk_ref, sl_ref, o_ref):
        hk = pl.program_id(0)
        n = pl.program_id(1)
        kcat = jnp.concatenate([kp_ref[...], kc_ref[...]], axis=0)
        vcat = jnp.concatenate([vp_ref[...], vc_ref[...]], axis=0).astype(BF16)
        rk = lax.rsqrt(jnp.mean(kcat * kcat, axis=-1, keepdims=True) + EPS)
        kn = (kcat * rk * kg_ref[...]).astype(BF16)
        row = _iota2((W, 2 * W), 0)
        col = _iota2((W, 2 * W), 1)
        dist = row + W - col
        valid = (dist >= 0) & (dist < W) & ((n > 0) | (col >= W))
        distf = dist.astype(F32)
        for g in range(G):
            qh = q_ref[g]
            rq = lax.rsqrt(jnp.mean(qh * qh, axis=-1, keepdims=True) + EPS)
            qn = (qh * rq * qg_ref[...]).astype(BF16)
            sink = sk_ref[pl.ds(hk * G + g, 1), :][:, :1]
            slope = sl_ref[pl.ds(hk * G + g, 1), :][:, :1]
            s = _dot(qn, kn, NT) * scale - slope * distf
            s = jnp.where(valid, s, -1e30)
            m = jnp.maximum(jnp.max(s, axis=1, keepdims=True), sink)
            p = jnp.where(valid, jnp.exp(s - m), 0.0)
            den = jnp.sum(p, axis=1, keepdims=True) + jnp.exp(sink - m)
            o_ref[g] = _bdot(p / den, vcat)

    qblk = pl.BlockSpec((G, W, d), lambda h, n: (h, n, 0))
    prev = pl.BlockSpec((None, W, d), lambda h, n: (h, jnp.maximum(n - 1, 0), 0))
    cur = pl.BlockSpec((None, W, d), lambda h, n: (h, n, 0))
    gain = pl.BlockSpec((1, d), lambda h, n: (0, 0))
    perhead = pl.BlockSpec((Hq, LANE), lambda h, n: (0, 0))
    return _pcall(
        body, name="swa_fwd", grid=(Hkv, nblk),
        in_specs=[qblk, prev, cur, prev, cur, gain, gain, perhead, perhead], out_specs=qblk,
        out_shape=jax.ShapeDtypeStruct((Hq, T, d), F32), compiler_params=_params(("parallel", "parallel")),
    )(q, k, k, v, v, qg, kg, sinks, slopes)


def _swa_bwd(q, k, v, qg, kg, sinks, slopes, do):
    Hq, T, d = q.shape
    Hkv = k.shape[0]
    G = Hq // Hkv
    W = WINDOW
    nblk = T // W
    scale = d ** -0.5

    def body(q_ref, kp_ref, kc_ref, vp_ref, vc_ref, qg_ref, kg_ref, sk_ref, sl_ref, do_ref,
             dq_ref, dk_ref, dv_ref, dqg_ref, dkg_ref, dsk_ref):
        hk = pl.program_id(0)
        n = pl.program_id(1)

        @pl.when((hk == 0) & (n == 0))
        def _():
            dqg_ref[...] = jnp.zeros_like(dqg_ref)
            dkg_ref[...] = jnp.zeros_like(dkg_ref)
            dsk_ref[...] = jnp.zeros_like(dsk_ref)

        @pl.when(n == 0)
        def _():
            dk_ref[...] = jnp.zeros_like(dk_ref)
            dv_ref[...] = jnp.zeros_like(dv_ref)

        kcat = jnp.concatenate([kp_ref[...], kc_ref[...]], axis=0)
        vcat = jnp.concatenate([vp_ref[...], vc_ref[...]], axis=0).astype(BF16)
        rk = lax.rsqrt(jnp.mean(kcat * kcat, axis=-1, keepdims=True) + EPS)
        kh = kcat * rk
        kn = (kh * kg_ref[...]).astype(BF16)
        row = _iota2((W, 2 * W), 0)
        col = _iota2((W, 2 * W), 1)
        dist = row + W - col
        valid = (dist >= 0) & (dist < W) & ((n > 0) | (col >= W))
        distf = dist.astype(F32)
        rowh = _iota2((Hq, LANE), 0)
        dkn = jnp.zeros((2 * W, d), F32)
        dvc = jnp.zeros((2 * W, d), F32)
        dqg = jnp.zeros((1, d), F32)
        dsk = jnp.zeros((Hq, LANE), F32)
        for g in range(G):
            qh = q_ref[g]
            rq = lax.rsqrt(jnp.mean(qh * qh, axis=-1, keepdims=True) + EPS)
            qhh = qh * rq
            qn = (qhh * qg_ref[...]).astype(BF16)
            sink = sk_ref[pl.ds(hk * G + g, 1), :][:, :1]
            slope = sl_ref[pl.ds(hk * G + g, 1), :][:, :1]
            s = _dot(qn, kn, NT) * scale - slope * distf
            s = jnp.where(valid, s, -1e30)
            m = jnp.maximum(jnp.max(s, axis=1, keepdims=True), sink)
            p = jnp.where(valid, jnp.exp(s - m), 0.0)
            esink = jnp.exp(sink - m)
            den = jnp.sum(p, axis=1, keepdims=True) + esink
            prob = p / den
            dov = do_ref[g].astype(BF16)
            dp = _dot(dov, vcat, NT)
            dd = jnp.sum(prob * dp, axis=1, keepdims=True)
            ds = prob * (dp - dd)
            dsink = -jnp.sum((esink / den) * dd, axis=0, keepdims=True)
            dsk = dsk + jnp.where(rowh == hk * G + g, dsink, 0.0)
            dsb = (ds * scale).astype(BF16)
            dqn = _dot(dsb, kn)
            dkn = dkn + _dot(dsb, qn, TN)
            dvc = dvc + _dot(prob.astype(BF16), dov, TN)
            dqh = dqn * qg_ref[...]
            dq_ref[g] = rq * (dqh - qhh * jnp.mean(dqh * qhh, axis=-1, keepdims=True))
            dqg = dqg + jnp.sum(dqn * qhh, axis=0, keepdims=True)
        dkh = dkn * kg_ref[...]
        dkraw = rk * (dkh - kh * jnp.mean(dkh * kh, axis=-1, keepdims=True))
        dqg_ref[...] += dqg
        dkg_ref[...] += jnp.sum(dkn * kh, axis=0, keepdims=True)
        dsk_ref[...] += dsk
        offp = pl.multiple_of(jnp.maximum(n - 1, 0) * W, W)
        offc = pl.multiple_of(n * W, W)
        dk_ref[pl.ds(offp, W), :] += dkraw[:W]
        dv_ref[pl.ds(offp, W), :] += dvc[:W]
        dk_ref[pl.ds(offc, W), :] += dkraw[W:]
        dv_ref[pl.ds(offc, W), :] += dvc[W:]

    qblk = pl.BlockSpec((G, W, d), lambda h, n: (h, n, 0))
    prev = pl.BlockSpec((None, W, d), lambda h, n: (h, jnp.maximum(n - 1, 0), 0))
    cur = pl.BlockSpec((None, W, d), lambda h, n: (h, n, 0))
    gain = pl.BlockSpec((1, d), lambda h, n: (0, 0))
    perhead = pl.BlockSpec((Hq, LANE), lambda h, n: (0, 0))
    full = pl.BlockSpec((None, T, d), lambda h, n: (h, 0, 0))
    kv = jax.ShapeDtypeStruct((Hkv, T, d), F32)
    gs = jax.ShapeDtypeStruct((1, d), F32)
    return _pcall(
        body, name="swa_bwd", grid=(Hkv, nblk),
        in_specs=[qblk, prev, cur, prev, cur, gain, gain, perhead, perhead, qblk],
        out_specs=[qblk, full, full, gain, gain, perhead],
        out_shape=[jax.ShapeDtypeStruct((Hq, T, d), F32), kv, kv, gs, gs, jax.ShapeDtypeStruct((Hq, LANE), F32)],
        compiler_params=_params(("arbitrary", "arbitrary")),
    )(q, k, k, v, v, qg, kg, sinks, slopes, do)


def _heads(z, n):
    T = z.shape[0]
    return z.reshape(T, n, HEAD_DIM).transpose(1, 0, 2)


def _unheads(z):
    n, T, d = z.shape
    return z.transpose(1, 0, 2).reshape(T, n * d)


def _alibi():
    s = [2.0 ** (-8.0 * (i + 1) / SWA_HEADS) for i in range(SWA_HEADS)]
    return jnp.broadcast_to(jnp.asarray(s, F32)[:, None], (SWA_HEADS, LANE))


def _att_fwd(h, g, w_in, w_out, q_gain, k_gain, sinks):
    hn = _rms_fwd(h, g, "att_norm")
    proj = _mm(hn, w_in, "nn", name="att_in")
    c = [0, SB_W, 2 * SB_W, 3 * SB_W, 3 * SB_W + SWA_QW, 3 * SB_W + SWA_QW + SWA_KVW, ATT_IN]
    sq, sk, sv = (_heads(proj[:, c[i]:c[i + 1]], SB_HEADS).astype(BF16) for i in range(3))
    bq = _heads(proj[:, c[3]:c[4]], SWA_HEADS)
    bk = _heads(proj[:, c[4]:c[5]], SWA_KV_HEADS)
    bv = _heads(proj[:, c[5]:c[6]], SWA_KV_HEADS)
    a_out, carry = _sb_fwd(sq, sk, sv)
    sk128 = jnp.broadcast_to(sinks.reshape(SWA_HEADS, 1), (SWA_HEADS, LANE))
    qg, kg = q_gain.reshape(1, HEAD_DIM), k_gain.reshape(1, HEAD_DIM)
    b_out = _swa_fwd(bq, bk, bv, qg, kg, sk128, _alibi())
    o = jnp.concatenate([_unheads(a_out), _unheads(b_out)], axis=-1).astype(BF16)
    h2 = _mm(o, w_out, "nn", res=h, name="att_out")
    return h2, (h, hn, sq, sk, sv, bq, bk, bv, carry, o, sk128, qg, kg)


def _att_bwd(dh2, saved, g, w_in, w_out):
    h, hn, sq, sk, sv, bq, bk, bv, carry, o, sk128, qg, kg = saved
    do = _mm(dh2, w_out, "nt", name="att_do")
    dw_out = _mm(o, dh2, "tn", out_dtype=BF16, name="att_dwout")
    da = _heads(do[:, :SB_W], SB_HEADS)
    db = _heads(do[:, SB_W:], SWA_HEADS)
    dsq, dsk, dsv = _sb_bwd(sq, sk, sv, carry, da)
    dbq, dbk, dbv, dqg, dkg, dsink = _swa_bwd(bq, bk, bv, qg, kg, sk128, _alibi(), db)
    dproj = jnp.concatenate([_unheads(z) for z in (dsq, dsk, dsv, dbq, dbk, dbv)], axis=-1).astype(BF16)
    dw_in = _mm(hn, dproj, "tn", out_dtype=BF16, name="att_dwin")
    dhn = _mm(dproj, w_in, "nt", name="att_dhn")
    dh, dg = _rms_bwd(dhn, h, g, dh2, "att_dnorm")
    return dh, dg, dw_in, dw_out, dqg.reshape(HEAD_DIM), dkg.reshape(HEAD_DIM), dsink[:, 0]


CONV_ROWS = 512
CONV_COLS = 512
HALO = 8


def _shifted(xcat, s, tm):
    if s == 0:
        return xcat[HALO:HALO + tm]
    return pltpu.roll(xcat, s, 0)[HALO:HALO + tm]


def _conv_pre(x_ref, halo_ref, w_ref, i, tm):
    xc = x_ref[...]
    halo = jnp.where(i > 0, halo_ref[...], 0.0)
    xcat = jnp.concatenate([halo, xc], axis=0)
    w = w_ref[...]
    y = w[GDN_CONV - 1:GDN_CONV] * xc
    for kk in range(GDN_CONV - 1):
        y = y + w[kk:kk + 1] * _shifted(xcat, GDN_CONV - 1 - kk, tm)
    return xcat, y


def _l2_heads(s, qscale_of):
    outs, rs = [], []
    for hh in range(s.shape[1] // GDN_HEAD_DIM):
        sh = s[:, hh * GDN_HEAD_DIM:(hh + 1) * GDN_HEAD_DIM]
        r = lax.rsqrt(jnp.sum(sh * sh, axis=-1, keepdims=True) + EPS)
        outs.append(sh * r)
        rs.append(r)
    return outs, rs


def _conv_specs(T, col0, tm, tc):
    cur = pl.BlockSpec((tm, tc), lambda j, i: (i, j + col0 // tc))
    halo = pl.BlockSpec((HALO, tc), lambda j, i: (jnp.maximum(i * (tm // HALO) - 1, 0), j + col0 // tc))
    wsp = pl.BlockSpec((GDN_CONV, tc), lambda j, i: (0, j + col0 // tc))
    out = pl.BlockSpec((tm, tc), lambda j, i: (i, j))
    return cur, halo, wsp, out


def _conv_fwd(proj, conv_w, col0, width, norm, name):
    T = proj.shape[0]
    tm, tc = _pick(T, CONV_ROWS), CONV_COLS
    cur, halo, wsp, out = _conv_specs(T, col0, tm, tc)
    n_q_tiles = (width // 2) // tc

    def body(x_ref, halo_ref, w_ref, o_ref):
        j, i = pl.program_id(0), pl.program_id(1)
        _, y = _conv_pre(x_ref, halo_ref, w_ref, i, tm)
        s = y * _sigmoid(y)
        if norm:
            outs, _ = _l2_heads(s, None)
            qs = jnp.where(j < n_q_tiles, GDN_HEAD_DIM ** -0.5, 1.0)
            o_ref[...] = jnp.concatenate(outs, axis=1) * qs
        else:
            o_ref[...] = s

    return _pcall(body, name=name, grid=(width // tc, T // tm), in_specs=[cur, halo, wsp], out_specs=out,
                  out_shape=jax.ShapeDtypeStruct((T, width), F32),
                  compiler_params=_params(("parallel", "parallel")))(proj, proj, conv_w)


def _conv_bwd_pre(proj, conv_w, dout, col0, width, norm, name):
    T = proj.shape[0]
    tm, tc = _pick(T, CONV_ROWS), CONV_COLS
    cur, halo, wsp, out = _conv_specs(T, col0, tm, tc)
    n_q_tiles = (width // 2) // tc

    def body(x_ref, halo_ref, w_ref, d_ref, dy_ref, dw_ref):
        j, i = pl.program_id(0), pl.program_id(1)
        xcat, y = _conv_pre(x_ref, halo_ref, w_ref, i, tm)
        sg = _sigmoid(y)
        s = y * sg
        d = d_ref[...]
        if norm:
            qs = jnp.where(j < n_q_tiles, GDN_HEAD_DIM ** -0.5, 1.0)
            d = d * qs
            outs, rs = _l2_heads(s, None)
            parts = []
            for hh, (nh, r) in enumerate(zip(outs, rs)):
                dh = d[:, hh * GDN_HEAD_DIM:(hh + 1) * GDN_HEAD_DIM]
                parts.append(r * (dh - nh * jnp.sum(dh * nh, axis=-1, keepdims=True)))
            ds = jnp.concatenate(parts, axis=1)
        else:
            ds = d
        dy = ds * sg * (1.0 + y * (1.0 - sg))
        dy_ref[...] = dy
        rows = [jnp.sum(dy * _shifted(xcat, GDN_CONV - 1 - kk, tm), axis=0, keepdims=True) for kk in range(GDN_CONV)]
        part = jnp.concatenate(rows, axis=0)

        @pl.when(i == 0)
        def _():
            dw_ref[...] = part

        @pl.when(i > 0)
        def _():
            dw_ref[...] += part

    wout = pl.BlockSpec((GDN_CONV, tc), lambda j, i: (0, j))
    return _pcall(body, name=name, grid=(width // tc, T // tm), in_specs=[cur, halo, wsp, out], out_specs=[out, wout],
                  out_shape=[jax.ShapeDtypeStruct((T, width), F32), jax.ShapeDtypeStruct((GDN_CONV, width), F32)],
                  compiler_params=_params(("parallel", "arbitrary")))(proj, proj, conv_w, dout)


def _conv_bwd_in(dy, conv_w, name):
    T, C = dy.shape
    tm, tc = _pick(T, CONV_ROWS), CONV_COLS
    nrow = T // tm

    def body(d_ref, nxt_ref, w_ref, dx_ref):
        i = pl.program_id(0)
        dc = d_ref[...]
        nxt = jnp.where(i < nrow - 1, nxt_ref[...], 0.0)
        dcat = jnp.concatenate([dc, nxt], axis=0)
        w = w_ref[...]
        dx = w[GDN_CONV - 1:GDN_CONV] * dc
        for kk in range(GDN_CONV - 1):
            s = GDN_CONV - 1 - kk
            dx = dx + w[kk:kk + 1] * pltpu.roll(dcat, tm + HALO - s, 0)[:tm]
        dx_ref[...] = dx.astype(BF16)

    cur = pl.BlockSpec((tm, tc), lambda i, j: (i, j))
    nxt = pl.BlockSpec((HALO, tc), lambda i, j: (jnp.minimum((i + 1) * (tm // HALO), T // HALO - 1), j))
    wsp = pl.BlockSpec((GDN_CONV, tc), lambda i, j: (0, j))
    return _pcall(body, name=name, grid=(nrow, C // tc), in_specs=[cur, nxt, wsp], out_specs=cur,
                  out_shape=jax.ShapeDtypeStruct((T, C), BF16),
                  compiler_params=_params(("parallel", "parallel")))(dy, dy, conv_w)


GATE_ROWS = 512


def _chunk_mask(n, lower):
    row = _iota2((n, n), 0)
    col = _iota2((n, n), 1)
    same = (row // GDN_CHUNK) == (col // GDN_CHUNK)
    tri = (row >= col) if lower else (row <= col)
    return (same & tri).astype(BF16)


def _gates_fwd(proj, a_log, dt_bias):
    T = proj.shape[0]
    tm = _pick(T, GATE_ROWS)
    c0 = (GDN_CONV_W + GDN_VW) // LANE

    def body(bl_ref, a_ref, alog_ref, dt_ref, beta_ref, g_ref, gc_ref):
        beta_ref[...] = _sigmoid(bl_ref[...])
        g = -jnp.exp(alog_ref[...]) * _softplus(a_ref[...] + dt_ref[...])
        g_ref[...] = g
        gc_ref[...] = _mdot2(_chunk_mask(tm, True), g)

    blk = lambda c: pl.BlockSpec((tm, LANE), lambda i: (i, c))
    vec = pl.BlockSpec((1, LANE), lambda i: (0, 0))
    sh = jax.ShapeDtypeStruct((T, LANE), F32)
    return _pcall(body, name="gdn_gates", grid=(T // tm,), in_specs=[blk(c0), blk(c0 + 1), vec, vec],
                  out_specs=[blk(0), blk(0), blk(0)], out_shape=[sh, sh, sh],
                  compiler_params=_params(("parallel",)))(proj, proj, a_log, dt_bias)


def _gates_bwd(proj, a_log, dt_bias, beta, g, dbeta, dgc):
    T = proj.shape[0]
    tm = _pick(T, GATE_ROWS)
    c0 = (GDN_CONV_W + GDN_VW) // LANE

    def body(a_ref, alog_ref, dt_ref, beta_ref, g_ref, dbeta_ref, dgc_ref, dbl_ref, da_ref, dalog_ref, ddt_ref):
        dg = _mdot2(_chunk_mask(tm, False), dgc_ref[...])
        b = beta_ref[...]
        dbl_ref[...] = (dbeta_ref[...] * b * (1.0 - b)).astype(BF16)
        da = dg * (-jnp.exp(alog_ref[...])) * _sigmoid(a_ref[...] + dt_ref[...])
        da_ref[...] = da.astype(BF16)
        p1 = jnp.sum(dg * g_ref[...], axis=0, keepdims=True)
        p2 = jnp.sum(da, axis=0, keepdims=True)

        @pl.when(pl.program_id(0) == 0)
        def _():
            dalog_ref[...] = p1
            ddt_ref[...] = p2

        @pl.when(pl.program_id(0) > 0)
        def _():
            dalog_ref[...] += p1
            ddt_ref[...] += p2

    blk = lambda c: pl.BlockSpec((tm, LANE), lambda i: (i, c))
    vec = pl.BlockSpec((1, LANE), lambda i: (0, 0))
    shb = jax.ShapeDtypeStruct((T, LANE), BF16)
    shv = jax.ShapeDtypeStruct((1, LANE), F32)
    return _pcall(body, name="gdn_dgates", grid=(T // tm,),
                  in_specs=[blk(c0 + 1), vec, vec, blk(0), blk(0), blk(0), blk(0)],
                  out_specs=[blk(0), blk(0), vec, vec], out_shape=[shb, shb, shv, shv],
                  compiler_params=_params(("arbitrary",)))(proj, a_log, dt_bias, beta, g, dbeta, dgc)


def _inv_unit_lower(Ls):
    C = Ls[0].shape[0]
    row = _iota2((C, C), 0)
    col = _iota2((C, C), 1)
    blk16 = (row // 16) == (col // 16)
    blk32 = (row // 32) == (col // 32)
    eye = (row == col).astype(F32)
    xs = [-jnp.where(blk16, L, 0.0) for L in Ls]
    inv = [eye + x for x in xs]
    for _ in range(3):
        xs = [_dot3(x, x) for x in xs]
        inv = [a + _dot3(a, x) for a, x in zip(inv, xs)]
    for mask in (blk32 & ~blk16, ~blk32):
        t = [_dot3(a, jnp.where(mask, L, 0.0)) for a, L in zip(inv, Ls)]
        inv = [a - _dot3(ti, a) for a, ti in zip(inv, t)]
    return inv


GDN_GROUP = 4
GDN_PREP_CHUNKS = 4


def _gdn_specs(T):
    C, D, E = GDN_CHUNK, GDN_HEAD_DIM, GDN_GROUP
    n = T // C
    qk = pl.BlockSpec((C, (E // 2) * D), lambda h, i: (i, h))
    vE = pl.BlockSpec((C, E * D), lambda h, i: (i, h))
    colv = pl.BlockSpec((E, C, 1), lambda h, i: (h, i, 0))
    rowv = pl.BlockSpec((E, None, 1, C), lambda h, i: (h, i, 0, 0))
    st = pl.BlockSpec((E, None, D, D), lambda h, i: (h, i, 0, 0))
    am = pl.BlockSpec((E, None, C, C), lambda h, i: (h, i, 0, 0))
    return n, qk, vE, colv, rowv, st, am


def _gdn_decay(gcol, grow):
    C = GDN_CHUNK
    row = _iota2((C, C), 0)
    col = _iota2((C, C), 1)
    incl = row >= col
    dm = jnp.where(incl, jnp.exp(jnp.where(incl, gcol - grow, 0.0)), 0.0)
    glast = grow[:, C - 1:C]
    return dm, jnp.exp(gcol), jnp.exp(glast), jnp.exp(glast - gcol), row > col, incl


def _gdn_prep(k, beta, gcol, grow):
    T = k.shape[0]
    C, D, B = GDN_CHUNK, GDN_HEAD_DIM, GDN_PREP_CHUNKS
    n = T // C

    def body(k_ref, b_ref, gc_ref, gr_ref, a_ref):
        idx = [(e, cb) for e in range(2) for cb in range(B)]
        kc = {cb: k_ref[cb * C:(cb + 1) * C, :] for cb in range(B)}
        lm = []
        for e, cb in idx:
            beta = b_ref[e, cb * C:(cb + 1) * C, :]
            dm, _, _, _, strict, _ = _gdn_decay(gc_ref[e, cb * C:(cb + 1) * C, :], gr_ref[e, cb])
            lm.append(jnp.where(strict, _bdot(kc[cb] * beta, kc[cb], NT) * dm, 0.0))
        inv = _inv_unit_lower(lm)
        for (e, cb), a in zip(idx, inv):
            a_ref[e, cb] = a

    return _pcall(
        body, name="gdn_prep", grid=(GDN_K_HEADS, n // B),
        in_specs=[pl.BlockSpec((B * C, D), lambda h, i: (i, h)), pl.BlockSpec((2, B * C, 1), lambda h, i: (h, i, 0)),
                  pl.BlockSpec((2, B * C, 1), lambda h, i: (h, i, 0)), pl.BlockSpec((2, B, 1, C), lambda h, i: (h, i, 0, 0))],
        out_specs=pl.BlockSpec((2, B, C, C), lambda h, i: (h, i, 0, 0)),
        out_shape=jax.ShapeDtypeStruct((GDN_V_HEADS, n, C, C), F32),
        compiler_params=_params(("parallel", "parallel")),
    )(k, beta, gcol, grow)


def _gdn_fwd(q, k, v, beta, gcol, grow, amat):
    T = q.shape[0]
    C, D, E = GDN_CHUNK, GDN_HEAD_DIM, GDN_GROUP
    n, qk, vE, colv, rowv, st, am = _gdn_specs(T)
    R = range(E)

    def body(q_ref, k_ref, v_ref, b_ref, gc_ref, gr_ref, a_ref, o_ref, s_ref, vn_ref, state):
        @pl.when(pl.program_id(1) == 0)
        def _():
            state[...] = jnp.zeros_like(state)

        qv = [q_ref[:, (e // 2) * D:(e // 2 + 1) * D] for e in R]
        kv = [k_ref[:, (e // 2) * D:(e // 2 + 1) * D] for e in R]
        vv = [v_ref[:, e * D:(e + 1) * D] for e in R]
        beta = [b_ref[e] for e in R]
        a = [a_ref[e] for e in R]
        s = [state[e] for e in R]
        dec = [_gdn_decay(gc_ref[e], gr_ref[e]) for e in R]
        pm = [_bdot(qv[e], kv[e], NT) * dec[e][0] for e in R]
        r = [beta[e] * (vv[e] - _bdot(kv[e] * dec[e][1], s[e])) for e in R]
        vn = [_dot3(a[e], r[e]) for e in R]
        o = [_bdot(qv[e] * dec[e][1], s[e]) + _bdot(pm[e], vn[e]) for e in R]
        s2 = [dec[e][2] * s[e] + _bdot(kv[e] * dec[e][3], vn[e], TN) for e in R]
        for e in R:
            s_ref[e] = s[e]
            vn_ref[:, e * D:(e + 1) * D] = vn[e]
            o_ref[:, e * D:(e + 1) * D] = o[e]
            state[e] = s2[e]

    shv = jax.ShapeDtypeStruct((T, GDN_V_HEADS * D), F32)
    return _pcall(
        body, name="gdn_fwd", grid=(GDN_V_HEADS // E, n), in_specs=[qk, qk, vE, colv, colv, rowv, am],
        out_specs=[vE, st, vE],
        out_shape=[shv, jax.ShapeDtypeStruct((GDN_V_HEADS, n, D, D), F32), shv],
        scratch_shapes=[pltpu.VMEM((E, D, D), F32)],
        compiler_params=_params(("parallel", "arbitrary")),
    )(q, k, v, beta, gcol, grow, amat)


def _gdn_bwd(q, k, v, beta, gcol, grow, states, amat, vnew, do):
    T = q.shape[0]
    C, D, E = GDN_CHUNK, GDN_HEAD_DIM, GDN_GROUP
    n, qk, vE, colv, rowv, st, am = _gdn_specs(T)
    rev = lambda spec: pl.BlockSpec(spec.block_shape, (lambda f: (lambda h, i: f(h, n - 1 - i)))(spec.index_map))
    qk, vE, colv, rowv, st, am = (rev(s) for s in (qk, vE, colv, rowv, st, am))
    R = range(E)

    def body(q_ref, k_ref, v_ref, b_ref, gc_ref, gr_ref, s_ref, a_ref, vn_ref, do_ref,
             dq_ref, dk_ref, dv_ref, db_ref, dgc_ref, dstate):
        @pl.when(pl.program_id(1) == 0)
        def _():
            dstate[...] = jnp.zeros_like(dstate)

        M = lambda f: [f(e) for e in R]
        rsum = lambda x: jnp.sum(x, axis=1, keepdims=True)
        qv = M(lambda e: q_ref[:, (e // 2) * D:(e // 2 + 1) * D])
        kv = M(lambda e: k_ref[:, (e // 2) * D:(e // 2 + 1) * D])
        vv = M(lambda e: v_ref[:, e * D:(e + 1) * D])
        vn = M(lambda e: vn_ref[:, e * D:(e + 1) * D])
        dov = M(lambda e: do_ref[:, e * D:(e + 1) * D])
        beta = M(lambda e: b_ref[e])
        s = M(lambda e: s_ref[e])
        a = M(lambda e: a_ref[e])
        dsn = M(lambda e: dstate[e])
        dec = M(lambda e: _gdn_decay(gc_ref[e], gr_ref[e]))
        dm, gam, glast, tail = (M(lambda e: dec[e][i]) for i in range(4))
        strict, incl = dec[0][4], dec[0][5]
        kb = M(lambda e: kv[e] * beta[e])
        kd = M(lambda e: kv[e] * gam[e])
        qd = M(lambda e: qv[e] * gam[e])
        kt = M(lambda e: kv[e] * tail[e])
        lmat = M(lambda e: jnp.where(strict, _bdot(kb[e], kv[e], NT) * dm[e], 0.0))
        pmat = M(lambda e: _bdot(qv[e], kv[e], NT) * dm[e])
        xres = M(lambda e: vv[e] - _bdot(kd[e], s[e]))
        dvn = M(lambda e: _bdot(pmat[e], dov[e], TN) + _bdot(kt[e], dsn[e]))
        dqd = M(lambda e: _bdot(dov[e], s[e], NT))
        dp = M(lambda e: jnp.where(incl, _bdot(dov[e], vn[e], NT), 0.0))
        dkt = M(lambda e: _bdot(vn[e], dsn[e], NT))
        dr = M(lambda e: _dot3(a[e], dvn[e], TN))
        drb = M(lambda e: beta[e] * dr[e])
        dkd = M(lambda e: -_bdot(drb[e], s[e], NT))
        ds2 = M(lambda e: _bdot(qd[e], dov[e], TN) + glast[e] * dsn[e] - _bdot(kd[e], drb[e], TN))
        dl = M(lambda e: -jnp.where(strict, _bdot(dr[e], vn[e], NT), 0.0))
        dmm = M(lambda e: dl[e] * dm[e])
        dnn = M(lambda e: dp[e] * dm[e])
        emat = M(lambda e: dl[e] * lmat[e] + dp[e] * pmat[e])
        dkb = M(lambda e: _bdot(dmm[e], kv[e]))
        dk = M(lambda e: beta[e] * dkb[e] + _bdot(dmm[e], kb[e], TN) + _bdot(dnn[e], qv[e], TN)
               + gam[e] * dkd[e] + tail[e] * dkt[e])
        dq = M(lambda e: _bdot(dnn[e], kv[e]) + gam[e] * dqd[e])
        dbeta = M(lambda e: rsum(dr[e] * xres[e]) + rsum(dkb[e] * kv[e]))
        ones = jnp.ones((C, LANE), BF16)
        colsum = M(lambda e: _dot2m(emat[e], ones, TN)[:, :1])
        tails = M(lambda e: rsum(dkt[e] * kt[e]))
        lastrow = _iota2((C, 1), 0) == C - 1
        dlast = M(lambda e: jnp.sum(tails[e], axis=0, keepdims=True)
                  + glast[e] * jnp.sum(rsum(s[e] * dsn[e]), axis=0, keepdims=True))
        dgc = M(lambda e: rsum(emat[e]) - colsum[e] + rsum(dkd[e] * kd[e]) + rsum(dqd[e] * qd[e]) - tails[e]
                + jnp.where(lastrow, dlast[e], 0.0))
        for e in R:
            dv_ref[:, e * D:(e + 1) * D] = drb[e]
            db_ref[e] = dbeta[e]
            dgc_ref[e] = dgc[e]
            dstate[e] = ds2[e]
        for kh in range(E // 2):
            dq_ref[:, kh * D:(kh + 1) * D] = dq[2 * kh] + dq[2 * kh + 1]
            dk_ref[:, kh * D:(kh + 1) * D] = dk[2 * kh] + dk[2 * kh + 1]

    shq = jax.ShapeDtypeStruct((T, GDN_K_HEADS * D), F32)
    shv = jax.ShapeDtypeStruct((T, GDN_V_HEADS * D), F32)
    shc = jax.ShapeDtypeStruct((GDN_V_HEADS, T, 1), F32)
    return _pcall(
        body, name="gdn_bwd", grid=(GDN_V_HEADS // E, n),
        in_specs=[qk, qk, vE, colv, colv, rowv, st, am, vE, vE],
        out_specs=[qk, qk, vE, colv, colv], out_shape=[shq, shq, shv, shc, shc],
        scratch_shapes=[pltpu.VMEM((E, D, D), F32)],
        compiler_params=_params(("parallel", "arbitrary")),
    )(q, k, v, beta, gcol, grow, states, amat, vnew, do)


def _outgate_fwd(o, proj, gain):
    T = o.shape[0]
    tm, tc = _pick(T, CONV_ROWS), CONV_COLS
    z0 = GDN_CONV_W // tc

    def body(o_ref, z_ref, g_ref, y_ref):
        z = z_ref[...]
        sz = z * _sigmoid(z)
        parts = []
        for hh in range(tc // GDN_HEAD_DIM):
            oh = o_ref[:, hh * GDN_HEAD_DIM:(hh + 1) * GDN_HEAD_DIM]
            r = lax.rsqrt(jnp.mean(oh * oh, axis=-1, keepdims=True) + EPS)
            parts.append(oh * r * g_ref[...])
        y_ref[...] = (jnp.concatenate(parts, axis=1) * sz).astype(BF16)

    blk = pl.BlockSpec((tm, tc), lambda i, j: (i, j))
    return _pcall(body, name="gdn_outgate", grid=(T // tm, GDN_VW // tc),
                  in_specs=[blk, pl.BlockSpec((tm, tc), lambda i, j: (i, j + z0)), pl.BlockSpec((1, GDN_HEAD_DIM), lambda i, j: (0, 0))],
                  out_specs=blk, out_shape=jax.ShapeDtypeStruct((T, GDN_VW), BF16),
                  compiler_params=_params(("parallel", "parallel")))(o, proj, gain)


def _outgate_bwd(dy, o, proj, gain):
    T = o.shape[0]
    tm, tc = _pick(T, CONV_ROWS), CONV_COLS
    z0 = GDN_CONV_W // tc
    nh = tc // GDN_HEAD_DIM

    def body(dy_ref, o_ref, z_ref, g_ref, do_ref, dz_ref, dg_ref):
        z = z_ref[...]
        sg = _sigmoid(z)
        sz = z * sg
        dy = dy_ref[...]
        dgain = jnp.zeros((1, GDN_HEAD_DIM), F32)
        dos, ys = [], []
        for hh in range(nh):
            sl = slice(hh * GDN_HEAD_DIM, (hh + 1) * GDN_HEAD_DIM)
            oh = o_ref[:, sl]
            r = lax.rsqrt(jnp.mean(oh * oh, axis=-1, keepdims=True) + EPS)
            xh = oh * r
            dn = dy[:, sl] * sz[:, sl]
            dgain = dgain + jnp.sum(dn * xh, axis=0, keepdims=True)
            dxh = dn * g_ref[...]
            dos.append(r * (dxh - xh * jnp.mean(dxh * xh, axis=-1, keepdims=True)))
            ys.append(xh * g_ref[...])
        do_ref[...] = jnp.concatenate(dos, axis=1)
        dz_ref[...] = (dy * jnp.concatenate(ys, axis=1) * sg * (1.0 + z * (1.0 - sg))).astype(BF16)
        first = (pl.program_id(0) == 0) & (pl.program_id(1) == 0)

        @pl.when(first)
        def _():
            dg_ref[...] = dgain

        @pl.when(jnp.logical_not(first))
        def _():
            dg_ref[...] += dgain

    blk = pl.BlockSpec((tm, tc), lambda i, j: (i, j))
    vec = pl.BlockSpec((1, GDN_HEAD_DIM), lambda i, j: (0, 0))
    return _pcall(body, name="gdn_doutgate", grid=(T // tm, GDN_VW // tc),
                  in_specs=[blk, blk, pl.BlockSpec((tm, tc), lambda i, j: (i, j + z0)), vec],
                  out_specs=[blk, blk, vec],
                  out_shape=[jax.ShapeDtypeStruct((T, GDN_VW), F32), jax.ShapeDtypeStruct((T, GDN_VW), BF16),
                             jax.ShapeDtypeStruct((1, GDN_HEAD_DIM), F32)],
                  compiler_params=_params(("arbitrary", "arbitrary")))(dy, o, proj, gain)


def _pad_lanes(vec):
    return jnp.pad(vec.reshape(1, -1), ((0, 0), (0, LANE - vec.shape[-1])))


def _head_cols(a):
    return a[:, :GDN_V_HEADS].T[:, :, None]


def _gdn_pad_in(w_in):
    c = GDN_CONV_W + GDN_VW
    z = jnp.zeros(w_in.shape[:-1] + (LANE - GDN_V_HEADS,), w_in.dtype)
    return jnp.concatenate([w_in[..., :c + GDN_V_HEADS], z, w_in[..., c + GDN_V_HEADS:], z], axis=-1)


def _gdn_unpad_in(dw):
    c = GDN_CONV_W + GDN_VW
    return jnp.concatenate([dw[..., :c + GDN_V_HEADS], dw[..., c + LANE:c + LANE + GDN_V_HEADS]], axis=-1)


def _gdn_mixer_fwd(h, g, w_in_pad, conv_w, a_log, dt_bias, out_gain, w_out):
    T = h.shape[0]
    hn = _rms_fwd(h, g, "gdn_norm")
    proj = _mm(hn, w_in_pad, "nn", name="gdn_in")
    qk = _conv_fwd(proj, conv_w, 0, 2 * GDN_KW, True, "gdn_conv_qk")
    vv = _conv_fwd(proj, conv_w, 2 * GDN_KW, GDN_VW, False, "gdn_conv_v")
    alog, dtb = _pad_lanes(a_log), _pad_lanes(dt_bias)
    beta, gl, gc = _gates_fwd(proj, alog, dtb)
    bcol, gcol = _head_cols(beta), _head_cols(gc)
    grow = gcol.reshape(GDN_V_HEADS, T // GDN_CHUNK, 1, GDN_CHUNK)
    qn, kn = qk[:, :GDN_KW], qk[:, GDN_KW:]
    amat = _gdn_prep(kn, bcol, gcol, grow)
    o, states, vnew = _gdn_fwd(qn, kn, vv, bcol, gcol, grow, amat)
    gain = out_gain.reshape(1, GDN_HEAD_DIM)
    y = _outgate_fwd(o, proj, gain)
    h2 = _mm(y, w_out, "nn", res=h, name="gdn_out")
    return h2, (h, hn, proj, qn, kn, vv, beta, gl, bcol, gcol, grow, o, states, amat, vnew, y, alog, dtb, gain)


def _gdn_mixer_bwd(dh2, saved, g, w_in_pad, conv_w, w_out):
    h, hn, proj, qn, kn, vv, beta, gl, bcol, gcol, grow, o, states, amat, vnew, y, alog, dtb, gain = saved
    T = h.shape[0]
    dy = _mm(dh2, w_out, "nt", name="gdn_dy")
    dw_out = _mm(y, dh2, "tn", out_dtype=BF16, name="gdn_dwout")
    do, dz, dgain = _outgate_bwd(dy, o, proj, gain)
    dq, dk, dv, dbcol, dgccol = _gdn_bwd(qn, kn, vv, bcol, gcol, grow, states, amat, vnew, do)
    dqk = jnp.concatenate([dq, dk], axis=1)
    dy_qk, dcw_qk = _conv_bwd_pre(proj, conv_w, dqk, 0, 2 * GDN_KW, True, "gdn_dconv_qk")
    dy_v, dcw_v = _conv_bwd_pre(proj, conv_w, dv, 2 * GDN_KW, GDN_VW, False, "gdn_dconv_v")
    dx_qk = _conv_bwd_in(dy_qk, conv_w[:, :2 * GDN_KW], "gdn_dconvin_qk")
    dx_v = _conv_bwd_in(dy_v, conv_w[:, 2 * GDN_KW:], "gdn_dconvin_v")
    lanes = lambda c: jnp.pad(c[:, :, 0].T, ((0, 0), (0, LANE - GDN_V_HEADS)))
    dbl, da, dalog, ddt = _gates_bwd(proj, alog, dtb, beta, gl, lanes(dbcol), lanes(dgccol))
    dproj = jnp.concatenate([dx_qk, dx_v, dz, dbl, da], axis=1)
    dw_in_pad = _mm(hn, dproj, "tn", out_dtype=BF16, name="gdn_dwin")
    dhn = _mm(dproj, w_in_pad, "nt", name="gdn_dhn")
    dh, dg = _rms_bwd(dhn, h, g, dh2, "gdn_dnorm")
    dconv = jnp.concatenate([dcw_qk, dcw_v], axis=1)
    return (dh, dg, _gdn_unpad_in(dw_in_pad), dconv, dalog[0, :GDN_V_HEADS], ddt[0, :GDN_V_HEADS],
            dgain.reshape(GDN_HEAD_DIM), dw_out)


def _local_step(x, p, target, w):
    h = x
    tape = []
    gdn_in_pad = _gdn_pad_in(w["gdn_w_in"][0])
    for i in range(2):
        h, s1 = _ffn_fwd(h, w["ffn_norm"][i, 0], w["ffn_w_gate"][i, 0], w["ffn_w_up"][i, 0], w["ffn_w_down"][i, 0], f"ffn{i}a")
        if i == 0:
            h, s2 = _att_fwd(h, w["mix_norm"][0], w["att_w_in"][0], w["att_w_out"][0], w["att_q_norm"][0],
                             w["att_k_norm"][0], w["att_sinks"][0])
        else:
            h, s2 = _gdn_mixer_fwd(h, w["mix_norm"][1], gdn_in_pad, w["gdn_conv_w"][0], w["gdn_a_log"][0],
                                   w["gdn_dt_bias"][0], w["gdn_out_norm"][0], w["gdn_w_out"][0])
        h, s3 = _ffn_fwd(h, w["ffn_norm"][i, 1], w["ffn_w_gate"][i, 1], w["ffn_w_up"][i, 1], w["ffn_w_down"][i, 1], f"ffn{i}b")
        h, s4 = _ple_fwd(h, p[i], w["ple_norm"][i], w["ple_w_gate"][i], w["ple_w_proj"][i], f"ple{i}")
        tape.append((s1, s2, s3, s4))

    loss, dh = _loss_head(h, target)

    g = {}
    ffn_norm = [[None, None], [None, None]]
    ffn_g = [[None, None], [None, None]]
    ffn_u = [[None, None], [None, None]]
    ffn_d = [[None, None], [None, None]]
    mix_norm, ple_norm, ple_g, ple_p = [None, None], [None, None], [None, None], [None, None]
    for i in (1, 0):
        s1, s2, s3, s4 = tape[i]
        dh, ple_norm[i], ple_g[i], ple_p[i] = _ple_bwd(dh, s4, p[i], w["ple_norm"][i], w["ple_w_gate"][i], f"ple{i}")
        dh, ffn_norm[i][1], ffn_g[i][1], ffn_u[i][1], ffn_d[i][1] = _ffn_bwd(
            dh, s3, w["ffn_norm"][i, 1], w["ffn_w_gate"][i, 1], w["ffn_w_up"][i, 1], w["ffn_w_down"][i, 1], f"ffn{i}b")
        if i == 0:
            dh, mix_norm[0], dwin, dwout, dqg, dkg, dsink = _att_bwd(dh, s2, w["mix_norm"][0], w["att_w_in"][0], w["att_w_out"][0])
            g["att_w_in"], g["att_w_out"] = dwin[None], dwout[None]
            g["att_q_norm"], g["att_k_norm"], g["att_sinks"] = dqg[None], dkg[None], dsink[None]
        else:
            dh, mix_norm[1], dwin, dconv, dalog, ddt, dgain, dwout = _gdn_mixer_bwd(
                dh, s2, w["mix_norm"][1], gdn_in_pad, w["gdn_conv_w"][0], w["gdn_w_out"][0])
            g["gdn_w_in"], g["gdn_conv_w"], g["gdn_w_out"] = dwin[None], dconv[None], dwout[None]
            g["gdn_a_log"], g["gdn_dt_bias"], g["gdn_out_norm"] = dalog[None], ddt[None], dgain[None]
        dh, ffn_norm[i][0], ffn_g[i][0], ffn_u[i][0], ffn_d[i][0] = _ffn_bwd(
            dh, s1, w["ffn_norm"][i, 0], w["ffn_w_gate"][i, 0], w["ffn_w_up"][i, 0], w["ffn_w_down"][i, 0], f"ffn{i}a")
    st2 = lambda rows: jnp.stack([jnp.stack(r) for r in rows])
    g["ffn_norm"], g["ffn_w_gate"], g["ffn_w_up"], g["ffn_w_down"] = st2(ffn_norm), st2(ffn_g), st2(ffn_u), st2(ffn_d)
    g["mix_norm"], g["ple_norm"] = jnp.stack(mix_norm), jnp.stack(ple_norm)
    g["ple_w_gate"], g["ple_w_proj"] = jnp.stack(ple_g), jnp.stack(ple_p)
    return loss, dh, g


MESH = pl.DeviceIdType.MESH
N_CHIP = 4


def _place():
    x, y, c = lax.axis_index("x"), lax.axis_index("y"), lax.axis_index("c")
    others = [((1 - x, y), 2 * (1 - x) + y), ((x, 1 - y), 2 * x + (1 - y)), ((1 - x, 1 - y), 2 * (1 - x) + (1 - y))]
    return x, y, c, 4 * x + 2 * y + c, 2 * x + y, (x, y, 1 - c), others


def _comm_call(body, arrays, out_shape, n_sems, name):
    hbm = pl.BlockSpec(memory_space=pl.ANY)
    n = len(arrays)
    return _pcall(
        body, name=name, in_specs=[hbm] * n, out_specs=[hbm] * len(out_shape), out_shape=out_shape,
        scratch_shapes=[pltpu.SemaphoreType.DMA((n, n_sems)), pltpu.SemaphoreType.DMA((n, n_sems)),
                        pltpu.SemaphoreType.DMA((n, N_CHIP))],
        compiler_params=pltpu.CompilerParams(has_side_effects=True),
    )(*arrays)


def _all_gather(arrays):
    n = len(arrays)

    def body(*refs):
        ins, outs = refs[:n], refs[n:2 * n]
        send_sems, recv_sems, local_sems = refs[2 * n:]
        x, y, c, me, my_chip, sibling, others = _place()

        def copy(a, k, block, to, src=None):
            dst = outs[a].at[block]
            return pltpu.make_async_remote_copy(
                src_ref=dst if src is None else src, dst_ref=dst, send_sem=send_sems.at[a, k],
                recv_sem=recv_sems.at[a, k], device_id=to, device_id_type=MESH)

        local = [pltpu.make_async_copy(ins[a], outs[a].at[me], local_sems.at[a, 0]) for a in range(n)]
        for cp in local:
            cp.start()
        first = []
        for a in range(n):
            first.append(copy(a, 0, me, sibling, src=ins[a]))
            first += [copy(a, 1 + j, me, (*chip, c), src=ins[a]) for j, (chip, _) in enumerate(others)]
        for cp in first:
            cp.start()
        passed = []
        for a in range(n):
            for j, (chip, chip_idx) in enumerate(others):
                blk = 2 * chip_idx + c
                copy(a, 1 + j, blk, (x, y, c)).wait_recv()
                fwd = copy(a, 4 + j, blk, sibling)
                fwd.start()
                passed.append(fwd)
        for a in range(n):
            copy(a, 0, 2 * my_chip + (1 - c), (x, y, c)).wait_recv()
            for j, (chip, chip_idx) in enumerate(others):
                copy(a, 4 + j, 2 * chip_idx + (1 - c), (x, y, c)).wait_recv()
        for cp in first + passed:
            cp.wait_send()
        for cp in local:
            cp.wait()

    out_shape = [jax.ShapeDtypeStruct((N_DEV,) + a.shape, a.dtype) for a in arrays]
    return _comm_call(body, arrays, out_shape, N_DEV - 1, "gather_weights")


def _exchange_sibling(arrays):
    n = len(arrays)

    def body(*refs):
        ins, got = refs[:n], refs[n:2 * n]
        send_sems, recv_sems, _ = refs[2 * n:]
        x, y, c, me, my_chip, sibling, others = _place()
        remote = []
        for a in range(n):
            for chip in range(N_CHIP):
                rc = pltpu.make_async_remote_copy(
                    src_ref=ins[a].at[2 * chip + (1 - c)], dst_ref=got[a].at[chip], send_sem=send_sems.at[a, chip],
                    recv_sem=recv_sems.at[a, chip], device_id=sibling, device_id_type=MESH)
                rc.start()
                remote.append(rc)
        for rc in remote:
            rc.wait()

    half = [jax.ShapeDtypeStruct((N_CHIP,) + a.shape[1:], a.dtype) for a in arrays]
    return _comm_call(body, arrays, half, N_CHIP, "exchange_sibling")


def _exchange_chips(arrays):
    n = len(arrays)

    def body(*refs):
        ins, outs = refs[:n], refs[n:2 * n]
        send_sems, recv_sems, local_sems = refs[2 * n:]
        x, y, c, me, my_chip, sibling, others = _place()
        local, remote = [], []
        for a in range(n):
            cp = pltpu.make_async_copy(ins[a].at[my_chip], outs[a].at[my_chip], local_sems.at[a, 0])
            cp.start()
            local.append(cp)
            for j, (chip, chip_idx) in enumerate(others):
                rc = pltpu.make_async_remote_copy(
                    src_ref=ins[a].at[chip_idx], dst_ref=outs[a].at[my_chip], send_sem=send_sems.at[a, j],
                    recv_sem=recv_sems.at[a, j], device_id=(*chip, c), device_id_type=MESH)
                rc.start()
                remote.append(rc)
        for rc in remote:
            rc.wait()
        for cp in local:
            cp.wait()

    out_shape = [jax.ShapeDtypeStruct(a.shape, a.dtype) for a in arrays]
    return _comm_call(body, arrays, out_shape, N_CHIP - 1, "exchange_chips")


def _as_rows(a, lead):
    shp = a.shape
    return a.reshape(shp[:lead] + (math.prod(shp[lead:-1]), shp[-1]))


def _row_tile(rows, cap=512):
    if rows <= cap:
        return rows
    for t in range(cap - cap % 8, 0, -8):
        if rows % t == 0:
            return t
    return rows


def _pair_sum(send, got, name):
    a3, b3 = _as_rows(send, 1), _as_rows(got, 1)
    _, rows, last = b3.shape
    tr = _row_tile(rows)

    def body(c_ref, a_ref, b_ref, o_ref):
        o_ref[...] = (a_ref[...].astype(F32) + b_ref[...].astype(F32)).astype(o_ref.dtype)

    core = lax.axis_index("c").astype(jnp.int32).reshape(1)
    out = _pcall(
        body, name=name,
        grid_spec=pltpu.PrefetchScalarGridSpec(
            num_scalar_prefetch=1, grid=(N_CHIP, rows // tr),
            in_specs=[pl.BlockSpec((None, tr, last), lambda k, i, c_ref: (2 * k + c_ref[0], i, 0)),
                      pl.BlockSpec((None, tr, last), lambda k, i, c_ref: (k, i, 0))],
            out_specs=pl.BlockSpec((None, tr, last), lambda k, i, c_ref: (k, i, 0))),
        out_shape=jax.ShapeDtypeStruct(b3.shape, got.dtype), compiler_params=_params(("parallel", "parallel")),
    )(core, a3, b3)
    return out.reshape(got.shape)


def _adamw(parts, w, m, v, name):
    p3 = _as_rows(parts, 1)
    w2, m2, v2 = (_as_rows(z, 0) for z in (w, m, v))
    rows, last = w2.shape
    tr = _row_tile(rows)
    c1 = 1.0 / (1.0 - ADAM_B1 ** ADAM_STEP)
    c2 = 1.0 / (1.0 - ADAM_B2 ** ADAM_STEP)

    def body(p_ref, w_ref, m_ref, v_ref, g_ref, d_ref, nm_ref, nv_ref):
        g = p_ref[0].astype(F32)
        for chip in range(1, N_CHIP):
            g = g + p_ref[chip].astype(F32)
        mn = ADAM_B1 * m_ref[...] + (1.0 - ADAM_B1) * g
        vn = ADAM_B2 * v_ref[...] + (1.0 - ADAM_B2) * (g * g)
        g_ref[...] = g
        nm_ref[...] = mn
        nv_ref[...] = vn
        d_ref[...] = -ADAM_LR * ((mn * c1) / (jnp.sqrt(vn * c2) + ADAM_EPS) + ADAM_WD * w_ref[...])

    row = pl.BlockSpec((tr, last), lambda i: (i, 0))
    sh = jax.ShapeDtypeStruct((rows, last), F32)
    outs = _pcall(body, name=name, grid=(rows // tr,),
                  in_specs=[pl.BlockSpec((N_CHIP, tr, last), lambda i: (0, i, 0)), row, row, row],
                  out_specs=[row, row, row, row], out_shape=[sh, sh, sh, sh],
                  compiler_params=_params(("parallel",)))(p3, w2, m2, v2)
    return [o.reshape(w.shape) for o in outs]


def _pack(pieces, row_align):
    rows, offs, r = [], [], 0
    for a in pieces:
        flat = a.reshape(-1)
        nr = -(-flat.shape[0] // PACK_W)
        flat = jnp.pad(flat, (0, nr * PACK_W - flat.shape[0]))
        rows.append(flat.reshape(nr, PACK_W))
        offs.append(r)
        r += nr
    pad = (-r) % row_align
    if pad:
        rows.append(jnp.zeros((pad, PACK_W), pieces[0].dtype))
    return jnp.concatenate(rows, axis=0), offs


def _unpack(flat, offs, shapes):
    out = []
    for off, shp in zip(offs, shapes):
        size = math.prod(shp)
        nr = -(-size // PACK_W)
        out.append(flat[..., off:off + nr, :].reshape(flat.shape[:-2] + (nr * PACK_W,))[..., :size].reshape(flat.shape[:-2] + tuple(shp)))
    return out


def _to_full(gathered, axis):
    z = jnp.moveaxis(gathered, 0, axis)
    shp = list(z.shape)
    return z.reshape(shp[:axis] + [shp[axis] * shp[axis + 1]] + shp[axis + 2:])


def _to_shards(full, axis):
    shp = list(full.shape)
    z = full.reshape(shp[:axis] + [N_DEV, shp[axis] // N_DEV] + shp[axis + 1:])
    return jnp.moveaxis(z, axis, 0)


def kernel(x, p, ffn_norm, ffn_w_gate, ffn_w_up, ffn_w_down, mix_norm, att_w_in, att_q_norm, att_k_norm, att_sinks, att_w_out, gdn_w_in, gdn_conv_w, gdn_a_log, gdn_dt_bias, gdn_out_norm, gdn_w_out, ple_norm, ple_w_gate, ple_w_proj, loss_target, m_ffn_norm, m_ffn_w_gate, m_ffn_w_up, m_ffn_w_down, m_mix_norm, m_att_w_in, m_att_q_norm, m_att_k_norm, m_att_sinks, m_att_w_out, m_gdn_w_in, m_gdn_conv_w, m_gdn_a_log, m_gdn_dt_bias, m_gdn_out_norm, m_gdn_w_out, m_ple_norm, m_ple_w_gate, m_ple_w_proj, v_ffn_norm, v_ffn_w_gate, v_ffn_w_up, v_ffn_w_down, v_mix_norm, v_att_w_in, v_att_q_norm, v_att_k_norm, v_att_sinks, v_att_w_out, v_gdn_w_in, v_gdn_conv_w, v_gdn_a_log, v_gdn_dt_bias, v_gdn_out_norm, v_gdn_w_out, v_ple_norm, v_ple_w_gate, v_ple_w_proj):
    args = dict(locals())
    wts = {n: args[n] for n in WEIGHTS}
    mom = {n: args["m_" + n] for n in WEIGHTS}
    var = {n: args["v_" + n] for n in WEIGHTS}
    axis = dict(SHARDED)
    vecs = [n for n, _ in SHARDED[:SMALL_SHARDED]]
    mats = [n for n, _ in SHARDED[SMALL_SHARDED:]]
    small = vecs + list(REPLICATED)
    small_shapes = [wts[n].shape for n in small]

    vec_pack, voffs = _pack([wts[n] for n in vecs], 8)
    gathered = _all_gather([wts[n].astype(BF16) for n in mats] + [vec_pack])
    full = {n: _to_full(g, axis[n]) for n, g in zip(mats, gathered)}
    for n, piece in zip(vecs, _unpack(gathered[-1], voffs, [wts[n].shape for n in vecs])):
        full[n] = _to_full(piece, axis[n])
    for n in REPLICATED:
        full[n] = wts[n]

    loss, grad_x, grads = _local_step(x[0], p[:, 0], loss_target[0], full)
    loss = lax.psum(loss, ("x", "y", "c"))

    vec_shards = [_to_shards(grads[n], axis[n]) for n in vecs]
    small_send = jnp.stack([_pack([s[d] for s in vec_shards] + [grads[n] for n in REPLICATED], 8)[0] for d in range(N_DEV)])
    send = [_to_shards(grads[n], axis[n]) for n in mats] + [small_send]
    got = _exchange_sibling(send)
    chip_sums = [_pair_sum(p_, g, f"pair_sum_{i}") for i, (p_, g) in enumerate(zip(send, got))]
    parts = _exchange_chips(chip_sums)

    outs = {}
    for n, part in zip(mats, parts):
        outs[n] = _adamw(part, wts[n], mom[n], var[n], f"adamw_{n}")
    small_w, soffs = _pack([wts[n] for n in small], 8)
    small_m, _ = _pack([mom[n] for n in small], 8)
    small_v, _ = _pack([var[n] for n in small], 8)
    small_out = [_unpack(z, soffs, small_shapes) for z in _adamw(parts[-1], small_w, small_m, small_v, "adamw_small")]
    for i, n in enumerate(small):
        outs[n] = [small_out[k][i] for k in range(4)]
    result = [loss, grad_x[None]]
    for k in range(4):
        result += [outs[n][k] for n in WEIGHTS]
    return tuple(result)
```

```python
import math

import jax
import jax.numpy as jnp
from jax import lax
from jax.experimental import pallas as pl
from jax.experimental.pallas import tpu as pltpu

F32 = jnp.float32
BF16 = jnp.bfloat16

N_DEV = 8
D_MODEL = 1024
D_FF = 2816
PLE_DIM = 256
HEAD_DIM = 64
SB_HEADS = 8
SWA_HEADS = 8
SWA_KV_HEADS = 2
SWA_GROUP = SWA_HEADS // SWA_KV_HEADS
WINDOW = 128
Q_BLOCK = 128
GDN_K_HEADS = 8
GDN_V_HEADS = 16
GDN_HEAD_DIM = 128
GDN_CONV = 4
GDN_CHUNK = 64
EPS = 1e-6
SB_W = SB_HEADS * HEAD_DIM
SWA_QW = SWA_HEADS * HEAD_DIM
SWA_KVW = SWA_KV_HEADS * HEAD_DIM
ATT_IN = 3 * SB_W + SWA_QW + 2 * SWA_KVW
GDN_KW = GDN_K_HEADS * GDN_HEAD_DIM
GDN_VW = GDN_V_HEADS * GDN_HEAD_DIM
GDN_CONV_W = 2 * GDN_KW + GDN_VW
GDN_IN = GDN_CONV_W + GDN_VW + 2 * GDN_V_HEADS
GDN_IN_PAD = GDN_CONV_W + GDN_VW + 2 * 128

ADAM_LR = 0.001
ADAM_B1 = 0.9
ADAM_B2 = 0.999
ADAM_EPS = 1e-08
ADAM_WD = 0.01
ADAM_STEP = 10

LANE = 128
VMEM_LIMIT = 56 * 1024 * 1024
PACK_W = 1024

NN = ((1,), (0,))
NT = ((1,), (1,))
TN = ((0,), (0,))

SHARDED = (
    ("ffn_norm", 2), ("gdn_conv_w", 2),
    ("ffn_w_gate", 3), ("ffn_w_up", 3), ("ffn_w_down", 2), ("att_w_in", 2), ("att_w_out", 1),
    ("gdn_w_in", 2), ("gdn_w_out", 1), ("ple_w_gate", 1), ("ple_w_proj", 2),
)
SMALL_SHARDED = 2
REPLICATED = ("mix_norm", "att_q_norm", "att_k_norm", "att_sinks", "gdn_a_log", "gdn_dt_bias",
              "gdn_out_norm", "ple_norm")
WEIGHTS = ("ffn_norm", "ffn_w_gate", "ffn_w_up", "ffn_w_down", "mix_norm", "att_w_in", "att_q_norm",
           "att_k_norm", "att_sinks", "att_w_out", "gdn_w_in", "gdn_conv_w", "gdn_a_log", "gdn_dt_bias",
           "gdn_out_norm", "gdn_w_out", "ple_norm", "ple_w_gate", "ple_w_proj")


_FFN_REST = [(0, 1), (1, 0), (1, 1)]
EARLY = [("ffn_w_gate", [(0, 0)]), ("ffn_w_up", [(0, 0)]), ("ffn_w_down", [(0, 0)]), ("att_w_in", [()])]
LATE = [("ffn_w_gate", _FFN_REST), ("ffn_w_up", _FFN_REST), ("ffn_w_down", _FFN_REST), ("att_w_out", [()]),
        ("gdn_w_in", [()]), ("gdn_w_out", [()]), ("ple_w_gate", [(0,), (1,)]), ("ple_w_proj", [(0,), (1,)])]
RIDE = [e for e in LATE if e[0] != "att_w_out"]
FINAL = EARLY + [("att_w_out", [()])]


def _pcall(body, **kw):
    return pl.pallas_call(body, **kw)


def _params(sem=None):
    if sem is None:
        return pltpu.CompilerParams(vmem_limit_bytes=VMEM_LIMIT)
    return pltpu.CompilerParams(dimension_semantics=sem, vmem_limit_bytes=VMEM_LIMIT)


def _dot(a, b, dims=NN):
    return lax.dot_general(a, b, (dims, ((), ())), preferred_element_type=F32)


def _bdot(a, b, dims=NN):
    return _dot(a.astype(BF16), b.astype(BF16), dims)


def _split(a):
    hi = a.astype(BF16)
    lo = (a - hi.astype(F32)).astype(BF16)
    return hi, lo


def _dot3(a, b, dims=NN):
    ah, al = _split(a)
    bh, bl = _split(b)
    return _dot(ah, bh, dims) + (_dot(ah, bl, dims) + _dot(al, bh, dims))


def _dot2m(a, m, dims=NN):
    ah, al = _split(a)
    return _dot(ah, m, dims) + _dot(al, m, dims)


def _mdot2(m, a, dims=NN):
    ah, al = _split(a)
    return _dot(m, ah, dims) + _dot(m, al, dims)


def _sigmoid(x):
    return 1.0 / (1.0 + jnp.exp(-x))


def _softplus(x):
    return jnp.maximum(x, 0.0) + jnp.log(1.0 + jnp.exp(-jnp.abs(x)))


def _pick(n, cap):
    if n <= cap:
        return n
    for t in range(cap - cap % LANE, 0, -LANE):
        if n % t == 0:
            return t
    raise ValueError(f"no tile for {n} under {cap}")


def _iota2(shape, axis):
    return lax.broadcasted_iota(jnp.int32, shape, axis)


def _mm(a, b, mode, out_dtype=F32, res=None, alpha=1.0, a2=None, b2=None, name="mm"):
    if mode == "nn":
        (M, K), N = a.shape, b.shape[1]
    elif mode == "nt":
        (M, K), N = a.shape, b.shape[0]
    else:
        (K, M), N = a.shape, b.shape[1]
    tm, tn, tk = _pick(M, 1408 if mode == "tn" else 512), _pick(N, 1408), _pick(K, 1024 if mode == "tn" else 1408)
    nk = K // tk
    dims = {"nn": NN, "nt": NT, "tn": TN}[mode]
    a_spec = pl.BlockSpec((tk, tm), lambda i, j, k: (k, i)) if mode == "tn" else pl.BlockSpec((tm, tk), lambda i, j, k: (i, k))
    b_spec = pl.BlockSpec((tn, tk), lambda i, j, k: (j, k)) if mode == "nt" else pl.BlockSpec((tk, tn), lambda i, j, k: (k, j))
    o_spec = pl.BlockSpec((tm, tn), lambda i, j, k: (i, j))
    two = a2 is not None
    has_res = res is not None

    def body(*refs):
        refs = list(refs)
        a_ref, b_ref = refs[0], refs[1]
        pos = 2
        if two:
            a2_ref, b2_ref = refs[2], refs[3]
            pos = 4
        if has_res:
            res_ref = refs[pos]
            pos += 1
        o_ref, acc_ref = refs[pos], refs[pos + 1]
        k = pl.program_id(2)
        part = _bdot(a_ref[...], b_ref[...], dims)
        if two:
            part = part + _bdot(a2_ref[...], b2_ref[...], dims)

        def finish(acc):
            out = acc * alpha if alpha != 1.0 else acc
            if has_res:
                out = res_ref[...] + out
            o_ref[...] = out.astype(out_dtype)

        if nk == 1:
            finish(part)
        else:
            @pl.when(k == 0)
            def _():
                acc_ref[...] = part

            @pl.when(k > 0)
            def _():
                acc_ref[...] += part

            @pl.when(k == nk - 1)
            def _():
                finish(acc_ref[...])

    ins = [a, b]
    specs = [a_spec, b_spec]
    if two:
        ins += [a2, b2]
        specs += [a_spec, b_spec]
    if has_res:
        ins.append(res)
        specs.append(o_spec)
    return _pcall(
        body, name=name, grid=(M // tm, N // tn, nk), in_specs=specs, out_specs=o_spec,
        out_shape=jax.ShapeDtypeStruct((M, N), out_dtype),
        scratch_shapes=[pltpu.VMEM((tm, tn) if nk > 1 else (8, LANE), F32)],
        compiler_params=_params(("parallel", "parallel", "arbitrary")),
    )(*ins)


ROW_TILE = 256


def _rms_fwd(h, g, name):
    T, D = h.shape
    tr = _pick(T, ROW_TILE)

    def body(h_ref, g_ref, n_ref):
        x = h_ref[...]
        r = lax.rsqrt(jnp.mean(x * x, axis=-1, keepdims=True) + EPS)
        n_ref[...] = (x * r * g_ref[...]).astype(BF16)

    return _pcall(
        body, name=name, grid=(T // tr,),
        in_specs=[pl.BlockSpec((tr, D), lambda i: (i, 0)), pl.BlockSpec((1, D), lambda i: (0, 0))],
        out_specs=pl.BlockSpec((tr, D), lambda i: (i, 0)),
        out_shape=jax.ShapeDtypeStruct((T, D), BF16), compiler_params=_params(("parallel",)),
    )(h, g.reshape(1, D))


def _rms_bwd(dn, h, g, dres, name):
    T, D = h.shape
    tr = _pick(T, ROW_TILE)

    def body(dn_ref, h_ref, g_ref, dres_ref, dh_ref, dg_ref):
        x = h_ref[...]
        r = lax.rsqrt(jnp.mean(x * x, axis=-1, keepdims=True) + EPS)
        xh = x * r
        d = dn_ref[...].astype(F32)
        dxh = d * g_ref[...]
        dh_ref[...] = dres_ref[...] + r * (dxh - xh * jnp.mean(dxh * xh, axis=-1, keepdims=True))
        part = jnp.sum(d * xh, axis=0, keepdims=True)

        @pl.when(pl.program_id(0) == 0)
        def _():
            dg_ref[...] = part

        @pl.when(pl.program_id(0) > 0)
        def _():
            dg_ref[...] += part

    row = pl.BlockSpec((tr, D), lambda i: (i, 0))
    vec = pl.BlockSpec((1, D), lambda i: (0, 0))
    dh, dg = _pcall(
        body, name=name, grid=(T // tr,), in_specs=[row, row, vec, row], out_specs=[row, vec],
        out_shape=[jax.ShapeDtypeStruct((T, D), F32), jax.ShapeDtypeStruct((1, D), F32)],
        compiler_params=_params(("arbitrary",)),
    )(dn, h, g.reshape(1, D), dres)
    return dh, dg.reshape(D)


def _gateup(n, wg, wu, name):
    T, D = n.shape
    F = wg.shape[1]
    tm, tn = _pick(T, 512), _pick(F, 1408)

    def body(n_ref, wg_ref, wu_ref, a_ref, b_ref, hid_ref):
        x = n_ref[...]
        a = _dot(x, wg_ref[...])
        b = _dot(x, wu_ref[...])
        a_ref[...] = a.astype(BF16)
        b_ref[...] = b.astype(BF16)
        hid_ref[...] = (a * _sigmoid(a) * b).astype(BF16)

    o_spec = pl.BlockSpec((tm, tn), lambda i, j: (i, j))
    w_spec = pl.BlockSpec((D, tn), lambda i, j: (0, j))
    sh = jax.ShapeDtypeStruct((T, F), BF16)
    return _pcall(
        body, name=name, grid=(T // tm, F // tn),
        in_specs=[pl.BlockSpec((tm, D), lambda i, j: (i, 0)), w_spec, w_spec],
        out_specs=[o_spec, o_spec, o_spec], out_shape=[sh, sh, sh],
        compiler_params=_params(("parallel", "parallel")),
    )(n, wg, wu)


def _ffn_dhid(dy, wd, a, b, name):
    T, D = dy.shape
    F = wd.shape[0]
    tm, tn = _pick(T, 512), _pick(F, 1408)

    def body(dy_ref, wd_ref, a_ref, b_ref, da_ref, db_ref):
        dhid = 0.5 * _bdot(dy_ref[...], wd_ref[...], NT)
        av = a_ref[...].astype(F32)
        bv = b_ref[...].astype(F32)
        s = _sigmoid(av)
        da_ref[...] = (dhid * bv * s * (1.0 + av * (1.0 - s))).astype(BF16)
        db_ref[...] = (dhid * av * s).astype(BF16)

    o_spec = pl.BlockSpec((tm, tn), lambda i, j: (i, j))
    sh = jax.ShapeDtypeStruct((T, F), BF16)
    return _pcall(
        body, name=name, grid=(T // tm, F // tn),
        in_specs=[pl.BlockSpec((tm, D), lambda i, j: (i, 0)), pl.BlockSpec((tn, D), lambda i, j: (j, 0)), o_spec, o_spec],
        out_specs=[o_spec, o_spec], out_shape=[sh, sh],
        compiler_params=_params(("parallel", "parallel")),
    )(dy, wd, a, b)


def _ffn_fwd(h, g, wg, wu, wd, tag):
    n = _rms_fwd(h, g, f"{tag}_norm")
    a, b, hid = _gateup(n, wg, wu, f"{tag}_gateup")
    h2 = _mm(hid, wd, "nn", res=h, alpha=0.5, name=f"{tag}_down")
    return h2, (h, n, a, b, hid)


def _ffn_bwd(dh2, saved, g, wg, wu, wd, tag):
    h, n, a, b, hid = saved
    da, db = _ffn_dhid(dh2, wd, a, b, f"{tag}_dhid")
    dwd = _mm(hid, dh2, "tn", alpha=0.5, out_dtype=BF16, name=f"{tag}_dwd")
    dwg = _mm(n, da, "tn", out_dtype=BF16, name=f"{tag}_dwg")
    dwu = _mm(n, db, "tn", out_dtype=BF16, name=f"{tag}_dwu")
    dn = _mm(da, wg, "nt", a2=db, b2=wu, name=f"{tag}_dn")
    dh, dg = _rms_bwd(dn, h, g, dh2, f"{tag}_dnorm")
    return dh, dg, dwg, dwu, dwd


def _ple_fwd(h, p, g, w_gate, w_proj, tag):
    T, D = h.shape
    pn = _rms_fwd(h, g, f"{tag}_norm")
    tm, tn = _pick(T, 512), _pick(D, 1024)
    P = p.shape[1]

    def body(pn_ref, p_ref, wg_ref, wp_ref, h_ref, o_ref, gl_ref, pp_ref):
        gl = _dot(pn_ref[...], wg_ref[...])
        pp = _bdot(p_ref[...], wp_ref[...])
        gl_ref[...] = gl
        pp_ref[...] = pp
        o_ref[...] = h_ref[...] + _sigmoid(gl) * pp

    o_spec = pl.BlockSpec((tm, tn), lambda i, j: (i, j))
    sh = jax.ShapeDtypeStruct((T, D), F32)
    h2, gl, pp = _pcall(
        body, name=f"{tag}_fwd", grid=(T // tm, D // tn),
        in_specs=[pl.BlockSpec((tm, D), lambda i, j: (i, 0)), pl.BlockSpec((tm, P), lambda i, j: (i, 0)),
                  pl.BlockSpec((D, tn), lambda i, j: (0, j)), pl.BlockSpec((P, tn), lambda i, j: (0, j)), o_spec],
        out_specs=[o_spec, o_spec, o_spec], out_shape=[sh, sh, sh],
        compiler_params=_params(("parallel", "parallel")),
    )(pn, p, w_gate, w_proj, h)
    return h2, (h, pn, gl, pp)


def _ple_bwd(dh2, saved, p, g, w_gate, tag):
    h, pn, gl, pp = saved
    T, D = h.shape
    tr = _pick(T, ROW_TILE)

    def body(d_ref, gl_ref, pp_ref, dgl_ref, dpp_ref):
        d = d_ref[...]
        s = _sigmoid(gl_ref[...])
        dpp_ref[...] = (d * s).astype(BF16)
        dgl_ref[...] = (d * pp_ref[...] * s * (1.0 - s)).astype(BF16)

    row = pl.BlockSpec((tr, D), lambda i: (i, 0))
    sh = jax.ShapeDtypeStruct((T, D), BF16)
    dgl, dpp = _pcall(body, name=f"{tag}_dgate", grid=(T // tr,), in_specs=[row, row, row], out_specs=[row, row],
                      out_shape=[sh, sh], compiler_params=_params(("parallel",)))(dh2, gl, pp)
    dw_proj = _mm(p, dpp, "tn", out_dtype=BF16, name=f"{tag}_dwproj")
    dw_gate = _mm(pn, dgl, "tn", out_dtype=BF16, name=f"{tag}_dwgate")
    dpn = _mm(dgl, w_gate, "nt", name=f"{tag}_dpn")
    dh, dg = _rms_bwd(dpn, h, g, dh2, f"{tag}_dnorm")
    return dh, dg, dw_gate, dw_proj


def _loss_head(y, target):
    T, D = y.shape
    tr = _pick(T, ROW_TILE)

    def body(y_ref, t_ref, dy_ref, l_ref):
        e = y_ref[...] - t_ref[...]
        dy_ref[...] = e * (1.0 / D)
        part = jnp.sum(e * e, axis=0, keepdims=True)

        @pl.when(pl.program_id(0) == 0)
        def _():
            l_ref[...] = part

        @pl.when(pl.program_id(0) > 0)
        def _():
            l_ref[...] += part

    row = pl.BlockSpec((tr, D), lambda i: (i, 0))
    vec = pl.BlockSpec((1, D), lambda i: (0, 0))
    dy, l = _pcall(body, name="loss_head", grid=(T // tr,), in_specs=[row, row], out_specs=[row, vec],
                   out_shape=[jax.ShapeDtypeStruct((T, D), F32), jax.ShapeDtypeStruct((1, D), F32)],
                   compiler_params=_params(("arbitrary",)))(y, target)
    return (0.5 / D) * jnp.sum(l), dy


SB_GROUP_FWD = 8
SB_GROUP_BWD = 4


def _sb_consts():
    row = _iota2((Q_BLOCK, Q_BLOCK), 0)
    col = _iota2((Q_BLOCK, Q_BLOCK), 1)
    after = (row > col).astype(BF16)
    before = (row < col).astype(BF16)
    return col < row, after, before, col


def _ride_specs(ride, out_shapes, n_sems):
    hbm = pl.BlockSpec(memory_space=pl.ANY)
    n = len(ride)
    sems = [pltpu.SemaphoreType.DMA((n, n_sems)), pltpu.SemaphoreType.DMA((n, n_sems)),
            pltpu.SemaphoreType.DMA((n, N_CHIP))] if n else []
    return [hbm] * n, [hbm] * len(out_shapes), sems


def _sb_fwd(q, k, v, ride=()):
    H, T, d = q.shape
    nblk = T // Q_BLOCK
    scale = d ** -0.5
    G = SB_GROUP_FWD
    n = len(ride)
    ride_out = [jax.ShapeDtypeStruct((N_DEV,) + a.shape, a.dtype) for a in ride]
    ride_in_specs, ride_out_specs, ride_sems = _ride_specs(ride, ride_out, N_DEV - 1)

    def body(*refs):
        q_ref, k_ref, v_ref = refs[:3]
        rin = refs[3:3 + n]
        o_ref, c_ref = refs[3 + n:5 + n]
        rout = refs[5 + n:5 + 2 * n]
        run_ref = refs[5 + 2 * n]
        i = pl.program_id(1)
        if n:
            start, finish = _gather_protocol(rin, rout, *refs[6 + 2 * n:])
            pl.when((pl.program_id(0) == 0) & (i == 0))(start)
        causal, after, _, col = _sb_consts()
        qs = [q_ref[g] * scale for g in range(G)]
        o_ref[...] = jnp.zeros_like(o_ref)
        c_ref[...] = jnp.zeros_like(c_ref)
        run_ref[...] = jnp.zeros_like(run_ref)

        def pair(j, diag):
            off = pl.multiple_of(j * Q_BLOCK, Q_BLOCK)
            R = range(G)
            kj = [k_ref[g, pl.ds(off, Q_BLOCK), :] for g in R]
            vj = [v_ref[g, pl.ds(off, Q_BLOCK), :] for g in R]
            c = [run_ref[g] for g in R]
            acc = [o_ref[g] for g in R]
            cm = None if diag else [c_ref[g] for g in R]
            z = [_dot(qs[g], kj[g], NT) for g in R]
            sp = [_softplus(z[g]) for g in R]
            lk = [jnp.where(causal, -sp[g], 0.0) if diag else -sp[g] for g in R]
            btw = [_dot2m(lk[g], after) for g in R]
            e = [jnp.exp((z[g] - sp[g]) + btw[g] + c[g]) for g in R]
            w = [jnp.where(causal, e[g], 0.0) if diag else e[g] for g in R]
            pv = [_bdot(w[g], vj[g]) for g in R]
            rs = [jnp.sum(lk[g], axis=1, keepdims=True) for g in R]
            for g in R:
                o_ref[g] = acc[g] + pv[g]
                if not diag:
                    c_ref[g] = jnp.where(col == j, c[g], cm[g])
                run_ref[g] = c[g] + rs[g]

        pair(i, True)

        @pl.loop(0, i)
        def _(jj):
            pair(i - 1 - jj, False)

        if n:
            pl.when((pl.program_id(0) == H // G - 1) & (i == nblk - 1))(finish)

    blk = pl.BlockSpec((G, Q_BLOCK, d), lambda h, i: (h, i, 0))
    full = pl.BlockSpec((G, T, d), lambda h, i: (h, 0, 0))
    res = _pcall(
        body, name="sb_fwd", grid=(H // G, nblk), in_specs=[blk, full, full] + ride_in_specs,
        out_specs=[blk, pl.BlockSpec((G, Q_BLOCK, LANE), lambda h, i: (h, i, 0))] + ride_out_specs,
        out_shape=[jax.ShapeDtypeStruct((H, T, d), F32), jax.ShapeDtypeStruct((H, T, LANE), F32)] + ride_out,
        scratch_shapes=[pltpu.VMEM((G, Q_BLOCK, 1), F32)] + ride_sems,
        compiler_params=_params(("arbitrary", "arbitrary") if n else ("parallel", "parallel")),
    )(q, k, v, *ride)
    return res[0], res[1], list(res[2:])


def _sb_bwd(q, k, v, carry, do, ride=()):
    H, T, d = q.shape
    nblk = T // Q_BLOCK
    scale = d ** -0.5
    G = SB_GROUP_BWD
    n = len(ride)
    ride_out = [jax.ShapeDtypeStruct(a.shape, a.dtype) for a in ride]
    ride_in_specs, ride_out_specs, ride_sems = _ride_specs(ride, ride_out, N_CHIP - 1)

    def body(*refs):
        q_ref, k_ref, v_ref, c_ref, do_ref = refs[:5]
        rin = refs[5:5 + n]
        dq_ref, dk_ref, dv_ref = refs[5 + n:8 + n]
        rout = refs[8 + n:8 + 2 * n]
        run_ref = refs[8 + 2 * n]
        i = pl.program_id(1)
        if n:
            start, finish = _chips_protocol(rin, rout, *refs[9 + 2 * n:])
            pl.when((pl.program_id(0) == 0) & (i == 0))(start)

        @pl.when(i == 0)
        def _():
            dk_ref[...] = jnp.zeros_like(dk_ref)
            dv_ref[...] = jnp.zeros_like(dv_ref)

        causal, after, before, col = _sb_consts()
        qs = [q_ref[g] * scale for g in range(G)]
        dov = [do_ref[g].astype(BF16) for g in range(G)]
        dq_ref[...] = jnp.zeros_like(dq_ref)
        run_ref[...] = jnp.zeros_like(run_ref)

        def pair(j, diag):
            off = pl.multiple_of(j * Q_BLOCK, Q_BLOCK)
            R = range(G)
            rows = pl.ds(off, Q_BLOCK)
            kj = [k_ref[g, rows, :] for g in R]
            vj = [v_ref[g, rows, :] for g in R]
            gsum = [run_ref[g] for g in R]
            dq0 = [dq_ref[g] for g in R]
            dk0 = [dk_ref[g, rows, :] for g in R]
            dv0 = [dv_ref[g, rows, :] for g in R]
            cm = None if diag else [c_ref[g] for g in R]
            z = [_dot(qs[g], kj[g], NT) for g in R]
            sp = [_softplus(z[g]) for g in R]
            lk = [jnp.where(causal, -sp[g], 0.0) if diag else -sp[g] for g in R]
            ls = [z[g] - sp[g] for g in R]
            logw = [ls[g] + _dot2m(lk[g], after) for g in R]
            if not diag:
                logw = [logw[g] + jnp.sum(jnp.where(col == j, cm[g], 0.0), axis=1, keepdims=True) for g in R]
            e = [jnp.exp(logw[g]) for g in R]
            w = [jnp.where(causal, e[g], 0.0) if diag else e[g] for g in R]
            gw = [_dot(dov[g], vj[g], NT) * w[g] for g in R]
            gpre = [gsum[g] + _dot2m(gw[g], before) for g in R]
            sig = [jnp.exp(ls[g]) for g in R]
            dz = [gw[g] * (1.0 - sig[g]) - sig[g] * gpre[g] for g in R]
            if diag:
                dz = [jnp.where(causal, dz[g], 0.0) for g in R]
            dzb = [dz[g].astype(BF16) for g in R]
            dq1 = [_dot(dzb[g], kj[g]) for g in R]
            dk1 = [_dot(dzb[g], qs[g], TN) for g in R]
            dv1 = [_dot(w[g].astype(BF16), dov[g], TN) for g in R]
            gs1 = [jnp.sum(gw[g], axis=1, keepdims=True) for g in R]
            for g in R:
                dq_ref[g] = dq0[g] + dq1[g]
                dk_ref[g, rows, :] = dk0[g] + dk1[g]
                dv_ref[g, rows, :] = dv0[g] + dv1[g]
                run_ref[g] = gsum[g] + gs1[g]

        @pl.loop(0, i)
        def _(j):
            pair(j, False)

        pair(i, True)
        dq_ref[...] = dq_ref[...] * scale
        if n:
            pl.when((pl.program_id(0) == H // G - 1) & (i == nblk - 1))(finish)

    blk = pl.BlockSpec((G, Q_BLOCK, d), lambda h, i: (h, i, 0))
    full = pl.BlockSpec((G, T, d), lambda h, i: (h, 0, 0), pipeline_mode=pl.Buffered(1))
    sh = jax.ShapeDtypeStruct((H, T, d), F32)
    res = _pcall(
        body, name="sb_bwd", grid=(H // G, nblk),
        in_specs=[blk, full, full, pl.BlockSpec((G, Q_BLOCK, LANE), lambda h, i: (h, i, 0)), blk] + ride_in_specs,
        out_specs=[blk, full, full] + ride_out_specs, out_shape=[sh, sh, sh] + ride_out,
        scratch_shapes=[pltpu.VMEM((G, Q_BLOCK, 1), F32)] + ride_sems,
        compiler_params=_params(("arbitrary", "arbitrary") if n else ("parallel", "arbitrary")),
    )(q, k, v, carry, do, *ride)
    return res[0], res[1], res[2], list(res[3:])


def _swa_fwd(q, k, v, qg, kg, sinks, slopes):
    Hq, T, d = q.shape
    Hkv = k.shape[0]
    G = Hq // Hkv
    W = WINDOW
    nblk = T // W
    scale = d ** -0.5

    def body(q_ref, kp_ref, kc_ref, vp_ref, vc_ref, qg_ref, kg_ref, sk_ref, sl_ref, o_ref):
        hk = pl.program_id(0)
        n = pl.program_id(1)
        kcat = jnp.concatenate([kp_ref[...], kc_ref[...]], axis=0)
        vcat = jnp.concatenate([vp_ref[...], vc_ref[...]], axis=0).astype(BF16)
        rk = lax.rsqrt(jnp.mean(kcat * kcat, axis=-1, keepdims=True) + EPS)
        kn = (kcat * rk * kg_ref[...]).astype(BF16)
        row = _iota2((W, 2 * W), 0)
        col = _iota2((W, 2 * W), 1)
        dist = row + W - col
        valid = (dist >= 0) & (dist < W) & ((n > 0) | (col >= W))
        distf = dist.astype(F32)
        for g in range(G):
            qh = q_ref[g]
            rq = lax.rsqrt(jnp.mean(qh * qh, axis=-1, keepdims=True) + EPS)
            qn = (qh * rq * qg_ref[...]).astype(BF16)
            sink = sk_ref[pl.ds(hk * G + g, 1), :][:, :1]
            slope = sl_ref[pl.ds(hk * G + g, 1), :][:, :1]
            s = _dot(qn, kn, NT) * scale - slope * distf
            s = jnp.where(valid, s, -1e30)
            m = jnp.maximum(jnp.max(s, axis=1, keepdims=True), sink)
            p = jnp.where(valid, jnp.exp(s - m), 0.0)
            den = jnp.sum(p, axis=1, keepdims=True) + jnp.exp(sink - m)
            o_ref[g] = _bdot(p / den, vcat)

    qblk = pl.BlockSpec((G, W, d), lambda h, n: (h, n, 0))
    prev = pl.BlockSpec((None, W, d), lambda h, n: (h, jnp.maximum(n - 1, 0), 0))
    cur = pl.BlockSpec((None, W, d), lambda h, n: (h, n, 0))
    gain = pl.BlockSpec((1, d), lambda h, n: (0, 0))
    perhead = pl.BlockSpec((Hq, LANE), lambda h, n: (0, 0))
    return _pcall(
        body, name="swa_fwd", grid=(Hkv, nblk),
        in_specs=[qblk, prev, cur, prev, cur, gain, gain, perhead, perhead], out_specs=qblk,
        out_shape=jax.ShapeDtypeStruct((Hq, T, d), F32), compiler_params=_params(("parallel", "parallel")),
    )(q, k, k, v, v, qg, kg, sinks, slopes)


def _swa_bwd(q, k, v, qg, kg, sinks, slopes, do):
    Hq, T, d = q.shape
    Hkv = k.shape[0]
    G = Hq // Hkv
    W = WINDOW
    nblk = T // W
    scale = d ** -0.5

    def body(q_ref, kp_ref, kc_ref, vp_ref, vc_ref, qg_ref, kg_ref, sk_ref, sl_ref, do_ref,
             dq_ref, dk_ref, dv_ref, dqg_ref, dkg_ref, dsk_ref):
        hk = pl.program_id(0)
        n = pl.program_id(1)

        @pl.when((hk == 0) & (n == 0))
        def _():
            dqg_ref[...] = jnp.zeros_like(dqg_ref)
            dkg_ref[...] = jnp.zeros_like(dkg_ref)
            dsk_ref[...] = jnp.zeros_like(dsk_ref)

        @pl.when(n == 0)
        def _():
            dk_ref[...] = jnp.zeros_like(dk_ref)
            dv_ref[...] = jnp.zeros_like(dv_ref)

        kcat = jnp.concatenate([kp_ref[...], kc_ref[...]], axis=0)
        vcat = jnp.concatenate([vp_ref[...], vc_ref[...]], axis=0).astype(BF16)
        rk = lax.rsqrt(jnp.mean(kcat * kcat, axis=-1, keepdims=True) + EPS)
        kh = kcat * rk
        kn = (kh * kg_ref[...]).astype(BF16)
        row = _iota2((W, 2 * W), 0)
        col = _iota2((W, 2 * W), 1)
        dist = row + W - col
        valid = (dist >= 0) & (dist < W) & ((n > 0) | (col >= W))
        distf = dist.astype(F32)
        rowh = _iota2((Hq, LANE), 0)
        dkn = jnp.zeros((2 * W, d), F32)
        dvc = jnp.zeros((2 * W, d), F32)
        dqg = jnp.zeros((1, d), F32)
        dsk = jnp.zeros((Hq, LANE), F32)
        for g in range(G):
            qh = q_ref[g]
            rq = lax.rsqrt(jnp.mean(qh * qh, axis=-1, keepdims=True) + EPS)
            qhh = qh * rq
            qn = (qhh * qg_ref[...]).astype(BF16)
            sink = sk_ref[pl.ds(hk * G + g, 1), :][:, :1]
            slope = sl_ref[pl.ds(hk * G + g, 1), :][:, :1]
            s = _dot(qn, kn, NT) * scale - slope * distf
            s = jnp.where(valid, s, -1e30)
            m = jnp.maximum(jnp.max(s, axis=1, keepdims=True), sink)
            p = jnp.where(valid, jnp.exp(s - m), 0.0)
            esink = jnp.exp(sink - m)
            den = jnp.sum(p, axis=1, keepdims=True) + esink
            prob = p / den
            dov = do_ref[g].astype(BF16)
            dp = _dot(dov, vcat, NT)
            dd = jnp.sum(prob * dp, axis=1, keepdims=True)
            ds = prob * (dp - dd)
            dsink = -jnp.sum((esink / den) * dd, axis=0, keepdims=True)
            dsk = dsk + jnp.where(rowh == hk * G + g, dsink, 0.0)
            dsb = (ds * scale).astype(BF16)
            dqn = _dot(dsb, kn)
            dkn = dkn + _dot(dsb, qn, TN)
            dvc = dvc + _dot(prob.astype(BF16), dov, TN)
            dqh = dqn * qg_ref[...]
            dq_ref[g] = rq * (dqh - qhh * jnp.mean(dqh * qhh, axis=-1, keepdims=True))
            dqg = dqg + jnp.sum(dqn * qhh, axis=0, keepdims=True)
        dkh = dkn * kg_ref[...]
        dkraw = rk * (dkh - kh * jnp.mean(dkh * kh, axis=-1, keepdims=True))
        dqg_ref[...] += dqg
        dkg_ref[...] += jnp.sum(dkn * kh, axis=0, keepdims=True)
        dsk_ref[...] += dsk
        offp = pl.multiple_of(jnp.maximum(n - 1, 0) * W, W)
        offc = pl.multiple_of(n * W, W)
        dk_ref[pl.ds(offp, W), :] += dkraw[:W]
        dv_ref[pl.ds(offp, W), :] += dvc[:W]
        dk_ref[pl.ds(offc, W), :] += dkraw[W:]
        dv_ref[pl.ds(offc, W), :] += dvc[W:]

    qblk = pl.BlockSpec((G, W, d), lambda h, n: (h, n, 0))
    prev = pl.BlockSpec((None, W, d), lambda h, n: (h, jnp.maximum(n - 1, 0), 0))
    cur = pl.BlockSpec((None, W, d), lambda h, n: (h, n, 0))
    gain = pl.BlockSpec((1, d), lambda h, n: (0, 0))
    perhead = pl.BlockSpec((Hq, LANE), lambda h, n: (0, 0))
    full = pl.BlockSpec((None, T, d), lambda h, n: (h, 0, 0))
    kv = jax.ShapeDtypeStruct((Hkv, T, d), F32)
    gs = jax.ShapeDtypeStruct((1, d), F32)
    return _pcall(
        body, name="swa_bwd", grid=(Hkv, nblk),
        in_specs=[qblk, prev, cur, prev, cur, gain, gain, perhead, perhead, qblk],
        out_specs=[qblk, full, full, gain, gain, perhead],
        out_shape=[jax.ShapeDtypeStruct((Hq, T, d), F32), kv, kv, gs, gs, jax.ShapeDtypeStruct((Hq, LANE), F32)],
        compiler_params=_params(("arbitrary", "arbitrary")),
    )(q, k, k, v, v, qg, kg, sinks, slopes, do)


def _heads(z, n):
    T = z.shape[0]
    return z.reshape(T, n, HEAD_DIM).transpose(1, 0, 2)


def _unheads(z):
    n, T, d = z.shape
    return z.transpose(1, 0, 2).reshape(T, n * d)


def _alibi():
    s = [2.0 ** (-8.0 * (i + 1) / SWA_HEADS) for i in range(SWA_HEADS)]
    return jnp.broadcast_to(jnp.asarray(s, F32)[:, None], (SWA_HEADS, LANE))


def _att_fwd(h, g, w_in, w_out_of, q_gain, k_gain, sinks, ride=()):
    hn = _rms_fwd(h, g, "att_norm")
    proj = _mm(hn, w_in, "nn", name="att_in")
    c = [0, SB_W, 2 * SB_W, 3 * SB_W, 3 * SB_W + SWA_QW, 3 * SB_W + SWA_QW + SWA_KVW, ATT_IN]
    sq, sk, sv = (_heads(proj[:, c[i]:c[i + 1]], SB_HEADS).astype(BF16) for i in range(3))
    bq = _heads(proj[:, c[3]:c[4]], SWA_HEADS)
    bk = _heads(proj[:, c[4]:c[5]], SWA_KV_HEADS)
    bv = _heads(proj[:, c[5]:c[6]], SWA_KV_HEADS)
    a_out, carry, gathered = _sb_fwd(sq, sk, sv, ride)
    w_out = w_out_of(gathered)
    sk128 = jnp.broadcast_to(sinks.reshape(SWA_HEADS, 1), (SWA_HEADS, LANE))
    qg, kg = q_gain.reshape(1, HEAD_DIM), k_gain.reshape(1, HEAD_DIM)
    b_out = _swa_fwd(bq, bk, bv, qg, kg, sk128, _alibi())
    o = jnp.concatenate([_unheads(a_out), _unheads(b_out)], axis=-1).astype(BF16)
    h2 = _mm(o, w_out, "nn", res=h, name="att_out")
    return h2, (h, hn, sq, sk, sv, bq, bk, bv, carry, o, sk128, qg, kg), gathered


def _att_bwd(dh2, saved, g, w_in, w_out, ride=()):
    h, hn, sq, sk, sv, bq, bk, bv, carry, o, sk128, qg, kg = saved
    do = _mm(dh2, w_out, "nt", name="att_do")
    dw_out = _mm(o, dh2, "tn", out_dtype=BF16, name="att_dwout")
    da = _heads(do[:, :SB_W], SB_HEADS)
    db = _heads(do[:, SB_W:], SWA_HEADS)
    dsq, dsk, dsv, rode = _sb_bwd(sq, sk, sv, carry, da, ride)
    dbq, dbk, dbv, dqg, dkg, dsink = _swa_bwd(bq, bk, bv, qg, kg, sk128, _alibi(), db)
    dproj = jnp.concatenate([_unheads(z) for z in (dsq, dsk, dsv, dbq, dbk, dbv)], axis=-1).astype(BF16)
    dw_in = _mm(hn, dproj, "tn", out_dtype=BF16, name="att_dwin")
    dhn = _mm(dproj, w_in, "nt", name="att_dhn")
    dh, dg = _rms_bwd(dhn, h, g, dh2, "att_dnorm")
    return dh, dg, dw_in, dw_out, dqg.reshape(HEAD_DIM), dkg.reshape(HEAD_DIM), dsink[:, 0], rode


CONV_ROWS = 512
CONV_COLS = 512
HALO = 8


def _shifted(xcat, s, tm):
    if s == 0:
        return xcat[HALO:HALO + tm]
    return pltpu.roll(xcat, s, 0)[HALO:HALO + tm]


def _conv_pre(x_ref, halo_ref, w_ref, i, tm):
    xc = x_ref[...]
    halo = jnp.where(i > 0, halo_ref[...], 0.0)
    xcat = jnp.concatenate([halo, xc], axis=0)
    w = w_ref[...]
    y = w[GDN_CONV - 1:GDN_CONV] * xc
    for kk in range(GDN_CONV - 1):
        y = y + w[kk:kk + 1] * _shifted(xcat, GDN_CONV - 1 - kk, tm)
    return xcat, y


def _l2_heads(s, qscale_of):
    outs, rs = [], []
    for hh in range(s.shape[1] // GDN_HEAD_DIM):
        sh = s[:, hh * GDN_HEAD_DIM:(hh + 1) * GDN_HEAD_DIM]
        r = lax.rsqrt(jnp.sum(sh * sh, axis=-1, keepdims=True) + EPS)
        outs.append(sh * r)
        rs.append(r)
    return outs, rs


def _conv_specs(T, col0, tm, tc):
    cur = pl.BlockSpec((tm, tc), lambda j, i: (i, j + col0 // tc))
    halo = pl.BlockSpec((HALO, tc), lambda j, i: (jnp.maximum(i * (tm // HALO) - 1, 0), j + col0 // tc))
    wsp = pl.BlockSpec((GDN_CONV, tc), lambda j, i: (0, j + col0 // tc))
    out = pl.BlockSpec((tm, tc), lambda j, i: (i, j))
    return cur, halo, wsp, out


def _conv_fwd(proj, conv_w, col0, width, norm, name):
    T = proj.shape[0]
    tm, tc = _pick(T, CONV_ROWS), CONV_COLS
    cur, halo, wsp, out = _conv_specs(T, col0, tm, tc)
    n_q_tiles = (width // 2) // tc

    def body(x_ref, halo_ref, w_ref, o_ref):
        j, i = pl.program_id(0), pl.program_id(1)
        _, y = _conv_pre(x_ref, halo_ref, w_ref, i, tm)
        s = y * _sigmoid(y)
        if norm:
            outs, _ = _l2_heads(s, None)
            qs = jnp.where(j < n_q_tiles, GDN_HEAD_DIM ** -0.5, 1.0)
            o_ref[...] = jnp.concatenate(outs, axis=1) * qs
        else:
            o_ref[...] = s

    return _pcall(body, name=name, grid=(width // tc, T // tm), in_specs=[cur, halo, wsp], out_specs=out,
                  out_shape=jax.ShapeDtypeStruct((T, width), F32),
                  compiler_params=_params(("parallel", "parallel")))(proj, proj, conv_w)


def _conv_bwd_pre(proj, conv_w, dout, col0, width, norm, name):
    T = proj.shape[0]
    tm, tc = _pick(T, CONV_ROWS), CONV_COLS
    cur, halo, wsp, out = _conv_specs(T, col0, tm, tc)
    n_q_tiles = (width // 2) // tc

    def body(x_ref, halo_ref, w_ref, d_ref, dy_ref, dw_ref):
        j, i = pl.program_id(0), pl.program_id(1)
        xcat, y = _conv_pre(x_ref, halo_ref, w_ref, i, tm)
        sg = _sigmoid(y)
        s = y * sg
        d = d_ref[...]
        if norm:
            qs = jnp.where(j < n_q_tiles, GDN_HEAD_DIM ** -0.5, 1.0)
            d = d * qs
            outs, rs = _l2_heads(s, None)
            parts = []
            for hh, (nh, r) in enumerate(zip(outs, rs)):
                dh = d[:, hh * GDN_HEAD_DIM:(hh + 1) * GDN_HEAD_DIM]
                parts.append(r * (dh - nh * jnp.sum(dh * nh, axis=-1, keepdims=True)))
            ds = jnp.concatenate(parts, axis=1)
        else:
            ds = d
        dy = ds * sg * (1.0 + y * (1.0 - sg))
        dy_ref[...] = dy
        rows = [jnp.sum(dy * _shifted(xcat, GDN_CONV - 1 - kk, tm), axis=0, keepdims=True) for kk in range(GDN_CONV)]
        part = jnp.concatenate(rows, axis=0)

        @pl.when(i == 0)
        def _():
            dw_ref[...] = part

        @pl.when(i > 0)
        def _():
            dw_ref[...] += part

    wout = pl.BlockSpec((GDN_CONV, tc), lambda j, i: (0, j))
    return _pcall(body, name=name, grid=(width // tc, T // tm), in_specs=[cur, halo, wsp, out], out_specs=[out, wout],
                  out_shape=[jax.ShapeDtypeStruct((T, width), F32), jax.ShapeDtypeStruct((GDN_CONV, width), F32)],
                  compiler_params=_params(("parallel", "arbitrary")))(proj, proj, conv_w, dout)


def _conv_bwd_in(dy, conv_w, name):
    T, C = dy.shape
    tm, tc = _pick(T, CONV_ROWS), CONV_COLS
    nrow = T // tm

    def body(d_ref, nxt_ref, w_ref, dx_ref):
        i = pl.program_id(0)
        dc = d_ref[...]
        nxt = jnp.where(i < nrow - 1, nxt_ref[...], 0.0)
        dcat = jnp.concatenate([dc, nxt], axis=0)
        w = w_ref[...]
        dx = w[GDN_CONV - 1:GDN_CONV] * dc
        for kk in range(GDN_CONV - 1):
            s = GDN_CONV - 1 - kk
            dx = dx + w[kk:kk + 1] * pltpu.roll(dcat, tm + HALO - s, 0)[:tm]
        dx_ref[...] = dx.astype(BF16)

    cur = pl.BlockSpec((tm, tc), lambda i, j: (i, j))
    nxt = pl.BlockSpec((HALO, tc), lambda i, j: (jnp.minimum((i + 1) * (tm // HALO), T // HALO - 1), j))
    wsp = pl.BlockSpec((GDN_CONV, tc), lambda i, j: (0, j))
    return _pcall(body, name=name, grid=(nrow, C // tc), in_specs=[cur, nxt, wsp], out_specs=cur,
                  out_shape=jax.ShapeDtypeStruct((T, C), BF16),
                  compiler_params=_params(("parallel", "parallel")))(dy, dy, conv_w)


GATE_ROWS = 512


def _chunk_mask(n, lower):
    row = _iota2((n, n), 0)
    col = _iota2((n, n), 1)
    same = (row // GDN_CHUNK) == (col // GDN_CHUNK)
    tri = (row >= col) if lower else (row <= col)
    return (same & tri).astype(BF16)


def _gates_fwd(proj, a_log, dt_bias):
    T = proj.shape[0]
    tm = _pick(T, GATE_ROWS)
    c0 = (GDN_CONV_W + GDN_VW) // LANE

    def body(bl_ref, a_ref, alog_ref, dt_ref, beta_ref, g_ref, gc_ref):
        beta_ref[...] = _sigmoid(bl_ref[...])
        g = -jnp.exp(alog_ref[...]) * _softplus(a_ref[...] + dt_ref[...])
        g_ref[...] = g
        gc_ref[...] = _mdot2(_chunk_mask(tm, True), g)

    blk = lambda c: pl.BlockSpec((tm, LANE), lambda i: (i, c))
    vec = pl.BlockSpec((1, LANE), lambda i: (0, 0))
    sh = jax.ShapeDtypeStruct((T, LANE), F32)
    return _pcall(body, name="gdn_gates", grid=(T // tm,), in_specs=[blk(c0), blk(c0 + 1), vec, vec],
                  out_specs=[blk(0), blk(0), blk(0)], out_shape=[sh, sh, sh],
                  compiler_params=_params(("parallel",)))(proj, proj, a_log, dt_bias)


def _gates_bwd(proj, a_log, dt_bias, beta, g, dbeta, dgc):
    T = proj.shape[0]
    tm = _pick(T, GATE_ROWS)
    c0 = (GDN_CONV_W + GDN_VW) // LANE

    def body(a_ref, alog_ref, dt_ref, beta_ref, g_ref, dbeta_ref, dgc_ref, dbl_ref, da_ref, dalog_ref, ddt_ref):
        dg = _mdot2(_chunk_mask(tm, False), dgc_ref[...])
        b = beta_ref[...]
        dbl_ref[...] = (dbeta_ref[...] * b * (1.0 - b)).astype(BF16)
        da = dg * (-jnp.exp(alog_ref[...])) * _sigmoid(a_ref[...] + dt_ref[...])
        da_ref[...] = da.astype(BF16)
        p1 = jnp.sum(dg * g_ref[...], axis=0, keepdims=True)
        p2 = jnp.sum(da, axis=0, keepdims=True)

        @pl.when(pl.program_id(0) == 0)
        def _():
            dalog_ref[...] = p1
            ddt_ref[...] = p2

        @pl.when(pl.program_id(0) > 0)
        def _():
            dalog_ref[...] += p1
            ddt_ref[...] += p2

    blk = lambda c: pl.BlockSpec((tm, LANE), lambda i: (i, c))
    vec = pl.BlockSpec((1, LANE), lambda i: (0, 0))
    shb = jax.ShapeDtypeStruct((T, LANE), BF16)
    shv = jax.ShapeDtypeStruct((1, LANE), F32)
    return _pcall(body, name="gdn_dgates", grid=(T // tm,),
                  in_specs=[blk(c0 + 1), vec, vec, blk(0), blk(0), blk(0), blk(0)],
                  out_specs=[blk(0), blk(0), vec, vec], out_shape=[shb, shb, shv, shv],
                  compiler_params=_params(("arbitrary",)))(proj, a_log, dt_bias, beta, g, dbeta, dgc)


def _inv_unit_lower(Ls):
    C = Ls[0].shape[0]
    row = _iota2((C, C), 0)
    col = _iota2((C, C), 1)
    blk16 = (row // 16) == (col // 16)
    blk32 = (row // 32) == (col // 32)
    eye = (row == col).astype(F32)
    xs = [-jnp.where(blk16, L, 0.0) for L in Ls]
    inv = [eye + x for x in xs]
    for _ in range(3):
        xs = [_dot3(x, x) for x in xs]
        inv = [a + _dot3(a, x) for a, x in zip(inv, xs)]
    for mask in (blk32 & ~blk16, ~blk32):
        t = [_dot3(a, jnp.where(mask, L, 0.0)) for a, L in zip(inv, Ls)]
        inv = [a - _dot3(ti, a) for a, ti in zip(inv, t)]
    return inv


GDN_GROUP = 4
GDN_PREP_CHUNKS = 4


def _gdn_specs(T):
    C, D, E = GDN_CHUNK, GDN_HEAD_DIM, GDN_GROUP
    n = T // C
    qk = pl.BlockSpec((C, (E // 2) * D), lambda h, i: (i, h))
    vE = pl.BlockSpec((C, E * D), lambda h, i: (i, h))
    colv = pl.BlockSpec((E, C, 1), lambda h, i: (h, i, 0))
    rowv = pl.BlockSpec((E, None, 1, C), lambda h, i: (h, i, 0, 0))
    st = pl.BlockSpec((E, None, D, D), lambda h, i: (h, i, 0, 0))
    am = pl.BlockSpec((E, None, C, C), lambda h, i: (h, i, 0, 0))
    return n, qk, vE, colv, rowv, st, am


def _gdn_decay(gcol, grow):
    C = GDN_CHUNK
    row = _iota2((C, C), 0)
    col = _iota2((C, C), 1)
    incl = row >= col
    dm = jnp.where(incl, jnp.exp(jnp.where(incl, gcol - grow, 0.0)), 0.0)
    glast = grow[:, C - 1:C]
    return dm, jnp.exp(gcol), jnp.exp(glast), jnp.exp(glast - gcol), row > col, incl


def _gdn_prep(k, beta, gcol, grow):
    T = k.shape[0]
    C, D, B = GDN_CHUNK, GDN_HEAD_DIM, GDN_PREP_CHUNKS
    n = T // C

    def body(k_ref, b_ref, gc_ref, gr_ref, a_ref):
        idx = [(e, cb) for e in range(2) for cb in range(B)]
        kc = {cb: k_ref[cb * C:(cb + 1) * C, :] for cb in range(B)}
        lm = []
        for e, cb in idx:
            beta = b_ref[e, cb * C:(cb + 1) * C, :]
            dm, _, _, _, strict, _ = _gdn_decay(gc_ref[e, cb * C:(cb + 1) * C, :], gr_ref[e, cb])
            lm.append(jnp.where(strict, _bdot(kc[cb] * beta, kc[cb], NT) * dm, 0.0))
        inv = _inv_unit_lower(lm)
        for (e, cb), a in zip(idx, inv):
            a_ref[e, cb] = a

    return _pcall(
        body, name="gdn_prep", grid=(GDN_K_HEADS, n // B),
        in_specs=[pl.BlockSpec((B * C, D), lambda h, i: (i, h)), pl.BlockSpec((2, B * C, 1), lambda h, i: (h, i, 0)),
                  pl.BlockSpec((2, B * C, 1), lambda h, i: (h, i, 0)), pl.BlockSpec((2, B, 1, C), lambda h, i: (h, i, 0, 0))],
        out_specs=pl.BlockSpec((2, B, C, C), lambda h, i: (h, i, 0, 0)),
        out_shape=jax.ShapeDtypeStruct((GDN_V_HEADS, n, C, C), F32),
        compiler_params=_params(("parallel", "parallel")),
    )(k, beta, gcol, grow)


def _gdn_fwd(q, k, v, beta, gcol, grow, amat):
    T = q.shape[0]
    C, D, E = GDN_CHUNK, GDN_HEAD_DIM, GDN_GROUP
    n, qk, vE, colv, rowv, st, am = _gdn_specs(T)
    R = range(E)

    def body(q_ref, k_ref, v_ref, b_ref, gc_ref, gr_ref, a_ref, o_ref, s_ref, vn_ref, state):
        @pl.when(pl.program_id(1) == 0)
        def _():
            state[...] = jnp.zeros_like(state)

        qv = [q_ref[:, (e // 2) * D:(e // 2 + 1) * D] for e in R]
        kv = [k_ref[:, (e // 2) * D:(e // 2 + 1) * D] for e in R]
        vv = [v_ref[:, e * D:(e + 1) * D] for e in R]
        beta = [b_ref[e] for e in R]
        a = [a_ref[e] for e in R]
        s = [state[e] for e in R]
        dec = [_gdn_decay(gc_ref[e], gr_ref[e]) for e in R]
        pm = [_bdot(qv[e], kv[e], NT) * dec[e][0] for e in R]
        r = [beta[e] * (vv[e] - _bdot(kv[e] * dec[e][1], s[e])) for e in R]
        vn = [_dot3(a[e], r[e]) for e in R]
        o = [_bdot(qv[e] * dec[e][1], s[e]) + _bdot(pm[e], vn[e]) for e in R]
        s2 = [dec[e][2] * s[e] + _bdot(kv[e] * dec[e][3], vn[e], TN) for e in R]
        for e in R:
            s_ref[e] = s[e]
            vn_ref[:, e * D:(e + 1) * D] = vn[e]
            o_ref[:, e * D:(e + 1) * D] = o[e]
            state[e] = s2[e]

    shv = jax.ShapeDtypeStruct((T, GDN_V_HEADS * D), F32)
    return _pcall(
        body, name="gdn_fwd", grid=(GDN_V_HEADS // E, n), in_specs=[qk, qk, vE, colv, colv, rowv, am],
        out_specs=[vE, st, vE],
        out_shape=[shv, jax.ShapeDtypeStruct((GDN_V_HEADS, n, D, D), F32), shv],
        scratch_shapes=[pltpu.VMEM((E, D, D), F32)],
        compiler_params=_params(("parallel", "arbitrary")),
    )(q, k, v, beta, gcol, grow, amat)


def _gdn_bwd(q, k, v, beta, gcol, grow, states, amat, vnew, do):
    T = q.shape[0]
    C, D, E = GDN_CHUNK, GDN_HEAD_DIM, GDN_GROUP
    n, qk, vE, colv, rowv, st, am = _gdn_specs(T)
    rev = lambda spec: pl.BlockSpec(spec.block_shape, (lambda f: (lambda h, i: f(h, n - 1 - i)))(spec.index_map))
    qk, vE, colv, rowv, st, am = (rev(s) for s in (qk, vE, colv, rowv, st, am))
    R = range(E)

    def body(q_ref, k_ref, v_ref, b_ref, gc_ref, gr_ref, s_ref, a_ref, vn_ref, do_ref,
             dq_ref, dk_ref, dv_ref, db_ref, dgc_ref, dstate):
        @pl.when(pl.program_id(1) == 0)
        def _():
            dstate[...] = jnp.zeros_like(dstate)

        M = lambda f: [f(e) for e in R]
        rsum = lambda x: jnp.sum(x, axis=1, keepdims=True)
        qv = M(lambda e: q_ref[:, (e // 2) * D:(e // 2 + 1) * D])
        kv = M(lambda e: k_ref[:, (e // 2) * D:(e // 2 + 1) * D])
        vv = M(lambda e: v_ref[:, e * D:(e + 1) * D])
        vn = M(lambda e: vn_ref[:, e * D:(e + 1) * D])
        dov = M(lambda e: do_ref[:, e * D:(e + 1) * D])
        beta = M(lambda e: b_ref[e])
        s = M(lambda e: s_ref[e])
        a = M(lambda e: a_ref[e])
        dsn = M(lambda e: dstate[e])
        dec = M(lambda e: _gdn_decay(gc_ref[e], gr_ref[e]))
        dm, gam, glast, tail = (M(lambda e: dec[e][i]) for i in range(4))
        strict, incl = dec[0][4], dec[0][5]
        kb = M(lambda e: kv[e] * beta[e])
        kd = M(lambda e: kv[e] * gam[e])
        qd = M(lambda e: qv[e] * gam[e])
        kt = M(lambda e: kv[e] * tail[e])
        lmat = M(lambda e: jnp.where(strict, _bdot(kb[e], kv[e], NT) * dm[e], 0.0))
        pmat = M(lambda e: _bdot(qv[e], kv[e], NT) * dm[e])
        xres = M(lambda e: vv[e] - _bdot(kd[e], s[e]))
        dvn = M(lambda e: _bdot(pmat[e], dov[e], TN) + _bdot(kt[e], dsn[e]))
        dqd = M(lambda e: _bdot(dov[e], s[e], NT))
        dp = M(lambda e: jnp.where(incl, _bdot(dov[e], vn[e], NT), 0.0))
        dkt = M(lambda e: _bdot(vn[e], dsn[e], NT))
        dr = M(lambda e: _dot3(a[e], dvn[e], TN))
        drb = M(lambda e: beta[e] * dr[e])
        dkd = M(lambda e: -_bdot(drb[e], s[e], NT))
        ds2 = M(lambda e: _bdot(qd[e], dov[e], TN) + glast[e] * dsn[e] - _bdot(kd[e], drb[e], TN))
        dl = M(lambda e: -jnp.where(strict, _bdot(dr[e], vn[e], NT), 0.0))
        dmm = M(lambda e: dl[e] * dm[e])
        dnn = M(lambda e: dp[e] * dm[e])
        emat = M(lambda e: dl[e] * lmat[e] + dp[e] * pmat[e])
        dkb = M(lambda e: _bdot(dmm[e], kv[e]))
        dk = M(lambda e: beta[e] * dkb[e] + _bdot(dmm[e], kb[e], TN) + _bdot(dnn[e], qv[e], TN)
               + gam[e] * dkd[e] + tail[e] * dkt[e])
        dq = M(lambda e: _bdot(dnn[e], kv[e]) + gam[e] * dqd[e])
        dbeta = M(lambda e: rsum(dr[e] * xres[e]) + rsum(dkb[e] * kv[e]))
        ones = jnp.ones((C, LANE), BF16)
        colsum = M(lambda e: _dot2m(emat[e], ones, TN)[:, :1])
        tails = M(lambda e: rsum(dkt[e] * kt[e]))
        lastrow = _iota2((C, 1), 0) == C - 1
        dlast = M(lambda e: jnp.sum(tails[e], axis=0, keepdims=True)
                  + glast[e] * jnp.sum(rsum(s[e] * dsn[e]), axis=0, keepdims=True))
        dgc = M(lambda e: rsum(emat[e]) - colsum[e] + rsum(dkd[e] * kd[e]) + rsum(dqd[e] * qd[e]) - tails[e]
                + jnp.where(lastrow, dlast[e], 0.0))
        for e in R:
            dv_ref[:, e * D:(e + 1) * D] = drb[e]
            db_ref[e] = dbeta[e]
            dgc_ref[e] = dgc[e]
            dstate[e] = ds2[e]
        for kh in range(E // 2):
            dq_ref[:, kh * D:(kh + 1) * D] = dq[2 * kh] + dq[2 * kh + 1]
            dk_ref[:, kh * D:(kh + 1) * D] = dk[2 * kh] + dk[2 * kh + 1]

    shq = jax.ShapeDtypeStruct((T, GDN_K_HEADS * D), F32)
    shv = jax.ShapeDtypeStruct((T, GDN_V_HEADS * D), F32)
    shc = jax.ShapeDtypeStruct((GDN_V_HEADS, T, 1), F32)
    return _pcall(
        body, name="gdn_bwd", grid=(GDN_V_HEADS // E, n),
        in_specs=[qk, qk, vE, colv, colv, rowv, st, am, vE, vE],
        out_specs=[qk, qk, vE, colv, colv], out_shape=[shq, shq, shv, shc, shc],
        scratch_shapes=[pltpu.VMEM((E, D, D), F32)],
        compiler_params=_params(("parallel", "arbitrary")),
    )(q, k, v, beta, gcol, grow, states, amat, vnew, do)


def _outgate_fwd(o, proj, gain):
    T = o.shape[0]
    tm, tc = _pick(T, CONV_ROWS), CONV_COLS
    z0 = GDN_CONV_W // tc

    def body(o_ref, z_ref, g_ref, y_ref):
        z = z_ref[...]
        sz = z * _sigmoid(z)
        parts = []
        for hh in range(tc // GDN_HEAD_DIM):
            oh = o_ref[:, hh * GDN_HEAD_DIM:(hh + 1) * GDN_HEAD_DIM]
            r = lax.rsqrt(jnp.mean(oh * oh, axis=-1, keepdims=True) + EPS)
            parts.append(oh * r * g_ref[...])
        y_ref[...] = (jnp.concatenate(parts, axis=1) * sz).astype(BF16)

    blk = pl.BlockSpec((tm, tc), lambda i, j: (i, j))
    return _pcall(body, name="gdn_outgate", grid=(T // tm, GDN_VW // tc),
                  in_specs=[blk, pl.BlockSpec((tm, tc), lambda i, j: (i, j + z0)), pl.BlockSpec((1, GDN_HEAD_DIM), lambda i, j: (0, 0))],
                  out_specs=blk, out_shape=jax.ShapeDtypeStruct((T, GDN_VW), BF16),
                  compiler_params=_params(("parallel", "parallel")))(o, proj, gain)


def _outgate_bwd(dy, o, proj, gain):
    T = o.shape[0]
    tm, tc = _pick(T, CONV_ROWS), CONV_COLS
    z0 = GDN_CONV_W // tc
    nh = tc // GDN_HEAD_DIM

    def body(dy_ref, o_ref, z_ref, g_ref, do_ref, dz_ref, dg_ref):
        z = z_ref[...]
        sg = _sigmoid(z)
        sz = z * sg
        dy = dy_ref[...]
        dgain = jnp.zeros((1, GDN_HEAD_DIM), F32)
        dos, ys = [], []
        for hh in range(nh):
            sl = slice(hh * GDN_HEAD_DIM, (hh + 1) * GDN_HEAD_DIM)
            oh = o_ref[:, sl]
            r = lax.rsqrt(jnp.mean(oh * oh, axis=-1, keepdims=True) + EPS)
            xh = oh * r
            dn = dy[:, sl] * sz[:, sl]
            dgain = dgain + jnp.sum(dn * xh, axis=0, keepdims=True)
            dxh = dn * g_ref[...]
            dos.append(r * (dxh - xh * jnp.mean(dxh * xh, axis=-1, keepdims=True)))
            ys.append(xh * g_ref[...])
        do_ref[...] = jnp.concatenate(dos, axis=1)
        dz_ref[...] = (dy * jnp.concatenate(ys, axis=1) * sg * (1.0 + z * (1.0 - sg))).astype(BF16)
        first = (pl.program_id(0) == 0) & (pl.program_id(1) == 0)

        @pl.when(first)
        def _():
            dg_ref[...] = dgain

        @pl.when(jnp.logical_not(first))
        def _():
            dg_ref[...] += dgain

    blk = pl.BlockSpec((tm, tc), lambda i, j: (i, j))
    vec = pl.BlockSpec((1, GDN_HEAD_DIM), lambda i, j: (0, 0))
    return _pcall(body, name="gdn_doutgate", grid=(T // tm, GDN_VW // tc),
                  in_specs=[blk, blk, pl.BlockSpec((tm, tc), lambda i, j: (i, j + z0)), vec],
                  out_specs=[blk, blk, vec],
                  out_shape=[jax.ShapeDtypeStruct((T, GDN_VW), F32), jax.ShapeDtypeStruct((T, GDN_VW), BF16),
                             jax.ShapeDtypeStruct((1, GDN_HEAD_DIM), F32)],
                  compiler_params=_params(("arbitrary", "arbitrary")))(dy, o, proj, gain)


def _pad_lanes(vec):
    return jnp.pad(vec.reshape(1, -1), ((0, 0), (0, LANE - vec.shape[-1])))


def _head_cols(a):
    return a[:, :GDN_V_HEADS].T[:, :, None]


def _gdn_pad_in(w_in):
    c = GDN_CONV_W + GDN_VW
    z = jnp.zeros(w_in.shape[:-1] + (LANE - GDN_V_HEADS,), w_in.dtype)
    return jnp.concatenate([w_in[..., :c + GDN_V_HEADS], z, w_in[..., c + GDN_V_HEADS:], z], axis=-1)


def _gdn_unpad_in(dw):
    c = GDN_CONV_W + GDN_VW
    return jnp.concatenate([dw[..., :c + GDN_V_HEADS], dw[..., c + LANE:c + LANE + GDN_V_HEADS]], axis=-1)


def _gdn_mixer_fwd(h, g, w_in_pad, conv_w, a_log, dt_bias, out_gain, w_out):
    T = h.shape[0]
    hn = _rms_fwd(h, g, "gdn_norm")
    proj = _mm(hn, w_in_pad, "nn", name="gdn_in")
    qk = _conv_fwd(proj, conv_w, 0, 2 * GDN_KW, True, "gdn_conv_qk")
    vv = _conv_fwd(proj, conv_w, 2 * GDN_KW, GDN_VW, False, "gdn_conv_v")
    alog, dtb = _pad_lanes(a_log), _pad_lanes(dt_bias)
    beta, gl, gc = _gates_fwd(proj, alog, dtb)
    bcol, gcol = _head_cols(beta), _head_cols(gc)
    grow = gcol.reshape(GDN_V_HEADS, T // GDN_CHUNK, 1, GDN_CHUNK)
    qn, kn = qk[:, :GDN_KW], qk[:, GDN_KW:]
    amat = _gdn_prep(kn, bcol, gcol, grow)
    o, states, vnew = _gdn_fwd(qn, kn, vv, bcol, gcol, grow, amat)
    gain = out_gain.reshape(1, GDN_HEAD_DIM)
    y = _outgate_fwd(o, proj, gain)
    h2 = _mm(y, w_out, "nn", res=h, name="gdn_out")
    return h2, (h, hn, proj, qn, kn, vv, beta, gl, bcol, gcol, grow, o, states, amat, vnew, y, alog, dtb, gain)


def _gdn_mixer_bwd(dh2, saved, g, w_in_pad, conv_w, w_out):
    h, hn, proj, qn, kn, vv, beta, gl, bcol, gcol, grow, o, states, amat, vnew, y, alog, dtb, gain = saved
    T = h.shape[0]
    dy = _mm(dh2, w_out, "nt", name="gdn_dy")
    dw_out = _mm(y, dh2, "tn", out_dtype=BF16, name="gdn_dwout")
    do, dz, dgain = _outgate_bwd(dy, o, proj, gain)
    dq, dk, dv, dbcol, dgccol = _gdn_bwd(qn, kn, vv, bcol, gcol, grow, states, amat, vnew, do)
    dqk = jnp.concatenate([dq, dk], axis=1)
    dy_qk, dcw_qk = _conv_bwd_pre(proj, conv_w, dqk, 0, 2 * GDN_KW, True, "gdn_dconv_qk")
    dy_v, dcw_v = _conv_bwd_pre(proj, conv_w, dv, 2 * GDN_KW, GDN_VW, False, "gdn_dconv_v")
    dx_qk = _conv_bwd_in(dy_qk, conv_w[:, :2 * GDN_KW], "gdn_dconvin_qk")
    dx_v = _conv_bwd_in(dy_v, conv_w[:, 2 * GDN_KW:], "gdn_dconvin_v")
    lanes = lambda c: jnp.pad(c[:, :, 0].T, ((0, 0), (0, LANE - GDN_V_HEADS)))
    dbl, da, dalog, ddt = _gates_bwd(proj, alog, dtb, beta, gl, lanes(dbcol), lanes(dgccol))
    dproj = jnp.concatenate([dx_qk, dx_v, dz, dbl, da], axis=1)
    dw_in_pad = _mm(hn, dproj, "tn", out_dtype=BF16, name="gdn_dwin")
    dhn = _mm(dproj, w_in_pad, "nt", name="gdn_dhn")
    dh, dg = _rms_bwd(dhn, h, g, dh2, "gdn_dnorm")
    dconv = jnp.concatenate([dcw_qk, dcw_v], axis=1)
    return (dh, dg, _gdn_unpad_in(dw_in_pad), dconv, dalog[0, :GDN_V_HEADS], ddt[0, :GDN_V_HEADS],
            dgain.reshape(GDN_HEAD_DIM), dw_out)


def _instances(full):
    out = {}
    for n, a in full.items():
        if n.startswith("ffn_"):
            for i in range(2):
                for j in range(2):
                    out[(n, i, j)] = a[i, j]
        elif n in ("mix_norm", "ple_norm", "ple_w_gate", "ple_w_proj"):
            for i in range(2):
                out[(n, i)] = a[i]
        else:
            out[(n,)] = a[0]
    return out


def _stacked(inst):
    out = {}
    for n in dict.fromkeys(k[0] for k in inst):
        if n.startswith("ffn_"):
            out[n] = jnp.stack([jnp.stack([inst[(n, i, j)] for j in range(2)]) for i in range(2)])
        elif n in ("mix_norm", "ple_norm", "ple_w_gate", "ple_w_proj"):
            out[n] = jnp.stack([inst[(n, i)] for i in range(2)])
        else:
            out[n] = inst[(n,)][None]
    return out


def _local_step(x, p, target, w, late_shards=(), late_weights=None, early_grads=None):
    w = dict(w)
    ffn = lambda i, j: (w[("ffn_norm", i, j)], w[("ffn_w_gate", i, j)], w[("ffn_w_up", i, j)], w[("ffn_w_down", i, j)])
    h = x
    tape = []
    for i in range(2):
        h, s1 = _ffn_fwd(h, *ffn(i, 0), f"ffn{i}a")
        if i == 0:
            def w_out_of(gathered):
                if late_weights is not None:
                    w.update(late_weights(gathered))
                return w[("att_w_out",)]
            h, s2, _ = _att_fwd(h, w[("mix_norm", 0)], w[("att_w_in",)], w_out_of, w[("att_q_norm",)],
                                w[("att_k_norm",)], w[("att_sinks",)], late_shards)
        else:
            gdn_in_pad = _gdn_pad_in(w[("gdn_w_in",)])
            h, s2 = _gdn_mixer_fwd(h, w[("mix_norm", 1)], gdn_in_pad, w[("gdn_conv_w",)], w[("gdn_a_log",)],
                                   w[("gdn_dt_bias",)], w[("gdn_out_norm",)], w[("gdn_w_out",)])
        h, s3 = _ffn_fwd(h, *ffn(i, 1), f"ffn{i}b")
        h, s4 = _ple_fwd(h, p[i], w[("ple_norm", i)], w[("ple_w_gate", i)], w[("ple_w_proj", i)], f"ple{i}")
        tape.append((s1, s2, s3, s4))

    loss, dh = _loss_head(h, target)

    g = {}
    rode = []
    for i in (1, 0):
        s1, s2, s3, s4 = tape[i]
        dh, g[("ple_norm", i)], g[("ple_w_gate", i)], g[("ple_w_proj", i)] = _ple_bwd(
            dh, s4, p[i], w[("ple_norm", i)], w[("ple_w_gate", i)], f"ple{i}")
        dh, g[("ffn_norm", i, 1)], g[("ffn_w_gate", i, 1)], g[("ffn_w_up", i, 1)], g[("ffn_w_down", i, 1)] = _ffn_bwd(
            dh, s3, *ffn(i, 1), f"ffn{i}b")
        if i == 0:
            ride = early_grads(g) if early_grads is not None else ()
            (dh, g[("mix_norm", 0)], g[("att_w_in",)], g[("att_w_out",)], g[("att_q_norm",)], g[("att_k_norm",)],
             g[("att_sinks",)], rode) = _att_bwd(dh, s2, w[("mix_norm", 0)], w[("att_w_in",)], w[("att_w_out",)], ride)
        else:
            (dh, g[("mix_norm", 1)], g[("gdn_w_in",)], g[("gdn_conv_w",)], g[("gdn_a_log",)], g[("gdn_dt_bias",)],
             g[("gdn_out_norm",)], g[("gdn_w_out",)]) = _gdn_mixer_bwd(
                dh, s2, w[("mix_norm", 1)], gdn_in_pad, w[("gdn_conv_w",)], w[("gdn_w_out",)])
        dh, g[("ffn_norm", i, 0)], g[("ffn_w_gate", i, 0)], g[("ffn_w_up", i, 0)], g[("ffn_w_down", i, 0)] = _ffn_bwd(
            dh, s1, *ffn(i, 0), f"ffn{i}a")
    return loss, dh, g, rode


MESH = pl.DeviceIdType.MESH
N_CHIP = 4


def _place():
    x, y, c = lax.axis_index("x"), lax.axis_index("y"), lax.axis_index("c")
    others = [((1 - x, y), 2 * (1 - x) + y), ((x, 1 - y), 2 * x + (1 - y)), ((1 - x, 1 - y), 2 * (1 - x) + (1 - y))]
    return x, y, c, 4 * x + 2 * y + c, 2 * x + y, (x, y, 1 - c), others


def _comm_call(body, arrays, out_shape, n_sems, name):
    hbm = pl.BlockSpec(memory_space=pl.ANY)
    n = len(arrays)
    return _pcall(
        body, name=name, in_specs=[hbm] * n, out_specs=[hbm] * len(out_shape), out_shape=out_shape,
        scratch_shapes=[pltpu.SemaphoreType.DMA((n, n_sems)), pltpu.SemaphoreType.DMA((n, n_sems)),
                        pltpu.SemaphoreType.DMA((n, N_CHIP))],
        compiler_params=pltpu.CompilerParams(has_side_effects=True),
    )(*arrays)


def _gather_protocol(ins, outs, send_sems, recv_sems, local_sems):
    n = len(ins)
    x, y, c, me, my_chip, sibling, others = _place()

    def copy(a, k, block, to, src=None):
        dst = outs[a].at[block]
        return pltpu.make_async_remote_copy(
            src_ref=dst if src is None else src, dst_ref=dst, send_sem=send_sems.at[a, k],
            recv_sem=recv_sems.at[a, k], device_id=to, device_id_type=MESH)

    local = [pltpu.make_async_copy(ins[a], outs[a].at[me], local_sems.at[a, 0]) for a in range(n)]
    first = []
    for a in range(n):
        first.append(copy(a, 0, me, sibling, src=ins[a]))
        first += [copy(a, 1 + j, me, (*chip, c), src=ins[a]) for j, (chip, _) in enumerate(others)]

    def start():
        for cp in local + first:
            cp.start()

    def finish():
        passed = []
        for a in range(n):
            for j, (chip, chip_idx) in enumerate(others):
                blk = 2 * chip_idx + c
                copy(a, 1 + j, blk, (x, y, c)).wait_recv()
                fwd = copy(a, 4 + j, blk, sibling)
                fwd.start()
                passed.append(fwd)
        for a in range(n):
            copy(a, 0, 2 * my_chip + (1 - c), (x, y, c)).wait_recv()
            for j, (chip, chip_idx) in enumerate(others):
                copy(a, 4 + j, 2 * chip_idx + (1 - c), (x, y, c)).wait_recv()
        for cp in first + passed:
            cp.wait_send()
        for cp in local:
            cp.wait()

    return start, finish


def _all_gather(arrays):
    n = len(arrays)

    def body(*refs):
        start, finish = _gather_protocol(refs[:n], refs[n:2 * n], *refs[2 * n:])
        start()
        finish()

    out_shape = [jax.ShapeDtypeStruct((N_DEV,) + a.shape, a.dtype) for a in arrays]
    return _comm_call(body, arrays, out_shape, N_DEV - 1, "gather_weights")


def _exchange_sibling(arrays, name):
    n = len(arrays)

    def body(*refs):
        ins, got = refs[:n], refs[n:2 * n]
        send_sems, recv_sems, _ = refs[2 * n:]
        x, y, c, me, my_chip, sibling, others = _place()
        remote = []
        for a in range(n):
            for chip in range(N_CHIP):
                rc = pltpu.make_async_remote_copy(
                    src_ref=ins[a].at[2 * chip + (1 - c)], dst_ref=got[a].at[chip], send_sem=send_sems.at[a, chip],
                    recv_sem=recv_sems.at[a, chip], device_id=sibling, device_id_type=MESH)
                rc.start()
                remote.append(rc)
        for rc in remote:
            rc.wait()

    half = [jax.ShapeDtypeStruct((N_CHIP,) + a.shape[1:], a.dtype) for a in arrays]
    return _comm_call(body, arrays, half, N_CHIP, name)


def _chips_protocol(ins, outs, send_sems, recv_sems, local_sems):
    n = len(ins)
    x, y, c, me, my_chip, sibling, others = _place()
    local = [pltpu.make_async_copy(ins[a].at[my_chip], outs[a].at[my_chip], local_sems.at[a, 0]) for a in range(n)]
    remote = [pltpu.make_async_remote_copy(
        src_ref=ins[a].at[chip_idx], dst_ref=outs[a].at[my_chip], send_sem=send_sems.at[a, j],
        recv_sem=recv_sems.at[a, j], device_id=(*chip, c), device_id_type=MESH)
        for a in range(n) for j, (chip, chip_idx) in enumerate(others)]

    def start():
        for cp in local + remote:
            cp.start()

    def finish():
        for cp in remote + local:
            cp.wait()

    return start, finish


def _exchange_chips(arrays, name):
    n = len(arrays)

    def body(*refs):
        start, finish = _chips_protocol(refs[:n], refs[n:2 * n], *refs[2 * n:])
        start()
        finish()

    out_shape = [jax.ShapeDtypeStruct(a.shape, a.dtype) for a in arrays]
    return _comm_call(body, arrays, out_shape, N_CHIP - 1, name)


def _as_rows(a, lead):
    shp = a.shape
    return a.reshape(shp[:lead] + (math.prod(shp[lead:-1]), shp[-1]))


def _row_tile(rows, cap=512):
    if rows <= cap:
        return rows
    for t in range(cap - cap % 8, 0, -8):
        if rows % t == 0:
            return t
    return rows


def _pair_sum(send, got, name):
    a3, b3 = _as_rows(send, 1), _as_rows(got, 1)
    _, rows, last = b3.shape
    tr = _row_tile(rows)

    def body(c_ref, a_ref, b_ref, o_ref):
        o_ref[...] = (a_ref[...].astype(F32) + b_ref[...].astype(F32)).astype(o_ref.dtype)

    core = lax.axis_index("c").astype(jnp.int32).reshape(1)
    out = _pcall(
        body, name=name,
        grid_spec=pltpu.PrefetchScalarGridSpec(
            num_scalar_prefetch=1, grid=(N_CHIP, rows // tr),
            in_specs=[pl.BlockSpec((None, tr, last), lambda k, i, c_ref: (2 * k + c_ref[0], i, 0)),
                      pl.BlockSpec((None, tr, last), lambda k, i, c_ref: (k, i, 0))],
            out_specs=pl.BlockSpec((None, tr, last), lambda k, i, c_ref: (k, i, 0))),
        out_shape=jax.ShapeDtypeStruct(b3.shape, got.dtype), compiler_params=_params(("parallel", "parallel")),
    )(core, a3, b3)
    return out.reshape(got.shape)


def _adamw(parts, w, m, v, name):
    p3 = _as_rows(parts, 1)
    w2, m2, v2 = (_as_rows(z, 0) for z in (w, m, v))
    rows, last = w2.shape
    tr = _row_tile(rows)
    c1 = 1.0 / (1.0 - ADAM_B1 ** ADAM_STEP)
    c2 = 1.0 / (1.0 - ADAM_B2 ** ADAM_STEP)

    def body(p_ref, w_ref, m_ref, v_ref, g_ref, d_ref, nm_ref, nv_ref):
        g = p_ref[0].astype(F32)
        for chip in range(1, N_CHIP):
            g = g + p_ref[chip].astype(F32)
        mn = ADAM_B1 * m_ref[...] + (1.0 - ADAM_B1) * g
        vn = ADAM_B2 * v_ref[...] + (1.0 - ADAM_B2) * (g * g)
        g_ref[...] = g
        nm_ref[...] = mn
        nv_ref[...] = vn
        d_ref[...] = -ADAM_LR * ((mn * c1) / (jnp.sqrt(vn * c2) + ADAM_EPS) + ADAM_WD * w_ref[...])

    row = pl.BlockSpec((tr, last), lambda i: (i, 0))
    sh = jax.ShapeDtypeStruct((rows, last), F32)
    outs = _pcall(body, name=name, grid=(rows // tr,),
                  in_specs=[pl.BlockSpec((N_CHIP, tr, last), lambda i: (0, i, 0)), row, row, row],
                  out_specs=[row, row, row, row], out_shape=[sh, sh, sh, sh],
                  compiler_params=_params(("parallel",)))(p3, w2, m2, v2)
    return [o.reshape(w.shape) for o in outs]


def _pack(pieces, row_align):
    rows, offs, r = [], [], 0
    for a in pieces:
        flat = a.reshape(-1)
        nr = -(-flat.shape[0] // PACK_W)
        flat = jnp.pad(flat, (0, nr * PACK_W - flat.shape[0]))
        rows.append(flat.reshape(nr, PACK_W))
        offs.append(r)
        r += nr
    pad = (-r) % row_align
    if pad:
        rows.append(jnp.zeros((pad, PACK_W), pieces[0].dtype))
    return jnp.concatenate(rows, axis=0), offs


def _unpack(flat, offs, shapes):
    out = []
    for off, shp in zip(offs, shapes):
        size = math.prod(shp)
        nr = -(-size // PACK_W)
        out.append(flat[..., off:off + nr, :].reshape(flat.shape[:-2] + (nr * PACK_W,))[..., :size].reshape(flat.shape[:-2] + tuple(shp)))
    return out


def _to_full(gathered, axis):
    z = jnp.moveaxis(gathered, 0, axis)
    shp = list(z.shape)
    return z.reshape(shp[:axis] + [shp[axis] * shp[axis + 1]] + shp[axis + 2:])


def _to_shards(full, axis):
    shp = list(full.shape)
    z = full.reshape(shp[:axis] + [N_DEV, shp[axis] // N_DEV] + shp[axis + 1:])
    return jnp.moveaxis(z, axis, 0)


def kernel(x, p, ffn_norm, ffn_w_gate, ffn_w_up, ffn_w_down, mix_norm, att_w_in, att_q_norm, att_k_norm, att_sinks, att_w_out, gdn_w_in, gdn_conv_w, gdn_a_log, gdn_dt_bias, gdn_out_norm, gdn_w_out, ple_norm, ple_w_gate, ple_w_proj, loss_target, m_ffn_norm, m_ffn_w_gate, m_ffn_w_up, m_ffn_w_down, m_mix_norm, m_att_w_in, m_att_q_norm, m_att_k_norm, m_att_sinks, m_att_w_out, m_gdn_w_in, m_gdn_conv_w, m_gdn_a_log, m_gdn_dt_bias, m_gdn_out_norm, m_gdn_w_out, m_ple_norm, m_ple_w_gate, m_ple_w_proj, v_ffn_norm, v_ffn_w_gate, v_ffn_w_up, v_ffn_w_down, v_mix_norm, v_att_w_in, v_att_q_norm, v_att_k_norm, v_att_sinks, v_att_w_out, v_gdn_w_in, v_gdn_conv_w, v_gdn_a_log, v_gdn_dt_bias, v_gdn_out_norm, v_gdn_w_out, v_ple_norm, v_ple_w_gate, v_ple_w_proj):
    args = dict(locals())
    wts = {n: args[n] for n in WEIGHTS}
    mom = {n: args["m_" + n] for n in WEIGHTS}
    var = {n: args["v_" + n] for n in WEIGHTS}
    axis = dict(SHARDED)
    vecs = [n for n, _ in SHARDED[:SMALL_SHARDED]]
    small = vecs + list(REPLICATED)
    small_shapes = [wts[n].shape for n in small]
    lead = lambda n: 2 if n.startswith("ffn_") else 1

    def stack_of(arrays, name, idxs):
        return jnp.stack([arrays[name][idx] if idx else arrays[name][0] for idx in idxs])

    def full_instances(gathered, group):
        out = {}
        for (name, idxs), g in zip(group, gathered):
            whole = _to_full(g, axis[name] - lead(name) + 1)
            for k, idx in enumerate(idxs):
                out[(name,) + idx] = whole[k]
        return out

    def shard_stacks(g, group):
        return [_to_shards(jnp.stack([g[(name,) + idx] for idx in idxs]), axis[name] - lead(name) + 1)
                for name, idxs in group]

    vec_pack, voffs = _pack([wts[n] for n in vecs], 8)
    early = _all_gather([stack_of(wts, n, idxs).astype(BF16) for n, idxs in EARLY] + [vec_pack])
    w = full_instances(early[:-1], EARLY)
    vec_full = {n: _to_full(piece, axis[n]) for n, piece in
                zip(vecs, _unpack(early[-1], voffs, [wts[n].shape for n in vecs]))}
    w.update(_instances({**vec_full, **{n: wts[n] for n in REPLICATED}}))
    late_shards = [stack_of(wts, n, idxs).astype(BF16) for n, idxs in LATE]

    def early_grads(g):
        send = shard_stacks(g, RIDE)
        got = _exchange_sibling(send, "exchange_sibling_early")
        return [_pair_sum(p_, q_, f"pair_sum_early_{i}") for i, (p_, q_) in enumerate(zip(send, got))]

    loss, grad_x, g, rode = _local_step(x[0], p[:, 0], loss_target[0], w, late_shards,
                                        lambda gathered: full_instances(gathered, LATE), early_grads)
    loss = lax.psum(loss, ("x", "y", "c"))

    gs = _stacked({k: v for k, v in g.items() if k[0] in small})
    vec_shards = [_to_shards(gs[n], axis[n]) for n in vecs]
    small_send = jnp.stack([_pack([sh[d] for sh in vec_shards] + [gs[n] for n in REPLICATED], 8)[0] for d in range(N_DEV)])
    send = shard_stacks(g, FINAL) + [small_send]
    got = _exchange_sibling(send, "exchange_sibling_final")
    chip_sums = [_pair_sum(p_, q_, f"pair_sum_final_{i}") for i, (p_, q_) in enumerate(zip(send, got))]
    last = _exchange_chips(chip_sums, "exchange_chips_final")

    pieces = {}
    for (name, idxs), part in list(zip(RIDE, rode)) + list(zip(FINAL, last[:-1])):
        for k, idx in enumerate(idxs):
            pieces[(name,) + idx] = part[:, k]
    outs = {}
    for n, _ in SHARDED[SMALL_SHARDED:]:
        if lead(n) == 2:
            part = jnp.stack([jnp.stack([pieces[(n, i, j)] for j in range(2)], axis=1) for i in range(2)], axis=1)
        elif (n, 0) in pieces:
            part = jnp.stack([pieces[(n, i)] for i in range(2)], axis=1)
        else:
            part = pieces[(n,)][:, None]
        outs[n] = _adamw(part, wts[n], mom[n], var[n], f"adamw_{n}")
    small_w, soffs = _pack([wts[n] for n in small], 8)
    small_m, _ = _pack([mom[n] for n in small], 8)
    small_v, _ = _pack([var[n] for n in small], 8)
    small_out = [_unpack(z, soffs, small_shapes) for z in _adamw(last[-1], small_w, small_m, small_v, "adamw_small")]
    for i, n in enumerate(small):
        outs[n] = [small_out[k][i] for k in range(4)]
    result = [loss, grad_x[None]]
    for k in range(4):
        result += [outs[n][k] for n in WEIGHTS]
    return tuple(result)
```

```python
import math

import jax
import jax.numpy as jnp
from jax import lax
from jax.experimental import pallas as pl
from jax.experimental.pallas import tpu as pltpu

F32 = jnp.float32
BF16 = jnp.bfloat16

N_DEV = 8
D_MODEL = 1024
D_FF = 2816
PLE_DIM = 256
HEAD_DIM = 64
SB_HEADS = 8
SWA_HEADS = 8
SWA_KV_HEADS = 2
SWA_GROUP = SWA_HEADS // SWA_KV_HEADS
WINDOW = 128
Q_BLOCK = 128
GDN_K_HEADS = 8
GDN_V_HEADS = 16
GDN_HEAD_DIM = 128
GDN_CONV = 4
GDN_CHUNK = 64
EPS = 1e-6
SB_W = SB_HEADS * HEAD_DIM
SWA_QW = SWA_HEADS * HEAD_DIM
SWA_KVW = SWA_KV_HEADS * HEAD_DIM
ATT_IN = 3 * SB_W + SWA_QW + 2 * SWA_KVW
GDN_KW = GDN_K_HEADS * GDN_HEAD_DIM
GDN_VW = GDN_V_HEADS * GDN_HEAD_DIM
GDN_CONV_W = 2 * GDN_KW + GDN_VW
GDN_IN = GDN_CONV_W + GDN_VW + 2 * GDN_V_HEADS
GDN_IN_PAD = GDN_CONV_W + GDN_VW + 2 * 128

ADAM_LR = 0.001
ADAM_B1 = 0.9
ADAM_B2 = 0.999
ADAM_EPS = 1e-08
ADAM_WD = 0.01
ADAM_STEP = 10

LANE = 128
VMEM_LIMIT = 56 * 1024 * 1024
PACK_W = 1024

NN = ((1,), (0,))
NT = ((1,), (1,))
TN = ((0,), (0,))

SHARDED = (
    ("ffn_norm", 2), ("gdn_conv_w", 2),
    ("ffn_w_gate", 3), ("ffn_w_up", 3), ("ffn_w_down", 2), ("att_w_in", 2), ("att_w_out", 1),
    ("gdn_w_in", 2), ("gdn_w_out", 1), ("ple_w_gate", 1), ("ple_w_proj", 2),
)
SMALL_SHARDED = 2
REPLICATED = ("mix_norm", "att_q_norm", "att_k_norm", "att_sinks", "gdn_a_log", "gdn_dt_bias",
              "gdn_out_norm", "ple_norm")
WEIGHTS = ("ffn_norm", "ffn_w_gate", "ffn_w_up", "ffn_w_down", "mix_norm", "att_w_in", "att_q_norm",
           "att_k_norm", "att_sinks", "att_w_out", "gdn_w_in", "gdn_conv_w", "gdn_a_log", "gdn_dt_bias",
           "gdn_out_norm", "gdn_w_out", "ple_norm", "ple_w_gate", "ple_w_proj")


_FFN_REST = [(0, 1), (1, 0), (1, 1)]
EARLY = [("ffn_w_gate", [(0, 0)]), ("ffn_w_up", [(0, 0)]), ("ffn_w_down", [(0, 0)]), ("att_w_in", [()])]
LATE = [("ffn_w_gate", _FFN_REST), ("ffn_w_up", _FFN_REST), ("ffn_w_down", _FFN_REST), ("att_w_out", [()]),
        ("gdn_w_in", [()]), ("gdn_w_out", [()]), ("ple_w_gate", [(0,), (1,)]), ("ple_w_proj", [(0,), (1,)])]
RIDE = [e for e in LATE if e[0] != "att_w_out"]
FINAL = EARLY + [("att_w_out", [()])]


def _pcall(body, **kw):
    return pl.pallas_call(body, **kw)


def _params(sem=None):
    if sem is None:
        return pltpu.CompilerParams(vmem_limit_bytes=VMEM_LIMIT)
    return pltpu.CompilerParams(dimension_semantics=sem, vmem_limit_bytes=VMEM_LIMIT)


def _dot(a, b, dims=NN):
    return lax.dot_general(a, b, (dims, ((), ())), preferred_element_type=F32)


def _bdot(a, b, dims=NN):
    return _dot(a.astype(BF16), b.astype(BF16), dims)


def _split(a):
    hi = a.astype(BF16)
    lo = (a - hi.astype(F32)).astype(BF16)
    return hi, lo


def _dot3(a, b, dims=NN):
    ah, al = _split(a)
    bh, bl = _split(b)
    return _dot(ah, bh, dims) + (_dot(ah, bl, dims) + _dot(al, bh, dims))


def _dot2m(a, m, dims=NN):
    ah, al = _split(a)
    return _dot(ah, m, dims) + _dot(al, m, dims)


def _mdot2(m, a, dims=NN):
    ah, al = _split(a)
    return _dot(m, ah, dims) + _dot(m, al, dims)


def _sigmoid(x):
    return 1.0 / (1.0 + jnp.exp(-x))


def _softplus(x):
    return jnp.maximum(x, 0.0) + jnp.log(1.0 + jnp.exp(-jnp.abs(x)))


def _pick(n, cap):
    if n <= cap:
        return n
    for t in range(cap - cap % LANE, 0, -LANE):
        if n % t == 0:
            return t
    raise ValueError(f"no tile for {n} under {cap}")


def _iota2(shape, axis):
    return lax.broadcasted_iota(jnp.int32, shape, axis)


def _mm(a, b, mode, out_dtype=F32, res=None, alpha=1.0, a2=None, b2=None, name="mm"):
    if mode == "nn":
        (M, K), N = a.shape, b.shape[1]
    elif mode == "nt":
        (M, K), N = a.shape, b.shape[0]
    else:
        (K, M), N = a.shape, b.shape[1]
    tm, tn, tk = _pick(M, 1408 if mode == "tn" else 512), _pick(N, 1408), _pick(K, 1024 if mode == "tn" else 1408)
    nk = K // tk
    dims = {"nn": NN, "nt": NT, "tn": TN}[mode]
    a_spec = pl.BlockSpec((tk, tm), lambda i, j, k: (k, i)) if mode == "tn" else pl.BlockSpec((tm, tk), lambda i, j, k: (i, k))
    b_spec = pl.BlockSpec((tn, tk), lambda i, j, k: (j, k)) if mode == "nt" else pl.BlockSpec((tk, tn), lambda i, j, k: (k, j))
    o_spec = pl.BlockSpec((tm, tn), lambda i, j, k: (i, j))
    two = a2 is not None
    has_res = res is not None

    def body(*refs):
        refs = list(refs)
        a_ref, b_ref = refs[0], refs[1]
        pos = 2
        if two:
            a2_ref, b2_ref = refs[2], refs[3]
            pos = 4
        if has_res:
            res_ref = refs[pos]
            pos += 1
        o_ref, acc_ref = refs[pos], refs[pos + 1]
        k = pl.program_id(2)
        part = _bdot(a_ref[...], b_ref[...], dims)
        if two:
            part = part + _bdot(a2_ref[...], b2_ref[...], dims)

        def finish(acc):
            out = acc * alpha if alpha != 1.0 else acc
            if has_res:
                out = res_ref[...] + out
            o_ref[...] = out.astype(out_dtype)

        if nk == 1:
            finish(part)
        else:
            @pl.when(k == 0)
            def _():
                acc_ref[...] = part

            @pl.when(k > 0)
            def _():
                acc_ref[...] += part

            @pl.when(k == nk - 1)
            def _():
                finish(acc_ref[...])

    ins = [a, b]
    specs = [a_spec, b_spec]
    if two:
        ins += [a2, b2]
        specs += [a_spec, b_spec]
    if has_res:
        ins.append(res)
        specs.append(o_spec)
    return _pcall(
        body, name=name, grid=(M // tm, N // tn, nk), in_specs=specs, out_specs=o_spec,
        out_shape=jax.ShapeDtypeStruct((M, N), out_dtype),
        scratch_shapes=[pltpu.VMEM((tm, tn) if nk > 1 else (8, LANE), F32)],
        compiler_params=_params(("parallel", "parallel", "arbitrary")),
    )(*ins)


ROW_TILE = 256


def _rms_fwd(h, g, name):
    T, D = h.shape
    tr = _pick(T, ROW_TILE)

    def body(h_ref, g_ref, n_ref):
        x = h_ref[...]
        r = lax.rsqrt(jnp.mean(x * x, axis=-1, keepdims=True) + EPS)
        n_ref[...] = (x * r * g_ref[...]).astype(BF16)

    return _pcall(
        body, name=name, grid=(T // tr,),
        in_specs=[pl.BlockSpec((tr, D), lambda i: (i, 0)), pl.BlockSpec((1, D), lambda i: (0, 0))],
        out_specs=pl.BlockSpec((tr, D), lambda i: (i, 0)),
        out_shape=jax.ShapeDtypeStruct((T, D), BF16), compiler_params=_params(("parallel",)),
    )(h, g.reshape(1, D))


def _rms_bwd(dn, h, g, dres, name):
    T, D = h.shape
    tr = _pick(T, ROW_TILE)

    def body(dn_ref, h_ref, g_ref, dres_ref, dh_ref, dg_ref):
        x = h_ref[...]
        r = lax.rsqrt(jnp.mean(x * x, axis=-1, keepdims=True) + EPS)
        xh = x * r
        d = dn_ref[...].astype(F32)
        dxh = d * g_ref[...]
        dh_ref[...] = dres_ref[...] + r * (dxh - xh * jnp.mean(dxh * xh, axis=-1, keepdims=True))
        part = jnp.sum(d * xh, axis=0, keepdims=True)

        @pl.when(pl.program_id(0) == 0)
        def _():
            dg_ref[...] = part

        @pl.when(pl.program_id(0) > 0)
        def _():
            dg_ref[...] += part

    row = pl.BlockSpec((tr, D), lambda i: (i, 0))
    vec = pl.BlockSpec((1, D), lambda i: (0, 0))
    dh, dg = _pcall(
        body, name=name, grid=(T // tr,), in_specs=[row, row, vec, row], out_specs=[row, vec],
        out_shape=[jax.ShapeDtypeStruct((T, D), F32), jax.ShapeDtypeStruct((1, D), F32)],
        compiler_params=_params(("arbitrary",)),
    )(dn, h, g.reshape(1, D), dres)
    return dh, dg.reshape(D)


def _gateup(n, wg, wu, name):
    T, D = n.shape
    F = wg.shape[1]
    tm, tn = _pick(T, 512), _pick(F, 1408)

    def body(n_ref, wg_ref, wu_ref, a_ref, b_ref, hid_ref):
        x = n_ref[...]
        a = _dot(x, wg_ref[...])
        b = _dot(x, wu_ref[...])
        a_ref[...] = a.astype(BF16)
        b_ref[...] = b.astype(BF16)
        hid_ref[...] = (a * _sigmoid(a) * b).astype(BF16)

    o_spec = pl.BlockSpec((tm, tn), lambda i, j: (i, j))
    w_spec = pl.BlockSpec((D, tn), lambda i, j: (0, j))
    sh = jax.ShapeDtypeStruct((T, F), BF16)
    return _pcall(
        body, name=name, grid=(T // tm, F // tn),
        in_specs=[pl.BlockSpec((tm, D), lambda i, j: (i, 0)), w_spec, w_spec],
        out_specs=[o_spec, o_spec, o_spec], out_shape=[sh, sh, sh],
        compiler_params=_params(("parallel", "parallel")),
    )(n, wg, wu)


def _ffn_dhid(dy, wd, a, b, name):
    T, D = dy.shape
    F = wd.shape[0]
    tm, tn = _pick(T, 512), _pick(F, 1408)

    def body(dy_ref, wd_ref, a_ref, b_ref, da_ref, db_ref):
        dhid = 0.5 * _bdot(dy_ref[...], wd_ref[...], NT)
        av = a_ref[...].astype(F32)
        bv = b_ref[...].astype(F32)
        s = _sigmoid(av)
        da_ref[...] = (dhid * bv * s * (1.0 + av * (1.0 - s))).astype(BF16)
        db_ref[...] = (dhid * av * s).astype(BF16)

    o_spec = pl.BlockSpec((tm, tn), lambda i, j: (i, j))
    sh = jax.ShapeDtypeStruct((T, F), BF16)
    return _pcall(
        body, name=name, grid=(T // tm, F // tn),
        in_specs=[pl.BlockSpec((tm, D), lambda i, j: (i, 0)), pl.BlockSpec((tn, D), lambda i, j: (j, 0)), o_spec, o_spec],
        out_specs=[o_spec, o_spec], out_shape=[sh, sh],
        compiler_params=_params(("parallel", "parallel")),
    )(dy, wd, a, b)


def _ffn_fwd(h, g, wg, wu, wd, tag):
    n = _rms_fwd(h, g, f"{tag}_norm")
    a, b, hid = _gateup(n, wg, wu, f"{tag}_gateup")
    h2 = _mm(hid, wd, "nn", res=h, alpha=0.5, name=f"{tag}_down")
    return h2, (h, n, a, b, hid)


def _ffn_bwd(dh2, saved, g, wg, wu, wd, tag):
    h, n, a, b, hid = saved
    da, db = _ffn_dhid(dh2, wd, a, b, f"{tag}_dhid")
    dwd = _mm(hid, dh2, "tn", alpha=0.5, out_dtype=BF16, name=f"{tag}_dwd")
    dwg = _mm(n, da, "tn", out_dtype=BF16, name=f"{tag}_dwg")
    dwu = _mm(n, db, "tn", out_dtype=BF16, name=f"{tag}_dwu")
    dn = _mm(da, wg, "nt", a2=db, b2=wu, name=f"{tag}_dn")
    dh, dg = _rms_bwd(dn, h, g, dh2, f"{tag}_dnorm")
    return dh, dg, dwg, dwu, dwd


def _ple_fwd(h, p, g, w_gate, w_proj, tag):
    T, D = h.shape
    pn = _rms_fwd(h, g, f"{tag}_norm")
    tm, tn = _pick(T, 512), _pick(D, 1024)
    P = p.shape[1]

    def body(pn_ref, p_ref, wg_ref, wp_ref, h_ref, o_ref, gl_ref, pp_ref):
        gl = _dot(pn_ref[...], wg_ref[...])
        pp = _bdot(p_ref[...], wp_ref[...])
        gl_ref[...] = gl
        pp_ref[...] = pp
        o_ref[...] = h_ref[...] + _sigmoid(gl) * pp

    o_spec = pl.BlockSpec((tm, tn), lambda i, j: (i, j))
    sh = jax.ShapeDtypeStruct((T, D), F32)
    h2, gl, pp = _pcall(
        body, name=f"{tag}_fwd", grid=(T // tm, D // tn),
        in_specs=[pl.BlockSpec((tm, D), lambda i, j: (i, 0)), pl.BlockSpec((tm, P), lambda i, j: (i, 0)),
                  pl.BlockSpec((D, tn), lambda i, j: (0, j)), pl.BlockSpec((P, tn), lambda i, j: (0, j)), o_spec],
        out_specs=[o_spec, o_spec, o_spec], out_shape=[sh, sh, sh],
        compiler_params=_params(("parallel", "parallel")),
    )(pn, p, w_gate, w_proj, h)
    return h2, (h, pn, gl, pp)


def _ple_bwd(dh2, saved, p, g, w_gate, tag):
    h, pn, gl, pp = saved
    T, D = h.shape
    tr = _pick(T, ROW_TILE)

    def body(d_ref, gl_ref, pp_ref, dgl_ref, dpp_ref):
        d = d_ref[...]
        s = _sigmoid(gl_ref[...])
        dpp_ref[...] = (d * s).astype(BF16)
        dgl_ref[...] = (d * pp_ref[...] * s * (1.0 - s)).astype(BF16)

    row = pl.BlockSpec((tr, D), lambda i: (i, 0))
    sh = jax.ShapeDtypeStruct((T, D), BF16)
    dgl, dpp = _pcall(body, name=f"{tag}_dgate", grid=(T // tr,), in_specs=[row, row, row], out_specs=[row, row],
                      out_shape=[sh, sh], compiler_params=_params(("parallel",)))(dh2, gl, pp)
    dw_proj = _mm(p, dpp, "tn", out_dtype=BF16, name=f"{tag}_dwproj")
    dw_gate = _mm(pn, dgl, "tn", out_dtype=BF16, name=f"{tag}_dwgate")
    dpn = _mm(dgl, w_gate, "nt", name=f"{tag}_dpn")
    dh, dg = _rms_bwd(dpn, h, g, dh2, f"{tag}_dnorm")
    return dh, dg, dw_gate, dw_proj


def _loss_head(y, target):
    T, D = y.shape
    tr = _pick(T, ROW_TILE)

    def body(y_ref, t_ref, dy_ref, l_ref):
        e = y_ref[...] - t_ref[...]
        dy_ref[...] = e * (1.0 / D)
        part = jnp.sum(e * e, axis=0, keepdims=True)

        @pl.when(pl.program_id(0) == 0)
        def _():
            l_ref[...] = part

        @pl.when(pl.program_id(0) > 0)
        def _():
            l_ref[...] += part

    row = pl.BlockSpec((tr, D), lambda i: (i, 0))
    vec = pl.BlockSpec((1, D), lambda i: (0, 0))
    dy, l = _pcall(body, name="loss_head", grid=(T // tr,), in_specs=[row, row], out_specs=[row, vec],
                   out_shape=[jax.ShapeDtypeStruct((T, D), F32), jax.ShapeDtypeStruct((1, D), F32)],
                   compiler_params=_params(("arbitrary",)))(y, target)
    return (0.5 / D) * jnp.sum(l), dy


SB_GROUP_FWD = 8
SB_GROUP_BWD = 8


def _sb_consts():
    row = _iota2((Q_BLOCK, Q_BLOCK), 0)
    col = _iota2((Q_BLOCK, Q_BLOCK), 1)
    after = (row > col).astype(BF16)
    before = (row < col).astype(BF16)
    return col < row, after, before, col


def _ride_specs(ride, out_shapes, n_sems):
    hbm = pl.BlockSpec(memory_space=pl.ANY)
    n = len(ride)
    sems = [pltpu.SemaphoreType.DMA((n, n_sems)), pltpu.SemaphoreType.DMA((n, n_sems)),
            pltpu.SemaphoreType.DMA((n, N_CHIP))] if n else []
    return [hbm] * n, [hbm] * len(out_shapes), sems


def _sb_fwd(q, kv, ride=()):
    H, T, d2 = q.shape
    d = d2 // 2
    nblk = T // Q_BLOCK
    scale = d ** -0.5
    G = SB_GROUP_FWD
    n = len(ride)
    ride_out = [jax.ShapeDtypeStruct((N_DEV,) + a.shape, a.dtype) for a in ride]
    ride_in_specs, ride_out_specs, ride_sems = _ride_specs(ride, ride_out, N_DEV - 1)

    def body(*refs):
        q_ref, kv_ref = refs[:2]
        rin = refs[2:2 + n]
        o_ref, c_ref = refs[2 + n:4 + n]
        rout = refs[4 + n:4 + 2 * n]
        run_ref = refs[4 + 2 * n]
        i = pl.program_id(1)
        if n:
            start, finish = _gather_protocol(rin, rout, *refs[5 + 2 * n:])
            pl.when((pl.program_id(0) == 0) & (i == 0))(start)
        causal, after, _, col = _sb_consts()
        qs = [q_ref[g] * scale for g in range(G)]
        o_ref[...] = jnp.zeros_like(o_ref)
        c_ref[...] = jnp.zeros_like(c_ref)
        run_ref[...] = jnp.zeros_like(run_ref)

        def pair(j, diag):
            off = pl.multiple_of(j * Q_BLOCK, Q_BLOCK)
            R = range(G)
            kvj = [kv_ref[g, pl.ds(off, Q_BLOCK), :] for g in R]
            c = [run_ref[g] for g in R]
            acc = [o_ref[g] for g in R]
            cm = None if diag else [c_ref[g] for g in R]
            z = [_dot(qs[g], kvj[g], NT) for g in R]
            sp = [_softplus(z[g]) for g in R]
            lk = [jnp.where(causal, -sp[g], 0.0) if diag else -sp[g] for g in R]
            btw = [_dot2m(lk[g], after) for g in R]
            e = [jnp.exp((z[g] - sp[g]) + btw[g] + c[g]) for g in R]
            w = [jnp.where(causal, e[g], 0.0) if diag else e[g] for g in R]
            pv = [_bdot(w[g], kvj[g]) for g in R]
            rs = [jnp.sum(lk[g], axis=1, keepdims=True) for g in R]
            for g in R:
                o_ref[g] = acc[g] + pv[g]
                if not diag:
                    c_ref[g] = jnp.where(col == j, c[g], cm[g])
                run_ref[g] = c[g] + rs[g]

        pair(i, True)

        @pl.loop(0, i)
        def _(jj):
            pair(i - 1 - jj, False)

        if n:
            pl.when((pl.program_id(0) == H // G - 1) & (i == nblk - 1))(finish)

    blk = pl.BlockSpec((G, Q_BLOCK, d2), lambda h, i: (h, i, 0))
    full = pl.BlockSpec((G, T, d2), lambda h, i: (h, 0, 0))
    res = _pcall(
        body, name="sb_fwd", grid=(H // G, nblk), in_specs=[blk, full] + ride_in_specs,
        out_specs=[blk, pl.BlockSpec((G, Q_BLOCK, LANE), lambda h, i: (h, i, 0))] + ride_out_specs,
        out_shape=[jax.ShapeDtypeStruct((H, T, d2), F32), jax.ShapeDtypeStruct((H, T, LANE), F32)] + ride_out,
        scratch_shapes=[pltpu.VMEM((G, Q_BLOCK, 1), F32)] + ride_sems,
        compiler_params=_params(("arbitrary", "arbitrary") if n else ("parallel", "parallel")),
    )(q, kv, *ride)
    return res[0], res[1], list(res[2:])


def _sb_bwd(q, kv, carry, do, ride=()):
    H, T, d2 = q.shape
    d = d2 // 2
    nblk = T // Q_BLOCK
    scale = d ** -0.5
    G = SB_GROUP_BWD
    n = len(ride)
    ride_out = [jax.ShapeDtypeStruct(a.shape, a.dtype) for a in ride]
    ride_in_specs, ride_out_specs, ride_sems = _ride_specs(ride, ride_out, N_CHIP - 1)

    def body(*refs):
        q_ref, kv_ref, c_ref, do_ref = refs[:4]
        rin = refs[4:4 + n]
        dq_ref, dkv_ref = refs[4 + n:6 + n]
        rout = refs[6 + n:6 + 2 * n]
        run_ref = refs[6 + 2 * n]
        i = pl.program_id(1)
        if n:
            start, finish = _chips_protocol(rin, rout, *refs[7 + 2 * n:])
            pl.when((pl.program_id(0) == 0) & (i == 0))(start)

        @pl.when(i == 0)
        def _():
            dkv_ref[...] = jnp.zeros_like(dkv_ref)

        causal, after, before, col = _sb_consts()
        qs = [q_ref[g] * scale for g in range(G)]
        dov = [do_ref[g].astype(BF16) for g in range(G)]
        dq_ref[...] = jnp.zeros_like(dq_ref)
        run_ref[...] = jnp.zeros_like(run_ref)

        def pair(j, diag):
            off = pl.multiple_of(j * Q_BLOCK, Q_BLOCK)
            R = range(G)
            rows = pl.ds(off, Q_BLOCK)
            kvj = [kv_ref[g, rows, :] for g in R]
            gsum = [run_ref[g] for g in R]
            dq0 = [dq_ref[g] for g in R]
            dkv0 = [dkv_ref[g, rows, :] for g in R]
            cm = None if diag else [c_ref[g] for g in R]
            z = [_dot(qs[g], kvj[g], NT) for g in R]
            sp = [_softplus(z[g]) for g in R]
            lk = [jnp.where(causal, -sp[g], 0.0) if diag else -sp[g] for g in R]
            ls = [z[g] - sp[g] for g in R]
            logw = [ls[g] + _dot2m(lk[g], after) for g in R]
            if not diag:
                logw = [logw[g] + jnp.sum(jnp.where(col == j, cm[g], 0.0), axis=1, keepdims=True) for g in R]
            e = [jnp.exp(logw[g]) for g in R]
            w = [jnp.where(causal, e[g], 0.0) if diag else e[g] for g in R]
            gw = [_dot(dov[g], kvj[g], NT) * w[g] for g in R]
            gpre = [gsum[g] + _dot2m(gw[g], before) for g in R]
            sig = [jnp.exp(ls[g]) for g in R]
            dz = [gw[g] * (1.0 - sig[g]) - sig[g] * gpre[g] for g in R]
            if diag:
                dz = [jnp.where(causal, dz[g], 0.0) for g in R]
            dzb = [dz[g].astype(BF16) for g in R]
            dq1 = [_dot(dzb[g], kvj[g]) for g in R]
            dkv1 = [_dot(dzb[g], qs[g], TN) + _dot(w[g].astype(BF16), dov[g], TN) for g in R]
            gs1 = [jnp.sum(gw[g], axis=1, keepdims=True) for g in R]
            for g in R:
                dq_ref[g] = dq0[g] + dq1[g]
                dkv_ref[g, rows, :] = dkv0[g] + dkv1[g]
                run_ref[g] = gsum[g] + gs1[g]

        @pl.loop(0, i)
        def _(j):
            pair(j, False)

        pair(i, True)
        dq_ref[...] = dq_ref[...] * scale
        if n:
            pl.when((pl.program_id(0) == H // G - 1) & (i == nblk - 1))(finish)

    blk = pl.BlockSpec((G, Q_BLOCK, d2), lambda h, i: (h, i, 0))
    full = pl.BlockSpec((G, T, d2), lambda h, i: (h, 0, 0), pipeline_mode=pl.Buffered(1))
    sh2 = jax.ShapeDtypeStruct((H, T, d2), F32)
    res = _pcall(
        body, name="sb_bwd", grid=(H // G, nblk),
        in_specs=[blk, full, pl.BlockSpec((G, Q_BLOCK, LANE), lambda h, i: (h, i, 0)), blk] + ride_in_specs,
        out_specs=[blk, full] + ride_out_specs, out_shape=[sh2, sh2] + ride_out,
        scratch_shapes=[pltpu.VMEM((G, Q_BLOCK, 1), F32)] + ride_sems,
        compiler_params=_params(("arbitrary", "arbitrary") if n else ("parallel", "arbitrary")),
    )(q, kv, carry, do, *ride)
    return res[0], res[1], list(res[2:])


def _swa_fwd(q, k, v, qg, kg, sinks, slopes):
    Hq, T, d = q.shape
    Hkv = k.shape[0]
    G = Hq // Hkv
    W = WINDOW
    nblk = T // W
    scale = d ** -0.5

    def body(q_ref, kp_ref, kc_ref, vp_ref, vc_ref, qg_ref, kg_ref, sk_ref, sl_ref, o_ref):
        hk = pl.program_id(0)
        n = pl.program_id(1)
        kcat = jnp.concatenate([kp_ref[...], kc_ref[...]], axis=0)
        vcat = jnp.concatenate([vp_ref[...], vc_ref[...]], axis=0).astype(BF16)
        rk = lax.rsqrt(jnp.mean(kcat * kcat, axis=-1, keepdims=True) + EPS)
        kn = (kcat * rk * kg_ref[...]).astype(BF16)
        row = _iota2((W, 2 * W), 0)
        col = _iota2((W, 2 * W), 1)
        dist = row + W - col
        valid = (dist >= 0) & (dist < W) & ((n > 0) | (col >= W))
        distf = dist.astype(F32)
        for g in range(G):
            qh = q_ref[g]
            rq = lax.rsqrt(jnp.mean(qh * qh, axis=-1, keepdims=True) + EPS)
            qn = (qh * rq * qg_ref[...]).astype(BF16)
            sink = sk_ref[pl.ds(hk * G + g, 1), :][:, :1]
            slope = sl_ref[pl.ds(hk * G + g, 1), :][:, :1]
            s = _dot(qn, kn, NT) * scale - slope * distf
            s = jnp.where(valid, s, -1e30)
            m = jnp.maximum(jnp.max(s, axis=1, keepdims=True), sink)
            p = jnp.where(valid, jnp.exp(s - m), 0.0)
            den = jnp.sum(p, axis=1, keepdims=True) + jnp.exp(sink - m)
            o_ref[g] = _bdot(p / den, vcat)

    qblk = pl.BlockSpec((G, W, d), lambda h, n: (h, n, 0))
    prev = pl.BlockSpec((None, W, d), lambda h, n: (h, jnp.maximum(n - 1, 0), 0))
    cur = pl.BlockSpec((None, W, d), lambda h, n: (h, n, 0))
    gain = pl.BlockSpec((1, d), lambda h, n: (0, 0))
    perhead = pl.BlockSpec((Hq, LANE), lambda h, n: (0, 0))
    return _pcall(
        body, name="swa_fwd", grid=(Hkv, nblk),
        in_specs=[qblk, prev, cur, prev, cur, gain, gain, perhead, perhead], out_specs=qblk,
        out_shape=jax.ShapeDtypeStruct((Hq, T, d), F32), compiler_params=_params(("parallel", "parallel")),
    )(q, k, k, v, v, qg, kg, sinks, slopes)


def _swa_bwd(q, k, v, qg, kg, sinks, slopes, do):
    Hq, T, d = q.shape
    Hkv = k.shape[0]
    G = Hq // Hkv
    W = WINDOW
    nblk = T // W
    scale = d ** -0.5

    def body(q_ref, kp_ref, kc_ref, vp_ref, vc_ref, qg_ref, kg_ref, sk_ref, sl_ref, do_ref,
             dq_ref, dk_ref, dv_ref, dqg_ref, dkg_ref, dsk_ref):
        hk = pl.program_id(0)
        n = pl.program_id(1)

        @pl.when((hk == 0) & (n == 0))
        def _():
            dqg_ref[...] = jnp.zeros_like(dqg_ref)
            dkg_ref[...] = jnp.zeros_like(dkg_ref)
            dsk_ref[...] = jnp.zeros_like(dsk_ref)

        @pl.when(n == 0)
        def _():
            dk_ref[...] = jnp.zeros_like(dk_ref)
            dv_ref[...] = jnp.zeros_like(dv_ref)

        kcat = jnp.concatenate([kp_ref[...], kc_ref[...]], axis=0)
        vcat = jnp.concatenate([vp_ref[...], vc_ref[...]], axis=0).astype(BF16)
        rk = lax.rsqrt(jnp.mean(kcat * kcat, axis=-1, keepdims=True) + EPS)
        kh = kcat * rk
        kn = (kh * kg_ref[...]).astype(BF16)
        row = _iota2((W, 2 * W), 0)
        col = _iota2((W, 2 * W), 1)
        dist = row + W - col
        valid = (dist >= 0) & (dist < W) & ((n > 0) | (col >= W))
        distf = dist.astype(F32)
        rowh = _iota2((Hq, LANE), 0)
        dkn = jnp.zeros((2 * W, d), F32)
        dvc = jnp.zeros((2 * W, d), F32)
        dqg = jnp.zeros((1, d), F32)
        dsk = jnp.zeros((Hq, LANE), F32)
        for g in range(G):
            qh = q_ref[g]
            rq = lax.rsqrt(jnp.mean(qh * qh, axis=-1, keepdims=True) + EPS)
            qhh = qh * rq
            qn = (qhh * qg_ref[...]).astype(BF16)
            sink = sk_ref[pl.ds(hk * G + g, 1), :][:, :1]
            slope = sl_ref[pl.ds(hk * G + g, 1), :][:, :1]
            s = _dot(qn, kn, NT) * scale - slope * distf
            s = jnp.where(valid, s, -1e30)
            m = jnp.maximum(jnp.max(s, axis=1, keepdims=True), sink)
            p = jnp.where(valid, jnp.exp(s - m), 0.0)
            esink = jnp.exp(sink - m)
            den = jnp.sum(p, axis=1, keepdims=True) + esink
            prob = p / den
            dov = do_ref[g].astype(BF16)
            dp = _dot(dov, vcat, NT)
            dd = jnp.sum(prob * dp, axis=1, keepdims=True)
            ds = prob * (dp - dd)
            dsink = -jnp.sum((esink / den) * dd, axis=0, keepdims=True)
            dsk = dsk + jnp.where(rowh == hk * G + g, dsink, 0.0)
            dsb = (ds * scale).astype(BF16)
            dqn = _dot(dsb, kn)
            dkn = dkn + _dot(dsb, qn, TN)
            dvc = dvc + _dot(prob.astype(BF16), dov, TN)
            dqh = dqn * qg_ref[...]
            dq_ref[g] = rq * (dqh - qhh * jnp.mean(dqh * qhh, axis=-1, keepdims=True))
            dqg = dqg + jnp.sum(dqn * qhh, axis=0, keepdims=True)
        dkh = dkn * kg_ref[...]
        dkraw = rk * (dkh - kh * jnp.mean(dkh * kh, axis=-1, keepdims=True))
        dqg_ref[...] += dqg
        dkg_ref[...] += jnp.sum(dkn * kh, axis=0, keepdims=True)
        dsk_ref[...] += dsk
        offp = pl.multiple_of(jnp.maximum(n - 1, 0) * W, W)
        offc = pl.multiple_of(n * W, W)
        dk_ref[pl.ds(offp, W), :] += dkraw[:W]
        dv_ref[pl.ds(offp, W), :] += dvc[:W]
        dk_ref[pl.ds(offc, W), :] += dkraw[W:]
        dv_ref[pl.ds(offc, W), :] += dvc[W:]

    qblk = pl.BlockSpec((G, W, d), lambda h, n: (h, n, 0))
    prev = pl.BlockSpec((None, W, d), lambda h, n: (h, jnp.maximum(n - 1, 0), 0))
    cur = pl.BlockSpec((None, W, d), lambda h, n: (h, n, 0))
    gain = pl.BlockSpec((1, d), lambda h, n: (0, 0))
    perhead = pl.BlockSpec((Hq, LANE), lambda h, n: (0, 0))
    full = pl.BlockSpec((None, T, d), lambda h, n: (h, 0, 0))
    kv = jax.ShapeDtypeStruct((Hkv, T, d), F32)
    gs = jax.ShapeDtypeStruct((1, d), F32)
    return _pcall(
        body, name="swa_bwd", grid=(Hkv, nblk),
        in_specs=[qblk, prev, cur, prev, cur, gain, gain, perhead, perhead, qblk],
        out_specs=[qblk, full, full, gain, gain, perhead],
        out_shape=[jax.ShapeDtypeStruct((Hq, T, d), F32), kv, kv, gs, gs, jax.ShapeDtypeStruct((Hq, LANE), F32)],
        compiler_params=_params(("arbitrary", "arbitrary")),
    )(q, k, k, v, v, qg, kg, sinks, slopes, do)


def _heads(z, n):
    T = z.shape[0]
    return z.reshape(T, n, HEAD_DIM).transpose(1, 0, 2)


def _unheads(z):
    n, T, d = z.shape
    return z.transpose(1, 0, 2).reshape(T, n * d)


def _alibi():
    s = [2.0 ** (-8.0 * (i + 1) / SWA_HEADS) for i in range(SWA_HEADS)]
    return jnp.broadcast_to(jnp.asarray(s, F32)[:, None], (SWA_HEADS, LANE))


def _att_fwd(h, g, w_in, w_out_of, q_gain, k_gain, sinks, ride=()):
    hn = _rms_fwd(h, g, "att_norm")
    proj = _mm(hn, w_in, "nn", name="att_in")
    c = [0, SB_W, 2 * SB_W, 3 * SB_W, 3 * SB_W + SWA_QW, 3 * SB_W + SWA_QW + SWA_KVW, ATT_IN]
    sq, sk, sv = (_heads(proj[:, c[i]:c[i + 1]], SB_HEADS).astype(BF16) for i in range(3))
    bq = _heads(proj[:, c[3]:c[4]], SWA_HEADS)
    bk = _heads(proj[:, c[4]:c[5]], SWA_KV_HEADS)
    bv = _heads(proj[:, c[5]:c[6]], SWA_KV_HEADS)
    skv = jnp.concatenate([sk, sv], axis=-1)
    sq = jnp.concatenate([sq, jnp.zeros_like(sq)], axis=-1)
    a_out, carry, gathered = _sb_fwd(sq, skv, ride)
    a_out = a_out[..., HEAD_DIM:]
    w_out = w_out_of(gathered)
    sk128 = jnp.broadcast_to(sinks.reshape(SWA_HEADS, 1), (SWA_HEADS, LANE))
    qg, kg = q_gain.reshape(1, HEAD_DIM), k_gain.reshape(1, HEAD_DIM)
    b_out = _swa_fwd(bq, bk, bv, qg, kg, sk128, _alibi())
    o = jnp.concatenate([_unheads(a_out), _unheads(b_out)], axis=-1).astype(BF16)
    h2 = _mm(o, w_out, "nn", res=h, name="att_out")
    return h2, (h, hn, sq, skv, bq, bk, bv, carry, o, sk128, qg, kg), gathered


def _att_bwd(dh2, saved, g, w_in, w_out, ride=()):
    h, hn, sq, skv, bq, bk, bv, carry, o, sk128, qg, kg = saved
    do = _mm(dh2, w_out, "nt", name="att_do")
    dw_out = _mm(o, dh2, "tn", out_dtype=BF16, name="att_dwout")
    da = _heads(do[:, :SB_W], SB_HEADS)
    db = _heads(do[:, SB_W:], SWA_HEADS)
    dsq, dskv, rode = _sb_bwd(sq, skv, carry, jnp.concatenate([jnp.zeros_like(da), da], axis=-1), ride)
    dsq, dsk, dsv = dsq[..., :HEAD_DIM], dskv[..., :HEAD_DIM], dskv[..., HEAD_DIM:]
    dbq, dbk, dbv, dqg, dkg, dsink = _swa_bwd(bq, bk, bv, qg, kg, sk128, _alibi(), db)
    dproj = jnp.concatenate([_unheads(z) for z in (dsq, dsk, dsv, dbq, dbk, dbv)], axis=-1).astype(BF16)
    dw_in = _mm(hn, dproj, "tn", out_dtype=BF16, name="att_dwin")
    dhn = _mm(dproj, w_in, "nt", name="att_dhn")
    dh, dg = _rms_bwd(dhn, h, g, dh2, "att_dnorm")
    return dh, dg, dw_in, dw_out, dqg.reshape(HEAD_DIM), dkg.reshape(HEAD_DIM), dsink[:, 0], rode


CONV_ROWS = 512
CONV_COLS = 512
HALO = 8


def _shifted(xcat, s, tm):
    if s == 0:
        return xcat[HALO:HALO + tm]
    return pltpu.roll(xcat, s, 0)[HALO:HALO + tm]


def _conv_pre(x_ref, halo_ref, w_ref, i, tm):
    xc = x_ref[...]
    halo = jnp.where(i > 0, halo_ref[...], 0.0)
    xcat = jnp.concatenate([halo, xc], axis=0)
    w = w_ref[...]
    y = w[GDN_CONV - 1:GDN_CONV] * xc
    for kk in range(GDN_CONV - 1):
        y = y + w[kk:kk + 1] * _shifted(xcat, GDN_CONV - 1 - kk, tm)
    return xcat, y


def _l2_heads(s, qscale_of):
    outs, rs = [], []
    for hh in range(s.shape[1] // GDN_HEAD_DIM):
        sh = s[:, hh * GDN_HEAD_DIM:(hh + 1) * GDN_HEAD_DIM]
        r = lax.rsqrt(jnp.sum(sh * sh, axis=-1, keepdims=True) + EPS)
        outs.append(sh * r)
        rs.append(r)
    return outs, rs


def _conv_specs(T, col0, tm, tc):
    cur = pl.BlockSpec((tm, tc), lambda j, i: (i, j + col0 // tc))
    halo = pl.BlockSpec((HALO, tc), lambda j, i: (jnp.maximum(i * (tm // HALO) - 1, 0), j + col0 // tc))
    wsp = pl.BlockSpec((GDN_CONV, tc), lambda j, i: (0, j + col0 // tc))
    out = pl.BlockSpec((tm, tc), lambda j, i: (i, j))
    return cur, halo, wsp, out


def _conv_fwd(proj, conv_w, col0, width, norm, name):
    T = proj.shape[0]
    tm, tc = _pick(T, CONV_ROWS), CONV_COLS
    cur, halo, wsp, out = _conv_specs(T, col0, tm, tc)
    n_q_tiles = (width // 2) // tc

    def body(x_ref, halo_ref, w_ref, o_ref):
        j, i = pl.program_id(0), pl.program_id(1)
        _, y = _conv_pre(x_ref, halo_ref, w_ref, i, tm)
        s = y * _sigmoid(y)
        if norm:
            outs, _ = _l2_heads(s, None)
            qs = jnp.where(j < n_q_tiles, GDN_HEAD_DIM ** -0.5, 1.0)
            o_ref[...] = jnp.concatenate(outs, axis=1) * qs
        else:
            o_ref[...] = s

    return _pcall(body, name=name, grid=(width // tc, T // tm), in_specs=[cur, halo, wsp], out_specs=out,
                  out_shape=jax.ShapeDtypeStruct((T, width), F32),
                  compiler_params=_params(("parallel", "parallel")))(proj, proj, conv_w)


def _conv_bwd_pre(proj, conv_w, dout, col0, width, norm, name):
    T = proj.shape[0]
    tm, tc = _pick(T, CONV_ROWS), CONV_COLS
    cur, halo, wsp, out = _conv_specs(T, col0, tm, tc)
    n_q_tiles = (width // 2) // tc

    def body(x_ref, halo_ref, w_ref, d_ref, dy_ref, dw_ref):
        j, i = pl.program_id(0), pl.program_id(1)
        xcat, y = _conv_pre(x_ref, halo_ref, w_ref, i, tm)
        sg = _sigmoid(y)
        s = y * sg
        d = d_ref[...]
        if norm:
            qs = jnp.where(j < n_q_tiles, GDN_HEAD_DIM ** -0.5, 1.0)
            d = d * qs
            outs, rs = _l2_heads(s, None)
            parts = []
            for hh, (nh, r) in enumerate(zip(outs, rs)):
                dh = d[:, hh * GDN_HEAD_DIM:(hh + 1) * GDN_HEAD_DIM]
                parts.append(r * (dh - nh * jnp.sum(dh * nh, axis=-1, keepdims=True)))
            ds = jnp.concatenate(parts, axis=1)
        else:
            ds = d
        dy = ds * sg * (1.0 + y * (1.0 - sg))
        dy_ref[...] = dy
        rows = [jnp.sum(dy * _shifted(xcat, GDN_CONV - 1 - kk, tm), axis=0, keepdims=True) for kk in range(GDN_CONV)]
        part = jnp.concatenate(rows, axis=0)

        @pl.when(i == 0)
        def _():
            dw_ref[...] = part

        @pl.when(i > 0)
        def _():
            dw_ref[...] += part

    wout = pl.BlockSpec((GDN_CONV, tc), lambda j, i: (0, j))
    return _pcall(body, name=name, grid=(width // tc, T // tm), in_specs=[cur, halo, wsp, out], out_specs=[out, wout],
                  out_shape=[jax.ShapeDtypeStruct((T, width), F32), jax.ShapeDtypeStruct((GDN_CONV, width), F32)],
                  compiler_params=_params(("parallel", "arbitrary")))(proj, proj, conv_w, dout)


def _conv_bwd_in(dy, conv_w, name):
    T, C = dy.shape
    tm, tc = _pick(T, CONV_ROWS), CONV_COLS
    nrow = T // tm

    def body(d_ref, nxt_ref, w_ref, dx_ref):
        i = pl.program_id(0)
        dc = d_ref[...]
        nxt = jnp.where(i < nrow - 1, nxt_ref[...], 0.0)
        dcat = jnp.concatenate([dc, nxt], axis=0)
        w = w_ref[...]
        dx = w[GDN_CONV - 1:GDN_CONV] * dc
        for kk in range(GDN_CONV - 1):
            s = GDN_CONV - 1 - kk
            dx = dx + w[kk:kk + 1] * pltpu.roll(dcat, tm + HALO - s, 0)[:tm]
        dx_ref[...] = dx.astype(BF16)

    cur = pl.BlockSpec((tm, tc), lambda i, j: (i, j))
    nxt = pl.BlockSpec((HALO, tc), lambda i, j: (jnp.minimum((i + 1) * (tm // HALO), T // HALO - 1), j))
    wsp = pl.BlockSpec((GDN_CONV, tc), lambda i, j: (0, j))
    return _pcall(body, name=name, grid=(nrow, C // tc), in_specs=[cur, nxt, wsp], out_specs=cur,
                  out_shape=jax.ShapeDtypeStruct((T, C), BF16),
                  compiler_params=_params(("parallel", "parallel")))(dy, dy, conv_w)


GATE_ROWS = 512


def _chunk_mask(n, lower):
    row = _iota2((n, n), 0)
    col = _iota2((n, n), 1)
    same = (row // GDN_CHUNK) == (col // GDN_CHUNK)
    tri = (row >= col) if lower else (row <= col)
    return (same & tri).astype(BF16)


def _gates_fwd(proj, a_log, dt_bias):
    T = proj.shape[0]
    tm = _pick(T, GATE_ROWS)
    c0 = (GDN_CONV_W + GDN_VW) // LANE

    def body(bl_ref, a_ref, alog_ref, dt_ref, beta_ref, g_ref, gc_ref):
        beta_ref[...] = _sigmoid(bl_ref[...])
        g = -jnp.exp(alog_ref[...]) * _softplus(a_ref[...] + dt_ref[...])
        g_ref[...] = g
        gc_ref[...] = _mdot2(_chunk_mask(tm, True), g)

    blk = lambda c: pl.BlockSpec((tm, LANE), lambda i: (i, c))
    vec = pl.BlockSpec((1, LANE), lambda i: (0, 0))
    sh = jax.ShapeDtypeStruct((T, LANE), F32)
    return _pcall(body, name="gdn_gates", grid=(T // tm,), in_specs=[blk(c0), blk(c0 + 1), vec, vec],
                  out_specs=[blk(0), blk(0), blk(0)], out_shape=[sh, sh, sh],
                  compiler_params=_params(("parallel",)))(proj, proj, a_log, dt_bias)


def _gates_bwd(proj, a_log, dt_bias, beta, g, dbeta, dgc):
    T = proj.shape[0]
    tm = _pick(T, GATE_ROWS)
    c0 = (GDN_CONV_W + GDN_VW) // LANE

    def body(a_ref, alog_ref, dt_ref, beta_ref, g_ref, dbeta_ref, dgc_ref, dbl_ref, da_ref, dalog_ref, ddt_ref):
        dg = _mdot2(_chunk_mask(tm, False), dgc_ref[...])
        b = beta_ref[...]
        dbl_ref[...] = (dbeta_ref[...] * b * (1.0 - b)).astype(BF16)
        da = dg * (-jnp.exp(alog_ref[...])) * _sigmoid(a_ref[...] + dt_ref[...])
        da_ref[...] = da.astype(BF16)
        p1 = jnp.sum(dg * g_ref[...], axis=0, keepdims=True)
        p2 = jnp.sum(da, axis=0, keepdims=True)

        @pl.when(pl.program_id(0) == 0)
        def _():
            dalog_ref[...] = p1
            ddt_ref[...] = p2

        @pl.when(pl.program_id(0) > 0)
        def _():
            dalog_ref[...] += p1
            ddt_ref[...] += p2

    blk = lambda c: pl.BlockSpec((tm, LANE), lambda i: (i, c))
    vec = pl.BlockSpec((1, LANE), lambda i: (0, 0))
    shb = jax.ShapeDtypeStruct((T, LANE), BF16)
    shv = jax.ShapeDtypeStruct((1, LANE), F32)
    return _pcall(body, name="gdn_dgates", grid=(T // tm,),
                  in_specs=[blk(c0 + 1), vec, vec, blk(0), blk(0), blk(0), blk(0)],
                  out_specs=[blk(0), blk(0), vec, vec], out_shape=[shb, shb, shv, shv],
                  compiler_params=_params(("arbitrary",)))(proj, a_log, dt_bias, beta, g, dbeta, dgc)


def _inv_unit_lower(Ls):
    C = Ls[0].shape[0]
    row = _iota2((C, C), 0)
    col = _iota2((C, C), 1)
    blk16 = (row // 16) == (col // 16)
    blk32 = (row // 32) == (col // 32)
    eye = (row == col).astype(F32)
    xs = [-jnp.where(blk16, L, 0.0) for L in Ls]
    inv = [eye + x for x in xs]
    for _ in range(3):
        xs = [_dot3(x, x) for x in xs]
        inv = [a + _dot3(a, x) for a, x in zip(inv, xs)]
    for mask in (blk32 & ~blk16, ~blk32):
        t = [_dot3(a, jnp.where(mask, L, 0.0)) for a, L in zip(inv, Ls)]
        inv = [a - _dot3(ti, a) for a, ti in zip(inv, t)]
    return inv


GDN_GROUP = 4
GDN_PREP_CHUNKS = 4


def _gdn_specs(T):
    C, D, E = GDN_CHUNK, GDN_HEAD_DIM, GDN_GROUP
    n = T // C
    qk = pl.BlockSpec((C, (E // 2) * D), lambda h, i: (i, h))
    vE = pl.BlockSpec((C, E * D), lambda h, i: (i, h))
    colv = pl.BlockSpec((C, LANE), lambda h, i: (i, 0))
    colo = pl.BlockSpec((None, C, LANE), lambda h, i: (h, i, 0))
    rowv = pl.BlockSpec((E, None, 1, C), lambda h, i: (h, i, 0, 0))
    st = pl.BlockSpec((E, None, D, D), lambda h, i: (h, i, 0, 0))
    am = pl.BlockSpec((E, None, C, C), lambda h, i: (h, i, 0, 0))
    return n, qk, vE, colv, colo, rowv, st, am


def _lane_col(blk, lane):
    return jnp.sum(jnp.where(_iota2(blk.shape, 1) == lane, blk, 0.0), axis=1, keepdims=True)


def _gdn_decay(gcol, grow):
    C = GDN_CHUNK
    row = _iota2((C, C), 0)
    col = _iota2((C, C), 1)
    incl = row >= col
    dm = jnp.where(incl, jnp.exp(jnp.where(incl, gcol - grow, 0.0)), 0.0)
    glast = grow[:, C - 1:C]
    return dm, jnp.exp(gcol), jnp.exp(glast), jnp.exp(glast - gcol), row > col, incl


def _gdn_prep(k, beta, gcol, grow):
    T = k.shape[0]
    C, D, B = GDN_CHUNK, GDN_HEAD_DIM, GDN_PREP_CHUNKS
    n = T // C

    def body(k_ref, b_ref, gc_ref, gr_ref, a_ref):
        idx = [(e, cb) for e in range(2) for cb in range(B)]
        kc = {cb: k_ref[cb * C:(cb + 1) * C, :] for cb in range(B)}
        lm = []
        head0 = 2 * pl.program_id(0)
        for e, cb in idx:
            beta = _lane_col(b_ref[cb * C:(cb + 1) * C, :], head0 + e)
            dm, _, _, _, strict, _ = _gdn_decay(_lane_col(gc_ref[cb * C:(cb + 1) * C, :], head0 + e), gr_ref[e, cb])
            lm.append(jnp.where(strict, _bdot(kc[cb] * beta, kc[cb], NT) * dm, 0.0))
        inv = _inv_unit_lower(lm)
        for (e, cb), a in zip(idx, inv):
            a_ref[e, cb] = a

    return _pcall(
        body, name="gdn_prep", grid=(GDN_K_HEADS, n // B),
        in_specs=[pl.BlockSpec((B * C, D), lambda h, i: (i, h)), pl.BlockSpec((B * C, LANE), lambda h, i: (i, 0)),
                  pl.BlockSpec((B * C, LANE), lambda h, i: (i, 0)), pl.BlockSpec((2, B, 1, C), lambda h, i: (h, i, 0, 0))],
        out_specs=pl.BlockSpec((2, B, C, C), lambda h, i: (h, i, 0, 0)),
        out_shape=jax.ShapeDtypeStruct((GDN_V_HEADS, n, C, C), F32),
        compiler_params=_params(("parallel", "parallel")),
    )(k, beta, gcol, grow)


def _gdn_fwd(q, k, v, beta, gcol, grow, amat):
    T = q.shape[0]
    C, D, E = GDN_CHUNK, GDN_HEAD_DIM, GDN_GROUP
    n, qk, vE, colv, colo, rowv, st, am = _gdn_specs(T)
    R = range(E)

    def body(q_ref, k_ref, v_ref, b_ref, gc_ref, gr_ref, a_ref, o_ref, s_ref, vn_ref, state):
        @pl.when(pl.program_id(1) == 0)
        def _():
            state[...] = jnp.zeros_like(state)

        qv = [q_ref[:, (e // 2) * D:(e // 2 + 1) * D] for e in R]
        kv = [k_ref[:, (e // 2) * D:(e // 2 + 1) * D] for e in R]
        vv = [v_ref[:, e * D:(e + 1) * D] for e in R]
        head0 = E * pl.program_id(0)
        beta = [_lane_col(b_ref[...], head0 + e) for e in R]
        a = [a_ref[e] for e in R]
        s = [state[e] for e in R]
        dec = [_gdn_decay(_lane_col(gc_ref[...], head0 + e), gr_ref[e]) for e in R]
        pm = [_bdot(qv[e], kv[e], NT) * dec[e][0] for e in R]
        r = [beta[e] * (vv[e] - _bdot(kv[e] * dec[e][1], s[e])) for e in R]
        vn = [_dot3(a[e], r[e]) for e in R]
        o = [_bdot(qv[e] * dec[e][1], s[e]) + _bdot(pm[e], vn[e]) for e in R]
        s2 = [dec[e][2] * s[e] + _bdot(kv[e] * dec[e][3], vn[e], TN) for e in R]
        for e in R:
            s_ref[e] = s[e]
            vn_ref[:, e * D:(e + 1) * D] = vn[e]
            o_ref[:, e * D:(e + 1) * D] = o[e]
            state[e] = s2[e]

    shv = jax.ShapeDtypeStruct((T, GDN_V_HEADS * D), F32)
    return _pcall(
        body, name="gdn_fwd", grid=(GDN_V_HEADS // E, n), in_specs=[qk, qk, vE, colv, colv, rowv, am],
        out_specs=[vE, st, vE],
        out_shape=[shv, jax.ShapeDtypeStruct((GDN_V_HEADS, n, D, D), F32), shv],
        scratch_shapes=[pltpu.VMEM((E, D, D), F32)],
        compiler_params=_params(("parallel", "arbitrary")),
    )(q, k, v, beta, gcol, grow, amat)


def _gdn_bwd(q, k, v, beta, gcol, grow, states, amat, vnew, do):
    T = q.shape[0]
    C, D, E = GDN_CHUNK, GDN_HEAD_DIM, GDN_GROUP
    n, qk, vE, colv, colo, rowv, st, am = _gdn_specs(T)
    rev = lambda spec: pl.BlockSpec(spec.block_shape, (lambda f: (lambda h, i: f(h, n - 1 - i)))(spec.index_map))
    qk, vE, colv, colo, rowv, st, am = (rev(s) for s in (qk, vE, colv, colo, rowv, st, am))
    R = range(E)

    def body(q_ref, k_ref, v_ref, b_ref, gc_ref, gr_ref, s_ref, a_ref, vn_ref, do_ref,
             dq_ref, dk_ref, dv_ref, db_ref, dgc_ref, dstate):
        @pl.when(pl.program_id(1) == 0)
        def _():
            dstate[...] = jnp.zeros_like(dstate)

        M = lambda f: [f(e) for e in R]
        rsum = lambda x: jnp.sum(x, axis=1, keepdims=True)
        qv = M(lambda e: q_ref[:, (e // 2) * D:(e // 2 + 1) * D])
        kv = M(lambda e: k_ref[:, (e // 2) * D:(e // 2 + 1) * D])
        vv = M(lambda e: v_ref[:, e * D:(e + 1) * D])
        vn = M(lambda e: vn_ref[:, e * D:(e + 1) * D])
        dov = M(lambda e: do_ref[:, e * D:(e + 1) * D])
        head0 = E * pl.program_id(0)
        beta = M(lambda e: _lane_col(b_ref[...], head0 + e))
        s = M(lambda e: s_ref[e])
        a = M(lambda e: a_ref[e])
        dsn = M(lambda e: dstate[e])
        dec = M(lambda e: _gdn_decay(_lane_col(gc_ref[...], head0 + e), gr_ref[e]))
        dm, gam, glast, tail = (M(lambda e: dec[e][i]) for i in range(4))
        strict, incl = dec[0][4], dec[0][5]
        kb = M(lambda e: kv[e] * beta[e])
        kd = M(lambda e: kv[e] * gam[e])
        qd = M(lambda e: qv[e] * gam[e])
        kt = M(lambda e: kv[e] * tail[e])
        lmat = M(lambda e: jnp.where(strict, _bdot(kb[e], kv[e], NT) * dm[e], 0.0))
        pmat = M(lambda e: _bdot(qv[e], kv[e], NT) * dm[e])
        xres = M(lambda e: vv[e] - _bdot(kd[e], s[e]))
        dvn = M(lambda e: _bdot(pmat[e], dov[e], TN) + _bdot(kt[e], dsn[e]))
        dqd = M(lambda e: _bdot(dov[e], s[e], NT))
        dp = M(lambda e: jnp.where(incl, _bdot(dov[e], vn[e], NT), 0.0))
        dkt = M(lambda e: _bdot(vn[e], dsn[e], NT))
        dr = M(lambda e: _dot3(a[e], dvn[e], TN))
        drb = M(lambda e: beta[e] * dr[e])
        dkd = M(lambda e: -_bdot(drb[e], s[e], NT))
        ds2 = M(lambda e: _bdot(qd[e], dov[e], TN) + glast[e] * dsn[e] - _bdot(kd[e], drb[e], TN))
        dl = M(lambda e: -jnp.where(strict, _bdot(dr[e], vn[e], NT), 0.0))
        dmm = M(lambda e: dl[e] * dm[e])
        dnn = M(lambda e: dp[e] * dm[e])
        emat = M(lambda e: dl[e] * lmat[e] + dp[e] * pmat[e])
        dkb = M(lambda e: _bdot(dmm[e], kv[e]))
        dk = M(lambda e: beta[e] * dkb[e] + _bdot(dmm[e], kb[e], TN) + _bdot(dnn[e], qv[e], TN)
               + gam[e] * dkd[e] + tail[e] * dkt[e])
        dq = M(lambda e: _bdot(dnn[e], kv[e]) + gam[e] * dqd[e])
        dbeta = M(lambda e: rsum(dr[e] * xres[e]) + rsum(dkb[e] * kv[e]))
        ones = jnp.ones((C, LANE), BF16)
        colsum = M(lambda e: _dot2m(emat[e], ones, TN)[:, :1])
        tails = M(lambda e: rsum(dkt[e] * kt[e]))
        lastrow = _iota2((C, 1), 0) == C - 1
        dlast = M(lambda e: jnp.sum(tails[e], axis=0, keepdims=True)
                  + glast[e] * jnp.sum(rsum(s[e] * dsn[e]), axis=0, keepdims=True))
        dgc = M(lambda e: rsum(emat[e]) - colsum[e] + rsum(dkd[e] * kd[e]) + rsum(dqd[e] * qd[e]) - tails[e]
                + jnp.where(lastrow, dlast[e], 0.0))
        lane = _iota2((C, LANE), 1)
        db_all = jnp.zeros((C, LANE), F32)
        dgc_all = jnp.zeros((C, LANE), F32)
        for e in R:
            dv_ref[:, e * D:(e + 1) * D] = drb[e]
            db_all = jnp.where(lane == e, dbeta[e], db_all)
            dgc_all = jnp.where(lane == e, dgc[e], dgc_all)
            dstate[e] = ds2[e]
        db_ref[...] = db_all
        dgc_ref[...] = dgc_all
        for kh in range(E // 2):
            dq_ref[:, kh * D:(kh + 1) * D] = dq[2 * kh] + dq[2 * kh + 1]
            dk_ref[:, kh * D:(kh + 1) * D] = dk[2 * kh] + dk[2 * kh + 1]

    shq = jax.ShapeDtypeStruct((T, GDN_K_HEADS * D), F32)
    shv = jax.ShapeDtypeStruct((T, GDN_V_HEADS * D), F32)
    shc = jax.ShapeDtypeStruct((GDN_V_HEADS // E, T, LANE), F32)
    return _pcall(
        body, name="gdn_bwd", grid=(GDN_V_HEADS // E, n),
        in_specs=[qk, qk, vE, colv, colv, rowv, st, am, vE, vE],
        out_specs=[qk, qk, vE, colo, colo], out_shape=[shq, shq, shv, shc, shc],
        scratch_shapes=[pltpu.VMEM((E, D, D), F32)],
        compiler_params=_params(("parallel", "arbitrary")),
    )(q, k, v, beta, gcol, grow, states, amat, vnew, do)


def _outgate_fwd(o, proj, gain):
    T = o.shape[0]
    tm, tc = _pick(T, CONV_ROWS), CONV_COLS
    z0 = GDN_CONV_W // tc

    def body(o_ref, z_ref, g_ref, y_ref):
        z = z_ref[...]
        sz = z * _sigmoid(z)
        parts = []
        for hh in range(tc // GDN_HEAD_DIM):
            oh = o_ref[:, hh * GDN_HEAD_DIM:(hh + 1) * GDN_HEAD_DIM]
            r = lax.rsqrt(jnp.mean(oh * oh, axis=-1, keepdims=True) + EPS)
            parts.append(oh * r * g_ref[...])
        y_ref[...] = (jnp.concatenate(parts, axis=1) * sz).astype(BF16)

    blk = pl.BlockSpec((tm, tc), lambda i, j: (i, j))
    return _pcall(body, name="gdn_outgate", grid=(T // tm, GDN_VW // tc),
                  in_specs=[blk, pl.BlockSpec((tm, tc), lambda i, j: (i, j + z0)), pl.BlockSpec((1, GDN_HEAD_DIM), lambda i, j: (0, 0))],
                  out_specs=blk, out_shape=jax.ShapeDtypeStruct((T, GDN_VW), BF16),
                  compiler_params=_params(("parallel", "parallel")))(o, proj, gain)


def _outgate_bwd(dy, o, proj, gain):
    T = o.shape[0]
    tm, tc = _pick(T, CONV_ROWS), CONV_COLS
    z0 = GDN_CONV_W // tc
    nh = tc // GDN_HEAD_DIM

    def body(dy_ref, o_ref, z_ref, g_ref, do_ref, dz_ref, dg_ref):
        z = z_ref[...]
        sg = _sigmoid(z)
        sz = z * sg
        dy = dy_ref[...]
        dgain = jnp.zeros((1, GDN_HEAD_DIM), F32)
        dos, ys = [], []
        for hh in range(nh):
            sl = slice(hh * GDN_HEAD_DIM, (hh + 1) * GDN_HEAD_DIM)
            oh = o_ref[:, sl]
            r = lax.rsqrt(jnp.mean(oh * oh, axis=-1, keepdims=True) + EPS)
            xh = oh * r
            dn = dy[:, sl] * sz[:, sl]
            dgain = dgain + jnp.sum(dn * xh, axis=0, keepdims=True)
            dxh = dn * g_ref[...]
            dos.append(r * (dxh - xh * jnp.mean(dxh * xh, axis=-1, keepdims=True)))
            ys.append(xh * g_ref[...])
        do_ref[...] = jnp.concatenate(dos, axis=1)
        dz_ref[...] = (dy * jnp.concatenate(ys, axis=1) * sg * (1.0 + z * (1.0 - sg))).astype(BF16)
        first = (pl.program_id(0) == 0) & (pl.program_id(1) == 0)

        @pl.when(first)
        def _():
            dg_ref[...] = dgain

        @pl.when(jnp.logical_not(first))
        def _():
            dg_ref[...] += dgain

    blk = pl.BlockSpec((tm, tc), lambda i, j: (i, j))
    vec = pl.BlockSpec((1, GDN_HEAD_DIM), lambda i, j: (0, 0))
    return _pcall(body, name="gdn_doutgate", grid=(T // tm, GDN_VW // tc),
                  in_specs=[blk, blk, pl.BlockSpec((tm, tc), lambda i, j: (i, j + z0)), vec],
                  out_specs=[blk, blk, vec],
                  out_shape=[jax.ShapeDtypeStruct((T, GDN_VW), F32), jax.ShapeDtypeStruct((T, GDN_VW), BF16),
                             jax.ShapeDtypeStruct((1, GDN_HEAD_DIM), F32)],
                  compiler_params=_params(("arbitrary", "arbitrary")))(dy, o, proj, gain)


def _pad_lanes(vec):
    return jnp.pad(vec.reshape(1, -1), ((0, 0), (0, LANE - vec.shape[-1])))


def _head_rows(a):
    T = a.shape[0]
    return a[:, :GDN_V_HEADS].T.reshape(GDN_V_HEADS, T // GDN_CHUNK, 1, GDN_CHUNK)


def _group_lanes(a):
    T = a.shape[1]
    heads = a[:, :, :GDN_GROUP].transpose(1, 0, 2).reshape(T, GDN_V_HEADS)
    return jnp.pad(heads, ((0, 0), (0, LANE - GDN_V_HEADS)))


def _gdn_pad_in(w_in):
    c = GDN_CONV_W + GDN_VW
    z = jnp.zeros(w_in.shape[:-1] + (LANE - GDN_V_HEADS,), w_in.dtype)
    return jnp.concatenate([w_in[..., :c + GDN_V_HEADS], z, w_in[..., c + GDN_V_HEADS:], z], axis=-1)


def _gdn_unpad_in(dw):
    c = GDN_CONV_W + GDN_VW
    return jnp.concatenate([dw[..., :c + GDN_V_HEADS], dw[..., c + LANE:c + LANE + GDN_V_HEADS]], axis=-1)


def _gdn_mixer_fwd(h, g, w_in_pad, conv_w, a_log, dt_bias, out_gain, w_out):
    T = h.shape[0]
    hn = _rms_fwd(h, g, "gdn_norm")
    proj = _mm(hn, w_in_pad, "nn", name="gdn_in")
    qk = _conv_fwd(proj, conv_w, 0, 2 * GDN_KW, True, "gdn_conv_qk")
    vv = _conv_fwd(proj, conv_w, 2 * GDN_KW, GDN_VW, False, "gdn_conv_v")
    alog, dtb = _pad_lanes(a_log), _pad_lanes(dt_bias)
    beta, gl, gc = _gates_fwd(proj, alog, dtb)
    grow = _head_rows(gc)
    qn, kn = qk[:, :GDN_KW], qk[:, GDN_KW:]
    amat = _gdn_prep(kn, beta, gc, grow)
    o, states, vnew = _gdn_fwd(qn, kn, vv, beta, gc, grow, amat)
    gain = out_gain.reshape(1, GDN_HEAD_DIM)
    y = _outgate_fwd(o, proj, gain)
    h2 = _mm(y, w_out, "nn", res=h, name="gdn_out")
    return h2, (h, hn, proj, qn, kn, vv, beta, gl, gc, grow, o, states, amat, vnew, y, alog, dtb, gain)


def _gdn_mixer_bwd(dh2, saved, g, w_in_pad, conv_w, w_out):
    h, hn, proj, qn, kn, vv, beta, gl, gc, grow, o, states, amat, vnew, y, alog, dtb, gain = saved
    T = h.shape[0]
    dy = _mm(dh2, w_out, "nt", name="gdn_dy")
    dw_out = _mm(y, dh2, "tn", out_dtype=BF16, name="gdn_dwout")
    do, dz, dgain = _outgate_bwd(dy, o, proj, gain)
    dq, dk, dv, dbeta, dgc = _gdn_bwd(qn, kn, vv, beta, gc, grow, states, amat, vnew, do)
    dqk = jnp.concatenate([dq, dk], axis=1)
    dy_qk, dcw_qk = _conv_bwd_pre(proj, conv_w, dqk, 0, 2 * GDN_KW, True, "gdn_dconv_qk")
    dy_v, dcw_v = _conv_bwd_pre(proj, conv_w, dv, 2 * GDN_KW, GDN_VW, False, "gdn_dconv_v")
    dx_qk = _conv_bwd_in(dy_qk, conv_w[:, :2 * GDN_KW], "gdn_dconvin_qk")
    dx_v = _conv_bwd_in(dy_v, conv_w[:, 2 * GDN_KW:], "gdn_dconvin_v")
    dbl, da, dalog, ddt = _gates_bwd(proj, alog, dtb, beta, gl, _group_lanes(dbeta), _group_lanes(dgc))
    dproj = jnp.concatenate([dx_qk, dx_v, dz, dbl, da], axis=1)
    dw_in_pad = _mm(hn, dproj, "tn", out_dtype=BF16, name="gdn_dwin")
    dhn = _mm(dproj, w_in_pad, "nt", name="gdn_dhn")
    dh, dg = _rms_bwd(dhn, h, g, dh2, "gdn_dnorm")
    dconv = jnp.concatenate([dcw_qk, dcw_v], axis=1)
    return (dh, dg, _gdn_unpad_in(dw_in_pad), dconv, dalog[0, :GDN_V_HEADS], ddt[0, :GDN_V_HEADS],
            dgain.reshape(GDN_HEAD_DIM), dw_out)


def _instances(full):
    out = {}
    for n, a in full.items():
        if n.startswith("ffn_"):
            for i in range(2):
                for j in range(2):
                    out[(n, i, j)] = a[i, j]
        elif n in ("mix_norm", "ple_norm", "ple_w_gate", "ple_w_proj"):
            for i in range(2):
                out[(n, i)] = a[i]
        else:
            out[(n,)] = a[0]
    return out


def _stacked(inst):
    out = {}
    for n in dict.fromkeys(k[0] for k in inst):
        if n.startswith("ffn_"):
            out[n] = jnp.stack([jnp.stack([inst[(n, i, j)] for j in range(2)]) for i in range(2)])
        elif n in ("mix_norm", "ple_norm", "ple_w_gate", "ple_w_proj"):
            out[n] = jnp.stack([inst[(n, i)] for i in range(2)])
        else:
            out[n] = inst[(n,)][None]
    return out


def _local_step(x, p, target, w, late_shards=(), late_weights=None, early_grads=None):
    w = dict(w)
    ffn = lambda i, j: (w[("ffn_norm", i, j)], w[("ffn_w_gate", i, j)], w[("ffn_w_up", i, j)], w[("ffn_w_down", i, j)])
    h = x
    tape = []
    for i in range(2):
        h, s1 = _ffn_fwd(h, *ffn(i, 0), f"ffn{i}a")
        if i == 0:
            def w_out_of(gathered):
                if late_weights is not None:
                    w.update(late_weights(gathered))
                return w[("att_w_out",)]
            h, s2, _ = _att_fwd(h, w[("mix_norm", 0)], w[("att_w_in",)], w_out_of, w[("att_q_norm",)],
                                w[("att_k_norm",)], w[("att_sinks",)], late_shards)
        else:
            gdn_in_pad = _gdn_pad_in(w[("gdn_w_in",)])
            h, s2 = _gdn_mixer_fwd(h, w[("mix_norm", 1)], gdn_in_pad, w[("gdn_conv_w",)], w[("gdn_a_log",)],
                                   w[("gdn_dt_bias",)], w[("gdn_out_norm",)], w[("gdn_w_out",)])
        h, s3 = _ffn_fwd(h, *ffn(i, 1), f"ffn{i}b")
        h, s4 = _ple_fwd(h, p[i], w[("ple_norm", i)], w[("ple_w_gate", i)], w[("ple_w_proj", i)], f"ple{i}")
        tape.append((s1, s2, s3, s4))

    loss, dh = _loss_head(h, target)

    g = {}
    rode = []
    for i in (1, 0):
        s1, s2, s3, s4 = tape[i]
        dh, g[("ple_norm", i)], g[("ple_w_gate", i)], g[("ple_w_proj", i)] = _ple_bwd(
            dh, s4, p[i], w[("ple_norm", i)], w[("ple_w_gate", i)], f"ple{i}")
        dh, g[("ffn_norm", i, 1)], g[("ffn_w_gate", i, 1)], g[("ffn_w_up", i, 1)], g[("ffn_w_down", i, 1)] = _ffn_bwd(
            dh, s3, *ffn(i, 1), f"ffn{i}b")
        if i == 0:
            ride = early_grads(g) if early_grads is not None else ()
            (dh, g[("mix_norm", 0)], g[("att_w_in",)], g[("att_w_out",)], g[("att_q_norm",)], g[("att_k_norm",)],
             g[("att_sinks",)], rode) = _att_bwd(dh, s2, w[("mix_norm", 0)], w[("att_w_in",)], w[("att_w_out",)], ride)
        else:
            (dh, g[("mix_norm", 1)], g[("gdn_w_in",)], g[("gdn_conv_w",)], g[("gdn_a_log",)], g[("gdn_dt_bias",)],
             g[("gdn_out_norm",)], g[("gdn_w_out",)]) = _gdn_mixer_bwd(
                dh, s2, w[("mix_norm", 1)], gdn_in_pad, w[("gdn_conv_w",)], w[("gdn_w_out",)])
        dh, g[("ffn_norm", i, 0)], g[("ffn_w_gate", i, 0)], g[("ffn_w_up", i, 0)], g[("ffn_w_down", i, 0)] = _ffn_bwd(
            dh, s1, *ffn(i, 0), f"ffn{i}a")
    return loss, dh, g, rode


MESH = pl.DeviceIdType.MESH
N_CHIP = 4


def _place():
    x, y, c = lax.axis_index("x"), lax.axis_index("y"), lax.axis_index("c")
    others = [((1 - x, y), 2 * (1 - x) + y), ((x, 1 - y), 2 * x + (1 - y)), ((1 - x, 1 - y), 2 * (1 - x) + (1 - y))]
    return x, y, c, 4 * x + 2 * y + c, 2 * x + y, (x, y, 1 - c), others


def _comm_call(body, arrays, out_shape, n_sems, name):
    hbm = pl.BlockSpec(memory_space=pl.ANY)
    n = len(arrays)
    return _pcall(
        body, name=name, in_specs=[hbm] * n, out_specs=[hbm] * len(out_shape), out_shape=out_shape,
        scratch_shapes=[pltpu.SemaphoreType.DMA((n, n_sems)), pltpu.SemaphoreType.DMA((n, n_sems)),
                        pltpu.SemaphoreType.DMA((n, N_CHIP))],
        compiler_params=pltpu.CompilerParams(has_side_effects=True),
    )(*arrays)


def _gather_protocol(ins, outs, send_sems, recv_sems, local_sems):
    n = len(ins)
    x, y, c, me, my_chip, sibling, others = _place()

    def copy(a, k, block, to, src=None):
        dst = outs[a].at[block]
        return pltpu.make_async_remote_copy(
            src_ref=dst if src is None else src, dst_ref=dst, send_sem=send_sems.at[a, k],
            recv_sem=recv_sems.at[a, k], device_id=to, device_id_type=MESH)

    local = [pltpu.make_async_copy(ins[a], outs[a].at[me], local_sems.at[a, 0]) for a in range(n)]
    first = []
    for a in range(n):
        first.append(copy(a, 0, me, sibling, src=ins[a]))
        first += [copy(a, 1 + j, me, (*chip, c), src=ins[a]) for j, (chip, _) in enumerate(others)]

    def start():
        for cp in local + first:
            cp.start()

    def finish():
        passed = []
        for a in range(n):
            for j, (chip, chip_idx) in enumerate(others):
                blk = 2 * chip_idx + c
                copy(a, 1 + j, blk, (x, y, c)).wait_recv()
                fwd = copy(a, 4 + j, blk, sibling)
                fwd.start()
                passed.append(fwd)
        for a in range(n):
            copy(a, 0, 2 * my_chip + (1 - c), (x, y, c)).wait_recv()
            for j, (chip, chip_idx) in enumerate(others):
                copy(a, 4 + j, 2 * chip_idx + (1 - c), (x, y, c)).wait_recv()
        for cp in first + passed:
            cp.wait_send()
        for cp in local:
            cp.wait()

    return start, finish


def _all_gather(arrays):
    n = len(arrays)

    def body(*refs):
        start, finish = _gather_protocol(refs[:n], refs[n:2 * n], *refs[2 * n:])
        start()
        finish()

    out_shape = [jax.ShapeDtypeStruct((N_DEV,) + a.shape, a.dtype) for a in arrays]
    return _comm_call(body, arrays, out_shape, N_DEV - 1, "gather_weights")


def _exchange_sibling(arrays, name):
    n = len(arrays)

    def body(*refs):
        ins, got = refs[:n], refs[n:2 * n]
        send_sems, recv_sems, _ = refs[2 * n:]
        x, y, c, me, my_chip, sibling, others = _place()
        remote = []
        for a in range(n):
            for chip in range(N_CHIP):
                rc = pltpu.make_async_remote_copy(
                    src_ref=ins[a].at[2 * chip + (1 - c)], dst_ref=got[a].at[chip], send_sem=send_sems.at[a, chip],
                    recv_sem=recv_sems.at[a, chip], device_id=sibling, device_id_type=MESH)
                rc.start()
                remote.append(rc)
        for rc in remote:
            rc.wait()

    half = [jax.ShapeDtypeStruct((N_CHIP,) + a.shape[1:], a.dtype) for a in arrays]
    return _comm_call(body, arrays, half, N_CHIP, name)


def _chips_protocol(ins, outs, send_sems, recv_sems, local_sems):
    n = len(ins)
    x, y, c, me, my_chip, sibling, others = _place()
    local = [pltpu.make_async_copy(ins[a].at[my_chip], outs[a].at[my_chip], local_sems.at[a, 0]) for a in range(n)]
    remote = [pltpu.make_async_remote_copy(
        src_ref=ins[a].at[chip_idx], dst_ref=outs[a].at[my_chip], send_sem=send_sems.at[a, j],
        recv_sem=recv_sems.at[a, j], device_id=(*chip, c), device_id_type=MESH)
        for a in range(n) for j, (chip, chip_idx) in enumerate(others)]

    def start():
        for cp in local + remote:
            cp.start()

    def finish():
        for cp in remote + local:
            cp.wait()

    return start, finish


def _exchange_chips(arrays, name):
    n = len(arrays)

    def body(*refs):
        start, finish = _chips_protocol(refs[:n], refs[n:2 * n], *refs[2 * n:])
        start()
        finish()

    out_shape = [jax.ShapeDtypeStruct(a.shape, a.dtype) for a in arrays]
    return _comm_call(body, arrays, out_shape, N_CHIP - 1, name)


def _as_rows(a, lead):
    shp = a.shape
    return a.reshape(shp[:lead] + (math.prod(shp[lead:-1]), shp[-1]))


def _row_tile(rows, cap=512):
    if rows <= cap:
        return rows
    for t in range(cap - cap % 8, 0, -8):
        if rows % t == 0:
            return t
    return rows


def _pair_sum(send, got, name):
    a3, b3 = _as_rows(send, 1), _as_rows(got, 1)
    _, rows, last = b3.shape
    tr = _row_tile(rows)

    def body(c_ref, a_ref, b_ref, o_ref):
        o_ref[...] = (a_ref[...].astype(F32) + b_ref[...].astype(F32)).astype(o_ref.dtype)

    core = lax.axis_index("c").astype(jnp.int32).reshape(1)
    out = _pcall(
        body, name=name,
        grid_spec=pltpu.PrefetchScalarGridSpec(
            num_scalar_prefetch=1, grid=(N_CHIP, rows // tr),
            in_specs=[pl.BlockSpec((None, tr, last), lambda k, i, c_ref: (2 * k + c_ref[0], i, 0)),
                      pl.BlockSpec((None, tr, last), lambda k, i, c_ref: (k, i, 0))],
            out_specs=pl.BlockSpec((None, tr, last), lambda k, i, c_ref: (k, i, 0))),
        out_shape=jax.ShapeDtypeStruct(b3.shape, got.dtype), compiler_params=_params(("parallel", "parallel")),
    )(core, a3, b3)
    return out.reshape(got.shape)


def _adamw(parts, w, m, v, name):
    p3 = _as_rows(parts, 1)
    w2, m2, v2 = (_as_rows(z, 0) for z in (w, m, v))
    rows, last = w2.shape
    tr = _row_tile(rows)
    c1 = 1.0 / (1.0 - ADAM_B1 ** ADAM_STEP)
    c2 = 1.0 / (1.0 - ADAM_B2 ** ADAM_STEP)

    def body(p_ref, w_ref, m_ref, v_ref, g_ref, d_ref, nm_ref, nv_ref):
        g = p_ref[0].astype(F32)
        for chip in range(1, N_CHIP):
            g = g + p_ref[chip].astype(F32)
        mn = ADAM_B1 * m_ref[...] + (1.0 - ADAM_B1) * g
        vn = ADAM_B2 * v_ref[...] + (1.0 - ADAM_B2) * (g * g)
        g_ref[...] = g
        nm_ref[...] = mn
        nv_ref[...] = vn
        d_ref[...] = -ADAM_LR * ((mn * c1) / (jnp.sqrt(vn * c2) + ADAM_EPS) + ADAM_WD * w_ref[...])

    row = pl.BlockSpec((tr, last), lambda i: (i, 0))
    sh = jax.ShapeDtypeStruct((rows, last), F32)
    outs = _pcall(body, name=name, grid=(rows // tr,),
                  in_specs=[pl.BlockSpec((N_CHIP, tr, last), lambda i: (0, i, 0)), row, row, row],
                  out_specs=[row, row, row, row], out_shape=[sh, sh, sh, sh],
                  compiler_params=_params(("parallel",)))(p3, w2, m2, v2)
    return [o.reshape(w.shape) for o in outs]


def _pack(pieces, row_align):
    rows, offs, r = [], [], 0
    for a in pieces:
        flat = a.reshape(-1)
        nr = -(-flat.shape[0] // PACK_W)
        flat = jnp.pad(flat, (0, nr * PACK_W - flat.shape[0]))
        rows.append(flat.reshape(nr, PACK_W))
        offs.append(r)
        r += nr
    pad = (-r) % row_align
    if pad:
        rows.append(jnp.zeros((pad, PACK_W), pieces[0].dtype))
    return jnp.concatenate(rows, axis=0), offs


def _unpack(flat, offs, shapes):
    out = []
    for off, shp in zip(offs, shapes):
        size = math.prod(shp)
        nr = -(-size // PACK_W)
        out.append(flat[..., off:off + nr, :].reshape(flat.shape[:-2] + (nr * PACK_W,))[..., :size].reshape(flat.shape[:-2] + tuple(shp)))
    return out


def _to_full(gathered, axis):
    z = jnp.moveaxis(gathered, 0, axis)
    shp = list(z.shape)
    return z.reshape(shp[:axis] + [shp[axis] * shp[axis + 1]] + shp[axis + 2:])


def _to_shards(full, axis):
    shp = list(full.shape)
    z = full.reshape(shp[:axis] + [N_DEV, shp[axis] // N_DEV] + shp[axis + 1:])
    return jnp.moveaxis(z, axis, 0)


def kernel(x, p, ffn_norm, ffn_w_gate, ffn_w_up, ffn_w_down, mix_norm, att_w_in, att_q_norm, att_k_norm, att_sinks, att_w_out, gdn_w_in, gdn_conv_w, gdn_a_log, gdn_dt_bias, gdn_out_norm, gdn_w_out, ple_norm, ple_w_gate, ple_w_proj, loss_target, m_ffn_norm, m_ffn_w_gate, m_ffn_w_up, m_ffn_w_down, m_mix_norm, m_att_w_in, m_att_q_norm, m_att_k_norm, m_att_sinks, m_att_w_out, m_gdn_w_in, m_gdn_conv_w, m_gdn_a_log, m_gdn_dt_bias, m_gdn_out_norm, m_gdn_w_out, m_ple_norm, m_ple_w_gate, m_ple_w_proj, v_ffn_norm, v_ffn_w_gate, v_ffn_w_up, v_ffn_w_down, v_mix_norm, v_att_w_in, v_att_q_norm, v_att_k_norm, v_att_sinks, v_att_w_out, v_gdn_w_in, v_gdn_conv_w, v_gdn_a_log, v_gdn_dt_bias, v_gdn_out_norm, v_gdn_w_out, v_ple_norm, v_ple_w_gate, v_ple_w_proj):
    args = dict(locals())
    wts = {n: args[n] for n in WEIGHTS}
    mom = {n: args["m_" + n] for n in WEIGHTS}
    var = {n: args["v_" + n] for n in WEIGHTS}
    axis = dict(SHARDED)
    vecs = [n for n, _ in SHARDED[:SMALL_SHARDED]]
    small = vecs + list(REPLICATED)
    small_shapes = [wts[n].shape for n in small]
    lead = lambda n: 2 if n.startswith("ffn_") else 1

    def stack_of(arrays, name, idxs):
        return jnp.stack([arrays[name][idx] if idx else arrays[name][0] for idx in idxs])

    def full_instances(gathered, group):
        out = {}
        for (name, idxs), g in zip(group, gathered):
            whole = _to_full(g, axis[name] - lead(name) + 1)
            for k, idx in enumerate(idxs):
                out[(name,) + idx] = whole[k]
        return out

    def shard_stacks(g, group):
        return [_to_shards(jnp.stack([g[(name,) + idx] for idx in idxs]), axis[name] - lead(name) + 1)
                for name, idxs in group]

    vec_pack, voffs = _pack([wts[n] for n in vecs], 8)
    early = _all_gather([stack_of(wts, n, idxs).astype(BF16) for n, idxs in EARLY] + [vec_pack])
    w = full_instances(early[:-1], EARLY)
    vec_full = {n: _to_full(piece, axis[n]) for n, piece in
                zip(vecs, _unpack(early[-1], voffs, [wts[n].shape for n in vecs]))}
    w.update(_instances({**vec_full, **{n: wts[n] for n in REPLICATED}}))
    late_shards = [stack_of(wts, n, idxs).astype(BF16) for n, idxs in LATE]

    def early_grads(g):
        send = shard_stacks(g, RIDE)
        got = _exchange_sibling(send, "exchange_sibling_early")
        return [_pair_sum(p_, q_, f"pair_sum_early_{i}") for i, (p_, q_) in enumerate(zip(send, got))]

    loss, grad_x, g, rode = _local_step(x[0], p[:, 0], loss_target[0], w, late_shards,
                                        lambda gathered: full_instances(gathered, LATE), early_grads)
    loss = lax.psum(loss, ("x", "y", "c"))

    gs = _stacked({k: v for k, v in g.items() if k[0] in small})
    vec_shards = [_to_shards(gs[n], axis[n]) for n in vecs]
    small_send = jnp.stack([_pack([sh[d] for sh in vec_shards] + [gs[n] for n in REPLICATED], 8)[0] for d in range(N_DEV)])
    send = shard_stacks(g, FINAL) + [small_send]
    got = _exchange_sibling(send, "exchange_sibling_final")
    chip_sums = [_pair_sum(p_, q_, f"pair_sum_final_{i}") for i, (p_, q_) in enumerate(zip(send, got))]
    last = _exchange_chips(chip_sums, "exchange_chips_final")

    pieces = {}
    for (name, idxs), part in list(zip(RIDE, rode)) + list(zip(FINAL, last[:-1])):
        for k, idx in enumerate(idxs):
            pieces[(name,) + idx] = part[:, k]
    outs = {}
    for n, _ in SHARDED[SMALL_SHARDED:]:
        if lead(n) == 2:
            part = jnp.stack([jnp.stack([pieces[(n, i, j)] for j in range(2)], axis=1) for i in range(2)], axis=1)
        elif (n, 0) in pieces:
            part = jnp.stack([pieces[(n, i)] for i in range(2)], axis=1)
        else:
            part = pieces[(n,)][:, None]
        outs[n] = _adamw(part, wts[n], mom[n], var[n], f"adamw_{n}")
    small_w, soffs = _pack([wts[n] for n in small], 8)
    small_m, _ = _pack([mom[n] for n in small], 8)
    small_v, _ = _pack([var[n] for n in small], 8)
    small_out = [_unpack(z, soffs, small_shapes) for z in _adamw(last[-1], small_w, small_m, small_v, "adamw_small")]
    for i, n in enumerate(small):
        outs[n] = [small_out[k][i] for k in range(4)]
    result = [loss, grad_x[None]]
    for k in range(4):
        result += [outs[n][k] for n in WEIGHTS]
    return tuple(result)
```

```python
import math

import jax
import jax.numpy as jnp
from jax import lax
from jax.experimental import pallas as pl
from jax.experimental.pallas import tpu as pltpu

F32 = jnp.float32
BF16 = jnp.bfloat16

N_DEV = 8
D_MODEL = 1024
D_FF = 2816
PLE_DIM = 256
HEAD_DIM = 64
SB_HEADS = 8
SWA_HEADS = 8
SWA_KV_HEADS = 2
SWA_GROUP = SWA_HEADS // SWA_KV_HEADS
WINDOW = 128
Q_BLOCK = 128
GDN_K_HEADS = 8
GDN_V_HEADS = 16
GDN_HEAD_DIM = 128
GDN_CONV = 4
GDN_CHUNK = 64
EPS = 1e-6
SB_W = SB_HEADS * HEAD_DIM
SWA_QW = SWA_HEADS * HEAD_DIM
SWA_KVW = SWA_KV_HEADS * HEAD_DIM
ATT_IN = 3 * SB_W + SWA_QW + 2 * SWA_KVW
GDN_KW = GDN_K_HEADS * GDN_HEAD_DIM
GDN_VW = GDN_V_HEADS * GDN_HEAD_DIM
GDN_CONV_W = 2 * GDN_KW + GDN_VW
GDN_IN = GDN_CONV_W + GDN_VW + 2 * GDN_V_HEADS
GDN_IN_PAD = GDN_CONV_W + GDN_VW + 2 * 128

ADAM_LR = 0.001
ADAM_B1 = 0.9
ADAM_B2 = 0.999
ADAM_EPS = 1e-08
ADAM_WD = 0.01
ADAM_STEP = 10

LANE = 128
VMEM_LIMIT = 56 * 1024 * 1024
PACK_W = 1024

NN = ((1,), (0,))
NT = ((1,), (1,))
TN = ((0,), (0,))

SHARDED = (
    ("ffn_norm", 2), ("gdn_conv_w", 2),
    ("ffn_w_gate", 3), ("ffn_w_up", 3), ("ffn_w_down", 2), ("att_w_in", 2), ("att_w_out", 1),
    ("gdn_w_in", 2), ("gdn_w_out", 1), ("ple_w_gate", 1), ("ple_w_proj", 2),
)
SMALL_SHARDED = 2
REPLICATED = ("mix_norm", "att_q_norm", "att_k_norm", "att_sinks", "gdn_a_log", "gdn_dt_bias",
              "gdn_out_norm", "ple_norm")
WEIGHTS = ("ffn_norm", "ffn_w_gate", "ffn_w_up", "ffn_w_down", "mix_norm", "att_w_in", "att_q_norm",
           "att_k_norm", "att_sinks", "att_w_out", "gdn_w_in", "gdn_conv_w", "gdn_a_log", "gdn_dt_bias",
           "gdn_out_norm", "gdn_w_out", "ple_norm", "ple_w_gate", "ple_w_proj")


_FFN_REST = [(0, 1), (1, 0), (1, 1)]
EARLY = [("ffn_w_gate", [(0, 0)]), ("ffn_w_up", [(0, 0)]), ("ffn_w_down", [(0, 0)]), ("att_w_in", [()])]
LATE = [("ffn_w_gate", _FFN_REST), ("ffn_w_up", _FFN_REST), ("ffn_w_down", _FFN_REST), ("att_w_out", [()]),
        ("gdn_w_in", [()]), ("gdn_w_out", [()]), ("ple_w_gate", [(0,), (1,)]), ("ple_w_proj", [(0,), (1,)])]
RIDE = [e for e in LATE if e[0] != "att_w_out"]
FINAL = EARLY + [("att_w_out", [()])]


def _pcall(body, **kw):
    return pl.pallas_call(body, **kw)


def _params(sem=None):
    if sem is None:
        return pltpu.CompilerParams(vmem_limit_bytes=VMEM_LIMIT)
    return pltpu.CompilerParams(dimension_semantics=sem, vmem_limit_bytes=VMEM_LIMIT)


def _dot(a, b, dims=NN):
    return lax.dot_general(a, b, (dims, ((), ())), preferred_element_type=F32)


def _bdot(a, b, dims=NN):
    return _dot(a.astype(BF16), b.astype(BF16), dims)


def _split(a):
    hi = a.astype(BF16)
    lo = (a - hi.astype(F32)).astype(BF16)
    return hi, lo


def _dot3(a, b, dims=NN):
    ah, al = _split(a)
    bh, bl = _split(b)
    return _dot(ah, bh, dims) + (_dot(ah, bl, dims) + _dot(al, bh, dims))


def _dot2m(a, m, dims=NN):
    ah, al = _split(a)
    return _dot(ah, m, dims) + _dot(al, m, dims)


def _mdot2(m, a, dims=NN):
    ah, al = _split(a)
    return _dot(m, ah, dims) + _dot(m, al, dims)


def _sigmoid(x):
    return 1.0 / (1.0 + jnp.exp(-x))


def _softplus(x):
    return jnp.maximum(x, 0.0) + jnp.log(1.0 + jnp.exp(-jnp.abs(x)))


def _pick(n, cap):
    if n <= cap:
        return n
    for t in range(cap - cap % LANE, 0, -LANE):
        if n % t == 0:
            return t
    raise ValueError(f"no tile for {n} under {cap}")


def _iota2(shape, axis):
    return lax.broadcasted_iota(jnp.int32, shape, axis)


def _mm(a, b, mode, out_dtype=F32, res=None, alpha=1.0, a2=None, b2=None, name="mm"):
    if mode == "nn":
        (M, K), N = a.shape, b.shape[1]
    elif mode == "nt":
        (M, K), N = a.shape, b.shape[0]
    else:
        (K, M), N = a.shape, b.shape[1]
    tm, tn, tk = _pick(M, 1408 if mode == "tn" else 512), _pick(N, 1408), _pick(K, 1024 if mode == "tn" else 1408)
    nk = K // tk
    dims = {"nn": NN, "nt": NT, "tn": TN}[mode]
    a_spec = pl.BlockSpec((tk, tm), lambda i, j, k: (k, i)) if mode == "tn" else pl.BlockSpec((tm, tk), lambda i, j, k: (i, k))
    b_spec = pl.BlockSpec((tn, tk), lambda i, j, k: (j, k)) if mode == "nt" else pl.BlockSpec((tk, tn), lambda i, j, k: (k, j))
    o_spec = pl.BlockSpec((tm, tn), lambda i, j, k: (i, j))
    two = a2 is not None
    has_res = res is not None
    a2_spec, b2_spec = a_spec, b_spec
    if two and a2.shape != a.shape:
        assert nk == 1 and mode == "nn" and a2.shape[0] == M and b2.shape[1] == N
        a2_spec = pl.BlockSpec((tm, a2.shape[1]), lambda i, j, k: (i, 0))
        b2_spec = pl.BlockSpec((a2.shape[1], tn), lambda i, j, k: (0, j))

    def body(*refs):
        refs = list(refs)
        a_ref, b_ref = refs[0], refs[1]
        pos = 2
        if two:
            a2_ref, b2_ref = refs[2], refs[3]
            pos = 4
        if has_res:
            res_ref = refs[pos]
            pos += 1
        o_ref, acc_ref = refs[pos], refs[pos + 1]
        k = pl.program_id(2)
        part = _bdot(a_ref[...], b_ref[...], dims)
        if two:
            part = part + _bdot(a2_ref[...], b2_ref[...], dims)

        def finish(acc):
            out = acc * alpha if alpha != 1.0 else acc
            if has_res:
                out = res_ref[...] + out
            o_ref[...] = out.astype(out_dtype)

        if nk == 1:
            finish(part)
        else:
            @pl.when(k == 0)
            def _():
                acc_ref[...] = part

            @pl.when(k > 0)
            def _():
                acc_ref[...] += part

            @pl.when(k == nk - 1)
            def _():
                finish(acc_ref[...])

    ins = [a, b]
    specs = [a_spec, b_spec]
    if two:
        ins += [a2, b2]
        specs += [a2_spec, b2_spec]
    if has_res:
        ins.append(res)
        specs.append(o_spec)
    return _pcall(
        body, name=name, grid=(M // tm, N // tn, nk), in_specs=specs, out_specs=o_spec,
        out_shape=jax.ShapeDtypeStruct((M, N), out_dtype),
        scratch_shapes=[pltpu.VMEM((tm, tn) if nk > 1 else (8, LANE), F32)],
        compiler_params=_params(("parallel", "parallel", "arbitrary")),
    )(*ins)


ROW_TILE = 256


def _rms_fwd(h, g, name):
    T, D = h.shape
    tr = _pick(T, ROW_TILE)

    def body(h_ref, g_ref, n_ref):
        x = h_ref[...]
        r = lax.rsqrt(jnp.mean(x * x, axis=-1, keepdims=True) + EPS)
        n_ref[...] = (x * r * g_ref[...]).astype(BF16)

    return _pcall(
        body, name=name, grid=(T // tr,),
        in_specs=[pl.BlockSpec((tr, D), lambda i: (i, 0)), pl.BlockSpec((1, D), lambda i: (0, 0))],
        out_specs=pl.BlockSpec((tr, D), lambda i: (i, 0)),
        out_shape=jax.ShapeDtypeStruct((T, D), BF16), compiler_params=_params(("parallel",)),
    )(h, g.reshape(1, D))


def _rms_bwd(dn, h, g, dres, name):
    T, D = h.shape
    tr = _pick(T, ROW_TILE)

    def body(dn_ref, h_ref, g_ref, dres_ref, dh_ref, dg_ref):
        x = h_ref[...]
        r = lax.rsqrt(jnp.mean(x * x, axis=-1, keepdims=True) + EPS)
        xh = x * r
        d = dn_ref[...].astype(F32)
        dxh = d * g_ref[...]
        dh_ref[...] = dres_ref[...] + r * (dxh - xh * jnp.mean(dxh * xh, axis=-1, keepdims=True))
        part = jnp.sum(d * xh, axis=0, keepdims=True)

        @pl.when(pl.program_id(0) == 0)
        def _():
            dg_ref[...] = part

        @pl.when(pl.program_id(0) > 0)
        def _():
            dg_ref[...] += part

    row = pl.BlockSpec((tr, D), lambda i: (i, 0))
    vec = pl.BlockSpec((1, D), lambda i: (0, 0))
    dh, dg = _pcall(
        body, name=name, grid=(T // tr,), in_specs=[row, row, vec, row], out_specs=[row, vec],
        out_shape=[jax.ShapeDtypeStruct((T, D), F32), jax.ShapeDtypeStruct((1, D), F32)],
        compiler_params=_params(("arbitrary",)),
    )(dn, h, g.reshape(1, D), dres)
    return dh, dg.reshape(D)


def _gateup(n, wg, wu, name):
    T, D = n.shape
    F = wg.shape[1]
    tm, tn = _pick(T, 512), _pick(F, 1408)

    def body(n_ref, wg_ref, wu_ref, a_ref, b_ref, hid_ref):
        x = n_ref[...]
        a = _dot(x, wg_ref[...])
        b = _dot(x, wu_ref[...])
        a_ref[...] = a.astype(BF16)
        b_ref[...] = b.astype(BF16)
        hid_ref[...] = (a * _sigmoid(a) * b).astype(BF16)

    o_spec = pl.BlockSpec((tm, tn), lambda i, j: (i, j))
    w_spec = pl.BlockSpec((D, tn), lambda i, j: (0, j))
    sh = jax.ShapeDtypeStruct((T, F), BF16)
    return _pcall(
        body, name=name, grid=(T // tm, F // tn),
        in_specs=[pl.BlockSpec((tm, D), lambda i, j: (i, 0)), w_spec, w_spec],
        out_specs=[o_spec, o_spec, o_spec], out_shape=[sh, sh, sh],
        compiler_params=_params(("parallel", "parallel")),
    )(n, wg, wu)


def _ffn_dhid(dy, wd, a, b, name):
    T, D = dy.shape
    F = wd.shape[0]
    tm, tn = _pick(T, 512), _pick(F, 1408)

    def body(dy_ref, wd_ref, a_ref, b_ref, da_ref, db_ref):
        dhid = 0.5 * _bdot(dy_ref[...], wd_ref[...], NT)
        av = a_ref[...].astype(F32)
        bv = b_ref[...].astype(F32)
        s = _sigmoid(av)
        da_ref[...] = (dhid * bv * s * (1.0 + av * (1.0 - s))).astype(BF16)
        db_ref[...] = (dhid * av * s).astype(BF16)

    o_spec = pl.BlockSpec((tm, tn), lambda i, j: (i, j))
    sh = jax.ShapeDtypeStruct((T, F), BF16)
    return _pcall(
        body, name=name, grid=(T // tm, F // tn),
        in_specs=[pl.BlockSpec((tm, D), lambda i, j: (i, 0)), pl.BlockSpec((tn, D), lambda i, j: (j, 0)), o_spec, o_spec],
        out_specs=[o_spec, o_spec], out_shape=[sh, sh],
        compiler_params=_params(("parallel", "parallel")),
    )(dy, wd, a, b)


def _ffn_fwd(h, g, wg, wu, wd, tag):
    n = _rms_fwd(h, g, f"{tag}_norm")
    a, b, hid = _gateup(n, wg, wu, f"{tag}_gateup")
    h2 = _mm(hid, wd, "nn", res=h, alpha=0.5, name=f"{tag}_down")
    return h2, (h, n, a, b, hid)


def _ffn_bwd(dh2, saved, g, wg, wu, wd, tag):
    h, n, a, b, hid = saved
    da, db = _ffn_dhid(dh2, wd, a, b, f"{tag}_dhid")
    dwd = _mm(hid, dh2, "tn", alpha=0.5, out_dtype=BF16, name=f"{tag}_dwd")
    dwg = _mm(n, da, "tn", out_dtype=BF16, name=f"{tag}_dwg")
    dwu = _mm(n, db, "tn", out_dtype=BF16, name=f"{tag}_dwu")
    dn = _mm(da, wg, "nt", a2=db, b2=wu, name=f"{tag}_dn")
    dh, dg = _rms_bwd(dn, h, g, dh2, f"{tag}_dnorm")
    return dh, dg, dwg, dwu, dwd


def _ple_fwd(h, p, g, w_gate, w_proj, tag):
    T, D = h.shape
    pn = _rms_fwd(h, g, f"{tag}_norm")
    tm, tn = _pick(T, 512), _pick(D, 1024)
    P = p.shape[1]

    def body(pn_ref, p_ref, wg_ref, wp_ref, h_ref, o_ref, gl_ref, pp_ref):
        gl = _dot(pn_ref[...], wg_ref[...])
        pp = _bdot(p_ref[...], wp_ref[...])
        gl_ref[...] = gl
        pp_ref[...] = pp
        o_ref[...] = h_ref[...] + _sigmoid(gl) * pp

    o_spec = pl.BlockSpec((tm, tn), lambda i, j: (i, j))
    sh = jax.ShapeDtypeStruct((T, D), F32)
    h2, gl, pp = _pcall(
        body, name=f"{tag}_fwd", grid=(T // tm, D // tn),
        in_specs=[pl.BlockSpec((tm, D), lambda i, j: (i, 0)), pl.BlockSpec((tm, P), lambda i, j: (i, 0)),
                  pl.BlockSpec((D, tn), lambda i, j: (0, j)), pl.BlockSpec((P, tn), lambda i, j: (0, j)), o_spec],
        out_specs=[o_spec, o_spec, o_spec], out_shape=[sh, sh, sh],
        compiler_params=_params(("parallel", "parallel")),
    )(pn, p, w_gate, w_proj, h)
    return h2, (h, pn, gl, pp)


def _ple_bwd(dh2, saved, p, g, w_gate, tag):
    h, pn, gl, pp = saved
    T, D = h.shape
    tr = _pick(T, ROW_TILE)

    def body(d_ref, gl_ref, pp_ref, dgl_ref, dpp_ref):
        d = d_ref[...]
        s = _sigmoid(gl_ref[...])
        dpp_ref[...] = (d * s).astype(BF16)
        dgl_ref[...] = (d * pp_ref[...] * s * (1.0 - s)).astype(BF16)

    row = pl.BlockSpec((tr, D), lambda i: (i, 0))
    sh = jax.ShapeDtypeStruct((T, D), BF16)
    dgl, dpp = _pcall(body, name=f"{tag}_dgate", grid=(T // tr,), in_specs=[row, row, row], out_specs=[row, row],
                      out_shape=[sh, sh], compiler_params=_params(("parallel",)))(dh2, gl, pp)
    dw_proj = _mm(p, dpp, "tn", out_dtype=BF16, name=f"{tag}_dwproj")
    dw_gate = _mm(pn, dgl, "tn", out_dtype=BF16, name=f"{tag}_dwgate")
    dpn = _mm(dgl, w_gate, "nt", name=f"{tag}_dpn")
    dh, dg = _rms_bwd(dpn, h, g, dh2, f"{tag}_dnorm")
    return dh, dg, dw_gate, dw_proj


def _loss_head(y, target):
    T, D = y.shape
    tr = _pick(T, ROW_TILE)

    def body(y_ref, t_ref, dy_ref, l_ref):
        e = y_ref[...] - t_ref[...]
        dy_ref[...] = e * (1.0 / D)
        part = jnp.sum(e * e, axis=0, keepdims=True)

        @pl.when(pl.program_id(0) == 0)
        def _():
            l_ref[...] = part

        @pl.when(pl.program_id(0) > 0)
        def _():
            l_ref[...] += part

    row = pl.BlockSpec((tr, D), lambda i: (i, 0))
    vec = pl.BlockSpec((1, D), lambda i: (0, 0))
    dy, l = _pcall(body, name="loss_head", grid=(T // tr,), in_specs=[row, row], out_specs=[row, vec],
                   out_shape=[jax.ShapeDtypeStruct((T, D), F32), jax.ShapeDtypeStruct((1, D), F32)],
                   compiler_params=_params(("arbitrary",)))(y, target)
    return (0.5 / D) * jnp.sum(l), dy


SB_LANES = SB_HEADS * 2 * HEAD_DIM


def _sb_consts():
    row = _iota2((Q_BLOCK, Q_BLOCK), 0)
    col = _iota2((Q_BLOCK, Q_BLOCK), 1)
    after = (row > col).astype(BF16)
    before = (row < col).astype(BF16)
    return col < row, after, before, col


def _ride_specs(ride, out_shapes, n_sems):
    hbm = pl.BlockSpec(memory_space=pl.ANY)
    n = len(ride)
    sems = [pltpu.SemaphoreType.DMA((n, n_sems)), pltpu.SemaphoreType.DMA((n, n_sems)),
            pltpu.SemaphoreType.DMA((n, N_CHIP))] if n else []
    return [hbm] * n, [hbm] * len(out_shapes), sems


def _sb_fwd(proj, ride=()):
    T = proj.shape[0]
    H, d, L = SB_HEADS, HEAD_DIM, 2 * HEAD_DIM
    nblk = T // Q_BLOCK
    scale = d ** -0.5
    n = len(ride)
    ride_out = [jax.ShapeDtypeStruct((N_DEV,) + a.shape, a.dtype) for a in ride]
    ride_in_specs, ride_out_specs, ride_sems = _ride_specs(ride, ride_out, N_DEV - 1)
    R = range(H)
    tile = lambda g: slice(g * L, (g + 1) * L)

    def body(*refs):
        q_ref, kv_ref = refs[:2]
        rin = refs[2:2 + n]
        o_ref, c_ref = refs[2 + n:4 + n]
        rout = refs[4 + n:4 + 2 * n]
        run_ref = refs[4 + 2 * n]
        i = pl.program_id(0)
        if n:
            start, finish = _gather_protocol(rin, rout, *refs[5 + 2 * n:])
            pl.when(i == 0)(start)
        causal, after, _, col = _sb_consts()
        qs = [q_ref[:, tile(g)] * scale for g in R]
        o_ref[...] = jnp.zeros_like(o_ref)
        c_ref[...] = jnp.zeros_like(c_ref)
        run_ref[...] = jnp.zeros_like(run_ref)

        def pair(j, diag):
            rows = pl.ds(pl.multiple_of(j * Q_BLOCK, Q_BLOCK), Q_BLOCK)
            kvj = [kv_ref[rows, tile(g)] for g in R]
            c = [run_ref[g] for g in R]
            acc = [o_ref[:, tile(g)] for g in R]
            cm = None if diag else [c_ref[:, tile(g)] for g in R]
            z = [_dot(qs[g], kvj[g], NT) for g in R]
            sp = [_softplus(z[g]) for g in R]
            lk = [jnp.where(causal, -sp[g], 0.0) if diag else -sp[g] for g in R]
            btw = [_dot2m(lk[g], after) for g in R]
            e = [jnp.exp((z[g] - sp[g]) + btw[g] + c[g]) for g in R]
            w = [jnp.where(causal, e[g], 0.0) if diag else e[g] for g in R]
            pv = [_bdot(w[g], kvj[g]) for g in R]
            rs = [jnp.sum(lk[g], axis=1, keepdims=True) for g in R]
            for g in R:
                o_ref[:, tile(g)] = acc[g] + pv[g]
                if not diag:
                    c_ref[:, tile(g)] = jnp.where(col == j, c[g], cm[g])
                run_ref[g] = c[g] + rs[g]

        pair(i, True)

        @pl.loop(0, i)
        def _(jj):
            pair(i - 1 - jj, False)

        if n:
            pl.when(i == nblk - 1)(finish)

    blk = pl.BlockSpec((Q_BLOCK, H * L), lambda i: (i, 0))
    full = pl.BlockSpec((T, H * L), lambda i: (0, 1))
    res = _pcall(
        body, name="sb_fwd", grid=(nblk,), in_specs=[blk, full] + ride_in_specs,
        out_specs=[blk, blk] + ride_out_specs,
        out_shape=[jax.ShapeDtypeStruct((T, H * L), F32), jax.ShapeDtypeStruct((T, H * L), F32)] + ride_out,
        scratch_shapes=[pltpu.VMEM((H, Q_BLOCK, 1), F32)] + ride_sems,
        compiler_params=_params(("arbitrary",)),
    )(proj, proj, *ride)
    return res[0], res[1], list(res[2:])


def _sb_bwd(proj, carry, do, ride=()):
    T = proj.shape[0]
    H, d, L = SB_HEADS, HEAD_DIM, 2 * HEAD_DIM
    nblk = T // Q_BLOCK
    scale = d ** -0.5
    n = len(ride)
    ride_out = [jax.ShapeDtypeStruct(a.shape, a.dtype) for a in ride]
    ride_in_specs, ride_out_specs, ride_sems = _ride_specs(ride, ride_out, N_CHIP - 1)
    R = range(H)
    tile = lambda g: slice(g * L, (g + 1) * L)

    def body(*refs):
        q_ref, kv_ref, c_ref, do_ref = refs[:4]
        rin = refs[4:4 + n]
        dq_ref, dkv_ref = refs[4 + n:6 + n]
        rout = refs[6 + n:6 + 2 * n]
        run_ref = refs[6 + 2 * n]
        i = pl.program_id(0)
        if n:
            start, finish = _chips_protocol(rin, rout, *refs[7 + 2 * n:])
            pl.when(i == 0)(start)

        @pl.when(i == 0)
        def _():
            dkv_ref[...] = jnp.zeros_like(dkv_ref)

        causal, after, before, col = _sb_consts()
        qs = [q_ref[:, tile(g)] * scale for g in R]
        dov = [do_ref[:, tile(g)] for g in R]
        dq_ref[...] = jnp.zeros_like(dq_ref)
        run_ref[...] = jnp.zeros_like(run_ref)

        def pair(j, diag):
            rows = pl.ds(pl.multiple_of(j * Q_BLOCK, Q_BLOCK), Q_BLOCK)
            kvj = [kv_ref[rows, tile(g)] for g in R]
            gsum = [run_ref[g] for g in R]
            dq0 = [dq_ref[:, tile(g)] for g in R]
            dkv0 = [dkv_ref[rows, tile(g)] for g in R]
            cm = None if diag else [c_ref[:, tile(g)] for g in R]
            z = [_dot(qs[g], kvj[g], NT) for g in R]
            sp = [_softplus(z[g]) for g in R]
            lk = [jnp.where(causal, -sp[g], 0.0) if diag else -sp[g] for g in R]
            ls = [z[g] - sp[g] for g in R]
            logw = [ls[g] + _dot2m(lk[g], after) for g in R]
            if not diag:
                logw = [logw[g] + jnp.sum(jnp.where(col == j, cm[g], 0.0), axis=1, keepdims=True) for g in R]
            e = [jnp.exp(logw[g]) for g in R]
            w = [jnp.where(causal, e[g], 0.0) if diag else e[g] for g in R]
            gw = [_dot(dov[g], kvj[g], NT) * w[g] for g in R]
            gpre = [gsum[g] + _dot2m(gw[g], before) for g in R]
            sig = [jnp.exp(ls[g]) for g in R]
            dz = [gw[g] * (1.0 - sig[g]) - sig[g] * gpre[g] for g in R]
            if diag:
                dz = [jnp.where(causal, dz[g], 0.0) for g in R]
            dzb = [dz[g].astype(BF16) for g in R]
            dq1 = [_dot(dzb[g], kvj[g]) for g in R]
            dkv1 = [_dot(dzb[g], qs[g], TN) + _dot(w[g].astype(BF16), dov[g], TN) for g in R]
            gs1 = [jnp.sum(gw[g], axis=1, keepdims=True) for g in R]
            for g in R:
                dq_ref[:, tile(g)] = dq0[g] + dq1[g]
                dkv_ref[rows, tile(g)] = dkv0[g] + dkv1[g]
                run_ref[g] = gsum[g] + gs1[g]

        @pl.loop(0, i)
        def _(j):
            pair(j, False)

        pair(i, True)
        dq_ref[...] = dq_ref[...] * scale
        if n:
            pl.when(i == nblk - 1)(finish)

    blk = pl.BlockSpec((Q_BLOCK, H * L), lambda i: (i, 0))
    once = pl.Buffered(1)
    sh = jax.ShapeDtypeStruct((T, H * L), F32)
    res = _pcall(
        body, name="sb_bwd", grid=(nblk,),
        in_specs=[blk, pl.BlockSpec((T, H * L), lambda i: (0, 1), pipeline_mode=once), blk, blk] + ride_in_specs,
        out_specs=[blk, pl.BlockSpec((T, H * L), lambda i: (0, 0), pipeline_mode=once)] + ride_out_specs,
        out_shape=[sh, sh] + ride_out,
        scratch_shapes=[pltpu.VMEM((H, Q_BLOCK, 1), F32)] + ride_sems,
        compiler_params=_params(("arbitrary",)),
    )(proj, proj, carry, do, *ride)
    return res[0], res[1], list(res[2:])


def _swa_common(q_ref, kvp_ref, kvc_ref, qg_ref, kg_ref, sk_ref, sl_ref, n):
    W, d, G = WINDOW, HEAD_DIM, SWA_GROUP
    scale = d ** -0.5
    row = _iota2((W, 2 * W), 0)
    col = _iota2((W, 2 * W), 1)
    dist = row + W - col
    valid = (dist >= 0) & (dist < W) & ((n > 0) | (col >= W))
    distf = dist.astype(F32)
    kvcat = jnp.concatenate([kvp_ref[...], kvc_ref[...]], axis=0)
    KH, QH = range(SWA_KV_HEADS), range(SWA_HEADS)
    kraw = [kvcat[:, hk * d:(hk + 1) * d] for hk in KH]
    vcat = [kvcat[:, SWA_KVW + hk * d:SWA_KVW + (hk + 1) * d].astype(BF16) for hk in KH]
    rk = [lax.rsqrt(jnp.mean(kraw[hk] * kraw[hk], axis=-1, keepdims=True) + EPS) for hk in KH]
    kh = [kraw[hk] * rk[hk] for hk in KH]
    kn = [(kh[hk] * kg_ref[...]).astype(BF16) for hk in KH]
    qraw = [q_ref[:, h * d:(h + 1) * d] for h in QH]
    rq = [lax.rsqrt(jnp.mean(qraw[h] * qraw[h], axis=-1, keepdims=True) + EPS) for h in QH]
    qh = [qraw[h] * rq[h] for h in QH]
    qn = [(qh[h] * qg_ref[...]).astype(BF16) for h in QH]
    sink = [sk_ref[h:h + 1, :1] for h in QH]
    s = [jnp.where(valid, _dot(qn[h], kn[h // G], NT) * scale - sl_ref[h:h + 1, :1] * distf, -1e30) for h in QH]
    m = [jnp.maximum(jnp.max(s[h], axis=1, keepdims=True), sink[h]) for h in QH]
    p = [jnp.where(valid, jnp.exp(s[h] - m[h]), 0.0) for h in QH]
    esink = [jnp.exp(sink[h] - m[h]) for h in QH]
    den = [jnp.sum(p[h], axis=1, keepdims=True) + esink[h] for h in QH]
    prob = [p[h] / den[h] for h in QH]
    return vcat, rk, kh, kn, rq, qh, qn, esink, den, prob


def _swa_specs(T):
    W = WINDOW
    q = pl.BlockSpec((W, SWA_QW), lambda n: (n, 0))
    prev = pl.BlockSpec((W, 2 * SWA_KVW), lambda n: (jnp.maximum(n - 1, 0), SWA_QW // (2 * SWA_KVW)))
    cur = pl.BlockSpec((W, 2 * SWA_KVW), lambda n: (n, SWA_QW // (2 * SWA_KVW)))
    gain = pl.BlockSpec((1, HEAD_DIM), lambda n: (0, 0))
    perhead = pl.BlockSpec((SWA_HEADS, LANE), lambda n: (0, 0))
    return q, prev, cur, gain, perhead


def _swa_fwd(proj, qg, kg, sinks, slopes):
    T = proj.shape[0]
    W, d, G = WINDOW, HEAD_DIM, SWA_GROUP

    def body(q_ref, kvp_ref, kvc_ref, qg_ref, kg_ref, sk_ref, sl_ref, o_ref):
        vcat, _, _, _, _, _, _, _, _, prob = _swa_common(q_ref, kvp_ref, kvc_ref, qg_ref, kg_ref, sk_ref, sl_ref,
                                                         pl.program_id(0))
        outs = [_bdot(prob[h], vcat[h // G]) for h in range(SWA_HEADS)]
        o_ref[...] = jnp.concatenate(outs, axis=1).astype(BF16)

    q, prev, cur, gain, perhead = _swa_specs(T)
    return _pcall(
        body, name="swa_fwd", grid=(T // W,), in_specs=[q, prev, cur, gain, gain, perhead, perhead], out_specs=q,
        out_shape=jax.ShapeDtypeStruct((T, SWA_QW), BF16), compiler_params=_params(("parallel",)),
    )(proj, proj, proj, qg, kg, sinks, slopes)


def _swa_bwd(proj, qg, kg, sinks, slopes, do):
    T = proj.shape[0]
    W, d, G = WINDOW, HEAD_DIM, SWA_GROUP
    scale = d ** -0.5
    KH, QH = range(SWA_KV_HEADS), range(SWA_HEADS)

    def body(q_ref, kvp_ref, kvc_ref, qg_ref, kg_ref, sk_ref, sl_ref, do_ref,
             dq_ref, dkv_ref, dqg_ref, dkg_ref, dsk_ref):
        n = pl.program_id(0)

        @pl.when(n == 0)
        def _():
            dqg_ref[...] = jnp.zeros_like(dqg_ref)
            dkg_ref[...] = jnp.zeros_like(dkg_ref)
            dsk_ref[...] = jnp.zeros_like(dsk_ref)
            dkv_ref[...] = jnp.zeros_like(dkv_ref)

        vcat, rk, kh, kn, rq, qh, qn, esink, den, prob = _swa_common(q_ref, kvp_ref, kvc_ref, qg_ref, kg_ref,
                                                                     sk_ref, sl_ref, n)
        dov = [do_ref[:, h * d:(h + 1) * d].astype(BF16) for h in QH]
        dp = [_dot(dov[h], vcat[h // G], NT) for h in QH]
        dd = [jnp.sum(prob[h] * dp[h], axis=1, keepdims=True) for h in QH]
        dsb = [(prob[h] * (dp[h] - dd[h]) * scale).astype(BF16) for h in QH]
        dsink = [-jnp.sum((esink[h] / den[h]) * dd[h], axis=0, keepdims=True) for h in QH]
        dqn = [_dot(dsb[h], kn[h // G]) for h in QH]
        dkn_h = [_dot(dsb[h], qn[h], TN) for h in QH]
        dv_h = [_dot(prob[h].astype(BF16), dov[h], TN) for h in QH]
        dqh = [dqn[h] * qg_ref[...] for h in QH]
        dq = [rq[h] * (dqh[h] - qh[h] * jnp.mean(dqh[h] * qh[h], axis=-1, keepdims=True)) for h in QH]
        dkn = [sum(dkn_h[hk * G + g] for g in range(G)) for hk in KH]
        dvc = [sum(dv_h[hk * G + g] for g in range(G)) for hk in KH]
        dkh = [dkn[hk] * kg_ref[...] for hk in KH]
        dkraw = [rk[hk] * (dkh[hk] - kh[hk] * jnp.mean(dkh[hk] * kh[hk], axis=-1, keepdims=True)) for hk in KH]
        dq_ref[...] = jnp.concatenate(dq, axis=1)
        dqg_ref[...] += sum(jnp.sum(dqn[h] * qh[h], axis=0, keepdims=True) for h in QH)
        dkg_ref[...] += sum(jnp.sum(dkn[hk] * kh[hk], axis=0, keepdims=True) for hk in KH)
        rowh = _iota2((SWA_HEADS, LANE), 0)
        dsk_ref[...] += sum(jnp.where(rowh == h, dsink[h], 0.0) for h in QH)
        upd = jnp.concatenate(dkraw + dvc, axis=1)
        offp = pl.multiple_of(jnp.maximum(n - 1, 0) * W, W)
        offc = pl.multiple_of(n * W, W)
        dkv_ref[pl.ds(offp, W), :] += upd[:W]
        dkv_ref[pl.ds(offc, W), :] += upd[W:]

    q, prev, cur, gain, perhead = _swa_specs(T)
    kvfull = pl.BlockSpec((T, 2 * SWA_KVW), lambda n: (0, 0))
    gs = jax.ShapeDtypeStruct((1, d), F32)
    return _pcall(
        body, name="swa_bwd", grid=(T // W,), in_specs=[q, prev, cur, gain, gain, perhead, perhead, q],
        out_specs=[q, kvfull, gain, gain, perhead],
        out_shape=[jax.ShapeDtypeStruct((T, SWA_QW), F32), jax.ShapeDtypeStruct((T, 2 * SWA_KVW), F32), gs, gs,
                   jax.ShapeDtypeStruct((SWA_HEADS, LANE), F32)],
        compiler_params=_params(("arbitrary",)),
    )(proj, proj, proj, qg, kg, sinks, slopes, do)


def _alibi():
    s = [2.0 ** (-8.0 * (i + 1) / SWA_HEADS) for i in range(SWA_HEADS)]
    return jnp.broadcast_to(jnp.asarray(s, F32)[:, None], (SWA_HEADS, LANE))


def _head_tiles(lo, hi):
    shp = lo.shape[:-1]
    return jnp.concatenate([lo.reshape(shp + (SB_HEADS, HEAD_DIM)), hi.reshape(shp + (SB_HEADS, HEAD_DIM))],
                           axis=-1).reshape(shp + (SB_LANES,))


def _tile_halves(x):
    shp = x.shape[:-1]
    t = x.reshape(shp + (SB_HEADS, 2, HEAD_DIM))
    return t[..., 0, :].reshape(shp + (SB_W,)), t[..., 1, :].reshape(shp + (SB_W,))


def _att_in_weights(w_in):
    sq, sk, sv = w_in[:, :SB_W], w_in[:, SB_W:2 * SB_W], w_in[:, 2 * SB_W:3 * SB_W]
    return jnp.concatenate([_head_tiles(sq, jnp.zeros_like(sq)), _head_tiles(sk, sv)], axis=1), w_in[:, 3 * SB_W:]


def _att_out_weights(w_out):
    wo = w_out[:SB_W]
    return _head_tiles(jnp.zeros_like(wo).T, wo.T).T, w_out[SB_W:]


def _att_fwd(h, g, w_in, w_out_of, q_gain, k_gain, sinks, ride=()):
    hn = _rms_fwd(h, g, "att_norm")
    w_sb, w_swa = _att_in_weights(w_in)
    proj_sb = _mm(hn, w_sb, "nn", out_dtype=BF16, name="att_in_sb")
    proj_swa = _mm(hn, w_swa, "nn", name="att_in_swa")
    a_out, carry, gathered = _sb_fwd(proj_sb, ride)
    w_out = w_out_of(gathered)
    wo_sb, wo_swa = _att_out_weights(w_out)
    sk128 = jnp.broadcast_to(sinks.reshape(SWA_HEADS, 1), (SWA_HEADS, LANE))
    qg, kg = q_gain.reshape(1, HEAD_DIM), k_gain.reshape(1, HEAD_DIM)
    b_out = _swa_fwd(proj_swa, qg, kg, sk128, _alibi())
    h2 = _mm(a_out, wo_sb, "nn", res=h, a2=b_out, b2=wo_swa, name="att_out")
    return h2, (h, hn, proj_sb, proj_swa, carry, a_out, b_out, sk128, qg, kg), gathered


def _att_bwd(dh2, saved, g, w_in, w_out, ride=()):
    h, hn, proj_sb, proj_swa, carry, a_out, b_out, sk128, qg, kg = saved
    w_sb, w_swa = _att_in_weights(w_in)
    wo_sb, wo_swa = _att_out_weights(w_out)
    da = _mm(dh2, wo_sb, "nt", out_dtype=BF16, name="att_do_sb")
    db = _mm(dh2, wo_swa, "nt", name="att_do_swa")
    dwo_sb = _mm(a_out, dh2, "tn", out_dtype=BF16, name="att_dwout_sb")
    dwo_swa = _mm(b_out, dh2, "tn", out_dtype=BF16, name="att_dwout_swa")
    dw_out = jnp.concatenate([_tile_halves(dwo_sb.T)[1].T, dwo_swa], axis=0)
    dq, dkv, rode = _sb_bwd(proj_sb, carry, da, ride)
    dbq, dbkv, dqg, dkg, dsink = _swa_bwd(proj_swa, qg, kg, sk128, _alibi(), db)
    dproj = jnp.concatenate([dq.astype(BF16), dkv.astype(BF16), dbq.astype(BF16), dbkv.astype(BF16)], axis=1)
    w_all = jnp.concatenate([w_sb, w_swa], axis=1)
    dw_all = _mm(hn, dproj, "tn", out_dtype=BF16, name="att_dwin")
    dhn = _mm(dproj, w_all, "nt", name="att_dhn")
    dsq, _ = _tile_halves(dw_all[:, :SB_LANES])
    dsk, dsv = _tile_halves(dw_all[:, SB_LANES:2 * SB_LANES])
    dw_in = jnp.concatenate([dsq, dsk, dsv, dw_all[:, 2 * SB_LANES:]], axis=1)
    dh, dg = _rms_bwd(dhn, h, g, dh2, "att_dnorm")
    return dh, dg, dw_in, dw_out, dqg.reshape(HEAD_DIM), dkg.reshape(HEAD_DIM), dsink[:, 0], rode


CONV_ROWS = 512
CONV_COLS = 512
HALO = 8


def _shifted(xcat, s, tm):
    if s == 0:
        return xcat[HALO:HALO + tm]
    return pltpu.roll(xcat, s, 0)[HALO:HALO + tm]


def _conv_pre(x_ref, halo_ref, w_ref, i, tm):
    xc = x_ref[...]
    halo = jnp.where(i > 0, halo_ref[...], 0.0)
    xcat = jnp.concatenate([halo, xc], axis=0)
    w = w_ref[...]
    y = w[GDN_CONV - 1:GDN_CONV] * xc
    for kk in range(GDN_CONV - 1):
        y = y + w[kk:kk + 1] * _shifted(xcat, GDN_CONV - 1 - kk, tm)
    return xcat, y


def _l2_heads(s, qscale_of):
    outs, rs = [], []
    for hh in range(s.shape[1] // GDN_HEAD_DIM):
        sh = s[:, hh * GDN_HEAD_DIM:(hh + 1) * GDN_HEAD_DIM]
        r = lax.rsqrt(jnp.sum(sh * sh, axis=-1, keepdims=True) + EPS)
        outs.append(sh * r)
        rs.append(r)
    return outs, rs


def _conv_specs(T, col0, tm, tc):
    cur = pl.BlockSpec((tm, tc), lambda j, i: (i, j + col0 // tc))
    halo = pl.BlockSpec((HALO, tc), lambda j, i: (jnp.maximum(i * (tm // HALO) - 1, 0), j + col0 // tc))
    wsp = pl.BlockSpec((GDN_CONV, tc), lambda j, i: (0, j + col0 // tc))
    out = pl.BlockSpec((tm, tc), lambda j, i: (i, j))
    return cur, halo, wsp, out


def _conv_fwd(proj, conv_w, col0, width, norm, name):
    T = proj.shape[0]
    tm, tc = _pick(T, CONV_ROWS), CONV_COLS
    cur, halo, wsp, out = _conv_specs(T, col0, tm, tc)
    n_q_tiles = (width // 2) // tc

    def body(x_ref, halo_ref, w_ref, o_ref):
        j, i = pl.program_id(0), pl.program_id(1)
        _, y = _conv_pre(x_ref, halo_ref, w_ref, i, tm)
        s = y * _sigmoid(y)
        if norm:
            outs, _ = _l2_heads(s, None)
            qs = jnp.where(j < n_q_tiles, GDN_HEAD_DIM ** -0.5, 1.0)
            o_ref[...] = jnp.concatenate(outs, axis=1) * qs
        else:
            o_ref[...] = s

    return _pcall(body, name=name, grid=(width // tc, T // tm), in_specs=[cur, halo, wsp], out_specs=out,
                  out_shape=jax.ShapeDtypeStruct((T, width), F32),
                  compiler_params=_params(("parallel", "parallel")))(proj, proj, conv_w)


def _conv_bwd_pre(proj, conv_w, dout, col0, width, norm, name):
    T = proj.shape[0]
    tm, tc = _pick(T, CONV_ROWS), CONV_COLS
    cur, halo, wsp, out = _conv_specs(T, col0, tm, tc)
    n_q_tiles = (width // 2) // tc

    def body(x_ref, halo_ref, w_ref, d_ref, dy_ref, dw_ref):
        j, i = pl.program_id(0), pl.program_id(1)
        xcat, y = _conv_pre(x_ref, halo_ref, w_ref, i, tm)
        sg = _sigmoid(y)
        s = y * sg
        d = d_ref[...]
        if norm:
            qs = jnp.where(j < n_q_tiles, GDN_HEAD_DIM ** -0.5, 1.0)
            d = d * qs
            outs, rs = _l2_heads(s, None)
            parts = []
            for hh, (nh, r) in enumerate(zip(outs, rs)):
                dh = d[:, hh * GDN_HEAD_DIM:(hh + 1) * GDN_HEAD_DIM]
                parts.append(r * (dh - nh * jnp.sum(dh * nh, axis=-1, keepdims=True)))
            ds = jnp.concatenate(parts, axis=1)
        else:
            ds = d
        dy = ds * sg * (1.0 + y * (1.0 - sg))
        dy_ref[...] = dy
        rows = [jnp.sum(dy * _shifted(xcat, GDN_CONV - 1 - kk, tm), axis=0, keepdims=True) for kk in range(GDN_CONV)]
        part = jnp.concatenate(rows, axis=0)

        @pl.when(i == 0)
        def _():
            dw_ref[...] = part

        @pl.when(i > 0)
        def _():
            dw_ref[...] += part

    wout = pl.BlockSpec((GDN_CONV, tc), lambda j, i: (0, j))
    return _pcall(body, name=name, grid=(width // tc, T // tm), in_specs=[cur, halo, wsp, out], out_specs=[out, wout],
                  out_shape=[jax.ShapeDtypeStruct((T, width), F32), jax.ShapeDtypeStruct((GDN_CONV, width), F32)],
                  compiler_params=_params(("parallel", "arbitrary")))(proj, proj, conv_w, dout)


def _conv_bwd_in(dy, conv_w, name):
    T, C = dy.shape
    tm, tc = _pick(T, CONV_ROWS), CONV_COLS
    nrow = T // tm

    def body(d_ref, nxt_ref, w_ref, dx_ref):
        i = pl.program_id(0)
        dc = d_ref[...]
        nxt = jnp.where(i < nrow - 1, nxt_ref[...], 0.0)
        dcat = jnp.concatenate([dc, nxt], axis=0)
        w = w_ref[...]
        dx = w[GDN_CONV - 1:GDN_CONV] * dc
        for kk in range(GDN_CONV - 1):
            s = GDN_CONV - 1 - kk
            dx = dx + w[kk:kk + 1] * pltpu.roll(dcat, tm + HALO - s, 0)[:tm]
        dx_ref[...] = dx.astype(BF16)

    cur = pl.BlockSpec((tm, tc), lambda i, j: (i, j))
    nxt = pl.BlockSpec((HALO, tc), lambda i, j: (jnp.minimum((i + 1) * (tm // HALO), T // HALO - 1), j))
    wsp = pl.BlockSpec((GDN_CONV, tc), lambda i, j: (0, j))
    return _pcall(body, name=name, grid=(nrow, C // tc), in_specs=[cur, nxt, wsp], out_specs=cur,
                  out_shape=jax.ShapeDtypeStruct((T, C), BF16),
                  compiler_params=_params(("parallel", "parallel")))(dy, dy, conv_w)


GATE_ROWS = 512


def _chunk_mask(n, lower):
    row = _iota2((n, n), 0)
    col = _iota2((n, n), 1)
    same = (row // GDN_CHUNK) == (col // GDN_CHUNK)
    tri = (row >= col) if lower else (row <= col)
    return (same & tri).astype(BF16)


def _gates_fwd(proj, a_log, dt_bias):
    T = proj.shape[0]
    tm = _pick(T, GATE_ROWS)
    c0 = (GDN_CONV_W + GDN_VW) // LANE

    def body(bl_ref, a_ref, alog_ref, dt_ref, beta_ref, g_ref, gc_ref):
        beta_ref[...] = _sigmoid(bl_ref[...])
        g = -jnp.exp(alog_ref[...]) * _softplus(a_ref[...] + dt_ref[...])
        g_ref[...] = g
        gc_ref[...] = _mdot2(_chunk_mask(tm, True), g)

    blk = lambda c: pl.BlockSpec((tm, LANE), lambda i: (i, c))
    vec = pl.BlockSpec((1, LANE), lambda i: (0, 0))
    sh = jax.ShapeDtypeStruct((T, LANE), F32)
    return _pcall(body, name="gdn_gates", grid=(T // tm,), in_specs=[blk(c0), blk(c0 + 1), vec, vec],
                  out_specs=[blk(0), blk(0), blk(0)], out_shape=[sh, sh, sh],
                  compiler_params=_params(("parallel",)))(proj, proj, a_log, dt_bias)


def _gates_bwd(proj, a_log, dt_bias, beta, g, dbeta, dgc):
    T = proj.shape[0]
    tm = _pick(T, GATE_ROWS)
    c0 = (GDN_CONV_W + GDN_VW) // LANE

    def body(a_ref, alog_ref, dt_ref, beta_ref, g_ref, dbeta_ref, dgc_ref, dbl_ref, da_ref, dalog_ref, ddt_ref):
        dg = _mdot2(_chunk_mask(tm, False), dgc_ref[...])
        b = beta_ref[...]
        dbl_ref[...] = (dbeta_ref[...] * b * (1.0 - b)).astype(BF16)
        da = dg * (-jnp.exp(alog_ref[...])) * _sigmoid(a_ref[...] + dt_ref[...])
        da_ref[...] = da.astype(BF16)
        p1 = jnp.sum(dg * g_ref[...], axis=0, keepdims=True)
        p2 = jnp.sum(da, axis=0, keepdims=True)

        @pl.when(pl.program_id(0) == 0)
        def _():
            dalog_ref[...] = p1
            ddt_ref[...] = p2

        @pl.when(pl.program_id(0) > 0)
        def _():
            dalog_ref[...] += p1
            ddt_ref[...] += p2

    blk = lambda c: pl.BlockSpec((tm, LANE), lambda i: (i, c))
    vec = pl.BlockSpec((1, LANE), lambda i: (0, 0))
    shb = jax.ShapeDtypeStruct((T, LANE), BF16)
    shv = jax.ShapeDtypeStruct((1, LANE), F32)
    return _pcall(body, name="gdn_dgates", grid=(T // tm,),
                  in_specs=[blk(c0 + 1), vec, vec, blk(0), blk(0), blk(0), blk(0)],
                  out_specs=[blk(0), blk(0), vec, vec], out_shape=[shb, shb, shv, shv],
                  compiler_params=_params(("arbitrary",)))(proj, a_log, dt_bias, beta, g, dbeta, dgc)


def _inv_unit_lower(Ls):
    C = Ls[0].shape[0]
    row = _iota2((C, C), 0)
    col = _iota2((C, C), 1)
    blk16 = (row // 16) == (col // 16)
    blk32 = (row // 32) == (col // 32)
    eye = (row == col).astype(F32)
    xs = [-jnp.where(blk16, L, 0.0) for L in Ls]
    inv = [eye + x for x in xs]
    for _ in range(3):
        xs = [_dot3(x, x) for x in xs]
        inv = [a + _dot3(a, x) for a, x in zip(inv, xs)]
    for mask in (blk32 & ~blk16, ~blk32):
        t = [_dot3(a, jnp.where(mask, L, 0.0)) for a, L in zip(inv, Ls)]
        inv = [a - _dot3(ti, a) for a, ti in zip(inv, t)]
    return inv


GDN_GROUP = 4
GDN_PREP_CHUNKS = 4


def _gdn_specs(T):
    C, D, E = GDN_CHUNK, GDN_HEAD_DIM, GDN_GROUP
    n = T // C
    qk = pl.BlockSpec((C, (E // 2) * D), lambda h, i: (i, h))
    vE = pl.BlockSpec((C, E * D), lambda h, i: (i, h))
    colv = pl.BlockSpec((C, LANE), lambda h, i: (i, 0))
    colo = pl.BlockSpec((None, C, LANE), lambda h, i: (h, i, 0))
    rowv = pl.BlockSpec((E, None, 1, C), lambda h, i: (h, i, 0, 0))
    st = pl.BlockSpec((E, None, D, D), lambda h, i: (h, i, 0, 0))
    am = pl.BlockSpec((E, None, C, C), lambda h, i: (h, i, 0, 0))
    return n, qk, vE, colv, colo, rowv, st, am


def _lane_col(blk, lane):
    return jnp.sum(jnp.where(_iota2(blk.shape, 1) == lane, blk, 0.0), axis=1, keepdims=True)


def _gdn_decay(gcol, grow):
    C = GDN_CHUNK
    row = _iota2((C, C), 0)
    col = _iota2((C, C), 1)
    incl = row >= col
    dm = jnp.where(incl, jnp.exp(jnp.where(incl, gcol - grow, 0.0)), 0.0)
    glast = grow[:, C - 1:C]
    return dm, jnp.exp(gcol), jnp.exp(glast), jnp.exp(glast - gcol), row > col, incl


def _gdn_prep(k, beta, gcol, grow):
    T = k.shape[0]
    C, D, B = GDN_CHUNK, GDN_HEAD_DIM, GDN_PREP_CHUNKS
    n = T // C

    def body(k_ref, b_ref, gc_ref, gr_ref, a_ref):
        idx = [(e, cb) for e in range(2) for cb in range(B)]
        kc = {cb: k_ref[cb * C:(cb + 1) * C, :] for cb in range(B)}
        lm = []
        head0 = 2 * pl.program_id(0)
        for e, cb in idx:
            beta = _lane_col(b_ref[cb * C:(cb + 1) * C, :], head0 + e)
            dm, _, _, _, strict, _ = _gdn_decay(_lane_col(gc_ref[cb * C:(cb + 1) * C, :], head0 + e), gr_ref[e, cb])
            lm.append(jnp.where(strict, _bdot(kc[cb] * beta, kc[cb], NT) * dm, 0.0))
        inv = _inv_unit_lower(lm)
        for (e, cb), a in zip(idx, inv):
            a_ref[e, cb] = a

    return _pcall(
        body, name="gdn_prep", grid=(GDN_K_HEADS, n // B),
        in_specs=[pl.BlockSpec((B * C, D), lambda h, i: (i, h)), pl.BlockSpec((B * C, LANE), lambda h, i: (i, 0)),
                  pl.BlockSpec((B * C, LANE), lambda h, i: (i, 0)), pl.BlockSpec((2, B, 1, C), lambda h, i: (h, i, 0, 0))],
        out_specs=pl.BlockSpec((2, B, C, C), lambda h, i: (h, i, 0, 0)),
        out_shape=jax.ShapeDtypeStruct((GDN_V_HEADS, n, C, C), F32),
        compiler_params=_params(("parallel", "parallel")),
    )(k, beta, gcol, grow)


def _gdn_fwd(q, k, v, beta, gcol, grow, amat):
    T = q.shape[0]
    C, D, E = GDN_CHUNK, GDN_HEAD_DIM, GDN_GROUP
    n, qk, vE, colv, colo, rowv, st, am = _gdn_specs(T)
    R = range(E)

    def body(q_ref, k_ref, v_ref, b_ref, gc_ref, gr_ref, a_ref, o_ref, s_ref, vn_ref, state):
        @pl.when(pl.program_id(1) == 0)
        def _():
            state[...] = jnp.zeros_like(state)

        qv = [q_ref[:, (e // 2) * D:(e // 2 + 1) * D] for e in R]
        kv = [k_ref[:, (e // 2) * D:(e // 2 + 1) * D] for e in R]
        vv = [v_ref[:, e * D:(e + 1) * D] for e in R]
        head0 = E * pl.program_id(0)
        beta = [_lane_col(b_ref[...], head0 + e) for e in R]
        a = [a_ref[e] for e in R]
        s = [state[e] for e in R]
        dec = [_gdn_decay(_lane_col(gc_ref[...], head0 + e), gr_ref[e]) for e in R]
        pm = [_bdot(qv[e], kv[e], NT) * dec[e][0] for e in R]
        r = [beta[e] * (vv[e] - _bdot(kv[e] * dec[e][1], s[e])) for e in R]
        vn = [_dot3(a[e], r[e]) for e in R]
        o = [_bdot(qv[e] * dec[e][1], s[e]) + _bdot(pm[e], vn[e]) for e in R]
        s2 = [dec[e][2] * s[e] + _bdot(kv[e] * dec[e][3], vn[e], TN) for e in R]
        for e in R:
            s_ref[e] = s[e]
            vn_ref[:, e * D:(e + 1) * D] = vn[e]
            o_ref[:, e * D:(e + 1) * D] = o[e]
            state[e] = s2[e]

    shv = jax.ShapeDtypeStruct((T, GDN_V_HEADS * D), F32)
    return _pcall(
        body, name="gdn_fwd", grid=(GDN_V_HEADS // E, n), in_specs=[qk, qk, vE, colv, colv, rowv, am],
        out_specs=[vE, st, vE],
        out_shape=[shv, jax.ShapeDtypeStruct((GDN_V_HEADS, n, D, D), F32), shv],
        scratch_shapes=[pltpu.VMEM((E, D, D), F32)],
        compiler_params=_params(("parallel", "arbitrary")),
    )(q, k, v, beta, gcol, grow, amat)


def _gdn_bwd(q, k, v, beta, gcol, grow, states, amat, vnew, do):
    T = q.shape[0]
    C, D, E = GDN_CHUNK, GDN_HEAD_DIM, GDN_GROUP
    n, qk, vE, colv, colo, rowv, st, am = _gdn_specs(T)
    rev = lambda spec: pl.BlockSpec(spec.block_shape, (lambda f: (lambda h, i: f(h, n - 1 - i)))(spec.index_map))
    qk, vE, colv, colo, rowv, st, am = (rev(s) for s in (qk, vE, colv, colo, rowv, st, am))
    R = range(E)

    def body(q_ref, k_ref, v_ref, b_ref, gc_ref, gr_ref, s_ref, a_ref, vn_ref, do_ref,
             dq_ref, dk_ref, dv_ref, db_ref, dgc_ref, dstate):
        @pl.when(pl.program_id(1) == 0)
        def _():
            dstate[...] = jnp.zeros_like(dstate)

        M = lambda f: [f(e) for e in R]
        rsum = lambda x: jnp.sum(x, axis=1, keepdims=True)
        qv = M(lambda e: q_ref[:, (e // 2) * D:(e // 2 + 1) * D])
        kv = M(lambda e: k_ref[:, (e // 2) * D:(e // 2 + 1) * D])
        vv = M(lambda e: v_ref[:, e * D:(e + 1) * D])
        vn = M(lambda e: vn_ref[:, e * D:(e + 1) * D])
        dov = M(lambda e: do_ref[:, e * D:(e + 1) * D])
        head0 = E * pl.program_id(0)
        beta = M(lambda e: _lane_col(b_ref[...], head0 + e))
        s = M(lambda e: s_ref[e])
        a = M(lambda e: a_ref[e])
        dsn = M(lambda e: dstate[e])
        dec = M(lambda e: _gdn_decay(_lane_col(gc_ref[...], head0 + e), gr_ref[e]))
        dm, gam, glast, tail = (M(lambda e: dec[e][i]) for i in range(4))
        strict, incl = dec[0][4], dec[0][5]
        kb = M(lambda e: kv[e] * beta[e])
        kd = M(lambda e: kv[e] * gam[e])
        qd = M(lambda e: qv[e] * gam[e])
        kt = M(lambda e: kv[e] * tail[e])
        lmat = M(lambda e: jnp.where(strict, _bdot(kb[e], kv[e], NT) * dm[e], 0.0))
        pmat = M(lambda e: _bdot(qv[e], kv[e], NT) * dm[e])
        xres = M(lambda e: vv[e] - _bdot(kd[e], s[e]))
        dvn = M(lambda e: _bdot(pmat[e], dov[e], TN) + _bdot(kt[e], dsn[e]))
        dqd = M(lambda e: _bdot(dov[e], s[e], NT))
        dp = M(lambda e: jnp.where(incl, _bdot(dov[e], vn[e], NT), 0.0))
        dkt = M(lambda e: _bdot(vn[e], dsn[e], NT))
        dr = M(lambda e: _dot3(a[e], dvn[e], TN))
        drb = M(lambda e: beta[e] * dr[e])
        dkd = M(lambda e: -_bdot(drb[e], s[e], NT))
        ds2 = M(lambda e: _bdot(qd[e], dov[e], TN) + glast[e] * dsn[e] - _bdot(kd[e], drb[e], TN))
        dl = M(lambda e: -jnp.where(strict, _bdot(dr[e], vn[e], NT), 0.0))
        dmm = M(lambda e: dl[e] * dm[e])
        dnn = M(lambda e: dp[e] * dm[e])
        emat = M(lambda e: dl[e] * lmat[e] + dp[e] * pmat[e])
        dkb = M(lambda e: _bdot(dmm[e], kv[e]))
        dk = M(lambda e: beta[e] * dkb[e] + _bdot(dmm[e], kb[e], TN) + _bdot(dnn[e], qv[e], TN)
               + gam[e] * dkd[e] + tail[e] * dkt[e])
        dq = M(lambda e: _bdot(dnn[e], kv[e]) + gam[e] * dqd[e])
        dbeta = M(lambda e: rsum(dr[e] * xres[e]) + rsum(dkb[e] * kv[e]))
        ones = jnp.ones((C, LANE), BF16)
        colsum = M(lambda e: _dot2m(emat[e], ones, TN)[:, :1])
        tails = M(lambda e: rsum(dkt[e] * kt[e]))
        lastrow = _iota2((C, 1), 0) == C - 1
        dlast = M(lambda e: jnp.sum(tails[e], axis=0, keepdims=True)
                  + glast[e] * jnp.sum(rsum(s[e] * dsn[e]), axis=0, keepdims=True))
        dgc = M(lambda e: rsum(emat[e]) - colsum[e] + rsum(dkd[e] * kd[e]) + rsum(dqd[e] * qd[e]) - tails[e]
                + jnp.where(lastrow, dlast[e], 0.0))
        lane = _iota2((C, LANE), 1)
        db_all = jnp.zeros((C, LANE), F32)
        dgc_all = jnp.zeros((C, LANE), F32)
        for e in R:
            dv_ref[:, e * D:(e + 1) * D] = drb[e]
            db_all = jnp.where(lane == e, dbeta[e], db_all)
            dgc_all = jnp.where(lane == e, dgc[e], dgc_all)
            dstate[e] = ds2[e]
        db_ref[...] = db_all
        dgc_ref[...] = dgc_all
        for kh in range(E // 2):
            dq_ref[:, kh * D:(kh + 1) * D] = dq[2 * kh] + dq[2 * kh + 1]
            dk_ref[:, kh * D:(kh + 1) * D] = dk[2 * kh] + dk[2 * kh + 1]

    shq = jax.ShapeDtypeStruct((T, GDN_K_HEADS * D), F32)
    shv = jax.ShapeDtypeStruct((T, GDN_V_HEADS * D), F32)
    shc = jax.ShapeDtypeStruct((GDN_V_HEADS // E, T, LANE), F32)
    return _pcall(
        body, name="gdn_bwd", grid=(GDN_V_HEADS // E, n),
        in_specs=[qk, qk, vE, colv, colv, rowv, st, am, vE, vE],
        out_specs=[qk, qk, vE, colo, colo], out_shape=[shq, shq, shv, shc, shc],
        scratch_shapes=[pltpu.VMEM((E, D, D), F32)],
        compiler_params=_params(("parallel", "arbitrary")),
    )(q, k, v, beta, gcol, grow, states, amat, vnew, do)


def _outgate_fwd(o, proj, gain):
    T = o.shape[0]
    tm, tc = _pick(T, CONV_ROWS), CONV_COLS
    z0 = GDN_CONV_W // tc

    def body(o_ref, z_ref, g_ref, y_ref):
        z = z_ref[...]
        sz = z * _sigmoid(z)
        parts = []
        for hh in range(tc // GDN_HEAD_DIM):
            oh = o_ref[:, hh * GDN_HEAD_DIM:(hh + 1) * GDN_HEAD_DIM]
            r = lax.rsqrt(jnp.mean(oh * oh, axis=-1, keepdims=True) + EPS)
            parts.append(oh * r * g_ref[...])
        y_ref[...] = (jnp.concatenate(parts, axis=1) * sz).astype(BF16)

    blk = pl.BlockSpec((tm, tc), lambda i, j: (i, j))
    return _pcall(body, name="gdn_outgate", grid=(T // tm, GDN_VW // tc),
                  in_specs=[blk, pl.BlockSpec((tm, tc), lambda i, j: (i, j + z0)), pl.BlockSpec((1, GDN_HEAD_DIM), lambda i, j: (0, 0))],
                  out_specs=blk, out_shape=jax.ShapeDtypeStruct((T, GDN_VW), BF16),
                  compiler_params=_params(("parallel", "parallel")))(o, proj, gain)


def _outgate_bwd(dy, o, proj, gain):
    T = o.shape[0]
    tm, tc = _pick(T, CONV_ROWS), CONV_COLS
    z0 = GDN_CONV_W // tc
    nh = tc // GDN_HEAD_DIM

    def body(dy_ref, o_ref, z_ref, g_ref, do_ref, dz_ref, dg_ref):
        z = z_ref[...]
        sg = _sigmoid(z)
        sz = z * sg
        dy = dy_ref[...]
        dgain = jnp.zeros((1, GDN_HEAD_DIM), F32)
        dos, ys = [], []
        for hh in range(nh):
            sl = slice(hh * GDN_HEAD_DIM, (hh + 1) * GDN_HEAD_DIM)
            oh = o_ref[:, sl]
            r = lax.rsqrt(jnp.mean(oh * oh, axis=-1, keepdims=True) + EPS)
            xh = oh * r
            dn = dy[:, sl] * sz[:, sl]
            dgain = dgain + jnp.sum(dn * xh, axis=0, keepdims=True)
            dxh = dn * g_ref[...]
            dos.append(r * (dxh - xh * jnp.mean(dxh * xh, axis=-1, keepdims=True)))
            ys.append(xh * g_ref[...])
        do_ref[...] = jnp.concatenate(dos, axis=1)
        dz_ref[...] = (dy * jnp.concatenate(ys, axis=1) * sg * (1.0 + z * (1.0 - sg))).astype(BF16)
        first = (pl.program_id(0) == 0) & (pl.program_id(1) == 0)

        @pl.when(first)
        def _():
            dg_ref[...] = dgain

        @pl.when(jnp.logical_not(first))
        def _():
            dg_ref[...] += dgain

    blk = pl.BlockSpec((tm, tc), lambda i, j: (i, j))
    vec = pl.BlockSpec((1, GDN_HEAD_DIM), lambda i, j: (0, 0))
    return _pcall(body, name="gdn_doutgate", grid=(T // tm, GDN_VW // tc),
                  in_specs=[blk, blk, pl.BlockSpec((tm, tc), lambda i, j: (i, j + z0)), vec],
                  out_specs=[blk, blk, vec],
                  out_shape=[jax.ShapeDtypeStruct((T, GDN_VW), F32), jax.ShapeDtypeStruct((T, GDN_VW), BF16),
                             jax.ShapeDtypeStruct((1, GDN_HEAD_DIM), F32)],
                  compiler_params=_params(("arbitrary", "arbitrary")))(dy, o, proj, gain)


def _pad_lanes(vec):
    return jnp.pad(vec.reshape(1, -1), ((0, 0), (0, LANE - vec.shape[-1])))


def _head_rows(a):
    T = a.shape[0]
    return a[:, :GDN_V_HEADS].T.reshape(GDN_V_HEADS, T // GDN_CHUNK, 1, GDN_CHUNK)


def _group_lanes(a):
    T = a.shape[1]
    heads = a[:, :, :GDN_GROUP].transpose(1, 0, 2).reshape(T, GDN_V_HEADS)
    return jnp.pad(heads, ((0, 0), (0, LANE - GDN_V_HEADS)))


def _gdn_pad_in(w_in):
    c = GDN_CONV_W + GDN_VW
    z = jnp.zeros(w_in.shape[:-1] + (LANE - GDN_V_HEADS,), w_in.dtype)
    return jnp.concatenate([w_in[..., :c + GDN_V_HEADS], z, w_in[..., c + GDN_V_HEADS:], z], axis=-1)


def _gdn_unpad_in(dw):
    c = GDN_CONV_W + GDN_VW
    return jnp.concatenate([dw[..., :c + GDN_V_HEADS], dw[..., c + LANE:c + LANE + GDN_V_HEADS]], axis=-1)


def _gdn_mixer_fwd(h, g, w_in_pad, conv_w, a_log, dt_bias, out_gain, w_out):
    T = h.shape[0]
    hn = _rms_fwd(h, g, "gdn_norm")
    proj = _mm(hn, w_in_pad, "nn", name="gdn_in")
    qk = _conv_fwd(proj, conv_w, 0, 2 * GDN_KW, True, "gdn_conv_qk")
    vv = _conv_fwd(proj, conv_w, 2 * GDN_KW, GDN_VW, False, "gdn_conv_v")
    alog, dtb = _pad_lanes(a_log), _pad_lanes(dt_bias)
    beta, gl, gc = _gates_fwd(proj, alog, dtb)
    grow = _head_rows(gc)
    qn, kn = qk[:, :GDN_KW], qk[:, GDN_KW:]
    amat = _gdn_prep(kn, beta, gc, grow)
    o, states, vnew = _gdn_fwd(qn, kn, vv, beta, gc, grow, amat)
    gain = out_gain.reshape(1, GDN_HEAD_DIM)
    y = _outgate_fwd(o, proj, gain)
    h2 = _mm(y, w_out, "nn", res=h, name="gdn_out")
    return h2, (h, hn, proj, qn, kn, vv, beta, gl, gc, grow, o, states, amat, vnew, y, alog, dtb, gain)


def _gdn_mixer_bwd(dh2, saved, g, w_in_pad, conv_w, w_out):
    h, hn, proj, qn, kn, vv, beta, gl, gc, grow, o, states, amat, vnew, y, alog, dtb, gain = saved
    T = h.shape[0]
    dy = _mm(dh2, w_out, "nt", name="gdn_dy")
    dw_out = _mm(y, dh2, "tn", out_dtype=BF16, name="gdn_dwout")
    do, dz, dgain = _outgate_bwd(dy, o, proj, gain)
    dq, dk, dv, dbeta, dgc = _gdn_bwd(qn, kn, vv, beta, gc, grow, states, amat, vnew, do)
    dqk = jnp.concatenate([dq, dk], axis=1)
    dy_qk, dcw_qk = _conv_bwd_pre(proj, conv_w, dqk, 0, 2 * GDN_KW, True, "gdn_dconv_qk")
    dy_v, dcw_v = _conv_bwd_pre(proj, conv_w, dv, 2 * GDN_KW, GDN_VW, False, "gdn_dconv_v")
    dx_qk = _conv_bwd_in(dy_qk, conv_w[:, :2 * GDN_KW], "gdn_dconvin_qk")
    dx_v = _conv_bwd_in(dy_v, conv_w[:, 2 * GDN_KW:], "gdn_dconvin_v")
    dbl, da, dalog, ddt = _gates_bwd(proj, alog, dtb, beta, gl, _group_lanes(dbeta), _group_lanes(dgc))
    dproj = jnp.concatenate([dx_qk, dx_v, dz, dbl, da], axis=1)
    dw_in_pad = _mm(hn, dproj, "tn", out_dtype=BF16, name="gdn_dwin")
    dhn = _mm(dproj, w_in_pad, "nt", name="gdn_dhn")
    dh, dg = _rms_bwd(dhn, h, g, dh2, "gdn_dnorm")
    dconv = jnp.concatenate([dcw_qk, dcw_v], axis=1)
    return (dh, dg, _gdn_unpad_in(dw_in_pad), dconv, dalog[0, :GDN_V_HEADS], ddt[0, :GDN_V_HEADS],
            dgain.reshape(GDN_HEAD_DIM), dw_out)


def _instances(full):
    out = {}
    for n, a in full.items():
        if n.startswith("ffn_"):
            for i in range(2):
                for j in range(2):
                    out[(n, i, j)] = a[i, j]
        elif n in ("mix_norm", "ple_norm", "ple_w_gate", "ple_w_proj"):
            for i in range(2):
                out[(n, i)] = a[i]
        else:
            out[(n,)] = a[0]
    return out


def _stacked(inst):
    out = {}
    for n in dict.fromkeys(k[0] for k in inst):
        if n.startswith("ffn_"):
            out[n] = jnp.stack([jnp.stack([inst[(n, i, j)] for j in range(2)]) for i in range(2)])
        elif n in ("mix_norm", "ple_norm", "ple_w_gate", "ple_w_proj"):
            out[n] = jnp.stack([inst[(n, i)] for i in range(2)])
        else:
            out[n] = inst[(n,)][None]
    return out


def _local_step(x, p, target, w, late_shards=(), late_weights=None, early_grads=None):
    w = dict(w)
    ffn = lambda i, j: (w[("ffn_norm", i, j)], w[("ffn_w_gate", i, j)], w[("ffn_w_up", i, j)], w[("ffn_w_down", i, j)])
    h = x
    tape = []
    for i in range(2):
        h, s1 = _ffn_fwd(h, *ffn(i, 0), f"ffn{i}a")
        if i == 0:
            def w_out_of(gathered):
                if late_weights is not None:
                    w.update(late_weights(gathered))
                return w[("att_w_out",)]
            h, s2, _ = _att_fwd(h, w[("mix_norm", 0)], w[("att_w_in",)], w_out_of, w[("att_q_norm",)],
                                w[("att_k_norm",)], w[("att_sinks",)], late_shards)
        else:
            gdn_in_pad = _gdn_pad_in(w[("gdn_w_in",)])
            h, s2 = _gdn_mixer_fwd(h, w[("mix_norm", 1)], gdn_in_pad, w[("gdn_conv_w",)], w[("gdn_a_log",)],
                                   w[("gdn_dt_bias",)], w[("gdn_out_norm",)], w[("gdn_w_out",)])
        h, s3 = _ffn_fwd(h, *ffn(i, 1), f"ffn{i}b")
        h, s4 = _ple_fwd(h, p[i], w[("ple_norm", i)], w[("ple_w_gate", i)], w[("ple_w_proj", i)], f"ple{i}")
        tape.append((s1, s2, s3, s4))

    loss, dh = _loss_head(h, target)

    g = {}
    rode = []
    for i in (1, 0):
        s1, s2, s3, s4 = tape[i]
        dh, g[("ple_norm", i)], g[("ple_w_gate", i)], g[("ple_w_proj", i)] = _ple_bwd(
            dh, s4, p[i], w[("ple_norm", i)], w[("ple_w_gate", i)], f"ple{i}")
        dh, g[("ffn_norm", i, 1)], g[("ffn_w_gate", i, 1)], g[("ffn_w_up", i, 1)], g[("ffn_w_down", i, 1)] = _ffn_bwd(
            dh, s3, *ffn(i, 1), f"ffn{i}b")
        if i == 0:
            ride = early_grads(g) if early_grads is not None else ()
            (dh, g[("mix_norm", 0)], g[("att_w_in",)], g[("att_w_out",)], g[("att_q_norm",)], g[("att_k_norm",)],
             g[("att_sinks",)], rode) = _att_bwd(dh, s2, w[("mix_norm", 0)], w[("att_w_in",)], w[("att_w_out",)], ride)
        else:
            (dh, g[("mix_norm", 1)], g[("gdn_w_in",)], g[("gdn_conv_w",)], g[("gdn_a_log",)], g[("gdn_dt_bias",)],
             g[("gdn_out_norm",)], g[("gdn_w_out",)]) = _gdn_mixer_bwd(
                dh, s2, w[("mix_norm", 1)], gdn_in_pad, w[("gdn_conv_w",)], w[("gdn_w_out",)])
        dh, g[("ffn_norm", i, 0)], g[("ffn_w_gate", i, 0)], g[("ffn_w_up", i, 0)], g[("ffn_w_down", i, 0)] = _ffn_bwd(
            dh, s1, *ffn(i, 0), f"ffn{i}a")
    return loss, dh, g, rode


MESH = pl.DeviceIdType.MESH
N_CHIP = 4


def _place():
    x, y, c = lax.axis_index("x"), lax.axis_index("y"), lax.axis_index("c")
    others = [((1 - x, y), 2 * (1 - x) + y), ((x, 1 - y), 2 * x + (1 - y)), ((1 - x, 1 - y), 2 * (1 - x) + (1 - y))]
    return x, y, c, 4 * x + 2 * y + c, 2 * x + y, (x, y, 1 - c), others


def _comm_call(body, arrays, out_shape, n_sems, name):
    hbm = pl.BlockSpec(memory_space=pl.ANY)
    n = len(arrays)
    return _pcall(
        body, name=name, in_specs=[hbm] * n, out_specs=[hbm] * len(out_shape), out_shape=out_shape,
        scratch_shapes=[pltpu.SemaphoreType.DMA((n, n_sems)), pltpu.SemaphoreType.DMA((n, n_sems)),
                        pltpu.SemaphoreType.DMA((n, N_CHIP))],
        compiler_params=pltpu.CompilerParams(has_side_effects=True),
    )(*arrays)


def _gather_protocol(ins, outs, send_sems, recv_sems, local_sems):
    n = len(ins)
    x, y, c, me, my_chip, sibling, others = _place()

    def copy(a, k, block, to, src=None):
        dst = outs[a].at[block]
        return pltpu.make_async_remote_copy(
            src_ref=dst if src is None else src, dst_ref=dst, send_sem=send_sems.at[a, k],
            recv_sem=recv_sems.at[a, k], device_id=to, device_id_type=MESH)

    local = [pltpu.make_async_copy(ins[a], outs[a].at[me], local_sems.at[a, 0]) for a in range(n)]
    first = []
    for a in range(n):
        first.append(copy(a, 0, me, sibling, src=ins[a]))
        first += [copy(a, 1 + j, me, (*chip, c), src=ins[a]) for j, (chip, _) in enumerate(others)]

    def start():
        for cp in local + first:
            cp.start()

    def finish():
        passed = []
        for a in range(n):
            for j, (chip, chip_idx) in enumerate(others):
                blk = 2 * chip_idx + c
                copy(a, 1 + j, blk, (x, y, c)).wait_recv()
                fwd = copy(a, 4 + j, blk, sibling)
                fwd.start()
                passed.append(fwd)
        for a in range(n):
            copy(a, 0, 2 * my_chip + (1 - c), (x, y, c)).wait_recv()
            for j, (chip, chip_idx) in enumerate(others):
                copy(a, 4 + j, 2 * chip_idx + (1 - c), (x, y, c)).wait_recv()
        for cp in first + passed:
            cp.wait_send()
        for cp in local:
            cp.wait()

    return start, finish


def _all_gather(arrays):
    n = len(arrays)

    def body(*refs):
        start, finish = _gather_protocol(refs[:n], refs[n:2 * n], *refs[2 * n:])
        start()
        finish()

    out_shape = [jax.ShapeDtypeStruct((N_DEV,) + a.shape, a.dtype) for a in arrays]
    return _comm_call(body, arrays, out_shape, N_DEV - 1, "gather_weights")


def _exchange_sibling(arrays, name):
    n = len(arrays)

    def body(*refs):
        ins, got = refs[:n], refs[n:2 * n]
        send_sems, recv_sems, _ = refs[2 * n:]
        x, y, c, me, my_chip, sibling, others = _place()
        remote = []
        for a in range(n):
            for chip in range(N_CHIP):
                rc = pltpu.make_async_remote_copy(
                    src_ref=ins[a].at[2 * chip + (1 - c)], dst_ref=got[a].at[chip], send_sem=send_sems.at[a, chip],
                    recv_sem=recv_sems.at[a, chip], device_id=sibling, device_id_type=MESH)
                rc.start()
                remote.append(rc)
        for rc in remote:
            rc.wait()

    half = [jax.ShapeDtypeStruct((N_CHIP,) + a.shape[1:], a.dtype) for a in arrays]
    return _comm_call(body, arrays, half, N_CHIP, name)


def _chips_protocol(ins, outs, send_sems, recv_sems, local_sems):
    n = len(ins)
    x, y, c, me, my_chip, sibling, others = _place()
    local = [pltpu.make_async_copy(ins[a].at[my_chip], outs[a].at[my_chip], local_sems.at[a, 0]) for a in range(n)]
    remote = [pltpu.make_async_remote_copy(
        src_ref=ins[a].at[chip_idx], dst_ref=outs[a].at[my_chip], send_sem=send_sems.at[a, j],
        recv_sem=recv_sems.at[a, j], device_id=(*chip, c), device_id_type=MESH)
        for a in range(n) for j, (chip, chip_idx) in enumerate(others)]

    def start():
        for cp in local + remote:
            cp.start()

    def finish():
        for cp in remote + local:
            cp.wait()

    return start, finish


def _exchange_chips(arrays, name):
    n = len(arrays)

    def body(*refs):
        start, finish = _chips_protocol(refs[:n], refs[n:2 * n], *refs[2 * n:])
        start()
        finish()

    out_shape = [jax.ShapeDtypeStruct(a.shape, a.dtype) for a in arrays]
    return _comm_call(body, arrays, out_shape, N_CHIP - 1, name)


def _as_rows(a, lead):
    shp = a.shape
    return a.reshape(shp[:lead] + (math.prod(shp[lead:-1]), shp[-1]))


def _row_tile(rows, cap=512):
    if rows <= cap:
        return rows
    for t in range(cap - cap % 8, 0, -8):
        if rows % t == 0:
            return t
    return rows


def _pair_sum(send, got, name):
    a3, b3 = _as_rows(send, 1), _as_rows(got, 1)
    _, rows, last = b3.shape
    tr = _row_tile(rows)

    def body(c_ref, a_ref, b_ref, o_ref):
        o_ref[...] = (a_ref[...].astype(F32) + b_ref[...].astype(F32)).astype(o_ref.dtype)

    core = lax.axis_index("c").astype(jnp.int32).reshape(1)
    out = _pcall(
        body, name=name,
        grid_spec=pltpu.PrefetchScalarGridSpec(
            num_scalar_prefetch=1, grid=(N_CHIP, rows // tr),
            in_specs=[pl.BlockSpec((None, tr, last), lambda k, i, c_ref: (2 * k + c_ref[0], i, 0)),
                      pl.BlockSpec((None, tr, last), lambda k, i, c_ref: (k, i, 0))],
            out_specs=pl.BlockSpec((None, tr, last), lambda k, i, c_ref: (k, i, 0))),
        out_shape=jax.ShapeDtypeStruct(b3.shape, got.dtype), compiler_params=_params(("parallel", "parallel")),
    )(core, a3, b3)
    return out.reshape(got.shape)


def _adamw(parts, w, m, v, name):
    p3 = _as_rows(parts, 1)
    w2, m2, v2 = (_as_rows(z, 0) for z in (w, m, v))
    rows, last = w2.shape
    tr = _row_tile(rows)
    c1 = 1.0 / (1.0 - ADAM_B1 ** ADAM_STEP)
    c2 = 1.0 / (1.0 - ADAM_B2 ** ADAM_STEP)

    def body(p_ref, w_ref, m_ref, v_ref, g_ref, d_ref, nm_ref, nv_ref):
        g = p_ref[0].astype(F32)
        for chip in range(1, N_CHIP):
            g = g + p_ref[chip].astype(F32)
        mn = ADAM_B1 * m_ref[...] + (1.0 - ADAM_B1) * g
        vn = ADAM_B2 * v_ref[...] + (1.0 - ADAM_B2) * (g * g)
        g_ref[...] = g
        nm_ref[...] = mn
        nv_ref[...] = vn
        d_ref[...] = -ADAM_LR * ((mn * c1) / (jnp.sqrt(vn * c2) + ADAM_EPS) + ADAM_WD * w_ref[...])

    row = pl.BlockSpec((tr, last), lambda i: (i, 0))
    sh = jax.ShapeDtypeStruct((rows, last), F32)
    outs = _pcall(body, name=name, grid=(rows // tr,),
                  in_specs=[pl.BlockSpec((N_CHIP, tr, last), lambda i: (0, i, 0)), row, row, row],
                  out_specs=[row, row, row, row], out_shape=[sh, sh, sh, sh],
                  compiler_params=_params(("parallel",)))(p3, w2, m2, v2)
    return [o.reshape(w.shape) for o in outs]


def _pack(pieces, row_align):
    rows, offs, r = [], [], 0
    for a in pieces:
        flat = a.reshape(-1)
        nr = -(-flat.shape[0] // PACK_W)
        flat = jnp.pad(flat, (0, nr * PACK_W - flat.shape[0]))
        rows.append(flat.reshape(nr, PACK_W))
        offs.append(r)
        r += nr
    pad = (-r) % row_align
    if pad:
        rows.append(jnp.zeros((pad, PACK_W), pieces[0].dtype))
    return jnp.concatenate(rows, axis=0), offs


def _unpack(flat, offs, shapes):
    out = []
    for off, shp in zip(offs, shapes):
        size = math.prod(shp)
        nr = -(-size // PACK_W)
        out.append(flat[..., off:off + nr, :].reshape(flat.shape[:-2] + (nr * PACK_W,))[..., :size].reshape(flat.shape[:-2] + tuple(shp)))
    return out


def _to_full(gathered, axis):
    z = jnp.moveaxis(gathered, 0, axis)
    shp = list(z.shape)
    return z.reshape(shp[:axis] + [shp[axis] * shp[axis + 1]] + shp[axis + 2:])


def _to_shards(full, axis):
    shp = list(full.shape)
    z = full.reshape(shp[:axis] + [N_DEV, shp[axis] // N_DEV] + shp[axis + 1:])
    return jnp.moveaxis(z, axis, 0)


def kernel(x, p, ffn_norm, ffn_w_gate, ffn_w_up, ffn_w_down, mix_norm, att_w_in, att_q_norm, att_k_norm, att_sinks, att_w_out, gdn_w_in, gdn_conv_w, gdn_a_log, gdn_dt_bias, gdn_out_norm, gdn_w_out, ple_norm, ple_w_gate, ple_w_proj, loss_target, m_ffn_norm, m_ffn_w_gate, m_ffn_w_up, m_ffn_w_down, m_mix_norm, m_att_w_in, m_att_q_norm, m_att_k_norm, m_att_sinks, m_att_w_out, m_gdn_w_in, m_gdn_conv_w, m_gdn_a_log, m_gdn_dt_bias, m_gdn_out_norm, m_gdn_w_out, m_ple_norm, m_ple_w_gate, m_ple_w_proj, v_ffn_norm, v_ffn_w_gate, v_ffn_w_up, v_ffn_w_down, v_mix_norm, v_att_w_in, v_att_q_norm, v_att_k_norm, v_att_sinks, v_att_w_out, v_gdn_w_in, v_gdn_conv_w, v_gdn_a_log, v_gdn_dt_bias, v_gdn_out_norm, v_gdn_w_out, v_ple_norm, v_ple_w_gate, v_ple_w_proj):
    args = dict(locals())
    wts = {n: args[n] for n in WEIGHTS}
    mom = {n: args["m_" + n] for n in WEIGHTS}
    var = {n: args["v_" + n] for n in WEIGHTS}
    axis = dict(SHARDED)
    vecs = [n for n, _ in SHARDED[:SMALL_SHARDED]]
    small = vecs + list(REPLICATED)
    small_shapes = [wts[n].shape for n in small]
    lead = lambda n: 2 if n.startswith("ffn_") else 1

    def stack_of(arrays, name, idxs):
        return jnp.stack([arrays[name][idx] if idx else arrays[name][0] for idx in idxs])

    def full_instances(gathered, group):
        out = {}
        for (name, idxs), g in zip(group, gathered):
            whole = _to_full(g, axis[name] - lead(name) + 1)
            for k, idx in enumerate(idxs):
                out[(name,) + idx] = whole[k]
        return out

    def shard_stacks(g, group):
        return [_to_shards(jnp.stack([g[(name,) + idx] for idx in idxs]), axis[name] - lead(name) + 1)
                for name, idxs in group]

    vec_pack, voffs = _pack([wts[n] for n in vecs], 8)
    early = _all_gather([stack_of(wts, n, idxs).astype(BF16) for n, idxs in EARLY] + [vec_pack])
    w = full_instances(early[:-1], EARLY)
    vec_full = {n: _to_full(piece, axis[n]) for n, piece in
                zip(vecs, _unpack(early[-1], voffs, [wts[n].shape for n in vecs]))}
    w.update(_instances({**vec_full, **{n: wts[n] for n in REPLICATED}}))
    late_shards = [stack_of(wts, n, idxs).astype(BF16) for n, idxs in LATE]

    def early_grads(g):
        send = shard_stacks(g, RIDE)
        got = _exchange_sibling(send, "exchange_sibling_early")
        return [_pair_sum(p_, q_, f"pair_sum_early_{i}") for i, (p_, q_) in enumerate(zip(send, got))]

    loss, grad_x, g, rode = _local_step(x[0], p[:, 0], loss_target[0], w, late_shards,
                                        lambda gathered: full_instances(gathered, LATE), early_grads)
    loss = lax.psum(loss, ("x", "y", "c"))

    gs = _stacked({k: v for k, v in g.items() if k[0] in small})
    vec_shards = [_to_shards(gs[n], axis[n]) for n in vecs]
    small_send = jnp.stack([_pack([sh[d] for sh in vec_shards] + [gs[n] for n in REPLICATED], 8)[0] for d in range(N_DEV)])
    send = shard_stacks(g, FINAL) + [small_send]
    got = _exchange_sibling(send, "exchange_sibling_final")
    chip_sums = [_pair_sum(p_, q_, f"pair_sum_final_{i}") for i, (p_, q_) in enumerate(zip(send, got))]
    last = _exchange_chips(chip_sums, "exchange_chips_final")

    pieces = {}
    for (name, idxs), part in list(zip(RIDE, rode)) + list(zip(FINAL, last[:-1])):
        for k, idx in enumerate(idxs):
            pieces[(name,) + idx] = part[:, k]
    outs = {}
    for n, _ in SHARDED[SMALL_SHARDED:]:
        if lead(n) == 2:
            part = jnp.stack([jnp.stack([pieces[(n, i, j)] for j in range(2)], axis=1) for i in range(2)], axis=1)
        elif (n, 0) in pieces:
            part = jnp.stack([pieces[(n, i)] for i in range(2)], axis=1)
        else:
            part = pieces[(n,)][:, None]
        outs[n] = _adamw(part, wts[n], mom[n], var[n], f"adamw_{n}")
    small_w, soffs = _pack([wts[n] for n in small], 8)
    small_m, _ = _pack([mom[n] for n in small], 8)
    small_v, _ = _pack([var[n] for n in small], 8)
    small_out = [_unpack(z, soffs, small_shapes) for z in _adamw(last[-1], small_w, small_m, small_v, "adamw_small")]
    for i, n in enumerate(small):
        outs[n] = [small_out[k][i] for k in range(4)]
    result = [loss, grad_x[None]]
    for k in range(4):
        result += [outs[n][k] for n in WEIGHTS]
    return tuple(result)
```

```python
import math

import jax
import jax.numpy as jnp
from jax import lax
from jax.experimental import pallas as pl
from jax.experimental.pallas import tpu as pltpu

F32 = jnp.float32
BF16 = jnp.bfloat16

N_DEV = 8
D_MODEL = 1024
D_FF = 2816
PLE_DIM = 256
HEAD_DIM = 64
SB_HEADS = 8
SWA_HEADS = 8
SWA_KV_HEADS = 2
SWA_GROUP = SWA_HEADS // SWA_KV_HEADS
WINDOW = 128
Q_BLOCK = 128
GDN_K_HEADS = 8
GDN_V_HEADS = 16
GDN_HEAD_DIM = 128
GDN_CONV = 4
GDN_CHUNK = 64
EPS = 1e-6
SB_W = SB_HEADS * HEAD_DIM
SWA_QW = SWA_HEADS * HEAD_DIM
SWA_KVW = SWA_KV_HEADS * HEAD_DIM
ATT_IN = 3 * SB_W + SWA_QW + 2 * SWA_KVW
GDN_KW = GDN_K_HEADS * GDN_HEAD_DIM
GDN_VW = GDN_V_HEADS * GDN_HEAD_DIM
GDN_CONV_W = 2 * GDN_KW + GDN_VW
GDN_IN = GDN_CONV_W + GDN_VW + 2 * GDN_V_HEADS
GDN_IN_PAD = GDN_CONV_W + GDN_VW + 2 * 128

ADAM_LR = 0.001
ADAM_B1 = 0.9
ADAM_B2 = 0.999
ADAM_EPS = 1e-08
ADAM_WD = 0.01
ADAM_STEP = 10

LANE = 128
VMEM_LIMIT = 56 * 1024 * 1024
PACK_W = 1024

NN = ((1,), (0,))
NT = ((1,), (1,))
TN = ((0,), (0,))

SHARDED = (
    ("ffn_norm", 2), ("gdn_conv_w", 2),
    ("ffn_w_gate", 3), ("ffn_w_up", 3), ("ffn_w_down", 2), ("att_w_in", 2), ("att_w_out", 1),
    ("gdn_w_in", 2), ("gdn_w_out", 1), ("ple_w_gate", 1), ("ple_w_proj", 2),
)
SMALL_SHARDED = 2
REPLICATED = ("mix_norm", "att_q_norm", "att_k_norm", "att_sinks", "gdn_a_log", "gdn_dt_bias",
              "gdn_out_norm", "ple_norm")
WEIGHTS = ("ffn_norm", "ffn_w_gate", "ffn_w_up", "ffn_w_down", "mix_norm", "att_w_in", "att_q_norm",
           "att_k_norm", "att_sinks", "att_w_out", "gdn_w_in", "gdn_conv_w", "gdn_a_log", "gdn_dt_bias",
           "gdn_out_norm", "gdn_w_out", "ple_norm", "ple_w_gate", "ple_w_proj")


_FFN_REST = [(0, 1), (1, 0), (1, 1)]
EARLY = [("ffn_w_gate", [(0, 0)]), ("ffn_w_up", [(0, 0)]), ("ffn_w_down", [(0, 0)]), ("att_w_in", [()])]
LATE = ([(n, [idx]) for n in ("ffn_w_gate", "ffn_w_up", "ffn_w_down") for idx in _FFN_REST]
        + [("att_w_out", [()]), ("gdn_w_in", [()]), ("gdn_w_out", [()]),
           ("ple_w_gate", [(0,), (1,)]), ("ple_w_proj", [(0,), (1,)])])
RIDE = [e for e in LATE if e[0] != "att_w_out"]
FINAL = EARLY + [("att_w_out", [()])]


def _pcall(body, **kw):
    return pl.pallas_call(body, **kw)


def _params(sem=None):
    if sem is None:
        return pltpu.CompilerParams(vmem_limit_bytes=VMEM_LIMIT)
    return pltpu.CompilerParams(dimension_semantics=sem, vmem_limit_bytes=VMEM_LIMIT)


def _dot(a, b, dims=NN):
    return lax.dot_general(a, b, (dims, ((), ())), preferred_element_type=F32)


def _bdot(a, b, dims=NN):
    return _dot(a.astype(BF16), b.astype(BF16), dims)


def _split(a):
    hi = a.astype(BF16)
    lo = (a - hi.astype(F32)).astype(BF16)
    return hi, lo


def _dot3(a, b, dims=NN):
    ah, al = _split(a)
    bh, bl = _split(b)
    return _dot(ah, bh, dims) + (_dot(ah, bl, dims) + _dot(al, bh, dims))


def _dot2m(a, m, dims=NN):
    ah, al = _split(a)
    return _dot(ah, m, dims) + _dot(al, m, dims)


def _mdot2(m, a, dims=NN):
    ah, al = _split(a)
    return _dot(m, ah, dims) + _dot(m, al, dims)


def _sigmoid(x):
    return 1.0 / (1.0 + jnp.exp(-x))


def _softplus(x):
    return jnp.maximum(x, 0.0) + jnp.log(1.0 + jnp.exp(-jnp.abs(x)))


def _pick(n, cap):
    if n <= cap:
        return n
    for t in range(cap - cap % LANE, 0, -LANE):
        if n % t == 0:
            return t
    raise ValueError(f"no tile for {n} under {cap}")


def _iota2(shape, axis):
    return lax.broadcasted_iota(jnp.int32, shape, axis)


def _mm(a, b, mode, out_dtype=F32, res=None, alpha=1.0, a2=None, b2=None, name="mm"):
    if mode == "nn":
        (M, K), N = a.shape, b.shape[1]
    elif mode == "nt":
        (M, K), N = a.shape, b.shape[0]
    else:
        (K, M), N = a.shape, b.shape[1]
    tm, tn, tk = _pick(M, 1408 if mode == "tn" else 512), _pick(N, 1408), _pick(K, 1024 if mode == "tn" else 1408)
    nk = K // tk
    dims = {"nn": NN, "nt": NT, "tn": TN}[mode]
    a_spec = pl.BlockSpec((tk, tm), lambda i, j, k: (k, i)) if mode == "tn" else pl.BlockSpec((tm, tk), lambda i, j, k: (i, k))
    b_spec = pl.BlockSpec((tn, tk), lambda i, j, k: (j, k)) if mode == "nt" else pl.BlockSpec((tk, tn), lambda i, j, k: (k, j))
    o_spec = pl.BlockSpec((tm, tn), lambda i, j, k: (i, j))
    two = a2 is not None
    has_res = res is not None
    a2_spec, b2_spec = a_spec, b_spec
    if two and a2.shape != a.shape:
        assert nk == 1 and mode == "nn" and a2.shape[0] == M and b2.shape[1] == N
        a2_spec = pl.BlockSpec((tm, a2.shape[1]), lambda i, j, k: (i, 0))
        b2_spec = pl.BlockSpec((a2.shape[1], tn), lambda i, j, k: (0, j))

    def body(*refs):
        refs = list(refs)
        a_ref, b_ref = refs[0], refs[1]
        pos = 2
        if two:
            a2_ref, b2_ref = refs[2], refs[3]
            pos = 4
        if has_res:
            res_ref = refs[pos]
            pos += 1
        o_ref, acc_ref = refs[pos], refs[pos + 1]
        k = pl.program_id(2)
        part = _bdot(a_ref[...], b_ref[...], dims)
        if two:
            part = part + _bdot(a2_ref[...], b2_ref[...], dims)

        def finish(acc):
            out = acc * alpha if alpha != 1.0 else acc
            if has_res:
                out = res_ref[...] + out
            o_ref[...] = out.astype(out_dtype)

        if nk == 1:
            finish(part)
        else:
            @pl.when(k == 0)
            def _():
                acc_ref[...] = part

            @pl.when(k > 0)
            def _():
                acc_ref[...] += part

            @pl.when(k == nk - 1)
            def _():
                finish(acc_ref[...])

    ins = [a, b]
    specs = [a_spec, b_spec]
    if two:
        ins += [a2, b2]
        specs += [a2_spec, b2_spec]
    if has_res:
        ins.append(res)
        specs.append(o_spec)
    return _pcall(
        body, name=name, grid=(M // tm, N // tn, nk), in_specs=specs, out_specs=o_spec,
        out_shape=jax.ShapeDtypeStruct((M, N), out_dtype),
        scratch_shapes=[pltpu.VMEM((tm, tn) if nk > 1 else (8, LANE), F32)],
        compiler_params=_params(("parallel", "parallel", "arbitrary")),
    )(*ins)


ROW_TILE = 256


def _rms_fwd(h, g, name):
    T, D = h.shape
    tr = _pick(T, ROW_TILE)

    def body(h_ref, g_ref, n_ref):
        x = h_ref[...]
        r = lax.rsqrt(jnp.mean(x * x, axis=-1, keepdims=True) + EPS)
        n_ref[...] = (x * r * g_ref[...]).astype(BF16)

    return _pcall(
        body, name=name, grid=(T // tr,),
        in_specs=[pl.BlockSpec((tr, D), lambda i: (i, 0)), pl.BlockSpec((1, D), lambda i: (0, 0))],
        out_specs=pl.BlockSpec((tr, D), lambda i: (i, 0)),
        out_shape=jax.ShapeDtypeStruct((T, D), BF16), compiler_params=_params(("parallel",)),
    )(h, g.reshape(1, D))


def _rms_bwd(dn, h, g, dres, name):
    T, D = h.shape
    tr = _pick(T, ROW_TILE)

    def body(dn_ref, h_ref, g_ref, dres_ref, dh_ref, dg_ref):
        x = h_ref[...]
        r = lax.rsqrt(jnp.mean(x * x, axis=-1, keepdims=True) + EPS)
        xh = x * r
        d = dn_ref[...].astype(F32)
        dxh = d * g_ref[...]
        dh_ref[...] = dres_ref[...] + r * (dxh - xh * jnp.mean(dxh * xh, axis=-1, keepdims=True))
        part = jnp.sum(d * xh, axis=0, keepdims=True)

        @pl.when(pl.program_id(0) == 0)
        def _():
            dg_ref[...] = part

        @pl.when(pl.program_id(0) > 0)
        def _():
            dg_ref[...] += part

    row = pl.BlockSpec((tr, D), lambda i: (i, 0))
    vec = pl.BlockSpec((1, D), lambda i: (0, 0))
    dh, dg = _pcall(
        body, name=name, grid=(T // tr,), in_specs=[row, row, vec, row], out_specs=[row, vec],
        out_shape=[jax.ShapeDtypeStruct((T, D), F32), jax.ShapeDtypeStruct((1, D), F32)],
        compiler_params=_params(("arbitrary",)),
    )(dn, h, g.reshape(1, D), dres)
    return dh, dg.reshape(D)


def _gateup(n, wg, wu, name):
    T, D = n.shape
    F = wg.shape[1]
    tm, tn = _pick(T, 512), _pick(F, 1408)

    def body(n_ref, wg_ref, wu_ref, a_ref, b_ref, hid_ref):
        x = n_ref[...]
        a = _dot(x, wg_ref[...])
        b = _dot(x, wu_ref[...])
        a_ref[...] = a.astype(BF16)
        b_ref[...] = b.astype(BF16)
        hid_ref[...] = (a * _sigmoid(a) * b).astype(BF16)

    o_spec = pl.BlockSpec((tm, tn), lambda i, j: (i, j))
    w_spec = pl.BlockSpec((D, tn), lambda i, j: (0, j))
    sh = jax.ShapeDtypeStruct((T, F), BF16)
    return _pcall(
        body, name=name, grid=(T // tm, F // tn),
        in_specs=[pl.BlockSpec((tm, D), lambda i, j: (i, 0)), w_spec, w_spec],
        out_specs=[o_spec, o_spec, o_spec], out_shape=[sh, sh, sh],
        compiler_params=_params(("parallel", "parallel")),
    )(n, wg, wu)


def _ffn_dhid(dy, wd, a, b, name):
    T, D = dy.shape
    F = wd.shape[0]
    tm, tn = _pick(T, 512), _pick(F, 1408)

    def body(dy_ref, wd_ref, a_ref, b_ref, da_ref, db_ref):
        dhid = 0.5 * _bdot(dy_ref[...], wd_ref[...], NT)
        av = a_ref[...].astype(F32)
        bv = b_ref[...].astype(F32)
        s = _sigmoid(av)
        da_ref[...] = (dhid * bv * s * (1.0 + av * (1.0 - s))).astype(BF16)
        db_ref[...] = (dhid * av * s).astype(BF16)

    o_spec = pl.BlockSpec((tm, tn), lambda i, j: (i, j))
    sh = jax.ShapeDtypeStruct((T, F), BF16)
    return _pcall(
        body, name=name, grid=(T // tm, F // tn),
        in_specs=[pl.BlockSpec((tm, D), lambda i, j: (i, 0)), pl.BlockSpec((tn, D), lambda i, j: (j, 0)), o_spec, o_spec],
        out_specs=[o_spec, o_spec], out_shape=[sh, sh],
        compiler_params=_params(("parallel", "parallel")),
    )(dy, wd, a, b)


def _ffn_fwd(h, g, wg, wu, wd, tag):
    n = _rms_fwd(h, g, f"{tag}_norm")
    a, b, hid = _gateup(n, wg, wu, f"{tag}_gateup")
    h2 = _mm(hid, wd, "nn", res=h, alpha=0.5, name=f"{tag}_down")
    return h2, (h, n, a, b, hid)


def _ffn_bwd(dh2, saved, g, wg, wu, wd, tag):
    h, n, a, b, hid = saved
    da, db = _ffn_dhid(dh2, wd, a, b, f"{tag}_dhid")
    dwd = _mm(hid, dh2, "tn", alpha=0.5, out_dtype=BF16, name=f"{tag}_dwd")
    dwg = _mm(n, da, "tn", out_dtype=BF16, name=f"{tag}_dwg")
    dwu = _mm(n, db, "tn", out_dtype=BF16, name=f"{tag}_dwu")
    dn = _mm(da, wg, "nt", a2=db, b2=wu, name=f"{tag}_dn")
    dh, dg = _rms_bwd(dn, h, g, dh2, f"{tag}_dnorm")
    return dh, dg, dwg, dwu, dwd


def _ple_fwd(h, p, g, w_gate, w_proj, tag):
    T, D = h.shape
    pn = _rms_fwd(h, g, f"{tag}_norm")
    tm, tn = _pick(T, 512), _pick(D, 1024)
    P = p.shape[1]

    def body(pn_ref, p_ref, wg_ref, wp_ref, h_ref, o_ref, gl_ref, pp_ref):
        gl = _dot(pn_ref[...], wg_ref[...])
        pp = _bdot(p_ref[...], wp_ref[...])
        gl_ref[...] = gl
        pp_ref[...] = pp
        o_ref[...] = h_ref[...] + _sigmoid(gl) * pp

    o_spec = pl.BlockSpec((tm, tn), lambda i, j: (i, j))
    sh = jax.ShapeDtypeStruct((T, D), F32)
    h2, gl, pp = _pcall(
        body, name=f"{tag}_fwd", grid=(T // tm, D // tn),
        in_specs=[pl.BlockSpec((tm, D), lambda i, j: (i, 0)), pl.BlockSpec((tm, P), lambda i, j: (i, 0)),
                  pl.BlockSpec((D, tn), lambda i, j: (0, j)), pl.BlockSpec((P, tn), lambda i, j: (0, j)), o_spec],
        out_specs=[o_spec, o_spec, o_spec], out_shape=[sh, sh, sh],
        compiler_params=_params(("parallel", "parallel")),
    )(pn, p, w_gate, w_proj, h)
    return h2, (h, pn, gl, pp)


def _ple_bwd(dh2, saved, p, g, w_gate, tag):
    h, pn, gl, pp = saved
    T, D = h.shape
    tr = _pick(T, ROW_TILE)

    def body(d_ref, gl_ref, pp_ref, dgl_ref, dpp_ref):
        d = d_ref[...]
        s = _sigmoid(gl_ref[...])
        dpp_ref[...] = (d * s).astype(BF16)
        dgl_ref[...] = (d * pp_ref[...] * s * (1.0 - s)).astype(BF16)

    row = pl.BlockSpec((tr, D), lambda i: (i, 0))
    sh = jax.ShapeDtypeStruct((T, D), BF16)
    dgl, dpp = _pcall(body, name=f"{tag}_dgate", grid=(T // tr,), in_specs=[row, row, row], out_specs=[row, row],
                      out_shape=[sh, sh], compiler_params=_params(("parallel",)))(dh2, gl, pp)
    dw_proj = _mm(p, dpp, "tn", out_dtype=BF16, name=f"{tag}_dwproj")
    dw_gate = _mm(pn, dgl, "tn", out_dtype=BF16, name=f"{tag}_dwgate")
    dpn = _mm(dgl, w_gate, "nt", name=f"{tag}_dpn")
    dh, dg = _rms_bwd(dpn, h, g, dh2, f"{tag}_dnorm")
    return dh, dg, dw_gate, dw_proj


def _loss_head(y, target):
    T, D = y.shape
    tr = _pick(T, ROW_TILE)

    def body(y_ref, t_ref, dy_ref, l_ref):
        e = y_ref[...] - t_ref[...]
        dy_ref[...] = e * (1.0 / D)
        part = jnp.sum(e * e, axis=0, keepdims=True)

        @pl.when(pl.program_id(0) == 0)
        def _():
            l_ref[...] = part

        @pl.when(pl.program_id(0) > 0)
        def _():
            l_ref[...] += part

    row = pl.BlockSpec((tr, D), lambda i: (i, 0))
    vec = pl.BlockSpec((1, D), lambda i: (0, 0))
    dy, l = _pcall(body, name="loss_head", grid=(T // tr,), in_specs=[row, row], out_specs=[row, vec],
                   out_shape=[jax.ShapeDtypeStruct((T, D), F32), jax.ShapeDtypeStruct((1, D), F32)],
                   compiler_params=_params(("arbitrary",)))(y, target)
    return (0.5 / D) * jnp.sum(l), dy


SB_LANES = SB_HEADS * 2 * HEAD_DIM


def _sb_consts():
    row = _iota2((Q_BLOCK, Q_BLOCK), 0)
    col = _iota2((Q_BLOCK, Q_BLOCK), 1)
    after = (row > col).astype(BF16)
    before = (row < col).astype(BF16)
    return col < row, after, before, col


def _ride_specs(ride, out_shapes, n_sems):
    hbm = pl.BlockSpec(memory_space=pl.ANY)
    n = len(ride)
    sems = [pltpu.SemaphoreType.DMA((n, n_sems)), pltpu.SemaphoreType.DMA((n, n_sems)),
            pltpu.SemaphoreType.DMA((n, N_CHIP))] if n else []
    return [hbm] * n, [hbm] * len(out_shapes), sems


def _sb_fwd(proj, ride=()):
    T = proj.shape[0]
    H, d, L = SB_HEADS, HEAD_DIM, 2 * HEAD_DIM
    nblk = T // Q_BLOCK
    scale = d ** -0.5
    n = len(ride)
    ride_out = [jax.ShapeDtypeStruct((N_DEV,) + a.shape, a.dtype) for a in ride]
    ride_in_specs, ride_out_specs, ride_sems = _ride_specs(ride, ride_out, N_DEV - 1)
    R = range(H)
    tile = lambda g: slice(g * L, (g + 1) * L)

    def body(*refs):
        q_ref, kv_ref = refs[:2]
        rin = refs[2:2 + n]
        o_ref, c_ref = refs[2 + n:4 + n]
        rout = refs[4 + n:4 + 2 * n]
        run_ref = refs[4 + 2 * n]
        i = pl.program_id(0)
        if n:
            start, finish = _gather_protocol(rin, rout, *refs[5 + 2 * n:])
            pl.when(i == 0)(start)
        causal, after, _, col = _sb_consts()
        qs = [q_ref[:, tile(g)] * scale for g in R]
        o_ref[...] = jnp.zeros_like(o_ref)
        c_ref[...] = jnp.zeros_like(c_ref)
        run_ref[...] = jnp.zeros_like(run_ref)

        def pair(j, diag):
            rows = pl.ds(pl.multiple_of(j * Q_BLOCK, Q_BLOCK), Q_BLOCK)
            kvj = [kv_ref[rows, tile(g)] for g in R]
            c = [run_ref[g] for g in R]
            acc = [o_ref[:, tile(g)] for g in R]
            cm = None if diag else [c_ref[:, tile(g)] for g in R]
            z = [_dot(qs[g], kvj[g], NT) for g in R]
            sp = [_softplus(z[g]) for g in R]
            lk = [jnp.where(causal, -sp[g], 0.0) if diag else -sp[g] for g in R]
            btw = [_dot2m(lk[g], after) for g in R]
            e = [jnp.exp((z[g] - sp[g]) + btw[g] + c[g]) for g in R]
            w = [jnp.where(causal, e[g], 0.0) if diag else e[g] for g in R]
            pv = [_bdot(w[g], kvj[g]) for g in R]
            rs = [jnp.sum(lk[g], axis=1, keepdims=True) for g in R]
            for g in R:
                o_ref[:, tile(g)] = acc[g] + pv[g]
                if not diag:
                    c_ref[:, tile(g)] = jnp.where(col == j, c[g], cm[g])
                run_ref[g] = c[g] + rs[g]

        pair(i, True)

        @pl.loop(0, i)
        def _(jj):
            pair(i - 1 - jj, False)

        if n:
            pl.when(i == nblk - 1)(finish)

    blk = pl.BlockSpec((Q_BLOCK, H * L), lambda i: (i, 0))
    full = pl.BlockSpec((T, H * L), lambda i: (0, 1))
    res = _pcall(
        body, name="sb_fwd", grid=(nblk,), in_specs=[blk, full] + ride_in_specs,
        out_specs=[blk, blk] + ride_out_specs,
        out_shape=[jax.ShapeDtypeStruct((T, H * L), F32), jax.ShapeDtypeStruct((T, H * L), F32)] + ride_out,
        scratch_shapes=[pltpu.VMEM((H, Q_BLOCK, 1), F32)] + ride_sems,
        compiler_params=_params(("arbitrary",)),
    )(proj, proj, *ride)
    return res[0], res[1], list(res[2:])


def _sb_bwd(proj, carry, do, ride=()):
    T = proj.shape[0]
    H, d, L = SB_HEADS, HEAD_DIM, 2 * HEAD_DIM
    nblk = T // Q_BLOCK
    scale = d ** -0.5
    n = len(ride)
    ride_out = [jax.ShapeDtypeStruct(a.shape, a.dtype) for a in ride]
    ride_in_specs, ride_out_specs, ride_sems = _ride_specs(ride, ride_out, N_CHIP - 1)
    R = range(H)
    tile = lambda g: slice(g * L, (g + 1) * L)

    def body(*refs):
        q_ref, kv_ref, c_ref, do_ref = refs[:4]
        rin = refs[4:4 + n]
        dq_ref, dkv_ref = refs[4 + n:6 + n]
        rout = refs[6 + n:6 + 2 * n]
        run_ref = refs[6 + 2 * n]
        i = pl.program_id(0)
        if n:
            start, finish = _chips_protocol(rin, rout, *refs[7 + 2 * n:])
            pl.when(i == 0)(start)

        @pl.when(i == 0)
        def _():
            dkv_ref[...] = jnp.zeros_like(dkv_ref)

        causal, after, before, col = _sb_consts()
        qs = [q_ref[:, tile(g)] * scale for g in R]
        dov = [do_ref[:, tile(g)] for g in R]
        dq_ref[...] = jnp.zeros_like(dq_ref)
        run_ref[...] = jnp.zeros_like(run_ref)

        def pair(j, diag):
            rows = pl.ds(pl.multiple_of(j * Q_BLOCK, Q_BLOCK), Q_BLOCK)
            kvj = [kv_ref[rows, tile(g)] for g in R]
            gsum = [run_ref[g] for g in R]
            dq0 = [dq_ref[:, tile(g)] for g in R]
            dkv0 = [dkv_ref[rows, tile(g)] for g in R]
            cm = None if diag else [c_ref[:, tile(g)] for g in R]
            z = [_dot(qs[g], kvj[g], NT) for g in R]
            sp = [_softplus(z[g]) for g in R]
            lk = [jnp.where(causal, -sp[g], 0.0) if diag else -sp[g] for g in R]
            ls = [z[g] - sp[g] for g in R]
            logw = [ls[g] + _dot2m(lk[g], after) for g in R]
            if not diag:
                logw = [logw[g] + jnp.sum(jnp.where(col == j, cm[g], 0.0), axis=1, keepdims=True) for g in R]
            e = [jnp.exp(logw[g]) for g in R]
            w = [jnp.where(causal, e[g], 0.0) if diag else e[g] for g in R]
            gw = [_dot(dov[g], kvj[g], NT) * w[g] for g in R]
            gpre = [gsum[g] + _dot2m(gw[g], before) for g in R]
            sig = [jnp.exp(ls[g]) for g in R]
            dz = [gw[g] * (1.0 - sig[g]) - sig[g] * gpre[g] for g in R]
            if diag:
                dz = [jnp.where(causal, dz[g], 0.0) for g in R]
            dzb = [dz[g].astype(BF16) for g in R]
            dq1 = [_dot(dzb[g], kvj[g]) for g in R]
            dkv1 = [_dot(dzb[g], qs[g], TN) + _dot(w[g].astype(BF16), dov[g], TN) for g in R]
            gs1 = [jnp.sum(gw[g], axis=1, keepdims=True) for g in R]
            for g in R:
                dq_ref[:, tile(g)] = dq0[g] + dq1[g]
                dkv_ref[rows, tile(g)] = dkv0[g] + dkv1[g]
                run_ref[g] = gsum[g] + gs1[g]

        @pl.loop(0, i)
        def _(j):
            pair(j, False)

        pair(i, True)
        dq_ref[...] = dq_ref[...] * scale
        if n:
            pl.when(i == nblk - 1)(finish)

    blk = pl.BlockSpec((Q_BLOCK, H * L), lambda i: (i, 0))
    once = pl.Buffered(1)
    sh = jax.ShapeDtypeStruct((T, H * L), F32)
    res = _pcall(
        body, name="sb_bwd", grid=(nblk,),
        in_specs=[blk, pl.BlockSpec((T, H * L), lambda i: (0, 1), pipeline_mode=once), blk, blk] + ride_in_specs,
        out_specs=[blk, pl.BlockSpec((T, H * L), lambda i: (0, 0), pipeline_mode=once)] + ride_out_specs,
        out_shape=[sh, sh] + ride_out,
        scratch_shapes=[pltpu.VMEM((H, Q_BLOCK, 1), F32)] + ride_sems,
        compiler_params=_params(("arbitrary",)),
    )(proj, proj, carry, do, *ride)
    return res[0], res[1], list(res[2:])


def _swa_common(q_ref, kvp_ref, kvc_ref, qg_ref, kg_ref, sk_ref, sl_ref, n):
    W, d, G = WINDOW, HEAD_DIM, SWA_GROUP
    scale = d ** -0.5
    row = _iota2((W, 2 * W), 0)
    col = _iota2((W, 2 * W), 1)
    dist = row + W - col
    valid = (dist >= 0) & (dist < W) & ((n > 0) | (col >= W))
    distf = dist.astype(F32)
    kvcat = jnp.concatenate([kvp_ref[...], kvc_ref[...]], axis=0)
    KH, QH = range(SWA_KV_HEADS), range(SWA_HEADS)
    kraw = [kvcat[:, hk * d:(hk + 1) * d] for hk in KH]
    vcat = [kvcat[:, SWA_KVW + hk * d:SWA_KVW + (hk + 1) * d].astype(BF16) for hk in KH]
    rk = [lax.rsqrt(jnp.mean(kraw[hk] * kraw[hk], axis=-1, keepdims=True) + EPS) for hk in KH]
    kh = [kraw[hk] * rk[hk] for hk in KH]
    kn = [(kh[hk] * kg_ref[...]).astype(BF16) for hk in KH]
    qraw = [q_ref[:, h * d:(h + 1) * d] for h in QH]
    rq = [lax.rsqrt(jnp.mean(qraw[h] * qraw[h], axis=-1, keepdims=True) + EPS) for h in QH]
    qh = [qraw[h] * rq[h] for h in QH]
    qn = [(qh[h] * qg_ref[...]).astype(BF16) for h in QH]
    sink = [sk_ref[h:h + 1, :1] for h in QH]
    s = [jnp.where(valid, _dot(qn[h], kn[h // G], NT) * scale - sl_ref[h:h + 1, :1] * distf, -1e30) for h in QH]
    m = [jnp.maximum(jnp.max(s[h], axis=1, keepdims=True), sink[h]) for h in QH]
    p = [jnp.where(valid, jnp.exp(s[h] - m[h]), 0.0) for h in QH]
    esink = [jnp.exp(sink[h] - m[h]) for h in QH]
    den = [jnp.sum(p[h], axis=1, keepdims=True) + esink[h] for h in QH]
    prob = [p[h] / den[h] for h in QH]
    return vcat, rk, kh, kn, rq, qh, qn, esink, den, prob


def _swa_specs(T):
    W = WINDOW
    q = pl.BlockSpec((W, SWA_QW), lambda n: (n, 0))
    prev = pl.BlockSpec((W, 2 * SWA_KVW), lambda n: (jnp.maximum(n - 1, 0), SWA_QW // (2 * SWA_KVW)))
    cur = pl.BlockSpec((W, 2 * SWA_KVW), lambda n: (n, SWA_QW // (2 * SWA_KVW)))
    gain = pl.BlockSpec((1, HEAD_DIM), lambda n: (0, 0))
    perhead = pl.BlockSpec((SWA_HEADS, LANE), lambda n: (0, 0))
    return q, prev, cur, gain, perhead


def _swa_fwd(proj, qg, kg, sinks, slopes):
    T = proj.shape[0]
    W, d, G = WINDOW, HEAD_DIM, SWA_GROUP

    def body(q_ref, kvp_ref, kvc_ref, qg_ref, kg_ref, sk_ref, sl_ref, o_ref):
        vcat, _, _, _, _, _, _, _, _, prob = _swa_common(q_ref, kvp_ref, kvc_ref, qg_ref, kg_ref, sk_ref, sl_ref,
                                                         pl.program_id(0))
        outs = [_bdot(prob[h], vcat[h // G]) for h in range(SWA_HEADS)]
        o_ref[...] = jnp.concatenate(outs, axis=1).astype(BF16)

    q, prev, cur, gain, perhead = _swa_specs(T)
    return _pcall(
        body, name="swa_fwd", grid=(T // W,), in_specs=[q, prev, cur, gain, gain, perhead, perhead], out_specs=q,
        out_shape=jax.ShapeDtypeStruct((T, SWA_QW), BF16), compiler_params=_params(("parallel",)),
    )(proj, proj, proj, qg, kg, sinks, slopes)


def _swa_bwd(proj, qg, kg, sinks, slopes, do):
    T = proj.shape[0]
    W, d, G = WINDOW, HEAD_DIM, SWA_GROUP
    scale = d ** -0.5
    KH, QH = range(SWA_KV_HEADS), range(SWA_HEADS)

    def body(q_ref, kvp_ref, kvc_ref, qg_ref, kg_ref, sk_ref, sl_ref, do_ref,
             dq_ref, dkv_ref, dqg_ref, dkg_ref, dsk_ref):
        n = pl.program_id(0)

        @pl.when(n == 0)
        def _():
            dqg_ref[...] = jnp.zeros_like(dqg_ref)
            dkg_ref[...] = jnp.zeros_like(dkg_ref)
            dsk_ref[...] = jnp.zeros_like(dsk_ref)
            dkv_ref[...] = jnp.zeros_like(dkv_ref)

        vcat, rk, kh, kn, rq, qh, qn, esink, den, prob = _swa_common(q_ref, kvp_ref, kvc_ref, qg_ref, kg_ref,
                                                                     sk_ref, sl_ref, n)
        dov = [do_ref[:, h * d:(h + 1) * d].astype(BF16) for h in QH]
        dp = [_dot(dov[h], vcat[h // G], NT) for h in QH]
        dd = [jnp.sum(prob[h] * dp[h], axis=1, keepdims=True) for h in QH]
        dsb = [(prob[h] * (dp[h] - dd[h]) * scale).astype(BF16) for h in QH]
        dsink = [-jnp.sum((esink[h] / den[h]) * dd[h], axis=0, keepdims=True) for h in QH]
        dqn = [_dot(dsb[h], kn[h // G]) for h in QH]
        dkn_h = [_dot(dsb[h], qn[h], TN) for h in QH]
        dv_h = [_dot(prob[h].astype(BF16), dov[h], TN) for h in QH]
        dqh = [dqn[h] * qg_ref[...] for h in QH]
        dq = [rq[h] * (dqh[h] - qh[h] * jnp.mean(dqh[h] * qh[h], axis=-1, keepdims=True)) for h in QH]
        dkn = [sum(dkn_h[hk * G + g] for g in range(G)) for hk in KH]
        dvc = [sum(dv_h[hk * G + g] for g in range(G)) for hk in KH]
        dkh = [dkn[hk] * kg_ref[...] for hk in KH]
        dkraw = [rk[hk] * (dkh[hk] - kh[hk] * jnp.mean(dkh[hk] * kh[hk], axis=-1, keepdims=True)) for hk in KH]
        dq_ref[...] = jnp.concatenate(dq, axis=1)
        dqg_ref[...] += sum(jnp.sum(dqn[h] * qh[h], axis=0, keepdims=True) for h in QH)
        dkg_ref[...] += sum(jnp.sum(dkn[hk] * kh[hk], axis=0, keepdims=True) for hk in KH)
        rowh = _iota2((SWA_HEADS, LANE), 0)
        dsk_ref[...] += sum(jnp.where(rowh == h, dsink[h], 0.0) for h in QH)
        upd = jnp.concatenate(dkraw + dvc, axis=1)
        offp = pl.multiple_of(jnp.maximum(n - 1, 0) * W, W)
        offc = pl.multiple_of(n * W, W)
        dkv_ref[pl.ds(offp, W), :] += upd[:W]
        dkv_ref[pl.ds(offc, W), :] += upd[W:]

    q, prev, cur, gain, perhead = _swa_specs(T)
    kvfull = pl.BlockSpec((T, 2 * SWA_KVW), lambda n: (0, 0))
    gs = jax.ShapeDtypeStruct((1, d), F32)
    return _pcall(
        body, name="swa_bwd", grid=(T // W,), in_specs=[q, prev, cur, gain, gain, perhead, perhead, q],
        out_specs=[q, kvfull, gain, gain, perhead],
        out_shape=[jax.ShapeDtypeStruct((T, SWA_QW), F32), jax.ShapeDtypeStruct((T, 2 * SWA_KVW), F32), gs, gs,
                   jax.ShapeDtypeStruct((SWA_HEADS, LANE), F32)],
        compiler_params=_params(("arbitrary",)),
    )(proj, proj, proj, qg, kg, sinks, slopes, do)


def _alibi():
    s = [2.0 ** (-8.0 * (i + 1) / SWA_HEADS) for i in range(SWA_HEADS)]
    return jnp.broadcast_to(jnp.asarray(s, F32)[:, None], (SWA_HEADS, LANE))


def _head_tiles(lo, hi):
    shp = lo.shape[:-1]
    return jnp.concatenate([lo.reshape(shp + (SB_HEADS, HEAD_DIM)), hi.reshape(shp + (SB_HEADS, HEAD_DIM))],
                           axis=-1).reshape(shp + (SB_LANES,))


def _tile_halves(x):
    shp = x.shape[:-1]
    t = x.reshape(shp + (SB_HEADS, 2, HEAD_DIM))
    return t[..., 0, :].reshape(shp + (SB_W,)), t[..., 1, :].reshape(shp + (SB_W,))


def _att_in_weights(w_in):
    sq, sk, sv = w_in[:, :SB_W], w_in[:, SB_W:2 * SB_W], w_in[:, 2 * SB_W:3 * SB_W]
    return jnp.concatenate([_head_tiles(sq, jnp.zeros_like(sq)), _head_tiles(sk, sv)], axis=1), w_in[:, 3 * SB_W:]


def _att_out_weights(w_out):
    wo = w_out[:SB_W]
    return _head_tiles(jnp.zeros_like(wo).T, wo.T).T, w_out[SB_W:]


def _att_fwd(h, g, w_in, w_out_of, q_gain, k_gain, sinks, ride=()):
    hn = _rms_fwd(h, g, "att_norm")
    w_sb, w_swa = _att_in_weights(w_in)
    proj_sb = _mm(hn, w_sb, "nn", out_dtype=BF16, name="att_in_sb")
    proj_swa = _mm(hn, w_swa, "nn", name="att_in_swa")
    a_out, carry, gathered = _sb_fwd(proj_sb, ride)
    w_out = w_out_of(gathered)
    wo_sb, wo_swa = _att_out_weights(w_out)
    sk128 = jnp.broadcast_to(sinks.reshape(SWA_HEADS, 1), (SWA_HEADS, LANE))
    qg, kg = q_gain.reshape(1, HEAD_DIM), k_gain.reshape(1, HEAD_DIM)
    b_out = _swa_fwd(proj_swa, qg, kg, sk128, _alibi())
    h2 = _mm(a_out, wo_sb, "nn", res=h, a2=b_out, b2=wo_swa, name="att_out")
    return h2, (h, hn, proj_sb, proj_swa, carry, a_out, b_out, sk128, qg, kg), gathered


def _att_bwd(dh2, saved, g, w_in, w_out, ride=()):
    h, hn, proj_sb, proj_swa, carry, a_out, b_out, sk128, qg, kg = saved
    w_sb, w_swa = _att_in_weights(w_in)
    wo_sb, wo_swa = _att_out_weights(w_out)
    da = _mm(dh2, wo_sb, "nt", out_dtype=BF16, name="att_do_sb")
    db = _mm(dh2, wo_swa, "nt", name="att_do_swa")
    dwo_sb = _mm(a_out, dh2, "tn", out_dtype=BF16, name="att_dwout_sb")
    dwo_swa = _mm(b_out, dh2, "tn", out_dtype=BF16, name="att_dwout_swa")
    dw_out = jnp.concatenate([_tile_halves(dwo_sb.T)[1].T, dwo_swa], axis=0)
    dq, dkv, rode = _sb_bwd(proj_sb, carry, da, ride)
    dbq, dbkv, dqg, dkg, dsink = _swa_bwd(proj_swa, qg, kg, sk128, _alibi(), db)
    dproj = jnp.concatenate([dq.astype(BF16), dkv.astype(BF16), dbq.astype(BF16), dbkv.astype(BF16)], axis=1)
    w_all = jnp.concatenate([w_sb, w_swa], axis=1)
    dw_all = _mm(hn, dproj, "tn", out_dtype=BF16, name="att_dwin")
    dhn = _mm(dproj, w_all, "nt", name="att_dhn")
    dsq, _ = _tile_halves(dw_all[:, :SB_LANES])
    dsk, dsv = _tile_halves(dw_all[:, SB_LANES:2 * SB_LANES])
    dw_in = jnp.concatenate([dsq, dsk, dsv, dw_all[:, 2 * SB_LANES:]], axis=1)
    dh, dg = _rms_bwd(dhn, h, g, dh2, "att_dnorm")
    return dh, dg, dw_in, dw_out, dqg.reshape(HEAD_DIM), dkg.reshape(HEAD_DIM), dsink[:, 0], rode


CONV_ROWS = 512
CONV_COLS = 512
HALO = 8


def _shifted(xcat, s, tm):
    if s == 0:
        return xcat[HALO:HALO + tm]
    return pltpu.roll(xcat, s, 0)[HALO:HALO + tm]


def _conv_pre(x_ref, halo_ref, w_ref, i, tm):
    xc = x_ref[...]
    halo = jnp.where(i > 0, halo_ref[...], 0.0)
    xcat = jnp.concatenate([halo, xc], axis=0)
    w = w_ref[...]
    y = w[GDN_CONV - 1:GDN_CONV] * xc
    for kk in range(GDN_CONV - 1):
        y = y + w[kk:kk + 1] * _shifted(xcat, GDN_CONV - 1 - kk, tm)
    return xcat, y


def _l2_heads(s, qscale_of):
    outs, rs = [], []
    for hh in range(s.shape[1] // GDN_HEAD_DIM):
        sh = s[:, hh * GDN_HEAD_DIM:(hh + 1) * GDN_HEAD_DIM]
        r = lax.rsqrt(jnp.sum(sh * sh, axis=-1, keepdims=True) + EPS)
        outs.append(sh * r)
        rs.append(r)
    return outs, rs


def _conv_specs(T, col0, tm, tc):
    cur = pl.BlockSpec((tm, tc), lambda j, i: (i, j + col0 // tc))
    halo = pl.BlockSpec((HALO, tc), lambda j, i: (jnp.maximum(i * (tm // HALO) - 1, 0), j + col0 // tc))
    wsp = pl.BlockSpec((GDN_CONV, tc), lambda j, i: (0, j + col0 // tc))
    out = pl.BlockSpec((tm, tc), lambda j, i: (i, j))
    return cur, halo, wsp, out


def _conv_fwd(proj, conv_w, col0, width, norm, name):
    T = proj.shape[0]
    tm, tc = _pick(T, CONV_ROWS), CONV_COLS
    cur, halo, wsp, out = _conv_specs(T, col0, tm, tc)
    n_q_tiles = (width // 2) // tc

    def body(x_ref, halo_ref, w_ref, o_ref):
        j, i = pl.program_id(0), pl.program_id(1)
        _, y = _conv_pre(x_ref, halo_ref, w_ref, i, tm)
        s = y * _sigmoid(y)
        if norm:
            outs, _ = _l2_heads(s, None)
            qs = jnp.where(j < n_q_tiles, GDN_HEAD_DIM ** -0.5, 1.0)
            o_ref[...] = jnp.concatenate(outs, axis=1) * qs
        else:
            o_ref[...] = s

    return _pcall(body, name=name, grid=(width // tc, T // tm), in_specs=[cur, halo, wsp], out_specs=out,
                  out_shape=jax.ShapeDtypeStruct((T, width), F32),
                  compiler_params=_params(("parallel", "parallel")))(proj, proj, conv_w)


def _conv_bwd_pre(proj, conv_w, dout, col0, width, norm, name):
    T = proj.shape[0]
    tm, tc = _pick(T, CONV_ROWS), CONV_COLS
    cur, halo, wsp, out = _conv_specs(T, col0, tm, tc)
    n_q_tiles = (width // 2) // tc

    def body(x_ref, halo_ref, w_ref, d_ref, dy_ref, dw_ref):
        j, i = pl.program_id(0), pl.program_id(1)
        xcat, y = _conv_pre(x_ref, halo_ref, w_ref, i, tm)
        sg = _sigmoid(y)
        s = y * sg
        d = d_ref[...]
        if norm:
            qs = jnp.where(j < n_q_tiles, GDN_HEAD_DIM ** -0.5, 1.0)
            d = d * qs
            outs, rs = _l2_heads(s, None)
            parts = []
            for hh, (nh, r) in enumerate(zip(outs, rs)):
                dh = d[:, hh * GDN_HEAD_DIM:(hh + 1) * GDN_HEAD_DIM]
                parts.append(r * (dh - nh * jnp.sum(dh * nh, axis=-1, keepdims=True)))
            ds = jnp.concatenate(parts, axis=1)
        else:
            ds = d
        dy = ds * sg * (1.0 + y * (1.0 - sg))
        dy_ref[...] = dy
        rows = [jnp.sum(dy * _shifted(xcat, GDN_CONV - 1 - kk, tm), axis=0, keepdims=True) for kk in range(GDN_CONV)]
        part = jnp.concatenate(rows, axis=0)

        @pl.when(i == 0)
        def _():
            dw_ref[...] = part

        @pl.when(i > 0)
        def _():
            dw_ref[...] += part

    wout = pl.BlockSpec((GDN_CONV, tc), lambda j, i: (0, j))
    return _pcall(body, name=name, grid=(width // tc, T // tm), in_specs=[cur, halo, wsp, out], out_specs=[out, wout],
                  out_shape=[jax.ShapeDtypeStruct((T, width), F32), jax.ShapeDtypeStruct((GDN_CONV, width), F32)],
                  compiler_params=_params(("parallel", "arbitrary")))(proj, proj, conv_w, dout)


def _conv_bwd_in(dy, conv_w, name):
    T, C = dy.shape
    tm, tc = _pick(T, CONV_ROWS), CONV_COLS
    nrow = T // tm

    def body(d_ref, nxt_ref, w_ref, dx_ref):
        i = pl.program_id(0)
        dc = d_ref[...]
        nxt = jnp.where(i < nrow - 1, nxt_ref[...], 0.0)
        dcat = jnp.concatenate([dc, nxt], axis=0)
        w = w_ref[...]
        dx = w[GDN_CONV - 1:GDN_CONV] * dc
        for kk in range(GDN_CONV - 1):
            s = GDN_CONV - 1 - kk
            dx = dx + w[kk:kk + 1] * pltpu.roll(dcat, tm + HALO - s, 0)[:tm]
        dx_ref[...] = dx.astype(BF16)

    cur = pl.BlockSpec((tm, tc), lambda i, j: (i, j))
    nxt = pl.BlockSpec((HALO, tc), lambda i, j: (jnp.minimum((i + 1) * (tm // HALO), T // HALO - 1), j))
    wsp = pl.BlockSpec((GDN_CONV, tc), lambda i, j: (0, j))
    return _pcall(body, name=name, grid=(nrow, C // tc), in_specs=[cur, nxt, wsp], out_specs=cur,
                  out_shape=jax.ShapeDtypeStruct((T, C), BF16),
                  compiler_params=_params(("parallel", "parallel")))(dy, dy, conv_w)


GATE_ROWS = 512


def _chunk_mask(n, lower):
    row = _iota2((n, n), 0)
    col = _iota2((n, n), 1)
    same = (row // GDN_CHUNK) == (col // GDN_CHUNK)
    tri = (row >= col) if lower else (row <= col)
    return (same & tri).astype(BF16)


def _gates_fwd(proj, a_log, dt_bias):
    T = proj.shape[0]
    tm = _pick(T, GATE_ROWS)
    c0 = (GDN_CONV_W + GDN_VW) // LANE

    def body(bl_ref, a_ref, alog_ref, dt_ref, beta_ref, g_ref, gc_ref):
        beta_ref[...] = _sigmoid(bl_ref[...])
        g = -jnp.exp(alog_ref[...]) * _softplus(a_ref[...] + dt_ref[...])
        g_ref[...] = g
        gc_ref[...] = _mdot2(_chunk_mask(tm, True), g)

    blk = lambda c: pl.BlockSpec((tm, LANE), lambda i: (i, c))
    vec = pl.BlockSpec((1, LANE), lambda i: (0, 0))
    sh = jax.ShapeDtypeStruct((T, LANE), F32)
    return _pcall(body, name="gdn_gates", grid=(T // tm,), in_specs=[blk(c0), blk(c0 + 1), vec, vec],
                  out_specs=[blk(0), blk(0), blk(0)], out_shape=[sh, sh, sh],
                  compiler_params=_params(("parallel",)))(proj, proj, a_log, dt_bias)


def _gates_bwd(proj, a_log, dt_bias, beta, g, dbeta, dgc):
    T = proj.shape[0]
    tm = _pick(T, GATE_ROWS)
    c0 = (GDN_CONV_W + GDN_VW) // LANE

    def heads_in_lanes(ref):
        lane = _iota2((tm, LANE), 1)
        out = jnp.where(lane < GDN_GROUP, ref[0], 0.0)
        for grp in range(1, GDN_V_HEADS // GDN_GROUP):
            out = out + jnp.where(lane // GDN_GROUP == grp, pltpu.roll(ref[grp], grp * GDN_GROUP, 1), 0.0)
        return out

    def body(a_ref, alog_ref, dt_ref, beta_ref, g_ref, dbeta_ref, dgc_ref, dbl_ref, da_ref, dalog_ref, ddt_ref):
        dg = _mdot2(_chunk_mask(tm, False), heads_in_lanes(dgc_ref))
        b = beta_ref[...]
        dbl_ref[...] = (heads_in_lanes(dbeta_ref) * b * (1.0 - b)).astype(BF16)
        da = dg * (-jnp.exp(alog_ref[...])) * _sigmoid(a_ref[...] + dt_ref[...])
        da_ref[...] = da.astype(BF16)
        p1 = jnp.sum(dg * g_ref[...], axis=0, keepdims=True)
        p2 = jnp.sum(da, axis=0, keepdims=True)

        @pl.when(pl.program_id(0) == 0)
        def _():
            dalog_ref[...] = p1
            ddt_ref[...] = p2

        @pl.when(pl.program_id(0) > 0)
        def _():
            dalog_ref[...] += p1
            ddt_ref[...] += p2

    blk = lambda c: pl.BlockSpec((tm, LANE), lambda i: (i, c))
    vec = pl.BlockSpec((1, LANE), lambda i: (0, 0))
    grp = pl.BlockSpec((GDN_V_HEADS // GDN_GROUP, tm, LANE), lambda i: (0, i, 0))
    shb = jax.ShapeDtypeStruct((T, LANE), BF16)
    shv = jax.ShapeDtypeStruct((1, LANE), F32)
    return _pcall(body, name="gdn_dgates", grid=(T // tm,),
                  in_specs=[blk(c0 + 1), vec, vec, blk(0), blk(0), grp, grp],
                  out_specs=[blk(0), blk(0), vec, vec], out_shape=[shb, shb, shv, shv],
                  compiler_params=_params(("arbitrary",)))(proj, a_log, dt_bias, beta, g, dbeta, dgc)


def _inv_unit_lower(Ls):
    C = Ls[0].shape[0]
    row = _iota2((C, C), 0)
    col = _iota2((C, C), 1)
    blk16 = (row // 16) == (col // 16)
    blk32 = (row // 32) == (col // 32)
    eye = (row == col).astype(F32)
    xs = [-jnp.where(blk16, L, 0.0) for L in Ls]
    inv = [eye + x for x in xs]
    for _ in range(3):
        xs = [_dot3(x, x) for x in xs]
        inv = [a + _dot3(a, x) for a, x in zip(inv, xs)]
    for mask in (blk32 & ~blk16, ~blk32):
        t = [_dot3(a, jnp.where(mask, L, 0.0)) for a, L in zip(inv, Ls)]
        inv = [a - _dot3(ti, a) for a, ti in zip(inv, t)]
    return inv


GDN_GROUP = 4
GDN_PREP_CHUNKS = 4


def _gdn_specs(T):
    C, D, E = GDN_CHUNK, GDN_HEAD_DIM, GDN_GROUP
    n = T // C
    qk = pl.BlockSpec((C, (E // 2) * D), lambda h, i: (i, h))
    vE = pl.BlockSpec((C, E * D), lambda h, i: (i, h))
    colv = pl.BlockSpec((C, LANE), lambda h, i: (i, 0))
    colo = pl.BlockSpec((None, C, LANE), lambda h, i: (h, i, 0))
    rowv = pl.BlockSpec((E, None, 1, C), lambda h, i: (h, i, 0, 0))
    st = pl.BlockSpec((E, None, D, D), lambda h, i: (h, i, 0, 0))
    am = pl.BlockSpec((E, None, C, C), lambda h, i: (h, i, 0, 0))
    return n, qk, vE, colv, colo, rowv, st, am


def _lane_col(blk, lane):
    return jnp.sum(jnp.where(_iota2(blk.shape, 1) == lane, blk, 0.0), axis=1, keepdims=True)


def _gdn_decay(gcol, grow):
    C = GDN_CHUNK
    row = _iota2((C, C), 0)
    col = _iota2((C, C), 1)
    incl = row >= col
    dm = jnp.where(incl, jnp.exp(jnp.where(incl, gcol - grow, 0.0)), 0.0)
    glast = grow[:, C - 1:C]
    return dm, jnp.exp(gcol), jnp.exp(glast), jnp.exp(glast - gcol), row > col, incl


def _gdn_prep(k, beta, gcol, grow):
    T = k.shape[0]
    C, D, B = GDN_CHUNK, GDN_HEAD_DIM, GDN_PREP_CHUNKS
    n = T // C

    def body(k_ref, b_ref, gc_ref, gr_ref, a_ref):
        idx = [(e, cb) for e in range(2) for cb in range(B)]
        kc = {cb: k_ref[cb * C:(cb + 1) * C, :] for cb in range(B)}
        lm = []
        head0 = 2 * pl.program_id(0)
        for e, cb in idx:
            beta = _lane_col(b_ref[cb * C:(cb + 1) * C, :], head0 + e)
            dm, _, _, _, strict, _ = _gdn_decay(_lane_col(gc_ref[cb * C:(cb + 1) * C, :], head0 + e), gr_ref[e, cb])
            lm.append(jnp.where(strict, _bdot(kc[cb] * beta, kc[cb], NT) * dm, 0.0))
        inv = _inv_unit_lower(lm)
        for (e, cb), a in zip(idx, inv):
            a_ref[e, cb] = a

    return _pcall(
        body, name="gdn_prep", grid=(GDN_K_HEADS, n // B),
        in_specs=[pl.BlockSpec((B * C, D), lambda h, i: (i, h)), pl.BlockSpec((B * C, LANE), lambda h, i: (i, 0)),
                  pl.BlockSpec((B * C, LANE), lambda h, i: (i, 0)), pl.BlockSpec((2, B, 1, C), lambda h, i: (h, i, 0, 0))],
        out_specs=pl.BlockSpec((2, B, C, C), lambda h, i: (h, i, 0, 0)),
        out_shape=jax.ShapeDtypeStruct((GDN_V_HEADS, n, C, C), F32),
        compiler_params=_params(("parallel", "parallel")),
    )(k, beta, gcol, grow)


def _gdn_fwd(q, k, v, beta, gcol, grow, amat):
    T = q.shape[0]
    C, D, E = GDN_CHUNK, GDN_HEAD_DIM, GDN_GROUP
    n, qk, vE, colv, colo, rowv, st, am = _gdn_specs(T)
    R = range(E)

    def body(q_ref, k_ref, v_ref, b_ref, gc_ref, gr_ref, a_ref, o_ref, s_ref, vn_ref, state):
        @pl.when(pl.program_id(1) == 0)
        def _():
            state[...] = jnp.zeros_like(state)

        qv = [q_ref[:, (e // 2) * D:(e // 2 + 1) * D] for e in R]
        kv = [k_ref[:, (e // 2) * D:(e // 2 + 1) * D] for e in R]
        vv = [v_ref[:, e * D:(e + 1) * D] for e in R]
        head0 = E * pl.program_id(0)
        beta = [_lane_col(b_ref[...], head0 + e) for e in R]
        a = [a_ref[e] for e in R]
        s = [state[e] for e in R]
        dec = [_gdn_decay(_lane_col(gc_ref[...], head0 + e), gr_ref[e]) for e in R]
        pm = [_bdot(qv[e], kv[e], NT) * dec[e][0] for e in R]
        r = [beta[e] * (vv[e] - _bdot(kv[e] * dec[e][1], s[e])) for e in R]
        vn = [_dot3(a[e], r[e]) for e in R]
        o = [_bdot(qv[e] * dec[e][1], s[e]) + _bdot(pm[e], vn[e]) for e in R]
        s2 = [dec[e][2] * s[e] + _bdot(kv[e] * dec[e][3], vn[e], TN) for e in R]
        for e in R:
            s_ref[e] = s[e]
            vn_ref[:, e * D:(e + 1) * D] = vn[e]
            o_ref[:, e * D:(e + 1) * D] = o[e]
            state[e] = s2[e]

    shv = jax.ShapeDtypeStruct((T, GDN_V_HEADS * D), F32)
    return _pcall(
        body, name="gdn_fwd", grid=(GDN_V_HEADS // E, n), in_specs=[qk, qk, vE, colv, colv, rowv, am],
        out_specs=[vE, st, vE],
        out_shape=[shv, jax.ShapeDtypeStruct((GDN_V_HEADS, n, D, D), F32), shv],
        scratch_shapes=[pltpu.VMEM((E, D, D), F32)],
        compiler_params=_params(("parallel", "arbitrary")),
    )(q, k, v, beta, gcol, grow, amat)


def _gdn_bwd(q, k, v, beta, gcol, grow, states, amat, vnew, do):
    T = q.shape[0]
    C, D, E = GDN_CHUNK, GDN_HEAD_DIM, GDN_GROUP
    n, qk, vE, colv, colo, rowv, st, am = _gdn_specs(T)
    rev = lambda spec: pl.BlockSpec(spec.block_shape, (lambda f: (lambda h, i: f(h, n - 1 - i)))(spec.index_map))
    qk, vE, colv, colo, rowv, st, am = (rev(s) for s in (qk, vE, colv, colo, rowv, st, am))
    R = range(E)

    def body(q_ref, k_ref, v_ref, b_ref, gc_ref, gr_ref, s_ref, a_ref, vn_ref, do_ref,
             dq_ref, dk_ref, dv_ref, db_ref, dgc_ref, dstate):
        @pl.when(pl.program_id(1) == 0)
        def _():
            dstate[...] = jnp.zeros_like(dstate)

        M = lambda f: [f(e) for e in R]
        rsum = lambda x: jnp.sum(x, axis=1, keepdims=True)
        qv = M(lambda e: q_ref[:, (e // 2) * D:(e // 2 + 1) * D])
        kv = M(lambda e: k_ref[:, (e // 2) * D:(e // 2 + 1) * D])
        vv = M(lambda e: v_ref[:, e * D:(e + 1) * D])
        vn = M(lambda e: vn_ref[:, e * D:(e + 1) * D])
        dov = M(lambda e: do_ref[:, e * D:(e + 1) * D])
        head0 = E * pl.program_id(0)
        beta = M(lambda e: _lane_col(b_ref[...], head0 + e))
        s = M(lambda e: s_ref[e])
        a = M(lambda e: a_ref[e])
        dsn = M(lambda e: dstate[e])
        dec = M(lambda e: _gdn_decay(_lane_col(gc_ref[...], head0 + e), gr_ref[e]))
        dm, gam, glast, tail = (M(lambda e: dec[e][i]) for i in range(4))
        strict, incl = dec[0][4], dec[0][5]
        kb = M(lambda e: kv[e] * beta[e])
        kd = M(lambda e: kv[e] * gam[e])
        qd = M(lambda e: qv[e] * gam[e])
        kt = M(lambda e: kv[e] * tail[e])
        lmat = M(lambda e: jnp.where(strict, _bdot(kb[e], kv[e], NT) * dm[e], 0.0))
        pmat = M(lambda e: _bdot(qv[e], kv[e], NT) * dm[e])
        xres = M(lambda e: vv[e] - _bdot(kd[e], s[e]))
        dvn = M(lambda e: _bdot(pmat[e], dov[e], TN) + _bdot(kt[e], dsn[e]))
        dqd = M(lambda e: _bdot(dov[e], s[e], NT))
        dp = M(lambda e: jnp.where(incl, _bdot(dov[e], vn[e], NT), 0.0))
        dkt = M(lambda e: _bdot(vn[e], dsn[e], NT))
        dr = M(lambda e: _dot3(a[e], dvn[e], TN))
        drb = M(lambda e: beta[e] * dr[e])
        dkd = M(lambda e: -_bdot(drb[e], s[e], NT))
        ds2 = M(lambda e: _bdot(qd[e], dov[e], TN) + glast[e] * dsn[e] - _bdot(kd[e], drb[e], TN))
        dl = M(lambda e: -jnp.where(strict, _bdot(dr[e], vn[e], NT), 0.0))
        dmm = M(lambda e: dl[e] * dm[e])
        dnn = M(lambda e: dp[e] * dm[e])
        emat = M(lambda e: dl[e] * lmat[e] + dp[e] * pmat[e])
        dkb = M(lambda e: _bdot(dmm[e], kv[e]))
        dk = M(lambda e: beta[e] * dkb[e] + _bdot(dmm[e], kb[e], TN) + _bdot(dnn[e], qv[e], TN)
               + gam[e] * dkd[e] + tail[e] * dkt[e])
        dq = M(lambda e: _bdot(dnn[e], kv[e]) + gam[e] * dqd[e])
        dbeta = M(lambda e: rsum(dr[e] * xres[e]) + rsum(dkb[e] * kv[e]))
        ones = jnp.ones((C, LANE), BF16)
        colsum = M(lambda e: _dot2m(emat[e], ones, TN)[:, :1])
        tails = M(lambda e: rsum(dkt[e] * kt[e]))
        lastrow = _iota2((C, 1), 0) == C - 1
        dlast = M(lambda e: jnp.sum(tails[e], axis=0, keepdims=True)
                  + glast[e] * jnp.sum(rsum(s[e] * dsn[e]), axis=0, keepdims=True))
        dgc = M(lambda e: rsum(emat[e]) - colsum[e] + rsum(dkd[e] * kd[e]) + rsum(dqd[e] * qd[e]) - tails[e]
                + jnp.where(lastrow, dlast[e], 0.0))
        lane = _iota2((C, LANE), 1)
        db_all = jnp.zeros((C, LANE), F32)
        dgc_all = jnp.zeros((C, LANE), F32)
        for e in R:
            dv_ref[:, e * D:(e + 1) * D] = drb[e]
            db_all = jnp.where(lane == e, dbeta[e], db_all)
            dgc_all = jnp.where(lane == e, dgc[e], dgc_all)
            dstate[e] = ds2[e]
        db_ref[...] = db_all
        dgc_ref[...] = dgc_all
        for kh in range(E // 2):
            dq_ref[:, kh * D:(kh + 1) * D] = dq[2 * kh] + dq[2 * kh + 1]
            dk_ref[:, kh * D:(kh + 1) * D] = dk[2 * kh] + dk[2 * kh + 1]

    shq = jax.ShapeDtypeStruct((T, GDN_K_HEADS * D), F32)
    shv = jax.ShapeDtypeStruct((T, GDN_V_HEADS * D), F32)
    shc = jax.ShapeDtypeStruct((GDN_V_HEADS // E, T, LANE), F32)
    return _pcall(
        body, name="gdn_bwd", grid=(GDN_V_HEADS // E, n),
        in_specs=[qk, qk, vE, colv, colv, rowv, st, am, vE, vE],
        out_specs=[qk, qk, vE, colo, colo], out_shape=[shq, shq, shv, shc, shc],
        scratch_shapes=[pltpu.VMEM((E, D, D), F32)],
        compiler_params=_params(("parallel", "arbitrary")),
    )(q, k, v, beta, gcol, grow, states, amat, vnew, do)


def _outgate_fwd(o, proj, gain):
    T = o.shape[0]
    tm, tc = _pick(T, CONV_ROWS), CONV_COLS
    z0 = GDN_CONV_W // tc

    def body(o_ref, z_ref, g_ref, y_ref):
        z = z_ref[...]
        sz = z * _sigmoid(z)
        parts = []
        for hh in range(tc // GDN_HEAD_DIM):
            oh = o_ref[:, hh * GDN_HEAD_DIM:(hh + 1) * GDN_HEAD_DIM]
            r = lax.rsqrt(jnp.mean(oh * oh, axis=-1, keepdims=True) + EPS)
            parts.append(oh * r * g_ref[...])
        y_ref[...] = (jnp.concatenate(parts, axis=1) * sz).astype(BF16)

    blk = pl.BlockSpec((tm, tc), lambda i, j: (i, j))
    return _pcall(body, name="gdn_outgate", grid=(T // tm, GDN_VW // tc),
                  in_specs=[blk, pl.BlockSpec((tm, tc), lambda i, j: (i, j + z0)), pl.BlockSpec((1, GDN_HEAD_DIM), lambda i, j: (0, 0))],
                  out_specs=blk, out_shape=jax.ShapeDtypeStruct((T, GDN_VW), BF16),
                  compiler_params=_params(("parallel", "parallel")))(o, proj, gain)


def _outgate_bwd(dy, o, proj, gain):
    T = o.shape[0]
    tm, tc = _pick(T, CONV_ROWS), CONV_COLS
    z0 = GDN_CONV_W // tc
    nh = tc // GDN_HEAD_DIM

    def body(dy_ref, o_ref, z_ref, g_ref, do_ref, dz_ref, dg_ref):
        z = z_ref[...]
        sg = _sigmoid(z)
        sz = z * sg
        dy = dy_ref[...]
        dgain = jnp.zeros((1, GDN_HEAD_DIM), F32)
        dos, ys = [], []
        for hh in range(nh):
            sl = slice(hh * GDN_HEAD_DIM, (hh + 1) * GDN_HEAD_DIM)
            oh = o_ref[:, sl]
            r = lax.rsqrt(jnp.mean(oh * oh, axis=-1, keepdims=True) + EPS)
            xh = oh * r
            dn = dy[:, sl] * sz[:, sl]
            dgain = dgain + jnp.sum(dn * xh, axis=0, keepdims=True)
            dxh = dn * g_ref[...]
            dos.append(r * (dxh - xh * jnp.mean(dxh * xh, axis=-1, keepdims=True)))
            ys.append(xh * g_ref[...])
        do_ref[...] = jnp.concatenate(dos, axis=1)
        dz_ref[...] = (dy * jnp.concatenate(ys, axis=1) * sg * (1.0 + z * (1.0 - sg))).astype(BF16)
        first = (pl.program_id(0) == 0) & (pl.program_id(1) == 0)

        @pl.when(first)
        def _():
            dg_ref[...] = dgain

        @pl.when(jnp.logical_not(first))
        def _():
            dg_ref[...] += dgain

    blk = pl.BlockSpec((tm, tc), lambda i, j: (i, j))
    vec = pl.BlockSpec((1, GDN_HEAD_DIM), lambda i, j: (0, 0))
    return _pcall(body, name="gdn_doutgate", grid=(T // tm, GDN_VW // tc),
                  in_specs=[blk, blk, pl.BlockSpec((tm, tc), lambda i, j: (i, j + z0)), vec],
                  out_specs=[blk, blk, vec],
                  out_shape=[jax.ShapeDtypeStruct((T, GDN_VW), F32), jax.ShapeDtypeStruct((T, GDN_VW), BF16),
                             jax.ShapeDtypeStruct((1, GDN_HEAD_DIM), F32)],
                  compiler_params=_params(("arbitrary", "arbitrary")))(dy, o, proj, gain)


def _pad_lanes(vec):
    return jnp.pad(vec.reshape(1, -1), ((0, 0), (0, LANE - vec.shape[-1])))


def _head_rows(a):
    T = a.shape[0]
    return a[:, :GDN_V_HEADS].T.reshape(GDN_V_HEADS, T // GDN_CHUNK, 1, GDN_CHUNK)


def _gdn_pad_in(w_in):
    c = GDN_CONV_W + GDN_VW
    z = jnp.zeros(w_in.shape[:-1] + (LANE - GDN_V_HEADS,), w_in.dtype)
    return jnp.concatenate([w_in[..., :c + GDN_V_HEADS], z, w_in[..., c + GDN_V_HEADS:], z], axis=-1)


def _gdn_unpad_in(dw):
    c = GDN_CONV_W + GDN_VW
    return jnp.concatenate([dw[..., :c + GDN_V_HEADS], dw[..., c + LANE:c + LANE + GDN_V_HEADS]], axis=-1)


def _gdn_mixer_fwd(h, g, w_in_pad, conv_w, a_log, dt_bias, out_gain, w_out):
    T = h.shape[0]
    hn = _rms_fwd(h, g, "gdn_norm")
    proj = _mm(hn, w_in_pad, "nn", name="gdn_in")
    qk = _conv_fwd(proj, conv_w, 0, 2 * GDN_KW, True, "gdn_conv_qk")
    vv = _conv_fwd(proj, conv_w, 2 * GDN_KW, GDN_VW, False, "gdn_conv_v")
    alog, dtb = _pad_lanes(a_log), _pad_lanes(dt_bias)
    beta, gl, gc = _gates_fwd(proj, alog, dtb)
    grow = _head_rows(gc)
    qn, kn = qk[:, :GDN_KW], qk[:, GDN_KW:]
    amat = _gdn_prep(kn, beta, gc, grow)
    o, states, vnew = _gdn_fwd(qn, kn, vv, beta, gc, grow, amat)
    gain = out_gain.reshape(1, GDN_HEAD_DIM)
    y = _outgate_fwd(o, proj, gain)
    h2 = _mm(y, w_out, "nn", res=h, name="gdn_out")
    return h2, (h, hn, proj, qn, kn, vv, beta, gl, gc, grow, o, states, amat, vnew, y, alog, dtb, gain)


def _gdn_mixer_bwd(dh2, saved, g, w_in_pad, conv_w, w_out):
    h, hn, proj, qn, kn, vv, beta, gl, gc, grow, o, states, amat, vnew, y, alog, dtb, gain = saved
    T = h.shape[0]
    dy = _mm(dh2, w_out, "nt", name="gdn_dy")
    dw_out = _mm(y, dh2, "tn", out_dtype=BF16, name="gdn_dwout")
    do, dz, dgain = _outgate_bwd(dy, o, proj, gain)
    dq, dk, dv, dbeta, dgc = _gdn_bwd(qn, kn, vv, beta, gc, grow, states, amat, vnew, do)
    dqk = jnp.concatenate([dq, dk], axis=1)
    dy_qk, dcw_qk = _conv_bwd_pre(proj, conv_w, dqk, 0, 2 * GDN_KW, True, "gdn_dconv_qk")
    dy_v, dcw_v = _conv_bwd_pre(proj, conv_w, dv, 2 * GDN_KW, GDN_VW, False, "gdn_dconv_v")
    dx_qk = _conv_bwd_in(dy_qk, conv_w[:, :2 * GDN_KW], "gdn_dconvin_qk")
    dx_v = _conv_bwd_in(dy_v, conv_w[:, 2 * GDN_KW:], "gdn_dconvin_v")
    dbl, da, dalog, ddt = _gates_bwd(proj, alog, dtb, beta, gl, dbeta, dgc)
    dproj = jnp.concatenate([dx_qk, dx_v, dz, dbl, da], axis=1)
    dw_in_pad = _mm(hn, dproj, "tn", out_dtype=BF16, name="gdn_dwin")
    dhn = _mm(dproj, w_in_pad, "nt", name="gdn_dhn")
    dh, dg = _rms_bwd(dhn, h, g, dh2, "gdn_dnorm")
    dconv = jnp.concatenate([dcw_qk, dcw_v], axis=1)
    return (dh, dg, _gdn_unpad_in(dw_in_pad), dconv, dalog[0, :GDN_V_HEADS], ddt[0, :GDN_V_HEADS],
            dgain.reshape(GDN_HEAD_DIM), dw_out)


def _instances(full):
    out = {}
    for n, a in full.items():
        if n.startswith("ffn_"):
            for i in range(2):
                for j in range(2):
                    out[(n, i, j)] = a[i, j]
        elif n in ("mix_norm", "ple_norm", "ple_w_gate", "ple_w_proj"):
            for i in range(2):
                out[(n, i)] = a[i]
        else:
            out[(n,)] = a[0]
    return out


def _stacked(inst):
    out = {}
    for n in dict.fromkeys(k[0] for k in inst):
        if n.startswith("ffn_"):
            out[n] = jnp.stack([jnp.stack([inst[(n, i, j)] for j in range(2)]) for i in range(2)])
        elif n in ("mix_norm", "ple_norm", "ple_w_gate", "ple_w_proj"):
            out[n] = jnp.stack([inst[(n, i)] for i in range(2)])
        else:
            out[n] = inst[(n,)][None]
    return out


def _local_step(x, p, target, w, late_shards=(), late_weights=None, early_grads=None):
    w = dict(w)
    ffn = lambda i, j: (w[("ffn_norm", i, j)], w[("ffn_w_gate", i, j)], w[("ffn_w_up", i, j)], w[("ffn_w_down", i, j)])
    h = x
    tape = []
    for i in range(2):
        h, s1 = _ffn_fwd(h, *ffn(i, 0), f"ffn{i}a")
        if i == 0:
            def w_out_of(gathered):
                if late_weights is not None:
                    w.update(late_weights(gathered))
                return w[("att_w_out",)]
            h, s2, _ = _att_fwd(h, w[("mix_norm", 0)], w[("att_w_in",)], w_out_of, w[("att_q_norm",)],
                                w[("att_k_norm",)], w[("att_sinks",)], late_shards)
        else:
            gdn_in_pad = _gdn_pad_in(w[("gdn_w_in",)])
            h, s2 = _gdn_mixer_fwd(h, w[("mix_norm", 1)], gdn_in_pad, w[("gdn_conv_w",)], w[("gdn_a_log",)],
                                   w[("gdn_dt_bias",)], w[("gdn_out_norm",)], w[("gdn_w_out",)])
        h, s3 = _ffn_fwd(h, *ffn(i, 1), f"ffn{i}b")
        h, s4 = _ple_fwd(h, p[i], w[("ple_norm", i)], w[("ple_w_gate", i)], w[("ple_w_proj", i)], f"ple{i}")
        tape.append((s1, s2, s3, s4))

    loss, dh = _loss_head(h, target)

    g = {}
    rode = []
    for i in (1, 0):
        s1, s2, s3, s4 = tape[i]
        dh, g[("ple_norm", i)], g[("ple_w_gate", i)], g[("ple_w_proj", i)] = _ple_bwd(
            dh, s4, p[i], w[("ple_norm", i)], w[("ple_w_gate", i)], f"ple{i}")
        dh, g[("ffn_norm", i, 1)], g[("ffn_w_gate", i, 1)], g[("ffn_w_up", i, 1)], g[("ffn_w_down", i, 1)] = _ffn_bwd(
            dh, s3, *ffn(i, 1), f"ffn{i}b")
        if i == 0:
            ride = early_grads(g) if early_grads is not None else ()
            (dh, g[("mix_norm", 0)], g[("att_w_in",)], g[("att_w_out",)], g[("att_q_norm",)], g[("att_k_norm",)],
             g[("att_sinks",)], rode) = _att_bwd(dh, s2, w[("mix_norm", 0)], w[("att_w_in",)], w[("att_w_out",)], ride)
        else:
            (dh, g[("mix_norm", 1)], g[("gdn_w_in",)], g[("gdn_conv_w",)], g[("gdn_a_log",)], g[("gdn_dt_bias",)],
             g[("gdn_out_norm",)], g[("gdn_w_out",)]) = _gdn_mixer_bwd(
                dh, s2, w[("mix_norm", 1)], gdn_in_pad, w[("gdn_conv_w",)], w[("gdn_w_out",)])
        dh, g[("ffn_norm", i, 0)], g[("ffn_w_gate", i, 0)], g[("ffn_w_up", i, 0)], g[("ffn_w_down", i, 0)] = _ffn_bwd(
            dh, s1, *ffn(i, 0), f"ffn{i}a")
    return loss, dh, g, rode


MESH = pl.DeviceIdType.MESH
N_CHIP = 4


def _place():
    x, y, c = lax.axis_index("x"), lax.axis_index("y"), lax.axis_index("c")
    others = [((1 - x, y), 2 * (1 - x) + y), ((x, 1 - y), 2 * x + (1 - y)), ((1 - x, 1 - y), 2 * (1 - x) + (1 - y))]
    return x, y, c, 4 * x + 2 * y + c, 2 * x + y, (x, y, 1 - c), others


def _comm_call(body, arrays, out_shape, n_sems, name):
    hbm = pl.BlockSpec(memory_space=pl.ANY)
    n = len(arrays)
    return _pcall(
        body, name=name, in_specs=[hbm] * n, out_specs=[hbm] * len(out_shape), out_shape=out_shape,
        scratch_shapes=[pltpu.SemaphoreType.DMA((n, n_sems)), pltpu.SemaphoreType.DMA((n, n_sems)),
                        pltpu.SemaphoreType.DMA((n, N_CHIP))],
        compiler_params=pltpu.CompilerParams(has_side_effects=True),
    )(*arrays)


def _gather_protocol(ins, outs, send_sems, recv_sems, local_sems):
    n = len(ins)
    x, y, c, me, my_chip, sibling, others = _place()

    def copy(a, k, block, to, src=None):
        dst = outs[a].at[block]
        return pltpu.make_async_remote_copy(
            src_ref=dst if src is None else src, dst_ref=dst, send_sem=send_sems.at[a, k],
            recv_sem=recv_sems.at[a, k], device_id=to, device_id_type=MESH)

    local = [pltpu.make_async_copy(ins[a], outs[a].at[me], local_sems.at[a, 0]) for a in range(n)]
    first = []
    for a in range(n):
        first.append(copy(a, 0, me, sibling, src=ins[a]))
        first += [copy(a, 1 + j, me, (*chip, c), src=ins[a]) for j, (chip, _) in enumerate(others)]

    def start():
        for cp in local + first:
            cp.start()

    def finish():
        passed = []
        for a in range(n):
            for j, (chip, chip_idx) in enumerate(others):
                blk = 2 * chip_idx + c
                copy(a, 1 + j, blk, (x, y, c)).wait_recv()
                fwd = copy(a, 4 + j, blk, sibling)
                fwd.start()
                passed.append(fwd)
        for a in range(n):
            copy(a, 0, 2 * my_chip + (1 - c), (x, y, c)).wait_recv()
            for j, (chip, chip_idx) in enumerate(others):
                copy(a, 4 + j, 2 * chip_idx + (1 - c), (x, y, c)).wait_recv()
        for cp in first + passed:
            cp.wait_send()
        for cp in local:
            cp.wait()

    return start, finish


def _all_gather(arrays):
    n = len(arrays)

    def body(*refs):
        start, finish = _gather_protocol(refs[:n], refs[n:2 * n], *refs[2 * n:])
        start()
        finish()

    out_shape = [jax.ShapeDtypeStruct((N_DEV,) + a.shape, a.dtype) for a in arrays]
    return _comm_call(body, arrays, out_shape, N_DEV - 1, "gather_weights")


def _exchange_sibling(arrays, name):
    n = len(arrays)

    def body(*refs):
        ins, got = refs[:n], refs[n:2 * n]
        send_sems, recv_sems, _ = refs[2 * n:]
        x, y, c, me, my_chip, sibling, others = _place()
        remote = []
        for a in range(n):
            for chip in range(N_CHIP):
                rc = pltpu.make_async_remote_copy(
                    src_ref=ins[a].at[2 * chip + (1 - c)], dst_ref=got[a].at[chip], send_sem=send_sems.at[a, chip],
                    recv_sem=recv_sems.at[a, chip], device_id=sibling, device_id_type=MESH)
                rc.start()
                remote.append(rc)
        for rc in remote:
            rc.wait()

    half = [jax.ShapeDtypeStruct((N_CHIP,) + a.shape[1:], a.dtype) for a in arrays]
    return _comm_call(body, arrays, half, N_CHIP, name)


def _chips_protocol(ins, outs, send_sems, recv_sems, local_sems):
    n = len(ins)
    x, y, c, me, my_chip, sibling, others = _place()
    local = [pltpu.make_async_copy(ins[a].at[my_chip], outs[a].at[my_chip], local_sems.at[a, 0]) for a in range(n)]
    remote = [pltpu.make_async_remote_copy(
        src_ref=ins[a].at[chip_idx], dst_ref=outs[a].at[my_chip], send_sem=send_sems.at[a, j],
        recv_sem=recv_sems.at[a, j], device_id=(*chip, c), device_id_type=MESH)
        for a in range(n) for j, (chip, chip_idx) in enumerate(others)]

    def start():
        for cp in local + remote:
            cp.start()

    def finish():
        for cp in remote + local:
            cp.wait()

    return start, finish


def _exchange_chips(arrays, name):
    n = len(arrays)

    def body(*refs):
        start, finish = _chips_protocol(refs[:n], refs[n:2 * n], *refs[2 * n:])
        start()
        finish()

    out_shape = [jax.ShapeDtypeStruct(a.shape, a.dtype) for a in arrays]
    return _comm_call(body, arrays, out_shape, N_CHIP - 1, name)


def _as_rows(a, lead):
    shp = a.shape
    return a.reshape(shp[:lead] + (math.prod(shp[lead:-1]), shp[-1]))


def _row_tile(rows, cap=512):
    if rows <= cap:
        return rows
    for t in range(cap - cap % 8, 0, -8):
        if rows % t == 0:
            return t
    return rows


def _pair_sum(send, got, name):
    a3, b3 = _as_rows(send, 1), _as_rows(got, 1)
    _, rows, last = b3.shape
    tr = _row_tile(rows)

    def body(c_ref, a_ref, b_ref, o_ref):
        o_ref[...] = (a_ref[...].astype(F32) + b_ref[...].astype(F32)).astype(o_ref.dtype)

    core = lax.axis_index("c").astype(jnp.int32).reshape(1)
    out = _pcall(
        body, name=name,
        grid_spec=pltpu.PrefetchScalarGridSpec(
            num_scalar_prefetch=1, grid=(N_CHIP, rows // tr),
            in_specs=[pl.BlockSpec((None, tr, last), lambda k, i, c_ref: (2 * k + c_ref[0], i, 0)),
                      pl.BlockSpec((None, tr, last), lambda k, i, c_ref: (k, i, 0))],
            out_specs=pl.BlockSpec((None, tr, last), lambda k, i, c_ref: (k, i, 0))),
        out_shape=jax.ShapeDtypeStruct(b3.shape, got.dtype), compiler_params=_params(("parallel", "parallel")),
    )(core, a3, b3)
    return out.reshape(got.shape)


def _adamw(parts, w, m, v, name):
    lead, (rows, last) = w.shape[:-2], w.shape[-2:]
    nl = len(lead)
    tr = _row_tile(rows)
    c1 = 1.0 / (1.0 - ADAM_B1 ** ADAM_STEP)
    c2 = 1.0 / (1.0 - ADAM_B2 ** ADAM_STEP)

    def body(p_ref, w_ref, m_ref, v_ref, g_ref, d_ref, nm_ref, nv_ref):
        g = p_ref[0].astype(F32)
        for chip in range(1, N_CHIP):
            g = g + p_ref[chip].astype(F32)
        mn = ADAM_B1 * m_ref[...] + (1.0 - ADAM_B1) * g
        vn = ADAM_B2 * v_ref[...] + (1.0 - ADAM_B2) * (g * g)
        g_ref[...] = g
        nm_ref[...] = mn
        nv_ref[...] = vn
        d_ref[...] = -ADAM_LR * ((mn * c1) / (jnp.sqrt(vn * c2) + ADAM_EPS) + ADAM_WD * w_ref[...])

    row = pl.BlockSpec((None,) * nl + (tr, last), lambda *ix: ix + (0,))
    part = pl.BlockSpec((N_CHIP,) + (None,) * nl + (tr, last), lambda *ix: (0,) + ix + (0,))
    sh = jax.ShapeDtypeStruct(w.shape, F32)
    return _pcall(body, name=name, grid=lead + (rows // tr,), in_specs=[part, row, row, row],
                  out_specs=[row, row, row, row], out_shape=[sh, sh, sh, sh],
                  compiler_params=_params(("parallel",) * (nl + 1)))(parts, w, m, v)


def _pack(pieces, row_align):
    rows, offs, r = [], [], 0
    for a in pieces:
        flat = a.reshape(-1)
        nr = -(-flat.shape[0] // PACK_W)
        flat = jnp.pad(flat, (0, nr * PACK_W - flat.shape[0]))
        rows.append(flat.reshape(nr, PACK_W))
        offs.append(r)
        r += nr
    pad = (-r) % row_align
    if pad:
        rows.append(jnp.zeros((pad, PACK_W), pieces[0].dtype))
    return jnp.concatenate(rows, axis=0), offs


def _unpack(flat, offs, shapes):
    out = []
    for off, shp in zip(offs, shapes):
        size = math.prod(shp)
        nr = -(-size // PACK_W)
        out.append(flat[..., off:off + nr, :].reshape(flat.shape[:-2] + (nr * PACK_W,))[..., :size].reshape(flat.shape[:-2] + tuple(shp)))
    return out


def _to_full(gathered, axis):
    z = jnp.moveaxis(gathered, 0, axis)
    shp = list(z.shape)
    return z.reshape(shp[:axis] + [shp[axis] * shp[axis + 1]] + shp[axis + 2:])


def _to_shards(full, axis):
    shp = list(full.shape)
    z = full.reshape(shp[:axis] + [N_DEV, shp[axis] // N_DEV] + shp[axis + 1:])
    return jnp.moveaxis(z, axis, 0)


def kernel(x, p, ffn_norm, ffn_w_gate, ffn_w_up, ffn_w_down, mix_norm, att_w_in, att_q_norm, att_k_norm, att_sinks, att_w_out, gdn_w_in, gdn_conv_w, gdn_a_log, gdn_dt_bias, gdn_out_norm, gdn_w_out, ple_norm, ple_w_gate, ple_w_proj, loss_target, m_ffn_norm, m_ffn_w_gate, m_ffn_w_up, m_ffn_w_down, m_mix_norm, m_att_w_in, m_att_q_norm, m_att_k_norm, m_att_sinks, m_att_w_out, m_gdn_w_in, m_gdn_conv_w, m_gdn_a_log, m_gdn_dt_bias, m_gdn_out_norm, m_gdn_w_out, m_ple_norm, m_ple_w_gate, m_ple_w_proj, v_ffn_norm, v_ffn_w_gate, v_ffn_w_up, v_ffn_w_down, v_mix_norm, v_att_w_in, v_att_q_norm, v_att_k_norm, v_att_sinks, v_att_w_out, v_gdn_w_in, v_gdn_conv_w, v_gdn_a_log, v_gdn_dt_bias, v_gdn_out_norm, v_gdn_w_out, v_ple_norm, v_ple_w_gate, v_ple_w_proj):
    args = dict(locals())
    wts = {n: args[n] for n in WEIGHTS}
    mom = {n: args["m_" + n] for n in WEIGHTS}
    var = {n: args["v_" + n] for n in WEIGHTS}
    axis = dict(SHARDED)
    vecs = [n for n, _ in SHARDED[:SMALL_SHARDED]]
    small = vecs + list(REPLICATED)
    small_shapes = [wts[n].shape for n in small]
    lead = lambda n: 2 if n.startswith("ffn_") else 1

    def stack_of(arrays, name, idxs):
        return jnp.stack([arrays[name][idx] if idx else arrays[name][0] for idx in idxs])

    def full_instances(gathered, group):
        out = {}
        for (name, idxs), g in zip(group, gathered):
            whole = _to_full(g, axis[name] - lead(name) + 1)
            for k, idx in enumerate(idxs):
                out[(name,) + idx] = whole[k]
        return out

    def shard_stacks(g, group):
        return [_to_shards(jnp.stack([g[(name,) + idx] for idx in idxs]), axis[name] - lead(name) + 1)
                for name, idxs in group]

    vec_pack, voffs = _pack([wts[n] for n in vecs], 8)
    early = _all_gather([stack_of(wts, n, idxs).astype(BF16) for n, idxs in EARLY] + [vec_pack])
    w = full_instances(early[:-1], EARLY)
    vec_full = {n: _to_full(piece, axis[n]) for n, piece in
                zip(vecs, _unpack(early[-1], voffs, [wts[n].shape for n in vecs]))}
    w.update(_instances({**vec_full, **{n: wts[n] for n in REPLICATED}}))
    late_shards = [stack_of(wts, n, idxs).astype(BF16) for n, idxs in LATE]

    def early_grads(g):
        send = shard_stacks(g, RIDE)
        got = _exchange_sibling(send, "exchange_sibling_early")
        return [_pair_sum(p_, q_, f"pair_sum_early_{i}") for i, (p_, q_) in enumerate(zip(send, got))]

    loss, grad_x, g, rode = _local_step(x[0], p[:, 0], loss_target[0], w, late_shards,
                                        lambda gathered: full_instances(gathered, LATE), early_grads)

    gs = _stacked({k: v for k, v in g.items() if k[0] in small})
    vec_shards = [_to_shards(gs[n], axis[n]) for n in vecs]
    small_send = jnp.stack([_pack([sh[d] for sh in vec_shards] + [gs[n] for n in REPLICATED] + [loss.reshape(1)], 8)[0]
                            for d in range(N_DEV)])
    send = shard_stacks(g, FINAL) + [small_send]
    got = _exchange_sibling(send, "exchange_sibling_final")
    chip_sums = [_pair_sum(p_, q_, f"pair_sum_final_{i}") for i, (p_, q_) in enumerate(zip(send, got))]
    last = _exchange_chips(chip_sums, "exchange_chips_final")

    pieces = {}
    for (name, idxs), part in list(zip(RIDE, rode)) + list(zip(FINAL, last[:-1])):
        for k, idx in enumerate(idxs):
            pieces[(name,) + idx] = part[:, k]
    outs = {}
    for n, _ in SHARDED[SMALL_SHARDED:]:
        if lead(n) == 2:
            part = jnp.stack([jnp.stack([pieces[(n, i, j)] for j in range(2)], axis=1) for i in range(2)], axis=1)
        elif (n, 0) in pieces:
            part = jnp.stack([pieces[(n, i)] for i in range(2)], axis=1)
        else:
            part = pieces[(n,)][:, None]
        outs[n] = _adamw(part, wts[n], mom[n], var[n], f"adamw_{n}")
    filler = [jnp.zeros((1,), F32)]
    small_w, soffs = _pack([wts[n] for n in small] + filler, 8)
    small_m, _ = _pack([mom[n] for n in small] + filler, 8)
    small_v, _ = _pack([var[n] for n in small] + filler, 8)
    small_out = [_unpack(z, soffs, small_shapes + [(1,)]) for z in _adamw(last[-1], small_w, small_m, small_v, "adamw_small")]
    loss = small_out[0][-1][0]
    for i, n in enumerate(small):
        outs[n] = [small_out[k][i] for k in range(4)]
    result = [loss, grad_x[None]]
    for k in range(4):
        result += [outs[n][k] for n in WEIGHTS]
    return tuple(result)
```

```python
import math

import jax
import jax.numpy as jnp
from jax import lax
from jax.experimental import pallas as pl
from jax.experimental.pallas import tpu as pltpu

F32 = jnp.float32
BF16 = jnp.bfloat16

N_DEV = 8
N_CHIP = 4
D_MODEL = 1024
D_FF = 2816
PLE_DIM = 256
HEAD_DIM = 64
SB_HEADS = 8
SWA_HEADS = 8
SWA_KV_HEADS = 2
SWA_GROUP = SWA_HEADS // SWA_KV_HEADS
WINDOW = 128
Q_BLOCK = 128
GDN_K_HEADS = 8
GDN_V_HEADS = 16
GDN_HEAD_DIM = 128
GDN_CONV = 4
GDN_CHUNK = 64
EPS = 1e-6
SB_W = SB_HEADS * HEAD_DIM
SWA_QW = SWA_HEADS * HEAD_DIM
SWA_KVW = SWA_KV_HEADS * HEAD_DIM
ATT_IN = 3 * SB_W + SWA_QW + 2 * SWA_KVW
GDN_KW = GDN_K_HEADS * GDN_HEAD_DIM
GDN_VW = GDN_V_HEADS * GDN_HEAD_DIM
GDN_CONV_W = 2 * GDN_KW + GDN_VW
GDN_IN = GDN_CONV_W + GDN_VW + 2 * GDN_V_HEADS
GDN_IN_PAD = GDN_CONV_W + GDN_VW + 2 * 128

ADAM_LR = 0.001
ADAM_B1 = 0.9
ADAM_B2 = 0.999
ADAM_EPS = 1e-08
ADAM_WD = 0.01
ADAM_STEP = 10

LANE = 128
VMEM_LIMIT = 56 * 1024 * 1024
PACK_W = 1024

NN = ((1,), (0,))
NT = ((1,), (1,))
TN = ((0,), (0,))

SHARDED = (
    ("ffn_norm", 2), ("gdn_conv_w", 2),
    ("ffn_w_gate", 3), ("ffn_w_up", 3), ("ffn_w_down", 2), ("att_w_in", 2), ("att_w_out", 1),
    ("gdn_w_in", 2), ("gdn_w_out", 1), ("ple_w_gate", 1), ("ple_w_proj", 2),
)
SMALL_SHARDED = 2
REPLICATED = ("mix_norm", "att_q_norm", "att_k_norm", "att_sinks", "gdn_a_log", "gdn_dt_bias",
              "gdn_out_norm", "ple_norm")
WEIGHTS = ("ffn_norm", "ffn_w_gate", "ffn_w_up", "ffn_w_down", "mix_norm", "att_w_in", "att_q_norm",
           "att_k_norm", "att_sinks", "att_w_out", "gdn_w_in", "gdn_conv_w", "gdn_a_log", "gdn_dt_bias",
           "gdn_out_norm", "gdn_w_out", "ple_norm", "ple_w_gate", "ple_w_proj")


_FFN_REST = [(0, 1), (1, 0), (1, 1)]
EARLY = [("ffn_w_gate", [(0, 0)]), ("ffn_w_up", [(0, 0)])]
FIRST = [("ffn_w_down", [(0, 0)]), ("att_w_in", [()])]
LATE = ([(n, [idx]) for n in ("ffn_w_gate", "ffn_w_up", "ffn_w_down") for idx in _FFN_REST]
        + [("att_w_out", [()]), ("gdn_w_in", [()]), ("gdn_w_out", [()]),
           ("ple_w_gate", [(0,), (1,)]), ("ple_w_proj", [(0,), (1,)])])
RIDE = [e for e in LATE if e[0] != "att_w_out"]
FINAL = EARLY + FIRST + [("att_w_out", [()])]


def _pcall(body, **kw):
    return pl.pallas_call(body, **kw)


def _params(sem=None):
    if sem is None:
        return pltpu.CompilerParams(vmem_limit_bytes=VMEM_LIMIT)
    return pltpu.CompilerParams(dimension_semantics=sem, vmem_limit_bytes=VMEM_LIMIT)


def _ride_specs(ride, out_shapes, n_sems):
    hbm = pl.BlockSpec(memory_space=pl.ANY)
    n = len(ride)
    sems = [pltpu.SemaphoreType.DMA((n, n_sems)), pltpu.SemaphoreType.DMA((n, n_sems)),
            pltpu.SemaphoreType.DMA((n, N_CHIP))] if n else []
    return [hbm] * n, [hbm] * len(out_shapes), sems


def _dot(a, b, dims=NN):
    return lax.dot_general(a, b, (dims, ((), ())), preferred_element_type=F32)


def _bdot(a, b, dims=NN):
    return _dot(a.astype(BF16), b.astype(BF16), dims)


def _split(a):
    hi = a.astype(BF16)
    lo = (a - hi.astype(F32)).astype(BF16)
    return hi, lo


def _dot3(a, b, dims=NN):
    ah, al = _split(a)
    bh, bl = _split(b)
    return _dot(ah, bh, dims) + (_dot(ah, bl, dims) + _dot(al, bh, dims))


def _dot2m(a, m, dims=NN):
    ah, al = _split(a)
    return _dot(ah, m, dims) + _dot(al, m, dims)


def _mdot2(m, a, dims=NN):
    ah, al = _split(a)
    return _dot(m, ah, dims) + _dot(m, al, dims)


def _sigmoid(x):
    return 1.0 / (1.0 + jnp.exp(-x))


def _softplus(x):
    return jnp.maximum(x, 0.0) + jnp.log(1.0 + jnp.exp(-jnp.abs(x)))


def _pick(n, cap):
    if n <= cap:
        return n
    for t in range(cap - cap % LANE, 0, -LANE):
        if n % t == 0:
            return t
    raise ValueError(f"no tile for {n} under {cap}")


def _iota2(shape, axis):
    return lax.broadcasted_iota(jnp.int32, shape, axis)


def _mm(a, b, mode, out_dtype=F32, res=None, alpha=1.0, a2=None, b2=None, name="mm"):
    if mode == "nn":
        (M, K), N = a.shape, b.shape[1]
    elif mode == "nt":
        (M, K), N = a.shape, b.shape[0]
    else:
        (K, M), N = a.shape, b.shape[1]
    tm, tn, tk = _pick(M, 1408 if mode == "tn" else 512), _pick(N, 1408), _pick(K, 1024 if mode == "tn" else 1408)
    nk = K // tk
    dims = {"nn": NN, "nt": NT, "tn": TN}[mode]
    a_spec = pl.BlockSpec((tk, tm), lambda i, j, k: (k, i)) if mode == "tn" else pl.BlockSpec((tm, tk), lambda i, j, k: (i, k))
    b_spec = pl.BlockSpec((tn, tk), lambda i, j, k: (j, k)) if mode == "nt" else pl.BlockSpec((tk, tn), lambda i, j, k: (k, j))
    o_spec = pl.BlockSpec((tm, tn), lambda i, j, k: (i, j))
    two = a2 is not None
    has_res = res is not None
    a2_spec, b2_spec = a_spec, b_spec
    if two and a2.shape != a.shape:
        assert nk == 1 and mode == "nn" and a2.shape[0] == M and b2.shape[1] == N
        a2_spec = pl.BlockSpec((tm, a2.shape[1]), lambda i, j, k: (i, 0))
        b2_spec = pl.BlockSpec((a2.shape[1], tn), lambda i, j, k: (0, j))

    def body(*refs):
        refs = list(refs)
        a_ref, b_ref = refs[0], refs[1]
        pos = 2
        if two:
            a2_ref, b2_ref = refs[2], refs[3]
            pos = 4
        if has_res:
            res_ref = refs[pos]
            pos += 1
        o_ref, acc_ref = refs[pos], refs[pos + 1]
        k = pl.program_id(2)
        part = _bdot(a_ref[...], b_ref[...], dims)
        if two:
            part = part + _bdot(a2_ref[...], b2_ref[...], dims)

        def finish(acc):
            out = acc * alpha if alpha != 1.0 else acc
            if has_res:
                out = res_ref[...] + out
            o_ref[...] = out.astype(out_dtype)

        if nk == 1:
            finish(part)
        else:
            @pl.when(k == 0)
            def _():
                acc_ref[...] = part

            @pl.when(k > 0)
            def _():
                acc_ref[...] += part

            @pl.when(k == nk - 1)
            def _():
                finish(acc_ref[...])

    ins = [a, b]
    specs = [a_spec, b_spec]
    if two:
        ins += [a2, b2]
        specs += [a2_spec, b2_spec]
    if has_res:
        ins.append(res)
        specs.append(o_spec)
    return _pcall(
        body, name=name, grid=(M // tm, N // tn, nk), in_specs=specs, out_specs=o_spec,
        out_shape=jax.ShapeDtypeStruct((M, N), out_dtype),
        scratch_shapes=[pltpu.VMEM((tm, tn) if nk > 1 else (8, LANE), F32)],
        compiler_params=_params(("parallel", "parallel", "arbitrary")),
    )(*ins)


ROW_TILE = 256


def _rms_fwd(h, g, name):
    T, D = h.shape
    tr = _pick(T, ROW_TILE)

    def body(h_ref, g_ref, n_ref):
        x = h_ref[...]
        r = lax.rsqrt(jnp.mean(x * x, axis=-1, keepdims=True) + EPS)
        n_ref[...] = (x * r * g_ref[...]).astype(BF16)

    return _pcall(
        body, name=name, grid=(T // tr,),
        in_specs=[pl.BlockSpec((tr, D), lambda i: (i, 0)), pl.BlockSpec((1, D), lambda i: (0, 0))],
        out_specs=pl.BlockSpec((tr, D), lambda i: (i, 0)),
        out_shape=jax.ShapeDtypeStruct((T, D), BF16), compiler_params=_params(("parallel",)),
    )(h, g.reshape(1, D))


def _rms_bwd(dn, h, g, dres, name):
    T, D = h.shape
    tr = _pick(T, ROW_TILE)

    def body(dn_ref, h_ref, g_ref, dres_ref, dh_ref, dg_ref):
        x = h_ref[...]
        r = lax.rsqrt(jnp.mean(x * x, axis=-1, keepdims=True) + EPS)
        xh = x * r
        d = dn_ref[...].astype(F32)
        dxh = d * g_ref[...]
        dh_ref[...] = dres_ref[...] + r * (dxh - xh * jnp.mean(dxh * xh, axis=-1, keepdims=True))
        part = jnp.sum(d * xh, axis=0, keepdims=True)

        @pl.when(pl.program_id(0) == 0)
        def _():
            dg_ref[...] = part

        @pl.when(pl.program_id(0) > 0)
        def _():
            dg_ref[...] += part

    row = pl.BlockSpec((tr, D), lambda i: (i, 0))
    vec = pl.BlockSpec((1, D), lambda i: (0, 0))
    dh, dg = _pcall(
        body, name=name, grid=(T // tr,), in_specs=[row, row, vec, row], out_specs=[row, vec],
        out_shape=[jax.ShapeDtypeStruct((T, D), F32), jax.ShapeDtypeStruct((1, D), F32)],
        compiler_params=_params(("arbitrary",)),
    )(dn, h, g.reshape(1, D), dres)
    return dh, dg.reshape(D)


def _gateup(n, wg, wu, name, ride=()):
    T, D = n.shape
    F = wg.shape[1]
    tm, tn = _pick(T, 512), _pick(F, 1408)
    nr = len(ride)
    ride_out = [jax.ShapeDtypeStruct((N_DEV,) + r.shape, r.dtype) for r in ride]
    ride_in_specs, ride_out_specs, ride_sems = _ride_specs(ride, ride_out, N_DEV - 1)
    grid = (T // tm, F // tn)

    def body(*refs):
        n_ref, wg_ref, wu_ref = refs[:3]
        a_ref, b_ref, hid_ref = refs[3 + nr:6 + nr]
        if nr:
            i, j = pl.program_id(0), pl.program_id(1)
            start, finish = _gather_protocol(refs[3:3 + nr], refs[6 + nr:6 + 2 * nr], *refs[6 + 2 * nr:])
            pl.when((i == 0) & (j == 0))(start)
        x = n_ref[...]
        a = _dot(x, wg_ref[...])
        b = _dot(x, wu_ref[...])
        a_ref[...] = a.astype(BF16)
        b_ref[...] = b.astype(BF16)
        hid_ref[...] = (a * _sigmoid(a) * b).astype(BF16)
        if nr:
            pl.when((i == grid[0] - 1) & (j == grid[1] - 1))(finish)

    o_spec = pl.BlockSpec((tm, tn), lambda i, j: (i, j))
    w_spec = pl.BlockSpec((D, tn), lambda i, j: (0, j))
    sh = jax.ShapeDtypeStruct((T, F), BF16)
    res = _pcall(
        body, name=name, grid=grid,
        in_specs=[pl.BlockSpec((tm, D), lambda i, j: (i, 0)), w_spec, w_spec] + ride_in_specs,
        out_specs=[o_spec, o_spec, o_spec] + ride_out_specs, out_shape=[sh, sh, sh] + ride_out,
        scratch_shapes=ride_sems,
        compiler_params=_params(("arbitrary", "arbitrary") if nr else ("parallel", "parallel")),
    )(n, wg, wu, *ride)
    return res[0], res[1], res[2], list(res[3:])


def _ffn_dhid(dy, wd, a, b, name):
    T, D = dy.shape
    F = wd.shape[0]
    tm, tn = _pick(T, 512), _pick(F, 1408)

    def body(dy_ref, wd_ref, a_ref, b_ref, da_ref, db_ref):
        dhid = 0.5 * _bdot(dy_ref[...], wd_ref[...], NT)
        av = a_ref[...].astype(F32)
        bv = b_ref[...].astype(F32)
        s = _sigmoid(av)
        da_ref[...] = (dhid * bv * s * (1.0 + av * (1.0 - s))).astype(BF16)
        db_ref[...] = (dhid * av * s).astype(BF16)

    o_spec = pl.BlockSpec((tm, tn), lambda i, j: (i, j))
    sh = jax.ShapeDtypeStruct((T, F), BF16)
    return _pcall(
        body, name=name, grid=(T // tm, F // tn),
        in_specs=[pl.BlockSpec((tm, D), lambda i, j: (i, 0)), pl.BlockSpec((tn, D), lambda i, j: (j, 0)), o_spec, o_spec],
        out_specs=[o_spec, o_spec], out_shape=[sh, sh],
        compiler_params=_params(("parallel", "parallel")),
    )(dy, wd, a, b)


def _ffn_fwd(h, g, wg, wu, wd, tag, ride=(), wd_of=None):
    n = _rms_fwd(h, g, f"{tag}_norm")
    a, b, hid, gathered = _gateup(n, wg, wu, f"{tag}_gateup", ride)
    if wd_of is not None:
        wd = wd_of(gathered)
    h2 = _mm(hid, wd, "nn", res=h, alpha=0.5, name=f"{tag}_down")
    return h2, (h, n, a, b, hid)


def _ffn_bwd(dh2, saved, g, wg, wu, wd, tag):
    h, n, a, b, hid = saved
    da, db = _ffn_dhid(dh2, wd, a, b, f"{tag}_dhid")
    dwd = _mm(hid, dh2, "tn", alpha=0.5, out_dtype=BF16, name=f"{tag}_dwd")
    dwg = _mm(n, da, "tn", out_dtype=BF16, name=f"{tag}_dwg")
    dwu = _mm(n, db, "tn", out_dtype=BF16, name=f"{tag}_dwu")
    dn = _mm(da, wg, "nt", a2=db, b2=wu, name=f"{tag}_dn")
    dh, dg = _rms_bwd(dn, h, g, dh2, f"{tag}_dnorm")
    return dh, dg, dwg, dwu, dwd


def _ple_fwd(h, p, g, w_gate, w_proj, tag):
    T, D = h.shape
    pn = _rms_fwd(h, g, f"{tag}_norm")
    tm, tn = _pick(T, 512), _pick(D, 1024)
    P = p.shape[1]

    def body(pn_ref, p_ref, wg_ref, wp_ref, h_ref, o_ref, gl_ref, pp_ref):
        gl = _dot(pn_ref[...], wg_ref[...])
        pp = _bdot(p_ref[...], wp_ref[...])
        gl_ref[...] = gl
        pp_ref[...] = pp
        o_ref[...] = h_ref[...] + _sigmoid(gl) * pp

    o_spec = pl.BlockSpec((tm, tn), lambda i, j: (i, j))
    sh = jax.ShapeDtypeStruct((T, D), F32)
    h2, gl, pp = _pcall(
        body, name=f"{tag}_fwd", grid=(T // tm, D // tn),
        in_specs=[pl.BlockSpec((tm, D), lambda i, j: (i, 0)), pl.BlockSpec((tm, P), lambda i, j: (i, 0)),
                  pl.BlockSpec((D, tn), lambda i, j: (0, j)), pl.BlockSpec((P, tn), lambda i, j: (0, j)), o_spec],
        out_specs=[o_spec, o_spec, o_spec], out_shape=[sh, sh, sh],
        compiler_params=_params(("parallel", "parallel")),
    )(pn, p, w_gate, w_proj, h)
    return h2, (h, pn, gl, pp)


def _ple_bwd(dh2, saved, p, g, w_gate, tag):
    h, pn, gl, pp = saved
    T, D = h.shape
    tr = _pick(T, ROW_TILE)

    def body(d_ref, gl_ref, pp_ref, dgl_ref, dpp_ref):
        d = d_ref[...]
        s = _sigmoid(gl_ref[...])
        dpp_ref[...] = (d * s).astype(BF16)
        dgl_ref[...] = (d * pp_ref[...] * s * (1.0 - s)).astype(BF16)

    row = pl.BlockSpec((tr, D), lambda i: (i, 0))
    sh = jax.ShapeDtypeStruct((T, D), BF16)
    dgl, dpp = _pcall(body, name=f"{tag}_dgate", grid=(T // tr,), in_specs=[row, row, row], out_specs=[row, row],
                      out_shape=[sh, sh], compiler_params=_params(("parallel",)))(dh2, gl, pp)
    dw_proj = _mm(p, dpp, "tn", out_dtype=BF16, name=f"{tag}_dwproj")
    dw_gate = _mm(pn, dgl, "tn", out_dtype=BF16, name=f"{tag}_dwgate")
    dpn = _mm(dgl, w_gate, "nt", name=f"{tag}_dpn")
    dh, dg = _rms_bwd(dpn, h, g, dh2, f"{tag}_dnorm")
    return dh, dg, dw_gate, dw_proj


def _loss_head(y, target):
    T, D = y.shape
    tr = _pick(T, ROW_TILE)

    def body(y_ref, t_ref, dy_ref, l_ref):
        e = y_ref[...] - t_ref[...]
        dy_ref[...] = e * (1.0 / D)
        part = jnp.sum(e * e, axis=0, keepdims=True)

        @pl.when(pl.program_id(0) == 0)
        def _():
            l_ref[...] = part

        @pl.when(pl.program_id(0) > 0)
        def _():
            l_ref[...] += part

    row = pl.BlockSpec((tr, D), lambda i: (i, 0))
    vec = pl.BlockSpec((1, D), lambda i: (0, 0))
    dy, l = _pcall(body, name="loss_head", grid=(T // tr,), in_specs=[row, row], out_specs=[row, vec],
                   out_shape=[jax.ShapeDtypeStruct((T, D), F32), jax.ShapeDtypeStruct((1, D), F32)],
                   compiler_params=_params(("arbitrary",)))(y, target)
    return (0.5 / D) * jnp.sum(l), dy


SB_LANES = SB_HEADS * 2 * HEAD_DIM


def _sb_consts():
    row = _iota2((Q_BLOCK, Q_BLOCK), 0)
    col = _iota2((Q_BLOCK, Q_BLOCK), 1)
    after = (row > col).astype(BF16)
    before = (row < col).astype(BF16)
    return col < row, after, before, col


def _sb_fwd(proj, ride=()):
    T = proj.shape[0]
    H, d, L = SB_HEADS, HEAD_DIM, 2 * HEAD_DIM
    nblk = T // Q_BLOCK
    scale = d ** -0.5
    n = len(ride)
    ride_out = [jax.ShapeDtypeStruct((N_DEV,) + a.shape, a.dtype) for a in ride]
    ride_in_specs, ride_out_specs, ride_sems = _ride_specs(ride, ride_out, N_DEV - 1)
    R = range(H)
    tile = lambda g: slice(g * L, (g + 1) * L)

    def body(*refs):
        q_ref, kv_ref = refs[:2]
        rin = refs[2:2 + n]
        o_ref, c_ref = refs[2 + n:4 + n]
        rout = refs[4 + n:4 + 2 * n]
        run_ref = refs[4 + 2 * n]
        i = pl.program_id(0)
        if n:
            start, finish = _gather_protocol(rin, rout, *refs[5 + 2 * n:])
            pl.when(i == 0)(start)
        causal, after, _, col = _sb_consts()
        qs = [q_ref[:, tile(g)] * scale for g in R]
        o_ref[...] = jnp.zeros_like(o_ref)
        c_ref[...] = jnp.zeros_like(c_ref)
        run_ref[...] = jnp.zeros_like(run_ref)

        def pair(j, diag):
            rows = pl.ds(pl.multiple_of(j * Q_BLOCK, Q_BLOCK), Q_BLOCK)
            kvj = [kv_ref[rows, tile(g)] for g in R]
            c = [run_ref[g] for g in R]
            acc = [o_ref[:, tile(g)] for g in R]
            cm = None if diag else [c_ref[:, tile(g)] for g in R]
            z = [_dot(qs[g], kvj[g], NT) for g in R]
            sp = [_softplus(z[g]) for g in R]
            lk = [jnp.where(causal, -sp[g], 0.0) if diag else -sp[g] for g in R]
            btw = [_dot2m(lk[g], after) for g in R]
            e = [jnp.exp((z[g] - sp[g]) + btw[g] + c[g]) for g in R]
            w = [jnp.where(causal, e[g], 0.0) if diag else e[g] for g in R]
            pv = [_bdot(w[g], kvj[g]) for g in R]
            rs = [jnp.sum(lk[g], axis=1, keepdims=True) for g in R]
            for g in R:
                o_ref[:, tile(g)] = acc[g] + pv[g]
                if not diag:
                    c_ref[:, tile(g)] = jnp.where(col == j, c[g], cm[g])
                run_ref[g] = c[g] + rs[g]

        pair(i, True)

        @pl.loop(0, i)
        def _(jj):
            pair(i - 1 - jj, False)

        if n:
            pl.when(i == nblk - 1)(finish)

    blk = pl.BlockSpec((Q_BLOCK, H * L), lambda i: (i, 0))
    full = pl.BlockSpec((T, H * L), lambda i: (0, 1))
    res = _pcall(
        body, name="sb_fwd", grid=(nblk,), in_specs=[blk, full] + ride_in_specs,
        out_specs=[blk, blk] + ride_out_specs,
        out_shape=[jax.ShapeDtypeStruct((T, H * L), F32), jax.ShapeDtypeStruct((T, H * L), F32)] + ride_out,
        scratch_shapes=[pltpu.VMEM((H, Q_BLOCK, 1), F32)] + ride_sems,
        compiler_params=_params(("arbitrary",)),
    )(proj, proj, *ride)
    return res[0], res[1], list(res[2:])


def _sb_bwd(proj, carry, do, ride=()):
    T = proj.shape[0]
    H, d, L = SB_HEADS, HEAD_DIM, 2 * HEAD_DIM
    nblk = T // Q_BLOCK
    scale = d ** -0.5
    n = len(ride)
    ride_out = [jax.ShapeDtypeStruct(a.shape, a.dtype) for a in ride]
    ride_in_specs, ride_out_specs, ride_sems = _ride_specs(ride, ride_out, N_CHIP - 1)
    R = range(H)
    tile = lambda g: slice(g * L, (g + 1) * L)

    def body(*refs):
        q_ref, kv_ref, c_ref, do_ref = refs[:4]
        rin = refs[4:4 + n]
        dq_ref, dkv_ref = refs[4 + n:6 + n]
        rout = refs[6 + n:6 + 2 * n]
        run_ref = refs[6 + 2 * n]
        i = pl.program_id(0)
        if n:
            start, finish = _chips_protocol(rin, rout, *refs[7 + 2 * n:])
            pl.when(i == 0)(start)

        @pl.when(i == 0)
        def _():
            dkv_ref[...] = jnp.zeros_like(dkv_ref)

        causal, after, before, col = _sb_consts()
        qs = [q_ref[:, tile(g)] * scale for g in R]
        dov = [do_ref[:, tile(g)] for g in R]
        qdo = [jnp.concatenate([qs[g], dov[g]], axis=0) for g in R]
        dq_ref[...] = jnp.zeros_like(dq_ref)
        run_ref[...] = jnp.zeros_like(run_ref)

        def pair(j, diag):
            rows = pl.ds(pl.multiple_of(j * Q_BLOCK, Q_BLOCK), Q_BLOCK)
            kvj = [kv_ref[rows, tile(g)] for g in R]
            gsum = [run_ref[g] for g in R]
            dq0 = [dq_ref[:, tile(g)] for g in R]
            dkv0 = [dkv_ref[rows, tile(g)] for g in R]
            cm = None if diag else [c_ref[:, tile(g)] for g in R]
            z = [_dot(qs[g], kvj[g], NT) for g in R]
            sp = [_softplus(z[g]) for g in R]
            lk = [jnp.where(causal, -sp[g], 0.0) if diag else -sp[g] for g in R]
            ls = [z[g] - sp[g] for g in R]
            logw = [ls[g] + _dot2m(lk[g], after) for g in R]
            if not diag:
                logw = [logw[g] + jnp.sum(jnp.where(col == j, cm[g], 0.0), axis=1, keepdims=True) for g in R]
            e = [jnp.exp(logw[g]) for g in R]
            w = [jnp.where(causal, e[g], 0.0) if diag else e[g] for g in R]
            gw = [_dot(dov[g], kvj[g], NT) * w[g] for g in R]
            gpre = [gsum[g] + _dot(gw[g].astype(BF16), before) for g in R]
            sig = [jnp.exp(ls[g]) for g in R]
            dz = [gw[g] * (1.0 - sig[g]) - sig[g] * gpre[g] for g in R]
            if diag:
                dz = [jnp.where(causal, dz[g], 0.0) for g in R]
            dzb = [dz[g].astype(BF16) for g in R]
            dq1 = [_dot(dzb[g], kvj[g]) for g in R]
            dkv1 = [_dot(jnp.concatenate([dzb[g], w[g].astype(BF16)], axis=0), qdo[g], TN) for g in R]
            gs1 = [jnp.sum(gw[g], axis=1, keepdims=True) for g in R]
            for g in R:
                dq_ref[:, tile(g)] = dq0[g] + dq1[g]
                dkv_ref[rows, tile(g)] = dkv0[g] + dkv1[g]
                run_ref[g] = gsum[g] + gs1[g]

        @pl.loop(0, i)
        def _(j):
            pair(j, False)

        pair(i, True)
        dq_ref[...] = dq_ref[...] * scale
        if n:
            pl.when(i == nblk - 1)(finish)

    blk = pl.BlockSpec((Q_BLOCK, H * L), lambda i: (i, 0))
    once = pl.Buffered(1)
    sh = jax.ShapeDtypeStruct((T, H * L), F32)
    res = _pcall(
        body, name="sb_bwd", grid=(nblk,),
        in_specs=[blk, pl.BlockSpec((T, H * L), lambda i: (0, 1), pipeline_mode=once), blk, blk] + ride_in_specs,
        out_specs=[blk, pl.BlockSpec((T, H * L), lambda i: (0, 0), pipeline_mode=once)] + ride_out_specs,
        out_shape=[sh, sh] + ride_out,
        scratch_shapes=[pltpu.VMEM((H, Q_BLOCK, 1), F32)] + ride_sems,
        compiler_params=_params(("arbitrary",)),
    )(proj, proj, carry, do, *ride)
    return res[0], res[1], list(res[2:])


def _swa_common(q_ref, kvp_ref, kvc_ref, qg_ref, kg_ref, sk_ref, sl_ref, n):
    W, d, G = WINDOW, HEAD_DIM, SWA_GROUP
    scale = d ** -0.5
    row = _iota2((W, 2 * W), 0)
    col = _iota2((W, 2 * W), 1)
    dist = row + W - col
    valid = (dist >= 0) & (dist < W) & ((n > 0) | (col >= W))
    distf = dist.astype(F32)
    kvcat = jnp.concatenate([kvp_ref[...], kvc_ref[...]], axis=0)
    KH, QH = range(SWA_KV_HEADS), range(SWA_HEADS)
    kraw = [kvcat[:, hk * d:(hk + 1) * d] for hk in KH]
    vcat = [kvcat[:, SWA_KVW + hk * d:SWA_KVW + (hk + 1) * d].astype(BF16) for hk in KH]
    rk = [lax.rsqrt(jnp.mean(kraw[hk] * kraw[hk], axis=-1, keepdims=True) + EPS) for hk in KH]
    kh = [kraw[hk] * rk[hk] for hk in KH]
    kn = [(kh[hk] * kg_ref[...]).astype(BF16) for hk in KH]
    qraw = [q_ref[:, h * d:(h + 1) * d] for h in QH]
    rq = [lax.rsqrt(jnp.mean(qraw[h] * qraw[h], axis=-1, keepdims=True) + EPS) for h in QH]
    qh = [qraw[h] * rq[h] for h in QH]
    qn = [(qh[h] * qg_ref[...]).astype(BF16) for h in QH]
    sink = [sk_ref[h:h + 1, :1] for h in QH]
    s = [jnp.where(valid, _dot(qn[h], kn[h // G], NT) * scale - sl_ref[h:h + 1, :1] * distf, -1e30) for h in QH]
    m = [jnp.maximum(jnp.max(s[h], axis=1, keepdims=True), sink[h]) for h in QH]
    p = [jnp.where(valid, jnp.exp(s[h] - m[h]), 0.0) for h in QH]
    esink = [jnp.exp(sink[h] - m[h]) for h in QH]
    den = [jnp.sum(p[h], axis=1, keepdims=True) + esink[h] for h in QH]
    prob = [p[h] / den[h] for h in QH]
    return vcat, rk, kh, kn, rq, qh, qn, esink, den, prob


def _swa_specs(T):
    W = WINDOW
    q = pl.BlockSpec((W, SWA_QW), lambda n: (n, 0))
    prev = pl.BlockSpec((W, 2 * SWA_KVW), lambda n: (jnp.maximum(n - 1, 0), SWA_QW // (2 * SWA_KVW)))
    cur = pl.BlockSpec((W, 2 * SWA_KVW), lambda n: (n, SWA_QW // (2 * SWA_KVW)))
    gain = pl.BlockSpec((1, HEAD_DIM), lambda n: (0, 0))
    perhead = pl.BlockSpec((SWA_HEADS, LANE), lambda n: (0, 0))
    return q, prev, cur, gain, perhead


def _swa_fwd(proj, qg, kg, sinks, slopes):
    T = proj.shape[0]
    W, d, G = WINDOW, HEAD_DIM, SWA_GROUP

    def body(q_ref, kvp_ref, kvc_ref, qg_ref, kg_ref, sk_ref, sl_ref, o_ref):
        vcat, _, _, _, _, _, _, _, _, prob = _swa_common(q_ref, kvp_ref, kvc_ref, qg_ref, kg_ref, sk_ref, sl_ref,
                                                         pl.program_id(0))
        outs = [_bdot(prob[h], vcat[h // G]) for h in range(SWA_HEADS)]
        o_ref[...] = jnp.concatenate(outs, axis=1).astype(BF16)

    q, prev, cur, gain, perhead = _swa_specs(T)
    return _pcall(
        body, name="swa_fwd", grid=(T // W,), in_specs=[q, prev, cur, gain, gain, perhead, perhead], out_specs=q,
        out_shape=jax.ShapeDtypeStruct((T, SWA_QW), BF16), compiler_params=_params(("parallel",)),
    )(proj, proj, proj, qg, kg, sinks, slopes)


def _swa_bwd(proj, qg, kg, sinks, slopes, do):
    T = proj.shape[0]
    W, d, G = WINDOW, HEAD_DIM, SWA_GROUP
    scale = d ** -0.5
    KH, QH = range(SWA_KV_HEADS), range(SWA_HEADS)

    def body(q_ref, kvp_ref, kvc_ref, qg_ref, kg_ref, sk_ref, sl_ref, do_ref,
             dq_ref, dkv_ref, dqg_ref, dkg_ref, dsk_ref):
        n = pl.program_id(0)

        @pl.when(n == 0)
        def _():
            dqg_ref[...] = jnp.zeros_like(dqg_ref)
            dkg_ref[...] = jnp.zeros_like(dkg_ref)
            dsk_ref[...] = jnp.zeros_like(dsk_ref)
            dkv_ref[...] = jnp.zeros_like(dkv_ref)

        vcat, rk, kh, kn, rq, qh, qn, esink, den, prob = _swa_common(q_ref, kvp_ref, kvc_ref, qg_ref, kg_ref,
                                                                     sk_ref, sl_ref, n)
        dov = [do_ref[:, h * d:(h + 1) * d].astype(BF16) for h in QH]
        dp = [_dot(dov[h], vcat[h // G], NT) for h in QH]
        dd = [jnp.sum(prob[h] * dp[h], axis=1, keepdims=True) for h in QH]
        dsb = [(prob[h] * (dp[h] - dd[h]) * scale).astype(BF16) for h in QH]
        dsink = [-jnp.sum((esink[h] / den[h]) * dd[h], axis=0, keepdims=True) for h in QH]
        dqn = [_dot(dsb[h], kn[h // G]) for h in QH]
        dkn_h = [_dot(dsb[h], qn[h], TN) for h in QH]
        dv_h = [_dot(prob[h].astype(BF16), dov[h], TN) for h in QH]
        dqh = [dqn[h] * qg_ref[...] for h in QH]
        dq = [rq[h] * (dqh[h] - qh[h] * jnp.mean(dqh[h] * qh[h], axis=-1, keepdims=True)) for h in QH]
        dkn = [sum(dkn_h[hk * G + g] for g in range(G)) for hk in KH]
        dvc = [sum(dv_h[hk * G + g] for g in range(G)) for hk in KH]
        dkh = [dkn[hk] * kg_ref[...] for hk in KH]
        dkraw = [rk[hk] * (dkh[hk] - kh[hk] * jnp.mean(dkh[hk] * kh[hk], axis=-1, keepdims=True)) for hk in KH]
        dq_ref[...] = jnp.concatenate(dq, axis=1)
        dqg_ref[...] += sum(jnp.sum(dqn[h] * qh[h], axis=0, keepdims=True) for h in QH)
        dkg_ref[...] += sum(jnp.sum(dkn[hk] * kh[hk], axis=0, keepdims=True) for hk in KH)
        rowh = _iota2((SWA_HEADS, LANE), 0)
        dsk_ref[...] += sum(jnp.where(rowh == h, dsink[h], 0.0) for h in QH)
        upd = jnp.concatenate(dkraw + dvc, axis=1)
        offp = pl.multiple_of(jnp.maximum(n - 1, 0) * W, W)
        offc = pl.multiple_of(n * W, W)
        dkv_ref[pl.ds(offp, W), :] += upd[:W]
        dkv_ref[pl.ds(offc, W), :] += upd[W:]

    q, prev, cur, gain, perhead = _swa_specs(T)
    kvfull = pl.BlockSpec((T, 2 * SWA_KVW), lambda n: (0, 0))
    gs = jax.ShapeDtypeStruct((1, d), F32)
    return _pcall(
        body, name="swa_bwd", grid=(T // W,), in_specs=[q, prev, cur, gain, gain, perhead, perhead, q],
        out_specs=[q, kvfull, gain, gain, perhead],
        out_shape=[jax.ShapeDtypeStruct((T, SWA_QW), F32), jax.ShapeDtypeStruct((T, 2 * SWA_KVW), F32), gs, gs,
                   jax.ShapeDtypeStruct((SWA_HEADS, LANE), F32)],
        compiler_params=_params(("arbitrary",)),
    )(proj, proj, proj, qg, kg, sinks, slopes, do)


def _alibi():
    s = [2.0 ** (-8.0 * (i + 1) / SWA_HEADS) for i in range(SWA_HEADS)]
    return jnp.broadcast_to(jnp.asarray(s, F32)[:, None], (SWA_HEADS, LANE))


def _head_tiles(lo, hi):
    shp = lo.shape[:-1]
    return jnp.concatenate([lo.reshape(shp + (SB_HEADS, HEAD_DIM)), hi.reshape(shp + (SB_HEADS, HEAD_DIM))],
                           axis=-1).reshape(shp + (SB_LANES,))


def _tile_halves(x):
    shp = x.shape[:-1]
    t = x.reshape(shp + (SB_HEADS, 2, HEAD_DIM))
    return t[..., 0, :].reshape(shp + (SB_W,)), t[..., 1, :].reshape(shp + (SB_W,))


def _att_in_weights(w_in):
    sq, sk, sv = w_in[:, :SB_W], w_in[:, SB_W:2 * SB_W], w_in[:, 2 * SB_W:3 * SB_W]
    return jnp.concatenate([_head_tiles(sq, jnp.zeros_like(sq)), _head_tiles(sk, sv)], axis=1), w_in[:, 3 * SB_W:]


def _att_out_weights(w_out):
    wo = w_out[:SB_W]
    return _head_tiles(jnp.zeros_like(wo).T, wo.T).T, w_out[SB_W:]


def _att_fwd(h, g, w_in, w_out_of, q_gain, k_gain, sinks, ride=()):
    hn = _rms_fwd(h, g, "att_norm")
    w_sb, w_swa = _att_in_weights(w_in)
    proj_sb = _mm(hn, w_sb, "nn", out_dtype=BF16, name="att_in_sb")
    proj_swa = _mm(hn, w_swa, "nn", name="att_in_swa")
    a_out, carry, gathered = _sb_fwd(proj_sb, ride)
    w_out = w_out_of(gathered)
    wo_sb, wo_swa = _att_out_weights(w_out)
    sk128 = jnp.broadcast_to(sinks.reshape(SWA_HEADS, 1), (SWA_HEADS, LANE))
    qg, kg = q_gain.reshape(1, HEAD_DIM), k_gain.reshape(1, HEAD_DIM)
    b_out = _swa_fwd(proj_swa, qg, kg, sk128, _alibi())
    h2 = _mm(a_out, wo_sb, "nn", res=h, a2=b_out, b2=wo_swa, name="att_out")
    return h2, (h, hn, proj_sb, proj_swa, carry, a_out, b_out, sk128, qg, kg), gathered


def _att_bwd(dh2, saved, g, w_in, w_out, ride=()):
    h, hn, proj_sb, proj_swa, carry, a_out, b_out, sk128, qg, kg = saved
    w_sb, w_swa = _att_in_weights(w_in)
    wo_sb, wo_swa = _att_out_weights(w_out)
    da = _mm(dh2, wo_sb, "nt", out_dtype=BF16, name="att_do_sb")
    db = _mm(dh2, wo_swa, "nt", name="att_do_swa")
    dwo_sb = _mm(a_out, dh2, "tn", out_dtype=BF16, name="att_dwout_sb")
    dwo_swa = _mm(b_out, dh2, "tn", out_dtype=BF16, name="att_dwout_swa")
    dw_out = jnp.concatenate([_tile_halves(dwo_sb.T)[1].T, dwo_swa], axis=0)
    dq, dkv, rode = _sb_bwd(proj_sb, carry, da, ride)
    dbq, dbkv, dqg, dkg, dsink = _swa_bwd(proj_swa, qg, kg, sk128, _alibi(), db)
    dproj = jnp.concatenate([dq.astype(BF16), dkv.astype(BF16), dbq.astype(BF16), dbkv.astype(BF16)], axis=1)
    w_all = jnp.concatenate([w_sb, w_swa], axis=1)
    dw_all = _mm(hn, dproj, "tn", out_dtype=BF16, name="att_dwin")
    dhn = _mm(dproj, w_all, "nt", name="att_dhn")
    dsq, _ = _tile_halves(dw_all[:, :SB_LANES])
    dsk, dsv = _tile_halves(dw_all[:, SB_LANES:2 * SB_LANES])
    dw_in = jnp.concatenate([dsq, dsk, dsv, dw_all[:, 2 * SB_LANES:]], axis=1)
    dh, dg = _rms_bwd(dhn, h, g, dh2, "att_dnorm")
    return dh, dg, dw_in, dw_out, dqg.reshape(HEAD_DIM), dkg.reshape(HEAD_DIM), dsink[:, 0], rode


CONV_ROWS = 512
CONV_COLS = 512
HALO = 8


def _shifted(xcat, s, tm):
    if s == 0:
        return xcat[HALO:HALO + tm]
    return pltpu.roll(xcat, s, 0)[HALO:HALO + tm]


def _conv_pre(x_ref, halo_ref, w_ref, i, tm):
    xc = x_ref[...]
    halo = jnp.where(i > 0, halo_ref[...], 0.0)
    xcat = jnp.concatenate([halo, xc], axis=0)
    w = w_ref[...]
    y = w[GDN_CONV - 1:GDN_CONV] * xc
    for kk in range(GDN_CONV - 1):
        y = y + w[kk:kk + 1] * _shifted(xcat, GDN_CONV - 1 - kk, tm)
    return xcat, y


def _l2_heads(s, qscale_of):
    outs, rs = [], []
    for hh in range(s.shape[1] // GDN_HEAD_DIM):
        sh = s[:, hh * GDN_HEAD_DIM:(hh + 1) * GDN_HEAD_DIM]
        r = lax.rsqrt(jnp.sum(sh * sh, axis=-1, keepdims=True) + EPS)
        outs.append(sh * r)
        rs.append(r)
    return outs, rs


def _conv_specs(T, col0, tm, tc):
    cur = pl.BlockSpec((tm, tc), lambda j, i: (i, j + col0 // tc))
    halo = pl.BlockSpec((HALO, tc), lambda j, i: (jnp.maximum(i * (tm // HALO) - 1, 0), j + col0 // tc))
    wsp = pl.BlockSpec((GDN_CONV, tc), lambda j, i: (0, j + col0 // tc))
    out = pl.BlockSpec((tm, tc), lambda j, i: (i, j))
    return cur, halo, wsp, out


def _conv_fwd(proj, conv_w, col0, width, norm, name):
    T = proj.shape[0]
    tm, tc = _pick(T, CONV_ROWS), CONV_COLS
    cur, halo, wsp, out = _conv_specs(T, col0, tm, tc)
    n_q_tiles = (width // 2) // tc

    def body(x_ref, halo_ref, w_ref, o_ref):
        j, i = pl.program_id(0), pl.program_id(1)
        _, y = _conv_pre(x_ref, halo_ref, w_ref, i, tm)
        s = y * _sigmoid(y)
        if norm:
            outs, _ = _l2_heads(s, None)
            qs = jnp.where(j < n_q_tiles, GDN_HEAD_DIM ** -0.5, 1.0)
            o_ref[...] = jnp.concatenate(outs, axis=1) * qs
        else:
            o_ref[...] = s

    return _pcall(body, name=name, grid=(width // tc, T // tm), in_specs=[cur, halo, wsp], out_specs=out,
                  out_shape=jax.ShapeDtypeStruct((T, width), F32),
                  compiler_params=_params(("parallel", "parallel")))(proj, proj, conv_w)


def _conv_bwd_pre(proj, conv_w, dout, col0, width, norm, name):
    T = proj.shape[0]
    tm, tc = _pick(T, CONV_ROWS), CONV_COLS
    cur, halo, wsp, out = _conv_specs(T, col0, tm, tc)
    n_q_tiles = (width // 2) // tc

    def body(x_ref, halo_ref, w_ref, d_ref, dy_ref, dw_ref):
        j, i = pl.program_id(0), pl.program_id(1)
        xcat, y = _conv_pre(x_ref, halo_ref, w_ref, i, tm)
        sg = _sigmoid(y)
        s = y * sg
        d = d_ref[...]
        if norm:
            qs = jnp.where(j < n_q_tiles, GDN_HEAD_DIM ** -0.5, 1.0)
            d = d * qs
            outs, rs = _l2_heads(s, None)
            parts = []
            for hh, (nh, r) in enumerate(zip(outs, rs)):
                dh = d[:, hh * GDN_HEAD_DIM:(hh + 1) * GDN_HEAD_DIM]
                parts.append(r * (dh - nh * jnp.sum(dh * nh, axis=-1, keepdims=True)))
            ds = jnp.concatenate(parts, axis=1)
        else:
            ds = d
        dy = ds * sg * (1.0 + y * (1.0 - sg))
        dy_ref[...] = dy
        rows = [jnp.sum(dy * _shifted(xcat, GDN_CONV - 1 - kk, tm), axis=0, keepdims=True) for kk in range(GDN_CONV)]
        part = jnp.concatenate(rows, axis=0)

        @pl.when(i == 0)
        def _():
            dw_ref[...] = part

        @pl.when(i > 0)
        def _():
            dw_ref[...] += part

    wout = pl.BlockSpec((GDN_CONV, tc), lambda j, i: (0, j))
    return _pcall(body, name=name, grid=(width // tc, T // tm), in_specs=[cur, halo, wsp, out], out_specs=[out, wout],
                  out_shape=[jax.ShapeDtypeStruct((T, width), F32), jax.ShapeDtypeStruct((GDN_CONV, width), F32)],
                  compiler_params=_params(("parallel", "arbitrary")))(proj, proj, conv_w, dout)


def _conv_bwd_in(dy, conv_w, name):
    T, C = dy.shape
    tm, tc = _pick(T, CONV_ROWS), CONV_COLS
    nrow = T // tm

    def body(d_ref, nxt_ref, w_ref, dx_ref):
        i = pl.program_id(0)
        dc = d_ref[...]
        nxt = jnp.where(i < nrow - 1, nxt_ref[...], 0.0)
        dcat = jnp.concatenate([dc, nxt], axis=0)
        w = w_ref[...]
        dx = w[GDN_CONV - 1:GDN_CONV] * dc
        for kk in range(GDN_CONV - 1):
            s = GDN_CONV - 1 - kk
            dx = dx + w[kk:kk + 1] * pltpu.roll(dcat, tm + HALO - s, 0)[:tm]
        dx_ref[...] = dx.astype(BF16)

    cur = pl.BlockSpec((tm, tc), lambda i, j: (i, j))
    nxt = pl.BlockSpec((HALO, tc), lambda i, j: (jnp.minimum((i + 1) * (tm // HALO), T // HALO - 1), j))
    wsp = pl.BlockSpec((GDN_CONV, tc), lambda i, j: (0, j))
    return _pcall(body, name=name, grid=(nrow, C // tc), in_specs=[cur, nxt, wsp], out_specs=cur,
                  out_shape=jax.ShapeDtypeStruct((T, C), BF16),
                  compiler_params=_params(("parallel", "parallel")))(dy, dy, conv_w)


GATE_ROWS = 512


def _chunk_mask(n, lower):
    row = _iota2((n, n), 0)
    col = _iota2((n, n), 1)
    same = (row // GDN_CHUNK) == (col // GDN_CHUNK)
    tri = (row >= col) if lower else (row <= col)
    return (same & tri).astype(BF16)


def _gates_fwd(proj, a_log, dt_bias):
    T = proj.shape[0]
    tm = _pick(T, GATE_ROWS)
    c0 = (GDN_CONV_W + GDN_VW) // LANE

    def body(bl_ref, a_ref, alog_ref, dt_ref, beta_ref, g_ref, gc_ref):
        beta_ref[...] = _sigmoid(bl_ref[...])
        g = -jnp.exp(alog_ref[...]) * _softplus(a_ref[...] + dt_ref[...])
        g_ref[...] = g
        gc_ref[...] = _mdot2(_chunk_mask(tm, True), g)

    blk = lambda c: pl.BlockSpec((tm, LANE), lambda i: (i, c))
    vec = pl.BlockSpec((1, LANE), lambda i: (0, 0))
    sh = jax.ShapeDtypeStruct((T, LANE), F32)
    return _pcall(body, name="gdn_gates", grid=(T // tm,), in_specs=[blk(c0), blk(c0 + 1), vec, vec],
                  out_specs=[blk(0), blk(0), blk(0)], out_shape=[sh, sh, sh],
                  compiler_params=_params(("parallel",)))(proj, proj, a_log, dt_bias)


def _gates_bwd(proj, a_log, dt_bias, beta, g, dbeta, dgc):
    T = proj.shape[0]
    tm = _pick(T, GATE_ROWS)
    c0 = (GDN_CONV_W + GDN_VW) // LANE

    def heads_in_lanes(ref):
        lane = _iota2((tm, LANE), 1)
        out = jnp.where(lane < GDN_GROUP, ref[0], 0.0)
        for grp in range(1, GDN_V_HEADS // GDN_GROUP):
            out = out + jnp.where(lane // GDN_GROUP == grp, pltpu.roll(ref[grp], grp * GDN_GROUP, 1), 0.0)
        return out

    def body(a_ref, alog_ref, dt_ref, beta_ref, g_ref, dbeta_ref, dgc_ref, dbl_ref, da_ref, dalog_ref, ddt_ref):
        dg = _mdot2(_chunk_mask(tm, False), heads_in_lanes(dgc_ref))
        b = beta_ref[...]
        dbl_ref[...] = (heads_in_lanes(dbeta_ref) * b * (1.0 - b)).astype(BF16)
        da = dg * (-jnp.exp(alog_ref[...])) * _sigmoid(a_ref[...] + dt_ref[...])
        da_ref[...] = da.astype(BF16)
        p1 = jnp.sum(dg * g_ref[...], axis=0, keepdims=True)
        p2 = jnp.sum(da, axis=0, keepdims=True)

        @pl.when(pl.program_id(0) == 0)
        def _():
            dalog_ref[...] = p1
            ddt_ref[...] = p2

        @pl.when(pl.program_id(0) > 0)
        def _():
            dalog_ref[...] += p1
            ddt_ref[...] += p2

    blk = lambda c: pl.BlockSpec((tm, LANE), lambda i: (i, c))
    vec = pl.BlockSpec((1, LANE), lambda i: (0, 0))
    grp = pl.BlockSpec((GDN_V_HEADS // GDN_GROUP, tm, LANE), lambda i: (0, i, 0))
    shb = jax.ShapeDtypeStruct((T, LANE), BF16)
    shv = jax.ShapeDtypeStruct((1, LANE), F32)
    return _pcall(body, name="gdn_dgates", grid=(T // tm,),
                  in_specs=[blk(c0 + 1), vec, vec, blk(0), blk(0), grp, grp],
                  out_specs=[blk(0), blk(0), vec, vec], out_shape=[shb, shb, shv, shv],
                  compiler_params=_params(("arbitrary",)))(proj, a_log, dt_bias, beta, g, dbeta, dgc)


def _inv_unit_lower(Ls):
    C = Ls[0].shape[0]
    row = _iota2((C, C), 0)
    col = _iota2((C, C), 1)
    blk16 = (row // 16) == (col // 16)
    blk32 = (row // 32) == (col // 32)
    eye = (row == col).astype(F32)
    xs = [-jnp.where(blk16, L, 0.0) for L in Ls]
    inv = [eye + x for x in xs]
    for _ in range(3):
        xs = [_dot3(x, x) for x in xs]
        inv = [a + _dot3(a, x) for a, x in zip(inv, xs)]
    for mask in (blk32 & ~blk16, ~blk32):
        t = [_dot3(a, jnp.where(mask, L, 0.0)) for a, L in zip(inv, Ls)]
        inv = [a - _dot3(ti, a) for a, ti in zip(inv, t)]
    return inv


GDN_GROUP = 4
GDN_PREP_CHUNKS = 4


def _gdn_specs(T):
    C, D, E = GDN_CHUNK, GDN_HEAD_DIM, GDN_GROUP
    n = T // C
    qk = pl.BlockSpec((C, (E // 2) * D), lambda h, i: (i, h))
    vE = pl.BlockSpec((C, E * D), lambda h, i: (i, h))
    colv = pl.BlockSpec((C, LANE), lambda h, i: (i, 0))
    colo = pl.BlockSpec((None, C, LANE), lambda h, i: (h, i, 0))
    rowv = pl.BlockSpec((E, None, 1, C), lambda h, i: (h, i, 0, 0))
    st = pl.BlockSpec((E, None, D, D), lambda h, i: (h, i, 0, 0))
    am = pl.BlockSpec((E, None, C, C), lambda h, i: (h, i, 0, 0))
    return n, qk, vE, colv, colo, rowv, st, am


def _lane_col(blk, lane):
    return jnp.sum(jnp.where(_iota2(blk.shape, 1) == lane, blk, 0.0), axis=1, keepdims=True)


def _gdn_decay(gcol, grow):
    C = GDN_CHUNK
    row = _iota2((C, C), 0)
    col = _iota2((C, C), 1)
    incl = row >= col
    dm = jnp.where(incl, jnp.exp(jnp.where(incl, gcol - grow, 0.0)), 0.0)
    glast = grow[:, C - 1:C]
    return dm, jnp.exp(gcol), jnp.exp(glast), jnp.exp(glast - gcol), row > col, incl


def _gdn_prep(k, beta, gcol, grow):
    T = k.shape[0]
    C, D, B = GDN_CHUNK, GDN_HEAD_DIM, GDN_PREP_CHUNKS
    n = T // C

    def body(k_ref, b_ref, gc_ref, gr_ref, a_ref):
        idx = [(e, cb) for e in range(2) for cb in range(B)]
        kc = {cb: k_ref[cb * C:(cb + 1) * C, :] for cb in range(B)}
        lm = []
        head0 = 2 * pl.program_id(0)
        for e, cb in idx:
            beta = _lane_col(b_ref[cb * C:(cb + 1) * C, :], head0 + e)
            dm, _, _, _, strict, _ = _gdn_decay(_lane_col(gc_ref[cb * C:(cb + 1) * C, :], head0 + e), gr_ref[e, cb])
            lm.append(jnp.where(strict, _bdot(kc[cb] * beta, kc[cb], NT) * dm, 0.0))
        inv = _inv_unit_lower(lm)
        for (e, cb), a in zip(idx, inv):
            a_ref[e, cb] = a

    return _pcall(
        body, name="gdn_prep", grid=(GDN_K_HEADS, n // B),
        in_specs=[pl.BlockSpec((B * C, D), lambda h, i: (i, h)), pl.BlockSpec((B * C, LANE), lambda h, i: (i, 0)),
                  pl.BlockSpec((B * C, LANE), lambda h, i: (i, 0)), pl.BlockSpec((2, B, 1, C), lambda h, i: (h, i, 0, 0))],
        out_specs=pl.BlockSpec((2, B, C, C), lambda h, i: (h, i, 0, 0)),
        out_shape=jax.ShapeDtypeStruct((GDN_V_HEADS, n, C, C), F32),
        compiler_params=_params(("parallel", "parallel")),
    )(k, beta, gcol, grow)


def _gdn_fwd(q, k, v, beta, gcol, grow, amat):
    T = q.shape[0]
    C, D, E = GDN_CHUNK, GDN_HEAD_DIM, GDN_GROUP
    n, qk, vE, colv, colo, rowv, st, am = _gdn_specs(T)
    R = range(E)

    def body(q_ref, k_ref, v_ref, b_ref, gc_ref, gr_ref, a_ref, o_ref, s_ref, vn_ref, state):
        @pl.when(pl.program_id(1) == 0)
        def _():
            state[...] = jnp.zeros_like(state)

        qv = [q_ref[:, (e // 2) * D:(e // 2 + 1) * D] for e in R]
        kv = [k_ref[:, (e // 2) * D:(e // 2 + 1) * D] for e in R]
        vv = [v_ref[:, e * D:(e + 1) * D] for e in R]
        head0 = E * pl.program_id(0)
        beta = [_lane_col(b_ref[...], head0 + e) for e in R]
        a = [a_ref[e] for e in R]
        s = [state[e] for e in R]
        dec = [_gdn_decay(_lane_col(gc_ref[...], head0 + e), gr_ref[e]) for e in R]
        pm = [_bdot(qv[e], kv[e], NT) * dec[e][0] for e in R]
        r = [beta[e] * (vv[e] - _bdot(kv[e] * dec[e][1], s[e])) for e in R]
        vn = [_dot3(a[e], r[e]) for e in R]
        o = [_bdot(qv[e] * dec[e][1], s[e]) + _bdot(pm[e], vn[e]) for e in R]
        s2 = [dec[e][2] * s[e] + _bdot(kv[e] * dec[e][3], vn[e], TN) for e in R]
        for e in R:
            s_ref[e] = s[e]
            vn_ref[:, e * D:(e + 1) * D] = vn[e]
            o_ref[:, e * D:(e + 1) * D] = o[e]
            state[e] = s2[e]

    shv = jax.ShapeDtypeStruct((T, GDN_V_HEADS * D), F32)
    return _pcall(
        body, name="gdn_fwd", grid=(GDN_V_HEADS // E, n), in_specs=[qk, qk, vE, colv, colv, rowv, am],
        out_specs=[vE, st, vE],
        out_shape=[shv, jax.ShapeDtypeStruct((GDN_V_HEADS, n, D, D), F32), shv],
        scratch_shapes=[pltpu.VMEM((E, D, D), F32)],
        compiler_params=_params(("parallel", "arbitrary")),
    )(q, k, v, beta, gcol, grow, amat)


def _gdn_bwd(q, k, v, beta, gcol, grow, states, amat, vnew, do):
    T = q.shape[0]
    C, D, E = GDN_CHUNK, GDN_HEAD_DIM, GDN_GROUP
    n, qk, vE, colv, colo, rowv, st, am = _gdn_specs(T)
    rev = lambda spec: pl.BlockSpec(spec.block_shape, (lambda f: (lambda h, i: f(h, n - 1 - i)))(spec.index_map))
    qk, vE, colv, colo, rowv, st, am = (rev(s) for s in (qk, vE, colv, colo, rowv, st, am))
    R = range(E)

    def body(q_ref, k_ref, v_ref, b_ref, gc_ref, gr_ref, s_ref, a_ref, vn_ref, do_ref,
             dq_ref, dk_ref, dv_ref, db_ref, dgc_ref, dstate):
        @pl.when(pl.program_id(1) == 0)
        def _():
            dstate[...] = jnp.zeros_like(dstate)

        M = lambda f: [f(e) for e in R]
        rsum = lambda x: jnp.sum(x, axis=1, keepdims=True)
        qv = M(lambda e: q_ref[:, (e // 2) * D:(e // 2 + 1) * D])
        kv = M(lambda e: k_ref[:, (e // 2) * D:(e // 2 + 1) * D])
        vv = M(lambda e: v_ref[:, e * D:(e + 1) * D])
        vn = M(lambda e: vn_ref[:, e * D:(e + 1) * D])
        dov = M(lambda e: do_ref[:, e * D:(e + 1) * D])
        head0 = E * pl.program_id(0)
        beta = M(lambda e: _lane_col(b_ref[...], head0 + e))
        s = M(lambda e: s_ref[e])
        a = M(lambda e: a_ref[e])
        dsn = M(lambda e: dstate[e])
        dec = M(lambda e: _gdn_decay(_lane_col(gc_ref[...], head0 + e), gr_ref[e]))
        dm, gam, glast, tail = (M(lambda e: dec[e][i]) for i in range(4))
        strict, incl = dec[0][4], dec[0][5]
        kb = M(lambda e: kv[e] * beta[e])
        kd = M(lambda e: kv[e] * gam[e])
        qd = M(lambda e: qv[e] * gam[e])
        kt = M(lambda e: kv[e] * tail[e])
        lmat = M(lambda e: jnp.where(strict, _bdot(kb[e], kv[e], NT) * dm[e], 0.0))
        pmat = M(lambda e: _bdot(qv[e], kv[e], NT) * dm[e])
        xres = M(lambda e: vv[e] - _bdot(kd[e], s[e]))
        dvn = M(lambda e: _bdot(pmat[e], dov[e], TN) + _bdot(kt[e], dsn[e]))
        dqd = M(lambda e: _bdot(dov[e], s[e], NT))
        dp = M(lambda e: jnp.where(incl, _bdot(dov[e], vn[e], NT), 0.0))
        dkt = M(lambda e: _bdot(vn[e], dsn[e], NT))
        dr = M(lambda e: _dot3(a[e], dvn[e], TN))
        drb = M(lambda e: beta[e] * dr[e])
        dkd = M(lambda e: -_bdot(drb[e], s[e], NT))
        ds2 = M(lambda e: _bdot(qd[e], dov[e], TN) + glast[e] * dsn[e] - _bdot(kd[e], drb[e], TN))
        dl = M(lambda e: -jnp.where(strict, _bdot(dr[e], vn[e], NT), 0.0))
        dmm = M(lambda e: dl[e] * dm[e])
        dnn = M(lambda e: dp[e] * dm[e])
        emat = M(lambda e: dl[e] * lmat[e] + dp[e] * pmat[e])
        dkb = M(lambda e: _bdot(dmm[e], kv[e]))
        dk = M(lambda e: beta[e] * dkb[e] + _bdot(dmm[e], kb[e], TN) + _bdot(dnn[e], qv[e], TN)
               + gam[e] * dkd[e] + tail[e] * dkt[e])
        dq = M(lambda e: _bdot(dnn[e], kv[e]) + gam[e] * dqd[e])
        dbeta = M(lambda e: rsum(dr[e] * xres[e]) + rsum(dkb[e] * kv[e]))
        ones = jnp.ones((C, LANE), BF16)
        colsum = M(lambda e: _dot2m(emat[e], ones, TN)[:, :1])
        tails = M(lambda e: rsum(dkt[e] * kt[e]))
        lastrow = _iota2((C, 1), 0) == C - 1
        dlast = M(lambda e: jnp.sum(tails[e], axis=0, keepdims=True)
                  + glast[e] * jnp.sum(rsum(s[e] * dsn[e]), axis=0, keepdims=True))
        dgc = M(lambda e: rsum(emat[e]) - colsum[e] + rsum(dkd[e] * kd[e]) + rsum(dqd[e] * qd[e]) - tails[e]
                + jnp.where(lastrow, dlast[e], 0.0))
        lane = _iota2((C, LANE), 1)
        db_all = jnp.zeros((C, LANE), F32)
        dgc_all = jnp.zeros((C, LANE), F32)
        for e in R:
            dv_ref[:, e * D:(e + 1) * D] = drb[e]
            db_all = jnp.where(lane == e, dbeta[e], db_all)
            dgc_all = jnp.where(lane == e, dgc[e], dgc_all)
            dstate[e] = ds2[e]
        db_ref[...] = db_all
        dgc_ref[...] = dgc_all
        for kh in range(E // 2):
            dq_ref[:, kh * D:(kh + 1) * D] = dq[2 * kh] + dq[2 * kh + 1]
            dk_ref[:, kh * D:(kh + 1) * D] = dk[2 * kh] + dk[2 * kh + 1]

    shq = jax.ShapeDtypeStruct((T, GDN_K_HEADS * D), F32)
    shv = jax.ShapeDtypeStruct((T, GDN_V_HEADS * D), F32)
    shc = jax.ShapeDtypeStruct((GDN_V_HEADS // E, T, LANE), F32)
    return _pcall(
        body, name="gdn_bwd", grid=(GDN_V_HEADS // E, n),
        in_specs=[qk, qk, vE, colv, colv, rowv, st, am, vE, vE],
        out_specs=[qk, qk, vE, colo, colo], out_shape=[shq, shq, shv, shc, shc],
        scratch_shapes=[pltpu.VMEM((E, D, D), F32)],
        compiler_params=_params(("parallel", "arbitrary")),
    )(q, k, v, beta, gcol, grow, states, amat, vnew, do)


def _outgate_fwd(o, proj, gain):
    T = o.shape[0]
    tm, tc = _pick(T, CONV_ROWS), CONV_COLS
    z0 = GDN_CONV_W // tc

    def body(o_ref, z_ref, g_ref, y_ref):
        z = z_ref[...]
        sz = z * _sigmoid(z)
        parts = []
        for hh in range(tc // GDN_HEAD_DIM):
            oh = o_ref[:, hh * GDN_HEAD_DIM:(hh + 1) * GDN_HEAD_DIM]
            r = lax.rsqrt(jnp.mean(oh * oh, axis=-1, keepdims=True) + EPS)
            parts.append(oh * r * g_ref[...])
        y_ref[...] = (jnp.concatenate(parts, axis=1) * sz).astype(BF16)

    blk = pl.BlockSpec((tm, tc), lambda i, j: (i, j))
    return _pcall(body, name="gdn_outgate", grid=(T // tm, GDN_VW // tc),
                  in_specs=[blk, pl.BlockSpec((tm, tc), lambda i, j: (i, j + z0)), pl.BlockSpec((1, GDN_HEAD_DIM), lambda i, j: (0, 0))],
                  out_specs=blk, out_shape=jax.ShapeDtypeStruct((T, GDN_VW), BF16),
                  compiler_params=_params(("parallel", "parallel")))(o, proj, gain)


def _outgate_bwd(dy, o, proj, gain):
    T = o.shape[0]
    tm, tc = _pick(T, CONV_ROWS), CONV_COLS
    z0 = GDN_CONV_W // tc
    nh = tc // GDN_HEAD_DIM

    def body(dy_ref, o_ref, z_ref, g_ref, do_ref, dz_ref, dg_ref):
        z = z_ref[...]
        sg = _sigmoid(z)
        sz = z * sg
        dy = dy_ref[...]
        dgain = jnp.zeros((1, GDN_HEAD_DIM), F32)
        dos, ys = [], []
        for hh in range(nh):
            sl = slice(hh * GDN_HEAD_DIM, (hh + 1) * GDN_HEAD_DIM)
            oh = o_ref[:, sl]
            r = lax.rsqrt(jnp.mean(oh * oh, axis=-1, keepdims=True) + EPS)
            xh = oh * r
            dn = dy[:, sl] * sz[:, sl]
            dgain = dgain + jnp.sum(dn * xh, axis=0, keepdims=True)
            dxh = dn * g_ref[...]
            dos.append(r * (dxh - xh * jnp.mean(dxh * xh, axis=-1, keepdims=True)))
            ys.append(xh * g_ref[...])
        do_ref[...] = jnp.concatenate(dos, axis=1)
        dz_ref[...] = (dy * jnp.concatenate(ys, axis=1) * sg * (1.0 + z * (1.0 - sg))).astype(BF16)
        first = (pl.program_id(0) == 0) & (pl.program_id(1) == 0)

        @pl.when(first)
        def _():
            dg_ref[...] = dgain

        @pl.when(jnp.logical_not(first))
        def _():
            dg_ref[...] += dgain

    blk = pl.BlockSpec((tm, tc), lambda i, j: (i, j))
    vec = pl.BlockSpec((1, GDN_HEAD_DIM), lambda i, j: (0, 0))
    return _pcall(body, name="gdn_doutgate", grid=(T // tm, GDN_VW // tc),
                  in_specs=[blk, blk, pl.BlockSpec((tm, tc), lambda i, j: (i, j + z0)), vec],
                  out_specs=[blk, blk, vec],
                  out_shape=[jax.ShapeDtypeStruct((T, GDN_VW), F32), jax.ShapeDtypeStruct((T, GDN_VW), BF16),
                             jax.ShapeDtypeStruct((1, GDN_HEAD_DIM), F32)],
                  compiler_params=_params(("arbitrary", "arbitrary")))(dy, o, proj, gain)


def _pad_lanes(vec):
    return jnp.pad(vec.reshape(1, -1), ((0, 0), (0, LANE - vec.shape[-1])))


def _head_rows(a):
    T = a.shape[0]
    return a[:, :GDN_V_HEADS].T.reshape(GDN_V_HEADS, T // GDN_CHUNK, 1, GDN_CHUNK)


def _gdn_pad_in(w_in):
    c = GDN_CONV_W + GDN_VW
    z = jnp.zeros(w_in.shape[:-1] + (LANE - GDN_V_HEADS,), w_in.dtype)
    return jnp.concatenate([w_in[..., :c + GDN_V_HEADS], z, w_in[..., c + GDN_V_HEADS:], z], axis=-1)


def _gdn_unpad_in(dw):
    c = GDN_CONV_W + GDN_VW
    return jnp.concatenate([dw[..., :c + GDN_V_HEADS], dw[..., c + LANE:c + LANE + GDN_V_HEADS]], axis=-1)


def _gdn_mixer_fwd(h, g, w_in_pad, conv_w, a_log, dt_bias, out_gain, w_out):
    T = h.shape[0]
    hn = _rms_fwd(h, g, "gdn_norm")
    proj = _mm(hn, w_in_pad, "nn", name="gdn_in")
    qk = _conv_fwd(proj, conv_w, 0, 2 * GDN_KW, True, "gdn_conv_qk")
    vv = _conv_fwd(proj, conv_w, 2 * GDN_KW, GDN_VW, False, "gdn_conv_v")
    alog, dtb = _pad_lanes(a_log), _pad_lanes(dt_bias)
    beta, gl, gc = _gates_fwd(proj, alog, dtb)
    grow = _head_rows(gc)
    qn, kn = qk[:, :GDN_KW], qk[:, GDN_KW:]
    amat = _gdn_prep(kn, beta, gc, grow)
    o, states, vnew = _gdn_fwd(qn, kn, vv, beta, gc, grow, amat)
    gain = out_gain.reshape(1, GDN_HEAD_DIM)
    y = _outgate_fwd(o, proj, gain)
    h2 = _mm(y, w_out, "nn", res=h, name="gdn_out")
    return h2, (h, hn, proj, qn, kn, vv, beta, gl, gc, grow, o, states, amat, vnew, y, alog, dtb, gain)


def _gdn_mixer_bwd(dh2, saved, g, w_in_pad, conv_w, w_out):
    h, hn, proj, qn, kn, vv, beta, gl, gc, grow, o, states, amat, vnew, y, alog, dtb, gain = saved
    T = h.shape[0]
    dy = _mm(dh2, w_out, "nt", name="gdn_dy")
    dw_out = _mm(y, dh2, "tn", out_dtype=BF16, name="gdn_dwout")
    do, dz, dgain = _outgate_bwd(dy, o, proj, gain)
    dq, dk, dv, dbeta, dgc = _gdn_bwd(qn, kn, vv, beta, gc, grow, states, amat, vnew, do)
    dqk = jnp.concatenate([dq, dk], axis=1)
    dy_qk, dcw_qk = _conv_bwd_pre(proj, conv_w, dqk, 0, 2 * GDN_KW, True, "gdn_dconv_qk")
    dy_v, dcw_v = _conv_bwd_pre(proj, conv_w, dv, 2 * GDN_KW, GDN_VW, False, "gdn_dconv_v")
    dx_qk = _conv_bwd_in(dy_qk, conv_w[:, :2 * GDN_KW], "gdn_dconvin_qk")
    dx_v = _conv_bwd_in(dy_v, conv_w[:, 2 * GDN_KW:], "gdn_dconvin_v")
    dbl, da, dalog, ddt = _gates_bwd(proj, alog, dtb, beta, gl, dbeta, dgc)
    dproj = jnp.concatenate([dx_qk, dx_v, dz, dbl, da], axis=1)
    dw_in_pad = _mm(hn, dproj, "tn", out_dtype=BF16, name="gdn_dwin")
    dhn = _mm(dproj, w_in_pad, "nt", name="gdn_dhn")
    dh, dg = _rms_bwd(dhn, h, g, dh2, "gdn_dnorm")
    dconv = jnp.concatenate([dcw_qk, dcw_v], axis=1)
    return (dh, dg, _gdn_unpad_in(dw_in_pad), dconv, dalog[0, :GDN_V_HEADS], ddt[0, :GDN_V_HEADS],
            dgain.reshape(GDN_HEAD_DIM), dw_out)


def _instances(full):
    out = {}
    for n, a in full.items():
        if n.startswith("ffn_"):
            for i in range(2):
                for j in range(2):
                    out[(n, i, j)] = a[i, j]
        elif n in ("mix_norm", "ple_norm", "ple_w_gate", "ple_w_proj"):
            for i in range(2):
                out[(n, i)] = a[i]
        else:
            out[(n,)] = a[0]
    return out


def _stacked(inst):
    out = {}
    for n in dict.fromkeys(k[0] for k in inst):
        if n.startswith("ffn_"):
            out[n] = jnp.stack([jnp.stack([inst[(n, i, j)] for j in range(2)]) for i in range(2)])
        elif n in ("mix_norm", "ple_norm", "ple_w_gate", "ple_w_proj"):
            out[n] = jnp.stack([inst[(n, i)] for i in range(2)])
        else:
            out[n] = inst[(n,)][None]
    return out


def _local_step(x, p, target, w, late_shards=(), late_weights=None, early_grads=None, first_shards=(), first_weights=None):
    w = dict(w)
    ffn = lambda i, j: (w[("ffn_norm", i, j)], w[("ffn_w_gate", i, j)], w[("ffn_w_up", i, j)], w[("ffn_w_down", i, j)])
    h = x
    tape = []
    for i in range(2):
        if i == 0 and first_weights is not None:
            def wd_of(gathered):
                w.update(first_weights(gathered))
                return w[("ffn_w_down", 0, 0)]
            h, s1 = _ffn_fwd(h, w[("ffn_norm", 0, 0)], w[("ffn_w_gate", 0, 0)], w[("ffn_w_up", 0, 0)], None, "ffn0a",
                             first_shards, wd_of)
        else:
            h, s1 = _ffn_fwd(h, *ffn(i, 0), f"ffn{i}a")
        if i == 0:
            def w_out_of(gathered):
                if late_weights is not None:
                    w.update(late_weights(gathered))
                return w[("att_w_out",)]
            h, s2, _ = _att_fwd(h, w[("mix_norm", 0)], w[("att_w_in",)], w_out_of, w[("att_q_norm",)],
                                w[("att_k_norm",)], w[("att_sinks",)], late_shards)
        else:
            gdn_in_pad = _gdn_pad_in(w[("gdn_w_in",)])
            h, s2 = _gdn_mixer_fwd(h, w[("mix_norm", 1)], gdn_in_pad, w[("gdn_conv_w",)], w[("gdn_a_log",)],
                                   w[("gdn_dt_bias",)], w[("gdn_out_norm",)], w[("gdn_w_out",)])
        h, s3 = _ffn_fwd(h, *ffn(i, 1), f"ffn{i}b")
        h, s4 = _ple_fwd(h, p[i], w[("ple_norm", i)], w[("ple_w_gate", i)], w[("ple_w_proj", i)], f"ple{i}")
        tape.append((s1, s2, s3, s4))

    loss, dh = _loss_head(h, target)

    g = {}
    rode = []
    for i in (1, 0):
        s1, s2, s3, s4 = tape[i]
        dh, g[("ple_norm", i)], g[("ple_w_gate", i)], g[("ple_w_proj", i)] = _ple_bwd(
            dh, s4, p[i], w[("ple_norm", i)], w[("ple_w_gate", i)], f"ple{i}")
        dh, g[("ffn_norm", i, 1)], g[("ffn_w_gate", i, 1)], g[("ffn_w_up", i, 1)], g[("ffn_w_down", i, 1)] = _ffn_bwd(
            dh, s3, *ffn(i, 1), f"ffn{i}b")
        if i == 0:
            ride = early_grads(g) if early_grads is not None else ()
            (dh, g[("mix_norm", 0)], g[("att_w_in",)], g[("att_w_out",)], g[("att_q_norm",)], g[("att_k_norm",)],
             g[("att_sinks",)], rode) = _att_bwd(dh, s2, w[("mix_norm", 0)], w[("att_w_in",)], w[("att_w_out",)], ride)
        else:
            (dh, g[("mix_norm", 1)], g[("gdn_w_in",)], g[("gdn_conv_w",)], g[("gdn_a_log",)], g[("gdn_dt_bias",)],
             g[("gdn_out_norm",)], g[("gdn_w_out",)]) = _gdn_mixer_bwd(
                dh, s2, w[("mix_norm", 1)], gdn_in_pad, w[("gdn_conv_w",)], w[("gdn_w_out",)])
        dh, g[("ffn_norm", i, 0)], g[("ffn_w_gate", i, 0)], g[("ffn_w_up", i, 0)], g[("ffn_w_down", i, 0)] = _ffn_bwd(
            dh, s1, *ffn(i, 0), f"ffn{i}a")
    return loss, dh, g, rode


MESH = pl.DeviceIdType.MESH


def _place():
    x, y, c = lax.axis_index("x"), lax.axis_index("y"), lax.axis_index("c")
    others = [((1 - x, y), 2 * (1 - x) + y), ((x, 1 - y), 2 * x + (1 - y)), ((1 - x, 1 - y), 2 * (1 - x) + (1 - y))]
    return x, y, c, 4 * x + 2 * y + c, 2 * x + y, (x, y, 1 - c), others


def _comm_call(body, arrays, out_shape, n_sems, name):
    hbm = pl.BlockSpec(memory_space=pl.ANY)
    n = len(arrays)
    return _pcall(
        body, name=name, in_specs=[hbm] * n, out_specs=[hbm] * len(out_shape), out_shape=out_shape,
        scratch_shapes=[pltpu.SemaphoreType.DMA((n, n_sems)), pltpu.SemaphoreType.DMA((n, n_sems)),
                        pltpu.SemaphoreType.DMA((n, N_CHIP))],
        compiler_params=pltpu.CompilerParams(has_side_effects=True),
    )(*arrays)


def _gather_protocol(ins, outs, send_sems, recv_sems, local_sems):
    n = len(ins)
    x, y, c, me, my_chip, sibling, others = _place()

    def copy(a, k, block, to, src=None):
        dst = outs[a].at[block]
        return pltpu.make_async_remote_copy(
            src_ref=dst if src is None else src, dst_ref=dst, send_sem=send_sems.at[a, k],
            recv_sem=recv_sems.at[a, k], device_id=to, device_id_type=MESH)

    local = [pltpu.make_async_copy(ins[a], outs[a].at[me], local_sems.at[a, 0]) for a in range(n)]
    first = []
    for a in range(n):
        first.append(copy(a, 0, me, sibling, src=ins[a]))
        first += [copy(a, 1 + j, me, (*chip, c), src=ins[a]) for j, (chip, _) in enumerate(others)]

    def start():
        for cp in local + first:
            cp.start()

    def finish():
        passed = []
        for a in range(n):
            for j, (chip, chip_idx) in enumerate(others):
                blk = 2 * chip_idx + c
                copy(a, 1 + j, blk, (x, y, c)).wait_recv()
                fwd = copy(a, 4 + j, blk, sibling)
                fwd.start()
                passed.append(fwd)
        for a in range(n):
            copy(a, 0, 2 * my_chip + (1 - c), (x, y, c)).wait_recv()
            for j, (chip, chip_idx) in enumerate(others):
                copy(a, 4 + j, 2 * chip_idx + (1 - c), (x, y, c)).wait_recv()
        for cp in first + passed:
            cp.wait_send()
        for cp in local:
            cp.wait()

    return start, finish


def _all_gather(arrays):
    n = len(arrays)

    def body(*refs):
        start, finish = _gather_protocol(refs[:n], refs[n:2 * n], *refs[2 * n:])
        start()
        finish()

    out_shape = [jax.ShapeDtypeStruct((N_DEV,) + a.shape, a.dtype) for a in arrays]
    return _comm_call(body, arrays, out_shape, N_DEV - 1, "gather_weights")


def _exchange_sibling(arrays, name):
    n = len(arrays)

    def body(*refs):
        ins, got = refs[:n], refs[n:2 * n]
        send_sems, recv_sems, _ = refs[2 * n:]
        x, y, c, me, my_chip, sibling, others = _place()
        remote = []
        for a in range(n):
            for chip in range(N_CHIP):
                rc = pltpu.make_async_remote_copy(
                    src_ref=ins[a].at[2 * chip + (1 - c)], dst_ref=got[a].at[chip], send_sem=send_sems.at[a, chip],
                    recv_sem=recv_sems.at[a, chip], device_id=sibling, device_id_type=MESH)
                rc.start()
                remote.append(rc)
        for rc in remote:
            rc.wait()

    half = [jax.ShapeDtypeStruct((N_CHIP,) + a.shape[1:], a.dtype) for a in arrays]
    return _comm_call(body, arrays, half, N_CHIP, name)


def _chips_protocol(ins, outs, send_sems, recv_sems, local_sems):
    n = len(ins)
    x, y, c, me, my_chip, sibling, others = _place()
    local = [pltpu.make_async_copy(ins[a].at[my_chip], outs[a].at[my_chip], local_sems.at[a, 0]) for a in range(n)]
    remote = [pltpu.make_async_remote_copy(
        src_ref=ins[a].at[chip_idx], dst_ref=outs[a].at[my_chip], send_sem=send_sems.at[a, j],
        recv_sem=recv_sems.at[a, j], device_id=(*chip, c), device_id_type=MESH)
        for a in range(n) for j, (chip, chip_idx) in enumerate(others)]

    def start():
        for cp in local + remote:
            cp.start()

    def finish():
        for cp in remote + local:
            cp.wait()

    return start, finish


def _exchange_chips(arrays, name):
    n = len(arrays)

    def body(*refs):
        start, finish = _chips_protocol(refs[:n], refs[n:2 * n], *refs[2 * n:])
        start()
        finish()

    out_shape = [jax.ShapeDtypeStruct(a.shape, a.dtype) for a in arrays]
    return _comm_call(body, arrays, out_shape, N_CHIP - 1, name)


def _as_rows(a, lead):
    shp = a.shape
    return a.reshape(shp[:lead] + (math.prod(shp[lead:-1]), shp[-1]))


def _row_tile(rows, cap=512):
    if rows <= cap:
        return rows
    for t in range(cap - cap % 8, 0, -8):
        if rows % t == 0:
            return t
    return rows


def _pair_sum(send, got, name):
    a3, b3 = _as_rows(send, 1), _as_rows(got, 1)
    _, rows, last = b3.shape
    tr = _row_tile(rows)

    def body(c_ref, a_ref, b_ref, o_ref):
        o_ref[...] = (a_ref[...].astype(F32) + b_ref[...].astype(F32)).astype(o_ref.dtype)

    core = lax.axis_index("c").astype(jnp.int32).reshape(1)
    out = _pcall(
        body, name=name,
        grid_spec=pltpu.PrefetchScalarGridSpec(
            num_scalar_prefetch=1, grid=(N_CHIP, rows // tr),
            in_specs=[pl.BlockSpec((None, tr, last), lambda k, i, c_ref: (2 * k + c_ref[0], i, 0)),
                      pl.BlockSpec((None, tr, last), lambda k, i, c_ref: (k, i, 0))],
            out_specs=pl.BlockSpec((None, tr, last), lambda k, i, c_ref: (k, i, 0))),
        out_shape=jax.ShapeDtypeStruct(b3.shape, got.dtype), compiler_params=_params(("parallel", "parallel")),
    )(core, a3, b3)
    return out.reshape(got.shape)


def _adamw(parts, w, m, v, name):
    lead, (rows, last) = w.shape[:-2], w.shape[-2:]
    nl = len(lead)
    tr = _row_tile(rows)
    c1 = 1.0 / (1.0 - ADAM_B1 ** ADAM_STEP)
    c2 = 1.0 / (1.0 - ADAM_B2 ** ADAM_STEP)

    def body(p_ref, w_ref, m_ref, v_ref, g_ref, d_ref, nm_ref, nv_ref):
        g = p_ref[0].astype(F32)
        for chip in range(1, N_CHIP):
            g = g + p_ref[chip].astype(F32)
        mn = ADAM_B1 * m_ref[...] + (1.0 - ADAM_B1) * g
        vn = ADAM_B2 * v_ref[...] + (1.0 - ADAM_B2) * (g * g)
        g_ref[...] = g
        nm_ref[...] = mn
        nv_ref[...] = vn
        d_ref[...] = -ADAM_LR * ((mn * c1) / (jnp.sqrt(vn * c2) + ADAM_EPS) + ADAM_WD * w_ref[...])

    row = pl.BlockSpec((None,) * nl + (tr, last), lambda *ix: ix + (0,))
    part = pl.BlockSpec((N_CHIP,) + (None,) * nl + (tr, last), lambda *ix: (0,) + ix + (0,))
    sh = jax.ShapeDtypeStruct(w.shape, F32)
    return _pcall(body, name=name, grid=lead + (rows // tr,), in_specs=[part, row, row, row],
                  out_specs=[row, row, row, row], out_shape=[sh, sh, sh, sh],
                  compiler_params=_params(("parallel",) * (nl + 1)))(parts, w, m, v)


def _pack(pieces, row_align):
    rows, offs, r = [], [], 0
    for a in pieces:
        flat = a.reshape(-1)
        nr = -(-flat.shape[0] // PACK_W)
        flat = jnp.pad(flat, (0, nr * PACK_W - flat.shape[0]))
        rows.append(flat.reshape(nr, PACK_W))
        offs.append(r)
        r += nr
    pad = (-r) % row_align
    if pad:
        rows.append(jnp.zeros((pad, PACK_W), pieces[0].dtype))
    return jnp.concatenate(rows, axis=0), offs


def _unpack(flat, offs, shapes):
    out = []
    for off, shp in zip(offs, shapes):
        size = math.prod(shp)
        nr = -(-size // PACK_W)
        out.append(flat[..., off:off + nr, :].reshape(flat.shape[:-2] + (nr * PACK_W,))[..., :size].reshape(flat.shape[:-2] + tuple(shp)))
    return out


def _to_full(gathered, axis):
    z = jnp.moveaxis(gathered, 0, axis)
    shp = list(z.shape)
    return z.reshape(shp[:axis] + [shp[axis] * shp[axis + 1]] + shp[axis + 2:])


def _to_shards(full, axis):
    shp = list(full.shape)
    z = full.reshape(shp[:axis] + [N_DEV, shp[axis] // N_DEV] + shp[axis + 1:])
    return jnp.moveaxis(z, axis, 0)


def kernel(x, p, ffn_norm, ffn_w_gate, ffn_w_up, ffn_w_down, mix_norm, att_w_in, att_q_norm, att_k_norm, att_sinks, att_w_out, gdn_w_in, gdn_conv_w, gdn_a_log, gdn_dt_bias, gdn_out_norm, gdn_w_out, ple_norm, ple_w_gate, ple_w_proj, loss_target, m_ffn_norm, m_ffn_w_gate, m_ffn_w_up, m_ffn_w_down, m_mix_norm, m_att_w_in, m_att_q_norm, m_att_k_norm, m_att_sinks, m_att_w_out, m_gdn_w_in, m_gdn_conv_w, m_gdn_a_log, m_gdn_dt_bias, m_gdn_out_norm, m_gdn_w_out, m_ple_norm, m_ple_w_gate, m_ple_w_proj, v_ffn_norm, v_ffn_w_gate, v_ffn_w_up, v_ffn_w_down, v_mix_norm, v_att_w_in, v_att_q_norm, v_att_k_norm, v_att_sinks, v_att_w_out, v_gdn_w_in, v_gdn_conv_w, v_gdn_a_log, v_gdn_dt_bias, v_gdn_out_norm, v_gdn_w_out, v_ple_norm, v_ple_w_gate, v_ple_w_proj):
    args = dict(locals())
    wts = {n: args[n] for n in WEIGHTS}
    mom = {n: args["m_" + n] for n in WEIGHTS}
    var = {n: args["v_" + n] for n in WEIGHTS}
    axis = dict(SHARDED)
    vecs = [n for n, _ in SHARDED[:SMALL_SHARDED]]
    small = vecs + list(REPLICATED)
    small_shapes = [wts[n].shape for n in small]
    lead = lambda n: 2 if n.startswith("ffn_") else 1

    def stack_of(arrays, name, idxs):
        return jnp.stack([arrays[name][idx] if idx else arrays[name][0] for idx in idxs])

    def full_instances(gathered, group):
        out = {}
        for (name, idxs), g in zip(group, gathered):
            whole = _to_full(g, axis[name] - lead(name) + 1)
            for k, idx in enumerate(idxs):
                out[(name,) + idx] = whole[k]
        return out

    def shard_stacks(g, group):
        return [_to_shards(jnp.stack([g[(name,) + idx] for idx in idxs]), axis[name] - lead(name) + 1)
                for name, idxs in group]

    vec_pack, voffs = _pack([wts[n] for n in vecs], 8)
    early = _all_gather([stack_of(wts, n, idxs).astype(BF16) for n, idxs in EARLY] + [vec_pack])
    w = full_instances(early[:-1], EARLY)
    vec_full = {n: _to_full(piece, axis[n]) for n, piece in
                zip(vecs, _unpack(early[-1], voffs, [wts[n].shape for n in vecs]))}
    w.update(_instances({**vec_full, **{n: wts[n] for n in REPLICATED}}))
    first_shards = [stack_of(wts, n, idxs).astype(BF16) for n, idxs in FIRST]
    late_shards = [stack_of(wts, n, idxs).astype(BF16) for n, idxs in LATE]

    def early_grads(g):
        send = shard_stacks(g, RIDE)
        got = _exchange_sibling(send, "exchange_sibling_early")
        return [_pair_sum(p_, q_, f"pair_sum_early_{i}") for i, (p_, q_) in enumerate(zip(send, got))]

    loss, grad_x, g, rode = _local_step(x[0], p[:, 0], loss_target[0], w, late_shards,
                                        lambda gathered: full_instances(gathered, LATE), early_grads,
                                        first_shards, lambda gathered: full_instances(gathered, FIRST))

    gs = _stacked({k: v for k, v in g.items() if k[0] in small})
    vec_shards = [_to_shards(gs[n], axis[n]) for n in vecs]
    small_send = jnp.stack([_pack([sh[d] for sh in vec_shards] + [gs[n] for n in REPLICATED] + [loss.reshape(1)], 8)[0]
                            for d in range(N_DEV)])
    send = shard_stacks(g, FINAL) + [small_send]
    got = _exchange_sibling(send, "exchange_sibling_final")
    chip_sums = [_pair_sum(p_, q_, f"pair_sum_final_{i}") for i, (p_, q_) in enumerate(zip(send, got))]
    last = _exchange_chips(chip_sums, "exchange_chips_final")

    pieces = {}
    for (name, idxs), part in list(zip(RIDE, rode)) + list(zip(FINAL, last[:-1])):
        for k, idx in enumerate(idxs):
            pieces[(name,) + idx] = part[:, k]
    outs = {}
    for n, _ in SHARDED[SMALL_SHARDED:]:
        if lead(n) == 2:
            part = jnp.stack([jnp.stack([pieces[(n, i, j)] for j in range(2)], axis=1) for i in range(2)], axis=1)
        elif (n, 0) in pieces:
            part = jnp.stack([pieces[(n, i)] for i in range(2)], axis=1)
        else:
            part = pieces[(n,)][:, None]
        outs[n] = _adamw(part, wts[n], mom[n], var[n], f"adamw_{n}")
    filler = [jnp.zeros((1,), F32)]
    small_w, soffs = _pack([wts[n] for n in small] + filler, 8)
    small_m, _ = _pack([mom[n] for n in small] + filler, 8)
    small_v, _ = _pack([var[n] for n in small] + filler, 8)
    small_out = [_unpack(z, soffs, small_shapes + [(1,)]) for z in _adamw(last[-1], small_w, small_m, small_v, "adamw_small")]
    loss = small_out[0][-1][0]
    for i, n in enumerate(small):
        outs[n] = [small_out[k][i] for k in range(4)]
    result = [loss, grad_x[None]]
    for k in range(4):
        result += [outs[n][k] for n in WEIGHTS]
    return tuple(result)
```

```python
import math

import jax
import jax.numpy as jnp
from jax import lax
from jax.experimental import pallas as pl
from jax.experimental.pallas import tpu as pltpu

F32 = jnp.float32
BF16 = jnp.bfloat16

N_DEV = 8
N_CHIP = 4
D_MODEL = 1024
D_FF = 2816
PLE_DIM = 256
HEAD_DIM = 64
SB_HEADS = 8
SWA_HEADS = 8
SWA_KV_HEADS = 2
SWA_GROUP = SWA_HEADS // SWA_KV_HEADS
WINDOW = 128
Q_BLOCK = 128
GDN_K_HEADS = 8
GDN_V_HEADS = 16
GDN_HEAD_DIM = 128
GDN_CONV = 4
GDN_CHUNK = 64
EPS = 1e-6
SB_W = SB_HEADS * HEAD_DIM
SWA_QW = SWA_HEADS * HEAD_DIM
SWA_KVW = SWA_KV_HEADS * HEAD_DIM
ATT_IN = 3 * SB_W + SWA_QW + 2 * SWA_KVW
GDN_KW = GDN_K_HEADS * GDN_HEAD_DIM
GDN_VW = GDN_V_HEADS * GDN_HEAD_DIM
GDN_CONV_W = 2 * GDN_KW + GDN_VW
GDN_IN = GDN_CONV_W + GDN_VW + 2 * GDN_V_HEADS
GDN_IN_PAD = GDN_CONV_W + GDN_VW + 2 * 128

ADAM_LR = 0.001
ADAM_B1 = 0.9
ADAM_B2 = 0.999
ADAM_EPS = 1e-08
ADAM_WD = 0.01
ADAM_STEP = 10

LANE = 128
VMEM_LIMIT = 56 * 1024 * 1024
PACK_W = 1024

NN = ((1,), (0,))
NT = ((1,), (1,))
TN = ((0,), (0,))

SHARDED = (
    ("ffn_norm", 2), ("gdn_conv_w", 2),
    ("ffn_w_gate", 3), ("ffn_w_up", 3), ("ffn_w_down", 2), ("att_w_in", 2), ("att_w_out", 1),
    ("gdn_w_in", 2), ("gdn_w_out", 1), ("ple_w_gate", 1), ("ple_w_proj", 2),
)
SMALL_SHARDED = 2
REPLICATED = ("mix_norm", "att_q_norm", "att_k_norm", "att_sinks", "gdn_a_log", "gdn_dt_bias",
              "gdn_out_norm", "ple_norm")
WEIGHTS = ("ffn_norm", "ffn_w_gate", "ffn_w_up", "ffn_w_down", "mix_norm", "att_w_in", "att_q_norm",
           "att_k_norm", "att_sinks", "att_w_out", "gdn_w_in", "gdn_conv_w", "gdn_a_log", "gdn_dt_bias",
           "gdn_out_norm", "gdn_w_out", "ple_norm", "ple_w_gate", "ple_w_proj")


_FFN_REST = [(0, 1), (1, 0), (1, 1)]
EARLY = [("ffn_w_gate", [(0, 0)]), ("ffn_w_up", [(0, 0)])]
FIRST = [("ffn_w_down", [(0, 0)]), ("att_w_in", [()])]
LATE = ([(n, [idx]) for n in ("ffn_w_gate", "ffn_w_up", "ffn_w_down") for idx in _FFN_REST]
        + [("att_w_out", [()]), ("gdn_w_in", [()]), ("gdn_w_out", [()]),
           ("ple_w_gate", [(0,), (1,)]), ("ple_w_proj", [(0,), (1,)])])
RIDE = [e for e in LATE if e[0] != "att_w_out"]
FINAL = EARLY + FIRST + [("att_w_out", [()])]


def _pcall(body, **kw):
    return pl.pallas_call(body, **kw)


def _params(sem=None):
    if sem is None:
        return pltpu.CompilerParams(vmem_limit_bytes=VMEM_LIMIT)
    return pltpu.CompilerParams(dimension_semantics=sem, vmem_limit_bytes=VMEM_LIMIT)


def _ride_specs(ride, out_shapes, n_sems):
    hbm = pl.BlockSpec(memory_space=pl.ANY)
    n = len(ride)
    sems = [pltpu.SemaphoreType.DMA((n, n_sems)), pltpu.SemaphoreType.DMA((n, n_sems)),
            pltpu.SemaphoreType.DMA((n, N_CHIP))] if n else []
    return [hbm] * n, [hbm] * len(out_shapes), sems


def _dot(a, b, dims=NN):
    return lax.dot_general(a, b, (dims, ((), ())), preferred_element_type=F32)


def _bdot(a, b, dims=NN):
    return _dot(a.astype(BF16), b.astype(BF16), dims)


def _split(a):
    hi = a.astype(BF16)
    lo = (a - hi.astype(F32)).astype(BF16)
    return hi, lo


def _dot3(a, b, dims=NN):
    ah, al = _split(a)
    bh, bl = _split(b)
    return _dot(ah, bh, dims) + (_dot(ah, bl, dims) + _dot(al, bh, dims))


def _dot2m(a, m, dims=NN):
    ah, al = _split(a)
    return _dot(ah, m, dims) + _dot(al, m, dims)


def _mdot2(m, a, dims=NN):
    ah, al = _split(a)
    return _dot(m, ah, dims) + _dot(m, al, dims)


def _sigmoid(x):
    return 1.0 / (1.0 + jnp.exp(-x))


def _softplus(x):
    return jnp.maximum(x, 0.0) + jnp.log(1.0 + jnp.exp(-jnp.abs(x)))


def _pick(n, cap):
    if n <= cap:
        return n
    for t in range(cap - cap % LANE, 0, -LANE):
        if n % t == 0:
            return t
    raise ValueError(f"no tile for {n} under {cap}")


def _iota2(shape, axis):
    return lax.broadcasted_iota(jnp.int32, shape, axis)


def _mm(a, b, mode, out_dtype=F32, res=None, alpha=1.0, a2=None, b2=None, name="mm"):
    if mode == "nn":
        (M, K), N = a.shape, b.shape[1]
    elif mode == "nt":
        (M, K), N = a.shape, b.shape[0]
    else:
        (K, M), N = a.shape, b.shape[1]
    tm, tn, tk = _pick(M, 1408 if mode == "tn" else 1024), _pick(N, 1408), _pick(K, 1024 if mode == "tn" else 1408)
    nk = K // tk
    dims = {"nn": NN, "nt": NT, "tn": TN}[mode]
    a_spec = pl.BlockSpec((tk, tm), lambda i, j, k: (k, i)) if mode == "tn" else pl.BlockSpec((tm, tk), lambda i, j, k: (i, k))
    b_spec = pl.BlockSpec((tn, tk), lambda i, j, k: (j, k)) if mode == "nt" else pl.BlockSpec((tk, tn), lambda i, j, k: (k, j))
    o_spec = pl.BlockSpec((tm, tn), lambda i, j, k: (i, j))
    two = a2 is not None
    has_res = res is not None
    a2_spec, b2_spec = a_spec, b_spec
    if two and a2.shape != a.shape:
        assert nk == 1 and mode == "nn" and a2.shape[0] == M and b2.shape[1] == N
        a2_spec = pl.BlockSpec((tm, a2.shape[1]), lambda i, j, k: (i, 0))
        b2_spec = pl.BlockSpec((a2.shape[1], tn), lambda i, j, k: (0, j))

    def body(*refs):
        refs = list(refs)
        a_ref, b_ref = refs[0], refs[1]
        pos = 2
        if two:
            a2_ref, b2_ref = refs[2], refs[3]
            pos = 4
        if has_res:
            res_ref = refs[pos]
            pos += 1
        o_ref, acc_ref = refs[pos], refs[pos + 1]
        k = pl.program_id(2)
        part = _bdot(a_ref[...], b_ref[...], dims)
        if two:
            part = part + _bdot(a2_ref[...], b2_ref[...], dims)

        def finish(acc):
            out = acc * alpha if alpha != 1.0 else acc
            if has_res:
                out = res_ref[...] + out
            o_ref[...] = out.astype(out_dtype)

        if nk == 1:
            finish(part)
        else:
            @pl.when(k == 0)
            def _():
                acc_ref[...] = part

            @pl.when(k > 0)
            def _():
                acc_ref[...] += part

            @pl.when(k == nk - 1)
            def _():
                finish(acc_ref[...])

    ins = [a, b]
    specs = [a_spec, b_spec]
    if two:
        ins += [a2, b2]
        specs += [a2_spec, b2_spec]
    if has_res:
        ins.append(res)
        specs.append(o_spec)
    return _pcall(
        body, name=name, grid=(M // tm, N // tn, nk), in_specs=specs, out_specs=o_spec,
        out_shape=jax.ShapeDtypeStruct((M, N), out_dtype),
        scratch_shapes=[pltpu.VMEM((tm, tn) if nk > 1 else (8, LANE), F32)],
        compiler_params=_params(("parallel", "parallel", "arbitrary")),
    )(*ins)


ROW_TILE = 256


def _rms_fwd(h, g, name):
    T, D = h.shape
    tr = _pick(T, ROW_TILE)

    def body(h_ref, g_ref, n_ref):
        x = h_ref[...]
        r = lax.rsqrt(jnp.mean(x * x, axis=-1, keepdims=True) + EPS)
        n_ref[...] = (x * r * g_ref[...]).astype(BF16)

    return _pcall(
        body, name=name, grid=(T // tr,),
        in_specs=[pl.BlockSpec((tr, D), lambda i: (i, 0)), pl.BlockSpec((1, D), lambda i: (0, 0))],
        out_specs=pl.BlockSpec((tr, D), lambda i: (i, 0)),
        out_shape=jax.ShapeDtypeStruct((T, D), BF16), compiler_params=_params(("parallel",)),
    )(h, g.reshape(1, D))


def _rms_bwd(dn, h, g, dres, name):
    T, D = h.shape
    tr = _pick(T, ROW_TILE)

    def body(dn_ref, h_ref, g_ref, dres_ref, dh_ref, dg_ref):
        x = h_ref[...]
        r = lax.rsqrt(jnp.mean(x * x, axis=-1, keepdims=True) + EPS)
        xh = x * r
        d = dn_ref[...].astype(F32)
        dxh = d * g_ref[...]
        dh_ref[...] = dres_ref[...] + r * (dxh - xh * jnp.mean(dxh * xh, axis=-1, keepdims=True))
        part = jnp.sum(d * xh, axis=0, keepdims=True)

        @pl.when(pl.program_id(0) == 0)
        def _():
            dg_ref[...] = part

        @pl.when(pl.program_id(0) > 0)
        def _():
            dg_ref[...] += part

    row = pl.BlockSpec((tr, D), lambda i: (i, 0))
    vec = pl.BlockSpec((1, D), lambda i: (0, 0))
    dh, dg = _pcall(
        body, name=name, grid=(T // tr,), in_specs=[row, row, vec, row], out_specs=[row, vec],
        out_shape=[jax.ShapeDtypeStruct((T, D), F32), jax.ShapeDtypeStruct((1, D), F32)],
        compiler_params=_params(("arbitrary",)),
    )(dn, h, g.reshape(1, D), dres)
    return dh, dg.reshape(D)


def _gateup(n, wg, wu, name, ride=()):
    T, D = n.shape
    F = wg.shape[1]
    tm, tn = _pick(T, 1024), _pick(F, 1408)
    nr = len(ride)
    ride_out = [jax.ShapeDtypeStruct((N_DEV,) + r.shape, r.dtype) for r in ride]
    ride_in_specs, ride_out_specs, ride_sems = _ride_specs(ride, ride_out, N_DEV - 1)
    grid = (T // tm, F // tn)

    def body(*refs):
        n_ref, wg_ref, wu_ref = refs[:3]
        a_ref, b_ref, hid_ref = refs[3 + nr:6 + nr]
        if nr:
            i, j = pl.program_id(0), pl.program_id(1)
            start, finish = _gather_protocol(refs[3:3 + nr], refs[6 + nr:6 + 2 * nr], *refs[6 + 2 * nr:])
            pl.when((i == 0) & (j == 0))(start)
        x = n_ref[...]
        a = _dot(x, wg_ref[...])
        b = _dot(x, wu_ref[...])
        a_ref[...] = a.astype(BF16)
        b_ref[...] = b.astype(BF16)
        hid_ref[...] = (a * _sigmoid(a) * b).astype(BF16)
        if nr:
            pl.when((i == grid[0] - 1) & (j == grid[1] - 1))(finish)

    o_spec = pl.BlockSpec((tm, tn), lambda i, j: (i, j))
    w_spec = pl.BlockSpec((D, tn), lambda i, j: (0, j))
    sh = jax.ShapeDtypeStruct((T, F), BF16)
    res = _pcall(
        body, name=name, grid=grid,
        in_specs=[pl.BlockSpec((tm, D), lambda i, j: (i, 0)), w_spec, w_spec] + ride_in_specs,
        out_specs=[o_spec, o_spec, o_spec] + ride_out_specs, out_shape=[sh, sh, sh] + ride_out,
        scratch_shapes=ride_sems,
        compiler_params=_params(("arbitrary", "arbitrary") if nr else ("parallel", "parallel")),
    )(n, wg, wu, *ride)
    return res[0], res[1], res[2], list(res[3:])


def _ffn_dhid(dy, wd, a, b, name):
    T, D = dy.shape
    F = wd.shape[0]
    tm, tn = _pick(T, 512), _pick(F, 1408)

    def body(dy_ref, wd_ref, a_ref, b_ref, da_ref, db_ref):
        dhid = 0.5 * _bdot(dy_ref[...], wd_ref[...], NT)
        av = a_ref[...].astype(F32)
        bv = b_ref[...].astype(F32)
        s = _sigmoid(av)
        da_ref[...] = (dhid * bv * s * (1.0 + av * (1.0 - s))).astype(BF16)
        db_ref[...] = (dhid * av * s).astype(BF16)

    o_spec = pl.BlockSpec((tm, tn), lambda i, j: (i, j))
    sh = jax.ShapeDtypeStruct((T, F), BF16)
    return _pcall(
        body, name=name, grid=(T // tm, F // tn),
        in_specs=[pl.BlockSpec((tm, D), lambda i, j: (i, 0)), pl.BlockSpec((tn, D), lambda i, j: (j, 0)), o_spec, o_spec],
        out_specs=[o_spec, o_spec], out_shape=[sh, sh],
        compiler_params=_params(("parallel", "parallel")),
    )(dy, wd, a, b)


def _ffn_fwd(h, g, wg, wu, wd, tag, ride=(), wd_of=None):
    n = _rms_fwd(h, g, f"{tag}_norm")
    a, b, hid, gathered = _gateup(n, wg, wu, f"{tag}_gateup", ride)
    if wd_of is not None:
        wd = wd_of(gathered)
    h2 = _mm(hid, wd, "nn", res=h, alpha=0.5, name=f"{tag}_down")
    return h2, (h, n, a, b, hid)


def _ffn_bwd(dh2, saved, g, wg, wu, wd, tag):
    h, n, a, b, hid = saved
    da, db = _ffn_dhid(dh2, wd, a, b, f"{tag}_dhid")
    dwd = _mm(hid, dh2, "tn", alpha=0.5, out_dtype=BF16, name=f"{tag}_dwd")
    dwg = _mm(n, da, "tn", out_dtype=BF16, name=f"{tag}_dwg")
    dwu = _mm(n, db, "tn", out_dtype=BF16, name=f"{tag}_dwu")
    dn = _mm(da, wg, "nt", a2=db, b2=wu, name=f"{tag}_dn")
    dh, dg = _rms_bwd(dn, h, g, dh2, f"{tag}_dnorm")
    return dh, dg, dwg, dwu, dwd


def _ple_fwd(h, p, g, w_gate, w_proj, tag):
    T, D = h.shape
    pn = _rms_fwd(h, g, f"{tag}_norm")
    tm, tn = _pick(T, 512), _pick(D, 1024)
    P = p.shape[1]

    def body(pn_ref, p_ref, wg_ref, wp_ref, h_ref, o_ref, gl_ref, pp_ref):
        gl = _dot(pn_ref[...], wg_ref[...])
        pp = _bdot(p_ref[...], wp_ref[...])
        gl_ref[...] = gl
        pp_ref[...] = pp
        o_ref[...] = h_ref[...] + _sigmoid(gl) * pp

    o_spec = pl.BlockSpec((tm, tn), lambda i, j: (i, j))
    sh = jax.ShapeDtypeStruct((T, D), F32)
    h2, gl, pp = _pcall(
        body, name=f"{tag}_fwd", grid=(T // tm, D // tn),
        in_specs=[pl.BlockSpec((tm, D), lambda i, j: (i, 0)), pl.BlockSpec((tm, P), lambda i, j: (i, 0)),
                  pl.BlockSpec((D, tn), lambda i, j: (0, j)), pl.BlockSpec((P, tn), lambda i, j: (0, j)), o_spec],
        out_specs=[o_spec, o_spec, o_spec], out_shape=[sh, sh, sh],
        compiler_params=_params(("parallel", "parallel")),
    )(pn, p, w_gate, w_proj, h)
    return h2, (h, pn, gl, pp)


def _ple_bwd(dh2, saved, p, g, w_gate, tag):
    h, pn, gl, pp = saved
    T, D = h.shape
    tr = _pick(T, ROW_TILE)

    def body(d_ref, gl_ref, pp_ref, dgl_ref, dpp_ref):
        d = d_ref[...]
        s = _sigmoid(gl_ref[...])
        dpp_ref[...] = (d * s).astype(BF16)
        dgl_ref[...] = (d * pp_ref[...] * s * (1.0 - s)).astype(BF16)

    row = pl.BlockSpec((tr, D), lambda i: (i, 0))
    sh = jax.ShapeDtypeStruct((T, D), BF16)
    dgl, dpp = _pcall(body, name=f"{tag}_dgate", grid=(T // tr,), in_specs=[row, row, row], out_specs=[row, row],
                      out_shape=[sh, sh], compiler_params=_params(("parallel",)))(dh2, gl, pp)
    dw_proj = _mm(p, dpp, "tn", out_dtype=BF16, name=f"{tag}_dwproj")
    dw_gate = _mm(pn, dgl, "tn", out_dtype=BF16, name=f"{tag}_dwgate")
    dpn = _mm(dgl, w_gate, "nt", name=f"{tag}_dpn")
    dh, dg = _rms_bwd(dpn, h, g, dh2, f"{tag}_dnorm")
    return dh, dg, dw_gate, dw_proj


def _loss_head(y, target):
    T, D = y.shape
    tr = _pick(T, ROW_TILE)

    def body(y_ref, t_ref, dy_ref, l_ref):
        e = y_ref[...] - t_ref[...]
        dy_ref[...] = e * (1.0 / D)
        part = jnp.sum(e * e, axis=0, keepdims=True)

        @pl.when(pl.program_id(0) == 0)
        def _():
            l_ref[...] = part

        @pl.when(pl.program_id(0) > 0)
        def _():
            l_ref[...] += part

    row = pl.BlockSpec((tr, D), lambda i: (i, 0))
    vec = pl.BlockSpec((1, D), lambda i: (0, 0))
    dy, l = _pcall(body, name="loss_head", grid=(T // tr,), in_specs=[row, row], out_specs=[row, vec],
                   out_shape=[jax.ShapeDtypeStruct((T, D), F32), jax.ShapeDtypeStruct((1, D), F32)],
                   compiler_params=_params(("arbitrary",)))(y, target)
    return (0.5 / D) * jnp.sum(l), dy


SB_LANES = SB_HEADS * 2 * HEAD_DIM


def _sb_consts():
    row = _iota2((Q_BLOCK, Q_BLOCK), 0)
    col = _iota2((Q_BLOCK, Q_BLOCK), 1)
    after = (row > col).astype(BF16)
    before = (row < col).astype(BF16)
    return col < row, after, before, col


def _sb_fwd(proj, ride=()):
    T = proj.shape[0]
    H, d, L = SB_HEADS, HEAD_DIM, 2 * HEAD_DIM
    nblk = T // Q_BLOCK
    scale = d ** -0.5
    n = len(ride)
    ride_out = [jax.ShapeDtypeStruct((N_DEV,) + a.shape, a.dtype) for a in ride]
    ride_in_specs, ride_out_specs, ride_sems = _ride_specs(ride, ride_out, N_DEV - 1)
    R = range(H)
    tile = lambda g: slice(g * L, (g + 1) * L)

    def body(*refs):
        q_ref, kv_ref = refs[:2]
        rin = refs[2:2 + n]
        o_ref, c_ref = refs[2 + n:4 + n]
        rout = refs[4 + n:4 + 2 * n]
        run_ref = refs[4 + 2 * n]
        i = pl.program_id(0)
        if n:
            start, finish = _gather_protocol(rin, rout, *refs[5 + 2 * n:])
            pl.when(i == 0)(start)
        causal, after, _, col = _sb_consts()
        qs = [q_ref[:, tile(g)] * scale for g in R]
        o_ref[...] = jnp.zeros_like(o_ref)
        c_ref[...] = jnp.zeros_like(c_ref)
        run_ref[...] = jnp.zeros_like(run_ref)

        def pair(j, diag):
            rows = pl.ds(pl.multiple_of(j * Q_BLOCK, Q_BLOCK), Q_BLOCK)
            kvj = [kv_ref[rows, tile(g)] for g in R]
            c = [run_ref[g] for g in R]
            acc = [o_ref[:, tile(g)] for g in R]
            cm = None if diag else [c_ref[:, tile(g)] for g in R]
            z = [_dot(qs[g], kvj[g], NT) for g in R]
            sp = [_softplus(z[g]) for g in R]
            lk = [jnp.where(causal, -sp[g], 0.0) if diag else -sp[g] for g in R]
            btw = [_dot2m(lk[g], after) for g in R]
            e = [jnp.exp((z[g] - sp[g]) + btw[g] + c[g]) for g in R]
            w = [jnp.where(causal, e[g], 0.0) if diag else e[g] for g in R]
            pv = [_bdot(w[g], kvj[g]) for g in R]
            rs = [jnp.sum(lk[g], axis=1, keepdims=True) for g in R]
            for g in R:
                o_ref[:, tile(g)] = acc[g] + pv[g]
                if not diag:
                    c_ref[:, tile(g)] = jnp.where(col == j, c[g], cm[g])
                run_ref[g] = c[g] + rs[g]

        pair(i, True)

        @pl.loop(0, i)
        def _(jj):
            pair(i - 1 - jj, False)

        if n:
            pl.when(i == nblk - 1)(finish)

    blk = pl.BlockSpec((Q_BLOCK, H * L), lambda i: (i, 0))
    full = pl.BlockSpec((T, H * L), lambda i: (0, 1))
    res = _pcall(
        body, name="sb_fwd", grid=(nblk,), in_specs=[blk, full] + ride_in_specs,
        out_specs=[blk, blk] + ride_out_specs,
        out_shape=[jax.ShapeDtypeStruct((T, H * L), F32), jax.ShapeDtypeStruct((T, H * L), F32)] + ride_out,
        scratch_shapes=[pltpu.VMEM((H, Q_BLOCK, 1), F32)] + ride_sems,
        compiler_params=_params(("arbitrary",)),
    )(proj, proj, *ride)
    return res[0], res[1], list(res[2:])


def _sb_bwd(proj, carry, do, ride=()):
    T = proj.shape[0]
    H, d, L = SB_HEADS, HEAD_DIM, 2 * HEAD_DIM
    nblk = T // Q_BLOCK
    scale = d ** -0.5
    n = len(ride)
    ride_out = [jax.ShapeDtypeStruct(a.shape, a.dtype) for a in ride]
    ride_in_specs, ride_out_specs, ride_sems = _ride_specs(ride, ride_out, N_CHIP - 1)
    R = range(H)
    tile = lambda g: slice(g * L, (g + 1) * L)

    def body(*refs):
        q_ref, kv_ref, c_ref, do_ref = refs[:4]
        rin = refs[4:4 + n]
        dq_ref, dkv_ref = refs[4 + n:6 + n]
        rout = refs[6 + n:6 + 2 * n]
        run_ref = refs[6 + 2 * n]
        i = pl.program_id(0)
        if n:
            start, finish = _chips_protocol(rin, rout, *refs[7 + 2 * n:])
            pl.when(i == 0)(start)

        @pl.when(i == 0)
        def _():
            dkv_ref[...] = jnp.zeros_like(dkv_ref)

        causal, after, before, col = _sb_consts()
        qs = [q_ref[:, tile(g)] * scale for g in R]
        dov = [do_ref[:, tile(g)] for g in R]
        qdo = [jnp.concatenate([qs[g], dov[g]], axis=0) for g in R]
        dq_ref[...] = jnp.zeros_like(dq_ref)
        run_ref[...] = jnp.zeros_like(run_ref)

        def pair(j, diag):
            rows = pl.ds(pl.multiple_of(j * Q_BLOCK, Q_BLOCK), Q_BLOCK)
            kvj = [kv_ref[rows, tile(g)] for g in R]
            gsum = [run_ref[g] for g in R]
            dq0 = [dq_ref[:, tile(g)] for g in R]
            dkv0 = [dkv_ref[rows, tile(g)] for g in R]
            cm = None if diag else [c_ref[:, tile(g)] for g in R]
            z = [_dot(qs[g], kvj[g], NT) for g in R]
            sp = [_softplus(z[g]) for g in R]
            lk = [jnp.where(causal, -sp[g], 0.0) if diag else -sp[g] for g in R]
            ls = [z[g] - sp[g] for g in R]
            logw = [ls[g] + _dot2m(lk[g], after) for g in R]
            if not diag:
                logw = [logw[g] + jnp.sum(jnp.where(col == j, cm[g], 0.0), axis=1, keepdims=True) for g in R]
            e = [jnp.exp(logw[g]) for g in R]
            w = [jnp.where(causal, e[g], 0.0) if diag else e[g] for g in R]
            gw = [_dot(dov[g], kvj[g], NT) * w[g] for g in R]
            gpre = [gsum[g] + _dot(gw[g].astype(BF16), before) for g in R]
            sig = [jnp.exp(ls[g]) for g in R]
            dz = [gw[g] * (1.0 - sig[g]) - sig[g] * gpre[g] for g in R]
            if diag:
                dz = [jnp.where(causal, dz[g], 0.0) for g in R]
            dzb = [dz[g].astype(BF16) for g in R]
            dq1 = [_dot(dzb[g], kvj[g]) for g in R]
            dkv1 = [_dot(jnp.concatenate([dzb[g], w[g].astype(BF16)], axis=0), qdo[g], TN) for g in R]
            gs1 = [jnp.sum(gw[g], axis=1, keepdims=True) for g in R]
            for g in R:
                dq_ref[:, tile(g)] = dq0[g] + dq1[g]
                dkv_ref[rows, tile(g)] = dkv0[g] + dkv1[g]
                run_ref[g] = gsum[g] + gs1[g]

        @pl.loop(0, i)
        def _(j):
            pair(j, False)

        pair(i, True)
        dq_ref[...] = dq_ref[...] * scale
        if n:
            pl.when(i == nblk - 1)(finish)

    blk = pl.BlockSpec((Q_BLOCK, H * L), lambda i: (i, 0))
    once = pl.Buffered(1)
    sh = jax.ShapeDtypeStruct((T, H * L), F32)
    res = _pcall(
        body, name="sb_bwd", grid=(nblk,),
        in_specs=[blk, pl.BlockSpec((T, H * L), lambda i: (0, 1), pipeline_mode=once), blk, blk] + ride_in_specs,
        out_specs=[blk, pl.BlockSpec((T, H * L), lambda i: (0, 0), pipeline_mode=once)] + ride_out_specs,
        out_shape=[sh, sh] + ride_out,
        scratch_shapes=[pltpu.VMEM((H, Q_BLOCK, 1), F32)] + ride_sems,
        compiler_params=_params(("arbitrary",)),
    )(proj, proj, carry, do, *ride)
    return res[0], res[1], list(res[2:])


def _swa_common(q_ref, kvp_ref, kvc_ref, qg_ref, kg_ref, sk_ref, sl_ref, n):
    W, d, G = WINDOW, HEAD_DIM, SWA_GROUP
    scale = d ** -0.5
    row = _iota2((W, 2 * W), 0)
    col = _iota2((W, 2 * W), 1)
    dist = row + W - col
    valid = (dist >= 0) & (dist < W) & ((n > 0) | (col >= W))
    distf = dist.astype(F32)
    kvcat = jnp.concatenate([kvp_ref[...], kvc_ref[...]], axis=0)
    KH, QH = range(SWA_KV_HEADS), range(SWA_HEADS)
    kraw = [kvcat[:, hk * d:(hk + 1) * d] for hk in KH]
    vcat = [kvcat[:, SWA_KVW + hk * d:SWA_KVW + (hk + 1) * d].astype(BF16) for hk in KH]
    rk = [lax.rsqrt(jnp.mean(kraw[hk] * kraw[hk], axis=-1, keepdims=True) + EPS) for hk in KH]
    kh = [kraw[hk] * rk[hk] for hk in KH]
    kn = [(kh[hk] * kg_ref[...]).astype(BF16) for hk in KH]
    qraw = [q_ref[:, h * d:(h + 1) * d] for h in QH]
    rq = [lax.rsqrt(jnp.mean(qraw[h] * qraw[h], axis=-1, keepdims=True) + EPS) for h in QH]
    qh = [qraw[h] * rq[h] for h in QH]
    qn = [(qh[h] * qg_ref[...]).astype(BF16) for h in QH]
    sink = [sk_ref[h:h + 1, :1] for h in QH]
    s = [jnp.where(valid, _dot(qn[h], kn[h // G], NT) * scale - sl_ref[h:h + 1, :1] * distf, -1e30) for h in QH]
    m = [jnp.maximum(jnp.max(s[h], axis=1, keepdims=True), sink[h]) for h in QH]
    p = [jnp.where(valid, jnp.exp(s[h] - m[h]), 0.0) for h in QH]
    esink = [jnp.exp(sink[h] - m[h]) for h in QH]
    den = [jnp.sum(p[h], axis=1, keepdims=True) + esink[h] for h in QH]
    prob = [p[h] / den[h] for h in QH]
    return vcat, rk, kh, kn, rq, qh, qn, esink, den, prob


def _swa_specs(T):
    W = WINDOW
    q = pl.BlockSpec((W, SWA_QW), lambda n: (n, 0))
    prev = pl.BlockSpec((W, 2 * SWA_KVW), lambda n: (jnp.maximum(n - 1, 0), SWA_QW // (2 * SWA_KVW)))
    cur = pl.BlockSpec((W, 2 * SWA_KVW), lambda n: (n, SWA_QW // (2 * SWA_KVW)))
    gain = pl.BlockSpec((1, HEAD_DIM), lambda n: (0, 0))
    perhead = pl.BlockSpec((SWA_HEADS, LANE), lambda n: (0, 0))
    return q, prev, cur, gain, perhead


def _swa_fwd(proj, qg, kg, sinks, slopes):
    T = proj.shape[0]
    W, d, G = WINDOW, HEAD_DIM, SWA_GROUP

    def body(q_ref, kvp_ref, kvc_ref, qg_ref, kg_ref, sk_ref, sl_ref, o_ref):
        vcat, _, _, _, _, _, _, _, _, prob = _swa_common(q_ref, kvp_ref, kvc_ref, qg_ref, kg_ref, sk_ref, sl_ref,
                                                         pl.program_id(0))
        outs = [_bdot(prob[h], vcat[h // G]) for h in range(SWA_HEADS)]
        o_ref[...] = jnp.concatenate(outs, axis=1).astype(BF16)

    q, prev, cur, gain, perhead = _swa_specs(T)
    return _pcall(
        body, name="swa_fwd", grid=(T // W,), in_specs=[q, prev, cur, gain, gain, perhead, perhead], out_specs=q,
        out_shape=jax.ShapeDtypeStruct((T, SWA_QW), BF16), compiler_params=_params(("parallel",)),
    )(proj, proj, proj, qg, kg, sinks, slopes)


def _swa_bwd(proj, qg, kg, sinks, slopes, do):
    T = proj.shape[0]
    W, d, G = WINDOW, HEAD_DIM, SWA_GROUP
    scale = d ** -0.5
    KH, QH = range(SWA_KV_HEADS), range(SWA_HEADS)

    def body(q_ref, kvp_ref, kvc_ref, qg_ref, kg_ref, sk_ref, sl_ref, do_ref,
             dq_ref, dkv_ref, dqg_ref, dkg_ref, dsk_ref):
        n = pl.program_id(0)

        @pl.when(n == 0)
        def _():
            dqg_ref[...] = jnp.zeros_like(dqg_ref)
            dkg_ref[...] = jnp.zeros_like(dkg_ref)
            dsk_ref[...] = jnp.zeros_like(dsk_ref)
            dkv_ref[...] = jnp.zeros_like(dkv_ref)

        vcat, rk, kh, kn, rq, qh, qn, esink, den, prob = _swa_common(q_ref, kvp_ref, kvc_ref, qg_ref, kg_ref,
                                                                     sk_ref, sl_ref, n)
        dov = [do_ref[:, h * d:(h + 1) * d].astype(BF16) for h in QH]
        dp = [_dot(dov[h], vcat[h // G], NT) for h in QH]
        dd = [jnp.sum(prob[h] * dp[h], axis=1, keepdims=True) for h in QH]
        dsb = [(prob[h] * (dp[h] - dd[h]) * scale).astype(BF16) for h in QH]
        dsink = [-jnp.sum((esink[h] / den[h]) * dd[h], axis=0, keepdims=True) for h in QH]
        dqn = [_dot(dsb[h], kn[h // G]) for h in QH]
        dkn_h = [_dot(dsb[h], qn[h], TN) for h in QH]
        dv_h = [_dot(prob[h].astype(BF16), dov[h], TN) for h in QH]
        dqh = [dqn[h] * qg_ref[...] for h in QH]
        dq = [rq[h] * (dqh[h] - qh[h] * jnp.mean(dqh[h] * qh[h], axis=-1, keepdims=True)) for h in QH]
        dkn = [sum(dkn_h[hk * G + g] for g in range(G)) for hk in KH]
        dvc = [sum(dv_h[hk * G + g] for g in range(G)) for hk in KH]
        dkh = [dkn[hk] * kg_ref[...] for hk in KH]
        dkraw = [rk[hk] * (dkh[hk] - kh[hk] * jnp.mean(dkh[hk] * kh[hk], axis=-1, keepdims=True)) for hk in KH]
        dq_ref[...] = jnp.concatenate(dq, axis=1)
        dqg_ref[...] += sum(jnp.sum(dqn[h] * qh[h], axis=0, keepdims=True) for h in QH)
        dkg_ref[...] += sum(jnp.sum(dkn[hk] * kh[hk], axis=0, keepdims=True) for hk in KH)
        rowh = _iota2((SWA_HEADS, LANE), 0)
        dsk_ref[...] += sum(jnp.where(rowh == h, dsink[h], 0.0) for h in QH)
        upd = jnp.concatenate(dkraw + dvc, axis=1)
        offp = pl.multiple_of(jnp.maximum(n - 1, 0) * W, W)
        offc = pl.multiple_of(n * W, W)
        dkv_ref[pl.ds(offp, W), :] += upd[:W]
        dkv_ref[pl.ds(offc, W), :] += upd[W:]

    q, prev, cur, gain, perhead = _swa_specs(T)
    kvfull = pl.BlockSpec((T, 2 * SWA_KVW), lambda n: (0, 0))
    gs = jax.ShapeDtypeStruct((1, d), F32)
    return _pcall(
        body, name="swa_bwd", grid=(T // W,), in_specs=[q, prev, cur, gain, gain, perhead, perhead, q],
        out_specs=[q, kvfull, gain, gain, perhead],
        out_shape=[jax.ShapeDtypeStruct((T, SWA_QW), F32), jax.ShapeDtypeStruct((T, 2 * SWA_KVW), F32), gs, gs,
                   jax.ShapeDtypeStruct((SWA_HEADS, LANE), F32)],
        compiler_params=_params(("arbitrary",)),
    )(proj, proj, proj, qg, kg, sinks, slopes, do)


def _alibi():
    s = [2.0 ** (-8.0 * (i + 1) / SWA_HEADS) for i in range(SWA_HEADS)]
    return jnp.broadcast_to(jnp.asarray(s, F32)[:, None], (SWA_HEADS, LANE))


def _head_tiles(lo, hi):
    shp = lo.shape[:-1]
    return jnp.concatenate([lo.reshape(shp + (SB_HEADS, HEAD_DIM)), hi.reshape(shp + (SB_HEADS, HEAD_DIM))],
                           axis=-1).reshape(shp + (SB_LANES,))


def _tile_halves(x):
    shp = x.shape[:-1]
    t = x.reshape(shp + (SB_HEADS, 2, HEAD_DIM))
    return t[..., 0, :].reshape(shp + (SB_W,)), t[..., 1, :].reshape(shp + (SB_W,))


def _att_in_weights(w_in):
    sq, sk, sv = w_in[:, :SB_W], w_in[:, SB_W:2 * SB_W], w_in[:, 2 * SB_W:3 * SB_W]
    return jnp.concatenate([_head_tiles(sq, jnp.zeros_like(sq)), _head_tiles(sk, sv)], axis=1), w_in[:, 3 * SB_W:]


def _att_out_weights(w_out):
    wo = w_out[:SB_W]
    return _head_tiles(jnp.zeros_like(wo).T, wo.T).T, w_out[SB_W:]


def _att_fwd(h, g, w_in, w_out_of, q_gain, k_gain, sinks, ride=()):
    hn = _rms_fwd(h, g, "att_norm")
    w_sb, w_swa = _att_in_weights(w_in)
    proj_sb = _mm(hn, w_sb, "nn", out_dtype=BF16, name="att_in_sb")
    proj_swa = _mm(hn, w_swa, "nn", name="att_in_swa")
    a_out, carry, gathered = _sb_fwd(proj_sb, ride)
    w_out = w_out_of(gathered)
    wo_sb, wo_swa = _att_out_weights(w_out)
    sk128 = jnp.broadcast_to(sinks.reshape(SWA_HEADS, 1), (SWA_HEADS, LANE))
    qg, kg = q_gain.reshape(1, HEAD_DIM), k_gain.reshape(1, HEAD_DIM)
    b_out = _swa_fwd(proj_swa, qg, kg, sk128, _alibi())
    h2 = _mm(a_out, wo_sb, "nn", res=h, a2=b_out, b2=wo_swa, name="att_out")
    return h2, (h, hn, proj_sb, proj_swa, carry, a_out, b_out, sk128, qg, kg), gathered


def _att_bwd(dh2, saved, g, w_in, w_out, ride=()):
    h, hn, proj_sb, proj_swa, carry, a_out, b_out, sk128, qg, kg = saved
    w_sb, w_swa = _att_in_weights(w_in)
    wo_sb, wo_swa = _att_out_weights(w_out)
    da = _mm(dh2, wo_sb, "nt", out_dtype=BF16, name="att_do_sb")
    db = _mm(dh2, wo_swa, "nt", name="att_do_swa")
    dwo_sb = _mm(a_out, dh2, "tn", out_dtype=BF16, name="att_dwout_sb")
    dwo_swa = _mm(b_out, dh2, "tn", out_dtype=BF16, name="att_dwout_swa")
    dw_out = jnp.concatenate([_tile_halves(dwo_sb.T)[1].T, dwo_swa], axis=0)
    dq, dkv, rode = _sb_bwd(proj_sb, carry, da, ride)
    dbq, dbkv, dqg, dkg, dsink = _swa_bwd(proj_swa, qg, kg, sk128, _alibi(), db)
    dproj = jnp.concatenate([dq.astype(BF16), dkv.astype(BF16), dbq.astype(BF16), dbkv.astype(BF16)], axis=1)
    w_all = jnp.concatenate([w_sb, w_swa], axis=1)
    dw_all = _mm(hn, dproj, "tn", out_dtype=BF16, name="att_dwin")
    dhn = _mm(dproj, w_all, "nt", name="att_dhn")
    dsq, _ = _tile_halves(dw_all[:, :SB_LANES])
    dsk, dsv = _tile_halves(dw_all[:, SB_LANES:2 * SB_LANES])
    dw_in = jnp.concatenate([dsq, dsk, dsv, dw_all[:, 2 * SB_LANES:]], axis=1)
    dh, dg = _rms_bwd(dhn, h, g, dh2, "att_dnorm")
    return dh, dg, dw_in, dw_out, dqg.reshape(HEAD_DIM), dkg.reshape(HEAD_DIM), dsink[:, 0], rode


CONV_ROWS = 512
CONV_COLS = 512
HALO = 8


def _shifted(xcat, s, tm):
    if s == 0:
        return xcat[HALO:HALO + tm]
    return pltpu.roll(xcat, s, 0)[HALO:HALO + tm]


def _conv_pre(x_ref, halo_ref, w_ref, i, tm):
    xc = x_ref[...]
    halo = jnp.where(i > 0, halo_ref[...], 0.0)
    xcat = jnp.concatenate([halo, xc], axis=0)
    w = w_ref[...]
    y = w[GDN_CONV - 1:GDN_CONV] * xc
    for kk in range(GDN_CONV - 1):
        y = y + w[kk:kk + 1] * _shifted(xcat, GDN_CONV - 1 - kk, tm)
    return xcat, y


def _l2_heads(s, qscale_of):
    outs, rs = [], []
    for hh in range(s.shape[1] // GDN_HEAD_DIM):
        sh = s[:, hh * GDN_HEAD_DIM:(hh + 1) * GDN_HEAD_DIM]
        r = lax.rsqrt(jnp.sum(sh * sh, axis=-1, keepdims=True) + EPS)
        outs.append(sh * r)
        rs.append(r)
    return outs, rs


def _conv_specs(T, col0, tm, tc):
    cur = pl.BlockSpec((tm, tc), lambda j, i: (i, j + col0 // tc))
    halo = pl.BlockSpec((HALO, tc), lambda j, i: (jnp.maximum(i * (tm // HALO) - 1, 0), j + col0 // tc))
    wsp = pl.BlockSpec((GDN_CONV, tc), lambda j, i: (0, j + col0 // tc))
    out = pl.BlockSpec((tm, tc), lambda j, i: (i, j))
    return cur, halo, wsp, out


def _conv_fwd(proj, conv_w, col0, width, norm, name):
    T = proj.shape[0]
    tm, tc = _pick(T, CONV_ROWS), CONV_COLS
    cur, halo, wsp, out = _conv_specs(T, col0, tm, tc)
    n_q_tiles = (width // 2) // tc

    def body(x_ref, halo_ref, w_ref, o_ref):
        j, i = pl.program_id(0), pl.program_id(1)
        _, y = _conv_pre(x_ref, halo_ref, w_ref, i, tm)
        s = y * _sigmoid(y)
        if norm:
            outs, _ = _l2_heads(s, None)
            qs = jnp.where(j < n_q_tiles, GDN_HEAD_DIM ** -0.5, 1.0)
            o_ref[...] = jnp.concatenate(outs, axis=1) * qs
        else:
            o_ref[...] = s

    return _pcall(body, name=name, grid=(width // tc, T // tm), in_specs=[cur, halo, wsp], out_specs=out,
                  out_shape=jax.ShapeDtypeStruct((T, width), F32),
                  compiler_params=_params(("parallel", "parallel")))(proj, proj, conv_w)


def _conv_bwd_pre(proj, conv_w, dout, col0, width, norm, name):
    T = proj.shape[0]
    tm, tc = _pick(T, CONV_ROWS), CONV_COLS
    cur, halo, wsp, out = _conv_specs(T, col0, tm, tc)
    n_q_tiles = (width // 2) // tc

    def body(x_ref, halo_ref, w_ref, d_ref, dy_ref, dw_ref):
        j, i = pl.program_id(0), pl.program_id(1)
        xcat, y = _conv_pre(x_ref, halo_ref, w_ref, i, tm)
        sg = _sigmoid(y)
        s = y * sg
        d = d_ref[...]
        if norm:
            qs = jnp.where(j < n_q_tiles, GDN_HEAD_DIM ** -0.5, 1.0)
            d = d * qs
            outs, rs = _l2_heads(s, None)
            parts = []
            for hh, (nh, r) in enumerate(zip(outs, rs)):
                dh = d[:, hh * GDN_HEAD_DIM:(hh + 1) * GDN_HEAD_DIM]
                parts.append(r * (dh - nh * jnp.sum(dh * nh, axis=-1, keepdims=True)))
            ds = jnp.concatenate(parts, axis=1)
        else:
            ds = d
        dy = ds * sg * (1.0 + y * (1.0 - sg))
        dy_ref[...] = dy
        rows = [jnp.sum(dy * _shifted(xcat, GDN_CONV - 1 - kk, tm), axis=0, keepdims=True) for kk in range(GDN_CONV)]
        part = jnp.concatenate(rows, axis=0)

        @pl.when(i == 0)
        def _():
            dw_ref[...] = part

        @pl.when(i > 0)
        def _():
            dw_ref[...] += part

    wout = pl.BlockSpec((GDN_CONV, tc), lambda j, i: (0, j))
    return _pcall(body, name=name, grid=(width // tc, T // tm), in_specs=[cur, halo, wsp, out], out_specs=[out, wout],
                  out_shape=[jax.ShapeDtypeStruct((T, width), F32), jax.ShapeDtypeStruct((GDN_CONV, width), F32)],
                  compiler_params=_params(("parallel", "arbitrary")))(proj, proj, conv_w, dout)


def _conv_bwd_in(dy, conv_w, name):
    T, C = dy.shape
    tm, tc = _pick(T, CONV_ROWS), CONV_COLS
    nrow = T // tm

    def body(d_ref, nxt_ref, w_ref, dx_ref):
        i = pl.program_id(0)
        dc = d_ref[...]
        nxt = jnp.where(i < nrow - 1, nxt_ref[...], 0.0)
        dcat = jnp.concatenate([dc, nxt], axis=0)
        w = w_ref[...]
        dx = w[GDN_CONV - 1:GDN_CONV] * dc
        for kk in range(GDN_CONV - 1):
            s = GDN_CONV - 1 - kk
            dx = dx + w[kk:kk + 1] * pltpu.roll(dcat, tm + HALO - s, 0)[:tm]
        dx_ref[...] = dx.astype(BF16)

    cur = pl.BlockSpec((tm, tc), lambda i, j: (i, j))
    nxt = pl.BlockSpec((HALO, tc), lambda i, j: (jnp.minimum((i + 1) * (tm // HALO), T // HALO - 1), j))
    wsp = pl.BlockSpec((GDN_CONV, tc), lambda i, j: (0, j))
    return _pcall(body, name=name, grid=(nrow, C // tc), in_specs=[cur, nxt, wsp], out_specs=cur,
                  out_shape=jax.ShapeDtypeStruct((T, C), BF16),
                  compiler_params=_params(("parallel", "parallel")))(dy, dy, conv_w)


GATE_ROWS = 512


def _chunk_mask(n, lower):
    row = _iota2((n, n), 0)
    col = _iota2((n, n), 1)
    same = (row // GDN_CHUNK) == (col // GDN_CHUNK)
    tri = (row >= col) if lower else (row <= col)
    return (same & tri).astype(BF16)


def _gates_fwd(proj, a_log, dt_bias):
    T = proj.shape[0]
    tm = _pick(T, GATE_ROWS)
    c0 = (GDN_CONV_W + GDN_VW) // LANE

    def body(bl_ref, a_ref, alog_ref, dt_ref, beta_ref, g_ref, gc_ref):
        beta_ref[...] = _sigmoid(bl_ref[...])
        g = -jnp.exp(alog_ref[...]) * _softplus(a_ref[...] + dt_ref[...])
        g_ref[...] = g
        gc_ref[...] = _mdot2(_chunk_mask(tm, True), g)

    blk = lambda c: pl.BlockSpec((tm, LANE), lambda i: (i, c))
    vec = pl.BlockSpec((1, LANE), lambda i: (0, 0))
    sh = jax.ShapeDtypeStruct((T, LANE), F32)
    return _pcall(body, name="gdn_gates", grid=(T // tm,), in_specs=[blk(c0), blk(c0 + 1), vec, vec],
                  out_specs=[blk(0), blk(0), blk(0)], out_shape=[sh, sh, sh],
                  compiler_params=_params(("parallel",)))(proj, proj, a_log, dt_bias)


def _gates_bwd(proj, a_log, dt_bias, beta, g, dbeta, dgc):
    T = proj.shape[0]
    tm = _pick(T, GATE_ROWS)
    c0 = (GDN_CONV_W + GDN_VW) // LANE

    def heads_in_lanes(ref):
        lane = _iota2((tm, LANE), 1)
        out = jnp.where(lane < GDN_GROUP, ref[0], 0.0)
        for grp in range(1, GDN_V_HEADS // GDN_GROUP):
            out = out + jnp.where(lane // GDN_GROUP == grp, pltpu.roll(ref[grp], grp * GDN_GROUP, 1), 0.0)
        return out

    def body(a_ref, alog_ref, dt_ref, beta_ref, g_ref, dbeta_ref, dgc_ref, dbl_ref, da_ref, dalog_ref, ddt_ref):
        dg = _mdot2(_chunk_mask(tm, False), heads_in_lanes(dgc_ref))
        b = beta_ref[...]
        dbl_ref[...] = (heads_in_lanes(dbeta_ref) * b * (1.0 - b)).astype(BF16)
        da = dg * (-jnp.exp(alog_ref[...])) * _sigmoid(a_ref[...] + dt_ref[...])
        da_ref[...] = da.astype(BF16)
        p1 = jnp.sum(dg * g_ref[...], axis=0, keepdims=True)
        p2 = jnp.sum(da, axis=0, keepdims=True)

        @pl.when(pl.program_id(0) == 0)
        def _():
            dalog_ref[...] = p1
            ddt_ref[...] = p2

        @pl.when(pl.program_id(0) > 0)
        def _():
            dalog_ref[...] += p1
            ddt_ref[...] += p2

    blk = lambda c: pl.BlockSpec((tm, LANE), lambda i: (i, c))
    vec = pl.BlockSpec((1, LANE), lambda i: (0, 0))
    grp = pl.BlockSpec((GDN_V_HEADS // GDN_GROUP, tm, LANE), lambda i: (0, i, 0))
    shb = jax.ShapeDtypeStruct((T, LANE), BF16)
    shv = jax.ShapeDtypeStruct((1, LANE), F32)
    return _pcall(body, name="gdn_dgates", grid=(T // tm,),
                  in_specs=[blk(c0 + 1), vec, vec, blk(0), blk(0), grp, grp],
                  out_specs=[blk(0), blk(0), vec, vec], out_shape=[shb, shb, shv, shv],
                  compiler_params=_params(("arbitrary",)))(proj, a_log, dt_bias, beta, g, dbeta, dgc)


def _inv_unit_lower(Ls):
    C = Ls[0].shape[0]
    row = _iota2((C, C), 0)
    col = _iota2((C, C), 1)
    blk16 = (row // 16) == (col // 16)
    blk32 = (row // 32) == (col // 32)
    eye = (row == col).astype(F32)
    xs = [-jnp.where(blk16, L, 0.0) for L in Ls]
    inv = [eye + x for x in xs]
    for _ in range(3):
        xs = [_dot3(x, x) for x in xs]
        inv = [a + _dot3(a, x) for a, x in zip(inv, xs)]
    for mask in (blk32 & ~blk16, ~blk32):
        t = [_dot3(a, jnp.where(mask, L, 0.0)) for a, L in zip(inv, Ls)]
        inv = [a - _dot3(ti, a) for a, ti in zip(inv, t)]
    return inv


GDN_GROUP = 4
GDN_PREP_CHUNKS = 4


def _gdn_specs(T):
    C, D, E = GDN_CHUNK, GDN_HEAD_DIM, GDN_GROUP
    n = T // C
    qk = pl.BlockSpec((C, (E // 2) * D), lambda h, i: (i, h))
    vE = pl.BlockSpec((C, E * D), lambda h, i: (i, h))
    colv = pl.BlockSpec((C, LANE), lambda h, i: (i, 0))
    colo = pl.BlockSpec((None, C, LANE), lambda h, i: (h, i, 0))
    rowv = pl.BlockSpec((E, None, 1, C), lambda h, i: (h, i, 0, 0))
    st = pl.BlockSpec((E, None, D, D), lambda h, i: (h, i, 0, 0))
    am = pl.BlockSpec((E, None, C, C), lambda h, i: (h, i, 0, 0))
    return n, qk, vE, colv, colo, rowv, st, am


def _lane_col(blk, lane):
    return jnp.sum(jnp.where(_iota2(blk.shape, 1) == lane, blk, 0.0), axis=1, keepdims=True)


def _gdn_decay(gcol, grow):
    C = GDN_CHUNK
    row = _iota2((C, C), 0)
    col = _iota2((C, C), 1)
    incl = row >= col
    dm = jnp.where(incl, jnp.exp(jnp.where(incl, gcol - grow, 0.0)), 0.0)
    glast = grow[:, C - 1:C]
    return dm, jnp.exp(gcol), jnp.exp(glast), jnp.exp(glast - gcol), row > col, incl


def _gdn_prep(k, beta, gcol, grow):
    T = k.shape[0]
    C, D, B = GDN_CHUNK, GDN_HEAD_DIM, GDN_PREP_CHUNKS
    n = T // C

    def body(k_ref, b_ref, gc_ref, gr_ref, a_ref):
        idx = [(e, cb) for e in range(2) for cb in range(B)]
        kc = {cb: k_ref[cb * C:(cb + 1) * C, :] for cb in range(B)}
        lm = []
        head0 = 2 * pl.program_id(0)
        for e, cb in idx:
            beta = _lane_col(b_ref[cb * C:(cb + 1) * C, :], head0 + e)
            dm, _, _, _, strict, _ = _gdn_decay(_lane_col(gc_ref[cb * C:(cb + 1) * C, :], head0 + e), gr_ref[e, cb])
            lm.append(jnp.where(strict, _bdot(kc[cb] * beta, kc[cb], NT) * dm, 0.0))
        inv = _inv_unit_lower(lm)
        for (e, cb), a in zip(idx, inv):
            a_ref[e, cb] = a

    return _pcall(
        body, name="gdn_prep", grid=(GDN_K_HEADS, n // B),
        in_specs=[pl.BlockSpec((B * C, D), lambda h, i: (i, h)), pl.BlockSpec((B * C, LANE), lambda h, i: (i, 0)),
                  pl.BlockSpec((B * C, LANE), lambda h, i: (i, 0)), pl.BlockSpec((2, B, 1, C), lambda h, i: (h, i, 0, 0))],
        out_specs=pl.BlockSpec((2, B, C, C), lambda h, i: (h, i, 0, 0)),
        out_shape=jax.ShapeDtypeStruct((GDN_V_HEADS, n, C, C), F32),
        compiler_params=_params(("parallel", "parallel")),
    )(k, beta, gcol, grow)


def _gdn_fwd(q, k, v, beta, gcol, grow, amat):
    T = q.shape[0]
    C, D, E = GDN_CHUNK, GDN_HEAD_DIM, GDN_GROUP
    n, qk, vE, colv, colo, rowv, st, am = _gdn_specs(T)
    R = range(E)

    def body(q_ref, k_ref, v_ref, b_ref, gc_ref, gr_ref, a_ref, o_ref, s_ref, vn_ref, state):
        @pl.when(pl.program_id(1) == 0)
        def _():
            state[...] = jnp.zeros_like(state)

        qv = [q_ref[:, (e // 2) * D:(e // 2 + 1) * D] for e in R]
        kv = [k_ref[:, (e // 2) * D:(e // 2 + 1) * D] for e in R]
        vv = [v_ref[:, e * D:(e + 1) * D] for e in R]
        head0 = E * pl.program_id(0)
        beta = [_lane_col(b_ref[...], head0 + e) for e in R]
        a = [a_ref[e] for e in R]
        s = [state[e] for e in R]
        dec = [_gdn_decay(_lane_col(gc_ref[...], head0 + e), gr_ref[e]) for e in R]
        pm = [_bdot(qv[e], kv[e], NT) * dec[e][0] for e in R]
        r = [beta[e] * (vv[e] - _bdot(kv[e] * dec[e][1], s[e])) for e in R]
        vn = [_dot3(a[e], r[e]) for e in R]
        o = [_bdot(qv[e] * dec[e][1], s[e]) + _bdot(pm[e], vn[e]) for e in R]
        s2 = [dec[e][2] * s[e] + _bdot(kv[e] * dec[e][3], vn[e], TN) for e in R]
        for e in R:
            s_ref[e] = s[e]
            vn_ref[:, e * D:(e + 1) * D] = vn[e]
            o_ref[:, e * D:(e + 1) * D] = o[e]
            state[e] = s2[e]

    shv = jax.ShapeDtypeStruct((T, GDN_V_HEADS * D), F32)
    return _pcall(
        body, name="gdn_fwd", grid=(GDN_V_HEADS // E, n), in_specs=[qk, qk, vE, colv, colv, rowv, am],
        out_specs=[vE, st, vE],
        out_shape=[shv, jax.ShapeDtypeStruct((GDN_V_HEADS, n, D, D), F32), shv],
        scratch_shapes=[pltpu.VMEM((E, D, D), F32)],
        compiler_params=_params(("parallel", "arbitrary")),
    )(q, k, v, beta, gcol, grow, amat)


def _gdn_bwd(q, k, v, beta, gcol, grow, states, amat, vnew, do):
    T = q.shape[0]
    C, D, E = GDN_CHUNK, GDN_HEAD_DIM, GDN_GROUP
    n, qk, vE, colv, colo, rowv, st, am = _gdn_specs(T)
    rev = lambda spec: pl.BlockSpec(spec.block_shape, (lambda f: (lambda h, i: f(h, n - 1 - i)))(spec.index_map))
    qk, vE, colv, colo, rowv, st, am = (rev(s) for s in (qk, vE, colv, colo, rowv, st, am))
    R = range(E)

    def body(q_ref, k_ref, v_ref, b_ref, gc_ref, gr_ref, s_ref, a_ref, vn_ref, do_ref,
             dq_ref, dk_ref, dv_ref, db_ref, dgc_ref, dstate):
        @pl.when(pl.program_id(1) == 0)
        def _():
            dstate[...] = jnp.zeros_like(dstate)

        M = lambda f: [f(e) for e in R]
        rsum = lambda x: jnp.sum(x, axis=1, keepdims=True)
        qv = M(lambda e: q_ref[:, (e // 2) * D:(e // 2 + 1) * D])
        kv = M(lambda e: k_ref[:, (e // 2) * D:(e // 2 + 1) * D])
        vv = M(lambda e: v_ref[:, e * D:(e + 1) * D])
        vn = M(lambda e: vn_ref[:, e * D:(e + 1) * D])
        dov = M(lambda e: do_ref[:, e * D:(e + 1) * D])
        head0 = E * pl.program_id(0)
        beta = M(lambda e: _lane_col(b_ref[...], head0 + e))
        s = M(lambda e: s_ref[e])
        a = M(lambda e: a_ref[e])
        dsn = M(lambda e: dstate[e])
        dec = M(lambda e: _gdn_decay(_lane_col(gc_ref[...], head0 + e), gr_ref[e]))
        dm, gam, glast, tail = (M(lambda e: dec[e][i]) for i in range(4))
        strict, incl = dec[0][4], dec[0][5]
        kb = M(lambda e: kv[e] * beta[e])
        kd = M(lambda e: kv[e] * gam[e])
        qd = M(lambda e: qv[e] * gam[e])
        kt = M(lambda e: kv[e] * tail[e])
        lmat = M(lambda e: jnp.where(strict, _bdot(kb[e], kv[e], NT) * dm[e], 0.0))
        pmat = M(lambda e: _bdot(qv[e], kv[e], NT) * dm[e])
        xres = M(lambda e: vv[e] - _bdot(kd[e], s[e]))
        dvn = M(lambda e: _bdot(pmat[e], dov[e], TN) + _bdot(kt[e], dsn[e]))
        dqd = M(lambda e: _bdot(dov[e], s[e], NT))
        dp = M(lambda e: jnp.where(incl, _bdot(dov[e], vn[e], NT), 0.0))
        dkt = M(lambda e: _bdot(vn[e], dsn[e], NT))
        dr = M(lambda e: _dot3(a[e], dvn[e], TN))
        drb = M(lambda e: beta[e] * dr[e])
        dkd = M(lambda e: -_bdot(drb[e], s[e], NT))
        ds2 = M(lambda e: _bdot(qd[e], dov[e], TN) + glast[e] * dsn[e] - _bdot(kd[e], drb[e], TN))
        dl = M(lambda e: -jnp.where(strict, _bdot(dr[e], vn[e], NT), 0.0))
        dmm = M(lambda e: dl[e] * dm[e])
        dnn = M(lambda e: dp[e] * dm[e])
        emat = M(lambda e: dl[e] * lmat[e] + dp[e] * pmat[e])
        dkb = M(lambda e: _bdot(dmm[e], kv[e]))
        dk = M(lambda e: beta[e] * dkb[e] + _bdot(dmm[e], kb[e], TN) + _bdot(dnn[e], qv[e], TN)
               + gam[e] * dkd[e] + tail[e] * dkt[e])
        dq = M(lambda e: _bdot(dnn[e], kv[e]) + gam[e] * dqd[e])
        dbeta = M(lambda e: rsum(dr[e] * xres[e]) + rsum(dkb[e] * kv[e]))
        ones = jnp.ones((C, LANE), BF16)
        colsum = M(lambda e: _dot2m(emat[e], ones, TN)[:, :1])
        tails = M(lambda e: rsum(dkt[e] * kt[e]))
        lastrow = _iota2((C, 1), 0) == C - 1
        dlast = M(lambda e: jnp.sum(tails[e], axis=0, keepdims=True)
                  + glast[e] * jnp.sum(rsum(s[e] * dsn[e]), axis=0, keepdims=True))
        dgc = M(lambda e: rsum(emat[e]) - colsum[e] + rsum(dkd[e] * kd[e]) + rsum(dqd[e] * qd[e]) - tails[e]
                + jnp.where(lastrow, dlast[e], 0.0))
        lane = _iota2((C, LANE), 1)
        db_all = jnp.zeros((C, LANE), F32)
        dgc_all = jnp.zeros((C, LANE), F32)
        for e in R:
            dv_ref[:, e * D:(e + 1) * D] = drb[e]
            db_all = jnp.where(lane == e, dbeta[e], db_all)
            dgc_all = jnp.where(lane == e, dgc[e], dgc_all)
            dstate[e] = ds2[e]
        db_ref[...] = db_all
        dgc_ref[...] = dgc_all
        for kh in range(E // 2):
            dq_ref[:, kh * D:(kh + 1) * D] = dq[2 * kh] + dq[2 * kh + 1]
            dk_ref[:, kh * D:(kh + 1) * D] = dk[2 * kh] + dk[2 * kh + 1]

    shq = jax.ShapeDtypeStruct((T, GDN_K_HEADS * D), F32)
    shv = jax.ShapeDtypeStruct((T, GDN_V_HEADS * D), F32)
    shc = jax.ShapeDtypeStruct((GDN_V_HEADS // E, T, LANE), F32)
    return _pcall(
        body, name="gdn_bwd", grid=(GDN_V_HEADS // E, n),
        in_specs=[qk, qk, vE, colv, colv, rowv, st, am, vE, vE],
        out_specs=[qk, qk, vE, colo, colo], out_shape=[shq, shq, shv, shc, shc],
        scratch_shapes=[pltpu.VMEM((E, D, D), F32)],
        compiler_params=_params(("parallel", "arbitrary")),
    )(q, k, v, beta, gcol, grow, states, amat, vnew, do)


def _outgate_fwd(o, proj, gain):
    T = o.shape[0]
    tm, tc = _pick(T, CONV_ROWS), CONV_COLS
    z0 = GDN_CONV_W // tc

    def body(o_ref, z_ref, g_ref, y_ref):
        z = z_ref[...]
        sz = z * _sigmoid(z)
        parts = []
        for hh in range(tc // GDN_HEAD_DIM):
            oh = o_ref[:, hh * GDN_HEAD_DIM:(hh + 1) * GDN_HEAD_DIM]
            r = lax.rsqrt(jnp.mean(oh * oh, axis=-1, keepdims=True) + EPS)
            parts.append(oh * r * g_ref[...])
        y_ref[...] = (jnp.concatenate(parts, axis=1) * sz).astype(BF16)

    blk = pl.BlockSpec((tm, tc), lambda i, j: (i, j))
    return _pcall(body, name="gdn_outgate", grid=(T // tm, GDN_VW // tc),
                  in_specs=[blk, pl.BlockSpec((tm, tc), lambda i, j: (i, j + z0)), pl.BlockSpec((1, GDN_HEAD_DIM), lambda i, j: (0, 0))],
                  out_specs=blk, out_shape=jax.ShapeDtypeStruct((T, GDN_VW), BF16),
                  compiler_params=_params(("parallel", "parallel")))(o, proj, gain)


def _outgate_bwd(dy, o, proj, gain):
    T = o.shape[0]
    tm, tc = _pick(T, CONV_ROWS), CONV_COLS
    z0 = GDN_CONV_W // tc
    nh = tc // GDN_HEAD_DIM

    def body(dy_ref, o_ref, z_ref, g_ref, do_ref, dz_ref, dg_ref):
        z = z_ref[...]
        sg = _sigmoid(z)
        sz = z * sg
        dy = dy_ref[...]
        dgain = jnp.zeros((1, GDN_HEAD_DIM), F32)
        dos, ys = [], []
        for hh in range(nh):
            sl = slice(hh * GDN_HEAD_DIM, (hh + 1) * GDN_HEAD_DIM)
            oh = o_ref[:, sl]
            r = lax.rsqrt(jnp.mean(oh * oh, axis=-1, keepdims=True) + EPS)
            xh = oh * r
            dn = dy[:, sl] * sz[:, sl]
            dgain = dgain + jnp.sum(dn * xh, axis=0, keepdims=True)
            dxh = dn * g_ref[...]
            dos.append(r * (dxh - xh * jnp.mean(dxh * xh, axis=-1, keepdims=True)))
            ys.append(xh * g_ref[...])
        do_ref[...] = jnp.concatenate(dos, axis=1)
        dz_ref[...] = (dy * jnp.concatenate(ys, axis=1) * sg * (1.0 + z * (1.0 - sg))).astype(BF16)
        first = (pl.program_id(0) == 0) & (pl.program_id(1) == 0)

        @pl.when(first)
        def _():
            dg_ref[...] = dgain

        @pl.when(jnp.logical_not(first))
        def _():
            dg_ref[...] += dgain

    blk = pl.BlockSpec((tm, tc), lambda i, j: (i, j))
    vec = pl.BlockSpec((1, GDN_HEAD_DIM), lambda i, j: (0, 0))
    return _pcall(body, name="gdn_doutgate", grid=(T // tm, GDN_VW // tc),
                  in_specs=[blk, blk, pl.BlockSpec((tm, tc), lambda i, j: (i, j + z0)), vec],
                  out_specs=[blk, blk, vec],
                  out_shape=[jax.ShapeDtypeStruct((T, GDN_VW), F32), jax.ShapeDtypeStruct((T, GDN_VW), BF16),
                             jax.ShapeDtypeStruct((1, GDN_HEAD_DIM), F32)],
                  compiler_params=_params(("arbitrary", "arbitrary")))(dy, o, proj, gain)


def _pad_lanes(vec):
    return jnp.pad(vec.reshape(1, -1), ((0, 0), (0, LANE - vec.shape[-1])))


def _head_rows(a):
    T = a.shape[0]
    return a[:, :GDN_V_HEADS].T.reshape(GDN_V_HEADS, T // GDN_CHUNK, 1, GDN_CHUNK)


def _gdn_pad_in(w_in):
    c = GDN_CONV_W + GDN_VW
    z = jnp.zeros(w_in.shape[:-1] + (LANE - GDN_V_HEADS,), w_in.dtype)
    return jnp.concatenate([w_in[..., :c + GDN_V_HEADS], z, w_in[..., c + GDN_V_HEADS:], z], axis=-1)


def _gdn_unpad_in(dw):
    c = GDN_CONV_W + GDN_VW
    return jnp.concatenate([dw[..., :c + GDN_V_HEADS], dw[..., c + LANE:c + LANE + GDN_V_HEADS]], axis=-1)


def _gdn_mixer_fwd(h, g, w_in_pad, conv_w, a_log, dt_bias, out_gain, w_out):
    T = h.shape[0]
    hn = _rms_fwd(h, g, "gdn_norm")
    proj = _mm(hn, w_in_pad, "nn", name="gdn_in")
    qk = _conv_fwd(proj, conv_w, 0, 2 * GDN_KW, True, "gdn_conv_qk")
    vv = _conv_fwd(proj, conv_w, 2 * GDN_KW, GDN_VW, False, "gdn_conv_v")
    alog, dtb = _pad_lanes(a_log), _pad_lanes(dt_bias)
    beta, gl, gc = _gates_fwd(proj, alog, dtb)
    grow = _head_rows(gc)
    qn, kn = qk[:, :GDN_KW], qk[:, GDN_KW:]
    amat = _gdn_prep(kn, beta, gc, grow)
    o, states, vnew = _gdn_fwd(qn, kn, vv, beta, gc, grow, amat)
    gain = out_gain.reshape(1, GDN_HEAD_DIM)
    y = _outgate_fwd(o, proj, gain)
    h2 = _mm(y, w_out, "nn", res=h, name="gdn_out")
    return h2, (h, hn, proj, qn, kn, vv, beta, gl, gc, grow, o, states, amat, vnew, y, alog, dtb, gain)


def _gdn_mixer_bwd(dh2, saved, g, w_in_pad, conv_w, w_out):
    h, hn, proj, qn, kn, vv, beta, gl, gc, grow, o, states, amat, vnew, y, alog, dtb, gain = saved
    T = h.shape[0]
    dy = _mm(dh2, w_out, "nt", name="gdn_dy")
    dw_out = _mm(y, dh2, "tn", out_dtype=BF16, name="gdn_dwout")
    do, dz, dgain = _outgate_bwd(dy, o, proj, gain)
    dq, dk, dv, dbeta, dgc = _gdn_bwd(qn, kn, vv, beta, gc, grow, states, amat, vnew, do)
    dqk = jnp.concatenate([dq, dk], axis=1)
    dy_qk, dcw_qk = _conv_bwd_pre(proj, conv_w, dqk, 0, 2 * GDN_KW, True, "gdn_dconv_qk")
    dy_v, dcw_v = _conv_bwd_pre(proj, conv_w, dv, 2 * GDN_KW, GDN_VW, False, "gdn_dconv_v")
    dx_qk = _conv_bwd_in(dy_qk, conv_w[:, :2 * GDN_KW], "gdn_dconvin_qk")
    dx_v = _conv_bwd_in(dy_v, conv_w[:, 2 * GDN_KW:], "gdn_dconvin_v")
    dbl, da, dalog, ddt = _gates_bwd(proj, alog, dtb, beta, gl, dbeta, dgc)
    dproj = jnp.concatenate([dx_qk, dx_v, dz, dbl, da], axis=1)
    dw_in_pad = _mm(hn, dproj, "tn", out_dtype=BF16, name="gdn_dwin")
    dhn = _mm(dproj, w_in_pad, "nt", name="gdn_dhn")
    dh, dg = _rms_bwd(dhn, h, g, dh2, "gdn_dnorm")
    dconv = jnp.concatenate([dcw_qk, dcw_v], axis=1)
    return (dh, dg, _gdn_unpad_in(dw_in_pad), dconv, dalog[0, :GDN_V_HEADS], ddt[0, :GDN_V_HEADS],
            dgain.reshape(GDN_HEAD_DIM), dw_out)


def _instances(full):
    out = {}
    for n, a in full.items():
        if n.startswith("ffn_"):
            for i in range(2):
                for j in range(2):
                    out[(n, i, j)] = a[i, j]
        elif n in ("mix_norm", "ple_norm", "ple_w_gate", "ple_w_proj"):
            for i in range(2):
                out[(n, i)] = a[i]
        else:
            out[(n,)] = a[0]
    return out


def _stacked(inst):
    out = {}
    for n in dict.fromkeys(k[0] for k in inst):
        if n.startswith("ffn_"):
            out[n] = jnp.stack([jnp.stack([inst[(n, i, j)] for j in range(2)]) for i in range(2)])
        elif n in ("mix_norm", "ple_norm", "ple_w_gate", "ple_w_proj"):
            out[n] = jnp.stack([inst[(n, i)] for i in range(2)])
        else:
            out[n] = inst[(n,)][None]
    return out


def _local_step(x, p, target, w, late_shards=(), late_weights=None, early_grads=None, first_shards=(), first_weights=None):
    w = dict(w)
    ffn = lambda i, j: (w[("ffn_norm", i, j)], w[("ffn_w_gate", i, j)], w[("ffn_w_up", i, j)], w[("ffn_w_down", i, j)])
    h = x
    tape = []
    for i in range(2):
        if i == 0 and first_weights is not None:
            def wd_of(gathered):
                w.update(first_weights(gathered))
                return w[("ffn_w_down", 0, 0)]
            h, s1 = _ffn_fwd(h, w[("ffn_norm", 0, 0)], w[("ffn_w_gate", 0, 0)], w[("ffn_w_up", 0, 0)], None, "ffn0a",
                             first_shards, wd_of)
        else:
            h, s1 = _ffn_fwd(h, *ffn(i, 0), f"ffn{i}a")
        if i == 0:
            def w_out_of(gathered):
                if late_weights is not None:
                    w.update(late_weights(gathered))
                return w[("att_w_out",)]
            h, s2, _ = _att_fwd(h, w[("mix_norm", 0)], w[("att_w_in",)], w_out_of, w[("att_q_norm",)],
                                w[("att_k_norm",)], w[("att_sinks",)], late_shards)
        else:
            gdn_in_pad = _gdn_pad_in(w[("gdn_w_in",)])
            h, s2 = _gdn_mixer_fwd(h, w[("mix_norm", 1)], gdn_in_pad, w[("gdn_conv_w",)], w[("gdn_a_log",)],
                                   w[("gdn_dt_bias",)], w[("gdn_out_norm",)], w[("gdn_w_out",)])
        h, s3 = _ffn_fwd(h, *ffn(i, 1), f"ffn{i}b")
        h, s4 = _ple_fwd(h, p[i], w[("ple_norm", i)], w[("ple_w_gate", i)], w[("ple_w_proj", i)], f"ple{i}")
        tape.append((s1, s2, s3, s4))

    loss, dh = _loss_head(h, target)

    g = {}
    rode = []
    for i in (1, 0):
        s1, s2, s3, s4 = tape[i]
        dh, g[("ple_norm", i)], g[("ple_w_gate", i)], g[("ple_w_proj", i)] = _ple_bwd(
            dh, s4, p[i], w[("ple_norm", i)], w[("ple_w_gate", i)], f"ple{i}")
        dh, g[("ffn_norm", i, 1)], g[("ffn_w_gate", i, 1)], g[("ffn_w_up", i, 1)], g[("ffn_w_down", i, 1)] = _ffn_bwd(
            dh, s3, *ffn(i, 1), f"ffn{i}b")
        if i == 0:
            ride = early_grads(g) if early_grads is not None else ()
            (dh, g[("mix_norm", 0)], g[("att_w_in",)], g[("att_w_out",)], g[("att_q_norm",)], g[("att_k_norm",)],
             g[("att_sinks",)], rode) = _att_bwd(dh, s2, w[("mix_norm", 0)], w[("att_w_in",)], w[("att_w_out",)], ride)
        else:
            (dh, g[("mix_norm", 1)], g[("gdn_w_in",)], g[("gdn_conv_w",)], g[("gdn_a_log",)], g[("gdn_dt_bias",)],
             g[("gdn_out_norm",)], g[("gdn_w_out",)]) = _gdn_mixer_bwd(
                dh, s2, w[("mix_norm", 1)], gdn_in_pad, w[("gdn_conv_w",)], w[("gdn_w_out",)])
        dh, g[("ffn_norm", i, 0)], g[("ffn_w_gate", i, 0)], g[("ffn_w_up", i, 0)], g[("ffn_w_down", i, 0)] = _ffn_bwd(
            dh, s1, *ffn(i, 0), f"ffn{i}a")
    return loss, dh, g, rode


MESH = pl.DeviceIdType.MESH


def _place():
    x, y, c = lax.axis_index("x"), lax.axis_index("y"), lax.axis_index("c")
    others = [((1 - x, y), 2 * (1 - x) + y), ((x, 1 - y), 2 * x + (1 - y)), ((1 - x, 1 - y), 2 * (1 - x) + (1 - y))]
    return x, y, c, 4 * x + 2 * y + c, 2 * x + y, (x, y, 1 - c), others


def _comm_call(body, arrays, out_shape, n_sems, name):
    hbm = pl.BlockSpec(memory_space=pl.ANY)
    n = len(arrays)
    return _pcall(
        body, name=name, in_specs=[hbm] * n, out_specs=[hbm] * len(out_shape), out_shape=out_shape,
        scratch_shapes=[pltpu.SemaphoreType.DMA((n, n_sems)), pltpu.SemaphoreType.DMA((n, n_sems)),
                        pltpu.SemaphoreType.DMA((n, N_CHIP))],
        compiler_params=pltpu.CompilerParams(has_side_effects=True),
    )(*arrays)


def _gather_protocol(ins, outs, send_sems, recv_sems, local_sems):
    n = len(ins)
    x, y, c, me, my_chip, sibling, others = _place()

    def copy(a, k, block, to, src=None):
        dst = outs[a].at[block]
        return pltpu.make_async_remote_copy(
            src_ref=dst if src is None else src, dst_ref=dst, send_sem=send_sems.at[a, k],
            recv_sem=recv_sems.at[a, k], device_id=to, device_id_type=MESH)

    local = [pltpu.make_async_copy(ins[a], outs[a].at[me], local_sems.at[a, 0]) for a in range(n)]
    first = []
    for a in range(n):
        first.append(copy(a, 0, me, sibling, src=ins[a]))
        first += [copy(a, 1 + j, me, (*chip, c), src=ins[a]) for j, (chip, _) in enumerate(others)]

    def start():
        for cp in local + first:
            cp.start()

    def finish():
        passed = []
        for a in range(n):
            for j, (chip, chip_idx) in enumerate(others):
                blk = 2 * chip_idx + c
                copy(a, 1 + j, blk, (x, y, c)).wait_recv()
                fwd = copy(a, 4 + j, blk, sibling)
                fwd.start()
                passed.append(fwd)
        for a in range(n):
            copy(a, 0, 2 * my_chip + (1 - c), (x, y, c)).wait_recv()
            for j, (chip, chip_idx) in enumerate(others):
                copy(a, 4 + j, 2 * chip_idx + (1 - c), (x, y, c)).wait_recv()
        for cp in first + passed:
            cp.wait_send()
        for cp in local:
            cp.wait()

    return start, finish


def _all_gather(arrays):
    n = len(arrays)

    def body(*refs):
        start, finish = _gather_protocol(refs[:n], refs[n:2 * n], *refs[2 * n:])
        start()
        finish()

    out_shape = [jax.ShapeDtypeStruct((N_DEV,) + a.shape, a.dtype) for a in arrays]
    return _comm_call(body, arrays, out_shape, N_DEV - 1, "gather_weights")


def _exchange_sibling(arrays, name):
    n = len(arrays)

    def body(*refs):
        ins, got = refs[:n], refs[n:2 * n]
        send_sems, recv_sems, _ = refs[2 * n:]
        x, y, c, me, my_chip, sibling, others = _place()
        remote = []
        for a in range(n):
            for chip in range(N_CHIP):
                rc = pltpu.make_async_remote_copy(
                    src_ref=ins[a].at[2 * chip + (1 - c)], dst_ref=got[a].at[chip], send_sem=send_sems.at[a, chip],
                    recv_sem=recv_sems.at[a, chip], device_id=sibling, device_id_type=MESH)
                rc.start()
                remote.append(rc)
        for rc in remote:
            rc.wait()

    half = [jax.ShapeDtypeStruct((N_CHIP,) + a.shape[1:], a.dtype) for a in arrays]
    return _comm_call(body, arrays, half, N_CHIP, name)


def _chips_protocol(ins, outs, send_sems, recv_sems, local_sems):
    n = len(ins)
    x, y, c, me, my_chip, sibling, others = _place()
    local = [pltpu.make_async_copy(ins[a].at[my_chip], outs[a].at[my_chip], local_sems.at[a, 0]) for a in range(n)]
    remote = [pltpu.make_async_remote_copy(
        src_ref=ins[a].at[chip_idx], dst_ref=outs[a].at[my_chip], send_sem=send_sems.at[a, j],
        recv_sem=recv_sems.at[a, j], device_id=(*chip, c), device_id_type=MESH)
        for a in range(n) for j, (chip, chip_idx) in enumerate(others)]

    def start():
        for cp in local + remote:
            cp.start()

    def finish():
        for cp in remote + local:
            cp.wait()

    return start, finish


def _exchange_chips(arrays, name):
    n = len(arrays)

    def body(*refs):
        start, finish = _chips_protocol(refs[:n], refs[n:2 * n], *refs[2 * n:])
        start()
        finish()

    out_shape = [jax.ShapeDtypeStruct(a.shape, a.dtype) for a in arrays]
    return _comm_call(body, arrays, out_shape, N_CHIP - 1, name)


def _as_rows(a, lead):
    shp = a.shape
    return a.reshape(shp[:lead] + (math.prod(shp[lead:-1]), shp[-1]))


def _row_tile(rows, cap=512):
    if rows <= cap:
        return rows
    for t in range(cap - cap % 8, 0, -8):
        if rows % t == 0:
            return t
    return rows


def _pair_sum(send, got, name):
    a3, b3 = _as_rows(send, 1), _as_rows(got, 1)
    _, rows, last = b3.shape
    tr = _row_tile(rows)

    def body(c_ref, a_ref, b_ref, o_ref):
        o_ref[...] = (a_ref[...].astype(F32) + b_ref[...].astype(F32)).astype(o_ref.dtype)

    core = lax.axis_index("c").astype(jnp.int32).reshape(1)
    out = _pcall(
        body, name=name,
        grid_spec=pltpu.PrefetchScalarGridSpec(
            num_scalar_prefetch=1, grid=(N_CHIP, rows // tr),
            in_specs=[pl.BlockSpec((None, tr, last), lambda k, i, c_ref: (2 * k + c_ref[0], i, 0)),
                      pl.BlockSpec((None, tr, last), lambda k, i, c_ref: (k, i, 0))],
            out_specs=pl.BlockSpec((None, tr, last), lambda k, i, c_ref: (k, i, 0))),
        out_shape=jax.ShapeDtypeStruct(b3.shape, got.dtype), compiler_params=_params(("parallel", "parallel")),
    )(core, a3, b3)
    return out.reshape(got.shape)


def _adamw(parts, w, m, v, name):
    lead, (rows, last) = w.shape[:-2], w.shape[-2:]
    nl = len(lead)
    tr = _row_tile(rows)
    c1 = 1.0 / (1.0 - ADAM_B1 ** ADAM_STEP)
    c2 = 1.0 / (1.0 - ADAM_B2 ** ADAM_STEP)

    def body(p_ref, w_ref, m_ref, v_ref, g_ref, d_ref, nm_ref, nv_ref):
        g = p_ref[0].astype(F32)
        for chip in range(1, N_CHIP):
            g = g + p_ref[chip].astype(F32)
        mn = ADAM_B1 * m_ref[...] + (1.0 - ADAM_B1) * g
        vn = ADAM_B2 * v_ref[...] + (1.0 - ADAM_B2) * (g * g)
        g_ref[...] = g
        nm_ref[...] = mn
        nv_ref[...] = vn
        d_ref[...] = -ADAM_LR * ((mn * c1) / (jnp.sqrt(vn * c2) + ADAM_EPS) + ADAM_WD * w_ref[...])

    row = pl.BlockSpec((None,) * nl + (tr, last), lambda *ix: ix + (0,))
    part = pl.BlockSpec((N_CHIP,) + (None,) * nl + (tr, last), lambda *ix: (0,) + ix + (0,))
    sh = jax.ShapeDtypeStruct(w.shape, F32)
    return _pcall(body, name=name, grid=lead + (rows // tr,), in_specs=[part, row, row, row],
                  out_specs=[row, row, row, row], out_shape=[sh, sh, sh, sh],
                  compiler_params=_params(("parallel",) * (nl + 1)))(parts, w, m, v)


def _pack(pieces, row_align):
    rows, offs, r = [], [], 0
    for a in pieces:
        flat = a.reshape(-1)
        nr = -(-flat.shape[0] // PACK_W)
        flat = jnp.pad(flat, (0, nr * PACK_W - flat.shape[0]))
        rows.append(flat.reshape(nr, PACK_W))
        offs.append(r)
        r += nr
    pad = (-r) % row_align
    if pad:
        rows.append(jnp.zeros((pad, PACK_W), pieces[0].dtype))
    return jnp.concatenate(rows, axis=0), offs


def _unpack(flat, offs, shapes):
    out = []
    for off, shp in zip(offs, shapes):
        size = math.prod(shp)
        nr = -(-size // PACK_W)
        out.append(flat[..., off:off + nr, :].reshape(flat.shape[:-2] + (nr * PACK_W,))[..., :size].reshape(flat.shape[:-2] + tuple(shp)))
    return out


def _to_full(gathered, axis):
    z = jnp.moveaxis(gathered, 0, axis)
    shp = list(z.shape)
    return z.reshape(shp[:axis] + [shp[axis] * shp[axis + 1]] + shp[axis + 2:])


def _to_shards(full, axis):
    shp = list(full.shape)
    z = full.reshape(shp[:axis] + [N_DEV, shp[axis] // N_DEV] + shp[axis + 1:])
    return jnp.moveaxis(z, axis, 0)


def kernel(x, p, ffn_norm, ffn_w_gate, ffn_w_up, ffn_w_down, mix_norm, att_w_in, att_q_norm, att_k_norm, att_sinks, att_w_out, gdn_w_in, gdn_conv_w, gdn_a_log, gdn_dt_bias, gdn_out_norm, gdn_w_out, ple_norm, ple_w_gate, ple_w_proj, loss_target, m_ffn_norm, m_ffn_w_gate, m_ffn_w_up, m_ffn_w_down, m_mix_norm, m_att_w_in, m_att_q_norm, m_att_k_norm, m_att_sinks, m_att_w_out, m_gdn_w_in, m_gdn_conv_w, m_gdn_a_log, m_gdn_dt_bias, m_gdn_out_norm, m_gdn_w_out, m_ple_norm, m_ple_w_gate, m_ple_w_proj, v_ffn_norm, v_ffn_w_gate, v_ffn_w_up, v_ffn_w_down, v_mix_norm, v_att_w_in, v_att_q_norm, v_att_k_norm, v_att_sinks, v_att_w_out, v_gdn_w_in, v_gdn_conv_w, v_gdn_a_log, v_gdn_dt_bias, v_gdn_out_norm, v_gdn_w_out, v_ple_norm, v_ple_w_gate, v_ple_w_proj):
    args = dict(locals())
    wts = {n: args[n] for n in WEIGHTS}
    mom = {n: args["m_" + n] for n in WEIGHTS}
    var = {n: args["v_" + n] for n in WEIGHTS}
    axis = dict(SHARDED)
    vecs = [n for n, _ in SHARDED[:SMALL_SHARDED]]
    small = vecs + list(REPLICATED)
    small_shapes = [wts[n].shape for n in small]
    lead = lambda n: 2 if n.startswith("ffn_") else 1

    def stack_of(arrays, name, idxs):
        return jnp.stack([arrays[name][idx] if idx else arrays[name][0] for idx in idxs])

    def full_instances(gathered, group):
        out = {}
        for (name, idxs), g in zip(group, gathered):
            whole = _to_full(g, axis[name] - lead(name) + 1)
            for k, idx in enumerate(idxs):
                out[(name,) + idx] = whole[k]
        return out

    def shard_stacks(g, group):
        return [_to_shards(jnp.stack([g[(name,) + idx] for idx in idxs]), axis[name] - lead(name) + 1)
                for name, idxs in group]

    vec_pack, voffs = _pack([wts[n] for n in vecs], 8)
    early = _all_gather([stack_of(wts, n, idxs).astype(BF16) for n, idxs in EARLY] + [vec_pack])
    w = full_instances(early[:-1], EARLY)
    vec_full = {n: _to_full(piece, axis[n]) for n, piece in
                zip(vecs, _unpack(early[-1], voffs, [wts[n].shape for n in vecs]))}
    w.update(_instances({**vec_full, **{n: wts[n] for n in REPLICATED}}))
    first_shards = [stack_of(wts, n, idxs).astype(BF16) for n, idxs in FIRST]
    late_shards = [stack_of(wts, n, idxs).astype(BF16) for n, idxs in LATE]

    def early_grads(g):
        send = shard_stacks(g, RIDE)
        got = _exchange_sibling(send, "exchange_sibling_early")
        return [_pair_sum(p_, q_, f"pair_sum_early_{i}") for i, (p_, q_) in enumerate(zip(send, got))]

    loss, grad_x, g, rode = _local_step(x[0], p[:, 0], loss_target[0], w, late_shards,
                                        lambda gathered: full_instances(gathered, LATE), early_grads,
                                        first_shards, lambda gathered: full_instances(gathered, FIRST))

    gs = _stacked({k: v for k, v in g.items() if k[0] in small})
    vec_shards = [_to_shards(gs[n], axis[n]) for n in vecs]
    small_send = jnp.stack([_pack([sh[d] for sh in vec_shards] + [gs[n] for n in REPLICATED] + [loss.reshape(1)], 8)[0]
                            for d in range(N_DEV)])
    send = shard_stacks(g, FINAL) + [small_send]
    got = _exchange_sibling(send, "exchange_sibling_final")
    chip_sums = [_pair_sum(p_, q_, f"pair_sum_final_{i}") for i, (p_, q_) in enumerate(zip(send, got))]
    last = _exchange_chips(chip_sums, "exchange_chips_final")

    pieces = {}
    for (name, idxs), part in list(zip(RIDE, rode)) + list(zip(FINAL, last[:-1])):
        for k, idx in enumerate(idxs):
            pieces[(name,) + idx] = part[:, k]
    outs = {}
    for n, _ in SHARDED[SMALL_SHARDED:]:
        if lead(n) == 2:
            part = jnp.stack([jnp.stack([pieces[(n, i, j)] for j in range(2)], axis=1) for i in range(2)], axis=1)
        elif (n, 0) in pieces:
            part = jnp.stack([pieces[(n, i)] for i in range(2)], axis=1)
        else:
            part = pieces[(n,)][:, None]
        outs[n] = _adamw(part, wts[n], mom[n], var[n], f"adamw_{n}")
    filler = [jnp.zeros((1,), F32)]
    small_w, soffs = _pack([wts[n] for n in small] + filler, 8)
    small_m, _ = _pack([mom[n] for n in small] + filler, 8)
    small_v, _ = _pack([var[n] for n in small] + filler, 8)
    small_out = [_unpack(z, soffs, small_shapes + [(1,)]) for z in _adamw(last[-1], small_w, small_m, small_v, "adamw_small")]
    loss = small_out[0][-1][0]
    for i, n in enumerate(small):
        outs[n] = [small_out[k][i] for k in range(4)]
    result = [loss, grad_x[None]]
    for k in range(4):
        result += [outs[n][k] for n in WEIGHTS]
    return tuple(result)
```

```python
import math

import jax
import jax.numpy as jnp
from jax import lax
from jax.experimental import pallas as pl
from jax.experimental.pallas import tpu as pltpu

F32 = jnp.float32
BF16 = jnp.bfloat16

N_DEV = 8
N_CHIP = 4
D_MODEL = 1024
D_FF = 2816
PLE_DIM = 256
HEAD_DIM = 64
SB_HEADS = 8
SWA_HEADS = 8
SWA_KV_HEADS = 2
SWA_GROUP = SWA_HEADS // SWA_KV_HEADS
WINDOW = 128
Q_BLOCK = 128
GDN_K_HEADS = 8
GDN_V_HEADS = 16
GDN_HEAD_DIM = 128
GDN_CONV = 4
GDN_CHUNK = 64
EPS = 1e-6
SB_W = SB_HEADS * HEAD_DIM
SWA_QW = SWA_HEADS * HEAD_DIM
SWA_KVW = SWA_KV_HEADS * HEAD_DIM
ATT_IN = 3 * SB_W + SWA_QW + 2 * SWA_KVW
GDN_KW = GDN_K_HEADS * GDN_HEAD_DIM
GDN_VW = GDN_V_HEADS * GDN_HEAD_DIM
GDN_CONV_W = 2 * GDN_KW + GDN_VW
GDN_IN = GDN_CONV_W + GDN_VW + 2 * GDN_V_HEADS
GDN_IN_PAD = GDN_CONV_W + GDN_VW + 2 * 128

ADAM_LR = 0.001
ADAM_B1 = 0.9
ADAM_B2 = 0.999
ADAM_EPS = 1e-08
ADAM_WD = 0.01
ADAM_STEP = 10

LANE = 128
VMEM_LIMIT = 56 * 1024 * 1024
MM_TILE_BUDGET = 40 * 1024 * 1024
PACK_W = 1024

NN = ((1,), (0,))
NT = ((1,), (1,))
TN = ((0,), (0,))

SHARDED = (
    ("ffn_norm", 2), ("gdn_conv_w", 2),
    ("ffn_w_gate", 3), ("ffn_w_up", 3), ("ffn_w_down", 2), ("att_w_in", 2), ("att_w_out", 1),
    ("gdn_w_in", 2), ("gdn_w_out", 1), ("ple_w_gate", 1), ("ple_w_proj", 2),
)
SMALL_SHARDED = 2
REPLICATED = ("mix_norm", "att_q_norm", "att_k_norm", "att_sinks", "gdn_a_log", "gdn_dt_bias",
              "gdn_out_norm", "ple_norm")
WEIGHTS = ("ffn_norm", "ffn_w_gate", "ffn_w_up", "ffn_w_down", "mix_norm", "att_w_in", "att_q_norm",
           "att_k_norm", "att_sinks", "att_w_out", "gdn_w_in", "gdn_conv_w", "gdn_a_log", "gdn_dt_bias",
           "gdn_out_norm", "gdn_w_out", "ple_norm", "ple_w_gate", "ple_w_proj")


_FFN_REST = [(0, 1), (1, 0), (1, 1)]
EARLY = [("ffn_w_gate", [(0, 0)]), ("ffn_w_up", [(0, 0)])]
FIRST = [("ffn_w_down", [(0, 0)]), ("att_w_in", [()])]
LATE = ([(n, [idx]) for n in ("ffn_w_gate", "ffn_w_up", "ffn_w_down") for idx in _FFN_REST]
        + [("att_w_out", [()]), ("gdn_w_in", [()]), ("gdn_w_out", [()]),
           ("ple_w_gate", [(0,), (1,)]), ("ple_w_proj", [(0,), (1,)])])
RIDE = [e for e in LATE if e[0] != "att_w_out"]
FINAL = EARLY + FIRST + [("att_w_out", [()])]


def _pcall(body, **kw):
    return pl.pallas_call(body, **kw)


def _params(sem=None):
    if sem is None:
        return pltpu.CompilerParams(vmem_limit_bytes=VMEM_LIMIT)
    return pltpu.CompilerParams(dimension_semantics=sem, vmem_limit_bytes=VMEM_LIMIT)


def _ride_specs(ride, out_shapes, n_sems):
    hbm = pl.BlockSpec(memory_space=pl.ANY)
    n = len(ride)
    sems = [pltpu.SemaphoreType.DMA((n, n_sems)), pltpu.SemaphoreType.DMA((n, n_sems)),
            pltpu.SemaphoreType.DMA((n, N_CHIP))] if n else []
    return [hbm] * n, [hbm] * len(out_shapes), sems


def _dot(a, b, dims=NN):
    return lax.dot_general(a, b, (dims, ((), ())), preferred_element_type=F32)


def _bdot(a, b, dims=NN):
    return _dot(a.astype(BF16), b.astype(BF16), dims)


def _split(a):
    hi = a.astype(BF16)
    lo = (a - hi.astype(F32)).astype(BF16)
    return hi, lo


def _dot3(a, b, dims=NN):
    ah, al = _split(a)
    bh, bl = _split(b)
    return _dot(ah, bh, dims) + (_dot(ah, bl, dims) + _dot(al, bh, dims))


def _dot2m(a, m, dims=NN):
    ah, al = _split(a)
    return _dot(ah, m, dims) + _dot(al, m, dims)


def _mdot2(m, a, dims=NN):
    ah, al = _split(a)
    return _dot(m, ah, dims) + _dot(m, al, dims)


def _sigmoid(x):
    return 1.0 / (1.0 + jnp.exp(-x))


def _softplus(x):
    return jnp.maximum(x, 0.0) + jnp.log(1.0 + jnp.exp(-jnp.abs(x)))


def _pick(n, cap):
    if n <= cap:
        return n
    for t in range(cap - cap % LANE, 0, -LANE):
        if n % t == 0:
            return t
    raise ValueError(f"no tile for {n} under {cap}")


def _iota2(shape, axis):
    return lax.broadcasted_iota(jnp.int32, shape, axis)


def _mm(a, b, mode, out_dtype=F32, res=None, alpha=1.0, a2=None, b2=None, name="mm"):
    if mode == "nn":
        (M, K), N = a.shape, b.shape[1]
    elif mode == "nt":
        (M, K), N = a.shape, b.shape[0]
    else:
        (K, M), N = a.shape, b.shape[1]
    tn, tk = _pick(N, 1408), _pick(K, 1024 if mode == "tn" else 1408)
    nk = K // tk
    pairs = 1 if a2 is None else 2

    def tile_bytes(tm):
        per = pairs * tk * (tm * a.dtype.itemsize + tn * b.dtype.itemsize) + tm * tn * jnp.dtype(out_dtype).itemsize
        return 2 * (per + (tm * tn * 4 if res is not None else 0)) + (tm * tn * 4 if nk > 1 else 0)

    tm = next(t for t in (_pick(M, c) for c in ((1408,) if mode == "tn" else (2048, 1024, 512))) if tile_bytes(t) <= MM_TILE_BUDGET or t <= 512)
    dims = {"nn": NN, "nt": NT, "tn": TN}[mode]
    a_spec = pl.BlockSpec((tk, tm), lambda i, j, k: (k, i)) if mode == "tn" else pl.BlockSpec((tm, tk), lambda i, j, k: (i, k))
    b_spec = pl.BlockSpec((tn, tk), lambda i, j, k: (j, k)) if mode == "nt" else pl.BlockSpec((tk, tn), lambda i, j, k: (k, j))
    o_spec = pl.BlockSpec((tm, tn), lambda i, j, k: (i, j))
    two = a2 is not None
    has_res = res is not None
    a2_spec, b2_spec = a_spec, b_spec
    if two and a2.shape != a.shape:
        assert nk == 1 and mode == "nn" and a2.shape[0] == M and b2.shape[1] == N
        a2_spec = pl.BlockSpec((tm, a2.shape[1]), lambda i, j, k: (i, 0))
        b2_spec = pl.BlockSpec((a2.shape[1], tn), lambda i, j, k: (0, j))

    def body(*refs):
        refs = list(refs)
        a_ref, b_ref = refs[0], refs[1]
        pos = 2
        if two:
            a2_ref, b2_ref = refs[2], refs[3]
            pos = 4
        if has_res:
            res_ref = refs[pos]
            pos += 1
        o_ref, acc_ref = refs[pos], refs[pos + 1]
        k = pl.program_id(2)
        part = _bdot(a_ref[...], b_ref[...], dims)
        if two:
            part = part + _bdot(a2_ref[...], b2_ref[...], dims)

        def finish(acc):
            out = acc * alpha if alpha != 1.0 else acc
            if has_res:
                out = res_ref[...] + out
            o_ref[...] = out.astype(out_dtype)

        if nk == 1:
            finish(part)
        else:
            @pl.when(k == 0)
            def _():
                acc_ref[...] = part

            @pl.when(k > 0)
            def _():
                acc_ref[...] += part

            @pl.when(k == nk - 1)
            def _():
                finish(acc_ref[...])

    ins = [a, b]
    specs = [a_spec, b_spec]
    if two:
        ins += [a2, b2]
        specs += [a2_spec, b2_spec]
    if has_res:
        ins.append(res)
        specs.append(o_spec)
    return _pcall(
        body, name=name, grid=(M // tm, N // tn, nk), in_specs=specs, out_specs=o_spec,
        out_shape=jax.ShapeDtypeStruct((M, N), out_dtype),
        scratch_shapes=[pltpu.VMEM((tm, tn) if nk > 1 else (8, LANE), F32)],
        compiler_params=_params(("parallel", "parallel", "arbitrary")),
    )(*ins)


ROW_TILE = 256


def _rms_fwd(h, g, name):
    T, D = h.shape
    tr = _pick(T, ROW_TILE)

    def body(h_ref, g_ref, n_ref):
        x = h_ref[...]
        r = lax.rsqrt(jnp.mean(x * x, axis=-1, keepdims=True) + EPS)
        n_ref[...] = (x * r * g_ref[...]).astype(BF16)

    return _pcall(
        body, name=name, grid=(T // tr,),
        in_specs=[pl.BlockSpec((tr, D), lambda i: (i, 0)), pl.BlockSpec((1, D), lambda i: (0, 0))],
        out_specs=pl.BlockSpec((tr, D), lambda i: (i, 0)),
        out_shape=jax.ShapeDtypeStruct((T, D), BF16), compiler_params=_params(("parallel",)),
    )(h, g.reshape(1, D))


def _rms_bwd(dn, h, g, dres, name):
    T, D = h.shape
    tr = _pick(T, ROW_TILE)

    def body(dn_ref, h_ref, g_ref, dres_ref, dh_ref, dg_ref):
        x = h_ref[...]
        r = lax.rsqrt(jnp.mean(x * x, axis=-1, keepdims=True) + EPS)
        xh = x * r
        d = dn_ref[...].astype(F32)
        dxh = d * g_ref[...]
        dh_ref[...] = dres_ref[...] + r * (dxh - xh * jnp.mean(dxh * xh, axis=-1, keepdims=True))
        part = jnp.sum(d * xh, axis=0, keepdims=True)

        @pl.when(pl.program_id(0) == 0)
        def _():
            dg_ref[...] = part

        @pl.when(pl.program_id(0) > 0)
        def _():
            dg_ref[...] += part

    row = pl.BlockSpec((tr, D), lambda i: (i, 0))
    vec = pl.BlockSpec((1, D), lambda i: (0, 0))
    dh, dg = _pcall(
        body, name=name, grid=(T // tr,), in_specs=[row, row, vec, row], out_specs=[row, vec],
        out_shape=[jax.ShapeDtypeStruct((T, D), F32), jax.ShapeDtypeStruct((1, D), F32)],
        compiler_params=_params(("arbitrary",)),
    )(dn, h, g.reshape(1, D), dres)
    return dh, dg.reshape(D)


def _gateup(n, wg, wu, name, ride=()):
    T, D = n.shape
    F = wg.shape[1]
    tm, tn = _pick(T, 1024), _pick(F, 1408)
    nr = len(ride)
    ride_out = [jax.ShapeDtypeStruct((N_DEV,) + r.shape, r.dtype) for r in ride]
    ride_in_specs, ride_out_specs, ride_sems = _ride_specs(ride, ride_out, N_DEV - 1)
    grid = (T // tm, F // tn)

    def body(*refs):
        n_ref, wg_ref, wu_ref = refs[:3]
        a_ref, b_ref, hid_ref = refs[3 + nr:6 + nr]
        if nr:
            i, j = pl.program_id(0), pl.program_id(1)
            start, finish = _gather_protocol(refs[3:3 + nr], refs[6 + nr:6 + 2 * nr], *refs[6 + 2 * nr:])
            pl.when((i == 0) & (j == 0))(start)
        x = n_ref[...]
        a = _dot(x, wg_ref[...])
        b = _dot(x, wu_ref[...])
        a_ref[...] = a.astype(BF16)
        b_ref[...] = b.astype(BF16)
        hid_ref[...] = (a * _sigmoid(a) * b).astype(BF16)
        if nr:
            pl.when((i == grid[0] - 1) & (j == grid[1] - 1))(finish)

    o_spec = pl.BlockSpec((tm, tn), lambda i, j: (i, j))
    w_spec = pl.BlockSpec((D, tn), lambda i, j: (0, j))
    sh = jax.ShapeDtypeStruct((T, F), BF16)
    res = _pcall(
        body, name=name, grid=grid,
        in_specs=[pl.BlockSpec((tm, D), lambda i, j: (i, 0)), w_spec, w_spec] + ride_in_specs,
        out_specs=[o_spec, o_spec, o_spec] + ride_out_specs, out_shape=[sh, sh, sh] + ride_out,
        scratch_shapes=ride_sems,
        compiler_params=_params(("arbitrary", "arbitrary") if nr else ("parallel", "parallel")),
    )(n, wg, wu, *ride)
    return res[0], res[1], res[2], list(res[3:])


def _ffn_dhid(dy, wd, a, b, name):
    T, D = dy.shape
    F = wd.shape[0]
    tm, tn = _pick(T, 1024), _pick(F, 1408)

    def body(dy_ref, wd_ref, a_ref, b_ref, da_ref, db_ref):
        dhid = 0.5 * _bdot(dy_ref[...], wd_ref[...], NT)
        av = a_ref[...].astype(F32)
        bv = b_ref[...].astype(F32)
        s = _sigmoid(av)
        da_ref[...] = (dhid * bv * s * (1.0 + av * (1.0 - s))).astype(BF16)
        db_ref[...] = (dhid * av * s).astype(BF16)

    o_spec = pl.BlockSpec((tm, tn), lambda i, j: (i, j))
    sh = jax.ShapeDtypeStruct((T, F), BF16)
    return _pcall(
        body, name=name, grid=(T // tm, F // tn),
        in_specs=[pl.BlockSpec((tm, D), lambda i, j: (i, 0)), pl.BlockSpec((tn, D), lambda i, j: (j, 0)), o_spec, o_spec],
        out_specs=[o_spec, o_spec], out_shape=[sh, sh],
        compiler_params=_params(("parallel", "parallel")),
    )(dy, wd, a, b)


def _ffn_fwd(h, g, wg, wu, wd, tag, ride=(), wd_of=None):
    n = _rms_fwd(h, g, f"{tag}_norm")
    a, b, hid, gathered = _gateup(n, wg, wu, f"{tag}_gateup", ride)
    if wd_of is not None:
        wd = wd_of(gathered)
    h2 = _mm(hid, wd, "nn", res=h, alpha=0.5, name=f"{tag}_down")
    return h2, (h, n, a, b, hid)


def _ffn_bwd(dh2, saved, g, wg, wu, wd, tag):
    h, n, a, b, hid = saved
    da, db = _ffn_dhid(dh2, wd, a, b, f"{tag}_dhid")
    dwd = _mm(hid, dh2, "tn", alpha=0.5, out_dtype=BF16, name=f"{tag}_dwd")
    dwg = _mm(n, da, "tn", out_dtype=BF16, name=f"{tag}_dwg")
    dwu = _mm(n, db, "tn", out_dtype=BF16, name=f"{tag}_dwu")
    dn = _mm(da, wg, "nt", a2=db, b2=wu, name=f"{tag}_dn")
    dh, dg = _rms_bwd(dn, h, g, dh2, f"{tag}_dnorm")
    return dh, dg, dwg, dwu, dwd


def _ple_fwd(h, p, g, w_gate, w_proj, tag):
    T, D = h.shape
    pn = _rms_fwd(h, g, f"{tag}_norm")
    tm, tn = _pick(T, 512), _pick(D, 1024)
    P = p.shape[1]

    def body(pn_ref, p_ref, wg_ref, wp_ref, h_ref, o_ref, gl_ref, pp_ref):
        gl = _dot(pn_ref[...], wg_ref[...])
        pp = _bdot(p_ref[...], wp_ref[...])
        gl_ref[...] = gl
        pp_ref[...] = pp
        o_ref[...] = h_ref[...] + _sigmoid(gl) * pp

    o_spec = pl.BlockSpec((tm, tn), lambda i, j: (i, j))
    sh = jax.ShapeDtypeStruct((T, D), F32)
    h2, gl, pp = _pcall(
        body, name=f"{tag}_fwd", grid=(T // tm, D // tn),
        in_specs=[pl.BlockSpec((tm, D), lambda i, j: (i, 0)), pl.BlockSpec((tm, P), lambda i, j: (i, 0)),
                  pl.BlockSpec((D, tn), lambda i, j: (0, j)), pl.BlockSpec((P, tn), lambda i, j: (0, j)), o_spec],
        out_specs=[o_spec, o_spec, o_spec], out_shape=[sh, sh, sh],
        compiler_params=_params(("parallel", "parallel")),
    )(pn, p, w_gate, w_proj, h)
    return h2, (h, pn, gl, pp)


def _ple_bwd(dh2, saved, p, g, w_gate, tag):
    h, pn, gl, pp = saved
    T, D = h.shape
    tr = _pick(T, ROW_TILE)

    def body(d_ref, gl_ref, pp_ref, dgl_ref, dpp_ref):
        d = d_ref[...]
        s = _sigmoid(gl_ref[...])
        dpp_ref[...] = (d * s).astype(BF16)
        dgl_ref[...] = (d * pp_ref[...] * s * (1.0 - s)).astype(BF16)

    row = pl.BlockSpec((tr, D), lambda i: (i, 0))
    sh = jax.ShapeDtypeStruct((T, D), BF16)
    dgl, dpp = _pcall(body, name=f"{tag}_dgate", grid=(T // tr,), in_specs=[row, row, row], out_specs=[row, row],
                      out_shape=[sh, sh], compiler_params=_params(("parallel",)))(dh2, gl, pp)
    dw_proj = _mm(p, dpp, "tn", out_dtype=BF16, name=f"{tag}_dwproj")
    dw_gate = _mm(pn, dgl, "tn", out_dtype=BF16, name=f"{tag}_dwgate")
    dpn = _mm(dgl, w_gate, "nt", name=f"{tag}_dpn")
    dh, dg = _rms_bwd(dpn, h, g, dh2, f"{tag}_dnorm")
    return dh, dg, dw_gate, dw_proj


def _loss_head(y, target):
    T, D = y.shape
    tr = _pick(T, ROW_TILE)

    def body(y_ref, t_ref, dy_ref, l_ref):
        e = y_ref[...] - t_ref[...]
        dy_ref[...] = e * (1.0 / D)
        part = jnp.sum(e * e, axis=0, keepdims=True)

        @pl.when(pl.program_id(0) == 0)
        def _():
            l_ref[...] = part

        @pl.when(pl.program_id(0) > 0)
        def _():
            l_ref[...] += part

    row = pl.BlockSpec((tr, D), lambda i: (i, 0))
    vec = pl.BlockSpec((1, D), lambda i: (0, 0))
    dy, l = _pcall(body, name="loss_head", grid=(T // tr,), in_specs=[row, row], out_specs=[row, vec],
                   out_shape=[jax.ShapeDtypeStruct((T, D), F32), jax.ShapeDtypeStruct((1, D), F32)],
                   compiler_params=_params(("arbitrary",)))(y, target)
    return (0.5 / D) * jnp.sum(l), dy


SB_LANES = SB_HEADS * 2 * HEAD_DIM


def _sb_consts():
    row = _iota2((Q_BLOCK, Q_BLOCK), 0)
    col = _iota2((Q_BLOCK, Q_BLOCK), 1)
    after = (row > col).astype(BF16)
    before = (row < col).astype(BF16)
    return col < row, after, before, col


def _sb_fwd(proj, ride=()):
    T = proj.shape[0]
    H, d, L = SB_HEADS, HEAD_DIM, 2 * HEAD_DIM
    nblk = T // Q_BLOCK
    scale = d ** -0.5
    n = len(ride)
    ride_out = [jax.ShapeDtypeStruct((N_DEV,) + a.shape, a.dtype) for a in ride]
    ride_in_specs, ride_out_specs, ride_sems = _ride_specs(ride, ride_out, N_DEV - 1)
    R = range(H)
    tile = lambda g: slice(g * L, (g + 1) * L)

    def body(*refs):
        q_ref, kv_ref = refs[:2]
        rin = refs[2:2 + n]
        o_ref, c_ref = refs[2 + n:4 + n]
        rout = refs[4 + n:4 + 2 * n]
        run_ref = refs[4 + 2 * n]
        i = pl.program_id(0)
        if n:
            start, finish = _gather_protocol(rin, rout, *refs[5 + 2 * n:])
            pl.when(i == 0)(start)
        causal, after, _, col = _sb_consts()
        qs = [q_ref[:, tile(g)] * scale for g in R]
        o_ref[...] = jnp.zeros_like(o_ref)
        c_ref[...] = jnp.zeros_like(c_ref)
        run_ref[...] = jnp.zeros_like(run_ref)

        def pair(j, diag):
            rows = pl.ds(pl.multiple_of(j * Q_BLOCK, Q_BLOCK), Q_BLOCK)
            kvj = [kv_ref[rows, tile(g)] for g in R]
            c = [run_ref[g] for g in R]
            acc = [o_ref[:, tile(g)] for g in R]
            cm = None if diag else [c_ref[:, tile(g)] for g in R]
            z = [_dot(qs[g], kvj[g], NT) for g in R]
            sp = [_softplus(z[g]) for g in R]
            lk = [jnp.where(causal, -sp[g], 0.0) if diag else -sp[g] for g in R]
            btw = [_dot2m(lk[g], after) for g in R]
            e = [jnp.exp((z[g] - sp[g]) + btw[g] + c[g]) for g in R]
            w = [jnp.where(causal, e[g], 0.0) if diag else e[g] for g in R]
            pv = [_bdot(w[g], kvj[g]) for g in R]
            rs = [jnp.sum(lk[g], axis=1, keepdims=True) for g in R]
            for g in R:
                o_ref[:, tile(g)] = acc[g] + pv[g]
                if not diag:
                    c_ref[:, tile(g)] = jnp.where(col == j, c[g], cm[g])
                run_ref[g] = c[g] + rs[g]

        pair(i, True)

        @pl.loop(0, i)
        def _(jj):
            pair(i - 1 - jj, False)

        if n:
            pl.when(i == nblk - 1)(finish)

    blk = pl.BlockSpec((Q_BLOCK, H * L), lambda i: (i, 0))
    full = pl.BlockSpec((T, H * L), lambda i: (0, 1))
    res = _pcall(
        body, name="sb_fwd", grid=(nblk,), in_specs=[blk, full] + ride_in_specs,
        out_specs=[blk, blk] + ride_out_specs,
        out_shape=[jax.ShapeDtypeStruct((T, H * L), F32), jax.ShapeDtypeStruct((T, H * L), F32)] + ride_out,
        scratch_shapes=[pltpu.VMEM((H, Q_BLOCK, 1), F32)] + ride_sems,
        compiler_params=_params(("arbitrary",)),
    )(proj, proj, *ride)
    return res[0], res[1], list(res[2:])


def _sb_bwd(proj, carry, do, ride=()):
    T = proj.shape[0]
    H, d, L = SB_HEADS, HEAD_DIM, 2 * HEAD_DIM
    nblk = T // Q_BLOCK
    scale = d ** -0.5
    n = len(ride)
    ride_out = [jax.ShapeDtypeStruct(a.shape, a.dtype) for a in ride]
    ride_in_specs, ride_out_specs, ride_sems = _ride_specs(ride, ride_out, N_CHIP - 1)
    R = range(H)
    tile = lambda g: slice(g * L, (g + 1) * L)

    def body(*refs):
        q_ref, kv_ref, c_ref, do_ref = refs[:4]
        rin = refs[4:4 + n]
        dq_ref, dkv_ref = refs[4 + n:6 + n]
        rout = refs[6 + n:6 + 2 * n]
        run_ref = refs[6 + 2 * n]
        i = pl.program_id(0)
        if n:
            start, finish = _chips_protocol(rin, rout, *refs[7 + 2 * n:])
            pl.when(i == 0)(start)

        @pl.when(i == 0)
        def _():
            dkv_ref[...] = jnp.zeros_like(dkv_ref)

        causal, after, before, col = _sb_consts()
        qs = [q_ref[:, tile(g)] * scale for g in R]
        dov = [do_ref[:, tile(g)] for g in R]
        qdo = [jnp.concatenate([qs[g], dov[g]], axis=0) for g in R]
        dq_ref[...] = jnp.zeros_like(dq_ref)
        run_ref[...] = jnp.zeros_like(run_ref)

        def pair(j, diag):
            rows = pl.ds(pl.multiple_of(j * Q_BLOCK, Q_BLOCK), Q_BLOCK)
            kvj = [kv_ref[rows, tile(g)] for g in R]
            gsum = [run_ref[g] for g in R]
            dq0 = [dq_ref[:, tile(g)] for g in R]
            dkv0 = [dkv_ref[rows, tile(g)] for g in R]
            cm = None if diag else [c_ref[:, tile(g)] for g in R]
            z = [_dot(qs[g], kvj[g], NT) for g in R]
            sp = [_softplus(z[g]) for g in R]
            lk = [jnp.where(causal, -sp[g], 0.0) if diag else -sp[g] for g in R]
            ls = [z[g] - sp[g] for g in R]
            logw = [ls[g] + _dot2m(lk[g], after) for g in R]
            if not diag:
                logw = [logw[g] + jnp.sum(jnp.where(col == j, cm[g], 0.0), axis=1, keepdims=True) for g in R]
            e = [jnp.exp(logw[g]) for g in R]
            w = [jnp.where(causal, e[g], 0.0) if diag else e[g] for g in R]
            gw = [_dot(dov[g], kvj[g], NT) * w[g] for g in R]
            gpre = [gsum[g] + _dot(gw[g].astype(BF16), before) for g in R]
            sig = [jnp.exp(ls[g]) for g in R]
            dz = [gw[g] * (1.0 - sig[g]) - sig[g] * gpre[g] for g in R]
            if diag:
                dz = [jnp.where(causal, dz[g], 0.0) for g in R]
            dzb = [dz[g].astype(BF16) for g in R]
            dq1 = [_dot(dzb[g], kvj[g]) for g in R]
            dkv1 = [_dot(jnp.concatenate([dzb[g], w[g].astype(BF16)], axis=0), qdo[g], TN) for g in R]
            gs1 = [jnp.sum(gw[g], axis=1, keepdims=True) for g in R]
            for g in R:
                dq_ref[:, tile(g)] = dq0[g] + dq1[g]
                dkv_ref[rows, tile(g)] = dkv0[g] + dkv1[g]
                run_ref[g] = gsum[g] + gs1[g]

        @pl.loop(0, i)
        def _(j):
            pair(j, False)

        pair(i, True)
        dq_ref[...] = dq_ref[...] * scale
        if n:
            pl.when(i == nblk - 1)(finish)

    blk = pl.BlockSpec((Q_BLOCK, H * L), lambda i: (i, 0))
    once = pl.Buffered(1)
    sh = jax.ShapeDtypeStruct((T, H * L), F32)
    res = _pcall(
        body, name="sb_bwd", grid=(nblk,),
        in_specs=[blk, pl.BlockSpec((T, H * L), lambda i: (0, 1), pipeline_mode=once), blk, blk] + ride_in_specs,
        out_specs=[blk, pl.BlockSpec((T, H * L), lambda i: (0, 0), pipeline_mode=once)] + ride_out_specs,
        out_shape=[sh, sh] + ride_out,
        scratch_shapes=[pltpu.VMEM((H, Q_BLOCK, 1), F32)] + ride_sems,
        compiler_params=_params(("arbitrary",)),
    )(proj, proj, carry, do, *ride)
    return res[0], res[1], list(res[2:])


def _swa_common(q_ref, kvp_ref, kvc_ref, qg_ref, kg_ref, sk_ref, sl_ref, n):
    W, d, G = WINDOW, HEAD_DIM, SWA_GROUP
    scale = d ** -0.5
    row = _iota2((W, 2 * W), 0)
    col = _iota2((W, 2 * W), 1)
    dist = row + W - col
    valid = (dist >= 0) & (dist < W) & ((n > 0) | (col >= W))
    distf = dist.astype(F32)
    kvcat = jnp.concatenate([kvp_ref[...], kvc_ref[...]], axis=0)
    KH, QH = range(SWA_KV_HEADS), range(SWA_HEADS)
    kraw = [kvcat[:, hk * d:(hk + 1) * d] for hk in KH]
    vcat = [kvcat[:, SWA_KVW + hk * d:SWA_KVW + (hk + 1) * d].astype(BF16) for hk in KH]
    rk = [lax.rsqrt(jnp.mean(kraw[hk] * kraw[hk], axis=-1, keepdims=True) + EPS) for hk in KH]
    kh = [kraw[hk] * rk[hk] for hk in KH]
    kn = [(kh[hk] * kg_ref[...]).astype(BF16) for hk in KH]
    qraw = [q_ref[:, h * d:(h + 1) * d] for h in QH]
    rq = [lax.rsqrt(jnp.mean(qraw[h] * qraw[h], axis=-1, keepdims=True) + EPS) for h in QH]
    qh = [qraw[h] * rq[h] for h in QH]
    qn = [(qh[h] * qg_ref[...]).astype(BF16) for h in QH]
    sink = [sk_ref[h:h + 1, :1] for h in QH]
    s = [jnp.where(valid, _dot(qn[h], kn[h // G], NT) * scale - sl_ref[h:h + 1, :1] * distf, -1e30) for h in QH]
    m = [jnp.maximum(jnp.max(s[h], axis=1, keepdims=True), sink[h]) for h in QH]
    p = [jnp.where(valid, jnp.exp(s[h] - m[h]), 0.0) for h in QH]
    esink = [jnp.exp(sink[h] - m[h]) for h in QH]
    den = [jnp.sum(p[h], axis=1, keepdims=True) + esink[h] for h in QH]
    prob = [p[h] / den[h] for h in QH]
    return vcat, rk, kh, kn, rq, qh, qn, esink, den, prob


def _swa_specs(T):
    W = WINDOW
    q = pl.BlockSpec((W, SWA_QW), lambda n: (n, 0))
    prev = pl.BlockSpec((W, 2 * SWA_KVW), lambda n: (jnp.maximum(n - 1, 0), SWA_QW // (2 * SWA_KVW)))
    cur = pl.BlockSpec((W, 2 * SWA_KVW), lambda n: (n, SWA_QW // (2 * SWA_KVW)))
    gain = pl.BlockSpec((1, HEAD_DIM), lambda n: (0, 0))
    perhead = pl.BlockSpec((SWA_HEADS, LANE), lambda n: (0, 0))
    return q, prev, cur, gain, perhead


def _swa_fwd(proj, qg, kg, sinks, slopes):
    T = proj.shape[0]
    W, d, G = WINDOW, HEAD_DIM, SWA_GROUP

    def body(q_ref, kvp_ref, kvc_ref, qg_ref, kg_ref, sk_ref, sl_ref, o_ref):
        vcat, _, _, _, _, _, _, _, _, prob = _swa_common(q_ref, kvp_ref, kvc_ref, qg_ref, kg_ref, sk_ref, sl_ref,
                                                         pl.program_id(0))
        outs = [_bdot(prob[h], vcat[h // G]) for h in range(SWA_HEADS)]
        o_ref[...] = jnp.concatenate(outs, axis=1).astype(BF16)

    q, prev, cur, gain, perhead = _swa_specs(T)
    return _pcall(
        body, name="swa_fwd", grid=(T // W,), in_specs=[q, prev, cur, gain, gain, perhead, perhead], out_specs=q,
        out_shape=jax.ShapeDtypeStruct((T, SWA_QW), BF16), compiler_params=_params(("parallel",)),
    )(proj, proj, proj, qg, kg, sinks, slopes)


def _swa_bwd(proj, qg, kg, sinks, slopes, do):
    T = proj.shape[0]
    W, d, G = WINDOW, HEAD_DIM, SWA_GROUP
    scale = d ** -0.5
    KH, QH = range(SWA_KV_HEADS), range(SWA_HEADS)

    def body(q_ref, kvp_ref, kvc_ref, qg_ref, kg_ref, sk_ref, sl_ref, do_ref,
             dq_ref, dkv_ref, dqg_ref, dkg_ref, dsk_ref):
        n = pl.program_id(0)

        @pl.when(n == 0)
        def _():
            dqg_ref[...] = jnp.zeros_like(dqg_ref)
            dkg_ref[...] = jnp.zeros_like(dkg_ref)
            dsk_ref[...] = jnp.zeros_like(dsk_ref)
            dkv_ref[...] = jnp.zeros_like(dkv_ref)

        vcat, rk, kh, kn, rq, qh, qn, esink, den, prob = _swa_common(q_ref, kvp_ref, kvc_ref, qg_ref, kg_ref,
                                                                     sk_ref, sl_ref, n)
        dov = [do_ref[:, h * d:(h + 1) * d].astype(BF16) for h in QH]
        dp = [_dot(dov[h], vcat[h // G], NT) for h in QH]
        dd = [jnp.sum(prob[h] * dp[h], axis=1, keepdims=True) for h in QH]
        dsb = [(prob[h] * (dp[h] - dd[h]) * scale).astype(BF16) for h in QH]
        dsink = [-jnp.sum((esink[h] / den[h]) * dd[h], axis=0, keepdims=True) for h in QH]
        dqn = [_dot(dsb[h], kn[h // G]) for h in QH]
        dkn_h = [_dot(dsb[h], qn[h], TN) for h in QH]
        dv_h = [_dot(prob[h].astype(BF16), dov[h], TN) for h in QH]
        dqh = [dqn[h] * qg_ref[...] for h in QH]
        dq = [rq[h] * (dqh[h] - qh[h] * jnp.mean(dqh[h] * qh[h], axis=-1, keepdims=True)) for h in QH]
        dkn = [sum(dkn_h[hk * G + g] for g in range(G)) for hk in KH]
        dvc = [sum(dv_h[hk * G + g] for g in range(G)) for hk in KH]
        dkh = [dkn[hk] * kg_ref[...] for hk in KH]
        dkraw = [rk[hk] * (dkh[hk] - kh[hk] * jnp.mean(dkh[hk] * kh[hk], axis=-1, keepdims=True)) for hk in KH]
        dq_ref[...] = jnp.concatenate(dq, axis=1)
        dqg_ref[...] += sum(jnp.sum(dqn[h] * qh[h], axis=0, keepdims=True) for h in QH)
        dkg_ref[...] += sum(jnp.sum(dkn[hk] * kh[hk], axis=0, keepdims=True) for hk in KH)
        rowh = _iota2((SWA_HEADS, LANE), 0)
        dsk_ref[...] += sum(jnp.where(rowh == h, dsink[h], 0.0) for h in QH)
        upd = jnp.concatenate(dkraw + dvc, axis=1)
        offp = pl.multiple_of(jnp.maximum(n - 1, 0) * W, W)
        offc = pl.multiple_of(n * W, W)
        dkv_ref[pl.ds(offp, W), :] += upd[:W]
        dkv_ref[pl.ds(offc, W), :] += upd[W:]

    q, prev, cur, gain, perhead = _swa_specs(T)
    kvfull = pl.BlockSpec((T, 2 * SWA_KVW), lambda n: (0, 0))
    gs = jax.ShapeDtypeStruct((1, d), F32)
    return _pcall(
        body, name="swa_bwd", grid=(T // W,), in_specs=[q, prev, cur, gain, gain, perhead, perhead, q],
        out_specs=[q, kvfull, gain, gain, perhead],
        out_shape=[jax.ShapeDtypeStruct((T, SWA_QW), F32), jax.ShapeDtypeStruct((T, 2 * SWA_KVW), F32), gs, gs,
                   jax.ShapeDtypeStruct((SWA_HEADS, LANE), F32)],
        compiler_params=_params(("arbitrary",)),
    )(proj, proj, proj, qg, kg, sinks, slopes, do)


def _alibi():
    s = [2.0 ** (-8.0 * (i + 1) / SWA_HEADS) for i in range(SWA_HEADS)]
    return jnp.broadcast_to(jnp.asarray(s, F32)[:, None], (SWA_HEADS, LANE))


def _head_tiles(lo, hi):
    shp = lo.shape[:-1]
    return jnp.concatenate([lo.reshape(shp + (SB_HEADS, HEAD_DIM)), hi.reshape(shp + (SB_HEADS, HEAD_DIM))],
                           axis=-1).reshape(shp + (SB_LANES,))


def _tile_halves(x):
    shp = x.shape[:-1]
    t = x.reshape(shp + (SB_HEADS, 2, HEAD_DIM))
    return t[..., 0, :].reshape(shp + (SB_W,)), t[..., 1, :].reshape(shp + (SB_W,))


def _att_in_weights(w_in):
    sq, sk, sv = w_in[:, :SB_W], w_in[:, SB_W:2 * SB_W], w_in[:, 2 * SB_W:3 * SB_W]
    return jnp.concatenate([_head_tiles(sq, jnp.zeros_like(sq)), _head_tiles(sk, sv)], axis=1), w_in[:, 3 * SB_W:]


def _att_out_weights(w_out):
    wo = w_out[:SB_W]
    return _head_tiles(jnp.zeros_like(wo).T, wo.T).T, w_out[SB_W:]


def _att_fwd(h, g, w_in, w_out_of, q_gain, k_gain, sinks, ride=()):
    hn = _rms_fwd(h, g, "att_norm")
    w_sb, w_swa = _att_in_weights(w_in)
    proj_sb = _mm(hn, w_sb, "nn", out_dtype=BF16, name="att_in_sb")
    proj_swa = _mm(hn, w_swa, "nn", name="att_in_swa")
    a_out, carry, gathered = _sb_fwd(proj_sb, ride)
    w_out = w_out_of(gathered)
    wo_sb, wo_swa = _att_out_weights(w_out)
    sk128 = jnp.broadcast_to(sinks.reshape(SWA_HEADS, 1), (SWA_HEADS, LANE))
    qg, kg = q_gain.reshape(1, HEAD_DIM), k_gain.reshape(1, HEAD_DIM)
    b_out = _swa_fwd(proj_swa, qg, kg, sk128, _alibi())
    h2 = _mm(a_out, wo_sb, "nn", res=h, a2=b_out, b2=wo_swa, name="att_out")
    return h2, (h, hn, proj_sb, proj_swa, carry, a_out, b_out, sk128, qg, kg), gathered


def _att_bwd(dh2, saved, g, w_in, w_out, ride=()):
    h, hn, proj_sb, proj_swa, carry, a_out, b_out, sk128, qg, kg = saved
    w_sb, w_swa = _att_in_weights(w_in)
    wo_sb, wo_swa = _att_out_weights(w_out)
    da = _mm(dh2, wo_sb, "nt", out_dtype=BF16, name="att_do_sb")
    db = _mm(dh2, wo_swa, "nt", name="att_do_swa")
    dwo_sb = _mm(a_out, dh2, "tn", out_dtype=BF16, name="att_dwout_sb")
    dwo_swa = _mm(b_out, dh2, "tn", out_dtype=BF16, name="att_dwout_swa")
    dw_out = jnp.concatenate([_tile_halves(dwo_sb.T)[1].T, dwo_swa], axis=0)
    dq, dkv, rode = _sb_bwd(proj_sb, carry, da, ride)
    dbq, dbkv, dqg, dkg, dsink = _swa_bwd(proj_swa, qg, kg, sk128, _alibi(), db)
    dproj = jnp.concatenate([dq.astype(BF16), dkv.astype(BF16), dbq.astype(BF16), dbkv.astype(BF16)], axis=1)
    w_all = jnp.concatenate([w_sb, w_swa], axis=1)
    dw_all = _mm(hn, dproj, "tn", out_dtype=BF16, name="att_dwin")
    dhn = _mm(dproj, w_all, "nt", name="att_dhn")
    dsq, _ = _tile_halves(dw_all[:, :SB_LANES])
    dsk, dsv = _tile_halves(dw_all[:, SB_LANES:2 * SB_LANES])
    dw_in = jnp.concatenate([dsq, dsk, dsv, dw_all[:, 2 * SB_LANES:]], axis=1)
    dh, dg = _rms_bwd(dhn, h, g, dh2, "att_dnorm")
    return dh, dg, dw_in, dw_out, dqg.reshape(HEAD_DIM), dkg.reshape(HEAD_DIM), dsink[:, 0], rode


CONV_ROWS = 512
CONV_COLS = 512
HALO = 8


def _shifted(xcat, s, tm):
    if s == 0:
        return xcat[HALO:HALO + tm]
    return pltpu.roll(xcat, s, 0)[HALO:HALO + tm]


def _conv_pre(x_ref, halo_ref, w_ref, i, tm):
    xc = x_ref[...]
    halo = jnp.where(i > 0, halo_ref[...], 0.0)
    xcat = jnp.concatenate([halo, xc], axis=0)
    w = w_ref[...]
    y = w[GDN_CONV - 1:GDN_CONV] * xc
    for kk in range(GDN_CONV - 1):
        y = y + w[kk:kk + 1] * _shifted(xcat, GDN_CONV - 1 - kk, tm)
    return xcat, y


def _l2_heads(s, qscale_of):
    outs, rs = [], []
    for hh in range(s.shape[1] // GDN_HEAD_DIM):
        sh = s[:, hh * GDN_HEAD_DIM:(hh + 1) * GDN_HEAD_DIM]
        r = lax.rsqrt(jnp.sum(sh * sh, axis=-1, keepdims=True) + EPS)
        outs.append(sh * r)
        rs.append(r)
    return outs, rs


def _conv_specs(T, col0, tm, tc):
    cur = pl.BlockSpec((tm, tc), lambda j, i: (i, j + col0 // tc))
    halo = pl.BlockSpec((HALO, tc), lambda j, i: (jnp.maximum(i * (tm // HALO) - 1, 0), j + col0 // tc))
    wsp = pl.BlockSpec((GDN_CONV, tc), lambda j, i: (0, j + col0 // tc))
    out = pl.BlockSpec((tm, tc), lambda j, i: (i, j))
    return cur, halo, wsp, out


def _conv_fwd(proj, conv_w, col0, width, norm, name):
    T = proj.shape[0]
    tm, tc = _pick(T, CONV_ROWS), CONV_COLS
    cur, halo, wsp, out = _conv_specs(T, col0, tm, tc)
    n_q_tiles = (width // 2) // tc

    def body(x_ref, halo_ref, w_ref, o_ref):
        j, i = pl.program_id(0), pl.program_id(1)
        _, y = _conv_pre(x_ref, halo_ref, w_ref, i, tm)
        s = y * _sigmoid(y)
        if norm:
            outs, _ = _l2_heads(s, None)
            qs = jnp.where(j < n_q_tiles, GDN_HEAD_DIM ** -0.5, 1.0)
            o_ref[...] = jnp.concatenate(outs, axis=1) * qs
        else:
            o_ref[...] = s

    return _pcall(body, name=name, grid=(width // tc, T // tm), in_specs=[cur, halo, wsp], out_specs=out,
                  out_shape=jax.ShapeDtypeStruct((T, width), F32),
                  compiler_params=_params(("parallel", "parallel")))(proj, proj, conv_w)


def _conv_bwd_pre(proj, conv_w, dout, col0, width, norm, name):
    T = proj.shape[0]
    tm, tc = _pick(T, CONV_ROWS), CONV_COLS
    cur, halo, wsp, out = _conv_specs(T, col0, tm, tc)
    n_q_tiles = (width // 2) // tc

    def body(x_ref, halo_ref, w_ref, d_ref, dy_ref, dw_ref):
        j, i = pl.program_id(0), pl.program_id(1)
        xcat, y = _conv_pre(x_ref, halo_ref, w_ref, i, tm)
        sg = _sigmoid(y)
        s = y * sg
        d = d_ref[...]
        if norm:
            qs = jnp.where(j < n_q_tiles, GDN_HEAD_DIM ** -0.5, 1.0)
            d = d * qs
            outs, rs = _l2_heads(s, None)
            parts = []
            for hh, (nh, r) in enumerate(zip(outs, rs)):
                dh = d[:, hh * GDN_HEAD_DIM:(hh + 1) * GDN_HEAD_DIM]
                parts.append(r * (dh - nh * jnp.sum(dh * nh, axis=-1, keepdims=True)))
            ds = jnp.concatenate(parts, axis=1)
        else:
            ds = d
        dy = ds * sg * (1.0 + y * (1.0 - sg))
        dy_ref[...] = dy
        rows = [jnp.sum(dy * _shifted(xcat, GDN_CONV - 1 - kk, tm), axis=0, keepdims=True) for kk in range(GDN_CONV)]
        part = jnp.concatenate(rows, axis=0)

        @pl.when(i == 0)
        def _():
            dw_ref[...] = part

        @pl.when(i > 0)
        def _():
            dw_ref[...] += part

    wout = pl.BlockSpec((GDN_CONV, tc), lambda j, i: (0, j))
    return _pcall(body, name=name, grid=(width // tc, T // tm), in_specs=[cur, halo, wsp, out], out_specs=[out, wout],
                  out_shape=[jax.ShapeDtypeStruct((T, width), F32), jax.ShapeDtypeStruct((GDN_CONV, width), F32)],
                  compiler_params=_params(("parallel", "arbitrary")))(proj, proj, conv_w, dout)


def _conv_bwd_in(dy, conv_w, name):
    T, C = dy.shape
    tm, tc = _pick(T, CONV_ROWS), CONV_COLS
    nrow = T // tm

    def body(d_ref, nxt_ref, w_ref, dx_ref):
        i = pl.program_id(0)
        dc = d_ref[...]
        nxt = jnp.where(i < nrow - 1, nxt_ref[...], 0.0)
        dcat = jnp.concatenate([dc, nxt], axis=0)
        w = w_ref[...]
        dx = w[GDN_CONV - 1:GDN_CONV] * dc
        for kk in range(GDN_CONV - 1):
            s = GDN_CONV - 1 - kk
            dx = dx + w[kk:kk + 1] * pltpu.roll(dcat, tm + HALO - s, 0)[:tm]
        dx_ref[...] = dx.astype(BF16)

    cur = pl.BlockSpec((tm, tc), lambda i, j: (i, j))
    nxt = pl.BlockSpec((HALO, tc), lambda i, j: (jnp.minimum((i + 1) * (tm // HALO), T // HALO - 1), j))
    wsp = pl.BlockSpec((GDN_CONV, tc), lambda i, j: (0, j))
    return _pcall(body, name=name, grid=(nrow, C // tc), in_specs=[cur, nxt, wsp], out_specs=cur,
                  out_shape=jax.ShapeDtypeStruct((T, C), BF16),
                  compiler_params=_params(("parallel", "parallel")))(dy, dy, conv_w)


GATE_ROWS = 512


def _chunk_mask(n, lower):
    row = _iota2((n, n), 0)
    col = _iota2((n, n), 1)
    same = (row // GDN_CHUNK) == (col // GDN_CHUNK)
    tri = (row >= col) if lower else (row <= col)
    return (same & tri).astype(BF16)


def _gates_fwd(proj, a_log, dt_bias):
    T = proj.shape[0]
    tm = _pick(T, GATE_ROWS)
    c0 = (GDN_CONV_W + GDN_VW) // LANE

    def body(bl_ref, a_ref, alog_ref, dt_ref, beta_ref, g_ref, gc_ref):
        beta_ref[...] = _sigmoid(bl_ref[...])
        g = -jnp.exp(alog_ref[...]) * _softplus(a_ref[...] + dt_ref[...])
        g_ref[...] = g
        gc_ref[...] = _mdot2(_chunk_mask(tm, True), g)

    blk = lambda c: pl.BlockSpec((tm, LANE), lambda i: (i, c))
    vec = pl.BlockSpec((1, LANE), lambda i: (0, 0))
    sh = jax.ShapeDtypeStruct((T, LANE), F32)
    return _pcall(body, name="gdn_gates", grid=(T // tm,), in_specs=[blk(c0), blk(c0 + 1), vec, vec],
                  out_specs=[blk(0), blk(0), blk(0)], out_shape=[sh, sh, sh],
                  compiler_params=_params(("parallel",)))(proj, proj, a_log, dt_bias)


def _gates_bwd(proj, a_log, dt_bias, beta, g, dbeta, dgc):
    T = proj.shape[0]
    tm = _pick(T, GATE_ROWS)
    c0 = (GDN_CONV_W + GDN_VW) // LANE

    def heads_in_lanes(ref):
        lane = _iota2((tm, LANE), 1)
        out = jnp.where(lane < GDN_GROUP, ref[0], 0.0)
        for grp in range(1, GDN_V_HEADS // GDN_GROUP):
            out = out + jnp.where(lane // GDN_GROUP == grp, pltpu.roll(ref[grp], grp * GDN_GROUP, 1), 0.0)
        return out

    def body(a_ref, alog_ref, dt_ref, beta_ref, g_ref, dbeta_ref, dgc_ref, dbl_ref, da_ref, dalog_ref, ddt_ref):
        dg = _mdot2(_chunk_mask(tm, False), heads_in_lanes(dgc_ref))
        b = beta_ref[...]
        dbl_ref[...] = (heads_in_lanes(dbeta_ref) * b * (1.0 - b)).astype(BF16)
        da = dg * (-jnp.exp(alog_ref[...])) * _sigmoid(a_ref[...] + dt_ref[...])
        da_ref[...] = da.astype(BF16)
        p1 = jnp.sum(dg * g_ref[...], axis=0, keepdims=True)
        p2 = jnp.sum(da, axis=0, keepdims=True)

        @pl.when(pl.program_id(0) == 0)
        def _():
            dalog_ref[...] = p1
            ddt_ref[...] = p2

        @pl.when(pl.program_id(0) > 0)
        def _():
            dalog_ref[...] += p1
            ddt_ref[...] += p2

    blk = lambda c: pl.BlockSpec((tm, LANE), lambda i: (i, c))
    vec = pl.BlockSpec((1, LANE), lambda i: (0, 0))
    grp = pl.BlockSpec((GDN_V_HEADS // GDN_GROUP, tm, LANE), lambda i: (0, i, 0))
    shb = jax.ShapeDtypeStruct((T, LANE), BF16)
    shv = jax.ShapeDtypeStruct((1, LANE), F32)
    return _pcall(body, name="gdn_dgates", grid=(T // tm,),
                  in_specs=[blk(c0 + 1), vec, vec, blk(0), blk(0), grp, grp],
                  out_specs=[blk(0), blk(0), vec, vec], out_shape=[shb, shb, shv, shv],
                  compiler_params=_params(("arbitrary",)))(proj, a_log, dt_bias, beta, g, dbeta, dgc)


def _inv_unit_lower(Ls):
    C = Ls[0].shape[0]
    row = _iota2((C, C), 0)
    col = _iota2((C, C), 1)
    blk16 = (row // 16) == (col // 16)
    blk32 = (row // 32) == (col // 32)
    eye = (row == col).astype(F32)
    xs = [-jnp.where(blk16, L, 0.0) for L in Ls]
    inv = [eye + x for x in xs]
    for _ in range(3):
        xs = [_dot3(x, x) for x in xs]
        inv = [a + _dot3(a, x) for a, x in zip(inv, xs)]
    for mask in (blk32 & ~blk16, ~blk32):
        t = [_dot3(a, jnp.where(mask, L, 0.0)) for a, L in zip(inv, Ls)]
        inv = [a - _dot3(ti, a) for a, ti in zip(inv, t)]
    return inv


GDN_GROUP = 4
GDN_PREP_CHUNKS = 4


def _gdn_specs(T):
    C, D, E = GDN_CHUNK, GDN_HEAD_DIM, GDN_GROUP
    n = T // C
    qk = pl.BlockSpec((C, (E // 2) * D), lambda h, i: (i, h))
    vE = pl.BlockSpec((C, E * D), lambda h, i: (i, h))
    colv = pl.BlockSpec((C, LANE), lambda h, i: (i, 0))
    colo = pl.BlockSpec((None, C, LANE), lambda h, i: (h, i, 0))
    rowv = pl.BlockSpec((E, None, 1, C), lambda h, i: (h, i, 0, 0))
    st = pl.BlockSpec((E, None, D, D), lambda h, i: (h, i, 0, 0))
    am = pl.BlockSpec((E, None, C, C), lambda h, i: (h, i, 0, 0))
    return n, qk, vE, colv, colo, rowv, st, am


def _lane_col(blk, lane):
    return jnp.sum(jnp.where(_iota2(blk.shape, 1) == lane, blk, 0.0), axis=1, keepdims=True)


def _gdn_decay(gcol, grow):
    C = GDN_CHUNK
    row = _iota2((C, C), 0)
    col = _iota2((C, C), 1)
    incl = row >= col
    dm = jnp.where(incl, jnp.exp(jnp.where(incl, gcol - grow, 0.0)), 0.0)
    glast = grow[:, C - 1:C]
    return dm, jnp.exp(gcol), jnp.exp(glast), jnp.exp(glast - gcol), row > col, incl


def _gdn_prep(k, beta, gcol, grow):
    T = k.shape[0]
    C, D, B = GDN_CHUNK, GDN_HEAD_DIM, GDN_PREP_CHUNKS
    n = T // C

    def body(k_ref, b_ref, gc_ref, gr_ref, a_ref):
        idx = [(e, cb) for e in range(2) for cb in range(B)]
        kc = {cb: k_ref[cb * C:(cb + 1) * C, :] for cb in range(B)}
        lm = []
        head0 = 2 * pl.program_id(0)
        for e, cb in idx:
            beta = _lane_col(b_ref[cb * C:(cb + 1) * C, :], head0 + e)
            dm, _, _, _, strict, _ = _gdn_decay(_lane_col(gc_ref[cb * C:(cb + 1) * C, :], head0 + e), gr_ref[e, cb])
            lm.append(jnp.where(strict, _bdot(kc[cb] * beta, kc[cb], NT) * dm, 0.0))
        inv = _inv_unit_lower(lm)
        for (e, cb), a in zip(idx, inv):
            a_ref[e, cb] = a

    return _pcall(
        body, name="gdn_prep", grid=(GDN_K_HEADS, n // B),
        in_specs=[pl.BlockSpec((B * C, D), lambda h, i: (i, h)), pl.BlockSpec((B * C, LANE), lambda h, i: (i, 0)),
                  pl.BlockSpec((B * C, LANE), lambda h, i: (i, 0)), pl.BlockSpec((2, B, 1, C), lambda h, i: (h, i, 0, 0))],
        out_specs=pl.BlockSpec((2, B, C, C), lambda h, i: (h, i, 0, 0)),
        out_shape=jax.ShapeDtypeStruct((GDN_V_HEADS, n, C, C), F32),
        compiler_params=_params(("parallel", "parallel")),
    )(k, beta, gcol, grow)


def _gdn_fwd(q, k, v, beta, gcol, grow, amat):
    T = q.shape[0]
    C, D, E = GDN_CHUNK, GDN_HEAD_DIM, GDN_GROUP
    n, qk, vE, colv, colo, rowv, st, am = _gdn_specs(T)
    R = range(E)

    def body(q_ref, k_ref, v_ref, b_ref, gc_ref, gr_ref, a_ref, o_ref, s_ref, vn_ref, state):
        @pl.when(pl.program_id(1) == 0)
        def _():
            state[...] = jnp.zeros_like(state)

        qv = [q_ref[:, (e // 2) * D:(e // 2 + 1) * D] for e in R]
        kv = [k_ref[:, (e // 2) * D:(e // 2 + 1) * D] for e in R]
        vv = [v_ref[:, e * D:(e + 1) * D] for e in R]
        head0 = E * pl.program_id(0)
        beta = [_lane_col(b_ref[...], head0 + e) for e in R]
        a = [a_ref[e] for e in R]
        s = [state[e] for e in R]
        dec = [_gdn_decay(_lane_col(gc_ref[...], head0 + e), gr_ref[e]) for e in R]
        pm = [_bdot(qv[e], kv[e], NT) * dec[e][0] for e in R]
        r = [beta[e] * (vv[e] - _bdot(kv[e] * dec[e][1], s[e])) for e in R]
        vn = [_dot3(a[e], r[e]) for e in R]
        o = [_bdot(qv[e] * dec[e][1], s[e]) + _bdot(pm[e], vn[e]) for e in R]
        s2 = [dec[e][2] * s[e] + _bdot(kv[e] * dec[e][3], vn[e], TN) for e in R]
        for e in R:
            s_ref[e] = s[e]
            vn_ref[:, e * D:(e + 1) * D] = vn[e]
            o_ref[:, e * D:(e + 1) * D] = o[e]
            state[e] = s2[e]

    shv = jax.ShapeDtypeStruct((T, GDN_V_HEADS * D), F32)
    return _pcall(
        body, name="gdn_fwd", grid=(GDN_V_HEADS // E, n), in_specs=[qk, qk, vE, colv, colv, rowv, am],
        out_specs=[vE, st, vE],
        out_shape=[shv, jax.ShapeDtypeStruct((GDN_V_HEADS, n, D, D), F32), shv],
        scratch_shapes=[pltpu.VMEM((E, D, D), F32)],
        compiler_params=_params(("parallel", "arbitrary")),
    )(q, k, v, beta, gcol, grow, amat)


def _gdn_bwd(q, k, v, beta, gcol, grow, states, amat, vnew, do):
    T = q.shape[0]
    C, D, E = GDN_CHUNK, GDN_HEAD_DIM, GDN_GROUP
    n, qk, vE, colv, colo, rowv, st, am = _gdn_specs(T)
    rev = lambda spec: pl.BlockSpec(spec.block_shape, (lambda f: (lambda h, i: f(h, n - 1 - i)))(spec.index_map))
    qk, vE, colv, colo, rowv, st, am = (rev(s) for s in (qk, vE, colv, colo, rowv, st, am))
    R = range(E)

    def body(q_ref, k_ref, v_ref, b_ref, gc_ref, gr_ref, s_ref, a_ref, vn_ref, do_ref,
             dq_ref, dk_ref, dv_ref, db_ref, dgc_ref, dstate):
        @pl.when(pl.program_id(1) == 0)
        def _():
            dstate[...] = jnp.zeros_like(dstate)

        M = lambda f: [f(e) for e in R]
        rsum = lambda x: jnp.sum(x, axis=1, keepdims=True)
        qv = M(lambda e: q_ref[:, (e // 2) * D:(e // 2 + 1) * D])
        kv = M(lambda e: k_ref[:, (e // 2) * D:(e // 2 + 1) * D])
        vv = M(lambda e: v_ref[:, e * D:(e + 1) * D])
        vn = M(lambda e: vn_ref[:, e * D:(e + 1) * D])
        dov = M(lambda e: do_ref[:, e * D:(e + 1) * D])
        head0 = E * pl.program_id(0)
        beta = M(lambda e: _lane_col(b_ref[...], head0 + e))
        s = M(lambda e: s_ref[e])
        a = M(lambda e: a_ref[e])
        dsn = M(lambda e: dstate[e])
        dec = M(lambda e: _gdn_decay(_lane_col(gc_ref[...], head0 + e), gr_ref[e]))
        dm, gam, glast, tail = (M(lambda e: dec[e][i]) for i in range(4))
        strict, incl = dec[0][4], dec[0][5]
        kb = M(lambda e: kv[e] * beta[e])
        kd = M(lambda e: kv[e] * gam[e])
        qd = M(lambda e: qv[e] * gam[e])
        kt = M(lambda e: kv[e] * tail[e])
        lmat = M(lambda e: jnp.where(strict, _bdot(kb[e], kv[e], NT) * dm[e], 0.0))
        pmat = M(lambda e: _bdot(qv[e], kv[e], NT) * dm[e])
        xres = M(lambda e: vv[e] - _bdot(kd[e], s[e]))
        dvn = M(lambda e: _bdot(pmat[e], dov[e], TN) + _bdot(kt[e], dsn[e]))
        dqd = M(lambda e: _bdot(dov[e], s[e], NT))
        dp = M(lambda e: jnp.where(incl, _bdot(dov[e], vn[e], NT), 0.0))
        dkt = M(lambda e: _bdot(vn[e], dsn[e], NT))
        dr = M(lambda e: _dot3(a[e], dvn[e], TN))
        drb = M(lambda e: beta[e] * dr[e])
        dkd = M(lambda e: -_bdot(drb[e], s[e], NT))
        ds2 = M(lambda e: _bdot(qd[e], dov[e], TN) + glast[e] * dsn[e] - _bdot(kd[e], drb[e], TN))
        dl = M(lambda e: -jnp.where(strict, _bdot(dr[e], vn[e], NT), 0.0))
        dmm = M(lambda e: dl[e] * dm[e])
        dnn = M(lambda e: dp[e] * dm[e])
        emat = M(lambda e: dl[e] * lmat[e] + dp[e] * pmat[e])
        dkb = M(lambda e: _bdot(dmm[e], kv[e]))
        dk = M(lambda e: beta[e] * dkb[e] + _bdot(dmm[e], kb[e], TN) + _bdot(dnn[e], qv[e], TN)
               + gam[e] * dkd[e] + tail[e] * dkt[e])
        dq = M(lambda e: _bdot(dnn[e], kv[e]) + gam[e] * dqd[e])
        dbeta = M(lambda e: rsum(dr[e] * xres[e]) + rsum(dkb[e] * kv[e]))
        ones = jnp.ones((C, LANE), BF16)
        colsum = M(lambda e: _dot2m(emat[e], ones, TN)[:, :1])
        tails = M(lambda e: rsum(dkt[e] * kt[e]))
        lastrow = _iota2((C, 1), 0) == C - 1
        dlast = M(lambda e: jnp.sum(tails[e], axis=0, keepdims=True)
                  + glast[e] * jnp.sum(rsum(s[e] * dsn[e]), axis=0, keepdims=True))
        dgc = M(lambda e: rsum(emat[e]) - colsum[e] + rsum(dkd[e] * kd[e]) + rsum(dqd[e] * qd[e]) - tails[e]
                + jnp.where(lastrow, dlast[e], 0.0))
        lane = _iota2((C, LANE), 1)
        db_all = jnp.zeros((C, LANE), F32)
        dgc_all = jnp.zeros((C, LANE), F32)
        for e in R:
            dv_ref[:, e * D:(e + 1) * D] = drb[e]
            db_all = jnp.where(lane == e, dbeta[e], db_all)
            dgc_all = jnp.where(lane == e, dgc[e], dgc_all)
            dstate[e] = ds2[e]
        db_ref[...] = db_all
        dgc_ref[...] = dgc_all
        for kh in range(E // 2):
            dq_ref[:, kh * D:(kh + 1) * D] = dq[2 * kh] + dq[2 * kh + 1]
            dk_ref[:, kh * D:(kh + 1) * D] = dk[2 * kh] + dk[2 * kh + 1]

    shq = jax.ShapeDtypeStruct((T, GDN_K_HEADS * D), F32)
    shv = jax.ShapeDtypeStruct((T, GDN_V_HEADS * D), F32)
    shc = jax.ShapeDtypeStruct((GDN_V_HEADS // E, T, LANE), F32)
    return _pcall(
        body, name="gdn_bwd", grid=(GDN_V_HEADS // E, n),
        in_specs=[qk, qk, vE, colv, colv, rowv, st, am, vE, vE],
        out_specs=[qk, qk, vE, colo, colo], out_shape=[shq, shq, shv, shc, shc],
        scratch_shapes=[pltpu.VMEM((E, D, D), F32)],
        compiler_params=_params(("parallel", "arbitrary")),
    )(q, k, v, beta, gcol, grow, states, amat, vnew, do)


def _outgate_fwd(o, proj, gain):
    T = o.shape[0]
    tm, tc = _pick(T, CONV_ROWS), CONV_COLS
    z0 = GDN_CONV_W // tc

    def body(o_ref, z_ref, g_ref, y_ref):
        z = z_ref[...]
        sz = z * _sigmoid(z)
        parts = []
        for hh in range(tc // GDN_HEAD_DIM):
            oh = o_ref[:, hh * GDN_HEAD_DIM:(hh + 1) * GDN_HEAD_DIM]
            r = lax.rsqrt(jnp.mean(oh * oh, axis=-1, keepdims=True) + EPS)
            parts.append(oh * r * g_ref[...])
        y_ref[...] = (jnp.concatenate(parts, axis=1) * sz).astype(BF16)

    blk = pl.BlockSpec((tm, tc), lambda i, j: (i, j))
    return _pcall(body, name="gdn_outgate", grid=(T // tm, GDN_VW // tc),
                  in_specs=[blk, pl.BlockSpec((tm, tc), lambda i, j: (i, j + z0)), pl.BlockSpec((1, GDN_HEAD_DIM), lambda i, j: (0, 0))],
                  out_specs=blk, out_shape=jax.ShapeDtypeStruct((T, GDN_VW), BF16),
                  compiler_params=_params(("parallel", "parallel")))(o, proj, gain)


def _outgate_bwd(dy, o, proj, gain):
    T = o.shape[0]
    tm, tc = _pick(T, CONV_ROWS), CONV_COLS
    z0 = GDN_CONV_W // tc
    nh = tc // GDN_HEAD_DIM

    def body(dy_ref, o_ref, z_ref, g_ref, do_ref, dz_ref, dg_ref):
        z = z_ref[...]
        sg = _sigmoid(z)
        sz = z * sg
        dy = dy_ref[...]
        dgain = jnp.zeros((1, GDN_HEAD_DIM), F32)
        dos, ys = [], []
        for hh in range(nh):
            sl = slice(hh * GDN_HEAD_DIM, (hh + 1) * GDN_HEAD_DIM)
            oh = o_ref[:, sl]
            r = lax.rsqrt(jnp.mean(oh * oh, axis=-1, keepdims=True) + EPS)
            xh = oh * r
            dn = dy[:, sl] * sz[:, sl]
            dgain = dgain + jnp.sum(dn * xh, axis=0, keepdims=True)
            dxh = dn * g_ref[...]
            dos.append(r * (dxh - xh * jnp.mean(dxh * xh, axis=-1, keepdims=True)))
            ys.append(xh * g_ref[...])
        do_ref[...] = jnp.concatenate(dos, axis=1)
        dz_ref[...] = (dy * jnp.concatenate(ys, axis=1) * sg * (1.0 + z * (1.0 - sg))).astype(BF16)
        first = (pl.program_id(0) == 0) & (pl.program_id(1) == 0)

        @pl.when(first)
        def _():
            dg_ref[...] = dgain

        @pl.when(jnp.logical_not(first))
        def _():
            dg_ref[...] += dgain

    blk = pl.BlockSpec((tm, tc), lambda i, j: (i, j))
    vec = pl.BlockSpec((1, GDN_HEAD_DIM), lambda i, j: (0, 0))
    return _pcall(body, name="gdn_doutgate", grid=(T // tm, GDN_VW // tc),
                  in_specs=[blk, blk, pl.BlockSpec((tm, tc), lambda i, j: (i, j + z0)), vec],
                  out_specs=[blk, blk, vec],
                  out_shape=[jax.ShapeDtypeStruct((T, GDN_VW), F32), jax.ShapeDtypeStruct((T, GDN_VW), BF16),
                             jax.ShapeDtypeStruct((1, GDN_HEAD_DIM), F32)],
                  compiler_params=_params(("arbitrary", "arbitrary")))(dy, o, proj, gain)


def _pad_lanes(vec):
    return jnp.pad(vec.reshape(1, -1), ((0, 0), (0, LANE - vec.shape[-1])))


def _head_rows(a):
    T = a.shape[0]
    return a[:, :GDN_V_HEADS].T.reshape(GDN_V_HEADS, T // GDN_CHUNK, 1, GDN_CHUNK)


def _gdn_pad_in(w_in):
    c = GDN_CONV_W + GDN_VW
    z = jnp.zeros(w_in.shape[:-1] + (LANE - GDN_V_HEADS,), w_in.dtype)
    return jnp.concatenate([w_in[..., :c + GDN_V_HEADS], z, w_in[..., c + GDN_V_HEADS:], z], axis=-1)


def _gdn_unpad_in(dw):
    c = GDN_CONV_W + GDN_VW
    return jnp.concatenate([dw[..., :c + GDN_V_HEADS], dw[..., c + LANE:c + LANE + GDN_V_HEADS]], axis=-1)


def _gdn_mixer_fwd(h, g, w_in_pad, conv_w, a_log, dt_bias, out_gain, w_out):
    T = h.shape[0]
    hn = _rms_fwd(h, g, "gdn_norm")
    proj = _mm(hn, w_in_pad, "nn", name="gdn_in")
    qk = _conv_fwd(proj, conv_w, 0, 2 * GDN_KW, True, "gdn_conv_qk")
    vv = _conv_fwd(proj, conv_w, 2 * GDN_KW, GDN_VW, False, "gdn_conv_v")
    alog, dtb = _pad_lanes(a_log), _pad_lanes(dt_bias)
    beta, gl, gc = _gates_fwd(proj, alog, dtb)
    grow = _head_rows(gc)
    qn, kn = qk[:, :GDN_KW], qk[:, GDN_KW:]
    amat = _gdn_prep(kn, beta, gc, grow)
    o, states, vnew = _gdn_fwd(qn, kn, vv, beta, gc, grow, amat)
    gain = out_gain.reshape(1, GDN_HEAD_DIM)
    y = _outgate_fwd(o, proj, gain)
    h2 = _mm(y, w_out, "nn", res=h, name="gdn_out")
    return h2, (h, hn, proj, qn, kn, vv, beta, gl, gc, grow, o, states, amat, vnew, y, alog, dtb, gain)


def _gdn_mixer_bwd(dh2, saved, g, w_in_pad, conv_w, w_out):
    h, hn, proj, qn, kn, vv, beta, gl, gc, grow, o, states, amat, vnew, y, alog, dtb, gain = saved
    T = h.shape[0]
    dy = _mm(dh2, w_out, "nt", name="gdn_dy")
    dw_out = _mm(y, dh2, "tn", out_dtype=BF16, name="gdn_dwout")
    do, dz, dgain = _outgate_bwd(dy, o, proj, gain)
    dq, dk, dv, dbeta, dgc = _gdn_bwd(qn, kn, vv, beta, gc, grow, states, amat, vnew, do)
    dqk = jnp.concatenate([dq, dk], axis=1)
    dy_qk, dcw_qk = _conv_bwd_pre(proj, conv_w, dqk, 0, 2 * GDN_KW, True, "gdn_dconv_qk")
    dy_v, dcw_v = _conv_bwd_pre(proj, conv_w, dv, 2 * GDN_KW, GDN_VW, False, "gdn_dconv_v")
    dx_qk = _conv_bwd_in(dy_qk, conv_w[:, :2 * GDN_KW], "gdn_dconvin_qk")
    dx_v = _conv_bwd_in(dy_v, conv_w[:, 2 * GDN_KW:], "gdn_dconvin_v")
    dbl, da, dalog, ddt = _gates_bwd(proj, alog, dtb, beta, gl, dbeta, dgc)
    dproj = jnp.concatenate([dx_qk, dx_v, dz, dbl, da], axis=1)
    dw_in_pad = _mm(hn, dproj, "tn", out_dtype=BF16, name="gdn_dwin")
    dhn = _mm(dproj, w_in_pad, "nt", name="gdn_dhn")
    dh, dg = _rms_bwd(dhn, h, g, dh2, "gdn_dnorm")
    dconv = jnp.concatenate([dcw_qk, dcw_v], axis=1)
    return (dh, dg, _gdn_unpad_in(dw_in_pad), dconv, dalog[0, :GDN_V_HEADS], ddt[0, :GDN_V_HEADS],
            dgain.reshape(GDN_HEAD_DIM), dw_out)


def _instances(full):
    out = {}
    for n, a in full.items():
        if n.startswith("ffn_"):
            for i in range(2):
                for j in range(2):
                    out[(n, i, j)] = a[i, j]
        elif n in ("mix_norm", "ple_norm", "ple_w_gate", "ple_w_proj"):
            for i in range(2):
                out[(n, i)] = a[i]
        else:
            out[(n,)] = a[0]
    return out


def _stacked(inst):
    out = {}
    for n in dict.fromkeys(k[0] for k in inst):
        if n.startswith("ffn_"):
            out[n] = jnp.stack([jnp.stack([inst[(n, i, j)] for j in range(2)]) for i in range(2)])
        elif n in ("mix_norm", "ple_norm", "ple_w_gate", "ple_w_proj"):
            out[n] = jnp.stack([inst[(n, i)] for i in range(2)])
        else:
            out[n] = inst[(n,)][None]
    return out


def _local_step(x, p, target, w, late_shards=(), late_weights=None, early_grads=None, first_shards=(), first_weights=None):
    w = dict(w)
    ffn = lambda i, j: (w[("ffn_norm", i, j)], w[("ffn_w_gate", i, j)], w[("ffn_w_up", i, j)], w[("ffn_w_down", i, j)])
    h = x
    tape = []
    for i in range(2):
        if i == 0 and first_weights is not None:
            def wd_of(gathered):
                w.update(first_weights(gathered))
                return w[("ffn_w_down", 0, 0)]
            h, s1 = _ffn_fwd(h, w[("ffn_norm", 0, 0)], w[("ffn_w_gate", 0, 0)], w[("ffn_w_up", 0, 0)], None, "ffn0a",
                             first_shards, wd_of)
        else:
            h, s1 = _ffn_fwd(h, *ffn(i, 0), f"ffn{i}a")
        if i == 0:
            def w_out_of(gathered):
                if late_weights is not None:
                    w.update(late_weights(gathered))
                return w[("att_w_out",)]
            h, s2, _ = _att_fwd(h, w[("mix_norm", 0)], w[("att_w_in",)], w_out_of, w[("att_q_norm",)],
                                w[("att_k_norm",)], w[("att_sinks",)], late_shards)
        else:
            gdn_in_pad = _gdn_pad_in(w[("gdn_w_in",)])
            h, s2 = _gdn_mixer_fwd(h, w[("mix_norm", 1)], gdn_in_pad, w[("gdn_conv_w",)], w[("gdn_a_log",)],
                                   w[("gdn_dt_bias",)], w[("gdn_out_norm",)], w[("gdn_w_out",)])
        h, s3 = _ffn_fwd(h, *ffn(i, 1), f"ffn{i}b")
        h, s4 = _ple_fwd(h, p[i], w[("ple_norm", i)], w[("ple_w_gate", i)], w[("ple_w_proj", i)], f"ple{i}")
        tape.append((s1, s2, s3, s4))

    loss, dh = _loss_head(h, target)

    g = {}
    rode = []
    for i in (1, 0):
        s1, s2, s3, s4 = tape[i]
        dh, g[("ple_norm", i)], g[("ple_w_gate", i)], g[("ple_w_proj", i)] = _ple_bwd(
            dh, s4, p[i], w[("ple_norm", i)], w[("ple_w_gate", i)], f"ple{i}")
        dh, g[("ffn_norm", i, 1)], g[("ffn_w_gate", i, 1)], g[("ffn_w_up", i, 1)], g[("ffn_w_down", i, 1)] = _ffn_bwd(
            dh, s3, *ffn(i, 1), f"ffn{i}b")
        if i == 0:
            ride = early_grads(g) if early_grads is not None else ()
            (dh, g[("mix_norm", 0)], g[("att_w_in",)], g[("att_w_out",)], g[("att_q_norm",)], g[("att_k_norm",)],
             g[("att_sinks",)], rode) = _att_bwd(dh, s2, w[("mix_norm", 0)], w[("att_w_in",)], w[("att_w_out",)], ride)
        else:
            (dh, g[("mix_norm", 1)], g[("gdn_w_in",)], g[("gdn_conv_w",)], g[("gdn_a_log",)], g[("gdn_dt_bias",)],
             g[("gdn_out_norm",)], g[("gdn_w_out",)]) = _gdn_mixer_bwd(
                dh, s2, w[("mix_norm", 1)], gdn_in_pad, w[("gdn_conv_w",)], w[("gdn_w_out",)])
        dh, g[("ffn_norm", i, 0)], g[("ffn_w_gate", i, 0)], g[("ffn_w_up", i, 0)], g[("ffn_w_down", i, 0)] = _ffn_bwd(
            dh, s1, *ffn(i, 0), f"ffn{i}a")
    return loss, dh, g, rode


MESH = pl.DeviceIdType.MESH


def _place():
    x, y, c = lax.axis_index("x"), lax.axis_index("y"), lax.axis_index("c")
    others = [((1 - x, y), 2 * (1 - x) + y), ((x, 1 - y), 2 * x + (1 - y)), ((1 - x, 1 - y), 2 * (1 - x) + (1 - y))]
    return x, y, c, 4 * x + 2 * y + c, 2 * x + y, (x, y, 1 - c), others


def _comm_call(body, arrays, out_shape, n_sems, name):
    hbm = pl.BlockSpec(memory_space=pl.ANY)
    n = len(arrays)
    return _pcall(
        body, name=name, in_specs=[hbm] * n, out_specs=[hbm] * len(out_shape), out_shape=out_shape,
        scratch_shapes=[pltpu.SemaphoreType.DMA((n, n_sems)), pltpu.SemaphoreType.DMA((n, n_sems)),
                        pltpu.SemaphoreType.DMA((n, N_CHIP))],
        compiler_params=pltpu.CompilerParams(has_side_effects=True),
    )(*arrays)


def _gather_protocol(ins, outs, send_sems, recv_sems, local_sems):
    n = len(ins)
    x, y, c, me, my_chip, sibling, others = _place()

    def copy(a, k, block, to, src=None):
        dst = outs[a].at[block]
        return pltpu.make_async_remote_copy(
            src_ref=dst if src is None else src, dst_ref=dst, send_sem=send_sems.at[a, k],
            recv_sem=recv_sems.at[a, k], device_id=to, device_id_type=MESH)

    local = [pltpu.make_async_copy(ins[a], outs[a].at[me], local_sems.at[a, 0]) for a in range(n)]
    first = []
    for a in range(n):
        first.append(copy(a, 0, me, sibling, src=ins[a]))
        first += [copy(a, 1 + j, me, (*chip, c), src=ins[a]) for j, (chip, _) in enumerate(others)]

    def start():
        for cp in local + first:
            cp.start()

    def finish():
        passed = []
        for a in range(n):
            for j, (chip, chip_idx) in enumerate(others):
                blk = 2 * chip_idx + c
                copy(a, 1 + j, blk, (x, y, c)).wait_recv()
                fwd = copy(a, 4 + j, blk, sibling)
                fwd.start()
                passed.append(fwd)
        for a in range(n):
            copy(a, 0, 2 * my_chip + (1 - c), (x, y, c)).wait_recv()
            for j, (chip, chip_idx) in enumerate(others):
                copy(a, 4 + j, 2 * chip_idx + (1 - c), (x, y, c)).wait_recv()
        for cp in first + passed:
            cp.wait_send()
        for cp in local:
            cp.wait()

    return start, finish


def _all_gather(arrays):
    n = len(arrays)

    def body(*refs):
        start, finish = _gather_protocol(refs[:n], refs[n:2 * n], *refs[2 * n:])
        start()
        finish()

    out_shape = [jax.ShapeDtypeStruct((N_DEV,) + a.shape, a.dtype) for a in arrays]
    return _comm_call(body, arrays, out_shape, N_DEV - 1, "gather_weights")


def _exchange_sibling(arrays, name):
    n = len(arrays)

    def body(*refs):
        ins, got = refs[:n], refs[n:2 * n]
        send_sems, recv_sems, _ = refs[2 * n:]
        x, y, c, me, my_chip, sibling, others = _place()
        remote = []
        for a in range(n):
            for chip in range(N_CHIP):
                rc = pltpu.make_async_remote_copy(
                    src_ref=ins[a].at[2 * chip + (1 - c)], dst_ref=got[a].at[chip], send_sem=send_sems.at[a, chip],
                    recv_sem=recv_sems.at[a, chip], device_id=sibling, device_id_type=MESH)
                rc.start()
                remote.append(rc)
        for rc in remote:
            rc.wait()

    half = [jax.ShapeDtypeStruct((N_CHIP,) + a.shape[1:], a.dtype) for a in arrays]
    return _comm_call(body, arrays, half, N_CHIP, name)


def _chips_protocol(ins, outs, send_sems, recv_sems, local_sems):
    n = len(ins)
    x, y, c, me, my_chip, sibling, others = _place()
    local = [pltpu.make_async_copy(ins[a].at[my_chip], outs[a].at[my_chip], local_sems.at[a, 0]) for a in range(n)]
    remote = [pltpu.make_async_remote_copy(
        src_ref=ins[a].at[chip_idx], dst_ref=outs[a].at[my_chip], send_sem=send_sems.at[a, j],
        recv_sem=recv_sems.at[a, j], device_id=(*chip, c), device_id_type=MESH)
        for a in range(n) for j, (chip, chip_idx) in enumerate(others)]

    def start():
        for cp in local + remote:
            cp.start()

    def finish():
        for cp in remote + local:
            cp.wait()

    return start, finish


def _exchange_chips(arrays, name):
    n = len(arrays)

    def body(*refs):
        start, finish = _chips_protocol(refs[:n], refs[n:2 * n], *refs[2 * n:])
        start()
        finish()

    out_shape = [jax.ShapeDtypeStruct(a.shape, a.dtype) for a in arrays]
    return _comm_call(body, arrays, out_shape, N_CHIP - 1, name)


def _as_rows(a, lead):
    shp = a.shape
    return a.reshape(shp[:lead] + (math.prod(shp[lead:-1]), shp[-1]))


def _row_tile(rows, cap=512):
    if rows <= cap:
        return rows
    for t in range(cap - cap % 8, 0, -8):
        if rows % t == 0:
            return t
    return rows


def _pair_sum(send, got, name):
    a3, b3 = _as_rows(send, 1), _as_rows(got, 1)
    _, rows, last = b3.shape
    tr = _row_tile(rows)

    def body(c_ref, a_ref, b_ref, o_ref):
        o_ref[...] = (a_ref[...].astype(F32) + b_ref[...].astype(F32)).astype(o_ref.dtype)

    core = lax.axis_index("c").astype(jnp.int32).reshape(1)
    out = _pcall(
        body, name=name,
        grid_spec=pltpu.PrefetchScalarGridSpec(
            num_scalar_prefetch=1, grid=(N_CHIP, rows // tr),
            in_specs=[pl.BlockSpec((None, tr, last), lambda k, i, c_ref: (2 * k + c_ref[0], i, 0)),
                      pl.BlockSpec((None, tr, last), lambda k, i, c_ref: (k, i, 0))],
            out_specs=pl.BlockSpec((None, tr, last), lambda k, i, c_ref: (k, i, 0))),
        out_shape=jax.ShapeDtypeStruct(b3.shape, got.dtype), compiler_params=_params(("parallel", "parallel")),
    )(core, a3, b3)
    return out.reshape(got.shape)


def _adamw(parts, w, m, v, name):
    lead, (rows, last) = w.shape[:-2], w.shape[-2:]
    nl = len(lead)
    tr = _row_tile(rows)
    c1 = 1.0 / (1.0 - ADAM_B1 ** ADAM_STEP)
    c2 = 1.0 / (1.0 - ADAM_B2 ** ADAM_STEP)

    def body(p_ref, w_ref, m_ref, v_ref, g_ref, d_ref, nm_ref, nv_ref):
        g = p_ref[0].astype(F32)
        for chip in range(1, N_CHIP):
            g = g + p_ref[chip].astype(F32)
        mn = ADAM_B1 * m_ref[...] + (1.0 - ADAM_B1) * g
        vn = ADAM_B2 * v_ref[...] + (1.0 - ADAM_B2) * (g * g)
        g_ref[...] = g
        nm_ref[...] = mn
        nv_ref[...] = vn
        d_ref[...] = -ADAM_LR * ((mn * c1) / (jnp.sqrt(vn * c2) + ADAM_EPS) + ADAM_WD * w_ref[...])

    row = pl.BlockSpec((None,) * nl + (tr, last), lambda *ix: ix + (0,))
    part = pl.BlockSpec((N_CHIP,) + (None,) * nl + (tr, last), lambda *ix: (0,) + ix + (0,))
    sh = jax.ShapeDtypeStruct(w.shape, F32)
    return _pcall(body, name=name, grid=lead + (rows // tr,), in_specs=[part, row, row, row],
                  out_specs=[row, row, row, row], out_shape=[sh, sh, sh, sh],
                  compiler_params=_params(("parallel",) * (nl + 1)))(parts, w, m, v)


def _pack(pieces, row_align):
    rows, offs, r = [], [], 0
    for a in pieces:
        flat = a.reshape(-1)
        nr = -(-flat.shape[0] // PACK_W)
        flat = jnp.pad(flat, (0, nr * PACK_W - flat.shape[0]))
        rows.append(flat.reshape(nr, PACK_W))
        offs.append(r)
        r += nr
    pad = (-r) % row_align
    if pad:
        rows.append(jnp.zeros((pad, PACK_W), pieces[0].dtype))
    return jnp.concatenate(rows, axis=0), offs


def _unpack(flat, offs, shapes):
    out = []
    for off, shp in zip(offs, shapes):
        size = math.prod(shp)
        nr = -(-size // PACK_W)
        out.append(flat[..., off:off + nr, :].reshape(flat.shape[:-2] + (nr * PACK_W,))[..., :size].reshape(flat.shape[:-2] + tuple(shp)))
    return out


def _to_full(gathered, axis):
    z = jnp.moveaxis(gathered, 0, axis)
    shp = list(z.shape)
    return z.reshape(shp[:axis] + [shp[axis] * shp[axis + 1]] + shp[axis + 2:])


def _to_shards(full, axis):
    shp = list(full.shape)
    z = full.reshape(shp[:axis] + [N_DEV, shp[axis] // N_DEV] + shp[axis + 1:])
    return jnp.moveaxis(z, axis, 0)


def kernel(x, p, ffn_norm, ffn_w_gate, ffn_w_up, ffn_w_down, mix_norm, att_w_in, att_q_norm, att_k_norm, att_sinks, att_w_out, gdn_w_in, gdn_conv_w, gdn_a_log, gdn_dt_bias, gdn_out_norm, gdn_w_out, ple_norm, ple_w_gate, ple_w_proj, loss_target, m_ffn_norm, m_ffn_w_gate, m_ffn_w_up, m_ffn_w_down, m_mix_norm, m_att_w_in, m_att_q_norm, m_att_k_norm, m_att_sinks, m_att_w_out, m_gdn_w_in, m_gdn_conv_w, m_gdn_a_log, m_gdn_dt_bias, m_gdn_out_norm, m_gdn_w_out, m_ple_norm, m_ple_w_gate, m_ple_w_proj, v_ffn_norm, v_ffn_w_gate, v_ffn_w_up, v_ffn_w_down, v_mix_norm, v_att_w_in, v_att_q_norm, v_att_k_norm, v_att_sinks, v_att_w_out, v_gdn_w_in, v_gdn_conv_w, v_gdn_a_log, v_gdn_dt_bias, v_gdn_out_norm, v_gdn_w_out, v_ple_norm, v_ple_w_gate, v_ple_w_proj):
    args = dict(locals())
    wts = {n: args[n] for n in WEIGHTS}
    mom = {n: args["m_" + n] for n in WEIGHTS}
    var = {n: args["v_" + n] for n in WEIGHTS}
    axis = dict(SHARDED)
    vecs = [n for n, _ in SHARDED[:SMALL_SHARDED]]
    small = vecs + list(REPLICATED)
    small_shapes = [wts[n].shape for n in small]
    lead = lambda n: 2 if n.startswith("ffn_") else 1

    def stack_of(arrays, name, idxs):
        return jnp.stack([arrays[name][idx] if idx else arrays[name][0] for idx in idxs])

    def full_instances(gathered, group):
        out = {}
        for (name, idxs), g in zip(group, gathered):
            whole = _to_full(g, axis[name] - lead(name) + 1)
            for k, idx in enumerate(idxs):
                out[(name,) + idx] = whole[k]
        return out

    def shard_stacks(g, group):
        return [_to_shards(jnp.stack([g[(name,) + idx] for idx in idxs]), axis[name] - lead(name) + 1)
                for name, idxs in group]

    vec_pack, voffs = _pack([wts[n] for n in vecs], 8)
    early = _all_gather([stack_of(wts, n, idxs).astype(BF16) for n, idxs in EARLY] + [vec_pack])
    w = full_instances(early[:-1], EARLY)
    vec_full = {n: _to_full(piece, axis[n]) for n, piece in
                zip(vecs, _unpack(early[-1], voffs, [wts[n].shape for n in vecs]))}
    w.update(_instances({**vec_full, **{n: wts[n] for n in REPLICATED}}))
    first_shards = [stack_of(wts, n, idxs).astype(BF16) for n, idxs in FIRST]
    late_shards = [stack_of(wts, n, idxs).astype(BF16) for n, idxs in LATE]

    def early_grads(g):
        send = shard_stacks(g, RIDE)
        got = _exchange_sibling(send, "exchange_sibling_early")
        return [_pair_sum(p_, q_, f"pair_sum_early_{i}") for i, (p_, q_) in enumerate(zip(send, got))]

    loss, grad_x, g, rode = _local_step(x[0], p[:, 0], loss_target[0], w, late_shards,
                                        lambda gathered: full_instances(gathered, LATE), early_grads,
                                        first_shards, lambda gathered: full_instances(gathered, FIRST))

    gs = _stacked({k: v for k, v in g.items() if k[0] in small})
    vec_shards = [_to_shards(gs[n], axis[n]) for n in vecs]
    small_send = jnp.stack([_pack([sh[d] for sh in vec_shards] + [gs[n] for n in REPLICATED] + [loss.reshape(1)], 8)[0]
                            for d in range(N_DEV)])
    send = shard_stacks(g, FINAL) + [small_send]
    got = _exchange_sibling(send, "exchange_sibling_final")
    chip_sums = [_pair_sum(p_, q_, f"pair_sum_final_{i}") for i, (p_, q_) in enumerate(zip(send, got))]
    last = _exchange_chips(chip_sums, "exchange_chips_final")

    pieces = {}
    for (name, idxs), part in list(zip(RIDE, rode)) + list(zip(FINAL, last[:-1])):
        for k, idx in enumerate(idxs):
            pieces[(name,) + idx] = part[:, k]
    outs = {}
    for n, _ in SHARDED[SMALL_SHARDED:]:
        if lead(n) == 2:
            part = jnp.stack([jnp.stack([pieces[(n, i, j)] for j in range(2)], axis=1) for i in range(2)], axis=1)
        elif (n, 0) in pieces:
            part = jnp.stack([pieces[(n, i)] for i in range(2)], axis=1)
        else:
            part = pieces[(n,)][:, None]
        outs[n] = _adamw(part, wts[n], mom[n], var[n], f"adamw_{n}")
    filler = [jnp.zeros((1,), F32)]
    small_w, soffs = _pack([wts[n] for n in small] + filler, 8)
    small_m, _ = _pack([mom[n] for n in small] + filler, 8)
    small_v, _ = _pack([var[n] for n in small] + filler, 8)
    small_out = [_unpack(z, soffs, small_shapes + [(1,)]) for z in _adamw(last[-1], small_w, small_m, small_v, "adamw_small")]
    loss = small_out[0][-1][0]
    for i, n in enumerate(small):
        outs[n] = [small_out[k][i] for k in range(4)]
    result = [loss, grad_x[None]]
    for k in range(4):
        result += [outs[n][k] for n in WEIGHTS]
    return tuple(result)
```

```python
import math

import jax
import jax.numpy as jnp
from jax import lax
from jax.experimental import pallas as pl
from jax.experimental.pallas import tpu as pltpu

F32 = jnp.float32
BF16 = jnp.bfloat16

N_DEV = 8
N_CHIP = 4
D_MODEL = 1024
D_FF = 2816
PLE_DIM = 256
HEAD_DIM = 64
SB_HEADS = 8
SWA_HEADS = 8
SWA_KV_HEADS = 2
SWA_GROUP = SWA_HEADS // SWA_KV_HEADS
WINDOW = 128
Q_BLOCK = 128
GDN_K_HEADS = 8
GDN_V_HEADS = 16
GDN_HEAD_DIM = 128
GDN_CONV = 4
GDN_CHUNK = 64
EPS = 1e-6
SB_W = SB_HEADS * HEAD_DIM
SWA_QW = SWA_HEADS * HEAD_DIM
SWA_KVW = SWA_KV_HEADS * HEAD_DIM
ATT_IN = 3 * SB_W + SWA_QW + 2 * SWA_KVW
GDN_KW = GDN_K_HEADS * GDN_HEAD_DIM
GDN_VW = GDN_V_HEADS * GDN_HEAD_DIM
GDN_CONV_W = 2 * GDN_KW + GDN_VW
GDN_IN = GDN_CONV_W + GDN_VW + 2 * GDN_V_HEADS
GDN_IN_PAD = GDN_CONV_W + GDN_VW + 2 * 128

ADAM_LR = 0.001
ADAM_B1 = 0.9
ADAM_B2 = 0.999
ADAM_EPS = 1e-08
ADAM_WD = 0.01
ADAM_STEP = 10

LANE = 128
VMEM_LIMIT = 56 * 1024 * 1024
MM_TILE_BUDGET = 40 * 1024 * 1024
PACK_W = 1024

NN = ((1,), (0,))
NT = ((1,), (1,))
TN = ((0,), (0,))

SHARDED = (
    ("ffn_norm", 2), ("gdn_conv_w", 2),
    ("ffn_w_gate", 3), ("ffn_w_up", 3), ("ffn_w_down", 2), ("att_w_in", 2), ("att_w_out", 1),
    ("gdn_w_in", 2), ("gdn_w_out", 1), ("ple_w_gate", 1), ("ple_w_proj", 2),
)
SMALL_SHARDED = 2
REPLICATED = ("mix_norm", "att_q_norm", "att_k_norm", "att_sinks", "gdn_a_log", "gdn_dt_bias",
              "gdn_out_norm", "ple_norm")
WEIGHTS = ("ffn_norm", "ffn_w_gate", "ffn_w_up", "ffn_w_down", "mix_norm", "att_w_in", "att_q_norm",
           "att_k_norm", "att_sinks", "att_w_out", "gdn_w_in", "gdn_conv_w", "gdn_a_log", "gdn_dt_bias",
           "gdn_out_norm", "gdn_w_out", "ple_norm", "ple_w_gate", "ple_w_proj")


_FFN_REST = [(0, 1), (1, 0), (1, 1)]
EARLY = [("ffn_w_gate", [(0, 0)]), ("ffn_w_up", [(0, 0)])]
FIRST = [("ffn_w_down", [(0, 0)]), ("att_w_in", [()])]
LATE = ([(n, [idx]) for n in ("ffn_w_gate", "ffn_w_up", "ffn_w_down") for idx in _FFN_REST]
        + [("att_w_out", [()]), ("gdn_w_in", [()]), ("gdn_w_out", [()]),
           ("ple_w_gate", [(0,), (1,)]), ("ple_w_proj", [(0,), (1,)])])
RIDE = [e for e in LATE if e[0] != "att_w_out"]
FINAL = EARLY + FIRST + [("att_w_out", [()])]


def _pcall(body, **kw):
    return pl.pallas_call(body, **kw)


def _params(sem=None):
    if sem is None:
        return pltpu.CompilerParams(vmem_limit_bytes=VMEM_LIMIT)
    return pltpu.CompilerParams(dimension_semantics=sem, vmem_limit_bytes=VMEM_LIMIT)


def _ride_specs(ride, out_shapes, n_sems):
    hbm = pl.BlockSpec(memory_space=pl.ANY)
    n = len(ride)
    sems = [pltpu.SemaphoreType.DMA((n, n_sems)), pltpu.SemaphoreType.DMA((n, n_sems)),
            pltpu.SemaphoreType.DMA((n, N_CHIP))] if n else []
    return [hbm] * n, [hbm] * len(out_shapes), sems


def _dot(a, b, dims=NN):
    return lax.dot_general(a, b, (dims, ((), ())), preferred_element_type=F32)


def _bdot(a, b, dims=NN):
    return _dot(a.astype(BF16), b.astype(BF16), dims)


def _split(a):
    hi = a.astype(BF16)
    lo = (a - hi.astype(F32)).astype(BF16)
    return hi, lo


def _dot3(a, b, dims=NN):
    ah, al = _split(a)
    bh, bl = _split(b)
    return _dot(ah, bh, dims) + (_dot(ah, bl, dims) + _dot(al, bh, dims))


def _dot2m(a, m, dims=NN):
    ah, al = _split(a)
    return _dot(ah, m, dims) + _dot(al, m, dims)


def _mdot2(m, a, dims=NN):
    ah, al = _split(a)
    return _dot(m, ah, dims) + _dot(m, al, dims)


def _sigmoid(x):
    return 1.0 / (1.0 + jnp.exp(-x))


def _softplus(x):
    return jnp.maximum(x, 0.0) + jnp.log(1.0 + jnp.exp(-jnp.abs(x)))


def _pick(n, cap):
    if n <= cap:
        return n
    for t in range(cap - cap % LANE, 0, -LANE):
        if n % t == 0:
            return t
    raise ValueError(f"no tile for {n} under {cap}")


def _iota2(shape, axis):
    return lax.broadcasted_iota(jnp.int32, shape, axis)


def _mm(a, b, mode, out_dtype=F32, res=None, alpha=1.0, a2=None, b2=None, name="mm"):
    if mode == "nn":
        (M, K), N = a.shape, b.shape[1]
    elif mode == "nt":
        (M, K), N = a.shape, b.shape[0]
    else:
        (K, M), N = a.shape, b.shape[1]
    tn, tk = _pick(N, 1408), _pick(K, 2048 if mode == "tn" else 1408)
    nk = K // tk
    pairs = 1 if a2 is None else 2

    def tile_bytes(tm):
        per = pairs * tk * (tm * a.dtype.itemsize + tn * b.dtype.itemsize) + tm * tn * jnp.dtype(out_dtype).itemsize
        return 2 * (per + (tm * tn * 4 if res is not None else 0)) + (tm * tn * 4 if nk > 1 else 0)

    tm = next(t for t in (_pick(M, c) for c in ((1408,) if mode == "tn" else (2048, 1024, 512))) if tile_bytes(t) <= MM_TILE_BUDGET or t <= 512)
    dims = {"nn": NN, "nt": NT, "tn": TN}[mode]
    a_spec = pl.BlockSpec((tk, tm), lambda i, j, k: (k, i)) if mode == "tn" else pl.BlockSpec((tm, tk), lambda i, j, k: (i, k))
    b_spec = pl.BlockSpec((tn, tk), lambda i, j, k: (j, k)) if mode == "nt" else pl.BlockSpec((tk, tn), lambda i, j, k: (k, j))
    o_spec = pl.BlockSpec((tm, tn), lambda i, j, k: (i, j))
    two = a2 is not None
    has_res = res is not None
    a2_spec, b2_spec = a_spec, b_spec
    if two and a2.shape != a.shape:
        assert nk == 1 and mode == "nn" and a2.shape[0] == M and b2.shape[1] == N
        a2_spec = pl.BlockSpec((tm, a2.shape[1]), lambda i, j, k: (i, 0))
        b2_spec = pl.BlockSpec((a2.shape[1], tn), lambda i, j, k: (0, j))

    def body(*refs):
        refs = list(refs)
        a_ref, b_ref = refs[0], refs[1]
        pos = 2
        if two:
            a2_ref, b2_ref = refs[2], refs[3]
            pos = 4
        if has_res:
            res_ref = refs[pos]
            pos += 1
        o_ref, acc_ref = refs[pos], refs[pos + 1]
        k = pl.program_id(2)
        part = _bdot(a_ref[...], b_ref[...], dims)
        if two:
            part = part + _bdot(a2_ref[...], b2_ref[...], dims)

        def finish(acc):
            out = acc * alpha if alpha != 1.0 else acc
            if has_res:
                out = res_ref[...] + out
            o_ref[...] = out.astype(out_dtype)

        if nk == 1:
            finish(part)
        else:
            @pl.when(k == 0)
            def _():
                acc_ref[...] = part

            @pl.when(k > 0)
            def _():
                acc_ref[...] += part

            @pl.when(k == nk - 1)
            def _():
                finish(acc_ref[...])

    ins = [a, b]
    specs = [a_spec, b_spec]
    if two:
        ins += [a2, b2]
        specs += [a2_spec, b2_spec]
    if has_res:
        ins.append(res)
        specs.append(o_spec)
    return _pcall(
        body, name=name, grid=(M // tm, N // tn, nk), in_specs=specs, out_specs=o_spec,
        out_shape=jax.ShapeDtypeStruct((M, N), out_dtype),
        scratch_shapes=[pltpu.VMEM((tm, tn) if nk > 1 else (8, LANE), F32)],
        compiler_params=_params(("parallel", "parallel", "arbitrary")),
    )(*ins)


ROW_TILE = 512


def _rms_fwd(h, g, name):
    T, D = h.shape
    tr = _pick(T, ROW_TILE)

    def body(h_ref, g_ref, n_ref):
        x = h_ref[...]
        r = lax.rsqrt(jnp.mean(x * x, axis=-1, keepdims=True) + EPS)
        n_ref[...] = (x * r * g_ref[...]).astype(BF16)

    return _pcall(
        body, name=name, grid=(T // tr,),
        in_specs=[pl.BlockSpec((tr, D), lambda i: (i, 0)), pl.BlockSpec((1, D), lambda i: (0, 0))],
        out_specs=pl.BlockSpec((tr, D), lambda i: (i, 0)),
        out_shape=jax.ShapeDtypeStruct((T, D), BF16), compiler_params=_params(("parallel",)),
    )(h, g.reshape(1, D))


def _rms_bwd(dn, h, g, dres, name):
    T, D = h.shape
    tr = _pick(T, ROW_TILE)

    def body(dn_ref, h_ref, g_ref, dres_ref, dh_ref, dg_ref):
        x = h_ref[...]
        r = lax.rsqrt(jnp.mean(x * x, axis=-1, keepdims=True) + EPS)
        xh = x * r
        d = dn_ref[...].astype(F32)
        dxh = d * g_ref[...]
        dh_ref[...] = dres_ref[...] + r * (dxh - xh * jnp.mean(dxh * xh, axis=-1, keepdims=True))
        part = jnp.sum(d * xh, axis=0, keepdims=True)

        @pl.when(pl.program_id(0) == 0)
        def _():
            dg_ref[...] = part

        @pl.when(pl.program_id(0) > 0)
        def _():
            dg_ref[...] += part

    row = pl.BlockSpec((tr, D), lambda i: (i, 0))
    vec = pl.BlockSpec((1, D), lambda i: (0, 0))
    dh, dg = _pcall(
        body, name=name, grid=(T // tr,), in_specs=[row, row, vec, row], out_specs=[row, vec],
        out_shape=[jax.ShapeDtypeStruct((T, D), F32), jax.ShapeDtypeStruct((1, D), F32)],
        compiler_params=_params(("arbitrary",)),
    )(dn, h, g.reshape(1, D), dres)
    return dh, dg.reshape(D)


def _gateup(n, wg, wu, name, ride=()):
    T, D = n.shape
    F = wg.shape[1]
    tm, tn = _pick(T, 1024), _pick(F, 1408)
    nr = len(ride)
    ride_out = [jax.ShapeDtypeStruct((N_DEV,) + r.shape, r.dtype) for r in ride]
    ride_in_specs, ride_out_specs, ride_sems = _ride_specs(ride, ride_out, N_DEV - 1)
    grid = (T // tm, F // tn)

    def body(*refs):
        n_ref, wg_ref, wu_ref = refs[:3]
        a_ref, b_ref, hid_ref = refs[3 + nr:6 + nr]
        if nr:
            i, j = pl.program_id(0), pl.program_id(1)
            start, finish = _gather_protocol(refs[3:3 + nr], refs[6 + nr:6 + 2 * nr], *refs[6 + 2 * nr:])
            pl.when((i == 0) & (j == 0))(start)
        x = n_ref[...]
        a = _dot(x, wg_ref[...])
        b = _dot(x, wu_ref[...])
        a_ref[...] = a.astype(BF16)
        b_ref[...] = b.astype(BF16)
        hid_ref[...] = (a * _sigmoid(a) * b).astype(BF16)
        if nr:
            pl.when((i == grid[0] - 1) & (j == grid[1] - 1))(finish)

    o_spec = pl.BlockSpec((tm, tn), lambda i, j: (i, j))
    w_spec = pl.BlockSpec((D, tn), lambda i, j: (0, j))
    sh = jax.ShapeDtypeStruct((T, F), BF16)
    res = _pcall(
        body, name=name, grid=grid,
        in_specs=[pl.BlockSpec((tm, D), lambda i, j: (i, 0)), w_spec, w_spec] + ride_in_specs,
        out_specs=[o_spec, o_spec, o_spec] + ride_out_specs, out_shape=[sh, sh, sh] + ride_out,
        scratch_shapes=ride_sems,
        compiler_params=_params(("arbitrary", "arbitrary") if nr else ("parallel", "parallel")),
    )(n, wg, wu, *ride)
    return res[0], res[1], res[2], list(res[3:])


def _ffn_dhid(dy, wd, a, b, name):
    T, D = dy.shape
    F = wd.shape[0]
    tm, tn = _pick(T, 1024), _pick(F, 1408)

    def body(dy_ref, wd_ref, a_ref, b_ref, da_ref, db_ref):
        dhid = 0.5 * _bdot(dy_ref[...], wd_ref[...], NT)
        av = a_ref[...].astype(F32)
        bv = b_ref[...].astype(F32)
        s = _sigmoid(av)
        da_ref[...] = (dhid * bv * s * (1.0 + av * (1.0 - s))).astype(BF16)
        db_ref[...] = (dhid * av * s).astype(BF16)

    o_spec = pl.BlockSpec((tm, tn), lambda i, j: (i, j))
    sh = jax.ShapeDtypeStruct((T, F), BF16)
    return _pcall(
        body, name=name, grid=(T // tm, F // tn),
        in_specs=[pl.BlockSpec((tm, D), lambda i, j: (i, 0)), pl.BlockSpec((tn, D), lambda i, j: (j, 0)), o_spec, o_spec],
        out_specs=[o_spec, o_spec], out_shape=[sh, sh],
        compiler_params=_params(("parallel", "parallel")),
    )(dy, wd, a, b)


def _ffn_fwd(h, g, wg, wu, wd, tag, ride=(), wd_of=None):
    n = _rms_fwd(h, g, f"{tag}_norm")
    a, b, hid, gathered = _gateup(n, wg, wu, f"{tag}_gateup", ride)
    if wd_of is not None:
        wd = wd_of(gathered)
    h2 = _mm(hid, wd, "nn", res=h, alpha=0.5, name=f"{tag}_down")
    return h2, (h, n, a, b, hid)


def _ffn_bwd(dh2, saved, g, wg, wu, wd, tag):
    h, n, a, b, hid = saved
    da, db = _ffn_dhid(dh2, wd, a, b, f"{tag}_dhid")
    dwd = _mm(hid, dh2, "tn", alpha=0.5, out_dtype=BF16, name=f"{tag}_dwd")
    dwg = _mm(n, da, "tn", out_dtype=BF16, name=f"{tag}_dwg")
    dwu = _mm(n, db, "tn", out_dtype=BF16, name=f"{tag}_dwu")
    dn = _mm(da, wg, "nt", a2=db, b2=wu, name=f"{tag}_dn")
    dh, dg = _rms_bwd(dn, h, g, dh2, f"{tag}_dnorm")
    return dh, dg, dwg, dwu, dwd


def _ple_fwd(h, p, g, w_gate, w_proj, tag):
    T, D = h.shape
    pn = _rms_fwd(h, g, f"{tag}_norm")
    tm, tn = _pick(T, 512), _pick(D, 1024)
    P = p.shape[1]

    def body(pn_ref, p_ref, wg_ref, wp_ref, h_ref, o_ref, gl_ref, pp_ref):
        gl = _dot(pn_ref[...], wg_ref[...])
        pp = _bdot(p_ref[...], wp_ref[...])
        gl_ref[...] = gl
        pp_ref[...] = pp
        o_ref[...] = h_ref[...] + _sigmoid(gl) * pp

    o_spec = pl.BlockSpec((tm, tn), lambda i, j: (i, j))
    sh = jax.ShapeDtypeStruct((T, D), F32)
    h2, gl, pp = _pcall(
        body, name=f"{tag}_fwd", grid=(T // tm, D // tn),
        in_specs=[pl.BlockSpec((tm, D), lambda i, j: (i, 0)), pl.BlockSpec((tm, P), lambda i, j: (i, 0)),
                  pl.BlockSpec((D, tn), lambda i, j: (0, j)), pl.BlockSpec((P, tn), lambda i, j: (0, j)), o_spec],
        out_specs=[o_spec, o_spec, o_spec], out_shape=[sh, sh, sh],
        compiler_params=_params(("parallel", "parallel")),
    )(pn, p, w_gate, w_proj, h)
    return h2, (h, pn, gl, pp)


def _ple_bwd(dh2, saved, p, g, w_gate, tag):
    h, pn, gl, pp = saved
    T, D = h.shape
    tr = _pick(T, ROW_TILE)

    def body(d_ref, gl_ref, pp_ref, dgl_ref, dpp_ref):
        d = d_ref[...]
        s = _sigmoid(gl_ref[...])
        dpp_ref[...] = (d * s).astype(BF16)
        dgl_ref[...] = (d * pp_ref[...] * s * (1.0 - s)).astype(BF16)

    row = pl.BlockSpec((tr, D), lambda i: (i, 0))
    sh = jax.ShapeDtypeStruct((T, D), BF16)
    dgl, dpp = _pcall(body, name=f"{tag}_dgate", grid=(T // tr,), in_specs=[row, row, row], out_specs=[row, row],
                      out_shape=[sh, sh], compiler_params=_params(("parallel",)))(dh2, gl, pp)
    dw_proj = _mm(p, dpp, "tn", out_dtype=BF16, name=f"{tag}_dwproj")
    dw_gate = _mm(pn, dgl, "tn", out_dtype=BF16, name=f"{tag}_dwgate")
    dpn = _mm(dgl, w_gate, "nt", name=f"{tag}_dpn")
    dh, dg = _rms_bwd(dpn, h, g, dh2, f"{tag}_dnorm")
    return dh, dg, dw_gate, dw_proj


def _loss_head(y, target):
    T, D = y.shape
    tr = _pick(T, ROW_TILE)

    def body(y_ref, t_ref, dy_ref, l_ref):
        e = y_ref[...] - t_ref[...]
        dy_ref[...] = e * (1.0 / D)
        part = jnp.sum(e * e, axis=0, keepdims=True)

        @pl.when(pl.program_id(0) == 0)
        def _():
            l_ref[...] = part

        @pl.when(pl.program_id(0) > 0)
        def _():
            l_ref[...] += part

    row = pl.BlockSpec((tr, D), lambda i: (i, 0))
    vec = pl.BlockSpec((1, D), lambda i: (0, 0))
    dy, l = _pcall(body, name="loss_head", grid=(T // tr,), in_specs=[row, row], out_specs=[row, vec],
                   out_shape=[jax.ShapeDtypeStruct((T, D), F32), jax.ShapeDtypeStruct((1, D), F32)],
                   compiler_params=_params(("arbitrary",)))(y, target)
    return (0.5 / D) * jnp.sum(l), dy


SB_LANES = SB_HEADS * 2 * HEAD_DIM


def _sb_consts():
    row = _iota2((Q_BLOCK, Q_BLOCK), 0)
    col = _iota2((Q_BLOCK, Q_BLOCK), 1)
    after = (row > col).astype(BF16)
    before = (row < col).astype(BF16)
    return col < row, after, before, col


def _sb_fwd(proj, ride=()):
    T = proj.shape[0]
    H, d, L = SB_HEADS, HEAD_DIM, 2 * HEAD_DIM
    nblk = T // Q_BLOCK
    scale = d ** -0.5
    n = len(ride)
    ride_out = [jax.ShapeDtypeStruct((N_DEV,) + a.shape, a.dtype) for a in ride]
    ride_in_specs, ride_out_specs, ride_sems = _ride_specs(ride, ride_out, N_DEV - 1)
    R = range(H)
    tile = lambda g: slice(g * L, (g + 1) * L)

    def body(*refs):
        q_ref, kv_ref = refs[:2]
        rin = refs[2:2 + n]
        o_ref, c_ref = refs[2 + n:4 + n]
        rout = refs[4 + n:4 + 2 * n]
        run_ref = refs[4 + 2 * n]
        i = pl.program_id(0)
        if n:
            start, finish = _gather_protocol(rin, rout, *refs[5 + 2 * n:])
            pl.when(i == 0)(start)
        causal, after, _, col = _sb_consts()
        qs = [q_ref[:, tile(g)] * scale for g in R]
        o_ref[...] = jnp.zeros_like(o_ref)
        c_ref[...] = jnp.zeros_like(c_ref)
        run_ref[...] = jnp.zeros_like(run_ref)

        def pair(j, diag):
            rows = pl.ds(pl.multiple_of(j * Q_BLOCK, Q_BLOCK), Q_BLOCK)
            kvj = [kv_ref[rows, tile(g)] for g in R]
            c = [run_ref[g] for g in R]
            acc = [o_ref[:, tile(g)] for g in R]
            cm = None if diag else [c_ref[:, tile(g)] for g in R]
            z = [_dot(qs[g], kvj[g], NT) for g in R]
            sp = [_softplus(z[g]) for g in R]
            lk = [jnp.where(causal, -sp[g], 0.0) if diag else -sp[g] for g in R]
            btw = [_dot2m(lk[g], after) for g in R]
            e = [jnp.exp((z[g] - sp[g]) + btw[g] + c[g]) for g in R]
            w = [jnp.where(causal, e[g], 0.0) if diag else e[g] for g in R]
            pv = [_bdot(w[g], kvj[g]) for g in R]
            rs = [jnp.sum(lk[g], axis=1, keepdims=True) for g in R]
            for g in R:
                o_ref[:, tile(g)] = acc[g] + pv[g]
                if not diag:
                    c_ref[:, tile(g)] = jnp.where(col == j, c[g], cm[g])
                run_ref[g] = c[g] + rs[g]

        pair(i, True)

        @pl.loop(0, i)
        def _(jj):
            pair(i - 1 - jj, False)

        if n:
            pl.when(i == nblk - 1)(finish)

    blk = pl.BlockSpec((Q_BLOCK, H * L), lambda i: (i, 0))
    full = pl.BlockSpec((T, H * L), lambda i: (0, 1))
    res = _pcall(
        body, name="sb_fwd", grid=(nblk,), in_specs=[blk, full] + ride_in_specs,
        out_specs=[blk, blk] + ride_out_specs,
        out_shape=[jax.ShapeDtypeStruct((T, H * L), F32), jax.ShapeDtypeStruct((T, H * L), F32)] + ride_out,
        scratch_shapes=[pltpu.VMEM((H, Q_BLOCK, 1), F32)] + ride_sems,
        compiler_params=_params(("arbitrary",)),
    )(proj, proj, *ride)
    return res[0], res[1], list(res[2:])


def _sb_bwd(proj, carry, do, ride=()):
    T = proj.shape[0]
    H, d, L = SB_HEADS, HEAD_DIM, 2 * HEAD_DIM
    nblk = T // Q_BLOCK
    scale = d ** -0.5
    n = len(ride)
    ride_out = [jax.ShapeDtypeStruct(a.shape, a.dtype) for a in ride]
    ride_in_specs, ride_out_specs, ride_sems = _ride_specs(ride, ride_out, N_CHIP - 1)
    R = range(H)
    tile = lambda g: slice(g * L, (g + 1) * L)

    def body(*refs):
        q_ref, kv_ref, c_ref, do_ref = refs[:4]
        rin = refs[4:4 + n]
        dq_ref, dkv_ref = refs[4 + n:6 + n]
        rout = refs[6 + n:6 + 2 * n]
        run_ref = refs[6 + 2 * n]
        i = pl.program_id(0)
        if n:
            start, finish = _chips_protocol(rin, rout, *refs[7 + 2 * n:])
            pl.when(i == 0)(start)

        @pl.when(i == 0)
        def _():
            dkv_ref[...] = jnp.zeros_like(dkv_ref)

        causal, after, before, col = _sb_consts()
        qs = [q_ref[:, tile(g)] * scale for g in R]
        dov = [do_ref[:, tile(g)] for g in R]
        qdo = [jnp.concatenate([qs[g], dov[g]], axis=0) for g in R]
        dq_ref[...] = jnp.zeros_like(dq_ref)
        run_ref[...] = jnp.zeros_like(run_ref)

        def pair(j, diag):
            rows = pl.ds(pl.multiple_of(j * Q_BLOCK, Q_BLOCK), Q_BLOCK)
            kvj = [kv_ref[rows, tile(g)] for g in R]
            gsum = [run_ref[g] for g in R]
            dq0 = [dq_ref[:, tile(g)] for g in R]
            dkv0 = [dkv_ref[rows, tile(g)] for g in R]
            cm = None if diag else [c_ref[:, tile(g)] for g in R]
            z = [_dot(qs[g], kvj[g], NT) for g in R]
            sp = [_softplus(z[g]) for g in R]
            lk = [jnp.where(causal, -sp[g], 0.0) if diag else -sp[g] for g in R]
            ls = [z[g] - sp[g] for g in R]
            logw = [ls[g] + _dot2m(lk[g], after) for g in R]
            if not diag:
                logw = [logw[g] + jnp.sum(jnp.where(col == j, cm[g], 0.0), axis=1, keepdims=True) for g in R]
            e = [jnp.exp(logw[g]) for g in R]
            w = [jnp.where(causal, e[g], 0.0) if diag else e[g] for g in R]
            gw = [_dot(dov[g], kvj[g], NT) * w[g] for g in R]
            gpre = [gsum[g] + _dot(gw[g].astype(BF16), before) for g in R]
            sig = [jnp.exp(ls[g]) for g in R]
            dz = [gw[g] * (1.0 - sig[g]) - sig[g] * gpre[g] for g in R]
            if diag:
                dz = [jnp.where(causal, dz[g], 0.0) for g in R]
            dzb = [dz[g].astype(BF16) for g in R]
            dq1 = [_dot(dzb[g], kvj[g]) for g in R]
            dkv1 = [_dot(jnp.concatenate([dzb[g], w[g].astype(BF16)], axis=0), qdo[g], TN) for g in R]
            gs1 = [jnp.sum(gw[g], axis=1, keepdims=True) for g in R]
            for g in R:
                dq_ref[:, tile(g)] = dq0[g] + dq1[g]
                dkv_ref[rows, tile(g)] = dkv0[g] + dkv1[g]
                run_ref[g] = gsum[g] + gs1[g]

        @pl.loop(0, i)
        def _(j):
            pair(j, False)

        pair(i, True)
        dq_ref[...] = dq_ref[...] * scale
        if n:
            pl.when(i == nblk - 1)(finish)

    blk = pl.BlockSpec((Q_BLOCK, H * L), lambda i: (i, 0))
    once = pl.Buffered(1)
    sh = jax.ShapeDtypeStruct((T, H * L), F32)
    res = _pcall(
        body, name="sb_bwd", grid=(nblk,),
        in_specs=[blk, pl.BlockSpec((T, H * L), lambda i: (0, 1), pipeline_mode=once), blk, blk] + ride_in_specs,
        out_specs=[blk, pl.BlockSpec((T, H * L), lambda i: (0, 0), pipeline_mode=once)] + ride_out_specs,
        out_shape=[sh, sh] + ride_out,
        scratch_shapes=[pltpu.VMEM((H, Q_BLOCK, 1), F32)] + ride_sems,
        compiler_params=_params(("arbitrary",)),
    )(proj, proj, carry, do, *ride)
    return res[0], res[1], list(res[2:])


def _swa_common(q_ref, kvp_ref, kvc_ref, qg_ref, kg_ref, sk_ref, sl_ref, n):
    W, d, G = WINDOW, HEAD_DIM, SWA_GROUP
    scale = d ** -0.5
    row = _iota2((W, 2 * W), 0)
    col = _iota2((W, 2 * W), 1)
    dist = row + W - col
    valid = (dist >= 0) & (dist < W) & ((n > 0) | (col >= W))
    distf = dist.astype(F32)
    kvcat = jnp.concatenate([kvp_ref[...], kvc_ref[...]], axis=0)
    KH, QH = range(SWA_KV_HEADS), range(SWA_HEADS)
    kraw = [kvcat[:, hk * d:(hk + 1) * d] for hk in KH]
    vcat = [kvcat[:, SWA_KVW + hk * d:SWA_KVW + (hk + 1) * d].astype(BF16) for hk in KH]
    rk = [lax.rsqrt(jnp.mean(kraw[hk] * kraw[hk], axis=-1, keepdims=True) + EPS) for hk in KH]
    kh = [kraw[hk] * rk[hk] for hk in KH]
    kn = [(kh[hk] * kg_ref[...]).astype(BF16) for hk in KH]
    qraw = [q_ref[:, h * d:(h + 1) * d] for h in QH]
    rq = [lax.rsqrt(jnp.mean(qraw[h] * qraw[h], axis=-1, keepdims=True) + EPS) for h in QH]
    qh = [qraw[h] * rq[h] for h in QH]
    qn = [(qh[h] * qg_ref[...]).astype(BF16) for h in QH]
    sink = [sk_ref[h:h + 1, :1] for h in QH]
    s = [jnp.where(valid, _dot(qn[h], kn[h // G], NT) * scale - sl_ref[h:h + 1, :1] * distf, -1e30) for h in QH]
    m = [jnp.maximum(jnp.max(s[h], axis=1, keepdims=True), sink[h]) for h in QH]
    p = [jnp.where(valid, jnp.exp(s[h] - m[h]), 0.0) for h in QH]
    esink = [jnp.exp(sink[h] - m[h]) for h in QH]
    den = [jnp.sum(p[h], axis=1, keepdims=True) + esink[h] for h in QH]
    prob = [p[h] / den[h] for h in QH]
    return vcat, rk, kh, kn, rq, qh, qn, esink, den, prob


def _swa_specs(T):
    W = WINDOW
    q = pl.BlockSpec((W, SWA_QW), lambda n: (n, 0))
    prev = pl.BlockSpec((W, 2 * SWA_KVW), lambda n: (jnp.maximum(n - 1, 0), SWA_QW // (2 * SWA_KVW)))
    cur = pl.BlockSpec((W, 2 * SWA_KVW), lambda n: (n, SWA_QW // (2 * SWA_KVW)))
    gain = pl.BlockSpec((1, HEAD_DIM), lambda n: (0, 0))
    perhead = pl.BlockSpec((SWA_HEADS, LANE), lambda n: (0, 0))
    return q, prev, cur, gain, perhead


def _swa_fwd(proj, qg, kg, sinks, slopes):
    T = proj.shape[0]
    W, d, G = WINDOW, HEAD_DIM, SWA_GROUP

    def body(q_ref, kvp_ref, kvc_ref, qg_ref, kg_ref, sk_ref, sl_ref, o_ref):
        vcat, _, _, _, _, _, _, _, _, prob = _swa_common(q_ref, kvp_ref, kvc_ref, qg_ref, kg_ref, sk_ref, sl_ref,
                                                         pl.program_id(0))
        outs = [_bdot(prob[h], vcat[h // G]) for h in range(SWA_HEADS)]
        o_ref[...] = jnp.concatenate(outs, axis=1).astype(BF16)

    q, prev, cur, gain, perhead = _swa_specs(T)
    return _pcall(
        body, name="swa_fwd", grid=(T // W,), in_specs=[q, prev, cur, gain, gain, perhead, perhead], out_specs=q,
        out_shape=jax.ShapeDtypeStruct((T, SWA_QW), BF16), compiler_params=_params(("parallel",)),
    )(proj, proj, proj, qg, kg, sinks, slopes)


def _swa_bwd(proj, qg, kg, sinks, slopes, do):
    T = proj.shape[0]
    W, d, G = WINDOW, HEAD_DIM, SWA_GROUP
    scale = d ** -0.5
    KH, QH = range(SWA_KV_HEADS), range(SWA_HEADS)

    def body(q_ref, kvp_ref, kvc_ref, qg_ref, kg_ref, sk_ref, sl_ref, do_ref,
             dq_ref, dkv_ref, dqg_ref, dkg_ref, dsk_ref):
        n = pl.program_id(0)

        @pl.when(n == 0)
        def _():
            dqg_ref[...] = jnp.zeros_like(dqg_ref)
            dkg_ref[...] = jnp.zeros_like(dkg_ref)
            dsk_ref[...] = jnp.zeros_like(dsk_ref)
            dkv_ref[...] = jnp.zeros_like(dkv_ref)

        vcat, rk, kh, kn, rq, qh, qn, esink, den, prob = _swa_common(q_ref, kvp_ref, kvc_ref, qg_ref, kg_ref,
                                                                     sk_ref, sl_ref, n)
        dov = [do_ref[:, h * d:(h + 1) * d].astype(BF16) for h in QH]
        dp = [_dot(dov[h], vcat[h // G], NT) for h in QH]
        dd = [jnp.sum(prob[h] * dp[h], axis=1, keepdims=True) for h in QH]
        dsb = [(prob[h] * (dp[h] - dd[h]) * scale).astype(BF16) for h in QH]
        dsink = [-jnp.sum((esink[h] / den[h]) * dd[h], axis=0, keepdims=True) for h in QH]
        dqn = [_dot(dsb[h], kn[h // G]) for h in QH]
        dkn_h = [_dot(dsb[h], qn[h], TN) for h in QH]
        dv_h = [_dot(prob[h].astype(BF16), dov[h], TN) for h in QH]
        dqh = [dqn[h] * qg_ref[...] for h in QH]
        dq = [rq[h] * (dqh[h] - qh[h] * jnp.mean(dqh[h] * qh[h], axis=-1, keepdims=True)) for h in QH]
        dkn = [sum(dkn_h[hk * G + g] for g in range(G)) for hk in KH]
        dvc = [sum(dv_h[hk * G + g] for g in range(G)) for hk in KH]
        dkh = [dkn[hk] * kg_ref[...] for hk in KH]
        dkraw = [rk[hk] * (dkh[hk] - kh[hk] * jnp.mean(dkh[hk] * kh[hk], axis=-1, keepdims=True)) for hk in KH]
        dq_ref[...] = jnp.concatenate(dq, axis=1)
        dqg_ref[...] += sum(jnp.sum(dqn[h] * qh[h], axis=0, keepdims=True) for h in QH)
        dkg_ref[...] += sum(jnp.sum(dkn[hk] * kh[hk], axis=0, keepdims=True) for hk in KH)
        rowh = _iota2((SWA_HEADS, LANE), 0)
        dsk_ref[...] += sum(jnp.where(rowh == h, dsink[h], 0.0) for h in QH)
        upd = jnp.concatenate(dkraw + dvc, axis=1)
        offp = pl.multiple_of(jnp.maximum(n - 1, 0) * W, W)
        offc = pl.multiple_of(n * W, W)
        dkv_ref[pl.ds(offp, W), :] += upd[:W]
        dkv_ref[pl.ds(offc, W), :] += upd[W:]

    q, prev, cur, gain, perhead = _swa_specs(T)
    kvfull = pl.BlockSpec((T, 2 * SWA_KVW), lambda n: (0, 0))
    gs = jax.ShapeDtypeStruct((1, d), F32)
    return _pcall(
        body, name="swa_bwd", grid=(T // W,), in_specs=[q, prev, cur, gain, gain, perhead, perhead, q],
        out_specs=[q, kvfull, gain, gain, perhead],
        out_shape=[jax.ShapeDtypeStruct((T, SWA_QW), F32), jax.ShapeDtypeStruct((T, 2 * SWA_KVW), F32), gs, gs,
                   jax.ShapeDtypeStruct((SWA_HEADS, LANE), F32)],
        compiler_params=_params(("arbitrary",)),
    )(proj, proj, proj, qg, kg, sinks, slopes, do)


def _alibi():
    s = [2.0 ** (-8.0 * (i + 1) / SWA_HEADS) for i in range(SWA_HEADS)]
    return jnp.broadcast_to(jnp.asarray(s, F32)[:, None], (SWA_HEADS, LANE))


def _head_tiles(lo, hi):
    shp = lo.shape[:-1]
    return jnp.concatenate([lo.reshape(shp + (SB_HEADS, HEAD_DIM)), hi.reshape(shp + (SB_HEADS, HEAD_DIM))],
                           axis=-1).reshape(shp + (SB_LANES,))


def _tile_halves(x):
    shp = x.shape[:-1]
    t = x.reshape(shp + (SB_HEADS, 2, HEAD_DIM))
    return t[..., 0, :].reshape(shp + (SB_W,)), t[..., 1, :].reshape(shp + (SB_W,))


def _att_in_weights(w_in):
    sq, sk, sv = w_in[:, :SB_W], w_in[:, SB_W:2 * SB_W], w_in[:, 2 * SB_W:3 * SB_W]
    return jnp.concatenate([_head_tiles(sq, jnp.zeros_like(sq)), _head_tiles(sk, sv)], axis=1), w_in[:, 3 * SB_W:]


def _att_out_weights(w_out):
    wo = w_out[:SB_W]
    return _head_tiles(jnp.zeros_like(wo).T, wo.T).T, w_out[SB_W:]


def _att_fwd(h, g, w_in, w_out_of, q_gain, k_gain, sinks, ride=()):
    hn = _rms_fwd(h, g, "att_norm")
    w_sb, w_swa = _att_in_weights(w_in)
    proj_sb = _mm(hn, w_sb, "nn", out_dtype=BF16, name="att_in_sb")
    proj_swa = _mm(hn, w_swa, "nn", name="att_in_swa")
    a_out, carry, gathered = _sb_fwd(proj_sb, ride)
    w_out = w_out_of(gathered)
    wo_sb, wo_swa = _att_out_weights(w_out)
    sk128 = jnp.broadcast_to(sinks.reshape(SWA_HEADS, 1), (SWA_HEADS, LANE))
    qg, kg = q_gain.reshape(1, HEAD_DIM), k_gain.reshape(1, HEAD_DIM)
    b_out = _swa_fwd(proj_swa, qg, kg, sk128, _alibi())
    h2 = _mm(a_out, wo_sb, "nn", res=h, a2=b_out, b2=wo_swa, name="att_out")
    return h2, (h, hn, proj_sb, proj_swa, carry, a_out, b_out, sk128, qg, kg), gathered


def _att_bwd(dh2, saved, g, w_in, w_out, ride=()):
    h, hn, proj_sb, proj_swa, carry, a_out, b_out, sk128, qg, kg = saved
    w_sb, w_swa = _att_in_weights(w_in)
    wo_sb, wo_swa = _att_out_weights(w_out)
    da = _mm(dh2, wo_sb, "nt", out_dtype=BF16, name="att_do_sb")
    db = _mm(dh2, wo_swa, "nt", name="att_do_swa")
    dwo_sb = _mm(a_out, dh2, "tn", out_dtype=BF16, name="att_dwout_sb")
    dwo_swa = _mm(b_out, dh2, "tn", out_dtype=BF16, name="att_dwout_swa")
    dw_out = jnp.concatenate([_tile_halves(dwo_sb.T)[1].T, dwo_swa], axis=0)
    dq, dkv, rode = _sb_bwd(proj_sb, carry, da, ride)
    dbq, dbkv, dqg, dkg, dsink = _swa_bwd(proj_swa, qg, kg, sk128, _alibi(), db)
    dproj = jnp.concatenate([dq.astype(BF16), dkv.astype(BF16), dbq.astype(BF16), dbkv.astype(BF16)], axis=1)
    w_all = jnp.concatenate([w_sb, w_swa], axis=1)
    dw_all = _mm(hn, dproj, "tn", out_dtype=BF16, name="att_dwin")
    dhn = _mm(dproj, w_all, "nt", name="att_dhn")
    dsq, _ = _tile_halves(dw_all[:, :SB_LANES])
    dsk, dsv = _tile_halves(dw_all[:, SB_LANES:2 * SB_LANES])
    dw_in = jnp.concatenate([dsq, dsk, dsv, dw_all[:, 2 * SB_LANES:]], axis=1)
    dh, dg = _rms_bwd(dhn, h, g, dh2, "att_dnorm")
    return dh, dg, dw_in, dw_out, dqg.reshape(HEAD_DIM), dkg.reshape(HEAD_DIM), dsink[:, 0], rode


CONV_ROWS = 1024
CONV_COLS = 512
HALO = 8


def _shifted(xcat, s, tm):
    if s == 0:
        return xcat[HALO:HALO + tm]
    return pltpu.roll(xcat, s, 0)[HALO:HALO + tm]


def _conv_pre(x_ref, halo_ref, w_ref, i, tm):
    xc = x_ref[...]
    halo = jnp.where(i > 0, halo_ref[...], 0.0)
    xcat = jnp.concatenate([halo, xc], axis=0)
    w = w_ref[...]
    y = w[GDN_CONV - 1:GDN_CONV] * xc
    for kk in range(GDN_CONV - 1):
        y = y + w[kk:kk + 1] * _shifted(xcat, GDN_CONV - 1 - kk, tm)
    return xcat, y


def _l2_heads(s, qscale_of):
    outs, rs = [], []
    for hh in range(s.shape[1] // GDN_HEAD_DIM):
        sh = s[:, hh * GDN_HEAD_DIM:(hh + 1) * GDN_HEAD_DIM]
        r = lax.rsqrt(jnp.sum(sh * sh, axis=-1, keepdims=True) + EPS)
        outs.append(sh * r)
        rs.append(r)
    return outs, rs


def _conv_specs(T, col0, tm, tc):
    cur = pl.BlockSpec((tm, tc), lambda j, i: (i, j + col0 // tc))
    halo = pl.BlockSpec((HALO, tc), lambda j, i: (jnp.maximum(i * (tm // HALO) - 1, 0), j + col0 // tc))
    wsp = pl.BlockSpec((GDN_CONV, tc), lambda j, i: (0, j + col0 // tc))
    out = pl.BlockSpec((tm, tc), lambda j, i: (i, j))
    return cur, halo, wsp, out


def _conv_fwd(proj, conv_w, col0, width, norm, name):
    T = proj.shape[0]
    tm, tc = _pick(T, CONV_ROWS), CONV_COLS
    cur, halo, wsp, out = _conv_specs(T, col0, tm, tc)
    n_q_tiles = (width // 2) // tc

    def body(x_ref, halo_ref, w_ref, o_ref):
        j, i = pl.program_id(0), pl.program_id(1)
        _, y = _conv_pre(x_ref, halo_ref, w_ref, i, tm)
        s = y * _sigmoid(y)
        if norm:
            outs, _ = _l2_heads(s, None)
            qs = jnp.where(j < n_q_tiles, GDN_HEAD_DIM ** -0.5, 1.0)
            o_ref[...] = jnp.concatenate(outs, axis=1) * qs
        else:
            o_ref[...] = s

    return _pcall(body, name=name, grid=(width // tc, T // tm), in_specs=[cur, halo, wsp], out_specs=out,
                  out_shape=jax.ShapeDtypeStruct((T, width), F32),
                  compiler_params=_params(("parallel", "parallel")))(proj, proj, conv_w)


def _conv_bwd_pre(proj, conv_w, dout, col0, width, norm, name):
    T = proj.shape[0]
    tm, tc = _pick(T, CONV_ROWS), CONV_COLS
    cur, halo, wsp, out = _conv_specs(T, col0, tm, tc)
    n_q_tiles = (width // 2) // tc

    def body(x_ref, halo_ref, w_ref, d_ref, dy_ref, dw_ref):
        j, i = pl.program_id(0), pl.program_id(1)
        xcat, y = _conv_pre(x_ref, halo_ref, w_ref, i, tm)
        sg = _sigmoid(y)
        s = y * sg
        d = d_ref[...]
        if norm:
            qs = jnp.where(j < n_q_tiles, GDN_HEAD_DIM ** -0.5, 1.0)
            d = d * qs
            outs, rs = _l2_heads(s, None)
            parts = []
            for hh, (nh, r) in enumerate(zip(outs, rs)):
                dh = d[:, hh * GDN_HEAD_DIM:(hh + 1) * GDN_HEAD_DIM]
                parts.append(r * (dh - nh * jnp.sum(dh * nh, axis=-1, keepdims=True)))
            ds = jnp.concatenate(parts, axis=1)
        else:
            ds = d
        dy = ds * sg * (1.0 + y * (1.0 - sg))
        dy_ref[...] = dy
        rows = [jnp.sum(dy * _shifted(xcat, GDN_CONV - 1 - kk, tm), axis=0, keepdims=True) for kk in range(GDN_CONV)]
        part = jnp.concatenate(rows, axis=0)

        @pl.when(i == 0)
        def _():
            dw_ref[...] = part

        @pl.when(i > 0)
        def _():
            dw_ref[...] += part

    wout = pl.BlockSpec((GDN_CONV, tc), lambda j, i: (0, j))
    return _pcall(body, name=name, grid=(width // tc, T // tm), in_specs=[cur, halo, wsp, out], out_specs=[out, wout],
                  out_shape=[jax.ShapeDtypeStruct((T, width), F32), jax.ShapeDtypeStruct((GDN_CONV, width), F32)],
                  compiler_params=_params(("parallel", "arbitrary")))(proj, proj, conv_w, dout)


def _conv_bwd_in(dy, conv_w, name):
    T, C = dy.shape
    tm, tc = _pick(T, CONV_ROWS), CONV_COLS
    nrow = T // tm

    def body(d_ref, nxt_ref, w_ref, dx_ref):
        i = pl.program_id(0)
        dc = d_ref[...]
        nxt = jnp.where(i < nrow - 1, nxt_ref[...], 0.0)
        dcat = jnp.concatenate([dc, nxt], axis=0)
        w = w_ref[...]
        dx = w[GDN_CONV - 1:GDN_CONV] * dc
        for kk in range(GDN_CONV - 1):
            s = GDN_CONV - 1 - kk
            dx = dx + w[kk:kk + 1] * pltpu.roll(dcat, tm + HALO - s, 0)[:tm]
        dx_ref[...] = dx.astype(BF16)

    cur = pl.BlockSpec((tm, tc), lambda i, j: (i, j))
    nxt = pl.BlockSpec((HALO, tc), lambda i, j: (jnp.minimum((i + 1) * (tm // HALO), T // HALO - 1), j))
    wsp = pl.BlockSpec((GDN_CONV, tc), lambda i, j: (0, j))
    return _pcall(body, name=name, grid=(nrow, C // tc), in_specs=[cur, nxt, wsp], out_specs=cur,
                  out_shape=jax.ShapeDtypeStruct((T, C), BF16),
                  compiler_params=_params(("parallel", "parallel")))(dy, dy, conv_w)


GATE_ROWS = 512


def _chunk_mask(n, lower):
    row = _iota2((n, n), 0)
    col = _iota2((n, n), 1)
    same = (row // GDN_CHUNK) == (col // GDN_CHUNK)
    tri = (row >= col) if lower else (row <= col)
    return (same & tri).astype(BF16)


def _gates_fwd(proj, a_log, dt_bias):
    T = proj.shape[0]
    tm = _pick(T, GATE_ROWS)
    c0 = (GDN_CONV_W + GDN_VW) // LANE

    def body(bl_ref, a_ref, alog_ref, dt_ref, beta_ref, g_ref, gc_ref):
        beta_ref[...] = _sigmoid(bl_ref[...])
        g = -jnp.exp(alog_ref[...]) * _softplus(a_ref[...] + dt_ref[...])
        g_ref[...] = g
        gc_ref[...] = _mdot2(_chunk_mask(tm, True), g)

    blk = lambda c: pl.BlockSpec((tm, LANE), lambda i: (i, c))
    vec = pl.BlockSpec((1, LANE), lambda i: (0, 0))
    sh = jax.ShapeDtypeStruct((T, LANE), F32)
    return _pcall(body, name="gdn_gates", grid=(T // tm,), in_specs=[blk(c0), blk(c0 + 1), vec, vec],
                  out_specs=[blk(0), blk(0), blk(0)], out_shape=[sh, sh, sh],
                  compiler_params=_params(("parallel",)))(proj, proj, a_log, dt_bias)


def _gates_bwd(proj, a_log, dt_bias, beta, g, dbeta, dgc):
    T = proj.shape[0]
    tm = _pick(T, GATE_ROWS)
    c0 = (GDN_CONV_W + GDN_VW) // LANE

    def heads_in_lanes(ref):
        lane = _iota2((tm, LANE), 1)
        out = jnp.where(lane < GDN_GROUP, ref[0], 0.0)
        for grp in range(1, GDN_V_HEADS // GDN_GROUP):
            out = out + jnp.where(lane // GDN_GROUP == grp, pltpu.roll(ref[grp], grp * GDN_GROUP, 1), 0.0)
        return out

    def body(a_ref, alog_ref, dt_ref, beta_ref, g_ref, dbeta_ref, dgc_ref, dbl_ref, da_ref, dalog_ref, ddt_ref):
        dg = _mdot2(_chunk_mask(tm, False), heads_in_lanes(dgc_ref))
        b = beta_ref[...]
        dbl_ref[...] = (heads_in_lanes(dbeta_ref) * b * (1.0 - b)).astype(BF16)
        da = dg * (-jnp.exp(alog_ref[...])) * _sigmoid(a_ref[...] + dt_ref[...])
        da_ref[...] = da.astype(BF16)
        p1 = jnp.sum(dg * g_ref[...], axis=0, keepdims=True)
        p2 = jnp.sum(da, axis=0, keepdims=True)

        @pl.when(pl.program_id(0) == 0)
        def _():
            dalog_ref[...] = p1
            ddt_ref[...] = p2

        @pl.when(pl.program_id(0) > 0)
        def _():
            dalog_ref[...] += p1
            ddt_ref[...] += p2

    blk = lambda c: pl.BlockSpec((tm, LANE), lambda i: (i, c))
    vec = pl.BlockSpec((1, LANE), lambda i: (0, 0))
    grp = pl.BlockSpec((GDN_V_HEADS // GDN_GROUP, tm, LANE), lambda i: (0, i, 0))
    shb = jax.ShapeDtypeStruct((T, LANE), BF16)
    shv = jax.ShapeDtypeStruct((1, LANE), F32)
    return _pcall(body, name="gdn_dgates", grid=(T // tm,),
                  in_specs=[blk(c0 + 1), vec, vec, blk(0), blk(0), grp, grp],
                  out_specs=[blk(0), blk(0), vec, vec], out_shape=[shb, shb, shv, shv],
                  compiler_params=_params(("arbitrary",)))(proj, a_log, dt_bias, beta, g, dbeta, dgc)


def _inv_unit_lower(Ls):
    C = Ls[0].shape[0]
    row = _iota2((C, C), 0)
    col = _iota2((C, C), 1)
    blk16 = (row // 16) == (col // 16)
    blk32 = (row // 32) == (col // 32)
    eye = (row == col).astype(F32)
    xs = [-jnp.where(blk16, L, 0.0) for L in Ls]
    inv = [eye + x for x in xs]
    for _ in range(3):
        xs = [_dot3(x, x) for x in xs]
        inv = [a + _dot3(a, x) for a, x in zip(inv, xs)]
    for mask in (blk32 & ~blk16, ~blk32):
        t = [_dot3(a, jnp.where(mask, L, 0.0)) for a, L in zip(inv, Ls)]
        inv = [a - _dot3(ti, a) for a, ti in zip(inv, t)]
    return inv


GDN_GROUP = 4
GDN_PREP_CHUNKS = 4


def _gdn_specs(T):
    C, D, E = GDN_CHUNK, GDN_HEAD_DIM, GDN_GROUP
    n = T // C
    qk = pl.BlockSpec((C, (E // 2) * D), lambda h, i: (i, h))
    vE = pl.BlockSpec((C, E * D), lambda h, i: (i, h))
    colv = pl.BlockSpec((C, LANE), lambda h, i: (i, 0))
    colo = pl.BlockSpec((None, C, LANE), lambda h, i: (h, i, 0))
    rowv = pl.BlockSpec((E, None, 1, C), lambda h, i: (h, i, 0, 0))
    st = pl.BlockSpec((E, None, D, D), lambda h, i: (h, i, 0, 0))
    am = pl.BlockSpec((E, None, C, C), lambda h, i: (h, i, 0, 0))
    return n, qk, vE, colv, colo, rowv, st, am


def _lane_col(blk, lane):
    return jnp.sum(jnp.where(_iota2(blk.shape, 1) == lane, blk, 0.0), axis=1, keepdims=True)


def _gdn_decay(gcol, grow):
    C = GDN_CHUNK
    row = _iota2((C, C), 0)
    col = _iota2((C, C), 1)
    incl = row >= col
    dm = jnp.where(incl, jnp.exp(jnp.where(incl, gcol - grow, 0.0)), 0.0)
    glast = grow[:, C - 1:C]
    return dm, jnp.exp(gcol), jnp.exp(glast), jnp.exp(glast - gcol), row > col, incl


def _gdn_prep(k, beta, gcol, grow):
    T = k.shape[0]
    C, D, B = GDN_CHUNK, GDN_HEAD_DIM, GDN_PREP_CHUNKS
    n = T // C

    def body(k_ref, b_ref, gc_ref, gr_ref, a_ref):
        idx = [(e, cb) for e in range(2) for cb in range(B)]
        kc = {cb: k_ref[cb * C:(cb + 1) * C, :] for cb in range(B)}
        lm = []
        head0 = 2 * pl.program_id(0)
        for e, cb in idx:
            beta = _lane_col(b_ref[cb * C:(cb + 1) * C, :], head0 + e)
            dm, _, _, _, strict, _ = _gdn_decay(_lane_col(gc_ref[cb * C:(cb + 1) * C, :], head0 + e), gr_ref[e, cb])
            lm.append(jnp.where(strict, _bdot(kc[cb] * beta, kc[cb], NT) * dm, 0.0))
        inv = _inv_unit_lower(lm)
        for (e, cb), a in zip(idx, inv):
            a_ref[e, cb] = a

    return _pcall(
        body, name="gdn_prep", grid=(GDN_K_HEADS, n // B),
        in_specs=[pl.BlockSpec((B * C, D), lambda h, i: (i, h)), pl.BlockSpec((B * C, LANE), lambda h, i: (i, 0)),
                  pl.BlockSpec((B * C, LANE), lambda h, i: (i, 0)), pl.BlockSpec((2, B, 1, C), lambda h, i: (h, i, 0, 0))],
        out_specs=pl.BlockSpec((2, B, C, C), lambda h, i: (h, i, 0, 0)),
        out_shape=jax.ShapeDtypeStruct((GDN_V_HEADS, n, C, C), F32),
        compiler_params=_params(("parallel", "parallel")),
    )(k, beta, gcol, grow)


def _gdn_fwd(q, k, v, beta, gcol, grow, amat):
    T = q.shape[0]
    C, D, E = GDN_CHUNK, GDN_HEAD_DIM, GDN_GROUP
    n, qk, vE, colv, colo, rowv, st, am = _gdn_specs(T)
    R = range(E)

    def body(q_ref, k_ref, v_ref, b_ref, gc_ref, gr_ref, a_ref, o_ref, s_ref, vn_ref, state):
        @pl.when(pl.program_id(1) == 0)
        def _():
            state[...] = jnp.zeros_like(state)

        qv = [q_ref[:, (e // 2) * D:(e // 2 + 1) * D] for e in R]
        kv = [k_ref[:, (e // 2) * D:(e // 2 + 1) * D] for e in R]
        vv = [v_ref[:, e * D:(e + 1) * D] for e in R]
        head0 = E * pl.program_id(0)
        beta = [_lane_col(b_ref[...], head0 + e) for e in R]
        a = [a_ref[e] for e in R]
        s = [state[e] for e in R]
        dec = [_gdn_decay(_lane_col(gc_ref[...], head0 + e), gr_ref[e]) for e in R]
        pm = [_bdot(qv[e], kv[e], NT) * dec[e][0] for e in R]
        r = [beta[e] * (vv[e] - _bdot(kv[e] * dec[e][1], s[e])) for e in R]
        vn = [_dot3(a[e], r[e]) for e in R]
        o = [_bdot(qv[e] * dec[e][1], s[e]) + _bdot(pm[e], vn[e]) for e in R]
        s2 = [dec[e][2] * s[e] + _bdot(kv[e] * dec[e][3], vn[e], TN) for e in R]
        for e in R:
            s_ref[e] = s[e]
            vn_ref[:, e * D:(e + 1) * D] = vn[e]
            o_ref[:, e * D:(e + 1) * D] = o[e]
            state[e] = s2[e]

    shv = jax.ShapeDtypeStruct((T, GDN_V_HEADS * D), F32)
    return _pcall(
        body, name="gdn_fwd", grid=(GDN_V_HEADS // E, n), in_specs=[qk, qk, vE, colv, colv, rowv, am],
        out_specs=[vE, st, vE],
        out_shape=[shv, jax.ShapeDtypeStruct((GDN_V_HEADS, n, D, D), F32), shv],
        scratch_shapes=[pltpu.VMEM((E, D, D), F32)],
        compiler_params=_params(("parallel", "arbitrary")),
    )(q, k, v, beta, gcol, grow, amat)


def _gdn_bwd(q, k, v, beta, gcol, grow, states, amat, vnew, do):
    T = q.shape[0]
    C, D, E = GDN_CHUNK, GDN_HEAD_DIM, GDN_GROUP
    n, qk, vE, colv, colo, rowv, st, am = _gdn_specs(T)
    rev = lambda spec: pl.BlockSpec(spec.block_shape, (lambda f: (lambda h, i: f(h, n - 1 - i)))(spec.index_map))
    qk, vE, colv, colo, rowv, st, am = (rev(s) for s in (qk, vE, colv, colo, rowv, st, am))
    R = range(E)

    def body(q_ref, k_ref, v_ref, b_ref, gc_ref, gr_ref, s_ref, a_ref, vn_ref, do_ref,
             dq_ref, dk_ref, dv_ref, db_ref, dgc_ref, dstate):
        @pl.when(pl.program_id(1) == 0)
        def _():
            dstate[...] = jnp.zeros_like(dstate)

        M = lambda f: [f(e) for e in R]
        rsum = lambda x: jnp.sum(x, axis=1, keepdims=True)
        qv = M(lambda e: q_ref[:, (e // 2) * D:(e // 2 + 1) * D])
        kv = M(lambda e: k_ref[:, (e // 2) * D:(e // 2 + 1) * D])
        vv = M(lambda e: v_ref[:, e * D:(e + 1) * D])
        vn = M(lambda e: vn_ref[:, e * D:(e + 1) * D])
        dov = M(lambda e: do_ref[:, e * D:(e + 1) * D])
        head0 = E * pl.program_id(0)
        beta = M(lambda e: _lane_col(b_ref[...], head0 + e))
        s = M(lambda e: s_ref[e])
        a = M(lambda e: a_ref[e])
        dsn = M(lambda e: dstate[e])
        dec = M(lambda e: _gdn_decay(_lane_col(gc_ref[...], head0 + e), gr_ref[e]))
        dm, gam, glast, tail = (M(lambda e: dec[e][i]) for i in range(4))
        strict, incl = dec[0][4], dec[0][5]
        kb = M(lambda e: kv[e] * beta[e])
        kd = M(lambda e: kv[e] * gam[e])
        qd = M(lambda e: qv[e] * gam[e])
        kt = M(lambda e: kv[e] * tail[e])
        lmat = M(lambda e: jnp.where(strict, _bdot(kb[e], kv[e], NT) * dm[e], 0.0))
        pmat = M(lambda e: _bdot(qv[e], kv[e], NT) * dm[e])
        xres = M(lambda e: vv[e] - _bdot(kd[e], s[e]))
        dvn = M(lambda e: _bdot(pmat[e], dov[e], TN) + _bdot(kt[e], dsn[e]))
        dqd = M(lambda e: _bdot(dov[e], s[e], NT))
        dp = M(lambda e: jnp.where(incl, _bdot(dov[e], vn[e], NT), 0.0))
        dkt = M(lambda e: _bdot(vn[e], dsn[e], NT))
        dr = M(lambda e: _dot3(a[e], dvn[e], TN))
        drb = M(lambda e: beta[e] * dr[e])
        dkd = M(lambda e: -_bdot(drb[e], s[e], NT))
        ds2 = M(lambda e: _bdot(qd[e], dov[e], TN) + glast[e] * dsn[e] - _bdot(kd[e], drb[e], TN))
        dl = M(lambda e: -jnp.where(strict, _bdot(dr[e], vn[e], NT), 0.0))
        dmm = M(lambda e: dl[e] * dm[e])
        dnn = M(lambda e: dp[e] * dm[e])
        emat = M(lambda e: dl[e] * lmat[e] + dp[e] * pmat[e])
        dkb = M(lambda e: _bdot(dmm[e], kv[e]))
        dk = M(lambda e: beta[e] * dkb[e] + _bdot(dmm[e], kb[e], TN) + _bdot(dnn[e], qv[e], TN)
               + gam[e] * dkd[e] + tail[e] * dkt[e])
        dq = M(lambda e: _bdot(dnn[e], kv[e]) + gam[e] * dqd[e])
        dbeta = M(lambda e: rsum(dr[e] * xres[e]) + rsum(dkb[e] * kv[e]))
        ones = jnp.ones((C, LANE), BF16)
        colsum = M(lambda e: _dot2m(emat[e], ones, TN)[:, :1])
        tails = M(lambda e: rsum(dkt[e] * kt[e]))
        lastrow = _iota2((C, 1), 0) == C - 1
        dlast = M(lambda e: jnp.sum(tails[e], axis=0, keepdims=True)
                  + glast[e] * jnp.sum(rsum(s[e] * dsn[e]), axis=0, keepdims=True))
        dgc = M(lambda e: rsum(emat[e]) - colsum[e] + rsum(dkd[e] * kd[e]) + rsum(dqd[e] * qd[e]) - tails[e]
                + jnp.where(lastrow, dlast[e], 0.0))
        lane = _iota2((C, LANE), 1)
        db_all = jnp.zeros((C, LANE), F32)
        dgc_all = jnp.zeros((C, LANE), F32)
        for e in R:
            dv_ref[:, e * D:(e + 1) * D] = drb[e]
            db_all = jnp.where(lane == e, dbeta[e], db_all)
            dgc_all = jnp.where(lane == e, dgc[e], dgc_all)
            dstate[e] = ds2[e]
        db_ref[...] = db_all
        dgc_ref[...] = dgc_all
        for kh in range(E // 2):
            dq_ref[:, kh * D:(kh + 1) * D] = dq[2 * kh] + dq[2 * kh + 1]
            dk_ref[:, kh * D:(kh + 1) * D] = dk[2 * kh] + dk[2 * kh + 1]

    shq = jax.ShapeDtypeStruct((T, GDN_K_HEADS * D), F32)
    shv = jax.ShapeDtypeStruct((T, GDN_V_HEADS * D), F32)
    shc = jax.ShapeDtypeStruct((GDN_V_HEADS // E, T, LANE), F32)
    return _pcall(
        body, name="gdn_bwd", grid=(GDN_V_HEADS // E, n),
        in_specs=[qk, qk, vE, colv, colv, rowv, st, am, vE, vE],
        out_specs=[qk, qk, vE, colo, colo], out_shape=[shq, shq, shv, shc, shc],
        scratch_shapes=[pltpu.VMEM((E, D, D), F32)],
        compiler_params=_params(("parallel", "arbitrary")),
    )(q, k, v, beta, gcol, grow, states, amat, vnew, do)


def _outgate_fwd(o, proj, gain):
    T = o.shape[0]
    tm, tc = _pick(T, CONV_ROWS), CONV_COLS
    z0 = GDN_CONV_W // tc

    def body(o_ref, z_ref, g_ref, y_ref):
        z = z_ref[...]
        sz = z * _sigmoid(z)
        parts = []
        for hh in range(tc // GDN_HEAD_DIM):
            oh = o_ref[:, hh * GDN_HEAD_DIM:(hh + 1) * GDN_HEAD_DIM]
            r = lax.rsqrt(jnp.mean(oh * oh, axis=-1, keepdims=True) + EPS)
            parts.append(oh * r * g_ref[...])
        y_ref[...] = (jnp.concatenate(parts, axis=1) * sz).astype(BF16)

    blk = pl.BlockSpec((tm, tc), lambda i, j: (i, j))
    return _pcall(body, name="gdn_outgate", grid=(T // tm, GDN_VW // tc),
                  in_specs=[blk, pl.BlockSpec((tm, tc), lambda i, j: (i, j + z0)), pl.BlockSpec((1, GDN_HEAD_DIM), lambda i, j: (0, 0))],
                  out_specs=blk, out_shape=jax.ShapeDtypeStruct((T, GDN_VW), BF16),
                  compiler_params=_params(("parallel", "parallel")))(o, proj, gain)


def _outgate_bwd(dy, o, proj, gain):
    T = o.shape[0]
    tm, tc = _pick(T, CONV_ROWS), CONV_COLS
    z0 = GDN_CONV_W // tc
    nh = tc // GDN_HEAD_DIM

    def body(dy_ref, o_ref, z_ref, g_ref, do_ref, dz_ref, dg_ref):
        z = z_ref[...]
        sg = _sigmoid(z)
        sz = z * sg
        dy = dy_ref[...]
        dgain = jnp.zeros((1, GDN_HEAD_DIM), F32)
        dos, ys = [], []
        for hh in range(nh):
            sl = slice(hh * GDN_HEAD_DIM, (hh + 1) * GDN_HEAD_DIM)
            oh = o_ref[:, sl]
            r = lax.rsqrt(jnp.mean(oh * oh, axis=-1, keepdims=True) + EPS)
            xh = oh * r
            dn = dy[:, sl] * sz[:, sl]
            dgain = dgain + jnp.sum(dn * xh, axis=0, keepdims=True)
            dxh = dn * g_ref[...]
            dos.append(r * (dxh - xh * jnp.mean(dxh * xh, axis=-1, keepdims=True)))
            ys.append(xh * g_ref[...])
        do_ref[...] = jnp.concatenate(dos, axis=1)
        dz_ref[...] = (dy * jnp.concatenate(ys, axis=1) * sg * (1.0 + z * (1.0 - sg))).astype(BF16)
        first = (pl.program_id(0) == 0) & (pl.program_id(1) == 0)

        @pl.when(first)
        def _():
            dg_ref[...] = dgain

        @pl.when(jnp.logical_not(first))
        def _():
            dg_ref[...] += dgain

    blk = pl.BlockSpec((tm, tc), lambda i, j: (i, j))
    vec = pl.BlockSpec((1, GDN_HEAD_DIM), lambda i, j: (0, 0))
    return _pcall(body, name="gdn_doutgate", grid=(T // tm, GDN_VW // tc),
                  in_specs=[blk, blk, pl.BlockSpec((tm, tc), lambda i, j: (i, j + z0)), vec],
                  out_specs=[blk, blk, vec],
                  out_shape=[jax.ShapeDtypeStruct((T, GDN_VW), F32), jax.ShapeDtypeStruct((T, GDN_VW), BF16),
                             jax.ShapeDtypeStruct((1, GDN_HEAD_DIM), F32)],
                  compiler_params=_params(("arbitrary", "arbitrary")))(dy, o, proj, gain)


def _pad_lanes(vec):
    return jnp.pad(vec.reshape(1, -1), ((0, 0), (0, LANE - vec.shape[-1])))


def _head_rows(a):
    T = a.shape[0]
    return a[:, :GDN_V_HEADS].T.reshape(GDN_V_HEADS, T // GDN_CHUNK, 1, GDN_CHUNK)


def _gdn_pad_in(w_in):
    c = GDN_CONV_W + GDN_VW
    z = jnp.zeros(w_in.shape[:-1] + (LANE - GDN_V_HEADS,), w_in.dtype)
    return jnp.concatenate([w_in[..., :c + GDN_V_HEADS], z, w_in[..., c + GDN_V_HEADS:], z], axis=-1)


def _gdn_unpad_in(dw):
    c = GDN_CONV_W + GDN_VW
    return jnp.concatenate([dw[..., :c + GDN_V_HEADS], dw[..., c + LANE:c + LANE + GDN_V_HEADS]], axis=-1)


def _gdn_mixer_fwd(h, g, w_in_pad, conv_w, a_log, dt_bias, out_gain, w_out):
    T = h.shape[0]
    hn = _rms_fwd(h, g, "gdn_norm")
    proj = _mm(hn, w_in_pad, "nn", name="gdn_in")
    qk = _conv_fwd(proj, conv_w, 0, 2 * GDN_KW, True, "gdn_conv_qk")
    vv = _conv_fwd(proj, conv_w, 2 * GDN_KW, GDN_VW, False, "gdn_conv_v")
    alog, dtb = _pad_lanes(a_log), _pad_lanes(dt_bias)
    beta, gl, gc = _gates_fwd(proj, alog, dtb)
    grow = _head_rows(gc)
    qn, kn = qk[:, :GDN_KW], qk[:, GDN_KW:]
    amat = _gdn_prep(kn, beta, gc, grow)
    o, states, vnew = _gdn_fwd(qn, kn, vv, beta, gc, grow, amat)
    gain = out_gain.reshape(1, GDN_HEAD_DIM)
    y = _outgate_fwd(o, proj, gain)
    h2 = _mm(y, w_out, "nn", res=h, name="gdn_out")
    return h2, (h, hn, proj, qn, kn, vv, beta, gl, gc, grow, o, states, amat, vnew, y, alog, dtb, gain)


def _gdn_mixer_bwd(dh2, saved, g, w_in_pad, conv_w, w_out):
    h, hn, proj, qn, kn, vv, beta, gl, gc, grow, o, states, amat, vnew, y, alog, dtb, gain = saved
    T = h.shape[0]
    dy = _mm(dh2, w_out, "nt", name="gdn_dy")
    dw_out = _mm(y, dh2, "tn", out_dtype=BF16, name="gdn_dwout")
    do, dz, dgain = _outgate_bwd(dy, o, proj, gain)
    dq, dk, dv, dbeta, dgc = _gdn_bwd(qn, kn, vv, beta, gc, grow, states, amat, vnew, do)
    dqk = jnp.concatenate([dq, dk], axis=1)
    dy_qk, dcw_qk = _conv_bwd_pre(proj, conv_w, dqk, 0, 2 * GDN_KW, True, "gdn_dconv_qk")
    dy_v, dcw_v = _conv_bwd_pre(proj, conv_w, dv, 2 * GDN_KW, GDN_VW, False, "gdn_dconv_v")
    dx_qk = _conv_bwd_in(dy_qk, conv_w[:, :2 * GDN_KW], "gdn_dconvin_qk")
    dx_v = _conv_bwd_in(dy_v, conv_w[:, 2 * GDN_KW:], "gdn_dconvin_v")
    dbl, da, dalog, ddt = _gates_bwd(proj, alog, dtb, beta, gl, dbeta, dgc)
    dproj = jnp.concatenate([dx_qk, dx_v, dz, dbl, da], axis=1)
    dw_in_pad = _mm(hn, dproj, "tn", out_dtype=BF16, name="gdn_dwin")
    dhn = _mm(dproj, w_in_pad, "nt", name="gdn_dhn")
    dh, dg = _rms_bwd(dhn, h, g, dh2, "gdn_dnorm")
    dconv = jnp.concatenate([dcw_qk, dcw_v], axis=1)
    return (dh, dg, _gdn_unpad_in(dw_in_pad), dconv, dalog[0, :GDN_V_HEADS], ddt[0, :GDN_V_HEADS],
            dgain.reshape(GDN_HEAD_DIM), dw_out)


def _instances(full):
    out = {}
    for n, a in full.items():
        if n.startswith("ffn_"):
            for i in range(2):
                for j in range(2):
                    out[(n, i, j)] = a[i, j]
        elif n in ("mix_norm", "ple_norm", "ple_w_gate", "ple_w_proj"):
            for i in range(2):
                out[(n, i)] = a[i]
        else:
            out[(n,)] = a[0]
    return out


def _stacked(inst):
    out = {}
    for n in dict.fromkeys(k[0] for k in inst):
        if n.startswith("ffn_"):
            out[n] = jnp.stack([jnp.stack([inst[(n, i, j)] for j in range(2)]) for i in range(2)])
        elif n in ("mix_norm", "ple_norm", "ple_w_gate", "ple_w_proj"):
            out[n] = jnp.stack([inst[(n, i)] for i in range(2)])
        else:
            out[n] = inst[(n,)][None]
    return out


def _local_step(x, p, target, w, late_shards=(), late_weights=None, early_grads=None, first_shards=(), first_weights=None):
    w = dict(w)
    ffn = lambda i, j: (w[("ffn_norm", i, j)], w[("ffn_w_gate", i, j)], w[("ffn_w_up", i, j)], w[("ffn_w_down", i, j)])
    h = x
    tape = []
    for i in range(2):
        if i == 0 and first_weights is not None:
            def wd_of(gathered):
                w.update(first_weights(gathered))
                return w[("ffn_w_down", 0, 0)]
            h, s1 = _ffn_fwd(h, w[("ffn_norm", 0, 0)], w[("ffn_w_gate", 0, 0)], w[("ffn_w_up", 0, 0)], None, "ffn0a",
                             first_shards, wd_of)
        else:
            h, s1 = _ffn_fwd(h, *ffn(i, 0), f"ffn{i}a")
        if i == 0:
            def w_out_of(gathered):
                if late_weights is not None:
                    w.update(late_weights(gathered))
                return w[("att_w_out",)]
            h, s2, _ = _att_fwd(h, w[("mix_norm", 0)], w[("att_w_in",)], w_out_of, w[("att_q_norm",)],
                                w[("att_k_norm",)], w[("att_sinks",)], late_shards)
        else:
            gdn_in_pad = _gdn_pad_in(w[("gdn_w_in",)])
            h, s2 = _gdn_mixer_fwd(h, w[("mix_norm", 1)], gdn_in_pad, w[("gdn_conv_w",)], w[("gdn_a_log",)],
                                   w[("gdn_dt_bias",)], w[("gdn_out_norm",)], w[("gdn_w_out",)])
        h, s3 = _ffn_fwd(h, *ffn(i, 1), f"ffn{i}b")
        h, s4 = _ple_fwd(h, p[i], w[("ple_norm", i)], w[("ple_w_gate", i)], w[("ple_w_proj", i)], f"ple{i}")
        tape.append((s1, s2, s3, s4))

    loss, dh = _loss_head(h, target)

    g = {}
    rode = []
    for i in (1, 0):
        s1, s2, s3, s4 = tape[i]
        dh, g[("ple_norm", i)], g[("ple_w_gate", i)], g[("ple_w_proj", i)] = _ple_bwd(
            dh, s4, p[i], w[("ple_norm", i)], w[("ple_w_gate", i)], f"ple{i}")
        dh, g[("ffn_norm", i, 1)], g[("ffn_w_gate", i, 1)], g[("ffn_w_up", i, 1)], g[("ffn_w_down", i, 1)] = _ffn_bwd(
            dh, s3, *ffn(i, 1), f"ffn{i}b")
        if i == 0:
            ride = early_grads(g) if early_grads is not None else ()
            (dh, g[("mix_norm", 0)], g[("att_w_in",)], g[("att_w_out",)], g[("att_q_norm",)], g[("att_k_norm",)],
             g[("att_sinks",)], rode) = _att_bwd(dh, s2, w[("mix_norm", 0)], w[("att_w_in",)], w[("att_w_out",)], ride)
        else:
            (dh, g[("mix_norm", 1)], g[("gdn_w_in",)], g[("gdn_conv_w",)], g[("gdn_a_log",)], g[("gdn_dt_bias",)],
             g[("gdn_out_norm",)], g[("gdn_w_out",)]) = _gdn_mixer_bwd(
                dh, s2, w[("mix_norm", 1)], gdn_in_pad, w[("gdn_conv_w",)], w[("gdn_w_out",)])
        dh, g[("ffn_norm", i, 0)], g[("ffn_w_gate", i, 0)], g[("ffn_w_up", i, 0)], g[("ffn_w_down", i, 0)] = _ffn_bwd(
            dh, s1, *ffn(i, 0), f"ffn{i}a")
    return loss, dh, g, rode


MESH = pl.DeviceIdType.MESH


def _place():
    x, y, c = lax.axis_index("x"), lax.axis_index("y"), lax.axis_index("c")
    others = [((1 - x, y), 2 * (1 - x) + y), ((x, 1 - y), 2 * x + (1 - y)), ((1 - x, 1 - y), 2 * (1 - x) + (1 - y))]
    return x, y, c, 4 * x + 2 * y + c, 2 * x + y, (x, y, 1 - c), others


def _comm_call(body, arrays, out_shape, n_sems, name):
    hbm = pl.BlockSpec(memory_space=pl.ANY)
    n = len(arrays)
    return _pcall(
        body, name=name, in_specs=[hbm] * n, out_specs=[hbm] * len(out_shape), out_shape=out_shape,
        scratch_shapes=[pltpu.SemaphoreType.DMA((n, n_sems)), pltpu.SemaphoreType.DMA((n, n_sems)),
                        pltpu.SemaphoreType.DMA((n, N_CHIP))],
        compiler_params=pltpu.CompilerParams(has_side_effects=True),
    )(*arrays)


def _gather_protocol(ins, outs, send_sems, recv_sems, local_sems):
    n = len(ins)
    x, y, c, me, my_chip, sibling, others = _place()

    def copy(a, k, block, to, src=None):
        dst = outs[a].at[block]
        return pltpu.make_async_remote_copy(
            src_ref=dst if src is None else src, dst_ref=dst, send_sem=send_sems.at[a, k],
            recv_sem=recv_sems.at[a, k], device_id=to, device_id_type=MESH)

    local = [pltpu.make_async_copy(ins[a], outs[a].at[me], local_sems.at[a, 0]) for a in range(n)]
    first = []
    for a in range(n):
        first.append(copy(a, 0, me, sibling, src=ins[a]))
        first += [copy(a, 1 + j, me, (*chip, c), src=ins[a]) for j, (chip, _) in enumerate(others)]

    def start():
        for cp in local + first:
            cp.start()

    def finish():
        passed = []
        for a in range(n):
            for j, (chip, chip_idx) in enumerate(others):
                blk = 2 * chip_idx + c
                copy(a, 1 + j, blk, (x, y, c)).wait_recv()
                fwd = copy(a, 4 + j, blk, sibling)
                fwd.start()
                passed.append(fwd)
        for a in range(n):
            copy(a, 0, 2 * my_chip + (1 - c), (x, y, c)).wait_recv()
            for j, (chip, chip_idx) in enumerate(others):
                copy(a, 4 + j, 2 * chip_idx + (1 - c), (x, y, c)).wait_recv()
        for cp in first + passed:
            cp.wait_send()
        for cp in local:
            cp.wait()

    return start, finish


def _all_gather(arrays):
    n = len(arrays)

    def body(*refs):
        start, finish = _gather_protocol(refs[:n], refs[n:2 * n], *refs[2 * n:])
        start()
        finish()

    out_shape = [jax.ShapeDtypeStruct((N_DEV,) + a.shape, a.dtype) for a in arrays]
    return _comm_call(body, arrays, out_shape, N_DEV - 1, "gather_weights")


def _exchange_sibling(arrays, name):
    n = len(arrays)

    def body(*refs):
        ins, got = refs[:n], refs[n:2 * n]
        send_sems, recv_sems, _ = refs[2 * n:]
        x, y, c, me, my_chip, sibling, others = _place()
        remote = []
        for a in range(n):
            for chip in range(N_CHIP):
                rc = pltpu.make_async_remote_copy(
                    src_ref=ins[a].at[2 * chip + (1 - c)], dst_ref=got[a].at[chip], send_sem=send_sems.at[a, chip],
                    recv_sem=recv_sems.at[a, chip], device_id=sibling, device_id_type=MESH)
                rc.start()
                remote.append(rc)
        for rc in remote:
            rc.wait()

    half = [jax.ShapeDtypeStruct((N_CHIP,) + a.shape[1:], a.dtype) for a in arrays]
    return _comm_call(body, arrays, half, N_CHIP, name)


def _chips_protocol(ins, outs, send_sems, recv_sems, local_sems):
    n = len(ins)
    x, y, c, me, my_chip, sibling, others = _place()
    local = [pltpu.make_async_copy(ins[a].at[my_chip], outs[a].at[my_chip], local_sems.at[a, 0]) for a in range(n)]
    remote = [pltpu.make_async_remote_copy(
        src_ref=ins[a].at[chip_idx], dst_ref=outs[a].at[my_chip], send_sem=send_sems.at[a, j],
        recv_sem=recv_sems.at[a, j], device_id=(*chip, c), device_id_type=MESH)
        for a in range(n) for j, (chip, chip_idx) in enumerate(others)]

    def start():
        for cp in local + remote:
            cp.start()

    def finish():
        for cp in remote + local:
            cp.wait()

    return start, finish


def _exchange_chips(arrays, name):
    n = len(arrays)

    def body(*refs):
        start, finish = _chips_protocol(refs[:n], refs[n:2 * n], *refs[2 * n:])
        start()
        finish()

    out_shape = [jax.ShapeDtypeStruct(a.shape, a.dtype) for a in arrays]
    return _comm_call(body, arrays, out_shape, N_CHIP - 1, name)


def _as_rows(a, lead):
    shp = a.shape
    return a.reshape(shp[:lead] + (math.prod(shp[lead:-1]), shp[-1]))


def _row_tile(rows, cap=512):
    if rows <= cap:
        return rows
    for t in range(cap - cap % 8, 0, -8):
        if rows % t == 0:
            return t
    return rows


def _pair_sum(send, got, name):
    a3, b3 = _as_rows(send, 1), _as_rows(got, 1)
    _, rows, last = b3.shape
    tr = _row_tile(rows, 2048)

    def body(c_ref, a_ref, b_ref, o_ref):
        o_ref[...] = (a_ref[...].astype(F32) + b_ref[...].astype(F32)).astype(o_ref.dtype)

    core = lax.axis_index("c").astype(jnp.int32).reshape(1)
    out = _pcall(
        body, name=name,
        grid_spec=pltpu.PrefetchScalarGridSpec(
            num_scalar_prefetch=1, grid=(N_CHIP, rows // tr),
            in_specs=[pl.BlockSpec((None, tr, last), lambda k, i, c_ref: (2 * k + c_ref[0], i, 0)),
                      pl.BlockSpec((None, tr, last), lambda k, i, c_ref: (k, i, 0))],
            out_specs=pl.BlockSpec((None, tr, last), lambda k, i, c_ref: (k, i, 0))),
        out_shape=jax.ShapeDtypeStruct(b3.shape, got.dtype), compiler_params=_params(("parallel", "parallel")),
    )(core, a3, b3)
    return out.reshape(got.shape)


def _adamw(parts, w, m, v, name):
    lead, (rows, last) = w.shape[:-2], w.shape[-2:]
    nl = len(lead)
    tr = _row_tile(rows, 1024)
    c1 = 1.0 / (1.0 - ADAM_B1 ** ADAM_STEP)
    c2 = 1.0 / (1.0 - ADAM_B2 ** ADAM_STEP)

    def body(p_ref, w_ref, m_ref, v_ref, g_ref, d_ref, nm_ref, nv_ref):
        g = p_ref[0].astype(F32)
        for chip in range(1, N_CHIP):
            g = g + p_ref[chip].astype(F32)
        mn = ADAM_B1 * m_ref[...] + (1.0 - ADAM_B1) * g
        vn = ADAM_B2 * v_ref[...] + (1.0 - ADAM_B2) * (g * g)
        g_ref[...] = g
        nm_ref[...] = mn
        nv_ref[...] = vn
        d_ref[...] = -ADAM_LR * ((mn * c1) / (jnp.sqrt(vn * c2) + ADAM_EPS) + ADAM_WD * w_ref[...])

    row = pl.BlockSpec((None,) * nl + (tr, last), lambda *ix: ix + (0,))
    part = pl.BlockSpec((N_CHIP,) + (None,) * nl + (tr, last), lambda *ix: (0,) + ix + (0,))
    sh = jax.ShapeDtypeStruct(w.shape, F32)
    return _pcall(body, name=name, grid=lead + (rows // tr,), in_specs=[part, row, row, row],
                  out_specs=[row, row, row, row], out_shape=[sh, sh, sh, sh],
                  compiler_params=_params(("parallel",) * (nl + 1)))(parts, w, m, v)


def _pack(pieces, row_align):
    rows, offs, r = [], [], 0
    for a in pieces:
        flat = a.reshape(-1)
        nr = -(-flat.shape[0] // PACK_W)
        flat = jnp.pad(flat, (0, nr * PACK_W - flat.shape[0]))
        rows.append(flat.reshape(nr, PACK_W))
        offs.append(r)
        r += nr
    pad = (-r) % row_align
    if pad:
        rows.append(jnp.zeros((pad, PACK_W), pieces[0].dtype))
    return jnp.concatenate(rows, axis=0), offs


def _unpack(flat, offs, shapes):
    out = []
    for off, shp in zip(offs, shapes):
        size = math.prod(shp)
        nr = -(-size // PACK_W)
        out.append(flat[..., off:off + nr, :].reshape(flat.shape[:-2] + (nr * PACK_W,))[..., :size].reshape(flat.shape[:-2] + tuple(shp)))
    return out


def _to_full(gathered, axis):
    z = jnp.moveaxis(gathered, 0, axis)
    shp = list(z.shape)
    return z.reshape(shp[:axis] + [shp[axis] * shp[axis + 1]] + shp[axis + 2:])


def _to_shards(full, axis):
    shp = list(full.shape)
    z = full.reshape(shp[:axis] + [N_DEV, shp[axis] // N_DEV] + shp[axis + 1:])
    return jnp.moveaxis(z, axis, 0)


def kernel(x, p, ffn_norm, ffn_w_gate, ffn_w_up, ffn_w_down, mix_norm, att_w_in, att_q_norm, att_k_norm, att_sinks, att_w_out, gdn_w_in, gdn_conv_w, gdn_a_log, gdn_dt_bias, gdn_out_norm, gdn_w_out, ple_norm, ple_w_gate, ple_w_proj, loss_target, m_ffn_norm, m_ffn_w_gate, m_ffn_w_up, m_ffn_w_down, m_mix_norm, m_att_w_in, m_att_q_norm, m_att_k_norm, m_att_sinks, m_att_w_out, m_gdn_w_in, m_gdn_conv_w, m_gdn_a_log, m_gdn_dt_bias, m_gdn_out_norm, m_gdn_w_out, m_ple_norm, m_ple_w_gate, m_ple_w_proj, v_ffn_norm, v_ffn_w_gate, v_ffn_w_up, v_ffn_w_down, v_mix_norm, v_att_w_in, v_att_q_norm, v_att_k_norm, v_att_sinks, v_att_w_out, v_gdn_w_in, v_gdn_conv_w, v_gdn_a_log, v_gdn_dt_bias, v_gdn_out_norm, v_gdn_w_out, v_ple_norm, v_ple_w_gate, v_ple_w_proj):
    args = dict(locals())
    wts = {n: args[n] for n in WEIGHTS}
    mom = {n: args["m_" + n] for n in WEIGHTS}
    var = {n: args["v_" + n] for n in WEIGHTS}
    axis = dict(SHARDED)
    vecs = [n for n, _ in SHARDED[:SMALL_SHARDED]]
    small = vecs + list(REPLICATED)
    small_shapes = [wts[n].shape for n in small]
    lead = lambda n: 2 if n.startswith("ffn_") else 1

    def stack_of(arrays, name, idxs):
        return jnp.stack([arrays[name][idx] if idx else arrays[name][0] for idx in idxs])

    def full_instances(gathered, group):
        out = {}
        for (name, idxs), g in zip(group, gathered):
            whole = _to_full(g, axis[name] - lead(name) + 1)
            for k, idx in enumerate(idxs):
                out[(name,) + idx] = whole[k]
        return out

    def shard_stacks(g, group):
        return [_to_shards(jnp.stack([g[(name,) + idx] for idx in idxs]), axis[name] - lead(name) + 1)
                for name, idxs in group]

    vec_pack, voffs = _pack([wts[n] for n in vecs], 8)
    early = _all_gather([stack_of(wts, n, idxs).astype(BF16) for n, idxs in EARLY] + [vec_pack])
    w = full_instances(early[:-1], EARLY)
    vec_full = {n: _to_full(piece, axis[n]) for n, piece in
                zip(vecs, _unpack(early[-1], voffs, [wts[n].shape for n in vecs]))}
    w.update(_instances({**vec_full, **{n: wts[n] for n in REPLICATED}}))
    first_shards = [stack_of(wts, n, idxs).astype(BF16) for n, idxs in FIRST]
    late_shards = [stack_of(wts, n, idxs).astype(BF16) for n, idxs in LATE]

    def early_grads(g):
        send = shard_stacks(g, RIDE)
        got = _exchange_sibling(send, "exchange_sibling_early")
        return [_pair_sum(p_, q_, f"pair_sum_early_{i}") for i, (p_, q_) in enumerate(zip(send, got))]

    loss, grad_x, g, rode = _local_step(x[0], p[:, 0], loss_target[0], w, late_shards,
                                        lambda gathered: full_instances(gathered, LATE), early_grads,
                                        first_shards, lambda gathered: full_instances(gathered, FIRST))

    gs = _stacked({k: v for k, v in g.items() if k[0] in small})
    vec_shards = [_to_shards(gs[n], axis[n]) for n in vecs]
    small_send = jnp.stack([_pack([sh[d] for sh in vec_shards] + [gs[n] for n in REPLICATED] + [loss.reshape(1)], 8)[0]
                            for d in range(N_DEV)])
    send = shard_stacks(g, FINAL) + [small_send]
    got = _exchange_sibling(send, "exchange_sibling_final")
    chip_sums = [_pair_sum(p_, q_, f"pair_sum_final_{i}") for i, (p_, q_) in enumerate(zip(send, got))]
    last = _exchange_chips(chip_sums, "exchange_chips_final")

    pieces = {}
    for (name, idxs), part in list(zip(RIDE, rode)) + list(zip(FINAL, last[:-1])):
        for k, idx in enumerate(idxs):
            pieces[(name,) + idx] = part[:, k]
    outs = {}
    for n, _ in SHARDED[SMALL_SHARDED:]:
        if lead(n) == 2:
            part = jnp.stack([jnp.stack([pieces[(n, i, j)] for j in range(2)], axis=1) for i in range(2)], axis=1)
        elif (n, 0) in pieces:
            part = jnp.stack([pieces[(n, i)] for i in range(2)], axis=1)
        else:
            part = pieces[(n,)][:, None]
        outs[n] = _adamw(part, wts[n], mom[n], var[n], f"adamw_{n}")
    filler = [jnp.zeros((1,), F32)]
    small_w, soffs = _pack([wts[n] for n in small] + filler, 8)
    small_m, _ = _pack([mom[n] for n in small] + filler, 8)
    small_v, _ = _pack([var[n] for n in small] + filler, 8)
    small_out = [_unpack(z, soffs, small_shapes + [(1,)]) for z in _adamw(last[-1], small_w, small_m, small_v, "adamw_small")]
    loss = small_out[0][-1][0]
    for i, n in enumerate(small):
        outs[n] = [small_out[k][i] for k in range(4)]
    result = [loss, grad_x[None]]
    for k in range(4):
        result += [outs[n][k] for n in WEIGHTS]
    return tuple(result)
```

```python
import math

import jax
import jax.numpy as jnp
from jax import lax
from jax.experimental import pallas as pl
from jax.experimental.pallas import tpu as pltpu

F32 = jnp.float32
BF16 = jnp.bfloat16

N_DEV = 8
N_CHIP = 4
D_MODEL = 1024
D_FF = 2816
PLE_DIM = 256
HEAD_DIM = 64
SB_HEADS = 8
SWA_HEADS = 8
SWA_KV_HEADS = 2
SWA_GROUP = SWA_HEADS // SWA_KV_HEADS
WINDOW = 128
Q_BLOCK = 128
GDN_K_HEADS = 8
GDN_V_HEADS = 16
GDN_HEAD_DIM = 128
GDN_CONV = 4
GDN_CHUNK = 64
EPS = 1e-6
SB_W = SB_HEADS * HEAD_DIM
SWA_QW = SWA_HEADS * HEAD_DIM
SWA_KVW = SWA_KV_HEADS * HEAD_DIM
ATT_IN = 3 * SB_W + SWA_QW + 2 * SWA_KVW
GDN_KW = GDN_K_HEADS * GDN_HEAD_DIM
GDN_VW = GDN_V_HEADS * GDN_HEAD_DIM
GDN_CONV_W = 2 * GDN_KW + GDN_VW
GDN_IN = GDN_CONV_W + GDN_VW + 2 * GDN_V_HEADS
GDN_IN_PAD = GDN_CONV_W + GDN_VW + 2 * 128

ADAM_LR = 0.001
ADAM_B1 = 0.9
ADAM_B2 = 0.999
ADAM_EPS = 1e-08
ADAM_WD = 0.01
ADAM_STEP = 10

LANE = 128
VMEM_LIMIT = 56 * 1024 * 1024
MM_TILE_BUDGET = 40 * 1024 * 1024
PACK_W = 1024

NN = ((1,), (0,))
NT = ((1,), (1,))
TN = ((0,), (0,))

SHARDED = (
    ("ffn_norm", 2), ("gdn_conv_w", 2),
    ("ffn_w_gate", 3), ("ffn_w_up", 3), ("ffn_w_down", 2), ("att_w_in", 2), ("att_w_out", 1),
    ("gdn_w_in", 2), ("gdn_w_out", 1), ("ple_w_gate", 1), ("ple_w_proj", 2),
)
SMALL_SHARDED = 2
REPLICATED = ("mix_norm", "att_q_norm", "att_k_norm", "att_sinks", "gdn_a_log", "gdn_dt_bias",
              "gdn_out_norm", "ple_norm")
WEIGHTS = ("ffn_norm", "ffn_w_gate", "ffn_w_up", "ffn_w_down", "mix_norm", "att_w_in", "att_q_norm",
           "att_k_norm", "att_sinks", "att_w_out", "gdn_w_in", "gdn_conv_w", "gdn_a_log", "gdn_dt_bias",
           "gdn_out_norm", "gdn_w_out", "ple_norm", "ple_w_gate", "ple_w_proj")


_FFN_REST = [(0, 1), (1, 0), (1, 1)]
EARLY = [("ffn_w_gate", [(0, 0)]), ("ffn_w_up", [(0, 0)])]
FIRST = [("ffn_w_down", [(0, 0)]), ("att_w_in", [()])]
LATE = ([(n, [idx]) for n in ("ffn_w_gate", "ffn_w_up", "ffn_w_down") for idx in _FFN_REST]
        + [("att_w_out", [()]), ("gdn_w_in", [()]), ("gdn_w_out", [()]),
           ("ple_w_gate", [(0,), (1,)]), ("ple_w_proj", [(0,), (1,)])])
RIDE = [e for e in LATE if e[0] != "att_w_out"]
FINAL = EARLY + FIRST + [("att_w_out", [()])]


def _pcall(body, **kw):
    return pl.pallas_call(body, **kw)


def _params(sem=None):
    if sem is None:
        return pltpu.CompilerParams(vmem_limit_bytes=VMEM_LIMIT)
    return pltpu.CompilerParams(dimension_semantics=sem, vmem_limit_bytes=VMEM_LIMIT)


def _ride_specs(ride, out_shapes, n_sems):
    hbm = pl.BlockSpec(memory_space=pl.ANY)
    n = len(ride)
    sems = [pltpu.SemaphoreType.DMA((n, n_sems)), pltpu.SemaphoreType.DMA((n, n_sems)),
            pltpu.SemaphoreType.DMA((n, N_CHIP))] if n else []
    return [hbm] * n, [hbm] * len(out_shapes), sems


def _dot(a, b, dims=NN):
    return lax.dot_general(a, b, (dims, ((), ())), preferred_element_type=F32)


def _bdot(a, b, dims=NN):
    return _dot(a.astype(BF16), b.astype(BF16), dims)


def _split(a):
    hi = a.astype(BF16)
    lo = (a - hi.astype(F32)).astype(BF16)
    return hi, lo


def _dot3(a, b, dims=NN):
    ah, al = _split(a)
    bh, bl = _split(b)
    return _dot(ah, bh, dims) + (_dot(ah, bl, dims) + _dot(al, bh, dims))


def _dot2m(a, m, dims=NN):
    ah, al = _split(a)
    return _dot(ah, m, dims) + _dot(al, m, dims)


def _mdot2(m, a, dims=NN):
    ah, al = _split(a)
    return _dot(m, ah, dims) + _dot(m, al, dims)


def _sigmoid(x):
    return 1.0 / (1.0 + jnp.exp(-x))


def _softplus(x):
    return jnp.maximum(x, 0.0) + jnp.log(1.0 + jnp.exp(-jnp.abs(x)))


def _pick(n, cap):
    if n <= cap:
        return n
    for t in range(cap - cap % LANE, 0, -LANE):
        if n % t == 0:
            return t
    raise ValueError(f"no tile for {n} under {cap}")


def _iota2(shape, axis):
    return lax.broadcasted_iota(jnp.int32, shape, axis)


def _mm(a, b, mode, out_dtype=F32, res=None, alpha=1.0, a2=None, b2=None, name="mm"):
    if mode == "nn":
        (M, K), N = a.shape, b.shape[1]
    elif mode == "nt":
        (M, K), N = a.shape, b.shape[0]
    else:
        (K, M), N = a.shape, b.shape[1]
    tn, tk = _pick(N, 1408), _pick(K, 2048 if mode == "tn" else 1408)
    nk = K // tk
    pairs = 1 if a2 is None else 2

    def tile_bytes(tm):
        per = pairs * tk * (tm * a.dtype.itemsize + tn * b.dtype.itemsize) + tm * tn * jnp.dtype(out_dtype).itemsize
        return 2 * (per + (tm * tn * 4 if res is not None else 0)) + (tm * tn * 4 if nk > 1 else 0)

    tm = next(t for t in (_pick(M, c) for c in ((1408,) if mode == "tn" else (2048, 1024, 512))) if tile_bytes(t) <= MM_TILE_BUDGET or t <= 512)
    dims = {"nn": NN, "nt": NT, "tn": TN}[mode]
    a_spec = pl.BlockSpec((tk, tm), lambda i, j, k: (k, i)) if mode == "tn" else pl.BlockSpec((tm, tk), lambda i, j, k: (i, k))
    b_spec = pl.BlockSpec((tn, tk), lambda i, j, k: (j, k)) if mode == "nt" else pl.BlockSpec((tk, tn), lambda i, j, k: (k, j))
    o_spec = pl.BlockSpec((tm, tn), lambda i, j, k: (i, j))
    two = a2 is not None
    has_res = res is not None
    a2_spec, b2_spec = a_spec, b_spec
    if two and a2.shape != a.shape:
        assert nk == 1 and mode == "nn" and a2.shape[0] == M and b2.shape[1] == N
        a2_spec = pl.BlockSpec((tm, a2.shape[1]), lambda i, j, k: (i, 0))
        b2_spec = pl.BlockSpec((a2.shape[1], tn), lambda i, j, k: (0, j))

    def body(*refs):
        refs = list(refs)
        a_ref, b_ref = refs[0], refs[1]
        pos = 2
        if two:
            a2_ref, b2_ref = refs[2], refs[3]
            pos = 4
        if has_res:
            res_ref = refs[pos]
            pos += 1
        o_ref, acc_ref = refs[pos], refs[pos + 1]
        k = pl.program_id(2)
        part = _bdot(a_ref[...], b_ref[...], dims)
        if two:
            part = part + _bdot(a2_ref[...], b2_ref[...], dims)

        def finish(acc):
            out = acc * alpha if alpha != 1.0 else acc
            if has_res:
                out = res_ref[...] + out
            o_ref[...] = out.astype(out_dtype)

        if nk == 1:
            finish(part)
        else:
            @pl.when(k == 0)
            def _():
                acc_ref[...] = part

            @pl.when(k > 0)
            def _():
                acc_ref[...] += part

            @pl.when(k == nk - 1)
            def _():
                finish(acc_ref[...])

    ins = [a, b]
    specs = [a_spec, b_spec]
    if two:
        ins += [a2, b2]
        specs += [a2_spec, b2_spec]
    if has_res:
        ins.append(res)
        specs.append(o_spec)
    return _pcall(
        body, name=name, grid=(M // tm, N // tn, nk), in_specs=specs, out_specs=o_spec,
        out_shape=jax.ShapeDtypeStruct((M, N), out_dtype),
        scratch_shapes=[pltpu.VMEM((tm, tn) if nk > 1 else (8, LANE), F32)],
        compiler_params=_params(("parallel", "parallel", "arbitrary")),
    )(*ins)


ROW_TILE = 1024


def _rms_fwd(h, g, name):
    T, D = h.shape
    tr = _pick(T, ROW_TILE)

    def body(h_ref, g_ref, n_ref):
        x = h_ref[...]
        r = lax.rsqrt(jnp.mean(x * x, axis=-1, keepdims=True) + EPS)
        n_ref[...] = (x * r * g_ref[...]).astype(BF16)

    return _pcall(
        body, name=name, grid=(T // tr,),
        in_specs=[pl.BlockSpec((tr, D), lambda i: (i, 0)), pl.BlockSpec((1, D), lambda i: (0, 0))],
        out_specs=pl.BlockSpec((tr, D), lambda i: (i, 0)),
        out_shape=jax.ShapeDtypeStruct((T, D), BF16), compiler_params=_params(("parallel",)),
    )(h, g.reshape(1, D))


def _rms_bwd(dn, h, g, dres, name):
    T, D = h.shape
    tr = _pick(T, ROW_TILE)

    def body(dn_ref, h_ref, g_ref, dres_ref, dh_ref, dg_ref):
        x = h_ref[...]
        r = lax.rsqrt(jnp.mean(x * x, axis=-1, keepdims=True) + EPS)
        xh = x * r
        d = dn_ref[...].astype(F32)
        dxh = d * g_ref[...]
        dh_ref[...] = dres_ref[...] + r * (dxh - xh * jnp.mean(dxh * xh, axis=-1, keepdims=True))
        part = jnp.sum(d * xh, axis=0, keepdims=True)

        @pl.when(pl.program_id(0) == 0)
        def _():
            dg_ref[...] = part

        @pl.when(pl.program_id(0) > 0)
        def _():
            dg_ref[...] += part

    row = pl.BlockSpec((tr, D), lambda i: (i, 0))
    vec = pl.BlockSpec((1, D), lambda i: (0, 0))
    dh, dg = _pcall(
        body, name=name, grid=(T // tr,), in_specs=[row, row, vec, row], out_specs=[row, vec],
        out_shape=[jax.ShapeDtypeStruct((T, D), F32), jax.ShapeDtypeStruct((1, D), F32)],
        compiler_params=_params(("arbitrary",)),
    )(dn, h, g.reshape(1, D), dres)
    return dh, dg.reshape(D)


def _gateup(n, wg, wu, name, ride=()):
    T, D = n.shape
    F = wg.shape[1]
    tm, tn = _pick(T, 1024), _pick(F, 1408)
    nr = len(ride)
    ride_out = [jax.ShapeDtypeStruct((N_DEV,) + r.shape, r.dtype) for r in ride]
    ride_in_specs, ride_out_specs, ride_sems = _ride_specs(ride, ride_out, N_DEV - 1)
    grid = (T // tm, F // tn)

    def body(*refs):
        n_ref, wg_ref, wu_ref = refs[:3]
        a_ref, b_ref, hid_ref = refs[3 + nr:6 + nr]
        if nr:
            i, j = pl.program_id(0), pl.program_id(1)
            start, finish = _gather_protocol(refs[3:3 + nr], refs[6 + nr:6 + 2 * nr], *refs[6 + 2 * nr:])
            pl.when((i == 0) & (j == 0))(start)
        x = n_ref[...]
        a = _dot(x, wg_ref[...])
        b = _dot(x, wu_ref[...])
        a_ref[...] = a.astype(BF16)
        b_ref[...] = b.astype(BF16)
        hid_ref[...] = (a * _sigmoid(a) * b).astype(BF16)
        if nr:
            pl.when((i == grid[0] - 1) & (j == grid[1] - 1))(finish)

    o_spec = pl.BlockSpec((tm, tn), lambda i, j: (i, j))
    w_spec = pl.BlockSpec((D, tn), lambda i, j: (0, j))
    sh = jax.ShapeDtypeStruct((T, F), BF16)
    res = _pcall(
        body, name=name, grid=grid,
        in_specs=[pl.BlockSpec((tm, D), lambda i, j: (i, 0)), w_spec, w_spec] + ride_in_specs,
        out_specs=[o_spec, o_spec, o_spec] + ride_out_specs, out_shape=[sh, sh, sh] + ride_out,
        scratch_shapes=ride_sems,
        compiler_params=_params(("arbitrary", "arbitrary") if nr else ("parallel", "parallel")),
    )(n, wg, wu, *ride)
    return res[0], res[1], res[2], list(res[3:])


def _ffn_dhid(dy, wd, a, b, name):
    T, D = dy.shape
    F = wd.shape[0]
    tm, tn = _pick(T, 1024), _pick(F, 1408)

    def body(dy_ref, wd_ref, a_ref, b_ref, da_ref, db_ref):
        dhid = 0.5 * _bdot(dy_ref[...], wd_ref[...], NT)
        av = a_ref[...].astype(F32)
        bv = b_ref[...].astype(F32)
        s = _sigmoid(av)
        da_ref[...] = (dhid * bv * s * (1.0 + av * (1.0 - s))).astype(BF16)
        db_ref[...] = (dhid * av * s).astype(BF16)

    o_spec = pl.BlockSpec((tm, tn), lambda i, j: (i, j))
    sh = jax.ShapeDtypeStruct((T, F), BF16)
    return _pcall(
        body, name=name, grid=(T // tm, F // tn),
        in_specs=[pl.BlockSpec((tm, D), lambda i, j: (i, 0)), pl.BlockSpec((tn, D), lambda i, j: (j, 0)), o_spec, o_spec],
        out_specs=[o_spec, o_spec], out_shape=[sh, sh],
        compiler_params=_params(("parallel", "parallel")),
    )(dy, wd, a, b)


def _ffn_fwd(h, g, wg, wu, wd, tag, ride=(), wd_of=None):
    n = _rms_fwd(h, g, f"{tag}_norm")
    a, b, hid, gathered = _gateup(n, wg, wu, f"{tag}_gateup", ride)
    if wd_of is not None:
        wd = wd_of(gathered)
    h2 = _mm(hid, wd, "nn", res=h, alpha=0.5, name=f"{tag}_down")
    return h2, (h, n, a, b, hid)


def _ffn_bwd(dh2, saved, g, wg, wu, wd, tag):
    h, n, a, b, hid = saved
    da, db = _ffn_dhid(dh2, wd, a, b, f"{tag}_dhid")
    dwd = _mm(hid, dh2, "tn", alpha=0.5, out_dtype=BF16, name=f"{tag}_dwd")
    dwg = _mm(n, da, "tn", out_dtype=BF16, name=f"{tag}_dwg")
    dwu = _mm(n, db, "tn", out_dtype=BF16, name=f"{tag}_dwu")
    dn = _mm(da, wg, "nt", a2=db, b2=wu, name=f"{tag}_dn")
    dh, dg = _rms_bwd(dn, h, g, dh2, f"{tag}_dnorm")
    return dh, dg, dwg, dwu, dwd


def _ple_fwd(h, p, g, w_gate, w_proj, tag):
    T, D = h.shape
    pn = _rms_fwd(h, g, f"{tag}_norm")
    tm, tn = _pick(T, 512), _pick(D, 1024)
    P = p.shape[1]

    def body(pn_ref, p_ref, wg_ref, wp_ref, h_ref, o_ref, gl_ref, pp_ref):
        gl = _dot(pn_ref[...], wg_ref[...])
        pp = _bdot(p_ref[...], wp_ref[...])
        gl_ref[...] = gl
        pp_ref[...] = pp
        o_ref[...] = h_ref[...] + _sigmoid(gl) * pp

    o_spec = pl.BlockSpec((tm, tn), lambda i, j: (i, j))
    sh = jax.ShapeDtypeStruct((T, D), F32)
    h2, gl, pp = _pcall(
        body, name=f"{tag}_fwd", grid=(T // tm, D // tn),
        in_specs=[pl.BlockSpec((tm, D), lambda i, j: (i, 0)), pl.BlockSpec((tm, P), lambda i, j: (i, 0)),
                  pl.BlockSpec((D, tn), lambda i, j: (0, j)), pl.BlockSpec((P, tn), lambda i, j: (0, j)), o_spec],
        out_specs=[o_spec, o_spec, o_spec], out_shape=[sh, sh, sh],
        compiler_params=_params(("parallel", "parallel")),
    )(pn, p, w_gate, w_proj, h)
    return h2, (h, pn, gl, pp)


def _ple_bwd(dh2, saved, p, g, w_gate, tag):
    h, pn, gl, pp = saved
    T, D = h.shape
    tr = _pick(T, ROW_TILE)

    def body(d_ref, gl_ref, pp_ref, dgl_ref, dpp_ref):
        d = d_ref[...]
        s = _sigmoid(gl_ref[...])
        dpp_ref[...] = (d * s).astype(BF16)
        dgl_ref[...] = (d * pp_ref[...] * s * (1.0 - s)).astype(BF16)

    row = pl.BlockSpec((tr, D), lambda i: (i, 0))
    sh = jax.ShapeDtypeStruct((T, D), BF16)
    dgl, dpp = _pcall(body, name=f"{tag}_dgate", grid=(T // tr,), in_specs=[row, row, row], out_specs=[row, row],
                      out_shape=[sh, sh], compiler_params=_params(("parallel",)))(dh2, gl, pp)
    dw_proj = _mm(p, dpp, "tn", out_dtype=BF16, name=f"{tag}_dwproj")
    dw_gate = _mm(pn, dgl, "tn", out_dtype=BF16, name=f"{tag}_dwgate")
    dpn = _mm(dgl, w_gate, "nt", name=f"{tag}_dpn")
    dh, dg = _rms_bwd(dpn, h, g, dh2, f"{tag}_dnorm")
    return dh, dg, dw_gate, dw_proj


def _loss_head(y, target):
    T, D = y.shape
    tr = _pick(T, ROW_TILE)

    def body(y_ref, t_ref, dy_ref, l_ref):
        e = y_ref[...] - t_ref[...]
        dy_ref[...] = e * (1.0 / D)
        part = jnp.sum(e * e, axis=0, keepdims=True)

        @pl.when(pl.program_id(0) == 0)
        def _():
            l_ref[...] = part

        @pl.when(pl.program_id(0) > 0)
        def _():
            l_ref[...] += part

    row = pl.BlockSpec((tr, D), lambda i: (i, 0))
    vec = pl.BlockSpec((1, D), lambda i: (0, 0))
    dy, l = _pcall(body, name="loss_head", grid=(T // tr,), in_specs=[row, row], out_specs=[row, vec],
                   out_shape=[jax.ShapeDtypeStruct((T, D), F32), jax.ShapeDtypeStruct((1, D), F32)],
                   compiler_params=_params(("arbitrary",)))(y, target)
    return (0.5 / D) * jnp.sum(l), dy


SB_LANES = SB_HEADS * 2 * HEAD_DIM


def _sb_consts():
    row = _iota2((Q_BLOCK, Q_BLOCK), 0)
    col = _iota2((Q_BLOCK, Q_BLOCK), 1)
    after = (row > col).astype(BF16)
    before = (row < col).astype(BF16)
    return col < row, after, before, col


def _sb_fwd(proj, ride=()):
    T = proj.shape[0]
    H, d, L = SB_HEADS, HEAD_DIM, 2 * HEAD_DIM
    nblk = T // Q_BLOCK
    scale = d ** -0.5
    n = len(ride)
    ride_out = [jax.ShapeDtypeStruct((N_DEV,) + a.shape, a.dtype) for a in ride]
    ride_in_specs, ride_out_specs, ride_sems = _ride_specs(ride, ride_out, N_DEV - 1)
    R = range(H)
    tile = lambda g: slice(g * L, (g + 1) * L)

    def body(*refs):
        q_ref, kv_ref = refs[:2]
        rin = refs[2:2 + n]
        o_ref, c_ref = refs[2 + n:4 + n]
        rout = refs[4 + n:4 + 2 * n]
        run_ref = refs[4 + 2 * n]
        i = pl.program_id(0)
        if n:
            start, finish = _gather_protocol(rin, rout, *refs[5 + 2 * n:])
            pl.when(i == 0)(start)
        causal, after, _, col = _sb_consts()
        qs = [q_ref[:, tile(g)] * scale for g in R]
        o_ref[...] = jnp.zeros_like(o_ref)
        c_ref[...] = jnp.zeros_like(c_ref)
        run_ref[...] = jnp.zeros_like(run_ref)

        def pair(j, diag):
            rows = pl.ds(pl.multiple_of(j * Q_BLOCK, Q_BLOCK), Q_BLOCK)
            kvj = [kv_ref[rows, tile(g)] for g in R]
            c = [run_ref[g] for g in R]
            acc = [o_ref[:, tile(g)] for g in R]
            cm = None if diag else [c_ref[:, tile(g)] for g in R]
            z = [_dot(qs[g], kvj[g], NT) for g in R]
            sp = [_softplus(z[g]) for g in R]
            lk = [jnp.where(causal, -sp[g], 0.0) if diag else -sp[g] for g in R]
            btw = [_dot2m(lk[g], after) for g in R]
            e = [jnp.exp((z[g] - sp[g]) + btw[g] + c[g]) for g in R]
            w = [jnp.where(causal, e[g], 0.0) if diag else e[g] for g in R]
            pv = [_bdot(w[g], kvj[g]) for g in R]
            rs = [jnp.sum(lk[g], axis=1, keepdims=True) for g in R]
            for g in R:
                o_ref[:, tile(g)] = acc[g] + pv[g]
                if not diag:
                    c_ref[:, tile(g)] = jnp.where(col == j, c[g], cm[g])
                run_ref[g] = c[g] + rs[g]

        pair(i, True)

        @pl.loop(0, i)
        def _(jj):
            pair(i - 1 - jj, False)

        if n:
            pl.when(i == nblk - 1)(finish)

    blk = pl.BlockSpec((Q_BLOCK, H * L), lambda i: (i, 0))
    full = pl.BlockSpec((T, H * L), lambda i: (0, 1))
    res = _pcall(
        body, name="sb_fwd", grid=(nblk,), in_specs=[blk, full] + ride_in_specs,
        out_specs=[blk, blk] + ride_out_specs,
        out_shape=[jax.ShapeDtypeStruct((T, H * L), F32), jax.ShapeDtypeStruct((T, H * L), F32)] + ride_out,
        scratch_shapes=[pltpu.VMEM((H, Q_BLOCK, 1), F32)] + ride_sems,
        compiler_params=_params(("arbitrary",)),
    )(proj, proj, *ride)
    return res[0], res[1], list(res[2:])


def _sb_bwd(proj, carry, do, ride=()):
    T = proj.shape[0]
    H, d, L = SB_HEADS, HEAD_DIM, 2 * HEAD_DIM
    nblk = T // Q_BLOCK
    scale = d ** -0.5
    n = len(ride)
    ride_out = [jax.ShapeDtypeStruct(a.shape, a.dtype) for a in ride]
    ride_in_specs, ride_out_specs, ride_sems = _ride_specs(ride, ride_out, N_CHIP - 1)
    R = range(H)
    tile = lambda g: slice(g * L, (g + 1) * L)

    def body(*refs):
        q_ref, kv_ref, c_ref, do_ref = refs[:4]
        rin = refs[4:4 + n]
        dq_ref, dkv_ref = refs[4 + n:6 + n]
        rout = refs[6 + n:6 + 2 * n]
        run_ref = refs[6 + 2 * n]
        i = pl.program_id(0)
        if n:
            start, finish = _chips_protocol(rin, rout, *refs[7 + 2 * n:])
            pl.when(i == 0)(start)

        @pl.when(i == 0)
        def _():
            dkv_ref[...] = jnp.zeros_like(dkv_ref)

        causal, after, before, col = _sb_consts()
        qs = [q_ref[:, tile(g)] * scale for g in R]
        dov = [do_ref[:, tile(g)] for g in R]
        qdo = [jnp.concatenate([qs[g], dov[g]], axis=0) for g in R]
        dq_ref[...] = jnp.zeros_like(dq_ref)
        run_ref[...] = jnp.zeros_like(run_ref)

        def pair(j, diag):
            rows = pl.ds(pl.multiple_of(j * Q_BLOCK, Q_BLOCK), Q_BLOCK)
            kvj = [kv_ref[rows, tile(g)] for g in R]
            gsum = [run_ref[g] for g in R]
            dq0 = [dq_ref[:, tile(g)] for g in R]
            dkv0 = [dkv_ref[rows, tile(g)] for g in R]
            cm = None if diag else [c_ref[:, tile(g)] for g in R]
            z = [_dot(qs[g], kvj[g], NT) for g in R]
            sp = [_softplus(z[g]) for g in R]
            lk = [jnp.where(causal, -sp[g], 0.0) if diag else -sp[g] for g in R]
            ls = [z[g] - sp[g] for g in R]
            logw = [ls[g] + _dot2m(lk[g], after) for g in R]
            if not diag:
                logw = [logw[g] + jnp.sum(jnp.where(col == j, cm[g], 0.0), axis=1, keepdims=True) for g in R]
            e = [jnp.exp(logw[g]) for g in R]
            w = [jnp.where(causal, e[g], 0.0) if diag else e[g] for g in R]
            gw = [_dot(dov[g], kvj[g], NT) * w[g] for g in R]
            gpre = [gsum[g] + _dot(gw[g].astype(BF16), before) for g in R]
            sig = [jnp.exp(ls[g]) for g in R]
            dz = [gw[g] * (1.0 - sig[g]) - sig[g] * gpre[g] for g in R]
            if diag:
                dz = [jnp.where(causal, dz[g], 0.0) for g in R]
            dzb = [dz[g].astype(BF16) for g in R]
            dq1 = [_dot(dzb[g], kvj[g]) for g in R]
            dkv1 = [_dot(jnp.concatenate([dzb[g], w[g].astype(BF16)], axis=0), qdo[g], TN) for g in R]
            gs1 = [jnp.sum(gw[g], axis=1, keepdims=True) for g in R]
            for g in R:
                dq_ref[:, tile(g)] = dq0[g] + dq1[g]
                dkv_ref[rows, tile(g)] = dkv0[g] + dkv1[g]
                run_ref[g] = gsum[g] + gs1[g]

        @pl.loop(0, i)
        def _(j):
            pair(j, False)

        pair(i, True)
        dq_ref[...] = dq_ref[...] * scale
        if n:
            pl.when(i == nblk - 1)(finish)

    blk = pl.BlockSpec((Q_BLOCK, H * L), lambda i: (i, 0))
    once = pl.Buffered(1)
    sh = jax.ShapeDtypeStruct((T, H * L), F32)
    res = _pcall(
        body, name="sb_bwd", grid=(nblk,),
        in_specs=[blk, pl.BlockSpec((T, H * L), lambda i: (0, 1), pipeline_mode=once), blk, blk] + ride_in_specs,
        out_specs=[blk, pl.BlockSpec((T, H * L), lambda i: (0, 0), pipeline_mode=once)] + ride_out_specs,
        out_shape=[sh, sh] + ride_out,
        scratch_shapes=[pltpu.VMEM((H, Q_BLOCK, 1), F32)] + ride_sems,
        compiler_params=_params(("arbitrary",)),
    )(proj, proj, carry, do, *ride)
    return res[0], res[1], list(res[2:])


def _swa_common(q_ref, kvp_ref, kvc_ref, qg_ref, kg_ref, sk_ref, sl_ref, n):
    W, d, G = WINDOW, HEAD_DIM, SWA_GROUP
    scale = d ** -0.5
    row = _iota2((W, 2 * W), 0)
    col = _iota2((W, 2 * W), 1)
    dist = row + W - col
    valid = (dist >= 0) & (dist < W) & ((n > 0) | (col >= W))
    distf = dist.astype(F32)
    kvcat = jnp.concatenate([kvp_ref[...], kvc_ref[...]], axis=0)
    KH, QH = range(SWA_KV_HEADS), range(SWA_HEADS)
    kraw = [kvcat[:, hk * d:(hk + 1) * d] for hk in KH]
    vcat = [kvcat[:, SWA_KVW + hk * d:SWA_KVW + (hk + 1) * d].astype(BF16) for hk in KH]
    rk = [lax.rsqrt(jnp.mean(kraw[hk] * kraw[hk], axis=-1, keepdims=True) + EPS) for hk in KH]
    kh = [kraw[hk] * rk[hk] for hk in KH]
    kn = [(kh[hk] * kg_ref[...]).astype(BF16) for hk in KH]
    qraw = [q_ref[:, h * d:(h + 1) * d] for h in QH]
    rq = [lax.rsqrt(jnp.mean(qraw[h] * qraw[h], axis=-1, keepdims=True) + EPS) for h in QH]
    qh = [qraw[h] * rq[h] for h in QH]
    qn = [(qh[h] * qg_ref[...]).astype(BF16) for h in QH]
    sink = [sk_ref[h:h + 1, :1] for h in QH]
    s = [jnp.where(valid, _dot(qn[h], kn[h // G], NT) * scale - sl_ref[h:h + 1, :1] * distf, -1e30) for h in QH]
    m = [jnp.maximum(jnp.max(s[h], axis=1, keepdims=True), sink[h]) for h in QH]
    p = [jnp.where(valid, jnp.exp(s[h] - m[h]), 0.0) for h in QH]
    esink = [jnp.exp(sink[h] - m[h]) for h in QH]
    den = [jnp.sum(p[h], axis=1, keepdims=True) + esink[h] for h in QH]
    prob = [p[h] / den[h] for h in QH]
    return vcat, rk, kh, kn, rq, qh, qn, esink, den, prob


def _swa_specs(T):
    W = WINDOW
    q = pl.BlockSpec((W, SWA_QW), lambda n: (n, 0))
    prev = pl.BlockSpec((W, 2 * SWA_KVW), lambda n: (jnp.maximum(n - 1, 0), SWA_QW // (2 * SWA_KVW)))
    cur = pl.BlockSpec((W, 2 * SWA_KVW), lambda n: (n, SWA_QW // (2 * SWA_KVW)))
    gain = pl.BlockSpec((1, HEAD_DIM), lambda n: (0, 0))
    perhead = pl.BlockSpec((SWA_HEADS, LANE), lambda n: (0, 0))
    return q, prev, cur, gain, perhead


def _swa_fwd(proj, qg, kg, sinks, slopes):
    T = proj.shape[0]
    W, d, G = WINDOW, HEAD_DIM, SWA_GROUP

    def body(q_ref, kvp_ref, kvc_ref, qg_ref, kg_ref, sk_ref, sl_ref, o_ref):
        vcat, _, _, _, _, _, _, _, _, prob = _swa_common(q_ref, kvp_ref, kvc_ref, qg_ref, kg_ref, sk_ref, sl_ref,
                                                         pl.program_id(0))
        outs = [_bdot(prob[h], vcat[h // G]) for h in range(SWA_HEADS)]
        o_ref[...] = jnp.concatenate(outs, axis=1).astype(BF16)

    q, prev, cur, gain, perhead = _swa_specs(T)
    return _pcall(
        body, name="swa_fwd", grid=(T // W,), in_specs=[q, prev, cur, gain, gain, perhead, perhead], out_specs=q,
        out_shape=jax.ShapeDtypeStruct((T, SWA_QW), BF16), compiler_params=_params(("parallel",)),
    )(proj, proj, proj, qg, kg, sinks, slopes)


def _swa_bwd(proj, qg, kg, sinks, slopes, do):
    T = proj.shape[0]
    W, d, G = WINDOW, HEAD_DIM, SWA_GROUP
    scale = d ** -0.5
    KH, QH = range(SWA_KV_HEADS), range(SWA_HEADS)

    def body(q_ref, kvp_ref, kvc_ref, qg_ref, kg_ref, sk_ref, sl_ref, do_ref,
             dq_ref, dkv_ref, dqg_ref, dkg_ref, dsk_ref):
        n = pl.program_id(0)

        @pl.when(n == 0)
        def _():
            dqg_ref[...] = jnp.zeros_like(dqg_ref)
            dkg_ref[...] = jnp.zeros_like(dkg_ref)
            dsk_ref[...] = jnp.zeros_like(dsk_ref)
            dkv_ref[...] = jnp.zeros_like(dkv_ref)

        vcat, rk, kh, kn, rq, qh, qn, esink, den, prob = _swa_common(q_ref, kvp_ref, kvc_ref, qg_ref, kg_ref,
                                                                     sk_ref, sl_ref, n)
        dov = [do_ref[:, h * d:(h + 1) * d].astype(BF16) for h in QH]
        dp = [_dot(dov[h], vcat[h // G], NT) for h in QH]
        dd = [jnp.sum(prob[h] * dp[h], axis=1, keepdims=True) for h in QH]
        dsb = [(prob[h] * (dp[h] - dd[h]) * scale).astype(BF16) for h in QH]
        dsink = [-jnp.sum((esink[h] / den[h]) * dd[h], axis=0, keepdims=True) for h in QH]
        dqn = [_dot(dsb[h], kn[h // G]) for h in QH]
        dkn_h = [_dot(dsb[h], qn[h], TN) for h in QH]
        dv_h = [_dot(prob[h].astype(BF16), dov[h], TN) for h in QH]
        dqh = [dqn[h] * qg_ref[...] for h in QH]
        dq = [rq[h] * (dqh[h] - qh[h] * jnp.mean(dqh[h] * qh[h], axis=-1, keepdims=True)) for h in QH]
        dkn = [sum(dkn_h[hk * G + g] for g in range(G)) for hk in KH]
        dvc = [sum(dv_h[hk * G + g] for g in range(G)) for hk in KH]
        dkh = [dkn[hk] * kg_ref[...] for hk in KH]
        dkraw = [rk[hk] * (dkh[hk] - kh[hk] * jnp.mean(dkh[hk] * kh[hk], axis=-1, keepdims=True)) for hk in KH]
        dq_ref[...] = jnp.concatenate(dq, axis=1)
        dqg_ref[...] += sum(jnp.sum(dqn[h] * qh[h], axis=0, keepdims=True) for h in QH)
        dkg_ref[...] += sum(jnp.sum(dkn[hk] * kh[hk], axis=0, keepdims=True) for hk in KH)
        rowh = _iota2((SWA_HEADS, LANE), 0)
        dsk_ref[...] += sum(jnp.where(rowh == h, dsink[h], 0.0) for h in QH)
        upd = jnp.concatenate(dkraw + dvc, axis=1)
        offp = pl.multiple_of(jnp.maximum(n - 1, 0) * W, W)
        offc = pl.multiple_of(n * W, W)
        dkv_ref[pl.ds(offp, W), :] += upd[:W]
        dkv_ref[pl.ds(offc, W), :] += upd[W:]

    q, prev, cur, gain, perhead = _swa_specs(T)
    kvfull = pl.BlockSpec((T, 2 * SWA_KVW), lambda n: (0, 0))
    gs = jax.ShapeDtypeStruct((1, d), F32)
    return _pcall(
        body, name="swa_bwd", grid=(T // W,), in_specs=[q, prev, cur, gain, gain, perhead, perhead, q],
        out_specs=[q, kvfull, gain, gain, perhead],
        out_shape=[jax.ShapeDtypeStruct((T, SWA_QW), F32), jax.ShapeDtypeStruct((T, 2 * SWA_KVW), F32), gs, gs,
                   jax.ShapeDtypeStruct((SWA_HEADS, LANE), F32)],
        compiler_params=_params(("arbitrary",)),
    )(proj, proj, proj, qg, kg, sinks, slopes, do)


def _alibi():
    s = [2.0 ** (-8.0 * (i + 1) / SWA_HEADS) for i in range(SWA_HEADS)]
    return jnp.broadcast_to(jnp.asarray(s, F32)[:, None], (SWA_HEADS, LANE))


def _head_tiles(lo, hi):
    shp = lo.shape[:-1]
    return jnp.concatenate([lo.reshape(shp + (SB_HEADS, HEAD_DIM)), hi.reshape(shp + (SB_HEADS, HEAD_DIM))],
                           axis=-1).reshape(shp + (SB_LANES,))


def _tile_halves(x):
    shp = x.shape[:-1]
    t = x.reshape(shp + (SB_HEADS, 2, HEAD_DIM))
    return t[..., 0, :].reshape(shp + (SB_W,)), t[..., 1, :].reshape(shp + (SB_W,))


def _att_in_weights(w_in):
    sq, sk, sv = w_in[:, :SB_W], w_in[:, SB_W:2 * SB_W], w_in[:, 2 * SB_W:3 * SB_W]
    return jnp.concatenate([_head_tiles(sq, jnp.zeros_like(sq)), _head_tiles(sk, sv)], axis=1), w_in[:, 3 * SB_W:]


def _att_out_weights(w_out):
    wo = w_out[:SB_W]
    return _head_tiles(jnp.zeros_like(wo).T, wo.T).T, w_out[SB_W:]


def _att_fwd(h, g, w_in, w_out_of, q_gain, k_gain, sinks, ride=()):
    hn = _rms_fwd(h, g, "att_norm")
    w_sb, w_swa = _att_in_weights(w_in)
    proj_sb = _mm(hn, w_sb, "nn", out_dtype=BF16, name="att_in_sb")
    proj_swa = _mm(hn, w_swa, "nn", name="att_in_swa")
    a_out, carry, gathered = _sb_fwd(proj_sb, ride)
    w_out = w_out_of(gathered)
    wo_sb, wo_swa = _att_out_weights(w_out)
    sk128 = jnp.broadcast_to(sinks.reshape(SWA_HEADS, 1), (SWA_HEADS, LANE))
    qg, kg = q_gain.reshape(1, HEAD_DIM), k_gain.reshape(1, HEAD_DIM)
    b_out = _swa_fwd(proj_swa, qg, kg, sk128, _alibi())
    h2 = _mm(a_out, wo_sb, "nn", res=h, a2=b_out, b2=wo_swa, name="att_out")
    return h2, (h, hn, proj_sb, proj_swa, carry, a_out, b_out, sk128, qg, kg), gathered


def _att_bwd(dh2, saved, g, w_in, w_out, ride=()):
    h, hn, proj_sb, proj_swa, carry, a_out, b_out, sk128, qg, kg = saved
    w_sb, w_swa = _att_in_weights(w_in)
    wo_sb, wo_swa = _att_out_weights(w_out)
    da = _mm(dh2, wo_sb, "nt", out_dtype=BF16, name="att_do_sb")
    db = _mm(dh2, wo_swa, "nt", name="att_do_swa")
    dwo_sb = _mm(a_out, dh2, "tn", out_dtype=BF16, name="att_dwout_sb")
    dwo_swa = _mm(b_out, dh2, "tn", out_dtype=BF16, name="att_dwout_swa")
    dw_out = jnp.concatenate([_tile_halves(dwo_sb.T)[1].T, dwo_swa], axis=0)
    dq, dkv, rode = _sb_bwd(proj_sb, carry, da, ride)
    dbq, dbkv, dqg, dkg, dsink = _swa_bwd(proj_swa, qg, kg, sk128, _alibi(), db)
    dproj = jnp.concatenate([dq.astype(BF16), dkv.astype(BF16), dbq.astype(BF16), dbkv.astype(BF16)], axis=1)
    w_all = jnp.concatenate([w_sb, w_swa], axis=1)
    dw_all = _mm(hn, dproj, "tn", out_dtype=BF16, name="att_dwin")
    dhn = _mm(dproj, w_all, "nt", name="att_dhn")
    dsq, _ = _tile_halves(dw_all[:, :SB_LANES])
    dsk, dsv = _tile_halves(dw_all[:, SB_LANES:2 * SB_LANES])
    dw_in = jnp.concatenate([dsq, dsk, dsv, dw_all[:, 2 * SB_LANES:]], axis=1)
    dh, dg = _rms_bwd(dhn, h, g, dh2, "att_dnorm")
    return dh, dg, dw_in, dw_out, dqg.reshape(HEAD_DIM), dkg.reshape(HEAD_DIM), dsink[:, 0], rode


CONV_ROWS = 1024
CONV_COLS = 512
HALO = 8


def _shifted(xcat, s, tm):
    if s == 0:
        return xcat[HALO:HALO + tm]
    return pltpu.roll(xcat, s, 0)[HALO:HALO + tm]


def _conv_pre(x_ref, halo_ref, w_ref, i, tm):
    xc = x_ref[...]
    halo = jnp.where(i > 0, halo_ref[...], 0.0)
    xcat = jnp.concatenate([halo, xc], axis=0)
    w = w_ref[...]
    y = w[GDN_CONV - 1:GDN_CONV] * xc
    for kk in range(GDN_CONV - 1):
        y = y + w[kk:kk + 1] * _shifted(xcat, GDN_CONV - 1 - kk, tm)
    return xcat, y


def _l2_heads(s, qscale_of):
    outs, rs = [], []
    for hh in range(s.shape[1] // GDN_HEAD_DIM):
        sh = s[:, hh * GDN_HEAD_DIM:(hh + 1) * GDN_HEAD_DIM]
        r = lax.rsqrt(jnp.sum(sh * sh, axis=-1, keepdims=True) + EPS)
        outs.append(sh * r)
        rs.append(r)
    return outs, rs


def _conv_specs(T, col0, tm, tc):
    cur = pl.BlockSpec((tm, tc), lambda j, i: (i, j + col0 // tc))
    halo = pl.BlockSpec((HALO, tc), lambda j, i: (jnp.maximum(i * (tm // HALO) - 1, 0), j + col0 // tc))
    wsp = pl.BlockSpec((GDN_CONV, tc), lambda j, i: (0, j + col0 // tc))
    out = pl.BlockSpec((tm, tc), lambda j, i: (i, j))
    return cur, halo, wsp, out


def _conv_fwd(proj, conv_w, col0, width, norm, name):
    T = proj.shape[0]
    tm, tc = _pick(T, CONV_ROWS), CONV_COLS
    cur, halo, wsp, out = _conv_specs(T, col0, tm, tc)
    n_q_tiles = (width // 2) // tc

    def body(x_ref, halo_ref, w_ref, o_ref):
        j, i = pl.program_id(0), pl.program_id(1)
        _, y = _conv_pre(x_ref, halo_ref, w_ref, i, tm)
        s = y * _sigmoid(y)
        if norm:
            outs, _ = _l2_heads(s, None)
            qs = jnp.where(j < n_q_tiles, GDN_HEAD_DIM ** -0.5, 1.0)
            o_ref[...] = jnp.concatenate(outs, axis=1) * qs
        else:
            o_ref[...] = s

    return _pcall(body, name=name, grid=(width // tc, T // tm), in_specs=[cur, halo, wsp], out_specs=out,
                  out_shape=jax.ShapeDtypeStruct((T, width), F32),
                  compiler_params=_params(("parallel", "parallel")))(proj, proj, conv_w)


def _conv_bwd_pre(proj, conv_w, dout, col0, width, norm, name):
    T = proj.shape[0]
    tm, tc = _pick(T, CONV_ROWS), CONV_COLS
    cur, halo, wsp, out = _conv_specs(T, col0, tm, tc)
    n_q_tiles = (width // 2) // tc

    def body(x_ref, halo_ref, w_ref, d_ref, dy_ref, dw_ref):
        j, i = pl.program_id(0), pl.program_id(1)
        xcat, y = _conv_pre(x_ref, halo_ref, w_ref, i, tm)
        sg = _sigmoid(y)
        s = y * sg
        d = d_ref[...]
        if norm:
            qs = jnp.where(j < n_q_tiles, GDN_HEAD_DIM ** -0.5, 1.0)
            d = d * qs
            outs, rs = _l2_heads(s, None)
            parts = []
            for hh, (nh, r) in enumerate(zip(outs, rs)):
                dh = d[:, hh * GDN_HEAD_DIM:(hh + 1) * GDN_HEAD_DIM]
                parts.append(r * (dh - nh * jnp.sum(dh * nh, axis=-1, keepdims=True)))
            ds = jnp.concatenate(parts, axis=1)
        else:
            ds = d
        dy = ds * sg * (1.0 + y * (1.0 - sg))
        dy_ref[...] = dy
        rows = [jnp.sum(dy * _shifted(xcat, GDN_CONV - 1 - kk, tm), axis=0, keepdims=True) for kk in range(GDN_CONV)]
        part = jnp.concatenate(rows, axis=0)

        @pl.when(i == 0)
        def _():
            dw_ref[...] = part

        @pl.when(i > 0)
        def _():
            dw_ref[...] += part

    wout = pl.BlockSpec((GDN_CONV, tc), lambda j, i: (0, j))
    return _pcall(body, name=name, grid=(width // tc, T // tm), in_specs=[cur, halo, wsp, out], out_specs=[out, wout],
                  out_shape=[jax.ShapeDtypeStruct((T, width), F32), jax.ShapeDtypeStruct((GDN_CONV, width), F32)],
                  compiler_params=_params(("parallel", "arbitrary")))(proj, proj, conv_w, dout)


def _conv_bwd_in(dy, conv_w, name):
    T, C = dy.shape
    tm, tc = _pick(T, CONV_ROWS), CONV_COLS
    nrow = T // tm

    def body(d_ref, nxt_ref, w_ref, dx_ref):
        i = pl.program_id(0)
        dc = d_ref[...]
        nxt = jnp.where(i < nrow - 1, nxt_ref[...], 0.0)
        dcat = jnp.concatenate([dc, nxt], axis=0)
        w = w_ref[...]
        dx = w[GDN_CONV - 1:GDN_CONV] * dc
        for kk in range(GDN_CONV - 1):
            s = GDN_CONV - 1 - kk
            dx = dx + w[kk:kk + 1] * pltpu.roll(dcat, tm + HALO - s, 0)[:tm]
        dx_ref[...] = dx.astype(BF16)

    cur = pl.BlockSpec((tm, tc), lambda i, j: (i, j))
    nxt = pl.BlockSpec((HALO, tc), lambda i, j: (jnp.minimum((i + 1) * (tm // HALO), T // HALO - 1), j))
    wsp = pl.BlockSpec((GDN_CONV, tc), lambda i, j: (0, j))
    return _pcall(body, name=name, grid=(nrow, C // tc), in_specs=[cur, nxt, wsp], out_specs=cur,
                  out_shape=jax.ShapeDtypeStruct((T, C), BF16),
                  compiler_params=_params(("parallel", "parallel")))(dy, dy, conv_w)


GATE_ROWS = 512


def _chunk_mask(n, lower):
    row = _iota2((n, n), 0)
    col = _iota2((n, n), 1)
    same = (row // GDN_CHUNK) == (col // GDN_CHUNK)
    tri = (row >= col) if lower else (row <= col)
    return (same & tri).astype(BF16)


def _gates_fwd(proj, a_log, dt_bias):
    T = proj.shape[0]
    tm = _pick(T, GATE_ROWS)
    c0 = (GDN_CONV_W + GDN_VW) // LANE

    def body(bl_ref, a_ref, alog_ref, dt_ref, beta_ref, g_ref, gc_ref):
        beta_ref[...] = _sigmoid(bl_ref[...])
        g = -jnp.exp(alog_ref[...]) * _softplus(a_ref[...] + dt_ref[...])
        g_ref[...] = g
        gc_ref[...] = _mdot2(_chunk_mask(tm, True), g)

    blk = lambda c: pl.BlockSpec((tm, LANE), lambda i: (i, c))
    vec = pl.BlockSpec((1, LANE), lambda i: (0, 0))
    sh = jax.ShapeDtypeStruct((T, LANE), F32)
    return _pcall(body, name="gdn_gates", grid=(T // tm,), in_specs=[blk(c0), blk(c0 + 1), vec, vec],
                  out_specs=[blk(0), blk(0), blk(0)], out_shape=[sh, sh, sh],
                  compiler_params=_params(("parallel",)))(proj, proj, a_log, dt_bias)


def _gates_bwd(proj, a_log, dt_bias, beta, g, dbeta, dgc):
    T = proj.shape[0]
    tm = _pick(T, GATE_ROWS)
    c0 = (GDN_CONV_W + GDN_VW) // LANE

    def heads_in_lanes(ref):
        lane = _iota2((tm, LANE), 1)
        out = jnp.where(lane < GDN_GROUP, ref[0], 0.0)
        for grp in range(1, GDN_V_HEADS // GDN_GROUP):
            out = out + jnp.where(lane // GDN_GROUP == grp, pltpu.roll(ref[grp], grp * GDN_GROUP, 1), 0.0)
        return out

    def body(a_ref, alog_ref, dt_ref, beta_ref, g_ref, dbeta_ref, dgc_ref, dbl_ref, da_ref, dalog_ref, ddt_ref):
        dg = _mdot2(_chunk_mask(tm, False), heads_in_lanes(dgc_ref))
        b = beta_ref[...]
        dbl_ref[...] = (heads_in_lanes(dbeta_ref) * b * (1.0 - b)).astype(BF16)
        da = dg * (-jnp.exp(alog_ref[...])) * _sigmoid(a_ref[...] + dt_ref[...])
        da_ref[...] = da.astype(BF16)
        p1 = jnp.sum(dg * g_ref[...], axis=0, keepdims=True)
        p2 = jnp.sum(da, axis=0, keepdims=True)

        @pl.when(pl.program_id(0) == 0)
        def _():
            dalog_ref[...] = p1
            ddt_ref[...] = p2

        @pl.when(pl.program_id(0) > 0)
        def _():
            dalog_ref[...] += p1
            ddt_ref[...] += p2

    blk = lambda c: pl.BlockSpec((tm, LANE), lambda i: (i, c))
    vec = pl.BlockSpec((1, LANE), lambda i: (0, 0))
    grp = pl.BlockSpec((GDN_V_HEADS // GDN_GROUP, tm, LANE), lambda i: (0, i, 0))
    shb = jax.ShapeDtypeStruct((T, LANE), BF16)
    shv = jax.ShapeDtypeStruct((1, LANE), F32)
    return _pcall(body, name="gdn_dgates", grid=(T // tm,),
                  in_specs=[blk(c0 + 1), vec, vec, blk(0), blk(0), grp, grp],
                  out_specs=[blk(0), blk(0), vec, vec], out_shape=[shb, shb, shv, shv],
                  compiler_params=_params(("arbitrary",)))(proj, a_log, dt_bias, beta, g, dbeta, dgc)


def _inv_unit_lower(Ls):
    C = Ls[0].shape[0]
    row = _iota2((C, C), 0)
    col = _iota2((C, C), 1)
    blk16 = (row // 16) == (col // 16)
    blk32 = (row // 32) == (col // 32)
    eye = (row == col).astype(F32)
    xs = [-jnp.where(blk16, L, 0.0) for L in Ls]
    inv = [eye + x for x in xs]
    for _ in range(3):
        xs = [_dot3(x, x) for x in xs]
        inv = [a + _dot3(a, x) for a, x in zip(inv, xs)]
    for mask in (blk32 & ~blk16, ~blk32):
        t = [_dot3(a, jnp.where(mask, L, 0.0)) for a, L in zip(inv, Ls)]
        inv = [a - _dot3(ti, a) for a, ti in zip(inv, t)]
    return inv


GDN_GROUP = 4
GDN_PREP_CHUNKS = 8
GDN_STEP_CHUNKS = 2


def _gdn_specs(T):
    C, D, E, J = GDN_CHUNK, GDN_HEAD_DIM, GDN_GROUP, GDN_STEP_CHUNKS
    n = T // (C * J)
    qk = pl.BlockSpec((J * C, (E // 2) * D), lambda h, i: (i, h))
    vE = pl.BlockSpec((J * C, E * D), lambda h, i: (i, h))
    colv = pl.BlockSpec((J * C, LANE), lambda h, i: (i, 0))
    colo = pl.BlockSpec((None, J * C, LANE), lambda h, i: (h, i, 0))
    rowv = pl.BlockSpec((E, J, 1, C), lambda h, i: (h, i, 0, 0))
    st = pl.BlockSpec((E, J, D, D), lambda h, i: (h, i, 0, 0))
    am = pl.BlockSpec((E, J, C, C), lambda h, i: (h, i, 0, 0))
    return n, qk, vE, colv, colo, rowv, st, am


def _lane_col(blk, lane):
    return jnp.sum(jnp.where(_iota2(blk.shape, 1) == lane, blk, 0.0), axis=1, keepdims=True)


def _gdn_decay(gcol, grow):
    C = GDN_CHUNK
    row = _iota2((C, C), 0)
    col = _iota2((C, C), 1)
    incl = row >= col
    dm = jnp.where(incl, jnp.exp(jnp.where(incl, gcol - grow, 0.0)), 0.0)
    glast = grow[:, C - 1:C]
    return dm, jnp.exp(gcol), jnp.exp(glast), jnp.exp(glast - gcol), row > col, incl


def _gdn_prep(k, beta, gcol, grow):
    T = k.shape[0]
    C, D, B = GDN_CHUNK, GDN_HEAD_DIM, GDN_PREP_CHUNKS
    n = T // C

    def body(k_ref, b_ref, gc_ref, gr_ref, a_ref):
        idx = [(e, cb) for e in range(2) for cb in range(B)]
        kc = {cb: k_ref[cb * C:(cb + 1) * C, :] for cb in range(B)}
        lm = []
        head0 = 2 * pl.program_id(0)
        for e, cb in idx:
            beta = _lane_col(b_ref[cb * C:(cb + 1) * C, :], head0 + e)
            dm, _, _, _, strict, _ = _gdn_decay(_lane_col(gc_ref[cb * C:(cb + 1) * C, :], head0 + e), gr_ref[e, cb])
            lm.append(jnp.where(strict, _bdot(kc[cb] * beta, kc[cb], NT) * dm, 0.0))
        inv = _inv_unit_lower(lm)
        for (e, cb), a in zip(idx, inv):
            a_ref[e, cb] = a

    return _pcall(
        body, name="gdn_prep", grid=(GDN_K_HEADS, n // B),
        in_specs=[pl.BlockSpec((B * C, D), lambda h, i: (i, h)), pl.BlockSpec((B * C, LANE), lambda h, i: (i, 0)),
                  pl.BlockSpec((B * C, LANE), lambda h, i: (i, 0)), pl.BlockSpec((2, B, 1, C), lambda h, i: (h, i, 0, 0))],
        out_specs=pl.BlockSpec((2, B, C, C), lambda h, i: (h, i, 0, 0)),
        out_shape=jax.ShapeDtypeStruct((GDN_V_HEADS, n, C, C), F32),
        compiler_params=_params(("parallel", "parallel")),
    )(k, beta, gcol, grow)


def _gdn_fwd(q, k, v, beta, gcol, grow, amat):
    T = q.shape[0]
    C, D, E = GDN_CHUNK, GDN_HEAD_DIM, GDN_GROUP
    n, qk, vE, colv, colo, rowv, st, am = _gdn_specs(T)
    R = range(E)

    def body(q_ref, k_ref, v_ref, b_ref, gc_ref, gr_ref, a_ref, o_ref, s_ref, vn_ref, state):
        @pl.when(pl.program_id(1) == 0)
        def _():
            state[...] = jnp.zeros_like(state)

        head0 = E * pl.program_id(0)
        s = [state[e] for e in R]
        for cc in range(GDN_STEP_CHUNKS):
            rows = slice(cc * C, (cc + 1) * C)
            qv = [q_ref[rows, (e // 2) * D:(e // 2 + 1) * D] for e in R]
            kv = [k_ref[rows, (e // 2) * D:(e // 2 + 1) * D] for e in R]
            vv = [v_ref[rows, e * D:(e + 1) * D] for e in R]
            beta = [_lane_col(b_ref[rows, :], head0 + e) for e in R]
            a = [a_ref[e, cc] for e in R]
            dec = [_gdn_decay(_lane_col(gc_ref[rows, :], head0 + e), gr_ref[e, cc]) for e in R]
            pm = [_bdot(qv[e], kv[e], NT) * dec[e][0] for e in R]
            r = [beta[e] * (vv[e] - _bdot(kv[e] * dec[e][1], s[e])) for e in R]
            vn = [_dot3(a[e], r[e]) for e in R]
            o = [_bdot(qv[e] * dec[e][1], s[e]) + _bdot(pm[e], vn[e]) for e in R]
            s2 = [dec[e][2] * s[e] + _bdot(kv[e] * dec[e][3], vn[e], TN) for e in R]
            for e in R:
                s_ref[e, cc] = s[e]
                vn_ref[rows, e * D:(e + 1) * D] = vn[e]
                o_ref[rows, e * D:(e + 1) * D] = o[e]
            s = s2
        for e in R:
            state[e] = s[e]

    shv = jax.ShapeDtypeStruct((T, GDN_V_HEADS * D), F32)
    return _pcall(
        body, name="gdn_fwd", grid=(GDN_V_HEADS // E, n), in_specs=[qk, qk, vE, colv, colv, rowv, am],
        out_specs=[vE, st, vE],
        out_shape=[shv, jax.ShapeDtypeStruct((GDN_V_HEADS, T // C, D, D), F32), shv],
        scratch_shapes=[pltpu.VMEM((E, D, D), F32)],
        compiler_params=_params(("parallel", "arbitrary")),
    )(q, k, v, beta, gcol, grow, amat)


def _gdn_bwd(q, k, v, beta, gcol, grow, states, amat, vnew, do):
    T = q.shape[0]
    C, D, E = GDN_CHUNK, GDN_HEAD_DIM, GDN_GROUP
    n, qk, vE, colv, colo, rowv, st, am = _gdn_specs(T)
    rev = lambda spec: pl.BlockSpec(spec.block_shape, (lambda f: (lambda h, i: f(h, n - 1 - i)))(spec.index_map))
    qk, vE, colv, colo, rowv, st, am = (rev(s) for s in (qk, vE, colv, colo, rowv, st, am))
    R = range(E)

    def body(q_ref, k_ref, v_ref, b_ref, gc_ref, gr_ref, s_ref, a_ref, vn_ref, do_ref,
             dq_ref, dk_ref, dv_ref, db_ref, dgc_ref, dstate):
        @pl.when(pl.program_id(1) == 0)
        def _():
            dstate[...] = jnp.zeros_like(dstate)

        dcur = [dstate[e] for e in R]
        for cc in reversed(range(GDN_STEP_CHUNKS)):
            rows = slice(cc * C, (cc + 1) * C)
            M = lambda f: [f(e) for e in R]
            rsum = lambda x: jnp.sum(x, axis=1, keepdims=True)
            qv = M(lambda e: q_ref[rows, (e // 2) * D:(e // 2 + 1) * D])
            kv = M(lambda e: k_ref[rows, (e // 2) * D:(e // 2 + 1) * D])
            vv = M(lambda e: v_ref[rows, e * D:(e + 1) * D])
            vn = M(lambda e: vn_ref[rows, e * D:(e + 1) * D])
            dov = M(lambda e: do_ref[rows, e * D:(e + 1) * D])
            head0 = E * pl.program_id(0)
            beta = M(lambda e: _lane_col(b_ref[rows, :], head0 + e))
            s = M(lambda e: s_ref[e, cc])
            a = M(lambda e: a_ref[e, cc])
            dsn = dcur
            dec = M(lambda e: _gdn_decay(_lane_col(gc_ref[rows, :], head0 + e), gr_ref[e, cc]))
            dm, gam, glast, tail = (M(lambda e: dec[e][i]) for i in range(4))
            strict, incl = dec[0][4], dec[0][5]
            kb = M(lambda e: kv[e] * beta[e])
            kd = M(lambda e: kv[e] * gam[e])
            qd = M(lambda e: qv[e] * gam[e])
            kt = M(lambda e: kv[e] * tail[e])
            lmat = M(lambda e: jnp.where(strict, _bdot(kb[e], kv[e], NT) * dm[e], 0.0))
            pmat = M(lambda e: _bdot(qv[e], kv[e], NT) * dm[e])
            xres = M(lambda e: vv[e] - _bdot(kd[e], s[e]))
            dvn = M(lambda e: _bdot(pmat[e], dov[e], TN) + _bdot(kt[e], dsn[e]))
            dqd = M(lambda e: _bdot(dov[e], s[e], NT))
            dp = M(lambda e: jnp.where(incl, _bdot(dov[e], vn[e], NT), 0.0))
            dkt = M(lambda e: _bdot(vn[e], dsn[e], NT))
            dr = M(lambda e: _dot3(a[e], dvn[e], TN))
            drb = M(lambda e: beta[e] * dr[e])
            dkd = M(lambda e: -_bdot(drb[e], s[e], NT))
            ds2 = M(lambda e: _bdot(qd[e], dov[e], TN) + glast[e] * dsn[e] - _bdot(kd[e], drb[e], TN))
            dl = M(lambda e: -jnp.where(strict, _bdot(dr[e], vn[e], NT), 0.0))
            dmm = M(lambda e: dl[e] * dm[e])
            dnn = M(lambda e: dp[e] * dm[e])
            emat = M(lambda e: dl[e] * lmat[e] + dp[e] * pmat[e])
            dkb = M(lambda e: _bdot(dmm[e], kv[e]))
            dk = M(lambda e: beta[e] * dkb[e] + _bdot(dmm[e], kb[e], TN) + _bdot(dnn[e], qv[e], TN)
                   + gam[e] * dkd[e] + tail[e] * dkt[e])
            dq = M(lambda e: _bdot(dnn[e], kv[e]) + gam[e] * dqd[e])
            dbeta = M(lambda e: rsum(dr[e] * xres[e]) + rsum(dkb[e] * kv[e]))
            ones = jnp.ones((C, LANE), BF16)
            colsum = M(lambda e: _dot2m(emat[e], ones, TN)[:, :1])
            tails = M(lambda e: rsum(dkt[e] * kt[e]))
            lastrow = _iota2((C, 1), 0) == C - 1
            dlast = M(lambda e: jnp.sum(tails[e], axis=0, keepdims=True)
                      + glast[e] * jnp.sum(rsum(s[e] * dsn[e]), axis=0, keepdims=True))
            dgc = M(lambda e: rsum(emat[e]) - colsum[e] + rsum(dkd[e] * kd[e]) + rsum(dqd[e] * qd[e]) - tails[e]
                    + jnp.where(lastrow, dlast[e], 0.0))
            lane = _iota2((C, LANE), 1)
            db_all = jnp.zeros((C, LANE), F32)
            dgc_all = jnp.zeros((C, LANE), F32)
            for e in R:
                dv_ref[rows, e * D:(e + 1) * D] = drb[e]
                db_all = jnp.where(lane == e, dbeta[e], db_all)
                dgc_all = jnp.where(lane == e, dgc[e], dgc_all)
            db_ref[rows, :] = db_all
            dgc_ref[rows, :] = dgc_all
            for kh in range(E // 2):
                dq_ref[rows, kh * D:(kh + 1) * D] = dq[2 * kh] + dq[2 * kh + 1]
                dk_ref[rows, kh * D:(kh + 1) * D] = dk[2 * kh] + dk[2 * kh + 1]

            dcur = ds2
        for e in R:
            dstate[e] = dcur[e]

    shq = jax.ShapeDtypeStruct((T, GDN_K_HEADS * D), F32)
    shv = jax.ShapeDtypeStruct((T, GDN_V_HEADS * D), F32)
    shc = jax.ShapeDtypeStruct((GDN_V_HEADS // E, T, LANE), F32)
    return _pcall(
        body, name="gdn_bwd", grid=(GDN_V_HEADS // E, n),
        in_specs=[qk, qk, vE, colv, colv, rowv, st, am, vE, vE],
        out_specs=[qk, qk, vE, colo, colo], out_shape=[shq, shq, shv, shc, shc],
        scratch_shapes=[pltpu.VMEM((E, D, D), F32)],
        compiler_params=_params(("parallel", "arbitrary")),
    )(q, k, v, beta, gcol, grow, states, amat, vnew, do)


def _outgate_fwd(o, proj, gain):
    T = o.shape[0]
    tm, tc = _pick(T, CONV_ROWS), CONV_COLS
    z0 = GDN_CONV_W // tc

    def body(o_ref, z_ref, g_ref, y_ref):
        z = z_ref[...]
        sz = z * _sigmoid(z)
        parts = []
        for hh in range(tc // GDN_HEAD_DIM):
            oh = o_ref[:, hh * GDN_HEAD_DIM:(hh + 1) * GDN_HEAD_DIM]
            r = lax.rsqrt(jnp.mean(oh * oh, axis=-1, keepdims=True) + EPS)
            parts.append(oh * r * g_ref[...])
        y_ref[...] = (jnp.concatenate(parts, axis=1) * sz).astype(BF16)

    blk = pl.BlockSpec((tm, tc), lambda i, j: (i, j))
    return _pcall(body, name="gdn_outgate", grid=(T // tm, GDN_VW // tc),
                  in_specs=[blk, pl.BlockSpec((tm, tc), lambda i, j: (i, j + z0)), pl.BlockSpec((1, GDN_HEAD_DIM), lambda i, j: (0, 0))],
                  out_specs=blk, out_shape=jax.ShapeDtypeStruct((T, GDN_VW), BF16),
                  compiler_params=_params(("parallel", "parallel")))(o, proj, gain)


def _outgate_bwd(dy, o, proj, gain):
    T = o.shape[0]
    tm, tc = _pick(T, CONV_ROWS), CONV_COLS
    z0 = GDN_CONV_W // tc
    nh = tc // GDN_HEAD_DIM

    def body(dy_ref, o_ref, z_ref, g_ref, do_ref, dz_ref, dg_ref):
        z = z_ref[...]
        sg = _sigmoid(z)
        sz = z * sg
        dy = dy_ref[...]
        dgain = jnp.zeros((1, GDN_HEAD_DIM), F32)
        dos, ys = [], []
        for hh in range(nh):
            sl = slice(hh * GDN_HEAD_DIM, (hh + 1) * GDN_HEAD_DIM)
            oh = o_ref[:, sl]
            r = lax.rsqrt(jnp.mean(oh * oh, axis=-1, keepdims=True) + EPS)
            xh = oh * r
            dn = dy[:, sl] * sz[:, sl]
            dgain = dgain + jnp.sum(dn * xh, axis=0, keepdims=True)
            dxh = dn * g_ref[...]
            dos.append(r * (dxh - xh * jnp.mean(dxh * xh, axis=-1, keepdims=True)))
            ys.append(xh * g_ref[...])
        do_ref[...] = jnp.concatenate(dos, axis=1)
        dz_ref[...] = (dy * jnp.concatenate(ys, axis=1) * sg * (1.0 + z * (1.0 - sg))).astype(BF16)
        first = (pl.program_id(0) == 0) & (pl.program_id(1) == 0)

        @pl.when(first)
        def _():
            dg_ref[...] = dgain

        @pl.when(jnp.logical_not(first))
        def _():
            dg_ref[...] += dgain

    blk = pl.BlockSpec((tm, tc), lambda i, j: (i, j))
    vec = pl.BlockSpec((1, GDN_HEAD_DIM), lambda i, j: (0, 0))
    return _pcall(body, name="gdn_doutgate", grid=(T // tm, GDN_VW // tc),
                  in_specs=[blk, blk, pl.BlockSpec((tm, tc), lambda i, j: (i, j + z0)), vec],
                  out_specs=[blk, blk, vec],
                  out_shape=[jax.ShapeDtypeStruct((T, GDN_VW), F32), jax.ShapeDtypeStruct((T, GDN_VW), BF16),
                             jax.ShapeDtypeStruct((1, GDN_HEAD_DIM), F32)],
                  compiler_params=_params(("arbitrary", "arbitrary")))(dy, o, proj, gain)


def _pad_lanes(vec):
    return jnp.pad(vec.reshape(1, -1), ((0, 0), (0, LANE - vec.shape[-1])))


def _head_rows(a):
    T = a.shape[0]
    return a[:, :GDN_V_HEADS].T.reshape(GDN_V_HEADS, T // GDN_CHUNK, 1, GDN_CHUNK)


def _gdn_pad_in(w_in):
    c = GDN_CONV_W + GDN_VW
    z = jnp.zeros(w_in.shape[:-1] + (LANE - GDN_V_HEADS,), w_in.dtype)
    return jnp.concatenate([w_in[..., :c + GDN_V_HEADS], z, w_in[..., c + GDN_V_HEADS:], z], axis=-1)


def _gdn_unpad_in(dw):
    c = GDN_CONV_W + GDN_VW
    return jnp.concatenate([dw[..., :c + GDN_V_HEADS], dw[..., c + LANE:c + LANE + GDN_V_HEADS]], axis=-1)


def _gdn_mixer_fwd(h, g, w_in_pad, conv_w, a_log, dt_bias, out_gain, w_out):
    T = h.shape[0]
    hn = _rms_fwd(h, g, "gdn_norm")
    proj = _mm(hn, w_in_pad, "nn", name="gdn_in")
    qk = _conv_fwd(proj, conv_w, 0, 2 * GDN_KW, True, "gdn_conv_qk")
    vv = _conv_fwd(proj, conv_w, 2 * GDN_KW, GDN_VW, False, "gdn_conv_v")
    alog, dtb = _pad_lanes(a_log), _pad_lanes(dt_bias)
    beta, gl, gc = _gates_fwd(proj, alog, dtb)
    grow = _head_rows(gc)
    qn, kn = qk[:, :GDN_KW], qk[:, GDN_KW:]
    amat = _gdn_prep(kn, beta, gc, grow)
    o, states, vnew = _gdn_fwd(qn, kn, vv, beta, gc, grow, amat)
    gain = out_gain.reshape(1, GDN_HEAD_DIM)
    y = _outgate_fwd(o, proj, gain)
    h2 = _mm(y, w_out, "nn", res=h, name="gdn_out")
    return h2, (h, hn, proj, qn, kn, vv, beta, gl, gc, grow, o, states, amat, vnew, y, alog, dtb, gain)


def _gdn_mixer_bwd(dh2, saved, g, w_in_pad, conv_w, w_out):
    h, hn, proj, qn, kn, vv, beta, gl, gc, grow, o, states, amat, vnew, y, alog, dtb, gain = saved
    T = h.shape[0]
    dy = _mm(dh2, w_out, "nt", name="gdn_dy")
    dw_out = _mm(y, dh2, "tn", out_dtype=BF16, name="gdn_dwout")
    do, dz, dgain = _outgate_bwd(dy, o, proj, gain)
    dq, dk, dv, dbeta, dgc = _gdn_bwd(qn, kn, vv, beta, gc, grow, states, amat, vnew, do)
    dqk = jnp.concatenate([dq, dk], axis=1)
    dy_qk, dcw_qk = _conv_bwd_pre(proj, conv_w, dqk, 0, 2 * GDN_KW, True, "gdn_dconv_qk")
    dy_v, dcw_v = _conv_bwd_pre(proj, conv_w, dv, 2 * GDN_KW, GDN_VW, False, "gdn_dconv_v")
    dx_qk = _conv_bwd_in(dy_qk, conv_w[:, :2 * GDN_KW], "gdn_dconvin_qk")
    dx_v = _conv_bwd_in(dy_v, conv_w[:, 2 * GDN_KW:], "gdn_dconvin_v")
    dbl, da, dalog, ddt = _gates_bwd(proj, alog, dtb, beta, gl, dbeta, dgc)
    dproj = jnp.concatenate([dx_qk, dx_v, dz, dbl, da], axis=1)
    dw_in_pad = _mm(hn, dproj, "tn", out_dtype=BF16, name="gdn_dwin")
    dhn = _mm(dproj, w_in_pad, "nt", name="gdn_dhn")
    dh, dg = _rms_bwd(dhn, h, g, dh2, "gdn_dnorm")
    dconv = jnp.concatenate([dcw_qk, dcw_v], axis=1)
    return (dh, dg, _gdn_unpad_in(dw_in_pad), dconv, dalog[0, :GDN_V_HEADS], ddt[0, :GDN_V_HEADS],
            dgain.reshape(GDN_HEAD_DIM), dw_out)


def _instances(full):
    out = {}
    for n, a in full.items():
        if n.startswith("ffn_"):
            for i in range(2):
                for j in range(2):
                    out[(n, i, j)] = a[i, j]
        elif n in ("mix_norm", "ple_norm", "ple_w_gate", "ple_w_proj"):
            for i in range(2):
                out[(n, i)] = a[i]
        else:
            out[(n,)] = a[0]
    return out


def _stacked(inst):
    out = {}
    for n in dict.fromkeys(k[0] for k in inst):
        if n.startswith("ffn_"):
            out[n] = jnp.stack([jnp.stack([inst[(n, i, j)] for j in range(2)]) for i in range(2)])
        elif n in ("mix_norm", "ple_norm", "ple_w_gate", "ple_w_proj"):
            out[n] = jnp.stack([inst[(n, i)] for i in range(2)])
        else:
            out[n] = inst[(n,)][None]
    return out


def _local_step(x, p, target, w, late_shards=(), late_weights=None, early_grads=None, first_shards=(), first_weights=None):
    w = dict(w)
    ffn = lambda i, j: (w[("ffn_norm", i, j)], w[("ffn_w_gate", i, j)], w[("ffn_w_up", i, j)], w[("ffn_w_down", i, j)])
    h = x
    tape = []
    for i in range(2):
        if i == 0 and first_weights is not None:
            def wd_of(gathered):
                w.update(first_weights(gathered))
                return w[("ffn_w_down", 0, 0)]
            h, s1 = _ffn_fwd(h, w[("ffn_norm", 0, 0)], w[("ffn_w_gate", 0, 0)], w[("ffn_w_up", 0, 0)], None, "ffn0a",
                             first_shards, wd_of)
        else:
            h, s1 = _ffn_fwd(h, *ffn(i, 0), f"ffn{i}a")
        if i == 0:
            def w_out_of(gathered):
                if late_weights is not None:
                    w.update(late_weights(gathered))
                return w[("att_w_out",)]
            h, s2, _ = _att_fwd(h, w[("mix_norm", 0)], w[("att_w_in",)], w_out_of, w[("att_q_norm",)],
                                w[("att_k_norm",)], w[("att_sinks",)], late_shards)
        else:
            gdn_in_pad = _gdn_pad_in(w[("gdn_w_in",)])
            h, s2 = _gdn_mixer_fwd(h, w[("mix_norm", 1)], gdn_in_pad, w[("gdn_conv_w",)], w[("gdn_a_log",)],
                                   w[("gdn_dt_bias",)], w[("gdn_out_norm",)], w[("gdn_w_out",)])
        h, s3 = _ffn_fwd(h, *ffn(i, 1), f"ffn{i}b")
        h, s4 = _ple_fwd(h, p[i], w[("ple_norm", i)], w[("ple_w_gate", i)], w[("ple_w_proj", i)], f"ple{i}")
        tape.append((s1, s2, s3, s4))

    loss, dh = _loss_head(h, target)

    g = {}
    rode = []
    for i in (1, 0):
        s1, s2, s3, s4 = tape[i]
        dh, g[("ple_norm", i)], g[("ple_w_gate", i)], g[("ple_w_proj", i)] = _ple_bwd(
            dh, s4, p[i], w[("ple_norm", i)], w[("ple_w_gate", i)], f"ple{i}")
        dh, g[("ffn_norm", i, 1)], g[("ffn_w_gate", i, 1)], g[("ffn_w_up", i, 1)], g[("ffn_w_down", i, 1)] = _ffn_bwd(
            dh, s3, *ffn(i, 1), f"ffn{i}b")
        if i == 0:
            ride = early_grads(g) if early_grads is not None else ()
            (dh, g[("mix_norm", 0)], g[("att_w_in",)], g[("att_w_out",)], g[("att_q_norm",)], g[("att_k_norm",)],
             g[("att_sinks",)], rode) = _att_bwd(dh, s2, w[("mix_norm", 0)], w[("att_w_in",)], w[("att_w_out",)], ride)
        else:
            (dh, g[("mix_norm", 1)], g[("gdn_w_in",)], g[("gdn_conv_w",)], g[("gdn_a_log",)], g[("gdn_dt_bias",)],
             g[("gdn_out_norm",)], g[("gdn_w_out",)]) = _gdn_mixer_bwd(
                dh, s2, w[("mix_norm", 1)], gdn_in_pad, w[("gdn_conv_w",)], w[("gdn_w_out",)])
        dh, g[("ffn_norm", i, 0)], g[("ffn_w_gate", i, 0)], g[("ffn_w_up", i, 0)], g[("ffn_w_down", i, 0)] = _ffn_bwd(
            dh, s1, *ffn(i, 0), f"ffn{i}a")
    return loss, dh, g, rode


MESH = pl.DeviceIdType.MESH


def _place():
    x, y, c = lax.axis_index("x"), lax.axis_index("y"), lax.axis_index("c")
    others = [((1 - x, y), 2 * (1 - x) + y), ((x, 1 - y), 2 * x + (1 - y)), ((1 - x, 1 - y), 2 * (1 - x) + (1 - y))]
    return x, y, c, 4 * x + 2 * y + c, 2 * x + y, (x, y, 1 - c), others


def _comm_call(body, arrays, out_shape, n_sems, name):
    hbm = pl.BlockSpec(memory_space=pl.ANY)
    n = len(arrays)
    return _pcall(
        body, name=name, in_specs=[hbm] * n, out_specs=[hbm] * len(out_shape), out_shape=out_shape,
        scratch_shapes=[pltpu.SemaphoreType.DMA((n, n_sems)), pltpu.SemaphoreType.DMA((n, n_sems)),
                        pltpu.SemaphoreType.DMA((n, N_CHIP))],
        compiler_params=pltpu.CompilerParams(has_side_effects=True),
    )(*arrays)


def _gather_protocol(ins, outs, send_sems, recv_sems, local_sems):
    n = len(ins)
    x, y, c, me, my_chip, sibling, others = _place()

    def copy(a, k, block, to, src=None):
        dst = outs[a].at[block]
        return pltpu.make_async_remote_copy(
            src_ref=dst if src is None else src, dst_ref=dst, send_sem=send_sems.at[a, k],
            recv_sem=recv_sems.at[a, k], device_id=to, device_id_type=MESH)

    local = [pltpu.make_async_copy(ins[a], outs[a].at[me], local_sems.at[a, 0]) for a in range(n)]
    first = []
    for a in range(n):
        first.append(copy(a, 0, me, sibling, src=ins[a]))
        first += [copy(a, 1 + j, me, (*chip, c), src=ins[a]) for j, (chip, _) in enumerate(others)]

    def start():
        for cp in local + first:
            cp.start()

    def finish():
        passed = []
        for a in range(n):
            for j, (chip, chip_idx) in enumerate(others):
                blk = 2 * chip_idx + c
                copy(a, 1 + j, blk, (x, y, c)).wait_recv()
                fwd = copy(a, 4 + j, blk, sibling)
                fwd.start()
                passed.append(fwd)
        for a in range(n):
            copy(a, 0, 2 * my_chip + (1 - c), (x, y, c)).wait_recv()
            for j, (chip, chip_idx) in enumerate(others):
                copy(a, 4 + j, 2 * chip_idx + (1 - c), (x, y, c)).wait_recv()
        for cp in first + passed:
            cp.wait_send()
        for cp in local:
            cp.wait()

    return start, finish


def _all_gather(arrays):
    n = len(arrays)

    def body(*refs):
        start, finish = _gather_protocol(refs[:n], refs[n:2 * n], *refs[2 * n:])
        start()
        finish()

    out_shape = [jax.ShapeDtypeStruct((N_DEV,) + a.shape, a.dtype) for a in arrays]
    return _comm_call(body, arrays, out_shape, N_DEV - 1, "gather_weights")


def _exchange_sibling(arrays, name):
    n = len(arrays)

    def body(*refs):
        ins, got = refs[:n], refs[n:2 * n]
        send_sems, recv_sems, _ = refs[2 * n:]
        x, y, c, me, my_chip, sibling, others = _place()
        remote = []
        for a in range(n):
            for chip in range(N_CHIP):
                rc = pltpu.make_async_remote_copy(
                    src_ref=ins[a].at[2 * chip + (1 - c)], dst_ref=got[a].at[chip], send_sem=send_sems.at[a, chip],
                    recv_sem=recv_sems.at[a, chip], device_id=sibling, device_id_type=MESH)
                rc.start()
                remote.append(rc)
        for rc in remote:
            rc.wait()

    half = [jax.ShapeDtypeStruct((N_CHIP,) + a.shape[1:], a.dtype) for a in arrays]
    return _comm_call(body, arrays, half, N_CHIP, name)


def _chips_protocol(ins, outs, send_sems, recv_sems, local_sems):
    n = len(ins)
    x, y, c, me, my_chip, sibling, others = _place()
    local = [pltpu.make_async_copy(ins[a].at[my_chip], outs[a].at[my_chip], local_sems.at[a, 0]) for a in range(n)]
    remote = [pltpu.make_async_remote_copy(
        src_ref=ins[a].at[chip_idx], dst_ref=outs[a].at[my_chip], send_sem=send_sems.at[a, j],
        recv_sem=recv_sems.at[a, j], device_id=(*chip, c), device_id_type=MESH)
        for a in range(n) for j, (chip, chip_idx) in enumerate(others)]

    def start():
        for cp in local + remote:
            cp.start()

    def finish():
        for cp in remote + local:
            cp.wait()

    return start, finish


def _exchange_chips(arrays, name):
    n = len(arrays)

    def body(*refs):
        start, finish = _chips_protocol(refs[:n], refs[n:2 * n], *refs[2 * n:])
        start()
        finish()

    out_shape = [jax.ShapeDtypeStruct(a.shape, a.dtype) for a in arrays]
    return _comm_call(body, arrays, out_shape, N_CHIP - 1, name)


def _as_rows(a, lead):
    shp = a.shape
    return a.reshape(shp[:lead] + (math.prod(shp[lead:-1]), shp[-1]))


def _row_tile(rows, cap=512):
    if rows <= cap:
        return rows
    for t in range(cap - cap % 8, 0, -8):
        if rows % t == 0:
            return t
    return rows


def _pair_sum(send, got, name):
    a3, b3 = _as_rows(send, 1), _as_rows(got, 1)
    _, rows, last = b3.shape
    tr = _row_tile(rows, 2048)

    def body(c_ref, a_ref, b_ref, o_ref):
        o_ref[...] = (a_ref[...].astype(F32) + b_ref[...].astype(F32)).astype(o_ref.dtype)

    core = lax.axis_index("c").astype(jnp.int32).reshape(1)
    out = _pcall(
        body, name=name,
        grid_spec=pltpu.PrefetchScalarGridSpec(
            num_scalar_prefetch=1, grid=(N_CHIP, rows // tr),
            in_specs=[pl.BlockSpec((None, tr, last), lambda k, i, c_ref: (2 * k + c_ref[0], i, 0)),
                      pl.BlockSpec((None, tr, last), lambda k, i, c_ref: (k, i, 0))],
            out_specs=pl.BlockSpec((None, tr, last), lambda k, i, c_ref: (k, i, 0))),
        out_shape=jax.ShapeDtypeStruct(b3.shape, got.dtype), compiler_params=_params(("parallel", "parallel")),
    )(core, a3, b3)
    return out.reshape(got.shape)


def _adamw(parts, w, m, v, name):
    lead, (rows, last) = w.shape[:-2], w.shape[-2:]
    nl = len(lead)
    tr = _row_tile(rows, 1024)
    c1 = 1.0 / (1.0 - ADAM_B1 ** ADAM_STEP)
    c2 = 1.0 / (1.0 - ADAM_B2 ** ADAM_STEP)

    def body(p_ref, w_ref, m_ref, v_ref, g_ref, d_ref, nm_ref, nv_ref):
        g = p_ref[0].astype(F32)
        for chip in range(1, N_CHIP):
            g = g + p_ref[chip].astype(F32)
        mn = ADAM_B1 * m_ref[...] + (1.0 - ADAM_B1) * g
        vn = ADAM_B2 * v_ref[...] + (1.0 - ADAM_B2) * (g * g)
        g_ref[...] = g
        nm_ref[...] = mn
        nv_ref[...] = vn
        d_ref[...] = -ADAM_LR * ((mn * c1) / (jnp.sqrt(vn * c2) + ADAM_EPS) + ADAM_WD * w_ref[...])

    row = pl.BlockSpec((None,) * nl + (tr, last), lambda *ix: ix + (0,))
    part = pl.BlockSpec((N_CHIP,) + (None,) * nl + (tr, last), lambda *ix: (0,) + ix + (0,))
    sh = jax.ShapeDtypeStruct(w.shape, F32)
    return _pcall(body, name=name, grid=lead + (rows // tr,), in_specs=[part, row, row, row],
                  out_specs=[row, row, row, row], out_shape=[sh, sh, sh, sh],
                  compiler_params=_params(("parallel",) * (nl + 1)))(parts, w, m, v)


def _pack(pieces, row_align):
    rows, offs, r = [], [], 0
    for a in pieces:
        flat = a.reshape(-1)
        nr = -(-flat.shape[0] // PACK_W)
        flat = jnp.pad(flat, (0, nr * PACK_W - flat.shape[0]))
        rows.append(flat.reshape(nr, PACK_W))
        offs.append(r)
        r += nr
    pad = (-r) % row_align
    if pad:
        rows.append(jnp.zeros((pad, PACK_W), pieces[0].dtype))
    return jnp.concatenate(rows, axis=0), offs


def _unpack(flat, offs, shapes):
    out = []
    for off, shp in zip(offs, shapes):
        size = math.prod(shp)
        nr = -(-size // PACK_W)
        out.append(flat[..., off:off + nr, :].reshape(flat.shape[:-2] + (nr * PACK_W,))[..., :size].reshape(flat.shape[:-2] + tuple(shp)))
    return out


def _to_full(gathered, axis):
    z = jnp.moveaxis(gathered, 0, axis)
    shp = list(z.shape)
    return z.reshape(shp[:axis] + [shp[axis] * shp[axis + 1]] + shp[axis + 2:])


def _to_shards(full, axis):
    shp = list(full.shape)
    z = full.reshape(shp[:axis] + [N_DEV, shp[axis] // N_DEV] + shp[axis + 1:])
    return jnp.moveaxis(z, axis, 0)


def kernel(x, p, ffn_norm, ffn_w_gate, ffn_w_up, ffn_w_down, mix_norm, att_w_in, att_q_norm, att_k_norm, att_sinks, att_w_out, gdn_w_in, gdn_conv_w, gdn_a_log, gdn_dt_bias, gdn_out_norm, gdn_w_out, ple_norm, ple_w_gate, ple_w_proj, loss_target, m_ffn_norm, m_ffn_w_gate, m_ffn_w_up, m_ffn_w_down, m_mix_norm, m_att_w_in, m_att_q_norm, m_att_k_norm, m_att_sinks, m_att_w_out, m_gdn_w_in, m_gdn_conv_w, m_gdn_a_log, m_gdn_dt_bias, m_gdn_out_norm, m_gdn_w_out, m_ple_norm, m_ple_w_gate, m_ple_w_proj, v_ffn_norm, v_ffn_w_gate, v_ffn_w_up, v_ffn_w_down, v_mix_norm, v_att_w_in, v_att_q_norm, v_att_k_norm, v_att_sinks, v_att_w_out, v_gdn_w_in, v_gdn_conv_w, v_gdn_a_log, v_gdn_dt_bias, v_gdn_out_norm, v_gdn_w_out, v_ple_norm, v_ple_w_gate, v_ple_w_proj):
    args = dict(locals())
    wts = {n: args[n] for n in WEIGHTS}
    mom = {n: args["m_" + n] for n in WEIGHTS}
    var = {n: args["v_" + n] for n in WEIGHTS}
    axis = dict(SHARDED)
    vecs = [n for n, _ in SHARDED[:SMALL_SHARDED]]
    small = vecs + list(REPLICATED)
    small_shapes = [wts[n].shape for n in small]
    lead = lambda n: 2 if n.startswith("ffn_") else 1

    def stack_of(arrays, name, idxs):
        return jnp.stack([arrays[name][idx] if idx else arrays[name][0] for idx in idxs])

    def full_instances(gathered, group):
        out = {}
        for (name, idxs), g in zip(group, gathered):
            whole = _to_full(g, axis[name] - lead(name) + 1)
            for k, idx in enumerate(idxs):
                out[(name,) + idx] = whole[k]
        return out

    def shard_stacks(g, group):
        return [_to_shards(jnp.stack([g[(name,) + idx] for idx in idxs]), axis[name] - lead(name) + 1)
                for name, idxs in group]

    vec_pack, voffs = _pack([wts[n] for n in vecs], 8)
    early = _all_gather([stack_of(wts, n, idxs).astype(BF16) for n, idxs in EARLY] + [vec_pack])
    w = full_instances(early[:-1], EARLY)
    vec_full = {n: _to_full(piece, axis[n]) for n, piece in
                zip(vecs, _unpack(early[-1], voffs, [wts[n].shape for n in vecs]))}
    w.update(_instances({**vec_full, **{n: wts[n] for n in REPLICATED}}))
    first_shards = [stack_of(wts, n, idxs).astype(BF16) for n, idxs in FIRST]
    late_shards = [stack_of(wts, n, idxs).astype(BF16) for n, idxs in LATE]

    def early_grads(g):
        send = shard_stacks(g, RIDE)
        got = _exchange_sibling(send, "exchange_sibling_early")
        return [_pair_sum(p_, q_, f"pair_sum_early_{i}") for i, (p_, q_) in enumerate(zip(send, got))]

    loss, grad_x, g, rode = _local_step(x[0], p[:, 0], loss_target[0], w, late_shards,
                                        lambda gathered: full_instances(gathered, LATE), early_grads,
                                        first_shards, lambda gathered: full_instances(gathered, FIRST))

    gs = _stacked({k: v for k, v in g.items() if k[0] in small})
    vec_shards = [_to_shards(gs[n], axis[n]) for n in vecs]
    small_send = jnp.stack([_pack([sh[d] for sh in vec_shards] + [gs[n] for n in REPLICATED] + [loss.reshape(1)], 8)[0]
                            for d in range(N_DEV)])
    send = shard_stacks(g, FINAL) + [small_send]
    got = _exchange_sibling(send, "exchange_sibling_final")
    chip_sums = [_pair_sum(p_, q_, f"pair_sum_final_{i}") for i, (p_, q_) in enumerate(zip(send, got))]
    last = _exchange_chips(chip_sums, "exchange_chips_final")

    pieces = {}
    for (name, idxs), part in list(zip(RIDE, rode)) + list(zip(FINAL, last[:-1])):
        for k, idx in enumerate(idxs):
            pieces[(name,) + idx] = part[:, k]
    outs = {}
    for n, _ in SHARDED[SMALL_SHARDED:]:
        if lead(n) == 2:
            part = jnp.stack([jnp.stack([pieces[(n, i, j)] for j in range(2)], axis=1) for i in range(2)], axis=1)
        elif (n, 0) in pieces:
            part = jnp.stack([pieces[(n, i)] for i in range(2)], axis=1)
        else:
            part = pieces[(n,)][:, None]
        outs[n] = _adamw(part, wts[n], mom[n], var[n], f"adamw_{n}")
    filler = [jnp.zeros((1,), F32)]
    small_w, soffs = _pack([wts[n] for n in small] + filler, 8)
    small_m, _ = _pack([mom[n] for n in small] + filler, 8)
    small_v, _ = _pack([var[n] for n in small] + filler, 8)
    small_out = [_unpack(z, soffs, small_shapes + [(1,)]) for z in _adamw(last[-1], small_w, small_m, small_v, "adamw_small")]
    loss = small_out[0][-1][0]
    for i, n in enumerate(small):
        outs[n] = [small_out[k][i] for k in range(4)]
    result = [loss, grad_x[None]]
    for k in range(4):
        result += [outs[n][k] for n in WEIGHTS]
    return tuple(result)
```

```python
import math

import jax
import jax.numpy as jnp
from jax import lax
from jax.experimental import pallas as pl
from jax.experimental.pallas import tpu as pltpu

F32 = jnp.float32
BF16 = jnp.bfloat16

N_DEV = 8
N_CHIP = 4
D_MODEL = 1024
D_FF = 2816
PLE_DIM = 256
HEAD_DIM = 64
SB_HEADS = 8
SWA_HEADS = 8
SWA_KV_HEADS = 2
SWA_GROUP = SWA_HEADS // SWA_KV_HEADS
WINDOW = 128
Q_BLOCK = 128
GDN_K_HEADS = 8
GDN_V_HEADS = 16
GDN_HEAD_DIM = 128
GDN_CONV = 4
GDN_CHUNK = 64
EPS = 1e-6
SB_W = SB_HEADS * HEAD_DIM
SWA_QW = SWA_HEADS * HEAD_DIM
SWA_KVW = SWA_KV_HEADS * HEAD_DIM
ATT_IN = 3 * SB_W + SWA_QW + 2 * SWA_KVW
GDN_KW = GDN_K_HEADS * GDN_HEAD_DIM
GDN_VW = GDN_V_HEADS * GDN_HEAD_DIM
GDN_CONV_W = 2 * GDN_KW + GDN_VW
GDN_IN = GDN_CONV_W + GDN_VW + 2 * GDN_V_HEADS
GDN_IN_PAD = GDN_CONV_W + GDN_VW + 2 * 128

ADAM_LR = 0.001
ADAM_B1 = 0.9
ADAM_B2 = 0.999
ADAM_EPS = 1e-08
ADAM_WD = 0.01
ADAM_STEP = 10

LANE = 128
VMEM_LIMIT = 56 * 1024 * 1024
MM_TILE_BUDGET = 40 * 1024 * 1024
PACK_W = 1024

NN = ((1,), (0,))
NT = ((1,), (1,))
TN = ((0,), (0,))

SHARDED = (
    ("ffn_norm", 2), ("gdn_conv_w", 2),
    ("ffn_w_gate", 3), ("ffn_w_up", 3), ("ffn_w_down", 2), ("att_w_in", 2), ("att_w_out", 1),
    ("gdn_w_in", 2), ("gdn_w_out", 1), ("ple_w_gate", 1), ("ple_w_proj", 2),
)
SMALL_SHARDED = 2
REPLICATED = ("mix_norm", "att_q_norm", "att_k_norm", "att_sinks", "gdn_a_log", "gdn_dt_bias",
              "gdn_out_norm", "ple_norm")
WEIGHTS = ("ffn_norm", "ffn_w_gate", "ffn_w_up", "ffn_w_down", "mix_norm", "att_w_in", "att_q_norm",
           "att_k_norm", "att_sinks", "att_w_out", "gdn_w_in", "gdn_conv_w", "gdn_a_log", "gdn_dt_bias",
           "gdn_out_norm", "gdn_w_out", "ple_norm", "ple_w_gate", "ple_w_proj")


_FFN_REST = [(0, 1), (1, 0), (1, 1)]
EARLY = [("ffn_w_gate", [(0, 0)]), ("ffn_w_up", [(0, 0)])]
FIRST = [("ffn_w_down", [(0, 0)]), ("att_w_in", [()])]
LATE = ([(n, [idx]) for n in ("ffn_w_gate", "ffn_w_up", "ffn_w_down") for idx in _FFN_REST]
        + [("att_w_out", [()]), ("gdn_w_in", [()]), ("gdn_w_out", [()]),
           ("ple_w_gate", [(0,), (1,)]), ("ple_w_proj", [(0,), (1,)])])
RIDE = [e for e in LATE if e[0] != "att_w_out"]
FINAL = EARLY + FIRST + [("att_w_out", [()])]


def _pcall(body, **kw):
    return pl.pallas_call(body, **kw)


def _params(sem=None):
    if sem is None:
        return pltpu.CompilerParams(vmem_limit_bytes=VMEM_LIMIT)
    return pltpu.CompilerParams(dimension_semantics=sem, vmem_limit_bytes=VMEM_LIMIT)


def _ride_specs(ride, out_shapes, n_sems):
    hbm = pl.BlockSpec(memory_space=pl.ANY)
    n = len(ride)
    sems = [pltpu.SemaphoreType.DMA((n, n_sems)), pltpu.SemaphoreType.DMA((n, n_sems)),
            pltpu.SemaphoreType.DMA((n, N_CHIP))] if n else []
    return [hbm] * n, [hbm] * len(out_shapes), sems


def _dot(a, b, dims=NN):
    return lax.dot_general(a, b, (dims, ((), ())), preferred_element_type=F32)


def _bdot(a, b, dims=NN):
    return _dot(a.astype(BF16), b.astype(BF16), dims)


def _split(a):
    hi = a.astype(BF16)
    lo = (a - hi.astype(F32)).astype(BF16)
    return hi, lo


def _dot3(a, b, dims=NN):
    ah, al = _split(a)
    bh, bl = _split(b)
    return _dot(ah, bh, dims) + (_dot(ah, bl, dims) + _dot(al, bh, dims))


def _dot2m(a, m, dims=NN):
    ah, al = _split(a)
    return _dot(ah, m, dims) + _dot(al, m, dims)


def _mdot2(m, a, dims=NN):
    ah, al = _split(a)
    return _dot(m, ah, dims) + _dot(m, al, dims)


def _sigmoid(x):
    return 1.0 / (1.0 + jnp.exp(-x))


def _softplus(x):
    return jnp.maximum(x, 0.0) + jnp.log(1.0 + jnp.exp(-jnp.abs(x)))


def _pick(n, cap):
    if n <= cap:
        return n
    for t in range(cap - cap % LANE, 0, -LANE):
        if n % t == 0:
            return t
    raise ValueError(f"no tile for {n} under {cap}")


def _iota2(shape, axis):
    return lax.broadcasted_iota(jnp.int32, shape, axis)


def _mm(a, b, mode, out_dtype=F32, res=None, alpha=1.0, a2=None, b2=None, name="mm"):
    if mode == "nn":
        (M, K), N = a.shape, b.shape[1]
    elif mode == "nt":
        (M, K), N = a.shape, b.shape[0]
    else:
        (K, M), N = a.shape, b.shape[1]
    tn, tk = _pick(N, 1408), _pick(K, 2048 if mode == "tn" else 1408)
    nk = K // tk
    pairs = 1 if a2 is None else 2

    def tile_bytes(tm):
        per = pairs * tk * (tm * a.dtype.itemsize + tn * b.dtype.itemsize) + tm * tn * jnp.dtype(out_dtype).itemsize
        return 2 * (per + (tm * tn * 4 if res is not None else 0)) + (tm * tn * 4 if nk > 1 else 0)

    tm = next(t for t in (_pick(M, c) for c in ((1408,) if mode == "tn" else (2048, 1024, 512))) if tile_bytes(t) <= MM_TILE_BUDGET or t <= 512)
    dims = {"nn": NN, "nt": NT, "tn": TN}[mode]
    a_spec = pl.BlockSpec((tk, tm), lambda i, j, k: (k, i)) if mode == "tn" else pl.BlockSpec((tm, tk), lambda i, j, k: (i, k))
    b_spec = pl.BlockSpec((tn, tk), lambda i, j, k: (j, k)) if mode == "nt" else pl.BlockSpec((tk, tn), lambda i, j, k: (k, j))
    o_spec = pl.BlockSpec((tm, tn), lambda i, j, k: (i, j))
    two = a2 is not None
    has_res = res is not None
    a2_spec, b2_spec = a_spec, b_spec
    if two and a2.shape != a.shape:
        assert nk == 1 and mode == "nn" and a2.shape[0] == M and b2.shape[1] == N
        a2_spec = pl.BlockSpec((tm, a2.shape[1]), lambda i, j, k: (i, 0))
        b2_spec = pl.BlockSpec((a2.shape[1], tn), lambda i, j, k: (0, j))

    def body(*refs):
        refs = list(refs)
        a_ref, b_ref = refs[0], refs[1]
        pos = 2
        if two:
            a2_ref, b2_ref = refs[2], refs[3]
            pos = 4
        if has_res:
            res_ref = refs[pos]
            pos += 1
        o_ref, acc_ref = refs[pos], refs[pos + 1]
        k = pl.program_id(2)
        part = _bdot(a_ref[...], b_ref[...], dims)
        if two:
            part = part + _bdot(a2_ref[...], b2_ref[...], dims)

        def finish(acc):
            out = acc * alpha if alpha != 1.0 else acc
            if has_res:
                out = res_ref[...] + out
            o_ref[...] = out.astype(out_dtype)

        if nk == 1:
            finish(part)
        else:
            @pl.when(k == 0)
            def _():
                acc_ref[...] = part

            @pl.when(k > 0)
            def _():
                acc_ref[...] += part

            @pl.when(k == nk - 1)
            def _():
                finish(acc_ref[...])

    ins = [a, b]
    specs = [a_spec, b_spec]
    if two:
        ins += [a2, b2]
        specs += [a2_spec, b2_spec]
    if has_res:
        ins.append(res)
        specs.append(o_spec)
    return _pcall(
        body, name=name, grid=(M // tm, N // tn, nk), in_specs=specs, out_specs=o_spec,
        out_shape=jax.ShapeDtypeStruct((M, N), out_dtype),
        scratch_shapes=[pltpu.VMEM((tm, tn) if nk > 1 else (8, LANE), F32)],
        compiler_params=_params(("parallel", "parallel", "arbitrary")),
    )(*ins)


ROW_TILE = 1024


def _rms_fwd(h, g, name):
    T, D = h.shape
    tr = _pick(T, ROW_TILE)

    def body(h_ref, g_ref, n_ref):
        x = h_ref[...]
        r = lax.rsqrt(jnp.mean(x * x, axis=-1, keepdims=True) + EPS)
        n_ref[...] = (x * r * g_ref[...]).astype(BF16)

    return _pcall(
        body, name=name, grid=(T // tr,),
        in_specs=[pl.BlockSpec((tr, D), lambda i: (i, 0)), pl.BlockSpec((1, D), lambda i: (0, 0))],
        out_specs=pl.BlockSpec((tr, D), lambda i: (i, 0)),
        out_shape=jax.ShapeDtypeStruct((T, D), BF16), compiler_params=_params(("parallel",)),
    )(h, g.reshape(1, D))


def _rms_bwd(dn, h, g, dres, name):
    T, D = h.shape
    tr = _pick(T, ROW_TILE)

    def body(dn_ref, h_ref, g_ref, dres_ref, dh_ref, dg_ref):
        x = h_ref[...]
        r = lax.rsqrt(jnp.mean(x * x, axis=-1, keepdims=True) + EPS)
        xh = x * r
        d = dn_ref[...].astype(F32)
        dxh = d * g_ref[...]
        dh_ref[...] = dres_ref[...] + r * (dxh - xh * jnp.mean(dxh * xh, axis=-1, keepdims=True))
        part = jnp.sum(d * xh, axis=0, keepdims=True)

        @pl.when(pl.program_id(0) == 0)
        def _():
            dg_ref[...] = part

        @pl.when(pl.program_id(0) > 0)
        def _():
            dg_ref[...] += part

    row = pl.BlockSpec((tr, D), lambda i: (i, 0))
    vec = pl.BlockSpec((1, D), lambda i: (0, 0))
    dh, dg = _pcall(
        body, name=name, grid=(T // tr,), in_specs=[row, row, vec, row], out_specs=[row, vec],
        out_shape=[jax.ShapeDtypeStruct((T, D), F32), jax.ShapeDtypeStruct((1, D), F32)],
        compiler_params=_params(("arbitrary",)),
    )(dn, h, g.reshape(1, D), dres)
    return dh, dg.reshape(D)


def _gateup(n, wg, wu, name, ride=()):
    T, D = n.shape
    F = wg.shape[1]
    tm, tn = _pick(T, 1024), _pick(F, 1408)
    nr = len(ride)
    ride_out = [jax.ShapeDtypeStruct((N_DEV,) + r.shape, r.dtype) for r in ride]
    ride_in_specs, ride_out_specs, ride_sems = _ride_specs(ride, ride_out, N_DEV - 1)
    grid = (T // tm, F // tn)

    def body(*refs):
        n_ref, wg_ref, wu_ref = refs[:3]
        a_ref, b_ref, hid_ref = refs[3 + nr:6 + nr]
        if nr:
            i, j = pl.program_id(0), pl.program_id(1)
            start, finish = _gather_protocol(refs[3:3 + nr], refs[6 + nr:6 + 2 * nr], *refs[6 + 2 * nr:])
            pl.when((i == 0) & (j == 0))(start)
        x = n_ref[...]
        a = _dot(x, wg_ref[...])
        b = _dot(x, wu_ref[...])
        a_ref[...] = a.astype(BF16)
        b_ref[...] = b.astype(BF16)
        hid_ref[...] = (a * _sigmoid(a) * b).astype(BF16)
        if nr:
            pl.when((i == grid[0] - 1) & (j == grid[1] - 1))(finish)

    o_spec = pl.BlockSpec((tm, tn), lambda i, j: (i, j))
    w_spec = pl.BlockSpec((D, tn), lambda i, j: (0, j))
    sh = jax.ShapeDtypeStruct((T, F), BF16)
    res = _pcall(
        body, name=name, grid=grid,
        in_specs=[pl.BlockSpec((tm, D), lambda i, j: (i, 0)), w_spec, w_spec] + ride_in_specs,
        out_specs=[o_spec, o_spec, o_spec] + ride_out_specs, out_shape=[sh, sh, sh] + ride_out,
        scratch_shapes=ride_sems,
        compiler_params=_params(("arbitrary", "arbitrary") if nr else ("parallel", "parallel")),
    )(n, wg, wu, *ride)
    return res[0], res[1], res[2], list(res[3:])


def _ffn_dhid(dy, wd, a, b, name):
    T, D = dy.shape
    F = wd.shape[0]
    tm, tn = _pick(T, 1024), _pick(F, 1408)

    def body(dy_ref, wd_ref, a_ref, b_ref, da_ref, db_ref):
        dhid = 0.5 * _bdot(dy_ref[...], wd_ref[...], NT)
        av = a_ref[...].astype(F32)
        bv = b_ref[...].astype(F32)
        s = _sigmoid(av)
        da_ref[...] = (dhid * bv * s * (1.0 + av * (1.0 - s))).astype(BF16)
        db_ref[...] = (dhid * av * s).astype(BF16)

    o_spec = pl.BlockSpec((tm, tn), lambda i, j: (i, j))
    sh = jax.ShapeDtypeStruct((T, F), BF16)
    return _pcall(
        body, name=name, grid=(T // tm, F // tn),
        in_specs=[pl.BlockSpec((tm, D), lambda i, j: (i, 0)), pl.BlockSpec((tn, D), lambda i, j: (j, 0)), o_spec, o_spec],
        out_specs=[o_spec, o_spec], out_shape=[sh, sh],
        compiler_params=_params(("parallel", "parallel")),
    )(dy, wd, a, b)


def _ffn_fwd(h, g, wg, wu, wd, tag, ride=(), wd_of=None):
    n = _rms_fwd(h, g, f"{tag}_norm")
    a, b, hid, gathered = _gateup(n, wg, wu, f"{tag}_gateup", ride)
    if wd_of is not None:
        wd = wd_of(gathered)
    h2 = _mm(hid, wd, "nn", res=h, alpha=0.5, name=f"{tag}_down")
    return h2, (h, n, a, b, hid)


def _ffn_bwd(dh2, saved, g, wg, wu, wd, tag):
    h, n, a, b, hid = saved
    da, db = _ffn_dhid(dh2, wd, a, b, f"{tag}_dhid")
    dwd = _mm(hid, dh2, "tn", alpha=0.5, out_dtype=BF16, name=f"{tag}_dwd")
    dwg = _mm(n, da, "tn", out_dtype=BF16, name=f"{tag}_dwg")
    dwu = _mm(n, db, "tn", out_dtype=BF16, name=f"{tag}_dwu")
    dn = _mm(da, wg, "nt", a2=db, b2=wu, name=f"{tag}_dn")
    dh, dg = _rms_bwd(dn, h, g, dh2, f"{tag}_dnorm")
    return dh, dg, dwg, dwu, dwd


def _ple_fwd(h, p, g, w_gate, w_proj, tag):
    T, D = h.shape
    pn = _rms_fwd(h, g, f"{tag}_norm")
    tm, tn = _pick(T, 512), _pick(D, 1024)
    P = p.shape[1]

    def body(pn_ref, p_ref, wg_ref, wp_ref, h_ref, o_ref, gl_ref, pp_ref):
        gl = _dot(pn_ref[...], wg_ref[...])
        pp = _bdot(p_ref[...], wp_ref[...])
        gl_ref[...] = gl
        pp_ref[...] = pp
        o_ref[...] = h_ref[...] + _sigmoid(gl) * pp

    o_spec = pl.BlockSpec((tm, tn), lambda i, j: (i, j))
    sh = jax.ShapeDtypeStruct((T, D), F32)
    h2, gl, pp = _pcall(
        body, name=f"{tag}_fwd", grid=(T // tm, D // tn),
        in_specs=[pl.BlockSpec((tm, D), lambda i, j: (i, 0)), pl.BlockSpec((tm, P), lambda i, j: (i, 0)),
                  pl.BlockSpec((D, tn), lambda i, j: (0, j)), pl.BlockSpec((P, tn), lambda i, j: (0, j)), o_spec],
        out_specs=[o_spec, o_spec, o_spec], out_shape=[sh, sh, sh],
        compiler_params=_params(("parallel", "parallel")),
    )(pn, p, w_gate, w_proj, h)
    return h2, (h, pn, gl, pp)


def _ple_bwd(dh2, saved, p, g, w_gate, tag):
    h, pn, gl, pp = saved
    T, D = h.shape
    tr = _pick(T, ROW_TILE)

    def body(d_ref, gl_ref, pp_ref, dgl_ref, dpp_ref):
        d = d_ref[...]
        s = _sigmoid(gl_ref[...])
        dpp_ref[...] = (d * s).astype(BF16)
        dgl_ref[...] = (d * pp_ref[...] * s * (1.0 - s)).astype(BF16)

    row = pl.BlockSpec((tr, D), lambda i: (i, 0))
    sh = jax.ShapeDtypeStruct((T, D), BF16)
    dgl, dpp = _pcall(body, name=f"{tag}_dgate", grid=(T // tr,), in_specs=[row, row, row], out_specs=[row, row],
                      out_shape=[sh, sh], compiler_params=_params(("parallel",)))(dh2, gl, pp)
    dw_proj = _mm(p, dpp, "tn", out_dtype=BF16, name=f"{tag}_dwproj")
    dw_gate = _mm(pn, dgl, "tn", out_dtype=BF16, name=f"{tag}_dwgate")
    dpn = _mm(dgl, w_gate, "nt", name=f"{tag}_dpn")
    dh, dg = _rms_bwd(dpn, h, g, dh2, f"{tag}_dnorm")
    return dh, dg, dw_gate, dw_proj


def _loss_head(y, target):
    T, D = y.shape
    tr = _pick(T, ROW_TILE)

    def body(y_ref, t_ref, dy_ref, l_ref):
        e = y_ref[...] - t_ref[...]
        dy_ref[...] = e * (1.0 / D)
        part = jnp.sum(e * e, axis=0, keepdims=True)

        @pl.when(pl.program_id(0) == 0)
        def _():
            l_ref[...] = part

        @pl.when(pl.program_id(0) > 0)
        def _():
            l_ref[...] += part

    row = pl.BlockSpec((tr, D), lambda i: (i, 0))
    vec = pl.BlockSpec((1, D), lambda i: (0, 0))
    dy, l = _pcall(body, name="loss_head", grid=(T // tr,), in_specs=[row, row], out_specs=[row, vec],
                   out_shape=[jax.ShapeDtypeStruct((T, D), F32), jax.ShapeDtypeStruct((1, D), F32)],
                   compiler_params=_params(("arbitrary",)))(y, target)
    return (0.5 / D) * jnp.sum(l), dy


SB_LANES = SB_HEADS * 2 * HEAD_DIM


def _sb_consts():
    row = _iota2((Q_BLOCK, Q_BLOCK), 0)
    col = _iota2((Q_BLOCK, Q_BLOCK), 1)
    after = (row > col).astype(BF16)
    before = (row < col).astype(BF16)
    return col < row, after, before, col


def _sb_fwd(proj, ride=()):
    T = proj.shape[0]
    H, d, L = SB_HEADS, HEAD_DIM, 2 * HEAD_DIM
    nblk = T // Q_BLOCK
    scale = d ** -0.5
    n = len(ride)
    ride_out = [jax.ShapeDtypeStruct((N_DEV,) + a.shape, a.dtype) for a in ride]
    ride_in_specs, ride_out_specs, ride_sems = _ride_specs(ride, ride_out, N_DEV - 1)
    R = range(H)
    tile = lambda g: slice(g * L, (g + 1) * L)

    def body(*refs):
        q_ref, kv_ref = refs[:2]
        rin = refs[2:2 + n]
        o_ref, c_ref = refs[2 + n:4 + n]
        rout = refs[4 + n:4 + 2 * n]
        run_ref = refs[4 + 2 * n]
        i = pl.program_id(0)
        if n:
            start, finish = _gather_protocol(rin, rout, *refs[5 + 2 * n:])
            pl.when(i == 0)(start)
        causal, after, _, col = _sb_consts()
        qs = [q_ref[:, tile(g)] * scale for g in R]
        o_ref[...] = jnp.zeros_like(o_ref)
        c_ref[...] = jnp.zeros_like(c_ref)
        run_ref[...] = jnp.zeros_like(run_ref)

        def pair(j, diag):
            rows = pl.ds(pl.multiple_of(j * Q_BLOCK, Q_BLOCK), Q_BLOCK)
            kvj = [kv_ref[rows, tile(g)] for g in R]
            c = [run_ref[g] for g in R]
            acc = [o_ref[:, tile(g)] for g in R]
            cm = None if diag else [c_ref[:, tile(g)] for g in R]
            z = [_dot(qs[g], kvj[g], NT) for g in R]
            sp = [_softplus(z[g]) for g in R]
            lk = [jnp.where(causal, -sp[g], 0.0) if diag else -sp[g] for g in R]
            btw = [_dot2m(lk[g], after) for g in R]
            e = [jnp.exp((z[g] - sp[g]) + btw[g] + c[g]) for g in R]
            w = [jnp.where(causal, e[g], 0.0) if diag else e[g] for g in R]
            pv = [_bdot(w[g], kvj[g]) for g in R]
            rs = [jnp.sum(lk[g], axis=1, keepdims=True) for g in R]
            for g in R:
                o_ref[:, tile(g)] = acc[g] + pv[g]
                if not diag:
                    c_ref[:, tile(g)] = jnp.where(col == j, c[g], cm[g])
                run_ref[g] = c[g] + rs[g]

        pair(i, True)

        @pl.loop(0, i)
        def _(jj):
            pair(i - 1 - jj, False)

        if n:
            pl.when(i == nblk - 1)(finish)

    blk = pl.BlockSpec((Q_BLOCK, H * L), lambda i: (i, 0))
    full = pl.BlockSpec((T, H * L), lambda i: (0, 1))
    res = _pcall(
        body, name="sb_fwd", grid=(nblk,), in_specs=[blk, full] + ride_in_specs,
        out_specs=[blk, blk] + ride_out_specs,
        out_shape=[jax.ShapeDtypeStruct((T, H * L), F32), jax.ShapeDtypeStruct((T, H * L), F32)] + ride_out,
        scratch_shapes=[pltpu.VMEM((H, Q_BLOCK, 1), F32)] + ride_sems,
        compiler_params=_params(("arbitrary",)),
    )(proj, proj, *ride)
    return res[0], res[1], list(res[2:])


def _sb_bwd(proj, carry, do, ride=()):
    T = proj.shape[0]
    H, d, L = SB_HEADS, HEAD_DIM, 2 * HEAD_DIM
    nblk = T // Q_BLOCK
    scale = d ** -0.5
    n = len(ride)
    ride_out = [jax.ShapeDtypeStruct(a.shape, a.dtype) for a in ride]
    ride_in_specs, ride_out_specs, ride_sems = _ride_specs(ride, ride_out, N_CHIP - 1)
    R = range(H)
    tile = lambda g: slice(g * L, (g + 1) * L)

    def body(*refs):
        q_ref, kv_ref, c_ref, do_ref = refs[:4]
        rin = refs[4:4 + n]
        dq_ref, dkv_ref = refs[4 + n:6 + n]
        rout = refs[6 + n:6 + 2 * n]
        run_ref = refs[6 + 2 * n]
        i = pl.program_id(0)
        if n:
            start, finish = _chips_protocol(rin, rout, *refs[7 + 2 * n:])
            pl.when(i == 0)(start)

        @pl.when(i == 0)
        def _():
            dkv_ref[...] = jnp.zeros_like(dkv_ref)

        causal, after, before, col = _sb_consts()
        qs = [q_ref[:, tile(g)] * scale for g in R]
        dov = [do_ref[:, tile(g)] for g in R]
        qdo = [jnp.concatenate([qs[g], dov[g]], axis=0) for g in R]
        dq_ref[...] = jnp.zeros_like(dq_ref)
        run_ref[...] = jnp.zeros_like(run_ref)

        def pair(j, diag):
            rows = pl.ds(pl.multiple_of(j * Q_BLOCK, Q_BLOCK), Q_BLOCK)
            kvj = [kv_ref[rows, tile(g)] for g in R]
            gsum = [run_ref[g] for g in R]
            dq0 = [dq_ref[:, tile(g)] for g in R]
            dkv0 = [dkv_ref[rows, tile(g)] for g in R]
            cm = None if diag else [c_ref[:, tile(g)] for g in R]
            z = [_dot(qs[g], kvj[g], NT) for g in R]
            sp = [_softplus(z[g]) for g in R]
            lk = [jnp.where(causal, -sp[g], 0.0) if diag else -sp[g] for g in R]
            ls = [z[g] - sp[g] for g in R]
            logw = [ls[g] + _dot2m(lk[g], after) for g in R]
            if not diag:
                logw = [logw[g] + jnp.sum(jnp.where(col == j, cm[g], 0.0), axis=1, keepdims=True) for g in R]
            e = [jnp.exp(logw[g]) for g in R]
            w = [jnp.where(causal, e[g], 0.0) if diag else e[g] for g in R]
            gw = [_dot(dov[g], kvj[g], NT) * w[g] for g in R]
            gpre = [gsum[g] + _dot(gw[g].astype(BF16), before) for g in R]
            sig = [jnp.exp(ls[g]) for g in R]
            dz = [gw[g] * (1.0 - sig[g]) - sig[g] * gpre[g] for g in R]
            if diag:
                dz = [jnp.where(causal, dz[g], 0.0) for g in R]
            dzb = [dz[g].astype(BF16) for g in R]
            dq1 = [_dot(dzb[g], kvj[g]) for g in R]
            dkv1 = [_dot(jnp.concatenate([dzb[g], w[g].astype(BF16)], axis=0), qdo[g], TN) for g in R]
            gs1 = [jnp.sum(gw[g], axis=1, keepdims=True) for g in R]
            for g in R:
                dq_ref[:, tile(g)] = dq0[g] + dq1[g]
                dkv_ref[rows, tile(g)] = dkv0[g] + dkv1[g]
                run_ref[g] = gsum[g] + gs1[g]

        @pl.loop(0, i)
        def _(j):
            pair(j, False)

        pair(i, True)
        dq_ref[...] = dq_ref[...] * scale
        if n:
            pl.when(i == nblk - 1)(finish)

    blk = pl.BlockSpec((Q_BLOCK, H * L), lambda i: (i, 0))
    once = pl.Buffered(1)
    sh = jax.ShapeDtypeStruct((T, H * L), F32)
    res = _pcall(
        body, name="sb_bwd", grid=(nblk,),
        in_specs=[blk, pl.BlockSpec((T, H * L), lambda i: (0, 1), pipeline_mode=once), blk, blk] + ride_in_specs,
        out_specs=[blk, pl.BlockSpec((T, H * L), lambda i: (0, 0), pipeline_mode=once)] + ride_out_specs,
        out_shape=[sh, sh] + ride_out,
        scratch_shapes=[pltpu.VMEM((H, Q_BLOCK, 1), F32)] + ride_sems,
        compiler_params=_params(("arbitrary",)),
    )(proj, proj, carry, do, *ride)
    return res[0], res[1], list(res[2:])


def _swa_common(q_ref, kvp_ref, kvc_ref, qg_ref, kg_ref, sk_ref, sl_ref, n):
    W, d, G = WINDOW, HEAD_DIM, SWA_GROUP
    scale = d ** -0.5
    row = _iota2((W, 2 * W), 0)
    col = _iota2((W, 2 * W), 1)
    dist = row + W - col
    valid = (dist >= 0) & (dist < W) & ((n > 0) | (col >= W))
    distf = dist.astype(F32)
    kvcat = jnp.concatenate([kvp_ref[...], kvc_ref[...]], axis=0)
    KH, QH = range(SWA_KV_HEADS), range(SWA_HEADS)
    kraw = [kvcat[:, hk * d:(hk + 1) * d] for hk in KH]
    vcat = [kvcat[:, SWA_KVW + hk * d:SWA_KVW + (hk + 1) * d].astype(BF16) for hk in KH]
    rk = [lax.rsqrt(jnp.mean(kraw[hk] * kraw[hk], axis=-1, keepdims=True) + EPS) for hk in KH]
    kh = [kraw[hk] * rk[hk] for hk in KH]
    kn = [(kh[hk] * kg_ref[...]).astype(BF16) for hk in KH]
    qraw = [q_ref[:, h * d:(h + 1) * d] for h in QH]
    rq = [lax.rsqrt(jnp.mean(qraw[h] * qraw[h], axis=-1, keepdims=True) + EPS) for h in QH]
    qh = [qraw[h] * rq[h] for h in QH]
    qn = [(qh[h] * qg_ref[...]).astype(BF16) for h in QH]
    sink = [sk_ref[h:h + 1, :1] for h in QH]
    s = [jnp.where(valid, _dot(qn[h], kn[h // G], NT) * scale - sl_ref[h:h + 1, :1] * distf, -1e30) for h in QH]
    m = [jnp.maximum(jnp.max(s[h], axis=1, keepdims=True), sink[h]) for h in QH]
    p = [jnp.where(valid, jnp.exp(s[h] - m[h]), 0.0) for h in QH]
    esink = [jnp.exp(sink[h] - m[h]) for h in QH]
    den = [jnp.sum(p[h], axis=1, keepdims=True) + esink[h] for h in QH]
    prob = [p[h] / den[h] for h in QH]
    return vcat, rk, kh, kn, rq, qh, qn, esink, den, prob


def _swa_specs(T):
    W = WINDOW
    q = pl.BlockSpec((W, SWA_QW), lambda n: (n, 0))
    prev = pl.BlockSpec((W, 2 * SWA_KVW), lambda n: (jnp.maximum(n - 1, 0), SWA_QW // (2 * SWA_KVW)))
    cur = pl.BlockSpec((W, 2 * SWA_KVW), lambda n: (n, SWA_QW // (2 * SWA_KVW)))
    gain = pl.BlockSpec((1, HEAD_DIM), lambda n: (0, 0))
    perhead = pl.BlockSpec((SWA_HEADS, LANE), lambda n: (0, 0))
    return q, prev, cur, gain, perhead


def _swa_fwd(proj, qg, kg, sinks, slopes):
    T = proj.shape[0]
    W, d, G = WINDOW, HEAD_DIM, SWA_GROUP

    def body(q_ref, kvp_ref, kvc_ref, qg_ref, kg_ref, sk_ref, sl_ref, o_ref):
        vcat, _, _, _, _, _, _, _, _, prob = _swa_common(q_ref, kvp_ref, kvc_ref, qg_ref, kg_ref, sk_ref, sl_ref,
                                                         pl.program_id(0))
        outs = [_bdot(prob[h], vcat[h // G]) for h in range(SWA_HEADS)]
        o_ref[...] = jnp.concatenate(outs, axis=1).astype(BF16)

    q, prev, cur, gain, perhead = _swa_specs(T)
    return _pcall(
        body, name="swa_fwd", grid=(T // W,), in_specs=[q, prev, cur, gain, gain, perhead, perhead], out_specs=q,
        out_shape=jax.ShapeDtypeStruct((T, SWA_QW), BF16), compiler_params=_params(("parallel",)),
    )(proj, proj, proj, qg, kg, sinks, slopes)


def _swa_bwd(proj, qg, kg, sinks, slopes, do):
    T = proj.shape[0]
    W, d, G = WINDOW, HEAD_DIM, SWA_GROUP
    scale = d ** -0.5
    KH, QH = range(SWA_KV_HEADS), range(SWA_HEADS)

    def body(q_ref, kvp_ref, kvc_ref, qg_ref, kg_ref, sk_ref, sl_ref, do_ref,
             dq_ref, dkv_ref, dqg_ref, dkg_ref, dsk_ref):
        n = pl.program_id(0)

        @pl.when(n == 0)
        def _():
            dqg_ref[...] = jnp.zeros_like(dqg_ref)
            dkg_ref[...] = jnp.zeros_like(dkg_ref)
            dsk_ref[...] = jnp.zeros_like(dsk_ref)
            dkv_ref[...] = jnp.zeros_like(dkv_ref)

        vcat, rk, kh, kn, rq, qh, qn, esink, den, prob = _swa_common(q_ref, kvp_ref, kvc_ref, qg_ref, kg_ref,
                                                                     sk_ref, sl_ref, n)
        dov = [do_ref[:, h * d:(h + 1) * d].astype(BF16) for h in QH]
        dp = [_dot(dov[h], vcat[h // G], NT) for h in QH]
        dd = [jnp.sum(prob[h] * dp[h], axis=1, keepdims=True) for h in QH]
        dsb = [(prob[h] * (dp[h] - dd[h]) * scale).astype(BF16) for h in QH]
        dsink = [-jnp.sum((esink[h] / den[h]) * dd[h], axis=0, keepdims=True) for h in QH]
        dqn = [_dot(dsb[h], kn[h // G]) for h in QH]
        dkn_h = [_dot(dsb[h], qn[h], TN) for h in QH]
        dv_h = [_dot(prob[h].astype(BF16), dov[h], TN) for h in QH]
        dqh = [dqn[h] * qg_ref[...] for h in QH]
        dq = [rq[h] * (dqh[h] - qh[h] * jnp.mean(dqh[h] * qh[h], axis=-1, keepdims=True)) for h in QH]
        dkn = [sum(dkn_h[hk * G + g] for g in range(G)) for hk in KH]
        dvc = [sum(dv_h[hk * G + g] for g in range(G)) for hk in KH]
        dkh = [dkn[hk] * kg_ref[...] for hk in KH]
        dkraw = [rk[hk] * (dkh[hk] - kh[hk] * jnp.mean(dkh[hk] * kh[hk], axis=-1, keepdims=True)) for hk in KH]
        dq_ref[...] = jnp.concatenate(dq, axis=1)
        dqg_ref[...] += sum(jnp.sum(dqn[h] * qh[h], axis=0, keepdims=True) for h in QH)
        dkg_ref[...] += sum(jnp.sum(dkn[hk] * kh[hk], axis=0, keepdims=True) for hk in KH)
        rowh = _iota2((SWA_HEADS, LANE), 0)
        dsk_ref[...] += sum(jnp.where(rowh == h, dsink[h], 0.0) for h in QH)
        upd = jnp.concatenate(dkraw + dvc, axis=1)
        offp = pl.multiple_of(jnp.maximum(n - 1, 0) * W, W)
        offc = pl.multiple_of(n * W, W)
        dkv_ref[pl.ds(offp, W), :] += upd[:W]
        dkv_ref[pl.ds(offc, W), :] += upd[W:]

    q, prev, cur, gain, perhead = _swa_specs(T)
    kvfull = pl.BlockSpec((T, 2 * SWA_KVW), lambda n: (0, 0))
    gs = jax.ShapeDtypeStruct((1, d), F32)
    return _pcall(
        body, name="swa_bwd", grid=(T // W,), in_specs=[q, prev, cur, gain, gain, perhead, perhead, q],
        out_specs=[q, kvfull, gain, gain, perhead],
        out_shape=[jax.ShapeDtypeStruct((T, SWA_QW), F32), jax.ShapeDtypeStruct((T, 2 * SWA_KVW), F32), gs, gs,
                   jax.ShapeDtypeStruct((SWA_HEADS, LANE), F32)],
        compiler_params=_params(("arbitrary",)),
    )(proj, proj, proj, qg, kg, sinks, slopes, do)


def _alibi():
    s = [2.0 ** (-8.0 * (i + 1) / SWA_HEADS) for i in range(SWA_HEADS)]
    return jnp.broadcast_to(jnp.asarray(s, F32)[:, None], (SWA_HEADS, LANE))


def _head_tiles(lo, hi):
    shp = lo.shape[:-1]
    return jnp.concatenate([lo.reshape(shp + (SB_HEADS, HEAD_DIM)), hi.reshape(shp + (SB_HEADS, HEAD_DIM))],
                           axis=-1).reshape(shp + (SB_LANES,))


def _tile_halves(x):
    shp = x.shape[:-1]
    t = x.reshape(shp + (SB_HEADS, 2, HEAD_DIM))
    return t[..., 0, :].reshape(shp + (SB_W,)), t[..., 1, :].reshape(shp + (SB_W,))


def _att_in_weights(w_in):
    sq, sk, sv = w_in[:, :SB_W], w_in[:, SB_W:2 * SB_W], w_in[:, 2 * SB_W:3 * SB_W]
    return jnp.concatenate([_head_tiles(sq, jnp.zeros_like(sq)), _head_tiles(sk, sv)], axis=1), w_in[:, 3 * SB_W:]


def _att_out_weights(w_out):
    wo = w_out[:SB_W]
    return _head_tiles(jnp.zeros_like(wo).T, wo.T).T, w_out[SB_W:]


def _att_fwd(h, g, w_in, w_out_of, q_gain, k_gain, sinks, ride=()):
    hn = _rms_fwd(h, g, "att_norm")
    w_sb, w_swa = _att_in_weights(w_in)
    proj_sb = _mm(hn, w_sb, "nn", out_dtype=BF16, name="att_in_sb")
    proj_swa = _mm(hn, w_swa, "nn", name="att_in_swa")
    a_out, carry, gathered = _sb_fwd(proj_sb, ride)
    w_out = w_out_of(gathered)
    wo_sb, wo_swa = _att_out_weights(w_out)
    sk128 = jnp.broadcast_to(sinks.reshape(SWA_HEADS, 1), (SWA_HEADS, LANE))
    qg, kg = q_gain.reshape(1, HEAD_DIM), k_gain.reshape(1, HEAD_DIM)
    b_out = _swa_fwd(proj_swa, qg, kg, sk128, _alibi())
    h2 = _mm(a_out, wo_sb, "nn", res=h, a2=b_out, b2=wo_swa, name="att_out")
    return h2, (h, hn, proj_sb, proj_swa, carry, a_out, b_out, sk128, qg, kg), gathered


def _att_bwd(dh2, saved, g, w_in, w_out, ride=()):
    h, hn, proj_sb, proj_swa, carry, a_out, b_out, sk128, qg, kg = saved
    w_sb, w_swa = _att_in_weights(w_in)
    wo_sb, wo_swa = _att_out_weights(w_out)
    da = _mm(dh2, wo_sb, "nt", out_dtype=BF16, name="att_do_sb")
    db = _mm(dh2, wo_swa, "nt", name="att_do_swa")
    dwo_sb = _mm(a_out, dh2, "tn", out_dtype=BF16, name="att_dwout_sb")
    dwo_swa = _mm(b_out, dh2, "tn", out_dtype=BF16, name="att_dwout_swa")
    dw_out = jnp.concatenate([_tile_halves(dwo_sb.T)[1].T, dwo_swa], axis=0)
    dq, dkv, rode = _sb_bwd(proj_sb, carry, da, ride)
    dbq, dbkv, dqg, dkg, dsink = _swa_bwd(proj_swa, qg, kg, sk128, _alibi(), db)
    dproj = jnp.concatenate([dq.astype(BF16), dkv.astype(BF16), dbq.astype(BF16), dbkv.astype(BF16)], axis=1)
    w_all = jnp.concatenate([w_sb, w_swa], axis=1)
    dw_all = _mm(hn, dproj, "tn", out_dtype=BF16, name="att_dwin")
    dhn = _mm(dproj, w_all, "nt", name="att_dhn")
    dsq, _ = _tile_halves(dw_all[:, :SB_LANES])
    dsk, dsv = _tile_halves(dw_all[:, SB_LANES:2 * SB_LANES])
    dw_in = jnp.concatenate([dsq, dsk, dsv, dw_all[:, 2 * SB_LANES:]], axis=1)
    dh, dg = _rms_bwd(dhn, h, g, dh2, "att_dnorm")
    return dh, dg, dw_in, dw_out, dqg.reshape(HEAD_DIM), dkg.reshape(HEAD_DIM), dsink[:, 0], rode


CONV_ROWS = 1024
CONV_COLS = 512
HALO = 8


def _shifted(xcat, s, tm):
    if s == 0:
        return xcat[HALO:HALO + tm]
    return pltpu.roll(xcat, s, 0)[HALO:HALO + tm]


def _conv_pre(x_ref, halo_ref, w_ref, i, tm):
    xc = x_ref[...]
    halo = jnp.where(i > 0, halo_ref[...], 0.0)
    xcat = jnp.concatenate([halo, xc], axis=0)
    w = w_ref[...]
    y = w[GDN_CONV - 1:GDN_CONV] * xc
    for kk in range(GDN_CONV - 1):
        y = y + w[kk:kk + 1] * _shifted(xcat, GDN_CONV - 1 - kk, tm)
    return xcat, y


def _l2_heads(s, qscale_of):
    outs, rs = [], []
    for hh in range(s.shape[1] // GDN_HEAD_DIM):
        sh = s[:, hh * GDN_HEAD_DIM:(hh + 1) * GDN_HEAD_DIM]
        r = lax.rsqrt(jnp.sum(sh * sh, axis=-1, keepdims=True) + EPS)
        outs.append(sh * r)
        rs.append(r)
    return outs, rs


def _conv_specs(T, col0, tm, tc):
    cur = pl.BlockSpec((tm, tc), lambda j, i: (i, j + col0 // tc))
    halo = pl.BlockSpec((HALO, tc), lambda j, i: (jnp.maximum(i * (tm // HALO) - 1, 0), j + col0 // tc))
    wsp = pl.BlockSpec((GDN_CONV, tc), lambda j, i: (0, j + col0 // tc))
    out = pl.BlockSpec((tm, tc), lambda j, i: (i, j))
    return cur, halo, wsp, out


def _conv_fwd(proj, conv_w, col0, width, norm, name):
    T = proj.shape[0]
    tm, tc = _pick(T, CONV_ROWS), CONV_COLS
    cur, halo, wsp, out = _conv_specs(T, col0, tm, tc)
    n_q_tiles = (width // 2) // tc

    def body(x_ref, halo_ref, w_ref, o_ref):
        j, i = pl.program_id(0), pl.program_id(1)
        _, y = _conv_pre(x_ref, halo_ref, w_ref, i, tm)
        s = y * _sigmoid(y)
        if norm:
            outs, _ = _l2_heads(s, None)
            qs = jnp.where(j < n_q_tiles, GDN_HEAD_DIM ** -0.5, 1.0)
            o_ref[...] = jnp.concatenate(outs, axis=1) * qs
        else:
            o_ref[...] = s

    return _pcall(body, name=name, grid=(width // tc, T // tm), in_specs=[cur, halo, wsp], out_specs=out,
                  out_shape=jax.ShapeDtypeStruct((T, width), F32),
                  compiler_params=_params(("parallel", "parallel")))(proj, proj, conv_w)


def _conv_bwd_pre(proj, conv_w, dout, col0, width, norm, name):
    T = proj.shape[0]
    tm, tc = _pick(T, CONV_ROWS), CONV_COLS
    cur, halo, wsp, out = _conv_specs(T, col0, tm, tc)
    n_q_tiles = (width // 2) // tc

    def body(x_ref, halo_ref, w_ref, d_ref, dy_ref, dw_ref):
        j, i = pl.program_id(0), pl.program_id(1)
        xcat, y = _conv_pre(x_ref, halo_ref, w_ref, i, tm)
        sg = _sigmoid(y)
        s = y * sg
        d = d_ref[...]
        if norm:
            qs = jnp.where(j < n_q_tiles, GDN_HEAD_DIM ** -0.5, 1.0)
            d = d * qs
            outs, rs = _l2_heads(s, None)
            parts = []
            for hh, (nh, r) in enumerate(zip(outs, rs)):
                dh = d[:, hh * GDN_HEAD_DIM:(hh + 1) * GDN_HEAD_DIM]
                parts.append(r * (dh - nh * jnp.sum(dh * nh, axis=-1, keepdims=True)))
            ds = jnp.concatenate(parts, axis=1)
        else:
            ds = d
        dy = ds * sg * (1.0 + y * (1.0 - sg))
        dy_ref[...] = dy
        rows = [jnp.sum(dy * _shifted(xcat, GDN_CONV - 1 - kk, tm), axis=0, keepdims=True) for kk in range(GDN_CONV)]
        part = jnp.concatenate(rows, axis=0)

        @pl.when(i == 0)
        def _():
            dw_ref[...] = part

        @pl.when(i > 0)
        def _():
            dw_ref[...] += part

    wout = pl.BlockSpec((GDN_CONV, tc), lambda j, i: (0, j))
    return _pcall(body, name=name, grid=(width // tc, T // tm), in_specs=[cur, halo, wsp, out], out_specs=[out, wout],
                  out_shape=[jax.ShapeDtypeStruct((T, width), F32), jax.ShapeDtypeStruct((GDN_CONV, width), F32)],
                  compiler_params=_params(("parallel", "arbitrary")))(proj, proj, conv_w, dout)


def _conv_bwd_in(dy, conv_w, name):
    T, C = dy.shape
    tm, tc = _pick(T, CONV_ROWS), CONV_COLS
    nrow = T // tm

    def body(d_ref, nxt_ref, w_ref, dx_ref):
        i = pl.program_id(0)
        dc = d_ref[...]
        nxt = jnp.where(i < nrow - 1, nxt_ref[...], 0.0)
        dcat = jnp.concatenate([dc, nxt], axis=0)
        w = w_ref[...]
        dx = w[GDN_CONV - 1:GDN_CONV] * dc
        for kk in range(GDN_CONV - 1):
            s = GDN_CONV - 1 - kk
            dx = dx + w[kk:kk + 1] * pltpu.roll(dcat, tm + HALO - s, 0)[:tm]
        dx_ref[...] = dx.astype(BF16)

    cur = pl.BlockSpec((tm, tc), lambda i, j: (i, j))
    nxt = pl.BlockSpec((HALO, tc), lambda i, j: (jnp.minimum((i + 1) * (tm // HALO), T // HALO - 1), j))
    wsp = pl.BlockSpec((GDN_CONV, tc), lambda i, j: (0, j))
    return _pcall(body, name=name, grid=(nrow, C // tc), in_specs=[cur, nxt, wsp], out_specs=cur,
                  out_shape=jax.ShapeDtypeStruct((T, C), BF16),
                  compiler_params=_params(("parallel", "parallel")))(dy, dy, conv_w)


GATE_ROWS = 512


def _chunk_mask(n, lower):
    row = _iota2((n, n), 0)
    col = _iota2((n, n), 1)
    same = (row // GDN_CHUNK) == (col // GDN_CHUNK)
    tri = (row >= col) if lower else (row <= col)
    return (same & tri).astype(BF16)


def _gates_fwd(proj, a_log, dt_bias):
    T = proj.shape[0]
    tm = _pick(T, GATE_ROWS)
    c0 = (GDN_CONV_W + GDN_VW) // LANE

    def body(bl_ref, a_ref, alog_ref, dt_ref, beta_ref, g_ref, gc_ref):
        beta_ref[...] = _sigmoid(bl_ref[...])
        g = -jnp.exp(alog_ref[...]) * _softplus(a_ref[...] + dt_ref[...])
        g_ref[...] = g
        gc_ref[...] = _mdot2(_chunk_mask(tm, True), g)

    blk = lambda c: pl.BlockSpec((tm, LANE), lambda i: (i, c))
    vec = pl.BlockSpec((1, LANE), lambda i: (0, 0))
    sh = jax.ShapeDtypeStruct((T, LANE), F32)
    return _pcall(body, name="gdn_gates", grid=(T // tm,), in_specs=[blk(c0), blk(c0 + 1), vec, vec],
                  out_specs=[blk(0), blk(0), blk(0)], out_shape=[sh, sh, sh],
                  compiler_params=_params(("parallel",)))(proj, proj, a_log, dt_bias)


def _gates_bwd(proj, a_log, dt_bias, beta, g, dbeta, dgc):
    T = proj.shape[0]
    tm = _pick(T, GATE_ROWS)
    c0 = (GDN_CONV_W + GDN_VW) // LANE

    def heads_in_lanes(ref):
        lane = _iota2((tm, LANE), 1)
        out = jnp.where(lane < GDN_GROUP, ref[0], 0.0)
        for grp in range(1, GDN_V_HEADS // GDN_GROUP):
            out = out + jnp.where(lane // GDN_GROUP == grp, pltpu.roll(ref[grp], grp * GDN_GROUP, 1), 0.0)
        return out

    def body(a_ref, alog_ref, dt_ref, beta_ref, g_ref, dbeta_ref, dgc_ref, dbl_ref, da_ref, dalog_ref, ddt_ref):
        dg = _mdot2(_chunk_mask(tm, False), heads_in_lanes(dgc_ref))
        b = beta_ref[...]
        dbl_ref[...] = (heads_in_lanes(dbeta_ref) * b * (1.0 - b)).astype(BF16)
        da = dg * (-jnp.exp(alog_ref[...])) * _sigmoid(a_ref[...] + dt_ref[...])
        da_ref[...] = da.astype(BF16)
        p1 = jnp.sum(dg * g_ref[...], axis=0, keepdims=True)
        p2 = jnp.sum(da, axis=0, keepdims=True)

        @pl.when(pl.program_id(0) == 0)
        def _():
            dalog_ref[...] = p1
            ddt_ref[...] = p2

        @pl.when(pl.program_id(0) > 0)
        def _():
            dalog_ref[...] += p1
            ddt_ref[...] += p2

    blk = lambda c: pl.BlockSpec((tm, LANE), lambda i: (i, c))
    vec = pl.BlockSpec((1, LANE), lambda i: (0, 0))
    grp = pl.BlockSpec((GDN_V_HEADS // GDN_GROUP, tm, LANE), lambda i: (0, i, 0))
    shb = jax.ShapeDtypeStruct((T, LANE), BF16)
    shv = jax.ShapeDtypeStruct((1, LANE), F32)
    return _pcall(body, name="gdn_dgates", grid=(T // tm,),
                  in_specs=[blk(c0 + 1), vec, vec, blk(0), blk(0), grp, grp],
                  out_specs=[blk(0), blk(0), vec, vec], out_shape=[shb, shb, shv, shv],
                  compiler_params=_params(("arbitrary",)))(proj, a_log, dt_bias, beta, g, dbeta, dgc)


def _inv_unit_lower(Ls):
    C = Ls[0].shape[0]
    row = _iota2((C, C), 0)
    col = _iota2((C, C), 1)
    blk16 = (row // 16) == (col // 16)
    blk32 = (row // 32) == (col // 32)
    eye = (row == col).astype(F32)
    xs = [-jnp.where(blk16, L, 0.0) for L in Ls]
    inv = [eye + x for x in xs]
    for _ in range(3):
        xs = [_dot3(x, x) for x in xs]
        inv = [a + _dot3(a, x) for a, x in zip(inv, xs)]
    for mask in (blk32 & ~blk16, ~blk32):
        t = [_dot3(a, jnp.where(mask, L, 0.0)) for a, L in zip(inv, Ls)]
        inv = [a - _dot3(ti, a) for a, ti in zip(inv, t)]
    return inv


GDN_GROUP = 4
GDN_PREP_CHUNKS = 16
GDN_STEP_CHUNKS = 4


def _gdn_specs(T):
    C, D, E, J = GDN_CHUNK, GDN_HEAD_DIM, GDN_GROUP, GDN_STEP_CHUNKS
    n = T // (C * J)
    qk = pl.BlockSpec((J * C, (E // 2) * D), lambda h, i: (i, h))
    vE = pl.BlockSpec((J * C, E * D), lambda h, i: (i, h))
    colv = pl.BlockSpec((J * C, LANE), lambda h, i: (i, 0))
    colo = pl.BlockSpec((None, J * C, LANE), lambda h, i: (h, i, 0))
    rowv = pl.BlockSpec((E, J, 1, C), lambda h, i: (h, i, 0, 0))
    st = pl.BlockSpec((E, J, D, D), lambda h, i: (h, i, 0, 0))
    am = pl.BlockSpec((E, J, C, C), lambda h, i: (h, i, 0, 0))
    return n, qk, vE, colv, colo, rowv, st, am


def _lane_col(blk, lane):
    return jnp.sum(jnp.where(_iota2(blk.shape, 1) == lane, blk, 0.0), axis=1, keepdims=True)


def _gdn_decay(gcol, grow):
    C = GDN_CHUNK
    row = _iota2((C, C), 0)
    col = _iota2((C, C), 1)
    incl = row >= col
    dm = jnp.where(incl, jnp.exp(jnp.where(incl, gcol - grow, 0.0)), 0.0)
    glast = grow[:, C - 1:C]
    return dm, jnp.exp(gcol), jnp.exp(glast), jnp.exp(glast - gcol), row > col, incl


def _gdn_prep(k, beta, gcol, grow):
    T = k.shape[0]
    C, D, B = GDN_CHUNK, GDN_HEAD_DIM, GDN_PREP_CHUNKS
    n = T // C

    def body(k_ref, b_ref, gc_ref, gr_ref, a_ref):
        idx = [(e, cb) for e in range(2) for cb in range(B)]
        kc = {cb: k_ref[cb * C:(cb + 1) * C, :] for cb in range(B)}
        lm = []
        head0 = 2 * pl.program_id(0)
        for e, cb in idx:
            beta = _lane_col(b_ref[cb * C:(cb + 1) * C, :], head0 + e)
            dm, _, _, _, strict, _ = _gdn_decay(_lane_col(gc_ref[cb * C:(cb + 1) * C, :], head0 + e), gr_ref[e, cb])
            lm.append(jnp.where(strict, _bdot(kc[cb] * beta, kc[cb], NT) * dm, 0.0))
        inv = _inv_unit_lower(lm)
        for (e, cb), a in zip(idx, inv):
            a_ref[e, cb] = a

    return _pcall(
        body, name="gdn_prep", grid=(GDN_K_HEADS, n // B),
        in_specs=[pl.BlockSpec((B * C, D), lambda h, i: (i, h)), pl.BlockSpec((B * C, LANE), lambda h, i: (i, 0)),
                  pl.BlockSpec((B * C, LANE), lambda h, i: (i, 0)), pl.BlockSpec((2, B, 1, C), lambda h, i: (h, i, 0, 0))],
        out_specs=pl.BlockSpec((2, B, C, C), lambda h, i: (h, i, 0, 0)),
        out_shape=jax.ShapeDtypeStruct((GDN_V_HEADS, n, C, C), F32),
        compiler_params=_params(("parallel", "parallel")),
    )(k, beta, gcol, grow)


def _gdn_fwd(q, k, v, beta, gcol, grow, amat):
    T = q.shape[0]
    C, D, E = GDN_CHUNK, GDN_HEAD_DIM, GDN_GROUP
    n, qk, vE, colv, colo, rowv, st, am = _gdn_specs(T)
    R = range(E)

    def body(q_ref, k_ref, v_ref, b_ref, gc_ref, gr_ref, a_ref, o_ref, s_ref, vn_ref, state):
        @pl.when(pl.program_id(1) == 0)
        def _():
            state[...] = jnp.zeros_like(state)

        head0 = E * pl.program_id(0)
        s = [state[e] for e in R]
        for cc in range(GDN_STEP_CHUNKS):
            rows = slice(cc * C, (cc + 1) * C)
            qv = [q_ref[rows, (e // 2) * D:(e // 2 + 1) * D] for e in R]
            kv = [k_ref[rows, (e // 2) * D:(e // 2 + 1) * D] for e in R]
            vv = [v_ref[rows, e * D:(e + 1) * D] for e in R]
            beta = [_lane_col(b_ref[rows, :], head0 + e) for e in R]
            a = [a_ref[e, cc] for e in R]
            dec = [_gdn_decay(_lane_col(gc_ref[rows, :], head0 + e), gr_ref[e, cc]) for e in R]
            pm = [_bdot(qv[e], kv[e], NT) * dec[e][0] for e in R]
            r = [beta[e] * (vv[e] - _bdot(kv[e] * dec[e][1], s[e])) for e in R]
            vn = [_dot3(a[e], r[e]) for e in R]
            o = [_bdot(qv[e] * dec[e][1], s[e]) + _bdot(pm[e], vn[e]) for e in R]
            s2 = [dec[e][2] * s[e] + _bdot(kv[e] * dec[e][3], vn[e], TN) for e in R]
            for e in R:
                s_ref[e, cc] = s[e]
                vn_ref[rows, e * D:(e + 1) * D] = vn[e]
                o_ref[rows, e * D:(e + 1) * D] = o[e]
            s = s2
        for e in R:
            state[e] = s[e]

    shv = jax.ShapeDtypeStruct((T, GDN_V_HEADS * D), F32)
    return _pcall(
        body, name="gdn_fwd", grid=(GDN_V_HEADS // E, n), in_specs=[qk, qk, vE, colv, colv, rowv, am],
        out_specs=[vE, st, vE],
        out_shape=[shv, jax.ShapeDtypeStruct((GDN_V_HEADS, T // C, D, D), F32), shv],
        scratch_shapes=[pltpu.VMEM((E, D, D), F32)],
        compiler_params=_params(("parallel", "arbitrary")),
    )(q, k, v, beta, gcol, grow, amat)


def _gdn_bwd(q, k, v, beta, gcol, grow, states, amat, vnew, do):
    T = q.shape[0]
    C, D, E = GDN_CHUNK, GDN_HEAD_DIM, GDN_GROUP
    n, qk, vE, colv, colo, rowv, st, am = _gdn_specs(T)
    rev = lambda spec: pl.BlockSpec(spec.block_shape, (lambda f: (lambda h, i: f(h, n - 1 - i)))(spec.index_map))
    qk, vE, colv, colo, rowv, st, am = (rev(s) for s in (qk, vE, colv, colo, rowv, st, am))
    R = range(E)

    def body(q_ref, k_ref, v_ref, b_ref, gc_ref, gr_ref, s_ref, a_ref, vn_ref, do_ref,
             dq_ref, dk_ref, dv_ref, db_ref, dgc_ref, dstate):
        @pl.when(pl.program_id(1) == 0)
        def _():
            dstate[...] = jnp.zeros_like(dstate)

        dcur = [dstate[e] for e in R]
        for cc in reversed(range(GDN_STEP_CHUNKS)):
            rows = slice(cc * C, (cc + 1) * C)
            M = lambda f: [f(e) for e in R]
            rsum = lambda x: jnp.sum(x, axis=1, keepdims=True)
            qv = M(lambda e: q_ref[rows, (e // 2) * D:(e // 2 + 1) * D])
            kv = M(lambda e: k_ref[rows, (e // 2) * D:(e // 2 + 1) * D])
            vv = M(lambda e: v_ref[rows, e * D:(e + 1) * D])
            vn = M(lambda e: vn_ref[rows, e * D:(e + 1) * D])
            dov = M(lambda e: do_ref[rows, e * D:(e + 1) * D])
            head0 = E * pl.program_id(0)
            beta = M(lambda e: _lane_col(b_ref[rows, :], head0 + e))
            s = M(lambda e: s_ref[e, cc])
            a = M(lambda e: a_ref[e, cc])
            dsn = dcur
            dec = M(lambda e: _gdn_decay(_lane_col(gc_ref[rows, :], head0 + e), gr_ref[e, cc]))
            dm, gam, glast, tail = (M(lambda e: dec[e][i]) for i in range(4))
            strict, incl = dec[0][4], dec[0][5]
            kb = M(lambda e: kv[e] * beta[e])
            kd = M(lambda e: kv[e] * gam[e])
            qd = M(lambda e: qv[e] * gam[e])
            kt = M(lambda e: kv[e] * tail[e])
            lmat = M(lambda e: jnp.where(strict, _bdot(kb[e], kv[e], NT) * dm[e], 0.0))
            pmat = M(lambda e: _bdot(qv[e], kv[e], NT) * dm[e])
            xres = M(lambda e: vv[e] - _bdot(kd[e], s[e]))
            dvn = M(lambda e: _bdot(pmat[e], dov[e], TN) + _bdot(kt[e], dsn[e]))
            dqd = M(lambda e: _bdot(dov[e], s[e], NT))
            dp = M(lambda e: jnp.where(incl, _bdot(dov[e], vn[e], NT), 0.0))
            dkt = M(lambda e: _bdot(vn[e], dsn[e], NT))
            dr = M(lambda e: _dot3(a[e], dvn[e], TN))
            drb = M(lambda e: beta[e] * dr[e])
            dkd = M(lambda e: -_bdot(drb[e], s[e], NT))
            ds2 = M(lambda e: _bdot(qd[e], dov[e], TN) + glast[e] * dsn[e] - _bdot(kd[e], drb[e], TN))
            dl = M(lambda e: -jnp.where(strict, _bdot(dr[e], vn[e], NT), 0.0))
            dmm = M(lambda e: dl[e] * dm[e])
            dnn = M(lambda e: dp[e] * dm[e])
            emat = M(lambda e: dl[e] * lmat[e] + dp[e] * pmat[e])
            dkb = M(lambda e: _bdot(dmm[e], kv[e]))
            dk = M(lambda e: beta[e] * dkb[e] + _bdot(dmm[e], kb[e], TN) + _bdot(dnn[e], qv[e], TN)
                   + gam[e] * dkd[e] + tail[e] * dkt[e])
            dq = M(lambda e: _bdot(dnn[e], kv[e]) + gam[e] * dqd[e])
            dbeta = M(lambda e: rsum(dr[e] * xres[e]) + rsum(dkb[e] * kv[e]))
            ones = jnp.ones((C, LANE), BF16)
            colsum = M(lambda e: _dot2m(emat[e], ones, TN)[:, :1])
            tails = M(lambda e: rsum(dkt[e] * kt[e]))
            lastrow = _iota2((C, 1), 0) == C - 1
            dlast = M(lambda e: jnp.sum(tails[e], axis=0, keepdims=True)
                      + glast[e] * jnp.sum(rsum(s[e] * dsn[e]), axis=0, keepdims=True))
            dgc = M(lambda e: rsum(emat[e]) - colsum[e] + rsum(dkd[e] * kd[e]) + rsum(dqd[e] * qd[e]) - tails[e]
                    + jnp.where(lastrow, dlast[e], 0.0))
            lane = _iota2((C, LANE), 1)
            db_all = jnp.zeros((C, LANE), F32)
            dgc_all = jnp.zeros((C, LANE), F32)
            for e in R:
                dv_ref[rows, e * D:(e + 1) * D] = drb[e]
                db_all = jnp.where(lane == e, dbeta[e], db_all)
                dgc_all = jnp.where(lane == e, dgc[e], dgc_all)
            db_ref[rows, :] = db_all
            dgc_ref[rows, :] = dgc_all
            for kh in range(E // 2):
                dq_ref[rows, kh * D:(kh + 1) * D] = dq[2 * kh] + dq[2 * kh + 1]
                dk_ref[rows, kh * D:(kh + 1) * D] = dk[2 * kh] + dk[2 * kh + 1]

            dcur = ds2
        for e in R:
            dstate[e] = dcur[e]

    shq = jax.ShapeDtypeStruct((T, GDN_K_HEADS * D), F32)
    shv = jax.ShapeDtypeStruct((T, GDN_V_HEADS * D), F32)
    shc = jax.ShapeDtypeStruct((GDN_V_HEADS // E, T, LANE), F32)
    return _pcall(
        body, name="gdn_bwd", grid=(GDN_V_HEADS // E, n),
        in_specs=[qk, qk, vE, colv, colv, rowv, st, am, vE, vE],
        out_specs=[qk, qk, vE, colo, colo], out_shape=[shq, shq, shv, shc, shc],
        scratch_shapes=[pltpu.VMEM((E, D, D), F32)],
        compiler_params=_params(("parallel", "arbitrary")),
    )(q, k, v, beta, gcol, grow, states, amat, vnew, do)


def _outgate_fwd(o, proj, gain):
    T = o.shape[0]
    tm, tc = _pick(T, CONV_ROWS), CONV_COLS
    z0 = GDN_CONV_W // tc

    def body(o_ref, z_ref, g_ref, y_ref):
        z = z_ref[...]
        sz = z * _sigmoid(z)
        parts = []
        for hh in range(tc // GDN_HEAD_DIM):
            oh = o_ref[:, hh * GDN_HEAD_DIM:(hh + 1) * GDN_HEAD_DIM]
            r = lax.rsqrt(jnp.mean(oh * oh, axis=-1, keepdims=True) + EPS)
            parts.append(oh * r * g_ref[...])
        y_ref[...] = (jnp.concatenate(parts, axis=1) * sz).astype(BF16)

    blk = pl.BlockSpec((tm, tc), lambda i, j: (i, j))
    return _pcall(body, name="gdn_outgate", grid=(T // tm, GDN_VW // tc),
                  in_specs=[blk, pl.BlockSpec((tm, tc), lambda i, j: (i, j + z0)), pl.BlockSpec((1, GDN_HEAD_DIM), lambda i, j: (0, 0))],
                  out_specs=blk, out_shape=jax.ShapeDtypeStruct((T, GDN_VW), BF16),
                  compiler_params=_params(("parallel", "parallel")))(o, proj, gain)


def _outgate_bwd(dy, o, proj, gain):
    T = o.shape[0]
    tm, tc = _pick(T, CONV_ROWS), CONV_COLS
    z0 = GDN_CONV_W // tc
    nh = tc // GDN_HEAD_DIM

    def body(dy_ref, o_ref, z_ref, g_ref, do_ref, dz_ref, dg_ref):
        z = z_ref[...]
        sg = _sigmoid(z)
        sz = z * sg
        dy = dy_ref[...]
        dgain = jnp.zeros((1, GDN_HEAD_DIM), F32)
        dos, ys = [], []
        for hh in range(nh):
            sl = slice(hh * GDN_HEAD_DIM, (hh + 1) * GDN_HEAD_DIM)
            oh = o_ref[:, sl]
            r = lax.rsqrt(jnp.mean(oh * oh, axis=-1, keepdims=True) + EPS)
            xh = oh * r
            dn = dy[:, sl] * sz[:, sl]
            dgain = dgain + jnp.sum(dn * xh, axis=0, keepdims=True)
            dxh = dn * g_ref[...]
            dos.append(r * (dxh - xh * jnp.mean(dxh * xh, axis=-1, keepdims=True)))
            ys.append(xh * g_ref[...])
        do_ref[...] = jnp.concatenate(dos, axis=1)
        dz_ref[...] = (dy * jnp.concatenate(ys, axis=1) * sg * (1.0 + z * (1.0 - sg))).astype(BF16)
        first = (pl.program_id(0) == 0) & (pl.program_id(1) == 0)

        @pl.when(first)
        def _():
            dg_ref[...] = dgain

        @pl.when(jnp.logical_not(first))
        def _():
            dg_ref[...] += dgain

    blk = pl.BlockSpec((tm, tc), lambda i, j: (i, j))
    vec = pl.BlockSpec((1, GDN_HEAD_DIM), lambda i, j: (0, 0))
    return _pcall(body, name="gdn_doutgate", grid=(T // tm, GDN_VW // tc),
                  in_specs=[blk, blk, pl.BlockSpec((tm, tc), lambda i, j: (i, j + z0)), vec],
                  out_specs=[blk, blk, vec],
                  out_shape=[jax.ShapeDtypeStruct((T, GDN_VW), F32), jax.ShapeDtypeStruct((T, GDN_VW), BF16),
                             jax.ShapeDtypeStruct((1, GDN_HEAD_DIM), F32)],
                  compiler_params=_params(("arbitrary", "arbitrary")))(dy, o, proj, gain)


def _pad_lanes(vec):
    return jnp.pad(vec.reshape(1, -1), ((0, 0), (0, LANE - vec.shape[-1])))


def _head_rows(a):
    T = a.shape[0]
    return a[:, :GDN_V_HEADS].T.reshape(GDN_V_HEADS, T // GDN_CHUNK, 1, GDN_CHUNK)


def _gdn_pad_in(w_in):
    c = GDN_CONV_W + GDN_VW
    z = jnp.zeros(w_in.shape[:-1] + (LANE - GDN_V_HEADS,), w_in.dtype)
    return jnp.concatenate([w_in[..., :c + GDN_V_HEADS], z, w_in[..., c + GDN_V_HEADS:], z], axis=-1)


def _gdn_unpad_in(dw):
    c = GDN_CONV_W + GDN_VW
    return jnp.concatenate([dw[..., :c + GDN_V_HEADS], dw[..., c + LANE:c + LANE + GDN_V_HEADS]], axis=-1)


def _gdn_mixer_fwd(h, g, w_in_pad, conv_w, a_log, dt_bias, out_gain, w_out):
    T = h.shape[0]
    hn = _rms_fwd(h, g, "gdn_norm")
    proj = _mm(hn, w_in_pad, "nn", name="gdn_in")
    qk = _conv_fwd(proj, conv_w, 0, 2 * GDN_KW, True, "gdn_conv_qk")
    vv = _conv_fwd(proj, conv_w, 2 * GDN_KW, GDN_VW, False, "gdn_conv_v")
    alog, dtb = _pad_lanes(a_log), _pad_lanes(dt_bias)
    beta, gl, gc = _gates_fwd(proj, alog, dtb)
    grow = _head_rows(gc)
    qn, kn = qk[:, :GDN_KW], qk[:, GDN_KW:]
    amat = _gdn_prep(kn, beta, gc, grow)
    o, states, vnew = _gdn_fwd(qn, kn, vv, beta, gc, grow, amat)
    gain = out_gain.reshape(1, GDN_HEAD_DIM)
    y = _outgate_fwd(o, proj, gain)
    h2 = _mm(y, w_out, "nn", res=h, name="gdn_out")
    return h2, (h, hn, proj, qn, kn, vv, beta, gl, gc, grow, o, states, amat, vnew, y, alog, dtb, gain)


def _gdn_mixer_bwd(dh2, saved, g, w_in_pad, conv_w, w_out):
    h, hn, proj, qn, kn, vv, beta, gl, gc, grow, o, states, amat, vnew, y, alog, dtb, gain = saved
    T = h.shape[0]
    dy = _mm(dh2, w_out, "nt", name="gdn_dy")
    dw_out = _mm(y, dh2, "tn", out_dtype=BF16, name="gdn_dwout")
    do, dz, dgain = _outgate_bwd(dy, o, proj, gain)
    dq, dk, dv, dbeta, dgc = _gdn_bwd(qn, kn, vv, beta, gc, grow, states, amat, vnew, do)
    dqk = jnp.concatenate([dq, dk], axis=1)
    dy_qk, dcw_qk = _conv_bwd_pre(proj, conv_w, dqk, 0, 2 * GDN_KW, True, "gdn_dconv_qk")
    dy_v, dcw_v = _conv_bwd_pre(proj, conv_w, dv, 2 * GDN_KW, GDN_VW, False, "gdn_dconv_v")
    dx_qk = _conv_bwd_in(dy_qk, conv_w[:, :2 * GDN_KW], "gdn_dconvin_qk")
    dx_v = _conv_bwd_in(dy_v, conv_w[:, 2 * GDN_KW:], "gdn_dconvin_v")
    dbl, da, dalog, ddt = _gates_bwd(proj, alog, dtb, beta, gl, dbeta, dgc)
    dproj = jnp.concatenate([dx_qk, dx_v, dz, dbl, da], axis=1)
    dw_in_pad = _mm(hn, dproj, "tn", out_dtype=BF16, name="gdn_dwin")
    dhn = _mm(dproj, w_in_pad, "nt", name="gdn_dhn")
    dh, dg = _rms_bwd(dhn, h, g, dh2, "gdn_dnorm")
    dconv = jnp.concatenate([dcw_qk, dcw_v], axis=1)
    return (dh, dg, _gdn_unpad_in(dw_in_pad), dconv, dalog[0, :GDN_V_HEADS], ddt[0, :GDN_V_HEADS],
            dgain.reshape(GDN_HEAD_DIM), dw_out)


def _instances(full):
    out = {}
    for n, a in full.items():
        if n.startswith("ffn_"):
            for i in range(2):
                for j in range(2):
                    out[(n, i, j)] = a[i, j]
        elif n in ("mix_norm", "ple_norm", "ple_w_gate", "ple_w_proj"):
            for i in range(2):
                out[(n, i)] = a[i]
        else:
            out[(n,)] = a[0]
    return out


def _stacked(inst):
    out = {}
    for n in dict.fromkeys(k[0] for k in inst):
        if n.startswith("ffn_"):
            out[n] = jnp.stack([jnp.stack([inst[(n, i, j)] for j in range(2)]) for i in range(2)])
        elif n in ("mix_norm", "ple_norm", "ple_w_gate", "ple_w_proj"):
            out[n] = jnp.stack([inst[(n, i)] for i in range(2)])
        else:
            out[n] = inst[(n,)][None]
    return out


def _local_step(x, p, target, w, late_shards=(), late_weights=None, early_grads=None, first_shards=(), first_weights=None):
    w = dict(w)
    ffn = lambda i, j: (w[("ffn_norm", i, j)], w[("ffn_w_gate", i, j)], w[("ffn_w_up", i, j)], w[("ffn_w_down", i, j)])
    h = x
    tape = []
    for i in range(2):
        if i == 0 and first_weights is not None:
            def wd_of(gathered):
                w.update(first_weights(gathered))
                return w[("ffn_w_down", 0, 0)]
            h, s1 = _ffn_fwd(h, w[("ffn_norm", 0, 0)], w[("ffn_w_gate", 0, 0)], w[("ffn_w_up", 0, 0)], None, "ffn0a",
                             first_shards, wd_of)
        else:
            h, s1 = _ffn_fwd(h, *ffn(i, 0), f"ffn{i}a")
        if i == 0:
            def w_out_of(gathered):
                if late_weights is not None:
                    w.update(late_weights(gathered))
                return w[("att_w_out",)]
            h, s2, _ = _att_fwd(h, w[("mix_norm", 0)], w[("att_w_in",)], w_out_of, w[("att_q_norm",)],
                                w[("att_k_norm",)], w[("att_sinks",)], late_shards)
        else:
            gdn_in_pad = _gdn_pad_in(w[("gdn_w_in",)])
            h, s2 = _gdn_mixer_fwd(h, w[("mix_norm", 1)], gdn_in_pad, w[("gdn_conv_w",)], w[("gdn_a_log",)],
                                   w[("gdn_dt_bias",)], w[("gdn_out_norm",)], w[("gdn_w_out",)])
        h, s3 = _ffn_fwd(h, *ffn(i, 1), f"ffn{i}b")
        h, s4 = _ple_fwd(h, p[i], w[("ple_norm", i)], w[("ple_w_gate", i)], w[("ple_w_proj", i)], f"ple{i}")
        tape.append((s1, s2, s3, s4))

    loss, dh = _loss_head(h, target)

    g = {}
    rode = []
    for i in (1, 0):
        s1, s2, s3, s4 = tape[i]
        dh, g[("ple_norm", i)], g[("ple_w_gate", i)], g[("ple_w_proj", i)] = _ple_bwd(
            dh, s4, p[i], w[("ple_norm", i)], w[("ple_w_gate", i)], f"ple{i}")
        dh, g[("ffn_norm", i, 1)], g[("ffn_w_gate", i, 1)], g[("ffn_w_up", i, 1)], g[("ffn_w_down", i, 1)] = _ffn_bwd(
            dh, s3, *ffn(i, 1), f"ffn{i}b")
        if i == 0:
            ride = early_grads(g) if early_grads is not None else ()
            (dh, g[("mix_norm", 0)], g[("att_w_in",)], g[("att_w_out",)], g[("att_q_norm",)], g[("att_k_norm",)],
             g[("att_sinks",)], rode) = _att_bwd(dh, s2, w[("mix_norm", 0)], w[("att_w_in",)], w[("att_w_out",)], ride)
        else:
            (dh, g[("mix_norm", 1)], g[("gdn_w_in",)], g[("gdn_conv_w",)], g[("gdn_a_log",)], g[("gdn_dt_bias",)],
             g[("gdn_out_norm",)], g[("gdn_w_out",)]) = _gdn_mixer_bwd(
                dh, s2, w[("mix_norm", 1)], gdn_in_pad, w[("gdn_conv_w",)], w[("gdn_w_out",)])
        dh, g[("ffn_norm", i, 0)], g[("ffn_w_gate", i, 0)], g[("ffn_w_up", i, 0)], g[("ffn_w_down", i, 0)] = _ffn_bwd(
            dh, s1, *ffn(i, 0), f"ffn{i}a")
    return loss, dh, g, rode


MESH = pl.DeviceIdType.MESH


def _place():
    x, y, c = lax.axis_index("x"), lax.axis_index("y"), lax.axis_index("c")
    others = [((1 - x, y), 2 * (1 - x) + y), ((x, 1 - y), 2 * x + (1 - y)), ((1 - x, 1 - y), 2 * (1 - x) + (1 - y))]
    return x, y, c, 4 * x + 2 * y + c, 2 * x + y, (x, y, 1 - c), others


def _comm_call(body, arrays, out_shape, n_sems, name):
    hbm = pl.BlockSpec(memory_space=pl.ANY)
    n = len(arrays)
    return _pcall(
        body, name=name, in_specs=[hbm] * n, out_specs=[hbm] * len(out_shape), out_shape=out_shape,
        scratch_shapes=[pltpu.SemaphoreType.DMA((n, n_sems)), pltpu.SemaphoreType.DMA((n, n_sems)),
                        pltpu.SemaphoreType.DMA((n, N_CHIP))],
        compiler_params=pltpu.CompilerParams(has_side_effects=True),
    )(*arrays)


def _gather_protocol(ins, outs, send_sems, recv_sems, local_sems):
    n = len(ins)
    x, y, c, me, my_chip, sibling, others = _place()

    def copy(a, k, block, to, src=None):
        dst = outs[a].at[block]
        return pltpu.make_async_remote_copy(
            src_ref=dst if src is None else src, dst_ref=dst, send_sem=send_sems.at[a, k],
            recv_sem=recv_sems.at[a, k], device_id=to, device_id_type=MESH)

    local = [pltpu.make_async_copy(ins[a], outs[a].at[me], local_sems.at[a, 0]) for a in range(n)]
    first = []
    for a in range(n):
        first.append(copy(a, 0, me, sibling, src=ins[a]))
        first += [copy(a, 1 + j, me, (*chip, c), src=ins[a]) for j, (chip, _) in enumerate(others)]

    def start():
        for cp in local + first:
            cp.start()

    def finish():
        passed = []
        for a in range(n):
            for j, (chip, chip_idx) in enumerate(others):
                blk = 2 * chip_idx + c
                copy(a, 1 + j, blk, (x, y, c)).wait_recv()
                fwd = copy(a, 4 + j, blk, sibling)
                fwd.start()
                passed.append(fwd)
        for a in range(n):
            copy(a, 0, 2 * my_chip + (1 - c), (x, y, c)).wait_recv()
            for j, (chip, chip_idx) in enumerate(others):
                copy(a, 4 + j, 2 * chip_idx + (1 - c), (x, y, c)).wait_recv()
        for cp in first + passed:
            cp.wait_send()
        for cp in local:
            cp.wait()

    return start, finish


def _all_gather(arrays):
    n = len(arrays)

    def body(*refs):
        start, finish = _gather_protocol(refs[:n], refs[n:2 * n], *refs[2 * n:])
        start()
        finish()

    out_shape = [jax.ShapeDtypeStruct((N_DEV,) + a.shape, a.dtype) for a in arrays]
    return _comm_call(body, arrays, out_shape, N_DEV - 1, "gather_weights")


def _exchange_sibling(arrays, name):
    n = len(arrays)

    def body(*refs):
        ins, got = refs[:n], refs[n:2 * n]
        send_sems, recv_sems, _ = refs[2 * n:]
        x, y, c, me, my_chip, sibling, others = _place()
        remote = []
        for a in range(n):
            for chip in range(N_CHIP):
                rc = pltpu.make_async_remote_copy(
                    src_ref=ins[a].at[2 * chip + (1 - c)], dst_ref=got[a].at[chip], send_sem=send_sems.at[a, chip],
                    recv_sem=recv_sems.at[a, chip], device_id=sibling, device_id_type=MESH)
                rc.start()
                remote.append(rc)
        for rc in remote:
            rc.wait()

    half = [jax.ShapeDtypeStruct((N_CHIP,) + a.shape[1:], a.dtype) for a in arrays]
    return _comm_call(body, arrays, half, N_CHIP, name)


def _chips_protocol(ins, outs, send_sems, recv_sems, local_sems):
    n = len(ins)
    x, y, c, me, my_chip, sibling, others = _place()
    local = [pltpu.make_async_copy(ins[a].at[my_chip], outs[a].at[my_chip], local_sems.at[a, 0]) for a in range(n)]
    remote = [pltpu.make_async_remote_copy(
        src_ref=ins[a].at[chip_idx], dst_ref=outs[a].at[my_chip], send_sem=send_sems.at[a, j],
        recv_sem=recv_sems.at[a, j], device_id=(*chip, c), device_id_type=MESH)
        for a in range(n) for j, (chip, chip_idx) in enumerate(others)]

    def start():
        for cp in local + remote:
            cp.start()

    def finish():
        for cp in remote + local:
            cp.wait()

    return start, finish


def _exchange_chips(arrays, name):
    n = len(arrays)

    def body(*refs):
        start, finish = _chips_protocol(refs[:n], refs[n:2 * n], *refs[2 * n:])
        start()
        finish()

    out_shape = [jax.ShapeDtypeStruct(a.shape, a.dtype) for a in arrays]
    return _comm_call(body, arrays, out_shape, N_CHIP - 1, name)


def _as_rows(a, lead):
    shp = a.shape
    return a.reshape(shp[:lead] + (math.prod(shp[lead:-1]), shp[-1]))


def _row_tile(rows, cap=512):
    if rows <= cap:
        return rows
    for t in range(cap - cap % 8, 0, -8):
        if rows % t == 0:
            return t
    return rows


def _pair_sum(send, got, name):
    a3, b3 = _as_rows(send, 1), _as_rows(got, 1)
    _, rows, last = b3.shape
    tr = _row_tile(rows, 2048)

    def body(c_ref, a_ref, b_ref, o_ref):
        o_ref[...] = (a_ref[...].astype(F32) + b_ref[...].astype(F32)).astype(o_ref.dtype)

    core = lax.axis_index("c").astype(jnp.int32).reshape(1)
    out = _pcall(
        body, name=name,
        grid_spec=pltpu.PrefetchScalarGridSpec(
            num_scalar_prefetch=1, grid=(N_CHIP, rows // tr),
            in_specs=[pl.BlockSpec((None, tr, last), lambda k, i, c_ref: (2 * k + c_ref[0], i, 0)),
                      pl.BlockSpec((None, tr, last), lambda k, i, c_ref: (k, i, 0))],
            out_specs=pl.BlockSpec((None, tr, last), lambda k, i, c_ref: (k, i, 0))),
        out_shape=jax.ShapeDtypeStruct(b3.shape, got.dtype), compiler_params=_params(("parallel", "parallel")),
    )(core, a3, b3)
    return out.reshape(got.shape)


def _adamw(parts, w, m, v, name):
    lead, (rows, last) = w.shape[:-2], w.shape[-2:]
    nl = len(lead)
    tr = _row_tile(rows, 1024)
    c1 = 1.0 / (1.0 - ADAM_B1 ** ADAM_STEP)
    c2 = 1.0 / (1.0 - ADAM_B2 ** ADAM_STEP)

    def body(p_ref, w_ref, m_ref, v_ref, g_ref, d_ref, nm_ref, nv_ref):
        g = p_ref[0].astype(F32)
        for chip in range(1, N_CHIP):
            g = g + p_ref[chip].astype(F32)
        mn = ADAM_B1 * m_ref[...] + (1.0 - ADAM_B1) * g
        vn = ADAM_B2 * v_ref[...] + (1.0 - ADAM_B2) * (g * g)
        g_ref[...] = g
        nm_ref[...] = mn
        nv_ref[...] = vn
        d_ref[...] = -ADAM_LR * ((mn * c1) / (jnp.sqrt(vn * c2) + ADAM_EPS) + ADAM_WD * w_ref[...])

    row = pl.BlockSpec((None,) * nl + (tr, last), lambda *ix: ix + (0,))
    part = pl.BlockSpec((N_CHIP,) + (None,) * nl + (tr, last), lambda *ix: (0,) + ix + (0,))
    sh = jax.ShapeDtypeStruct(w.shape, F32)
    return _pcall(body, name=name, grid=lead + (rows // tr,), in_specs=[part, row, row, row],
                  out_specs=[row, row, row, row], out_shape=[sh, sh, sh, sh],
                  compiler_params=_params(("parallel",) * (nl + 1)))(parts, w, m, v)


def _pack(pieces, row_align):
    rows, offs, r = [], [], 0
    for a in pieces:
        flat = a.reshape(-1)
        nr = -(-flat.shape[0] // PACK_W)
        flat = jnp.pad(flat, (0, nr * PACK_W - flat.shape[0]))
        rows.append(flat.reshape(nr, PACK_W))
        offs.append(r)
        r += nr
    pad = (-r) % row_align
    if pad:
        rows.append(jnp.zeros((pad, PACK_W), pieces[0].dtype))
    return jnp.concatenate(rows, axis=0), offs


def _unpack(flat, offs, shapes):
    out = []
    for off, shp in zip(offs, shapes):
        size = math.prod(shp)
        nr = -(-size // PACK_W)
        out.append(flat[..., off:off + nr, :].reshape(flat.shape[:-2] + (nr * PACK_W,))[..., :size].reshape(flat.shape[:-2] + tuple(shp)))
    return out


def _to_full(gathered, axis):
    z = jnp.moveaxis(gathered, 0, axis)
    shp = list(z.shape)
    return z.reshape(shp[:axis] + [shp[axis] * shp[axis + 1]] + shp[axis + 2:])


def _to_shards(full, axis):
    shp = list(full.shape)
    z = full.reshape(shp[:axis] + [N_DEV, shp[axis] // N_DEV] + shp[axis + 1:])
    return jnp.moveaxis(z, axis, 0)


def kernel(x, p, ffn_norm, ffn_w_gate, ffn_w_up, ffn_w_down, mix_norm, att_w_in, att_q_norm, att_k_norm, att_sinks, att_w_out, gdn_w_in, gdn_conv_w, gdn_a_log, gdn_dt_bias, gdn_out_norm, gdn_w_out, ple_norm, ple_w_gate, ple_w_proj, loss_target, m_ffn_norm, m_ffn_w_gate, m_ffn_w_up, m_ffn_w_down, m_mix_norm, m_att_w_in, m_att_q_norm, m_att_k_norm, m_att_sinks, m_att_w_out, m_gdn_w_in, m_gdn_conv_w, m_gdn_a_log, m_gdn_dt_bias, m_gdn_out_norm, m_gdn_w_out, m_ple_norm, m_ple_w_gate, m_ple_w_proj, v_ffn_norm, v_ffn_w_gate, v_ffn_w_up, v_ffn_w_down, v_mix_norm, v_att_w_in, v_att_q_norm, v_att_k_norm, v_att_sinks, v_att_w_out, v_gdn_w_in, v_gdn_conv_w, v_gdn_a_log, v_gdn_dt_bias, v_gdn_out_norm, v_gdn_w_out, v_ple_norm, v_ple_w_gate, v_ple_w_proj):
    args = dict(locals())
    wts = {n: args[n] for n in WEIGHTS}
    mom = {n: args["m_" + n] for n in WEIGHTS}
    var = {n: args["v_" + n] for n in WEIGHTS}
    axis = dict(SHARDED)
    vecs = [n for n, _ in SHARDED[:SMALL_SHARDED]]
    small = vecs + list(REPLICATED)
    small_shapes = [wts[n].shape for n in small]
    lead = lambda n: 2 if n.startswith("ffn_") else 1

    def stack_of(arrays, name, idxs):
        return jnp.stack([arrays[name][idx] if idx else arrays[name][0] for idx in idxs])

    def full_instances(gathered, group):
        out = {}
        for (name, idxs), g in zip(group, gathered):
            whole = _to_full(g, axis[name] - lead(name) + 1)
            for k, idx in enumerate(idxs):
                out[(name,) + idx] = whole[k]
        return out

    def shard_stacks(g, group):
        return [_to_shards(jnp.stack([g[(name,) + idx] for idx in idxs]), axis[name] - lead(name) + 1)
                for name, idxs in group]

    vec_pack, voffs = _pack([wts[n] for n in vecs], 8)
    early = _all_gather([stack_of(wts, n, idxs).astype(BF16) for n, idxs in EARLY] + [vec_pack])
    w = full_instances(early[:-1], EARLY)
    vec_full = {n: _to_full(piece, axis[n]) for n, piece in
                zip(vecs, _unpack(early[-1], voffs, [wts[n].shape for n in vecs]))}
    w.update(_instances({**vec_full, **{n: wts[n] for n in REPLICATED}}))
    first_shards = [stack_of(wts, n, idxs).astype(BF16) for n, idxs in FIRST]
    late_shards = [stack_of(wts, n, idxs).astype(BF16) for n, idxs in LATE]

    def early_grads(g):
        send = shard_stacks(g, RIDE)
        got = _exchange_sibling(send, "exchange_sibling_early")
        return [_pair_sum(p_, q_, f"pair_sum_early_{i}") for i, (p_, q_) in enumerate(zip(send, got))]

    loss, grad_x, g, rode = _local_step(x[0], p[:, 0], loss_target[0], w, late_shards,
                                        lambda gathered: full_instances(gathered, LATE), early_grads,
                                        first_shards, lambda gathered: full_instances(gathered, FIRST))

    gs = _stacked({k: v for k, v in g.items() if k[0] in small})
    vec_shards = [_to_shards(gs[n], axis[n]) for n in vecs]
    small_send = jnp.stack([_pack([sh[d] for sh in vec_shards] + [gs[n] for n in REPLICATED] + [loss.reshape(1)], 8)[0]
                            for d in range(N_DEV)])
    send = shard_stacks(g, FINAL) + [small_send]
    got = _exchange_sibling(send, "exchange_sibling_final")
    chip_sums = [_pair_sum(p_, q_, f"pair_sum_final_{i}") for i, (p_, q_) in enumerate(zip(send, got))]
    last = _exchange_chips(chip_sums, "exchange_chips_final")

    pieces = {}
    for (name, idxs), part in list(zip(RIDE, rode)) + list(zip(FINAL, last[:-1])):
        for k, idx in enumerate(idxs):
            pieces[(name,) + idx] = part[:, k]
    outs = {}
    for n, _ in SHARDED[SMALL_SHARDED:]:
        if lead(n) == 2:
            part = jnp.stack([jnp.stack([pieces[(n, i, j)] for j in range(2)], axis=1) for i in range(2)], axis=1)
        elif (n, 0) in pieces:
            part = jnp.stack([pieces[(n, i)] for i in range(2)], axis=1)
        else:
            part = pieces[(n,)][:, None]
        outs[n] = _adamw(part, wts[n], mom[n], var[n], f"adamw_{n}")
    filler = [jnp.zeros((1,), F32)]
    small_w, soffs = _pack([wts[n] for n in small] + filler, 8)
    small_m, _ = _pack([mom[n] for n in small] + filler, 8)
    small_v, _ = _pack([var[n] for n in small] + filler, 8)
    small_out = [_unpack(z, soffs, small_shapes + [(1,)]) for z in _adamw(last[-1], small_w, small_m, small_v, "adamw_small")]
    loss = small_out[0][-1][0]
    for i, n in enumerate(small):
        outs[n] = [small_out[k][i] for k in range(4)]
    result = [loss, grad_x[None]]
    for k in range(4):
        result += [outs[n][k] for n in WEIGHTS]
    return tuple(result)
```

```python
import math

import jax
import jax.numpy as jnp
from jax import lax
from jax.experimental import pallas as pl
from jax.experimental.pallas import tpu as pltpu

F32 = jnp.float32
BF16 = jnp.bfloat16

N_DEV = 8
N_CHIP = 4
D_MODEL = 1024
D_FF = 2816
PLE_DIM = 256
HEAD_DIM = 64
SB_HEADS = 8
SWA_HEADS = 8
SWA_KV_HEADS = 2
SWA_GROUP = SWA_HEADS // SWA_KV_HEADS
WINDOW = 128
Q_BLOCK = 128
GDN_K_HEADS = 8
GDN_V_HEADS = 16
GDN_HEAD_DIM = 128
GDN_CONV = 4
GDN_CHUNK = 64
EPS = 1e-6
SB_W = SB_HEADS * HEAD_DIM
SWA_QW = SWA_HEADS * HEAD_DIM
SWA_KVW = SWA_KV_HEADS * HEAD_DIM
ATT_IN = 3 * SB_W + SWA_QW + 2 * SWA_KVW
GDN_KW = GDN_K_HEADS * GDN_HEAD_DIM
GDN_VW = GDN_V_HEADS * GDN_HEAD_DIM
GDN_CONV_W = 2 * GDN_KW + GDN_VW
GDN_IN = GDN_CONV_W + GDN_VW + 2 * GDN_V_HEADS
GDN_IN_PAD = GDN_CONV_W + GDN_VW + 2 * 128

ADAM_LR = 0.001
ADAM_B1 = 0.9
ADAM_B2 = 0.999
ADAM_EPS = 1e-08
ADAM_WD = 0.01
ADAM_STEP = 10

LANE = 128
VMEM_LIMIT = 56 * 1024 * 1024
MM_TILE_BUDGET = 40 * 1024 * 1024
PACK_W = 1024

NN = ((1,), (0,))
NT = ((1,), (1,))
TN = ((0,), (0,))

SHARDED = (
    ("ffn_norm", 2), ("gdn_conv_w", 2),
    ("ffn_w_gate", 3), ("ffn_w_up", 3), ("ffn_w_down", 2), ("att_w_in", 2), ("att_w_out", 1),
    ("gdn_w_in", 2), ("gdn_w_out", 1), ("ple_w_gate", 1), ("ple_w_proj", 2),
)
SMALL_SHARDED = 2
REPLICATED = ("mix_norm", "att_q_norm", "att_k_norm", "att_sinks", "gdn_a_log", "gdn_dt_bias",
              "gdn_out_norm", "ple_norm")
WEIGHTS = ("ffn_norm", "ffn_w_gate", "ffn_w_up", "ffn_w_down", "mix_norm", "att_w_in", "att_q_norm",
           "att_k_norm", "att_sinks", "att_w_out", "gdn_w_in", "gdn_conv_w", "gdn_a_log", "gdn_dt_bias",
           "gdn_out_norm", "gdn_w_out", "ple_norm", "ple_w_gate", "ple_w_proj")


_FFN_REST = [(0, 1), (1, 0), (1, 1)]
EARLY = [("ffn_w_gate", [(0, 0)]), ("ffn_w_up", [(0, 0)])]
FIRST = [("ffn_w_down", [(0, 0)]), ("att_w_in", [()])]
LATE = ([(n, [idx]) for n in ("ffn_w_gate", "ffn_w_up", "ffn_w_down") for idx in _FFN_REST]
        + [("att_w_out", [()]), ("gdn_w_in", [()]), ("gdn_w_out", [()]),
           ("ple_w_gate", [(0,), (1,)]), ("ple_w_proj", [(0,), (1,)])])
RIDE = [e for e in LATE if e[0] != "att_w_out"]
FINAL = EARLY + FIRST + [("att_w_out", [()])]


def _pcall(body, **kw):
    return pl.pallas_call(body, **kw)


def _params(sem=None):
    if sem is None:
        return pltpu.CompilerParams(vmem_limit_bytes=VMEM_LIMIT)
    return pltpu.CompilerParams(dimension_semantics=sem, vmem_limit_bytes=VMEM_LIMIT)


def _ride_specs(ride, out_shapes, n_sems):
    hbm = pl.BlockSpec(memory_space=pl.ANY)
    n = len(ride)
    sems = [pltpu.SemaphoreType.DMA((n, n_sems)), pltpu.SemaphoreType.DMA((n, n_sems)),
            pltpu.SemaphoreType.DMA((n, N_CHIP))] if n else []
    return [hbm] * n, [hbm] * len(out_shapes), sems


def _dot(a, b, dims=NN):
    return lax.dot_general(a, b, (dims, ((), ())), preferred_element_type=F32)


def _bdot(a, b, dims=NN):
    return _dot(a.astype(BF16), b.astype(BF16), dims)


def _split(a):
    hi = a.astype(BF16)
    lo = (a - hi.astype(F32)).astype(BF16)
    return hi, lo


def _dot3(a, b, dims=NN):
    ah, al = _split(a)
    bh, bl = _split(b)
    return _dot(ah, bh, dims) + (_dot(ah, bl, dims) + _dot(al, bh, dims))


def _dot2m(a, m, dims=NN):
    ah, al = _split(a)
    return _dot(ah, m, dims) + _dot(al, m, dims)


def _mdot2(m, a, dims=NN):
    ah, al = _split(a)
    return _dot(m, ah, dims) + _dot(m, al, dims)


def _sigmoid(x):
    return 1.0 / (1.0 + jnp.exp(-x))


def _softplus(x):
    return jnp.maximum(x, 0.0) + jnp.log(1.0 + jnp.exp(-jnp.abs(x)))


def _pick(n, cap):
    if n <= cap:
        return n
    for t in range(cap - cap % LANE, 0, -LANE):
        if n % t == 0:
            return t
    raise ValueError(f"no tile for {n} under {cap}")


def _iota2(shape, axis):
    return lax.broadcasted_iota(jnp.int32, shape, axis)


def _mm(a, b, mode, out_dtype=F32, res=None, alpha=1.0, a2=None, b2=None, name="mm"):
    if mode == "nn":
        (M, K), N = a.shape, b.shape[1]
    elif mode == "nt":
        (M, K), N = a.shape, b.shape[0]
    else:
        (K, M), N = a.shape, b.shape[1]
    tn, tk = _pick(N, 1408), _pick(K, 2048 if mode == "tn" else 1408)
    nk = K // tk
    pairs = 1 if a2 is None else 2

    def tile_bytes(tm):
        per = pairs * tk * (tm * a.dtype.itemsize + tn * b.dtype.itemsize) + tm * tn * jnp.dtype(out_dtype).itemsize
        return 2 * (per + (tm * tn * 4 if res is not None else 0)) + (tm * tn * 4 if nk > 1 else 0)

    tm = next(t for t in (_pick(M, c) for c in ((1408,) if mode == "tn" else (2048, 1024, 512))) if tile_bytes(t) <= MM_TILE_BUDGET or t <= 512)
    dims = {"nn": NN, "nt": NT, "tn": TN}[mode]
    a_spec = pl.BlockSpec((tk, tm), lambda i, j, k: (k, i)) if mode == "tn" else pl.BlockSpec((tm, tk), lambda i, j, k: (i, k))
    b_spec = pl.BlockSpec((tn, tk), lambda i, j, k: (j, k)) if mode == "nt" else pl.BlockSpec((tk, tn), lambda i, j, k: (k, j))
    o_spec = pl.BlockSpec((tm, tn), lambda i, j, k: (i, j))
    two = a2 is not None
    has_res = res is not None
    a2_spec, b2_spec = a_spec, b_spec
    if two and a2.shape != a.shape:
        assert nk == 1 and mode == "nn" and a2.shape[0] == M and b2.shape[1] == N
        a2_spec = pl.BlockSpec((tm, a2.shape[1]), lambda i, j, k: (i, 0))
        b2_spec = pl.BlockSpec((a2.shape[1], tn), lambda i, j, k: (0, j))

    def body(*refs):
        refs = list(refs)
        a_ref, b_ref = refs[0], refs[1]
        pos = 2
        if two:
            a2_ref, b2_ref = refs[2], refs[3]
            pos = 4
        if has_res:
            res_ref = refs[pos]
            pos += 1
        o_ref, acc_ref = refs[pos], refs[pos + 1]
        k = pl.program_id(2)
        part = _bdot(a_ref[...], b_ref[...], dims)
        if two:
            part = part + _bdot(a2_ref[...], b2_ref[...], dims)

        def finish(acc):
            out = acc * alpha if alpha != 1.0 else acc
            if has_res:
                out = res_ref[...] + out
            o_ref[...] = out.astype(out_dtype)

        if nk == 1:
            finish(part)
        else:
            @pl.when(k == 0)
            def _():
                acc_ref[...] = part

            @pl.when(k > 0)
            def _():
                acc_ref[...] += part

            @pl.when(k == nk - 1)
            def _():
                finish(acc_ref[...])

    ins = [a, b]
    specs = [a_spec, b_spec]
    if two:
        ins += [a2, b2]
        specs += [a2_spec, b2_spec]
    if has_res:
        ins.append(res)
        specs.append(o_spec)
    return _pcall(
        body, name=name, grid=(M // tm, N // tn, nk), in_specs=specs, out_specs=o_spec,
        out_shape=jax.ShapeDtypeStruct((M, N), out_dtype),
        scratch_shapes=[pltpu.VMEM((tm, tn) if nk > 1 else (8, LANE), F32)],
        compiler_params=_params(("parallel", "parallel", "arbitrary")),
    )(*ins)


ROW_TILE = 1024


def _rms_fwd(h, g, name):
    T, D = h.shape
    tr = _pick(T, ROW_TILE)

    def body(h_ref, g_ref, n_ref):
        x = h_ref[...]
        r = lax.rsqrt(jnp.mean(x * x, axis=-1, keepdims=True) + EPS)
        n_ref[...] = (x * r * g_ref[...]).astype(BF16)

    return _pcall(
        body, name=name, grid=(T // tr,),
        in_specs=[pl.BlockSpec((tr, D), lambda i: (i, 0)), pl.BlockSpec((1, D), lambda i: (0, 0))],
        out_specs=pl.BlockSpec((tr, D), lambda i: (i, 0)),
        out_shape=jax.ShapeDtypeStruct((T, D), BF16), compiler_params=_params(("parallel",)),
    )(h, g.reshape(1, D))


def _rms_bwd(dn, h, g, dres, name):
    T, D = h.shape
    tr = _pick(T, ROW_TILE)

    def body(dn_ref, h_ref, g_ref, dres_ref, dh_ref, dg_ref):
        x = h_ref[...]
        r = lax.rsqrt(jnp.mean(x * x, axis=-1, keepdims=True) + EPS)
        xh = x * r
        d = dn_ref[...].astype(F32)
        dxh = d * g_ref[...]
        dh_ref[...] = dres_ref[...] + r * (dxh - xh * jnp.mean(dxh * xh, axis=-1, keepdims=True))
        part = jnp.sum(d * xh, axis=0, keepdims=True)

        @pl.when(pl.program_id(0) == 0)
        def _():
            dg_ref[...] = part

        @pl.when(pl.program_id(0) > 0)
        def _():
            dg_ref[...] += part

    row = pl.BlockSpec((tr, D), lambda i: (i, 0))
    vec = pl.BlockSpec((1, D), lambda i: (0, 0))
    dh, dg = _pcall(
        body, name=name, grid=(T // tr,), in_specs=[row, row, vec, row], out_specs=[row, vec],
        out_shape=[jax.ShapeDtypeStruct((T, D), F32), jax.ShapeDtypeStruct((1, D), F32)],
        compiler_params=_params(("arbitrary",)),
    )(dn, h, g.reshape(1, D), dres)
    return dh, dg.reshape(D)


def _gateup(n, wg, wu, name, ride=()):
    T, D = n.shape
    F = wg.shape[1]
    tm, tn = _pick(T, 1024), _pick(F, 1408)
    nr = len(ride)
    ride_out = [jax.ShapeDtypeStruct((N_DEV,) + r.shape, r.dtype) for r in ride]
    ride_in_specs, ride_out_specs, ride_sems = _ride_specs(ride, ride_out, N_DEV - 1)
    grid = (T // tm, F // tn)

    def body(*refs):
        n_ref, wg_ref, wu_ref = refs[:3]
        a_ref, b_ref, hid_ref = refs[3 + nr:6 + nr]
        if nr:
            i, j = pl.program_id(0), pl.program_id(1)
            start, finish = _gather_protocol(refs[3:3 + nr], refs[6 + nr:6 + 2 * nr], *refs[6 + 2 * nr:])
            pl.when((i == 0) & (j == 0))(start)
        x = n_ref[...]
        a = _dot(x, wg_ref[...])
        b = _dot(x, wu_ref[...])
        a_ref[...] = a.astype(BF16)
        b_ref[...] = b.astype(BF16)
        hid_ref[...] = (a * _sigmoid(a) * b).astype(BF16)
        if nr:
            pl.when((i == grid[0] - 1) & (j == grid[1] - 1))(finish)

    o_spec = pl.BlockSpec((tm, tn), lambda i, j: (i, j))
    w_spec = pl.BlockSpec((D, tn), lambda i, j: (0, j))
    sh = jax.ShapeDtypeStruct((T, F), BF16)
    res = _pcall(
        body, name=name, grid=grid,
        in_specs=[pl.BlockSpec((tm, D), lambda i, j: (i, 0)), w_spec, w_spec] + ride_in_specs,
        out_specs=[o_spec, o_spec, o_spec] + ride_out_specs, out_shape=[sh, sh, sh] + ride_out,
        scratch_shapes=ride_sems,
        compiler_params=_params(("arbitrary", "arbitrary") if nr else ("parallel", "parallel")),
    )(n, wg, wu, *ride)
    return res[0], res[1], res[2], list(res[3:])


def _ffn_dhid(dy, wd, a, b, name):
    T, D = dy.shape
    F = wd.shape[0]
    tm, tn = _pick(T, 1024), _pick(F, 1408)

    def body(dy_ref, wd_ref, a_ref, b_ref, da_ref, db_ref):
        dhid = 0.5 * _bdot(dy_ref[...], wd_ref[...], NT)
        av = a_ref[...].astype(F32)
        bv = b_ref[...].astype(F32)
        s = _sigmoid(av)
        da_ref[...] = (dhid * bv * s * (1.0 + av * (1.0 - s))).astype(BF16)
        db_ref[...] = (dhid * av * s).astype(BF16)

    o_spec = pl.BlockSpec((tm, tn), lambda i, j: (i, j))
    sh = jax.ShapeDtypeStruct((T, F), BF16)
    return _pcall(
        body, name=name, grid=(T // tm, F // tn),
        in_specs=[pl.BlockSpec((tm, D), lambda i, j: (i, 0)), pl.BlockSpec((tn, D), lambda i, j: (j, 0)), o_spec, o_spec],
        out_specs=[o_spec, o_spec], out_shape=[sh, sh],
        compiler_params=_params(("parallel", "parallel")),
    )(dy, wd, a, b)


def _ffn_fwd(h, g, wg, wu, wd, tag, ride=(), wd_of=None):
    n = _rms_fwd(h, g, f"{tag}_norm")
    a, b, hid, gathered = _gateup(n, wg, wu, f"{tag}_gateup", ride)
    if wd_of is not None:
        wd = wd_of(gathered)
    h2 = _mm(hid, wd, "nn", res=h, alpha=0.5, name=f"{tag}_down")
    return h2, (h, n, a, b, hid)


def _ffn_bwd(dh2, saved, g, wg, wu, wd, tag):
    h, n, a, b, hid = saved
    da, db = _ffn_dhid(dh2, wd, a, b, f"{tag}_dhid")
    dwd = _mm(hid, dh2, "tn", alpha=0.5, out_dtype=BF16, name=f"{tag}_dwd")
    dwg = _mm(n, da, "tn", out_dtype=BF16, name=f"{tag}_dwg")
    dwu = _mm(n, db, "tn", out_dtype=BF16, name=f"{tag}_dwu")
    dn = _mm(da, wg, "nt", a2=db, b2=wu, name=f"{tag}_dn")
    dh, dg = _rms_bwd(dn, h, g, dh2, f"{tag}_dnorm")
    return dh, dg, dwg, dwu, dwd


def _ple_fwd(h, p, g, w_gate, w_proj, tag):
    T, D = h.shape
    pn = _rms_fwd(h, g, f"{tag}_norm")
    tm, tn = _pick(T, 512), _pick(D, 1024)
    P = p.shape[1]

    def body(pn_ref, p_ref, wg_ref, wp_ref, h_ref, o_ref, gl_ref, pp_ref):
        gl = _dot(pn_ref[...], wg_ref[...])
        pp = _bdot(p_ref[...], wp_ref[...])
        gl_ref[...] = gl
        pp_ref[...] = pp
        o_ref[...] = h_ref[...] + _sigmoid(gl) * pp

    o_spec = pl.BlockSpec((tm, tn), lambda i, j: (i, j))
    sh = jax.ShapeDtypeStruct((T, D), F32)
    h2, gl, pp = _pcall(
        body, name=f"{tag}_fwd", grid=(T // tm, D // tn),
        in_specs=[pl.BlockSpec((tm, D), lambda i, j: (i, 0)), pl.BlockSpec((tm, P), lambda i, j: (i, 0)),
                  pl.BlockSpec((D, tn), lambda i, j: (0, j)), pl.BlockSpec((P, tn), lambda i, j: (0, j)), o_spec],
        out_specs=[o_spec, o_spec, o_spec], out_shape=[sh, sh, sh],
        compiler_params=_params(("parallel", "parallel")),
    )(pn, p, w_gate, w_proj, h)
    return h2, (h, pn, gl, pp)


def _ple_bwd(dh2, saved, p, g, w_gate, tag):
    h, pn, gl, pp = saved
    T, D = h.shape
    tr = _pick(T, ROW_TILE)

    def body(d_ref, gl_ref, pp_ref, dgl_ref, dpp_ref):
        d = d_ref[...]
        s = _sigmoid(gl_ref[...])
        dpp_ref[...] = (d * s).astype(BF16)
        dgl_ref[...] = (d * pp_ref[...] * s * (1.0 - s)).astype(BF16)

    row = pl.BlockSpec((tr, D), lambda i: (i, 0))
    sh = jax.ShapeDtypeStruct((T, D), BF16)
    dgl, dpp = _pcall(body, name=f"{tag}_dgate", grid=(T // tr,), in_specs=[row, row, row], out_specs=[row, row],
                      out_shape=[sh, sh], compiler_params=_params(("parallel",)))(dh2, gl, pp)
    dw_proj = _mm(p, dpp, "tn", out_dtype=BF16, name=f"{tag}_dwproj")
    dw_gate = _mm(pn, dgl, "tn", out_dtype=BF16, name=f"{tag}_dwgate")
    dpn = _mm(dgl, w_gate, "nt", name=f"{tag}_dpn")
    dh, dg = _rms_bwd(dpn, h, g, dh2, f"{tag}_dnorm")
    return dh, dg, dw_gate, dw_proj


def _loss_head(y, target):
    T, D = y.shape
    tr = _pick(T, ROW_TILE)

    def body(y_ref, t_ref, dy_ref, l_ref):
        e = y_ref[...] - t_ref[...]
        dy_ref[...] = e * (1.0 / D)
        part = jnp.sum(e * e, axis=0, keepdims=True)

        @pl.when(pl.program_id(0) == 0)
        def _():
            l_ref[...] = part

        @pl.when(pl.program_id(0) > 0)
        def _():
            l_ref[...] += part

    row = pl.BlockSpec((tr, D), lambda i: (i, 0))
    vec = pl.BlockSpec((1, D), lambda i: (0, 0))
    dy, l = _pcall(body, name="loss_head", grid=(T // tr,), in_specs=[row, row], out_specs=[row, vec],
                   out_shape=[jax.ShapeDtypeStruct((T, D), F32), jax.ShapeDtypeStruct((1, D), F32)],
                   compiler_params=_params(("arbitrary",)))(y, target)
    return (0.5 / D) * jnp.sum(l), dy


SB_LANES = SB_HEADS * 2 * HEAD_DIM


def _sb_consts():
    row = _iota2((Q_BLOCK, Q_BLOCK), 0)
    col = _iota2((Q_BLOCK, Q_BLOCK), 1)
    after = (row > col).astype(BF16)
    before = (row < col).astype(BF16)
    return col < row, after, before, col


def _sb_fwd(proj, ride=()):
    T = proj.shape[0]
    H, d, L = SB_HEADS, HEAD_DIM, 2 * HEAD_DIM
    nblk = T // Q_BLOCK
    scale = d ** -0.5
    n = len(ride)
    ride_out = [jax.ShapeDtypeStruct((N_DEV,) + a.shape, a.dtype) for a in ride]
    ride_in_specs, ride_out_specs, ride_sems = _ride_specs(ride, ride_out, N_DEV - 1)
    R = range(H)
    tile = lambda g: slice(g * L, (g + 1) * L)

    def body(*refs):
        q_ref, kv_ref = refs[:2]
        rin = refs[2:2 + n]
        o_ref, c_ref = refs[2 + n:4 + n]
        rout = refs[4 + n:4 + 2 * n]
        run_ref = refs[4 + 2 * n]
        i = pl.program_id(0)
        if n:
            start, finish = _gather_protocol(rin, rout, *refs[5 + 2 * n:])
            pl.when(i == 0)(start)
        causal, after, _, col = _sb_consts()
        qs = [q_ref[:, tile(g)] * scale for g in R]
        o_ref[...] = jnp.zeros_like(o_ref)
        c_ref[...] = jnp.zeros_like(c_ref)
        run_ref[...] = jnp.zeros_like(run_ref)

        def pair(j, diag):
            rows = pl.ds(pl.multiple_of(j * Q_BLOCK, Q_BLOCK), Q_BLOCK)
            kvj = [kv_ref[rows, tile(g)] for g in R]
            c = [run_ref[g] for g in R]
            acc = [o_ref[:, tile(g)] for g in R]
            cm = None if diag else [c_ref[:, tile(g)] for g in R]
            z = [_dot(qs[g], kvj[g], NT) for g in R]
            sp = [_softplus(z[g]) for g in R]
            lk = [jnp.where(causal, -sp[g], 0.0) if diag else -sp[g] for g in R]
            btw = [_dot2m(lk[g], after) for g in R]
            e = [jnp.exp((z[g] - sp[g]) + btw[g] + c[g]) for g in R]
            w = [jnp.where(causal, e[g], 0.0) if diag else e[g] for g in R]
            pv = [_bdot(w[g], kvj[g]) for g in R]
            rs = [jnp.sum(lk[g], axis=1, keepdims=True) for g in R]
            for g in R:
                o_ref[:, tile(g)] = acc[g] + pv[g]
                if not diag:
                    c_ref[:, tile(g)] = jnp.where(col == j, c[g], cm[g])
                run_ref[g] = c[g] + rs[g]

        pair(i, True)

        @pl.loop(0, i)
        def _(jj):
            pair(i - 1 - jj, False)

        if n:
            pl.when(i == nblk - 1)(finish)

    blk = pl.BlockSpec((Q_BLOCK, H * L), lambda i: (i, 0))
    full = pl.BlockSpec((T, H * L), lambda i: (0, 1))
    res = _pcall(
        body, name="sb_fwd", grid=(nblk,), in_specs=[blk, full] + ride_in_specs,
        out_specs=[blk, blk] + ride_out_specs,
        out_shape=[jax.ShapeDtypeStruct((T, H * L), F32), jax.ShapeDtypeStruct((T, H * L), F32)] + ride_out,
        scratch_shapes=[pltpu.VMEM((H, Q_BLOCK, 1), F32)] + ride_sems,
        compiler_params=_params(("arbitrary",)),
    )(proj, proj, *ride)
    return res[0], res[1], list(res[2:])


def _sb_bwd(proj, carry, do, ride=()):
    T = proj.shape[0]
    H, d, L = SB_HEADS, HEAD_DIM, 2 * HEAD_DIM
    nblk = T // Q_BLOCK
    scale = d ** -0.5
    n = len(ride)
    ride_out = [jax.ShapeDtypeStruct(a.shape, a.dtype) for a in ride]
    ride_in_specs, ride_out_specs, ride_sems = _ride_specs(ride, ride_out, N_CHIP - 1)
    R = range(H)
    tile = lambda g: slice(g * L, (g + 1) * L)

    def body(*refs):
        q_ref, kv_ref, c_ref, do_ref = refs[:4]
        rin = refs[4:4 + n]
        dq_ref, dkv_ref = refs[4 + n:6 + n]
        rout = refs[6 + n:6 + 2 * n]
        run_ref = refs[6 + 2 * n]
        i = pl.program_id(0)
        if n:
            start, finish = _chips_protocol(rin, rout, *refs[7 + 2 * n:])
            pl.when(i == 0)(start)

        @pl.when(i == 0)
        def _():
            dkv_ref[...] = jnp.zeros_like(dkv_ref)

        causal, after, before, col = _sb_consts()
        qs = [q_ref[:, tile(g)] * scale for g in R]
        dov = [do_ref[:, tile(g)] for g in R]
        qdo = [jnp.concatenate([qs[g], dov[g]], axis=0) for g in R]
        dq_ref[...] = jnp.zeros_like(dq_ref)
        run_ref[...] = jnp.zeros_like(run_ref)

        def pair(j, diag):
            rows = pl.ds(pl.multiple_of(j * Q_BLOCK, Q_BLOCK), Q_BLOCK)
            kvj = [kv_ref[rows, tile(g)] for g in R]
            gsum = [run_ref[g] for g in R]
            dq0 = [dq_ref[:, tile(g)] for g in R]
            dkv0 = [dkv_ref[rows, tile(g)] for g in R]
            cm = None if diag else [c_ref[:, tile(g)] for g in R]
            z = [_dot(qs[g], kvj[g], NT) for g in R]
            sp = [_softplus(z[g]) for g in R]
            lk = [jnp.where(causal, -sp[g], 0.0) if diag else -sp[g] for g in R]
            ls = [z[g] - sp[g] for g in R]
            logw = [ls[g] + _dot2m(lk[g], after) for g in R]
            if not diag:
                logw = [logw[g] + jnp.sum(jnp.where(col == j, cm[g], 0.0), axis=1, keepdims=True) for g in R]
            e = [jnp.exp(logw[g]) for g in R]
            w = [jnp.where(causal, e[g], 0.0) if diag else e[g] for g in R]
            gw = [_dot(dov[g], kvj[g], NT) * w[g] for g in R]
            gpre = [gsum[g] + _dot(gw[g].astype(BF16), before) for g in R]
            sig = [jnp.exp(ls[g]) for g in R]
            dz = [gw[g] * (1.0 - sig[g]) - sig[g] * gpre[g] for g in R]
            if diag:
                dz = [jnp.where(causal, dz[g], 0.0) for g in R]
            dzb = [dz[g].astype(BF16) for g in R]
            dq1 = [_dot(dzb[g], kvj[g]) for g in R]
            dkv1 = [_dot(jnp.concatenate([dzb[g], w[g].astype(BF16)], axis=0), qdo[g], TN) for g in R]
            gs1 = [jnp.sum(gw[g], axis=1, keepdims=True) for g in R]
            for g in R:
                dq_ref[:, tile(g)] = dq0[g] + dq1[g]
                dkv_ref[rows, tile(g)] = dkv0[g] + dkv1[g]
                run_ref[g] = gsum[g] + gs1[g]

        @pl.loop(0, i)
        def _(j):
            pair(j, False)

        pair(i, True)
        dq_ref[...] = dq_ref[...] * scale
        if n:
            pl.when(i == nblk - 1)(finish)

    blk = pl.BlockSpec((Q_BLOCK, H * L), lambda i: (i, 0))
    once = pl.Buffered(1)
    sh = jax.ShapeDtypeStruct((T, H * L), F32)
    res = _pcall(
        body, name="sb_bwd", grid=(nblk,),
        in_specs=[blk, pl.BlockSpec((T, H * L), lambda i: (0, 1), pipeline_mode=once), blk, blk] + ride_in_specs,
        out_specs=[blk, pl.BlockSpec((T, H * L), lambda i: (0, 0), pipeline_mode=once)] + ride_out_specs,
        out_shape=[sh, sh] + ride_out,
        scratch_shapes=[pltpu.VMEM((H, Q_BLOCK, 1), F32)] + ride_sems,
        compiler_params=_params(("arbitrary",)),
    )(proj, proj, carry, do, *ride)
    return res[0], res[1], list(res[2:])


def _swa_common(q_ref, kvp_ref, kvc_ref, qg_ref, kg_ref, sk_ref, sl_ref, n):
    W, d, G = WINDOW, HEAD_DIM, SWA_GROUP
    scale = d ** -0.5
    row = _iota2((W, 2 * W), 0)
    col = _iota2((W, 2 * W), 1)
    dist = row + W - col
    valid = (dist >= 0) & (dist < W) & ((n > 0) | (col >= W))
    distf = dist.astype(F32)
    kvcat = jnp.concatenate([kvp_ref[...], kvc_ref[...]], axis=0)
    KH, QH = range(SWA_KV_HEADS), range(SWA_HEADS)
    kraw = [kvcat[:, hk * d:(hk + 1) * d] for hk in KH]
    vcat = [kvcat[:, SWA_KVW + hk * d:SWA_KVW + (hk + 1) * d].astype(BF16) for hk in KH]
    rk = [lax.rsqrt(jnp.mean(kraw[hk] * kraw[hk], axis=-1, keepdims=True) + EPS) for hk in KH]
    kh = [kraw[hk] * rk[hk] for hk in KH]
    kn = [(kh[hk] * kg_ref[...]).astype(BF16) for hk in KH]
    qraw = [q_ref[:, h * d:(h + 1) * d] for h in QH]
    rq = [lax.rsqrt(jnp.mean(qraw[h] * qraw[h], axis=-1, keepdims=True) + EPS) for h in QH]
    qh = [qraw[h] * rq[h] for h in QH]
    qn = [(qh[h] * qg_ref[...]).astype(BF16) for h in QH]
    sink = [sk_ref[h:h + 1, :1] for h in QH]
    s = [jnp.where(valid, _dot(qn[h], kn[h // G], NT) * scale - sl_ref[h:h + 1, :1] * distf, -1e30) for h in QH]
    m = [jnp.maximum(jnp.max(s[h], axis=1, keepdims=True), sink[h]) for h in QH]
    p = [jnp.where(valid, jnp.exp(s[h] - m[h]), 0.0) for h in QH]
    esink = [jnp.exp(sink[h] - m[h]) for h in QH]
    den = [jnp.sum(p[h], axis=1, keepdims=True) + esink[h] for h in QH]
    prob = [p[h] / den[h] for h in QH]
    return vcat, rk, kh, kn, rq, qh, qn, esink, den, prob


def _swa_specs(T):
    W = WINDOW
    q = pl.BlockSpec((W, SWA_QW), lambda n: (n, 0))
    prev = pl.BlockSpec((W, 2 * SWA_KVW), lambda n: (jnp.maximum(n - 1, 0), SWA_QW // (2 * SWA_KVW)))
    cur = pl.BlockSpec((W, 2 * SWA_KVW), lambda n: (n, SWA_QW // (2 * SWA_KVW)))
    gain = pl.BlockSpec((1, HEAD_DIM), lambda n: (0, 0))
    perhead = pl.BlockSpec((SWA_HEADS, LANE), lambda n: (0, 0))
    return q, prev, cur, gain, perhead


def _swa_fwd(proj, qg, kg, sinks, slopes):
    T = proj.shape[0]
    W, d, G = WINDOW, HEAD_DIM, SWA_GROUP

    def body(q_ref, kvp_ref, kvc_ref, qg_ref, kg_ref, sk_ref, sl_ref, o_ref):
        vcat, _, _, _, _, _, _, _, _, prob = _swa_common(q_ref, kvp_ref, kvc_ref, qg_ref, kg_ref, sk_ref, sl_ref,
                                                         pl.program_id(0))
        outs = [_bdot(prob[h], vcat[h // G]) for h in range(SWA_HEADS)]
        o_ref[...] = jnp.concatenate(outs, axis=1).astype(BF16)

    q, prev, cur, gain, perhead = _swa_specs(T)
    return _pcall(
        body, name="swa_fwd", grid=(T // W,), in_specs=[q, prev, cur, gain, gain, perhead, perhead], out_specs=q,
        out_shape=jax.ShapeDtypeStruct((T, SWA_QW), BF16), compiler_params=_params(("parallel",)),
    )(proj, proj, proj, qg, kg, sinks, slopes)


def _swa_bwd(proj, qg, kg, sinks, slopes, do):
    T = proj.shape[0]
    W, d, G = WINDOW, HEAD_DIM, SWA_GROUP
    scale = d ** -0.5
    KH, QH = range(SWA_KV_HEADS), range(SWA_HEADS)

    def body(q_ref, kvp_ref, kvc_ref, qg_ref, kg_ref, sk_ref, sl_ref, do_ref,
             dq_ref, dkv_ref, dqg_ref, dkg_ref, dsk_ref):
        n = pl.program_id(0)

        @pl.when(n == 0)
        def _():
            dqg_ref[...] = jnp.zeros_like(dqg_ref)
            dkg_ref[...] = jnp.zeros_like(dkg_ref)
            dsk_ref[...] = jnp.zeros_like(dsk_ref)
            dkv_ref[...] = jnp.zeros_like(dkv_ref)

        vcat, rk, kh, kn, rq, qh, qn, esink, den, prob = _swa_common(q_ref, kvp_ref, kvc_ref, qg_ref, kg_ref,
                                                                     sk_ref, sl_ref, n)
        dov = [do_ref[:, h * d:(h + 1) * d].astype(BF16) for h in QH]
        dp = [_dot(dov[h], vcat[h // G], NT) for h in QH]
        dd = [jnp.sum(prob[h] * dp[h], axis=1, keepdims=True) for h in QH]
        dsb = [(prob[h] * (dp[h] - dd[h]) * scale).astype(BF16) for h in QH]
        dsink = [-jnp.sum((esink[h] / den[h]) * dd[h], axis=0, keepdims=True) for h in QH]
        dqn = [_dot(dsb[h], kn[h // G]) for h in QH]
        dkn_h = [_dot(dsb[h], qn[h], TN) for h in QH]
        dv_h = [_dot(prob[h].astype(BF16), dov[h], TN) for h in QH]
        dqh = [dqn[h] * qg_ref[...] for h in QH]
        dq = [rq[h] * (dqh[h] - qh[h] * jnp.mean(dqh[h] * qh[h], axis=-1, keepdims=True)) for h in QH]
        dkn = [sum(dkn_h[hk * G + g] for g in range(G)) for hk in KH]
        dvc = [sum(dv_h[hk * G + g] for g in range(G)) for hk in KH]
        dkh = [dkn[hk] * kg_ref[...] for hk in KH]
        dkraw = [rk[hk] * (dkh[hk] - kh[hk] * jnp.mean(dkh[hk] * kh[hk], axis=-1, keepdims=True)) for hk in KH]
        dq_ref[...] = jnp.concatenate(dq, axis=1)
        dqg_ref[...] += sum(jnp.sum(dqn[h] * qh[h], axis=0, keepdims=True) for h in QH)
        dkg_ref[...] += sum(jnp.sum(dkn[hk] * kh[hk], axis=0, keepdims=True) for hk in KH)
        rowh = _iota2((SWA_HEADS, LANE), 0)
        dsk_ref[...] += sum(jnp.where(rowh == h, dsink[h], 0.0) for h in QH)
        upd = jnp.concatenate(dkraw + dvc, axis=1)
        offp = pl.multiple_of(jnp.maximum(n - 1, 0) * W, W)
        offc = pl.multiple_of(n * W, W)
        dkv_ref[pl.ds(offp, W), :] += upd[:W]
        dkv_ref[pl.ds(offc, W), :] += upd[W:]

    q, prev, cur, gain, perhead = _swa_specs(T)
    kvfull = pl.BlockSpec((T, 2 * SWA_KVW), lambda n: (0, 0))
    gs = jax.ShapeDtypeStruct((1, d), F32)
    return _pcall(
        body, name="swa_bwd", grid=(T // W,), in_specs=[q, prev, cur, gain, gain, perhead, perhead, q],
        out_specs=[q, kvfull, gain, gain, perhead],
        out_shape=[jax.ShapeDtypeStruct((T, SWA_QW), F32), jax.ShapeDtypeStruct((T, 2 * SWA_KVW), F32), gs, gs,
                   jax.ShapeDtypeStruct((SWA_HEADS, LANE), F32)],
        compiler_params=_params(("arbitrary",)),
    )(proj, proj, proj, qg, kg, sinks, slopes, do)


def _alibi():
    s = [2.0 ** (-8.0 * (i + 1) / SWA_HEADS) for i in range(SWA_HEADS)]
    return jnp.broadcast_to(jnp.asarray(s, F32)[:, None], (SWA_HEADS, LANE))


def _head_tiles(lo, hi):
    shp = lo.shape[:-1]
    return jnp.concatenate([lo.reshape(shp + (SB_HEADS, HEAD_DIM)), hi.reshape(shp + (SB_HEADS, HEAD_DIM))],
                           axis=-1).reshape(shp + (SB_LANES,))


def _tile_halves(x):
    shp = x.shape[:-1]
    t = x.reshape(shp + (SB_HEADS, 2, HEAD_DIM))
    return t[..., 0, :].reshape(shp + (SB_W,)), t[..., 1, :].reshape(shp + (SB_W,))


def _att_in_weights(w_in):
    sq, sk, sv = w_in[:, :SB_W], w_in[:, SB_W:2 * SB_W], w_in[:, 2 * SB_W:3 * SB_W]
    return jnp.concatenate([_head_tiles(sq, jnp.zeros_like(sq)), _head_tiles(sk, sv)], axis=1), w_in[:, 3 * SB_W:]


def _att_out_weights(w_out):
    wo = w_out[:SB_W]
    return _head_tiles(jnp.zeros_like(wo).T, wo.T).T, w_out[SB_W:]


def _att_fwd(h, g, w_in, w_out_of, q_gain, k_gain, sinks, ride=()):
    hn = _rms_fwd(h, g, "att_norm")
    w_sb, w_swa = _att_in_weights(w_in)
    proj_sb = _mm(hn, w_sb, "nn", out_dtype=BF16, name="att_in_sb")
    proj_swa = _mm(hn, w_swa, "nn", name="att_in_swa")
    a_out, carry, gathered = _sb_fwd(proj_sb, ride)
    w_out = w_out_of(gathered)
    wo_sb, wo_swa = _att_out_weights(w_out)
    sk128 = jnp.broadcast_to(sinks.reshape(SWA_HEADS, 1), (SWA_HEADS, LANE))
    qg, kg = q_gain.reshape(1, HEAD_DIM), k_gain.reshape(1, HEAD_DIM)
    b_out = _swa_fwd(proj_swa, qg, kg, sk128, _alibi())
    h2 = _mm(a_out, wo_sb, "nn", res=h, a2=b_out, b2=wo_swa, name="att_out")
    return h2, (h, hn, proj_sb, proj_swa, carry, a_out, b_out, sk128, qg, kg), gathered


def _att_bwd(dh2, saved, g, w_in, w_out, ride=()):
    h, hn, proj_sb, proj_swa, carry, a_out, b_out, sk128, qg, kg = saved
    w_sb, w_swa = _att_in_weights(w_in)
    wo_sb, wo_swa = _att_out_weights(w_out)
    da = _mm(dh2, wo_sb, "nt", out_dtype=BF16, name="att_do_sb")
    db = _mm(dh2, wo_swa, "nt", name="att_do_swa")
    dwo_sb = _mm(a_out, dh2, "tn", out_dtype=BF16, name="att_dwout_sb")
    dwo_swa = _mm(b_out, dh2, "tn", out_dtype=BF16, name="att_dwout_swa")
    dw_out = jnp.concatenate([_tile_halves(dwo_sb.T)[1].T, dwo_swa], axis=0)
    dq, dkv, rode = _sb_bwd(proj_sb, carry, da, ride)
    dbq, dbkv, dqg, dkg, dsink = _swa_bwd(proj_swa, qg, kg, sk128, _alibi(), db)
    dproj = jnp.concatenate([dq.astype(BF16), dkv.astype(BF16), dbq.astype(BF16), dbkv.astype(BF16)], axis=1)
    w_all = jnp.concatenate([w_sb, w_swa], axis=1)
    dw_all = _mm(hn, dproj, "tn", out_dtype=BF16, name="att_dwin")
    dhn = _mm(dproj, w_all, "nt", name="att_dhn")
    dsq, _ = _tile_halves(dw_all[:, :SB_LANES])
    dsk, dsv = _tile_halves(dw_all[:, SB_LANES:2 * SB_LANES])
    dw_in = jnp.concatenate([dsq, dsk, dsv, dw_all[:, 2 * SB_LANES:]], axis=1)
    dh, dg = _rms_bwd(dhn, h, g, dh2, "att_dnorm")
    return dh, dg, dw_in, dw_out, dqg.reshape(HEAD_DIM), dkg.reshape(HEAD_DIM), dsink[:, 0], rode


CONV_ROWS = 1024
CONV_COLS = 1024
HALO = 8


def _shifted(xcat, s, tm):
    if s == 0:
        return xcat[HALO:HALO + tm]
    return pltpu.roll(xcat, s, 0)[HALO:HALO + tm]


def _conv_pre(x_ref, halo_ref, w_ref, i, tm):
    xc = x_ref[...]
    halo = jnp.where(i > 0, halo_ref[...], 0.0)
    xcat = jnp.concatenate([halo, xc], axis=0)
    w = w_ref[...]
    y = w[GDN_CONV - 1:GDN_CONV] * xc
    for kk in range(GDN_CONV - 1):
        y = y + w[kk:kk + 1] * _shifted(xcat, GDN_CONV - 1 - kk, tm)
    return xcat, y


def _l2_heads(s, qscale_of):
    outs, rs = [], []
    for hh in range(s.shape[1] // GDN_HEAD_DIM):
        sh = s[:, hh * GDN_HEAD_DIM:(hh + 1) * GDN_HEAD_DIM]
        r = lax.rsqrt(jnp.sum(sh * sh, axis=-1, keepdims=True) + EPS)
        outs.append(sh * r)
        rs.append(r)
    return outs, rs


def _conv_specs(T, col0, tm, tc):
    cur = pl.BlockSpec((tm, tc), lambda j, i: (i, j + col0 // tc))
    halo = pl.BlockSpec((HALO, tc), lambda j, i: (jnp.maximum(i * (tm // HALO) - 1, 0), j + col0 // tc))
    wsp = pl.BlockSpec((GDN_CONV, tc), lambda j, i: (0, j + col0 // tc))
    out = pl.BlockSpec((tm, tc), lambda j, i: (i, j))
    return cur, halo, wsp, out


def _conv_fwd(proj, conv_w, col0, width, norm, name):
    T = proj.shape[0]
    tm, tc = _pick(T, CONV_ROWS), CONV_COLS
    cur, halo, wsp, out = _conv_specs(T, col0, tm, tc)
    n_q_tiles = (width // 2) // tc

    def body(x_ref, halo_ref, w_ref, o_ref):
        j, i = pl.program_id(0), pl.program_id(1)
        _, y = _conv_pre(x_ref, halo_ref, w_ref, i, tm)
        s = y * _sigmoid(y)
        if norm:
            outs, _ = _l2_heads(s, None)
            qs = jnp.where(j < n_q_tiles, GDN_HEAD_DIM ** -0.5, 1.0)
            o_ref[...] = jnp.concatenate(outs, axis=1) * qs
        else:
            o_ref[...] = s

    return _pcall(body, name=name, grid=(width // tc, T // tm), in_specs=[cur, halo, wsp], out_specs=out,
                  out_shape=jax.ShapeDtypeStruct((T, width), F32),
                  compiler_params=_params(("parallel", "parallel")))(proj, proj, conv_w)


def _conv_bwd_pre(proj, conv_w, dout, col0, width, norm, name):
    T = proj.shape[0]
    tm, tc = _pick(T, CONV_ROWS), CONV_COLS
    cur, halo, wsp, out = _conv_specs(T, col0, tm, tc)
    n_q_tiles = (width // 2) // tc

    def body(x_ref, halo_ref, w_ref, d_ref, dy_ref, dw_ref):
        j, i = pl.program_id(0), pl.program_id(1)
        xcat, y = _conv_pre(x_ref, halo_ref, w_ref, i, tm)
        sg = _sigmoid(y)
        s = y * sg
        d = d_ref[...]
        if norm:
            qs = jnp.where(j < n_q_tiles, GDN_HEAD_DIM ** -0.5, 1.0)
            d = d * qs
            outs, rs = _l2_heads(s, None)
            parts = []
            for hh, (nh, r) in enumerate(zip(outs, rs)):
                dh = d[:, hh * GDN_HEAD_DIM:(hh + 1) * GDN_HEAD_DIM]
                parts.append(r * (dh - nh * jnp.sum(dh * nh, axis=-1, keepdims=True)))
            ds = jnp.concatenate(parts, axis=1)
        else:
            ds = d
        dy = ds * sg * (1.0 + y * (1.0 - sg))
        dy_ref[...] = dy
        rows = [jnp.sum(dy * _shifted(xcat, GDN_CONV - 1 - kk, tm), axis=0, keepdims=True) for kk in range(GDN_CONV)]
        part = jnp.concatenate(rows, axis=0)

        @pl.when(i == 0)
        def _():
            dw_ref[...] = part

        @pl.when(i > 0)
        def _():
            dw_ref[...] += part

    wout = pl.BlockSpec((GDN_CONV, tc), lambda j, i: (0, j))
    return _pcall(body, name=name, grid=(width // tc, T // tm), in_specs=[cur, halo, wsp, out], out_specs=[out, wout],
                  out_shape=[jax.ShapeDtypeStruct((T, width), F32), jax.ShapeDtypeStruct((GDN_CONV, width), F32)],
                  compiler_params=_params(("parallel", "arbitrary")))(proj, proj, conv_w, dout)


def _conv_bwd_in(dy, conv_w, name):
    T, C = dy.shape
    tm, tc = _pick(T, CONV_ROWS), CONV_COLS
    nrow = T // tm

    def body(d_ref, nxt_ref, w_ref, dx_ref):
        i = pl.program_id(0)
        dc = d_ref[...]
        nxt = jnp.where(i < nrow - 1, nxt_ref[...], 0.0)
        dcat = jnp.concatenate([dc, nxt], axis=0)
        w = w_ref[...]
        dx = w[GDN_CONV - 1:GDN_CONV] * dc
        for kk in range(GDN_CONV - 1):
            s = GDN_CONV - 1 - kk
            dx = dx + w[kk:kk + 1] * pltpu.roll(dcat, tm + HALO - s, 0)[:tm]
        dx_ref[...] = dx.astype(BF16)

    cur = pl.BlockSpec((tm, tc), lambda i, j: (i, j))
    nxt = pl.BlockSpec((HALO, tc), lambda i, j: (jnp.minimum((i + 1) * (tm // HALO), T // HALO - 1), j))
    wsp = pl.BlockSpec((GDN_CONV, tc), lambda i, j: (0, j))
    return _pcall(body, name=name, grid=(nrow, C // tc), in_specs=[cur, nxt, wsp], out_specs=cur,
                  out_shape=jax.ShapeDtypeStruct((T, C), BF16),
                  compiler_params=_params(("parallel", "parallel")))(dy, dy, conv_w)


GATE_ROWS = 512


def _chunk_mask(n, lower):
    row = _iota2((n, n), 0)
    col = _iota2((n, n), 1)
    same = (row // GDN_CHUNK) == (col // GDN_CHUNK)
    tri = (row >= col) if lower else (row <= col)
    return (same & tri).astype(BF16)


def _gates_fwd(proj, a_log, dt_bias):
    T = proj.shape[0]
    tm = _pick(T, GATE_ROWS)
    c0 = (GDN_CONV_W + GDN_VW) // LANE

    def body(bl_ref, a_ref, alog_ref, dt_ref, beta_ref, g_ref, gc_ref):
        beta_ref[...] = _sigmoid(bl_ref[...])
        g = -jnp.exp(alog_ref[...]) * _softplus(a_ref[...] + dt_ref[...])
        g_ref[...] = g
        gc_ref[...] = _mdot2(_chunk_mask(tm, True), g)

    blk = lambda c: pl.BlockSpec((tm, LANE), lambda i: (i, c))
    vec = pl.BlockSpec((1, LANE), lambda i: (0, 0))
    sh = jax.ShapeDtypeStruct((T, LANE), F32)
    return _pcall(body, name="gdn_gates", grid=(T // tm,), in_specs=[blk(c0), blk(c0 + 1), vec, vec],
                  out_specs=[blk(0), blk(0), blk(0)], out_shape=[sh, sh, sh],
                  compiler_params=_params(("parallel",)))(proj, proj, a_log, dt_bias)


def _gates_bwd(proj, a_log, dt_bias, beta, g, dbeta, dgc):
    T = proj.shape[0]
    tm = _pick(T, GATE_ROWS)
    c0 = (GDN_CONV_W + GDN_VW) // LANE

    def heads_in_lanes(ref):
        lane = _iota2((tm, LANE), 1)
        out = jnp.where(lane < GDN_GROUP, ref[0], 0.0)
        for grp in range(1, GDN_V_HEADS // GDN_GROUP):
            out = out + jnp.where(lane // GDN_GROUP == grp, pltpu.roll(ref[grp], grp * GDN_GROUP, 1), 0.0)
        return out

    def body(a_ref, alog_ref, dt_ref, beta_ref, g_ref, dbeta_ref, dgc_ref, dbl_ref, da_ref, dalog_ref, ddt_ref):
        dg = _mdot2(_chunk_mask(tm, False), heads_in_lanes(dgc_ref))
        b = beta_ref[...]
        dbl_ref[...] = (heads_in_lanes(dbeta_ref) * b * (1.0 - b)).astype(BF16)
        da = dg * (-jnp.exp(alog_ref[...])) * _sigmoid(a_ref[...] + dt_ref[...])
        da_ref[...] = da.astype(BF16)
        p1 = jnp.sum(dg * g_ref[...], axis=0, keepdims=True)
        p2 = jnp.sum(da, axis=0, keepdims=True)

        @pl.when(pl.program_id(0) == 0)
        def _():
            dalog_ref[...] = p1
            ddt_ref[...] = p2

        @pl.when(pl.program_id(0) > 0)
        def _():
            dalog_ref[...] += p1
            ddt_ref[...] += p2

    blk = lambda c: pl.BlockSpec((tm, LANE), lambda i: (i, c))
    vec = pl.BlockSpec((1, LANE), lambda i: (0, 0))
    grp = pl.BlockSpec((GDN_V_HEADS // GDN_GROUP, tm, LANE), lambda i: (0, i, 0))
    shb = jax.ShapeDtypeStruct((T, LANE), BF16)
    shv = jax.ShapeDtypeStruct((1, LANE), F32)
    return _pcall(body, name="gdn_dgates", grid=(T // tm,),
                  in_specs=[blk(c0 + 1), vec, vec, blk(0), blk(0), grp, grp],
                  out_specs=[blk(0), blk(0), vec, vec], out_shape=[shb, shb, shv, shv],
                  compiler_params=_params(("arbitrary",)))(proj, a_log, dt_bias, beta, g, dbeta, dgc)


def _inv_unit_lower(Ls):
    C = Ls[0].shape[0]
    row = _iota2((C, C), 0)
    col = _iota2((C, C), 1)
    blk16 = (row // 16) == (col // 16)
    blk32 = (row // 32) == (col // 32)
    eye = (row == col).astype(F32)
    xs = [-jnp.where(blk16, L, 0.0) for L in Ls]
    inv = [eye + x for x in xs]
    for _ in range(3):
        xs = [_dot3(x, x) for x in xs]
        inv = [a + _dot3(a, x) for a, x in zip(inv, xs)]
    for mask in (blk32 & ~blk16, ~blk32):
        t = [_dot3(a, jnp.where(mask, L, 0.0)) for a, L in zip(inv, Ls)]
        inv = [a - _dot3(ti, a) for a, ti in zip(inv, t)]
    return inv


GDN_GROUP = 4
GDN_PREP_CHUNKS = 32
GDN_STEP_CHUNKS = 8


def _gdn_specs(T):
    C, D, E, J = GDN_CHUNK, GDN_HEAD_DIM, GDN_GROUP, GDN_STEP_CHUNKS
    n = T // (C * J)
    qk = pl.BlockSpec((J * C, (E // 2) * D), lambda h, i: (i, h))
    vE = pl.BlockSpec((J * C, E * D), lambda h, i: (i, h))
    colv = pl.BlockSpec((J * C, LANE), lambda h, i: (i, 0))
    colo = pl.BlockSpec((None, J * C, LANE), lambda h, i: (h, i, 0))
    rowv = pl.BlockSpec((E, J, 1, C), lambda h, i: (h, i, 0, 0))
    st = pl.BlockSpec((E, J, D, D), lambda h, i: (h, i, 0, 0))
    am = pl.BlockSpec((E, J, C, C), lambda h, i: (h, i, 0, 0))
    return n, qk, vE, colv, colo, rowv, st, am


def _lane_col(blk, lane):
    return jnp.sum(jnp.where(_iota2(blk.shape, 1) == lane, blk, 0.0), axis=1, keepdims=True)


def _gdn_decay(gcol, grow):
    C = GDN_CHUNK
    row = _iota2((C, C), 0)
    col = _iota2((C, C), 1)
    incl = row >= col
    dm = jnp.where(incl, jnp.exp(jnp.where(incl, gcol - grow, 0.0)), 0.0)
    glast = grow[:, C - 1:C]
    return dm, jnp.exp(gcol), jnp.exp(glast), jnp.exp(glast - gcol), row > col, incl


def _gdn_prep(k, beta, gcol, grow):
    T = k.shape[0]
    C, D, B = GDN_CHUNK, GDN_HEAD_DIM, GDN_PREP_CHUNKS
    n = T // C

    def body(k_ref, b_ref, gc_ref, gr_ref, a_ref):
        idx = [(e, cb) for e in range(2) for cb in range(B)]
        kc = {cb: k_ref[cb * C:(cb + 1) * C, :] for cb in range(B)}
        lm = []
        head0 = 2 * pl.program_id(0)
        for e, cb in idx:
            beta = _lane_col(b_ref[cb * C:(cb + 1) * C, :], head0 + e)
            dm, _, _, _, strict, _ = _gdn_decay(_lane_col(gc_ref[cb * C:(cb + 1) * C, :], head0 + e), gr_ref[e, cb])
            lm.append(jnp.where(strict, _bdot(kc[cb] * beta, kc[cb], NT) * dm, 0.0))
        inv = _inv_unit_lower(lm)
        for (e, cb), a in zip(idx, inv):
            a_ref[e, cb] = a

    return _pcall(
        body, name="gdn_prep", grid=(GDN_K_HEADS, n // B),
        in_specs=[pl.BlockSpec((B * C, D), lambda h, i: (i, h)), pl.BlockSpec((B * C, LANE), lambda h, i: (i, 0)),
                  pl.BlockSpec((B * C, LANE), lambda h, i: (i, 0)), pl.BlockSpec((2, B, 1, C), lambda h, i: (h, i, 0, 0))],
        out_specs=pl.BlockSpec((2, B, C, C), lambda h, i: (h, i, 0, 0)),
        out_shape=jax.ShapeDtypeStruct((GDN_V_HEADS, n, C, C), F32),
        compiler_params=_params(("parallel", "parallel")),
    )(k, beta, gcol, grow)


def _gdn_fwd(q, k, v, beta, gcol, grow, amat):
    T = q.shape[0]
    C, D, E = GDN_CHUNK, GDN_HEAD_DIM, GDN_GROUP
    n, qk, vE, colv, colo, rowv, st, am = _gdn_specs(T)
    R = range(E)

    def body(q_ref, k_ref, v_ref, b_ref, gc_ref, gr_ref, a_ref, o_ref, s_ref, vn_ref, state):
        @pl.when(pl.program_id(1) == 0)
        def _():
            state[...] = jnp.zeros_like(state)

        head0 = E * pl.program_id(0)
        s = [state[e] for e in R]
        for cc in range(GDN_STEP_CHUNKS):
            rows = slice(cc * C, (cc + 1) * C)
            qv = [q_ref[rows, (e // 2) * D:(e // 2 + 1) * D] for e in R]
            kv = [k_ref[rows, (e // 2) * D:(e // 2 + 1) * D] for e in R]
            vv = [v_ref[rows, e * D:(e + 1) * D] for e in R]
            beta = [_lane_col(b_ref[rows, :], head0 + e) for e in R]
            a = [a_ref[e, cc] for e in R]
            dec = [_gdn_decay(_lane_col(gc_ref[rows, :], head0 + e), gr_ref[e, cc]) for e in R]
            pm = [_bdot(qv[e], kv[e], NT) * dec[e][0] for e in R]
            r = [beta[e] * (vv[e] - _bdot(kv[e] * dec[e][1], s[e])) for e in R]
            vn = [_dot3(a[e], r[e]) for e in R]
            o = [_bdot(qv[e] * dec[e][1], s[e]) + _bdot(pm[e], vn[e]) for e in R]
            s2 = [dec[e][2] * s[e] + _bdot(kv[e] * dec[e][3], vn[e], TN) for e in R]
            for e in R:
                s_ref[e, cc] = s[e]
                vn_ref[rows, e * D:(e + 1) * D] = vn[e]
                o_ref[rows, e * D:(e + 1) * D] = o[e]
            s = s2
        for e in R:
            state[e] = s[e]

    shv = jax.ShapeDtypeStruct((T, GDN_V_HEADS * D), F32)
    return _pcall(
        body, name="gdn_fwd", grid=(GDN_V_HEADS // E, n), in_specs=[qk, qk, vE, colv, colv, rowv, am],
        out_specs=[vE, st, vE],
        out_shape=[shv, jax.ShapeDtypeStruct((GDN_V_HEADS, T // C, D, D), F32), shv],
        scratch_shapes=[pltpu.VMEM((E, D, D), F32)],
        compiler_params=_params(("parallel", "arbitrary")),
    )(q, k, v, beta, gcol, grow, amat)


def _gdn_bwd(q, k, v, beta, gcol, grow, states, amat, vnew, do):
    T = q.shape[0]
    C, D, E = GDN_CHUNK, GDN_HEAD_DIM, GDN_GROUP
    n, qk, vE, colv, colo, rowv, st, am = _gdn_specs(T)
    rev = lambda spec: pl.BlockSpec(spec.block_shape, (lambda f: (lambda h, i: f(h, n - 1 - i)))(spec.index_map))
    qk, vE, colv, colo, rowv, st, am = (rev(s) for s in (qk, vE, colv, colo, rowv, st, am))
    R = range(E)

    def body(q_ref, k_ref, v_ref, b_ref, gc_ref, gr_ref, s_ref, a_ref, vn_ref, do_ref,
             dq_ref, dk_ref, dv_ref, db_ref, dgc_ref, dstate):
        @pl.when(pl.program_id(1) == 0)
        def _():
            dstate[...] = jnp.zeros_like(dstate)

        dcur = [dstate[e] for e in R]
        for cc in reversed(range(GDN_STEP_CHUNKS)):
            rows = slice(cc * C, (cc + 1) * C)
            M = lambda f: [f(e) for e in R]
            rsum = lambda x: jnp.sum(x, axis=1, keepdims=True)
            qv = M(lambda e: q_ref[rows, (e // 2) * D:(e // 2 + 1) * D])
            kv = M(lambda e: k_ref[rows, (e // 2) * D:(e // 2 + 1) * D])
            vv = M(lambda e: v_ref[rows, e * D:(e + 1) * D])
            vn = M(lambda e: vn_ref[rows, e * D:(e + 1) * D])
            dov = M(lambda e: do_ref[rows, e * D:(e + 1) * D])
            head0 = E * pl.program_id(0)
            beta = M(lambda e: _lane_col(b_ref[rows, :], head0 + e))
            s = M(lambda e: s_ref[e, cc])
            a = M(lambda e: a_ref[e, cc])
            dsn = dcur
            dec = M(lambda e: _gdn_decay(_lane_col(gc_ref[rows, :], head0 + e), gr_ref[e, cc]))
            dm, gam, glast, tail = (M(lambda e: dec[e][i]) for i in range(4))
            strict, incl = dec[0][4], dec[0][5]
            kb = M(lambda e: kv[e] * beta[e])
            kd = M(lambda e: kv[e] * gam[e])
            qd = M(lambda e: qv[e] * gam[e])
            kt = M(lambda e: kv[e] * tail[e])
            lmat = M(lambda e: jnp.where(strict, _bdot(kb[e], kv[e], NT) * dm[e], 0.0))
            pmat = M(lambda e: _bdot(qv[e], kv[e], NT) * dm[e])
            xres = M(lambda e: vv[e] - _bdot(kd[e], s[e]))
            dvn = M(lambda e: _bdot(pmat[e], dov[e], TN) + _bdot(kt[e], dsn[e]))
            dqd = M(lambda e: _bdot(dov[e], s[e], NT))
            dp = M(lambda e: jnp.where(incl, _bdot(dov[e], vn[e], NT), 0.0))
            dkt = M(lambda e: _bdot(vn[e], dsn[e], NT))
            dr = M(lambda e: _dot3(a[e], dvn[e], TN))
            drb = M(lambda e: beta[e] * dr[e])
            dkd = M(lambda e: -_bdot(drb[e], s[e], NT))
            ds2 = M(lambda e: _bdot(qd[e], dov[e], TN) + glast[e] * dsn[e] - _bdot(kd[e], drb[e], TN))
            dl = M(lambda e: -jnp.where(strict, _bdot(dr[e], vn[e], NT), 0.0))
            dmm = M(lambda e: dl[e] * dm[e])
            dnn = M(lambda e: dp[e] * dm[e])
            emat = M(lambda e: dl[e] * lmat[e] + dp[e] * pmat[e])
            dkb = M(lambda e: _bdot(dmm[e], kv[e]))
            dk = M(lambda e: beta[e] * dkb[e] + _bdot(dmm[e], kb[e], TN) + _bdot(dnn[e], qv[e], TN)
                   + gam[e] * dkd[e] + tail[e] * dkt[e])
            dq = M(lambda e: _bdot(dnn[e], kv[e]) + gam[e] * dqd[e])
            dbeta = M(lambda e: rsum(dr[e] * xres[e]) + rsum(dkb[e] * kv[e]))
            ones = jnp.ones((C, LANE), BF16)
            colsum = M(lambda e: _dot2m(emat[e], ones, TN)[:, :1])
            tails = M(lambda e: rsum(dkt[e] * kt[e]))
            lastrow = _iota2((C, 1), 0) == C - 1
            dlast = M(lambda e: jnp.sum(tails[e], axis=0, keepdims=True)
                      + glast[e] * jnp.sum(rsum(s[e] * dsn[e]), axis=0, keepdims=True))
            dgc = M(lambda e: rsum(emat[e]) - colsum[e] + rsum(dkd[e] * kd[e]) + rsum(dqd[e] * qd[e]) - tails[e]
                    + jnp.where(lastrow, dlast[e], 0.0))
            lane = _iota2((C, LANE), 1)
            db_all = jnp.zeros((C, LANE), F32)
            dgc_all = jnp.zeros((C, LANE), F32)
            for e in R:
                dv_ref[rows, e * D:(e + 1) * D] = drb[e]
                db_all = jnp.where(lane == e, dbeta[e], db_all)
                dgc_all = jnp.where(lane == e, dgc[e], dgc_all)
            db_ref[rows, :] = db_all
            dgc_ref[rows, :] = dgc_all
            for kh in range(E // 2):
                dq_ref[rows, kh * D:(kh + 1) * D] = dq[2 * kh] + dq[2 * kh + 1]
                dk_ref[rows, kh * D:(kh + 1) * D] = dk[2 * kh] + dk[2 * kh + 1]

            dcur = ds2
        for e in R:
            dstate[e] = dcur[e]

    shq = jax.ShapeDtypeStruct((T, GDN_K_HEADS * D), F32)
    shv = jax.ShapeDtypeStruct((T, GDN_V_HEADS * D), F32)
    shc = jax.ShapeDtypeStruct((GDN_V_HEADS // E, T, LANE), F32)
    return _pcall(
        body, name="gdn_bwd", grid=(GDN_V_HEADS // E, n),
        in_specs=[qk, qk, vE, colv, colv, rowv, st, am, vE, vE],
        out_specs=[qk, qk, vE, colo, colo], out_shape=[shq, shq, shv, shc, shc],
        scratch_shapes=[pltpu.VMEM((E, D, D), F32)],
        compiler_params=_params(("parallel", "arbitrary")),
    )(q, k, v, beta, gcol, grow, states, amat, vnew, do)


def _outgate_fwd(o, proj, gain):
    T = o.shape[0]
    tm, tc = _pick(T, CONV_ROWS), CONV_COLS
    z0 = GDN_CONV_W // tc

    def body(o_ref, z_ref, g_ref, y_ref):
        z = z_ref[...]
        sz = z * _sigmoid(z)
        parts = []
        for hh in range(tc // GDN_HEAD_DIM):
            oh = o_ref[:, hh * GDN_HEAD_DIM:(hh + 1) * GDN_HEAD_DIM]
            r = lax.rsqrt(jnp.mean(oh * oh, axis=-1, keepdims=True) + EPS)
            parts.append(oh * r * g_ref[...])
        y_ref[...] = (jnp.concatenate(parts, axis=1) * sz).astype(BF16)

    blk = pl.BlockSpec((tm, tc), lambda i, j: (i, j))
    return _pcall(body, name="gdn_outgate", grid=(T // tm, GDN_VW // tc),
                  in_specs=[blk, pl.BlockSpec((tm, tc), lambda i, j: (i, j + z0)), pl.BlockSpec((1, GDN_HEAD_DIM), lambda i, j: (0, 0))],
                  out_specs=blk, out_shape=jax.ShapeDtypeStruct((T, GDN_VW), BF16),
                  compiler_params=_params(("parallel", "parallel")))(o, proj, gain)


def _outgate_bwd(dy, o, proj, gain):
    T = o.shape[0]
    tm, tc = _pick(T, CONV_ROWS), CONV_COLS
    z0 = GDN_CONV_W // tc
    nh = tc // GDN_HEAD_DIM

    def body(dy_ref, o_ref, z_ref, g_ref, do_ref, dz_ref, dg_ref):
        z = z_ref[...]
        sg = _sigmoid(z)
        sz = z * sg
        dy = dy_ref[...]
        dgain = jnp.zeros((1, GDN_HEAD_DIM), F32)
        dos, ys = [], []
        for hh in range(nh):
            sl = slice(hh * GDN_HEAD_DIM, (hh + 1) * GDN_HEAD_DIM)
            oh = o_ref[:, sl]
            r = lax.rsqrt(jnp.mean(oh * oh, axis=-1, keepdims=True) + EPS)
            xh = oh * r
            dn = dy[:, sl] * sz[:, sl]
            dgain = dgain + jnp.sum(dn * xh, axis=0, keepdims=True)
            dxh = dn * g_ref[...]
            dos.append(r * (dxh - xh * jnp.mean(dxh * xh, axis=-1, keepdims=True)))
            ys.append(xh * g_ref[...])
        do_ref[...] = jnp.concatenate(dos, axis=1)
        dz_ref[...] = (dy * jnp.concatenate(ys, axis=1) * sg * (1.0 + z * (1.0 - sg))).astype(BF16)
        first = (pl.program_id(0) == 0) & (pl.program_id(1) == 0)

        @pl.when(first)
        def _():
            dg_ref[...] = dgain

        @pl.when(jnp.logical_not(first))
        def _():
            dg_ref[...] += dgain

    blk = pl.BlockSpec((tm, tc), lambda i, j: (i, j))
    vec = pl.BlockSpec((1, GDN_HEAD_DIM), lambda i, j: (0, 0))
    return _pcall(body, name="gdn_doutgate", grid=(T // tm, GDN_VW // tc),
                  in_specs=[blk, blk, pl.BlockSpec((tm, tc), lambda i, j: (i, j + z0)), vec],
                  out_specs=[blk, blk, vec],
                  out_shape=[jax.ShapeDtypeStruct((T, GDN_VW), F32), jax.ShapeDtypeStruct((T, GDN_VW), BF16),
                             jax.ShapeDtypeStruct((1, GDN_HEAD_DIM), F32)],
                  compiler_params=_params(("arbitrary", "arbitrary")))(dy, o, proj, gain)


def _pad_lanes(vec):
    return jnp.pad(vec.reshape(1, -1), ((0, 0), (0, LANE - vec.shape[-1])))


def _head_rows(a):
    T = a.shape[0]
    return a[:, :GDN_V_HEADS].T.reshape(GDN_V_HEADS, T // GDN_CHUNK, 1, GDN_CHUNK)


def _gdn_pad_in(w_in):
    c = GDN_CONV_W + GDN_VW
    z = jnp.zeros(w_in.shape[:-1] + (LANE - GDN_V_HEADS,), w_in.dtype)
    return jnp.concatenate([w_in[..., :c + GDN_V_HEADS], z, w_in[..., c + GDN_V_HEADS:], z], axis=-1)


def _gdn_unpad_in(dw):
    c = GDN_CONV_W + GDN_VW
    return jnp.concatenate([dw[..., :c + GDN_V_HEADS], dw[..., c + LANE:c + LANE + GDN_V_HEADS]], axis=-1)


def _gdn_mixer_fwd(h, g, w_in_pad, conv_w, a_log, dt_bias, out_gain, w_out):
    T = h.shape[0]
    hn = _rms_fwd(h, g, "gdn_norm")
    proj = _mm(hn, w_in_pad, "nn", name="gdn_in")
    qk = _conv_fwd(proj, conv_w, 0, 2 * GDN_KW, True, "gdn_conv_qk")
    vv = _conv_fwd(proj, conv_w, 2 * GDN_KW, GDN_VW, False, "gdn_conv_v")
    alog, dtb = _pad_lanes(a_log), _pad_lanes(dt_bias)
    beta, gl, gc = _gates_fwd(proj, alog, dtb)
    grow = _head_rows(gc)
    qn, kn = qk[:, :GDN_KW], qk[:, GDN_KW:]
    amat = _gdn_prep(kn, beta, gc, grow)
    o, states, vnew = _gdn_fwd(qn, kn, vv, beta, gc, grow, amat)
    gain = out_gain.reshape(1, GDN_HEAD_DIM)
    y = _outgate_fwd(o, proj, gain)
    h2 = _mm(y, w_out, "nn", res=h, name="gdn_out")
    return h2, (h, hn, proj, qn, kn, vv, beta, gl, gc, grow, o, states, amat, vnew, y, alog, dtb, gain)


def _gdn_mixer_bwd(dh2, saved, g, w_in_pad, conv_w, w_out):
    h, hn, proj, qn, kn, vv, beta, gl, gc, grow, o, states, amat, vnew, y, alog, dtb, gain = saved
    T = h.shape[0]
    dy = _mm(dh2, w_out, "nt", name="gdn_dy")
    dw_out = _mm(y, dh2, "tn", out_dtype=BF16, name="gdn_dwout")
    do, dz, dgain = _outgate_bwd(dy, o, proj, gain)
    dq, dk, dv, dbeta, dgc = _gdn_bwd(qn, kn, vv, beta, gc, grow, states, amat, vnew, do)
    dqk = jnp.concatenate([dq, dk], axis=1)
    dy_qk, dcw_qk = _conv_bwd_pre(proj, conv_w, dqk, 0, 2 * GDN_KW, True, "gdn_dconv_qk")
    dy_v, dcw_v = _conv_bwd_pre(proj, conv_w, dv, 2 * GDN_KW, GDN_VW, False, "gdn_dconv_v")
    dx_qk = _conv_bwd_in(dy_qk, conv_w[:, :2 * GDN_KW], "gdn_dconvin_qk")
    dx_v = _conv_bwd_in(dy_v, conv_w[:, 2 * GDN_KW:], "gdn_dconvin_v")
    dbl, da, dalog, ddt = _gates_bwd(proj, alog, dtb, beta, gl, dbeta, dgc)
    dproj = jnp.concatenate([dx_qk, dx_v, dz, dbl, da], axis=1)
    dw_in_pad = _mm(hn, dproj, "tn", out_dtype=BF16, name="gdn_dwin")
    dhn = _mm(dproj, w_in_pad, "nt", name="gdn_dhn")
    dh, dg = _rms_bwd(dhn, h, g, dh2, "gdn_dnorm")
    dconv = jnp.concatenate([dcw_qk, dcw_v], axis=1)
    return (dh, dg, _gdn_unpad_in(dw_in_pad), dconv, dalog[0, :GDN_V_HEADS], ddt[0, :GDN_V_HEADS],
            dgain.reshape(GDN_HEAD_DIM), dw_out)


def _instances(full):
    out = {}
    for n, a in full.items():
        if n.startswith("ffn_"):
            for i in range(2):
                for j in range(2):
                    out[(n, i, j)] = a[i, j]
        elif n in ("mix_norm", "ple_norm", "ple_w_gate", "ple_w_proj"):
            for i in range(2):
                out[(n, i)] = a[i]
        else:
            out[(n,)] = a[0]
    return out


def _stacked(inst):
    out = {}
    for n in dict.fromkeys(k[0] for k in inst):
        if n.startswith("ffn_"):
            out[n] = jnp.stack([jnp.stack([inst[(n, i, j)] for j in range(2)]) for i in range(2)])
        elif n in ("mix_norm", "ple_norm", "ple_w_gate", "ple_w_proj"):
            out[n] = jnp.stack([inst[(n, i)] for i in range(2)])
        else:
            out[n] = inst[(n,)][None]
    return out


def _local_step(x, p, target, w, late_shards=(), late_weights=None, early_grads=None, first_shards=(), first_weights=None):
    w = dict(w)
    ffn = lambda i, j: (w[("ffn_norm", i, j)], w[("ffn_w_gate", i, j)], w[("ffn_w_up", i, j)], w[("ffn_w_down", i, j)])
    h = x
    tape = []
    for i in range(2):
        if i == 0 and first_weights is not None:
            def wd_of(gathered):
                w.update(first_weights(gathered))
                return w[("ffn_w_down", 0, 0)]
            h, s1 = _ffn_fwd(h, w[("ffn_norm", 0, 0)], w[("ffn_w_gate", 0, 0)], w[("ffn_w_up", 0, 0)], None, "ffn0a",
                             first_shards, wd_of)
        else:
            h, s1 = _ffn_fwd(h, *ffn(i, 0), f"ffn{i}a")
        if i == 0:
            def w_out_of(gathered):
                if late_weights is not None:
                    w.update(late_weights(gathered))
                return w[("att_w_out",)]
            h, s2, _ = _att_fwd(h, w[("mix_norm", 0)], w[("att_w_in",)], w_out_of, w[("att_q_norm",)],
                                w[("att_k_norm",)], w[("att_sinks",)], late_shards)
        else:
            gdn_in_pad = _gdn_pad_in(w[("gdn_w_in",)])
            h, s2 = _gdn_mixer_fwd(h, w[("mix_norm", 1)], gdn_in_pad, w[("gdn_conv_w",)], w[("gdn_a_log",)],
                                   w[("gdn_dt_bias",)], w[("gdn_out_norm",)], w[("gdn_w_out",)])
        h, s3 = _ffn_fwd(h, *ffn(i, 1), f"ffn{i}b")
        h, s4 = _ple_fwd(h, p[i], w[("ple_norm", i)], w[("ple_w_gate", i)], w[("ple_w_proj", i)], f"ple{i}")
        tape.append((s1, s2, s3, s4))

    loss, dh = _loss_head(h, target)

    g = {}
    rode = []
    for i in (1, 0):
        s1, s2, s3, s4 = tape[i]
        dh, g[("ple_norm", i)], g[("ple_w_gate", i)], g[("ple_w_proj", i)] = _ple_bwd(
            dh, s4, p[i], w[("ple_norm", i)], w[("ple_w_gate", i)], f"ple{i}")
        dh, g[("ffn_norm", i, 1)], g[("ffn_w_gate", i, 1)], g[("ffn_w_up", i, 1)], g[("ffn_w_down", i, 1)] = _ffn_bwd(
            dh, s3, *ffn(i, 1), f"ffn{i}b")
        if i == 0:
            ride = early_grads(g) if early_grads is not None else ()
            (dh, g[("mix_norm", 0)], g[("att_w_in",)], g[("att_w_out",)], g[("att_q_norm",)], g[("att_k_norm",)],
             g[("att_sinks",)], rode) = _att_bwd(dh, s2, w[("mix_norm", 0)], w[("att_w_in",)], w[("att_w_out",)], ride)
        else:
            (dh, g[("mix_norm", 1)], g[("gdn_w_in",)], g[("gdn_conv_w",)], g[("gdn_a_log",)], g[("gdn_dt_bias",)],
             g[("gdn_out_norm",)], g[("gdn_w_out",)]) = _gdn_mixer_bwd(
                dh, s2, w[("mix_norm", 1)], gdn_in_pad, w[("gdn_conv_w",)], w[("gdn_w_out",)])
        dh, g[("ffn_norm", i, 0)], g[("ffn_w_gate", i, 0)], g[("ffn_w_up", i, 0)], g[("ffn_w_down", i, 0)] = _ffn_bwd(
            dh, s1, *ffn(i, 0), f"ffn{i}a")
    return loss, dh, g, rode


MESH = pl.DeviceIdType.MESH


def _place():
    x, y, c = lax.axis_index("x"), lax.axis_index("y"), lax.axis_index("c")
    others = [((1 - x, y), 2 * (1 - x) + y), ((x, 1 - y), 2 * x + (1 - y)), ((1 - x, 1 - y), 2 * (1 - x) + (1 - y))]
    return x, y, c, 4 * x + 2 * y + c, 2 * x + y, (x, y, 1 - c), others


def _comm_call(body, arrays, out_shape, n_sems, name):
    hbm = pl.BlockSpec(memory_space=pl.ANY)
    n = len(arrays)
    return _pcall(
        body, name=name, in_specs=[hbm] * n, out_specs=[hbm] * len(out_shape), out_shape=out_shape,
        scratch_shapes=[pltpu.SemaphoreType.DMA((n, n_sems)), pltpu.SemaphoreType.DMA((n, n_sems)),
                        pltpu.SemaphoreType.DMA((n, N_CHIP))],
        compiler_params=pltpu.CompilerParams(has_side_effects=True),
    )(*arrays)


def _gather_protocol(ins, outs, send_sems, recv_sems, local_sems):
    n = len(ins)
    x, y, c, me, my_chip, sibling, others = _place()

    def copy(a, k, block, to, src=None):
        dst = outs[a].at[block]
        return pltpu.make_async_remote_copy(
            src_ref=dst if src is None else src, dst_ref=dst, send_sem=send_sems.at[a, k],
            recv_sem=recv_sems.at[a, k], device_id=to, device_id_type=MESH)

    local = [pltpu.make_async_copy(ins[a], outs[a].at[me], local_sems.at[a, 0]) for a in range(n)]
    first = []
    for a in range(n):
        first.append(copy(a, 0, me, sibling, src=ins[a]))
        first += [copy(a, 1 + j, me, (*chip, c), src=ins[a]) for j, (chip, _) in enumerate(others)]

    def start():
        for cp in local + first:
            cp.start()

    def finish():
        passed = []
        for a in range(n):
            for j, (chip, chip_idx) in enumerate(others):
                blk = 2 * chip_idx + c
                copy(a, 1 + j, blk, (x, y, c)).wait_recv()
                fwd = copy(a, 4 + j, blk, sibling)
                fwd.start()
                passed.append(fwd)
        for a in range(n):
            copy(a, 0, 2 * my_chip + (1 - c), (x, y, c)).wait_recv()
            for j, (chip, chip_idx) in enumerate(others):
                copy(a, 4 + j, 2 * chip_idx + (1 - c), (x, y, c)).wait_recv()
        for cp in first + passed:
            cp.wait_send()
        for cp in local:
            cp.wait()

    return start, finish


def _all_gather(arrays):
    n = len(arrays)

    def body(*refs):
        start, finish = _gather_protocol(refs[:n], refs[n:2 * n], *refs[2 * n:])
        start()
        finish()

    out_shape = [jax.ShapeDtypeStruct((N_DEV,) + a.shape, a.dtype) for a in arrays]
    return _comm_call(body, arrays, out_shape, N_DEV - 1, "gather_weights")


def _exchange_sibling(arrays, name):
    n = len(arrays)

    def body(*refs):
        ins, got = refs[:n], refs[n:2 * n]
        send_sems, recv_sems, _ = refs[2 * n:]
        x, y, c, me, my_chip, sibling, others = _place()
        remote = []
        for a in range(n):
            for chip in range(N_CHIP):
                rc = pltpu.make_async_remote_copy(
                    src_ref=ins[a].at[2 * chip + (1 - c)], dst_ref=got[a].at[chip], send_sem=send_sems.at[a, chip],
                    recv_sem=recv_sems.at[a, chip], device_id=sibling, device_id_type=MESH)
                rc.start()
                remote.append(rc)
        for rc in remote:
            rc.wait()

    half = [jax.ShapeDtypeStruct((N_CHIP,) + a.shape[1:], a.dtype) for a in arrays]
    return _comm_call(body, arrays, half, N_CHIP, name)


def _chips_protocol(ins, outs, send_sems, recv_sems, local_sems):
    n = len(ins)
    x, y, c, me, my_chip, sibling, others = _place()
    local = [pltpu.make_async_copy(ins[a].at[my_chip], outs[a].at[my_chip], local_sems.at[a, 0]) for a in range(n)]
    remote = [pltpu.make_async_remote_copy(
        src_ref=ins[a].at[chip_idx], dst_ref=outs[a].at[my_chip], send_sem=send_sems.at[a, j],
        recv_sem=recv_sems.at[a, j], device_id=(*chip, c), device_id_type=MESH)
        for a in range(n) for j, (chip, chip_idx) in enumerate(others)]

    def start():
        for cp in local + remote:
            cp.start()

    def finish():
        for cp in remote + local:
            cp.wait()

    return start, finish


def _exchange_chips(arrays, name):
    n = len(arrays)

    def body(*refs):
        start, finish = _chips_protocol(refs[:n], refs[n:2 * n], *refs[2 * n:])
        start()
        finish()

    out_shape = [jax.ShapeDtypeStruct(a.shape, a.dtype) for a in arrays]
    return _comm_call(body, arrays, out_shape, N_CHIP - 1, name)


def _as_rows(a, lead):
    shp = a.shape
    return a.reshape(shp[:lead] + (math.prod(shp[lead:-1]), shp[-1]))


def _row_tile(rows, cap=512):
    if rows <= cap:
        return rows
    for t in range(cap - cap % 8, 0, -8):
        if rows % t == 0:
            return t
    return rows


def _pair_sum(send, got, name):
    a3, b3 = _as_rows(send, 1), _as_rows(got, 1)
    _, rows, last = b3.shape
    tr = _row_tile(rows, 2048)

    def body(c_ref, a_ref, b_ref, o_ref):
        o_ref[...] = (a_ref[...].astype(F32) + b_ref[...].astype(F32)).astype(o_ref.dtype)

    core = lax.axis_index("c").astype(jnp.int32).reshape(1)
    out = _pcall(
        body, name=name,
        grid_spec=pltpu.PrefetchScalarGridSpec(
            num_scalar_prefetch=1, grid=(N_CHIP, rows // tr),
            in_specs=[pl.BlockSpec((None, tr, last), lambda k, i, c_ref: (2 * k + c_ref[0], i, 0)),
                      pl.BlockSpec((None, tr, last), lambda k, i, c_ref: (k, i, 0))],
            out_specs=pl.BlockSpec((None, tr, last), lambda k, i, c_ref: (k, i, 0))),
        out_shape=jax.ShapeDtypeStruct(b3.shape, got.dtype), compiler_params=_params(("parallel", "parallel")),
    )(core, a3, b3)
    return out.reshape(got.shape)


def _adamw(parts, w, m, v, name):
    lead, (rows, last) = w.shape[:-2], w.shape[-2:]
    nl = len(lead)
    tr = _row_tile(rows, 1024)
    c1 = 1.0 / (1.0 - ADAM_B1 ** ADAM_STEP)
    c2 = 1.0 / (1.0 - ADAM_B2 ** ADAM_STEP)

    def body(p_ref, w_ref, m_ref, v_ref, g_ref, d_ref, nm_ref, nv_ref):
        g = p_ref[0].astype(F32)
        for chip in range(1, N_CHIP):
            g = g + p_ref[chip].astype(F32)
        mn = ADAM_B1 * m_ref[...] + (1.0 - ADAM_B1) * g
        vn = ADAM_B2 * v_ref[...] + (1.0 - ADAM_B2) * (g * g)
        g_ref[...] = g
        nm_ref[...] = mn
        nv_ref[...] = vn
        d_ref[...] = -ADAM_LR * ((mn * c1) / (jnp.sqrt(vn * c2) + ADAM_EPS) + ADAM_WD * w_ref[...])

    row = pl.BlockSpec((None,) * nl + (tr, last), lambda *ix: ix + (0,))
    part = pl.BlockSpec((N_CHIP,) + (None,) * nl + (tr, last), lambda *ix: (0,) + ix + (0,))
    sh = jax.ShapeDtypeStruct(w.shape, F32)
    return _pcall(body, name=name, grid=lead + (rows // tr,), in_specs=[part, row, row, row],
                  out_specs=[row, row, row, row], out_shape=[sh, sh, sh, sh],
                  compiler_params=_params(("parallel",) * (nl + 1)))(parts, w, m, v)


def _pack(pieces, row_align):
    rows, offs, r = [], [], 0
    for a in pieces:
        flat = a.reshape(-1)
        nr = -(-flat.shape[0] // PACK_W)
        flat = jnp.pad(flat, (0, nr * PACK_W - flat.shape[0]))
        rows.append(flat.reshape(nr, PACK_W))
        offs.append(r)
        r += nr
    pad = (-r) % row_align
    if pad:
        rows.append(jnp.zeros((pad, PACK_W), pieces[0].dtype))
    return jnp.concatenate(rows, axis=0), offs


def _unpack(flat, offs, shapes):
    out = []
    for off, shp in zip(offs, shapes):
        size = math.prod(shp)
        nr = -(-size // PACK_W)
        out.append(flat[..., off:off + nr, :].reshape(flat.shape[:-2] + (nr * PACK_W,))[..., :size].reshape(flat.shape[:-2] + tuple(shp)))
    return out


def _to_full(gathered, axis):
    z = jnp.moveaxis(gathered, 0, axis)
    shp = list(z.shape)
    return z.reshape(shp[:axis] + [shp[axis] * shp[axis + 1]] + shp[axis + 2:])


def _to_shards(full, axis):
    shp = list(full.shape)
    z = full.reshape(shp[:axis] + [N_DEV, shp[axis] // N_DEV] + shp[axis + 1:])
    return jnp.moveaxis(z, axis, 0)


def kernel(x, p, ffn_norm, ffn_w_gate, ffn_w_up, ffn_w_down, mix_norm, att_w_in, att_q_norm, att_k_norm, att_sinks, att_w_out, gdn_w_in, gdn_conv_w, gdn_a_log, gdn_dt_bias, gdn_out_norm, gdn_w_out, ple_norm, ple_w_gate, ple_w_proj, loss_target, m_ffn_norm, m_ffn_w_gate, m_ffn_w_up, m_ffn_w_down, m_mix_norm, m_att_w_in, m_att_q_norm, m_att_k_norm, m_att_sinks, m_att_w_out, m_gdn_w_in, m_gdn_conv_w, m_gdn_a_log, m_gdn_dt_bias, m_gdn_out_norm, m_gdn_w_out, m_ple_norm, m_ple_w_gate, m_ple_w_proj, v_ffn_norm, v_ffn_w_gate, v_ffn_w_up, v_ffn_w_down, v_mix_norm, v_att_w_in, v_att_q_norm, v_att_k_norm, v_att_sinks, v_att_w_out, v_gdn_w_in, v_gdn_conv_w, v_gdn_a_log, v_gdn_dt_bias, v_gdn_out_norm, v_gdn_w_out, v_ple_norm, v_ple_w_gate, v_ple_w_proj):
    args = dict(locals())
    wts = {n: args[n] for n in WEIGHTS}
    mom = {n: args["m_" + n] for n in WEIGHTS}
    var = {n: args["v_" + n] for n in WEIGHTS}
    axis = dict(SHARDED)
    vecs = [n for n, _ in SHARDED[:SMALL_SHARDED]]
    small = vecs + list(REPLICATED)
    small_shapes = [wts[n].shape for n in small]
    lead = lambda n: 2 if n.startswith("ffn_") else 1

    def stack_of(arrays, name, idxs):
        return jnp.stack([arrays[name][idx] if idx else arrays[name][0] for idx in idxs])

    def full_instances(gathered, group):
        out = {}
        for (name, idxs), g in zip(group, gathered):
            whole = _to_full(g, axis[name] - lead(name) + 1)
            for k, idx in enumerate(idxs):
                out[(name,) + idx] = whole[k]
        return out

    def shard_stacks(g, group):
        return [_to_shards(jnp.stack([g[(name,) + idx] for idx in idxs]), axis[name] - lead(name) + 1)
                for name, idxs in group]

    vec_pack, voffs = _pack([wts[n] for n in vecs], 8)
    early = _all_gather([stack_of(wts, n, idxs).astype(BF16) for n, idxs in EARLY] + [vec_pack])
    w = full_instances(early[:-1], EARLY)
    vec_full = {n: _to_full(piece, axis[n]) for n, piece in
                zip(vecs, _unpack(early[-1], voffs, [wts[n].shape for n in vecs]))}
    w.update(_instances({**vec_full, **{n: wts[n] for n in REPLICATED}}))
    first_shards = [stack_of(wts, n, idxs).astype(BF16) for n, idxs in FIRST]
    late_shards = [stack_of(wts, n, idxs).astype(BF16) for n, idxs in LATE]

    def early_grads(g):
        send = shard_stacks(g, RIDE)
        got = _exchange_sibling(send, "exchange_sibling_early")
        return [_pair_sum(p_, q_, f"pair_sum_early_{i}") for i, (p_, q_) in enumerate(zip(send, got))]

    loss, grad_x, g, rode = _local_step(x[0], p[:, 0], loss_target[0], w, late_shards,
                                        lambda gathered: full_instances(gathered, LATE), early_grads,
                                        first_shards, lambda gathered: full_instances(gathered, FIRST))

    gs = _stacked({k: v for k, v in g.items() if k[0] in small})
    vec_shards = [_to_shards(gs[n], axis[n]) for n in vecs]
    small_send = jnp.stack([_pack([sh[d] for sh in vec_shards] + [gs[n] for n in REPLICATED] + [loss.reshape(1)], 8)[0]
                            for d in range(N_DEV)])
    send = shard_stacks(g, FINAL) + [small_send]
    got = _exchange_sibling(send, "exchange_sibling_final")
    chip_sums = [_pair_sum(p_, q_, f"pair_sum_final_{i}") for i, (p_, q_) in enumerate(zip(send, got))]
    last = _exchange_chips(chip_sums, "exchange_chips_final")

    pieces = {}
    for (name, idxs), part in list(zip(RIDE, rode)) + list(zip(FINAL, last[:-1])):
        for k, idx in enumerate(idxs):
            pieces[(name,) + idx] = part[:, k]
    outs = {}
    for n, _ in SHARDED[SMALL_SHARDED:]:
        if lead(n) == 2:
            part = jnp.stack([jnp.stack([pieces[(n, i, j)] for j in range(2)], axis=1) for i in range(2)], axis=1)
        elif (n, 0) in pieces:
            part = jnp.stack([pieces[(n, i)] for i in range(2)], axis=1)
        else:
            part = pieces[(n,)][:, None]
        outs[n] = _adamw(part, wts[n], mom[n], var[n], f"adamw_{n}")
    filler = [jnp.zeros((1,), F32)]
    small_w, soffs = _pack([wts[n] for n in small] + filler, 8)
    small_m, _ = _pack([mom[n] for n in small] + filler, 8)
    small_v, _ = _pack([var[n] for n in small] + filler, 8)
    small_out = [_unpack(z, soffs, small_shapes + [(1,)]) for z in _adamw(last[-1], small_w, small_m, small_v, "adamw_small")]
    loss = small_out[0][-1][0]
    for i, n in enumerate(small):
        outs[n] = [small_out[k][i] for k in range(4)]
    result = [loss, grad_x[None]]
    for k in range(4):
        result += [outs[n][k] for n in WEIGHTS]
    return tuple(result)
```

```python
import math

import jax
import jax.numpy as jnp
from jax import lax
from jax.experimental import pallas as pl
from jax.experimental.pallas import tpu as pltpu

F32 = jnp.float32
BF16 = jnp.bfloat16

N_DEV = 8
N_CHIP = 4
D_MODEL = 1024
D_FF = 2816
PLE_DIM = 256
HEAD_DIM = 64
SB_HEADS = 8
SWA_HEADS = 8
SWA_KV_HEADS = 2
SWA_GROUP = SWA_HEADS // SWA_KV_HEADS
WINDOW = 128
Q_BLOCK = 128
GDN_K_HEADS = 8
GDN_V_HEADS = 16
GDN_HEAD_DIM = 128
GDN_CONV = 4
GDN_CHUNK = 64
EPS = 1e-6
SB_W = SB_HEADS * HEAD_DIM
SWA_QW = SWA_HEADS * HEAD_DIM
SWA_KVW = SWA_KV_HEADS * HEAD_DIM
ATT_IN = 3 * SB_W + SWA_QW + 2 * SWA_KVW
GDN_KW = GDN_K_HEADS * GDN_HEAD_DIM
GDN_VW = GDN_V_HEADS * GDN_HEAD_DIM
GDN_CONV_W = 2 * GDN_KW + GDN_VW
GDN_IN = GDN_CONV_W + GDN_VW + 2 * GDN_V_HEADS
GDN_IN_PAD = GDN_CONV_W + GDN_VW + 2 * 128

ADAM_LR = 0.001
ADAM_B1 = 0.9
ADAM_B2 = 0.999
ADAM_EPS = 1e-08
ADAM_WD = 0.01
ADAM_STEP = 10

LANE = 128
VMEM_LIMIT = 56 * 1024 * 1024
MM_TILE_BUDGET = 40 * 1024 * 1024
PACK_W = 1024

NN = ((1,), (0,))
NT = ((1,), (1,))
TN = ((0,), (0,))

SHARDED = (
    ("ffn_norm", 2), ("gdn_conv_w", 2),
    ("ffn_w_gate", 3), ("ffn_w_up", 3), ("ffn_w_down", 2), ("att_w_in", 2), ("att_w_out", 1),
    ("gdn_w_in", 2), ("gdn_w_out", 1), ("ple_w_gate", 1), ("ple_w_proj", 2),
)
SMALL_SHARDED = 2
REPLICATED = ("mix_norm", "att_q_norm", "att_k_norm", "att_sinks", "gdn_a_log", "gdn_dt_bias",
              "gdn_out_norm", "ple_norm")
WEIGHTS = ("ffn_norm", "ffn_w_gate", "ffn_w_up", "ffn_w_down", "mix_norm", "att_w_in", "att_q_norm",
           "att_k_norm", "att_sinks", "att_w_out", "gdn_w_in", "gdn_conv_w", "gdn_a_log", "gdn_dt_bias",
           "gdn_out_norm", "gdn_w_out", "ple_norm", "ple_w_gate", "ple_w_proj")


_FFN_REST = [(0, 1), (1, 0), (1, 1)]
EARLY = [("ffn_w_gate", [(0, 0)]), ("ffn_w_up", [(0, 0)])]
FIRST = [("ffn_w_down", [(0, 0)]), ("att_w_in", [()])]
LATE = ([(n, [idx]) for n in ("ffn_w_gate", "ffn_w_up", "ffn_w_down") for idx in _FFN_REST]
        + [("att_w_out", [()]), ("gdn_w_in", [()]), ("gdn_w_out", [()]),
           ("ple_w_gate", [(0,), (1,)]), ("ple_w_proj", [(0,), (1,)])])
RIDE = [e for e in LATE if e[0] != "att_w_out"]
FINAL = EARLY + FIRST + [("att_w_out", [()])]


def _pcall(body, **kw):
    return pl.pallas_call(body, **kw)


def _params(sem=None):
    if sem is None:
        return pltpu.CompilerParams(vmem_limit_bytes=VMEM_LIMIT)
    return pltpu.CompilerParams(dimension_semantics=sem, vmem_limit_bytes=VMEM_LIMIT)


def _ride_specs(ride, out_shapes, n_sems):
    hbm = pl.BlockSpec(memory_space=pl.ANY)
    n = len(ride)
    sems = [pltpu.SemaphoreType.DMA((n, n_sems)), pltpu.SemaphoreType.DMA((n, n_sems)),
            pltpu.SemaphoreType.DMA((n, N_CHIP))] if n else []
    return [hbm] * n, [hbm] * len(out_shapes), sems


def _dot(a, b, dims=NN):
    return lax.dot_general(a, b, (dims, ((), ())), preferred_element_type=F32)


def _bdot(a, b, dims=NN):
    return _dot(a.astype(BF16), b.astype(BF16), dims)


def _split(a):
    hi = a.astype(BF16)
    lo = (a - hi.astype(F32)).astype(BF16)
    return hi, lo


def _dot3(a, b, dims=NN):
    ah, al = _split(a)
    bh, bl = _split(b)
    return _dot(ah, bh, dims) + (_dot(ah, bl, dims) + _dot(al, bh, dims))


def _dot2m(a, m, dims=NN):
    ah, al = _split(a)
    return _dot(ah, m, dims) + _dot(al, m, dims)


def _mdot2(m, a, dims=NN):
    ah, al = _split(a)
    return _dot(m, ah, dims) + _dot(m, al, dims)


def _sigmoid(x):
    return 1.0 / (1.0 + jnp.exp(-x))


def _softplus(x):
    return jnp.maximum(x, 0.0) + jnp.log(1.0 + jnp.exp(-jnp.abs(x)))


def _pick(n, cap):
    if n <= cap:
        return n
    for t in range(cap - cap % LANE, 0, -LANE):
        if n % t == 0:
            return t
    raise ValueError(f"no tile for {n} under {cap}")


def _iota2(shape, axis):
    return lax.broadcasted_iota(jnp.int32, shape, axis)


def _mm(a, b, mode, out_dtype=F32, res=None, alpha=1.0, a2=None, b2=None, name="mm", ride=()):
    if mode == "nn":
        (M, K), N = a.shape, b.shape[1]
    elif mode == "nt":
        (M, K), N = a.shape, b.shape[0]
    else:
        (K, M), N = a.shape, b.shape[1]
    tn, tk = _pick(N, 1408), _pick(K, 2048 if mode == "tn" else 1408)
    nk = K // tk
    pairs = 1 if a2 is None else 2

    def tile_bytes(tm):
        per = pairs * tk * (tm * a.dtype.itemsize + tn * b.dtype.itemsize) + tm * tn * jnp.dtype(out_dtype).itemsize
        return 2 * (per + (tm * tn * 4 if res is not None else 0)) + (tm * tn * 4 if nk > 1 else 0)

    tm = next(t for t in (_pick(M, c) for c in ((1408,) if mode == "tn" else (2048, 1024, 512))) if tile_bytes(t) <= MM_TILE_BUDGET or t <= 512)
    dims = {"nn": NN, "nt": NT, "tn": TN}[mode]
    a_spec = pl.BlockSpec((tk, tm), lambda i, j, k: (k, i)) if mode == "tn" else pl.BlockSpec((tm, tk), lambda i, j, k: (i, k))
    b_spec = pl.BlockSpec((tn, tk), lambda i, j, k: (j, k)) if mode == "nt" else pl.BlockSpec((tk, tn), lambda i, j, k: (k, j))
    o_spec = pl.BlockSpec((tm, tn), lambda i, j, k: (i, j))
    two = a2 is not None
    has_res = res is not None
    grid = (M // tm, N // tn, nk)
    nr = len(ride)
    ride_out = [jax.ShapeDtypeStruct(r.shape, r.dtype) for r in ride]
    ride_in_specs, ride_out_specs, ride_sems = _ride_specs(ride, ride_out, N_CHIP - 1)
    a2_spec, b2_spec = a_spec, b_spec
    if two and a2.shape != a.shape:
        assert nk == 1 and mode == "nn" and a2.shape[0] == M and b2.shape[1] == N
        a2_spec = pl.BlockSpec((tm, a2.shape[1]), lambda i, j, k: (i, 0))
        b2_spec = pl.BlockSpec((a2.shape[1], tn), lambda i, j, k: (0, j))

    def body(*refs):
        refs = list(refs)
        a_ref, b_ref = refs[0], refs[1]
        pos = 2
        if two:
            a2_ref, b2_ref = refs[2], refs[3]
            pos = 4
        if has_res:
            res_ref = refs[pos]
            pos += 1
        rin = refs[pos:pos + nr]
        o_ref = refs[pos + nr]
        rout = refs[pos + nr + 1:pos + 2 * nr + 1]
        acc_ref = refs[pos + 2 * nr + 1]
        k = pl.program_id(2)
        if nr:
            i, j = pl.program_id(0), pl.program_id(1)
            start, done = _chips_protocol(rin, rout, *refs[pos + 2 * nr + 2:])
            pl.when((i == 0) & (j == 0) & (k == 0))(start)
        part = _bdot(a_ref[...], b_ref[...], dims)
        if two:
            part = part + _bdot(a2_ref[...], b2_ref[...], dims)

        def finish(acc):
            out = acc * alpha if alpha != 1.0 else acc
            if has_res:
                out = res_ref[...] + out
            o_ref[...] = out.astype(out_dtype)

        if nk == 1:
            finish(part)
        else:
            @pl.when(k == 0)
            def _():
                acc_ref[...] = part

            @pl.when(k > 0)
            def _():
                acc_ref[...] += part

            @pl.when(k == nk - 1)
            def _():
                finish(acc_ref[...])

        if nr:
            pl.when((i == grid[0] - 1) & (j == grid[1] - 1) & (k == nk - 1))(done)

    ins = [a, b]
    specs = [a_spec, b_spec]
    if two:
        ins += [a2, b2]
        specs += [a2_spec, b2_spec]
    if has_res:
        ins.append(res)
        specs.append(o_spec)
    out = _pcall(
        body, name=name, grid=grid, in_specs=specs + ride_in_specs, out_specs=[o_spec] + ride_out_specs,
        out_shape=[jax.ShapeDtypeStruct((M, N), out_dtype)] + ride_out,
        scratch_shapes=[pltpu.VMEM((tm, tn) if nk > 1 else (8, LANE), F32)] + ride_sems,
        compiler_params=_params(("arbitrary",) * 3 if nr else ("parallel", "parallel", "arbitrary")),
    )(*ins, *ride)
    return (out[0], list(out[1:])) if nr else out[0]


ROW_TILE = 1024


def _rms_fwd(h, g, name):
    T, D = h.shape
    tr = _pick(T, ROW_TILE)

    def body(h_ref, g_ref, n_ref):
        x = h_ref[...]
        r = lax.rsqrt(jnp.mean(x * x, axis=-1, keepdims=True) + EPS)
        n_ref[...] = (x * r * g_ref[...]).astype(BF16)

    return _pcall(
        body, name=name, grid=(T // tr,),
        in_specs=[pl.BlockSpec((tr, D), lambda i: (i, 0)), pl.BlockSpec((1, D), lambda i: (0, 0))],
        out_specs=pl.BlockSpec((tr, D), lambda i: (i, 0)),
        out_shape=jax.ShapeDtypeStruct((T, D), BF16), compiler_params=_params(("parallel",)),
    )(h, g.reshape(1, D))


def _rms_bwd(dn, h, g, dres, name):
    T, D = h.shape
    tr = _pick(T, ROW_TILE)

    def body(dn_ref, h_ref, g_ref, dres_ref, dh_ref, dg_ref):
        x = h_ref[...]
        r = lax.rsqrt(jnp.mean(x * x, axis=-1, keepdims=True) + EPS)
        xh = x * r
        d = dn_ref[...].astype(F32)
        dxh = d * g_ref[...]
        dh_ref[...] = dres_ref[...] + r * (dxh - xh * jnp.mean(dxh * xh, axis=-1, keepdims=True))
        part = jnp.sum(d * xh, axis=0, keepdims=True)

        @pl.when(pl.program_id(0) == 0)
        def _():
            dg_ref[...] = part

        @pl.when(pl.program_id(0) > 0)
        def _():
            dg_ref[...] += part

    row = pl.BlockSpec((tr, D), lambda i: (i, 0))
    vec = pl.BlockSpec((1, D), lambda i: (0, 0))
    dh, dg = _pcall(
        body, name=name, grid=(T // tr,), in_specs=[row, row, vec, row], out_specs=[row, vec],
        out_shape=[jax.ShapeDtypeStruct((T, D), F32), jax.ShapeDtypeStruct((1, D), F32)],
        compiler_params=_params(("arbitrary",)),
    )(dn, h, g.reshape(1, D), dres)
    return dh, dg.reshape(D)


def _gateup(n, wg, wu, name, ride=()):
    T, D = n.shape
    F = wg.shape[1]
    tm, tn = _pick(T, 1024), _pick(F, 1408)
    nr = len(ride)
    ride_out = [jax.ShapeDtypeStruct((N_DEV,) + r.shape, r.dtype) for r in ride]
    ride_in_specs, ride_out_specs, ride_sems = _ride_specs(ride, ride_out, N_DEV - 1)
    grid = (T // tm, F // tn)

    def body(*refs):
        n_ref, wg_ref, wu_ref = refs[:3]
        a_ref, b_ref, hid_ref = refs[3 + nr:6 + nr]
        if nr:
            i, j = pl.program_id(0), pl.program_id(1)
            start, finish = _gather_protocol(refs[3:3 + nr], refs[6 + nr:6 + 2 * nr], *refs[6 + 2 * nr:])
            pl.when((i == 0) & (j == 0))(start)
        x = n_ref[...]
        a = _dot(x, wg_ref[...])
        b = _dot(x, wu_ref[...])
        a_ref[...] = a.astype(BF16)
        b_ref[...] = b.astype(BF16)
        hid_ref[...] = (a * _sigmoid(a) * b).astype(BF16)
        if nr:
            pl.when((i == grid[0] - 1) & (j == grid[1] - 1))(finish)

    o_spec = pl.BlockSpec((tm, tn), lambda i, j: (i, j))
    w_spec = pl.BlockSpec((D, tn), lambda i, j: (0, j))
    sh = jax.ShapeDtypeStruct((T, F), BF16)
    res = _pcall(
        body, name=name, grid=grid,
        in_specs=[pl.BlockSpec((tm, D), lambda i, j: (i, 0)), w_spec, w_spec] + ride_in_specs,
        out_specs=[o_spec, o_spec, o_spec] + ride_out_specs, out_shape=[sh, sh, sh] + ride_out,
        scratch_shapes=ride_sems,
        compiler_params=_params(("arbitrary", "arbitrary") if nr else ("parallel", "parallel")),
    )(n, wg, wu, *ride)
    return res[0], res[1], res[2], list(res[3:])


def _ffn_dhid(dy, wd, a, b, name):
    T, D = dy.shape
    F = wd.shape[0]
    tm, tn = _pick(T, 1024), _pick(F, 1408)

    def body(dy_ref, wd_ref, a_ref, b_ref, da_ref, db_ref):
        dhid = 0.5 * _bdot(dy_ref[...], wd_ref[...], NT)
        av = a_ref[...].astype(F32)
        bv = b_ref[...].astype(F32)
        s = _sigmoid(av)
        da_ref[...] = (dhid * bv * s * (1.0 + av * (1.0 - s))).astype(BF16)
        db_ref[...] = (dhid * av * s).astype(BF16)

    o_spec = pl.BlockSpec((tm, tn), lambda i, j: (i, j))
    sh = jax.ShapeDtypeStruct((T, F), BF16)
    return _pcall(
        body, name=name, grid=(T // tm, F // tn),
        in_specs=[pl.BlockSpec((tm, D), lambda i, j: (i, 0)), pl.BlockSpec((tn, D), lambda i, j: (j, 0)), o_spec, o_spec],
        out_specs=[o_spec, o_spec], out_shape=[sh, sh],
        compiler_params=_params(("parallel", "parallel")),
    )(dy, wd, a, b)


def _ffn_fwd(h, g, wg, wu, wd, tag, ride=(), wd_of=None):
    n = _rms_fwd(h, g, f"{tag}_norm")
    a, b, hid, gathered = _gateup(n, wg, wu, f"{tag}_gateup", ride)
    if wd_of is not None:
        wd = wd_of(gathered)
    h2 = _mm(hid, wd, "nn", res=h, alpha=0.5, name=f"{tag}_down")
    return h2, (h, n, a, b, hid)


def _ffn_bwd(dh2, saved, g, wg, wu, wd, tag, ride_of=None):
    h, n, a, b, hid = saved
    da, db = _ffn_dhid(dh2, wd, a, b, f"{tag}_dhid")
    dwd = _mm(hid, dh2, "tn", alpha=0.5, out_dtype=BF16, name=f"{tag}_dwd")
    dwg = _mm(n, da, "tn", out_dtype=BF16, name=f"{tag}_dwg")
    dwu = _mm(n, db, "tn", out_dtype=BF16, name=f"{tag}_dwu")
    rode = []
    if ride_of is None:
        dn = _mm(da, wg, "nt", a2=db, b2=wu, name=f"{tag}_dn")
    else:
        dn, rode = _mm(da, wg, "nt", a2=db, b2=wu, name=f"{tag}_dn", ride=ride_of(dwg, dwu, dwd))
    dh, dg = _rms_bwd(dn, h, g, dh2, f"{tag}_dnorm")
    return dh, dg, dwg, dwu, dwd, rode


def _ple_fwd(h, p, g, w_gate, w_proj, tag):
    T, D = h.shape
    pn = _rms_fwd(h, g, f"{tag}_norm")
    tm, tn = _pick(T, 512), _pick(D, 1024)
    P = p.shape[1]

    def body(pn_ref, p_ref, wg_ref, wp_ref, h_ref, o_ref, gl_ref, pp_ref):
        gl = _dot(pn_ref[...], wg_ref[...])
        pp = _bdot(p_ref[...], wp_ref[...])
        gl_ref[...] = gl
        pp_ref[...] = pp
        o_ref[...] = h_ref[...] + _sigmoid(gl) * pp

    o_spec = pl.BlockSpec((tm, tn), lambda i, j: (i, j))
    sh = jax.ShapeDtypeStruct((T, D), F32)
    h2, gl, pp = _pcall(
        body, name=f"{tag}_fwd", grid=(T // tm, D // tn),
        in_specs=[pl.BlockSpec((tm, D), lambda i, j: (i, 0)), pl.BlockSpec((tm, P), lambda i, j: (i, 0)),
                  pl.BlockSpec((D, tn), lambda i, j: (0, j)), pl.BlockSpec((P, tn), lambda i, j: (0, j)), o_spec],
        out_specs=[o_spec, o_spec, o_spec], out_shape=[sh, sh, sh],
        compiler_params=_params(("parallel", "parallel")),
    )(pn, p, w_gate, w_proj, h)
    return h2, (h, pn, gl, pp)


def _ple_bwd(dh2, saved, p, g, w_gate, tag):
    h, pn, gl, pp = saved
    T, D = h.shape
    tr = _pick(T, ROW_TILE)

    def body(d_ref, gl_ref, pp_ref, dgl_ref, dpp_ref):
        d = d_ref[...]
        s = _sigmoid(gl_ref[...])
        dpp_ref[...] = (d * s).astype(BF16)
        dgl_ref[...] = (d * pp_ref[...] * s * (1.0 - s)).astype(BF16)

    row = pl.BlockSpec((tr, D), lambda i: (i, 0))
    sh = jax.ShapeDtypeStruct((T, D), BF16)
    dgl, dpp = _pcall(body, name=f"{tag}_dgate", grid=(T // tr,), in_specs=[row, row, row], out_specs=[row, row],
                      out_shape=[sh, sh], compiler_params=_params(("parallel",)))(dh2, gl, pp)
    dw_proj = _mm(p, dpp, "tn", out_dtype=BF16, name=f"{tag}_dwproj")
    dw_gate = _mm(pn, dgl, "tn", out_dtype=BF16, name=f"{tag}_dwgate")
    dpn = _mm(dgl, w_gate, "nt", name=f"{tag}_dpn")
    dh, dg = _rms_bwd(dpn, h, g, dh2, f"{tag}_dnorm")
    return dh, dg, dw_gate, dw_proj


def _loss_head(y, target):
    T, D = y.shape
    tr = _pick(T, ROW_TILE)

    def body(y_ref, t_ref, dy_ref, l_ref):
        e = y_ref[...] - t_ref[...]
        dy_ref[...] = e * (1.0 / D)
        part = jnp.sum(e * e, axis=0, keepdims=True)

        @pl.when(pl.program_id(0) == 0)
        def _():
            l_ref[...] = part

        @pl.when(pl.program_id(0) > 0)
        def _():
            l_ref[...] += part

    row = pl.BlockSpec((tr, D), lambda i: (i, 0))
    vec = pl.BlockSpec((1, D), lambda i: (0, 0))
    dy, l = _pcall(body, name="loss_head", grid=(T // tr,), in_specs=[row, row], out_specs=[row, vec],
                   out_shape=[jax.ShapeDtypeStruct((T, D), F32), jax.ShapeDtypeStruct((1, D), F32)],
                   compiler_params=_params(("arbitrary",)))(y, target)
    return (0.5 / D) * jnp.sum(l), dy


SB_LANES = SB_HEADS * 2 * HEAD_DIM


def _sb_consts():
    row = _iota2((Q_BLOCK, Q_BLOCK), 0)
    col = _iota2((Q_BLOCK, Q_BLOCK), 1)
    after = (row > col).astype(BF16)
    before = (row < col).astype(BF16)
    return col < row, after, before, col


def _sb_fwd(proj, ride=()):
    T = proj.shape[0]
    H, d, L = SB_HEADS, HEAD_DIM, 2 * HEAD_DIM
    nblk = T // Q_BLOCK
    scale = d ** -0.5
    n = len(ride)
    ride_out = [jax.ShapeDtypeStruct((N_DEV,) + a.shape, a.dtype) for a in ride]
    ride_in_specs, ride_out_specs, ride_sems = _ride_specs(ride, ride_out, N_DEV - 1)
    R = range(H)
    tile = lambda g: slice(g * L, (g + 1) * L)

    def body(*refs):
        q_ref, kv_ref = refs[:2]
        rin = refs[2:2 + n]
        o_ref, c_ref = refs[2 + n:4 + n]
        rout = refs[4 + n:4 + 2 * n]
        run_ref = refs[4 + 2 * n]
        i = pl.program_id(0)
        if n:
            start, finish = _gather_protocol(rin, rout, *refs[5 + 2 * n:])
            pl.when(i == 0)(start)
        causal, after, _, col = _sb_consts()
        qs = [q_ref[:, tile(g)] * scale for g in R]
        o_ref[...] = jnp.zeros_like(o_ref)
        c_ref[...] = jnp.zeros_like(c_ref)
        run_ref[...] = jnp.zeros_like(run_ref)

        def pair(j, diag):
            rows = pl.ds(pl.multiple_of(j * Q_BLOCK, Q_BLOCK), Q_BLOCK)
            kvj = [kv_ref[rows, tile(g)] for g in R]
            c = [run_ref[g] for g in R]
            acc = [o_ref[:, tile(g)] for g in R]
            cm = None if diag else [c_ref[:, tile(g)] for g in R]
            z = [_dot(qs[g], kvj[g], NT) for g in R]
            sp = [_softplus(z[g]) for g in R]
            lk = [jnp.where(causal, -sp[g], 0.0) if diag else -sp[g] for g in R]
            btw = [_dot2m(lk[g], after) for g in R]
            e = [jnp.exp((z[g] - sp[g]) + btw[g] + c[g]) for g in R]
            w = [jnp.where(causal, e[g], 0.0) if diag else e[g] for g in R]
            pv = [_bdot(w[g], kvj[g]) for g in R]
            rs = [jnp.sum(lk[g], axis=1, keepdims=True) for g in R]
            for g in R:
                o_ref[:, tile(g)] = acc[g] + pv[g]
                if not diag:
                    c_ref[:, tile(g)] = jnp.where(col == j, c[g], cm[g])
                run_ref[g] = c[g] + rs[g]

        pair(i, True)

        @pl.loop(0, i)
        def _(jj):
            pair(i - 1 - jj, False)

        if n:
            pl.when(i == nblk - 1)(finish)

    blk = pl.BlockSpec((Q_BLOCK, H * L), lambda i: (i, 0))
    full = pl.BlockSpec((T, H * L), lambda i: (0, 1))
    res = _pcall(
        body, name="sb_fwd", grid=(nblk,), in_specs=[blk, full] + ride_in_specs,
        out_specs=[blk, blk] + ride_out_specs,
        out_shape=[jax.ShapeDtypeStruct((T, H * L), F32), jax.ShapeDtypeStruct((T, H * L), F32)] + ride_out,
        scratch_shapes=[pltpu.VMEM((H, Q_BLOCK, 1), F32)] + ride_sems,
        compiler_params=_params(("arbitrary",)),
    )(proj, proj, *ride)
    return res[0], res[1], list(res[2:])


def _sb_bwd(proj, carry, do, ride=()):
    T = proj.shape[0]
    H, d, L = SB_HEADS, HEAD_DIM, 2 * HEAD_DIM
    nblk = T // Q_BLOCK
    scale = d ** -0.5
    n = len(ride)
    ride_out = [jax.ShapeDtypeStruct(a.shape, a.dtype) for a in ride]
    ride_in_specs, ride_out_specs, ride_sems = _ride_specs(ride, ride_out, N_CHIP - 1)
    R = range(H)
    tile = lambda g: slice(g * L, (g + 1) * L)

    def body(*refs):
        q_ref, kv_ref, c_ref, do_ref = refs[:4]
        rin = refs[4:4 + n]
        dq_ref, dkv_ref = refs[4 + n:6 + n]
        rout = refs[6 + n:6 + 2 * n]
        run_ref = refs[6 + 2 * n]
        i = pl.program_id(0)
        if n:
            start, finish = _chips_protocol(rin, rout, *refs[7 + 2 * n:])
            pl.when(i == 0)(start)

        @pl.when(i == 0)
        def _():
            dkv_ref[...] = jnp.zeros_like(dkv_ref)

        causal, after, before, col = _sb_consts()
        qs = [q_ref[:, tile(g)] * scale for g in R]
        dov = [do_ref[:, tile(g)] for g in R]
        qdo = [jnp.concatenate([qs[g], dov[g]], axis=0) for g in R]
        dq_ref[...] = jnp.zeros_like(dq_ref)
        run_ref[...] = jnp.zeros_like(run_ref)

        def pair(j, diag):
            rows = pl.ds(pl.multiple_of(j * Q_BLOCK, Q_BLOCK), Q_BLOCK)
            kvj = [kv_ref[rows, tile(g)] for g in R]
            gsum = [run_ref[g] for g in R]
            dq0 = [dq_ref[:, tile(g)] for g in R]
            dkv0 = [dkv_ref[rows, tile(g)] for g in R]
            cm = None if diag else [c_ref[:, tile(g)] for g in R]
            z = [_dot(qs[g], kvj[g], NT) for g in R]
            sp = [_softplus(z[g]) for g in R]
            lk = [jnp.where(causal, -sp[g], 0.0) if diag else -sp[g] for g in R]
            ls = [z[g] - sp[g] for g in R]
            logw = [ls[g] + _dot2m(lk[g], after) for g in R]
            if not diag:
                logw = [logw[g] + jnp.sum(jnp.where(col == j, cm[g], 0.0), axis=1, keepdims=True) for g in R]
            e = [jnp.exp(logw[g]) for g in R]
            w = [jnp.where(causal, e[g], 0.0) if diag else e[g] for g in R]
            gw = [_dot(dov[g], kvj[g], NT) * w[g] for g in R]
            gpre = [gsum[g] + _dot(gw[g].astype(BF16), before) for g in R]
            sig = [jnp.exp(ls[g]) for g in R]
            dz = [gw[g] * (1.0 - sig[g]) - sig[g] * gpre[g] for g in R]
            if diag:
                dz = [jnp.where(causal, dz[g], 0.0) for g in R]
            dzb = [dz[g].astype(BF16) for g in R]
            dq1 = [_dot(dzb[g], kvj[g]) for g in R]
            dkv1 = [_dot(jnp.concatenate([dzb[g], w[g].astype(BF16)], axis=0), qdo[g], TN) for g in R]
            gs1 = [jnp.sum(gw[g], axis=1, keepdims=True) for g in R]
            for g in R:
                dq_ref[:, tile(g)] = dq0[g] + dq1[g]
                dkv_ref[rows, tile(g)] = dkv0[g] + dkv1[g]
                run_ref[g] = gsum[g] + gs1[g]

        @pl.loop(0, i)
        def _(j):
            pair(j, False)

        pair(i, True)
        dq_ref[...] = dq_ref[...] * scale
        if n:
            pl.when(i == nblk - 1)(finish)

    blk = pl.BlockSpec((Q_BLOCK, H * L), lambda i: (i, 0))
    once = pl.Buffered(1)
    sh = jax.ShapeDtypeStruct((T, H * L), F32)
    res = _pcall(
        body, name="sb_bwd", grid=(nblk,),
        in_specs=[blk, pl.BlockSpec((T, H * L), lambda i: (0, 1), pipeline_mode=once), blk, blk] + ride_in_specs,
        out_specs=[blk, pl.BlockSpec((T, H * L), lambda i: (0, 0), pipeline_mode=once)] + ride_out_specs,
        out_shape=[sh, sh] + ride_out,
        scratch_shapes=[pltpu.VMEM((H, Q_BLOCK, 1), F32)] + ride_sems,
        compiler_params=_params(("arbitrary",)),
    )(proj, proj, carry, do, *ride)
    return res[0], res[1], list(res[2:])


def _swa_common(q_ref, kvp_ref, kvc_ref, qg_ref, kg_ref, sk_ref, sl_ref, n):
    W, d, G = WINDOW, HEAD_DIM, SWA_GROUP
    scale = d ** -0.5
    row = _iota2((W, 2 * W), 0)
    col = _iota2((W, 2 * W), 1)
    dist = row + W - col
    valid = (dist >= 0) & (dist < W) & ((n > 0) | (col >= W))
    distf = dist.astype(F32)
    kvcat = jnp.concatenate([kvp_ref[...], kvc_ref[...]], axis=0)
    KH, QH = range(SWA_KV_HEADS), range(SWA_HEADS)
    kraw = [kvcat[:, hk * d:(hk + 1) * d] for hk in KH]
    vcat = [kvcat[:, SWA_KVW + hk * d:SWA_KVW + (hk + 1) * d].astype(BF16) for hk in KH]
    rk = [lax.rsqrt(jnp.mean(kraw[hk] * kraw[hk], axis=-1, keepdims=True) + EPS) for hk in KH]
    kh = [kraw[hk] * rk[hk] for hk in KH]
    kn = [(kh[hk] * kg_ref[...]).astype(BF16) for hk in KH]
    qraw = [q_ref[:, h * d:(h + 1) * d] for h in QH]
    rq = [lax.rsqrt(jnp.mean(qraw[h] * qraw[h], axis=-1, keepdims=True) + EPS) for h in QH]
    qh = [qraw[h] * rq[h] for h in QH]
    qn = [(qh[h] * qg_ref[...]).astype(BF16) for h in QH]
    sink = [sk_ref[h:h + 1, :1] for h in QH]
    s = [jnp.where(valid, _dot(qn[h], kn[h // G], NT) * scale - sl_ref[h:h + 1, :1] * distf, -1e30) for h in QH]
    m = [jnp.maximum(jnp.max(s[h], axis=1, keepdims=True), sink[h]) for h in QH]
    p = [jnp.where(valid, jnp.exp(s[h] - m[h]), 0.0) for h in QH]
    esink = [jnp.exp(sink[h] - m[h]) for h in QH]
    den = [jnp.sum(p[h], axis=1, keepdims=True) + esink[h] for h in QH]
    prob = [p[h] / den[h] for h in QH]
    return vcat, rk, kh, kn, rq, qh, qn, esink, den, prob


def _swa_specs(T):
    W = WINDOW
    q = pl.BlockSpec((W, SWA_QW), lambda n: (n, 0))
    prev = pl.BlockSpec((W, 2 * SWA_KVW), lambda n: (jnp.maximum(n - 1, 0), SWA_QW // (2 * SWA_KVW)))
    cur = pl.BlockSpec((W, 2 * SWA_KVW), lambda n: (n, SWA_QW // (2 * SWA_KVW)))
    gain = pl.BlockSpec((1, HEAD_DIM), lambda n: (0, 0))
    perhead = pl.BlockSpec((SWA_HEADS, LANE), lambda n: (0, 0))
    return q, prev, cur, gain, perhead


def _swa_fwd(proj, qg, kg, sinks, slopes):
    T = proj.shape[0]
    W, d, G = WINDOW, HEAD_DIM, SWA_GROUP

    def body(q_ref, kvp_ref, kvc_ref, qg_ref, kg_ref, sk_ref, sl_ref, o_ref):
        vcat, _, _, _, _, _, _, _, _, prob = _swa_common(q_ref, kvp_ref, kvc_ref, qg_ref, kg_ref, sk_ref, sl_ref,
                                                         pl.program_id(0))
        outs = [_bdot(prob[h], vcat[h // G]) for h in range(SWA_HEADS)]
        o_ref[...] = jnp.concatenate(outs, axis=1).astype(BF16)

    q, prev, cur, gain, perhead = _swa_specs(T)
    return _pcall(
        body, name="swa_fwd", grid=(T // W,), in_specs=[q, prev, cur, gain, gain, perhead, perhead], out_specs=q,
        out_shape=jax.ShapeDtypeStruct((T, SWA_QW), BF16), compiler_params=_params(("parallel",)),
    )(proj, proj, proj, qg, kg, sinks, slopes)


def _swa_bwd(proj, qg, kg, sinks, slopes, do):
    T = proj.shape[0]
    W, d, G = WINDOW, HEAD_DIM, SWA_GROUP
    scale = d ** -0.5
    KH, QH = range(SWA_KV_HEADS), range(SWA_HEADS)

    def body(q_ref, kvp_ref, kvc_ref, qg_ref, kg_ref, sk_ref, sl_ref, do_ref,
             dq_ref, dkv_ref, dqg_ref, dkg_ref, dsk_ref):
        n = pl.program_id(0)

        @pl.when(n == 0)
        def _():
            dqg_ref[...] = jnp.zeros_like(dqg_ref)
            dkg_ref[...] = jnp.zeros_like(dkg_ref)
            dsk_ref[...] = jnp.zeros_like(dsk_ref)
            dkv_ref[...] = jnp.zeros_like(dkv_ref)

        vcat, rk, kh, kn, rq, qh, qn, esink, den, prob = _swa_common(q_ref, kvp_ref, kvc_ref, qg_ref, kg_ref,
                                                                     sk_ref, sl_ref, n)
        dov = [do_ref[:, h * d:(h + 1) * d].astype(BF16) for h in QH]
        dp = [_dot(dov[h], vcat[h // G], NT) for h in QH]
        dd = [jnp.sum(prob[h] * dp[h], axis=1, keepdims=True) for h in QH]
        dsb = [(prob[h] * (dp[h] - dd[h]) * scale).astype(BF16) for h in QH]
        dsink = [-jnp.sum((esink[h] / den[h]) * dd[h], axis=0, keepdims=True) for h in QH]
        dqn = [_dot(dsb[h], kn[h // G]) for h in QH]
        dkn_h = [_dot(dsb[h], qn[h], TN) for h in QH]
        dv_h = [_dot(prob[h].astype(BF16), dov[h], TN) for h in QH]
        dqh = [dqn[h] * qg_ref[...] for h in QH]
        dq = [rq[h] * (dqh[h] - qh[h] * jnp.mean(dqh[h] * qh[h], axis=-1, keepdims=True)) for h in QH]
        dkn = [sum(dkn_h[hk * G + g] for g in range(G)) for hk in KH]
        dvc = [sum(dv_h[hk * G + g] for g in range(G)) for hk in KH]
        dkh = [dkn[hk] * kg_ref[...] for hk in KH]
        dkraw = [rk[hk] * (dkh[hk] - kh[hk] * jnp.mean(dkh[hk] * kh[hk], axis=-1, keepdims=True)) for hk in KH]
        dq_ref[...] = jnp.concatenate(dq, axis=1)
        dqg_ref[...] += sum(jnp.sum(dqn[h] * qh[h], axis=0, keepdims=True) for h in QH)
        dkg_ref[...] += sum(jnp.sum(dkn[hk] * kh[hk], axis=0, keepdims=True) for hk in KH)
        rowh = _iota2((SWA_HEADS, LANE), 0)
        dsk_ref[...] += sum(jnp.where(rowh == h, dsink[h], 0.0) for h in QH)
        upd = jnp.concatenate(dkraw + dvc, axis=1)
        offp = pl.multiple_of(jnp.maximum(n - 1, 0) * W, W)
        offc = pl.multiple_of(n * W, W)
        dkv_ref[pl.ds(offp, W), :] += upd[:W]
        dkv_ref[pl.ds(offc, W), :] += upd[W:]

    q, prev, cur, gain, perhead = _swa_specs(T)
    kvfull = pl.BlockSpec((T, 2 * SWA_KVW), lambda n: (0, 0))
    gs = jax.ShapeDtypeStruct((1, d), F32)
    return _pcall(
        body, name="swa_bwd", grid=(T // W,), in_specs=[q, prev, cur, gain, gain, perhead, perhead, q],
        out_specs=[q, kvfull, gain, gain, perhead],
        out_shape=[jax.ShapeDtypeStruct((T, SWA_QW), F32), jax.ShapeDtypeStruct((T, 2 * SWA_KVW), F32), gs, gs,
                   jax.ShapeDtypeStruct((SWA_HEADS, LANE), F32)],
        compiler_params=_params(("arbitrary",)),
    )(proj, proj, proj, qg, kg, sinks, slopes, do)


def _alibi():
    s = [2.0 ** (-8.0 * (i + 1) / SWA_HEADS) for i in range(SWA_HEADS)]
    return jnp.broadcast_to(jnp.asarray(s, F32)[:, None], (SWA_HEADS, LANE))


def _head_tiles(lo, hi):
    shp = lo.shape[:-1]
    return jnp.concatenate([lo.reshape(shp + (SB_HEADS, HEAD_DIM)), hi.reshape(shp + (SB_HEADS, HEAD_DIM))],
                           axis=-1).reshape(shp + (SB_LANES,))


def _tile_halves(x):
    shp = x.shape[:-1]
    t = x.reshape(shp + (SB_HEADS, 2, HEAD_DIM))
    return t[..., 0, :].reshape(shp + (SB_W,)), t[..., 1, :].reshape(shp + (SB_W,))


def _att_in_weights(w_in):
    sq, sk, sv = w_in[:, :SB_W], w_in[:, SB_W:2 * SB_W], w_in[:, 2 * SB_W:3 * SB_W]
    return jnp.concatenate([_head_tiles(sq, jnp.zeros_like(sq)), _head_tiles(sk, sv)], axis=1), w_in[:, 3 * SB_W:]


def _att_out_weights(w_out):
    wo = w_out[:SB_W]
    return _head_tiles(jnp.zeros_like(wo).T, wo.T).T, w_out[SB_W:]


def _att_fwd(h, g, w_in, w_out_of, q_gain, k_gain, sinks, ride=()):
    hn = _rms_fwd(h, g, "att_norm")
    w_sb, w_swa = _att_in_weights(w_in)
    proj_sb = _mm(hn, w_sb, "nn", out_dtype=BF16, name="att_in_sb")
    proj_swa = _mm(hn, w_swa, "nn", name="att_in_swa")
    a_out, carry, gathered = _sb_fwd(proj_sb, ride)
    w_out = w_out_of(gathered)
    wo_sb, wo_swa = _att_out_weights(w_out)
    sk128 = jnp.broadcast_to(sinks.reshape(SWA_HEADS, 1), (SWA_HEADS, LANE))
    qg, kg = q_gain.reshape(1, HEAD_DIM), k_gain.reshape(1, HEAD_DIM)
    b_out = _swa_fwd(proj_swa, qg, kg, sk128, _alibi())
    h2 = _mm(a_out, wo_sb, "nn", res=h, a2=b_out, b2=wo_swa, name="att_out")
    return h2, (h, hn, proj_sb, proj_swa, carry, a_out, b_out, sk128, qg, kg), gathered


def _att_bwd(dh2, saved, g, w_in, w_out, ride=()):
    h, hn, proj_sb, proj_swa, carry, a_out, b_out, sk128, qg, kg = saved
    w_sb, w_swa = _att_in_weights(w_in)
    wo_sb, wo_swa = _att_out_weights(w_out)
    da = _mm(dh2, wo_sb, "nt", out_dtype=BF16, name="att_do_sb")
    db = _mm(dh2, wo_swa, "nt", name="att_do_swa")
    dwo_sb = _mm(a_out, dh2, "tn", out_dtype=BF16, name="att_dwout_sb")
    dwo_swa = _mm(b_out, dh2, "tn", out_dtype=BF16, name="att_dwout_swa")
    dw_out = jnp.concatenate([_tile_halves(dwo_sb.T)[1].T, dwo_swa], axis=0)
    dq, dkv, rode = _sb_bwd(proj_sb, carry, da, ride)
    dbq, dbkv, dqg, dkg, dsink = _swa_bwd(proj_swa, qg, kg, sk128, _alibi(), db)
    dproj = jnp.concatenate([dq.astype(BF16), dkv.astype(BF16), dbq.astype(BF16), dbkv.astype(BF16)], axis=1)
    w_all = jnp.concatenate([w_sb, w_swa], axis=1)
    dw_all = _mm(hn, dproj, "tn", out_dtype=BF16, name="att_dwin")
    dhn = _mm(dproj, w_all, "nt", name="att_dhn")
    dsq, _ = _tile_halves(dw_all[:, :SB_LANES])
    dsk, dsv = _tile_halves(dw_all[:, SB_LANES:2 * SB_LANES])
    dw_in = jnp.concatenate([dsq, dsk, dsv, dw_all[:, 2 * SB_LANES:]], axis=1)
    dh, dg = _rms_bwd(dhn, h, g, dh2, "att_dnorm")
    return dh, dg, dw_in, dw_out, dqg.reshape(HEAD_DIM), dkg.reshape(HEAD_DIM), dsink[:, 0], rode


CONV_ROWS = 1024
CONV_COLS = 1024
HALO = 8


def _shifted(xcat, s, tm):
    if s == 0:
        return xcat[HALO:HALO + tm]
    return pltpu.roll(xcat, s, 0)[HALO:HALO + tm]


def _conv_pre(x_ref, halo_ref, w_ref, i, tm):
    xc = x_ref[...]
    halo = jnp.where(i > 0, halo_ref[...], 0.0)
    xcat = jnp.concatenate([halo, xc], axis=0)
    w = w_ref[...]
    y = w[GDN_CONV - 1:GDN_CONV] * xc
    for kk in range(GDN_CONV - 1):
        y = y + w[kk:kk + 1] * _shifted(xcat, GDN_CONV - 1 - kk, tm)
    return xcat, y


def _l2_heads(s, qscale_of):
    outs, rs = [], []
    for hh in range(s.shape[1] // GDN_HEAD_DIM):
        sh = s[:, hh * GDN_HEAD_DIM:(hh + 1) * GDN_HEAD_DIM]
        r = lax.rsqrt(jnp.sum(sh * sh, axis=-1, keepdims=True) + EPS)
        outs.append(sh * r)
        rs.append(r)
    return outs, rs


def _conv_specs(T, col0, tm, tc):
    cur = pl.BlockSpec((tm, tc), lambda j, i: (i, j + col0 // tc))
    halo = pl.BlockSpec((HALO, tc), lambda j, i: (jnp.maximum(i * (tm // HALO) - 1, 0), j + col0 // tc))
    wsp = pl.BlockSpec((GDN_CONV, tc), lambda j, i: (0, j + col0 // tc))
    out = pl.BlockSpec((tm, tc), lambda j, i: (i, j))
    return cur, halo, wsp, out


def _conv_fwd(proj, conv_w, col0, width, norm, name):
    T = proj.shape[0]
    tm, tc = _pick(T, CONV_ROWS), CONV_COLS
    cur, halo, wsp, out = _conv_specs(T, col0, tm, tc)
    n_q_tiles = (width // 2) // tc

    def body(x_ref, halo_ref, w_ref, o_ref):
        j, i = pl.program_id(0), pl.program_id(1)
        _, y = _conv_pre(x_ref, halo_ref, w_ref, i, tm)
        s = y * _sigmoid(y)
        if norm:
            outs, _ = _l2_heads(s, None)
            qs = jnp.where(j < n_q_tiles, GDN_HEAD_DIM ** -0.5, 1.0)
            o_ref[...] = jnp.concatenate(outs, axis=1) * qs
        else:
            o_ref[...] = s

    return _pcall(body, name=name, grid=(width // tc, T // tm), in_specs=[cur, halo, wsp], out_specs=out,
                  out_shape=jax.ShapeDtypeStruct((T, width), F32),
                  compiler_params=_params(("parallel", "parallel")))(proj, proj, conv_w)


def _conv_bwd_pre(proj, conv_w, dout, col0, width, norm, name):
    T = proj.shape[0]
    tm, tc = _pick(T, CONV_ROWS), CONV_COLS
    cur, halo, wsp, out = _conv_specs(T, col0, tm, tc)
    n_q_tiles = (width // 2) // tc

    def body(x_ref, halo_ref, w_ref, d_ref, dy_ref, dw_ref):
        j, i = pl.program_id(0), pl.program_id(1)
        xcat, y = _conv_pre(x_ref, halo_ref, w_ref, i, tm)
        sg = _sigmoid(y)
        s = y * sg
        d = d_ref[...]
        if norm:
            qs = jnp.where(j < n_q_tiles, GDN_HEAD_DIM ** -0.5, 1.0)
            d = d * qs
            outs, rs = _l2_heads(s, None)
            parts = []
            for hh, (nh, r) in enumerate(zip(outs, rs)):
                dh = d[:, hh * GDN_HEAD_DIM:(hh + 1) * GDN_HEAD_DIM]
                parts.append(r * (dh - nh * jnp.sum(dh * nh, axis=-1, keepdims=True)))
            ds = jnp.concatenate(parts, axis=1)
        else:
            ds = d
        dy = ds * sg * (1.0 + y * (1.0 - sg))
        dy_ref[...] = dy
        rows = [jnp.sum(dy * _shifted(xcat, GDN_CONV - 1 - kk, tm), axis=0, keepdims=True) for kk in range(GDN_CONV)]
        part = jnp.concatenate(rows, axis=0)

        @pl.when(i == 0)
        def _():
            dw_ref[...] = part

        @pl.when(i > 0)
        def _():
            dw_ref[...] += part

    wout = pl.BlockSpec((GDN_CONV, tc), lambda j, i: (0, j))
    return _pcall(body, name=name, grid=(width // tc, T // tm), in_specs=[cur, halo, wsp, out], out_specs=[out, wout],
                  out_shape=[jax.ShapeDtypeStruct((T, width), F32), jax.ShapeDtypeStruct((GDN_CONV, width), F32)],
                  compiler_params=_params(("parallel", "arbitrary")))(proj, proj, conv_w, dout)


def _conv_bwd_in(dy, conv_w, name):
    T, C = dy.shape
    tm, tc = _pick(T, CONV_ROWS), CONV_COLS
    nrow = T // tm

    def body(d_ref, nxt_ref, w_ref, dx_ref):
        i = pl.program_id(0)
        dc = d_ref[...]
        nxt = jnp.where(i < nrow - 1, nxt_ref[...], 0.0)
        dcat = jnp.concatenate([dc, nxt], axis=0)
        w = w_ref[...]
        dx = w[GDN_CONV - 1:GDN_CONV] * dc
        for kk in range(GDN_CONV - 1):
            s = GDN_CONV - 1 - kk
            dx = dx + w[kk:kk + 1] * pltpu.roll(dcat, tm + HALO - s, 0)[:tm]
        dx_ref[...] = dx.astype(BF16)

    cur = pl.BlockSpec((tm, tc), lambda i, j: (i, j))
    nxt = pl.BlockSpec((HALO, tc), lambda i, j: (jnp.minimum((i + 1) * (tm // HALO), T // HALO - 1), j))
    wsp = pl.BlockSpec((GDN_CONV, tc), lambda i, j: (0, j))
    return _pcall(body, name=name, grid=(nrow, C // tc), in_specs=[cur, nxt, wsp], out_specs=cur,
                  out_shape=jax.ShapeDtypeStruct((T, C), BF16),
                  compiler_params=_params(("parallel", "parallel")))(dy, dy, conv_w)


GATE_ROWS = 512


def _chunk_mask(n, lower):
    row = _iota2((n, n), 0)
    col = _iota2((n, n), 1)
    same = (row // GDN_CHUNK) == (col // GDN_CHUNK)
    tri = (row >= col) if lower else (row <= col)
    return (same & tri).astype(BF16)


def _gates_fwd(proj, a_log, dt_bias):
    T = proj.shape[0]
    tm = _pick(T, GATE_ROWS)
    c0 = (GDN_CONV_W + GDN_VW) // LANE

    def body(bl_ref, a_ref, alog_ref, dt_ref, beta_ref, g_ref, gc_ref):
        beta_ref[...] = _sigmoid(bl_ref[...])
        g = -jnp.exp(alog_ref[...]) * _softplus(a_ref[...] + dt_ref[...])
        g_ref[...] = g
        gc_ref[...] = _mdot2(_chunk_mask(tm, True), g)

    blk = lambda c: pl.BlockSpec((tm, LANE), lambda i: (i, c))
    vec = pl.BlockSpec((1, LANE), lambda i: (0, 0))
    sh = jax.ShapeDtypeStruct((T, LANE), F32)
    return _pcall(body, name="gdn_gates", grid=(T // tm,), in_specs=[blk(c0), blk(c0 + 1), vec, vec],
                  out_specs=[blk(0), blk(0), blk(0)], out_shape=[sh, sh, sh],
                  compiler_params=_params(("parallel",)))(proj, proj, a_log, dt_bias)


def _gates_bwd(proj, a_log, dt_bias, beta, g, dbeta, dgc):
    T = proj.shape[0]
    tm = _pick(T, GATE_ROWS)
    c0 = (GDN_CONV_W + GDN_VW) // LANE

    def heads_in_lanes(ref):
        lane = _iota2((tm, LANE), 1)
        out = jnp.where(lane < GDN_GROUP, ref[0], 0.0)
        for grp in range(1, GDN_V_HEADS // GDN_GROUP):
            out = out + jnp.where(lane // GDN_GROUP == grp, pltpu.roll(ref[grp], grp * GDN_GROUP, 1), 0.0)
        return out

    def body(a_ref, alog_ref, dt_ref, beta_ref, g_ref, dbeta_ref, dgc_ref, dbl_ref, da_ref, dalog_ref, ddt_ref):
        dg = _mdot2(_chunk_mask(tm, False), heads_in_lanes(dgc_ref))
        b = beta_ref[...]
        dbl_ref[...] = (heads_in_lanes(dbeta_ref) * b * (1.0 - b)).astype(BF16)
        da = dg * (-jnp.exp(alog_ref[...])) * _sigmoid(a_ref[...] + dt_ref[...])
        da_ref[...] = da.astype(BF16)
        p1 = jnp.sum(dg * g_ref[...], axis=0, keepdims=True)
        p2 = jnp.sum(da, axis=0, keepdims=True)

        @pl.when(pl.program_id(0) == 0)
        def _():
            dalog_ref[...] = p1
            ddt_ref[...] = p2

        @pl.when(pl.program_id(0) > 0)
        def _():
            dalog_ref[...] += p1
            ddt_ref[...] += p2

    blk = lambda c: pl.BlockSpec((tm, LANE), lambda i: (i, c))
    vec = pl.BlockSpec((1, LANE), lambda i: (0, 0))
    grp = pl.BlockSpec((GDN_V_HEADS // GDN_GROUP, tm, LANE), lambda i: (0, i, 0))
    shb = jax.ShapeDtypeStruct((T, LANE), BF16)
    shv = jax.ShapeDtypeStruct((1, LANE), F32)
    return _pcall(body, name="gdn_dgates", grid=(T // tm,),
                  in_specs=[blk(c0 + 1), vec, vec, blk(0), blk(0), grp, grp],
                  out_specs=[blk(0), blk(0), vec, vec], out_shape=[shb, shb, shv, shv],
                  compiler_params=_params(("arbitrary",)))(proj, a_log, dt_bias, beta, g, dbeta, dgc)


def _inv_unit_lower(Ls):
    C = Ls[0].shape[0]
    row = _iota2((C, C), 0)
    col = _iota2((C, C), 1)
    blk16 = (row // 16) == (col // 16)
    blk32 = (row // 32) == (col // 32)
    eye = (row == col).astype(F32)
    xs = [-jnp.where(blk16, L, 0.0) for L in Ls]
    inv = [eye + x for x in xs]
    for _ in range(3):
        xs = [_dot3(x, x) for x in xs]
        inv = [a + _dot3(a, x) for a, x in zip(inv, xs)]
    for mask in (blk32 & ~blk16, ~blk32):
        t = [_dot3(a, jnp.where(mask, L, 0.0)) for a, L in zip(inv, Ls)]
        inv = [a - _dot3(ti, a) for a, ti in zip(inv, t)]
    return inv


GDN_GROUP = 4
GDN_PREP_CHUNKS = 32
GDN_STEP_CHUNKS = 8


def _gdn_specs(T):
    C, D, E, J = GDN_CHUNK, GDN_HEAD_DIM, GDN_GROUP, GDN_STEP_CHUNKS
    n = T // (C * J)
    qk = pl.BlockSpec((J * C, (E // 2) * D), lambda h, i: (i, h))
    vE = pl.BlockSpec((J * C, E * D), lambda h, i: (i, h))
    colv = pl.BlockSpec((J * C, LANE), lambda h, i: (i, 0))
    colo = pl.BlockSpec((None, J * C, LANE), lambda h, i: (h, i, 0))
    rowv = pl.BlockSpec((E, J, 1, C), lambda h, i: (h, i, 0, 0))
    st = pl.BlockSpec((E, J, D, D), lambda h, i: (h, i, 0, 0))
    am = pl.BlockSpec((E, J, C, C), lambda h, i: (h, i, 0, 0))
    return n, qk, vE, colv, colo, rowv, st, am


def _lane_col(blk, lane):
    return jnp.sum(jnp.where(_iota2(blk.shape, 1) == lane, blk, 0.0), axis=1, keepdims=True)


def _gdn_decay(gcol, grow):
    C = GDN_CHUNK
    row = _iota2((C, C), 0)
    col = _iota2((C, C), 1)
    incl = row >= col
    dm = jnp.where(incl, jnp.exp(jnp.where(incl, gcol - grow, 0.0)), 0.0)
    glast = grow[:, C - 1:C]
    return dm, jnp.exp(gcol), jnp.exp(glast), jnp.exp(glast - gcol), row > col, incl


def _gdn_prep(k, beta, gcol, grow):
    T = k.shape[0]
    C, D, B = GDN_CHUNK, GDN_HEAD_DIM, GDN_PREP_CHUNKS
    n = T // C

    def body(k_ref, b_ref, gc_ref, gr_ref, a_ref):
        idx = [(e, cb) for e in range(2) for cb in range(B)]
        kc = {cb: k_ref[cb * C:(cb + 1) * C, :] for cb in range(B)}
        lm = []
        head0 = 2 * pl.program_id(0)
        for e, cb in idx:
            beta = _lane_col(b_ref[cb * C:(cb + 1) * C, :], head0 + e)
            dm, _, _, _, strict, _ = _gdn_decay(_lane_col(gc_ref[cb * C:(cb + 1) * C, :], head0 + e), gr_ref[e, cb])
            lm.append(jnp.where(strict, _bdot(kc[cb] * beta, kc[cb], NT) * dm, 0.0))
        inv = _inv_unit_lower(lm)
        for (e, cb), a in zip(idx, inv):
            a_ref[e, cb] = a

    return _pcall(
        body, name="gdn_prep", grid=(GDN_K_HEADS, n // B),
        in_specs=[pl.BlockSpec((B * C, D), lambda h, i: (i, h)), pl.BlockSpec((B * C, LANE), lambda h, i: (i, 0)),
                  pl.BlockSpec((B * C, LANE), lambda h, i: (i, 0)), pl.BlockSpec((2, B, 1, C), lambda h, i: (h, i, 0, 0))],
        out_specs=pl.BlockSpec((2, B, C, C), lambda h, i: (h, i, 0, 0)),
        out_shape=jax.ShapeDtypeStruct((GDN_V_HEADS, n, C, C), F32),
        compiler_params=_params(("parallel", "parallel")),
    )(k, beta, gcol, grow)


def _gdn_fwd(q, k, v, beta, gcol, grow, amat):
    T = q.shape[0]
    C, D, E = GDN_CHUNK, GDN_HEAD_DIM, GDN_GROUP
    n, qk, vE, colv, colo, rowv, st, am = _gdn_specs(T)
    R = range(E)

    def body(q_ref, k_ref, v_ref, b_ref, gc_ref, gr_ref, a_ref, o_ref, s_ref, vn_ref, state):
        @pl.when(pl.program_id(1) == 0)
        def _():
            state[...] = jnp.zeros_like(state)

        head0 = E * pl.program_id(0)
        s = [state[e] for e in R]
        for cc in range(GDN_STEP_CHUNKS):
            rows = slice(cc * C, (cc + 1) * C)
            qv = [q_ref[rows, (e // 2) * D:(e // 2 + 1) * D] for e in R]
            kv = [k_ref[rows, (e // 2) * D:(e // 2 + 1) * D] for e in R]
            vv = [v_ref[rows, e * D:(e + 1) * D] for e in R]
            beta = [_lane_col(b_ref[rows, :], head0 + e) for e in R]
            a = [a_ref[e, cc] for e in R]
            dec = [_gdn_decay(_lane_col(gc_ref[rows, :], head0 + e), gr_ref[e, cc]) for e in R]
            pm = [_bdot(qv[e], kv[e], NT) * dec[e][0] for e in R]
            r = [beta[e] * (vv[e] - _bdot(kv[e] * dec[e][1], s[e])) for e in R]
            vn = [_dot3(a[e], r[e]) for e in R]
            o = [_bdot(qv[e] * dec[e][1], s[e]) + _bdot(pm[e], vn[e]) for e in R]
            s2 = [dec[e][2] * s[e] + _bdot(kv[e] * dec[e][3], vn[e], TN) for e in R]
            for e in R:
                s_ref[e, cc] = s[e]
                vn_ref[rows, e * D:(e + 1) * D] = vn[e]
                o_ref[rows, e * D:(e + 1) * D] = o[e]
            s = s2
        for e in R:
            state[e] = s[e]

    shv = jax.ShapeDtypeStruct((T, GDN_V_HEADS * D), F32)
    return _pcall(
        body, name="gdn_fwd", grid=(GDN_V_HEADS // E, n), in_specs=[qk, qk, vE, colv, colv, rowv, am],
        out_specs=[vE, st, vE],
        out_shape=[shv, jax.ShapeDtypeStruct((GDN_V_HEADS, T // C, D, D), F32), shv],
        scratch_shapes=[pltpu.VMEM((E, D, D), F32)],
        compiler_params=_params(("parallel", "arbitrary")),
    )(q, k, v, beta, gcol, grow, amat)


def _gdn_bwd(q, k, v, beta, gcol, grow, states, amat, vnew, do):
    T = q.shape[0]
    C, D, E = GDN_CHUNK, GDN_HEAD_DIM, GDN_GROUP
    n, qk, vE, colv, colo, rowv, st, am = _gdn_specs(T)
    rev = lambda spec: pl.BlockSpec(spec.block_shape, (lambda f: (lambda h, i: f(h, n - 1 - i)))(spec.index_map))
    qk, vE, colv, colo, rowv, st, am = (rev(s) for s in (qk, vE, colv, colo, rowv, st, am))
    R = range(E)

    def body(q_ref, k_ref, v_ref, b_ref, gc_ref, gr_ref, s_ref, a_ref, vn_ref, do_ref,
             dq_ref, dk_ref, dv_ref, db_ref, dgc_ref, dstate):
        @pl.when(pl.program_id(1) == 0)
        def _():
            dstate[...] = jnp.zeros_like(dstate)

        dcur = [dstate[e] for e in R]
        for cc in reversed(range(GDN_STEP_CHUNKS)):
            rows = slice(cc * C, (cc + 1) * C)
            M = lambda f: [f(e) for e in R]
            rsum = lambda x: jnp.sum(x, axis=1, keepdims=True)
            qv = M(lambda e: q_ref[rows, (e // 2) * D:(e // 2 + 1) * D])
            kv = M(lambda e: k_ref[rows, (e // 2) * D:(e // 2 + 1) * D])
            vv = M(lambda e: v_ref[rows, e * D:(e + 1) * D])
            vn = M(lambda e: vn_ref[rows, e * D:(e + 1) * D])
            dov = M(lambda e: do_ref[rows, e * D:(e + 1) * D])
            head0 = E * pl.program_id(0)
            beta = M(lambda e: _lane_col(b_ref[rows, :], head0 + e))
            s = M(lambda e: s_ref[e, cc])
            a = M(lambda e: a_ref[e, cc])
            dsn = dcur
            dec = M(lambda e: _gdn_decay(_lane_col(gc_ref[rows, :], head0 + e), gr_ref[e, cc]))
            dm, gam, glast, tail = (M(lambda e: dec[e][i]) for i in range(4))
            strict, incl = dec[0][4], dec[0][5]
            kb = M(lambda e: kv[e] * beta[e])
            kd = M(lambda e: kv[e] * gam[e])
            qd = M(lambda e: qv[e] * gam[e])
            kt = M(lambda e: kv[e] * tail[e])
            lmat = M(lambda e: jnp.where(strict, _bdot(kb[e], kv[e], NT) * dm[e], 0.0))
            pmat = M(lambda e: _bdot(qv[e], kv[e], NT) * dm[e])
            xres = M(lambda e: vv[e] - _bdot(kd[e], s[e]))
            dvn = M(lambda e: _bdot(pmat[e], dov[e], TN) + _bdot(kt[e], dsn[e]))
            dqd = M(lambda e: _bdot(dov[e], s[e], NT))
            dp = M(lambda e: jnp.where(incl, _bdot(dov[e], vn[e], NT), 0.0))
            dkt = M(lambda e: _bdot(vn[e], dsn[e], NT))
            dr = M(lambda e: _dot3(a[e], dvn[e], TN))
            drb = M(lambda e: beta[e] * dr[e])
            dkd = M(lambda e: -_bdot(drb[e], s[e], NT))
            ds2 = M(lambda e: _bdot(qd[e], dov[e], TN) + glast[e] * dsn[e] - _bdot(kd[e], drb[e], TN))
            dl = M(lambda e: -jnp.where(strict, _bdot(dr[e], vn[e], NT), 0.0))
            dmm = M(lambda e: dl[e] * dm[e])
            dnn = M(lambda e: dp[e] * dm[e])
            emat = M(lambda e: dl[e] * lmat[e] + dp[e] * pmat[e])
            dkb = M(lambda e: _bdot(dmm[e], kv[e]))
            dk = M(lambda e: beta[e] * dkb[e] + _bdot(dmm[e], kb[e], TN) + _bdot(dnn[e], qv[e], TN)
                   + gam[e] * dkd[e] + tail[e] * dkt[e])
            dq = M(lambda e: _bdot(dnn[e], kv[e]) + gam[e] * dqd[e])
            dbeta = M(lambda e: rsum(dr[e] * xres[e]) + rsum(dkb[e] * kv[e]))
            ones = jnp.ones((C, LANE), BF16)
            colsum = M(lambda e: _dot2m(emat[e], ones, TN)[:, :1])
            tails = M(lambda e: rsum(dkt[e] * kt[e]))
            lastrow = _iota2((C, 1), 0) == C - 1
            dlast = M(lambda e: jnp.sum(tails[e], axis=0, keepdims=True)
                      + glast[e] * jnp.sum(rsum(s[e] * dsn[e]), axis=0, keepdims=True))
            dgc = M(lambda e: rsum(emat[e]) - colsum[e] + rsum(dkd[e] * kd[e]) + rsum(dqd[e] * qd[e]) - tails[e]
                    + jnp.where(lastrow, dlast[e], 0.0))
            lane = _iota2((C, LANE), 1)
            db_all = jnp.zeros((C, LANE), F32)
            dgc_all = jnp.zeros((C, LANE), F32)
            for e in R:
                dv_ref[rows, e * D:(e + 1) * D] = drb[e]
                db_all = jnp.where(lane == e, dbeta[e], db_all)
                dgc_all = jnp.where(lane == e, dgc[e], dgc_all)
            db_ref[rows, :] = db_all
            dgc_ref[rows, :] = dgc_all
            for kh in range(E // 2):
                dq_ref[rows, kh * D:(kh + 1) * D] = dq[2 * kh] + dq[2 * kh + 1]
                dk_ref[rows, kh * D:(kh + 1) * D] = dk[2 * kh] + dk[2 * kh + 1]

            dcur = ds2
        for e in R:
            dstate[e] = dcur[e]

    shq = jax.ShapeDtypeStruct((T, GDN_K_HEADS * D), F32)
    shv = jax.ShapeDtypeStruct((T, GDN_V_HEADS * D), F32)
    shc = jax.ShapeDtypeStruct((GDN_V_HEADS // E, T, LANE), F32)
    return _pcall(
        body, name="gdn_bwd", grid=(GDN_V_HEADS // E, n),
        in_specs=[qk, qk, vE, colv, colv, rowv, st, am, vE, vE],
        out_specs=[qk, qk, vE, colo, colo], out_shape=[shq, shq, shv, shc, shc],
        scratch_shapes=[pltpu.VMEM((E, D, D), F32)],
        compiler_params=_params(("parallel", "arbitrary")),
    )(q, k, v, beta, gcol, grow, states, amat, vnew, do)


def _outgate_fwd(o, proj, gain):
    T = o.shape[0]
    tm, tc = _pick(T, CONV_ROWS), CONV_COLS
    z0 = GDN_CONV_W // tc

    def body(o_ref, z_ref, g_ref, y_ref):
        z = z_ref[...]
        sz = z * _sigmoid(z)
        parts = []
        for hh in range(tc // GDN_HEAD_DIM):
            oh = o_ref[:, hh * GDN_HEAD_DIM:(hh + 1) * GDN_HEAD_DIM]
            r = lax.rsqrt(jnp.mean(oh * oh, axis=-1, keepdims=True) + EPS)
            parts.append(oh * r * g_ref[...])
        y_ref[...] = (jnp.concatenate(parts, axis=1) * sz).astype(BF16)

    blk = pl.BlockSpec((tm, tc), lambda i, j: (i, j))
    return _pcall(body, name="gdn_outgate", grid=(T // tm, GDN_VW // tc),
                  in_specs=[blk, pl.BlockSpec((tm, tc), lambda i, j: (i, j + z0)), pl.BlockSpec((1, GDN_HEAD_DIM), lambda i, j: (0, 0))],
                  out_specs=blk, out_shape=jax.ShapeDtypeStruct((T, GDN_VW), BF16),
                  compiler_params=_params(("parallel", "parallel")))(o, proj, gain)


def _outgate_bwd(dy, o, proj, gain):
    T = o.shape[0]
    tm, tc = _pick(T, CONV_ROWS), CONV_COLS
    z0 = GDN_CONV_W // tc
    nh = tc // GDN_HEAD_DIM

    def body(dy_ref, o_ref, z_ref, g_ref, do_ref, dz_ref, dg_ref):
        z = z_ref[...]
        sg = _sigmoid(z)
        sz = z * sg
        dy = dy_ref[...]
        dgain = jnp.zeros((1, GDN_HEAD_DIM), F32)
        dos, ys = [], []
        for hh in range(nh):
            sl = slice(hh * GDN_HEAD_DIM, (hh + 1) * GDN_HEAD_DIM)
            oh = o_ref[:, sl]
            r = lax.rsqrt(jnp.mean(oh * oh, axis=-1, keepdims=True) + EPS)
            xh = oh * r
            dn = dy[:, sl] * sz[:, sl]
            dgain = dgain + jnp.sum(dn * xh, axis=0, keepdims=True)
            dxh = dn * g_ref[...]
            dos.append(r * (dxh - xh * jnp.mean(dxh * xh, axis=-1, keepdims=True)))
            ys.append(xh * g_ref[...])
        do_ref[...] = jnp.concatenate(dos, axis=1)
        dz_ref[...] = (dy * jnp.concatenate(ys, axis=1) * sg * (1.0 + z * (1.0 - sg))).astype(BF16)
        first = (pl.program_id(0) == 0) & (pl.program_id(1) == 0)

        @pl.when(first)
        def _():
            dg_ref[...] = dgain

        @pl.when(jnp.logical_not(first))
        def _():
            dg_ref[...] += dgain

    blk = pl.BlockSpec((tm, tc), lambda i, j: (i, j))
    vec = pl.BlockSpec((1, GDN_HEAD_DIM), lambda i, j: (0, 0))
    return _pcall(body, name="gdn_doutgate", grid=(T // tm, GDN_VW // tc),
                  in_specs=[blk, blk, pl.BlockSpec((tm, tc), lambda i, j: (i, j + z0)), vec],
                  out_specs=[blk, blk, vec],
                  out_shape=[jax.ShapeDtypeStruct((T, GDN_VW), F32), jax.ShapeDtypeStruct((T, GDN_VW), BF16),
                             jax.ShapeDtypeStruct((1, GDN_HEAD_DIM), F32)],
                  compiler_params=_params(("arbitrary", "arbitrary")))(dy, o, proj, gain)


def _pad_lanes(vec):
    return jnp.pad(vec.reshape(1, -1), ((0, 0), (0, LANE - vec.shape[-1])))


def _head_rows(a):
    T = a.shape[0]
    return a[:, :GDN_V_HEADS].T.reshape(GDN_V_HEADS, T // GDN_CHUNK, 1, GDN_CHUNK)


def _gdn_pad_in(w_in):
    c = GDN_CONV_W + GDN_VW
    z = jnp.zeros(w_in.shape[:-1] + (LANE - GDN_V_HEADS,), w_in.dtype)
    return jnp.concatenate([w_in[..., :c + GDN_V_HEADS], z, w_in[..., c + GDN_V_HEADS:], z], axis=-1)


def _gdn_unpad_in(dw):
    c = GDN_CONV_W + GDN_VW
    return jnp.concatenate([dw[..., :c + GDN_V_HEADS], dw[..., c + LANE:c + LANE + GDN_V_HEADS]], axis=-1)


def _gdn_mixer_fwd(h, g, w_in_pad, conv_w, a_log, dt_bias, out_gain, w_out):
    T = h.shape[0]
    hn = _rms_fwd(h, g, "gdn_norm")
    proj = _mm(hn, w_in_pad, "nn", name="gdn_in")
    qk = _conv_fwd(proj, conv_w, 0, 2 * GDN_KW, True, "gdn_conv_qk")
    vv = _conv_fwd(proj, conv_w, 2 * GDN_KW, GDN_VW, False, "gdn_conv_v")
    alog, dtb = _pad_lanes(a_log), _pad_lanes(dt_bias)
    beta, gl, gc = _gates_fwd(proj, alog, dtb)
    grow = _head_rows(gc)
    qn, kn = qk[:, :GDN_KW], qk[:, GDN_KW:]
    amat = _gdn_prep(kn, beta, gc, grow)
    o, states, vnew = _gdn_fwd(qn, kn, vv, beta, gc, grow, amat)
    gain = out_gain.reshape(1, GDN_HEAD_DIM)
    y = _outgate_fwd(o, proj, gain)
    h2 = _mm(y, w_out, "nn", res=h, name="gdn_out")
    return h2, (h, hn, proj, qn, kn, vv, beta, gl, gc, grow, o, states, amat, vnew, y, alog, dtb, gain)


def _gdn_mixer_bwd(dh2, saved, g, w_in_pad, conv_w, w_out):
    h, hn, proj, qn, kn, vv, beta, gl, gc, grow, o, states, amat, vnew, y, alog, dtb, gain = saved
    T = h.shape[0]
    dy = _mm(dh2, w_out, "nt", name="gdn_dy")
    dw_out = _mm(y, dh2, "tn", out_dtype=BF16, name="gdn_dwout")
    do, dz, dgain = _outgate_bwd(dy, o, proj, gain)
    dq, dk, dv, dbeta, dgc = _gdn_bwd(qn, kn, vv, beta, gc, grow, states, amat, vnew, do)
    dqk = jnp.concatenate([dq, dk], axis=1)
    dy_qk, dcw_qk = _conv_bwd_pre(proj, conv_w, dqk, 0, 2 * GDN_KW, True, "gdn_dconv_qk")
    dy_v, dcw_v = _conv_bwd_pre(proj, conv_w, dv, 2 * GDN_KW, GDN_VW, False, "gdn_dconv_v")
    dx_qk = _conv_bwd_in(dy_qk, conv_w[:, :2 * GDN_KW], "gdn_dconvin_qk")
    dx_v = _conv_bwd_in(dy_v, conv_w[:, 2 * GDN_KW:], "gdn_dconvin_v")
    dbl, da, dalog, ddt = _gates_bwd(proj, alog, dtb, beta, gl, dbeta, dgc)
    dproj = jnp.concatenate([dx_qk, dx_v, dz, dbl, da], axis=1)
    dw_in_pad = _mm(hn, dproj, "tn", out_dtype=BF16, name="gdn_dwin")
    dhn = _mm(dproj, w_in_pad, "nt", name="gdn_dhn")
    dh, dg = _rms_bwd(dhn, h, g, dh2, "gdn_dnorm")
    dconv = jnp.concatenate([dcw_qk, dcw_v], axis=1)
    return (dh, dg, _gdn_unpad_in(dw_in_pad), dconv, dalog[0, :GDN_V_HEADS], ddt[0, :GDN_V_HEADS],
            dgain.reshape(GDN_HEAD_DIM), dw_out)


def _instances(full):
    out = {}
    for n, a in full.items():
        if n.startswith("ffn_"):
            for i in range(2):
                for j in range(2):
                    out[(n, i, j)] = a[i, j]
        elif n in ("mix_norm", "ple_norm", "ple_w_gate", "ple_w_proj"):
            for i in range(2):
                out[(n, i)] = a[i]
        else:
            out[(n,)] = a[0]
    return out


def _stacked(inst):
    out = {}
    for n in dict.fromkeys(k[0] for k in inst):
        if n.startswith("ffn_"):
            out[n] = jnp.stack([jnp.stack([inst[(n, i, j)] for j in range(2)]) for i in range(2)])
        elif n in ("mix_norm", "ple_norm", "ple_w_gate", "ple_w_proj"):
            out[n] = jnp.stack([inst[(n, i)] for i in range(2)])
        else:
            out[n] = inst[(n,)][None]
    return out


def _local_step(x, p, target, w, late_shards=(), late_weights=None, early_grads=None, first_shards=(), first_weights=None,
                last_grads=None):
    w = dict(w)
    ffn = lambda i, j: (w[("ffn_norm", i, j)], w[("ffn_w_gate", i, j)], w[("ffn_w_up", i, j)], w[("ffn_w_down", i, j)])
    h = x
    tape = []
    for i in range(2):
        if i == 0 and first_weights is not None:
            def wd_of(gathered):
                w.update(first_weights(gathered))
                return w[("ffn_w_down", 0, 0)]
            h, s1 = _ffn_fwd(h, w[("ffn_norm", 0, 0)], w[("ffn_w_gate", 0, 0)], w[("ffn_w_up", 0, 0)], None, "ffn0a",
                             first_shards, wd_of)
        else:
            h, s1 = _ffn_fwd(h, *ffn(i, 0), f"ffn{i}a")
        if i == 0:
            def w_out_of(gathered):
                if late_weights is not None:
                    w.update(late_weights(gathered))
                return w[("att_w_out",)]
            h, s2, _ = _att_fwd(h, w[("mix_norm", 0)], w[("att_w_in",)], w_out_of, w[("att_q_norm",)],
                                w[("att_k_norm",)], w[("att_sinks",)], late_shards)
        else:
            gdn_in_pad = _gdn_pad_in(w[("gdn_w_in",)])
            h, s2 = _gdn_mixer_fwd(h, w[("mix_norm", 1)], gdn_in_pad, w[("gdn_conv_w",)], w[("gdn_a_log",)],
                                   w[("gdn_dt_bias",)], w[("gdn_out_norm",)], w[("gdn_w_out",)])
        h, s3 = _ffn_fwd(h, *ffn(i, 1), f"ffn{i}b")
        h, s4 = _ple_fwd(h, p[i], w[("ple_norm", i)], w[("ple_w_gate", i)], w[("ple_w_proj", i)], f"ple{i}")
        tape.append((s1, s2, s3, s4))

    loss, dh = _loss_head(h, target)

    g = {}
    rode = []
    for i in (1, 0):
        s1, s2, s3, s4 = tape[i]
        dh, g[("ple_norm", i)], g[("ple_w_gate", i)], g[("ple_w_proj", i)] = _ple_bwd(
            dh, s4, p[i], w[("ple_norm", i)], w[("ple_w_gate", i)], f"ple{i}")
        dh, g[("ffn_norm", i, 1)], g[("ffn_w_gate", i, 1)], g[("ffn_w_up", i, 1)], g[("ffn_w_down", i, 1)], _ = _ffn_bwd(
            dh, s3, *ffn(i, 1), f"ffn{i}b")
        if i == 0:
            ride = early_grads(g) if early_grads is not None else ()
            (dh, g[("mix_norm", 0)], g[("att_w_in",)], g[("att_w_out",)], g[("att_q_norm",)], g[("att_k_norm",)],
             g[("att_sinks",)], rode) = _att_bwd(dh, s2, w[("mix_norm", 0)], w[("att_w_in",)], w[("att_w_out",)], ride)
        else:
            (dh, g[("mix_norm", 1)], g[("gdn_w_in",)], g[("gdn_conv_w",)], g[("gdn_a_log",)], g[("gdn_dt_bias",)],
             g[("gdn_out_norm",)], g[("gdn_w_out",)]) = _gdn_mixer_bwd(
                dh, s2, w[("mix_norm", 1)], gdn_in_pad, w[("gdn_conv_w",)], w[("gdn_w_out",)])
        last_of = None
        if i == 0 and last_grads is not None:
            def last_of(dwg, dwu, dwd):
                return last_grads({**g, ("ffn_w_gate", 0, 0): dwg, ("ffn_w_up", 0, 0): dwu, ("ffn_w_down", 0, 0): dwd})
        (dh, g[("ffn_norm", i, 0)], g[("ffn_w_gate", i, 0)], g[("ffn_w_up", i, 0)], g[("ffn_w_down", i, 0)],
         rode_last) = _ffn_bwd(dh, s1, *ffn(i, 0), f"ffn{i}a", last_of)
    return loss, dh, g, rode, rode_last


MESH = pl.DeviceIdType.MESH


def _place():
    x, y, c = lax.axis_index("x"), lax.axis_index("y"), lax.axis_index("c")
    others = [((1 - x, y), 2 * (1 - x) + y), ((x, 1 - y), 2 * x + (1 - y)), ((1 - x, 1 - y), 2 * (1 - x) + (1 - y))]
    return x, y, c, 4 * x + 2 * y + c, 2 * x + y, (x, y, 1 - c), others


def _comm_call(body, arrays, out_shape, n_sems, name):
    hbm = pl.BlockSpec(memory_space=pl.ANY)
    n = len(arrays)
    return _pcall(
        body, name=name, in_specs=[hbm] * n, out_specs=[hbm] * len(out_shape), out_shape=out_shape,
        scratch_shapes=[pltpu.SemaphoreType.DMA((n, n_sems)), pltpu.SemaphoreType.DMA((n, n_sems)),
                        pltpu.SemaphoreType.DMA((n, N_CHIP))],
        compiler_params=pltpu.CompilerParams(has_side_effects=True),
    )(*arrays)


def _gather_protocol(ins, outs, send_sems, recv_sems, local_sems):
    n = len(ins)
    x, y, c, me, my_chip, sibling, others = _place()

    def copy(a, k, block, to, src=None):
        dst = outs[a].at[block]
        return pltpu.make_async_remote_copy(
            src_ref=dst if src is None else src, dst_ref=dst, send_sem=send_sems.at[a, k],
            recv_sem=recv_sems.at[a, k], device_id=to, device_id_type=MESH)

    local = [pltpu.make_async_copy(ins[a], outs[a].at[me], local_sems.at[a, 0]) for a in range(n)]
    first = []
    for a in range(n):
        first.append(copy(a, 0, me, sibling, src=ins[a]))
        first += [copy(a, 1 + j, me, (*chip, c), src=ins[a]) for j, (chip, _) in enumerate(others)]

    def start():
        for cp in local + first:
            cp.start()

    def finish():
        passed = []
        for a in range(n):
            for j, (chip, chip_idx) in enumerate(others):
                blk = 2 * chip_idx + c
                copy(a, 1 + j, blk, (x, y, c)).wait_recv()
                fwd = copy(a, 4 + j, blk, sibling)
                fwd.start()
                passed.append(fwd)
        for a in range(n):
            copy(a, 0, 2 * my_chip + (1 - c), (x, y, c)).wait_recv()
            for j, (chip, chip_idx) in enumerate(others):
                copy(a, 4 + j, 2 * chip_idx + (1 - c), (x, y, c)).wait_recv()
        for cp in first + passed:
            cp.wait_send()
        for cp in local:
            cp.wait()

    return start, finish


def _all_gather(arrays):
    n = len(arrays)

    def body(*refs):
        start, finish = _gather_protocol(refs[:n], refs[n:2 * n], *refs[2 * n:])
        start()
        finish()

    out_shape = [jax.ShapeDtypeStruct((N_DEV,) + a.shape, a.dtype) for a in arrays]
    return _comm_call(body, arrays, out_shape, N_DEV - 1, "gather_weights")


def _exchange_sibling(arrays, name):
    n = len(arrays)

    def body(*refs):
        ins, got = refs[:n], refs[n:2 * n]
        send_sems, recv_sems, _ = refs[2 * n:]
        x, y, c, me, my_chip, sibling, others = _place()
        remote = []
        for a in range(n):
            for chip in range(N_CHIP):
                rc = pltpu.make_async_remote_copy(
                    src_ref=ins[a].at[2 * chip + (1 - c)], dst_ref=got[a].at[chip], send_sem=send_sems.at[a, chip],
                    recv_sem=recv_sems.at[a, chip], device_id=sibling, device_id_type=MESH)
                rc.start()
                remote.append(rc)
        for rc in remote:
            rc.wait()

    half = [jax.ShapeDtypeStruct((N_CHIP,) + a.shape[1:], a.dtype) for a in arrays]
    return _comm_call(body, arrays, half, N_CHIP, name)


def _chips_protocol(ins, outs, send_sems, recv_sems, local_sems):
    n = len(ins)
    x, y, c, me, my_chip, sibling, others = _place()
    local = [pltpu.make_async_copy(ins[a].at[my_chip], outs[a].at[my_chip], local_sems.at[a, 0]) for a in range(n)]
    remote = [pltpu.make_async_remote_copy(
        src_ref=ins[a].at[chip_idx], dst_ref=outs[a].at[my_chip], send_sem=send_sems.at[a, j],
        recv_sem=recv_sems.at[a, j], device_id=(*chip, c), device_id_type=MESH)
        for a in range(n) for j, (chip, chip_idx) in enumerate(others)]

    def start():
        for cp in local + remote:
            cp.start()

    def finish():
        for cp in remote + local:
            cp.wait()

    return start, finish


def _exchange_chips(arrays, name):
    n = len(arrays)

    def body(*refs):
        start, finish = _chips_protocol(refs[:n], refs[n:2 * n], *refs[2 * n:])
        start()
        finish()

    out_shape = [jax.ShapeDtypeStruct(a.shape, a.dtype) for a in arrays]
    return _comm_call(body, arrays, out_shape, N_CHIP - 1, name)


def _as_rows(a, lead):
    shp = a.shape
    return a.reshape(shp[:lead] + (math.prod(shp[lead:-1]), shp[-1]))


def _row_tile(rows, cap=512):
    if rows <= cap:
        return rows
    for t in range(cap - cap % 8, 0, -8):
        if rows % t == 0:
            return t
    return rows


def _pair_sum(send, got, name):
    a3, b3 = _as_rows(send, 1), _as_rows(got, 1)
    _, rows, last = b3.shape
    tr = _row_tile(rows, 2048)

    def body(c_ref, a_ref, b_ref, o_ref):
        o_ref[...] = (a_ref[...].astype(F32) + b_ref[...].astype(F32)).astype(o_ref.dtype)

    core = lax.axis_index("c").astype(jnp.int32).reshape(1)
    out = _pcall(
        body, name=name,
        grid_spec=pltpu.PrefetchScalarGridSpec(
            num_scalar_prefetch=1, grid=(N_CHIP, rows // tr),
            in_specs=[pl.BlockSpec((None, tr, last), lambda k, i, c_ref: (2 * k + c_ref[0], i, 0)),
                      pl.BlockSpec((None, tr, last), lambda k, i, c_ref: (k, i, 0))],
            out_specs=pl.BlockSpec((None, tr, last), lambda k, i, c_ref: (k, i, 0))),
        out_shape=jax.ShapeDtypeStruct(b3.shape, got.dtype), compiler_params=_params(("parallel", "parallel")),
    )(core, a3, b3)
    return out.reshape(got.shape)


def _adamw(parts, w, m, v, name):
    lead, (rows, last) = w.shape[:-2], w.shape[-2:]
    nl = len(lead)
    tr = _row_tile(rows, 1024)
    c1 = 1.0 / (1.0 - ADAM_B1 ** ADAM_STEP)
    c2 = 1.0 / (1.0 - ADAM_B2 ** ADAM_STEP)

    def body(p_ref, w_ref, m_ref, v_ref, g_ref, d_ref, nm_ref, nv_ref):
        g = p_ref[0].astype(F32)
        for chip in range(1, N_CHIP):
            g = g + p_ref[chip].astype(F32)
        mn = ADAM_B1 * m_ref[...] + (1.0 - ADAM_B1) * g
        vn = ADAM_B2 * v_ref[...] + (1.0 - ADAM_B2) * (g * g)
        g_ref[...] = g
        nm_ref[...] = mn
        nv_ref[...] = vn
        d_ref[...] = -ADAM_LR * ((mn * c1) / (jnp.sqrt(vn * c2) + ADAM_EPS) + ADAM_WD * w_ref[...])

    row = pl.BlockSpec((None,) * nl + (tr, last), lambda *ix: ix + (0,))
    part = pl.BlockSpec((N_CHIP,) + (None,) * nl + (tr, last), lambda *ix: (0,) + ix + (0,))
    sh = jax.ShapeDtypeStruct(w.shape, F32)
    return _pcall(body, name=name, grid=lead + (rows // tr,), in_specs=[part, row, row, row],
                  out_specs=[row, row, row, row], out_shape=[sh, sh, sh, sh],
                  compiler_params=_params(("parallel",) * (nl + 1)))(parts, w, m, v)


def _pack(pieces, row_align):
    rows, offs, r = [], [], 0
    for a in pieces:
        flat = a.reshape(-1)
        nr = -(-flat.shape[0] // PACK_W)
        flat = jnp.pad(flat, (0, nr * PACK_W - flat.shape[0]))
        rows.append(flat.reshape(nr, PACK_W))
        offs.append(r)
        r += nr
    pad = (-r) % row_align
    if pad:
        rows.append(jnp.zeros((pad, PACK_W), pieces[0].dtype))
    return jnp.concatenate(rows, axis=0), offs


def _unpack(flat, offs, shapes):
    out = []
    for off, shp in zip(offs, shapes):
        size = math.prod(shp)
        nr = -(-size // PACK_W)
        out.append(flat[..., off:off + nr, :].reshape(flat.shape[:-2] + (nr * PACK_W,))[..., :size].reshape(flat.shape[:-2] + tuple(shp)))
    return out


def _to_full(gathered, axis):
    z = jnp.moveaxis(gathered, 0, axis)
    shp = list(z.shape)
    return z.reshape(shp[:axis] + [shp[axis] * shp[axis + 1]] + shp[axis + 2:])


def _to_shards(full, axis):
    shp = list(full.shape)
    z = full.reshape(shp[:axis] + [N_DEV, shp[axis] // N_DEV] + shp[axis + 1:])
    return jnp.moveaxis(z, axis, 0)


def kernel(x, p, ffn_norm, ffn_w_gate, ffn_w_up, ffn_w_down, mix_norm, att_w_in, att_q_norm, att_k_norm, att_sinks, att_w_out, gdn_w_in, gdn_conv_w, gdn_a_log, gdn_dt_bias, gdn_out_norm, gdn_w_out, ple_norm, ple_w_gate, ple_w_proj, loss_target, m_ffn_norm, m_ffn_w_gate, m_ffn_w_up, m_ffn_w_down, m_mix_norm, m_att_w_in, m_att_q_norm, m_att_k_norm, m_att_sinks, m_att_w_out, m_gdn_w_in, m_gdn_conv_w, m_gdn_a_log, m_gdn_dt_bias, m_gdn_out_norm, m_gdn_w_out, m_ple_norm, m_ple_w_gate, m_ple_w_proj, v_ffn_norm, v_ffn_w_gate, v_ffn_w_up, v_ffn_w_down, v_mix_norm, v_att_w_in, v_att_q_norm, v_att_k_norm, v_att_sinks, v_att_w_out, v_gdn_w_in, v_gdn_conv_w, v_gdn_a_log, v_gdn_dt_bias, v_gdn_out_norm, v_gdn_w_out, v_ple_norm, v_ple_w_gate, v_ple_w_proj):
    args = dict(locals())
    wts = {n: args[n] for n in WEIGHTS}
    mom = {n: args["m_" + n] for n in WEIGHTS}
    var = {n: args["v_" + n] for n in WEIGHTS}
    axis = dict(SHARDED)
    vecs = [n for n, _ in SHARDED[:SMALL_SHARDED]]
    small = vecs + list(REPLICATED)
    small_shapes = [wts[n].shape for n in small]
    lead = lambda n: 2 if n.startswith("ffn_") else 1

    def stack_of(arrays, name, idxs):
        return jnp.stack([arrays[name][idx] if idx else arrays[name][0] for idx in idxs])

    def full_instances(gathered, group):
        out = {}
        for (name, idxs), g in zip(group, gathered):
            whole = _to_full(g, axis[name] - lead(name) + 1)
            for k, idx in enumerate(idxs):
                out[(name,) + idx] = whole[k]
        return out

    def shard_stacks(g, group):
        return [_to_shards(jnp.stack([g[(name,) + idx] for idx in idxs]), axis[name] - lead(name) + 1)
                for name, idxs in group]

    vec_pack, voffs = _pack([wts[n] for n in vecs], 8)
    early = _all_gather([stack_of(wts, n, idxs).astype(BF16) for n, idxs in EARLY] + [vec_pack])
    w = full_instances(early[:-1], EARLY)
    vec_full = {n: _to_full(piece, axis[n]) for n, piece in
                zip(vecs, _unpack(early[-1], voffs, [wts[n].shape for n in vecs]))}
    w.update(_instances({**vec_full, **{n: wts[n] for n in REPLICATED}}))
    first_shards = [stack_of(wts, n, idxs).astype(BF16) for n, idxs in FIRST]
    late_shards = [stack_of(wts, n, idxs).astype(BF16) for n, idxs in LATE]

    def chip_sums(send, tag):
        got = _exchange_sibling(send, f"exchange_sibling_{tag}")
        return [_pair_sum(p_, q_, f"pair_sum_{tag}_{i}") for i, (p_, q_) in enumerate(zip(send, got))]

    loss, grad_x, g, rode, rode_last = _local_step(
        x[0], p[:, 0], loss_target[0], w, late_shards, lambda gathered: full_instances(gathered, LATE),
        lambda g: chip_sums(shard_stacks(g, RIDE), "early"),
        first_shards, lambda gathered: full_instances(gathered, FIRST),
        lambda g: chip_sums(shard_stacks(g, FINAL), "final"))

    gs = _stacked({k: v for k, v in g.items() if k[0] in small})
    vec_shards = [_to_shards(gs[n], axis[n]) for n in vecs]
    small_send = jnp.stack([_pack([sh[d] for sh in vec_shards] + [gs[n] for n in REPLICATED] + [loss.reshape(1)], 8)[0]
                            for d in range(N_DEV)])
    last = list(rode_last) + list(_exchange_chips(chip_sums([small_send], "small"), "exchange_chips_small"))

    pieces = {}
    for (name, idxs), part in list(zip(RIDE, rode)) + list(zip(FINAL, last[:-1])):
        for k, idx in enumerate(idxs):
            pieces[(name,) + idx] = part[:, k]
    outs = {}
    for n, _ in SHARDED[SMALL_SHARDED:]:
        if lead(n) == 2:
            part = jnp.stack([jnp.stack([pieces[(n, i, j)] for j in range(2)], axis=1) for i in range(2)], axis=1)
        elif (n, 0) in pieces:
            part = jnp.stack([pieces[(n, i)] for i in range(2)], axis=1)
        else:
            part = pieces[(n,)][:, None]
        outs[n] = _adamw(part, wts[n], mom[n], var[n], f"adamw_{n}")
    filler = [jnp.zeros((1,), F32)]
    small_w, soffs = _pack([wts[n] for n in small] + filler, 8)
    small_m, _ = _pack([mom[n] for n in small] + filler, 8)
    small_v, _ = _pack([var[n] for n in small] + filler, 8)
    small_out = [_unpack(z, soffs, small_shapes + [(1,)]) for z in _adamw(last[-1], small_w, small_m, small_v, "adamw_small")]
    loss = small_out[0][-1][0]
    for i, n in enumerate(small):
        outs[n] = [small_out[k][i] for k in range(4)]
    result = [loss, grad_x[None]]
    for k in range(4):
        result += [outs[n][k] for n in WEIGHTS]
    return tuple(result)
```

```python
import math

import jax
import jax.numpy as jnp
from jax import lax
from jax.experimental import pallas as pl
from jax.experimental.pallas import tpu as pltpu

F32 = jnp.float32
BF16 = jnp.bfloat16

N_DEV = 8
N_CHIP = 4
D_MODEL = 1024
D_FF = 2816
PLE_DIM = 256
HEAD_DIM = 64
SB_HEADS = 8
SWA_HEADS = 8
SWA_KV_HEADS = 2
SWA_GROUP = SWA_HEADS // SWA_KV_HEADS
WINDOW = 128
Q_BLOCK = 128
GDN_K_HEADS = 8
GDN_V_HEADS = 16
GDN_HEAD_DIM = 128
GDN_CONV = 4
GDN_CHUNK = 64
EPS = 1e-6
SB_W = SB_HEADS * HEAD_DIM
SWA_QW = SWA_HEADS * HEAD_DIM
SWA_KVW = SWA_KV_HEADS * HEAD_DIM
ATT_IN = 3 * SB_W + SWA_QW + 2 * SWA_KVW
GDN_KW = GDN_K_HEADS * GDN_HEAD_DIM
GDN_VW = GDN_V_HEADS * GDN_HEAD_DIM
GDN_CONV_W = 2 * GDN_KW + GDN_VW
GDN_IN = GDN_CONV_W + GDN_VW + 2 * GDN_V_HEADS
GDN_IN_PAD = GDN_CONV_W + GDN_VW + 2 * 128

ADAM_LR = 0.001
ADAM_B1 = 0.9
ADAM_B2 = 0.999
ADAM_EPS = 1e-08
ADAM_WD = 0.01
ADAM_STEP = 10

LANE = 128
VMEM_LIMIT = 56 * 1024 * 1024
MM_TILE_BUDGET = 40 * 1024 * 1024
PACK_W = 1024

NN = ((1,), (0,))
NT = ((1,), (1,))
TN = ((0,), (0,))

SHARDED = (
    ("ffn_norm", 2), ("gdn_conv_w", 2),
    ("ffn_w_gate", 3), ("ffn_w_up", 3), ("ffn_w_down", 2), ("att_w_in", 2), ("att_w_out", 1),
    ("gdn_w_in", 2), ("gdn_w_out", 1), ("ple_w_gate", 1), ("ple_w_proj", 2),
)
SMALL_SHARDED = 2
REPLICATED = ("mix_norm", "att_q_norm", "att_k_norm", "att_sinks", "gdn_a_log", "gdn_dt_bias",
              "gdn_out_norm", "ple_norm")
WEIGHTS = ("ffn_norm", "ffn_w_gate", "ffn_w_up", "ffn_w_down", "mix_norm", "att_w_in", "att_q_norm",
           "att_k_norm", "att_sinks", "att_w_out", "gdn_w_in", "gdn_conv_w", "gdn_a_log", "gdn_dt_bias",
           "gdn_out_norm", "gdn_w_out", "ple_norm", "ple_w_gate", "ple_w_proj")


_FFN_REST = [(0, 1), (1, 0), (1, 1)]
EARLY = [("ffn_w_gate", [(0, 0)]), ("ffn_w_up", [(0, 0)])]
FIRST = [("ffn_w_down", [(0, 0)]), ("att_w_in", [()])]
LATE = ([(n, [idx]) for n in ("ffn_w_gate", "ffn_w_up", "ffn_w_down") for idx in _FFN_REST]
        + [("att_w_out", [()]), ("gdn_w_in", [()]), ("gdn_w_out", [()]),
           ("ple_w_gate", [(0,), (1,)]), ("ple_w_proj", [(0,), (1,)])])
RIDE = [e for e in LATE if e[0] != "att_w_out"]
FINAL = EARLY + FIRST + [("att_w_out", [()])]


def _pcall(body, **kw):
    return pl.pallas_call(body, **kw)


def _params(sem=None):
    if sem is None:
        return pltpu.CompilerParams(vmem_limit_bytes=VMEM_LIMIT)
    return pltpu.CompilerParams(dimension_semantics=sem, vmem_limit_bytes=VMEM_LIMIT)


def _ride_specs(ride, out_shapes, n_sems):
    hbm = pl.BlockSpec(memory_space=pl.ANY)
    n = len(ride)
    sems = [pltpu.SemaphoreType.DMA((n, n_sems)), pltpu.SemaphoreType.DMA((n, n_sems)),
            pltpu.SemaphoreType.DMA((n, N_CHIP))] if n else []
    return [hbm] * n, [hbm] * len(out_shapes), sems


def _dot(a, b, dims=NN):
    return lax.dot_general(a, b, (dims, ((), ())), preferred_element_type=F32)


def _bdot(a, b, dims=NN):
    return _dot(a.astype(BF16), b.astype(BF16), dims)


def _split(a):
    hi = a.astype(BF16)
    lo = (a - hi.astype(F32)).astype(BF16)
    return hi, lo


def _dot3(a, b, dims=NN):
    ah, al = _split(a)
    bh, bl = _split(b)
    return _dot(ah, bh, dims) + (_dot(ah, bl, dims) + _dot(al, bh, dims))


def _dot2m(a, m, dims=NN):
    ah, al = _split(a)
    return _dot(ah, m, dims) + _dot(al, m, dims)


def _mdot2(m, a, dims=NN):
    ah, al = _split(a)
    return _dot(m, ah, dims) + _dot(m, al, dims)


def _sigmoid(x):
    return 1.0 / (1.0 + jnp.exp(-x))


def _softplus(x):
    return jnp.maximum(x, 0.0) + jnp.log(1.0 + jnp.exp(-jnp.abs(x)))


def _pick(n, cap):
    if n <= cap:
        return n
    for t in range(cap - cap % LANE, 0, -LANE):
        if n % t == 0:
            return t
    raise ValueError(f"no tile for {n} under {cap}")


def _iota2(shape, axis):
    return lax.broadcasted_iota(jnp.int32, shape, axis)


def _mm(a, b, mode, out_dtype=F32, res=None, alpha=1.0, a2=None, b2=None, name="mm", ride=()):
    if mode == "nn":
        (M, K), N = a.shape, b.shape[1]
    elif mode == "nt":
        (M, K), N = a.shape, b.shape[0]
    else:
        (K, M), N = a.shape, b.shape[1]
    tn, tk = _pick(N, 1408), _pick(K, 2048 if mode == "tn" else 1408)
    nk = K // tk
    pairs = 1 if a2 is None else 2

    def tile_bytes(tm):
        per = pairs * tk * (tm * a.dtype.itemsize + tn * b.dtype.itemsize) + tm * tn * jnp.dtype(out_dtype).itemsize
        return 2 * (per + (tm * tn * 4 if res is not None else 0)) + (tm * tn * 4 if nk > 1 else 0)

    tm = next(t for t in (_pick(M, c) for c in ((1408,) if mode == "tn" else (2048, 1024, 512))) if tile_bytes(t) <= MM_TILE_BUDGET or t <= 512)
    dims = {"nn": NN, "nt": NT, "tn": TN}[mode]
    a_spec = pl.BlockSpec((tk, tm), lambda i, j, k: (k, i)) if mode == "tn" else pl.BlockSpec((tm, tk), lambda i, j, k: (i, k))
    b_spec = pl.BlockSpec((tn, tk), lambda i, j, k: (j, k)) if mode == "nt" else pl.BlockSpec((tk, tn), lambda i, j, k: (k, j))
    o_spec = pl.BlockSpec((tm, tn), lambda i, j, k: (i, j))
    two = a2 is not None
    has_res = res is not None
    grid = (M // tm, N // tn, nk)
    nr = len(ride)
    ride_out = [jax.ShapeDtypeStruct(r.shape, r.dtype) for r in ride]
    ride_in_specs, ride_out_specs, ride_sems = _ride_specs(ride, ride_out, N_CHIP - 1)
    a2_spec, b2_spec = a_spec, b_spec
    if two and a2.shape != a.shape:
        assert nk == 1 and mode == "nn" and a2.shape[0] == M and b2.shape[1] == N
        a2_spec = pl.BlockSpec((tm, a2.shape[1]), lambda i, j, k: (i, 0))
        b2_spec = pl.BlockSpec((a2.shape[1], tn), lambda i, j, k: (0, j))

    def body(*refs):
        refs = list(refs)
        a_ref, b_ref = refs[0], refs[1]
        pos = 2
        if two:
            a2_ref, b2_ref = refs[2], refs[3]
            pos = 4
        if has_res:
            res_ref = refs[pos]
            pos += 1
        rin = refs[pos:pos + nr]
        o_ref = refs[pos + nr]
        rout = refs[pos + nr + 1:pos + 2 * nr + 1]
        acc_ref = refs[pos + 2 * nr + 1]
        k = pl.program_id(2)
        if nr:
            i, j = pl.program_id(0), pl.program_id(1)
            start, done = _chips_protocol(rin, rout, *refs[pos + 2 * nr + 2:])
            pl.when((i == 0) & (j == 0) & (k == 0))(start)
        part = _bdot(a_ref[...], b_ref[...], dims)
        if two:
            part = part + _bdot(a2_ref[...], b2_ref[...], dims)

        def finish(acc):
            out = acc * alpha if alpha != 1.0 else acc
            if has_res:
                out = res_ref[...] + out
            o_ref[...] = out.astype(out_dtype)

        if nk == 1:
            finish(part)
        else:
            @pl.when(k == 0)
            def _():
                acc_ref[...] = part

            @pl.when(k > 0)
            def _():
                acc_ref[...] += part

            @pl.when(k == nk - 1)
            def _():
                finish(acc_ref[...])

        if nr:
            pl.when((i == grid[0] - 1) & (j == grid[1] - 1) & (k == nk - 1))(done)

    ins = [a, b]
    specs = [a_spec, b_spec]
    if two:
        ins += [a2, b2]
        specs += [a2_spec, b2_spec]
    if has_res:
        ins.append(res)
        specs.append(o_spec)
    out = _pcall(
        body, name=name, grid=grid, in_specs=specs + ride_in_specs, out_specs=[o_spec] + ride_out_specs,
        out_shape=[jax.ShapeDtypeStruct((M, N), out_dtype)] + ride_out,
        scratch_shapes=[pltpu.VMEM((tm, tn) if nk > 1 else (8, LANE), F32)] + ride_sems,
        compiler_params=_params(("arbitrary",) * 3 if nr else ("parallel", "parallel", "arbitrary")),
    )(*ins, *ride)
    return (out[0], list(out[1:])) if nr else out[0]


ROW_TILE = 1024


def _rms_fwd(h, g, name):
    T, D = h.shape
    tr = _pick(T, ROW_TILE)

    def body(h_ref, g_ref, n_ref):
        x = h_ref[...]
        r = lax.rsqrt(jnp.mean(x * x, axis=-1, keepdims=True) + EPS)
        n_ref[...] = (x * r * g_ref[...]).astype(BF16)

    return _pcall(
        body, name=name, grid=(T // tr,),
        in_specs=[pl.BlockSpec((tr, D), lambda i: (i, 0)), pl.BlockSpec((1, D), lambda i: (0, 0))],
        out_specs=pl.BlockSpec((tr, D), lambda i: (i, 0)),
        out_shape=jax.ShapeDtypeStruct((T, D), BF16), compiler_params=_params(("parallel",)),
    )(h, g.reshape(1, D))


def _rms_bwd(dn, h, g, dres, name):
    T, D = h.shape
    tr = _pick(T, ROW_TILE)

    def body(dn_ref, h_ref, g_ref, dres_ref, dh_ref, dg_ref):
        x = h_ref[...]
        r = lax.rsqrt(jnp.mean(x * x, axis=-1, keepdims=True) + EPS)
        xh = x * r
        d = dn_ref[...].astype(F32)
        dxh = d * g_ref[...]
        dh_ref[...] = dres_ref[...] + r * (dxh - xh * jnp.mean(dxh * xh, axis=-1, keepdims=True))
        part = jnp.sum(d * xh, axis=0, keepdims=True)

        @pl.when(pl.program_id(0) == 0)
        def _():
            dg_ref[...] = part

        @pl.when(pl.program_id(0) > 0)
        def _():
            dg_ref[...] += part

    row = pl.BlockSpec((tr, D), lambda i: (i, 0))
    vec = pl.BlockSpec((1, D), lambda i: (0, 0))
    dh, dg = _pcall(
        body, name=name, grid=(T // tr,), in_specs=[row, row, vec, row], out_specs=[row, vec],
        out_shape=[jax.ShapeDtypeStruct((T, D), F32), jax.ShapeDtypeStruct((1, D), F32)],
        compiler_params=_params(("arbitrary",)),
    )(dn, h, g.reshape(1, D), dres)
    return dh, dg.reshape(D)


def _gateup(n, wg, wu, name, ride=()):
    T, D = n.shape
    F = wg.shape[1]
    tm, tn = _pick(T, 1024), _pick(F, 1408)
    nr = len(ride)
    ride_out = [jax.ShapeDtypeStruct((N_DEV,) + r.shape, r.dtype) for r in ride]
    ride_in_specs, ride_out_specs, ride_sems = _ride_specs(ride, ride_out, N_DEV - 1)
    grid = (T // tm, F // tn)

    def body(*refs):
        n_ref, wg_ref, wu_ref = refs[:3]
        a_ref, b_ref, hid_ref = refs[3 + nr:6 + nr]
        if nr:
            i, j = pl.program_id(0), pl.program_id(1)
            start, finish = _gather_protocol(refs[3:3 + nr], refs[6 + nr:6 + 2 * nr], *refs[6 + 2 * nr:])
            pl.when((i == 0) & (j == 0))(start)
        x = n_ref[...]
        a = _dot(x, wg_ref[...])
        b = _dot(x, wu_ref[...])
        a_ref[...] = a.astype(BF16)
        b_ref[...] = b.astype(BF16)
        hid_ref[...] = (a * _sigmoid(a) * b).astype(BF16)
        if nr:
            pl.when((i == grid[0] - 1) & (j == grid[1] - 1))(finish)

    o_spec = pl.BlockSpec((tm, tn), lambda i, j: (i, j))
    w_spec = pl.BlockSpec((D, tn), lambda i, j: (0, j))
    sh = jax.ShapeDtypeStruct((T, F), BF16)
    res = _pcall(
        body, name=name, grid=grid,
        in_specs=[pl.BlockSpec((tm, D), lambda i, j: (i, 0)), w_spec, w_spec] + ride_in_specs,
        out_specs=[o_spec, o_spec, o_spec] + ride_out_specs, out_shape=[sh, sh, sh] + ride_out,
        scratch_shapes=ride_sems,
        compiler_params=_params(("arbitrary", "arbitrary") if nr else ("parallel", "parallel")),
    )(n, wg, wu, *ride)
    return res[0], res[1], res[2], list(res[3:])


def _ffn_dhid(dy, wd, a, b, name):
    T, D = dy.shape
    F = wd.shape[0]
    tm, tn = _pick(T, 1024), _pick(F, 1408)

    def body(dy_ref, wd_ref, a_ref, b_ref, da_ref, db_ref):
        dhid = 0.5 * _bdot(dy_ref[...], wd_ref[...], NT)
        av = a_ref[...].astype(F32)
        bv = b_ref[...].astype(F32)
        s = _sigmoid(av)
        da_ref[...] = (dhid * bv * s * (1.0 + av * (1.0 - s))).astype(BF16)
        db_ref[...] = (dhid * av * s).astype(BF16)

    o_spec = pl.BlockSpec((tm, tn), lambda i, j: (i, j))
    sh = jax.ShapeDtypeStruct((T, F), BF16)
    return _pcall(
        body, name=name, grid=(T // tm, F // tn),
        in_specs=[pl.BlockSpec((tm, D), lambda i, j: (i, 0)), pl.BlockSpec((tn, D), lambda i, j: (j, 0)), o_spec, o_spec],
        out_specs=[o_spec, o_spec], out_shape=[sh, sh],
        compiler_params=_params(("parallel", "parallel")),
    )(dy, wd, a, b)


def _ffn_fwd(h, g, wg, wu, wd, tag, ride=(), wd_of=None):
    n = _rms_fwd(h, g, f"{tag}_norm")
    a, b, hid, gathered = _gateup(n, wg, wu, f"{tag}_gateup", ride)
    if wd_of is not None:
        wd = wd_of(gathered)
    h2 = _mm(hid, wd, "nn", res=h, alpha=0.5, name=f"{tag}_down")
    return h2, (h, n, a, b, hid)


def _ffn_bwd(dh2, saved, g, wg, wu, wd, tag, ride_of=None):
    h, n, a, b, hid = saved
    da, db = _ffn_dhid(dh2, wd, a, b, f"{tag}_dhid")
    dwd = _mm(hid, dh2, "tn", alpha=0.5, out_dtype=BF16, name=f"{tag}_dwd")
    dwg = _mm(n, da, "tn", out_dtype=BF16, name=f"{tag}_dwg")
    dwu = _mm(n, db, "tn", out_dtype=BF16, name=f"{tag}_dwu")
    rode = []
    if ride_of is None:
        dn = _mm(da, wg, "nt", a2=db, b2=wu, name=f"{tag}_dn")
    else:
        dn, rode = _mm(da, wg, "nt", a2=db, b2=wu, name=f"{tag}_dn", ride=ride_of(dwg, dwu, dwd))
    dh, dg = _rms_bwd(dn, h, g, dh2, f"{tag}_dnorm")
    return dh, dg, dwg, dwu, dwd, rode


def _ple_fwd(h, p, g, w_gate, w_proj, tag):
    T, D = h.shape
    pn = _rms_fwd(h, g, f"{tag}_norm")
    tm, tn = _pick(T, 1024), _pick(D, 1024)
    P = p.shape[1]

    def body(pn_ref, p_ref, wg_ref, wp_ref, h_ref, o_ref, gl_ref, pp_ref):
        gl = _dot(pn_ref[...], wg_ref[...])
        pp = _bdot(p_ref[...], wp_ref[...])
        gl_ref[...] = gl
        pp_ref[...] = pp
        o_ref[...] = h_ref[...] + _sigmoid(gl) * pp

    o_spec = pl.BlockSpec((tm, tn), lambda i, j: (i, j))
    sh = jax.ShapeDtypeStruct((T, D), F32)
    h2, gl, pp = _pcall(
        body, name=f"{tag}_fwd", grid=(T // tm, D // tn),
        in_specs=[pl.BlockSpec((tm, D), lambda i, j: (i, 0)), pl.BlockSpec((tm, P), lambda i, j: (i, 0)),
                  pl.BlockSpec((D, tn), lambda i, j: (0, j)), pl.BlockSpec((P, tn), lambda i, j: (0, j)), o_spec],
        out_specs=[o_spec, o_spec, o_spec], out_shape=[sh, sh, sh],
        compiler_params=_params(("parallel", "parallel")),
    )(pn, p, w_gate, w_proj, h)
    return h2, (h, pn, gl, pp)


def _ple_bwd(dh2, saved, p, g, w_gate, tag):
    h, pn, gl, pp = saved
    T, D = h.shape
    tr = _pick(T, ROW_TILE)

    def body(d_ref, gl_ref, pp_ref, dgl_ref, dpp_ref):
        d = d_ref[...]
        s = _sigmoid(gl_ref[...])
        dpp_ref[...] = (d * s).astype(BF16)
        dgl_ref[...] = (d * pp_ref[...] * s * (1.0 - s)).astype(BF16)

    row = pl.BlockSpec((tr, D), lambda i: (i, 0))
    sh = jax.ShapeDtypeStruct((T, D), BF16)
    dgl, dpp = _pcall(body, name=f"{tag}_dgate", grid=(T // tr,), in_specs=[row, row, row], out_specs=[row, row],
                      out_shape=[sh, sh], compiler_params=_params(("parallel",)))(dh2, gl, pp)
    dw_proj = _mm(p, dpp, "tn", out_dtype=BF16, name=f"{tag}_dwproj")
    dw_gate = _mm(pn, dgl, "tn", out_dtype=BF16, name=f"{tag}_dwgate")
    dpn = _mm(dgl, w_gate, "nt", name=f"{tag}_dpn")
    dh, dg = _rms_bwd(dpn, h, g, dh2, f"{tag}_dnorm")
    return dh, dg, dw_gate, dw_proj


def _loss_head(y, target):
    T, D = y.shape
    tr = _pick(T, ROW_TILE)

    def body(y_ref, t_ref, dy_ref, l_ref):
        e = y_ref[...] - t_ref[...]
        dy_ref[...] = e * (1.0 / D)
        part = jnp.sum(e * e, axis=0, keepdims=True)

        @pl.when(pl.program_id(0) == 0)
        def _():
            l_ref[...] = part

        @pl.when(pl.program_id(0) > 0)
        def _():
            l_ref[...] += part

    row = pl.BlockSpec((tr, D), lambda i: (i, 0))
    vec = pl.BlockSpec((1, D), lambda i: (0, 0))
    dy, l = _pcall(body, name="loss_head", grid=(T // tr,), in_specs=[row, row], out_specs=[row, vec],
                   out_shape=[jax.ShapeDtypeStruct((T, D), F32), jax.ShapeDtypeStruct((1, D), F32)],
                   compiler_params=_params(("arbitrary",)))(y, target)
    return (0.5 / D) * jnp.sum(l), dy


SB_LANES = SB_HEADS * 2 * HEAD_DIM


def _sb_consts():
    row = _iota2((Q_BLOCK, Q_BLOCK), 0)
    col = _iota2((Q_BLOCK, Q_BLOCK), 1)
    after = (row > col).astype(BF16)
    before = (row < col).astype(BF16)
    return col < row, after, before, col


def _sb_fwd(proj, ride=()):
    T = proj.shape[0]
    H, d, L = SB_HEADS, HEAD_DIM, 2 * HEAD_DIM
    nblk = T // Q_BLOCK
    scale = d ** -0.5
    n = len(ride)
    ride_out = [jax.ShapeDtypeStruct((N_DEV,) + a.shape, a.dtype) for a in ride]
    ride_in_specs, ride_out_specs, ride_sems = _ride_specs(ride, ride_out, N_DEV - 1)
    R = range(H)
    tile = lambda g: slice(g * L, (g + 1) * L)

    def body(*refs):
        q_ref, kv_ref = refs[:2]
        rin = refs[2:2 + n]
        o_ref, c_ref = refs[2 + n:4 + n]
        rout = refs[4 + n:4 + 2 * n]
        run_ref = refs[4 + 2 * n]
        i = pl.program_id(0)
        if n:
            start, finish = _gather_protocol(rin, rout, *refs[5 + 2 * n:])
            pl.when(i == 0)(start)
        causal, after, _, col = _sb_consts()
        qs = [q_ref[:, tile(g)] * scale for g in R]
        o_ref[...] = jnp.zeros_like(o_ref)
        c_ref[...] = jnp.zeros_like(c_ref)
        run_ref[...] = jnp.zeros_like(run_ref)

        def pair(j, diag):
            rows = pl.ds(pl.multiple_of(j * Q_BLOCK, Q_BLOCK), Q_BLOCK)
            kvj = [kv_ref[rows, tile(g)] for g in R]
            c = [run_ref[g] for g in R]
            acc = [o_ref[:, tile(g)] for g in R]
            cm = None if diag else [c_ref[:, tile(g)] for g in R]
            z = [_dot(qs[g], kvj[g], NT) for g in R]
            sp = [_softplus(z[g]) for g in R]
            lk = [jnp.where(causal, -sp[g], 0.0) if diag else -sp[g] for g in R]
            btw = [_dot2m(lk[g], after) for g in R]
            e = [jnp.exp((z[g] - sp[g]) + btw[g] + c[g]) for g in R]
            w = [jnp.where(causal, e[g], 0.0) if diag else e[g] for g in R]
            pv = [_bdot(w[g], kvj[g]) for g in R]
            rs = [jnp.sum(lk[g], axis=1, keepdims=True) for g in R]
            for g in R:
                o_ref[:, tile(g)] = acc[g] + pv[g]
                if not diag:
                    c_ref[:, tile(g)] = jnp.where(col == j, c[g], cm[g])
                run_ref[g] = c[g] + rs[g]

        pair(i, True)

        @pl.loop(0, i)
        def _(jj):
            pair(i - 1 - jj, False)

        if n:
            pl.when(i == nblk - 1)(finish)

    blk = pl.BlockSpec((Q_BLOCK, H * L), lambda i: (i, 0))
    full = pl.BlockSpec((T, H * L), lambda i: (0, 1))
    res = _pcall(
        body, name="sb_fwd", grid=(nblk,), in_specs=[blk, full] + ride_in_specs,
        out_specs=[blk, blk] + ride_out_specs,
        out_shape=[jax.ShapeDtypeStruct((T, H * L), F32), jax.ShapeDtypeStruct((T, H * L), F32)] + ride_out,
        scratch_shapes=[pltpu.VMEM((H, Q_BLOCK, 1), F32)] + ride_sems,
        compiler_params=_params(("arbitrary",)),
    )(proj, proj, *ride)
    return res[0], res[1], list(res[2:])


def _sb_bwd(proj, carry, do, ride=()):
    T = proj.shape[0]
    H, d, L = SB_HEADS, HEAD_DIM, 2 * HEAD_DIM
    nblk = T // Q_BLOCK
    scale = d ** -0.5
    n = len(ride)
    ride_out = [jax.ShapeDtypeStruct(a.shape, a.dtype) for a in ride]
    ride_in_specs, ride_out_specs, ride_sems = _ride_specs(ride, ride_out, N_CHIP - 1)
    R = range(H)
    tile = lambda g: slice(g * L, (g + 1) * L)

    def body(*refs):
        q_ref, kv_ref, c_ref, do_ref = refs[:4]
        rin = refs[4:4 + n]
        dq_ref, dkv_ref = refs[4 + n:6 + n]
        rout = refs[6 + n:6 + 2 * n]
        run_ref = refs[6 + 2 * n]
        i = pl.program_id(0)
        if n:
            start, finish = _chips_protocol(rin, rout, *refs[7 + 2 * n:])
            pl.when(i == 0)(start)

        @pl.when(i == 0)
        def _():
            dkv_ref[...] = jnp.zeros_like(dkv_ref)

        causal, after, before, col = _sb_consts()
        qs = [q_ref[:, tile(g)] * scale for g in R]
        dov = [do_ref[:, tile(g)] for g in R]
        qdo = [jnp.concatenate([qs[g], dov[g]], axis=0) for g in R]
        dq_ref[...] = jnp.zeros_like(dq_ref)
        run_ref[...] = jnp.zeros_like(run_ref)

        def pair(j, diag):
            rows = pl.ds(pl.multiple_of(j * Q_BLOCK, Q_BLOCK), Q_BLOCK)
            kvj = [kv_ref[rows, tile(g)] for g in R]
            gsum = [run_ref[g] for g in R]
            dq0 = [dq_ref[:, tile(g)] for g in R]
            dkv0 = [dkv_ref[rows, tile(g)] for g in R]
            cm = None if diag else [c_ref[:, tile(g)] for g in R]
            z = [_dot(qs[g], kvj[g], NT) for g in R]
            sp = [_softplus(z[g]) for g in R]
            lk = [jnp.where(causal, -sp[g], 0.0) if diag else -sp[g] for g in R]
            ls = [z[g] - sp[g] for g in R]
            logw = [ls[g] + _dot2m(lk[g], after) for g in R]
            if not diag:
                logw = [logw[g] + jnp.sum(jnp.where(col == j, cm[g], 0.0), axis=1, keepdims=True) for g in R]
            e = [jnp.exp(logw[g]) for g in R]
            w = [jnp.where(causal, e[g], 0.0) if diag else e[g] for g in R]
            gw = [_dot(dov[g], kvj[g], NT) * w[g] for g in R]
            gpre = [gsum[g] + _dot(gw[g].astype(BF16), before) for g in R]
            sig = [jnp.exp(ls[g]) for g in R]
            dz = [gw[g] * (1.0 - sig[g]) - sig[g] * gpre[g] for g in R]
            if diag:
                dz = [jnp.where(causal, dz[g], 0.0) for g in R]
            dzb = [dz[g].astype(BF16) for g in R]
            dq1 = [_dot(dzb[g], kvj[g]) for g in R]
            dkv1 = [_dot(jnp.concatenate([dzb[g], w[g].astype(BF16)], axis=0), qdo[g], TN) for g in R]
            gs1 = [jnp.sum(gw[g], axis=1, keepdims=True) for g in R]
            for g in R:
                dq_ref[:, tile(g)] = dq0[g] + dq1[g]
                dkv_ref[rows, tile(g)] = dkv0[g] + dkv1[g]
                run_ref[g] = gsum[g] + gs1[g]

        @pl.loop(0, i)
        def _(j):
            pair(j, False)

        pair(i, True)
        dq_ref[...] = dq_ref[...] * scale
        if n:
            pl.when(i == nblk - 1)(finish)

    blk = pl.BlockSpec((Q_BLOCK, H * L), lambda i: (i, 0))
    once = pl.Buffered(1)
    sh = jax.ShapeDtypeStruct((T, H * L), F32)
    res = _pcall(
        body, name="sb_bwd", grid=(nblk,),
        in_specs=[blk, pl.BlockSpec((T, H * L), lambda i: (0, 1), pipeline_mode=once), blk, blk] + ride_in_specs,
        out_specs=[blk, pl.BlockSpec((T, H * L), lambda i: (0, 0), pipeline_mode=once)] + ride_out_specs,
        out_shape=[sh, sh] + ride_out,
        scratch_shapes=[pltpu.VMEM((H, Q_BLOCK, 1), F32)] + ride_sems,
        compiler_params=_params(("arbitrary",)),
    )(proj, proj, carry, do, *ride)
    return res[0], res[1], list(res[2:])


def _swa_common(q_ref, kvp_ref, kvc_ref, qg_ref, kg_ref, sk_ref, sl_ref, n):
    W, d, G = WINDOW, HEAD_DIM, SWA_GROUP
    scale = d ** -0.5
    row = _iota2((W, 2 * W), 0)
    col = _iota2((W, 2 * W), 1)
    dist = row + W - col
    valid = (dist >= 0) & (dist < W) & ((n > 0) | (col >= W))
    distf = dist.astype(F32)
    kvcat = jnp.concatenate([kvp_ref[...], kvc_ref[...]], axis=0)
    KH, QH = range(SWA_KV_HEADS), range(SWA_HEADS)
    kraw = [kvcat[:, hk * d:(hk + 1) * d] for hk in KH]
    vcat = [kvcat[:, SWA_KVW + hk * d:SWA_KVW + (hk + 1) * d].astype(BF16) for hk in KH]
    rk = [lax.rsqrt(jnp.mean(kraw[hk] * kraw[hk], axis=-1, keepdims=True) + EPS) for hk in KH]
    kh = [kraw[hk] * rk[hk] for hk in KH]
    kn = [(kh[hk] * kg_ref[...]).astype(BF16) for hk in KH]
    qraw = [q_ref[:, h * d:(h + 1) * d] for h in QH]
    rq = [lax.rsqrt(jnp.mean(qraw[h] * qraw[h], axis=-1, keepdims=True) + EPS) for h in QH]
    qh = [qraw[h] * rq[h] for h in QH]
    qn = [(qh[h] * qg_ref[...]).astype(BF16) for h in QH]
    sink = [sk_ref[h:h + 1, :1] for h in QH]
    s = [jnp.where(valid, _dot(qn[h], kn[h // G], NT) * scale - sl_ref[h:h + 1, :1] * distf, -1e30) for h in QH]
    m = [jnp.maximum(jnp.max(s[h], axis=1, keepdims=True), sink[h]) for h in QH]
    p = [jnp.where(valid, jnp.exp(s[h] - m[h]), 0.0) for h in QH]
    esink = [jnp.exp(sink[h] - m[h]) for h in QH]
    den = [jnp.sum(p[h], axis=1, keepdims=True) + esink[h] for h in QH]
    prob = [p[h] / den[h] for h in QH]
    return vcat, rk, kh, kn, rq, qh, qn, esink, den, prob


def _swa_specs(T):
    W = WINDOW
    q = pl.BlockSpec((W, SWA_QW), lambda n: (n, 0))
    prev = pl.BlockSpec((W, 2 * SWA_KVW), lambda n: (jnp.maximum(n - 1, 0), SWA_QW // (2 * SWA_KVW)))
    cur = pl.BlockSpec((W, 2 * SWA_KVW), lambda n: (n, SWA_QW // (2 * SWA_KVW)))
    gain = pl.BlockSpec((1, HEAD_DIM), lambda n: (0, 0))
    perhead = pl.BlockSpec((SWA_HEADS, LANE), lambda n: (0, 0))
    return q, prev, cur, gain, perhead


def _swa_fwd(proj, qg, kg, sinks, slopes):
    T = proj.shape[0]
    W, d, G = WINDOW, HEAD_DIM, SWA_GROUP

    def body(q_ref, kvp_ref, kvc_ref, qg_ref, kg_ref, sk_ref, sl_ref, o_ref):
        vcat, _, _, _, _, _, _, _, _, prob = _swa_common(q_ref, kvp_ref, kvc_ref, qg_ref, kg_ref, sk_ref, sl_ref,
                                                         pl.program_id(0))
        outs = [_bdot(prob[h], vcat[h // G]) for h in range(SWA_HEADS)]
        o_ref[...] = jnp.concatenate(outs, axis=1).astype(BF16)

    q, prev, cur, gain, perhead = _swa_specs(T)
    return _pcall(
        body, name="swa_fwd", grid=(T // W,), in_specs=[q, prev, cur, gain, gain, perhead, perhead], out_specs=q,
        out_shape=jax.ShapeDtypeStruct((T, SWA_QW), BF16), compiler_params=_params(("parallel",)),
    )(proj, proj, proj, qg, kg, sinks, slopes)


def _swa_bwd(proj, qg, kg, sinks, slopes, do):
    T = proj.shape[0]
    W, d, G = WINDOW, HEAD_DIM, SWA_GROUP
    scale = d ** -0.5
    KH, QH = range(SWA_KV_HEADS), range(SWA_HEADS)

    def body(q_ref, kvp_ref, kvc_ref, qg_ref, kg_ref, sk_ref, sl_ref, do_ref,
             dq_ref, dkv_ref, dqg_ref, dkg_ref, dsk_ref):
        n = pl.program_id(0)

        @pl.when(n == 0)
        def _():
            dqg_ref[...] = jnp.zeros_like(dqg_ref)
            dkg_ref[...] = jnp.zeros_like(dkg_ref)
            dsk_ref[...] = jnp.zeros_like(dsk_ref)
            dkv_ref[...] = jnp.zeros_like(dkv_ref)

        vcat, rk, kh, kn, rq, qh, qn, esink, den, prob = _swa_common(q_ref, kvp_ref, kvc_ref, qg_ref, kg_ref,
                                                                     sk_ref, sl_ref, n)
        dov = [do_ref[:, h * d:(h + 1) * d].astype(BF16) for h in QH]
        dp = [_dot(dov[h], vcat[h // G], NT) for h in QH]
        dd = [jnp.sum(prob[h] * dp[h], axis=1, keepdims=True) for h in QH]
        dsb = [(prob[h] * (dp[h] - dd[h]) * scale).astype(BF16) for h in QH]
        dsink = [-jnp.sum((esink[h] / den[h]) * dd[h], axis=0, keepdims=True) for h in QH]
        dqn = [_dot(dsb[h], kn[h // G]) for h in QH]
        dkn_h = [_dot(dsb[h], qn[h], TN) for h in QH]
        dv_h = [_dot(prob[h].astype(BF16), dov[h], TN) for h in QH]
        dqh = [dqn[h] * qg_ref[...] for h in QH]
        dq = [rq[h] * (dqh[h] - qh[h] * jnp.mean(dqh[h] * qh[h], axis=-1, keepdims=True)) for h in QH]
        dkn = [sum(dkn_h[hk * G + g] for g in range(G)) for hk in KH]
        dvc = [sum(dv_h[hk * G + g] for g in range(G)) for hk in KH]
        dkh = [dkn[hk] * kg_ref[...] for hk in KH]
        dkraw = [rk[hk] * (dkh[hk] - kh[hk] * jnp.mean(dkh[hk] * kh[hk], axis=-1, keepdims=True)) for hk in KH]
        dq_ref[...] = jnp.concatenate(dq, axis=1)
        dqg_ref[...] += sum(jnp.sum(dqn[h] * qh[h], axis=0, keepdims=True) for h in QH)
        dkg_ref[...] += sum(jnp.sum(dkn[hk] * kh[hk], axis=0, keepdims=True) for hk in KH)
        rowh = _iota2((SWA_HEADS, LANE), 0)
        dsk_ref[...] += sum(jnp.where(rowh == h, dsink[h], 0.0) for h in QH)
        upd = jnp.concatenate(dkraw + dvc, axis=1)
        offp = pl.multiple_of(jnp.maximum(n - 1, 0) * W, W)
        offc = pl.multiple_of(n * W, W)
        dkv_ref[pl.ds(offp, W), :] += upd[:W]
        dkv_ref[pl.ds(offc, W), :] += upd[W:]

    q, prev, cur, gain, perhead = _swa_specs(T)
    kvfull = pl.BlockSpec((T, 2 * SWA_KVW), lambda n: (0, 0))
    gs = jax.ShapeDtypeStruct((1, d), F32)
    return _pcall(
        body, name="swa_bwd", grid=(T // W,), in_specs=[q, prev, cur, gain, gain, perhead, perhead, q],
        out_specs=[q, kvfull, gain, gain, perhead],
        out_shape=[jax.ShapeDtypeStruct((T, SWA_QW), F32), jax.ShapeDtypeStruct((T, 2 * SWA_KVW), F32), gs, gs,
                   jax.ShapeDtypeStruct((SWA_HEADS, LANE), F32)],
        compiler_params=_params(("arbitrary",)),
    )(proj, proj, proj, qg, kg, sinks, slopes, do)


def _alibi():
    s = [2.0 ** (-8.0 * (i + 1) / SWA_HEADS) for i in range(SWA_HEADS)]
    return jnp.broadcast_to(jnp.asarray(s, F32)[:, None], (SWA_HEADS, LANE))


def _head_tiles(lo, hi):
    shp = lo.shape[:-1]
    return jnp.concatenate([lo.reshape(shp + (SB_HEADS, HEAD_DIM)), hi.reshape(shp + (SB_HEADS, HEAD_DIM))],
                           axis=-1).reshape(shp + (SB_LANES,))


def _tile_halves(x):
    shp = x.shape[:-1]
    t = x.reshape(shp + (SB_HEADS, 2, HEAD_DIM))
    return t[..., 0, :].reshape(shp + (SB_W,)), t[..., 1, :].reshape(shp + (SB_W,))


def _att_in_weights(w_in):
    sq, sk, sv = w_in[:, :SB_W], w_in[:, SB_W:2 * SB_W], w_in[:, 2 * SB_W:3 * SB_W]
    return jnp.concatenate([_head_tiles(sq, jnp.zeros_like(sq)), _head_tiles(sk, sv)], axis=1), w_in[:, 3 * SB_W:]


def _att_out_weights(w_out):
    wo = w_out[:SB_W]
    return _head_tiles(jnp.zeros_like(wo).T, wo.T).T, w_out[SB_W:]


def _att_fwd(h, g, w_in, w_out_of, q_gain, k_gain, sinks, ride=()):
    hn = _rms_fwd(h, g, "att_norm")
    w_sb, w_swa = _att_in_weights(w_in)
    proj_sb = _mm(hn, w_sb, "nn", out_dtype=BF16, name="att_in_sb")
    proj_swa = _mm(hn, w_swa, "nn", name="att_in_swa")
    a_out, carry, gathered = _sb_fwd(proj_sb, ride)
    w_out = w_out_of(gathered)
    wo_sb, wo_swa = _att_out_weights(w_out)
    sk128 = jnp.broadcast_to(sinks.reshape(SWA_HEADS, 1), (SWA_HEADS, LANE))
    qg, kg = q_gain.reshape(1, HEAD_DIM), k_gain.reshape(1, HEAD_DIM)
    b_out = _swa_fwd(proj_swa, qg, kg, sk128, _alibi())
    h2 = _mm(a_out, wo_sb, "nn", res=h, a2=b_out, b2=wo_swa, name="att_out")
    return h2, (h, hn, proj_sb, proj_swa, carry, a_out, b_out, sk128, qg, kg), gathered


def _att_bwd(dh2, saved, g, w_in, w_out, ride=()):
    h, hn, proj_sb, proj_swa, carry, a_out, b_out, sk128, qg, kg = saved
    w_sb, w_swa = _att_in_weights(w_in)
    wo_sb, wo_swa = _att_out_weights(w_out)
    da = _mm(dh2, wo_sb, "nt", out_dtype=BF16, name="att_do_sb")
    db = _mm(dh2, wo_swa, "nt", name="att_do_swa")
    dwo_sb = _mm(a_out, dh2, "tn", out_dtype=BF16, name="att_dwout_sb")
    dwo_swa = _mm(b_out, dh2, "tn", out_dtype=BF16, name="att_dwout_swa")
    dw_out = jnp.concatenate([_tile_halves(dwo_sb.T)[1].T, dwo_swa], axis=0)
    dq, dkv, rode = _sb_bwd(proj_sb, carry, da, ride)
    dbq, dbkv, dqg, dkg, dsink = _swa_bwd(proj_swa, qg, kg, sk128, _alibi(), db)
    dproj = jnp.concatenate([dq.astype(BF16), dkv.astype(BF16), dbq.astype(BF16), dbkv.astype(BF16)], axis=1)
    w_all = jnp.concatenate([w_sb, w_swa], axis=1)
    dw_all = _mm(hn, dproj, "tn", out_dtype=BF16, name="att_dwin")
    dhn = _mm(dproj, w_all, "nt", name="att_dhn")
    dsq, _ = _tile_halves(dw_all[:, :SB_LANES])
    dsk, dsv = _tile_halves(dw_all[:, SB_LANES:2 * SB_LANES])
    dw_in = jnp.concatenate([dsq, dsk, dsv, dw_all[:, 2 * SB_LANES:]], axis=1)
    dh, dg = _rms_bwd(dhn, h, g, dh2, "att_dnorm")
    return dh, dg, dw_in, dw_out, dqg.reshape(HEAD_DIM), dkg.reshape(HEAD_DIM), dsink[:, 0], rode


CONV_ROWS = 1024
CONV_COLS = 1024
HALO = 8


def _shifted(xcat, s, tm):
    if s == 0:
        return xcat[HALO:HALO + tm]
    return pltpu.roll(xcat, s, 0)[HALO:HALO + tm]


def _conv_pre(x_ref, halo_ref, w_ref, i, tm):
    xc = x_ref[...]
    halo = jnp.where(i > 0, halo_ref[...], 0.0)
    xcat = jnp.concatenate([halo, xc], axis=0)
    w = w_ref[...]
    y = w[GDN_CONV - 1:GDN_CONV] * xc
    for kk in range(GDN_CONV - 1):
        y = y + w[kk:kk + 1] * _shifted(xcat, GDN_CONV - 1 - kk, tm)
    return xcat, y


def _l2_heads(s, qscale_of):
    outs, rs = [], []
    for hh in range(s.shape[1] // GDN_HEAD_DIM):
        sh = s[:, hh * GDN_HEAD_DIM:(hh + 1) * GDN_HEAD_DIM]
        r = lax.rsqrt(jnp.sum(sh * sh, axis=-1, keepdims=True) + EPS)
        outs.append(sh * r)
        rs.append(r)
    return outs, rs


def _conv_specs(T, col0, tm, tc):
    cur = pl.BlockSpec((tm, tc), lambda j, i: (i, j + col0 // tc))
    halo = pl.BlockSpec((HALO, tc), lambda j, i: (jnp.maximum(i * (tm // HALO) - 1, 0), j + col0 // tc))
    wsp = pl.BlockSpec((GDN_CONV, tc), lambda j, i: (0, j + col0 // tc))
    out = pl.BlockSpec((tm, tc), lambda j, i: (i, j))
    return cur, halo, wsp, out


def _conv_fwd(proj, conv_w, col0, width, norm, name):
    T = proj.shape[0]
    tm, tc = _pick(T, CONV_ROWS), CONV_COLS
    cur, halo, wsp, out = _conv_specs(T, col0, tm, tc)
    n_q_tiles = (width // 2) // tc

    def body(x_ref, halo_ref, w_ref, o_ref):
        j, i = pl.program_id(0), pl.program_id(1)
        _, y = _conv_pre(x_ref, halo_ref, w_ref, i, tm)
        s = y * _sigmoid(y)
        if norm:
            outs, _ = _l2_heads(s, None)
            qs = jnp.where(j < n_q_tiles, GDN_HEAD_DIM ** -0.5, 1.0)
            o_ref[...] = jnp.concatenate(outs, axis=1) * qs
        else:
            o_ref[...] = s

    return _pcall(body, name=name, grid=(width // tc, T // tm), in_specs=[cur, halo, wsp], out_specs=out,
                  out_shape=jax.ShapeDtypeStruct((T, width), F32),
                  compiler_params=_params(("parallel", "parallel")))(proj, proj, conv_w)


def _conv_bwd_pre(proj, conv_w, dout, col0, width, norm, name):
    T = proj.shape[0]
    tm, tc = _pick(T, CONV_ROWS), CONV_COLS
    cur, halo, wsp, out = _conv_specs(T, col0, tm, tc)
    n_q_tiles = (width // 2) // tc

    def body(x_ref, halo_ref, w_ref, d_ref, dy_ref, dw_ref):
        j, i = pl.program_id(0), pl.program_id(1)
        xcat, y = _conv_pre(x_ref, halo_ref, w_ref, i, tm)
        sg = _sigmoid(y)
        s = y * sg
        d = d_ref[...]
        if norm:
            qs = jnp.where(j < n_q_tiles, GDN_HEAD_DIM ** -0.5, 1.0)
            d = d * qs
            outs, rs = _l2_heads(s, None)
            parts = []
            for hh, (nh, r) in enumerate(zip(outs, rs)):
                dh = d[:, hh * GDN_HEAD_DIM:(hh + 1) * GDN_HEAD_DIM]
                parts.append(r * (dh - nh * jnp.sum(dh * nh, axis=-1, keepdims=True)))
            ds = jnp.concatenate(parts, axis=1)
        else:
            ds = d
        dy = ds * sg * (1.0 + y * (1.0 - sg))
        dy_ref[...] = dy
        rows = [jnp.sum(dy * _shifted(xcat, GDN_CONV - 1 - kk, tm), axis=0, keepdims=True) for kk in range(GDN_CONV)]
        part = jnp.concatenate(rows, axis=0)

        @pl.when(i == 0)
        def _():
            dw_ref[...] = part

        @pl.when(i > 0)
        def _():
            dw_ref[...] += part

    wout = pl.BlockSpec((GDN_CONV, tc), lambda j, i: (0, j))
    return _pcall(body, name=name, grid=(width // tc, T // tm), in_specs=[cur, halo, wsp, out], out_specs=[out, wout],
                  out_shape=[jax.ShapeDtypeStruct((T, width), F32), jax.ShapeDtypeStruct((GDN_CONV, width), F32)],
                  compiler_params=_params(("parallel", "arbitrary")))(proj, proj, conv_w, dout)


def _conv_bwd_in(dy, conv_w, name):
    T, C = dy.shape
    tm, tc = _pick(T, CONV_ROWS), CONV_COLS
    nrow = T // tm

    def body(d_ref, nxt_ref, w_ref, dx_ref):
        i = pl.program_id(0)
        dc = d_ref[...]
        nxt = jnp.where(i < nrow - 1, nxt_ref[...], 0.0)
        dcat = jnp.concatenate([dc, nxt], axis=0)
        w = w_ref[...]
        dx = w[GDN_CONV - 1:GDN_CONV] * dc
        for kk in range(GDN_CONV - 1):
            s = GDN_CONV - 1 - kk
            dx = dx + w[kk:kk + 1] * pltpu.roll(dcat, tm + HALO - s, 0)[:tm]
        dx_ref[...] = dx.astype(BF16)

    cur = pl.BlockSpec((tm, tc), lambda i, j: (i, j))
    nxt = pl.BlockSpec((HALO, tc), lambda i, j: (jnp.minimum((i + 1) * (tm // HALO), T // HALO - 1), j))
    wsp = pl.BlockSpec((GDN_CONV, tc), lambda i, j: (0, j))
    return _pcall(body, name=name, grid=(nrow, C // tc), in_specs=[cur, nxt, wsp], out_specs=cur,
                  out_shape=jax.ShapeDtypeStruct((T, C), BF16),
                  compiler_params=_params(("parallel", "parallel")))(dy, dy, conv_w)


GATE_ROWS = 512


def _chunk_mask(n, lower):
    row = _iota2((n, n), 0)
    col = _iota2((n, n), 1)
    same = (row // GDN_CHUNK) == (col // GDN_CHUNK)
    tri = (row >= col) if lower else (row <= col)
    return (same & tri).astype(BF16)


def _gates_fwd(proj, a_log, dt_bias):
    T = proj.shape[0]
    tm = _pick(T, GATE_ROWS)
    c0 = (GDN_CONV_W + GDN_VW) // LANE

    def body(bl_ref, a_ref, alog_ref, dt_ref, beta_ref, g_ref, gc_ref):
        beta_ref[...] = _sigmoid(bl_ref[...])
        g = -jnp.exp(alog_ref[...]) * _softplus(a_ref[...] + dt_ref[...])
        g_ref[...] = g
        gc_ref[...] = _mdot2(_chunk_mask(tm, True), g)

    blk = lambda c: pl.BlockSpec((tm, LANE), lambda i: (i, c))
    vec = pl.BlockSpec((1, LANE), lambda i: (0, 0))
    sh = jax.ShapeDtypeStruct((T, LANE), F32)
    return _pcall(body, name="gdn_gates", grid=(T // tm,), in_specs=[blk(c0), blk(c0 + 1), vec, vec],
                  out_specs=[blk(0), blk(0), blk(0)], out_shape=[sh, sh, sh],
                  compiler_params=_params(("parallel",)))(proj, proj, a_log, dt_bias)


def _gates_bwd(proj, a_log, dt_bias, beta, g, dbeta, dgc):
    T = proj.shape[0]
    tm = _pick(T, GATE_ROWS)
    c0 = (GDN_CONV_W + GDN_VW) // LANE

    def heads_in_lanes(ref):
        lane = _iota2((tm, LANE), 1)
        out = jnp.where(lane < GDN_GROUP, ref[0], 0.0)
        for grp in range(1, GDN_V_HEADS // GDN_GROUP):
            out = out + jnp.where(lane // GDN_GROUP == grp, pltpu.roll(ref[grp], grp * GDN_GROUP, 1), 0.0)
        return out

    def body(a_ref, alog_ref, dt_ref, beta_ref, g_ref, dbeta_ref, dgc_ref, dbl_ref, da_ref, dalog_ref, ddt_ref):
        dg = _mdot2(_chunk_mask(tm, False), heads_in_lanes(dgc_ref))
        b = beta_ref[...]
        dbl_ref[...] = (heads_in_lanes(dbeta_ref) * b * (1.0 - b)).astype(BF16)
        da = dg * (-jnp.exp(alog_ref[...])) * _sigmoid(a_ref[...] + dt_ref[...])
        da_ref[...] = da.astype(BF16)
        p1 = jnp.sum(dg * g_ref[...], axis=0, keepdims=True)
        p2 = jnp.sum(da, axis=0, keepdims=True)

        @pl.when(pl.program_id(0) == 0)
        def _():
            dalog_ref[...] = p1
            ddt_ref[...] = p2

        @pl.when(pl.program_id(0) > 0)
        def _():
            dalog_ref[...] += p1
            ddt_ref[...] += p2

    blk = lambda c: pl.BlockSpec((tm, LANE), lambda i: (i, c))
    vec = pl.BlockSpec((1, LANE), lambda i: (0, 0))
    grp = pl.BlockSpec((GDN_V_HEADS // GDN_GROUP, tm, LANE), lambda i: (0, i, 0))
    shb = jax.ShapeDtypeStruct((T, LANE), BF16)
    shv = jax.ShapeDtypeStruct((1, LANE), F32)
    return _pcall(body, name="gdn_dgates", grid=(T // tm,),
                  in_specs=[blk(c0 + 1), vec, vec, blk(0), blk(0), grp, grp],
                  out_specs=[blk(0), blk(0), vec, vec], out_shape=[shb, shb, shv, shv],
                  compiler_params=_params(("arbitrary",)))(proj, a_log, dt_bias, beta, g, dbeta, dgc)


def _inv_unit_lower(Ls):
    C = Ls[0].shape[0]
    row = _iota2((C, C), 0)
    col = _iota2((C, C), 1)
    blk16 = (row // 16) == (col // 16)
    blk32 = (row // 32) == (col // 32)
    eye = (row == col).astype(F32)
    xs = [-jnp.where(blk16, L, 0.0) for L in Ls]
    inv = [eye + x for x in xs]
    for _ in range(3):
        xs = [_dot3(x, x) for x in xs]
        inv = [a + _dot3(a, x) for a, x in zip(inv, xs)]
    for mask in (blk32 & ~blk16, ~blk32):
        t = [_dot3(a, jnp.where(mask, L, 0.0)) for a, L in zip(inv, Ls)]
        inv = [a - _dot3(ti, a) for a, ti in zip(inv, t)]
    return inv


GDN_GROUP = 4
GDN_PREP_CHUNKS = 32
GDN_STEP_CHUNKS = 8


def _gdn_specs(T):
    C, D, E, J = GDN_CHUNK, GDN_HEAD_DIM, GDN_GROUP, GDN_STEP_CHUNKS
    n = T // (C * J)
    qk = pl.BlockSpec((J * C, (E // 2) * D), lambda h, i: (i, h))
    vE = pl.BlockSpec((J * C, E * D), lambda h, i: (i, h))
    colv = pl.BlockSpec((J * C, LANE), lambda h, i: (i, 0))
    colo = pl.BlockSpec((None, J * C, LANE), lambda h, i: (h, i, 0))
    rowv = pl.BlockSpec((E, J, 1, C), lambda h, i: (h, i, 0, 0))
    st = pl.BlockSpec((E, J, D, D), lambda h, i: (h, i, 0, 0))
    am = pl.BlockSpec((E, J, C, C), lambda h, i: (h, i, 0, 0))
    return n, qk, vE, colv, colo, rowv, st, am


def _lane_col(blk, lane):
    return jnp.sum(jnp.where(_iota2(blk.shape, 1) == lane, blk, 0.0), axis=1, keepdims=True)


def _gdn_decay(gcol, grow):
    C = GDN_CHUNK
    row = _iota2((C, C), 0)
    col = _iota2((C, C), 1)
    incl = row >= col
    dm = jnp.where(incl, jnp.exp(jnp.where(incl, gcol - grow, 0.0)), 0.0)
    glast = grow[:, C - 1:C]
    return dm, jnp.exp(gcol), jnp.exp(glast), jnp.exp(glast - gcol), row > col, incl


def _gdn_prep(k, beta, gcol, grow):
    T = k.shape[0]
    C, D, B = GDN_CHUNK, GDN_HEAD_DIM, GDN_PREP_CHUNKS
    n = T // C

    def body(k_ref, b_ref, gc_ref, gr_ref, a_ref):
        idx = [(e, cb) for e in range(2) for cb in range(B)]
        kc = {cb: k_ref[cb * C:(cb + 1) * C, :] for cb in range(B)}
        lm = []
        head0 = 2 * pl.program_id(0)
        for e, cb in idx:
            beta = _lane_col(b_ref[cb * C:(cb + 1) * C, :], head0 + e)
            dm, _, _, _, strict, _ = _gdn_decay(_lane_col(gc_ref[cb * C:(cb + 1) * C, :], head0 + e), gr_ref[e, cb])
            lm.append(jnp.where(strict, _bdot(kc[cb] * beta, kc[cb], NT) * dm, 0.0))
        inv = _inv_unit_lower(lm)
        for (e, cb), a in zip(idx, inv):
            a_ref[e, cb] = a

    return _pcall(
        body, name="gdn_prep", grid=(GDN_K_HEADS, n // B),
        in_specs=[pl.BlockSpec((B * C, D), lambda h, i: (i, h)), pl.BlockSpec((B * C, LANE), lambda h, i: (i, 0)),
                  pl.BlockSpec((B * C, LANE), lambda h, i: (i, 0)), pl.BlockSpec((2, B, 1, C), lambda h, i: (h, i, 0, 0))],
        out_specs=pl.BlockSpec((2, B, C, C), lambda h, i: (h, i, 0, 0)),
        out_shape=jax.ShapeDtypeStruct((GDN_V_HEADS, n, C, C), F32),
        compiler_params=_params(("parallel", "parallel")),
    )(k, beta, gcol, grow)


def _gdn_fwd(q, k, v, beta, gcol, grow, amat):
    T = q.shape[0]
    C, D, E = GDN_CHUNK, GDN_HEAD_DIM, GDN_GROUP
    n, qk, vE, colv, colo, rowv, st, am = _gdn_specs(T)
    R = range(E)

    def body(q_ref, k_ref, v_ref, b_ref, gc_ref, gr_ref, a_ref, o_ref, s_ref, vn_ref, state):
        @pl.when(pl.program_id(1) == 0)
        def _():
            state[...] = jnp.zeros_like(state)

        head0 = E * pl.program_id(0)
        s = [state[e] for e in R]
        for cc in range(GDN_STEP_CHUNKS):
            rows = slice(cc * C, (cc + 1) * C)
            qv = [q_ref[rows, (e // 2) * D:(e // 2 + 1) * D] for e in R]
            kv = [k_ref[rows, (e // 2) * D:(e // 2 + 1) * D] for e in R]
            vv = [v_ref[rows, e * D:(e + 1) * D] for e in R]
            beta = [_lane_col(b_ref[rows, :], head0 + e) for e in R]
            a = [a_ref[e, cc] for e in R]
            dec = [_gdn_decay(_lane_col(gc_ref[rows, :], head0 + e), gr_ref[e, cc]) for e in R]
            pm = [_bdot(qv[e], kv[e], NT) * dec[e][0] for e in R]
            r = [beta[e] * (vv[e] - _bdot(kv[e] * dec[e][1], s[e])) for e in R]
            vn = [_dot3(a[e], r[e]) for e in R]
            o = [_bdot(qv[e] * dec[e][1], s[e]) + _bdot(pm[e], vn[e]) for e in R]
            s2 = [dec[e][2] * s[e] + _bdot(kv[e] * dec[e][3], vn[e], TN) for e in R]
            for e in R:
                s_ref[e, cc] = s[e]
                vn_ref[rows, e * D:(e + 1) * D] = vn[e]
                o_ref[rows, e * D:(e + 1) * D] = o[e]
            s = s2
        for e in R:
            state[e] = s[e]

    shv = jax.ShapeDtypeStruct((T, GDN_V_HEADS * D), F32)
    return _pcall(
        body, name="gdn_fwd", grid=(GDN_V_HEADS // E, n), in_specs=[qk, qk, vE, colv, colv, rowv, am],
        out_specs=[vE, st, vE],
        out_shape=[shv, jax.ShapeDtypeStruct((GDN_V_HEADS, T // C, D, D), F32), shv],
        scratch_shapes=[pltpu.VMEM((E, D, D), F32)],
        compiler_params=_params(("parallel", "arbitrary")),
    )(q, k, v, beta, gcol, grow, amat)


def _gdn_bwd(q, k, v, beta, gcol, grow, states, amat, vnew, do):
    T = q.shape[0]
    C, D, E = GDN_CHUNK, GDN_HEAD_DIM, GDN_GROUP
    n, qk, vE, colv, colo, rowv, st, am = _gdn_specs(T)
    rev = lambda spec: pl.BlockSpec(spec.block_shape, (lambda f: (lambda h, i: f(h, n - 1 - i)))(spec.index_map))
    qk, vE, colv, colo, rowv, st, am = (rev(s) for s in (qk, vE, colv, colo, rowv, st, am))
    R = range(E)

    def body(q_ref, k_ref, v_ref, b_ref, gc_ref, gr_ref, s_ref, a_ref, vn_ref, do_ref,
             dq_ref, dk_ref, dv_ref, db_ref, dgc_ref, dstate):
        @pl.when(pl.program_id(1) == 0)
        def _():
            dstate[...] = jnp.zeros_like(dstate)

        dcur = [dstate[e] for e in R]
        for cc in reversed(range(GDN_STEP_CHUNKS)):
            rows = slice(cc * C, (cc + 1) * C)
            M = lambda f: [f(e) for e in R]
            rsum = lambda x: jnp.sum(x, axis=1, keepdims=True)
            qv = M(lambda e: q_ref[rows, (e // 2) * D:(e // 2 + 1) * D])
            kv = M(lambda e: k_ref[rows, (e // 2) * D:(e // 2 + 1) * D])
            vv = M(lambda e: v_ref[rows, e * D:(e + 1) * D])
            vn = M(lambda e: vn_ref[rows, e * D:(e + 1) * D])
            dov = M(lambda e: do_ref[rows, e * D:(e + 1) * D])
            head0 = E * pl.program_id(0)
            beta = M(lambda e: _lane_col(b_ref[rows, :], head0 + e))
            s = M(lambda e: s_ref[e, cc])
            a = M(lambda e: a_ref[e, cc])
            dsn = dcur
            dec = M(lambda e: _gdn_decay(_lane_col(gc_ref[rows, :], head0 + e), gr_ref[e, cc]))
            dm, gam, glast, tail = (M(lambda e: dec[e][i]) for i in range(4))
            strict, incl = dec[0][4], dec[0][5]
            kb = M(lambda e: kv[e] * beta[e])
            kd = M(lambda e: kv[e] * gam[e])
            qd = M(lambda e: qv[e] * gam[e])
            kt = M(lambda e: kv[e] * tail[e])
            lmat = M(lambda e: jnp.where(strict, _bdot(kb[e], kv[e], NT) * dm[e], 0.0))
            pmat = M(lambda e: _bdot(qv[e], kv[e], NT) * dm[e])
            xres = M(lambda e: vv[e] - _bdot(kd[e], s[e]))
            dvn = M(lambda e: _bdot(pmat[e], dov[e], TN) + _bdot(kt[e], dsn[e]))
            dqd = M(lambda e: _bdot(dov[e], s[e], NT))
            dp = M(lambda e: jnp.where(incl, _bdot(dov[e], vn[e], NT), 0.0))
            dkt = M(lambda e: _bdot(vn[e], dsn[e], NT))
            dr = M(lambda e: _dot3(a[e], dvn[e], TN))
            drb = M(lambda e: beta[e] * dr[e])
            dkd = M(lambda e: -_bdot(drb[e], s[e], NT))
            ds2 = M(lambda e: _bdot(qd[e], dov[e], TN) + glast[e] * dsn[e] - _bdot(kd[e], drb[e], TN))
            dl = M(lambda e: -jnp.where(strict, _bdot(dr[e], vn[e], NT), 0.0))
            dmm = M(lambda e: dl[e] * dm[e])
            dnn = M(lambda e: dp[e] * dm[e])
            emat = M(lambda e: dl[e] * lmat[e] + dp[e] * pmat[e])
            dkb = M(lambda e: _bdot(dmm[e], kv[e]))
            dk = M(lambda e: beta[e] * dkb[e] + _bdot(dmm[e], kb[e], TN) + _bdot(dnn[e], qv[e], TN)
                   + gam[e] * dkd[e] + tail[e] * dkt[e])
            dq = M(lambda e: _bdot(dnn[e], kv[e]) + gam[e] * dqd[e])
            dbeta = M(lambda e: rsum(dr[e] * xres[e]) + rsum(dkb[e] * kv[e]))
            ones = jnp.ones((C, LANE), BF16)
            colsum = M(lambda e: _dot2m(emat[e], ones, TN)[:, :1])
            tails = M(lambda e: rsum(dkt[e] * kt[e]))
            lastrow = _iota2((C, 1), 0) == C - 1
            dlast = M(lambda e: jnp.sum(tails[e], axis=0, keepdims=True)
                      + glast[e] * jnp.sum(rsum(s[e] * dsn[e]), axis=0, keepdims=True))
            dgc = M(lambda e: rsum(emat[e]) - colsum[e] + rsum(dkd[e] * kd[e]) + rsum(dqd[e] * qd[e]) - tails[e]
                    + jnp.where(lastrow, dlast[e], 0.0))
            lane = _iota2((C, LANE), 1)
            db_all = jnp.zeros((C, LANE), F32)
            dgc_all = jnp.zeros((C, LANE), F32)
            for e in R:
                dv_ref[rows, e * D:(e + 1) * D] = drb[e]
                db_all = jnp.where(lane == e, dbeta[e], db_all)
                dgc_all = jnp.where(lane == e, dgc[e], dgc_all)
            db_ref[rows, :] = db_all
            dgc_ref[rows, :] = dgc_all
            for kh in range(E // 2):
                dq_ref[rows, kh * D:(kh + 1) * D] = dq[2 * kh] + dq[2 * kh + 1]
                dk_ref[rows, kh * D:(kh + 1) * D] = dk[2 * kh] + dk[2 * kh + 1]

            dcur = ds2
        for e in R:
            dstate[e] = dcur[e]

    shq = jax.ShapeDtypeStruct((T, GDN_K_HEADS * D), F32)
    shv = jax.ShapeDtypeStruct((T, GDN_V_HEADS * D), F32)
    shc = jax.ShapeDtypeStruct((GDN_V_HEADS // E, T, LANE), F32)
    return _pcall(
        body, name="gdn_bwd", grid=(GDN_V_HEADS // E, n),
        in_specs=[qk, qk, vE, colv, colv, rowv, st, am, vE, vE],
        out_specs=[qk, qk, vE, colo, colo], out_shape=[shq, shq, shv, shc, shc],
        scratch_shapes=[pltpu.VMEM((E, D, D), F32)],
        compiler_params=_params(("parallel", "arbitrary")),
    )(q, k, v, beta, gcol, grow, states, amat, vnew, do)


def _outgate_fwd(o, proj, gain):
    T = o.shape[0]
    tm, tc = _pick(T, CONV_ROWS), CONV_COLS
    z0 = GDN_CONV_W // tc

    def body(o_ref, z_ref, g_ref, y_ref):
        z = z_ref[...]
        sz = z * _sigmoid(z)
        parts = []
        for hh in range(tc // GDN_HEAD_DIM):
            oh = o_ref[:, hh * GDN_HEAD_DIM:(hh + 1) * GDN_HEAD_DIM]
            r = lax.rsqrt(jnp.mean(oh * oh, axis=-1, keepdims=True) + EPS)
            parts.append(oh * r * g_ref[...])
        y_ref[...] = (jnp.concatenate(parts, axis=1) * sz).astype(BF16)

    blk = pl.BlockSpec((tm, tc), lambda i, j: (i, j))
    return _pcall(body, name="gdn_outgate", grid=(T // tm, GDN_VW // tc),
                  in_specs=[blk, pl.BlockSpec((tm, tc), lambda i, j: (i, j + z0)), pl.BlockSpec((1, GDN_HEAD_DIM), lambda i, j: (0, 0))],
                  out_specs=blk, out_shape=jax.ShapeDtypeStruct((T, GDN_VW), BF16),
                  compiler_params=_params(("parallel", "parallel")))(o, proj, gain)


def _outgate_bwd(dy, o, proj, gain):
    T = o.shape[0]
    tm, tc = _pick(T, CONV_ROWS), CONV_COLS
    z0 = GDN_CONV_W // tc
    nh = tc // GDN_HEAD_DIM

    def body(dy_ref, o_ref, z_ref, g_ref, do_ref, dz_ref, dg_ref):
        z = z_ref[...]
        sg = _sigmoid(z)
        sz = z * sg
        dy = dy_ref[...]
        dgain = jnp.zeros((1, GDN_HEAD_DIM), F32)
        dos, ys = [], []
        for hh in range(nh):
            sl = slice(hh * GDN_HEAD_DIM, (hh + 1) * GDN_HEAD_DIM)
            oh = o_ref[:, sl]
            r = lax.rsqrt(jnp.mean(oh * oh, axis=-1, keepdims=True) + EPS)
            xh = oh * r
            dn = dy[:, sl] * sz[:, sl]
            dgain = dgain + jnp.sum(dn * xh, axis=0, keepdims=True)
            dxh = dn * g_ref[...]
            dos.append(r * (dxh - xh * jnp.mean(dxh * xh, axis=-1, keepdims=True)))
            ys.append(xh * g_ref[...])
        do_ref[...] = jnp.concatenate(dos, axis=1)
        dz_ref[...] = (dy * jnp.concatenate(ys, axis=1) * sg * (1.0 + z * (1.0 - sg))).astype(BF16)
        first = (pl.program_id(0) == 0) & (pl.program_id(1) == 0)

        @pl.when(first)
        def _():
            dg_ref[...] = dgain

        @pl.when(jnp.logical_not(first))
        def _():
            dg_ref[...] += dgain

    blk = pl.BlockSpec((tm, tc), lambda i, j: (i, j))
    vec = pl.BlockSpec((1, GDN_HEAD_DIM), lambda i, j: (0, 0))
    return _pcall(body, name="gdn_doutgate", grid=(T // tm, GDN_VW // tc),
                  in_specs=[blk, blk, pl.BlockSpec((tm, tc), lambda i, j: (i, j + z0)), vec],
                  out_specs=[blk, blk, vec],
                  out_shape=[jax.ShapeDtypeStruct((T, GDN_VW), F32), jax.ShapeDtypeStruct((T, GDN_VW), BF16),
                             jax.ShapeDtypeStruct((1, GDN_HEAD_DIM), F32)],
                  compiler_params=_params(("arbitrary", "arbitrary")))(dy, o, proj, gain)


def _pad_lanes(vec):
    return jnp.pad(vec.reshape(1, -1), ((0, 0), (0, LANE - vec.shape[-1])))


def _head_rows(a):
    T = a.shape[0]
    return a[:, :GDN_V_HEADS].T.reshape(GDN_V_HEADS, T // GDN_CHUNK, 1, GDN_CHUNK)


def _gdn_pad_in(w_in):
    c = GDN_CONV_W + GDN_VW
    z = jnp.zeros(w_in.shape[:-1] + (LANE - GDN_V_HEADS,), w_in.dtype)
    return jnp.concatenate([w_in[..., :c + GDN_V_HEADS], z, w_in[..., c + GDN_V_HEADS:], z], axis=-1)


def _gdn_unpad_in(dw):
    c = GDN_CONV_W + GDN_VW
    return jnp.concatenate([dw[..., :c + GDN_V_HEADS], dw[..., c + LANE:c + LANE + GDN_V_HEADS]], axis=-1)


def _gdn_mixer_fwd(h, g, w_in_pad, conv_w, a_log, dt_bias, out_gain, w_out):
    T = h.shape[0]
    hn = _rms_fwd(h, g, "gdn_norm")
    proj = _mm(hn, w_in_pad, "nn", name="gdn_in")
    qk = _conv_fwd(proj, conv_w, 0, 2 * GDN_KW, True, "gdn_conv_qk")
    vv = _conv_fwd(proj, conv_w, 2 * GDN_KW, GDN_VW, False, "gdn_conv_v")
    alog, dtb = _pad_lanes(a_log), _pad_lanes(dt_bias)
    beta, gl, gc = _gates_fwd(proj, alog, dtb)
    grow = _head_rows(gc)
    qn, kn = qk[:, :GDN_KW], qk[:, GDN_KW:]
    amat = _gdn_prep(kn, beta, gc, grow)
    o, states, vnew = _gdn_fwd(qn, kn, vv, beta, gc, grow, amat)
    gain = out_gain.reshape(1, GDN_HEAD_DIM)
    y = _outgate_fwd(o, proj, gain)
    h2 = _mm(y, w_out, "nn", res=h, name="gdn_out")
    return h2, (h, hn, proj, qn, kn, vv, beta, gl, gc, grow, o, states, amat, vnew, y, alog, dtb, gain)


def _gdn_mixer_bwd(dh2, saved, g, w_in_pad, conv_w, w_out):
    h, hn, proj, qn, kn, vv, beta, gl, gc, grow, o, states, amat, vnew, y, alog, dtb, gain = saved
    T = h.shape[0]
    dy = _mm(dh2, w_out, "nt", name="gdn_dy")
    dw_out = _mm(y, dh2, "tn", out_dtype=BF16, name="gdn_dwout")
    do, dz, dgain = _outgate_bwd(dy, o, proj, gain)
    dq, dk, dv, dbeta, dgc = _gdn_bwd(qn, kn, vv, beta, gc, grow, states, amat, vnew, do)
    dqk = jnp.concatenate([dq, dk], axis=1)
    dy_qk, dcw_qk = _conv_bwd_pre(proj, conv_w, dqk, 0, 2 * GDN_KW, True, "gdn_dconv_qk")
    dy_v, dcw_v = _conv_bwd_pre(proj, conv_w, dv, 2 * GDN_KW, GDN_VW, False, "gdn_dconv_v")
    dx_qk = _conv_bwd_in(dy_qk, conv_w[:, :2 * GDN_KW], "gdn_dconvin_qk")
    dx_v = _conv_bwd_in(dy_v, conv_w[:, 2 * GDN_KW:], "gdn_dconvin_v")
    dbl, da, dalog, ddt = _gates_bwd(proj, alog, dtb, beta, gl, dbeta, dgc)
    dproj = jnp.concatenate([dx_qk, dx_v, dz, dbl, da], axis=1)
    dw_in_pad = _mm(hn, dproj, "tn", out_dtype=BF16, name="gdn_dwin")
    dhn = _mm(dproj, w_in_pad, "nt", name="gdn_dhn")
    dh, dg = _rms_bwd(dhn, h, g, dh2, "gdn_dnorm")
    dconv = jnp.concatenate([dcw_qk, dcw_v], axis=1)
    return (dh, dg, _gdn_unpad_in(dw_in_pad), dconv, dalog[0, :GDN_V_HEADS], ddt[0, :GDN_V_HEADS],
            dgain.reshape(GDN_HEAD_DIM), dw_out)


def _instances(full):
    out = {}
    for n, a in full.items():
        if n.startswith("ffn_"):
            for i in range(2):
                for j in range(2):
                    out[(n, i, j)] = a[i, j]
        elif n in ("mix_norm", "ple_norm", "ple_w_gate", "ple_w_proj"):
            for i in range(2):
                out[(n, i)] = a[i]
        else:
            out[(n,)] = a[0]
    return out


def _stacked(inst):
    out = {}
    for n in dict.fromkeys(k[0] for k in inst):
        if n.startswith("ffn_"):
            out[n] = jnp.stack([jnp.stack([inst[(n, i, j)] for j in range(2)]) for i in range(2)])
        elif n in ("mix_norm", "ple_norm", "ple_w_gate", "ple_w_proj"):
            out[n] = jnp.stack([inst[(n, i)] for i in range(2)])
        else:
            out[n] = inst[(n,)][None]
    return out


def _local_step(x, p, target, w, late_shards=(), late_weights=None, early_grads=None, first_shards=(), first_weights=None,
                last_grads=None):
    w = dict(w)
    ffn = lambda i, j: (w[("ffn_norm", i, j)], w[("ffn_w_gate", i, j)], w[("ffn_w_up", i, j)], w[("ffn_w_down", i, j)])
    h = x
    tape = []
    for i in range(2):
        if i == 0 and first_weights is not None:
            def wd_of(gathered):
                w.update(first_weights(gathered))
                return w[("ffn_w_down", 0, 0)]
            h, s1 = _ffn_fwd(h, w[("ffn_norm", 0, 0)], w[("ffn_w_gate", 0, 0)], w[("ffn_w_up", 0, 0)], None, "ffn0a",
                             first_shards, wd_of)
        else:
            h, s1 = _ffn_fwd(h, *ffn(i, 0), f"ffn{i}a")
        if i == 0:
            def w_out_of(gathered):
                if late_weights is not None:
                    w.update(late_weights(gathered))
                return w[("att_w_out",)]
            h, s2, _ = _att_fwd(h, w[("mix_norm", 0)], w[("att_w_in",)], w_out_of, w[("att_q_norm",)],
                                w[("att_k_norm",)], w[("att_sinks",)], late_shards)
        else:
            gdn_in_pad = _gdn_pad_in(w[("gdn_w_in",)])
            h, s2 = _gdn_mixer_fwd(h, w[("mix_norm", 1)], gdn_in_pad, w[("gdn_conv_w",)], w[("gdn_a_log",)],
                                   w[("gdn_dt_bias",)], w[("gdn_out_norm",)], w[("gdn_w_out",)])
        h, s3 = _ffn_fwd(h, *ffn(i, 1), f"ffn{i}b")
        h, s4 = _ple_fwd(h, p[i], w[("ple_norm", i)], w[("ple_w_gate", i)], w[("ple_w_proj", i)], f"ple{i}")
        tape.append((s1, s2, s3, s4))

    loss, dh = _loss_head(h, target)

    g = {}
    rode = []
    for i in (1, 0):
        s1, s2, s3, s4 = tape[i]
        dh, g[("ple_norm", i)], g[("ple_w_gate", i)], g[("ple_w_proj", i)] = _ple_bwd(
            dh, s4, p[i], w[("ple_norm", i)], w[("ple_w_gate", i)], f"ple{i}")
        dh, g[("ffn_norm", i, 1)], g[("ffn_w_gate", i, 1)], g[("ffn_w_up", i, 1)], g[("ffn_w_down", i, 1)], _ = _ffn_bwd(
            dh, s3, *ffn(i, 1), f"ffn{i}b")
        if i == 0:
            ride = early_grads(g) if early_grads is not None else ()
            (dh, g[("mix_norm", 0)], g[("att_w_in",)], g[("att_w_out",)], g[("att_q_norm",)], g[("att_k_norm",)],
             g[("att_sinks",)], rode) = _att_bwd(dh, s2, w[("mix_norm", 0)], w[("att_w_in",)], w[("att_w_out",)], ride)
        else:
            (dh, g[("mix_norm", 1)], g[("gdn_w_in",)], g[("gdn_conv_w",)], g[("gdn_a_log",)], g[("gdn_dt_bias",)],
             g[("gdn_out_norm",)], g[("gdn_w_out",)]) = _gdn_mixer_bwd(
                dh, s2, w[("mix_norm", 1)], gdn_in_pad, w[("gdn_conv_w",)], w[("gdn_w_out",)])
        last_of = None
        if i == 0 and last_grads is not None:
            def last_of(dwg, dwu, dwd):
                return last_grads({**g, ("ffn_w_gate", 0, 0): dwg, ("ffn_w_up", 0, 0): dwu, ("ffn_w_down", 0, 0): dwd})
        (dh, g[("ffn_norm", i, 0)], g[("ffn_w_gate", i, 0)], g[("ffn_w_up", i, 0)], g[("ffn_w_down", i, 0)],
         rode_last) = _ffn_bwd(dh, s1, *ffn(i, 0), f"ffn{i}a", last_of)
    return loss, dh, g, rode, rode_last


MESH = pl.DeviceIdType.MESH


def _place():
    x, y, c = lax.axis_index("x"), lax.axis_index("y"), lax.axis_index("c")
    others = [((1 - x, y), 2 * (1 - x) + y), ((x, 1 - y), 2 * x + (1 - y)), ((1 - x, 1 - y), 2 * (1 - x) + (1 - y))]
    return x, y, c, 4 * x + 2 * y + c, 2 * x + y, (x, y, 1 - c), others


def _comm_call(body, arrays, out_shape, n_sems, name):
    hbm = pl.BlockSpec(memory_space=pl.ANY)
    n = len(arrays)
    return _pcall(
        body, name=name, in_specs=[hbm] * n, out_specs=[hbm] * len(out_shape), out_shape=out_shape,
        scratch_shapes=[pltpu.SemaphoreType.DMA((n, n_sems)), pltpu.SemaphoreType.DMA((n, n_sems)),
                        pltpu.SemaphoreType.DMA((n, N_CHIP))],
        compiler_params=pltpu.CompilerParams(has_side_effects=True),
    )(*arrays)


def _gather_protocol(ins, outs, send_sems, recv_sems, local_sems):
    n = len(ins)
    x, y, c, me, my_chip, sibling, others = _place()

    def copy(a, k, block, to, src=None):
        dst = outs[a].at[block]
        return pltpu.make_async_remote_copy(
            src_ref=dst if src is None else src, dst_ref=dst, send_sem=send_sems.at[a, k],
            recv_sem=recv_sems.at[a, k], device_id=to, device_id_type=MESH)

    local = [pltpu.make_async_copy(ins[a], outs[a].at[me], local_sems.at[a, 0]) for a in range(n)]
    first = []
    for a in range(n):
        first.append(copy(a, 0, me, sibling, src=ins[a]))
        first += [copy(a, 1 + j, me, (*chip, c), src=ins[a]) for j, (chip, _) in enumerate(others)]

    def start():
        for cp in local + first:
            cp.start()

    def finish():
        passed = []
        for a in range(n):
            for j, (chip, chip_idx) in enumerate(others):
                blk = 2 * chip_idx + c
                copy(a, 1 + j, blk, (x, y, c)).wait_recv()
                fwd = copy(a, 4 + j, blk, sibling)
                fwd.start()
                passed.append(fwd)
        for a in range(n):
            copy(a, 0, 2 * my_chip + (1 - c), (x, y, c)).wait_recv()
            for j, (chip, chip_idx) in enumerate(others):
                copy(a, 4 + j, 2 * chip_idx + (1 - c), (x, y, c)).wait_recv()
        for cp in first + passed:
            cp.wait_send()
        for cp in local:
            cp.wait()

    return start, finish


def _all_gather(arrays):
    n = len(arrays)

    def body(*refs):
        start, finish = _gather_protocol(refs[:n], refs[n:2 * n], *refs[2 * n:])
        start()
        finish()

    out_shape = [jax.ShapeDtypeStruct((N_DEV,) + a.shape, a.dtype) for a in arrays]
    return _comm_call(body, arrays, out_shape, N_DEV - 1, "gather_weights")


def _exchange_sibling(arrays, name):
    n = len(arrays)

    def body(*refs):
        ins, got = refs[:n], refs[n:2 * n]
        send_sems, recv_sems, _ = refs[2 * n:]
        x, y, c, me, my_chip, sibling, others = _place()
        remote = []
        for a in range(n):
            for chip in range(N_CHIP):
                rc = pltpu.make_async_remote_copy(
                    src_ref=ins[a].at[2 * chip + (1 - c)], dst_ref=got[a].at[chip], send_sem=send_sems.at[a, chip],
                    recv_sem=recv_sems.at[a, chip], device_id=sibling, device_id_type=MESH)
                rc.start()
                remote.append(rc)
        for rc in remote:
            rc.wait()

    half = [jax.ShapeDtypeStruct((N_CHIP,) + a.shape[1:], a.dtype) for a in arrays]
    return _comm_call(body, arrays, half, N_CHIP, name)


def _chips_protocol(ins, outs, send_sems, recv_sems, local_sems):
    n = len(ins)
    x, y, c, me, my_chip, sibling, others = _place()
    local = [pltpu.make_async_copy(ins[a].at[my_chip], outs[a].at[my_chip], local_sems.at[a, 0]) for a in range(n)]
    remote = [pltpu.make_async_remote_copy(
        src_ref=ins[a].at[chip_idx], dst_ref=outs[a].at[my_chip], send_sem=send_sems.at[a, j],
        recv_sem=recv_sems.at[a, j], device_id=(*chip, c), device_id_type=MESH)
        for a in range(n) for j, (chip, chip_idx) in enumerate(others)]

    def start():
        for cp in local + remote:
            cp.start()

    def finish():
        for cp in remote + local:
            cp.wait()

    return start, finish


def _exchange_chips(arrays, name):
    n = len(arrays)

    def body(*refs):
        start, finish = _chips_protocol(refs[:n], refs[n:2 * n], *refs[2 * n:])
        start()
        finish()

    out_shape = [jax.ShapeDtypeStruct(a.shape, a.dtype) for a in arrays]
    return _comm_call(body, arrays, out_shape, N_CHIP - 1, name)


def _as_rows(a, lead):
    shp = a.shape
    return a.reshape(shp[:lead] + (math.prod(shp[lead:-1]), shp[-1]))


def _row_tile(rows, cap=512):
    if rows <= cap:
        return rows
    for t in range(cap - cap % 8, 0, -8):
        if rows % t == 0:
            return t
    return rows


def _pair_sum(send, got, name):
    a3, b3 = _as_rows(send, 1), _as_rows(got, 1)
    _, rows, last = b3.shape
    tr = _row_tile(rows, 2048)

    def body(c_ref, a_ref, b_ref, o_ref):
        o_ref[...] = (a_ref[...].astype(F32) + b_ref[...].astype(F32)).astype(o_ref.dtype)

    core = lax.axis_index("c").astype(jnp.int32).reshape(1)
    out = _pcall(
        body, name=name,
        grid_spec=pltpu.PrefetchScalarGridSpec(
            num_scalar_prefetch=1, grid=(N_CHIP, rows // tr),
            in_specs=[pl.BlockSpec((None, tr, last), lambda k, i, c_ref: (2 * k + c_ref[0], i, 0)),
                      pl.BlockSpec((None, tr, last), lambda k, i, c_ref: (k, i, 0))],
            out_specs=pl.BlockSpec((None, tr, last), lambda k, i, c_ref: (k, i, 0))),
        out_shape=jax.ShapeDtypeStruct(b3.shape, got.dtype), compiler_params=_params(("parallel", "parallel")),
    )(core, a3, b3)
    return out.reshape(got.shape)


def _adamw(parts, w, m, v, name):
    lead, (rows, last) = w.shape[:-2], w.shape[-2:]
    nl = len(lead)
    tr = _row_tile(rows, 1024)
    c1 = 1.0 / (1.0 - ADAM_B1 ** ADAM_STEP)
    c2 = 1.0 / (1.0 - ADAM_B2 ** ADAM_STEP)

    def body(p_ref, w_ref, m_ref, v_ref, g_ref, d_ref, nm_ref, nv_ref):
        g = p_ref[0].astype(F32)
        for chip in range(1, N_CHIP):
            g = g + p_ref[chip].astype(F32)
        mn = ADAM_B1 * m_ref[...] + (1.0 - ADAM_B1) * g
        vn = ADAM_B2 * v_ref[...] + (1.0 - ADAM_B2) * (g * g)
        g_ref[...] = g
        nm_ref[...] = mn
        nv_ref[...] = vn
        d_ref[...] = -ADAM_LR * ((mn * c1) / (jnp.sqrt(vn * c2) + ADAM_EPS) + ADAM_WD * w_ref[...])

    row = pl.BlockSpec((None,) * nl + (tr, last), lambda *ix: ix + (0,))
    part = pl.BlockSpec((N_CHIP,) + (None,) * nl + (tr, last), lambda *ix: (0,) + ix + (0,))
    sh = jax.ShapeDtypeStruct(w.shape, F32)
    return _pcall(body, name=name, grid=lead + (rows // tr,), in_specs=[part, row, row, row],
                  out_specs=[row, row, row, row], out_shape=[sh, sh, sh, sh],
                  compiler_params=_params(("parallel",) * (nl + 1)))(parts, w, m, v)


def _pack(pieces, row_align):
    rows, offs, r = [], [], 0
    for a in pieces:
        flat = a.reshape(-1)
        nr = -(-flat.shape[0] // PACK_W)
        flat = jnp.pad(flat, (0, nr * PACK_W - flat.shape[0]))
        rows.append(flat.reshape(nr, PACK_W))
        offs.append(r)
        r += nr
    pad = (-r) % row_align
    if pad:
        rows.append(jnp.zeros((pad, PACK_W), pieces[0].dtype))
    return jnp.concatenate(rows, axis=0), offs


def _unpack(flat, offs, shapes):
    out = []
    for off, shp in zip(offs, shapes):
        size = math.prod(shp)
        nr = -(-size // PACK_W)
        out.append(flat[..., off:off + nr, :].reshape(flat.shape[:-2] + (nr * PACK_W,))[..., :size].reshape(flat.shape[:-2] + tuple(shp)))
    return out


def _to_full(gathered, axis):
    z = jnp.moveaxis(gathered, 0, axis)
    shp = list(z.shape)
    return z.reshape(shp[:axis] + [shp[axis] * shp[axis + 1]] + shp[axis + 2:])


def _to_shards(full, axis):
    shp = list(full.shape)
    z = full.reshape(shp[:axis] + [N_DEV, shp[axis] // N_DEV] + shp[axis + 1:])
    return jnp.moveaxis(z, axis, 0)


def kernel(x, p, ffn_norm, ffn_w_gate, ffn_w_up, ffn_w_down, mix_norm, att_w_in, att_q_norm, att_k_norm, att_sinks, att_w_out, gdn_w_in, gdn_conv_w, gdn_a_log, gdn_dt_bias, gdn_out_norm, gdn_w_out, ple_norm, ple_w_gate, ple_w_proj, loss_target, m_ffn_norm, m_ffn_w_gate, m_ffn_w_up, m_ffn_w_down, m_mix_norm, m_att_w_in, m_att_q_norm, m_att_k_norm, m_att_sinks, m_att_w_out, m_gdn_w_in, m_gdn_conv_w, m_gdn_a_log, m_gdn_dt_bias, m_gdn_out_norm, m_gdn_w_out, m_ple_norm, m_ple_w_gate, m_ple_w_proj, v_ffn_norm, v_ffn_w_gate, v_ffn_w_up, v_ffn_w_down, v_mix_norm, v_att_w_in, v_att_q_norm, v_att_k_norm, v_att_sinks, v_att_w_out, v_gdn_w_in, v_gdn_conv_w, v_gdn_a_log, v_gdn_dt_bias, v_gdn_out_norm, v_gdn_w_out, v_ple_norm, v_ple_w_gate, v_ple_w_proj):
    args = dict(locals())
    wts = {n: args[n] for n in WEIGHTS}
    mom = {n: args["m_" + n] for n in WEIGHTS}
    var = {n: args["v_" + n] for n in WEIGHTS}
    axis = dict(SHARDED)
    vecs = [n for n, _ in SHARDED[:SMALL_SHARDED]]
    small = vecs + list(REPLICATED)
    small_shapes = [wts[n].shape for n in small]
    lead = lambda n: 2 if n.startswith("ffn_") else 1

    def stack_of(arrays, name, idxs):
        return jnp.stack([arrays[name][idx] if idx else arrays[name][0] for idx in idxs])

    def full_instances(gathered, group):
        out = {}
        for (name, idxs), g in zip(group, gathered):
            whole = _to_full(g, axis[name] - lead(name) + 1)
            for k, idx in enumerate(idxs):
                out[(name,) + idx] = whole[k]
        return out

    def shard_stacks(g, group):
        return [_to_shards(jnp.stack([g[(name,) + idx] for idx in idxs]), axis[name] - lead(name) + 1)
                for name, idxs in group]

    vec_pack, voffs = _pack([wts[n] for n in vecs], 8)
    early = _all_gather([stack_of(wts, n, idxs).astype(BF16) for n, idxs in EARLY] + [vec_pack])
    w = full_instances(early[:-1], EARLY)
    vec_full = {n: _to_full(piece, axis[n]) for n, piece in
                zip(vecs, _unpack(early[-1], voffs, [wts[n].shape for n in vecs]))}
    w.update(_instances({**vec_full, **{n: wts[n] for n in REPLICATED}}))
    first_shards = [stack_of(wts, n, idxs).astype(BF16) for n, idxs in FIRST]
    late_shards = [stack_of(wts, n, idxs).astype(BF16) for n, idxs in LATE]

    def chip_sums(send, tag):
        got = _exchange_sibling(send, f"exchange_sibling_{tag}")
        return [_pair_sum(p_, q_, f"pair_sum_{tag}_{i}") for i, (p_, q_) in enumerate(zip(send, got))]

    loss, grad_x, g, rode, rode_last = _local_step(
        x[0], p[:, 0], loss_target[0], w, late_shards, lambda gathered: full_instances(gathered, LATE),
        lambda g: chip_sums(shard_stacks(g, RIDE), "early"),
        first_shards, lambda gathered: full_instances(gathered, FIRST),
        lambda g: chip_sums(shard_stacks(g, FINAL), "final"))

    gs = _stacked({k: v for k, v in g.items() if k[0] in small})
    vec_shards = [_to_shards(gs[n], axis[n]) for n in vecs]
    small_send = jnp.stack([_pack([sh[d] for sh in vec_shards] + [gs[n] for n in REPLICATED] + [loss.reshape(1)], 8)[0]
                            for d in range(N_DEV)])
    last = list(rode_last) + list(_exchange_chips(chip_sums([small_send], "small"), "exchange_chips_small"))

    pieces = {}
    for (name, idxs), part in list(zip(RIDE, rode)) + list(zip(FINAL, last[:-1])):
        for k, idx in enumerate(idxs):
            pieces[(name,) + idx] = part[:, k]
    outs = {}
    for n, _ in SHARDED[SMALL_SHARDED:]:
        if lead(n) == 2:
            part = jnp.stack([jnp.stack([pieces[(n, i, j)] for j in range(2)], axis=1) for i in range(2)], axis=1)
        elif (n, 0) in pieces:
            part = jnp.stack([pieces[(n, i)] for i in range(2)], axis=1)
        else:
            part = pieces[(n,)][:, None]
        outs[n] = _adamw(part, wts[n], mom[n], var[n], f"adamw_{n}")
    filler = [jnp.zeros((1,), F32)]
    small_w, soffs = _pack([wts[n] for n in small] + filler, 8)
    small_m, _ = _pack([mom[n] for n in small] + filler, 8)
    small_v, _ = _pack([var[n] for n in small] + filler, 8)
    small_out = [_unpack(z, soffs, small_shapes + [(1,)]) for z in _adamw(last[-1], small_w, small_m, small_v, "adamw_small")]
    loss = small_out[0][-1][0]
    for i, n in enumerate(small):
        outs[n] = [small_out[k][i] for k in range(4)]
    result = [loss, grad_x[None]]
    for k in range(4):
        result += [outs[n][k] for n in WEIGHTS]
    return tuple(result)
```

```python
import math

import jax
import jax.numpy as jnp
from jax import lax
from jax.experimental import pallas as pl
from jax.experimental.pallas import tpu as pltpu

F32 = jnp.float32
BF16 = jnp.bfloat16

N_DEV = 8
N_CHIP = 4
D_MODEL = 1024
D_FF = 2816
PLE_DIM = 256
HEAD_DIM = 64
SB_HEADS = 8
SWA_HEADS = 8
SWA_KV_HEADS = 2
SWA_GROUP = SWA_HEADS // SWA_KV_HEADS
WINDOW = 128
Q_BLOCK = 128
GDN_K_HEADS = 8
GDN_V_HEADS = 16
GDN_HEAD_DIM = 128
GDN_CONV = 4
GDN_CHUNK = 64
EPS = 1e-6
SB_W = SB_HEADS * HEAD_DIM
SWA_QW = SWA_HEADS * HEAD_DIM
SWA_KVW = SWA_KV_HEADS * HEAD_DIM
ATT_IN = 3 * SB_W + SWA_QW + 2 * SWA_KVW
GDN_KW = GDN_K_HEADS * GDN_HEAD_DIM
GDN_VW = GDN_V_HEADS * GDN_HEAD_DIM
GDN_CONV_W = 2 * GDN_KW + GDN_VW
GDN_IN = GDN_CONV_W + GDN_VW + 2 * GDN_V_HEADS
GDN_IN_PAD = GDN_CONV_W + GDN_VW + 2 * 128

ADAM_LR = 0.001
ADAM_B1 = 0.9
ADAM_B2 = 0.999
ADAM_EPS = 1e-08
ADAM_WD = 0.01
ADAM_STEP = 10

LANE = 128
VMEM_LIMIT = 56 * 1024 * 1024
MM_TILE_BUDGET = 40 * 1024 * 1024
PACK_W = 1024

NN = ((1,), (0,))
NT = ((1,), (1,))
TN = ((0,), (0,))

SHARDED = (
    ("ffn_norm", 2), ("gdn_conv_w", 2),
    ("ffn_w_gate", 3), ("ffn_w_up", 3), ("ffn_w_down", 2), ("att_w_in", 2), ("att_w_out", 1),
    ("gdn_w_in", 2), ("gdn_w_out", 1), ("ple_w_gate", 1), ("ple_w_proj", 2),
)
SMALL_SHARDED = 2
REPLICATED = ("mix_norm", "att_q_norm", "att_k_norm", "att_sinks", "gdn_a_log", "gdn_dt_bias",
              "gdn_out_norm", "ple_norm")
WEIGHTS = ("ffn_norm", "ffn_w_gate", "ffn_w_up", "ffn_w_down", "mix_norm", "att_w_in", "att_q_norm",
           "att_k_norm", "att_sinks", "att_w_out", "gdn_w_in", "gdn_conv_w", "gdn_a_log", "gdn_dt_bias",
           "gdn_out_norm", "gdn_w_out", "ple_norm", "ple_w_gate", "ple_w_proj")


_FFN_REST = [(0, 1), (1, 0), (1, 1)]
EARLY = [("ffn_w_gate", [(0, 0)]), ("ffn_w_up", [(0, 0)])]
FIRST = [("ffn_w_down", [(0, 0)]), ("att_w_in", [()])]
LATE = ([(n, [idx]) for n in ("ffn_w_gate", "ffn_w_up", "ffn_w_down") for idx in _FFN_REST]
        + [("att_w_out", [()]), ("gdn_w_in", [()]), ("gdn_w_out", [()]),
           ("ple_w_gate", [(0,), (1,)]), ("ple_w_proj", [(0,), (1,)])])
RIDE = [e for e in LATE if e[0] != "att_w_out"]
FINAL = EARLY + FIRST + [("att_w_out", [()])]


def _pcall(body, **kw):
    return pl.pallas_call(body, **kw)


def _params(sem=None):
    if sem is None:
        return pltpu.CompilerParams(vmem_limit_bytes=VMEM_LIMIT)
    return pltpu.CompilerParams(dimension_semantics=sem, vmem_limit_bytes=VMEM_LIMIT)


def _ride_specs(ride, out_shapes, n_sems):
    hbm = pl.BlockSpec(memory_space=pl.ANY)
    n = len(ride)
    sems = [pltpu.SemaphoreType.DMA((n, n_sems)), pltpu.SemaphoreType.DMA((n, n_sems)),
            pltpu.SemaphoreType.DMA((n, N_CHIP))] if n else []
    return [hbm] * n, [hbm] * len(out_shapes), sems


def _dot(a, b, dims=NN):
    return lax.dot_general(a, b, (dims, ((), ())), preferred_element_type=F32)


def _bdot(a, b, dims=NN):
    return _dot(a.astype(BF16), b.astype(BF16), dims)


def _split(a):
    hi = a.astype(BF16)
    lo = (a - hi.astype(F32)).astype(BF16)
    return hi, lo


def _dot3(a, b, dims=NN):
    ah, al = _split(a)
    bh, bl = _split(b)
    return _dot(ah, bh, dims) + (_dot(ah, bl, dims) + _dot(al, bh, dims))


def _dot2m(a, m, dims=NN):
    ah, al = _split(a)
    return _dot(ah, m, dims) + _dot(al, m, dims)


def _mdot2(m, a, dims=NN):
    ah, al = _split(a)
    return _dot(m, ah, dims) + _dot(m, al, dims)


def _sigmoid(x):
    return 1.0 / (1.0 + jnp.exp(-x))


def _softplus(x):
    return jnp.maximum(x, 0.0) + jnp.log(1.0 + jnp.exp(-jnp.abs(x)))


def _pick(n, cap):
    if n <= cap:
        return n
    for t in range(cap - cap % LANE, 0, -LANE):
        if n % t == 0:
            return t
    raise ValueError(f"no tile for {n} under {cap}")


def _iota2(shape, axis):
    return lax.broadcasted_iota(jnp.int32, shape, axis)


def _mm(a, b, mode, out_dtype=F32, res=None, alpha=1.0, a2=None, b2=None, name="mm", ride=()):
    if mode == "nn":
        (M, K), N = a.shape, b.shape[1]
    elif mode == "nt":
        (M, K), N = a.shape, b.shape[0]
    else:
        (K, M), N = a.shape, b.shape[1]
    tn, tk = _pick(N, 1408), _pick(K, 2048 if mode == "tn" else 1408)
    nk = K // tk
    pairs = 1 if a2 is None else 2

    def tile_bytes(tm):
        per = pairs * tk * (tm * a.dtype.itemsize + tn * b.dtype.itemsize) + tm * tn * jnp.dtype(out_dtype).itemsize
        return 2 * (per + (tm * tn * 4 if res is not None else 0)) + (tm * tn * 4 if nk > 1 else 0)

    tm = next(t for t in (_pick(M, c) for c in ((1408,) if mode == "tn" else (2048, 1024, 512))) if tile_bytes(t) <= MM_TILE_BUDGET or t <= 512)
    dims = {"nn": NN, "nt": NT, "tn": TN}[mode]
    a_spec = pl.BlockSpec((tk, tm), lambda i, j, k: (k, i)) if mode == "tn" else pl.BlockSpec((tm, tk), lambda i, j, k: (i, k))
    b_spec = pl.BlockSpec((tn, tk), lambda i, j, k: (j, k)) if mode == "nt" else pl.BlockSpec((tk, tn), lambda i, j, k: (k, j))
    o_spec = pl.BlockSpec((tm, tn), lambda i, j, k: (i, j))
    two = a2 is not None
    has_res = res is not None
    grid = (M // tm, N // tn, nk)
    nr = len(ride)
    ride_out = [jax.ShapeDtypeStruct(r.shape, r.dtype) for r in ride]
    ride_in_specs, ride_out_specs, ride_sems = _ride_specs(ride, ride_out, N_CHIP - 1)
    a2_spec, b2_spec = a_spec, b_spec
    if two and a2.shape != a.shape:
        assert nk == 1 and mode == "nn" and a2.shape[0] == M and b2.shape[1] == N
        a2_spec = pl.BlockSpec((tm, a2.shape[1]), lambda i, j, k: (i, 0))
        b2_spec = pl.BlockSpec((a2.shape[1], tn), lambda i, j, k: (0, j))

    def body(*refs):
        refs = list(refs)
        a_ref, b_ref = refs[0], refs[1]
        pos = 2
        if two:
            a2_ref, b2_ref = refs[2], refs[3]
            pos = 4
        if has_res:
            res_ref = refs[pos]
            pos += 1
        rin = refs[pos:pos + nr]
        o_ref = refs[pos + nr]
        rout = refs[pos + nr + 1:pos + 2 * nr + 1]
        acc_ref = refs[pos + 2 * nr + 1]
        k = pl.program_id(2)
        if nr:
            i, j = pl.program_id(0), pl.program_id(1)
            start, done = _chips_protocol(rin, rout, *refs[pos + 2 * nr + 2:])
            pl.when((i == 0) & (j == 0) & (k == 0))(start)
        part = _bdot(a_ref[...], b_ref[...], dims)
        if two:
            part = part + _bdot(a2_ref[...], b2_ref[...], dims)

        def finish(acc):
            out = acc * alpha if alpha != 1.0 else acc
            if has_res:
                out = res_ref[...] + out
            o_ref[...] = out.astype(out_dtype)

        if nk == 1:
            finish(part)
        else:
            @pl.when(k == 0)
            def _():
                acc_ref[...] = part

            @pl.when(k > 0)
            def _():
                acc_ref[...] += part

            @pl.when(k == nk - 1)
            def _():
                finish(acc_ref[...])

        if nr:
            pl.when((i == grid[0] - 1) & (j == grid[1] - 1) & (k == nk - 1))(done)

    ins = [a, b]
    specs = [a_spec, b_spec]
    if two:
        ins += [a2, b2]
        specs += [a2_spec, b2_spec]
    if has_res:
        ins.append(res)
        specs.append(o_spec)
    out = _pcall(
        body, name=name, grid=grid, in_specs=specs + ride_in_specs, out_specs=[o_spec] + ride_out_specs,
        out_shape=[jax.ShapeDtypeStruct((M, N), out_dtype)] + ride_out,
        scratch_shapes=[pltpu.VMEM((tm, tn) if nk > 1 else (8, LANE), F32)] + ride_sems,
        compiler_params=_params(("arbitrary",) * 3 if nr else ("parallel", "parallel", "arbitrary")),
    )(*ins, *ride)
    return (out[0], list(out[1:])) if nr else out[0]


ROW_TILE = 1024


def _rms_fwd(h, g, name):
    T, D = h.shape
    tr = _pick(T, ROW_TILE)

    def body(h_ref, g_ref, n_ref):
        x = h_ref[...]
        r = lax.rsqrt(jnp.mean(x * x, axis=-1, keepdims=True) + EPS)
        n_ref[...] = (x * r * g_ref[...]).astype(BF16)

    return _pcall(
        body, name=name, grid=(T // tr,),
        in_specs=[pl.BlockSpec((tr, D), lambda i: (i, 0)), pl.BlockSpec((1, D), lambda i: (0, 0))],
        out_specs=pl.BlockSpec((tr, D), lambda i: (i, 0)),
        out_shape=jax.ShapeDtypeStruct((T, D), BF16), compiler_params=_params(("parallel",)),
    )(h, g.reshape(1, D))


def _rms_bwd(dn, h, g, dres, name):
    T, D = h.shape
    tr = _pick(T, ROW_TILE)

    def body(dn_ref, h_ref, g_ref, dres_ref, dh_ref, dg_ref):
        x = h_ref[...]
        r = lax.rsqrt(jnp.mean(x * x, axis=-1, keepdims=True) + EPS)
        xh = x * r
        d = dn_ref[...].astype(F32)
        dxh = d * g_ref[...]
        dh_ref[...] = dres_ref[...] + r * (dxh - xh * jnp.mean(dxh * xh, axis=-1, keepdims=True))
        part = jnp.sum(d * xh, axis=0, keepdims=True)

        @pl.when(pl.program_id(0) == 0)
        def _():
            dg_ref[...] = part

        @pl.when(pl.program_id(0) > 0)
        def _():
            dg_ref[...] += part

    row = pl.BlockSpec((tr, D), lambda i: (i, 0))
    vec = pl.BlockSpec((1, D), lambda i: (0, 0))
    dh, dg = _pcall(
        body, name=name, grid=(T // tr,), in_specs=[row, row, vec, row], out_specs=[row, vec],
        out_shape=[jax.ShapeDtypeStruct((T, D), F32), jax.ShapeDtypeStruct((1, D), F32)],
        compiler_params=_params(("arbitrary",)),
    )(dn, h, g.reshape(1, D), dres)
    return dh, dg.reshape(D)


def _gateup(n, wg, wu, name, ride=()):
    T, D = n.shape
    F = wg.shape[1]
    tm, tn = _pick(T, 1024), _pick(F, 1408)
    nr = len(ride)
    ride_out = [jax.ShapeDtypeStruct((N_DEV,) + r.shape, r.dtype) for r in ride]
    ride_in_specs, ride_out_specs, ride_sems = _ride_specs(ride, ride_out, N_DEV - 1)
    grid = (T // tm, F // tn)

    def body(*refs):
        n_ref, wg_ref, wu_ref = refs[:3]
        a_ref, b_ref, hid_ref = refs[3 + nr:6 + nr]
        if nr:
            i, j = pl.program_id(0), pl.program_id(1)
            start, finish = _gather_protocol(refs[3:3 + nr], refs[6 + nr:6 + 2 * nr], *refs[6 + 2 * nr:])
            pl.when((i == 0) & (j == 0))(start)
        x = n_ref[...]
        a = _dot(x, wg_ref[...])
        b = _dot(x, wu_ref[...])
        a_ref[...] = a.astype(BF16)
        b_ref[...] = b.astype(BF16)
        hid_ref[...] = (a * _sigmoid(a) * b).astype(BF16)
        if nr:
            pl.when((i == grid[0] - 1) & (j == grid[1] - 1))(finish)

    o_spec = pl.BlockSpec((tm, tn), lambda i, j: (i, j))
    w_spec = pl.BlockSpec((D, tn), lambda i, j: (0, j))
    sh = jax.ShapeDtypeStruct((T, F), BF16)
    res = _pcall(
        body, name=name, grid=grid,
        in_specs=[pl.BlockSpec((tm, D), lambda i, j: (i, 0)), w_spec, w_spec] + ride_in_specs,
        out_specs=[o_spec, o_spec, o_spec] + ride_out_specs, out_shape=[sh, sh, sh] + ride_out,
        scratch_shapes=ride_sems,
        compiler_params=_params(("arbitrary", "arbitrary") if nr else ("parallel", "parallel")),
    )(n, wg, wu, *ride)
    return res[0], res[1], res[2], list(res[3:])


def _ffn_dhid(dy, wd, a, b, name):
    T, D = dy.shape
    F = wd.shape[0]
    tm, tn = _pick(T, 1024), _pick(F, 1408)

    def body(dy_ref, wd_ref, a_ref, b_ref, da_ref, db_ref):
        dhid = 0.5 * _bdot(dy_ref[...], wd_ref[...], NT)
        av = a_ref[...].astype(F32)
        bv = b_ref[...].astype(F32)
        s = _sigmoid(av)
        da_ref[...] = (dhid * bv * s * (1.0 + av * (1.0 - s))).astype(BF16)
        db_ref[...] = (dhid * av * s).astype(BF16)

    o_spec = pl.BlockSpec((tm, tn), lambda i, j: (i, j))
    sh = jax.ShapeDtypeStruct((T, F), BF16)
    return _pcall(
        body, name=name, grid=(T // tm, F // tn),
        in_specs=[pl.BlockSpec((tm, D), lambda i, j: (i, 0)), pl.BlockSpec((tn, D), lambda i, j: (j, 0)), o_spec, o_spec],
        out_specs=[o_spec, o_spec], out_shape=[sh, sh],
        compiler_params=_params(("parallel", "parallel")),
    )(dy, wd, a, b)


def _ffn_fwd(h, g, wg, wu, wd, tag, ride=(), wd_of=None):
    n = _rms_fwd(h, g, f"{tag}_norm")
    a, b, hid, gathered = _gateup(n, wg, wu, f"{tag}_gateup", ride)
    if wd_of is not None:
        wd = wd_of(gathered)
    h2 = _mm(hid, wd, "nn", res=h, alpha=0.5, name=f"{tag}_down")
    return h2, (h, n, a, b, hid)


def _ffn_bwd(dh2, saved, g, wg, wu, wd, tag, ride_of=None):
    h, n, a, b, hid = saved
    da, db = _ffn_dhid(dh2, wd, a, b, f"{tag}_dhid")
    dwd = _mm(hid, dh2, "tn", alpha=0.5, out_dtype=BF16, name=f"{tag}_dwd")
    dwg = _mm(n, da, "tn", out_dtype=BF16, name=f"{tag}_dwg")
    dwu = _mm(n, db, "tn", out_dtype=BF16, name=f"{tag}_dwu")
    rode = []
    if ride_of is None:
        dn = _mm(da, wg, "nt", a2=db, b2=wu, name=f"{tag}_dn")
    else:
        dn, rode = _mm(da, wg, "nt", a2=db, b2=wu, name=f"{tag}_dn", ride=ride_of(dwg, dwu, dwd))
    dh, dg = _rms_bwd(dn, h, g, dh2, f"{tag}_dnorm")
    return dh, dg, dwg, dwu, dwd, rode


def _ple_fwd(h, p, g, w_gate, w_proj, tag):
    T, D = h.shape
    pn = _rms_fwd(h, g, f"{tag}_norm")
    tm, tn = _pick(T, 1024), _pick(D, 1024)
    P = p.shape[1]

    def body(pn_ref, p_ref, wg_ref, wp_ref, h_ref, o_ref, gl_ref, pp_ref):
        gl = _dot(pn_ref[...], wg_ref[...])
        pp = _bdot(p_ref[...], wp_ref[...])
        gl_ref[...] = gl
        pp_ref[...] = pp
        o_ref[...] = h_ref[...] + _sigmoid(gl) * pp

    o_spec = pl.BlockSpec((tm, tn), lambda i, j: (i, j))
    sh = jax.ShapeDtypeStruct((T, D), F32)
    h2, gl, pp = _pcall(
        body, name=f"{tag}_fwd", grid=(T // tm, D // tn),
        in_specs=[pl.BlockSpec((tm, D), lambda i, j: (i, 0)), pl.BlockSpec((tm, P), lambda i, j: (i, 0)),
                  pl.BlockSpec((D, tn), lambda i, j: (0, j)), pl.BlockSpec((P, tn), lambda i, j: (0, j)), o_spec],
        out_specs=[o_spec, o_spec, o_spec], out_shape=[sh, sh, sh],
        compiler_params=_params(("parallel", "parallel")),
    )(pn, p, w_gate, w_proj, h)
    return h2, (h, pn, gl, pp)


def _ple_bwd(dh2, saved, p, g, w_gate, tag):
    h, pn, gl, pp = saved
    T, D = h.shape
    tr = _pick(T, ROW_TILE)

    def body(d_ref, gl_ref, pp_ref, dgl_ref, dpp_ref):
        d = d_ref[...]
        s = _sigmoid(gl_ref[...])
        dpp_ref[...] = (d * s).astype(BF16)
        dgl_ref[...] = (d * pp_ref[...] * s * (1.0 - s)).astype(BF16)

    row = pl.BlockSpec((tr, D), lambda i: (i, 0))
    sh = jax.ShapeDtypeStruct((T, D), BF16)
    dgl, dpp = _pcall(body, name=f"{tag}_dgate", grid=(T // tr,), in_specs=[row, row, row], out_specs=[row, row],
                      out_shape=[sh, sh], compiler_params=_params(("parallel",)))(dh2, gl, pp)
    dw_proj = _mm(p, dpp, "tn", out_dtype=BF16, name=f"{tag}_dwproj")
    dw_gate = _mm(pn, dgl, "tn", out_dtype=BF16, name=f"{tag}_dwgate")
    dpn = _mm(dgl, w_gate, "nt", name=f"{tag}_dpn")
    dh, dg = _rms_bwd(dpn, h, g, dh2, f"{tag}_dnorm")
    return dh, dg, dw_gate, dw_proj


def _loss_head(y, target):
    T, D = y.shape
    tr = _pick(T, ROW_TILE)

    def body(y_ref, t_ref, dy_ref, l_ref):
        e = y_ref[...] - t_ref[...]
        dy_ref[...] = e * (1.0 / D)
        part = jnp.sum(e * e, axis=0, keepdims=True)

        @pl.when(pl.program_id(0) == 0)
        def _():
            l_ref[...] = part

        @pl.when(pl.program_id(0) > 0)
        def _():
            l_ref[...] += part

    row = pl.BlockSpec((tr, D), lambda i: (i, 0))
    vec = pl.BlockSpec((1, D), lambda i: (0, 0))
    dy, l = _pcall(body, name="loss_head", grid=(T // tr,), in_specs=[row, row], out_specs=[row, vec],
                   out_shape=[jax.ShapeDtypeStruct((T, D), F32), jax.ShapeDtypeStruct((1, D), F32)],
                   compiler_params=_params(("arbitrary",)))(y, target)
    return (0.5 / D) * jnp.sum(l), dy


SB_LANES = SB_HEADS * 2 * HEAD_DIM


def _sb_consts():
    row = _iota2((Q_BLOCK, Q_BLOCK), 0)
    col = _iota2((Q_BLOCK, Q_BLOCK), 1)
    after = (row > col).astype(BF16)
    before = (row < col).astype(BF16)
    return col < row, after, before, col


def _sb_fwd(proj, ride=()):
    T = proj.shape[0]
    H, d, L = SB_HEADS, HEAD_DIM, 2 * HEAD_DIM
    nblk = T // Q_BLOCK
    scale = d ** -0.5
    n = len(ride)
    ride_out = [jax.ShapeDtypeStruct((N_DEV,) + a.shape, a.dtype) for a in ride]
    ride_in_specs, ride_out_specs, ride_sems = _ride_specs(ride, ride_out, N_DEV - 1)
    R = range(H)
    tile = lambda g: slice(g * L, (g + 1) * L)

    def body(*refs):
        q_ref, kv_ref = refs[:2]
        rin = refs[2:2 + n]
        o_ref, c_ref = refs[2 + n:4 + n]
        rout = refs[4 + n:4 + 2 * n]
        run_ref = refs[4 + 2 * n]
        i = pl.program_id(0)
        if n:
            start, finish = _gather_protocol(rin, rout, *refs[5 + 2 * n:])
            pl.when(i == 0)(start)
        causal, after, _, col = _sb_consts()
        qs = [q_ref[:, tile(g)] * scale for g in R]
        o_ref[...] = jnp.zeros_like(o_ref)
        c_ref[...] = jnp.zeros_like(c_ref)
        run_ref[...] = jnp.zeros_like(run_ref)

        def pair(j, diag):
            rows = pl.ds(pl.multiple_of(j * Q_BLOCK, Q_BLOCK), Q_BLOCK)
            kvj = [kv_ref[rows, tile(g)] for g in R]
            c = [run_ref[g] for g in R]
            acc = [o_ref[:, tile(g)] for g in R]
            cm = None if diag else [c_ref[:, tile(g)] for g in R]
            z = [_dot(qs[g], kvj[g], NT) for g in R]
            sp = [_softplus(z[g]) for g in R]
            lk = [jnp.where(causal, -sp[g], 0.0) if diag else -sp[g] for g in R]
            btw = [_dot2m(lk[g], after) for g in R]
            e = [jnp.exp((z[g] - sp[g]) + btw[g] + c[g]) for g in R]
            w = [jnp.where(causal, e[g], 0.0) if diag else e[g] for g in R]
            pv = [_bdot(w[g], kvj[g]) for g in R]
            rs = [jnp.sum(lk[g], axis=1, keepdims=True) for g in R]
            for g in R:
                o_ref[:, tile(g)] = acc[g] + pv[g]
                if not diag:
                    c_ref[:, tile(g)] = jnp.where(col == j, c[g], cm[g])
                run_ref[g] = c[g] + rs[g]

        pair(i, True)

        @pl.loop(0, i)
        def _(jj):
            pair(i - 1 - jj, False)

        if n:
            pl.when(i == nblk - 1)(finish)

    blk = pl.BlockSpec((Q_BLOCK, H * L), lambda i: (i, 0))
    full = pl.BlockSpec((T, H * L), lambda i: (0, 1))
    res = _pcall(
        body, name="sb_fwd", grid=(nblk,), in_specs=[blk, full] + ride_in_specs,
        out_specs=[blk, blk] + ride_out_specs,
        out_shape=[jax.ShapeDtypeStruct((T, H * L), F32), jax.ShapeDtypeStruct((T, H * L), F32)] + ride_out,
        scratch_shapes=[pltpu.VMEM((H, Q_BLOCK, 1), F32)] + ride_sems,
        compiler_params=_params(("arbitrary",)),
    )(proj, proj, *ride)
    return res[0], res[1], list(res[2:])


def _sb_bwd(proj, carry, do, ride=()):
    T = proj.shape[0]
    H, d, L = SB_HEADS, HEAD_DIM, 2 * HEAD_DIM
    nblk = T // Q_BLOCK
    scale = d ** -0.5
    n = len(ride)
    ride_out = [jax.ShapeDtypeStruct(a.shape, a.dtype) for a in ride]
    ride_in_specs, ride_out_specs, ride_sems = _ride_specs(ride, ride_out, N_CHIP - 1)
    R = range(H)
    tile = lambda g: slice(g * L, (g + 1) * L)

    def body(*refs):
        q_ref, kv_ref, c_ref, do_ref = refs[:4]
        rin = refs[4:4 + n]
        dq_ref, dkv_ref = refs[4 + n:6 + n]
        rout = refs[6 + n:6 + 2 * n]
        run_ref = refs[6 + 2 * n]
        i = pl.program_id(0)
        if n:
            start, finish = _chips_protocol(rin, rout, *refs[7 + 2 * n:])
            pl.when(i == 0)(start)

        @pl.when(i == 0)
        def _():
            dkv_ref[...] = jnp.zeros_like(dkv_ref)

        causal, after, before, col = _sb_consts()
        qs = [q_ref[:, tile(g)] * scale for g in R]
        dov = [do_ref[:, tile(g)] for g in R]
        qdo = [jnp.concatenate([qs[g], dov[g]], axis=0) for g in R]
        dq_ref[...] = jnp.zeros_like(dq_ref)
        run_ref[...] = jnp.zeros_like(run_ref)

        def pair(j, diag):
            rows = pl.ds(pl.multiple_of(j * Q_BLOCK, Q_BLOCK), Q_BLOCK)
            kvj = [kv_ref[rows, tile(g)] for g in R]
            gsum = [run_ref[g] for g in R]
            dq0 = [dq_ref[:, tile(g)] for g in R]
            dkv0 = [dkv_ref[rows, tile(g)] for g in R]
            cm = None if diag else [c_ref[:, tile(g)] for g in R]
            z = [_dot(qs[g], kvj[g], NT) for g in R]
            sp = [_softplus(z[g]) for g in R]
            lk = [jnp.where(causal, -sp[g], 0.0) if diag else -sp[g] for g in R]
            ls = [z[g] - sp[g] for g in R]
            logw = [ls[g] + _dot2m(lk[g], after) for g in R]
            if not diag:
                logw = [logw[g] + jnp.sum(jnp.where(col == j, cm[g], 0.0), axis=1, keepdims=True) for g in R]
            e = [jnp.exp(logw[g]) for g in R]
            w = [jnp.where(causal, e[g], 0.0) if diag else e[g] for g in R]
            gw = [_dot(dov[g], kvj[g], NT) * w[g] for g in R]
            gpre = [gsum[g] + _dot(gw[g].astype(BF16), before) for g in R]
            sig = [jnp.exp(ls[g]) for g in R]
            dz = [gw[g] * (1.0 - sig[g]) - sig[g] * gpre[g] for g in R]
            if diag:
                dz = [jnp.where(causal, dz[g], 0.0) for g in R]
            dzb = [dz[g].astype(BF16) for g in R]
            dq1 = [_dot(dzb[g], kvj[g]) for g in R]
            dkv1 = [_dot(jnp.concatenate([dzb[g], w[g].astype(BF16)], axis=0), qdo[g], TN) for g in R]
            gs1 = [jnp.sum(gw[g], axis=1, keepdims=True) for g in R]
            for g in R:
                dq_ref[:, tile(g)] = dq0[g] + dq1[g]
                dkv_ref[rows, tile(g)] = dkv0[g] + dkv1[g]
                run_ref[g] = gsum[g] + gs1[g]

        @pl.loop(0, i)
        def _(j):
            pair(j, False)

        pair(i, True)
        dq_ref[...] = dq_ref[...] * scale
        if n:
            pl.when(i == nblk - 1)(finish)

    blk = pl.BlockSpec((Q_BLOCK, H * L), lambda i: (i, 0))
    once = pl.Buffered(1)
    sh = jax.ShapeDtypeStruct((T, H * L), F32)
    res = _pcall(
        body, name="sb_bwd", grid=(nblk,),
        in_specs=[blk, pl.BlockSpec((T, H * L), lambda i: (0, 1), pipeline_mode=once), blk, blk] + ride_in_specs,
        out_specs=[blk, pl.BlockSpec((T, H * L), lambda i: (0, 0), pipeline_mode=once)] + ride_out_specs,
        out_shape=[sh, sh] + ride_out,
        scratch_shapes=[pltpu.VMEM((H, Q_BLOCK, 1), F32)] + ride_sems,
        compiler_params=_params(("arbitrary",)),
    )(proj, proj, carry, do, *ride)
    return res[0], res[1], list(res[2:])


def _swa_common(q_ref, kvp_ref, kvc_ref, qg_ref, kg_ref, sk_ref, sl_ref, n):
    W, d, G = WINDOW, HEAD_DIM, SWA_GROUP
    scale = d ** -0.5
    row = _iota2((W, 2 * W), 0)
    col = _iota2((W, 2 * W), 1)
    dist = row + W - col
    valid = (dist >= 0) & (dist < W) & ((n > 0) | (col >= W))
    distf = dist.astype(F32)
    kvcat = jnp.concatenate([kvp_ref[...], kvc_ref[...]], axis=0)
    KH, QH = range(SWA_KV_HEADS), range(SWA_HEADS)
    kraw = [kvcat[:, hk * d:(hk + 1) * d] for hk in KH]
    vcat = [kvcat[:, SWA_KVW + hk * d:SWA_KVW + (hk + 1) * d].astype(BF16) for hk in KH]
    rk = [lax.rsqrt(jnp.mean(kraw[hk] * kraw[hk], axis=-1, keepdims=True) + EPS) for hk in KH]
    kh = [kraw[hk] * rk[hk] for hk in KH]
    kn = [(kh[hk] * kg_ref[...]).astype(BF16) for hk in KH]
    qraw = [q_ref[:, h * d:(h + 1) * d] for h in QH]
    rq = [lax.rsqrt(jnp.mean(qraw[h] * qraw[h], axis=-1, keepdims=True) + EPS) for h in QH]
    qh = [qraw[h] * rq[h] for h in QH]
    qn = [(qh[h] * qg_ref[...]).astype(BF16) for h in QH]
    sink = [sk_ref[h:h + 1, :1] for h in QH]
    s = [jnp.where(valid, _dot(qn[h], kn[h // G], NT) * scale - sl_ref[h:h + 1, :1] * distf, -1e30) for h in QH]
    m = [jnp.maximum(jnp.max(s[h], axis=1, keepdims=True), sink[h]) for h in QH]
    p = [jnp.where(valid, jnp.exp(s[h] - m[h]), 0.0) for h in QH]
    esink = [jnp.exp(sink[h] - m[h]) for h in QH]
    den = [jnp.sum(p[h], axis=1, keepdims=True) + esink[h] for h in QH]
    prob = [p[h] / den[h] for h in QH]
    return vcat, rk, kh, kn, rq, qh, qn, esink, den, prob


def _swa_specs(T):
    W = WINDOW
    q = pl.BlockSpec((W, SWA_QW), lambda n: (n, 0))
    prev = pl.BlockSpec((W, 2 * SWA_KVW), lambda n: (jnp.maximum(n - 1, 0), SWA_QW // (2 * SWA_KVW)))
    cur = pl.BlockSpec((W, 2 * SWA_KVW), lambda n: (n, SWA_QW // (2 * SWA_KVW)))
    gain = pl.BlockSpec((1, HEAD_DIM), lambda n: (0, 0))
    perhead = pl.BlockSpec((SWA_HEADS, LANE), lambda n: (0, 0))
    return q, prev, cur, gain, perhead


def _swa_fwd(proj, qg, kg, sinks, slopes):
    T = proj.shape[0]
    W, d, G = WINDOW, HEAD_DIM, SWA_GROUP

    def body(q_ref, kvp_ref, kvc_ref, qg_ref, kg_ref, sk_ref, sl_ref, o_ref):
        vcat, _, _, _, _, _, _, _, _, prob = _swa_common(q_ref, kvp_ref, kvc_ref, qg_ref, kg_ref, sk_ref, sl_ref,
                                                         pl.program_id(0))
        outs = [_bdot(prob[h], vcat[h // G]) for h in range(SWA_HEADS)]
        o_ref[...] = jnp.concatenate(outs, axis=1).astype(BF16)

    q, prev, cur, gain, perhead = _swa_specs(T)
    return _pcall(
        body, name="swa_fwd", grid=(T // W,), in_specs=[q, prev, cur, gain, gain, perhead, perhead], out_specs=q,
        out_shape=jax.ShapeDtypeStruct((T, SWA_QW), BF16), compiler_params=_params(("parallel",)),
    )(proj, proj, proj, qg, kg, sinks, slopes)


def _swa_bwd(proj, qg, kg, sinks, slopes, do):
    T = proj.shape[0]
    W, d, G = WINDOW, HEAD_DIM, SWA_GROUP
    scale = d ** -0.5
    KH, QH = range(SWA_KV_HEADS), range(SWA_HEADS)

    def body(q_ref, kvp_ref, kvc_ref, qg_ref, kg_ref, sk_ref, sl_ref, do_ref,
             dq_ref, dkv_ref, dqg_ref, dkg_ref, dsk_ref):
        n = pl.program_id(0)

        @pl.when(n == 0)
        def _():
            dqg_ref[...] = jnp.zeros_like(dqg_ref)
            dkg_ref[...] = jnp.zeros_like(dkg_ref)
            dsk_ref[...] = jnp.zeros_like(dsk_ref)
            dkv_ref[...] = jnp.zeros_like(dkv_ref)

        vcat, rk, kh, kn, rq, qh, qn, esink, den, prob = _swa_common(q_ref, kvp_ref, kvc_ref, qg_ref, kg_ref,
                                                                     sk_ref, sl_ref, n)
        dov = [do_ref[:, h * d:(h + 1) * d].astype(BF16) for h in QH]
        dp = [_dot(dov[h], vcat[h // G], NT) for h in QH]
        dd = [jnp.sum(prob[h] * dp[h], axis=1, keepdims=True) for h in QH]
        dsb = [(prob[h] * (dp[h] - dd[h]) * scale).astype(BF16) for h in QH]
        dsink = [-jnp.sum((esink[h] / den[h]) * dd[h], axis=0, keepdims=True) for h in QH]
        dqn = [_dot(dsb[h], kn[h // G]) for h in QH]
        dkn_h = [_dot(dsb[h], qn[h], TN) for h in QH]
        dv_h = [_dot(prob[h].astype(BF16), dov[h], TN) for h in QH]
        dqh = [dqn[h] * qg_ref[...] for h in QH]
        dq = [rq[h] * (dqh[h] - qh[h] * jnp.mean(dqh[h] * qh[h], axis=-1, keepdims=True)) for h in QH]
        dkn = [sum(dkn_h[hk * G + g] for g in range(G)) for hk in KH]
        dvc = [sum(dv_h[hk * G + g] for g in range(G)) for hk in KH]
        dkh = [dkn[hk] * kg_ref[...] for hk in KH]
        dkraw = [rk[hk] * (dkh[hk] - kh[hk] * jnp.mean(dkh[hk] * kh[hk], axis=-1, keepdims=True)) for hk in KH]
        dq_ref[...] = jnp.concatenate(dq, axis=1)
        dqg_ref[...] += sum(jnp.sum(dqn[h] * qh[h], axis=0, keepdims=True) for h in QH)
        dkg_ref[...] += sum(jnp.sum(dkn[hk] * kh[hk], axis=0, keepdims=True) for hk in KH)
        rowh = _iota2((SWA_HEADS, LANE), 0)
        dsk_ref[...] += sum(jnp.where(rowh == h, dsink[h], 0.0) for h in QH)
        upd = jnp.concatenate(dkraw + dvc, axis=1)
        offp = pl.multiple_of(jnp.maximum(n - 1, 0) * W, W)
        offc = pl.multiple_of(n * W, W)
        dkv_ref[pl.ds(offp, W), :] += upd[:W]
        dkv_ref[pl.ds(offc, W), :] += upd[W:]

    q, prev, cur, gain, perhead = _swa_specs(T)
    kvfull = pl.BlockSpec((T, 2 * SWA_KVW), lambda n: (0, 0))
    gs = jax.ShapeDtypeStruct((1, d), F32)
    return _pcall(
        body, name="swa_bwd", grid=(T // W,), in_specs=[q, prev, cur, gain, gain, perhead, perhead, q],
        out_specs=[q, kvfull, gain, gain, perhead],
        out_shape=[jax.ShapeDtypeStruct((T, SWA_QW), F32), jax.ShapeDtypeStruct((T, 2 * SWA_KVW), F32), gs, gs,
                   jax.ShapeDtypeStruct((SWA_HEADS, LANE), F32)],
        compiler_params=_params(("arbitrary",)),
    )(proj, proj, proj, qg, kg, sinks, slopes, do)


def _alibi():
    s = [2.0 ** (-8.0 * (i + 1) / SWA_HEADS) for i in range(SWA_HEADS)]
    return jnp.broadcast_to(jnp.asarray(s, F32)[:, None], (SWA_HEADS, LANE))


def _head_tiles(lo, hi):
    shp = lo.shape[:-1]
    return jnp.concatenate([lo.reshape(shp + (SB_HEADS, HEAD_DIM)), hi.reshape(shp + (SB_HEADS, HEAD_DIM))],
                           axis=-1).reshape(shp + (SB_LANES,))


def _tile_halves(x):
    shp = x.shape[:-1]
    t = x.reshape(shp + (SB_HEADS, 2, HEAD_DIM))
    return t[..., 0, :].reshape(shp + (SB_W,)), t[..., 1, :].reshape(shp + (SB_W,))


def _att_in_weights(w_in):
    sq, sk, sv = w_in[:, :SB_W], w_in[:, SB_W:2 * SB_W], w_in[:, 2 * SB_W:3 * SB_W]
    return jnp.concatenate([_head_tiles(sq, jnp.zeros_like(sq)), _head_tiles(sk, sv)], axis=1), w_in[:, 3 * SB_W:]


def _att_out_weights(w_out):
    wo = w_out[:SB_W]
    return _head_tiles(jnp.zeros_like(wo).T, wo.T).T, w_out[SB_W:]


def _att_fwd(h, g, w_in, w_out_of, q_gain, k_gain, sinks, ride=()):
    hn = _rms_fwd(h, g, "att_norm")
    w_sb, w_swa = _att_in_weights(w_in)
    proj_sb = _mm(hn, w_sb, "nn", out_dtype=BF16, name="att_in_sb")
    proj_swa = _mm(hn, w_swa, "nn", name="att_in_swa")
    a_out, carry, gathered = _sb_fwd(proj_sb, ride)
    w_out = w_out_of(gathered)
    wo_sb, wo_swa = _att_out_weights(w_out)
    sk128 = jnp.broadcast_to(sinks.reshape(SWA_HEADS, 1), (SWA_HEADS, LANE))
    qg, kg = q_gain.reshape(1, HEAD_DIM), k_gain.reshape(1, HEAD_DIM)
    b_out = _swa_fwd(proj_swa, qg, kg, sk128, _alibi())
    h2 = _mm(a_out, wo_sb, "nn", res=h, a2=b_out, b2=wo_swa, name="att_out")
    return h2, (h, hn, proj_sb, proj_swa, carry, a_out, b_out, sk128, qg, kg), gathered


def _att_bwd(dh2, saved, g, w_in, w_out, ride=()):
    h, hn, proj_sb, proj_swa, carry, a_out, b_out, sk128, qg, kg = saved
    w_sb, w_swa = _att_in_weights(w_in)
    wo_sb, wo_swa = _att_out_weights(w_out)
    da = _mm(dh2, wo_sb, "nt", out_dtype=BF16, name="att_do_sb")
    db = _mm(dh2, wo_swa, "nt", name="att_do_swa")
    dwo_sb = _mm(a_out, dh2, "tn", out_dtype=BF16, name="att_dwout_sb")
    dwo_swa = _mm(b_out, dh2, "tn", out_dtype=BF16, name="att_dwout_swa")
    dw_out = jnp.concatenate([_tile_halves(dwo_sb.T)[1].T, dwo_swa], axis=0)
    dq, dkv, rode = _sb_bwd(proj_sb, carry, da, ride)
    dbq, dbkv, dqg, dkg, dsink = _swa_bwd(proj_swa, qg, kg, sk128, _alibi(), db)
    dproj = jnp.concatenate([dq.astype(BF16), dkv.astype(BF16), dbq.astype(BF16), dbkv.astype(BF16)], axis=1)
    w_all = jnp.concatenate([w_sb, w_swa], axis=1)
    dw_all = _mm(hn, dproj, "tn", out_dtype=BF16, name="att_dwin")
    dhn = _mm(dproj, w_all, "nt", name="att_dhn")
    dsq, _ = _tile_halves(dw_all[:, :SB_LANES])
    dsk, dsv = _tile_halves(dw_all[:, SB_LANES:2 * SB_LANES])
    dw_in = jnp.concatenate([dsq, dsk, dsv, dw_all[:, 2 * SB_LANES:]], axis=1)
    dh, dg = _rms_bwd(dhn, h, g, dh2, "att_dnorm")
    return dh, dg, dw_in, dw_out, dqg.reshape(HEAD_DIM), dkg.reshape(HEAD_DIM), dsink[:, 0], rode


CONV_ROWS = 1024
CONV_COLS = 1024
HALO = 8


def _shifted(xcat, s, tm):
    if s == 0:
        return xcat[HALO:HALO + tm]
    return pltpu.roll(xcat, s, 0)[HALO:HALO + tm]


def _conv_pre(x_ref, halo_ref, w_ref, i, tm):
    xc = x_ref[...]
    halo = jnp.where(i > 0, halo_ref[...], 0.0)
    xcat = jnp.concatenate([halo, xc], axis=0)
    w = w_ref[...]
    y = w[GDN_CONV - 1:GDN_CONV] * xc
    for kk in range(GDN_CONV - 1):
        y = y + w[kk:kk + 1] * _shifted(xcat, GDN_CONV - 1 - kk, tm)
    return xcat, y


def _l2_heads(s, qscale_of):
    outs, rs = [], []
    for hh in range(s.shape[1] // GDN_HEAD_DIM):
        sh = s[:, hh * GDN_HEAD_DIM:(hh + 1) * GDN_HEAD_DIM]
        r = lax.rsqrt(jnp.sum(sh * sh, axis=-1, keepdims=True) + EPS)
        outs.append(sh * r)
        rs.append(r)
    return outs, rs


def _conv_specs(T, col0, tm, tc):
    cur = pl.BlockSpec((tm, tc), lambda j, i: (i, j + col0 // tc))
    halo = pl.BlockSpec((HALO, tc), lambda j, i: (jnp.maximum(i * (tm // HALO) - 1, 0), j + col0 // tc))
    wsp = pl.BlockSpec((GDN_CONV, tc), lambda j, i: (0, j + col0 // tc))
    out = pl.BlockSpec((tm, tc), lambda j, i: (i, j))
    return cur, halo, wsp, out


def _conv_fwd(proj, conv_w, col0, width, norm, name):
    T = proj.shape[0]
    tm, tc = _pick(T, CONV_ROWS), CONV_COLS
    cur, halo, wsp, out = _conv_specs(T, col0, tm, tc)
    n_q_tiles = (width // 2) // tc

    def body(x_ref, halo_ref, w_ref, o_ref):
        j, i = pl.program_id(0), pl.program_id(1)
        _, y = _conv_pre(x_ref, halo_ref, w_ref, i, tm)
        s = y * _sigmoid(y)
        if norm:
            outs, _ = _l2_heads(s, None)
            qs = jnp.where(j < n_q_tiles, GDN_HEAD_DIM ** -0.5, 1.0)
            o_ref[...] = jnp.concatenate(outs, axis=1) * qs
        else:
            o_ref[...] = s

    return _pcall(body, name=name, grid=(width // tc, T // tm), in_specs=[cur, halo, wsp], out_specs=out,
                  out_shape=jax.ShapeDtypeStruct((T, width), F32),
                  compiler_params=_params(("parallel", "parallel")))(proj, proj, conv_w)


def _conv_bwd_pre(proj, conv_w, dout, col0, width, norm, name):
    T = proj.shape[0]
    tm, tc = _pick(T, CONV_ROWS), CONV_COLS
    cur, halo, wsp, out = _conv_specs(T, col0, tm, tc)
    n_q_tiles = (width // 2) // tc

    def body(x_ref, halo_ref, w_ref, d_ref, dy_ref, dw_ref):
        j, i = pl.program_id(0), pl.program_id(1)
        xcat, y = _conv_pre(x_ref, halo_ref, w_ref, i, tm)
        sg = _sigmoid(y)
        s = y * sg
        d = d_ref[...]
        if norm:
            qs = jnp.where(j < n_q_tiles, GDN_HEAD_DIM ** -0.5, 1.0)
            d = d * qs
            outs, rs = _l2_heads(s, None)
            parts = []
            for hh, (nh, r) in enumerate(zip(outs, rs)):
                dh = d[:, hh * GDN_HEAD_DIM:(hh + 1) * GDN_HEAD_DIM]
                parts.append(r * (dh - nh * jnp.sum(dh * nh, axis=-1, keepdims=True)))
            ds = jnp.concatenate(parts, axis=1)
        else:
            ds = d
        dy = ds * sg * (1.0 + y * (1.0 - sg))
        dy_ref[...] = dy
        rows = [jnp.sum(dy * _shifted(xcat, GDN_CONV - 1 - kk, tm), axis=0, keepdims=True) for kk in range(GDN_CONV)]
        part = jnp.concatenate(rows, axis=0)

        @pl.when(i == 0)
        def _():
            dw_ref[...] = part

        @pl.when(i > 0)
        def _():
            dw_ref[...] += part

    wout = pl.BlockSpec((GDN_CONV, tc), lambda j, i: (0, j))
    return _pcall(body, name=name, grid=(width // tc, T // tm), in_specs=[cur, halo, wsp, out], out_specs=[out, wout],
                  out_shape=[jax.ShapeDtypeStruct((T, width), F32), jax.ShapeDtypeStruct((GDN_CONV, width), F32)],
                  compiler_params=_params(("parallel", "arbitrary")))(proj, proj, conv_w, dout)


def _conv_bwd_in(dy, conv_w, name):
    T, C = dy.shape
    tm, tc = _pick(T, CONV_ROWS), CONV_COLS
    nrow = T // tm

    def body(d_ref, nxt_ref, w_ref, dx_ref):
        i = pl.program_id(0)
        dc = d_ref[...]
        nxt = jnp.where(i < nrow - 1, nxt_ref[...], 0.0)
        dcat = jnp.concatenate([dc, nxt], axis=0)
        w = w_ref[...]
        dx = w[GDN_CONV - 1:GDN_CONV] * dc
        for kk in range(GDN_CONV - 1):
            s = GDN_CONV - 1 - kk
            dx = dx + w[kk:kk + 1] * pltpu.roll(dcat, tm + HALO - s, 0)[:tm]
        dx_ref[...] = dx.astype(BF16)

    cur = pl.BlockSpec((tm, tc), lambda i, j: (i, j))
    nxt = pl.BlockSpec((HALO, tc), lambda i, j: (jnp.minimum((i + 1) * (tm // HALO), T // HALO - 1), j))
    wsp = pl.BlockSpec((GDN_CONV, tc), lambda i, j: (0, j))
    return _pcall(body, name=name, grid=(nrow, C // tc), in_specs=[cur, nxt, wsp], out_specs=cur,
                  out_shape=jax.ShapeDtypeStruct((T, C), BF16),
                  compiler_params=_params(("parallel", "parallel")))(dy, dy, conv_w)


GATE_ROWS = 512


def _chunk_mask(n, lower):
    row = _iota2((n, n), 0)
    col = _iota2((n, n), 1)
    same = (row // GDN_CHUNK) == (col // GDN_CHUNK)
    tri = (row >= col) if lower else (row <= col)
    return (same & tri).astype(BF16)


def _gates_fwd(proj, a_log, dt_bias):
    T = proj.shape[0]
    tm = _pick(T, GATE_ROWS)
    c0 = (GDN_CONV_W + GDN_VW) // LANE

    def body(bl_ref, a_ref, alog_ref, dt_ref, beta_ref, g_ref, gc_ref):
        beta_ref[...] = _sigmoid(bl_ref[...])
        g = -jnp.exp(alog_ref[...]) * _softplus(a_ref[...] + dt_ref[...])
        g_ref[...] = g
        gc_ref[...] = _mdot2(_chunk_mask(tm, True), g)

    blk = lambda c: pl.BlockSpec((tm, LANE), lambda i: (i, c))
    vec = pl.BlockSpec((1, LANE), lambda i: (0, 0))
    sh = jax.ShapeDtypeStruct((T, LANE), F32)
    return _pcall(body, name="gdn_gates", grid=(T // tm,), in_specs=[blk(c0), blk(c0 + 1), vec, vec],
                  out_specs=[blk(0), blk(0), blk(0)], out_shape=[sh, sh, sh],
                  compiler_params=_params(("parallel",)))(proj, proj, a_log, dt_bias)


def _gates_bwd(proj, a_log, dt_bias, beta, g, dbeta, dgc):
    T = proj.shape[0]
    tm = _pick(T, GATE_ROWS)
    c0 = (GDN_CONV_W + GDN_VW) // LANE

    def heads_in_lanes(ref):
        lane = _iota2((tm, LANE), 1)
        out = jnp.where(lane < GDN_GROUP, ref[0], 0.0)
        for grp in range(1, GDN_V_HEADS // GDN_GROUP):
            out = out + jnp.where(lane // GDN_GROUP == grp, pltpu.roll(ref[grp], grp * GDN_GROUP, 1), 0.0)
        return out

    def body(a_ref, alog_ref, dt_ref, beta_ref, g_ref, dbeta_ref, dgc_ref, dbl_ref, da_ref, dalog_ref, ddt_ref):
        dg = _mdot2(_chunk_mask(tm, False), heads_in_lanes(dgc_ref))
        b = beta_ref[...]
        dbl_ref[...] = (heads_in_lanes(dbeta_ref) * b * (1.0 - b)).astype(BF16)
        da = dg * (-jnp.exp(alog_ref[...])) * _sigmoid(a_ref[...] + dt_ref[...])
        da_ref[...] = da.astype(BF16)
        p1 = jnp.sum(dg * g_ref[...], axis=0, keepdims=True)
        p2 = jnp.sum(da, axis=0, keepdims=True)

        @pl.when(pl.program_id(0) == 0)
        def _():
            dalog_ref[...] = p1
            ddt_ref[...] = p2

        @pl.when(pl.program_id(0) > 0)
        def _():
            dalog_ref[...] += p1
            ddt_ref[...] += p2

    blk = lambda c: pl.BlockSpec((tm, LANE), lambda i: (i, c))
    vec = pl.BlockSpec((1, LANE), lambda i: (0, 0))
    grp = pl.BlockSpec((GDN_V_HEADS // GDN_GROUP, tm, LANE), lambda i: (0, i, 0))
    shb = jax.ShapeDtypeStruct((T, LANE), BF16)
    shv = jax.ShapeDtypeStruct((1, LANE), F32)
    return _pcall(body, name="gdn_dgates", grid=(T // tm,),
                  in_specs=[blk(c0 + 1), vec, vec, blk(0), blk(0), grp, grp],
                  out_specs=[blk(0), blk(0), vec, vec], out_shape=[shb, shb, shv, shv],
                  compiler_params=_params(("arbitrary",)))(proj, a_log, dt_bias, beta, g, dbeta, dgc)


def _inv_unit_lower(Ls):
    C = Ls[0].shape[0]
    row = _iota2((C, C), 0)
    col = _iota2((C, C), 1)
    blk16 = (row // 16) == (col // 16)
    blk32 = (row // 32) == (col // 32)
    eye = (row == col).astype(F32)
    xs = [-jnp.where(blk16, L, 0.0) for L in Ls]
    inv = [eye + x for x in xs]
    for _ in range(3):
        xs = [_dot3(x, x) for x in xs]
        inv = [a + _dot3(a, x) for a, x in zip(inv, xs)]
    for mask in (blk32 & ~blk16, ~blk32):
        t = [_dot3(a, jnp.where(mask, L, 0.0)) for a, L in zip(inv, Ls)]
        inv = [a - _dot3(ti, a) for a, ti in zip(inv, t)]
    return inv


GDN_GROUP = 8
GDN_PREP_CHUNKS = 32
GDN_STEP_CHUNKS = 4


def _gdn_specs(T):
    C, D, E, J = GDN_CHUNK, GDN_HEAD_DIM, GDN_GROUP, GDN_STEP_CHUNKS
    n = T // (C * J)
    qk = pl.BlockSpec((J * C, (E // 2) * D), lambda h, i: (i, h))
    vE = pl.BlockSpec((J * C, E * D), lambda h, i: (i, h))
    colv = pl.BlockSpec((J * C, LANE), lambda h, i: (i, 0))
    colo = pl.BlockSpec((None, J * C, LANE), lambda h, i: (h, i, 0))
    rowv = pl.BlockSpec((E, J, 1, C), lambda h, i: (h, i, 0, 0))
    st = pl.BlockSpec((E, J, D, D), lambda h, i: (h, i, 0, 0))
    am = pl.BlockSpec((E, J, C, C), lambda h, i: (h, i, 0, 0))
    return n, qk, vE, colv, colo, rowv, st, am


def _lane_col(blk, lane):
    return jnp.sum(jnp.where(_iota2(blk.shape, 1) == lane, blk, 0.0), axis=1, keepdims=True)


def _gdn_decay(gcol, grow):
    C = GDN_CHUNK
    row = _iota2((C, C), 0)
    col = _iota2((C, C), 1)
    incl = row >= col
    dm = jnp.where(incl, jnp.exp(jnp.where(incl, gcol - grow, 0.0)), 0.0)
    glast = grow[:, C - 1:C]
    return dm, jnp.exp(gcol), jnp.exp(glast), jnp.exp(glast - gcol), row > col, incl


def _gdn_prep(k, beta, gcol, grow):
    T = k.shape[0]
    C, D, B = GDN_CHUNK, GDN_HEAD_DIM, GDN_PREP_CHUNKS
    n = T // C

    def body(k_ref, b_ref, gc_ref, gr_ref, a_ref):
        idx = [(e, cb) for e in range(2) for cb in range(B)]
        kc = {cb: k_ref[cb * C:(cb + 1) * C, :] for cb in range(B)}
        lm = []
        head0 = 2 * pl.program_id(0)
        for e, cb in idx:
            beta = _lane_col(b_ref[cb * C:(cb + 1) * C, :], head0 + e)
            dm, _, _, _, strict, _ = _gdn_decay(_lane_col(gc_ref[cb * C:(cb + 1) * C, :], head0 + e), gr_ref[e, cb])
            lm.append(jnp.where(strict, _bdot(kc[cb] * beta, kc[cb], NT) * dm, 0.0))
        inv = _inv_unit_lower(lm)
        for (e, cb), a in zip(idx, inv):
            a_ref[e, cb] = a

    return _pcall(
        body, name="gdn_prep", grid=(GDN_K_HEADS, n // B),
        in_specs=[pl.BlockSpec((B * C, D), lambda h, i: (i, h)), pl.BlockSpec((B * C, LANE), lambda h, i: (i, 0)),
                  pl.BlockSpec((B * C, LANE), lambda h, i: (i, 0)), pl.BlockSpec((2, B, 1, C), lambda h, i: (h, i, 0, 0))],
        out_specs=pl.BlockSpec((2, B, C, C), lambda h, i: (h, i, 0, 0)),
        out_shape=jax.ShapeDtypeStruct((GDN_V_HEADS, n, C, C), F32),
        compiler_params=_params(("parallel", "parallel")),
    )(k, beta, gcol, grow)


def _gdn_fwd(q, k, v, beta, gcol, grow, amat):
    T = q.shape[0]
    C, D, E = GDN_CHUNK, GDN_HEAD_DIM, GDN_GROUP
    n, qk, vE, colv, colo, rowv, st, am = _gdn_specs(T)
    R = range(E)

    def body(q_ref, k_ref, v_ref, b_ref, gc_ref, gr_ref, a_ref, o_ref, s_ref, vn_ref, state):
        @pl.when(pl.program_id(1) == 0)
        def _():
            state[...] = jnp.zeros_like(state)

        head0 = E * pl.program_id(0)
        s = [state[e] for e in R]
        for cc in range(GDN_STEP_CHUNKS):
            rows = slice(cc * C, (cc + 1) * C)
            qv = [q_ref[rows, (e // 2) * D:(e // 2 + 1) * D] for e in R]
            kv = [k_ref[rows, (e // 2) * D:(e // 2 + 1) * D] for e in R]
            vv = [v_ref[rows, e * D:(e + 1) * D] for e in R]
            beta = [_lane_col(b_ref[rows, :], head0 + e) for e in R]
            a = [a_ref[e, cc] for e in R]
            dec = [_gdn_decay(_lane_col(gc_ref[rows, :], head0 + e), gr_ref[e, cc]) for e in R]
            pm = [_bdot(qv[e], kv[e], NT) * dec[e][0] for e in R]
            r = [beta[e] * (vv[e] - _bdot(kv[e] * dec[e][1], s[e])) for e in R]
            vn = [_dot3(a[e], r[e]) for e in R]
            o = [_bdot(qv[e] * dec[e][1], s[e]) + _bdot(pm[e], vn[e]) for e in R]
            s2 = [dec[e][2] * s[e] + _bdot(kv[e] * dec[e][3], vn[e], TN) for e in R]
            for e in R:
                s_ref[e, cc] = s[e]
                vn_ref[rows, e * D:(e + 1) * D] = vn[e]
                o_ref[rows, e * D:(e + 1) * D] = o[e]
            s = s2
        for e in R:
            state[e] = s[e]

    shv = jax.ShapeDtypeStruct((T, GDN_V_HEADS * D), F32)
    return _pcall(
        body, name="gdn_fwd", grid=(GDN_V_HEADS // E, n), in_specs=[qk, qk, vE, colv, colv, rowv, am],
        out_specs=[vE, st, vE],
        out_shape=[shv, jax.ShapeDtypeStruct((GDN_V_HEADS, T // C, D, D), F32), shv],
        scratch_shapes=[pltpu.VMEM((E, D, D), F32)],
        compiler_params=_params(("parallel", "arbitrary")),
    )(q, k, v, beta, gcol, grow, amat)


def _gdn_bwd(q, k, v, beta, gcol, grow, states, amat, vnew, do):
    T = q.shape[0]
    C, D, E = GDN_CHUNK, GDN_HEAD_DIM, GDN_GROUP
    n, qk, vE, colv, colo, rowv, st, am = _gdn_specs(T)
    rev = lambda spec: pl.BlockSpec(spec.block_shape, (lambda f: (lambda h, i: f(h, n - 1 - i)))(spec.index_map))
    qk, vE, colv, colo, rowv, st, am = (rev(s) for s in (qk, vE, colv, colo, rowv, st, am))
    R = range(E)

    def body(q_ref, k_ref, v_ref, b_ref, gc_ref, gr_ref, s_ref, a_ref, vn_ref, do_ref,
             dq_ref, dk_ref, dv_ref, db_ref, dgc_ref, dstate):
        @pl.when(pl.program_id(1) == 0)
        def _():
            dstate[...] = jnp.zeros_like(dstate)

        dcur = [dstate[e] for e in R]
        for cc in reversed(range(GDN_STEP_CHUNKS)):
            rows = slice(cc * C, (cc + 1) * C)
            M = lambda f: [f(e) for e in R]
            rsum = lambda x: jnp.sum(x, axis=1, keepdims=True)
            qv = M(lambda e: q_ref[rows, (e // 2) * D:(e // 2 + 1) * D])
            kv = M(lambda e: k_ref[rows, (e // 2) * D:(e // 2 + 1) * D])
            vv = M(lambda e: v_ref[rows, e * D:(e + 1) * D])
            vn = M(lambda e: vn_ref[rows, e * D:(e + 1) * D])
            dov = M(lambda e: do_ref[rows, e * D:(e + 1) * D])
            head0 = E * pl.program_id(0)
            beta = M(lambda e: _lane_col(b_ref[rows, :], head0 + e))
            s = M(lambda e: s_ref[e, cc])
            a = M(lambda e: a_ref[e, cc])
            dsn = dcur
            dec = M(lambda e: _gdn_decay(_lane_col(gc_ref[rows, :], head0 + e), gr_ref[e, cc]))
            dm, gam, glast, tail = (M(lambda e: dec[e][i]) for i in range(4))
            strict, incl = dec[0][4], dec[0][5]
            kb = M(lambda e: kv[e] * beta[e])
            kd = M(lambda e: kv[e] * gam[e])
            qd = M(lambda e: qv[e] * gam[e])
            kt = M(lambda e: kv[e] * tail[e])
            lmat = M(lambda e: jnp.where(strict, _bdot(kb[e], kv[e], NT) * dm[e], 0.0))
            pmat = M(lambda e: _bdot(qv[e], kv[e], NT) * dm[e])
            xres = M(lambda e: vv[e] - _bdot(kd[e], s[e]))
            dvn = M(lambda e: _bdot(pmat[e], dov[e], TN) + _bdot(kt[e], dsn[e]))
            dqd = M(lambda e: _bdot(dov[e], s[e], NT))
            dp = M(lambda e: jnp.where(incl, _bdot(dov[e], vn[e], NT), 0.0))
            dkt = M(lambda e: _bdot(vn[e], dsn[e], NT))
            dr = M(lambda e: _dot3(a[e], dvn[e], TN))
            drb = M(lambda e: beta[e] * dr[e])
            dkd = M(lambda e: -_bdot(drb[e], s[e], NT))
            ds2 = M(lambda e: _bdot(qd[e], dov[e], TN) + glast[e] * dsn[e] - _bdot(kd[e], drb[e], TN))
            dl = M(lambda e: -jnp.where(strict, _bdot(dr[e], vn[e], NT), 0.0))
            dmm = M(lambda e: dl[e] * dm[e])
            dnn = M(lambda e: dp[e] * dm[e])
            emat = M(lambda e: dl[e] * lmat[e] + dp[e] * pmat[e])
            dkb = M(lambda e: _bdot(dmm[e], kv[e]))
            dk = M(lambda e: beta[e] * dkb[e] + _bdot(dmm[e], kb[e], TN) + _bdot(dnn[e], qv[e], TN)
                   + gam[e] * dkd[e] + tail[e] * dkt[e])
            dq = M(lambda e: _bdot(dnn[e], kv[e]) + gam[e] * dqd[e])
            dbeta = M(lambda e: rsum(dr[e] * xres[e]) + rsum(dkb[e] * kv[e]))
            ones = jnp.ones((C, LANE), BF16)
            colsum = M(lambda e: _dot2m(emat[e], ones, TN)[:, :1])
            tails = M(lambda e: rsum(dkt[e] * kt[e]))
            lastrow = _iota2((C, 1), 0) == C - 1
            dlast = M(lambda e: jnp.sum(tails[e], axis=0, keepdims=True)
                      + glast[e] * jnp.sum(rsum(s[e] * dsn[e]), axis=0, keepdims=True))
            dgc = M(lambda e: rsum(emat[e]) - colsum[e] + rsum(dkd[e] * kd[e]) + rsum(dqd[e] * qd[e]) - tails[e]
                    + jnp.where(lastrow, dlast[e], 0.0))
            lane = _iota2((C, LANE), 1)
            db_all = jnp.zeros((C, LANE), F32)
            dgc_all = jnp.zeros((C, LANE), F32)
            for e in R:
                dv_ref[rows, e * D:(e + 1) * D] = drb[e]
                db_all = jnp.where(lane == e, dbeta[e], db_all)
                dgc_all = jnp.where(lane == e, dgc[e], dgc_all)
            db_ref[rows, :] = db_all
            dgc_ref[rows, :] = dgc_all
            for kh in range(E // 2):
                dq_ref[rows, kh * D:(kh + 1) * D] = dq[2 * kh] + dq[2 * kh + 1]
                dk_ref[rows, kh * D:(kh + 1) * D] = dk[2 * kh] + dk[2 * kh + 1]

            dcur = ds2
        for e in R:
            dstate[e] = dcur[e]

    shq = jax.ShapeDtypeStruct((T, GDN_K_HEADS * D), F32)
    shv = jax.ShapeDtypeStruct((T, GDN_V_HEADS * D), F32)
    shc = jax.ShapeDtypeStruct((GDN_V_HEADS // E, T, LANE), F32)
    return _pcall(
        body, name="gdn_bwd", grid=(GDN_V_HEADS // E, n),
        in_specs=[qk, qk, vE, colv, colv, rowv, st, am, vE, vE],
        out_specs=[qk, qk, vE, colo, colo], out_shape=[shq, shq, shv, shc, shc],
        scratch_shapes=[pltpu.VMEM((E, D, D), F32)],
        compiler_params=_params(("parallel", "arbitrary")),
    )(q, k, v, beta, gcol, grow, states, amat, vnew, do)


def _outgate_fwd(o, proj, gain):
    T = o.shape[0]
    tm, tc = _pick(T, CONV_ROWS), CONV_COLS
    z0 = GDN_CONV_W // tc

    def body(o_ref, z_ref, g_ref, y_ref):
        z = z_ref[...]
        sz = z * _sigmoid(z)
        parts = []
        for hh in range(tc // GDN_HEAD_DIM):
            oh = o_ref[:, hh * GDN_HEAD_DIM:(hh + 1) * GDN_HEAD_DIM]
            r = lax.rsqrt(jnp.mean(oh * oh, axis=-1, keepdims=True) + EPS)
            parts.append(oh * r * g_ref[...])
        y_ref[...] = (jnp.concatenate(parts, axis=1) * sz).astype(BF16)

    blk = pl.BlockSpec((tm, tc), lambda i, j: (i, j))
    return _pcall(body, name="gdn_outgate", grid=(T // tm, GDN_VW // tc),
                  in_specs=[blk, pl.BlockSpec((tm, tc), lambda i, j: (i, j + z0)), pl.BlockSpec((1, GDN_HEAD_DIM), lambda i, j: (0, 0))],
                  out_specs=blk, out_shape=jax.ShapeDtypeStruct((T, GDN_VW), BF16),
                  compiler_params=_params(("parallel", "parallel")))(o, proj, gain)


def _outgate_bwd(dy, o, proj, gain):
    T = o.shape[0]
    tm, tc = _pick(T, CONV_ROWS), CONV_COLS
    z0 = GDN_CONV_W // tc
    nh = tc // GDN_HEAD_DIM

    def body(dy_ref, o_ref, z_ref, g_ref, do_ref, dz_ref, dg_ref):
        z = z_ref[...]
        sg = _sigmoid(z)
        sz = z * sg
        dy = dy_ref[...]
        dgain = jnp.zeros((1, GDN_HEAD_DIM), F32)
        dos, ys = [], []
        for hh in range(nh):
            sl = slice(hh * GDN_HEAD_DIM, (hh + 1) * GDN_HEAD_DIM)
            oh = o_ref[:, sl]
            r = lax.rsqrt(jnp.mean(oh * oh, axis=-1, keepdims=True) + EPS)
            xh = oh * r
            dn = dy[:, sl] * sz[:, sl]
            dgain = dgain + jnp.sum(dn * xh, axis=0, keepdims=True)
            dxh = dn * g_ref[...]
            dos.append(r * (dxh - xh * jnp.mean(dxh * xh, axis=-1, keepdims=True)))
            ys.append(xh * g_ref[...])
        do_ref[...] = jnp.concatenate(dos, axis=1)
        dz_ref[...] = (dy * jnp.concatenate(ys, axis=1) * sg * (1.0 + z * (1.0 - sg))).astype(BF16)
        first = (pl.program_id(0) == 0) & (pl.program_id(1) == 0)

        @pl.when(first)
        def _():
            dg_ref[...] = dgain

        @pl.when(jnp.logical_not(first))
        def _():
            dg_ref[...] += dgain

    blk = pl.BlockSpec((tm, tc), lambda i, j: (i, j))
    vec = pl.BlockSpec((1, GDN_HEAD_DIM), lambda i, j: (0, 0))
    return _pcall(body, name="gdn_doutgate", grid=(T // tm, GDN_VW // tc),
                  in_specs=[blk, blk, pl.BlockSpec((tm, tc), lambda i, j: (i, j + z0)), vec],
                  out_specs=[blk, blk, vec],
                  out_shape=[jax.ShapeDtypeStruct((T, GDN_VW), F32), jax.ShapeDtypeStruct((T, GDN_VW), BF16),
                             jax.ShapeDtypeStruct((1, GDN_HEAD_DIM), F32)],
                  compiler_params=_params(("arbitrary", "arbitrary")))(dy, o, proj, gain)


def _pad_lanes(vec):
    return jnp.pad(vec.reshape(1, -1), ((0, 0), (0, LANE - vec.shape[-1])))


def _head_rows(a):
    T = a.shape[0]
    return a[:, :GDN_V_HEADS].T.reshape(GDN_V_HEADS, T // GDN_CHUNK, 1, GDN_CHUNK)


def _gdn_pad_in(w_in):
    c = GDN_CONV_W + GDN_VW
    z = jnp.zeros(w_in.shape[:-1] + (LANE - GDN_V_HEADS,), w_in.dtype)
    return jnp.concatenate([w_in[..., :c + GDN_V_HEADS], z, w_in[..., c + GDN_V_HEADS:], z], axis=-1)


def _gdn_unpad_in(dw):
    c = GDN_CONV_W + GDN_VW
    return jnp.concatenate([dw[..., :c + GDN_V_HEADS], dw[..., c + LANE:c + LANE + GDN_V_HEADS]], axis=-1)


def _gdn_mixer_fwd(h, g, w_in_pad, conv_w, a_log, dt_bias, out_gain, w_out):
    T = h.shape[0]
    hn = _rms_fwd(h, g, "gdn_norm")
    proj = _mm(hn, w_in_pad, "nn", name="gdn_in")
    qk = _conv_fwd(proj, conv_w, 0, 2 * GDN_KW, True, "gdn_conv_qk")
    vv = _conv_fwd(proj, conv_w, 2 * GDN_KW, GDN_VW, False, "gdn_conv_v")
    alog, dtb = _pad_lanes(a_log), _pad_lanes(dt_bias)
    beta, gl, gc = _gates_fwd(proj, alog, dtb)
    grow = _head_rows(gc)
    qn, kn = qk[:, :GDN_KW], qk[:, GDN_KW:]
    amat = _gdn_prep(kn, beta, gc, grow)
    o, states, vnew = _gdn_fwd(qn, kn, vv, beta, gc, grow, amat)
    gain = out_gain.reshape(1, GDN_HEAD_DIM)
    y = _outgate_fwd(o, proj, gain)
    h2 = _mm(y, w_out, "nn", res=h, name="gdn_out")
    return h2, (h, hn, proj, qn, kn, vv, beta, gl, gc, grow, o, states, amat, vnew, y, alog, dtb, gain)


def _gdn_mixer_bwd(dh2, saved, g, w_in_pad, conv_w, w_out):
    h, hn, proj, qn, kn, vv, beta, gl, gc, grow, o, states, amat, vnew, y, alog, dtb, gain = saved
    T = h.shape[0]
    dy = _mm(dh2, w_out, "nt", name="gdn_dy")
    dw_out = _mm(y, dh2, "tn", out_dtype=BF16, name="gdn_dwout")
    do, dz, dgain = _outgate_bwd(dy, o, proj, gain)
    dq, dk, dv, dbeta, dgc = _gdn_bwd(qn, kn, vv, beta, gc, grow, states, amat, vnew, do)
    dqk = jnp.concatenate([dq, dk], axis=1)
    dy_qk, dcw_qk = _conv_bwd_pre(proj, conv_w, dqk, 0, 2 * GDN_KW, True, "gdn_dconv_qk")
    dy_v, dcw_v = _conv_bwd_pre(proj, conv_w, dv, 2 * GDN_KW, GDN_VW, False, "gdn_dconv_v")
    dx_qk = _conv_bwd_in(dy_qk, conv_w[:, :2 * GDN_KW], "gdn_dconvin_qk")
    dx_v = _conv_bwd_in(dy_v, conv_w[:, 2 * GDN_KW:], "gdn_dconvin_v")
    dbl, da, dalog, ddt = _gates_bwd(proj, alog, dtb, beta, gl, dbeta, dgc)
    dproj = jnp.concatenate([dx_qk, dx_v, dz, dbl, da], axis=1)
    dw_in_pad = _mm(hn, dproj, "tn", out_dtype=BF16, name="gdn_dwin")
    dhn = _mm(dproj, w_in_pad, "nt", name="gdn_dhn")
    dh, dg = _rms_bwd(dhn, h, g, dh2, "gdn_dnorm")
    dconv = jnp.concatenate([dcw_qk, dcw_v], axis=1)
    return (dh, dg, _gdn_unpad_in(dw_in_pad), dconv, dalog[0, :GDN_V_HEADS], ddt[0, :GDN_V_HEADS],
            dgain.reshape(GDN_HEAD_DIM), dw_out)


def _instances(full):
    out = {}
    for n, a in full.items():
        if n.startswith("ffn_"):
            for i in range(2):
                for j in range(2):
                    out[(n, i, j)] = a[i, j]
        elif n in ("mix_norm", "ple_norm", "ple_w_gate", "ple_w_proj"):
            for i in range(2):
                out[(n, i)] = a[i]
        else:
            out[(n,)] = a[0]
    return out


def _stacked(inst):
    out = {}
    for n in dict.fromkeys(k[0] for k in inst):
        if n.startswith("ffn_"):
            out[n] = jnp.stack([jnp.stack([inst[(n, i, j)] for j in range(2)]) for i in range(2)])
        elif n in ("mix_norm", "ple_norm", "ple_w_gate", "ple_w_proj"):
            out[n] = jnp.stack([inst[(n, i)] for i in range(2)])
        else:
            out[n] = inst[(n,)][None]
    return out


def _local_step(x, p, target, w, late_shards=(), late_weights=None, early_grads=None, first_shards=(), first_weights=None,
                last_grads=None):
    w = dict(w)
    ffn = lambda i, j: (w[("ffn_norm", i, j)], w[("ffn_w_gate", i, j)], w[("ffn_w_up", i, j)], w[("ffn_w_down", i, j)])
    h = x
    tape = []
    for i in range(2):
        if i == 0 and first_weights is not None:
            def wd_of(gathered):
                w.update(first_weights(gathered))
                return w[("ffn_w_down", 0, 0)]
            h, s1 = _ffn_fwd(h, w[("ffn_norm", 0, 0)], w[("ffn_w_gate", 0, 0)], w[("ffn_w_up", 0, 0)], None, "ffn0a",
                             first_shards, wd_of)
        else:
            h, s1 = _ffn_fwd(h, *ffn(i, 0), f"ffn{i}a")
        if i == 0:
            def w_out_of(gathered):
                if late_weights is not None:
                    w.update(late_weights(gathered))
                return w[("att_w_out",)]
            h, s2, _ = _att_fwd(h, w[("mix_norm", 0)], w[("att_w_in",)], w_out_of, w[("att_q_norm",)],
                                w[("att_k_norm",)], w[("att_sinks",)], late_shards)
        else:
            gdn_in_pad = _gdn_pad_in(w[("gdn_w_in",)])
            h, s2 = _gdn_mixer_fwd(h, w[("mix_norm", 1)], gdn_in_pad, w[("gdn_conv_w",)], w[("gdn_a_log",)],
                                   w[("gdn_dt_bias",)], w[("gdn_out_norm",)], w[("gdn_w_out",)])
        h, s3 = _ffn_fwd(h, *ffn(i, 1), f"ffn{i}b")
        h, s4 = _ple_fwd(h, p[i], w[("ple_norm", i)], w[("ple_w_gate", i)], w[("ple_w_proj", i)], f"ple{i}")
        tape.append((s1, s2, s3, s4))

    loss, dh = _loss_head(h, target)

    g = {}
    rode = []
    for i in (1, 0):
        s1, s2, s3, s4 = tape[i]
        dh, g[("ple_norm", i)], g[("ple_w_gate", i)], g[("ple_w_proj", i)] = _ple_bwd(
            dh, s4, p[i], w[("ple_norm", i)], w[("ple_w_gate", i)], f"ple{i}")
        dh, g[("ffn_norm", i, 1)], g[("ffn_w_gate", i, 1)], g[("ffn_w_up", i, 1)], g[("ffn_w_down", i, 1)], _ = _ffn_bwd(
            dh, s3, *ffn(i, 1), f"ffn{i}b")
        if i == 0:
            ride = early_grads(g) if early_grads is not None else ()
            (dh, g[("mix_norm", 0)], g[("att_w_in",)], g[("att_w_out",)], g[("att_q_norm",)], g[("att_k_norm",)],
             g[("att_sinks",)], rode) = _att_bwd(dh, s2, w[("mix_norm", 0)], w[("att_w_in",)], w[("att_w_out",)], ride)
        else:
            (dh, g[("mix_norm", 1)], g[("gdn_w_in",)], g[("gdn_conv_w",)], g[("gdn_a_log",)], g[("gdn_dt_bias",)],
             g[("gdn_out_norm",)], g[("gdn_w_out",)]) = _gdn_mixer_bwd(
                dh, s2, w[("mix_norm", 1)], gdn_in_pad, w[("gdn_conv_w",)], w[("gdn_w_out",)])
        last_of = None
        if i == 0 and last_grads is not None:
            def last_of(dwg, dwu, dwd):
                return last_grads({**g, ("ffn_w_gate", 0, 0): dwg, ("ffn_w_up", 0, 0): dwu, ("ffn_w_down", 0, 0): dwd})
        (dh, g[("ffn_norm", i, 0)], g[("ffn_w_gate", i, 0)], g[("ffn_w_up", i, 0)], g[("ffn_w_down", i, 0)],
         rode_last) = _ffn_bwd(dh, s1, *ffn(i, 0), f"ffn{i}a", last_of)
    return loss, dh, g, rode, rode_last


MESH = pl.DeviceIdType.MESH


def _place():
    x, y, c = lax.axis_index("x"), lax.axis_index("y"), lax.axis_index("c")
    others = [((1 - x, y), 2 * (1 - x) + y), ((x, 1 - y), 2 * x + (1 - y)), ((1 - x, 1 - y), 2 * (1 - x) + (1 - y))]
    return x, y, c, 4 * x + 2 * y + c, 2 * x + y, (x, y, 1 - c), others


def _comm_call(body, arrays, out_shape, n_sems, name):
    hbm = pl.BlockSpec(memory_space=pl.ANY)
    n = len(arrays)
    return _pcall(
        body, name=name, in_specs=[hbm] * n, out_specs=[hbm] * len(out_shape), out_shape=out_shape,
        scratch_shapes=[pltpu.SemaphoreType.DMA((n, n_sems)), pltpu.SemaphoreType.DMA((n, n_sems)),
                        pltpu.SemaphoreType.DMA((n, N_CHIP))],
        compiler_params=pltpu.CompilerParams(has_side_effects=True),
    )(*arrays)


def _gather_protocol(ins, outs, send_sems, recv_sems, local_sems):
    n = len(ins)
    x, y, c, me, my_chip, sibling, others = _place()

    def copy(a, k, block, to, src=None):
        dst = outs[a].at[block]
        return pltpu.make_async_remote_copy(
            src_ref=dst if src is None else src, dst_ref=dst, send_sem=send_sems.at[a, k],
            recv_sem=recv_sems.at[a, k], device_id=to, device_id_type=MESH)

    local = [pltpu.make_async_copy(ins[a], outs[a].at[me], local_sems.at[a, 0]) for a in range(n)]
    first = []
    for a in range(n):
        first.append(copy(a, 0, me, sibling, src=ins[a]))
        first += [copy(a, 1 + j, me, (*chip, c), src=ins[a]) for j, (chip, _) in enumerate(others)]

    def start():
        for cp in local + first:
            cp.start()

    def finish():
        passed = []
        for a in range(n):
            for j, (chip, chip_idx) in enumerate(others):
                blk = 2 * chip_idx + c
                copy(a, 1 + j, blk, (x, y, c)).wait_recv()
                fwd = copy(a, 4 + j, blk, sibling)
                fwd.start()
                passed.append(fwd)
        for a in range(n):
            copy(a, 0, 2 * my_chip + (1 - c), (x, y, c)).wait_recv()
            for j, (chip, chip_idx) in enumerate(others):
                copy(a, 4 + j, 2 * chip_idx + (1 - c), (x, y, c)).wait_recv()
        for cp in first + passed:
            cp.wait_send()
        for cp in local:
            cp.wait()

    return start, finish


def _all_gather(arrays):
    n = len(arrays)

    def body(*refs):
        start, finish = _gather_protocol(refs[:n], refs[n:2 * n], *refs[2 * n:])
        start()
        finish()

    out_shape = [jax.ShapeDtypeStruct((N_DEV,) + a.shape, a.dtype) for a in arrays]
    return _comm_call(body, arrays, out_shape, N_DEV - 1, "gather_weights")


def _exchange_sibling(arrays, name):
    n = len(arrays)

    def body(*refs):
        ins, got = refs[:n], refs[n:2 * n]
        send_sems, recv_sems, _ = refs[2 * n:]
        x, y, c, me, my_chip, sibling, others = _place()
        remote = []
        for a in range(n):
            for chip in range(N_CHIP):
                rc = pltpu.make_async_remote_copy(
                    src_ref=ins[a].at[2 * chip + (1 - c)], dst_ref=got[a].at[chip], send_sem=send_sems.at[a, chip],
                    recv_sem=recv_sems.at[a, chip], device_id=sibling, device_id_type=MESH)
                rc.start()
                remote.append(rc)
        for rc in remote:
            rc.wait()

    half = [jax.ShapeDtypeStruct((N_CHIP,) + a.shape[1:], a.dtype) for a in arrays]
    return _comm_call(body, arrays, half, N_CHIP, name)


def _chips_protocol(ins, outs, send_sems, recv_sems, local_sems):
    n = len(ins)
    x, y, c, me, my_chip, sibling, others = _place()
    local = [pltpu.make_async_copy(ins[a].at[my_chip], outs[a].at[my_chip], local_sems.at[a, 0]) for a in range(n)]
    remote = [pltpu.make_async_remote_copy(
        src_ref=ins[a].at[chip_idx], dst_ref=outs[a].at[my_chip], send_sem=send_sems.at[a, j],
        recv_sem=recv_sems.at[a, j], device_id=(*chip, c), device_id_type=MESH)
        for a in range(n) for j, (chip, chip_idx) in enumerate(others)]

    def start():
        for cp in local + remote:
            cp.start()

    def finish():
        for cp in remote + local:
            cp.wait()

    return start, finish


def _exchange_chips(arrays, name):
    n = len(arrays)

    def body(*refs):
        start, finish = _chips_protocol(refs[:n], refs[n:2 * n], *refs[2 * n:])
        start()
        finish()

    out_shape = [jax.ShapeDtypeStruct(a.shape, a.dtype) for a in arrays]
    return _comm_call(body, arrays, out_shape, N_CHIP - 1, name)


def _as_rows(a, lead):
    shp = a.shape
    return a.reshape(shp[:lead] + (math.prod(shp[lead:-1]), shp[-1]))


def _row_tile(rows, cap=512):
    if rows <= cap:
        return rows
    for t in range(cap - cap % 8, 0, -8):
        if rows % t == 0:
            return t
    return rows


def _pair_sum(send, got, name):
    a3, b3 = _as_rows(send, 1), _as_rows(got, 1)
    _, rows, last = b3.shape
    tr = _row_tile(rows, 2048)

    def body(c_ref, a_ref, b_ref, o_ref):
        o_ref[...] = (a_ref[...].astype(F32) + b_ref[...].astype(F32)).astype(o_ref.dtype)

    core = lax.axis_index("c").astype(jnp.int32).reshape(1)
    out = _pcall(
        body, name=name,
        grid_spec=pltpu.PrefetchScalarGridSpec(
            num_scalar_prefetch=1, grid=(N_CHIP, rows // tr),
            in_specs=[pl.BlockSpec((None, tr, last), lambda k, i, c_ref: (2 * k + c_ref[0], i, 0)),
                      pl.BlockSpec((None, tr, last), lambda k, i, c_ref: (k, i, 0))],
            out_specs=pl.BlockSpec((None, tr, last), lambda k, i, c_ref: (k, i, 0))),
        out_shape=jax.ShapeDtypeStruct(b3.shape, got.dtype), compiler_params=_params(("parallel", "parallel")),
    )(core, a3, b3)
    return out.reshape(got.shape)


def _adamw(parts, w, m, v, name):
    lead, (rows, last) = w.shape[:-2], w.shape[-2:]
    nl = len(lead)
    tr = _row_tile(rows, 1024)
    c1 = 1.0 / (1.0 - ADAM_B1 ** ADAM_STEP)
    c2 = 1.0 / (1.0 - ADAM_B2 ** ADAM_STEP)

    def body(p_ref, w_ref, m_ref, v_ref, g_ref, d_ref, nm_ref, nv_ref):
        g = p_ref[0].astype(F32)
        for chip in range(1, N_CHIP):
            g = g + p_ref[chip].astype(F32)
        mn = ADAM_B1 * m_ref[...] + (1.0 - ADAM_B1) * g
        vn = ADAM_B2 * v_ref[...] + (1.0 - ADAM_B2) * (g * g)
        g_ref[...] = g
        nm_ref[...] = mn
        nv_ref[...] = vn
        d_ref[...] = -ADAM_LR * ((mn * c1) / (jnp.sqrt(vn * c2) + ADAM_EPS) + ADAM_WD * w_ref[...])

    row = pl.BlockSpec((None,) * nl + (tr, last), lambda *ix: ix + (0,))
    part = pl.BlockSpec((N_CHIP,) + (None,) * nl + (tr, last), lambda *ix: (0,) + ix + (0,))
    sh = jax.ShapeDtypeStruct(w.shape, F32)
    return _pcall(body, name=name, grid=lead + (rows // tr,), in_specs=[part, row, row, row],
                  out_specs=[row, row, row, row], out_shape=[sh, sh, sh, sh],
                  compiler_params=_params(("parallel",) * (nl + 1)))(parts, w, m, v)


def _pack(pieces, row_align):
    rows, offs, r = [], [], 0
    for a in pieces:
        flat = a.reshape(-1)
        nr = -(-flat.shape[0] // PACK_W)
        flat = jnp.pad(flat, (0, nr * PACK_W - flat.shape[0]))
        rows.append(flat.reshape(nr, PACK_W))
        offs.append(r)
        r += nr
    pad = (-r) % row_align
    if pad:
        rows.append(jnp.zeros((pad, PACK_W), pieces[0].dtype))
    return jnp.concatenate(rows, axis=0), offs


def _unpack(flat, offs, shapes):
    out = []
    for off, shp in zip(offs, shapes):
        size = math.prod(shp)
        nr = -(-size // PACK_W)
        out.append(flat[..., off:off + nr, :].reshape(flat.shape[:-2] + (nr * PACK_W,))[..., :size].reshape(flat.shape[:-2] + tuple(shp)))
    return out


def _to_full(gathered, axis):
    z = jnp.moveaxis(gathered, 0, axis)
    shp = list(z.shape)
    return z.reshape(shp[:axis] + [shp[axis] * shp[axis + 1]] + shp[axis + 2:])


def _to_shards(full, axis):
    shp = list(full.shape)
    z = full.reshape(shp[:axis] + [N_DEV, shp[axis] // N_DEV] + shp[axis + 1:])
    return jnp.moveaxis(z, axis, 0)


def kernel(x, p, ffn_norm, ffn_w_gate, ffn_w_up, ffn_w_down, mix_norm, att_w_in, att_q_norm, att_k_norm, att_sinks, att_w_out, gdn_w_in, gdn_conv_w, gdn_a_log, gdn_dt_bias, gdn_out_norm, gdn_w_out, ple_norm, ple_w_gate, ple_w_proj, loss_target, m_ffn_norm, m_ffn_w_gate, m_ffn_w_up, m_ffn_w_down, m_mix_norm, m_att_w_in, m_att_q_norm, m_att_k_norm, m_att_sinks, m_att_w_out, m_gdn_w_in, m_gdn_conv_w, m_gdn_a_log, m_gdn_dt_bias, m_gdn_out_norm, m_gdn_w_out, m_ple_norm, m_ple_w_gate, m_ple_w_proj, v_ffn_norm, v_ffn_w_gate, v_ffn_w_up, v_ffn_w_down, v_mix_norm, v_att_w_in, v_att_q_norm, v_att_k_norm, v_att_sinks, v_att_w_out, v_gdn_w_in, v_gdn_conv_w, v_gdn_a_log, v_gdn_dt_bias, v_gdn_out_norm, v_gdn_w_out, v_ple_norm, v_ple_w_gate, v_ple_w_proj):
    args = dict(locals())
    wts = {n: args[n] for n in WEIGHTS}
    mom = {n: args["m_" + n] for n in WEIGHTS}
    var = {n: args["v_" + n] for n in WEIGHTS}
    axis = dict(SHARDED)
    vecs = [n for n, _ in SHARDED[:SMALL_SHARDED]]
    small = vecs + list(REPLICATED)
    small_shapes = [wts[n].shape for n in small]
    lead = lambda n: 2 if n.startswith("ffn_") else 1

    def stack_of(arrays, name, idxs):
        return jnp.stack([arrays[name][idx] if idx else arrays[name][0] for idx in idxs])

    def full_instances(gathered, group):
        out = {}
        for (name, idxs), g in zip(group, gathered):
            whole = _to_full(g, axis[name] - lead(name) + 1)
            for k, idx in enumerate(idxs):
                out[(name,) + idx] = whole[k]
        return out

    def shard_stacks(g, group):
        return [_to_shards(jnp.stack([g[(name,) + idx] for idx in idxs]), axis[name] - lead(name) + 1)
                for name, idxs in group]

    vec_pack, voffs = _pack([wts[n] for n in vecs], 8)
    early = _all_gather([stack_of(wts, n, idxs).astype(BF16) for n, idxs in EARLY] + [vec_pack])
    w = full_instances(early[:-1], EARLY)
    vec_full = {n: _to_full(piece, axis[n]) for n, piece in
                zip(vecs, _unpack(early[-1], voffs, [wts[n].shape for n in vecs]))}
    w.update(_instances({**vec_full, **{n: wts[n] for n in REPLICATED}}))
    first_shards = [stack_of(wts, n, idxs).astype(BF16) for n, idxs in FIRST]
    late_shards = [stack_of(wts, n, idxs).astype(BF16) for n, idxs in LATE]

    def chip_sums(send, tag):
        got = _exchange_sibling(send, f"exchange_sibling_{tag}")
        return [_pair_sum(p_, q_, f"pair_sum_{tag}_{i}") for i, (p_, q_) in enumerate(zip(send, got))]

    loss, grad_x, g, rode, rode_last = _local_step(
        x[0], p[:, 0], loss_target[0], w, late_shards, lambda gathered: full_instances(gathered, LATE),
        lambda g: chip_sums(shard_stacks(g, RIDE), "early"),
        first_shards, lambda gathered: full_instances(gathered, FIRST),
        lambda g: chip_sums(shard_stacks(g, FINAL), "final"))

    gs = _stacked({k: v for k, v in g.items() if k[0] in small})
    vec_shards = [_to_shards(gs[n], axis[n]) for n in vecs]
    small_send = jnp.stack([_pack([sh[d] for sh in vec_shards] + [gs[n] for n in REPLICATED] + [loss.reshape(1)], 8)[0]
                            for d in range(N_DEV)])
    last = list(rode_last) + list(_exchange_chips(chip_sums([small_send], "small"), "exchange_chips_small"))

    pieces = {}
    for (name, idxs), part in list(zip(RIDE, rode)) + list(zip(FINAL, last[:-1])):
        for k, idx in enumerate(idxs):
            pieces[(name,) + idx] = part[:, k]
    outs = {}
    for n, _ in SHARDED[SMALL_SHARDED:]:
        if lead(n) == 2:
            part = jnp.stack([jnp.stack([pieces[(n, i, j)] for j in range(2)], axis=1) for i in range(2)], axis=1)
        elif (n, 0) in pieces:
            part = jnp.stack([pieces[(n, i)] for i in range(2)], axis=1)
        else:
            part = pieces[(n,)][:, None]
        outs[n] = _adamw(part, wts[n], mom[n], var[n], f"adamw_{n}")
    filler = [jnp.zeros((1,), F32)]
    small_w, soffs = _pack([wts[n] for n in small] + filler, 8)
    small_m, _ = _pack([mom[n] for n in small] + filler, 8)
    small_v, _ = _pack([var[n] for n in small] + filler, 8)
    small_out = [_unpack(z, soffs, small_shapes + [(1,)]) for z in _adamw(last[-1], small_w, small_m, small_v, "adamw_small")]
    loss = small_out[0][-1][0]
    for i, n in enumerate(small):
        outs[n] = [small_out[k][i] for k in range(4)]
    result = [loss, grad_x[None]]
    for k in range(4):
        result += [outs[n][k] for n in WEIGHTS]
    return tuple(result)
```
